```python
import jax, jax.numpy as jnp
from jax import lax
import numpy as np

D_MODEL = 1024
BATCH = 8
SEQ = 2048
DEPTH = 2

BRANCH_W = D_MODEL // 2
LRU_W = BRANCH_W
LRU_BLOCKS = 8
LRU_BLOCK_W = LRU_W // LRU_BLOCKS
LRU_CONV = 4
LRU_C = 8.0
SC_W = BRANCH_W
SC_CONV = 3
HEAD_DIM = 64
N_Q_HEADS = BRANCH_W // HEAD_DIM
N_KV_HEADS = 2
GQA_GROUP = N_Q_HEADS // N_KV_HEADS
WINDOW = 128
BLOCK = 128
CF_W = BRANCH_W
CF_CONV = 31
N_BRANCHES = 4
D_FF = -(-8 * D_MODEL // (3 * 256)) * 256
EPS = 1e-6
NEG_INF = -1e30

SPLIT_SIZES = (
    LRU_W, LRU_W,
    SC_W, SC_W, SC_W,
    N_Q_HEADS * HEAD_DIM,
    N_KV_HEADS * HEAD_DIM,
    N_KV_HEADS * HEAD_DIM,
    2 * CF_W,
    N_BRANCHES * D_MODEL,
)
IN_W = int(sum(SPLIT_SIZES))
SPLIT_POINTS = tuple(int(v) for v in np.cumsum(SPLIT_SIZES)[:-1])

kernel_name = "hybrid_rglru_shortconv_swa_conformer"


def rms_norm(x, g):
    xf = x.astype(jnp.float32)
    y = xf * lax.rsqrt(jnp.mean(xf * xf, axis=-1, keepdims=True) + EPS)
    return (y * g.astype(jnp.float32)).astype(x.dtype)


def layer_norm(x, g, b):
    xf = x.astype(jnp.float32)
    mu = jnp.mean(xf, axis=-1, keepdims=True)
    var = jnp.mean(jnp.square(xf - mu), axis=-1, keepdims=True)
    y = (xf - mu) * lax.rsqrt(var + EPS)
    return (y * g.astype(jnp.float32) + b.astype(jnp.float32)).astype(x.dtype)


def causal_dwconv(x, w, b=None):
    k, c = w.shape
    y = lax.conv_general_dilated(
        x, w[:, None, :].astype(x.dtype), window_strides=(1,), padding=[(k - 1, 0)],
        dimension_numbers=("NWC", "WIO", "NWC"), feature_group_count=c)
    if b is not None:
        y = y + b
    return y


def rg_lru(x, wx, bx, wa, ba, lam):
    b, s, w = x.shape
    xb = x.reshape(b, s, LRU_BLOCKS, LRU_BLOCK_W)
    gate_i = jax.nn.sigmoid(jnp.einsum("bshi,hij->bshj", xb, wx).reshape(b, s, w) + bx)
    gate_r = jax.nn.sigmoid(jnp.einsum("bshi,hij->bshj", xb, wa).reshape(b, s, w) + ba)
    log_a = -LRU_C * gate_r.astype(jnp.float32) * jax.nn.softplus(-lam.astype(jnp.float32))
    a = jnp.exp(log_a)
    mult = jnp.sqrt(-jnp.expm1(2.0 * log_a))
    u = (x * gate_i).astype(jnp.float32) * mult

    def combine(left, right):
        a_l, b_l = left
        a_r, b_r = right
        return a_l * a_r, a_r * b_l + b_r

    _, h = lax.associative_scan(combine, (a, u), axis=1)
    return h.astype(x.dtype)


def alibi_slopes(n):
    return jnp.asarray([2.0 ** (-8.0 * (i + 1) / n) for i in range(n)], dtype=jnp.float32)


def sliding_window_attention(q, k, v, sinks):
    b, s, _ = q.shape
    nb = s // BLOCK
    q = q.reshape(b, nb, BLOCK, N_KV_HEADS, GQA_GROUP, HEAD_DIM)
    k = k.reshape(b, nb, BLOCK, N_KV_HEADS, HEAD_DIM)
    v = v.reshape(b, nb, BLOCK, N_KV_HEADS, HEAD_DIM)
    zero = jnp.zeros_like(k[:, :1])
    k2 = jnp.concatenate([jnp.concatenate([zero, k[:, :-1]], axis=1), k], axis=2)
    v2 = jnp.concatenate([jnp.concatenate([zero, v[:, :-1]], axis=1), v], axis=2)
    scores = jnp.einsum("bnqhgd,bnkhd->bnhgqk", q, k2).astype(jnp.float32) * (HEAD_DIM ** -0.5)
    qi = jnp.arange(BLOCK)[:, None]
    ki = jnp.arange(2 * BLOCK)[None, :]
    dist = qi + BLOCK - ki
    key_pos = (jnp.arange(nb)[:, None, None] - 1) * BLOCK + ki[None]
    valid = (dist >= 0)[None] & (dist < WINDOW)[None] & (key_pos >= 0)
    slopes = alibi_slopes(N_Q_HEADS).reshape(N_KV_HEADS, GQA_GROUP)
    scores = scores - slopes[:, :, None, None] * dist.astype(jnp.float32)
    scores = jnp.where(valid[None, :, None, None], scores, NEG_INF)
    sink = jnp.broadcast_to(
        sinks.astype(jnp.float32).reshape(1, 1, N_KV_HEADS, GQA_GROUP, 1, 1),
        scores.shape[:-1] + (1,))
    probs = jax.nn.softmax(jnp.concatenate([scores, sink], axis=-1), axis=-1)[..., :-1]
    out = jnp.einsum("bnhgqk,bnkhd->bnqhgd", probs.astype(v2.dtype), v2)
    return out.reshape(b, s, N_Q_HEADS * HEAD_DIM)


def hybrid_mixer(xn, w_in, conv_a_w, conv_a_b, lru_wx, lru_bx, lru_wa, lru_ba, lru_lambda,
                 w_a_out, conv_b_w, w_b_out, sinks, w_c_out, conv_d_w, conv_d_b,
                 ln_d_g, ln_d_b, w_d_out, w_o):
    b, s, _ = xn.shape
    proj = xn @ w_in
    a_x, a_gate, b_v, b_c, b_b, q, k, v, d_in, gate_logits = jnp.split(proj, SPLIT_POINTS, axis=-1)
    a_h = rg_lru(causal_dwconv(a_x, conv_a_w, conv_a_b), lru_wx, lru_bx, lru_wa, lru_ba, lru_lambda)
    y_a = (a_h * jax.nn.gelu(a_gate)) @ w_a_out
    y_b = (b_b * causal_dwconv(b_c * b_v, conv_b_w)) @ w_b_out
    y_c = sliding_window_attention(q, k, v, sinks) @ w_c_out
    d = d_in[..., :CF_W] * jax.nn.sigmoid(d_in[..., CF_W:])
    d = jax.nn.silu(layer_norm(causal_dwconv(d, conv_d_w, conv_d_b), ln_d_g, ln_d_b))
    y_d = d @ w_d_out
    g = jax.nn.sigmoid(gate_logits).reshape(b, s, N_BRANCHES, D_MODEL)
    merged = g[:, :, 0] * y_a + g[:, :, 1] * y_b + g[:, :, 2] * y_c + g[:, :, 3] * y_d
    return merged @ w_o


def swiglu(x, w_gate, w_up, w_down):
    return (jax.nn.silu(x @ w_gate) * (x @ w_up)) @ w_down


def _fwd_setup_inputs(seed: int = 0) -> dict:
    key = jax.random.key(seed)
    ks = jax.random.split(key, 32)
    f32 = jnp.float32

    def nrm(k, shape, scale):
        return jax.random.normal(k, shape, f32) * scale

    L = DEPTH
    target = jax.random.uniform(ks[9], (L, LRU_W), f32, 0.9, 0.999)
    sig = target ** (1.0 / LRU_C)
    lru_lambda = jnp.log(sig) - jnp.log1p(-sig)
    return {
        "x": nrm(ks[0], (BATCH, SEQ, D_MODEL), 1.0),
        "norm1_g": 1.0 + nrm(ks[1], (L, D_MODEL), 0.02),
        "w_in": nrm(ks[2], (L, D_MODEL, IN_W), D_MODEL ** -0.5),
        "conv_a_w": nrm(ks[3], (L, LRU_CONV, LRU_W), LRU_CONV ** -0.5),
        "conv_a_b": nrm(ks[4], (L, LRU_W), 0.02),
        "lru_wx": nrm(ks[5], (L, LRU_BLOCKS, LRU_BLOCK_W, LRU_BLOCK_W), LRU_BLOCK_W ** -0.5),
        "lru_bx": nrm(ks[6], (L, LRU_W), 0.02),
        "lru_wa": nrm(ks[7], (L, LRU_BLOCKS, LRU_BLOCK_W, LRU_BLOCK_W), LRU_BLOCK_W ** -0.5),
        "lru_ba": nrm(ks[8], (L, LRU_W), 0.02),
        "lru_lambda": lru_lambda,
        "w_a_out": nrm(ks[10], (L, LRU_W, D_MODEL), LRU_W ** -0.5),
        "conv_b_w": nrm(ks[11], (L, SC_CONV, SC_W), SC_CONV ** -0.5),
        "w_b_out": nrm(ks[12], (L, SC_W, D_MODEL), SC_W ** -0.5),
        "sinks": nrm(ks[13], (L, N_Q_HEADS), 0.5),
        "w_c_out": nrm(ks[14], (L, N_Q_HEADS * HEAD_DIM, D_MODEL), (N_Q_HEADS * HEAD_DIM) ** -0.5),
        "conv_d_w": nrm(ks[15], (L, CF_CONV, CF_W), CF_CONV ** -0.5),
        "conv_d_b": nrm(ks[16], (L, CF_W), 0.02),
        "ln_d_g": 1.0 + nrm(ks[17], (L, CF_W), 0.02),
        "ln_d_b": nrm(ks[18], (L, CF_W), 0.02),
        "w_d_out": nrm(ks[19], (L, CF_W, D_MODEL), CF_W ** -0.5),
        "w_o": nrm(ks[20], (L, D_MODEL, D_MODEL), D_MODEL ** -0.5),
        "norm2_g": 1.0 + nrm(ks[21], (L, D_MODEL), 0.02),
        "w_ffn_gate": nrm(ks[22], (L, D_MODEL, D_FF), D_MODEL ** -0.5),
        "w_ffn_up": nrm(ks[23], (L, D_MODEL, D_FF), D_MODEL ** -0.5),
        "w_ffn_down": nrm(ks[24], (L, D_FF, D_MODEL), D_FF ** -0.5),
        "final_g": 1.0 + nrm(ks[25], (D_MODEL,), 0.02),
    }


def _fwd_reference(x, norm1_g, w_in, conv_a_w, conv_a_b, lru_wx, lru_bx, lru_wa, lru_ba, lru_lambda,
              w_a_out, conv_b_w, w_b_out, sinks, w_c_out, conv_d_w, conv_d_b, ln_d_g, ln_d_b,
              w_d_out, w_o, norm2_g, w_ffn_gate, w_ffn_up, w_ffn_down, final_g):
    for l in range(DEPTH):
        xn = rms_norm(x, norm1_g[l])
        x = x + hybrid_mixer(xn, w_in[l], conv_a_w[l], conv_a_b[l], lru_wx[l], lru_bx[l],
                             lru_wa[l], lru_ba[l], lru_lambda[l], w_a_out[l], conv_b_w[l],
                             w_b_out[l], sinks[l], w_c_out[l], conv_d_w[l], conv_d_b[l],
                             ln_d_g[l], ln_d_b[l], w_d_out[l], w_o[l])
        x = x + swiglu(rms_norm(x, norm2_g[l]), w_ffn_gate[l], w_ffn_up[l], w_ffn_down[l])
    return rms_norm(x, final_g)


import jax as _jax
import jax.numpy as _jnp

TWIN_FORMAT = 'train_step'
FWD_PARAMS = ['x', 'norm1_g', 'w_in', 'conv_a_w', 'conv_a_b', 'lru_wx', 'lru_bx', 'lru_wa', 'lru_ba', 'lru_lambda', 'w_a_out', 'conv_b_w', 'w_b_out', 'sinks', 'w_c_out', 'conv_d_w', 'conv_d_b', 'ln_d_g', 'ln_d_b', 'w_d_out', 'w_o', 'norm2_g', 'w_ffn_gate', 'w_ffn_up', 'w_ffn_down', 'final_g']
TWIN_WEIGHTS = ['norm1_g', 'w_in', 'conv_a_w', 'conv_a_b', 'lru_wx', 'lru_bx', 'lru_wa', 'lru_ba', 'lru_lambda', 'w_a_out', 'conv_b_w', 'w_b_out', 'sinks', 'w_c_out', 'conv_d_w', 'conv_d_b', 'ln_d_g', 'ln_d_b', 'w_d_out', 'w_o', 'norm2_g', 'w_ffn_gate', 'w_ffn_up', 'w_ffn_down', 'final_g']
TWIN_DIFF_INPUT = 'x'
TWIN_INPUTS = ['x', 'norm1_g', 'w_in', 'conv_a_w', 'conv_a_b', 'lru_wx', 'lru_bx', 'lru_wa', 'lru_ba', 'lru_lambda', 'w_a_out', 'conv_b_w', 'w_b_out', 'sinks', 'w_c_out', 'conv_d_w', 'conv_d_b', 'ln_d_g', 'ln_d_b', 'w_d_out', 'w_o', 'norm2_g', 'w_ffn_gate', 'w_ffn_up', 'w_ffn_down', 'final_g', 'loss_target', 'm_norm1_g', 'm_w_in', 'm_conv_a_w', 'm_conv_a_b', 'm_lru_wx', 'm_lru_bx', 'm_lru_wa', 'm_lru_ba', 'm_lru_lambda', 'm_w_a_out', 'm_conv_b_w', 'm_w_b_out', 'm_sinks', 'm_w_c_out', 'm_conv_d_w', 'm_conv_d_b', 'm_ln_d_g', 'm_ln_d_b', 'm_w_d_out', 'm_w_o', 'm_norm2_g', 'm_w_ffn_gate', 'm_w_ffn_up', 'm_w_ffn_down', 'm_final_g', 'v_norm1_g', 'v_w_in', 'v_conv_a_w', 'v_conv_a_b', 'v_lru_wx', 'v_lru_bx', 'v_lru_wa', 'v_lru_ba', 'v_lru_lambda', 'v_w_a_out', 'v_conv_b_w', 'v_w_b_out', 'v_sinks', 'v_w_c_out', 'v_conv_d_w', 'v_conv_d_b', 'v_ln_d_g', 'v_ln_d_b', 'v_w_d_out', 'v_w_o', 'v_norm2_g', 'v_w_ffn_gate', 'v_w_ffn_up', 'v_w_ffn_down', 'v_final_g']
TWIN_OUTPUTS = ['loss', 'grad_x', 'grad_norm1_g', 'grad_w_in', 'grad_conv_a_w', 'grad_conv_a_b', 'grad_lru_wx', 'grad_lru_bx', 'grad_lru_wa', 'grad_lru_ba', 'grad_lru_lambda', 'grad_w_a_out', 'grad_conv_b_w', 'grad_w_b_out', 'grad_sinks', 'grad_w_c_out', 'grad_conv_d_w', 'grad_conv_d_b', 'grad_ln_d_g', 'grad_ln_d_b', 'grad_w_d_out', 'grad_w_o', 'grad_norm2_g', 'grad_w_ffn_gate', 'grad_w_ffn_up', 'grad_w_ffn_down', 'grad_final_g', 'delta_norm1_g', 'delta_w_in', 'delta_conv_a_w', 'delta_conv_a_b', 'delta_lru_wx', 'delta_lru_bx', 'delta_lru_wa', 'delta_lru_ba', 'delta_lru_lambda', 'delta_w_a_out', 'delta_conv_b_w', 'delta_w_b_out', 'delta_sinks', 'delta_w_c_out', 'delta_conv_d_w', 'delta_conv_d_b', 'delta_ln_d_g', 'delta_ln_d_b', 'delta_w_d_out', 'delta_w_o', 'delta_norm2_g', 'delta_w_ffn_gate', 'delta_w_ffn_up', 'delta_w_ffn_down', 'delta_final_g', 'new_m_norm1_g', 'new_m_w_in', 'new_m_conv_a_w', 'new_m_conv_a_b', 'new_m_lru_wx', 'new_m_lru_bx', 'new_m_lru_wa', 'new_m_lru_ba', 'new_m_lru_lambda', 'new_m_w_a_out', 'new_m_conv_b_w', 'new_m_w_b_out', 'new_m_sinks', 'new_m_w_c_out', 'new_m_conv_d_w', 'new_m_conv_d_b', 'new_m_ln_d_g', 'new_m_ln_d_b', 'new_m_w_d_out', 'new_m_w_o', 'new_m_norm2_g', 'new_m_w_ffn_gate', 'new_m_w_ffn_up', 'new_m_w_ffn_down', 'new_m_final_g', 'new_v_norm1_g', 'new_v_w_in', 'new_v_conv_a_w', 'new_v_conv_a_b', 'new_v_lru_wx', 'new_v_lru_bx', 'new_v_lru_wa', 'new_v_lru_ba', 'new_v_lru_lambda', 'new_v_w_a_out', 'new_v_conv_b_w', 'new_v_w_b_out', 'new_v_sinks', 'new_v_w_c_out', 'new_v_conv_d_w', 'new_v_conv_d_b', 'new_v_ln_d_g', 'new_v_ln_d_b', 'new_v_w_d_out', 'new_v_w_o', 'new_v_norm2_g', 'new_v_w_ffn_gate', 'new_v_w_ffn_up', 'new_v_w_ffn_down', 'new_v_final_g']
TWIN_LEAF_KINDS = {'loss': 'loss', 'grad_x': 'grad_x', 'grad_norm1_g': 'grad_w', 'grad_w_in': 'grad_w', 'grad_conv_a_w': 'grad_w', 'grad_conv_a_b': 'grad_w', 'grad_lru_wx': 'grad_w', 'grad_lru_bx': 'grad_w', 'grad_lru_wa': 'grad_w', 'grad_lru_ba': 'grad_w', 'grad_lru_lambda': 'grad_w', 'grad_w_a_out': 'grad_w', 'grad_conv_b_w': 'grad_w', 'grad_w_b_out': 'grad_w', 'grad_sinks': 'grad_w', 'grad_w_c_out': 'grad_w', 'grad_conv_d_w': 'grad_w', 'grad_conv_d_b': 'grad_w', 'grad_ln_d_g': 'grad_w', 'grad_ln_d_b': 'grad_w', 'grad_w_d_out': 'grad_w', 'grad_w_o': 'grad_w', 'grad_norm2_g': 'grad_w', 'grad_w_ffn_gate': 'grad_w', 'grad_w_ffn_up': 'grad_w', 'grad_w_ffn_down': 'grad_w', 'grad_final_g': 'grad_w', 'delta_norm1_g': 'delta_w', 'delta_w_in': 'delta_w', 'delta_conv_a_w': 'delta_w', 'delta_conv_a_b': 'delta_w', 'delta_lru_wx': 'delta_w', 'delta_lru_bx': 'delta_w', 'delta_lru_wa': 'delta_w', 'delta_lru_ba': 'delta_w', 'delta_lru_lambda': 'delta_w', 'delta_w_a_out': 'delta_w', 'delta_conv_b_w': 'delta_w', 'delta_w_b_out': 'delta_w', 'delta_sinks': 'delta_w', 'delta_w_c_out': 'delta_w', 'delta_conv_d_w': 'delta_w', 'delta_conv_d_b': 'delta_w', 'delta_ln_d_g': 'delta_w', 'delta_ln_d_b': 'delta_w', 'delta_w_d_out': 'delta_w', 'delta_w_o': 'delta_w', 'delta_norm2_g': 'delta_w', 'delta_w_ffn_gate': 'delta_w', 'delta_w_ffn_up': 'delta_w', 'delta_w_ffn_down': 'delta_w', 'delta_final_g': 'delta_w', 'new_m_norm1_g': 'new_m', 'new_m_w_in': 'new_m', 'new_m_conv_a_w': 'new_m', 'new_m_conv_a_b': 'new_m', 'new_m_lru_wx': 'new_m', 'new_m_lru_bx': 'new_m', 'new_m_lru_wa': 'new_m', 'new_m_lru_ba': 'new_m', 'new_m_lru_lambda': 'new_m', 'new_m_w_a_out': 'new_m', 'new_m_conv_b_w': 'new_m', 'new_m_w_b_out': 'new_m', 'new_m_sinks': 'new_m', 'new_m_w_c_out': 'new_m', 'new_m_conv_d_w': 'new_m', 'new_m_conv_d_b': 'new_m', 'new_m_ln_d_g': 'new_m', 'new_m_ln_d_b': 'new_m', 'new_m_w_d_out': 'new_m', 'new_m_w_o': 'new_m', 'new_m_norm2_g': 'new_m', 'new_m_w_ffn_gate': 'new_m', 'new_m_w_ffn_up': 'new_m', 'new_m_w_ffn_down': 'new_m', 'new_m_final_g': 'new_m', 'new_v_norm1_g': 'new_v', 'new_v_w_in': 'new_v', 'new_v_conv_a_w': 'new_v', 'new_v_conv_a_b': 'new_v', 'new_v_lru_wx': 'new_v', 'new_v_lru_bx': 'new_v', 'new_v_lru_wa': 'new_v', 'new_v_lru_ba': 'new_v', 'new_v_lru_lambda': 'new_v', 'new_v_w_a_out': 'new_v', 'new_v_conv_b_w': 'new_v', 'new_v_w_b_out': 'new_v', 'new_v_sinks': 'new_v', 'new_v_w_c_out': 'new_v', 'new_v_conv_d_w': 'new_v', 'new_v_conv_d_b': 'new_v', 'new_v_ln_d_g': 'new_v', 'new_v_ln_d_b': 'new_v', 'new_v_w_d_out': 'new_v', 'new_v_w_o': 'new_v', 'new_v_norm2_g': 'new_v', 'new_v_w_ffn_gate': 'new_v', 'new_v_w_ffn_up': 'new_v', 'new_v_w_ffn_down': 'new_v', 'new_v_final_g': 'new_v'}


def _forward(args):
    return _fwd_reference(*[args[k] for k in FWD_PARAMS])


def _output_shape():
    out = _jax.eval_shape(lambda: _forward(_fwd_setup_inputs(0)))
    return out.shape, out.dtype

N_MICROBATCH = 1
ADAM_LR = 0.001
ADAM_B1 = 0.9
ADAM_B2 = 0.999
ADAM_EPS = 1e-08
ADAM_WD = 0.01
ADAM_STEP = 10
PER_EXAMPLE_BATCH_AXIS = {'x': 0, 'loss_target': 0}
SHARED_INPUTS = []
_WEIGHT_DTYPES = {'norm1_g': _jnp.float32, 'w_in': _jnp.float32, 'conv_a_w': _jnp.float32, 'conv_a_b': _jnp.float32, 'lru_wx': _jnp.float32, 'lru_bx': _jnp.float32, 'lru_wa': _jnp.float32, 'lru_ba': _jnp.float32, 'lru_lambda': _jnp.float32, 'w_a_out': _jnp.float32, 'conv_b_w': _jnp.float32, 'w_b_out': _jnp.float32, 'sinks': _jnp.float32, 'w_c_out': _jnp.float32, 'conv_d_w': _jnp.float32, 'conv_d_b': _jnp.float32, 'ln_d_g': _jnp.float32, 'ln_d_b': _jnp.float32, 'w_d_out': _jnp.float32, 'w_o': _jnp.float32, 'norm2_g': _jnp.float32, 'w_ffn_gate': _jnp.float32, 'w_ffn_up': _jnp.float32, 'w_ffn_down': _jnp.float32, 'final_g': _jnp.float32}
MOMENT_SCALE = {'norm1_g': 1.280805e-01, 'w_in': 4.505123e-02, 'conv_a_w': 4.850397e-02, 'conv_a_b': 4.177127e-01, 'lru_wx': 2.713871e-02, 'lru_bx': 1.852618e-02, 'lru_wa': 1.489667e-02, 'lru_ba': 1.130781e-02, 'lru_lambda': 2.187110e-02, 'w_a_out': 3.444001e-02, 'conv_b_w': 8.965964e-02, 'w_b_out': 6.194192e-02, 'sinks': 8.657523e-02, 'w_c_out': 1.959941e-02, 'conv_d_w': 5.591187e-02, 'conv_d_b': 1.353786e-01, 'ln_d_g': 7.700831e-02, 'ln_d_b': 6.967461e-02, 'w_d_out': 3.987356e-02, 'w_o': 8.297118e-02, 'norm2_g': 8.212677e-02, 'w_ffn_gate': 3.633464e-02, 'w_ffn_up': 3.519706e-02, 'w_ffn_down': 5.848797e-02, 'final_g': 1.601824e+01}


def _to_microbatches(a, axis):
    t = _jnp.moveaxis(a, axis, 0)
    t = t.reshape((N_MICROBATCH, t.shape[0] // N_MICROBATCH) + t.shape[1:])
    return _jnp.moveaxis(t, 1, axis + 1)


def setup_inputs(seed: int = 0) -> dict:
    inp = _fwd_setup_inputs(seed)
    key = _jax.random.fold_in(_jax.random.key(seed), 7919)
    shape, _ = _output_shape()
    out = dict(inp)
    out["loss_target"] = _jax.random.normal(_jax.random.fold_in(key, 0), shape, _jnp.float32)
    for i, name in enumerate(TWIN_WEIGHTS):
        w = inp[name].astype(_jnp.float32)
        if MOMENT_SCALE is None:
            s = _jnp.sqrt(_jnp.mean(_jnp.square(w)) + 1e-30)
        else:
            s = MOMENT_SCALE[name]
        km, kv = _jax.random.split(_jax.random.fold_in(key, i + 1))
        out[name] = w
        out["m_" + name] = s * _jax.random.normal(km, w.shape, _jnp.float32)
        out["v_" + name] = (s * s) * _jax.random.uniform(kv, w.shape, _jnp.float32, 0.5, 1.5)
    if N_MICROBATCH > 1:
        for name, axis in PER_EXAMPLE_BATCH_AXIS.items():
            out[name] = _to_microbatches(out[name], axis)
    return {'x': out['x'], 'norm1_g': out['norm1_g'], 'w_in': out['w_in'], 'conv_a_w': out['conv_a_w'], 'conv_a_b': out['conv_a_b'], 'lru_wx': out['lru_wx'], 'lru_bx': out['lru_bx'], 'lru_wa': out['lru_wa'], 'lru_ba': out['lru_ba'], 'lru_lambda': out['lru_lambda'], 'w_a_out': out['w_a_out'], 'conv_b_w': out['conv_b_w'], 'w_b_out': out['w_b_out'], 'sinks': out['sinks'], 'w_c_out': out['w_c_out'], 'conv_d_w': out['conv_d_w'], 'conv_d_b': out['conv_d_b'], 'ln_d_g': out['ln_d_g'], 'ln_d_b': out['ln_d_b'], 'w_d_out': out['w_d_out'], 'w_o': out['w_o'], 'norm2_g': out['norm2_g'], 'w_ffn_gate': out['w_ffn_gate'], 'w_ffn_up': out['w_ffn_up'], 'w_ffn_down': out['w_ffn_down'], 'final_g': out['final_g'], 'loss_target': out['loss_target'], 'm_norm1_g': out['m_norm1_g'], 'm_w_in': out['m_w_in'], 'm_conv_a_w': out['m_conv_a_w'], 'm_conv_a_b': out['m_conv_a_b'], 'm_lru_wx': out['m_lru_wx'], 'm_lru_bx': out['m_lru_bx'], 'm_lru_wa': out['m_lru_wa'], 'm_lru_ba': out['m_lru_ba'], 'm_lru_lambda': out['m_lru_lambda'], 'm_w_a_out': out['m_w_a_out'], 'm_conv_b_w': out['m_conv_b_w'], 'm_w_b_out': out['m_w_b_out'], 'm_sinks': out['m_sinks'], 'm_w_c_out': out['m_w_c_out'], 'm_conv_d_w': out['m_conv_d_w'], 'm_conv_d_b': out['m_conv_d_b'], 'm_ln_d_g': out['m_ln_d_g'], 'm_ln_d_b': out['m_ln_d_b'], 'm_w_d_out': out['m_w_d_out'], 'm_w_o': out['m_w_o'], 'm_norm2_g': out['m_norm2_g'], 'm_w_ffn_gate': out['m_w_ffn_gate'], 'm_w_ffn_up': out['m_w_ffn_up'], 'm_w_ffn_down': out['m_w_ffn_down'], 'm_final_g': out['m_final_g'], 'v_norm1_g': out['v_norm1_g'], 'v_w_in': out['v_w_in'], 'v_conv_a_w': out['v_conv_a_w'], 'v_conv_a_b': out['v_conv_a_b'], 'v_lru_wx': out['v_lru_wx'], 'v_lru_bx': out['v_lru_bx'], 'v_lru_wa': out['v_lru_wa'], 'v_lru_ba': out['v_lru_ba'], 'v_lru_lambda': out['v_lru_lambda'], 'v_w_a_out': out['v_w_a_out'], 'v_conv_b_w': out['v_conv_b_w'], 'v_w_b_out': out['v_w_b_out'], 'v_sinks': out['v_sinks'], 'v_w_c_out': out['v_w_c_out'], 'v_conv_d_w': out['v_conv_d_w'], 'v_conv_d_b': out['v_conv_d_b'], 'v_ln_d_g': out['v_ln_d_g'], 'v_ln_d_b': out['v_ln_d_b'], 'v_w_d_out': out['v_w_d_out'], 'v_w_o': out['v_w_o'], 'v_norm2_g': out['v_norm2_g'], 'v_w_ffn_gate': out['v_w_ffn_gate'], 'v_w_ffn_up': out['v_w_ffn_up'], 'v_w_ffn_down': out['v_w_ffn_down'], 'v_final_g': out['v_final_g']}


def _loss(weights, diff, rest, loss_target):
    with _jax.named_scope("forward"):
        args = {**rest, TWIN_DIFF_INPUT: diff, **{k: w.astype(_WEIGHT_DTYPES[k]) for k, w in weights.items()}}
        y = _forward(args)
    with _jax.named_scope("loss_head"):
        err = _jnp.square(y.astype(_jnp.float32) - loss_target)
        return 0.5 * _jnp.sum(_jnp.mean(err, axis=-1)) if err.ndim else 0.5 * err


def _adamw(w, g, m, v):
    m = ADAM_B1 * m + (1.0 - ADAM_B1) * g
    v = ADAM_B2 * v + (1.0 - ADAM_B2) * _jnp.square(g)
    m_hat = m / (1.0 - ADAM_B1 ** ADAM_STEP)
    v_hat = v / (1.0 - ADAM_B2 ** ADAM_STEP)
    delta = -ADAM_LR * (m_hat / (_jnp.sqrt(v_hat) + ADAM_EPS) + ADAM_WD * w)
    return delta, m, v


def reference(x, norm1_g, w_in, conv_a_w, conv_a_b, lru_wx, lru_bx, lru_wa, lru_ba, lru_lambda, w_a_out, conv_b_w, w_b_out, sinks, w_c_out, conv_d_w, conv_d_b, ln_d_g, ln_d_b, w_d_out, w_o, norm2_g, w_ffn_gate, w_ffn_up, w_ffn_down, final_g, loss_target, m_norm1_g, m_w_in, m_conv_a_w, m_conv_a_b, m_lru_wx, m_lru_bx, m_lru_wa, m_lru_ba, m_lru_lambda, m_w_a_out, m_conv_b_w, m_w_b_out, m_sinks, m_w_c_out, m_conv_d_w, m_conv_d_b, m_ln_d_g, m_ln_d_b, m_w_d_out, m_w_o, m_norm2_g, m_w_ffn_gate, m_w_ffn_up, m_w_ffn_down, m_final_g, v_norm1_g, v_w_in, v_conv_a_w, v_conv_a_b, v_lru_wx, v_lru_bx, v_lru_wa, v_lru_ba, v_lru_lambda, v_w_a_out, v_conv_b_w, v_w_b_out, v_sinks, v_w_c_out, v_conv_d_w, v_conv_d_b, v_ln_d_g, v_ln_d_b, v_w_d_out, v_w_o, v_norm2_g, v_w_ffn_gate, v_w_ffn_up, v_w_ffn_down, v_final_g):
    given = dict(x=x, norm1_g=norm1_g, w_in=w_in, conv_a_w=conv_a_w, conv_a_b=conv_a_b, lru_wx=lru_wx, lru_bx=lru_bx, lru_wa=lru_wa, lru_ba=lru_ba, lru_lambda=lru_lambda, w_a_out=w_a_out, conv_b_w=conv_b_w, w_b_out=w_b_out, sinks=sinks, w_c_out=w_c_out, conv_d_w=conv_d_w, conv_d_b=conv_d_b, ln_d_g=ln_d_g, ln_d_b=ln_d_b, w_d_out=w_d_out, w_o=w_o, norm2_g=norm2_g, w_ffn_gate=w_ffn_gate, w_ffn_up=w_ffn_up, w_ffn_down=w_ffn_down, final_g=final_g, loss_target=loss_target, m_norm1_g=m_norm1_g, m_w_in=m_w_in, m_conv_a_w=m_conv_a_w, m_conv_a_b=m_conv_a_b, m_lru_wx=m_lru_wx, m_lru_bx=m_lru_bx, m_lru_wa=m_lru_wa, m_lru_ba=m_lru_ba, m_lru_lambda=m_lru_lambda, m_w_a_out=m_w_a_out, m_conv_b_w=m_conv_b_w, m_w_b_out=m_w_b_out, m_sinks=m_sinks, m_w_c_out=m_w_c_out, m_conv_d_w=m_conv_d_w, m_conv_d_b=m_conv_d_b, m_ln_d_g=m_ln_d_g, m_ln_d_b=m_ln_d_b, m_w_d_out=m_w_d_out, m_w_o=m_w_o, m_norm2_g=m_norm2_g, m_w_ffn_gate=m_w_ffn_gate, m_w_ffn_up=m_w_ffn_up, m_w_ffn_down=m_w_ffn_down, m_final_g=m_final_g, v_norm1_g=v_norm1_g, v_w_in=v_w_in, v_conv_a_w=v_conv_a_w, v_conv_a_b=v_conv_a_b, v_lru_wx=v_lru_wx, v_lru_bx=v_lru_bx, v_lru_wa=v_lru_wa, v_lru_ba=v_lru_ba, v_lru_lambda=v_lru_lambda, v_w_a_out=v_w_a_out, v_conv_b_w=v_conv_b_w, v_w_b_out=v_w_b_out, v_sinks=v_sinks, v_w_c_out=v_w_c_out, v_conv_d_w=v_conv_d_w, v_conv_d_b=v_conv_d_b, v_ln_d_g=v_ln_d_g, v_ln_d_b=v_ln_d_b, v_w_d_out=v_w_d_out, v_w_o=v_w_o, v_norm2_g=v_norm2_g, v_w_ffn_gate=v_w_ffn_gate, v_w_ffn_up=v_w_ffn_up, v_w_ffn_down=v_w_ffn_down, v_final_g=v_final_g)
    weights = {n: given[n] for n in TWIN_WEIGHTS}
    shared = {n: given[n] for n in SHARED_INPUTS}
    per_example = {n: given[n] for n in ['x']}
    grad_fn = _jax.value_and_grad(_loss, argnums=(0, 1))

    def one_microbatch(ex, loss_target):
        ex = dict(ex)
        diff = ex.pop(TWIN_DIFF_INPUT)
        return grad_fn(weights, diff, {**shared, **ex}, loss_target)

    if N_MICROBATCH == 1:
        loss, (grad_w, grad_x) = one_microbatch(per_example, given["loss_target"])
    else:
        def body(carry, xs):
            loss_sum, grad_sum = carry
            l_k, (gw_k, gx_k) = one_microbatch(xs[0], xs[1])
            with _jax.named_scope("update"):
                return (loss_sum + l_k, _jax.tree.map(_jnp.add, grad_sum, gw_k)), gx_k

        init = (_jnp.zeros((), _jnp.float32), _jax.tree.map(_jnp.zeros_like, weights))
        (loss, grad_w), grad_x = _jax.lax.scan(body, init, (per_example, given["loss_target"]))
    with _jax.named_scope("update"):
        delta_w, new_m, new_v = {}, {}, {}
        for n in TWIN_WEIGHTS:
            delta_w[n], new_m[n], new_v[n] = _adamw(weights[n], grad_w[n], given["m_" + n], given["v_" + n])
    return (loss, grad_x, *[grad_w[n] for n in TWIN_WEIGHTS], *[delta_w[n] for n in TWIN_WEIGHTS],
            *[new_m[n] for n in TWIN_WEIGHTS], *[new_v[n] for n in TWIN_WEIGHTS])
```

```python
import functools

import jax
import jax.numpy as jnp
from jax import lax
from jax.experimental import pallas as pl
from jax.experimental.pallas import tpu as pltpu

F32 = jnp.float32
BF16 = jnp.bfloat16
E = pl.Element

D = 1024
BW = 512
IN_W = 8448
GL0 = 4352
FF = 2816
N_HEADS = 8
N_KV = 2
HD = 64
ATT_BLK = 128
EPS = 1e-6
LRU_C = 8.0
NEG_INF = -1e30
DEPTH = 2
NDEV = 8
CONV_A, CONV_B, CONV_D = 4, 3, 31
C_AX, C_AG, C_BV, C_BC, C_BB, C_Q, C_K, C_V, C_D1, C_D2 = 0, 512, 1024, 1536, 2048, 2560, 3072, 3200, 3328, 3840
CW_A, CW_B, CW_D, CW_ROWS = 0, 4, 8, 40
V_CAB, V_BX, V_BA, V_LAM, V_CDB, V_LNG, V_LNB, V_SINK, V_ROWS = 0, 1, 2, 3, 4, 5, 6, 7, 8
HALO = 32

ADAM_LR, ADAM_B1, ADAM_B2, ADAM_EPS, ADAM_WD, ADAM_STEP = 0.001, 0.9, 0.999, 1e-08, 0.01, 10

VMEM_LIMIT = 56 * 1024 * 1024

_NN = (((1,), (0,)), ((), ()))
_NT = (((1,), (1,)), ((), ()))
_TN = (((0,), (0,)), ((), ()))


def _dot(a, b, dims):
    return lax.dot_general(a.astype(BF16), b.astype(BF16), dims, preferred_element_type=F32)


def _cparams(n_axes):
    return pltpu.CompilerParams(dimension_semantics=("arbitrary",) * n_axes, vmem_limit_bytes=VMEM_LIMIT)


def _sds(shape, dtype):
    return jax.ShapeDtypeStruct(tuple(shape), dtype)


def _sigmoid(x):
    return jax.nn.sigmoid(x)


def _neg_expm1(x):
    p = x * (1.0 + x * (0.5 + x * (1.0 / 6.0 + x * (1.0 / 24.0 + x * (1.0 / 120.0)))))
    return jnp.where(x > -0.1, -p, 1.0 - jnp.exp(x))


def _softplus(z):
    return jnp.maximum(z, 0.0) + jnp.log1p(jnp.exp(-jnp.abs(z)))


def _gelu_and_grad(x):
    c = 0.7978845608028654
    inner = c * (x + 0.044715 * x * x * x)
    t = jnp.tanh(inner)
    g = 0.5 * x * (1.0 + t)
    dg = 0.5 * (1.0 + t) + 0.5 * x * (1.0 - t * t) * c * (1.0 + 3.0 * 0.044715 * x * x)
    return g, dg


def fwd_proj(x, g1, wt_in, l):
    s = x.shape[0]
    tm = min(512, s)
    tn = 1408

    def body(x_ref, g_ref, w_ref, o_ref, xn_ref):
        @pl.when(pl.program_id(1) == 0)
        def _():
            xv = x_ref[...]
            r = lax.rsqrt(jnp.mean(xv * xv, axis=-1, keepdims=True) + EPS)
            xn_ref[...] = (xv * r * g_ref[l:l + 1, :]).astype(BF16)

        o_ref[...] = _dot(xn_ref[...], w_ref[...], _NT).astype(BF16)

    return pl.pallas_call(
        body, grid=(s // tm, IN_W // tn),
        in_specs=[pl.BlockSpec((tm, D), lambda i, j: (i, 0)),
                  pl.BlockSpec((DEPTH, D), lambda i, j: (0, 0)),
                  pl.BlockSpec((None, tn, D), lambda i, j: (l, j, 0))],
        out_specs=pl.BlockSpec((tm, tn), lambda i, j: (i, j)),
        out_shape=_sds((s, IN_W), BF16),
        scratch_shapes=[pltpu.VMEM((tm, D), BF16)],
        compiler_params=_cparams(2), name=f"fwd_proj{l}")(x, g1, wt_in)


def _scan_fwd(a_ref, u_ref, h_ref, h0, n_rows):
    row = lax.broadcasted_iota(jnp.int32, (8, BW), 0)

    def body(g, hprev):
        r = pl.multiple_of(g * 8, 8)
        a = a_ref[pl.ds(r, 8), :]
        u = u_ref[pl.ds(r, 8), :]
        for sft in (1, 2, 4):
            a_sh = jnp.where(row >= sft, pltpu.roll(a, sft, 0), 1.0)
            u_sh = jnp.where(row >= sft, pltpu.roll(u, sft, 0), 0.0)
            u = u + a * u_sh
            a = a * a_sh
        h = u + a * hprev
        h_ref[pl.ds(r, 8), :] = h
        return h[7:8, :]

    return lax.fori_loop(0, n_rows // 8, body, h0)


def _scan_bwd(b_ref, g_ref, o_ref, c0, n_rows):
    row = lax.broadcasted_iota(jnp.int32, (8, BW), 0)

    def body(k, cnext):
        r = pl.multiple_of((n_rows // 8 - 1 - k) * 8, 8)
        b = b_ref[pl.ds(r, 8), :]
        g = g_ref[pl.ds(r, 8), :]
        for sft in (1, 2, 4):
            b_sh = jnp.where(row < 8 - sft, pltpu.roll(b, 8 - sft, 0), 1.0)
            g_sh = jnp.where(row < 8 - sft, pltpu.roll(g, 8 - sft, 0), 0.0)
            g = g + b * g_sh
            b = b * b_sh
        o = g + b * cnext
        o_ref[pl.ds(r, 8), :] = o
        return o[0:1, :]

    return lax.fori_loop(0, n_rows // 8, body, c0)


def _branch_fwd_math(cur_ref, halo_ref, cw_ref, vec_ref, wx_ref, wa_ref, bufa, bufb, bufd, first, t):
    def halo(c0):
        v = halo_ref[:, c0:c0 + BW].astype(F32)
        return jnp.where(first, 0.0, v)

    def cur(c0):
        return cur_ref[:, c0:c0 + BW].astype(F32)

    out = {}
    bufa[0:HALO, :] = halo(C_AX)
    bufa[HALO:HALO + t, :] = cur(C_AX)
    ca = jnp.zeros((t, BW), F32) + vec_ref[V_CAB:V_CAB + 1, :]
    for k in range(CONV_A):
        ca = ca + cw_ref[CW_A + k:CW_A + k + 1, :] * bufa[pl.ds(HALO - (CONV_A - 1) + k, t), :]
    gi = _sigmoid(_dot(ca, wx_ref[...], _NN) + vec_ref[V_BX:V_BX + 1, :])
    gr = _sigmoid(_dot(ca, wa_ref[...], _NN) + vec_ref[V_BA:V_BA + 1, :])
    sp = _softplus(-vec_ref[V_LAM:V_LAM + 1, :])
    la = -LRU_C * sp * gr
    a = jnp.exp(la)
    mult = jnp.sqrt(_neg_expm1(2.0 * la))
    out.update(ca=ca, gi=gi, gr=gr, sp=sp, a=a, mult=mult)
    bufb[0:HALO, :] = halo(C_BC) * halo(C_BV)
    bufb[HALO:HALO + t, :] = cur(C_BC) * cur(C_BV)
    cb = jnp.zeros((t, BW), F32)
    for k in range(CONV_B):
        cb = cb + cw_ref[CW_B + k:CW_B + k + 1, :] * bufb[pl.ds(HALO - (CONV_B - 1) + k, t), :]
    out.update(cb=cb)
    bufd[0:HALO, :] = halo(C_D1) * _sigmoid(halo(C_D2))
    s2 = _sigmoid(cur(C_D2))
    bufd[HALO:HALO + t, :] = cur(C_D1) * s2
    cd = jnp.zeros((t, BW), F32) + vec_ref[V_CDB:V_CDB + 1, :]
    for k in range(CONV_D):
        cd = cd + cw_ref[CW_D + k:CW_D + k + 1, :] * bufd[pl.ds(HALO - (CONV_D - 1) + k, t), :]
    mu = jnp.mean(cd, axis=-1, keepdims=True)
    xc = cd - mu
    rstd = lax.rsqrt(jnp.mean(xc * xc, axis=-1, keepdims=True) + EPS)
    xh = xc * rstd
    ln = xh * vec_ref[V_LNG:V_LNG + 1, :] + vec_ref[V_LNB:V_LNB + 1, :]
    out.update(s2=s2, xh=xh, rstd=rstd, ln=ln)
    return out


def fwd_branch(proj, convw, vecs, wx_bd, wa_bd, l):
    s = proj.shape[0]
    t = min(256, s)

    def body(cur_ref, halo_ref, cw_ref, vec_ref, wx_ref, wa_ref, pre_ref, h_ref, bufa, bufb, bufd, a_s, u_s, hcar):
        first = pl.program_id(0) == 0

        @pl.when(first)
        def _():
            hcar[...] = jnp.zeros((1, BW), F32)

        v = _branch_fwd_math(cur_ref, halo_ref, cw_ref, vec_ref, wx_ref, wa_ref, bufa, bufb, bufd, first, t)
        a_s[...] = v["a"]
        u_s[...] = v["ca"] * v["gi"] * v["mult"]
        hcar[...] = _scan_fwd(a_s, u_s, h_ref, hcar[...], t)
        gg, _ = _gelu_and_grad(cur_ref[:, C_AG:C_AG + BW].astype(F32))
        pre_ref[:, 0:BW] = (h_ref[...] * gg).astype(BF16)
        pre_ref[:, BW:2 * BW] = (cur_ref[:, C_BB:C_BB + BW].astype(F32) * v["cb"]).astype(BF16)
        ln = v["ln"]
        pre_ref[:, 2 * BW:3 * BW] = (ln * _sigmoid(ln)).astype(BF16)

    hb = t // HALO
    return pl.pallas_call(
        body, grid=(s // t,),
        in_specs=[pl.BlockSpec((t, GL0), lambda i: (i, 0)),
                  pl.BlockSpec((HALO, GL0), lambda i: (jnp.maximum(i * hb - 1, 0), 0)),
                  pl.BlockSpec((None, CW_ROWS, BW), lambda i: (l, 0, 0)),
                  pl.BlockSpec((None, V_ROWS, BW), lambda i: (l, 0, 0)),
                  pl.BlockSpec((None, BW, BW), lambda i: (l, 0, 0)),
                  pl.BlockSpec((None, BW, BW), lambda i: (l, 0, 0))],
        out_specs=[pl.BlockSpec((t, 3 * BW), lambda i: (i, 0)), pl.BlockSpec((t, BW), lambda i: (i, 0))],
        out_shape=[_sds((s, 3 * BW), BF16), _sds((s, BW), F32)],
        scratch_shapes=[pltpu.VMEM((t + HALO, BW), F32)] * 3 + [pltpu.VMEM((t, BW), F32)] * 2 + [pltpu.VMEM((1, BW), F32)],
        compiler_params=_cparams(1), name=f"fwd_branch{l}")(proj, proj, convw, vecs, wx_bd, wa_bd)


def _attn_mask_bias():
    qi = lax.broadcasted_iota(jnp.int32, (ATT_BLK, 2 * ATT_BLK), 0)
    ki = lax.broadcasted_iota(jnp.int32, (ATT_BLK, 2 * ATT_BLK), 1)
    dist = qi + ATT_BLK - ki
    valid = (dist >= 0) & (dist < ATT_BLK)
    return dist.astype(F32), valid, ki


def _attn_probs(q_ref, kvp_ref, kvc_ref, vec_ref, h, distf, valid):
    hk = h // (N_HEADS // N_KV)
    slope = 2.0 ** (-8.0 * (h + 1) / N_HEADS)
    qh = q_ref[:, h * HD:(h + 1) * HD]
    k2 = jnp.concatenate([kvp_ref[:, hk * HD:(hk + 1) * HD], kvc_ref[:, hk * HD:(hk + 1) * HD]], axis=0)
    v2 = jnp.concatenate([kvp_ref[:, (N_KV + hk) * HD:(N_KV + hk + 1) * HD],
                          kvc_ref[:, (N_KV + hk) * HD:(N_KV + hk + 1) * HD]], axis=0)
    sc = _dot(qh, k2, _NT) * (HD ** -0.5) - slope * distf
    sc = jnp.where(valid, sc, NEG_INF)
    sink = vec_ref[V_SINK:V_SINK + 1, h:h + 1]
    m = jnp.maximum(jnp.max(sc, axis=-1, keepdims=True), sink)
    p = jnp.exp(sc - m)
    es = jnp.exp(sink - m)
    inv = 1.0 / (jnp.sum(p, axis=-1, keepdims=True) + es)
    return qh, k2, v2, p * inv, es * inv


def fwd_attn(proj, vecs, l):
    s = proj.shape[0]
    nb = s // ATT_BLK

    def body(q_ref, kvp_ref, kvc_ref, vec_ref, o_ref):
        distf, valid, ki = _attn_mask_bias()
        valid = valid & ((pl.program_id(0) > 0) | (ki >= ATT_BLK))
        for h in range(N_HEADS):
            _, _, v2, p, _ = _attn_probs(q_ref, kvp_ref, kvc_ref, vec_ref, h, distf, valid)
            o_ref[:, h * HD:(h + 1) * HD] = _dot(p, v2, _NN).astype(BF16)

    return pl.pallas_call(
        body, grid=(nb,),
        in_specs=[pl.BlockSpec((ATT_BLK, BW), lambda i: (i, C_Q // BW)),
                  pl.BlockSpec((ATT_BLK, 256), lambda i: (jnp.maximum(i - 1, 0), C_K // 256)),
                  pl.BlockSpec((ATT_BLK, 256), lambda i: (i, C_K // 256)),
                  pl.BlockSpec((None, V_ROWS, BW), lambda i: (l, 0, 0))],
        out_specs=pl.BlockSpec((ATT_BLK, BW), lambda i: (i, 0)),
        out_shape=_sds((s, BW), BF16),
        compiler_params=_cparams(1), name=f"fwd_attn{l}")(proj, proj, proj, vecs)


def fwd_merge(x, proj, pre_abd, pre_c, wt_a, wt_b, wt_c, wt_d, w_o, l):
    s = x.shape[0]
    tm = min(256, s)

    def body(x_ref, gl_ref, pabd_ref, pc_ref, wa_ref, wb_ref, wc_ref, wd_ref, wo_ref, y_ref, mg_ref, h1_ref):
        pres = (pabd_ref[:, 0:BW], pabd_ref[:, BW:2 * BW], pc_ref[...], pabd_ref[:, 2 * BW:3 * BW])
        merged = jnp.zeros((tm, D), F32)
        for k, (pre, w_ref) in enumerate(zip(pres, (wa_ref, wb_ref, wc_ref, wd_ref))):
            yk = _dot(pre, w_ref[...], _NT)
            y_ref[:, k * D:(k + 1) * D] = yk.astype(BF16)
            merged = merged + _sigmoid(gl_ref[:, k * D:(k + 1) * D].astype(F32)) * yk
        mg_ref[...] = merged.astype(BF16)
        h1_ref[...] = x_ref[...] + _dot(merged, wo_ref[...], _NN)

    wspec = pl.BlockSpec((None, D, BW), lambda i: (l, 0, 0))
    return pl.pallas_call(
        body, grid=(s // tm,),
        in_specs=[pl.BlockSpec((tm, D), lambda i: (i, 0)),
                  pl.BlockSpec((E(tm), E(4 * D)), lambda i: (i * tm, GL0)),
                  pl.BlockSpec((tm, 3 * BW), lambda i: (i, 0)),
                  pl.BlockSpec((tm, BW), lambda i: (i, 0)),
                  wspec, wspec, wspec, wspec,
                  pl.BlockSpec((None, D, D), lambda i: (l, 0, 0))],
        out_specs=[pl.BlockSpec((tm, 4 * D), lambda i: (i, 0)), pl.BlockSpec((tm, D), lambda i: (i, 0)),
                   pl.BlockSpec((tm, D), lambda i: (i, 0))],
        out_shape=[_sds((s, 4 * D), BF16), _sds((s, D), BF16), _sds((s, D), F32)],
        compiler_params=_cparams(1), name=f"fwd_merge{l}")(x, proj, pre_abd, pre_c, wt_a, wt_b, wt_c, wt_d, w_o)


def fwd_ffn(h1, g2, wt_gate, wt_up, w_down, l):
    s = h1.shape[0]
    tm = min(512, s)
    fc = FF // 2

    def body(h_ref, g_ref, wg_ref, wu_ref, wd_ref, xo_ref, fg_ref, fu_ref, hn_ref, acc_ref):
        j = pl.program_id(1)

        @pl.when(j == 0)
        def _():
            hv = h_ref[...]
            r = lax.rsqrt(jnp.mean(hv * hv, axis=-1, keepdims=True) + EPS)
            hn_ref[...] = (hv * r * g_ref[l:l + 1, :]).astype(BF16)
            acc_ref[...] = hv

        fg = _dot(hn_ref[...], wg_ref[...], _NT)
        fu = _dot(hn_ref[...], wu_ref[...], _NT)
        fg_ref[...] = fg.astype(BF16)
        fu_ref[...] = fu.astype(BF16)
        acc_ref[...] += _dot(fg * _sigmoid(fg) * fu, wd_ref[...], _NN)

        @pl.when(j == pl.num_programs(1) - 1)
        def _():
            xo_ref[...] = acc_ref[...]

    wspec = pl.BlockSpec((None, fc, D), lambda i, j: (l, j, 0))
    return pl.pallas_call(
        body, grid=(s // tm, FF // fc),
        in_specs=[pl.BlockSpec((tm, D), lambda i, j: (i, 0)), pl.BlockSpec((DEPTH, D), lambda i, j: (0, 0)),
                  wspec, wspec, wspec],
        out_specs=[pl.BlockSpec((tm, D), lambda i, j: (i, 0)), pl.BlockSpec((tm, fc), lambda i, j: (i, j)),
                   pl.BlockSpec((tm, fc), lambda i, j: (i, j))],
        out_shape=[_sds((s, D), F32), _sds((s, FF), BF16), _sds((s, FF), BF16)],
        scratch_shapes=[pltpu.VMEM((tm, D), BF16), pltpu.VMEM((tm, D), F32)],
        compiler_params=_cparams(2), name=f"fwd_ffn{l}")(h1, g2, wt_gate, wt_up, w_down)


def loss_head(x, gf, target):
    s = x.shape[0]
    tm = min(512, s)

    def body(x_ref, g_ref, t_ref, dx_ref, st_ref):
        @pl.when(pl.program_id(0) == 0)
        def _():
            st_ref[...] = jnp.zeros((8, D), F32)

        xv = x_ref[...]
        g = g_ref[...]
        r = lax.rsqrt(jnp.mean(xv * xv, axis=-1, keepdims=True) + EPS)
        n = xv * r
        err = n * g - t_ref[...]
        dy = err * (1.0 / D)
        dn = dy * g
        dx_ref[...] = r * (dn - n * jnp.mean(dn * n, axis=-1, keepdims=True))
        st_ref[0:1, :] += jnp.sum(dy * n, axis=0, keepdims=True)
        lsum = 0.5 * jnp.sum(jnp.mean(err * err, axis=-1, keepdims=True), axis=0, keepdims=True)
        st_ref[1:2, :] += jnp.broadcast_to(lsum, (1, D))

    return pl.pallas_call(
        body, grid=(s // tm,),
        in_specs=[pl.BlockSpec((tm, D), lambda i: (i, 0)), pl.BlockSpec((1, D), lambda i: (0, 0)),
                  pl.BlockSpec((tm, D), lambda i: (i, 0))],
        out_specs=[pl.BlockSpec((tm, D), lambda i: (i, 0)), pl.BlockSpec((8, D), lambda i: (0, 0))],
        out_shape=[_sds((s, D), F32), _sds((8, D), F32)],
        compiler_params=_cparams(1), name="loss_head")(x, gf, target)


def _edge_index(j, i, n_j, n_i):
    return jnp.where((j == 0) | (j == n_j - 1), i, n_i - 1)


def bwd_ffn(dxo, h1, fg, fu, g2, wt_gate, wt_up, w_down, l):
    s = h1.shape[0]
    tm = min(512, s)
    fc = 256
    n_j, n_i = FF // fc, s // tm

    def body(dxo_ref, h_ref, fg_ref, fu_ref, g_ref, wg_ref, wu_ref, wd_ref,
             dh_ref, dwg_ref, dwu_ref, dwd_ref, st_ref, dhn, dxo_b, hn_b, ag, au, ad):
        j, i = pl.program_id(0), pl.program_id(1)
        rows = pl.ds(pl.multiple_of(i * tm, tm), tm)
        g = g_ref[l:l + 1, :]

        @pl.when(j == 0)
        def _():
            hv = h_ref[...]
            r = lax.rsqrt(jnp.mean(hv * hv, axis=-1, keepdims=True) + EPS)
            hn_b[rows, :] = (hv * r * g).astype(BF16)
            dxo_b[rows, :] = dxo_ref[...].astype(BF16)
            dhn[rows, :] = jnp.zeros((tm, D), F32)

        @pl.when((j == 0) & (i == 0))
        def _():
            st_ref[...] = jnp.zeros((8, D), F32)

        @pl.when(i == 0)
        def _():
            ag[...] = jnp.zeros((fc, D), F32)
            au[...] = jnp.zeros((fc, D), F32)
            ad[...] = jnp.zeros((fc, D), F32)

        fgv = fg_ref[...].astype(F32)
        fuv = fu_ref[...].astype(F32)
        sg = _sigmoid(fgv)
        sil = fgv * sg
        dxb = dxo_b[rows, :]
        hnb = hn_b[rows, :]
        d_act = _dot(dxb, wd_ref[...], _NT)
        ad[...] += _dot(sil * fuv, dxb, _TN)
        d_fg = (d_act * fuv * (sg * (1.0 + fgv * (1.0 - sg)))).astype(BF16)
        d_fu = (d_act * sil).astype(BF16)
        ag[...] += _dot(d_fg, hnb, _TN)
        au[...] += _dot(d_fu, hnb, _TN)
        dhn[rows, :] += _dot(d_fg, wg_ref[...], _NN) + _dot(d_fu, wu_ref[...], _NN)

        @pl.when(i == n_i - 1)
        def _():
            dwg_ref[...] = ag[...].astype(BF16)
            dwu_ref[...] = au[...].astype(BF16)
            dwd_ref[...] = ad[...].astype(BF16)

        @pl.when(j == n_j - 1)
        def _():
            hv = h_ref[...]
            r = lax.rsqrt(jnp.mean(hv * hv, axis=-1, keepdims=True) + EPS)
            n = hv * r
            dv = dhn[rows, :]
            dn = dv * g
            dh_ref[...] = dxo_ref[...] + r * (dn - n * jnp.mean(dn * n, axis=-1, keepdims=True))
            st_ref[0:1, :] += jnp.sum(dv * n, axis=0, keepdims=True)

    edge = lambda j, i: (_edge_index(j, i, n_j, n_i), 0)
    wspec = pl.BlockSpec((None, fc, D), lambda j, i: (l, j, 0))
    dwspec = pl.BlockSpec((fc, D), lambda j, i: (j, 0))
    return pl.pallas_call(
        body, grid=(n_j, n_i),
        in_specs=[pl.BlockSpec((tm, D), edge), pl.BlockSpec((tm, D), edge),
                  pl.BlockSpec((tm, fc), lambda j, i: (i, j)), pl.BlockSpec((tm, fc), lambda j, i: (i, j)),
                  pl.BlockSpec((DEPTH, D), lambda j, i: (0, 0)), wspec, wspec, wspec],
        out_specs=[pl.BlockSpec((tm, D), lambda j, i: (jnp.where(j == n_j - 1, i, 0), 0)),
                   dwspec, dwspec, dwspec, pl.BlockSpec((8, D), lambda j, i: (0, 0))],
        out_shape=[_sds((s, D), F32), _sds((FF, D), BF16), _sds((FF, D), BF16), _sds((FF, D), BF16), _sds((8, D), F32)],
        scratch_shapes=[pltpu.VMEM((s, D), F32), pltpu.VMEM((s, D), BF16), pltpu.VMEM((s, D), BF16),
                        pltpu.VMEM((fc, D), F32), pltpu.VMEM((fc, D), F32), pltpu.VMEM((fc, D), F32)],
        compiler_params=_cparams(2), name=f"bwd_ffn{l}")(dxo, h1, fg, fu, g2, wt_gate, wt_up, w_down)


def bwd_merge(dh1, y4, proj, merged, pre_abd, pre_c, wt_a, wt_b, wt_c, wt_d, w_o, l):
    s = dh1.shape[0]
    tm = min(256, s)
    n_i = s // tm

    def body(dh_ref, y_ref, gl_ref, mg_ref, pabd_ref, pc_ref, wa_ref, wb_ref, wc_ref, wd_ref, wo_ref,
             dgl_ref, dpre_ref, dwo_ref, dwa_ref, dwb_ref, dwc_ref, dwd_ref, ao, aa, ab, ac, ad):
        i = pl.program_id(0)
        accs = (aa, ab, ac, ad)

        @pl.when(i == 0)
        def _():
            ao[...] = jnp.zeros((D, D), F32)
            for acc in accs:
                acc[...] = jnp.zeros((D, BW), F32)

        dhb = dh_ref[...].astype(BF16)
        dmg = _dot(dhb, wo_ref[...], _NT)
        ao[...] += _dot(mg_ref[...], dhb, _TN)
        pres = (pabd_ref[:, 0:BW], pabd_ref[:, BW:2 * BW], pc_ref[...], pabd_ref[:, 2 * BW:3 * BW])
        for k, (pre, w_ref, acc) in enumerate(zip(pres, (wa_ref, wb_ref, wc_ref, wd_ref), accs)):
            gk = _sigmoid(gl_ref[:, k * D:(k + 1) * D].astype(F32))
            yk = y_ref[:, k * D:(k + 1) * D].astype(F32)
            dgl_ref[:, k * D:(k + 1) * D] = (dmg * yk * gk * (1.0 - gk)).astype(BF16)
            dyk = (dmg * gk).astype(BF16)
            dpre_ref[:, k * BW:(k + 1) * BW] = _dot(dyk, w_ref[...], _NN).astype(BF16)
            acc[...] += _dot(dyk, pre, _TN)

        @pl.when(i == n_i - 1)
        def _():
            dwo_ref[...] = ao[...].astype(BF16)
            for o_ref, acc in zip((dwa_ref, dwb_ref, dwc_ref, dwd_ref), accs):
                o_ref[...] = acc[...].astype(BF16)

    wspec = pl.BlockSpec((None, D, BW), lambda i: (l, 0, 0))
    dwspec = pl.BlockSpec((D, BW), lambda i: (0, 0))
    return pl.pallas_call(
        body, grid=(n_i,),
        in_specs=[pl.BlockSpec((tm, D), lambda i: (i, 0)),
                  pl.BlockSpec((tm, 4 * D), lambda i: (i, 0)),
                  pl.BlockSpec((E(tm), E(4 * D)), lambda i: (i * tm, GL0)),
                  pl.BlockSpec((tm, D), lambda i: (i, 0)),
                  pl.BlockSpec((tm, 3 * BW), lambda i: (i, 0)),
                  pl.BlockSpec((tm, BW), lambda i: (i, 0)),
                  wspec, wspec, wspec, wspec,
                  pl.BlockSpec((None, D, D), lambda i: (l, 0, 0))],
        out_specs=[pl.BlockSpec((E(tm), E(4 * D)), lambda i: (i * tm, GL0)),
                   pl.BlockSpec((tm, 4 * BW), lambda i: (i, 0)),
                   pl.BlockSpec((D, D), lambda i: (0, 0)), dwspec, dwspec, dwspec, dwspec],
        out_shape=[_sds((s, IN_W), BF16), _sds((s, 4 * BW), BF16), _sds((D, D), BF16)] + [_sds((D, BW), BF16)] * 4,
        scratch_shapes=[pltpu.VMEM((D, D), F32)] + [pltpu.VMEM((D, BW), F32)] * 4,
        compiler_params=_cparams(1), name=f"bwd_merge{l}")(
            dh1, y4, proj, merged, pre_abd, pre_c, wt_a, wt_b, wt_c, wt_d, w_o)


def bwd_attn(proj, dpre, vecs, l):
    s = proj.shape[0]
    nb = s // ATT_BLK
    grp = N_HEADS // N_KV

    def body(q_ref, kvp_ref, kvc_ref, do_ref, vec_ref, dq_ref, dkc_ref, dkp_ref, st_ref):
        @pl.when(pl.program_id(0) == 0)
        def _():
            st_ref[...] = jnp.zeros((8, 128), F32)

        distf, valid, ki = _attn_mask_bias()
        valid = valid & ((pl.program_id(0) > 0) | (ki >= ATT_BLK))
        lane = lax.broadcasted_iota(jnp.int32, (1, 128), 1)
        dsink = jnp.zeros((1, 128), F32)
        for hk in range(N_KV):
            dk2 = jnp.zeros((2 * ATT_BLK, HD), F32)
            dv2 = jnp.zeros((2 * ATT_BLK, HD), F32)
            for gq in range(grp):
                h = hk * grp + gq
                qh, k2, v2, p, ps = _attn_probs(q_ref, kvp_ref, kvc_ref, vec_ref, h, distf, valid)
                doh = do_ref[:, h * HD:(h + 1) * HD]
                dp = _dot(doh, v2, _NT)
                delta = jnp.sum(p * dp, axis=-1, keepdims=True)
                ds = p * (dp - delta) * (HD ** -0.5)
                dq_ref[:, h * HD:(h + 1) * HD] = _dot(ds, k2, _NN).astype(BF16)
                dk2 = dk2 + _dot(ds, qh, _TN)
                dv2 = dv2 + _dot(p, doh, _TN)
                dsink = dsink + jnp.where(lane == h, -jnp.sum(ps * delta, axis=0, keepdims=True), 0.0)
            dkp_ref[:, hk * HD:(hk + 1) * HD] = dk2[0:ATT_BLK].astype(BF16)
            dkc_ref[:, hk * HD:(hk + 1) * HD] = dk2[ATT_BLK:].astype(BF16)
            dkp_ref[:, (N_KV + hk) * HD:(N_KV + hk + 1) * HD] = dv2[0:ATT_BLK].astype(BF16)
            dkc_ref[:, (N_KV + hk) * HD:(N_KV + hk + 1) * HD] = dv2[ATT_BLK:].astype(BF16)
        st_ref[0:1, :] += dsink

    return pl.pallas_call(
        body, grid=(nb,),
        in_specs=[pl.BlockSpec((ATT_BLK, BW), lambda i: (i, C_Q // BW)),
                  pl.BlockSpec((ATT_BLK, 256), lambda i: (jnp.maximum(i - 1, 0), C_K // 256)),
                  pl.BlockSpec((ATT_BLK, 256), lambda i: (i, C_K // 256)),
                  pl.BlockSpec((ATT_BLK, BW), lambda i: (i, 2)),
                  pl.BlockSpec((None, V_ROWS, BW), lambda i: (l, 0, 0))],
        out_specs=[pl.BlockSpec((ATT_BLK, BW), lambda i: (i, 0)), pl.BlockSpec((ATT_BLK, 256), lambda i: (i, 0)),
                   pl.BlockSpec((ATT_BLK, 256), lambda i: (i, 0)), pl.BlockSpec((8, 128), lambda i: (0, 0))],
        out_shape=[_sds((s, BW), BF16), _sds((s, 256), BF16), _sds((s, 256), BF16), _sds((8, 128), F32)],
        compiler_params=_cparams(1), name=f"bwd_attn{l}")(proj, proj, proj, dpre, vecs)


def bwd_branch(proj, dproj, dpre, h, dq, dkc, dkp, convw, vecs, wx_bd, wa_bd, l):
    s = proj.shape[0]
    t = 2 * ATT_BLK
    nt = s // t
    nb = s // ATT_BLK
    hb = t // HALO

    def body(cur_ref, halo_ref, dpre_ref, h_ref, hp_ref, dq_ref, dkc_ref, dkp1_ref, dkp2_ref,
             cw_ref, vec_ref, wx_ref, wa_ref, dproj_in, dp_ref, dcw_ref, dvec_ref, dwx_ref, dwa_ref,
             bufa, bufb, bufd, a_ext, hbuf, b_s, g_s, dh_s, ga, gb, gd, dhcar):
        del dproj_in
        step = pl.program_id(0)
        ti = nt - 1 - step
        first = ti == 0

        @pl.when(step == 0)
        def _():
            dcw_ref[...] = jnp.zeros((CW_ROWS, BW), F32)
            dvec_ref[...] = jnp.zeros((V_ROWS, BW), F32)
            dwx_ref[...] = jnp.zeros((BW, BW), F32)
            dwa_ref[...] = jnp.zeros((BW, BW), F32)
            dhcar[...] = jnp.zeros((1, BW), F32)
            a_ext[t:t + 8, :] = jnp.zeros((8, BW), F32)
            ga[t:t + 8, :] = jnp.zeros((8, BW), F32)
            gb[t:t + 8, :] = jnp.zeros((8, BW), F32)
            gd[t:t + HALO, :] = jnp.zeros((HALO, BW), F32)

        def cur(c0):
            return cur_ref[:, c0:c0 + BW].astype(F32)

        def rsum(v):
            return jnp.sum(v, axis=0, keepdims=True)

        def put(c0, v):
            dp_ref[:, c0:c0 + BW] = v.astype(BF16)

        v = _branch_fwd_math(cur_ref, halo_ref, cw_ref, vec_ref, wx_ref, wa_ref, bufa, bufb, bufd, first, t)
        ca, gi, gr, sp, a, mult = v["ca"], v["gi"], v["gr"], v["sp"], v["a"], v["mult"]
        dpa = dpre_ref[:, 0:BW].astype(F32)
        gg, dgg = _gelu_and_grad(cur(C_AG))
        hv = h_ref[...]
        put(C_AG, dpa * hv * dgg)
        a_ext[0:t, :] = a
        b_s[...] = a_ext[pl.ds(1, t), :]
        g_s[...] = dpa * gg
        dhcar[...] = _scan_bwd(b_s, g_s, dh_s, dhcar[...], t)
        a_ext[t:t + 1, :] = a[0:1, :]
        dh = dh_s[...]
        hbuf[0:8, :] = jnp.where(first, 0.0, hp_ref[...])
        hbuf[8:8 + t, :] = hv
        da = dh * hbuf[pl.ds(7, t), :]
        d_ca = dh * gi * mult
        d_gi = dh * ca * mult
        d_mult = dh * ca * gi
        d_la = da * a - d_mult * (a * a) / mult
        lam = vec_ref[V_LAM:V_LAM + 1, :]
        dvec_ref[V_LAM:V_LAM + 1, :] += rsum(d_la * gr) * (LRU_C * _sigmoid(-lam))
        d_gr = d_la * (-LRU_C * sp)
        d_zr = d_gr * gr * (1.0 - gr)
        d_zi = d_gi * gi * (1.0 - gi)
        dvec_ref[V_BA:V_BA + 1, :] += rsum(d_zr)
        dvec_ref[V_BX:V_BX + 1, :] += rsum(d_zi)
        dwa_ref[...] += _dot(ca, d_zr, _TN)
        dwx_ref[...] += _dot(ca, d_zi, _TN)
        d_ca = d_ca + _dot(d_zi, wx_ref[...], _NT) + _dot(d_zr, wa_ref[...], _NT)
        dvec_ref[V_CAB:V_CAB + 1, :] += rsum(d_ca)
        ga[0:t, :] = d_ca
        d_ax = jnp.zeros((t, BW), F32)
        for k in range(CONV_A):
            d_ax = d_ax + cw_ref[CW_A + k:CW_A + k + 1, :] * ga[pl.ds(CONV_A - 1 - k, t), :]
            dcw_ref[CW_A + k:CW_A + k + 1, :] += rsum(d_ca * bufa[pl.ds(HALO - (CONV_A - 1) + k, t), :])
        ga[t:t + 8, :] = d_ca[0:8, :]
        put(C_AX, d_ax)
        dpb = dpre_ref[:, BW:2 * BW].astype(F32)
        put(C_BB, dpb * v["cb"])
        d_cb = dpb * cur(C_BB)
        gb[0:t, :] = d_cb
        d_cbin = jnp.zeros((t, BW), F32)
        for k in range(CONV_B):
            d_cbin = d_cbin + cw_ref[CW_B + k:CW_B + k + 1, :] * gb[pl.ds(CONV_B - 1 - k, t), :]
            dcw_ref[CW_B + k:CW_B + k + 1, :] += rsum(d_cb * bufb[pl.ds(HALO - (CONV_B - 1) + k, t), :])
        gb[t:t + 8, :] = d_cb[0:8, :]
        put(C_BC, d_cbin * cur(C_BV))
        put(C_BV, d_cbin * cur(C_BC))
        dpd = dpre_ref[:, 3 * BW:4 * BW].astype(F32)
        ln, xh, rstd, s2 = v["ln"], v["xh"], v["rstd"], v["s2"]
        sg = _sigmoid(ln)
        d_ln = dpd * sg * (1.0 + ln * (1.0 - sg))
        dvec_ref[V_LNG:V_LNG + 1, :] += rsum(d_ln * xh)
        dvec_ref[V_LNB:V_LNB + 1, :] += rsum(d_ln)
        d_xh = d_ln * vec_ref[V_LNG:V_LNG + 1, :]
        d_cd = rstd * (d_xh - jnp.mean(d_xh, axis=-1, keepdims=True)
                       - xh * jnp.mean(d_xh * xh, axis=-1, keepdims=True))
        dvec_ref[V_CDB:V_CDB + 1, :] += rsum(d_cd)
        gd[0:t, :] = d_cd
        d_dg = jnp.zeros((t, BW), F32)
        for k in range(CONV_D):
            d_dg = d_dg + cw_ref[CW_D + k:CW_D + k + 1, :] * gd[pl.ds(CONV_D - 1 - k, t), :]
            dcw_ref[CW_D + k:CW_D + k + 1, :] += rsum(d_cd * bufd[pl.ds(HALO - (CONV_D - 1) + k, t), :])
        gd[t:t + HALO, :] = d_cd[0:HALO, :]
        put(C_D1, d_dg * s2)
        put(C_D2, d_dg * cur(C_D1) * s2 * (1.0 - s2))
        dp_ref[:, C_Q:C_Q + BW] = dq_ref[...]
        dkp2 = jnp.where(step == 0, 0.0, dkp2_ref[...].astype(F32))
        dp_ref[0:ATT_BLK, C_K:C_K + 256] = (dkc_ref[0:ATT_BLK, :].astype(F32) + dkp1_ref[...].astype(F32)).astype(BF16)
        dp_ref[ATT_BLK:t, C_K:C_K + 256] = (dkc_ref[ATT_BLK:t, :].astype(F32) + dkp2).astype(BF16)

    rev = lambda i: nt - 1 - i
    full = lambda r, c: pl.BlockSpec((r, c), lambda i: (0, 0))
    return pl.pallas_call(
        body, grid=(nt,),
        in_specs=[pl.BlockSpec((t, GL0), lambda i: (rev(i), 0)),
                  pl.BlockSpec((HALO, GL0), lambda i: (jnp.maximum(rev(i) * hb - 1, 0), 0)),
                  pl.BlockSpec((t, 4 * BW), lambda i: (rev(i), 0)),
                  pl.BlockSpec((t, BW), lambda i: (rev(i), 0)),
                  pl.BlockSpec((8, BW), lambda i: (jnp.maximum(rev(i) * (t // 8) - 1, 0), 0)),
                  pl.BlockSpec((t, BW), lambda i: (rev(i), 0)),
                  pl.BlockSpec((t, 256), lambda i: (rev(i), 0)),
                  pl.BlockSpec((ATT_BLK, 256), lambda i: (2 * rev(i) + 1, 0)),
                  pl.BlockSpec((ATT_BLK, 256), lambda i: (jnp.minimum(2 * rev(i) + 2, nb - 1), 0)),
                  pl.BlockSpec((None, CW_ROWS, BW), lambda i: (l, 0, 0)),
                  pl.BlockSpec((None, V_ROWS, BW), lambda i: (l, 0, 0)),
                  pl.BlockSpec((None, BW, BW), lambda i: (l, 0, 0)),
                  pl.BlockSpec((None, BW, BW), lambda i: (l, 0, 0)),
                  pl.BlockSpec(memory_space=pl.ANY)],
        out_specs=[pl.BlockSpec((t, GL0), lambda i: (rev(i), 0)),
                   full(CW_ROWS, BW), full(V_ROWS, BW), full(BW, BW), full(BW, BW)],
        out_shape=[_sds((s, IN_W), BF16), _sds((CW_ROWS, BW), F32), _sds((V_ROWS, BW), F32),
                   _sds((BW, BW), F32), _sds((BW, BW), F32)],
        scratch_shapes=[pltpu.VMEM((t + HALO, BW), F32)] * 3
        + [pltpu.VMEM((t + 8, BW), F32), pltpu.VMEM((t + 8, BW), F32)]
        + [pltpu.VMEM((t, BW), F32)] * 3
        + [pltpu.VMEM((t + 8, BW), F32), pltpu.VMEM((t + 8, BW), F32), pltpu.VMEM((t + HALO, BW), F32),
           pltpu.VMEM((1, BW), F32)],
        input_output_aliases={13: 0},
        compiler_params=_cparams(1), name=f"bwd_branch{l}")(
            proj, proj, dpre, h, h, dq, dkc, dkp, dkp, convw, vecs, wx_bd, wa_bd, dproj)


def bwd_proj(dproj, x, dh1, g1, wt_in, l):
    s = x.shape[0]
    tm = min(512, s)
    ck = 1408
    n_j, n_i = IN_W // ck, s // tm

    def body(dp_ref, x_ref, dh_ref, g_ref, w_ref, dx_ref, dw_ref, st_ref, dxn, xn_b, acc):
        j, i = pl.program_id(0), pl.program_id(1)
        rows = pl.ds(pl.multiple_of(i * tm, tm), tm)
        g = g_ref[l:l + 1, :]

        @pl.when(j == 0)
        def _():
            xv = x_ref[...]
            r = lax.rsqrt(jnp.mean(xv * xv, axis=-1, keepdims=True) + EPS)
            xn_b[rows, :] = (xv * r * g).astype(BF16)
            dxn[rows, :] = jnp.zeros((tm, D), F32)

        @pl.when((j == 0) & (i == 0))
        def _():
            st_ref[...] = jnp.zeros((8, D), F32)

        @pl.when(i == 0)
        def _():
            acc[...] = jnp.zeros((ck, D), F32)

        dp = dp_ref[...]
        dxn[rows, :] += _dot(dp, w_ref[...], _NN)
        acc[...] += _dot(dp, xn_b[rows, :], _TN)

        @pl.when(i == n_i - 1)
        def _():
            dw_ref[...] = acc[...].astype(BF16)

        @pl.when(j == n_j - 1)
        def _():
            xv = x_ref[...]
            r = lax.rsqrt(jnp.mean(xv * xv, axis=-1, keepdims=True) + EPS)
            n = xv * r
            dv = dxn[rows, :]
            dn = dv * g
            dx_ref[...] = dh_ref[...] + r * (dn - n * jnp.mean(dn * n, axis=-1, keepdims=True))
            st_ref[0:1, :] += jnp.sum(dv * n, axis=0, keepdims=True)

    lastrow = lambda j, i: (jnp.where(j == n_j - 1, i, 0), 0)
    return pl.pallas_call(
        body, grid=(n_j, n_i),
        in_specs=[pl.BlockSpec((tm, ck), lambda j, i: (i, j)),
                  pl.BlockSpec((tm, D), lambda j, i: (_edge_index(j, i, n_j, n_i), 0)),
                  pl.BlockSpec((tm, D), lastrow),
                  pl.BlockSpec((DEPTH, D), lambda j, i: (0, 0)),
                  pl.BlockSpec((None, ck, D), lambda j, i: (l, j, 0))],
        out_specs=[pl.BlockSpec((tm, D), lastrow), pl.BlockSpec((ck, D), lambda j, i: (j, 0)),
                   pl.BlockSpec((8, D), lambda j, i: (0, 0))],
        out_shape=[_sds((s, D), F32), _sds((IN_W, D), BF16), _sds((8, D), F32)],
        scratch_shapes=[pltpu.VMEM((s, D), F32), pltpu.VMEM((s, D), BF16), pltpu.VMEM((ck, D), F32)],
        compiler_params=_cparams(2), name=f"bwd_proj{l}")(dproj, x, dh1, g1, wt_in)


def _block_diag(w):
    nl, nb, bw, _ = w.shape
    eye = jnp.eye(nb, dtype=w.dtype)
    return jnp.einsum("lhij,hk->lhikj", w, eye).reshape(nl, nb * bw, nb * bw).astype(BF16)


def _diag_blocks(m):
    return jnp.stack([m[HD * h:HD * (h + 1), HD * h:HD * (h + 1)] for h in range(BW // HD)])


def local_step(x, target, norm1_g, norm2_g, final_g, convw, vecs, lru_wx, lru_wa, big):
    wx_bd, wa_bd = _block_diag(lru_wx), _block_diag(lru_wa)
    saved = []
    for l in range(DEPTH):
        proj = fwd_proj(x, norm1_g, big["in_t"], l)
        pre_abd, h = fwd_branch(proj, convw, vecs, wx_bd, wa_bd, l)
        pre_c = fwd_attn(proj, vecs, l)
        y4, merged, h1 = fwd_merge(x, proj, pre_abd, pre_c, big["a_t"], big["b_t"], big["c_t"], big["d_t"], big["o"], l)
        x_out, fg, fu = fwd_ffn(h1, norm2_g, big["gate_t"], big["up_t"], big["down"], l)
        saved.append((x, proj, pre_abd, h, pre_c, y4, merged, h1, fg, fu))
        x = x_out
    dx, head_stats = loss_head(x, final_g.reshape(1, D), target)
    grads = [None] * DEPTH
    for l in reversed(range(DEPTH)):
        x_in, proj, pre_abd, h, pre_c, y4, merged, h1, fg, fu = saved[l]
        dh1, d_gate, d_up, d_down, st_ffn = bwd_ffn(dx, h1, fg, fu, norm2_g, big["gate_t"], big["up_t"], big["down"], l)
        dproj, dpre, d_o, d_a, d_b, d_c, d_d = bwd_merge(
            dh1, y4, proj, merged, pre_abd, pre_c, big["a_t"], big["b_t"], big["c_t"], big["d_t"], big["o"], l)
        dq, dkc, dkp, st_attn = bwd_attn(proj, dpre, vecs, l)
        dproj, dcw, dvec, dwx, dwa = bwd_branch(proj, dproj, dpre, h, dq, dkc, dkp, convw, vecs, wx_bd, wa_bd, l)
        dx, d_in, st_proj = bwd_proj(dproj, x_in, dh1, norm1_g, big["in_t"], l)
        grads[l] = dict(in_t=d_in, a_t=d_a, b_t=d_b, c_t=d_c, d_t=d_d, o=d_o, gate_t=d_gate, up_t=d_up, down=d_down,
                        norm1_g=st_proj[0], norm2_g=st_ffn[0], convw=dcw, vecs=dvec, sinks=st_attn[0, :N_HEADS],
                        lru_wx=_diag_blocks(dwx), lru_wa=_diag_blocks(dwa))
    return head_stats, dx, grads


ANY = pl.BlockSpec(memory_space=pl.ANY)
MESH = pl.DeviceIdType.MESH
BIG = dict(in_t=("w_in", True), a_t=("w_a_out", True), b_t=("w_b_out", True), c_t=("w_c_out", True),
           d_t=("w_d_out", True), o=("w_o", False), gate_t=("w_ffn_gate", True), up_t=("w_ffn_up", True),
           down=("w_ffn_down", False))


def cast_transpose(w, name):
    nl, a, b = w.shape
    ta = min(256, a)

    def body(w_ref, o_ref):
        o_ref[...] = w_ref[...].T.astype(BF16)

    return pl.pallas_call(
        body, grid=(nl, a // ta),
        in_specs=[pl.BlockSpec((None, ta, b), lambda l, i: (l, i, 0))],
        out_specs=pl.BlockSpec((None, b, ta), lambda l, i: (l, 0, i)),
        out_shape=_sds((nl, b, a), BF16), compiler_params=_cparams(2), name=name)(w)


def _mesh_pos():
    return lax.axis_index("x"), lax.axis_index("y"), lax.axis_index("c")


def _dev_index(px, py, pc):
    return 4 * px + 2 * py + pc


def all_gather(shards, name):
    n = len(shards)

    def body(*refs):
        ins, outs = refs[:n], refs[n:2 * n]
        send_sems, recv_sems, local_sems = refs[2 * n:]
        x, y, c = _mesh_pos()
        me = (x, y, c)
        sibling = (x, y, 1 - c)
        chips = [(1 - x, y), (x, 1 - y), (1 - x, 1 - y)]

        def slot(k, dev):
            rows = ins[k].shape[1]
            return outs[k].at[:, pl.ds(_dev_index(*dev) * rows, rows), :]

        def copy(g, k, dev, to, src=None):
            return pltpu.make_async_remote_copy(
                src_ref=slot(k, dev) if src is None else src, dst_ref=slot(k, dev),
                send_sem=send_sems.at[g, k], recv_sem=recv_sems.at[g, k], device_id=to, device_id_type=MESH)

        local = [pltpu.make_async_copy(ins[k], slot(k, me), local_sems.at[k]) for k in range(n)]
        for cp in local:
            cp.start()
        first = [copy(0, k, me, sibling, src=ins[k]) for k in range(n)]
        for j, chip in enumerate(chips):
            first += [copy(1 + j, k, me, (*chip, c), src=ins[k]) for k in range(n)]
        for cp in first:
            cp.start()
        passed = []
        for j, chip in enumerate(chips):
            for k in range(n):
                copy(1 + j, k, (*chip, c), me).wait_recv()
            forward = [copy(4 + j, k, (*chip, c), sibling) for k in range(n)]
            for cp in forward:
                cp.start()
            passed += forward
        for k in range(n):
            copy(0, k, sibling, me).wait_recv()
        for j, chip in enumerate(chips):
            for k in range(n):
                copy(4 + j, k, (*chip, 1 - c), me).wait_recv()
        for cp in first + passed:
            cp.wait_send()
        for cp in local:
            cp.wait()

    return pl.pallas_call(
        body, in_specs=[ANY] * n, out_specs=[ANY] * n,
        out_shape=[_sds((a.shape[0], NDEV * a.shape[1], a.shape[2]), a.dtype) for a in shards],
        scratch_shapes=[pltpu.SemaphoreType.DMA((7, n)), pltpu.SemaphoreType.DMA((7, n)), pltpu.SemaphoreType.DMA((n,))],
        name=name)(*shards)


def reduce_scatter_send(grads, name):
    kinds = len(grads)
    flat = [g for per_layer in grads for g in per_layer]
    n = len(flat)

    def body(*refs):
        ins, outs = refs[:n], refs[n:n + kinds]
        send_sems, recv_sems, local_sems = refs[n + kinds:]
        x, y, c = _mesh_pos()
        me = _dev_index(x, y, c)

        def piece(q, dev):
            rows = ins[q].shape[0] // NDEV
            return ins[q].at[pl.ds(dev * rows, rows), :]

        local = [pltpu.make_async_copy(piece(q, me), outs[q // DEPTH].at[me, q % DEPTH], local_sems.at[q]) for q in range(n)]
        for cp in local:
            cp.start()
        sends = []
        for f in range(1, NDEV):
            peer = (1 - x if f & 4 else x, 1 - y if f & 2 else y, 1 - c if f & 1 else c)
            pidx = _dev_index(*peer)
            for q in range(n):
                sends.append((pltpu.make_async_remote_copy(
                    src_ref=piece(q, pidx), dst_ref=outs[q // DEPTH].at[me, q % DEPTH],
                    send_sem=send_sems.at[f - 1, q], recv_sem=recv_sems.at[f - 1, q], device_id=peer, device_id_type=MESH),
                    pltpu.make_async_remote_copy(
                    src_ref=piece(q, pidx), dst_ref=outs[q // DEPTH].at[pidx, q % DEPTH],
                    send_sem=send_sems.at[f - 1, q], recv_sem=recv_sems.at[f - 1, q], device_id=peer, device_id_type=MESH)))
        for send, _ in sends:
            send.start()
        for _, recv in sends:
            recv.wait_recv()
        for send, _ in sends:
            send.wait_send()
        for cp in local:
            cp.wait()

    return pl.pallas_call(
        body, in_specs=[ANY] * n, out_specs=[ANY] * kinds,
        out_shape=[_sds((NDEV, DEPTH, g[0].shape[0] // NDEV, g[0].shape[1]), g[0].dtype) for g in grads],
        scratch_shapes=[pltpu.SemaphoreType.DMA((NDEV - 1, n)), pltpu.SemaphoreType.DMA((NDEV - 1, n)),
                        pltpu.SemaphoreType.DMA((n,))],
        name=name)(*flat)


def _adamw(w, g, m, v):
    m = ADAM_B1 * m + (1.0 - ADAM_B1) * g
    v = ADAM_B2 * v + (1.0 - ADAM_B2) * (g * g)
    m_hat = m / (1.0 - ADAM_B1 ** ADAM_STEP)
    v_hat = v / (1.0 - ADAM_B2 ** ADAM_STEP)
    delta = -ADAM_LR * (m_hat / (jnp.sqrt(v_hat) + ADAM_EPS) + ADAM_WD * w)
    return delta, m, v


def adamw_big(contrib, w, m, v, transposed, name):
    _, nl, rows, cols = contrib.shape
    ct = 256 if transposed else cols

    def body(c_ref, w_ref, m_ref, v_ref, g_out, d_out, m_out, v_out):
        g = c_ref[0].astype(F32)
        for dev in range(1, NDEV):
            g = g + c_ref[dev].astype(F32)
        if transposed:
            g = g.T
        delta, mn, vn = _adamw(w_ref[...], g, m_ref[...], v_ref[...])
        g_out[...] = g
        d_out[...] = delta
        m_out[...] = mn
        v_out[...] = vn

    if transposed:
        wspec = pl.BlockSpec((None, ct, rows), lambda l, j: (l, j, 0))
    else:
        wspec = pl.BlockSpec((None, rows, ct), lambda l, j: (l, 0, j))
    return pl.pallas_call(
        body, grid=(nl, cols // ct),
        in_specs=[pl.BlockSpec((NDEV, None, rows, ct), lambda l, j: (0, l, 0, j)), wspec, wspec, wspec],
        out_specs=[wspec] * 4, out_shape=[_sds(w.shape, F32)] * 4,
        compiler_params=_cparams(2), name=name)(contrib, w, m, v)


def adamw_small(gathered, w, m, v):
    r = w.shape[0]

    def body(c_ref, w_ref, m_ref, v_ref, g_out, d_out, m_out, v_out):
        g = c_ref[0]
        for dev in range(1, NDEV):
            g = g + c_ref[dev]
        delta, mn, vn = _adamw(w_ref[...], g, m_ref[...], v_ref[...])
        g_out[...] = g
        d_out[...] = delta
        m_out[...] = mn
        v_out[...] = vn

    return pl.pallas_call(body, out_shape=[_sds((r, D), F32)] * 4, name="adamw_small",
                          compiler_params=pltpu.CompilerParams(vmem_limit_bytes=VMEM_LIMIT))(gathered, w, m, v)


VEC_NAMES = ("conv_a_b", "lru_bx", "lru_ba", "lru_lambda", "conv_d_b", "ln_d_g", "ln_d_b")
P_N1, P_N2, P_VEC, P_CONV, P_WX, P_WA, P_LAYER = 0, 1, 2, 6, 26, 58, 90
P_FINAL, P_LOSS, P_ROWS = DEPTH * P_LAYER, DEPTH * P_LAYER + 1, 184


def _stack_vecs(p):
    rows = [p[n] for n in VEC_NAMES] + [jnp.pad(p["sinks"], ((0, 0), (0, BW - N_HEADS)))]
    return jnp.stack(rows, axis=1)


def _stack_convs(p):
    nl, _, ch = p["conv_a_w"].shape
    z = jnp.zeros((nl, 1, ch), F32)
    return jnp.concatenate([p["conv_a_w"], p["conv_b_w"], z, p["conv_d_w"], z], axis=1)


def _pack_small(p, convw, extra_rows):
    vec = _stack_vecs(p)
    rows = []
    for l in range(DEPTH):
        rows += [p["norm1_g"][l].reshape(1, D), p["norm2_g"][l].reshape(1, D), vec[l].reshape(-1, D),
                 convw[l].reshape(-1, D), p["lru_wx"][l].reshape(-1, D), p["lru_wa"][l].reshape(-1, D)]
    rows += [p["final_g"].reshape(1, D)] + extra_rows
    pack = jnp.concatenate(rows, axis=0)
    return jnp.pad(pack, ((0, P_ROWS - pack.shape[0]), (0, 0)))


def _unpack_small(pack, ch0):
    per_layer = [pack[l * P_LAYER:(l + 1) * P_LAYER] for l in range(DEPTH)]
    out = dict(norm1_g=jnp.stack([q[P_N1] for q in per_layer]), norm2_g=jnp.stack([q[P_N2] for q in per_layer]),
               final_g=pack[P_FINAL])
    vec = jnp.stack([q[P_VEC:P_CONV].reshape(V_ROWS, BW) for q in per_layer])
    for i, n in enumerate(VEC_NAMES):
        out[n] = vec[:, i]
    out["sinks"] = vec[:, V_SINK, :N_HEADS]
    convw = jnp.stack([q[P_CONV:P_WX].reshape(CW_ROWS, BW) for q in per_layer])
    mine = lax.dynamic_slice_in_dim(convw, ch0, BW // NDEV, axis=2)
    out.update(conv_a_w=mine[:, CW_A:CW_A + CONV_A], conv_b_w=mine[:, CW_B:CW_B + CONV_B],
               conv_d_w=mine[:, CW_D:CW_D + CONV_D])
    out["lru_wx"] = jnp.stack([q[P_WX:P_WA].reshape(BW // HD, HD, HD) for q in per_layer])
    out["lru_wa"] = jnp.stack([q[P_WA:P_LAYER].reshape(BW // HD, HD, HD) for q in per_layer])
    return out


SMALL = ("norm1_g", "conv_a_w", "conv_a_b", "lru_wx", "lru_bx", "lru_wa", "lru_ba", "lru_lambda", "conv_b_w", "sinks",
         "conv_d_w", "conv_d_b", "ln_d_g", "ln_d_b", "norm2_g", "final_g")
WEIGHTS = ("norm1_g", "w_in", "conv_a_w", "conv_a_b", "lru_wx", "lru_bx", "lru_wa", "lru_ba", "lru_lambda", "w_a_out",
           "conv_b_w", "w_b_out", "sinks", "w_c_out", "conv_d_w", "conv_d_b", "ln_d_g", "ln_d_b", "w_d_out", "w_o",
           "norm2_g", "w_ffn_gate", "w_ffn_up", "w_ffn_down", "final_g")


def kernel(x, norm1_g, w_in, conv_a_w, conv_a_b, lru_wx, lru_bx, lru_wa, lru_ba, lru_lambda, w_a_out, conv_b_w, w_b_out, sinks, w_c_out, conv_d_w, conv_d_b, ln_d_g, ln_d_b, w_d_out, w_o, norm2_g, w_ffn_gate, w_ffn_up, w_ffn_down, final_g, loss_target, m_norm1_g, m_w_in, m_conv_a_w, m_conv_a_b, m_lru_wx, m_lru_bx, m_lru_wa, m_lru_ba, m_lru_lambda, m_w_a_out, m_conv_b_w, m_w_b_out, m_sinks, m_w_c_out, m_conv_d_w, m_conv_d_b, m_ln_d_g, m_ln_d_b, m_w_d_out, m_w_o, m_norm2_g, m_w_ffn_gate, m_w_ffn_up, m_w_ffn_down, m_final_g, v_norm1_g, v_w_in, v_conv_a_w, v_conv_a_b, v_lru_wx, v_lru_bx, v_lru_wa, v_lru_ba, v_lru_lambda, v_w_a_out, v_conv_b_w, v_w_b_out, v_sinks, v_w_c_out, v_conv_d_w, v_conv_d_b, v_ln_d_g, v_ln_d_b, v_w_d_out, v_w_o, v_norm2_g, v_w_ffn_gate, v_w_ffn_up, v_w_ffn_down, v_final_g):
    args = dict(locals())
    w = {n: args[n] for n in WEIGHTS}
    m = {n: args["m_" + n] for n in WEIGHTS}
    v = {n: args["v_" + n] for n in WEIGHTS}
    me = _dev_index(*_mesh_pos())
    ch0 = me * (BW // NDEV)

    shards = {k: cast_transpose(w[n], "prep_" + k) if tr else w[n].astype(BF16) for k, (n, tr) in BIG.items()}
    gathered = all_gather(list(shards.values()) + [_stack_convs(w)], "gather_weights")
    big = dict(zip(BIG, gathered[:-1]))
    convw = gathered[-1].reshape(DEPTH, NDEV, CW_ROWS, BW // NDEV).transpose(0, 2, 1, 3).reshape(DEPTH, CW_ROWS, BW)

    vecs = _stack_vecs(w)
    head_stats, grad_x, grads = local_step(x[0], loss_target[0], norm1_g, norm2_g, final_g, convw, vecs, lru_wx, lru_wa, big)

    contrib = reduce_scatter_send([[grads[l][k] for l in range(DEPTH)] for k in BIG], "scatter_grads")
    out = {}
    for (k, (n, tr)), cb in zip(BIG.items(), contrib):
        out[n] = adamw_big(cb, w[n], m[n], v[n], tr, "adamw_" + k)

    gsmall = {n: jnp.stack([grads[l][n] for l in range(DEPTH)]) for n in ("norm1_g", "norm2_g", "lru_wx", "lru_wa", "sinks")}
    gvec = jnp.stack([grads[l]["vecs"] for l in range(DEPTH)])
    gsmall.update({n: gvec[:, i] for i, n in enumerate(VEC_NAMES)})
    gsmall["final_g"] = head_stats[0]
    gpack = _pack_small(gsmall, jnp.stack([grads[l]["convw"] for l in range(DEPTH)]), [head_stats[1:2]])
    gall = all_gather([gpack[None]], "gather_small")[0].reshape(NDEV, P_ROWS, D)

    def padded_convs(p):
        return lax.dynamic_update_slice_in_dim(jnp.zeros((DEPTH, CW_ROWS, BW), F32), _stack_convs(p), ch0, axis=2)

    zero_row = [jnp.zeros((1, D), F32)]
    packs = adamw_small(gall, *[_pack_small(p, padded_convs(p), zero_row) for p in (w, m, v)])
    small = [_unpack_small(p, ch0) for p in packs]
    for n in SMALL:
        out[n] = [s[n] for s in small]
    loss = packs[0][P_LOSS, 0]
    return (loss, grad_x[None], *[out[n][0] for n in WEIGHTS], *[out[n][1] for n in WEIGHTS],
            *[out[n][2] for n in WEIGHTS], *[out[n][3] for n in WEIGHTS])
```

```python
import functools

import jax
import jax.numpy as jnp
from jax import lax
from jax.experimental import pallas as pl
from jax.experimental.pallas import tpu as pltpu

F32 = jnp.float32
BF16 = jnp.bfloat16
E = pl.Element

D = 1024
BW = 512
IN_W = 8448
GL0 = 4352
FF = 2816
N_HEADS = 8
N_KV = 2
HD = 64
ATT_BLK = 128
EPS = 1e-6
LRU_C = 8.0
NEG_INF = -1e30
DEPTH = 2
NDEV = 8
CONV_A, CONV_B, CONV_D = 4, 3, 31
C_AX, C_AG, C_BV, C_BC, C_BB, C_Q, C_K, C_V, C_D1, C_D2 = 0, 512, 1024, 1536, 2048, 2560, 3072, 3200, 3328, 3840
CW_A, CW_B, CW_D, CW_ROWS = 0, 4, 8, 40
V_CAB, V_BX, V_BA, V_LAM, V_CDB, V_LNG, V_LNB, V_SINK, V_ROWS = 0, 1, 2, 3, 4, 5, 6, 7, 8
HALO = 32

ADAM_LR, ADAM_B1, ADAM_B2, ADAM_EPS, ADAM_WD, ADAM_STEP = 0.001, 0.9, 0.999, 1e-08, 0.01, 10

VMEM_LIMIT = 56 * 1024 * 1024

_NN = (((1,), (0,)), ((), ()))
_NT = (((1,), (1,)), ((), ()))
_TN = (((0,), (0,)), ((), ()))


def _dot(a, b, dims):
    return lax.dot_general(a.astype(BF16), b.astype(BF16), dims, preferred_element_type=F32)


def _cparams(n_axes):
    return pltpu.CompilerParams(dimension_semantics=("arbitrary",) * n_axes, vmem_limit_bytes=VMEM_LIMIT)


def _sds(shape, dtype):
    return jax.ShapeDtypeStruct(tuple(shape), dtype)


def _sigmoid(x):
    return jax.nn.sigmoid(x)


def _neg_expm1(x):
    p = x * (1.0 + x * (0.5 + x * (1.0 / 6.0 + x * (1.0 / 24.0 + x * (1.0 / 120.0)))))
    return jnp.where(x > -0.1, -p, 1.0 - jnp.exp(x))


def _softplus(z):
    return jnp.maximum(z, 0.0) + jnp.log1p(jnp.exp(-jnp.abs(z)))


def _gelu_and_grad(x):
    c = 0.7978845608028654
    inner = c * (x + 0.044715 * x * x * x)
    t = jnp.tanh(inner)
    g = 0.5 * x * (1.0 + t)
    dg = 0.5 * (1.0 + t) + 0.5 * x * (1.0 - t * t) * c * (1.0 + 3.0 * 0.044715 * x * x)
    return g, dg


def fwd_proj(x, g1, wt_in, l):
    s = x.shape[0]
    tm = min(512, s)
    tn = 1408

    def body(x_ref, g_ref, w_ref, o_ref, xn_ref):
        @pl.when(pl.program_id(1) == 0)
        def _():
            xv = x_ref[...]
            r = lax.rsqrt(jnp.mean(xv * xv, axis=-1, keepdims=True) + EPS)
            xn_ref[...] = (xv * r * g_ref[l:l + 1, :]).astype(BF16)

        o_ref[...] = _dot(xn_ref[...], w_ref[...], _NT).astype(BF16)

    return pl.pallas_call(
        body, grid=(s // tm, IN_W // tn),
        in_specs=[pl.BlockSpec((tm, D), lambda i, j: (i, 0)),
                  pl.BlockSpec((DEPTH, D), lambda i, j: (0, 0)),
                  pl.BlockSpec((None, tn, D), lambda i, j: (l, j, 0))],
        out_specs=pl.BlockSpec((tm, tn), lambda i, j: (i, j)),
        out_shape=_sds((s, IN_W), BF16),
        scratch_shapes=[pltpu.VMEM((tm, D), BF16)],
        compiler_params=_cparams(2), name=f"fwd_proj{l}")(x, g1, wt_in)


def _scan_fwd(a_ref, u_ref, h_ref, h0, n_rows):
    row = lax.broadcasted_iota(jnp.int32, (8, BW), 0)

    def body(g, hprev):
        r = pl.multiple_of(g * 8, 8)
        a = a_ref[pl.ds(r, 8), :]
        u = u_ref[pl.ds(r, 8), :]
        for sft in (1, 2, 4):
            a_sh = jnp.where(row >= sft, pltpu.roll(a, sft, 0), 1.0)
            u_sh = jnp.where(row >= sft, pltpu.roll(u, sft, 0), 0.0)
            u = u + a * u_sh
            a = a * a_sh
        h = u + a * hprev
        h_ref[pl.ds(r, 8), :] = h
        return h[7:8, :]

    return lax.fori_loop(0, n_rows // 8, body, h0)


def _scan_bwd(b_ref, g_ref, o_ref, c0, n_rows):
    row = lax.broadcasted_iota(jnp.int32, (8, BW), 0)

    def body(k, cnext):
        r = pl.multiple_of((n_rows // 8 - 1 - k) * 8, 8)
        b = b_ref[pl.ds(r, 8), :]
        g = g_ref[pl.ds(r, 8), :]
        for sft in (1, 2, 4):
            b_sh = jnp.where(row < 8 - sft, pltpu.roll(b, 8 - sft, 0), 1.0)
            g_sh = jnp.where(row < 8 - sft, pltpu.roll(g, 8 - sft, 0), 0.0)
            g = g + b * g_sh
            b = b * b_sh
        o = g + b * cnext
        o_ref[pl.ds(r, 8), :] = o
        return o[0:1, :]

    return lax.fori_loop(0, n_rows // 8, body, c0)


def _branch_fwd_math(cur_ref, halo_ref, cw_ref, vec_ref, wx_ref, wa_ref, bufa, bufb, bufd, first, t):
    def halo(c0):
        v = halo_ref[:, c0:c0 + BW].astype(F32)
        return jnp.where(first, 0.0, v)

    def cur(c0):
        return cur_ref[:, c0:c0 + BW].astype(F32)

    out = {}
    bufa[0:HALO, :] = halo(C_AX)
    bufa[HALO:HALO + t, :] = cur(C_AX)
    ca = jnp.zeros((t, BW), F32) + vec_ref[V_CAB:V_CAB + 1, :]
    for k in range(CONV_A):
        ca = ca + cw_ref[CW_A + k:CW_A + k + 1, :] * bufa[pl.ds(HALO - (CONV_A - 1) + k, t), :]
    gi = _sigmoid(_dot(ca, wx_ref[...], _NN) + vec_ref[V_BX:V_BX + 1, :])
    gr = _sigmoid(_dot(ca, wa_ref[...], _NN) + vec_ref[V_BA:V_BA + 1, :])
    sp = _softplus(-vec_ref[V_LAM:V_LAM + 1, :])
    la = -LRU_C * sp * gr
    a = jnp.exp(la)
    mult = jnp.sqrt(_neg_expm1(2.0 * la))
    out.update(ca=ca, gi=gi, gr=gr, sp=sp, a=a, mult=mult)
    bufb[0:HALO, :] = halo(C_BC) * halo(C_BV)
    bufb[HALO:HALO + t, :] = cur(C_BC) * cur(C_BV)
    cb = jnp.zeros((t, BW), F32)
    for k in range(CONV_B):
        cb = cb + cw_ref[CW_B + k:CW_B + k + 1, :] * bufb[pl.ds(HALO - (CONV_B - 1) + k, t), :]
    out.update(cb=cb)
    bufd[0:HALO, :] = halo(C_D1) * _sigmoid(halo(C_D2))
    s2 = _sigmoid(cur(C_D2))
    bufd[HALO:HALO + t, :] = cur(C_D1) * s2
    cd = jnp.zeros((t, BW), F32) + vec_ref[V_CDB:V_CDB + 1, :]
    for k in range(CONV_D):
        cd = cd + cw_ref[CW_D + k:CW_D + k + 1, :] * bufd[pl.ds(HALO - (CONV_D - 1) + k, t), :]
    mu = jnp.mean(cd, axis=-1, keepdims=True)
    xc = cd - mu
    rstd = lax.rsqrt(jnp.mean(xc * xc, axis=-1, keepdims=True) + EPS)
    xh = xc * rstd
    ln = xh * vec_ref[V_LNG:V_LNG + 1, :] + vec_ref[V_LNB:V_LNB + 1, :]
    out.update(s2=s2, xh=xh, rstd=rstd, ln=ln)
    return out


def fwd_branch(proj, convw, vecs, wx_bd, wa_bd, l):
    s = proj.shape[0]
    t = min(256, s)

    def body(cur_ref, halo_ref, cw_ref, vec_ref, wx_ref, wa_ref, pre_ref, h_ref, bufa, bufb, bufd, a_s, u_s, hcar):
        first = pl.program_id(0) == 0

        @pl.when(first)
        def _():
            hcar[...] = jnp.zeros((1, BW), F32)

        v = _branch_fwd_math(cur_ref, halo_ref, cw_ref, vec_ref, wx_ref, wa_ref, bufa, bufb, bufd, first, t)
        a_s[...] = v["a"]
        u_s[...] = v["ca"] * v["gi"] * v["mult"]
        hcar[...] = _scan_fwd(a_s, u_s, h_ref, hcar[...], t)
        gg, _ = _gelu_and_grad(cur_ref[:, C_AG:C_AG + BW].astype(F32))
        pre_ref[:, 0:BW] = (h_ref[...] * gg).astype(BF16)
        pre_ref[:, BW:2 * BW] = (cur_ref[:, C_BB:C_BB + BW].astype(F32) * v["cb"]).astype(BF16)
        ln = v["ln"]
        pre_ref[:, 2 * BW:3 * BW] = (ln * _sigmoid(ln)).astype(BF16)

    hb = t // HALO
    return pl.pallas_call(
        body, grid=(s // t,),
        in_specs=[pl.BlockSpec((t, GL0), lambda i: (i, 0)),
                  pl.BlockSpec((HALO, GL0), lambda i: (jnp.maximum(i * hb - 1, 0), 0)),
                  pl.BlockSpec((None, CW_ROWS, BW), lambda i: (l, 0, 0)),
                  pl.BlockSpec((None, V_ROWS, BW), lambda i: (l, 0, 0)),
                  pl.BlockSpec((None, BW, BW), lambda i: (l, 0, 0)),
                  pl.BlockSpec((None, BW, BW), lambda i: (l, 0, 0))],
        out_specs=[pl.BlockSpec((t, 3 * BW), lambda i: (i, 0)), pl.BlockSpec((t, BW), lambda i: (i, 0))],
        out_shape=[_sds((s, 3 * BW), BF16), _sds((s, BW), F32)],
        scratch_shapes=[pltpu.VMEM((t + HALO, BW), F32)] * 3 + [pltpu.VMEM((t, BW), F32)] * 2 + [pltpu.VMEM((1, BW), F32)],
        compiler_params=_cparams(1), name=f"fwd_branch{l}")(proj, proj, convw, vecs, wx_bd, wa_bd)


def _attn_mask_bias():
    qi = lax.broadcasted_iota(jnp.int32, (ATT_BLK, 2 * ATT_BLK), 0)
    ki = lax.broadcasted_iota(jnp.int32, (ATT_BLK, 2 * ATT_BLK), 1)
    dist = qi + ATT_BLK - ki
    valid = (dist >= 0) & (dist < ATT_BLK)
    return dist.astype(F32), valid, ki


def _attn_probs(q_ref, kvp_ref, kvc_ref, vec_ref, h, distf, valid):
    hk = h // (N_HEADS // N_KV)
    slope = 2.0 ** (-8.0 * (h + 1) / N_HEADS)
    qh = q_ref[:, h * HD:(h + 1) * HD]
    k2 = jnp.concatenate([kvp_ref[:, hk * HD:(hk + 1) * HD], kvc_ref[:, hk * HD:(hk + 1) * HD]], axis=0)
    v2 = jnp.concatenate([kvp_ref[:, (N_KV + hk) * HD:(N_KV + hk + 1) * HD],
                          kvc_ref[:, (N_KV + hk) * HD:(N_KV + hk + 1) * HD]], axis=0)
    sc = _dot(qh, k2, _NT) * (HD ** -0.5) - slope * distf
    sc = jnp.where(valid, sc, NEG_INF)
    sink = vec_ref[V_SINK:V_SINK + 1, h:h + 1]
    m = jnp.maximum(jnp.max(sc, axis=-1, keepdims=True), sink)
    p = jnp.exp(sc - m)
    es = jnp.exp(sink - m)
    inv = 1.0 / (jnp.sum(p, axis=-1, keepdims=True) + es)
    return qh, k2, v2, p * inv, es * inv


def fwd_attn(proj, vecs, l):
    s = proj.shape[0]
    nb = s // ATT_BLK

    def body(q_ref, kvp_ref, kvc_ref, vec_ref, o_ref):
        distf, valid, ki = _attn_mask_bias()
        valid = valid & ((pl.program_id(0) > 0) | (ki >= ATT_BLK))
        for h in range(N_HEADS):
            _, _, v2, p, _ = _attn_probs(q_ref, kvp_ref, kvc_ref, vec_ref, h, distf, valid)
            o_ref[:, h * HD:(h + 1) * HD] = _dot(p, v2, _NN).astype(BF16)

    return pl.pallas_call(
        body, grid=(nb,),
        in_specs=[pl.BlockSpec((ATT_BLK, BW), lambda i: (i, C_Q // BW)),
                  pl.BlockSpec((ATT_BLK, 256), lambda i: (jnp.maximum(i - 1, 0), C_K // 256)),
                  pl.BlockSpec((ATT_BLK, 256), lambda i: (i, C_K // 256)),
                  pl.BlockSpec((None, V_ROWS, BW), lambda i: (l, 0, 0))],
        out_specs=pl.BlockSpec((ATT_BLK, BW), lambda i: (i, 0)),
        out_shape=_sds((s, BW), BF16),
        compiler_params=_cparams(1), name=f"fwd_attn{l}")(proj, proj, proj, vecs)


def fwd_merge(x, proj, pre_abd, pre_c, wt_a, wt_b, wt_c, wt_d, w_o, l):
    s = x.shape[0]
    tm = min(256, s)

    def body(x_ref, gl_ref, pabd_ref, pc_ref, wa_ref, wb_ref, wc_ref, wd_ref, wo_ref, y_ref, mg_ref, h1_ref):
        pres = (pabd_ref[:, 0:BW], pabd_ref[:, BW:2 * BW], pc_ref[...], pabd_ref[:, 2 * BW:3 * BW])
        merged = jnp.zeros((tm, D), F32)
        for k, (pre, w_ref) in enumerate(zip(pres, (wa_ref, wb_ref, wc_ref, wd_ref))):
            yk = _dot(pre, w_ref[...], _NT)
            y_ref[:, k * D:(k + 1) * D] = yk.astype(BF16)
            merged = merged + _sigmoid(gl_ref[:, k * D:(k + 1) * D].astype(F32)) * yk
        mg_ref[...] = merged.astype(BF16)
        h1_ref[...] = x_ref[...] + _dot(merged, wo_ref[...], _NN)

    wspec = pl.BlockSpec((None, D, BW), lambda i: (l, 0, 0))
    return pl.pallas_call(
        body, grid=(s // tm,),
        in_specs=[pl.BlockSpec((tm, D), lambda i: (i, 0)),
                  pl.BlockSpec((E(tm), E(4 * D)), lambda i: (i * tm, GL0)),
                  pl.BlockSpec((tm, 3 * BW), lambda i: (i, 0)),
                  pl.BlockSpec((tm, BW), lambda i: (i, 0)),
                  wspec, wspec, wspec, wspec,
                  pl.BlockSpec((None, D, D), lambda i: (l, 0, 0))],
        out_specs=[pl.BlockSpec((tm, 4 * D), lambda i: (i, 0)), pl.BlockSpec((tm, D), lambda i: (i, 0)),
                   pl.BlockSpec((tm, D), lambda i: (i, 0))],
        out_shape=[_sds((s, 4 * D), BF16), _sds((s, D), BF16), _sds((s, D), F32)],
        compiler_params=_cparams(1), name=f"fwd_merge{l}")(x, proj, pre_abd, pre_c, wt_a, wt_b, wt_c, wt_d, w_o)


def fwd_ffn(h1, g2, wt_gate, wt_up, w_down, l):
    s = h1.shape[0]
    tm = min(512, s)
    fc = FF // 2

    def body(h_ref, g_ref, wg_ref, wu_ref, wd_ref, xo_ref, fg_ref, fu_ref, hn_ref, acc_ref):
        j = pl.program_id(1)

        @pl.when(j == 0)
        def _():
            hv = h_ref[...]
            r = lax.rsqrt(jnp.mean(hv * hv, axis=-1, keepdims=True) + EPS)
            hn_ref[...] = (hv * r * g_ref[l:l + 1, :]).astype(BF16)
            acc_ref[...] = hv

        fg = _dot(hn_ref[...], wg_ref[...], _NT)
        fu = _dot(hn_ref[...], wu_ref[...], _NT)
        fg_ref[...] = fg.astype(BF16)
        fu_ref[...] = fu.astype(BF16)
        acc_ref[...] += _dot(fg * _sigmoid(fg) * fu, wd_ref[...], _NN)

        @pl.when(j == pl.num_programs(1) - 1)
        def _():
            xo_ref[...] = acc_ref[...]

    wspec = pl.BlockSpec((None, fc, D), lambda i, j: (l, j, 0))
    return pl.pallas_call(
        body, grid=(s // tm, FF // fc),
        in_specs=[pl.BlockSpec((tm, D), lambda i, j: (i, 0)), pl.BlockSpec((DEPTH, D), lambda i, j: (0, 0)),
                  wspec, wspec, wspec],
        out_specs=[pl.BlockSpec((tm, D), lambda i, j: (i, 0)), pl.BlockSpec((tm, fc), lambda i, j: (i, j)),
                   pl.BlockSpec((tm, fc), lambda i, j: (i, j))],
        out_shape=[_sds((s, D), F32), _sds((s, FF), BF16), _sds((s, FF), BF16)],
        scratch_shapes=[pltpu.VMEM((tm, D), BF16), pltpu.VMEM((tm, D), F32)],
        compiler_params=_cparams(2), name=f"fwd_ffn{l}")(h1, g2, wt_gate, wt_up, w_down)


def loss_head(x, gf, target):
    s = x.shape[0]
    tm = min(512, s)

    def body(x_ref, g_ref, t_ref, dx_ref, st_ref):
        @pl.when(pl.program_id(0) == 0)
        def _():
            st_ref[...] = jnp.zeros((8, D), F32)

        xv = x_ref[...]
        g = g_ref[...]
        r = lax.rsqrt(jnp.mean(xv * xv, axis=-1, keepdims=True) + EPS)
        n = xv * r
        err = n * g - t_ref[...]
        dy = err * (1.0 / D)
        dn = dy * g
        dx_ref[...] = r * (dn - n * jnp.mean(dn * n, axis=-1, keepdims=True))
        st_ref[0:1, :] += jnp.sum(dy * n, axis=0, keepdims=True)
        lsum = 0.5 * jnp.sum(jnp.mean(err * err, axis=-1, keepdims=True), axis=0, keepdims=True)
        st_ref[1:2, :] += jnp.broadcast_to(lsum, (1, D))

    return pl.pallas_call(
        body, grid=(s // tm,),
        in_specs=[pl.BlockSpec((tm, D), lambda i: (i, 0)), pl.BlockSpec((1, D), lambda i: (0, 0)),
                  pl.BlockSpec((tm, D), lambda i: (i, 0))],
        out_specs=[pl.BlockSpec((tm, D), lambda i: (i, 0)), pl.BlockSpec((8, D), lambda i: (0, 0))],
        out_shape=[_sds((s, D), F32), _sds((8, D), F32)],
        compiler_params=_cparams(1), name="loss_head")(x, gf, target)


def _edge_index(j, i, n_j, n_i):
    return jnp.where((j == 0) | (j == n_j - 1), i, n_i - 1)


def bwd_ffn(dxo, h1, fg, fu, g2, wt_gate, wt_up, w_down, l):
    s = h1.shape[0]
    tm = min(512, s)
    fc = 256
    n_j, n_i = FF // fc, s // tm

    def body(dxo_ref, h_ref, fg_ref, fu_ref, g_ref, wg_ref, wu_ref, wd_ref,
             dh_ref, dwg_ref, dwu_ref, dwd_ref, st_ref, dhn, dxo_b, hn_b, ag, au, ad):
        j, i = pl.program_id(0), pl.program_id(1)
        rows = pl.ds(pl.multiple_of(i * tm, tm), tm)
        g = g_ref[l:l + 1, :]

        @pl.when(j == 0)
        def _():
            hv = h_ref[...]
            r = lax.rsqrt(jnp.mean(hv * hv, axis=-1, keepdims=True) + EPS)
            hn_b[rows, :] = (hv * r * g).astype(BF16)
            dxo_b[rows, :] = dxo_ref[...].astype(BF16)
            dhn[rows, :] = jnp.zeros((tm, D), F32)

        @pl.when((j == 0) & (i == 0))
        def _():
            st_ref[...] = jnp.zeros((8, D), F32)

        @pl.when(i == 0)
        def _():
            ag[...] = jnp.zeros((fc, D), F32)
            au[...] = jnp.zeros((fc, D), F32)
            ad[...] = jnp.zeros((fc, D), F32)

        fgv = fg_ref[...].astype(F32)
        fuv = fu_ref[...].astype(F32)
        sg = _sigmoid(fgv)
        sil = fgv * sg
        dxb = dxo_b[rows, :]
        hnb = hn_b[rows, :]
        d_act = _dot(dxb, wd_ref[...], _NT)
        ad[...] += _dot(sil * fuv, dxb, _TN)
        d_fg = (d_act * fuv * (sg * (1.0 + fgv * (1.0 - sg)))).astype(BF16)
        d_fu = (d_act * sil).astype(BF16)
        ag[...] += _dot(d_fg, hnb, _TN)
        au[...] += _dot(d_fu, hnb, _TN)
        dhn[rows, :] += _dot(d_fg, wg_ref[...], _NN) + _dot(d_fu, wu_ref[...], _NN)

        @pl.when(i == n_i - 1)
        def _():
            dwg_ref[...] = ag[...].astype(BF16)
            dwu_ref[...] = au[...].astype(BF16)
            dwd_ref[...] = ad[...].astype(BF16)

        @pl.when(j == n_j - 1)
        def _():
            hv = h_ref[...]
            r = lax.rsqrt(jnp.mean(hv * hv, axis=-1, keepdims=True) + EPS)
            n = hv * r
            dv = dhn[rows, :]
            dn = dv * g
            dh_ref[...] = dxo_ref[...] + r * (dn - n * jnp.mean(dn * n, axis=-1, keepdims=True))
            st_ref[0:1, :] += jnp.sum(dv * n, axis=0, keepdims=True)

    edge = lambda j, i: (_edge_index(j, i, n_j, n_i), 0)
    wspec = pl.BlockSpec((None, fc, D), lambda j, i: (l, j, 0))
    dwspec = pl.BlockSpec((fc, D), lambda j, i: (j, 0))
    return pl.pallas_call(
        body, grid=(n_j, n_i),
        in_specs=[pl.BlockSpec((tm, D), edge), pl.BlockSpec((tm, D), edge),
                  pl.BlockSpec((tm, fc), lambda j, i: (i, j)), pl.BlockSpec((tm, fc), lambda j, i: (i, j)),
                  pl.BlockSpec((DEPTH, D), lambda j, i: (0, 0)), wspec, wspec, wspec],
        out_specs=[pl.BlockSpec((tm, D), lambda j, i: (jnp.where(j == n_j - 1, i, 0), 0)),
                   dwspec, dwspec, dwspec, pl.BlockSpec((8, D), lambda j, i: (0, 0))],
        out_shape=[_sds((s, D), F32), _sds((FF, D), BF16), _sds((FF, D), BF16), _sds((FF, D), BF16), _sds((8, D), F32)],
        scratch_shapes=[pltpu.VMEM((s, D), F32), pltpu.VMEM((s, D), BF16), pltpu.VMEM((s, D), BF16),
                        pltpu.VMEM((fc, D), F32), pltpu.VMEM((fc, D), F32), pltpu.VMEM((fc, D), F32)],
        compiler_params=_cparams(2), name=f"bwd_ffn{l}")(dxo, h1, fg, fu, g2, wt_gate, wt_up, w_down)


def bwd_merge(dh1, y4, proj, merged, pre_abd, pre_c, wt_a, wt_b, wt_c, wt_d, w_o, l):
    s = dh1.shape[0]
    tm = min(256, s)
    n_i = s // tm

    def body(dh_ref, y_ref, gl_ref, mg_ref, pabd_ref, pc_ref, wa_ref, wb_ref, wc_ref, wd_ref, wo_ref,
             dgl_ref, dpre_ref, dwo_ref, dwa_ref, dwb_ref, dwc_ref, dwd_ref, ao, aa, ab, ac, ad):
        i = pl.program_id(0)
        accs = (aa, ab, ac, ad)

        @pl.when(i == 0)
        def _():
            ao[...] = jnp.zeros((D, D), F32)
            for acc in accs:
                acc[...] = jnp.zeros((D, BW), F32)

        dhb = dh_ref[...].astype(BF16)
        dmg = _dot(dhb, wo_ref[...], _NT)
        ao[...] += _dot(mg_ref[...], dhb, _TN)
        pres = (pabd_ref[:, 0:BW], pabd_ref[:, BW:2 * BW], pc_ref[...], pabd_ref[:, 2 * BW:3 * BW])
        for k, (pre, w_ref, acc) in enumerate(zip(pres, (wa_ref, wb_ref, wc_ref, wd_ref), accs)):
            gk = _sigmoid(gl_ref[:, k * D:(k + 1) * D].astype(F32))
            yk = y_ref[:, k * D:(k + 1) * D].astype(F32)
            dgl_ref[:, k * D:(k + 1) * D] = (dmg * yk * gk * (1.0 - gk)).astype(BF16)
            dyk = (dmg * gk).astype(BF16)
            dpre_ref[:, k * BW:(k + 1) * BW] = _dot(dyk, w_ref[...], _NN).astype(BF16)
            acc[...] += _dot(dyk, pre, _TN)

        @pl.when(i == n_i - 1)
        def _():
            dwo_ref[...] = ao[...].astype(BF16)
            for o_ref, acc in zip((dwa_ref, dwb_ref, dwc_ref, dwd_ref), accs):
                o_ref[...] = acc[...].astype(BF16)

    wspec = pl.BlockSpec((None, D, BW), lambda i: (l, 0, 0))
    dwspec = pl.BlockSpec((D, BW), lambda i: (0, 0))
    return pl.pallas_call(
        body, grid=(n_i,),
        in_specs=[pl.BlockSpec((tm, D), lambda i: (i, 0)),
                  pl.BlockSpec((tm, 4 * D), lambda i: (i, 0)),
                  pl.BlockSpec((E(tm), E(4 * D)), lambda i: (i * tm, GL0)),
                  pl.BlockSpec((tm, D), lambda i: (i, 0)),
                  pl.BlockSpec((tm, 3 * BW), lambda i: (i, 0)),
                  pl.BlockSpec((tm, BW), lambda i: (i, 0)),
                  wspec, wspec, wspec, wspec,
                  pl.BlockSpec((None, D, D), lambda i: (l, 0, 0))],
        out_specs=[pl.BlockSpec((E(tm), E(4 * D)), lambda i: (i * tm, GL0)),
                   pl.BlockSpec((tm, 4 * BW), lambda i: (i, 0)),
                   pl.BlockSpec((D, D), lambda i: (0, 0)), dwspec, dwspec, dwspec, dwspec],
        out_shape=[_sds((s, IN_W), BF16), _sds((s, 4 * BW), BF16), _sds((D, D), BF16)] + [_sds((D, BW), BF16)] * 4,
        scratch_shapes=[pltpu.VMEM((D, D), F32)] + [pltpu.VMEM((D, BW), F32)] * 4,
        compiler_params=_cparams(1), name=f"bwd_merge{l}")(
            dh1, y4, proj, merged, pre_abd, pre_c, wt_a, wt_b, wt_c, wt_d, w_o)


def bwd_attn(proj, dpre, vecs, l):
    s = proj.shape[0]
    nb = s // ATT_BLK
    grp = N_HEADS // N_KV

    def body(q_ref, kvp_ref, kvc_ref, do_ref, vec_ref, dq_ref, dkc_ref, dkp_ref, st_ref):
        @pl.when(pl.program_id(0) == 0)
        def _():
            st_ref[...] = jnp.zeros((8, 128), F32)

        distf, valid, ki = _attn_mask_bias()
        valid = valid & ((pl.program_id(0) > 0) | (ki >= ATT_BLK))
        lane = lax.broadcasted_iota(jnp.int32, (1, 128), 1)
        dsink = jnp.zeros((1, 128), F32)
        for hk in range(N_KV):
            dk2 = jnp.zeros((2 * ATT_BLK, HD), F32)
            dv2 = jnp.zeros((2 * ATT_BLK, HD), F32)
            for gq in range(grp):
                h = hk * grp + gq
                qh, k2, v2, p, ps = _attn_probs(q_ref, kvp_ref, kvc_ref, vec_ref, h, distf, valid)
                doh = do_ref[:, h * HD:(h + 1) * HD]
                dp = _dot(doh, v2, _NT)
                delta = jnp.sum(p * dp, axis=-1, keepdims=True)
                ds = p * (dp - delta) * (HD ** -0.5)
                dq_ref[:, h * HD:(h + 1) * HD] = _dot(ds, k2, _NN).astype(BF16)
                dk2 = dk2 + _dot(ds, qh, _TN)
                dv2 = dv2 + _dot(p, doh, _TN)
                dsink = dsink + jnp.where(lane == h, -jnp.sum(ps * delta, axis=0, keepdims=True), 0.0)
            dkp_ref[:, hk * HD:(hk + 1) * HD] = dk2[0:ATT_BLK].astype(BF16)
            dkc_ref[:, hk * HD:(hk + 1) * HD] = dk2[ATT_BLK:].astype(BF16)
            dkp_ref[:, (N_KV + hk) * HD:(N_KV + hk + 1) * HD] = dv2[0:ATT_BLK].astype(BF16)
            dkc_ref[:, (N_KV + hk) * HD:(N_KV + hk + 1) * HD] = dv2[ATT_BLK:].astype(BF16)
        st_ref[0:1, :] += dsink

    return pl.pallas_call(
        body, grid=(nb,),
        in_specs=[pl.BlockSpec((ATT_BLK, BW), lambda i: (i, C_Q // BW)),
                  pl.BlockSpec((ATT_BLK, 256), lambda i: (jnp.maximum(i - 1, 0), C_K // 256)),
                  pl.BlockSpec((ATT_BLK, 256), lambda i: (i, C_K // 256)),
                  pl.BlockSpec((ATT_BLK, BW), lambda i: (i, 2)),
                  pl.BlockSpec((None, V_ROWS, BW), lambda i: (l, 0, 0))],
        out_specs=[pl.BlockSpec((ATT_BLK, BW), lambda i: (i, 0)), pl.BlockSpec((ATT_BLK, 256), lambda i: (i, 0)),
                   pl.BlockSpec((ATT_BLK, 256), lambda i: (i, 0)), pl.BlockSpec((8, 128), lambda i: (0, 0))],
        out_shape=[_sds((s, BW), BF16), _sds((s, 256), BF16), _sds((s, 256), BF16), _sds((8, 128), F32)],
        compiler_params=_cparams(1), name=f"bwd_attn{l}")(proj, proj, proj, dpre, vecs)


def bwd_branch(proj, dproj, dpre, h, dq, dkc, dkp, convw, vecs, wx_bd, wa_bd, l):
    s = proj.shape[0]
    t = 2 * ATT_BLK
    nt = s // t
    nb = s // ATT_BLK
    hb = t // HALO

    def body(cur_ref, halo_ref, dpre_ref, h_ref, hp_ref, dq_ref, dkc_ref, dkp1_ref, dkp2_ref,
             cw_ref, vec_ref, wx_ref, wa_ref, dproj_in, dp_ref, dcw_ref, dvec_ref, dwx_ref, dwa_ref,
             bufa, bufb, bufd, a_ext, hbuf, b_s, g_s, dh_s, ga, gb, gd, dhcar):
        del dproj_in
        step = pl.program_id(0)
        ti = nt - 1 - step
        first = ti == 0

        @pl.when(step == 0)
        def _():
            dcw_ref[...] = jnp.zeros((CW_ROWS, BW), F32)
            dvec_ref[...] = jnp.zeros((V_ROWS, BW), F32)
            dwx_ref[...] = jnp.zeros((BW, BW), F32)
            dwa_ref[...] = jnp.zeros((BW, BW), F32)
            dhcar[...] = jnp.zeros((1, BW), F32)
            a_ext[t:t + 8, :] = jnp.zeros((8, BW), F32)
            ga[t:t + 8, :] = jnp.zeros((8, BW), F32)
            gb[t:t + 8, :] = jnp.zeros((8, BW), F32)
            gd[t:t + HALO, :] = jnp.zeros((HALO, BW), F32)

        def cur(c0):
            return cur_ref[:, c0:c0 + BW].astype(F32)

        def rsum(v):
            return jnp.sum(v, axis=0, keepdims=True)

        def put(c0, v):
            dp_ref[:, c0:c0 + BW] = v.astype(BF16)

        v = _branch_fwd_math(cur_ref, halo_ref, cw_ref, vec_ref, wx_ref, wa_ref, bufa, bufb, bufd, first, t)
        ca, gi, gr, sp, a, mult = v["ca"], v["gi"], v["gr"], v["sp"], v["a"], v["mult"]
        dpa = dpre_ref[:, 0:BW].astype(F32)
        gg, dgg = _gelu_and_grad(cur(C_AG))
        hv = h_ref[...]
        put(C_AG, dpa * hv * dgg)
        a_ext[0:t, :] = a
        b_s[...] = a_ext[pl.ds(1, t), :]
        g_s[...] = dpa * gg
        dhcar[...] = _scan_bwd(b_s, g_s, dh_s, dhcar[...], t)
        a_ext[t:t + 1, :] = a[0:1, :]
        dh = dh_s[...]
        hbuf[0:8, :] = jnp.where(first, 0.0, hp_ref[...])
        hbuf[8:8 + t, :] = hv
        da = dh * hbuf[pl.ds(7, t), :]
        d_ca = dh * gi * mult
        d_gi = dh * ca * mult
        d_mult = dh * ca * gi
        d_la = da * a - d_mult * (a * a) / mult
        lam = vec_ref[V_LAM:V_LAM + 1, :]
        dvec_ref[V_LAM:V_LAM + 1, :] += rsum(d_la * gr) * (LRU_C * _sigmoid(-lam))
        d_gr = d_la * (-LRU_C * sp)
        d_zr = d_gr * gr * (1.0 - gr)
        d_zi = d_gi * gi * (1.0 - gi)
        dvec_ref[V_BA:V_BA + 1, :] += rsum(d_zr)
        dvec_ref[V_BX:V_BX + 1, :] += rsum(d_zi)
        dwa_ref[...] += _dot(ca, d_zr, _TN)
        dwx_ref[...] += _dot(ca, d_zi, _TN)
        d_ca = d_ca + _dot(d_zi, wx_ref[...], _NT) + _dot(d_zr, wa_ref[...], _NT)
        dvec_ref[V_CAB:V_CAB + 1, :] += rsum(d_ca)
        ga[0:t, :] = d_ca
        d_ax = jnp.zeros((t, BW), F32)
        for k in range(CONV_A):
            d_ax = d_ax + cw_ref[CW_A + k:CW_A + k + 1, :] * ga[pl.ds(CONV_A - 1 - k, t), :]
            dcw_ref[CW_A + k:CW_A + k + 1, :] += rsum(d_ca * bufa[pl.ds(HALO - (CONV_A - 1) + k, t), :])
        ga[t:t + 8, :] = d_ca[0:8, :]
        put(C_AX, d_ax)
        dpb = dpre_ref[:, BW:2 * BW].astype(F32)
        put(C_BB, dpb * v["cb"])
        d_cb = dpb * cur(C_BB)
        gb[0:t, :] = d_cb
        d_cbin = jnp.zeros((t, BW), F32)
        for k in range(CONV_B):
            d_cbin = d_cbin + cw_ref[CW_B + k:CW_B + k + 1, :] * gb[pl.ds(CONV_B - 1 - k, t), :]
            dcw_ref[CW_B + k:CW_B + k + 1, :] += rsum(d_cb * bufb[pl.ds(HALO - (CONV_B - 1) + k, t), :])
        gb[t:t + 8, :] = d_cb[0:8, :]
        put(C_BC, d_cbin * cur(C_BV))
        put(C_BV, d_cbin * cur(C_BC))
        dpd = dpre_ref[:, 3 * BW:4 * BW].astype(F32)
        ln, xh, rstd, s2 = v["ln"], v["xh"], v["rstd"], v["s2"]
        sg = _sigmoid(ln)
        d_ln = dpd * sg * (1.0 + ln * (1.0 - sg))
        dvec_ref[V_LNG:V_LNG + 1, :] += rsum(d_ln * xh)
        dvec_ref[V_LNB:V_LNB + 1, :] += rsum(d_ln)
        d_xh = d_ln * vec_ref[V_LNG:V_LNG + 1, :]
        d_cd = rstd * (d_xh - jnp.mean(d_xh, axis=-1, keepdims=True)
                       - xh * jnp.mean(d_xh * xh, axis=-1, keepdims=True))
        dvec_ref[V_CDB:V_CDB + 1, :] += rsum(d_cd)
        gd[0:t, :] = d_cd
        d_dg = jnp.zeros((t, BW), F32)
        for k in range(CONV_D):
            d_dg = d_dg + cw_ref[CW_D + k:CW_D + k + 1, :] * gd[pl.ds(CONV_D - 1 - k, t), :]
            dcw_ref[CW_D + k:CW_D + k + 1, :] += rsum(d_cd * bufd[pl.ds(HALO - (CONV_D - 1) + k, t), :])
        gd[t:t + HALO, :] = d_cd[0:HALO, :]
        put(C_D1, d_dg * s2)
        put(C_D2, d_dg * cur(C_D1) * s2 * (1.0 - s2))
        dp_ref[:, C_Q:C_Q + BW] = dq_ref[...]
        dkp2 = jnp.where(step == 0, 0.0, dkp2_ref[...].astype(F32))
        dp_ref[0:ATT_BLK, C_K:C_K + 256] = (dkc_ref[0:ATT_BLK, :].astype(F32) + dkp1_ref[...].astype(F32)).astype(BF16)
        dp_ref[ATT_BLK:t, C_K:C_K + 256] = (dkc_ref[ATT_BLK:t, :].astype(F32) + dkp2).astype(BF16)

    rev = lambda i: nt - 1 - i
    full = lambda r, c: pl.BlockSpec((r, c), lambda i: (0, 0))
    return pl.pallas_call(
        body, grid=(nt,),
        in_specs=[pl.BlockSpec((t, GL0), lambda i: (rev(i), 0)),
                  pl.BlockSpec((HALO, GL0), lambda i: (jnp.maximum(rev(i) * hb - 1, 0), 0)),
                  pl.BlockSpec((t, 4 * BW), lambda i: (rev(i), 0)),
                  pl.BlockSpec((t, BW), lambda i: (rev(i), 0)),
                  pl.BlockSpec((8, BW), lambda i: (jnp.maximum(rev(i) * (t // 8) - 1, 0), 0)),
                  pl.BlockSpec((t, BW), lambda i: (rev(i), 0)),
                  pl.BlockSpec((t, 256), lambda i: (rev(i), 0)),
                  pl.BlockSpec((ATT_BLK, 256), lambda i: (2 * rev(i) + 1, 0)),
                  pl.BlockSpec((ATT_BLK, 256), lambda i: (jnp.minimum(2 * rev(i) + 2, nb - 1), 0)),
                  pl.BlockSpec((None, CW_ROWS, BW), lambda i: (l, 0, 0)),
                  pl.BlockSpec((None, V_ROWS, BW), lambda i: (l, 0, 0)),
                  pl.BlockSpec((None, BW, BW), lambda i: (l, 0, 0)),
                  pl.BlockSpec((None, BW, BW), lambda i: (l, 0, 0)),
                  pl.BlockSpec(memory_space=pl.ANY)],
        out_specs=[pl.BlockSpec((t, GL0), lambda i: (rev(i), 0)),
                   full(CW_ROWS, BW), full(V_ROWS, BW), full(BW, BW), full(BW, BW)],
        out_shape=[_sds((s, IN_W), BF16), _sds((CW_ROWS, BW), F32), _sds((V_ROWS, BW), F32),
                   _sds((BW, BW), F32), _sds((BW, BW), F32)],
        scratch_shapes=[pltpu.VMEM((t + HALO, BW), F32)] * 3
        + [pltpu.VMEM((t + 8, BW), F32), pltpu.VMEM((t + 8, BW), F32)]
        + [pltpu.VMEM((t, BW), F32)] * 3
        + [pltpu.VMEM((t + 8, BW), F32), pltpu.VMEM((t + 8, BW), F32), pltpu.VMEM((t + HALO, BW), F32),
           pltpu.VMEM((1, BW), F32)],
        input_output_aliases={13: 0},
        compiler_params=_cparams(1), name=f"bwd_branch{l}")(
            proj, proj, dpre, h, h, dq, dkc, dkp, dkp, convw, vecs, wx_bd, wa_bd, dproj)


def bwd_proj(dproj, x, dh1, g1, wt_in, l):
    s = x.shape[0]
    tm = min(512, s)
    ck = 1408
    n_j, n_i = IN_W // ck, s // tm

    def body(dp_ref, x_ref, dh_ref, g_ref, w_ref, dx_ref, dw_ref, st_ref, dxn, xn_b, acc):
        j, i = pl.program_id(0), pl.program_id(1)
        rows = pl.ds(pl.multiple_of(i * tm, tm), tm)
        g = g_ref[l:l + 1, :]

        @pl.when(j == 0)
        def _():
            xv = x_ref[...]
            r = lax.rsqrt(jnp.mean(xv * xv, axis=-1, keepdims=True) + EPS)
            xn_b[rows, :] = (xv * r * g).astype(BF16)
            dxn[rows, :] = jnp.zeros((tm, D), F32)

        @pl.when((j == 0) & (i == 0))
        def _():
            st_ref[...] = jnp.zeros((8, D), F32)

        @pl.when(i == 0)
        def _():
            acc[...] = jnp.zeros((ck, D), F32)

        dp = dp_ref[...]
        dxn[rows, :] += _dot(dp, w_ref[...], _NN)
        acc[...] += _dot(dp, xn_b[rows, :], _TN)

        @pl.when(i == n_i - 1)
        def _():
            dw_ref[...] = acc[...].astype(BF16)

        @pl.when(j == n_j - 1)
        def _():
            xv = x_ref[...]
            r = lax.rsqrt(jnp.mean(xv * xv, axis=-1, keepdims=True) + EPS)
            n = xv * r
            dv = dxn[rows, :]
            dn = dv * g
            dx_ref[...] = dh_ref[...] + r * (dn - n * jnp.mean(dn * n, axis=-1, keepdims=True))
            st_ref[0:1, :] += jnp.sum(dv * n, axis=0, keepdims=True)

    lastrow = lambda j, i: (jnp.where(j == n_j - 1, i, 0), 0)
    return pl.pallas_call(
        body, grid=(n_j, n_i),
        in_specs=[pl.BlockSpec((tm, ck), lambda j, i: (i, j)),
                  pl.BlockSpec((tm, D), lambda j, i: (_edge_index(j, i, n_j, n_i), 0)),
                  pl.BlockSpec((tm, D), lastrow),
                  pl.BlockSpec((DEPTH, D), lambda j, i: (0, 0)),
                  pl.BlockSpec((None, ck, D), lambda j, i: (l, j, 0))],
        out_specs=[pl.BlockSpec((tm, D), lastrow), pl.BlockSpec((ck, D), lambda j, i: (j, 0)),
                   pl.BlockSpec((8, D), lambda j, i: (0, 0))],
        out_shape=[_sds((s, D), F32), _sds((IN_W, D), BF16), _sds((8, D), F32)],
        scratch_shapes=[pltpu.VMEM((s, D), F32), pltpu.VMEM((s, D), BF16), pltpu.VMEM((ck, D), F32)],
        compiler_params=_cparams(2), name=f"bwd_proj{l}")(dproj, x, dh1, g1, wt_in)


def _block_diag(w):
    nl, nb, bw, _ = w.shape
    eye = jnp.eye(nb, dtype=w.dtype)
    return jnp.einsum("lhij,hk->lhikj", w, eye).reshape(nl, nb * bw, nb * bw).astype(BF16)


def _diag_blocks(m):
    return jnp.stack([m[HD * h:HD * (h + 1), HD * h:HD * (h + 1)] for h in range(BW // HD)])


def local_step(x, target, norm1_g, norm2_g, final_g, convw, vecs, lru_wx, lru_wa, big):
    wx_bd, wa_bd = _block_diag(lru_wx), _block_diag(lru_wa)
    saved = []
    for l in range(DEPTH):
        proj = fwd_proj(x, norm1_g, big["in_t"], l)
        pre_abd, h = fwd_branch(proj, convw, vecs, wx_bd, wa_bd, l)
        pre_c = fwd_attn(proj, vecs, l)
        y4, merged, h1 = fwd_merge(x, proj, pre_abd, pre_c, big["a_t"], big["b_t"], big["c_t"], big["d_t"], big["o"], l)
        x_out, fg, fu = fwd_ffn(h1, norm2_g, big["gate_t"], big["up_t"], big["down"], l)
        saved.append((x, proj, pre_abd, h, pre_c, y4, merged, h1, fg, fu))
        x = x_out
    dx, head_stats = loss_head(x, final_g.reshape(1, D), target)
    grads = [None] * DEPTH
    for l in reversed(range(DEPTH)):
        x_in, proj, pre_abd, h, pre_c, y4, merged, h1, fg, fu = saved[l]
        dh1, d_gate, d_up, d_down, st_ffn = bwd_ffn(dx, h1, fg, fu, norm2_g, big["gate_t"], big["up_t"], big["down"], l)
        dproj, dpre, d_o, d_a, d_b, d_c, d_d = bwd_merge(
            dh1, y4, proj, merged, pre_abd, pre_c, big["a_t"], big["b_t"], big["c_t"], big["d_t"], big["o"], l)
        dq, dkc, dkp, st_attn = bwd_attn(proj, dpre, vecs, l)
        dproj, dcw, dvec, dwx, dwa = bwd_branch(proj, dproj, dpre, h, dq, dkc, dkp, convw, vecs, wx_bd, wa_bd, l)
        dx, d_in, st_proj = bwd_proj(dproj, x_in, dh1, norm1_g, big["in_t"], l)
        grads[l] = dict(in_t=d_in, a_t=d_a, b_t=d_b, c_t=d_c, d_t=d_d, o=d_o, gate_t=d_gate, up_t=d_up, down=d_down,
                        norm1_g=st_proj[0], norm2_g=st_ffn[0], convw=dcw, vecs=dvec, sinks=st_attn[0, :N_HEADS],
                        lru_wx=_diag_blocks(dwx), lru_wa=_diag_blocks(dwa))
    return head_stats, dx, grads


ANY = pl.BlockSpec(memory_space=pl.ANY)
MESH = pl.DeviceIdType.MESH
BIG = dict(in_t=("w_in", "view"), a_t=("w_a_out", "transpose"), b_t=("w_b_out", "transpose"), c_t=("w_c_out", "transpose"),
           d_t=("w_d_out", "transpose"), o=("w_o", "plain"), gate_t=("w_ffn_gate", "view"), up_t=("w_ffn_up", "view"),
           down=("w_ffn_down", "plain"))


def cast_transpose(w, name):
    nl, a, b = w.shape
    ta = min(256, a)

    def body(w_ref, o_ref):
        o_ref[...] = w_ref[...].T.astype(BF16)

    return pl.pallas_call(
        body, grid=(nl, a // ta),
        in_specs=[pl.BlockSpec((None, ta, b), lambda l, i: (l, i, 0))],
        out_specs=pl.BlockSpec((None, b, ta), lambda l, i: (l, 0, i)),
        out_shape=_sds((nl, b, a), BF16), compiler_params=_cparams(2), name=name)(w)


def _mesh_pos():
    return lax.axis_index("x"), lax.axis_index("y"), lax.axis_index("c")


def _dev_index(px, py, pc):
    return 4 * px + 2 * py + pc


def all_gather(shards, name):
    n = len(shards)

    def body(*refs):
        ins, outs = refs[:n], refs[n:2 * n]
        send_sems, recv_sems, local_sems = refs[2 * n:]
        x, y, c = _mesh_pos()
        me = (x, y, c)
        sibling = (x, y, 1 - c)
        chips = [(1 - x, y), (x, 1 - y), (1 - x, 1 - y)]

        def slot(k, dev):
            rows = ins[k].shape[1]
            return outs[k].at[:, pl.ds(_dev_index(*dev) * rows, rows), :]

        def copy(g, k, dev, to, src=None):
            return pltpu.make_async_remote_copy(
                src_ref=slot(k, dev) if src is None else src, dst_ref=slot(k, dev),
                send_sem=send_sems.at[g, k], recv_sem=recv_sems.at[g, k], device_id=to, device_id_type=MESH)

        local = [pltpu.make_async_copy(ins[k], slot(k, me), local_sems.at[k]) for k in range(n)]
        for cp in local:
            cp.start()
        first = [copy(0, k, me, sibling, src=ins[k]) for k in range(n)]
        for j, chip in enumerate(chips):
            first += [copy(1 + j, k, me, (*chip, c), src=ins[k]) for k in range(n)]
        for cp in first:
            cp.start()
        passed = []
        for j, chip in enumerate(chips):
            for k in range(n):
                copy(1 + j, k, (*chip, c), me).wait_recv()
            forward = [copy(4 + j, k, (*chip, c), sibling) for k in range(n)]
            for cp in forward:
                cp.start()
            passed += forward
        for k in range(n):
            copy(0, k, sibling, me).wait_recv()
        for j, chip in enumerate(chips):
            for k in range(n):
                copy(4 + j, k, (*chip, 1 - c), me).wait_recv()
        for cp in first + passed:
            cp.wait_send()
        for cp in local:
            cp.wait()

    return pl.pallas_call(
        body, in_specs=[ANY] * n, out_specs=[ANY] * n,
        out_shape=[_sds((a.shape[0], NDEV * a.shape[1], a.shape[2]), a.dtype) for a in shards],
        scratch_shapes=[pltpu.SemaphoreType.DMA((7, n)), pltpu.SemaphoreType.DMA((7, n)), pltpu.SemaphoreType.DMA((n,))],
        name=name)(*shards)


def scatter_to_sibling(grads, name):
    n = len(grads)

    def body(*refs):
        ins, outs = refs[:n], refs[n:2 * n]
        send_sems, recv_sems = refs[2 * n:]
        x, y, c = _mesh_pos()
        copies = [pltpu.make_async_remote_copy(
            src_ref=ins[q].at[:, 1 - c], dst_ref=outs[q], send_sem=send_sems.at[q], recv_sem=recv_sems.at[q],
            device_id=(x, y, 1 - c), device_id_type=MESH) for q in range(n)]
        for cp in copies:
            cp.start()
        for cp in copies:
            cp.wait_recv()
        for cp in copies:
            cp.wait_send()

    return pl.pallas_call(
        body, in_specs=[ANY] * n, out_specs=[ANY] * n,
        out_shape=[_sds((4,) + g.shape[2:], g.dtype) for g in grads],
        scratch_shapes=[pltpu.SemaphoreType.DMA((n,)), pltpu.SemaphoreType.DMA((n,))],
        name=name)(*grads)


def add_partial(mine, recv, core, name):
    _, _, rows, cols = mine.shape

    def body(core_ref, a_ref, b_ref, o_ref):
        del core_ref
        o_ref[...] = (a_ref[...].astype(F32) + b_ref[...].astype(F32)).astype(BF16)

    return pl.pallas_call(
        body,
        grid_spec=pltpu.PrefetchScalarGridSpec(
            num_scalar_prefetch=1, grid=(4,),
            in_specs=[pl.BlockSpec((None, None, rows, cols), lambda i, cr: (i, cr[0], 0, 0)),
                      pl.BlockSpec((None, rows, cols), lambda i, cr: (i, 0, 0))],
            out_specs=pl.BlockSpec((None, rows, cols), lambda i, cr: (i, 0, 0))),
        out_shape=_sds((4, rows, cols), BF16), compiler_params=_cparams(1), name=name)(core, mine, recv)


def scatter_to_chips(partials, name):
    kinds = len(partials)
    flat = [p for per_layer in partials for p in per_layer]
    n = len(flat)

    def body(*refs):
        ins, outs = refs[:n], refs[n:n + kinds]
        send_sems, recv_sems, local_sems = refs[n + kinds:]
        x, y, c = _mesh_pos()
        mine = 2 * x + y
        local = [pltpu.make_async_copy(ins[q].at[mine], outs[q // DEPTH].at[mine, q % DEPTH], local_sems.at[q])
                 for q in range(n)]
        for cp in local:
            cp.start()
        sends, recvs = [], []
        for j, (cx, cy) in enumerate([(1 - x, y), (x, 1 - y), (1 - x, 1 - y)]):
            theirs = 2 * cx + cy
            for q in range(n):
                def copy(slot_there, j=j, q=q, cx=cx, cy=cy, theirs=theirs):
                    return pltpu.make_async_remote_copy(
                        src_ref=ins[q].at[theirs], dst_ref=outs[q // DEPTH].at[slot_there, q % DEPTH],
                        send_sem=send_sems.at[j, q], recv_sem=recv_sems.at[j, q], device_id=(cx, cy, c), device_id_type=MESH)
                sends.append(copy(mine))
                recvs.append(copy(theirs))
        for cp in sends:
            cp.start()
        for cp in recvs:
            cp.wait_recv()
        for cp in sends:
            cp.wait_send()
        for cp in local:
            cp.wait()

    return pl.pallas_call(
        body, in_specs=[ANY] * n, out_specs=[ANY] * kinds,
        out_shape=[_sds((4, DEPTH) + p[0].shape[1:], p[0].dtype) for p in partials],
        scratch_shapes=[pltpu.SemaphoreType.DMA((3, n)), pltpu.SemaphoreType.DMA((3, n)), pltpu.SemaphoreType.DMA((n,))],
        name=name)(*flat)


def _adamw(w, g, m, v):
    m = ADAM_B1 * m + (1.0 - ADAM_B1) * g
    v = ADAM_B2 * v + (1.0 - ADAM_B2) * (g * g)
    m_hat = m / (1.0 - ADAM_B1 ** ADAM_STEP)
    v_hat = v / (1.0 - ADAM_B2 ** ADAM_STEP)
    delta = -ADAM_LR * (m_hat / (jnp.sqrt(v_hat) + ADAM_EPS) + ADAM_WD * w)
    return delta, m, v


def adamw_big(contrib, w, m, v, transposed, name):
    nsrc, nl, rows, cols = contrib.shape
    ct = 256

    def body(c_ref, w_ref, m_ref, v_ref, g_out, d_out, m_out, v_out):
        g = c_ref[0].astype(F32)
        for src in range(1, nsrc):
            g = g + c_ref[src].astype(F32)
        if transposed:
            g = g.T
        delta, mn, vn = _adamw(w_ref[...], g, m_ref[...], v_ref[...])
        g_out[...] = g
        d_out[...] = delta
        m_out[...] = mn
        v_out[...] = vn

    if transposed:
        wspec = pl.BlockSpec((None, ct, rows), lambda l, j: (l, j, 0))
    else:
        wspec = pl.BlockSpec((None, rows, ct), lambda l, j: (l, 0, j))
    return pl.pallas_call(
        body, grid=(nl, cols // ct),
        in_specs=[pl.BlockSpec((nsrc, None, rows, ct), lambda l, j: (0, l, 0, j)), wspec, wspec, wspec],
        out_specs=[wspec] * 4, out_shape=[_sds(w.shape, F32)] * 4,
        compiler_params=_cparams(2), name=name)(contrib, w, m, v)


def adamw_small(gathered, w, m, v):
    r = w.shape[0]

    def body(c_ref, w_ref, m_ref, v_ref, g_out, d_out, m_out, v_out):
        g = c_ref[0]
        for dev in range(1, NDEV):
            g = g + c_ref[dev]
        delta, mn, vn = _adamw(w_ref[...], g, m_ref[...], v_ref[...])
        g_out[...] = g
        d_out[...] = delta
        m_out[...] = mn
        v_out[...] = vn

    return pl.pallas_call(body, out_shape=[_sds((r, D), F32)] * 4, name="adamw_small",
                          compiler_params=pltpu.CompilerParams(vmem_limit_bytes=VMEM_LIMIT))(gathered, w, m, v)


VEC_NAMES = ("conv_a_b", "lru_bx", "lru_ba", "lru_lambda", "conv_d_b", "ln_d_g", "ln_d_b")
P_N1, P_N2, P_VEC, P_CONV, P_WX, P_WA, P_LAYER = 0, 1, 2, 6, 26, 58, 90
P_FINAL, P_LOSS, P_ROWS = DEPTH * P_LAYER, DEPTH * P_LAYER + 1, 184


def _stack_vecs(p):
    rows = [p[n] for n in VEC_NAMES] + [jnp.pad(p["sinks"], ((0, 0), (0, BW - N_HEADS)))]
    return jnp.stack(rows, axis=1)


def _stack_convs(p):
    nl, _, ch = p["conv_a_w"].shape
    z = jnp.zeros((nl, 1, ch), F32)
    return jnp.concatenate([p["conv_a_w"], p["conv_b_w"], z, p["conv_d_w"], z], axis=1)


def _pack_small(p, convw, extra_rows):
    vec = _stack_vecs(p)
    rows = []
    for l in range(DEPTH):
        rows += [p["norm1_g"][l].reshape(1, D), p["norm2_g"][l].reshape(1, D), vec[l].reshape(-1, D),
                 convw[l].reshape(-1, D), p["lru_wx"][l].reshape(-1, D), p["lru_wa"][l].reshape(-1, D)]
    rows += [p["final_g"].reshape(1, D)] + extra_rows
    pack = jnp.concatenate(rows, axis=0)
    return jnp.pad(pack, ((0, P_ROWS - pack.shape[0]), (0, 0)))


def _unpack_small(pack, ch0):
    per_layer = [pack[l * P_LAYER:(l + 1) * P_LAYER] for l in range(DEPTH)]
    out = dict(norm1_g=jnp.stack([q[P_N1] for q in per_layer]), norm2_g=jnp.stack([q[P_N2] for q in per_layer]),
               final_g=pack[P_FINAL])
    vec = jnp.stack([q[P_VEC:P_CONV].reshape(V_ROWS, BW) for q in per_layer])
    for i, n in enumerate(VEC_NAMES):
        out[n] = vec[:, i]
    out["sinks"] = vec[:, V_SINK, :N_HEADS]
    convw = jnp.stack([q[P_CONV:P_WX].reshape(CW_ROWS, BW) for q in per_layer])
    mine = lax.dynamic_slice_in_dim(convw, ch0, BW // NDEV, axis=2)
    out.update(conv_a_w=mine[:, CW_A:CW_A + CONV_A], conv_b_w=mine[:, CW_B:CW_B + CONV_B],
               conv_d_w=mine[:, CW_D:CW_D + CONV_D])
    out["lru_wx"] = jnp.stack([q[P_WX:P_WA].reshape(BW // HD, HD, HD) for q in per_layer])
    out["lru_wa"] = jnp.stack([q[P_WA:P_LAYER].reshape(BW // HD, HD, HD) for q in per_layer])
    return out


SMALL = ("norm1_g", "conv_a_w", "conv_a_b", "lru_wx", "lru_bx", "lru_wa", "lru_ba", "lru_lambda", "conv_b_w", "sinks",
         "conv_d_w", "conv_d_b", "ln_d_g", "ln_d_b", "norm2_g", "final_g")
WEIGHTS = ("norm1_g", "w_in", "conv_a_w", "conv_a_b", "lru_wx", "lru_bx", "lru_wa", "lru_ba", "lru_lambda", "w_a_out",
           "conv_b_w", "w_b_out", "sinks", "w_c_out", "conv_d_w", "conv_d_b", "ln_d_g", "ln_d_b", "w_d_out", "w_o",
           "norm2_g", "w_ffn_gate", "w_ffn_up", "w_ffn_down", "final_g")


def kernel(x, norm1_g, w_in, conv_a_w, conv_a_b, lru_wx, lru_bx, lru_wa, lru_ba, lru_lambda, w_a_out, conv_b_w, w_b_out, sinks, w_c_out, conv_d_w, conv_d_b, ln_d_g, ln_d_b, w_d_out, w_o, norm2_g, w_ffn_gate, w_ffn_up, w_ffn_down, final_g, loss_target, m_norm1_g, m_w_in, m_conv_a_w, m_conv_a_b, m_lru_wx, m_lru_bx, m_lru_wa, m_lru_ba, m_lru_lambda, m_w_a_out, m_conv_b_w, m_w_b_out, m_sinks, m_w_c_out, m_conv_d_w, m_conv_d_b, m_ln_d_g, m_ln_d_b, m_w_d_out, m_w_o, m_norm2_g, m_w_ffn_gate, m_w_ffn_up, m_w_ffn_down, m_final_g, v_norm1_g, v_w_in, v_conv_a_w, v_conv_a_b, v_lru_wx, v_lru_bx, v_lru_wa, v_lru_ba, v_lru_lambda, v_w_a_out, v_conv_b_w, v_w_b_out, v_sinks, v_w_c_out, v_conv_d_w, v_conv_d_b, v_ln_d_g, v_ln_d_b, v_w_d_out, v_w_o, v_norm2_g, v_w_ffn_gate, v_w_ffn_up, v_w_ffn_down, v_final_g):
    args = dict(locals())
    w = {n: args[n] for n in WEIGHTS}
    m = {n: args["m_" + n] for n in WEIGHTS}
    v = {n: args["v_" + n] for n in WEIGHTS}
    me = _dev_index(*_mesh_pos())
    ch0 = me * (BW // NDEV)

    def rows_major(a, how):
        return jnp.swapaxes(a, 1, 2) if how == "view" else a

    shards = {k: cast_transpose(w[n], "prep_" + k) if how == "transpose" else rows_major(w[n], how).astype(BF16)
              for k, (n, how) in BIG.items()}
    gathered = all_gather(list(shards.values()) + [_stack_convs(w)], "gather_weights")
    big = dict(zip(BIG, gathered[:-1]))
    convw = gathered[-1].reshape(DEPTH, NDEV, CW_ROWS, BW // NDEV).transpose(0, 2, 1, 3).reshape(DEPTH, CW_ROWS, BW)

    vecs = _stack_vecs(w)
    head_stats, grad_x, grads = local_step(x[0], loss_target[0], norm1_g, norm2_g, final_g, convw, vecs, lru_wx, lru_wa, big)

    views = [grads[l][k].reshape(4, 2, -1, grads[l][k].shape[1]) for k in BIG for l in range(DEPTH)]
    from_sibling = scatter_to_sibling(views, "scatter_sibling")
    core = lax.axis_index("c").astype(jnp.int32).reshape(1)
    partial = [add_partial(a, b, core, f"add_partial{q}") for q, (a, b) in enumerate(zip(views, from_sibling))]
    contrib = scatter_to_chips([partial[i * DEPTH:(i + 1) * DEPTH] for i in range(len(BIG))], "scatter_chips")
    out = {}
    for (k, (n, how)), cb in zip(BIG.items(), contrib):
        res = adamw_big(cb, rows_major(w[n], how), rows_major(m[n], how), rows_major(v[n], how), how == "transpose", "adamw_" + k)
        out[n] = [rows_major(r, how) for r in res]

    gsmall = {n: jnp.stack([grads[l][n] for l in range(DEPTH)]) for n in ("norm1_g", "norm2_g", "lru_wx", "lru_wa", "sinks")}
    gvec = jnp.stack([grads[l]["vecs"] for l in range(DEPTH)])
    gsmall.update({n: gvec[:, i] for i, n in enumerate(VEC_NAMES)})
    gsmall["final_g"] = head_stats[0]
    gpack = _pack_small(gsmall, jnp.stack([grads[l]["convw"] for l in range(DEPTH)]), [head_stats[1:2]])
    gall = all_gather([gpack[None]], "gather_small")[0].reshape(NDEV, P_ROWS, D)

    def padded_convs(p):
        return lax.dynamic_update_slice_in_dim(jnp.zeros((DEPTH, CW_ROWS, BW), F32), _stack_convs(p), ch0, axis=2)

    zero_row = [jnp.zeros((1, D), F32)]
    packs = adamw_small(gall, *[_pack_small(p, padded_convs(p), zero_row) for p in (w, m, v)])
    small = [_unpack_small(p, ch0) for p in packs]
    for n in SMALL:
        out[n] = [s[n] for s in small]
    loss = packs[0][P_LOSS, 0]
    return (loss, grad_x[None], *[out[n][0] for n in WEIGHTS], *[out[n][1] for n in WEIGHTS],
            *[out[n][2] for n in WEIGHTS], *[out[n][3] for n in WEIGHTS])
```

```python
import functools

import jax
import jax.numpy as jnp
from jax import lax
from jax.experimental import pallas as pl
from jax.experimental.pallas import tpu as pltpu

F32 = jnp.float32
BF16 = jnp.bfloat16
E = pl.Element

D = 1024
BW = 512
IN_W = 8448
GL0 = 4352
FF = 2816
N_HEADS = 8
N_KV = 2
HD = 64
ATT_BLK = 128
EPS = 1e-6
LRU_C = 8.0
NEG_INF = -1e30
DEPTH = 2
NDEV = 8
CONV_A, CONV_B, CONV_D = 4, 3, 31
C_AX, C_AG, C_BV, C_BC, C_BB, C_Q, C_K, C_V, C_D1, C_D2 = 0, 512, 1024, 1536, 2048, 2560, 3072, 3200, 3328, 3840
CW_A, CW_B, CW_D, CW_ROWS = 0, 4, 8, 40
V_CAB, V_BX, V_BA, V_LAM, V_CDB, V_LNG, V_LNB, V_SINK, V_ROWS = 0, 1, 2, 3, 4, 5, 6, 7, 8
HALO = 32

ADAM_LR, ADAM_B1, ADAM_B2, ADAM_EPS, ADAM_WD, ADAM_STEP = 0.001, 0.9, 0.999, 1e-08, 0.01, 10

VMEM_LIMIT = 56 * 1024 * 1024

_NN = (((1,), (0,)), ((), ()))
_NT = (((1,), (1,)), ((), ()))
_TN = (((0,), (0,)), ((), ()))


def _dot(a, b, dims):
    return lax.dot_general(a.astype(BF16), b.astype(BF16), dims, preferred_element_type=F32)


def _cparams(n_axes):
    return pltpu.CompilerParams(dimension_semantics=("arbitrary",) * n_axes, vmem_limit_bytes=VMEM_LIMIT)


def _sds(shape, dtype):
    return jax.ShapeDtypeStruct(tuple(shape), dtype)


def _sigmoid(x):
    return jax.nn.sigmoid(x)


def _neg_expm1(x):
    p = x * (1.0 + x * (0.5 + x * (1.0 / 6.0 + x * (1.0 / 24.0 + x * (1.0 / 120.0)))))
    return jnp.where(x > -0.1, -p, 1.0 - jnp.exp(x))


def _softplus(z):
    return jnp.maximum(z, 0.0) + jnp.log1p(jnp.exp(-jnp.abs(z)))


def _gelu_and_grad(x):
    c = 0.7978845608028654
    inner = c * (x + 0.044715 * x * x * x)
    t = jnp.tanh(inner)
    g = 0.5 * x * (1.0 + t)
    dg = 0.5 * (1.0 + t) + 0.5 * x * (1.0 - t * t) * c * (1.0 + 3.0 * 0.044715 * x * x)
    return g, dg


ANY = pl.BlockSpec(memory_space=pl.ANY)
MESH = pl.DeviceIdType.MESH


def _mesh_pos():
    return lax.axis_index("x"), lax.axis_index("y"), lax.axis_index("c")


def _dev_index(px, py, pc):
    return 4 * px + 2 * py + pc


class CommJob:
    def __init__(self, inputs, aliases, out_shapes, sem_shapes, start, finish):
        self.inputs, self.aliases, self.out_shapes, self.sem_shapes = list(inputs), dict(aliases), list(out_shapes), list(sem_shapes)
        self.start, self.finish = start, finish


def _call(body, comm, args, *, grid, in_specs, out_specs, out_shape, scratch_shapes=(), name, aliases=None):
    single = not isinstance(out_shape, (list, tuple))
    out_specs = [out_specs] if single else list(out_specs)
    out_shape = [out_shape] if single else list(out_shape)
    scratch_shapes = list(scratch_shapes)
    n_in, n_out, n_scr, n_axes = len(in_specs), len(out_shape), len(scratch_shapes), len(grid)
    params = pltpu.CompilerParams(dimension_semantics=("arbitrary",) * n_axes, vmem_limit_bytes=VMEM_LIMIT)
    io_aliases = dict(aliases or {})
    if comm is None:
        outs = pl.pallas_call(body, grid=grid, in_specs=in_specs, out_specs=out_specs, out_shape=out_shape,
                              scratch_shapes=scratch_shapes, input_output_aliases=io_aliases, compiler_params=params,
                              name=name)(*args)
        return (outs[0] if single else outs), []
    c_in, c_out = len(comm.inputs), len(comm.out_shapes)
    io_aliases.update({n_in + i: n_out + o for i, o in comm.aliases.items()})

    def wrapped(*refs):
        ins, cins = refs[:n_in], refs[n_in:n_in + c_in]
        outs = refs[n_in + c_in:n_in + c_in + n_out]
        couts = refs[n_in + c_in + n_out:n_in + c_in + n_out + c_out]
        rest = refs[n_in + c_in + n_out + c_out:]
        scr, sems = rest[:n_scr], rest[n_scr:]
        first = functools.reduce(lambda a, b: a & b, [pl.program_id(a) == 0 for a in range(n_axes)])
        last = functools.reduce(lambda a, b: a & b, [pl.program_id(a) == pl.num_programs(a) - 1 for a in range(n_axes)])

        @pl.when(first)
        def _():
            comm.start(cins, couts, sems)

        body(*ins, *outs, *scr)

        @pl.when(last)
        def _():
            comm.finish(cins, couts, sems)

    outs = pl.pallas_call(
        wrapped, grid=grid, in_specs=list(in_specs) + [ANY] * c_in, out_specs=out_specs + [ANY] * c_out,
        out_shape=out_shape + comm.out_shapes, scratch_shapes=scratch_shapes + comm.sem_shapes,
        input_output_aliases=io_aliases, compiler_params=params, name=name)(*args, *comm.inputs)
    res, cres = outs[:n_out], outs[n_out:]
    return (res[0] if single else res), cres


def _comm_only(comm, name):
    c_in, c_out = len(comm.inputs), len(comm.out_shapes)

    def body(*refs):
        cins, couts, sems = refs[:c_in], refs[c_in:c_in + c_out], refs[c_in + c_out:]
        comm.start(cins, couts, sems)
        comm.finish(cins, couts, sems)

    return pl.pallas_call(body, in_specs=[ANY] * c_in, out_specs=[ANY] * c_out, out_shape=comm.out_shapes,
                          scratch_shapes=comm.sem_shapes, input_output_aliases=comm.aliases, name=name)(*comm.inputs)


def gather_job(pieces):
    inputs, aliases, out_shapes, plan, where = [], {}, [], [], {}
    for key, shard, gathered, row0, nrows in pieces:
        if key not in where:
            where[key] = (len(inputs), len(out_shapes))
            inputs.append(shard)
            if gathered is not None:
                aliases[len(inputs)] = len(out_shapes)
                inputs.append(gathered)
            out_shapes.append(_sds((NDEV * shard.shape[0], shard.shape[1]), shard.dtype))
        plan.append((*where[key], shard.shape[0], row0, nrows))
    n = len(plan)

    def copies(cins, couts, sems):
        send_sems, recv_sems, local_sems = sems
        x, y, c = _mesh_pos()
        me, sibling = (x, y, c), (x, y, 1 - c)
        chips = [(1 - x, y), (x, 1 - y), (1 - x, 1 - y)]
        local, first, relay, recv_ici, recv_d2d = [], [], [], [], []
        for p, (i_shard, i_out, rows, row0, nrows) in enumerate(plan):
            src = cins[i_shard].at[pl.ds(row0, nrows), :]

            def slot(dev, i_out=i_out, rows=rows, row0=row0, nrows=nrows):
                return couts[i_out].at[pl.ds(_dev_index(*dev) * rows + row0, nrows), :]

            def copy(g, dev, to, src=None, p=p, slot=slot):
                return pltpu.make_async_remote_copy(
                    src_ref=slot(dev) if src is None else src, dst_ref=slot(dev),
                    send_sem=send_sems.at[g, p], recv_sem=recv_sems.at[g, p], device_id=to, device_id_type=MESH)

            local.append(pltpu.make_async_copy(src, slot(me), local_sems.at[p]))
            first.append(copy(0, me, sibling, src=src))
            recv_d2d.append(copy(0, sibling, me))
            for j, chip in enumerate(chips):
                first.append(copy(1 + j, me, (*chip, c), src=src))
                recv_ici.append(copy(1 + j, (*chip, c), me))
                relay.append(copy(4 + j, (*chip, c), sibling))
                recv_d2d.append(copy(4 + j, (*chip, 1 - c), me))
        return local, first, relay, recv_ici, recv_d2d

    def start(cins, couts, sems):
        local, first, _, _, _ = copies(cins, couts, sems)
        for cp in local + first:
            cp.start()

    def finish(cins, couts, sems):
        local, first, relay, recv_ici, recv_d2d = copies(cins, couts, sems)
        for cp in recv_ici:
            cp.wait_recv()
        for cp in relay:
            cp.start()
        for cp in recv_d2d:
            cp.wait_recv()
        for cp in first + relay:
            cp.wait_send()
        for cp in local:
            cp.wait()

    sem_shapes = [pltpu.SemaphoreType.DMA((7, n)), pltpu.SemaphoreType.DMA((7, n)), pltpu.SemaphoreType.DMA((n,))]
    return CommJob(inputs, aliases, out_shapes, sem_shapes, start, finish)


def sibling_exchange_job(grads):
    n = len(grads)

    def copies(cins, couts, sems):
        send_sems, recv_sems = sems
        x, y, c = _mesh_pos()
        return [pltpu.make_async_remote_copy(
            src_ref=cins[q].at[:, 1 - c], dst_ref=couts[q], send_sem=send_sems.at[q], recv_sem=recv_sems.at[q],
            device_id=(x, y, 1 - c), device_id_type=MESH) for q in range(n)]

    def start(cins, couts, sems):
        for cp in copies(cins, couts, sems):
            cp.start()

    def finish(cins, couts, sems):
        cps = copies(cins, couts, sems)
        for cp in cps:
            cp.wait_recv()
        for cp in cps:
            cp.wait_send()

    return CommJob(grads, {}, [_sds((4,) + g.shape[2:], g.dtype) for g in grads],
                   [pltpu.SemaphoreType.DMA((n,)), pltpu.SemaphoreType.DMA((n,))], start, finish)


def chip_exchange_job(pieces):
    inputs, aliases, out_shapes, plan, where = [], {}, [], [], {}
    for partial, contrib, key, layer, row0, nrows in pieces:
        if key not in where:
            where[key] = len(out_shapes)
            out_shapes.append(_sds((4, DEPTH) + partial.shape[1:], partial.dtype))
            if contrib is not None:
                aliases[len(inputs)] = where[key]
                inputs.append(contrib)
        plan.append((len(inputs), where[key], layer, row0, nrows))
        inputs.append(partial)
    n = len(plan)

    def copies(cins, couts, sems):
        send_sems, recv_sems, local_sems = sems
        x, y, c = _mesh_pos()
        mine = 2 * x + y
        local, sends, recvs = [], [], []
        for p, (i_in, i_out, layer, row0, nrows) in enumerate(plan):
            rows = pl.ds(row0, nrows)
            local.append(pltpu.make_async_copy(cins[i_in].at[mine, rows, :], couts[i_out].at[mine, layer, rows, :],
                                               local_sems.at[p]))
            for j, (cx, cy) in enumerate([(1 - x, y), (x, 1 - y), (1 - x, 1 - y)]):
                theirs = 2 * cx + cy

                def copy(slot_there, j=j, p=p, cx=cx, cy=cy, theirs=theirs, i_in=i_in, i_out=i_out, layer=layer, rows=rows):
                    return pltpu.make_async_remote_copy(
                        src_ref=cins[i_in].at[theirs, rows, :], dst_ref=couts[i_out].at[slot_there, layer, rows, :],
                        send_sem=send_sems.at[j, p], recv_sem=recv_sems.at[j, p], device_id=(cx, cy, c), device_id_type=MESH)
                sends.append(copy(mine))
                recvs.append(copy(theirs))
        return local, sends, recvs

    def start(cins, couts, sems):
        local, sends, _ = copies(cins, couts, sems)
        for cp in local + sends:
            cp.start()

    def finish(cins, couts, sems):
        local, sends, recvs = copies(cins, couts, sems)
        for cp in recvs:
            cp.wait_recv()
        for cp in sends:
            cp.wait_send()
        for cp in local:
            cp.wait()

    sem_shapes = [pltpu.SemaphoreType.DMA((3, n)), pltpu.SemaphoreType.DMA((3, n)), pltpu.SemaphoreType.DMA((n,))]
    return CommJob(inputs, aliases, out_shapes, sem_shapes, start, finish)


def fwd_proj(x, g1, wt_in, l, comm=None):
    s = x.shape[0]
    tm = min(512, s)
    tn = 1408

    def body(x_ref, g_ref, w_ref, o_ref, xn_ref):
        @pl.when(pl.program_id(1) == 0)
        def _():
            xv = x_ref[...]
            r = lax.rsqrt(jnp.mean(xv * xv, axis=-1, keepdims=True) + EPS)
            xn_ref[...] = (xv * r * g_ref[l:l + 1, :]).astype(BF16)

        o_ref[...] = _dot(xn_ref[...], w_ref[...], _NT).astype(BF16)

    return _call(
        body, comm, (x, g1, wt_in), grid=(s // tm, IN_W // tn),
        in_specs=[pl.BlockSpec((tm, D), lambda i, j: (i, 0)),
                  pl.BlockSpec((DEPTH, D), lambda i, j: (0, 0)),
                  pl.BlockSpec((tn, D), lambda i, j: (j, 0))],
        out_specs=pl.BlockSpec((tm, tn), lambda i, j: (i, j)),
        out_shape=_sds((s, IN_W), BF16),
        scratch_shapes=[pltpu.VMEM((tm, D), BF16)], name=f"fwd_proj{l}")


def _scan_fwd(a_ref, u_ref, h_ref, h0, n_rows):
    row = lax.broadcasted_iota(jnp.int32, (8, BW), 0)

    def body(g, hprev):
        r = pl.multiple_of(g * 8, 8)
        a = a_ref[pl.ds(r, 8), :]
        u = u_ref[pl.ds(r, 8), :]
        for sft in (1, 2, 4):
            a_sh = jnp.where(row >= sft, pltpu.roll(a, sft, 0), 1.0)
            u_sh = jnp.where(row >= sft, pltpu.roll(u, sft, 0), 0.0)
            u = u + a * u_sh
            a = a * a_sh
        h = u + a * hprev
        h_ref[pl.ds(r, 8), :] = h
        return h[7:8, :]

    return lax.fori_loop(0, n_rows // 8, body, h0)


def _scan_bwd(b_ref, g_ref, o_ref, c0, n_rows):
    row = lax.broadcasted_iota(jnp.int32, (8, BW), 0)

    def body(k, cnext):
        r = pl.multiple_of((n_rows // 8 - 1 - k) * 8, 8)
        b = b_ref[pl.ds(r, 8), :]
        g = g_ref[pl.ds(r, 8), :]
        for sft in (1, 2, 4):
            b_sh = jnp.where(row < 8 - sft, pltpu.roll(b, 8 - sft, 0), 1.0)
            g_sh = jnp.where(row < 8 - sft, pltpu.roll(g, 8 - sft, 0), 0.0)
            g = g + b * g_sh
            b = b * b_sh
        o = g + b * cnext
        o_ref[pl.ds(r, 8), :] = o
        return o[0:1, :]

    return lax.fori_loop(0, n_rows // 8, body, c0)


def _branch_fwd_math(cur_ref, halo_ref, cw_ref, vec_ref, wx_ref, wa_ref, bufa, bufb, bufd, first, t):
    def halo(c0):
        v = halo_ref[:, c0:c0 + BW].astype(F32)
        return jnp.where(first, 0.0, v)

    def cur(c0):
        return cur_ref[:, c0:c0 + BW].astype(F32)

    out = {}
    bufa[0:HALO, :] = halo(C_AX)
    bufa[HALO:HALO + t, :] = cur(C_AX)
    ca = jnp.zeros((t, BW), F32) + vec_ref[V_CAB:V_CAB + 1, :]
    for k in range(CONV_A):
        ca = ca + cw_ref[CW_A + k:CW_A + k + 1, :] * bufa[pl.ds(HALO - (CONV_A - 1) + k, t), :]
    gi = _sigmoid(_dot(ca, wx_ref[...], _NN) + vec_ref[V_BX:V_BX + 1, :])
    gr = _sigmoid(_dot(ca, wa_ref[...], _NN) + vec_ref[V_BA:V_BA + 1, :])
    sp = _softplus(-vec_ref[V_LAM:V_LAM + 1, :])
    la = -LRU_C * sp * gr
    a = jnp.exp(la)
    mult = jnp.sqrt(_neg_expm1(2.0 * la))
    out.update(ca=ca, gi=gi, gr=gr, sp=sp, a=a, mult=mult)
    bufb[0:HALO, :] = halo(C_BC) * halo(C_BV)
    bufb[HALO:HALO + t, :] = cur(C_BC) * cur(C_BV)
    cb = jnp.zeros((t, BW), F32)
    for k in range(CONV_B):
        cb = cb + cw_ref[CW_B + k:CW_B + k + 1, :] * bufb[pl.ds(HALO - (CONV_B - 1) + k, t), :]
    out.update(cb=cb)
    bufd[0:HALO, :] = halo(C_D1) * _sigmoid(halo(C_D2))
    s2 = _sigmoid(cur(C_D2))
    bufd[HALO:HALO + t, :] = cur(C_D1) * s2
    cd = jnp.zeros((t, BW), F32) + vec_ref[V_CDB:V_CDB + 1, :]
    for k in range(CONV_D):
        cd = cd + cw_ref[CW_D + k:CW_D + k + 1, :] * bufd[pl.ds(HALO - (CONV_D - 1) + k, t), :]
    mu = jnp.mean(cd, axis=-1, keepdims=True)
    xc = cd - mu
    rstd = lax.rsqrt(jnp.mean(xc * xc, axis=-1, keepdims=True) + EPS)
    xh = xc * rstd
    ln = xh * vec_ref[V_LNG:V_LNG + 1, :] + vec_ref[V_LNB:V_LNB + 1, :]
    out.update(s2=s2, xh=xh, rstd=rstd, ln=ln)
    return out


def fwd_branch(proj, convw, vecs, wx_bd, wa_bd, l, comm=None):
    s = proj.shape[0]
    t = min(256, s)

    def body(cur_ref, halo_ref, cw_ref, vec_ref, wx_ref, wa_ref, pre_ref, h_ref, bufa, bufb, bufd, a_s, u_s, hcar):
        first = pl.program_id(0) == 0

        @pl.when(first)
        def _():
            hcar[...] = jnp.zeros((1, BW), F32)

        v = _branch_fwd_math(cur_ref, halo_ref, cw_ref, vec_ref, wx_ref, wa_ref, bufa, bufb, bufd, first, t)
        a_s[...] = v["a"]
        u_s[...] = v["ca"] * v["gi"] * v["mult"]
        hcar[...] = _scan_fwd(a_s, u_s, h_ref, hcar[...], t)
        gg, _ = _gelu_and_grad(cur_ref[:, C_AG:C_AG + BW].astype(F32))
        pre_ref[:, 0:BW] = (h_ref[...] * gg).astype(BF16)
        pre_ref[:, BW:2 * BW] = (cur_ref[:, C_BB:C_BB + BW].astype(F32) * v["cb"]).astype(BF16)
        ln = v["ln"]
        pre_ref[:, 2 * BW:3 * BW] = (ln * _sigmoid(ln)).astype(BF16)

    hb = t // HALO
    return _call(
        body, comm, (proj, proj, convw, vecs, wx_bd, wa_bd), grid=(s // t,),
        in_specs=[pl.BlockSpec((t, GL0), lambda i: (i, 0)),
                  pl.BlockSpec((HALO, GL0), lambda i: (jnp.maximum(i * hb - 1, 0), 0)),
                  pl.BlockSpec((None, CW_ROWS, BW), lambda i: (l, 0, 0)),
                  pl.BlockSpec((None, V_ROWS, BW), lambda i: (l, 0, 0)),
                  pl.BlockSpec((None, BW, BW), lambda i: (l, 0, 0)),
                  pl.BlockSpec((None, BW, BW), lambda i: (l, 0, 0))],
        out_specs=[pl.BlockSpec((t, 3 * BW), lambda i: (i, 0)), pl.BlockSpec((t, BW), lambda i: (i, 0))],
        out_shape=[_sds((s, 3 * BW), BF16), _sds((s, BW), F32)],
        scratch_shapes=[pltpu.VMEM((t + HALO, BW), F32)] * 3 + [pltpu.VMEM((t, BW), F32)] * 2 + [pltpu.VMEM((1, BW), F32)],
        name=f"fwd_branch{l}")


def _attn_mask_bias():
    qi = lax.broadcasted_iota(jnp.int32, (ATT_BLK, 2 * ATT_BLK), 0)
    ki = lax.broadcasted_iota(jnp.int32, (ATT_BLK, 2 * ATT_BLK), 1)
    dist = qi + ATT_BLK - ki
    valid = (dist >= 0) & (dist < ATT_BLK)
    return dist.astype(F32), valid, ki


def _attn_probs(q_ref, kvp_ref, kvc_ref, vec_ref, h, distf, valid):
    hk = h // (N_HEADS // N_KV)
    slope = 2.0 ** (-8.0 * (h + 1) / N_HEADS)
    qh = q_ref[:, h * HD:(h + 1) * HD]
    k2 = jnp.concatenate([kvp_ref[:, hk * HD:(hk + 1) * HD], kvc_ref[:, hk * HD:(hk + 1) * HD]], axis=0)
    v2 = jnp.concatenate([kvp_ref[:, (N_KV + hk) * HD:(N_KV + hk + 1) * HD],
                          kvc_ref[:, (N_KV + hk) * HD:(N_KV + hk + 1) * HD]], axis=0)
    sc = _dot(qh, k2, _NT) * (HD ** -0.5) - slope * distf
    sc = jnp.where(valid, sc, NEG_INF)
    sink = vec_ref[V_SINK:V_SINK + 1, h:h + 1]
    m = jnp.maximum(jnp.max(sc, axis=-1, keepdims=True), sink)
    p = jnp.exp(sc - m)
    es = jnp.exp(sink - m)
    inv = 1.0 / (jnp.sum(p, axis=-1, keepdims=True) + es)
    return qh, k2, v2, p * inv, es * inv


def fwd_attn(proj, vecs, l, comm=None):
    s = proj.shape[0]
    nb = s // ATT_BLK

    def body(q_ref, kvp_ref, kvc_ref, vec_ref, o_ref):
        distf, valid, ki = _attn_mask_bias()
        valid = valid & ((pl.program_id(0) > 0) | (ki >= ATT_BLK))
        for h in range(N_HEADS):
            _, _, v2, p, _ = _attn_probs(q_ref, kvp_ref, kvc_ref, vec_ref, h, distf, valid)
            o_ref[:, h * HD:(h + 1) * HD] = _dot(p, v2, _NN).astype(BF16)

    return _call(
        body, comm, (proj, proj, proj, vecs), grid=(nb,),
        in_specs=[pl.BlockSpec((ATT_BLK, BW), lambda i: (i, C_Q // BW)),
                  pl.BlockSpec((ATT_BLK, 256), lambda i: (jnp.maximum(i - 1, 0), C_K // 256)),
                  pl.BlockSpec((ATT_BLK, 256), lambda i: (i, C_K // 256)),
                  pl.BlockSpec((None, V_ROWS, BW), lambda i: (l, 0, 0))],
        out_specs=pl.BlockSpec((ATT_BLK, BW), lambda i: (i, 0)),
        out_shape=_sds((s, BW), BF16), name=f"fwd_attn{l}")


def fwd_merge(x, proj, pre_abd, pre_c, wt_a, wt_b, wt_c, wt_d, w_o, l, comm=None):
    s = x.shape[0]
    tm = min(256, s)

    def body(x_ref, gl_ref, pabd_ref, pc_ref, wa_ref, wb_ref, wc_ref, wd_ref, wo_ref, y_ref, mg_ref, h1_ref):
        pres = (pabd_ref[:, 0:BW], pabd_ref[:, BW:2 * BW], pc_ref[...], pabd_ref[:, 2 * BW:3 * BW])
        merged = jnp.zeros((tm, D), F32)
        for k, (pre, w_ref) in enumerate(zip(pres, (wa_ref, wb_ref, wc_ref, wd_ref))):
            yk = _dot(pre, w_ref[...], _NT)
            y_ref[:, k * D:(k + 1) * D] = yk.astype(BF16)
            merged = merged + _sigmoid(gl_ref[:, k * D:(k + 1) * D].astype(F32)) * yk
        mg_ref[...] = merged.astype(BF16)
        h1_ref[...] = x_ref[...] + _dot(merged, wo_ref[...], _NN)

    wspec = pl.BlockSpec((D, BW), lambda i: (0, 0))
    return _call(
        body, comm, (x, proj, pre_abd, pre_c, wt_a, wt_b, wt_c, wt_d, w_o), grid=(s // tm,),
        in_specs=[pl.BlockSpec((tm, D), lambda i: (i, 0)),
                  pl.BlockSpec((E(tm), E(4 * D)), lambda i: (i * tm, GL0)),
                  pl.BlockSpec((tm, 3 * BW), lambda i: (i, 0)),
                  pl.BlockSpec((tm, BW), lambda i: (i, 0)),
                  wspec, wspec, wspec, wspec,
                  pl.BlockSpec((D, D), lambda i: (0, 0))],
        out_specs=[pl.BlockSpec((tm, 4 * D), lambda i: (i, 0)), pl.BlockSpec((tm, D), lambda i: (i, 0)),
                   pl.BlockSpec((tm, D), lambda i: (i, 0))],
        out_shape=[_sds((s, 4 * D), BF16), _sds((s, D), BF16), _sds((s, D), F32)], name=f"fwd_merge{l}")


def fwd_ffn(h1, g2, wt_gate, wt_up, w_down, l, comm=None):
    s = h1.shape[0]
    tm = min(512, s)
    fc = FF // 2

    def body(h_ref, g_ref, wg_ref, wu_ref, wd_ref, xo_ref, fg_ref, fu_ref, hn_ref, acc_ref):
        j = pl.program_id(1)

        @pl.when(j == 0)
        def _():
            hv = h_ref[...]
            r = lax.rsqrt(jnp.mean(hv * hv, axis=-1, keepdims=True) + EPS)
            hn_ref[...] = (hv * r * g_ref[l:l + 1, :]).astype(BF16)
            acc_ref[...] = hv

        fg = _dot(hn_ref[...], wg_ref[...], _NT)
        fu = _dot(hn_ref[...], wu_ref[...], _NT)
        fg_ref[...] = fg.astype(BF16)
        fu_ref[...] = fu.astype(BF16)
        acc_ref[...] += _dot(fg * _sigmoid(fg) * fu, wd_ref[...], _NN)

        @pl.when(j == pl.num_programs(1) - 1)
        def _():
            xo_ref[...] = acc_ref[...]

    wspec = pl.BlockSpec((fc, D), lambda i, j: (j, 0))
    return _call(
        body, comm, (h1, g2, wt_gate, wt_up, w_down), grid=(s // tm, FF // fc),
        in_specs=[pl.BlockSpec((tm, D), lambda i, j: (i, 0)), pl.BlockSpec((DEPTH, D), lambda i, j: (0, 0)),
                  wspec, wspec, wspec],
        out_specs=[pl.BlockSpec((tm, D), lambda i, j: (i, 0)), pl.BlockSpec((tm, fc), lambda i, j: (i, j)),
                   pl.BlockSpec((tm, fc), lambda i, j: (i, j))],
        out_shape=[_sds((s, D), F32), _sds((s, FF), BF16), _sds((s, FF), BF16)],
        scratch_shapes=[pltpu.VMEM((tm, D), BF16), pltpu.VMEM((tm, D), F32)], name=f"fwd_ffn{l}")


def loss_head(x, gf, target):
    s = x.shape[0]
    tm = min(512, s)

    def body(x_ref, g_ref, t_ref, dx_ref, st_ref):
        @pl.when(pl.program_id(0) == 0)
        def _():
            st_ref[...] = jnp.zeros((8, D), F32)

        xv = x_ref[...]
        g = g_ref[...]
        r = lax.rsqrt(jnp.mean(xv * xv, axis=-1, keepdims=True) + EPS)
        n = xv * r
        err = n * g - t_ref[...]
        dy = err * (1.0 / D)
        dn = dy * g
        dx_ref[...] = r * (dn - n * jnp.mean(dn * n, axis=-1, keepdims=True))
        st_ref[0:1, :] += jnp.sum(dy * n, axis=0, keepdims=True)
        lsum = 0.5 * jnp.sum(jnp.mean(err * err, axis=-1, keepdims=True), axis=0, keepdims=True)
        st_ref[1:2, :] += jnp.broadcast_to(lsum, (1, D))

    return pl.pallas_call(
        body, grid=(s // tm,),
        in_specs=[pl.BlockSpec((tm, D), lambda i: (i, 0)), pl.BlockSpec((1, D), lambda i: (0, 0)),
                  pl.BlockSpec((tm, D), lambda i: (i, 0))],
        out_specs=[pl.BlockSpec((tm, D), lambda i: (i, 0)), pl.BlockSpec((8, D), lambda i: (0, 0))],
        out_shape=[_sds((s, D), F32), _sds((8, D), F32)],
        compiler_params=_cparams(1), name="loss_head")(x, gf, target)


def _edge_index(j, i, n_j, n_i):
    return jnp.where((j == 0) | (j == n_j - 1), i, n_i - 1)


def bwd_ffn(dxo, h1, fg, fu, g2, wt_gate, wt_up, w_down, l, comm=None):
    s = h1.shape[0]
    tm = min(512, s)
    fc = 256
    n_j, n_i = FF // fc, s // tm

    def body(dxo_ref, h_ref, fg_ref, fu_ref, g_ref, wg_ref, wu_ref, wd_ref,
             dh_ref, dwg_ref, dwu_ref, dwd_ref, st_ref, dhn, dxo_b, hn_b, ag, au, ad):
        j, i = pl.program_id(0), pl.program_id(1)
        rows = pl.ds(pl.multiple_of(i * tm, tm), tm)
        g = g_ref[l:l + 1, :]

        @pl.when(j == 0)
        def _():
            hv = h_ref[...]
            r = lax.rsqrt(jnp.mean(hv * hv, axis=-1, keepdims=True) + EPS)
            hn_b[rows, :] = (hv * r * g).astype(BF16)
            dxo_b[rows, :] = dxo_ref[...].astype(BF16)
            dhn[rows, :] = jnp.zeros((tm, D), F32)

        @pl.when((j == 0) & (i == 0))
        def _():
            st_ref[...] = jnp.zeros((8, D), F32)

        @pl.when(i == 0)
        def _():
            ag[...] = jnp.zeros((fc, D), F32)
            au[...] = jnp.zeros((fc, D), F32)
            ad[...] = jnp.zeros((fc, D), F32)

        fgv = fg_ref[...].astype(F32)
        fuv = fu_ref[...].astype(F32)
        sg = _sigmoid(fgv)
        sil = fgv * sg
        dxb = dxo_b[rows, :]
        hnb = hn_b[rows, :]
        d_act = _dot(dxb, wd_ref[...], _NT)
        ad[...] += _dot(sil * fuv, dxb, _TN)
        d_fg = (d_act * fuv * (sg * (1.0 + fgv * (1.0 - sg)))).astype(BF16)
        d_fu = (d_act * sil).astype(BF16)
        ag[...] += _dot(d_fg, hnb, _TN)
        au[...] += _dot(d_fu, hnb, _TN)
        dhn[rows, :] += _dot(d_fg, wg_ref[...], _NN) + _dot(d_fu, wu_ref[...], _NN)

        @pl.when(i == n_i - 1)
        def _():
            dwg_ref[...] = ag[...].astype(BF16)
            dwu_ref[...] = au[...].astype(BF16)
            dwd_ref[...] = ad[...].astype(BF16)

        @pl.when(j == n_j - 1)
        def _():
            hv = h_ref[...]
            r = lax.rsqrt(jnp.mean(hv * hv, axis=-1, keepdims=True) + EPS)
            n = hv * r
            dv = dhn[rows, :]
            dn = dv * g
            dh_ref[...] = dxo_ref[...] + r * (dn - n * jnp.mean(dn * n, axis=-1, keepdims=True))
            st_ref[0:1, :] += jnp.sum(dv * n, axis=0, keepdims=True)

    edge = lambda j, i: (_edge_index(j, i, n_j, n_i), 0)
    wspec = pl.BlockSpec((fc, D), lambda j, i: (j, 0))
    dwspec = pl.BlockSpec((fc, D), lambda j, i: (j, 0))
    return _call(
        body, comm, (dxo, h1, fg, fu, g2, wt_gate, wt_up, w_down), grid=(n_j, n_i),
        in_specs=[pl.BlockSpec((tm, D), edge), pl.BlockSpec((tm, D), edge),
                  pl.BlockSpec((tm, fc), lambda j, i: (i, j)), pl.BlockSpec((tm, fc), lambda j, i: (i, j)),
                  pl.BlockSpec((DEPTH, D), lambda j, i: (0, 0)), wspec, wspec, wspec],
        out_specs=[pl.BlockSpec((tm, D), lambda j, i: (jnp.where(j == n_j - 1, i, 0), 0)),
                   dwspec, dwspec, dwspec, pl.BlockSpec((8, D), lambda j, i: (0, 0))],
        out_shape=[_sds((s, D), F32), _sds((FF, D), BF16), _sds((FF, D), BF16), _sds((FF, D), BF16), _sds((8, D), F32)],
        scratch_shapes=[pltpu.VMEM((s, D), F32), pltpu.VMEM((s, D), BF16), pltpu.VMEM((s, D), BF16),
                        pltpu.VMEM((fc, D), F32), pltpu.VMEM((fc, D), F32), pltpu.VMEM((fc, D), F32)],
        name=f"bwd_ffn{l}")


def bwd_merge(dh1, y4, proj, merged, pre_abd, pre_c, wt_a, wt_b, wt_c, wt_d, w_o, l, comm=None):
    s = dh1.shape[0]
    tm = min(256, s)
    n_i = s // tm

    def body(dh_ref, y_ref, gl_ref, mg_ref, pabd_ref, pc_ref, wa_ref, wb_ref, wc_ref, wd_ref, wo_ref,
             dgl_ref, dpre_ref, dwo_ref, dwa_ref, dwb_ref, dwc_ref, dwd_ref, ao, aa, ab, ac, ad):
        i = pl.program_id(0)
        accs = (aa, ab, ac, ad)

        @pl.when(i == 0)
        def _():
            ao[...] = jnp.zeros((D, D), F32)
            for acc in accs:
                acc[...] = jnp.zeros((D, BW), F32)

        dhb = dh_ref[...].astype(BF16)
        dmg = _dot(dhb, wo_ref[...], _NT)
        ao[...] += _dot(mg_ref[...], dhb, _TN)
        pres = (pabd_ref[:, 0:BW], pabd_ref[:, BW:2 * BW], pc_ref[...], pabd_ref[:, 2 * BW:3 * BW])
        for k, (pre, w_ref, acc) in enumerate(zip(pres, (wa_ref, wb_ref, wc_ref, wd_ref), accs)):
            gk = _sigmoid(gl_ref[:, k * D:(k + 1) * D].astype(F32))
            yk = y_ref[:, k * D:(k + 1) * D].astype(F32)
            dgl_ref[:, k * D:(k + 1) * D] = (dmg * yk * gk * (1.0 - gk)).astype(BF16)
            dyk = (dmg * gk).astype(BF16)
            dpre_ref[:, k * BW:(k + 1) * BW] = _dot(dyk, w_ref[...], _NN).astype(BF16)
            acc[...] += _dot(dyk, pre, _TN)

        @pl.when(i == n_i - 1)
        def _():
            dwo_ref[...] = ao[...].astype(BF16)
            for o_ref, acc in zip((dwa_ref, dwb_ref, dwc_ref, dwd_ref), accs):
                o_ref[...] = acc[...].astype(BF16)

    wspec = pl.BlockSpec((D, BW), lambda i: (0, 0))
    dwspec = pl.BlockSpec((D, BW), lambda i: (0, 0))
    return _call(
        body, comm, (dh1, y4, proj, merged, pre_abd, pre_c, wt_a, wt_b, wt_c, wt_d, w_o), grid=(n_i,),
        in_specs=[pl.BlockSpec((tm, D), lambda i: (i, 0)),
                  pl.BlockSpec((tm, 4 * D), lambda i: (i, 0)),
                  pl.BlockSpec((E(tm), E(4 * D)), lambda i: (i * tm, GL0)),
                  pl.BlockSpec((tm, D), lambda i: (i, 0)),
                  pl.BlockSpec((tm, 3 * BW), lambda i: (i, 0)),
                  pl.BlockSpec((tm, BW), lambda i: (i, 0)),
                  wspec, wspec, wspec, wspec,
                  pl.BlockSpec((D, D), lambda i: (0, 0))],
        out_specs=[pl.BlockSpec((E(tm), E(4 * D)), lambda i: (i * tm, GL0)),
                   pl.BlockSpec((tm, 4 * BW), lambda i: (i, 0)),
                   pl.BlockSpec((D, D), lambda i: (0, 0)), dwspec, dwspec, dwspec, dwspec],
        out_shape=[_sds((s, IN_W), BF16), _sds((s, 4 * BW), BF16), _sds((D, D), BF16)] + [_sds((D, BW), BF16)] * 4,
        scratch_shapes=[pltpu.VMEM((D, D), F32)] + [pltpu.VMEM((D, BW), F32)] * 4, name=f"bwd_merge{l}")


def bwd_attn(proj, dpre, vecs, l, comm=None):
    s = proj.shape[0]
    nb = s // ATT_BLK
    grp = N_HEADS // N_KV

    def body(q_ref, kvp_ref, kvc_ref, do_ref, vec_ref, dq_ref, dkc_ref, dkp_ref, st_ref):
        @pl.when(pl.program_id(0) == 0)
        def _():
            st_ref[...] = jnp.zeros((8, 128), F32)

        distf, valid, ki = _attn_mask_bias()
        valid = valid & ((pl.program_id(0) > 0) | (ki >= ATT_BLK))
        lane = lax.broadcasted_iota(jnp.int32, (1, 128), 1)
        dsink = jnp.zeros((1, 128), F32)
        for hk in range(N_KV):
            dk2 = jnp.zeros((2 * ATT_BLK, HD), F32)
            dv2 = jnp.zeros((2 * ATT_BLK, HD), F32)
            for gq in range(grp):
                h = hk * grp + gq
                qh, k2, v2, p, ps = _attn_probs(q_ref, kvp_ref, kvc_ref, vec_ref, h, distf, valid)
                doh = do_ref[:, h * HD:(h + 1) * HD]
                dp = _dot(doh, v2, _NT)
                delta = jnp.sum(p * dp, axis=-1, keepdims=True)
                ds = p * (dp - delta) * (HD ** -0.5)
                dq_ref[:, h * HD:(h + 1) * HD] = _dot(ds, k2, _NN).astype(BF16)
                dk2 = dk2 + _dot(ds, qh, _TN)
                dv2 = dv2 + _dot(p, doh, _TN)
                dsink = dsink + jnp.where(lane == h, -jnp.sum(ps * delta, axis=0, keepdims=True), 0.0)
            dkp_ref[:, hk * HD:(hk + 1) * HD] = dk2[0:ATT_BLK].astype(BF16)
            dkc_ref[:, hk * HD:(hk + 1) * HD] = dk2[ATT_BLK:].astype(BF16)
            dkp_ref[:, (N_KV + hk) * HD:(N_KV + hk + 1) * HD] = dv2[0:ATT_BLK].astype(BF16)
            dkc_ref[:, (N_KV + hk) * HD:(N_KV + hk + 1) * HD] = dv2[ATT_BLK:].astype(BF16)
        st_ref[0:1, :] += dsink

    return _call(
        body, comm, (proj, proj, proj, dpre, vecs), grid=(nb,),
        in_specs=[pl.BlockSpec((ATT_BLK, BW), lambda i: (i, C_Q // BW)),
                  pl.BlockSpec((ATT_BLK, 256), lambda i: (jnp.maximum(i - 1, 0), C_K // 256)),
                  pl.BlockSpec((ATT_BLK, 256), lambda i: (i, C_K // 256)),
                  pl.BlockSpec((ATT_BLK, BW), lambda i: (i, 2)),
                  pl.BlockSpec((None, V_ROWS, BW), lambda i: (l, 0, 0))],
        out_specs=[pl.BlockSpec((ATT_BLK, BW), lambda i: (i, 0)), pl.BlockSpec((ATT_BLK, 256), lambda i: (i, 0)),
                   pl.BlockSpec((ATT_BLK, 256), lambda i: (i, 0)), pl.BlockSpec((8, 128), lambda i: (0, 0))],
        out_shape=[_sds((s, BW), BF16), _sds((s, 256), BF16), _sds((s, 256), BF16), _sds((8, 128), F32)],
        name=f"bwd_attn{l}")


def bwd_branch(proj, dproj, dpre, h, dq, dkc, dkp, convw, vecs, wx_bd, wa_bd, l, comm=None):
    s = proj.shape[0]
    t = 2 * ATT_BLK
    nt = s // t
    nb = s // ATT_BLK
    hb = t // HALO

    def body(cur_ref, halo_ref, dpre_ref, h_ref, hp_ref, dq_ref, dkc_ref, dkp1_ref, dkp2_ref,
             cw_ref, vec_ref, wx_ref, wa_ref, dproj_in, dp_ref, dcw_ref, dvec_ref, dwx_ref, dwa_ref,
             bufa, bufb, bufd, a_ext, hbuf, b_s, g_s, dh_s, ga, gb, gd, dhcar):
        del dproj_in
        step = pl.program_id(0)
        ti = nt - 1 - step
        first = ti == 0

        @pl.when(step == 0)
        def _():
            dcw_ref[...] = jnp.zeros((CW_ROWS, BW), F32)
            dvec_ref[...] = jnp.zeros((V_ROWS, BW), F32)
            dwx_ref[...] = jnp.zeros((BW, BW), F32)
            dwa_ref[...] = jnp.zeros((BW, BW), F32)
            dhcar[...] = jnp.zeros((1, BW), F32)
            a_ext[t:t + 8, :] = jnp.zeros((8, BW), F32)
            ga[t:t + 8, :] = jnp.zeros((8, BW), F32)
            gb[t:t + 8, :] = jnp.zeros((8, BW), F32)
            gd[t:t + HALO, :] = jnp.zeros((HALO, BW), F32)

        def cur(c0):
            return cur_ref[:, c0:c0 + BW].astype(F32)

        def rsum(v):
            return jnp.sum(v, axis=0, keepdims=True)

        def put(c0, v):
            dp_ref[:, c0:c0 + BW] = v.astype(BF16)

        v = _branch_fwd_math(cur_ref, halo_ref, cw_ref, vec_ref, wx_ref, wa_ref, bufa, bufb, bufd, first, t)
        ca, gi, gr, sp, a, mult = v["ca"], v["gi"], v["gr"], v["sp"], v["a"], v["mult"]
        dpa = dpre_ref[:, 0:BW].astype(F32)
        gg, dgg = _gelu_and_grad(cur(C_AG))
        hv = h_ref[...]
        put(C_AG, dpa * hv * dgg)
        a_ext[0:t, :] = a
        b_s[...] = a_ext[pl.ds(1, t), :]
        g_s[...] = dpa * gg
        dhcar[...] = _scan_bwd(b_s, g_s, dh_s, dhcar[...], t)
        a_ext[t:t + 1, :] = a[0:1, :]
        dh = dh_s[...]
        hbuf[0:8, :] = jnp.where(first, 0.0, hp_ref[...])
        hbuf[8:8 + t, :] = hv
        da = dh * hbuf[pl.ds(7, t), :]
        d_ca = dh * gi * mult
        d_gi = dh * ca * mult
        d_mult = dh * ca * gi
        d_la = da * a - d_mult * (a * a) / mult
        lam = vec_ref[V_LAM:V_LAM + 1, :]
        dvec_ref[V_LAM:V_LAM + 1, :] += rsum(d_la * gr) * (LRU_C * _sigmoid(-lam))
        d_gr = d_la * (-LRU_C * sp)
        d_zr = d_gr * gr * (1.0 - gr)
        d_zi = d_gi * gi * (1.0 - gi)
        dvec_ref[V_BA:V_BA + 1, :] += rsum(d_zr)
        dvec_ref[V_BX:V_BX + 1, :] += rsum(d_zi)
        dwa_ref[...] += _dot(ca, d_zr, _TN)
        dwx_ref[...] += _dot(ca, d_zi, _TN)
        d_ca = d_ca + _dot(d_zi, wx_ref[...], _NT) + _dot(d_zr, wa_ref[...], _NT)
        dvec_ref[V_CAB:V_CAB + 1, :] += rsum(d_ca)
        ga[0:t, :] = d_ca
        d_ax = jnp.zeros((t, BW), F32)
        for k in range(CONV_A):
            d_ax = d_ax + cw_ref[CW_A + k:CW_A + k + 1, :] * ga[pl.ds(CONV_A - 1 - k, t), :]
            dcw_ref[CW_A + k:CW_A + k + 1, :] += rsum(d_ca * bufa[pl.ds(HALO - (CONV_A - 1) + k, t), :])
        ga[t:t + 8, :] = d_ca[0:8, :]
        put(C_AX, d_ax)
        dpb = dpre_ref[:, BW:2 * BW].astype(F32)
        put(C_BB, dpb * v["cb"])
        d_cb = dpb * cur(C_BB)
        gb[0:t, :] = d_cb
        d_cbin = jnp.zeros((t, BW), F32)
        for k in range(CONV_B):
            d_cbin = d_cbin + cw_ref[CW_B + k:CW_B + k + 1, :] * gb[pl.ds(CONV_B - 1 - k, t), :]
            dcw_ref[CW_B + k:CW_B + k + 1, :] += rsum(d_cb * bufb[pl.ds(HALO - (CONV_B - 1) + k, t), :])
        gb[t:t + 8, :] = d_cb[0:8, :]
        put(C_BC, d_cbin * cur(C_BV))
        put(C_BV, d_cbin * cur(C_BC))
        dpd = dpre_ref[:, 3 * BW:4 * BW].astype(F32)
        ln, xh, rstd, s2 = v["ln"], v["xh"], v["rstd"], v["s2"]
        sg = _sigmoid(ln)
        d_ln = dpd * sg * (1.0 + ln * (1.0 - sg))
        dvec_ref[V_LNG:V_LNG + 1, :] += rsum(d_ln * xh)
        dvec_ref[V_LNB:V_LNB + 1, :] += rsum(d_ln)
        d_xh = d_ln * vec_ref[V_LNG:V_LNG + 1, :]
        d_cd = rstd * (d_xh - jnp.mean(d_xh, axis=-1, keepdims=True)
                       - xh * jnp.mean(d_xh * xh, axis=-1, keepdims=True))
        dvec_ref[V_CDB:V_CDB + 1, :] += rsum(d_cd)
        gd[0:t, :] = d_cd
        d_dg = jnp.zeros((t, BW), F32)
        for k in range(CONV_D):
            d_dg = d_dg + cw_ref[CW_D + k:CW_D + k + 1, :] * gd[pl.ds(CONV_D - 1 - k, t), :]
            dcw_ref[CW_D + k:CW_D + k + 1, :] += rsum(d_cd * bufd[pl.ds(HALO - (CONV_D - 1) + k, t), :])
        gd[t:t + HALO, :] = d_cd[0:HALO, :]
        put(C_D1, d_dg * s2)
        put(C_D2, d_dg * cur(C_D1) * s2 * (1.0 - s2))
        dp_ref[:, C_Q:C_Q + BW] = dq_ref[...]
        dkp2 = jnp.where(step == 0, 0.0, dkp2_ref[...].astype(F32))
        dp_ref[0:ATT_BLK, C_K:C_K + 256] = (dkc_ref[0:ATT_BLK, :].astype(F32) + dkp1_ref[...].astype(F32)).astype(BF16)
        dp_ref[ATT_BLK:t, C_K:C_K + 256] = (dkc_ref[ATT_BLK:t, :].astype(F32) + dkp2).astype(BF16)

    rev = lambda i: nt - 1 - i
    full = lambda r, c: pl.BlockSpec((r, c), lambda i: (0, 0))
    return _call(
        body, comm, (proj, proj, dpre, h, h, dq, dkc, dkp, dkp, convw, vecs, wx_bd, wa_bd, dproj), grid=(nt,),
        in_specs=[pl.BlockSpec((t, GL0), lambda i: (rev(i), 0)),
                  pl.BlockSpec((HALO, GL0), lambda i: (jnp.maximum(rev(i) * hb - 1, 0), 0)),
                  pl.BlockSpec((t, 4 * BW), lambda i: (rev(i), 0)),
                  pl.BlockSpec((t, BW), lambda i: (rev(i), 0)),
                  pl.BlockSpec((8, BW), lambda i: (jnp.maximum(rev(i) * (t // 8) - 1, 0), 0)),
                  pl.BlockSpec((t, BW), lambda i: (rev(i), 0)),
                  pl.BlockSpec((t, 256), lambda i: (rev(i), 0)),
                  pl.BlockSpec((ATT_BLK, 256), lambda i: (2 * rev(i) + 1, 0)),
                  pl.BlockSpec((ATT_BLK, 256), lambda i: (jnp.minimum(2 * rev(i) + 2, nb - 1), 0)),
                  pl.BlockSpec((None, CW_ROWS, BW), lambda i: (l, 0, 0)),
                  pl.BlockSpec((None, V_ROWS, BW), lambda i: (l, 0, 0)),
                  pl.BlockSpec((None, BW, BW), lambda i: (l, 0, 0)),
                  pl.BlockSpec((None, BW, BW), lambda i: (l, 0, 0)),
                  pl.BlockSpec(memory_space=pl.ANY)],
        out_specs=[pl.BlockSpec((t, GL0), lambda i: (rev(i), 0)),
                   full(CW_ROWS, BW), full(V_ROWS, BW), full(BW, BW), full(BW, BW)],
        out_shape=[_sds((s, IN_W), BF16), _sds((CW_ROWS, BW), F32), _sds((V_ROWS, BW), F32),
                   _sds((BW, BW), F32), _sds((BW, BW), F32)],
        scratch_shapes=[pltpu.VMEM((t + HALO, BW), F32)] * 3
        + [pltpu.VMEM((t + 8, BW), F32), pltpu.VMEM((t + 8, BW), F32)]
        + [pltpu.VMEM((t, BW), F32)] * 3
        + [pltpu.VMEM((t + 8, BW), F32), pltpu.VMEM((t + 8, BW), F32), pltpu.VMEM((t + HALO, BW), F32),
           pltpu.VMEM((1, BW), F32)],
        aliases={13: 0}, name=f"bwd_branch{l}")


def bwd_proj(dproj, x, dh1, g1, wt_in, l, comm=None):
    s = x.shape[0]
    tm = min(512, s)
    ck = 1408
    n_j, n_i = IN_W // ck, s // tm

    def body(dp_ref, x_ref, dh_ref, g_ref, w_ref, dx_ref, dw_ref, st_ref, dxn, xn_b, acc):
        j, i = pl.program_id(0), pl.program_id(1)
        rows = pl.ds(pl.multiple_of(i * tm, tm), tm)
        g = g_ref[l:l + 1, :]

        @pl.when(j == 0)
        def _():
            xv = x_ref[...]
            r = lax.rsqrt(jnp.mean(xv * xv, axis=-1, keepdims=True) + EPS)
            xn_b[rows, :] = (xv * r * g).astype(BF16)
            dxn[rows, :] = jnp.zeros((tm, D), F32)

        @pl.when((j == 0) & (i == 0))
        def _():
            st_ref[...] = jnp.zeros((8, D), F32)

        @pl.when(i == 0)
        def _():
            acc[...] = jnp.zeros((ck, D), F32)

        dp = dp_ref[...]
        dxn[rows, :] += _dot(dp, w_ref[...], _NN)
        acc[...] += _dot(dp, xn_b[rows, :], _TN)

        @pl.when(i == n_i - 1)
        def _():
            dw_ref[...] = acc[...].astype(BF16)

        @pl.when(j == n_j - 1)
        def _():
            xv = x_ref[...]
            r = lax.rsqrt(jnp.mean(xv * xv, axis=-1, keepdims=True) + EPS)
            n = xv * r
            dv = dxn[rows, :]
            dn = dv * g
            dx_ref[...] = dh_ref[...] + r * (dn - n * jnp.mean(dn * n, axis=-1, keepdims=True))
            st_ref[0:1, :] += jnp.sum(dv * n, axis=0, keepdims=True)

    lastrow = lambda j, i: (jnp.where(j == n_j - 1, i, 0), 0)
    return _call(
        body, comm, (dproj, x, dh1, g1, wt_in), grid=(n_j, n_i),
        in_specs=[pl.BlockSpec((tm, ck), lambda j, i: (i, j)),
                  pl.BlockSpec((tm, D), lambda j, i: (_edge_index(j, i, n_j, n_i), 0)),
                  pl.BlockSpec((tm, D), lastrow),
                  pl.BlockSpec((DEPTH, D), lambda j, i: (0, 0)),
                  pl.BlockSpec((ck, D), lambda j, i: (j, 0))],
        out_specs=[pl.BlockSpec((tm, D), lastrow), pl.BlockSpec((ck, D), lambda j, i: (j, 0)),
                   pl.BlockSpec((8, D), lambda j, i: (0, 0))],
        out_shape=[_sds((s, D), F32), _sds((IN_W, D), BF16), _sds((8, D), F32)],
        scratch_shapes=[pltpu.VMEM((s, D), F32), pltpu.VMEM((s, D), BF16), pltpu.VMEM((ck, D), F32)],
        name=f"bwd_proj{l}")


def _block_diag(w):
    nl, nb, bw, _ = w.shape
    eye = jnp.eye(nb, dtype=w.dtype)
    return jnp.einsum("lhij,hk->lhikj", w, eye).reshape(nl, nb * bw, nb * bw).astype(BF16)


def _diag_blocks(m):
    return jnp.stack([m[HD * h:HD * (h + 1), HD * h:HD * (h + 1)] for h in range(BW // HD)])


class NoOverlap:
    def __init__(self, big):
        self.big = big

    def weights(self, l):
        return self.big[l]

    def job(self, slot, l):
        return None

    def done(self, slot, l, results):
        pass

    def new_grads(self, group, l, grads):
        pass


def local_step(x, target, norm1_g, norm2_g, final_g, convw, vecs, lru_wx, lru_wa, plan):
    wx_bd, wa_bd = _block_diag(lru_wx), _block_diag(lru_wa)

    def run(fn, slot, l, *args):
        res, cres = fn(*args, l, comm=plan.job(slot, l))
        plan.done(slot, l, cres)
        return res

    saved = []
    for l in range(DEPTH):
        proj = run(fwd_proj, "fwd_proj", l, x, norm1_g, plan.weights(l)["in_t"])
        pre_abd, h = run(fwd_branch, "fwd_branch", l, proj, convw, vecs, wx_bd, wa_bd)
        pre_c = run(fwd_attn, "fwd_attn", l, proj, vecs)
        w = plan.weights(l)
        y4, merged, h1 = run(fwd_merge, "fwd_merge", l, x, proj, pre_abd, pre_c, w["a_t"], w["b_t"], w["c_t"], w["d_t"], w["o"])
        w = plan.weights(l)
        x_out, fg, fu = run(fwd_ffn, "fwd_ffn", l, h1, norm2_g, w["gate_t"], w["up_t"], w["down"])
        saved.append((x, proj, pre_abd, h, pre_c, y4, merged, h1, fg, fu))
        x = x_out
    dx, head_stats = loss_head(x, final_g.reshape(1, D), target)
    small = [None] * DEPTH
    for l in reversed(range(DEPTH)):
        x_in, proj, pre_abd, h, pre_c, y4, merged, h1, fg, fu = saved[l]
        w = plan.weights(l)
        dh1, d_gate, d_up, d_down, st_ffn = run(bwd_ffn, "bwd_ffn", l, dx, h1, fg, fu, norm2_g, w["gate_t"], w["up_t"], w["down"])
        plan.new_grads("ffn", l, dict(gate_t=d_gate, up_t=d_up, down=d_down))
        dproj, dpre, d_o, d_a, d_b, d_c, d_d = run(
            bwd_merge, "bwd_merge", l, dh1, y4, proj, merged, pre_abd, pre_c, w["a_t"], w["b_t"], w["c_t"], w["d_t"], w["o"])
        plan.new_grads("out", l, dict(a_t=d_a, b_t=d_b, c_t=d_c, d_t=d_d, o=d_o))
        dq, dkc, dkp, st_attn = run(bwd_attn, "bwd_attn", l, proj, dpre, vecs)
        dproj, dcw, dvec, dwx, dwa = run(bwd_branch, "bwd_branch", l, proj, dproj, dpre, h, dq, dkc, dkp, convw, vecs, wx_bd, wa_bd)
        dx, d_in, st_proj = run(bwd_proj, "bwd_proj", l, dproj, x_in, dh1, norm1_g, w["in_t"])
        plan.new_grads("in", l, dict(in_t=d_in))
        small[l] = dict(norm1_g=st_proj[0], norm2_g=st_ffn[0], convw=dcw, vecs=dvec, sinks=st_attn[0, :N_HEADS],
                        lru_wx=_diag_blocks(dwx), lru_wa=_diag_blocks(dwa))
    return head_stats, dx, small


BIG = dict(in_t=("w_in", "view"), a_t=("w_a_out", "transpose"), b_t=("w_b_out", "transpose"), c_t=("w_c_out", "transpose"),
           d_t=("w_d_out", "transpose"), o=("w_o", "plain"), gate_t=("w_ffn_gate", "view"), up_t=("w_ffn_up", "view"),
           down=("w_ffn_down", "plain"))


def cast_transpose(w, name):
    nl, a, b = w.shape
    ta = min(256, a)

    def body(w_ref, o_ref):
        o_ref[...] = w_ref[...].T.astype(BF16)

    return pl.pallas_call(
        body, grid=(nl, a // ta),
        in_specs=[pl.BlockSpec((None, ta, b), lambda l, i: (l, i, 0))],
        out_specs=pl.BlockSpec((None, b, ta), lambda l, i: (l, 0, i)),
        out_shape=_sds((nl, b, a), BF16), compiler_params=_cparams(2), name=name)(w)


def add_partial(mine, recv, core, name):
    _, _, rows, cols = mine.shape

    def body(core_ref, a_ref, b_ref, o_ref):
        del core_ref
        o_ref[...] = (a_ref[...].astype(F32) + b_ref[...].astype(F32)).astype(BF16)

    return pl.pallas_call(
        body,
        grid_spec=pltpu.PrefetchScalarGridSpec(
            num_scalar_prefetch=1, grid=(4,),
            in_specs=[pl.BlockSpec((None, None, rows, cols), lambda i, cr: (i, cr[0], 0, 0)),
                      pl.BlockSpec((None, rows, cols), lambda i, cr: (i, 0, 0))],
            out_specs=pl.BlockSpec((None, rows, cols), lambda i, cr: (i, 0, 0))),
        out_shape=_sds((4, rows, cols), BF16), compiler_params=_cparams(1), name=name)(core, mine, recv)


def _adamw(w, g, m, v):
    m = ADAM_B1 * m + (1.0 - ADAM_B1) * g
    v = ADAM_B2 * v + (1.0 - ADAM_B2) * (g * g)
    m_hat = m / (1.0 - ADAM_B1 ** ADAM_STEP)
    v_hat = v / (1.0 - ADAM_B2 ** ADAM_STEP)
    delta = -ADAM_LR * (m_hat / (jnp.sqrt(v_hat) + ADAM_EPS) + ADAM_WD * w)
    return delta, m, v


def adamw_big(contrib, w, m, v, transposed, name, comm=None):
    nsrc, nl, rows, cols = contrib.shape
    ct = 256

    def body(c_ref, w_ref, m_ref, v_ref, g_out, d_out, m_out, v_out):
        g = c_ref[0].astype(F32)
        for src in range(1, nsrc):
            g = g + c_ref[src].astype(F32)
        if transposed:
            g = g.T
        delta, mn, vn = _adamw(w_ref[...], g, m_ref[...], v_ref[...])
        g_out[...] = g
        d_out[...] = delta
        m_out[...] = mn
        v_out[...] = vn

    if transposed:
        wspec = pl.BlockSpec((None, ct, rows), lambda l, j: (l, j, 0))
    else:
        wspec = pl.BlockSpec((None, rows, ct), lambda l, j: (l, 0, j))
    return _call(
        body, comm, (contrib, w, m, v), grid=(nl, cols // ct),
        in_specs=[pl.BlockSpec((nsrc, None, rows, ct), lambda l, j: (0, l, 0, j)), wspec, wspec, wspec],
        out_specs=[wspec] * 4, out_shape=[_sds(w.shape, F32)] * 4, name=name)


def adamw_small(gathered, w, m, v):
    r = w.shape[0]

    def body(c_ref, w_ref, m_ref, v_ref, g_out, d_out, m_out, v_out):
        g = c_ref[0]
        for dev in range(1, NDEV):
            g = g + c_ref[dev]
        delta, mn, vn = _adamw(w_ref[...], g, m_ref[...], v_ref[...])
        g_out[...] = g
        d_out[...] = delta
        m_out[...] = mn
        v_out[...] = vn

    return pl.pallas_call(body, out_shape=[_sds((r, D), F32)] * 4, name="adamw_small",
                          compiler_params=pltpu.CompilerParams(vmem_limit_bytes=VMEM_LIMIT))(gathered, w, m, v)


VEC_NAMES = ("conv_a_b", "lru_bx", "lru_ba", "lru_lambda", "conv_d_b", "ln_d_g", "ln_d_b")
P_N1, P_N2, P_VEC, P_CONV, P_WX, P_WA, P_LAYER = 0, 1, 2, 6, 26, 58, 90
P_FINAL, P_LOSS, P_ROWS = DEPTH * P_LAYER, DEPTH * P_LAYER + 1, 184


def _stack_vecs(p):
    rows = [p[n] for n in VEC_NAMES] + [jnp.pad(p["sinks"], ((0, 0), (0, BW - N_HEADS)))]
    return jnp.stack(rows, axis=1)


def _stack_convs(p):
    nl, _, ch = p["conv_a_w"].shape
    z = jnp.zeros((nl, 1, ch), F32)
    return jnp.concatenate([p["conv_a_w"], p["conv_b_w"], z, p["conv_d_w"], z], axis=1)


def _pack_small(p, convw, extra_rows):
    vec = _stack_vecs(p)
    rows = []
    for l in range(DEPTH):
        rows += [p["norm1_g"][l].reshape(1, D), p["norm2_g"][l].reshape(1, D), vec[l].reshape(-1, D),
                 convw[l].reshape(-1, D), p["lru_wx"][l].reshape(-1, D), p["lru_wa"][l].reshape(-1, D)]
    rows += [p["final_g"].reshape(1, D)] + extra_rows
    pack = jnp.concatenate(rows, axis=0)
    return jnp.pad(pack, ((0, P_ROWS - pack.shape[0]), (0, 0)))


def _unpack_small(pack, ch0):
    per_layer = [pack[l * P_LAYER:(l + 1) * P_LAYER] for l in range(DEPTH)]
    out = dict(norm1_g=jnp.stack([q[P_N1] for q in per_layer]), norm2_g=jnp.stack([q[P_N2] for q in per_layer]),
               final_g=pack[P_FINAL])
    vec = jnp.stack([q[P_VEC:P_CONV].reshape(V_ROWS, BW) for q in per_layer])
    for i, n in enumerate(VEC_NAMES):
        out[n] = vec[:, i]
    out["sinks"] = vec[:, V_SINK, :N_HEADS]
    convw = jnp.stack([q[P_CONV:P_WX].reshape(CW_ROWS, BW) for q in per_layer])
    mine = lax.dynamic_slice_in_dim(convw, ch0, BW // NDEV, axis=2)
    out.update(conv_a_w=mine[:, CW_A:CW_A + CONV_A], conv_b_w=mine[:, CW_B:CW_B + CONV_B],
               conv_d_w=mine[:, CW_D:CW_D + CONV_D])
    out["lru_wx"] = jnp.stack([q[P_WX:P_WA].reshape(BW // HD, HD, HD) for q in per_layer])
    out["lru_wa"] = jnp.stack([q[P_WA:P_LAYER].reshape(BW // HD, HD, HD) for q in per_layer])
    return out


def merge_jobs(jobs):
    jobs = [j for j in jobs if j is not None]
    if not jobs:
        return None, []
    inputs, aliases, outs, sems, cuts = [], {}, [], [], []
    for j in jobs:
        i0, o0, s0 = len(inputs), len(outs), len(sems)
        aliases.update({i0 + i: o0 + o for i, o in j.aliases.items()})
        inputs += j.inputs
        outs += j.out_shapes
        sems += j.sem_shapes
        cuts.append((i0, len(inputs), o0, len(outs), s0, len(sems)))

    def each(which):
        def go(cins, couts, s):
            for j, (i0, i1, o0, o1, s0, s1) in zip(jobs, cuts):
                getattr(j, which)(cins[i0:i1], couts[o0:o1], s[s0:s1])
        return go

    return CommJob(inputs, aliases, outs, sems, each("start"), each("finish")), [(c[2], c[3]) for c in cuts]


SIXTHS = 6
OUT_KINDS = ("a_t", "b_t", "c_t", "d_t", "o")
GATHER_PLAN = {
    "fwd_proj": [(k, 0, 0, 6) for k in OUT_KINDS] + [("gate_t", 0, 0, 3)],
    "fwd_branch": [("gate_t", 0, 3, 6), ("up_t", 0, 0, 3)],
    "fwd_attn": [("up_t", 0, 3, 6), ("down", 0, 0, 6)],
    "fwd_merge": [("in_t", 1, 0, 2)],
    "fwd_ffn": [("in_t", 1, 2, 6)],
}
SIBLING_PLAN = {"bwd_merge": ("ffn", 0), "bwd_branch": ("out", 0), "bwd_ffn": ("in", 1), "adamw_down": ("in", 0)}
GROUPS = dict(ffn=("gate_t", "up_t", "down"), out=OUT_KINDS)
GROUPS["in"] = ("in_t",)
CHIP_PLAN = {
    "bwd_attn": [("in_t", 1, 3, 6), ("gate_t", 0, 0, 3)],
    "bwd_branch": [("gate_t", 0, 3, 6), ("up_t", 0, 0, 6), ("down", 0, 0, 6)],
    "bwd_proj": [(k, 0, 0, 6) for k in OUT_KINDS],
    "bwd_merge": [("in_t", 1, 0, 3)],
    "adamw_gate_t": [("in_t", 0, 0, 2)], "adamw_up_t": [("in_t", 0, 2, 4)], "adamw_o": [("in_t", 0, 4, 6)],
}
SMALL_GATHER_SLOT = "adamw_gate_t"


class Overlap:
    def __init__(self, shards, core):
        self.shards = shards
        self.core = core
        self.gathered = [dict.fromkeys(BIG) for _ in range(DEPTH)]
        self.views = {}
        self.partial = {}
        self.contrib = dict.fromkeys(BIG)
        self.small_pack = self.small_gathered = None
        self._open = None

    def weights(self, l):
        return self.gathered[l]

    def new_grads(self, group, l, grads):
        for k, g in grads.items():
            self.views[k, l] = g.reshape(4, 2, g.shape[0] // NDEV, g.shape[1])

    @staticmethod
    def _rows(shard_rows, f0, f1):
        return shard_rows * f0 // SIXTHS, shard_rows * (f1 - f0) // SIXTHS

    def job(self, slot, l):
        jobs, notes = [], []
        pieces = [(k, l + dl, f0, f1) for k, dl, f0, f1 in GATHER_PLAN.get(slot, []) if l + dl < DEPTH]
        if pieces:
            jobs.append(gather_job([((k, ll), self.shards[ll][k], self.gathered[ll][k],
                                     *self._rows(self.shards[ll][k].shape[0], f0, f1)) for k, ll, f0, f1 in pieces]))
            notes.append(("gather", list(dict.fromkeys((k, ll) for k, ll, _, _ in pieces))))
        if slot in SIBLING_PLAN and l + SIBLING_PLAN[slot][1] < DEPTH:
            group, dl = SIBLING_PLAN[slot]
            keys = [(k, l + dl) for k in GROUPS[group]]
            jobs.append(sibling_exchange_job([self.views[key] for key in keys]))
            notes.append(("sibling", keys))
        pieces = [(k, l + dl, f0, f1) for k, dl, f0, f1 in CHIP_PLAN.get(slot, []) if l + dl < DEPTH]
        if pieces:
            jobs.append(chip_exchange_job([(self.partial[k, ll], self.contrib[k], k, ll,
                                            *self._rows(self.partial[k, ll].shape[1], f0, f1)) for k, ll, f0, f1 in pieces]))
            notes.append(("chips", list(dict.fromkeys(k for k, _, _, _ in pieces))))
        if slot == SMALL_GATHER_SLOT:
            jobs.append(gather_job([("small", self.small_pack, None, 0, self.small_pack.shape[0])]))
            notes.append(("small", None))
        job, spans = merge_jobs(jobs)
        self._open = (slot, l, notes, spans)
        return job

    def done(self, slot, l, results):
        open_slot, open_l, notes, spans = self._open
        assert (open_slot, open_l) == (slot, l)
        for (what, keys), (r0, r1) in zip(notes, spans):
            res = results[r0:r1]
            if what == "gather":
                for (k, ll), g in zip(keys, res):
                    self.gathered[ll][k] = g
            elif what == "sibling":
                for (k, ll), theirs in zip(keys, res):
                    self.partial[k, ll] = add_partial(self.views[k, ll], theirs, self.core, f"chip_sum_{k}{ll}")
            elif what == "chips":
                for k, c in zip(keys, res):
                    self.contrib[k] = c
            else:
                self.small_gathered, = res


SMALL = ("norm1_g", "conv_a_w", "conv_a_b", "lru_wx", "lru_bx", "lru_wa", "lru_ba", "lru_lambda", "conv_b_w", "sinks",
         "conv_d_w", "conv_d_b", "ln_d_g", "ln_d_b", "norm2_g", "final_g")
WEIGHTS = ("norm1_g", "w_in", "conv_a_w", "conv_a_b", "lru_wx", "lru_bx", "lru_wa", "lru_ba", "lru_lambda", "w_a_out",
           "conv_b_w", "w_b_out", "sinks", "w_c_out", "conv_d_w", "conv_d_b", "ln_d_g", "ln_d_b", "w_d_out", "w_o",
           "norm2_g", "w_ffn_gate", "w_ffn_up", "w_ffn_down", "final_g")


def kernel(x, norm1_g, w_in, conv_a_w, conv_a_b, lru_wx, lru_bx, lru_wa, lru_ba, lru_lambda, w_a_out, conv_b_w, w_b_out, sinks, w_c_out, conv_d_w, conv_d_b, ln_d_g, ln_d_b, w_d_out, w_o, norm2_g, w_ffn_gate, w_ffn_up, w_ffn_down, final_g, loss_target, m_norm1_g, m_w_in, m_conv_a_w, m_conv_a_b, m_lru_wx, m_lru_bx, m_lru_wa, m_lru_ba, m_lru_lambda, m_w_a_out, m_conv_b_w, m_w_b_out, m_sinks, m_w_c_out, m_conv_d_w, m_conv_d_b, m_ln_d_g, m_ln_d_b, m_w_d_out, m_w_o, m_norm2_g, m_w_ffn_gate, m_w_ffn_up, m_w_ffn_down, m_final_g, v_norm1_g, v_w_in, v_conv_a_w, v_conv_a_b, v_lru_wx, v_lru_bx, v_lru_wa, v_lru_ba, v_lru_lambda, v_w_a_out, v_conv_b_w, v_w_b_out, v_sinks, v_w_c_out, v_conv_d_w, v_conv_d_b, v_ln_d_g, v_ln_d_b, v_w_d_out, v_w_o, v_norm2_g, v_w_ffn_gate, v_w_ffn_up, v_w_ffn_down, v_final_g):
    args = dict(locals())
    w = {n: args[n] for n in WEIGHTS}
    m = {n: args["m_" + n] for n in WEIGHTS}
    v = {n: args["v_" + n] for n in WEIGHTS}
    me = _dev_index(*_mesh_pos())
    ch0 = me * (BW // NDEV)

    def rows_major(a, how):
        return jnp.swapaxes(a, 1, 2) if how == "view" else a

    stacked = {k: cast_transpose(w[n], "prep_" + k) if how == "transpose" else rows_major(w[n], how).astype(BF16)
               for k, (n, how) in BIG.items()}
    plan = Overlap([{k: stacked[k][l] for k in BIG} for l in range(DEPTH)], lax.axis_index("c").astype(jnp.int32).reshape(1))
    convs = _stack_convs(w).reshape(DEPTH * CW_ROWS, BW // NDEV)
    g_in0, g_conv = _comm_only(gather_job([(("in_t", 0), plan.shards[0]["in_t"], None, 0, plan.shards[0]["in_t"].shape[0]),
                                           ("convs", convs, None, 0, convs.shape[0])]), "gather_first")
    plan.gathered[0]["in_t"] = g_in0
    convw = g_conv.reshape(NDEV, DEPTH, CW_ROWS, BW // NDEV).transpose(1, 2, 0, 3).reshape(DEPTH, CW_ROWS, BW)

    vecs = _stack_vecs(w)
    head_stats, grad_x, grads = local_step(x[0], loss_target[0], norm1_g, norm2_g, final_g, convw, vecs, lru_wx, lru_wa, plan)

    gsmall = {n: jnp.stack([grads[l][n] for l in range(DEPTH)]) for n in ("norm1_g", "norm2_g", "lru_wx", "lru_wa", "sinks")}
    gvec = jnp.stack([grads[l]["vecs"] for l in range(DEPTH)])
    gsmall.update({n: gvec[:, i] for i, n in enumerate(VEC_NAMES)})
    gsmall["final_g"] = head_stats[0]
    gpack = _pack_small(gsmall, jnp.stack([grads[l]["convw"] for l in range(DEPTH)]), [head_stats[1:2]])
    plan.small_pack = gpack

    out = {}
    for k in ("down", "gate_t", "up_t", "o", "a_t", "b_t", "c_t", "d_t", "in_t"):
        n, how = BIG[k]
        res, cres = adamw_big(plan.contrib[k], rows_major(w[n], how), rows_major(m[n], how), rows_major(v[n], how),
                              how == "transpose", "adamw_" + k, comm=plan.job("adamw_" + k, 0))
        plan.done("adamw_" + k, 0, cres)
        out[n] = [rows_major(r, how) for r in res]
    gall = plan.small_gathered.reshape(NDEV, P_ROWS, D)

    def padded_convs(p):
        return lax.dynamic_update_slice_in_dim(jnp.zeros((DEPTH, CW_ROWS, BW), F32), _stack_convs(p), ch0, axis=2)

    zero_row = [jnp.zeros((1, D), F32)]
    packs = adamw_small(gall, *[_pack_small(p, padded_convs(p), zero_row) for p in (w, m, v)])
    small = [_unpack_small(p, ch0) for p in packs]
    for n in SMALL:
        out[n] = [s[n] for s in small]
    loss = packs[0][P_LOSS, 0]
    return (loss, grad_x[None], *[out[n][0] for n in WEIGHTS], *[out[n][1] for n in WEIGHTS],
            *[out[n][2] for n in WEIGHTS], *[out[n][3] for n in WEIGHTS])
```

```python
import functools

import jax
import jax.numpy as jnp
from jax import lax
from jax.experimental import pallas as pl
from jax.experimental.pallas import tpu as pltpu

F32 = jnp.float32
BF16 = jnp.bfloat16
E = pl.Element

D = 1024
BW = 512
IN_W = 8448
GL0 = 4352
FF = 2816
N_HEADS = 8
N_KV = 2
HD = 64
ATT_BLK = 128
EPS = 1e-6
LRU_C = 8.0
NEG_INF = -1e30
DEPTH = 2
NDEV = 8
CONV_A, CONV_B, CONV_D = 4, 3, 31
C_AX, C_AG, C_BV, C_BC, C_BB, C_Q, C_K, C_V, C_D1, C_D2 = 0, 512, 1024, 1536, 2048, 2560, 3072, 3200, 3328, 3840
CW_A, CW_B, CW_D, CW_ROWS = 0, 4, 8, 40
V_CAB, V_BX, V_BA, V_LAM, V_CDB, V_LNG, V_LNB, V_SINK, V_ROWS = 0, 1, 2, 3, 4, 5, 6, 7, 8
HALO = 32

ADAM_LR, ADAM_B1, ADAM_B2, ADAM_EPS, ADAM_WD, ADAM_STEP = 0.001, 0.9, 0.999, 1e-08, 0.01, 10

VMEM_LIMIT = 56 * 1024 * 1024

_NN = (((1,), (0,)), ((), ()))
_NT = (((1,), (1,)), ((), ()))
_TN = (((0,), (0,)), ((), ()))


def _dot(a, b, dims):
    return lax.dot_general(a.astype(BF16), b.astype(BF16), dims, preferred_element_type=F32)


def _cparams(n_axes):
    return pltpu.CompilerParams(dimension_semantics=("arbitrary",) * n_axes, vmem_limit_bytes=VMEM_LIMIT)


def _sds(shape, dtype):
    return jax.ShapeDtypeStruct(tuple(shape), dtype)


def _sigmoid(x):
    return jax.nn.sigmoid(x)


def _neg_expm1(x):
    p = x * (1.0 + x * (0.5 + x * (1.0 / 6.0 + x * (1.0 / 24.0 + x * (1.0 / 120.0)))))
    return jnp.where(x > -0.1, -p, 1.0 - jnp.exp(x))


def _softplus(z):
    return jnp.maximum(z, 0.0) + jnp.log1p(jnp.exp(-jnp.abs(z)))


def _gelu_and_grad(x):
    c = 0.7978845608028654
    inner = c * (x + 0.044715 * x * x * x)
    t = jnp.tanh(inner)
    g = 0.5 * x * (1.0 + t)
    dg = 0.5 * (1.0 + t) + 0.5 * x * (1.0 - t * t) * c * (1.0 + 3.0 * 0.044715 * x * x)
    return g, dg


ANY = pl.BlockSpec(memory_space=pl.ANY)
MESH = pl.DeviceIdType.MESH


def _mesh_pos():
    return lax.axis_index("x"), lax.axis_index("y"), lax.axis_index("c")


def _dev_index(px, py, pc):
    return 4 * px + 2 * py + pc


class CommJob:
    def __init__(self, inputs, aliases, out_shapes, sem_shapes, start, finish):
        self.inputs, self.aliases, self.out_shapes, self.sem_shapes = list(inputs), dict(aliases), list(out_shapes), list(sem_shapes)
        self.start, self.finish = start, finish


def _call(body, comm, args, *, grid, in_specs, out_specs, out_shape, scratch_shapes=(), name, aliases=None):
    single = not isinstance(out_shape, (list, tuple))
    out_specs = [out_specs] if single else list(out_specs)
    out_shape = [out_shape] if single else list(out_shape)
    scratch_shapes = list(scratch_shapes)
    n_in, n_out, n_scr, n_axes = len(in_specs), len(out_shape), len(scratch_shapes), len(grid)
    params = pltpu.CompilerParams(dimension_semantics=("arbitrary",) * n_axes, vmem_limit_bytes=VMEM_LIMIT)
    io_aliases = dict(aliases or {})
    if comm is None:
        outs = pl.pallas_call(body, grid=grid, in_specs=in_specs, out_specs=out_specs, out_shape=out_shape,
                              scratch_shapes=scratch_shapes, input_output_aliases=io_aliases, compiler_params=params,
                              name=name)(*args)
        return (outs[0] if single else outs), []
    c_in, c_out = len(comm.inputs), len(comm.out_shapes)
    io_aliases.update({n_in + i: n_out + o for i, o in comm.aliases.items()})

    def wrapped(*refs):
        ins, cins = refs[:n_in], refs[n_in:n_in + c_in]
        outs = refs[n_in + c_in:n_in + c_in + n_out]
        couts = refs[n_in + c_in + n_out:n_in + c_in + n_out + c_out]
        rest = refs[n_in + c_in + n_out + c_out:]
        scr, sems = rest[:n_scr], rest[n_scr:]
        first = functools.reduce(lambda a, b: a & b, [pl.program_id(a) == 0 for a in range(n_axes)])
        last = functools.reduce(lambda a, b: a & b, [pl.program_id(a) == pl.num_programs(a) - 1 for a in range(n_axes)])

        @pl.when(first)
        def _():
            comm.start(cins, couts, sems)

        body(*ins, *outs, *scr)

        @pl.when(last)
        def _():
            comm.finish(cins, couts, sems)

    outs = pl.pallas_call(
        wrapped, grid=grid, in_specs=list(in_specs) + [ANY] * c_in, out_specs=out_specs + [ANY] * c_out,
        out_shape=out_shape + comm.out_shapes, scratch_shapes=scratch_shapes + comm.sem_shapes,
        input_output_aliases=io_aliases, compiler_params=params, name=name)(*args, *comm.inputs)
    res, cres = outs[:n_out], outs[n_out:]
    return (res[0] if single else res), cres


def _comm_only(comm, name):
    c_in, c_out = len(comm.inputs), len(comm.out_shapes)

    def body(*refs):
        cins, couts, sems = refs[:c_in], refs[c_in:c_in + c_out], refs[c_in + c_out:]
        comm.start(cins, couts, sems)
        comm.finish(cins, couts, sems)

    return pl.pallas_call(body, in_specs=[ANY] * c_in, out_specs=[ANY] * c_out, out_shape=comm.out_shapes,
                          scratch_shapes=comm.sem_shapes, input_output_aliases=comm.aliases, name=name)(*comm.inputs)


def gather_job(pieces):
    inputs, aliases, out_shapes, plan, where = [], {}, [], [], {}
    for key, shard, gathered, row0, nrows in pieces:
        if key not in where:
            where[key] = (len(inputs), len(out_shapes))
            inputs.append(shard)
            if gathered is not None:
                aliases[len(inputs)] = len(out_shapes)
                inputs.append(gathered)
            out_shapes.append(_sds((NDEV * shard.shape[0], shard.shape[1]), shard.dtype))
        plan.append((*where[key], shard.shape[0], row0, nrows))
    n = len(plan)

    def copies(cins, couts, sems):
        send_sems, recv_sems, local_sems = sems
        x, y, c = _mesh_pos()
        me, sibling = (x, y, c), (x, y, 1 - c)
        chips = [(1 - x, y), (x, 1 - y), (1 - x, 1 - y)]
        local, first, relay, recv_ici, recv_d2d = [], [], [], [], []
        for p, (i_shard, i_out, rows, row0, nrows) in enumerate(plan):
            src = cins[i_shard].at[pl.ds(row0, nrows), :]

            def slot(dev, i_out=i_out, rows=rows, row0=row0, nrows=nrows):
                return couts[i_out].at[pl.ds(_dev_index(*dev) * rows + row0, nrows), :]

            def copy(g, dev, to, src=None, p=p, slot=slot):
                return pltpu.make_async_remote_copy(
                    src_ref=slot(dev) if src is None else src, dst_ref=slot(dev),
                    send_sem=send_sems.at[g, p], recv_sem=recv_sems.at[g, p], device_id=to, device_id_type=MESH)

            local.append(pltpu.make_async_copy(src, slot(me), local_sems.at[p]))
            first.append(copy(0, me, sibling, src=src))
            recv_d2d.append(copy(0, sibling, me))
            for j, chip in enumerate(chips):
                first.append(copy(1 + j, me, (*chip, c), src=src))
                recv_ici.append(copy(1 + j, (*chip, c), me))
                relay.append(copy(4 + j, (*chip, c), sibling))
                recv_d2d.append(copy(4 + j, (*chip, 1 - c), me))
        return local, first, relay, recv_ici, recv_d2d

    def start(cins, couts, sems):
        local, first, _, _, _ = copies(cins, couts, sems)
        for cp in local + first:
            cp.start()

    def finish(cins, couts, sems):
        local, first, relay, recv_ici, recv_d2d = copies(cins, couts, sems)
        for cp in recv_ici:
            cp.wait_recv()
        for cp in relay:
            cp.start()
        for cp in recv_d2d:
            cp.wait_recv()
        for cp in first + relay:
            cp.wait_send()
        for cp in local:
            cp.wait()

    sem_shapes = [pltpu.SemaphoreType.DMA((7, n)), pltpu.SemaphoreType.DMA((7, n)), pltpu.SemaphoreType.DMA((n,))]
    return CommJob(inputs, aliases, out_shapes, sem_shapes, start, finish)


def sibling_exchange_job(grads):
    n = len(grads)

    def copies(cins, couts, sems):
        send_sems, recv_sems = sems
        x, y, c = _mesh_pos()
        return [pltpu.make_async_remote_copy(
            src_ref=cins[q].at[:, 1 - c], dst_ref=couts[q], send_sem=send_sems.at[q], recv_sem=recv_sems.at[q],
            device_id=(x, y, 1 - c), device_id_type=MESH) for q in range(n)]

    def start(cins, couts, sems):
        for cp in copies(cins, couts, sems):
            cp.start()

    def finish(cins, couts, sems):
        cps = copies(cins, couts, sems)
        for cp in cps:
            cp.wait_recv()
        for cp in cps:
            cp.wait_send()

    return CommJob(grads, {}, [_sds((4,) + g.shape[2:], g.dtype) for g in grads],
                   [pltpu.SemaphoreType.DMA((n,)), pltpu.SemaphoreType.DMA((n,))], start, finish)


def chip_exchange_job(pieces):
    inputs, aliases, out_shapes, plan, where = [], {}, [], [], {}
    for partial, contrib, key, layer, row0, nrows in pieces:
        if key not in where:
            where[key] = len(out_shapes)
            out_shapes.append(_sds((4, DEPTH) + partial.shape[1:], partial.dtype))
            if contrib is not None:
                aliases[len(inputs)] = where[key]
                inputs.append(contrib)
        plan.append((len(inputs), where[key], layer, row0, nrows))
        inputs.append(partial)
    n = len(plan)

    def copies(cins, couts, sems):
        send_sems, recv_sems, local_sems = sems
        x, y, c = _mesh_pos()
        mine = 2 * x + y
        local, sends, recvs = [], [], []
        for p, (i_in, i_out, layer, row0, nrows) in enumerate(plan):
            rows = pl.ds(row0, nrows)
            local.append(pltpu.make_async_copy(cins[i_in].at[mine, rows, :], couts[i_out].at[mine, layer, rows, :],
                                               local_sems.at[p]))
            for j, (cx, cy) in enumerate([(1 - x, y), (x, 1 - y), (1 - x, 1 - y)]):
                theirs = 2 * cx + cy

                def copy(slot_there, j=j, p=p, cx=cx, cy=cy, theirs=theirs, i_in=i_in, i_out=i_out, layer=layer, rows=rows):
                    return pltpu.make_async_remote_copy(
                        src_ref=cins[i_in].at[theirs, rows, :], dst_ref=couts[i_out].at[slot_there, layer, rows, :],
                        send_sem=send_sems.at[j, p], recv_sem=recv_sems.at[j, p], device_id=(cx, cy, c), device_id_type=MESH)
                sends.append(copy(mine))
                recvs.append(copy(theirs))
        return local, sends, recvs

    def start(cins, couts, sems):
        local, sends, _ = copies(cins, couts, sems)
        for cp in local + sends:
            cp.start()

    def finish(cins, couts, sems):
        local, sends, recvs = copies(cins, couts, sems)
        for cp in recvs:
            cp.wait_recv()
        for cp in sends:
            cp.wait_send()
        for cp in local:
            cp.wait()

    sem_shapes = [pltpu.SemaphoreType.DMA((3, n)), pltpu.SemaphoreType.DMA((3, n)), pltpu.SemaphoreType.DMA((n,))]
    return CommJob(inputs, aliases, out_shapes, sem_shapes, start, finish)


def fwd_proj(x, g1, wt_in, l, comm=None):
    s = x.shape[0]
    tm = min(512, s)
    tn = 1408

    def body(x_ref, g_ref, w_ref, o_ref, xn_ref):
        @pl.when(pl.program_id(1) == 0)
        def _():
            xv = x_ref[...]
            r = lax.rsqrt(jnp.mean(xv * xv, axis=-1, keepdims=True) + EPS)
            xn_ref[...] = (xv * r * g_ref[l:l + 1, :]).astype(BF16)

        o_ref[...] = _dot(xn_ref[...], w_ref[...], _NT).astype(BF16)

    return _call(
        body, comm, (x, g1, wt_in), grid=(s // tm, IN_W // tn),
        in_specs=[pl.BlockSpec((tm, D), lambda i, j: (i, 0)),
                  pl.BlockSpec((DEPTH, D), lambda i, j: (0, 0)),
                  pl.BlockSpec((tn, D), lambda i, j: (j, 0))],
        out_specs=pl.BlockSpec((tm, tn), lambda i, j: (i, j)),
        out_shape=_sds((s, IN_W), BF16),
        scratch_shapes=[pltpu.VMEM((tm, D), BF16)], name=f"fwd_proj{l}")


def _scan_fwd(a_ref, u_ref, h_ref, h0, n_rows):
    row = lax.broadcasted_iota(jnp.int32, (8, BW), 0)

    def body(g, hprev):
        r = pl.multiple_of(g * 8, 8)
        a = a_ref[pl.ds(r, 8), :]
        u = u_ref[pl.ds(r, 8), :]
        for sft in (1, 2, 4):
            a_sh = jnp.where(row >= sft, pltpu.roll(a, sft, 0), 1.0)
            u_sh = jnp.where(row >= sft, pltpu.roll(u, sft, 0), 0.0)
            u = u + a * u_sh
            a = a * a_sh
        h = u + a * hprev
        h_ref[pl.ds(r, 8), :] = h
        return h[7:8, :]

    return lax.fori_loop(0, n_rows // 8, body, h0)


def _scan_bwd(b_ref, g_ref, o_ref, c0, n_rows):
    row = lax.broadcasted_iota(jnp.int32, (8, BW), 0)

    def body(k, cnext):
        r = pl.multiple_of((n_rows // 8 - 1 - k) * 8, 8)
        b = b_ref[pl.ds(r, 8), :]
        g = g_ref[pl.ds(r, 8), :]
        for sft in (1, 2, 4):
            b_sh = jnp.where(row < 8 - sft, pltpu.roll(b, 8 - sft, 0), 1.0)
            g_sh = jnp.where(row < 8 - sft, pltpu.roll(g, 8 - sft, 0), 0.0)
            g = g + b * g_sh
            b = b * b_sh
        o = g + b * cnext
        o_ref[pl.ds(r, 8), :] = o
        return o[0:1, :]

    return lax.fori_loop(0, n_rows // 8, body, c0)


def _shifted_copies(buf, shifted, n_rows):
    for r in range(1, 8):
        shifted[r - 1, 0:n_rows - 8, :] = buf[pl.ds(r, n_rows - 8), :]


def _window(buf, shifted, off, t):
    r = off % 8
    return buf[pl.ds(off, t), :] if r == 0 else shifted[r - 1, pl.ds(off - r, t), :]


def _branch_fwd_math(cur_ref, halo_ref, cw_ref, vec_ref, wx_ref, wa_ref, bufa, bufb, bufd, xd, first, t):
    def halo(c0):
        v = halo_ref[:, c0:c0 + BW].astype(F32)
        return jnp.where(first, 0.0, v)

    def cur(c0):
        return cur_ref[:, c0:c0 + BW].astype(F32)

    out = {}
    bufa[0:HALO, :] = halo(C_AX)
    bufa[HALO:HALO + t, :] = cur(C_AX)
    ca = jnp.zeros((t, BW), F32) + vec_ref[V_CAB:V_CAB + 1, :]
    for k in range(CONV_A):
        ca = ca + cw_ref[CW_A + k:CW_A + k + 1, :] * bufa[pl.ds(HALO - (CONV_A - 1) + k, t), :]
    gi = _sigmoid(_dot(ca, wx_ref[...], _NN) + vec_ref[V_BX:V_BX + 1, :])
    gr = _sigmoid(_dot(ca, wa_ref[...], _NN) + vec_ref[V_BA:V_BA + 1, :])
    sp = _softplus(-vec_ref[V_LAM:V_LAM + 1, :])
    la = -LRU_C * sp * gr
    a = jnp.exp(la)
    mult = jnp.sqrt(_neg_expm1(2.0 * la))
    out.update(ca=ca, gi=gi, gr=gr, sp=sp, a=a, mult=mult)
    bufb[0:HALO, :] = halo(C_BC) * halo(C_BV)
    bufb[HALO:HALO + t, :] = cur(C_BC) * cur(C_BV)
    cb = jnp.zeros((t, BW), F32)
    for k in range(CONV_B):
        cb = cb + cw_ref[CW_B + k:CW_B + k + 1, :] * bufb[pl.ds(HALO - (CONV_B - 1) + k, t), :]
    out.update(cb=cb)
    bufd[0:HALO, :] = halo(C_D1) * _sigmoid(halo(C_D2))
    s2 = _sigmoid(cur(C_D2))
    bufd[HALO:HALO + t, :] = cur(C_D1) * s2
    _shifted_copies(bufd, xd, t + HALO)
    cd = jnp.zeros((t, BW), F32) + vec_ref[V_CDB:V_CDB + 1, :]
    for k in range(CONV_D):
        cd = cd + cw_ref[CW_D + k:CW_D + k + 1, :] * _window(bufd, xd, HALO - (CONV_D - 1) + k, t)
    mu = jnp.mean(cd, axis=-1, keepdims=True)
    xc = cd - mu
    rstd = lax.rsqrt(jnp.mean(xc * xc, axis=-1, keepdims=True) + EPS)
    xh = xc * rstd
    ln = xh * vec_ref[V_LNG:V_LNG + 1, :] + vec_ref[V_LNB:V_LNB + 1, :]
    out.update(s2=s2, xh=xh, rstd=rstd, ln=ln)
    return out


def fwd_branch(proj, convw, vecs, wx_bd, wa_bd, l, comm=None):
    s = proj.shape[0]
    t = min(256, s)

    def body(cur_ref, halo_ref, cw_ref, vec_ref, wx_ref, wa_ref, pre_ref, h_ref, bufa, bufb, bufd, xd, a_s, u_s, hcar):
        first = pl.program_id(0) == 0

        @pl.when(first)
        def _():
            hcar[...] = jnp.zeros((1, BW), F32)

        v = _branch_fwd_math(cur_ref, halo_ref, cw_ref, vec_ref, wx_ref, wa_ref, bufa, bufb, bufd, xd, first, t)
        a_s[...] = v["a"]
        u_s[...] = v["ca"] * v["gi"] * v["mult"]
        hcar[...] = _scan_fwd(a_s, u_s, h_ref, hcar[...], t)
        gg, _ = _gelu_and_grad(cur_ref[:, C_AG:C_AG + BW].astype(F32))
        pre_ref[:, 0:BW] = (h_ref[...] * gg).astype(BF16)
        pre_ref[:, BW:2 * BW] = (cur_ref[:, C_BB:C_BB + BW].astype(F32) * v["cb"]).astype(BF16)
        ln = v["ln"]
        pre_ref[:, 2 * BW:3 * BW] = (ln * _sigmoid(ln)).astype(BF16)

    hb = t // HALO
    return _call(
        body, comm, (proj, proj, convw, vecs, wx_bd, wa_bd), grid=(s // t,),
        in_specs=[pl.BlockSpec((t, GL0), lambda i: (i, 0)),
                  pl.BlockSpec((HALO, GL0), lambda i: (jnp.maximum(i * hb - 1, 0), 0)),
                  pl.BlockSpec((None, CW_ROWS, BW), lambda i: (l, 0, 0)),
                  pl.BlockSpec((None, V_ROWS, BW), lambda i: (l, 0, 0)),
                  pl.BlockSpec((None, BW, BW), lambda i: (l, 0, 0)),
                  pl.BlockSpec((None, BW, BW), lambda i: (l, 0, 0))],
        out_specs=[pl.BlockSpec((t, 3 * BW), lambda i: (i, 0)), pl.BlockSpec((t, BW), lambda i: (i, 0))],
        out_shape=[_sds((s, 3 * BW), BF16), _sds((s, BW), F32)],
        scratch_shapes=[pltpu.VMEM((t + HALO, BW), F32)] * 3 + [pltpu.VMEM((7, t + HALO - 8, BW), F32)]
        + [pltpu.VMEM((t, BW), F32)] * 2 + [pltpu.VMEM((1, BW), F32)],
        name=f"fwd_branch{l}")


GRP = N_HEADS // N_KV


def _attn_mask_bias(first_block):
    shape = (GRP * ATT_BLK, 2 * ATT_BLK)
    qi = lax.broadcasted_iota(jnp.int32, shape, 0) & (ATT_BLK - 1)
    ki = lax.broadcasted_iota(jnp.int32, shape, 1)
    dist = qi + ATT_BLK - ki
    valid = (dist >= 0) & (dist < ATT_BLK) & (jnp.logical_not(first_block) | (ki >= ATT_BLK))
    return dist.astype(F32), valid


def _per_head(hk, values):
    hl = lax.broadcasted_iota(jnp.int32, (GRP * ATT_BLK, 1), 0) // ATT_BLK
    out = values[GRP - 1]
    for j in range(GRP - 2, -1, -1):
        out = jnp.where(hl == j, values[j], out)
    return out


def _attn_probs(q_ref, kvp_ref, kvc_ref, vec_ref, distf, valid):
    kvs = range(N_KV)
    heads = [range(hk * GRP, (hk + 1) * GRP) for hk in kvs]
    q4 = [jnp.concatenate([q_ref[:, h * HD:(h + 1) * HD] for h in heads[hk]], axis=0) for hk in kvs]
    k2 = [jnp.concatenate([kvp_ref[:, hk * HD:(hk + 1) * HD], kvc_ref[:, hk * HD:(hk + 1) * HD]], axis=0) for hk in kvs]
    v2 = [jnp.concatenate([kvp_ref[:, (N_KV + hk) * HD:(N_KV + hk + 1) * HD],
                           kvc_ref[:, (N_KV + hk) * HD:(N_KV + hk + 1) * HD]], axis=0) for hk in kvs]
    slope = [_per_head(hk, [2.0 ** (-8.0 * (h + 1) / N_HEADS) for h in heads[hk]]) for hk in kvs]
    sink = [_per_head(hk, [vec_ref[V_SINK:V_SINK + 1, h:h + 1] for h in heads[hk]]) for hk in kvs]
    sc = [_dot(q4[hk], k2[hk], _NT) for hk in kvs]
    sc = [jnp.where(valid, sc[hk] * (HD ** -0.5) - slope[hk] * distf, NEG_INF) for hk in kvs]
    m = [jnp.maximum(jnp.max(sc[hk], axis=-1, keepdims=True), sink[hk]) for hk in kvs]
    p = [jnp.exp(sc[hk] - m[hk]) for hk in kvs]
    es = [jnp.exp(sink[hk] - m[hk]) for hk in kvs]
    inv = [1.0 / (jnp.sum(p[hk], axis=-1, keepdims=True) + es[hk]) for hk in kvs]
    return [(q4[hk], k2[hk], v2[hk], p[hk] * inv[hk], es[hk] * inv[hk]) for hk in kvs]


def fwd_attn(proj, vecs, l, comm=None):
    s = proj.shape[0]
    nb = s // ATT_BLK

    def body(q_ref, kvp_ref, kvc_ref, vec_ref, o_ref):
        distf, valid = _attn_mask_bias(pl.program_id(0) == 0)
        groups = _attn_probs(q_ref, kvp_ref, kvc_ref, vec_ref, distf, valid)
        outs = [_dot(p, v2, _NN).astype(BF16) for _, _, v2, p, _ in groups]
        for hk, out in enumerate(outs):
            for j in range(GRP):
                h = hk * GRP + j
                o_ref[:, h * HD:(h + 1) * HD] = out[j * ATT_BLK:(j + 1) * ATT_BLK]

    return _call(
        body, comm, (proj, proj, proj, vecs), grid=(nb,),
        in_specs=[pl.BlockSpec((ATT_BLK, BW), lambda i: (i, C_Q // BW)),
                  pl.BlockSpec((ATT_BLK, 256), lambda i: (jnp.maximum(i - 1, 0), C_K // 256)),
                  pl.BlockSpec((ATT_BLK, 256), lambda i: (i, C_K // 256)),
                  pl.BlockSpec((None, V_ROWS, BW), lambda i: (l, 0, 0))],
        out_specs=pl.BlockSpec((ATT_BLK, BW), lambda i: (i, 0)),
        out_shape=_sds((s, BW), BF16), name=f"fwd_attn{l}")


def fwd_merge(x, proj, pre_abd, pre_c, wt_a, wt_b, wt_c, wt_d, w_o, l, comm=None):
    s = x.shape[0]
    tm = min(256, s)

    def body(x_ref, gl_ref, pabd_ref, pc_ref, wa_ref, wb_ref, wc_ref, wd_ref, wo_ref, y_ref, mg_ref, h1_ref):
        pres = (pabd_ref[:, 0:BW], pabd_ref[:, BW:2 * BW], pc_ref[...], pabd_ref[:, 2 * BW:3 * BW])
        merged = jnp.zeros((tm, D), F32)
        for k, (pre, w_ref) in enumerate(zip(pres, (wa_ref, wb_ref, wc_ref, wd_ref))):
            yk = _dot(pre, w_ref[...], _NT)
            y_ref[:, k * D:(k + 1) * D] = yk.astype(BF16)
            merged = merged + _sigmoid(gl_ref[:, k * D:(k + 1) * D].astype(F32)) * yk
        mg_ref[...] = merged.astype(BF16)
        h1_ref[...] = x_ref[...] + _dot(merged, wo_ref[...], _NN)

    wspec = pl.BlockSpec((D, BW), lambda i: (0, 0))
    return _call(
        body, comm, (x, proj, pre_abd, pre_c, wt_a, wt_b, wt_c, wt_d, w_o), grid=(s // tm,),
        in_specs=[pl.BlockSpec((tm, D), lambda i: (i, 0)),
                  pl.BlockSpec((E(tm), E(4 * D)), lambda i: (i * tm, GL0)),
                  pl.BlockSpec((tm, 3 * BW), lambda i: (i, 0)),
                  pl.BlockSpec((tm, BW), lambda i: (i, 0)),
                  wspec, wspec, wspec, wspec,
                  pl.BlockSpec((D, D), lambda i: (0, 0))],
        out_specs=[pl.BlockSpec((tm, 4 * D), lambda i: (i, 0)), pl.BlockSpec((tm, D), lambda i: (i, 0)),
                   pl.BlockSpec((tm, D), lambda i: (i, 0))],
        out_shape=[_sds((s, 4 * D), BF16), _sds((s, D), BF16), _sds((s, D), F32)], name=f"fwd_merge{l}")


def fwd_ffn(h1, g2, wt_gate, wt_up, w_down, l, comm=None):
    s = h1.shape[0]
    tm = min(512, s)
    fc = FF // 2

    def body(h_ref, g_ref, wg_ref, wu_ref, wd_ref, xo_ref, fg_ref, fu_ref, hn_ref, acc_ref):
        j = pl.program_id(1)

        @pl.when(j == 0)
        def _():
            hv = h_ref[...]
            r = lax.rsqrt(jnp.mean(hv * hv, axis=-1, keepdims=True) + EPS)
            hn_ref[...] = (hv * r * g_ref[l:l + 1, :]).astype(BF16)
            acc_ref[...] = hv

        fg = _dot(hn_ref[...], wg_ref[...], _NT)
        fu = _dot(hn_ref[...], wu_ref[...], _NT)
        fg_ref[...] = fg.astype(BF16)
        fu_ref[...] = fu.astype(BF16)
        acc_ref[...] += _dot(fg * _sigmoid(fg) * fu, wd_ref[...], _NN)

        @pl.when(j == pl.num_programs(1) - 1)
        def _():
            xo_ref[...] = acc_ref[...]

    wspec = pl.BlockSpec((fc, D), lambda i, j: (j, 0))
    return _call(
        body, comm, (h1, g2, wt_gate, wt_up, w_down), grid=(s // tm, FF // fc),
        in_specs=[pl.BlockSpec((tm, D), lambda i, j: (i, 0)), pl.BlockSpec((DEPTH, D), lambda i, j: (0, 0)),
                  wspec, wspec, wspec],
        out_specs=[pl.BlockSpec((tm, D), lambda i, j: (i, 0)), pl.BlockSpec((tm, fc), lambda i, j: (i, j)),
                   pl.BlockSpec((tm, fc), lambda i, j: (i, j))],
        out_shape=[_sds((s, D), F32), _sds((s, FF), BF16), _sds((s, FF), BF16)],
        scratch_shapes=[pltpu.VMEM((tm, D), BF16), pltpu.VMEM((tm, D), F32)], name=f"fwd_ffn{l}")


def loss_head(x, gf, target):
    s = x.shape[0]
    tm = min(512, s)

    def body(x_ref, g_ref, t_ref, dx_ref, st_ref):
        @pl.when(pl.program_id(0) == 0)
        def _():
            st_ref[...] = jnp.zeros((8, D), F32)

        xv = x_ref[...]
        g = g_ref[...]
        r = lax.rsqrt(jnp.mean(xv * xv, axis=-1, keepdims=True) + EPS)
        n = xv * r
        err = n * g - t_ref[...]
        dy = err * (1.0 / D)
        dn = dy * g
        dx_ref[...] = r * (dn - n * jnp.mean(dn * n, axis=-1, keepdims=True))
        st_ref[0:1, :] += jnp.sum(dy * n, axis=0, keepdims=True)
        lsum = 0.5 * jnp.sum(jnp.mean(err * err, axis=-1, keepdims=True), axis=0, keepdims=True)
        st_ref[1:2, :] += jnp.broadcast_to(lsum, (1, D))

    return pl.pallas_call(
        body, grid=(s // tm,),
        in_specs=[pl.BlockSpec((tm, D), lambda i: (i, 0)), pl.BlockSpec((1, D), lambda i: (0, 0)),
                  pl.BlockSpec((tm, D), lambda i: (i, 0))],
        out_specs=[pl.BlockSpec((tm, D), lambda i: (i, 0)), pl.BlockSpec((8, D), lambda i: (0, 0))],
        out_shape=[_sds((s, D), F32), _sds((8, D), F32)],
        compiler_params=_cparams(1), name="loss_head")(x, gf, target)


def _edge_index(j, i, n_j, n_i):
    return jnp.where((j == 0) | (j == n_j - 1), i, n_i - 1)


def bwd_ffn(dxo, h1, fg, fu, g2, wt_gate, wt_up, w_down, l, comm=None):
    s = h1.shape[0]
    tm = min(512, s)
    fc = 256
    n_j, n_i = FF // fc, s // tm

    def body(dxo_ref, h_ref, fg_ref, fu_ref, g_ref, wg_ref, wu_ref, wd_ref,
             dh_ref, dwg_ref, dwu_ref, dwd_ref, st_ref, dhn, dxo_b, hn_b, ag, au, ad):
        j, i = pl.program_id(0), pl.program_id(1)
        rows = pl.ds(pl.multiple_of(i * tm, tm), tm)
        g = g_ref[l:l + 1, :]

        @pl.when(j == 0)
        def _():
            hv = h_ref[...]
            r = lax.rsqrt(jnp.mean(hv * hv, axis=-1, keepdims=True) + EPS)
            hn_b[rows, :] = (hv * r * g).astype(BF16)
            dxo_b[rows, :] = dxo_ref[...].astype(BF16)
            dhn[rows, :] = jnp.zeros((tm, D), F32)

        @pl.when((j == 0) & (i == 0))
        def _():
            st_ref[...] = jnp.zeros((8, D), F32)

        @pl.when(i == 0)
        def _():
            ag[...] = jnp.zeros((fc, D), F32)
            au[...] = jnp.zeros((fc, D), F32)
            ad[...] = jnp.zeros((fc, D), F32)

        fgv = fg_ref[...].astype(F32)
        fuv = fu_ref[...].astype(F32)
        sg = _sigmoid(fgv)
        sil = fgv * sg
        dxb = dxo_b[rows, :]
        hnb = hn_b[rows, :]
        d_act = _dot(dxb, wd_ref[...], _NT)
        ad[...] += _dot(sil * fuv, dxb, _TN)
        d_fg = (d_act * fuv * (sg * (1.0 + fgv * (1.0 - sg)))).astype(BF16)
        d_fu = (d_act * sil).astype(BF16)
        ag[...] += _dot(d_fg, hnb, _TN)
        au[...] += _dot(d_fu, hnb, _TN)
        dhn[rows, :] += _dot(d_fg, wg_ref[...], _NN) + _dot(d_fu, wu_ref[...], _NN)

        @pl.when(i == n_i - 1)
        def _():
            dwg_ref[...] = ag[...].astype(BF16)
            dwu_ref[...] = au[...].astype(BF16)
            dwd_ref[...] = ad[...].astype(BF16)

        @pl.when(j == n_j - 1)
        def _():
            hv = h_ref[...]
            r = lax.rsqrt(jnp.mean(hv * hv, axis=-1, keepdims=True) + EPS)
            n = hv * r
            dv = dhn[rows, :]
            dn = dv * g
            dh_ref[...] = dxo_ref[...] + r * (dn - n * jnp.mean(dn * n, axis=-1, keepdims=True))
            st_ref[0:1, :] += jnp.sum(dv * n, axis=0, keepdims=True)

    edge = lambda j, i: (_edge_index(j, i, n_j, n_i), 0)
    wspec = pl.BlockSpec((fc, D), lambda j, i: (j, 0))
    dwspec = pl.BlockSpec((fc, D), lambda j, i: (j, 0))
    return _call(
        body, comm, (dxo, h1, fg, fu, g2, wt_gate, wt_up, w_down), grid=(n_j, n_i),
        in_specs=[pl.BlockSpec((tm, D), edge), pl.BlockSpec((tm, D), edge),
                  pl.BlockSpec((tm, fc), lambda j, i: (i, j)), pl.BlockSpec((tm, fc), lambda j, i: (i, j)),
                  pl.BlockSpec((DEPTH, D), lambda j, i: (0, 0)), wspec, wspec, wspec],
        out_specs=[pl.BlockSpec((tm, D), lambda j, i: (jnp.where(j == n_j - 1, i, 0), 0)),
                   dwspec, dwspec, dwspec, pl.BlockSpec((8, D), lambda j, i: (0, 0))],
        out_shape=[_sds((s, D), F32), _sds((FF, D), BF16), _sds((FF, D), BF16), _sds((FF, D), BF16), _sds((8, D), F32)],
        scratch_shapes=[pltpu.VMEM((s, D), F32), pltpu.VMEM((s, D), BF16), pltpu.VMEM((s, D), BF16),
                        pltpu.VMEM((fc, D), F32), pltpu.VMEM((fc, D), F32), pltpu.VMEM((fc, D), F32)],
        name=f"bwd_ffn{l}")


def bwd_merge(dh1, y4, proj, merged, pre_abd, pre_c, wt_a, wt_b, wt_c, wt_d, w_o, l, comm=None):
    s = dh1.shape[0]
    tm = min(256, s)
    n_i = s // tm

    def body(dh_ref, y_ref, gl_ref, mg_ref, pabd_ref, pc_ref, wa_ref, wb_ref, wc_ref, wd_ref, wo_ref,
             dgl_ref, dpre_ref, dwo_ref, dwa_ref, dwb_ref, dwc_ref, dwd_ref, ao, aa, ab, ac, ad):
        i = pl.program_id(0)
        accs = (aa, ab, ac, ad)

        @pl.when(i == 0)
        def _():
            ao[...] = jnp.zeros((D, D), F32)
            for acc in accs:
                acc[...] = jnp.zeros((D, BW), F32)

        dhb = dh_ref[...].astype(BF16)
        dmg = _dot(dhb, wo_ref[...], _NT)
        ao[...] += _dot(mg_ref[...], dhb, _TN)
        pres = (pabd_ref[:, 0:BW], pabd_ref[:, BW:2 * BW], pc_ref[...], pabd_ref[:, 2 * BW:3 * BW])
        for k, (pre, w_ref, acc) in enumerate(zip(pres, (wa_ref, wb_ref, wc_ref, wd_ref), accs)):
            gk = _sigmoid(gl_ref[:, k * D:(k + 1) * D].astype(F32))
            yk = y_ref[:, k * D:(k + 1) * D].astype(F32)
            dgl_ref[:, k * D:(k + 1) * D] = (dmg * yk * gk * (1.0 - gk)).astype(BF16)
            dyk = (dmg * gk).astype(BF16)
            dpre_ref[:, k * BW:(k + 1) * BW] = _dot(dyk, w_ref[...], _NN).astype(BF16)
            acc[...] += _dot(dyk, pre, _TN)

        @pl.when(i == n_i - 1)
        def _():
            dwo_ref[...] = ao[...].astype(BF16)
            for o_ref, acc in zip((dwa_ref, dwb_ref, dwc_ref, dwd_ref), accs):
                o_ref[...] = acc[...].astype(BF16)

    wspec = pl.BlockSpec((D, BW), lambda i: (0, 0))
    dwspec = pl.BlockSpec((D, BW), lambda i: (0, 0))
    return _call(
        body, comm, (dh1, y4, proj, merged, pre_abd, pre_c, wt_a, wt_b, wt_c, wt_d, w_o), grid=(n_i,),
        in_specs=[pl.BlockSpec((tm, D), lambda i: (i, 0)),
                  pl.BlockSpec((tm, 4 * D), lambda i: (i, 0)),
                  pl.BlockSpec((E(tm), E(4 * D)), lambda i: (i * tm, GL0)),
                  pl.BlockSpec((tm, D), lambda i: (i, 0)),
                  pl.BlockSpec((tm, 3 * BW), lambda i: (i, 0)),
                  pl.BlockSpec((tm, BW), lambda i: (i, 0)),
                  wspec, wspec, wspec, wspec,
                  pl.BlockSpec((D, D), lambda i: (0, 0))],
        out_specs=[pl.BlockSpec((E(tm), E(4 * D)), lambda i: (i * tm, GL0)),
                   pl.BlockSpec((tm, 4 * BW), lambda i: (i, 0)),
                   pl.BlockSpec((D, D), lambda i: (0, 0)), dwspec, dwspec, dwspec, dwspec],
        out_shape=[_sds((s, IN_W), BF16), _sds((s, 4 * BW), BF16), _sds((D, D), BF16)] + [_sds((D, BW), BF16)] * 4,
        scratch_shapes=[pltpu.VMEM((D, D), F32)] + [pltpu.VMEM((D, BW), F32)] * 4, name=f"bwd_merge{l}")


def bwd_attn(proj, dpre, vecs, l, comm=None):
    s = proj.shape[0]
    nb = s // ATT_BLK
    grp = N_HEADS // N_KV

    def body(q_ref, kvp_ref, kvc_ref, do_ref, vec_ref, dq_ref, dkc_ref, dkp_ref, st_ref):
        @pl.when(pl.program_id(0) == 0)
        def _():
            st_ref[...] = jnp.zeros((8, 128), F32)

        distf, valid = _attn_mask_bias(pl.program_id(0) == 0)
        lane = lax.broadcasted_iota(jnp.int32, (1, 128), 1)
        dsink = jnp.zeros((1, 128), F32)
        groups = _attn_probs(q_ref, kvp_ref, kvc_ref, vec_ref, distf, valid)
        kvs = range(N_KV)
        do4s = [jnp.concatenate([do_ref[:, h * HD:(h + 1) * HD] for h in range(hk * grp, (hk + 1) * grp)], axis=0) for hk in kvs]
        dps = [_dot(do4s[hk], groups[hk][2], _NT) for hk in kvs]
        deltas = [jnp.sum(groups[hk][3] * dps[hk], axis=-1, keepdims=True) for hk in kvs]
        dss = [groups[hk][3] * (dps[hk] - deltas[hk]) * (HD ** -0.5) for hk in kvs]
        for hk in kvs:
            q4, k2, v2, p, ps = groups[hk]
            do4, delta, ds = do4s[hk], deltas[hk], dss[hk]
            dq4 = _dot(ds, k2, _NN).astype(BF16)
            dk2 = _dot(ds, q4, _TN)
            dv2 = _dot(p, do4, _TN)
            psd = ps * delta
            for j in range(grp):
                h = hk * grp + j
                rows = slice(j * ATT_BLK, (j + 1) * ATT_BLK)
                dq_ref[:, h * HD:(h + 1) * HD] = dq4[rows]
                dsink = dsink + jnp.where(lane == h, -jnp.sum(psd[rows], axis=0, keepdims=True), 0.0)
            dkp_ref[:, hk * HD:(hk + 1) * HD] = dk2[0:ATT_BLK].astype(BF16)
            dkc_ref[:, hk * HD:(hk + 1) * HD] = dk2[ATT_BLK:].astype(BF16)
            dkp_ref[:, (N_KV + hk) * HD:(N_KV + hk + 1) * HD] = dv2[0:ATT_BLK].astype(BF16)
            dkc_ref[:, (N_KV + hk) * HD:(N_KV + hk + 1) * HD] = dv2[ATT_BLK:].astype(BF16)
        st_ref[0:1, :] += dsink

    return _call(
        body, comm, (proj, proj, proj, dpre, vecs), grid=(nb,),
        in_specs=[pl.BlockSpec((ATT_BLK, BW), lambda i: (i, C_Q // BW)),
                  pl.BlockSpec((ATT_BLK, 256), lambda i: (jnp.maximum(i - 1, 0), C_K // 256)),
                  pl.BlockSpec((ATT_BLK, 256), lambda i: (i, C_K // 256)),
                  pl.BlockSpec((ATT_BLK, BW), lambda i: (i, 2)),
                  pl.BlockSpec((None, V_ROWS, BW), lambda i: (l, 0, 0))],
        out_specs=[pl.BlockSpec((ATT_BLK, BW), lambda i: (i, 0)), pl.BlockSpec((ATT_BLK, 256), lambda i: (i, 0)),
                   pl.BlockSpec((ATT_BLK, 256), lambda i: (i, 0)), pl.BlockSpec((8, 128), lambda i: (0, 0))],
        out_shape=[_sds((s, BW), BF16), _sds((s, 256), BF16), _sds((s, 256), BF16), _sds((8, 128), F32)],
        name=f"bwd_attn{l}")


def bwd_branch(proj, dproj, dpre, h, dq, dkc, dkp, convw, vecs, wx_bd, wa_bd, l, comm=None):
    s = proj.shape[0]
    t = 2 * ATT_BLK
    nt = s // t
    nb = s // ATT_BLK
    hb = t // HALO

    def body(cur_ref, halo_ref, dpre_ref, h_ref, hp_ref, dq_ref, dkc_ref, dkp1_ref, dkp2_ref,
             cw_ref, vec_ref, wx_ref, wa_ref, dproj_in, dp_ref, dcw_ref, dvec_ref, dwx_ref, dwa_ref,
             bufa, bufb, bufd, xd, xg, a_ext, hbuf, b_s, g_s, dh_s, ga, gb, gd, dhcar):
        del dproj_in
        step = pl.program_id(0)
        ti = nt - 1 - step
        first = ti == 0

        @pl.when(step == 0)
        def _():
            dcw_ref[...] = jnp.zeros((CW_ROWS, BW), F32)
            dvec_ref[...] = jnp.zeros((V_ROWS, BW), F32)
            dwx_ref[...] = jnp.zeros((BW, BW), F32)
            dwa_ref[...] = jnp.zeros((BW, BW), F32)
            dhcar[...] = jnp.zeros((1, BW), F32)
            a_ext[t:t + 8, :] = jnp.zeros((8, BW), F32)
            ga[t:t + 8, :] = jnp.zeros((8, BW), F32)
            gb[t:t + 8, :] = jnp.zeros((8, BW), F32)
            gd[t:t + HALO, :] = jnp.zeros((HALO, BW), F32)

        def cur(c0):
            return cur_ref[:, c0:c0 + BW].astype(F32)

        def rsum(v):
            return jnp.sum(v, axis=0, keepdims=True)

        def put(c0, v):
            dp_ref[:, c0:c0 + BW] = v.astype(BF16)

        v = _branch_fwd_math(cur_ref, halo_ref, cw_ref, vec_ref, wx_ref, wa_ref, bufa, bufb, bufd, xd, first, t)
        ca, gi, gr, sp, a, mult = v["ca"], v["gi"], v["gr"], v["sp"], v["a"], v["mult"]
        dpa = dpre_ref[:, 0:BW].astype(F32)
        gg, dgg = _gelu_and_grad(cur(C_AG))
        hv = h_ref[...]
        put(C_AG, dpa * hv * dgg)
        a_ext[0:t, :] = a
        b_s[...] = a_ext[pl.ds(1, t), :]
        g_s[...] = dpa * gg
        dhcar[...] = _scan_bwd(b_s, g_s, dh_s, dhcar[...], t)
        a_ext[t:t + 1, :] = a[0:1, :]
        dh = dh_s[...]
        hbuf[0:8, :] = jnp.where(first, 0.0, hp_ref[...])
        hbuf[8:8 + t, :] = hv
        da = dh * hbuf[pl.ds(7, t), :]
        d_ca = dh * gi * mult
        d_gi = dh * ca * mult
        d_mult = dh * ca * gi
        d_la = da * a - d_mult * (a * a) / mult
        lam = vec_ref[V_LAM:V_LAM + 1, :]
        dvec_ref[V_LAM:V_LAM + 1, :] += rsum(d_la * gr) * (LRU_C * _sigmoid(-lam))
        d_gr = d_la * (-LRU_C * sp)
        d_zr = d_gr * gr * (1.0 - gr)
        d_zi = d_gi * gi * (1.0 - gi)
        dvec_ref[V_BA:V_BA + 1, :] += rsum(d_zr)
        dvec_ref[V_BX:V_BX + 1, :] += rsum(d_zi)
        dwa_ref[...] += _dot(ca, d_zr, _TN)
        dwx_ref[...] += _dot(ca, d_zi, _TN)
        d_ca = d_ca + _dot(d_zi, wx_ref[...], _NT) + _dot(d_zr, wa_ref[...], _NT)
        dvec_ref[V_CAB:V_CAB + 1, :] += rsum(d_ca)
        ga[0:t, :] = d_ca
        d_ax = jnp.zeros((t, BW), F32)
        for k in range(CONV_A):
            d_ax = d_ax + cw_ref[CW_A + k:CW_A + k + 1, :] * ga[pl.ds(CONV_A - 1 - k, t), :]
            dcw_ref[CW_A + k:CW_A + k + 1, :] += rsum(d_ca * bufa[pl.ds(HALO - (CONV_A - 1) + k, t), :])
        ga[t:t + 8, :] = d_ca[0:8, :]
        put(C_AX, d_ax)
        dpb = dpre_ref[:, BW:2 * BW].astype(F32)
        put(C_BB, dpb * v["cb"])
        d_cb = dpb * cur(C_BB)
        gb[0:t, :] = d_cb
        d_cbin = jnp.zeros((t, BW), F32)
        for k in range(CONV_B):
            d_cbin = d_cbin + cw_ref[CW_B + k:CW_B + k + 1, :] * gb[pl.ds(CONV_B - 1 - k, t), :]
            dcw_ref[CW_B + k:CW_B + k + 1, :] += rsum(d_cb * bufb[pl.ds(HALO - (CONV_B - 1) + k, t), :])
        gb[t:t + 8, :] = d_cb[0:8, :]
        put(C_BC, d_cbin * cur(C_BV))
        put(C_BV, d_cbin * cur(C_BC))
        dpd = dpre_ref[:, 3 * BW:4 * BW].astype(F32)
        ln, xh, rstd, s2 = v["ln"], v["xh"], v["rstd"], v["s2"]
        sg = _sigmoid(ln)
        d_ln = dpd * sg * (1.0 + ln * (1.0 - sg))
        dvec_ref[V_LNG:V_LNG + 1, :] += rsum(d_ln * xh)
        dvec_ref[V_LNB:V_LNB + 1, :] += rsum(d_ln)
        d_xh = d_ln * vec_ref[V_LNG:V_LNG + 1, :]
        d_cd = rstd * (d_xh - jnp.mean(d_xh, axis=-1, keepdims=True)
                       - xh * jnp.mean(d_xh * xh, axis=-1, keepdims=True))
        dvec_ref[V_CDB:V_CDB + 1, :] += rsum(d_cd)
        gd[0:t, :] = d_cd
        _shifted_copies(gd, xg, t + HALO)
        d_dg = jnp.zeros((t, BW), F32)
        for k in range(CONV_D):
            d_dg = d_dg + cw_ref[CW_D + k:CW_D + k + 1, :] * _window(gd, xg, CONV_D - 1 - k, t)
            dcw_ref[CW_D + k:CW_D + k + 1, :] += rsum(d_cd * _window(bufd, xd, HALO - (CONV_D - 1) + k, t))
        gd[t:t + HALO, :] = d_cd[0:HALO, :]
        put(C_D1, d_dg * s2)
        put(C_D2, d_dg * cur(C_D1) * s2 * (1.0 - s2))
        dp_ref[:, C_Q:C_Q + BW] = dq_ref[...]
        dkp2 = jnp.where(step == 0, 0.0, dkp2_ref[...].astype(F32))
        dp_ref[0:ATT_BLK, C_K:C_K + 256] = (dkc_ref[0:ATT_BLK, :].astype(F32) + dkp1_ref[...].astype(F32)).astype(BF16)
        dp_ref[ATT_BLK:t, C_K:C_K + 256] = (dkc_ref[ATT_BLK:t, :].astype(F32) + dkp2).astype(BF16)

    rev = lambda i: nt - 1 - i
    full = lambda r, c: pl.BlockSpec((r, c), lambda i: (0, 0))
    return _call(
        body, comm, (proj, proj, dpre, h, h, dq, dkc, dkp, dkp, convw, vecs, wx_bd, wa_bd, dproj), grid=(nt,),
        in_specs=[pl.BlockSpec((t, GL0), lambda i: (rev(i), 0)),
                  pl.BlockSpec((HALO, GL0), lambda i: (jnp.maximum(rev(i) * hb - 1, 0), 0)),
                  pl.BlockSpec((t, 4 * BW), lambda i: (rev(i), 0)),
                  pl.BlockSpec((t, BW), lambda i: (rev(i), 0)),
                  pl.BlockSpec((8, BW), lambda i: (jnp.maximum(rev(i) * (t // 8) - 1, 0), 0)),
                  pl.BlockSpec((t, BW), lambda i: (rev(i), 0)),
                  pl.BlockSpec((t, 256), lambda i: (rev(i), 0)),
                  pl.BlockSpec((ATT_BLK, 256), lambda i: (2 * rev(i) + 1, 0)),
                  pl.BlockSpec((ATT_BLK, 256), lambda i: (jnp.minimum(2 * rev(i) + 2, nb - 1), 0)),
                  pl.BlockSpec((None, CW_ROWS, BW), lambda i: (l, 0, 0)),
                  pl.BlockSpec((None, V_ROWS, BW), lambda i: (l, 0, 0)),
                  pl.BlockSpec((None, BW, BW), lambda i: (l, 0, 0)),
                  pl.BlockSpec((None, BW, BW), lambda i: (l, 0, 0)),
                  pl.BlockSpec(memory_space=pl.ANY)],
        out_specs=[pl.BlockSpec((t, GL0), lambda i: (rev(i), 0)),
                   full(CW_ROWS, BW), full(V_ROWS, BW), full(BW, BW), full(BW, BW)],
        out_shape=[_sds((s, IN_W), BF16), _sds((CW_ROWS, BW), F32), _sds((V_ROWS, BW), F32),
                   _sds((BW, BW), F32), _sds((BW, BW), F32)],
        scratch_shapes=[pltpu.VMEM((t + HALO, BW), F32)] * 3 + [pltpu.VMEM((7, t + HALO - 8, BW), F32)] * 2
        + [pltpu.VMEM((t + 8, BW), F32), pltpu.VMEM((t + 8, BW), F32)]
        + [pltpu.VMEM((t, BW), F32)] * 3
        + [pltpu.VMEM((t + 8, BW), F32), pltpu.VMEM((t + 8, BW), F32), pltpu.VMEM((t + HALO, BW), F32),
           pltpu.VMEM((1, BW), F32)],
        aliases={13: 0}, name=f"bwd_branch{l}")


def bwd_proj(dproj, x, dh1, g1, wt_in, l, comm=None):
    s = x.shape[0]
    tm = min(512, s)
    ck = 1408
    n_j, n_i = IN_W // ck, s // tm

    def body(dp_ref, x_ref, dh_ref, g_ref, w_ref, dx_ref, dw_ref, st_ref, dxn, xn_b, acc):
        j, i = pl.program_id(0), pl.program_id(1)
        rows = pl.ds(pl.multiple_of(i * tm, tm), tm)
        g = g_ref[l:l + 1, :]

        @pl.when(j == 0)
        def _():
            xv = x_ref[...]
            r = lax.rsqrt(jnp.mean(xv * xv, axis=-1, keepdims=True) + EPS)
            xn_b[rows, :] = (xv * r * g).astype(BF16)
            dxn[rows, :] = jnp.zeros((tm, D), F32)

        @pl.when((j == 0) & (i == 0))
        def _():
            st_ref[...] = jnp.zeros((8, D), F32)

        @pl.when(i == 0)
        def _():
            acc[...] = jnp.zeros((ck, D), F32)

        dp = dp_ref[...]
        dxn[rows, :] += _dot(dp, w_ref[...], _NN)
        acc[...] += _dot(dp, xn_b[rows, :], _TN)

        @pl.when(i == n_i - 1)
        def _():
            dw_ref[...] = acc[...].astype(BF16)

        @pl.when(j == n_j - 1)
        def _():
            xv = x_ref[...]
            r = lax.rsqrt(jnp.mean(xv * xv, axis=-1, keepdims=True) + EPS)
            n = xv * r
            dv = dxn[rows, :]
            dn = dv * g
            dx_ref[...] = dh_ref[...] + r * (dn - n * jnp.mean(dn * n, axis=-1, keepdims=True))
            st_ref[0:1, :] += jnp.sum(dv * n, axis=0, keepdims=True)

    lastrow = lambda j, i: (jnp.where(j == n_j - 1, i, 0), 0)
    return _call(
        body, comm, (dproj, x, dh1, g1, wt_in), grid=(n_j, n_i),
        in_specs=[pl.BlockSpec((tm, ck), lambda j, i: (i, j)),
                  pl.BlockSpec((tm, D), lambda j, i: (_edge_index(j, i, n_j, n_i), 0)),
                  pl.BlockSpec((tm, D), lastrow),
                  pl.BlockSpec((DEPTH, D), lambda j, i: (0, 0)),
                  pl.BlockSpec((ck, D), lambda j, i: (j, 0))],
        out_specs=[pl.BlockSpec((tm, D), lastrow), pl.BlockSpec((ck, D), lambda j, i: (j, 0)),
                   pl.BlockSpec((8, D), lambda j, i: (0, 0))],
        out_shape=[_sds((s, D), F32), _sds((IN_W, D), BF16), _sds((8, D), F32)],
        scratch_shapes=[pltpu.VMEM((s, D), F32), pltpu.VMEM((s, D), BF16), pltpu.VMEM((ck, D), F32)],
        name=f"bwd_proj{l}")


def _block_diag(w):
    nl, nb, bw, _ = w.shape
    eye = jnp.eye(nb, dtype=w.dtype)
    return jnp.einsum("lhij,hk->lhikj", w, eye).reshape(nl, nb * bw, nb * bw).astype(BF16)


def _diag_blocks(m):
    return jnp.stack([m[HD * h:HD * (h + 1), HD * h:HD * (h + 1)] for h in range(BW // HD)])


class NoOverlap:
    def __init__(self, big):
        self.big = big

    def weights(self, l):
        return self.big[l]

    def job(self, slot, l):
        return None

    def done(self, slot, l, results):
        pass

    def new_grads(self, group, l, grads):
        pass


def local_step(x, target, norm1_g, norm2_g, final_g, convw, vecs, lru_wx, lru_wa, plan):
    wx_bd, wa_bd = _block_diag(lru_wx), _block_diag(lru_wa)

    def run(fn, slot, l, *args):
        res, cres = fn(*args, l, comm=plan.job(slot, l))
        plan.done(slot, l, cres)
        return res

    saved = []
    for l in range(DEPTH):
        proj = run(fwd_proj, "fwd_proj", l, x, norm1_g, plan.weights(l)["in_t"])
        pre_abd, h = run(fwd_branch, "fwd_branch", l, proj, convw, vecs, wx_bd, wa_bd)
        pre_c = run(fwd_attn, "fwd_attn", l, proj, vecs)
        w = plan.weights(l)
        y4, merged, h1 = run(fwd_merge, "fwd_merge", l, x, proj, pre_abd, pre_c, w["a_t"], w["b_t"], w["c_t"], w["d_t"], w["o"])
        w = plan.weights(l)
        x_out, fg, fu = run(fwd_ffn, "fwd_ffn", l, h1, norm2_g, w["gate_t"], w["up_t"], w["down"])
        saved.append((x, proj, pre_abd, h, pre_c, y4, merged, h1, fg, fu))
        x = x_out
    dx, head_stats = loss_head(x, final_g.reshape(1, D), target)
    small = [None] * DEPTH
    for l in reversed(range(DEPTH)):
        x_in, proj, pre_abd, h, pre_c, y4, merged, h1, fg, fu = saved[l]
        w = plan.weights(l)
        dh1, d_gate, d_up, d_down, st_ffn = run(bwd_ffn, "bwd_ffn", l, dx, h1, fg, fu, norm2_g, w["gate_t"], w["up_t"], w["down"])
        plan.new_grads("ffn", l, dict(gate_t=d_gate, up_t=d_up, down=d_down))
        dproj, dpre, d_o, d_a, d_b, d_c, d_d = run(
            bwd_merge, "bwd_merge", l, dh1, y4, proj, merged, pre_abd, pre_c, w["a_t"], w["b_t"], w["c_t"], w["d_t"], w["o"])
        plan.new_grads("out", l, dict(a_t=d_a, b_t=d_b, c_t=d_c, d_t=d_d, o=d_o))
        dq, dkc, dkp, st_attn = run(bwd_attn, "bwd_attn", l, proj, dpre, vecs)
        dproj, dcw, dvec, dwx, dwa = run(bwd_branch, "bwd_branch", l, proj, dproj, dpre, h, dq, dkc, dkp, convw, vecs, wx_bd, wa_bd)
        dx, d_in, st_proj = run(bwd_proj, "bwd_proj", l, dproj, x_in, dh1, norm1_g, w["in_t"])
        plan.new_grads("in", l, dict(in_t=d_in))
        small[l] = dict(norm1_g=st_proj[0], norm2_g=st_ffn[0], convw=dcw, vecs=dvec, sinks=st_attn[0, :N_HEADS],
                        lru_wx=_diag_blocks(dwx), lru_wa=_diag_blocks(dwa))
    return head_stats, dx, small


BIG = dict(in_t=("w_in", "view"), a_t=("w_a_out", "transpose"), b_t=("w_b_out", "transpose"), c_t=("w_c_out", "transpose"),
           d_t=("w_d_out", "transpose"), o=("w_o", "plain"), gate_t=("w_ffn_gate", "view"), up_t=("w_ffn_up", "view"),
           down=("w_ffn_down", "plain"))


def cast_transpose(w, name):
    nl, a, b = w.shape
    ta = min(256, a)

    def body(w_ref, o_ref):
        o_ref[...] = w_ref[...].T.astype(BF16)

    return pl.pallas_call(
        body, grid=(nl, a // ta),
        in_specs=[pl.BlockSpec((None, ta, b), lambda l, i: (l, i, 0))],
        out_specs=pl.BlockSpec((None, b, ta), lambda l, i: (l, 0, i)),
        out_shape=_sds((nl, b, a), BF16), compiler_params=_cparams(2), name=name)(w)


def add_partials(mine, recv, core, name):
    n = len(mine)

    def body(core_ref, *refs):
        del core_ref
        for a_ref, b_ref, o_ref in zip(refs[:n], refs[n:2 * n], refs[2 * n:]):
            o_ref[...] = (a_ref[...].astype(F32) + b_ref[...].astype(F32)).astype(BF16)

    return pl.pallas_call(
        body,
        grid_spec=pltpu.PrefetchScalarGridSpec(
            num_scalar_prefetch=1, grid=(4,),
            in_specs=[pl.BlockSpec((None, None) + a.shape[2:], lambda i, cr: (i, cr[0], 0, 0)) for a in mine]
            + [pl.BlockSpec((None,) + b.shape[1:], lambda i, cr: (i, 0, 0)) for b in recv],
            out_specs=[pl.BlockSpec((None,) + b.shape[1:], lambda i, cr: (i, 0, 0)) for b in recv]),
        out_shape=[_sds(b.shape, BF16) for b in recv], compiler_params=_cparams(1), name=name)(core, *mine, *recv)


def _adamw(w, g, m, v):
    m = ADAM_B1 * m + (1.0 - ADAM_B1) * g
    v = ADAM_B2 * v + (1.0 - ADAM_B2) * (g * g)
    m_hat = m / (1.0 - ADAM_B1 ** ADAM_STEP)
    v_hat = v / (1.0 - ADAM_B2 ** ADAM_STEP)
    delta = -ADAM_LR * (m_hat / (jnp.sqrt(v_hat) + ADAM_EPS) + ADAM_WD * w)
    return delta, m, v


def adamw_big(contrib, w, m, v, transposed, name, comm=None):
    nsrc, nl, rows, cols = contrib.shape
    ct = 256

    def body(c_ref, w_ref, m_ref, v_ref, g_out, d_out, m_out, v_out):
        g = c_ref[0].astype(F32)
        for src in range(1, nsrc):
            g = g + c_ref[src].astype(F32)
        if transposed:
            g = g.T
        delta, mn, vn = _adamw(w_ref[...], g, m_ref[...], v_ref[...])
        g_out[...] = g
        d_out[...] = delta
        m_out[...] = mn
        v_out[...] = vn

    if transposed:
        wspec = pl.BlockSpec((None, ct, rows), lambda l, j: (l, j, 0))
    else:
        wspec = pl.BlockSpec((None, rows, ct), lambda l, j: (l, 0, j))
    return _call(
        body, comm, (contrib, w, m, v), grid=(nl, cols // ct),
        in_specs=[pl.BlockSpec((nsrc, None, rows, ct), lambda l, j: (0, l, 0, j)), wspec, wspec, wspec],
        out_specs=[wspec] * 4, out_shape=[_sds(w.shape, F32)] * 4, name=name)


def adamw_small(gathered, w, m, v):
    r = w.shape[0]

    def body(c_ref, w_ref, m_ref, v_ref, g_out, d_out, m_out, v_out):
        g = c_ref[0]
        for dev in range(1, NDEV):
            g = g + c_ref[dev]
        delta, mn, vn = _adamw(w_ref[...], g, m_ref[...], v_ref[...])
        g_out[...] = g
        d_out[...] = delta
        m_out[...] = mn
        v_out[...] = vn

    return pl.pallas_call(body, out_shape=[_sds((r, D), F32)] * 4, name="adamw_small",
                          compiler_params=pltpu.CompilerParams(vmem_limit_bytes=VMEM_LIMIT))(gathered, w, m, v)


VEC_NAMES = ("conv_a_b", "lru_bx", "lru_ba", "lru_lambda", "conv_d_b", "ln_d_g", "ln_d_b")
P_N1, P_N2, P_VEC, P_CONV, P_WX, P_WA, P_LAYER = 0, 1, 2, 6, 26, 58, 90
P_FINAL, P_LOSS, P_ROWS = DEPTH * P_LAYER, DEPTH * P_LAYER + 1, 184


def _stack_vecs(p):
    rows = [p[n] for n in VEC_NAMES] + [jnp.pad(p["sinks"], ((0, 0), (0, BW - N_HEADS)))]
    return jnp.stack(rows, axis=1)


def _stack_convs(p):
    nl, _, ch = p["conv_a_w"].shape
    z = jnp.zeros((nl, 1, ch), F32)
    return jnp.concatenate([p["conv_a_w"], p["conv_b_w"], z, p["conv_d_w"], z], axis=1)


def _pack_small(p, convw, extra_rows):
    vec = _stack_vecs(p)
    rows = []
    for l in range(DEPTH):
        rows += [p["norm1_g"][l].reshape(1, D), p["norm2_g"][l].reshape(1, D), vec[l].reshape(-1, D),
                 convw[l].reshape(-1, D), p["lru_wx"][l].reshape(-1, D), p["lru_wa"][l].reshape(-1, D)]
    rows += [p["final_g"].reshape(1, D)] + extra_rows
    pack = jnp.concatenate(rows, axis=0)
    return jnp.pad(pack, ((0, P_ROWS - pack.shape[0]), (0, 0)))


def _unpack_small(pack, ch0):
    per_layer = [pack[l * P_LAYER:(l + 1) * P_LAYER] for l in range(DEPTH)]
    out = dict(norm1_g=jnp.stack([q[P_N1] for q in per_layer]), norm2_g=jnp.stack([q[P_N2] for q in per_layer]),
               final_g=pack[P_FINAL])
    vec = jnp.stack([q[P_VEC:P_CONV].reshape(V_ROWS, BW) for q in per_layer])
    for i, n in enumerate(VEC_NAMES):
        out[n] = vec[:, i]
    out["sinks"] = vec[:, V_SINK, :N_HEADS]
    convw = jnp.stack([q[P_CONV:P_WX].reshape(CW_ROWS, BW) for q in per_layer])
    mine = lax.dynamic_slice_in_dim(convw, ch0, BW // NDEV, axis=2)
    out.update(conv_a_w=mine[:, CW_A:CW_A + CONV_A], conv_b_w=mine[:, CW_B:CW_B + CONV_B],
               conv_d_w=mine[:, CW_D:CW_D + CONV_D])
    out["lru_wx"] = jnp.stack([q[P_WX:P_WA].reshape(BW // HD, HD, HD) for q in per_layer])
    out["lru_wa"] = jnp.stack([q[P_WA:P_LAYER].reshape(BW // HD, HD, HD) for q in per_layer])
    return out


def merge_jobs(jobs):
    jobs = [j for j in jobs if j is not None]
    if not jobs:
        return None, []
    inputs, aliases, outs, sems, cuts = [], {}, [], [], []
    for j in jobs:
        i0, o0, s0 = len(inputs), len(outs), len(sems)
        aliases.update({i0 + i: o0 + o for i, o in j.aliases.items()})
        inputs += j.inputs
        outs += j.out_shapes
        sems += j.sem_shapes
        cuts.append((i0, len(inputs), o0, len(outs), s0, len(sems)))

    def each(which):
        def go(cins, couts, s):
            for j, (i0, i1, o0, o1, s0, s1) in zip(jobs, cuts):
                getattr(j, which)(cins[i0:i1], couts[o0:o1], s[s0:s1])
        return go

    return CommJob(inputs, aliases, outs, sems, each("start"), each("finish")), [(c[2], c[3]) for c in cuts]


SIXTHS = 6
OUT_KINDS = ("a_t", "b_t", "c_t", "d_t", "o")
GATHER_PLAN = {
    "fwd_proj": [(k, 0, 0, 6) for k in OUT_KINDS] + [("gate_t", 0, 0, 3)],
    "fwd_branch": [("gate_t", 0, 3, 6), ("up_t", 0, 0, 3)],
    "fwd_attn": [("up_t", 0, 3, 6), ("down", 0, 0, 6)],
    "fwd_merge": [("in_t", 1, 0, 2)],
    "fwd_ffn": [("in_t", 1, 2, 6)],
}
SIBLING_PLAN = {"bwd_merge": ("ffn", 0), "bwd_branch": ("out", 0), "bwd_ffn": ("in", 1), "adamw_down": ("in", 0)}
GROUPS = dict(ffn=("gate_t", "up_t", "down"), out=OUT_KINDS)
GROUPS["in"] = ("in_t",)
CHIP_PLAN = {
    "bwd_attn": [("in_t", 1, 3, 6), ("gate_t", 0, 0, 3)],
    "bwd_branch": [("gate_t", 0, 3, 6), ("up_t", 0, 0, 6), ("down", 0, 0, 6)],
    "bwd_proj": [(k, 0, 0, 6) for k in OUT_KINDS],
    "bwd_merge": [("in_t", 1, 0, 3)],
    "adamw_gate_t": [("in_t", 0, 0, 2)], "adamw_up_t": [("in_t", 0, 2, 4)], "adamw_o": [("in_t", 0, 4, 6)],
}
SMALL_GATHER_SLOT = "adamw_gate_t"


class Overlap:
    def __init__(self, shards, core):
        self.shards = shards
        self.core = core
        self.gathered = [dict.fromkeys(BIG) for _ in range(DEPTH)]
        self.views = {}
        self.partial = {}
        self.contrib = dict.fromkeys(BIG)
        self.small_pack = self.small_gathered = None
        self._open = None

    def weights(self, l):
        return self.gathered[l]

    def new_grads(self, group, l, grads):
        for k, g in grads.items():
            self.views[k, l] = g.reshape(4, 2, g.shape[0] // NDEV, g.shape[1])

    @staticmethod
    def _rows(shard_rows, f0, f1):
        return shard_rows * f0 // SIXTHS, shard_rows * (f1 - f0) // SIXTHS

    def job(self, slot, l):
        jobs, notes = [], []
        pieces = [(k, l + dl, f0, f1) for k, dl, f0, f1 in GATHER_PLAN.get(slot, []) if l + dl < DEPTH]
        if pieces:
            jobs.append(gather_job([((k, ll), self.shards[ll][k], self.gathered[ll][k],
                                     *self._rows(self.shards[ll][k].shape[0], f0, f1)) for k, ll, f0, f1 in pieces]))
            notes.append(("gather", list(dict.fromkeys((k, ll) for k, ll, _, _ in pieces))))
        if slot in SIBLING_PLAN and l + SIBLING_PLAN[slot][1] < DEPTH:
            group, dl = SIBLING_PLAN[slot]
            keys = [(k, l + dl) for k in GROUPS[group]]
            jobs.append(sibling_exchange_job([self.views[key] for key in keys]))
            notes.append(("sibling", keys))
        pieces = [(k, l + dl, f0, f1) for k, dl, f0, f1 in CHIP_PLAN.get(slot, []) if l + dl < DEPTH]
        if pieces:
            jobs.append(chip_exchange_job([(self.partial[k, ll], self.contrib[k], k, ll,
                                            *self._rows(self.partial[k, ll].shape[1], f0, f1)) for k, ll, f0, f1 in pieces]))
            notes.append(("chips", list(dict.fromkeys(k for k, _, _, _ in pieces))))
        if slot == SMALL_GATHER_SLOT:
            jobs.append(gather_job([("small", self.small_pack, None, 0, self.small_pack.shape[0])]))
            notes.append(("small", None))
        job, spans = merge_jobs(jobs)
        self._open = (slot, l, notes, spans)
        return job

    def done(self, slot, l, results):
        open_slot, open_l, notes, spans = self._open
        assert (open_slot, open_l) == (slot, l)
        for (what, keys), (r0, r1) in zip(notes, spans):
            res = results[r0:r1]
            if what == "gather":
                for (k, ll), g in zip(keys, res):
                    self.gathered[ll][k] = g
            elif what == "sibling":
                sums = add_partials([self.views[key] for key in keys], list(res), self.core, f"chip_sum_{keys[0][0]}{keys[0][1]}")
                self.partial.update(zip(keys, sums))
            elif what == "chips":
                for k, c in zip(keys, res):
                    self.contrib[k] = c
            else:
                self.small_gathered, = res


SMALL = ("norm1_g", "conv_a_w", "conv_a_b", "lru_wx", "lru_bx", "lru_wa", "lru_ba", "lru_lambda", "conv_b_w", "sinks",
         "conv_d_w", "conv_d_b", "ln_d_g", "ln_d_b", "norm2_g", "final_g")
WEIGHTS = ("norm1_g", "w_in", "conv_a_w", "conv_a_b", "lru_wx", "lru_bx", "lru_wa", "lru_ba", "lru_lambda", "w_a_out",
           "conv_b_w", "w_b_out", "sinks", "w_c_out", "conv_d_w", "conv_d_b", "ln_d_g", "ln_d_b", "w_d_out", "w_o",
           "norm2_g", "w_ffn_gate", "w_ffn_up", "w_ffn_down", "final_g")


def kernel(x, norm1_g, w_in, conv_a_w, conv_a_b, lru_wx, lru_bx, lru_wa, lru_ba, lru_lambda, w_a_out, conv_b_w, w_b_out, sinks, w_c_out, conv_d_w, conv_d_b, ln_d_g, ln_d_b, w_d_out, w_o, norm2_g, w_ffn_gate, w_ffn_up, w_ffn_down, final_g, loss_target, m_norm1_g, m_w_in, m_conv_a_w, m_conv_a_b, m_lru_wx, m_lru_bx, m_lru_wa, m_lru_ba, m_lru_lambda, m_w_a_out, m_conv_b_w, m_w_b_out, m_sinks, m_w_c_out, m_conv_d_w, m_conv_d_b, m_ln_d_g, m_ln_d_b, m_w_d_out, m_w_o, m_norm2_g, m_w_ffn_gate, m_w_ffn_up, m_w_ffn_down, m_final_g, v_norm1_g, v_w_in, v_conv_a_w, v_conv_a_b, v_lru_wx, v_lru_bx, v_lru_wa, v_lru_ba, v_lru_lambda, v_w_a_out, v_conv_b_w, v_w_b_out, v_sinks, v_w_c_out, v_conv_d_w, v_conv_d_b, v_ln_d_g, v_ln_d_b, v_w_d_out, v_w_o, v_norm2_g, v_w_ffn_gate, v_w_ffn_up, v_w_ffn_down, v_final_g):
    args = dict(locals())
    w = {n: args[n] for n in WEIGHTS}
    m = {n: args["m_" + n] for n in WEIGHTS}
    v = {n: args["v_" + n] for n in WEIGHTS}
    me = _dev_index(*_mesh_pos())
    ch0 = me * (BW // NDEV)

    def rows_major(a, how):
        return jnp.swapaxes(a, 1, 2) if how == "view" else a

    stacked = {k: cast_transpose(w[n], "prep_" + k) if how == "transpose" else rows_major(w[n], how).astype(BF16)
               for k, (n, how) in BIG.items()}
    plan = Overlap([{k: stacked[k][l] for k in BIG} for l in range(DEPTH)], lax.axis_index("c").astype(jnp.int32).reshape(1))
    convs = _stack_convs(w).reshape(DEPTH * CW_ROWS, BW // NDEV)
    g_in0, g_conv = _comm_only(gather_job([(("in_t", 0), plan.shards[0]["in_t"], None, 0, plan.shards[0]["in_t"].shape[0]),
                                           ("convs", convs, None, 0, convs.shape[0])]), "gather_first")
    plan.gathered[0]["in_t"] = g_in0
    convw = g_conv.reshape(NDEV, DEPTH, CW_ROWS, BW // NDEV).transpose(1, 2, 0, 3).reshape(DEPTH, CW_ROWS, BW)

    vecs = _stack_vecs(w)
    head_stats, grad_x, grads = local_step(x[0], loss_target[0], norm1_g, norm2_g, final_g, convw, vecs, lru_wx, lru_wa, plan)

    gsmall = {n: jnp.stack([grads[l][n] for l in range(DEPTH)]) for n in ("norm1_g", "norm2_g", "lru_wx", "lru_wa", "sinks")}
    gvec = jnp.stack([grads[l]["vecs"] for l in range(DEPTH)])
    gsmall.update({n: gvec[:, i] for i, n in enumerate(VEC_NAMES)})
    gsmall["final_g"] = head_stats[0]
    gpack = _pack_small(gsmall, jnp.stack([grads[l]["convw"] for l in range(DEPTH)]), [head_stats[1:2]])
    plan.small_pack = gpack

    out = {}
    for k in ("down", "gate_t", "up_t", "o", "a_t", "b_t", "c_t", "d_t", "in_t"):
        n, how = BIG[k]
        res, cres = adamw_big(plan.contrib[k], rows_major(w[n], how), rows_major(m[n], how), rows_major(v[n], how),
                              how == "transpose", "adamw_" + k, comm=plan.job("adamw_" + k, 0))
        plan.done("adamw_" + k, 0, cres)
        out[n] = [rows_major(r, how) for r in res]
    gall = plan.small_gathered.reshape(NDEV, P_ROWS, D)

    def padded_convs(p):
        return lax.dynamic_update_slice_in_dim(jnp.zeros((DEPTH, CW_ROWS, BW), F32), _stack_convs(p), ch0, axis=2)

    zero_row = [jnp.zeros((1, D), F32)]
    packs = adamw_small(gall, *[_pack_small(p, padded_convs(p), zero_row) for p in (w, m, v)])
    small = [_unpack_small(p, ch0) for p in packs]
    for n in SMALL:
        out[n] = [s[n] for s in small]
    loss = packs[0][P_LOSS, 0]
    return (loss, grad_x[None], *[out[n][0] for n in WEIGHTS], *[out[n][1] for n in WEIGHTS],
            *[out[n][2] for n in WEIGHTS], *[out[n][3] for n in WEIGHTS])
```

```python
import functools

import jax
import jax.numpy as jnp
from jax import lax
from jax.experimental import pallas as pl
from jax.experimental.pallas import tpu as pltpu

F32 = jnp.float32
BF16 = jnp.bfloat16
E = pl.Element

D = 1024
BW = 512
IN_W = 8448
GL0 = 4352
FF = 2816
N_HEADS = 8
N_KV = 2
HD = 64
ATT_BLK = 128
EPS = 1e-6
LRU_C = 8.0
NEG_INF = -1e30
DEPTH = 2
NDEV = 8
CONV_A, CONV_B, CONV_D = 4, 3, 31
C_AX, C_AG, C_BV, C_BC, C_BB, C_Q, C_K, C_V, C_D1, C_D2 = 0, 512, 1024, 1536, 2048, 2560, 3072, 3200, 3328, 3840
CW_A, CW_B, CW_D, CW_ROWS = 0, 4, 8, 40
V_CAB, V_BX, V_BA, V_LAM, V_CDB, V_LNG, V_LNB, V_SINK, V_ROWS = 0, 1, 2, 3, 4, 5, 6, 7, 8
HALO = 32

ADAM_LR, ADAM_B1, ADAM_B2, ADAM_EPS, ADAM_WD, ADAM_STEP = 0.001, 0.9, 0.999, 1e-08, 0.01, 10

VMEM_LIMIT = 56 * 1024 * 1024

_NN = (((1,), (0,)), ((), ()))
_NT = (((1,), (1,)), ((), ()))
_TN = (((0,), (0,)), ((), ()))


def _dot(a, b, dims):
    return lax.dot_general(a.astype(BF16), b.astype(BF16), dims, preferred_element_type=F32)


def _cparams(n_axes):
    return pltpu.CompilerParams(dimension_semantics=("arbitrary",) * n_axes, vmem_limit_bytes=VMEM_LIMIT)


def _sds(shape, dtype):
    return jax.ShapeDtypeStruct(tuple(shape), dtype)


def _sigmoid(x):
    return jax.nn.sigmoid(x)


def _neg_expm1(x):
    p = x * (1.0 + x * (0.5 + x * (1.0 / 6.0 + x * (1.0 / 24.0 + x * (1.0 / 120.0)))))
    return jnp.where(x > -0.1, -p, 1.0 - jnp.exp(x))


def _softplus(z):
    return jnp.maximum(z, 0.0) + jnp.log1p(jnp.exp(-jnp.abs(z)))


def _gelu_and_grad(x):
    c = 0.7978845608028654
    inner = c * (x + 0.044715 * x * x * x)
    t = jnp.tanh(inner)
    g = 0.5 * x * (1.0 + t)
    dg = 0.5 * (1.0 + t) + 0.5 * x * (1.0 - t * t) * c * (1.0 + 3.0 * 0.044715 * x * x)
    return g, dg


ANY = pl.BlockSpec(memory_space=pl.ANY)
MESH = pl.DeviceIdType.MESH


def _mesh_pos():
    return lax.axis_index("x"), lax.axis_index("y"), lax.axis_index("c")


def _dev_index(px, py, pc):
    return 4 * px + 2 * py + pc


class CommJob:
    def __init__(self, inputs, aliases, out_shapes, sem_shapes, start, finish):
        self.inputs, self.aliases, self.out_shapes, self.sem_shapes = list(inputs), dict(aliases), list(out_shapes), list(sem_shapes)
        self.start, self.finish = start, finish


def _call(body, comm, args, *, grid, in_specs, out_specs, out_shape, scratch_shapes=(), name, aliases=None):
    single = not isinstance(out_shape, (list, tuple))
    out_specs = [out_specs] if single else list(out_specs)
    out_shape = [out_shape] if single else list(out_shape)
    scratch_shapes = list(scratch_shapes)
    n_in, n_out, n_scr, n_axes = len(in_specs), len(out_shape), len(scratch_shapes), len(grid)
    params = pltpu.CompilerParams(dimension_semantics=("arbitrary",) * n_axes, vmem_limit_bytes=VMEM_LIMIT)
    io_aliases = dict(aliases or {})
    if comm is None:
        outs = pl.pallas_call(body, grid=grid, in_specs=in_specs, out_specs=out_specs, out_shape=out_shape,
                              scratch_shapes=scratch_shapes, input_output_aliases=io_aliases, compiler_params=params,
                              name=name)(*args)
        return (outs[0] if single else outs), []
    c_in, c_out = len(comm.inputs), len(comm.out_shapes)
    io_aliases.update({n_in + i: n_out + o for i, o in comm.aliases.items()})

    def wrapped(*refs):
        ins, cins = refs[:n_in], refs[n_in:n_in + c_in]
        outs = refs[n_in + c_in:n_in + c_in + n_out]
        couts = refs[n_in + c_in + n_out:n_in + c_in + n_out + c_out]
        rest = refs[n_in + c_in + n_out + c_out:]
        scr, sems = rest[:n_scr], rest[n_scr:]
        first = functools.reduce(lambda a, b: a & b, [pl.program_id(a) == 0 for a in range(n_axes)])
        last = functools.reduce(lambda a, b: a & b, [pl.program_id(a) == pl.num_programs(a) - 1 for a in range(n_axes)])

        @pl.when(first)
        def _():
            comm.start(cins, couts, sems)

        body(*ins, *outs, *scr)

        @pl.when(last)
        def _():
            comm.finish(cins, couts, sems)

    outs = pl.pallas_call(
        wrapped, grid=grid, in_specs=list(in_specs) + [ANY] * c_in, out_specs=out_specs + [ANY] * c_out,
        out_shape=out_shape + comm.out_shapes, scratch_shapes=scratch_shapes + comm.sem_shapes,
        input_output_aliases=io_aliases, compiler_params=params, name=name)(*args, *comm.inputs)
    res, cres = outs[:n_out], outs[n_out:]
    return (res[0] if single else res), cres


def _comm_only(comm, name):
    c_in, c_out = len(comm.inputs), len(comm.out_shapes)

    def body(*refs):
        cins, couts, sems = refs[:c_in], refs[c_in:c_in + c_out], refs[c_in + c_out:]
        comm.start(cins, couts, sems)
        comm.finish(cins, couts, sems)

    return pl.pallas_call(body, in_specs=[ANY] * c_in, out_specs=[ANY] * c_out, out_shape=comm.out_shapes,
                          scratch_shapes=comm.sem_shapes, input_output_aliases=comm.aliases, name=name)(*comm.inputs)


def gather_job(pieces):
    inputs, aliases, out_shapes, plan, where = [], {}, [], [], {}
    for key, shard, gathered, row0, nrows in pieces:
        if key not in where:
            where[key] = (len(inputs), len(out_shapes))
            inputs.append(shard)
            if gathered is not None:
                aliases[len(inputs)] = len(out_shapes)
                inputs.append(gathered)
            out_shapes.append(_sds((NDEV * shard.shape[0], shard.shape[1]), shard.dtype))
        plan.append((*where[key], shard.shape[0], row0, nrows))
    n = len(plan)

    def copies(cins, couts, sems):
        send_sems, recv_sems, local_sems = sems
        x, y, c = _mesh_pos()
        me, sibling = (x, y, c), (x, y, 1 - c)
        chips = [(1 - x, y), (x, 1 - y), (1 - x, 1 - y)]
        local, first, relay, recv_ici, recv_d2d = [], [], [], [], []
        for p, (i_shard, i_out, rows, row0, nrows) in enumerate(plan):
            src = cins[i_shard].at[pl.ds(row0, nrows), :]

            def slot(dev, i_out=i_out, rows=rows, row0=row0, nrows=nrows):
                return couts[i_out].at[pl.ds(_dev_index(*dev) * rows + row0, nrows), :]

            def copy(g, dev, to, src=None, p=p, slot=slot):
                return pltpu.make_async_remote_copy(
                    src_ref=slot(dev) if src is None else src, dst_ref=slot(dev),
                    send_sem=send_sems.at[g, p], recv_sem=recv_sems.at[g, p], device_id=to, device_id_type=MESH)

            local.append(pltpu.make_async_copy(src, slot(me), local_sems.at[p]))
            first.append(copy(0, me, sibling, src=src))
            recv_d2d.append(copy(0, sibling, me))
            for j, chip in enumerate(chips):
                first.append(copy(1 + j, me, (*chip, c), src=src))
                recv_ici.append(copy(1 + j, (*chip, c), me))
                relay.append(copy(4 + j, (*chip, c), sibling))
                recv_d2d.append(copy(4 + j, (*chip, 1 - c), me))
        return local, first, relay, recv_ici, recv_d2d

    def start(cins, couts, sems):
        local, first, _, _, _ = copies(cins, couts, sems)
        for cp in local + first:
            cp.start()

    def finish(cins, couts, sems):
        local, first, relay, recv_ici, recv_d2d = copies(cins, couts, sems)
        for cp in recv_ici:
            cp.wait_recv()
        for cp in relay:
            cp.start()
        for cp in recv_d2d:
            cp.wait_recv()
        for cp in first + relay:
            cp.wait_send()
        for cp in local:
            cp.wait()

    sem_shapes = [pltpu.SemaphoreType.DMA((7, n)), pltpu.SemaphoreType.DMA((7, n)), pltpu.SemaphoreType.DMA((n,))]
    return CommJob(inputs, aliases, out_shapes, sem_shapes, start, finish)


def sibling_exchange_job(grads):
    n = len(grads)

    def copies(cins, couts, sems):
        send_sems, recv_sems = sems
        x, y, c = _mesh_pos()
        return [pltpu.make_async_remote_copy(
            src_ref=cins[q].at[:, 1 - c], dst_ref=couts[q], send_sem=send_sems.at[q], recv_sem=recv_sems.at[q],
            device_id=(x, y, 1 - c), device_id_type=MESH) for q in range(n)]

    def start(cins, couts, sems):
        for cp in copies(cins, couts, sems):
            cp.start()

    def finish(cins, couts, sems):
        cps = copies(cins, couts, sems)
        for cp in cps:
            cp.wait_recv()
        for cp in cps:
            cp.wait_send()

    return CommJob(grads, {}, [_sds((4,) + g.shape[2:], g.dtype) for g in grads],
                   [pltpu.SemaphoreType.DMA((n,)), pltpu.SemaphoreType.DMA((n,))], start, finish)


def chip_exchange_job(pieces):
    inputs, aliases, out_shapes, plan, where = [], {}, [], [], {}
    for partial, contrib, key, layer, row0, nrows in pieces:
        if key not in where:
            where[key] = len(out_shapes)
            out_shapes.append(_sds((4, DEPTH) + partial.shape[1:], partial.dtype))
            if contrib is not None:
                aliases[len(inputs)] = where[key]
                inputs.append(contrib)
        plan.append((len(inputs), where[key], layer, row0, nrows))
        inputs.append(partial)
    n = len(plan)

    def copies(cins, couts, sems):
        send_sems, recv_sems, local_sems = sems
        x, y, c = _mesh_pos()
        mine = 2 * x + y
        local, sends, recvs = [], [], []
        for p, (i_in, i_out, layer, row0, nrows) in enumerate(plan):
            rows = pl.ds(row0, nrows)
            local.append(pltpu.make_async_copy(cins[i_in].at[mine, rows, :], couts[i_out].at[mine, layer, rows, :],
                                               local_sems.at[p]))
            for j, (cx, cy) in enumerate([(1 - x, y), (x, 1 - y), (1 - x, 1 - y)]):
                theirs = 2 * cx + cy

                def copy(slot_there, j=j, p=p, cx=cx, cy=cy, theirs=theirs, i_in=i_in, i_out=i_out, layer=layer, rows=rows):
                    return pltpu.make_async_remote_copy(
                        src_ref=cins[i_in].at[theirs, rows, :], dst_ref=couts[i_out].at[slot_there, layer, rows, :],
                        send_sem=send_sems.at[j, p], recv_sem=recv_sems.at[j, p], device_id=(cx, cy, c), device_id_type=MESH)
                sends.append(copy(mine))
                recvs.append(copy(theirs))
        return local, sends, recvs

    def start(cins, couts, sems):
        local, sends, _ = copies(cins, couts, sems)
        for cp in local + sends:
            cp.start()

    def finish(cins, couts, sems):
        local, sends, recvs = copies(cins, couts, sems)
        for cp in recvs:
            cp.wait_recv()
        for cp in sends:
            cp.wait_send()
        for cp in local:
            cp.wait()

    sem_shapes = [pltpu.SemaphoreType.DMA((3, n)), pltpu.SemaphoreType.DMA((3, n)), pltpu.SemaphoreType.DMA((n,))]
    return CommJob(inputs, aliases, out_shapes, sem_shapes, start, finish)


def fwd_proj(x, g1, wt_in, l, comm=None):
    s = x.shape[0]
    tm = min(512, s)
    tn = 1408

    def body(x_ref, g_ref, w_ref, o_ref, xn_ref):
        @pl.when(pl.program_id(1) == 0)
        def _():
            xv = x_ref[...]
            r = lax.rsqrt(jnp.mean(xv * xv, axis=-1, keepdims=True) + EPS)
            xn_ref[...] = (xv * r * g_ref[l:l + 1, :]).astype(BF16)

        o_ref[...] = _dot(xn_ref[...], w_ref[...], _NT).astype(BF16)

    return _call(
        body, comm, (x, g1, wt_in), grid=(s // tm, IN_W // tn),
        in_specs=[pl.BlockSpec((tm, D), lambda i, j: (i, 0)),
                  pl.BlockSpec((DEPTH, D), lambda i, j: (0, 0)),
                  pl.BlockSpec((tn, D), lambda i, j: (j, 0))],
        out_specs=pl.BlockSpec((tm, tn), lambda i, j: (i, j)),
        out_shape=_sds((s, IN_W), BF16),
        scratch_shapes=[pltpu.VMEM((tm, D), BF16)], name=f"fwd_proj{l}")


def _scan_fwd(a_ref, u_ref, h_ref, h0, n_rows):
    row = lax.broadcasted_iota(jnp.int32, (8, BW), 0)

    def body(g, hprev):
        r = pl.multiple_of(g * 8, 8)
        a = a_ref[pl.ds(r, 8), :]
        u = u_ref[pl.ds(r, 8), :]
        for sft in (1, 2, 4):
            a_sh = jnp.where(row >= sft, pltpu.roll(a, sft, 0), 1.0)
            u_sh = jnp.where(row >= sft, pltpu.roll(u, sft, 0), 0.0)
            u = u + a * u_sh
            a = a * a_sh
        h = u + a * hprev
        h_ref[pl.ds(r, 8), :] = h
        return h[7:8, :]

    return lax.fori_loop(0, n_rows // 8, body, h0)


def _scan_bwd(b_ref, g_ref, o_ref, c0, n_rows):
    row = lax.broadcasted_iota(jnp.int32, (8, BW), 0)

    def body(k, cnext):
        r = pl.multiple_of((n_rows // 8 - 1 - k) * 8, 8)
        b = b_ref[pl.ds(r, 8), :]
        g = g_ref[pl.ds(r, 8), :]
        for sft in (1, 2, 4):
            b_sh = jnp.where(row < 8 - sft, pltpu.roll(b, 8 - sft, 0), 1.0)
            g_sh = jnp.where(row < 8 - sft, pltpu.roll(g, 8 - sft, 0), 0.0)
            g = g + b * g_sh
            b = b * b_sh
        o = g + b * cnext
        o_ref[pl.ds(r, 8), :] = o
        return o[0:1, :]

    return lax.fori_loop(0, n_rows // 8, body, c0)


def _shifted_copies(buf, shifted, n_rows):
    for r in range(1, 8):
        shifted[r - 1, 0:n_rows - 8, :] = buf[pl.ds(r, n_rows - 8), :]


def _window(buf, shifted, off, t):
    r = off % 8
    return buf[pl.ds(off, t), :] if r == 0 else shifted[r - 1, pl.ds(off - r, t), :]


def _branch_fwd_math(cur_ref, halo_ref, cw_ref, vec_ref, wx_ref, wa_ref, bufa, bufb, bufd, xd, first, t):
    def halo(c0):
        v = halo_ref[:, c0:c0 + BW].astype(F32)
        return jnp.where(first, 0.0, v)

    def cur(c0):
        return cur_ref[:, c0:c0 + BW].astype(F32)

    out = {}
    bufa[0:HALO, :] = halo(C_AX)
    bufa[HALO:HALO + t, :] = cur(C_AX)
    ca = jnp.zeros((t, BW), F32) + vec_ref[V_CAB:V_CAB + 1, :]
    for k in range(CONV_A):
        ca = ca + cw_ref[CW_A + k:CW_A + k + 1, :] * bufa[pl.ds(HALO - (CONV_A - 1) + k, t), :]
    gi = _sigmoid(_dot(ca, wx_ref[...], _NN) + vec_ref[V_BX:V_BX + 1, :])
    gr = _sigmoid(_dot(ca, wa_ref[...], _NN) + vec_ref[V_BA:V_BA + 1, :])
    sp = _softplus(-vec_ref[V_LAM:V_LAM + 1, :])
    la = -LRU_C * sp * gr
    a = jnp.exp(la)
    mult = jnp.sqrt(_neg_expm1(2.0 * la))
    out.update(ca=ca, gi=gi, gr=gr, sp=sp, a=a, mult=mult)
    bufb[0:HALO, :] = halo(C_BC) * halo(C_BV)
    bufb[HALO:HALO + t, :] = cur(C_BC) * cur(C_BV)
    cb = jnp.zeros((t, BW), F32)
    for k in range(CONV_B):
        cb = cb + cw_ref[CW_B + k:CW_B + k + 1, :] * bufb[pl.ds(HALO - (CONV_B - 1) + k, t), :]
    out.update(cb=cb)
    bufd[0:HALO, :] = halo(C_D1) * _sigmoid(halo(C_D2))
    s2 = _sigmoid(cur(C_D2))
    bufd[HALO:HALO + t, :] = cur(C_D1) * s2
    _shifted_copies(bufd, xd, t + HALO)
    cd = jnp.zeros((t, BW), F32) + vec_ref[V_CDB:V_CDB + 1, :]
    for k in range(CONV_D):
        cd = cd + cw_ref[CW_D + k:CW_D + k + 1, :] * _window(bufd, xd, HALO - (CONV_D - 1) + k, t)
    mu = jnp.mean(cd, axis=-1, keepdims=True)
    xc = cd - mu
    rstd = lax.rsqrt(jnp.mean(xc * xc, axis=-1, keepdims=True) + EPS)
    xh = xc * rstd
    ln = xh * vec_ref[V_LNG:V_LNG + 1, :] + vec_ref[V_LNB:V_LNB + 1, :]
    out.update(s2=s2, xh=xh, rstd=rstd, ln=ln)
    return out


def fwd_branch(proj, convw, vecs, wx_bd, wa_bd, l, comm=None):
    s = proj.shape[0]
    t = min(256, s)

    def body(cur_ref, halo_ref, cw_ref, vec_ref, wx_ref, wa_ref, pre_ref, h_ref, bufa, bufb, bufd, xd, a_s, u_s, hcar):
        first = pl.program_id(0) == 0

        @pl.when(first)
        def _():
            hcar[...] = jnp.zeros((1, BW), F32)

        v = _branch_fwd_math(cur_ref, halo_ref, cw_ref, vec_ref, wx_ref, wa_ref, bufa, bufb, bufd, xd, first, t)
        a_s[...] = v["a"]
        u_s[...] = v["ca"] * v["gi"] * v["mult"]
        hcar[...] = _scan_fwd(a_s, u_s, h_ref, hcar[...], t)
        gg, _ = _gelu_and_grad(cur_ref[:, C_AG:C_AG + BW].astype(F32))
        pre_ref[:, 0:BW] = (h_ref[...] * gg).astype(BF16)
        pre_ref[:, BW:2 * BW] = (cur_ref[:, C_BB:C_BB + BW].astype(F32) * v["cb"]).astype(BF16)
        ln = v["ln"]
        pre_ref[:, 2 * BW:3 * BW] = (ln * _sigmoid(ln)).astype(BF16)

    hb = t // HALO
    return _call(
        body, comm, (proj, proj, convw, vecs, wx_bd, wa_bd), grid=(s // t,),
        in_specs=[pl.BlockSpec((t, GL0), lambda i: (i, 0)),
                  pl.BlockSpec((HALO, GL0), lambda i: (jnp.maximum(i * hb - 1, 0), 0)),
                  pl.BlockSpec((None, CW_ROWS, BW), lambda i: (l, 0, 0)),
                  pl.BlockSpec((None, V_ROWS, BW), lambda i: (l, 0, 0)),
                  pl.BlockSpec((None, BW, BW), lambda i: (l, 0, 0)),
                  pl.BlockSpec((None, BW, BW), lambda i: (l, 0, 0))],
        out_specs=[pl.BlockSpec((t, 3 * BW), lambda i: (i, 0)), pl.BlockSpec((t, BW), lambda i: (i, 0))],
        out_shape=[_sds((s, 3 * BW), BF16), _sds((s, BW), F32)],
        scratch_shapes=[pltpu.VMEM((t + HALO, BW), F32)] * 3 + [pltpu.VMEM((7, t + HALO - 8, BW), F32)]
        + [pltpu.VMEM((t, BW), F32)] * 2 + [pltpu.VMEM((1, BW), F32)],
        name=f"fwd_branch{l}")


GRP = N_HEADS // N_KV


def _attn_mask_bias(first_block):
    shape = (GRP * ATT_BLK, 2 * ATT_BLK)
    qi = lax.broadcasted_iota(jnp.int32, shape, 0) & (ATT_BLK - 1)
    ki = lax.broadcasted_iota(jnp.int32, shape, 1)
    dist = qi + ATT_BLK - ki
    valid = (dist >= 0) & (dist < ATT_BLK) & (jnp.logical_not(first_block) | (ki >= ATT_BLK))
    return dist.astype(F32), valid


def _per_head(hk, values):
    hl = lax.broadcasted_iota(jnp.int32, (GRP * ATT_BLK, 1), 0) // ATT_BLK
    out = values[GRP - 1]
    for j in range(GRP - 2, -1, -1):
        out = jnp.where(hl == j, values[j], out)
    return out


def _attn_probs(q_ref, kvp_ref, kvc_ref, vec_ref, distf, valid):
    kvs = range(N_KV)
    heads = [range(hk * GRP, (hk + 1) * GRP) for hk in kvs]
    q4 = [jnp.concatenate([q_ref[:, h * HD:(h + 1) * HD] for h in heads[hk]], axis=0) for hk in kvs]
    k2 = [jnp.concatenate([kvp_ref[:, hk * HD:(hk + 1) * HD], kvc_ref[:, hk * HD:(hk + 1) * HD]], axis=0) for hk in kvs]
    v2 = [jnp.concatenate([kvp_ref[:, (N_KV + hk) * HD:(N_KV + hk + 1) * HD],
                           kvc_ref[:, (N_KV + hk) * HD:(N_KV + hk + 1) * HD]], axis=0) for hk in kvs]
    slope = [_per_head(hk, [2.0 ** (-8.0 * (h + 1) / N_HEADS) for h in heads[hk]]) for hk in kvs]
    sink = [_per_head(hk, [vec_ref[V_SINK:V_SINK + 1, h:h + 1] for h in heads[hk]]) for hk in kvs]
    sc = [_dot(q4[hk], k2[hk], _NT) for hk in kvs]
    sc = [jnp.where(valid, sc[hk] * (HD ** -0.5) - slope[hk] * distf, NEG_INF) for hk in kvs]
    m = [jnp.maximum(jnp.max(sc[hk], axis=-1, keepdims=True), sink[hk]) for hk in kvs]
    p = [jnp.exp(sc[hk] - m[hk]) for hk in kvs]
    es = [jnp.exp(sink[hk] - m[hk]) for hk in kvs]
    inv = [1.0 / (jnp.sum(p[hk], axis=-1, keepdims=True) + es[hk]) for hk in kvs]
    return [(q4[hk], k2[hk], v2[hk], p[hk] * inv[hk], es[hk] * inv[hk]) for hk in kvs]


def fwd_attn(proj, vecs, l, comm=None):
    s = proj.shape[0]
    nb = s // ATT_BLK

    def body(q_ref, kvp_ref, kvc_ref, vec_ref, o_ref):
        distf, valid = _attn_mask_bias(pl.program_id(0) == 0)
        groups = _attn_probs(q_ref, kvp_ref, kvc_ref, vec_ref, distf, valid)
        outs = [_dot(p, v2, _NN).astype(BF16) for _, _, v2, p, _ in groups]
        for hk, out in enumerate(outs):
            for j in range(GRP):
                h = hk * GRP + j
                o_ref[:, h * HD:(h + 1) * HD] = out[j * ATT_BLK:(j + 1) * ATT_BLK]

    return _call(
        body, comm, (proj, proj, proj, vecs), grid=(nb,),
        in_specs=[pl.BlockSpec((ATT_BLK, BW), lambda i: (i, C_Q // BW)),
                  pl.BlockSpec((ATT_BLK, 256), lambda i: (jnp.maximum(i - 1, 0), C_K // 256)),
                  pl.BlockSpec((ATT_BLK, 256), lambda i: (i, C_K // 256)),
                  pl.BlockSpec((None, V_ROWS, BW), lambda i: (l, 0, 0))],
        out_specs=pl.BlockSpec((ATT_BLK, BW), lambda i: (i, 0)),
        out_shape=_sds((s, BW), BF16), name=f"fwd_attn{l}")


def fwd_merge(x, proj, pre_abd, pre_c, wt_a, wt_b, wt_c, wt_d, w_o, l, comm=None):
    s = x.shape[0]
    tm = min(256, s)

    def body(x_ref, gl_ref, pabd_ref, pc_ref, wa_ref, wb_ref, wc_ref, wd_ref, wo_ref, y_ref, mg_ref, h1_ref):
        pres = (pabd_ref[:, 0:BW], pabd_ref[:, BW:2 * BW], pc_ref[...], pabd_ref[:, 2 * BW:3 * BW])
        merged = jnp.zeros((tm, D), F32)
        for k, (pre, w_ref) in enumerate(zip(pres, (wa_ref, wb_ref, wc_ref, wd_ref))):
            yk = _dot(pre, w_ref[...], _NT)
            y_ref[:, k * D:(k + 1) * D] = yk.astype(BF16)
            merged = merged + _sigmoid(gl_ref[:, k * D:(k + 1) * D].astype(F32)) * yk
        mg_ref[...] = merged.astype(BF16)
        h1_ref[...] = x_ref[...] + _dot(merged, wo_ref[...], _NN)

    wspec = pl.BlockSpec((D, BW), lambda i: (0, 0))
    return _call(
        body, comm, (x, proj, pre_abd, pre_c, wt_a, wt_b, wt_c, wt_d, w_o), grid=(s // tm,),
        in_specs=[pl.BlockSpec((tm, D), lambda i: (i, 0)),
                  pl.BlockSpec((E(tm), E(4 * D)), lambda i: (i * tm, GL0)),
                  pl.BlockSpec((tm, 3 * BW), lambda i: (i, 0)),
                  pl.BlockSpec((tm, BW), lambda i: (i, 0)),
                  wspec, wspec, wspec, wspec,
                  pl.BlockSpec((D, D), lambda i: (0, 0))],
        out_specs=[pl.BlockSpec((tm, 4 * D), lambda i: (i, 0)), pl.BlockSpec((tm, D), lambda i: (i, 0)),
                   pl.BlockSpec((tm, D), lambda i: (i, 0))],
        out_shape=[_sds((s, 4 * D), BF16), _sds((s, D), BF16), _sds((s, D), F32)], name=f"fwd_merge{l}")


def fwd_ffn(h1, g2, wt_gate, wt_up, w_down, l, comm=None):
    s = h1.shape[0]
    tm = min(512, s)
    fc = FF // 2

    def body(h_ref, g_ref, wg_ref, wu_ref, wd_ref, xo_ref, fg_ref, fu_ref, hn_ref, acc_ref):
        j = pl.program_id(1)

        @pl.when(j == 0)
        def _():
            hv = h_ref[...]
            r = lax.rsqrt(jnp.mean(hv * hv, axis=-1, keepdims=True) + EPS)
            hn_ref[...] = (hv * r * g_ref[l:l + 1, :]).astype(BF16)
            acc_ref[...] = hv

        fg = _dot(hn_ref[...], wg_ref[...], _NT)
        fu = _dot(hn_ref[...], wu_ref[...], _NT)
        fg_ref[...] = fg.astype(BF16)
        fu_ref[...] = fu.astype(BF16)
        acc_ref[...] += _dot(fg * _sigmoid(fg) * fu, wd_ref[...], _NN)

        @pl.when(j == pl.num_programs(1) - 1)
        def _():
            xo_ref[...] = acc_ref[...]

    wspec = pl.BlockSpec((fc, D), lambda i, j: (j, 0))
    return _call(
        body, comm, (h1, g2, wt_gate, wt_up, w_down), grid=(s // tm, FF // fc),
        in_specs=[pl.BlockSpec((tm, D), lambda i, j: (i, 0)), pl.BlockSpec((DEPTH, D), lambda i, j: (0, 0)),
                  wspec, wspec, wspec],
        out_specs=[pl.BlockSpec((tm, D), lambda i, j: (i, 0)), pl.BlockSpec((tm, fc), lambda i, j: (i, j)),
                   pl.BlockSpec((tm, fc), lambda i, j: (i, j))],
        out_shape=[_sds((s, D), F32), _sds((s, FF), BF16), _sds((s, FF), BF16)],
        scratch_shapes=[pltpu.VMEM((tm, D), BF16), pltpu.VMEM((tm, D), F32)], name=f"fwd_ffn{l}")


def loss_head(x, gf, target):
    s = x.shape[0]
    tm = min(512, s)

    def body(x_ref, g_ref, t_ref, dx_ref, st_ref):
        @pl.when(pl.program_id(0) == 0)
        def _():
            st_ref[...] = jnp.zeros((8, D), F32)

        xv = x_ref[...]
        g = g_ref[...]
        r = lax.rsqrt(jnp.mean(xv * xv, axis=-1, keepdims=True) + EPS)
        n = xv * r
        err = n * g - t_ref[...]
        dy = err * (1.0 / D)
        dn = dy * g
        dx_ref[...] = r * (dn - n * jnp.mean(dn * n, axis=-1, keepdims=True))
        st_ref[0:1, :] += jnp.sum(dy * n, axis=0, keepdims=True)
        lsum = 0.5 * jnp.sum(jnp.mean(err * err, axis=-1, keepdims=True), axis=0, keepdims=True)
        st_ref[1:2, :] += jnp.broadcast_to(lsum, (1, D))

    return pl.pallas_call(
        body, grid=(s // tm,),
        in_specs=[pl.BlockSpec((tm, D), lambda i: (i, 0)), pl.BlockSpec((1, D), lambda i: (0, 0)),
                  pl.BlockSpec((tm, D), lambda i: (i, 0))],
        out_specs=[pl.BlockSpec((tm, D), lambda i: (i, 0)), pl.BlockSpec((8, D), lambda i: (0, 0))],
        out_shape=[_sds((s, D), F32), _sds((8, D), F32)],
        compiler_params=_cparams(1), name="loss_head")(x, gf, target)


def _edge_index(j, i, n_j, n_i):
    return jnp.where((j == 0) | (j == n_j - 1), i, n_i - 1)


def bwd_ffn(dxo, h1, fg, fu, g2, wt_gate, wt_up, w_down, l, comm=None):
    s = h1.shape[0]
    tm = min(512, s)
    fc = 256
    n_j, n_i = FF // fc, s // tm

    def body(dxo_ref, h_ref, fg_ref, fu_ref, g_ref, wg_ref, wu_ref, wd_ref,
             dh_ref, dwg_ref, dwu_ref, dwd_ref, st_ref, dhn, dxo_b, hn_b, ag, au, ad):
        j, i = pl.program_id(0), pl.program_id(1)
        rows = pl.ds(pl.multiple_of(i * tm, tm), tm)
        g = g_ref[l:l + 1, :]

        @pl.when(j == 0)
        def _():
            hv = h_ref[...]
            r = lax.rsqrt(jnp.mean(hv * hv, axis=-1, keepdims=True) + EPS)
            hn_b[rows, :] = (hv * r * g).astype(BF16)
            dxo_b[rows, :] = dxo_ref[...].astype(BF16)
            dhn[rows, :] = jnp.zeros((tm, D), F32)

        @pl.when((j == 0) & (i == 0))
        def _():
            st_ref[...] = jnp.zeros((8, D), F32)

        @pl.when(i == 0)
        def _():
            ag[...] = jnp.zeros((fc, D), F32)
            au[...] = jnp.zeros((fc, D), F32)
            ad[...] = jnp.zeros((fc, D), F32)

        fgv = fg_ref[...].astype(F32)
        fuv = fu_ref[...].astype(F32)
        sg = _sigmoid(fgv)
        sil = fgv * sg
        dxb = dxo_b[rows, :]
        hnb = hn_b[rows, :]
        d_act = _dot(dxb, wd_ref[...], _NT)
        ad[...] += _dot(sil * fuv, dxb, _TN)
        d_fg = (d_act * fuv * (sg * (1.0 + fgv * (1.0 - sg)))).astype(BF16)
        d_fu = (d_act * sil).astype(BF16)
        ag[...] += _dot(d_fg, hnb, _TN)
        au[...] += _dot(d_fu, hnb, _TN)
        dhn[rows, :] += _dot(d_fg, wg_ref[...], _NN) + _dot(d_fu, wu_ref[...], _NN)

        @pl.when(i == n_i - 1)
        def _():
            dwg_ref[...] = ag[...].astype(BF16)
            dwu_ref[...] = au[...].astype(BF16)
            dwd_ref[...] = ad[...].astype(BF16)

        @pl.when(j == n_j - 1)
        def _():
            hv = h_ref[...]
            r = lax.rsqrt(jnp.mean(hv * hv, axis=-1, keepdims=True) + EPS)
            n = hv * r
            dv = dhn[rows, :]
            dn = dv * g
            dh_ref[...] = dxo_ref[...] + r * (dn - n * jnp.mean(dn * n, axis=-1, keepdims=True))
            st_ref[0:1, :] += jnp.sum(dv * n, axis=0, keepdims=True)

    edge = lambda j, i: (_edge_index(j, i, n_j, n_i), 0)
    wspec = pl.BlockSpec((fc, D), lambda j, i: (j, 0))
    dwspec = pl.BlockSpec((fc, D), lambda j, i: (j, 0))
    return _call(
        body, comm, (dxo, h1, fg, fu, g2, wt_gate, wt_up, w_down), grid=(n_j, n_i),
        in_specs=[pl.BlockSpec((tm, D), edge), pl.BlockSpec((tm, D), edge),
                  pl.BlockSpec((tm, fc), lambda j, i: (i, j)), pl.BlockSpec((tm, fc), lambda j, i: (i, j)),
                  pl.BlockSpec((DEPTH, D), lambda j, i: (0, 0)), wspec, wspec, wspec],
        out_specs=[pl.BlockSpec((tm, D), lambda j, i: (jnp.where(j == n_j - 1, i, 0), 0)),
                   dwspec, dwspec, dwspec, pl.BlockSpec((8, D), lambda j, i: (0, 0))],
        out_shape=[_sds((s, D), F32), _sds((FF, D), BF16), _sds((FF, D), BF16), _sds((FF, D), BF16), _sds((8, D), F32)],
        scratch_shapes=[pltpu.VMEM((s, D), F32), pltpu.VMEM((s, D), BF16), pltpu.VMEM((s, D), BF16),
                        pltpu.VMEM((fc, D), F32), pltpu.VMEM((fc, D), F32), pltpu.VMEM((fc, D), F32)],
        name=f"bwd_ffn{l}")


def bwd_merge(dh1, y4, proj, merged, pre_abd, pre_c, wt_a, wt_b, wt_c, wt_d, w_o, l, comm=None):
    s = dh1.shape[0]
    tm = min(256, s)
    n_i = s // tm

    def body(dh_ref, y_ref, gl_ref, mg_ref, pabd_ref, pc_ref, wa_ref, wb_ref, wc_ref, wd_ref, wo_ref,
             dgl_ref, dpre_ref, dwo_ref, dwa_ref, dwb_ref, dwc_ref, dwd_ref, ao, aa, ab, ac, ad):
        i = pl.program_id(0)
        accs = (aa, ab, ac, ad)

        @pl.when(i == 0)
        def _():
            ao[...] = jnp.zeros((D, D), F32)
            for acc in accs:
                acc[...] = jnp.zeros((D, BW), F32)

        dhb = dh_ref[...].astype(BF16)
        dmg = _dot(dhb, wo_ref[...], _NT)
        ao[...] += _dot(mg_ref[...], dhb, _TN)
        pres = (pabd_ref[:, 0:BW], pabd_ref[:, BW:2 * BW], pc_ref[...], pabd_ref[:, 2 * BW:3 * BW])
        for k, (pre, w_ref, acc) in enumerate(zip(pres, (wa_ref, wb_ref, wc_ref, wd_ref), accs)):
            gk = _sigmoid(gl_ref[:, k * D:(k + 1) * D].astype(F32))
            yk = y_ref[:, k * D:(k + 1) * D].astype(F32)
            dgl_ref[:, k * D:(k + 1) * D] = (dmg * yk * gk * (1.0 - gk)).astype(BF16)
            dyk = (dmg * gk).astype(BF16)
            dpre_ref[:, k * BW:(k + 1) * BW] = _dot(dyk, w_ref[...], _NN).astype(BF16)
            acc[...] += _dot(dyk, pre, _TN)

        @pl.when(i == n_i - 1)
        def _():
            dwo_ref[...] = ao[...].astype(BF16)
            for o_ref, acc in zip((dwa_ref, dwb_ref, dwc_ref, dwd_ref), accs):
                o_ref[...] = acc[...].astype(BF16)

    wspec = pl.BlockSpec((D, BW), lambda i: (0, 0))
    dwspec = pl.BlockSpec((D, BW), lambda i: (0, 0))
    return _call(
        body, comm, (dh1, y4, proj, merged, pre_abd, pre_c, wt_a, wt_b, wt_c, wt_d, w_o), grid=(n_i,),
        in_specs=[pl.BlockSpec((tm, D), lambda i: (i, 0)),
                  pl.BlockSpec((tm, 4 * D), lambda i: (i, 0)),
                  pl.BlockSpec((E(tm), E(4 * D)), lambda i: (i * tm, GL0)),
                  pl.BlockSpec((tm, D), lambda i: (i, 0)),
                  pl.BlockSpec((tm, 3 * BW), lambda i: (i, 0)),
                  pl.BlockSpec((tm, BW), lambda i: (i, 0)),
                  wspec, wspec, wspec, wspec,
                  pl.BlockSpec((D, D), lambda i: (0, 0))],
        out_specs=[pl.BlockSpec((E(tm), E(4 * D)), lambda i: (i * tm, GL0)),
                   pl.BlockSpec((tm, 4 * BW), lambda i: (i, 0)),
                   pl.BlockSpec((D, D), lambda i: (0, 0)), dwspec, dwspec, dwspec, dwspec],
        out_shape=[_sds((s, IN_W), BF16), _sds((s, 4 * BW), BF16), _sds((D, D), BF16)] + [_sds((D, BW), BF16)] * 4,
        scratch_shapes=[pltpu.VMEM((D, D), F32)] + [pltpu.VMEM((D, BW), F32)] * 4, name=f"bwd_merge{l}")


def bwd_attn(proj, dpre, vecs, l, comm=None):
    s = proj.shape[0]
    nb = s // ATT_BLK
    grp = N_HEADS // N_KV

    def body(q_ref, kvp_ref, kvc_ref, do_ref, vec_ref, dq_ref, dkc_ref, dkp_ref, st_ref):
        @pl.when(pl.program_id(0) == 0)
        def _():
            st_ref[...] = jnp.zeros((8, 128), F32)

        distf, valid = _attn_mask_bias(pl.program_id(0) == 0)
        lane = lax.broadcasted_iota(jnp.int32, (1, 128), 1)
        dsink = jnp.zeros((1, 128), F32)
        groups = _attn_probs(q_ref, kvp_ref, kvc_ref, vec_ref, distf, valid)
        kvs = range(N_KV)
        do4s = [jnp.concatenate([do_ref[:, h * HD:(h + 1) * HD] for h in range(hk * grp, (hk + 1) * grp)], axis=0) for hk in kvs]
        dps = [_dot(do4s[hk], groups[hk][2], _NT) for hk in kvs]
        deltas = [jnp.sum(groups[hk][3] * dps[hk], axis=-1, keepdims=True) for hk in kvs]
        dss = [groups[hk][3] * (dps[hk] - deltas[hk]) * (HD ** -0.5) for hk in kvs]
        for hk in kvs:
            q4, k2, v2, p, ps = groups[hk]
            do4, delta, ds = do4s[hk], deltas[hk], dss[hk]
            dq4 = _dot(ds, k2, _NN).astype(BF16)
            dk2 = _dot(ds, q4, _TN)
            dv2 = _dot(p, do4, _TN)
            psd = ps * delta
            for j in range(grp):
                h = hk * grp + j
                rows = slice(j * ATT_BLK, (j + 1) * ATT_BLK)
                dq_ref[:, h * HD:(h + 1) * HD] = dq4[rows]
                dsink = dsink + jnp.where(lane == h, -jnp.sum(psd[rows], axis=0, keepdims=True), 0.0)
            dkp_ref[:, hk * HD:(hk + 1) * HD] = dk2[0:ATT_BLK].astype(BF16)
            dkc_ref[:, hk * HD:(hk + 1) * HD] = dk2[ATT_BLK:].astype(BF16)
            dkp_ref[:, (N_KV + hk) * HD:(N_KV + hk + 1) * HD] = dv2[0:ATT_BLK].astype(BF16)
            dkc_ref[:, (N_KV + hk) * HD:(N_KV + hk + 1) * HD] = dv2[ATT_BLK:].astype(BF16)
        st_ref[0:1, :] += dsink

    return _call(
        body, comm, (proj, proj, proj, dpre, vecs), grid=(nb,),
        in_specs=[pl.BlockSpec((ATT_BLK, BW), lambda i: (i, C_Q // BW)),
                  pl.BlockSpec((ATT_BLK, 256), lambda i: (jnp.maximum(i - 1, 0), C_K // 256)),
                  pl.BlockSpec((ATT_BLK, 256), lambda i: (i, C_K // 256)),
                  pl.BlockSpec((ATT_BLK, BW), lambda i: (i, 2)),
                  pl.BlockSpec((None, V_ROWS, BW), lambda i: (l, 0, 0))],
        out_specs=[pl.BlockSpec((ATT_BLK, BW), lambda i: (i, 0)), pl.BlockSpec((ATT_BLK, 256), lambda i: (i, 0)),
                   pl.BlockSpec((ATT_BLK, 256), lambda i: (i, 0)), pl.BlockSpec((8, 128), lambda i: (0, 0))],
        out_shape=[_sds((s, BW), BF16), _sds((s, 256), BF16), _sds((s, 256), BF16), _sds((8, 128), F32)],
        name=f"bwd_attn{l}")


def bwd_branch(proj, dproj, dpre, h, dq, dkc, dkp, convw, vecs, wx_bd, wa_bd, l, comm=None):
    s = proj.shape[0]
    t = 2 * ATT_BLK
    nt = s // t
    nb = s // ATT_BLK
    hb = t // HALO

    def body(cur_ref, halo_ref, dpre_ref, h_ref, hp_ref, dq_ref, dkc_ref, dkp1_ref, dkp2_ref,
             cw_ref, vec_ref, wx_ref, wa_ref, dproj_in, dp_ref, dcw_ref, dvec_ref, dwx_ref, dwa_ref,
             bufa, bufb, bufd, xd, xg, a_ext, hbuf, b_s, g_s, dh_s, ga, gb, gd, dhcar):
        del dproj_in
        step = pl.program_id(0)
        ti = nt - 1 - step
        first = ti == 0

        @pl.when(step == 0)
        def _():
            dcw_ref[...] = jnp.zeros((CW_ROWS, BW), F32)
            dvec_ref[...] = jnp.zeros((V_ROWS, BW), F32)
            dwx_ref[...] = jnp.zeros((BW, BW), F32)
            dwa_ref[...] = jnp.zeros((BW, BW), F32)
            dhcar[...] = jnp.zeros((1, BW), F32)
            a_ext[t:t + 8, :] = jnp.zeros((8, BW), F32)
            ga[t:t + 8, :] = jnp.zeros((8, BW), F32)
            gb[t:t + 8, :] = jnp.zeros((8, BW), F32)
            gd[t:t + HALO, :] = jnp.zeros((HALO, BW), F32)

        def cur(c0):
            return cur_ref[:, c0:c0 + BW].astype(F32)

        def rsum(v):
            return jnp.sum(v, axis=0, keepdims=True)

        def put(c0, v):
            dp_ref[:, c0:c0 + BW] = v.astype(BF16)

        v = _branch_fwd_math(cur_ref, halo_ref, cw_ref, vec_ref, wx_ref, wa_ref, bufa, bufb, bufd, xd, first, t)
        ca, gi, gr, sp, a, mult = v["ca"], v["gi"], v["gr"], v["sp"], v["a"], v["mult"]
        dpa = dpre_ref[:, 0:BW].astype(F32)
        gg, dgg = _gelu_and_grad(cur(C_AG))
        hv = h_ref[...]
        put(C_AG, dpa * hv * dgg)
        a_ext[0:t, :] = a
        b_s[...] = a_ext[pl.ds(1, t), :]
        g_s[...] = dpa * gg
        dhcar[...] = _scan_bwd(b_s, g_s, dh_s, dhcar[...], t)
        a_ext[t:t + 1, :] = a[0:1, :]
        dh = dh_s[...]
        hbuf[0:8, :] = jnp.where(first, 0.0, hp_ref[...])
        hbuf[8:8 + t, :] = hv
        da = dh * hbuf[pl.ds(7, t), :]
        d_ca = dh * gi * mult
        d_gi = dh * ca * mult
        d_mult = dh * ca * gi
        d_la = da * a - d_mult * (a * a) / mult
        lam = vec_ref[V_LAM:V_LAM + 1, :]
        dvec_ref[V_LAM:V_LAM + 1, :] += rsum(d_la * gr) * (LRU_C * _sigmoid(-lam))
        d_gr = d_la * (-LRU_C * sp)
        d_zr = d_gr * gr * (1.0 - gr)
        d_zi = d_gi * gi * (1.0 - gi)
        dvec_ref[V_BA:V_BA + 1, :] += rsum(d_zr)
        dvec_ref[V_BX:V_BX + 1, :] += rsum(d_zi)
        dwa_ref[...] += _dot(ca, d_zr, _TN)
        dwx_ref[...] += _dot(ca, d_zi, _TN)
        d_ca = d_ca + _dot(d_zi, wx_ref[...], _NT) + _dot(d_zr, wa_ref[...], _NT)
        dvec_ref[V_CAB:V_CAB + 1, :] += rsum(d_ca)
        ga[0:t, :] = d_ca
        d_ax = jnp.zeros((t, BW), F32)
        for k in range(CONV_A):
            d_ax = d_ax + cw_ref[CW_A + k:CW_A + k + 1, :] * ga[pl.ds(CONV_A - 1 - k, t), :]
            dcw_ref[CW_A + k:CW_A + k + 1, :] += rsum(d_ca * bufa[pl.ds(HALO - (CONV_A - 1) + k, t), :])
        ga[t:t + 8, :] = d_ca[0:8, :]
        put(C_AX, d_ax)
        dpb = dpre_ref[:, BW:2 * BW].astype(F32)
        put(C_BB, dpb * v["cb"])
        d_cb = dpb * cur(C_BB)
        gb[0:t, :] = d_cb
        d_cbin = jnp.zeros((t, BW), F32)
        for k in range(CONV_B):
            d_cbin = d_cbin + cw_ref[CW_B + k:CW_B + k + 1, :] * gb[pl.ds(CONV_B - 1 - k, t), :]
            dcw_ref[CW_B + k:CW_B + k + 1, :] += rsum(d_cb * bufb[pl.ds(HALO - (CONV_B - 1) + k, t), :])
        gb[t:t + 8, :] = d_cb[0:8, :]
        put(C_BC, d_cbin * cur(C_BV))
        put(C_BV, d_cbin * cur(C_BC))
        dpd = dpre_ref[:, 3 * BW:4 * BW].astype(F32)
        ln, xh, rstd, s2 = v["ln"], v["xh"], v["rstd"], v["s2"]
        sg = _sigmoid(ln)
        d_ln = dpd * sg * (1.0 + ln * (1.0 - sg))
        dvec_ref[V_LNG:V_LNG + 1, :] += rsum(d_ln * xh)
        dvec_ref[V_LNB:V_LNB + 1, :] += rsum(d_ln)
        d_xh = d_ln * vec_ref[V_LNG:V_LNG + 1, :]
        d_cd = rstd * (d_xh - jnp.mean(d_xh, axis=-1, keepdims=True)
                       - xh * jnp.mean(d_xh * xh, axis=-1, keepdims=True))
        dvec_ref[V_CDB:V_CDB + 1, :] += rsum(d_cd)
        gd[0:t, :] = d_cd
        _shifted_copies(gd, xg, t + HALO)
        d_dg = jnp.zeros((t, BW), F32)
        for k in range(CONV_D):
            d_dg = d_dg + cw_ref[CW_D + k:CW_D + k + 1, :] * _window(gd, xg, CONV_D - 1 - k, t)
            dcw_ref[CW_D + k:CW_D + k + 1, :] += rsum(d_cd * _window(bufd, xd, HALO - (CONV_D - 1) + k, t))
        gd[t:t + HALO, :] = d_cd[0:HALO, :]
        put(C_D1, d_dg * s2)
        put(C_D2, d_dg * cur(C_D1) * s2 * (1.0 - s2))
        dp_ref[:, C_Q:C_Q + BW] = dq_ref[...]
        dkp2 = jnp.where(step == 0, 0.0, dkp2_ref[...].astype(F32))
        dp_ref[0:ATT_BLK, C_K:C_K + 256] = (dkc_ref[0:ATT_BLK, :].astype(F32) + dkp1_ref[...].astype(F32)).astype(BF16)
        dp_ref[ATT_BLK:t, C_K:C_K + 256] = (dkc_ref[ATT_BLK:t, :].astype(F32) + dkp2).astype(BF16)

    rev = lambda i: nt - 1 - i
    full = lambda r, c: pl.BlockSpec((r, c), lambda i: (0, 0))
    return _call(
        body, comm, (proj, proj, dpre, h, h, dq, dkc, dkp, dkp, convw, vecs, wx_bd, wa_bd, dproj), grid=(nt,),
        in_specs=[pl.BlockSpec((t, GL0), lambda i: (rev(i), 0)),
                  pl.BlockSpec((HALO, GL0), lambda i: (jnp.maximum(rev(i) * hb - 1, 0), 0)),
                  pl.BlockSpec((t, 4 * BW), lambda i: (rev(i), 0)),
                  pl.BlockSpec((t, BW), lambda i: (rev(i), 0)),
                  pl.BlockSpec((8, BW), lambda i: (jnp.maximum(rev(i) * (t // 8) - 1, 0), 0)),
                  pl.BlockSpec((t, BW), lambda i: (rev(i), 0)),
                  pl.BlockSpec((t, 256), lambda i: (rev(i), 0)),
                  pl.BlockSpec((ATT_BLK, 256), lambda i: (2 * rev(i) + 1, 0)),
                  pl.BlockSpec((ATT_BLK, 256), lambda i: (jnp.minimum(2 * rev(i) + 2, nb - 1), 0)),
                  pl.BlockSpec((None, CW_ROWS, BW), lambda i: (l, 0, 0)),
                  pl.BlockSpec((None, V_ROWS, BW), lambda i: (l, 0, 0)),
                  pl.BlockSpec((None, BW, BW), lambda i: (l, 0, 0)),
                  pl.BlockSpec((None, BW, BW), lambda i: (l, 0, 0)),
                  pl.BlockSpec(memory_space=pl.ANY)],
        out_specs=[pl.BlockSpec((t, GL0), lambda i: (rev(i), 0)),
                   full(CW_ROWS, BW), full(V_ROWS, BW), full(BW, BW), full(BW, BW)],
        out_shape=[_sds((s, IN_W), BF16), _sds((CW_ROWS, BW), F32), _sds((V_ROWS, BW), F32),
                   _sds((BW, BW), F32), _sds((BW, BW), F32)],
        scratch_shapes=[pltpu.VMEM((t + HALO, BW), F32)] * 3 + [pltpu.VMEM((7, t + HALO - 8, BW), F32)] * 2
        + [pltpu.VMEM((t + 8, BW), F32), pltpu.VMEM((t + 8, BW), F32)]
        + [pltpu.VMEM((t, BW), F32)] * 3
        + [pltpu.VMEM((t + 8, BW), F32), pltpu.VMEM((t + 8, BW), F32), pltpu.VMEM((t + HALO, BW), F32),
           pltpu.VMEM((1, BW), F32)],
        aliases={13: 0}, name=f"bwd_branch{l}")


def bwd_proj(dproj, x, dh1, g1, wt_in, l, comm=None):
    s = x.shape[0]
    tm = min(512, s)
    ck = 1408
    n_j, n_i = IN_W // ck, s // tm

    def body(dp_ref, x_ref, dh_ref, g_ref, w_ref, dx_ref, dw_ref, st_ref, dxn, xn_b, acc):
        j, i = pl.program_id(0), pl.program_id(1)
        rows = pl.ds(pl.multiple_of(i * tm, tm), tm)
        g = g_ref[l:l + 1, :]

        @pl.when(j == 0)
        def _():
            xv = x_ref[...]
            r = lax.rsqrt(jnp.mean(xv * xv, axis=-1, keepdims=True) + EPS)
            xn_b[rows, :] = (xv * r * g).astype(BF16)
            dxn[rows, :] = jnp.zeros((tm, D), F32)

        @pl.when((j == 0) & (i == 0))
        def _():
            st_ref[...] = jnp.zeros((8, D), F32)

        @pl.when(i == 0)
        def _():
            acc[...] = jnp.zeros((ck, D), F32)

        dp = dp_ref[...]
        dxn[rows, :] += _dot(dp, w_ref[...], _NN)
        acc[...] += _dot(dp, xn_b[rows, :], _TN)

        @pl.when(i == n_i - 1)
        def _():
            dw_ref[...] = acc[...].astype(BF16)

        @pl.when(j == n_j - 1)
        def _():
            xv = x_ref[...]
            r = lax.rsqrt(jnp.mean(xv * xv, axis=-1, keepdims=True) + EPS)
            n = xv * r
            dv = dxn[rows, :]
            dn = dv * g
            dx_ref[...] = dh_ref[...] + r * (dn - n * jnp.mean(dn * n, axis=-1, keepdims=True))
            st_ref[0:1, :] += jnp.sum(dv * n, axis=0, keepdims=True)

    lastrow = lambda j, i: (jnp.where(j == n_j - 1, i, 0), 0)
    return _call(
        body, comm, (dproj, x, dh1, g1, wt_in), grid=(n_j, n_i),
        in_specs=[pl.BlockSpec((tm, ck), lambda j, i: (i, j)),
                  pl.BlockSpec((tm, D), lambda j, i: (_edge_index(j, i, n_j, n_i), 0)),
                  pl.BlockSpec((tm, D), lastrow),
                  pl.BlockSpec((DEPTH, D), lambda j, i: (0, 0)),
                  pl.BlockSpec((ck, D), lambda j, i: (j, 0))],
        out_specs=[pl.BlockSpec((tm, D), lastrow), pl.BlockSpec((ck, D), lambda j, i: (j, 0)),
                   pl.BlockSpec((8, D), lambda j, i: (0, 0))],
        out_shape=[_sds((s, D), F32), _sds((IN_W, D), BF16), _sds((8, D), F32)],
        scratch_shapes=[pltpu.VMEM((s, D), F32), pltpu.VMEM((s, D), BF16), pltpu.VMEM((ck, D), F32)],
        name=f"bwd_proj{l}")


def _block_diag(w):
    nl, nb, bw, _ = w.shape
    eye = jnp.eye(nb, dtype=w.dtype)
    return jnp.einsum("lhij,hk->lhikj", w, eye).reshape(nl, nb * bw, nb * bw).astype(BF16)


class NoOverlap:
    def __init__(self, big):
        self.big = big

    def weights(self, l):
        return self.big[l]

    def job(self, slot, l):
        return None

    def done(self, slot, l, results):
        pass

    def new_grads(self, group, l, grads):
        pass


def local_step(x, target, norm1_g, norm2_g, final_g, convw, vecs, lru_wx, lru_wa, plan):
    wx_bd, wa_bd = _block_diag(lru_wx), _block_diag(lru_wa)

    def run(fn, slot, l, *args):
        res, cres = fn(*args, l, comm=plan.job(slot, l))
        plan.done(slot, l, cres)
        return res

    saved = []
    for l in range(DEPTH):
        proj = run(fwd_proj, "fwd_proj", l, x, norm1_g, plan.weights(l)["in_t"])
        pre_abd, h = run(fwd_branch, "fwd_branch", l, proj, convw, vecs, wx_bd, wa_bd)
        pre_c = run(fwd_attn, "fwd_attn", l, proj, vecs)
        w = plan.weights(l)
        y4, merged, h1 = run(fwd_merge, "fwd_merge", l, x, proj, pre_abd, pre_c, w["a_t"], w["b_t"], w["c_t"], w["d_t"], w["o"])
        w = plan.weights(l)
        x_out, fg, fu = run(fwd_ffn, "fwd_ffn", l, h1, norm2_g, w["gate_t"], w["up_t"], w["down"])
        saved.append((x, proj, pre_abd, h, pre_c, y4, merged, h1, fg, fu))
        x = x_out
    dx, head_stats = loss_head(x, final_g.reshape(1, D), target)
    small = [None] * DEPTH
    for l in reversed(range(DEPTH)):
        x_in, proj, pre_abd, h, pre_c, y4, merged, h1, fg, fu = saved[l]
        w = plan.weights(l)
        dh1, d_gate, d_up, d_down, st_ffn = run(bwd_ffn, "bwd_ffn", l, dx, h1, fg, fu, norm2_g, w["gate_t"], w["up_t"], w["down"])
        plan.new_grads("ffn", l, dict(gate_t=d_gate, up_t=d_up, down=d_down))
        dproj, dpre, d_o, d_a, d_b, d_c, d_d = run(
            bwd_merge, "bwd_merge", l, dh1, y4, proj, merged, pre_abd, pre_c, w["a_t"], w["b_t"], w["c_t"], w["d_t"], w["o"])
        plan.new_grads("out", l, dict(a_t=d_a, b_t=d_b, c_t=d_c, d_t=d_d, o=d_o))
        dq, dkc, dkp, st_attn = run(bwd_attn, "bwd_attn", l, proj, dpre, vecs)
        dproj, dcw, dvec, dwx, dwa = run(bwd_branch, "bwd_branch", l, proj, dproj, dpre, h, dq, dkc, dkp, convw, vecs, wx_bd, wa_bd)
        dx, d_in, st_proj = run(bwd_proj, "bwd_proj", l, dproj, x_in, dh1, norm1_g, w["in_t"])
        plan.new_grads("in", l, dict(in_t=d_in))
        small[l] = (st_proj, st_ffn, dvec, st_attn, dcw, dwx, dwa)
    return head_stats, dx, small


BIG = dict(in_t=("w_in", "view"), a_t=("w_a_out", "transpose"), b_t=("w_b_out", "transpose"), c_t=("w_c_out", "transpose"),
           d_t=("w_d_out", "transpose"), o=("w_o", "plain"), gate_t=("w_ffn_gate", "view"), up_t=("w_ffn_up", "view"),
           down=("w_ffn_down", "plain"))


def cast_transpose(w, name):
    nl, a, b = w.shape
    ta = min(256, a)

    def body(w_ref, o_ref):
        o_ref[...] = w_ref[...].T.astype(BF16)

    return pl.pallas_call(
        body, grid=(nl, a // ta),
        in_specs=[pl.BlockSpec((None, ta, b), lambda l, i: (l, i, 0))],
        out_specs=pl.BlockSpec((None, b, ta), lambda l, i: (l, 0, i)),
        out_shape=_sds((nl, b, a), BF16), compiler_params=_cparams(2), name=name)(w)


def add_partials(mine, recv, core, name):
    n = len(mine)

    def body(core_ref, *refs):
        del core_ref
        for a_ref, b_ref, o_ref in zip(refs[:n], refs[n:2 * n], refs[2 * n:]):
            o_ref[...] = (a_ref[...].astype(F32) + b_ref[...].astype(F32)).astype(BF16)

    return pl.pallas_call(
        body,
        grid_spec=pltpu.PrefetchScalarGridSpec(
            num_scalar_prefetch=1, grid=(4,),
            in_specs=[pl.BlockSpec((None, None) + a.shape[2:], lambda i, cr: (i, cr[0], 0, 0)) for a in mine]
            + [pl.BlockSpec((None,) + b.shape[1:], lambda i, cr: (i, 0, 0)) for b in recv],
            out_specs=[pl.BlockSpec((None,) + b.shape[1:], lambda i, cr: (i, 0, 0)) for b in recv]),
        out_shape=[_sds(b.shape, BF16) for b in recv], compiler_params=_cparams(1), name=name)(core, *mine, *recv)


def _adamw(w, g, m, v):
    m = ADAM_B1 * m + (1.0 - ADAM_B1) * g
    v = ADAM_B2 * v + (1.0 - ADAM_B2) * (g * g)
    m_hat = m / (1.0 - ADAM_B1 ** ADAM_STEP)
    v_hat = v / (1.0 - ADAM_B2 ** ADAM_STEP)
    delta = -ADAM_LR * (m_hat / (jnp.sqrt(v_hat) + ADAM_EPS) + ADAM_WD * w)
    return delta, m, v


def adamw_big(contrib, w, m, v, transposed, name, comm=None):
    nsrc, nl, rows, cols = contrib.shape
    ct = 256

    def body(c_ref, w_ref, m_ref, v_ref, g_out, d_out, m_out, v_out):
        g = c_ref[0].astype(F32)
        for src in range(1, nsrc):
            g = g + c_ref[src].astype(F32)
        if transposed:
            g = g.T
        delta, mn, vn = _adamw(w_ref[...], g, m_ref[...], v_ref[...])
        g_out[...] = g
        d_out[...] = delta
        m_out[...] = mn
        v_out[...] = vn

    if transposed:
        wspec = pl.BlockSpec((None, ct, rows), lambda l, j: (l, j, 0))
    else:
        wspec = pl.BlockSpec((None, rows, ct), lambda l, j: (l, 0, j))
    return _call(
        body, comm, (contrib, w, m, v), grid=(nl, cols // ct),
        in_specs=[pl.BlockSpec((nsrc, None, rows, ct), lambda l, j: (0, l, 0, j)), wspec, wspec, wspec],
        out_specs=[wspec] * 4, out_shape=[_sds(w.shape, F32)] * 4, name=name)


VEC_NAMES = ("conv_a_b", "lru_bx", "lru_ba", "lru_lambda", "conv_d_b", "ln_d_g", "ln_d_b")
P_N1, P_N2, P_VEC, P_CONV, P_LRU = 0, 1, 2, 6, 6 + CW_ROWS
P_LAYER = P_LRU + HD
P_FINAL, P_LOSS, P_ROWS = DEPTH * P_LAYER, DEPTH * P_LAYER + 1, 8 * ((DEPTH * P_LAYER + 2 + 7) // 8)
SMALL = ("norm1_g", "conv_a_w", "conv_a_b", "lru_wx", "lru_bx", "lru_wa", "lru_ba", "lru_lambda", "conv_b_w", "sinks",
         "conv_d_w", "conv_d_b", "ln_d_g", "ln_d_b", "norm2_g", "final_g")
VMEM_FULL = pl.BlockSpec(memory_space=pltpu.VMEM)


def _stack_vecs(p):
    rows = [p[n] for n in VEC_NAMES] + [jnp.pad(p["sinks"], ((0, 0), (0, BW - N_HEADS)))]
    return jnp.stack(rows, axis=1)


def _stack_convs(p):
    nl, _, ch = p["conv_a_w"].shape
    z = jnp.zeros((nl, 1, ch), F32)
    return jnp.concatenate([p["conv_a_w"], p["conv_b_w"], z, p["conv_d_w"], z], axis=1)


def _vec_place(l, r):
    return l * P_LAYER + P_VEC + r // 2, (r % 2) * BW


def pack_small(per_layer, head_stats):
    n = len(per_layer[0])

    def body(*refs):
        head_ref, pack = refs[DEPTH * n], refs[DEPTH * n + 1]
        pack[...] = jnp.zeros((P_ROWS, D), F32)
        lane = lax.broadcasted_iota(jnp.int32, (HD, BW), 1)
        for l in range(DEPTH):
            st_proj, st_ffn, dvec, st_attn, dcw, dwx, dwa = refs[l * n:(l + 1) * n]
            b = l * P_LAYER
            pack[b + P_N1:b + P_N1 + 1, :] = st_proj[0:1, :]
            pack[b + P_N2:b + P_N2 + 1, :] = st_ffn[0:1, :]
            for r in range(len(VEC_NAMES)):
                row, c0 = _vec_place(l, r)
                pack[row:row + 1, c0:c0 + BW] = dvec[r:r + 1, :]
            row, c0 = _vec_place(l, V_SINK)
            pack[row:row + 1, c0:c0 + 128] = st_attn[0:1, :]
            pack[b + P_CONV:b + P_CONV + CW_ROWS, 0:BW] = dcw[...]
            for mat, c0 in ((dwx, 0), (dwa, BW)):
                blocks = jnp.zeros((HD, BW), F32)
                for h in range(BW // HD):
                    blocks = jnp.where((lane >= HD * h) & (lane < HD * (h + 1)), mat[HD * h:HD * (h + 1), :], blocks)
                pack[b + P_LRU:b + P_LRU + HD, c0:c0 + BW] = blocks
        pack[P_FINAL:P_FINAL + 1, :] = head_ref[0:1, :]
        pack[P_LOSS:P_LOSS + 1, :] = head_ref[1:2, :]

    flat = [a for layer in per_layer for a in layer] + [head_stats]
    return pl.pallas_call(body, out_shape=_sds((P_ROWS, D), F32), in_specs=[VMEM_FULL] * len(flat), out_specs=VMEM_FULL,
                          name="pack_small", compiler_params=pltpu.CompilerParams(vmem_limit_bytes=VMEM_LIMIT))(*flat)


def adamw_small(gathered, me, w, m, v):
    ns = len(SMALL)

    def body(me_ref, c_ref, *refs):
        w_refs, m_refs, v_refs = refs[:ns], refs[ns:2 * ns], refs[2 * ns:3 * ns]
        loss_ref, outs, gs = refs[3 * ns], refs[3 * ns + 1:3 * ns + 1 + 4 * ns], refs[-1]
        gs[...] = c_ref[0]
        for dev in range(1, NDEV):
            gs[...] += c_ref[dev]
        loss_ref[...] = gs[P_LOSS:P_LOSS + 1, 0:128]

        def update(name, sel, g):
            i = SMALL.index(name)
            delta, mn, vn = _adamw(w_refs[i][sel], g, m_refs[i][sel], v_refs[i][sel])
            for o_ref, val in zip(outs[4 * i:4 * i + 4], (g, delta, mn, vn)):
                o_ref[sel] = val

        update("final_g", (slice(0, 1), slice(None)), gs[P_FINAL:P_FINAL + 1, :])
        shift = (BW - me_ref[0] * (BW // NDEV)) & (BW - 1)
        for l in range(DEPTH):
            b = l * P_LAYER
            row = (slice(l, l + 1), slice(None))
            update("norm1_g", row, gs[b + P_N1:b + P_N1 + 1, :])
            update("norm2_g", row, gs[b + P_N2:b + P_N2 + 1, :])
            for r, name in enumerate(VEC_NAMES):
                prow, c0 = _vec_place(l, r)
                update(name, row, gs[prow:prow + 1, c0:c0 + BW])
            prow, c0 = _vec_place(l, V_SINK)
            update("sinks", row, gs[prow:prow + 1, c0:c0 + N_HEADS])
            mine = pltpu.roll(gs[b + P_CONV:b + P_CONV + CW_ROWS, 0:BW], shift, 1)[:, 0:BW // NDEV]
            update("conv_a_w", (l,), mine[CW_A:CW_A + CONV_A])
            update("conv_b_w", (l,), mine[CW_B:CW_B + CONV_B])
            update("conv_d_w", (l,), mine[CW_D:CW_D + CONV_D])
            for h in range(BW // HD):
                update("lru_wx", (l, h), gs[b + P_LRU:b + P_LRU + HD, HD * h:HD * (h + 1)])
                update("lru_wa", (l, h), gs[b + P_LRU:b + P_LRU + HD, BW + HD * h:BW + HD * (h + 1)])

    args = [p[n] for p in (w, m, v) for n in SMALL]
    full = lambda a: pl.BlockSpec(a.shape, lambda i, me_ref: (0,) * a.ndim)
    out_shape = [_sds((1, 128), F32)] + [_sds(w[n].shape, F32) for n in SMALL for _ in range(4)]
    outs = pl.pallas_call(
        body,
        grid_spec=pltpu.PrefetchScalarGridSpec(
            num_scalar_prefetch=1, grid=(1,),
            in_specs=[full(gathered)] + [full(a) for a in args], out_specs=[full(o) for o in out_shape],
            scratch_shapes=[pltpu.VMEM((P_ROWS, D), F32)]),
        out_shape=out_shape, name="adamw_small", compiler_params=_cparams(1))(me, gathered, *args)
    return outs[0], {n: outs[1 + 4 * i:5 + 4 * i] for i, n in enumerate(SMALL)}


def merge_jobs(jobs):
    jobs = [j for j in jobs if j is not None]
    if not jobs:
        return None, []
    inputs, aliases, outs, sems, cuts = [], {}, [], [], []
    for j in jobs:
        i0, o0, s0 = len(inputs), len(outs), len(sems)
        aliases.update({i0 + i: o0 + o for i, o in j.aliases.items()})
        inputs += j.inputs
        outs += j.out_shapes
        sems += j.sem_shapes
        cuts.append((i0, len(inputs), o0, len(outs), s0, len(sems)))

    def each(which):
        def go(cins, couts, s):
            for j, (i0, i1, o0, o1, s0, s1) in zip(jobs, cuts):
                getattr(j, which)(cins[i0:i1], couts[o0:o1], s[s0:s1])
        return go

    return CommJob(inputs, aliases, outs, sems, each("start"), each("finish")), [(c[2], c[3]) for c in cuts]


SIXTHS = 6
OUT_KINDS = ("a_t", "b_t", "c_t", "d_t", "o")
GATHER_PLAN = {
    "fwd_proj": [(k, 0, 0, 6) for k in OUT_KINDS] + [("gate_t", 0, 0, 3)],
    "fwd_branch": [("gate_t", 0, 3, 6), ("up_t", 0, 0, 3)],
    "fwd_attn": [("up_t", 0, 3, 6), ("down", 0, 0, 6)],
    "fwd_merge": [("in_t", 1, 0, 2)],
    "fwd_ffn": [("in_t", 1, 2, 6)],
}
SIBLING_PLAN = {"bwd_merge": ("ffn", 0), "bwd_branch": ("out", 0), "bwd_ffn": ("in", 1), "adamw_down": ("in", 0)}
GROUPS = dict(ffn=("gate_t", "up_t", "down"), out=OUT_KINDS)
GROUPS["in"] = ("in_t",)
CHIP_PLAN = {
    "bwd_attn": [("in_t", 1, 3, 6), ("gate_t", 0, 0, 3)],
    "bwd_branch": [("gate_t", 0, 3, 6), ("up_t", 0, 0, 6), ("down", 0, 0, 6)],
    "bwd_proj": [(k, 0, 0, 6) for k in OUT_KINDS],
    "bwd_merge": [("in_t", 1, 0, 3)],
    "adamw_gate_t": [("in_t", 0, 0, 2)], "adamw_up_t": [("in_t", 0, 2, 4)], "adamw_o": [("in_t", 0, 4, 6)],
}
SMALL_GATHER_SLOT = "adamw_gate_t"


class Overlap:
    def __init__(self, shards, core):
        self.shards = shards
        self.core = core
        self.gathered = [dict.fromkeys(BIG) for _ in range(DEPTH)]
        self.views = {}
        self.partial = {}
        self.contrib = dict.fromkeys(BIG)
        self.small_pack = self.small_gathered = None
        self._open = None

    def weights(self, l):
        return self.gathered[l]

    def new_grads(self, group, l, grads):
        for k, g in grads.items():
            self.views[k, l] = g.reshape(4, 2, g.shape[0] // NDEV, g.shape[1])

    @staticmethod
    def _rows(shard_rows, f0, f1):
        return shard_rows * f0 // SIXTHS, shard_rows * (f1 - f0) // SIXTHS

    def job(self, slot, l):
        jobs, notes = [], []
        pieces = [(k, l + dl, f0, f1) for k, dl, f0, f1 in GATHER_PLAN.get(slot, []) if l + dl < DEPTH]
        if pieces:
            jobs.append(gather_job([((k, ll), self.shards[ll][k], self.gathered[ll][k],
                                     *self._rows(self.shards[ll][k].shape[0], f0, f1)) for k, ll, f0, f1 in pieces]))
            notes.append(("gather", list(dict.fromkeys((k, ll) for k, ll, _, _ in pieces))))
        if slot in SIBLING_PLAN and l + SIBLING_PLAN[slot][1] < DEPTH:
            group, dl = SIBLING_PLAN[slot]
            keys = [(k, l + dl) for k in GROUPS[group]]
            jobs.append(sibling_exchange_job([self.views[key] for key in keys]))
            notes.append(("sibling", keys))
        pieces = [(k, l + dl, f0, f1) for k, dl, f0, f1 in CHIP_PLAN.get(slot, []) if l + dl < DEPTH]
        if pieces:
            jobs.append(chip_exchange_job([(self.partial[k, ll], self.contrib[k], k, ll,
                                            *self._rows(self.partial[k, ll].shape[1], f0, f1)) for k, ll, f0, f1 in pieces]))
            notes.append(("chips", list(dict.fromkeys(k for k, _, _, _ in pieces))))
        if slot == SMALL_GATHER_SLOT:
            jobs.append(gather_job([("small", self.small_pack, None, 0, self.small_pack.shape[0])]))
            notes.append(("small", None))
        job, spans = merge_jobs(jobs)
        self._open = (slot, l, notes, spans)
        return job

    def done(self, slot, l, results):
        open_slot, open_l, notes, spans = self._open
        assert (open_slot, open_l) == (slot, l)
        for (what, keys), (r0, r1) in zip(notes, spans):
            res = results[r0:r1]
            if what == "gather":
                for (k, ll), g in zip(keys, res):
                    self.gathered[ll][k] = g
            elif what == "sibling":
                sums = add_partials([self.views[key] for key in keys], list(res), self.core, f"chip_sum_{keys[0][0]}{keys[0][1]}")
                self.partial.update(zip(keys, sums))
            elif what == "chips":
                for k, c in zip(keys, res):
                    self.contrib[k] = c
            else:
                self.small_gathered, = res


SMALL = ("norm1_g", "conv_a_w", "conv_a_b", "lru_wx", "lru_bx", "lru_wa", "lru_ba", "lru_lambda", "conv_b_w", "sinks",
         "conv_d_w", "conv_d_b", "ln_d_g", "ln_d_b", "norm2_g", "final_g")
WEIGHTS = ("norm1_g", "w_in", "conv_a_w", "conv_a_b", "lru_wx", "lru_bx", "lru_wa", "lru_ba", "lru_lambda", "w_a_out",
           "conv_b_w", "w_b_out", "sinks", "w_c_out", "conv_d_w", "conv_d_b", "ln_d_g", "ln_d_b", "w_d_out", "w_o",
           "norm2_g", "w_ffn_gate", "w_ffn_up", "w_ffn_down", "final_g")


def kernel(x, norm1_g, w_in, conv_a_w, conv_a_b, lru_wx, lru_bx, lru_wa, lru_ba, lru_lambda, w_a_out, conv_b_w, w_b_out, sinks, w_c_out, conv_d_w, conv_d_b, ln_d_g, ln_d_b, w_d_out, w_o, norm2_g, w_ffn_gate, w_ffn_up, w_ffn_down, final_g, loss_target, m_norm1_g, m_w_in, m_conv_a_w, m_conv_a_b, m_lru_wx, m_lru_bx, m_lru_wa, m_lru_ba, m_lru_lambda, m_w_a_out, m_conv_b_w, m_w_b_out, m_sinks, m_w_c_out, m_conv_d_w, m_conv_d_b, m_ln_d_g, m_ln_d_b, m_w_d_out, m_w_o, m_norm2_g, m_w_ffn_gate, m_w_ffn_up, m_w_ffn_down, m_final_g, v_norm1_g, v_w_in, v_conv_a_w, v_conv_a_b, v_lru_wx, v_lru_bx, v_lru_wa, v_lru_ba, v_lru_lambda, v_w_a_out, v_conv_b_w, v_w_b_out, v_sinks, v_w_c_out, v_conv_d_w, v_conv_d_b, v_ln_d_g, v_ln_d_b, v_w_d_out, v_w_o, v_norm2_g, v_w_ffn_gate, v_w_ffn_up, v_w_ffn_down, v_final_g):
    args = dict(locals())
    w = {n: args[n] for n in WEIGHTS}
    m = {n: args["m_" + n] for n in WEIGHTS}
    v = {n: args["v_" + n] for n in WEIGHTS}
    me = _dev_index(*_mesh_pos())

    def rows_major(a, how):
        return jnp.swapaxes(a, 1, 2) if how == "view" else a

    stacked = {k: cast_transpose(w[n], "prep_" + k) if how == "transpose" else rows_major(w[n], how).astype(BF16)
               for k, (n, how) in BIG.items()}
    plan = Overlap([{k: stacked[k][l] for k in BIG} for l in range(DEPTH)], lax.axis_index("c").astype(jnp.int32).reshape(1))
    convs = _stack_convs(w).reshape(DEPTH * CW_ROWS, BW // NDEV)
    g_in0, g_conv = _comm_only(gather_job([(("in_t", 0), plan.shards[0]["in_t"], None, 0, plan.shards[0]["in_t"].shape[0]),
                                           ("convs", convs, None, 0, convs.shape[0])]), "gather_first")
    plan.gathered[0]["in_t"] = g_in0
    convw = g_conv.reshape(NDEV, DEPTH, CW_ROWS, BW // NDEV).transpose(1, 2, 0, 3).reshape(DEPTH, CW_ROWS, BW)

    vecs = _stack_vecs(w)
    head_stats, grad_x, grads = local_step(x[0], loss_target[0], norm1_g, norm2_g, final_g, convw, vecs, lru_wx, lru_wa, plan)

    plan.small_pack = pack_small(grads, head_stats)

    out = {}
    for k in ("down", "gate_t", "up_t", "o", "a_t", "b_t", "c_t", "d_t", "in_t"):
        n, how = BIG[k]
        res, cres = adamw_big(plan.contrib[k], rows_major(w[n], how), rows_major(m[n], how), rows_major(v[n], how),
                              how == "transpose", "adamw_" + k, comm=plan.job("adamw_" + k, 0))
        plan.done("adamw_" + k, 0, cres)
        out[n] = [rows_major(r, how) for r in res]

    def own_shapes(p):
        return {n: p[n].reshape(1, D) if n == "final_g" else p[n] for n in SMALL}

    loss, small = adamw_small(plan.small_gathered.reshape(NDEV, P_ROWS, D), me.astype(jnp.int32).reshape(1),
                              own_shapes(w), own_shapes(m), own_shapes(v))
    for n in SMALL:
        out[n] = [r.reshape(w[n].shape) for r in small[n]]
    loss = loss[0, 0]
    return (loss, grad_x[None], *[out[n][0] for n in WEIGHTS], *[out[n][1] for n in WEIGHTS],
            *[out[n][2] for n in WEIGHTS], *[out[n][3] for n in WEIGHTS])
```

```python
import functools

import jax
import jax.numpy as jnp
from jax import lax
from jax.experimental import pallas as pl
from jax.experimental.pallas import tpu as pltpu

F32 = jnp.float32
BF16 = jnp.bfloat16
E = pl.Element

D = 1024
BW = 512
IN_W = 8448
GL0 = 4352
FF = 2816
N_HEADS = 8
N_KV = 2
HD = 64
ATT_BLK = 128
EPS = 1e-6
LRU_C = 8.0
NEG_INF = -1e30
DEPTH = 2
NDEV = 8
CONV_A, CONV_B, CONV_D = 4, 3, 31
C_AX, C_AG, C_BV, C_BC, C_BB, C_Q, C_K, C_V, C_D1, C_D2 = 0, 512, 1024, 1536, 2048, 2560, 3072, 3200, 3328, 3840
CW_A, CW_B, CW_D, CW_ROWS = 0, 4, 8, 40
V_CAB, V_BX, V_BA, V_LAM, V_CDB, V_LNG, V_LNB, V_SINK, V_ROWS = 0, 1, 2, 3, 4, 5, 6, 7, 8
HALO = 32

ADAM_LR, ADAM_B1, ADAM_B2, ADAM_EPS, ADAM_WD, ADAM_STEP = 0.001, 0.9, 0.999, 1e-08, 0.01, 10

VMEM_LIMIT = 56 * 1024 * 1024

_NN = (((1,), (0,)), ((), ()))
_NT = (((1,), (1,)), ((), ()))
_TN = (((0,), (0,)), ((), ()))


def _dot(a, b, dims):
    return lax.dot_general(a.astype(BF16), b.astype(BF16), dims, preferred_element_type=F32)


def _cparams(n_axes):
    return pltpu.CompilerParams(dimension_semantics=("arbitrary",) * n_axes, vmem_limit_bytes=VMEM_LIMIT)


def _sds(shape, dtype):
    return jax.ShapeDtypeStruct(tuple(shape), dtype)


def _sigmoid(x):
    return jax.nn.sigmoid(x)


def _neg_expm1(x):
    p = x * (1.0 + x * (0.5 + x * (1.0 / 6.0 + x * (1.0 / 24.0 + x * (1.0 / 120.0)))))
    return jnp.where(x > -0.1, -p, 1.0 - jnp.exp(x))


def _softplus(z):
    return jnp.maximum(z, 0.0) + jnp.log1p(jnp.exp(-jnp.abs(z)))


def _gelu_and_grad(x):
    c = 0.7978845608028654
    inner = c * (x + 0.044715 * x * x * x)
    t = jnp.tanh(inner)
    g = 0.5 * x * (1.0 + t)
    dg = 0.5 * (1.0 + t) + 0.5 * x * (1.0 - t * t) * c * (1.0 + 3.0 * 0.044715 * x * x)
    return g, dg


ANY = pl.BlockSpec(memory_space=pl.ANY)
MESH = pl.DeviceIdType.MESH


def _mesh_pos():
    return lax.axis_index("x"), lax.axis_index("y"), lax.axis_index("c")


def _dev_index(px, py, pc):
    return 4 * px + 2 * py + pc


class CommJob:
    def __init__(self, inputs, aliases, out_shapes, sem_shapes, start, finish):
        self.inputs, self.aliases, self.out_shapes, self.sem_shapes = list(inputs), dict(aliases), list(out_shapes), list(sem_shapes)
        self.start, self.finish = start, finish


def _call(body, comm, args, *, grid, in_specs, out_specs, out_shape, scratch_shapes=(), name, aliases=None):
    single = not isinstance(out_shape, (list, tuple))
    out_specs = [out_specs] if single else list(out_specs)
    out_shape = [out_shape] if single else list(out_shape)
    scratch_shapes = list(scratch_shapes)
    n_in, n_out, n_scr, n_axes = len(in_specs), len(out_shape), len(scratch_shapes), len(grid)
    params = pltpu.CompilerParams(dimension_semantics=("arbitrary",) * n_axes, vmem_limit_bytes=VMEM_LIMIT)
    io_aliases = dict(aliases or {})
    if comm is None:
        outs = pl.pallas_call(body, grid=grid, in_specs=in_specs, out_specs=out_specs, out_shape=out_shape,
                              scratch_shapes=scratch_shapes, input_output_aliases=io_aliases, compiler_params=params,
                              name=name)(*args)
        return (outs[0] if single else outs), []
    c_in, c_out = len(comm.inputs), len(comm.out_shapes)
    io_aliases.update({n_in + i: n_out + o for i, o in comm.aliases.items()})

    def wrapped(*refs):
        ins, cins = refs[:n_in], refs[n_in:n_in + c_in]
        outs = refs[n_in + c_in:n_in + c_in + n_out]
        couts = refs[n_in + c_in + n_out:n_in + c_in + n_out + c_out]
        rest = refs[n_in + c_in + n_out + c_out:]
        scr, sems = rest[:n_scr], rest[n_scr:]
        first = functools.reduce(lambda a, b: a & b, [pl.program_id(a) == 0 for a in range(n_axes)])
        last = functools.reduce(lambda a, b: a & b, [pl.program_id(a) == pl.num_programs(a) - 1 for a in range(n_axes)])

        @pl.when(first)
        def _():
            comm.start(cins, couts, sems)

        body(*ins, *outs, *scr)

        @pl.when(last)
        def _():
            comm.finish(cins, couts, sems)

    outs = pl.pallas_call(
        wrapped, grid=grid, in_specs=list(in_specs) + [ANY] * c_in, out_specs=out_specs + [ANY] * c_out,
        out_shape=out_shape + comm.out_shapes, scratch_shapes=scratch_shapes + comm.sem_shapes,
        input_output_aliases=io_aliases, compiler_params=params, name=name)(*args, *comm.inputs)
    res, cres = outs[:n_out], outs[n_out:]
    return (res[0] if single else res), cres


def _comm_only(comm, name):
    c_in, c_out = len(comm.inputs), len(comm.out_shapes)

    def body(*refs):
        cins, couts, sems = refs[:c_in], refs[c_in:c_in + c_out], refs[c_in + c_out:]
        comm.start(cins, couts, sems)
        comm.finish(cins, couts, sems)

    return pl.pallas_call(body, in_specs=[ANY] * c_in, out_specs=[ANY] * c_out, out_shape=comm.out_shapes,
                          scratch_shapes=comm.sem_shapes, input_output_aliases=comm.aliases, name=name)(*comm.inputs)


def gather_job(pieces):
    inputs, aliases, out_shapes, plan, where = [], {}, [], [], {}
    for key, shard, gathered, row0, nrows in pieces:
        if key not in where:
            where[key] = (len(inputs), len(out_shapes))
            inputs.append(shard)
            if gathered is not None:
                aliases[len(inputs)] = len(out_shapes)
                inputs.append(gathered)
            out_shapes.append(_sds((NDEV * shard.shape[0], shard.shape[1]), shard.dtype))
        plan.append((*where[key], shard.shape[0], row0, nrows))
    n = len(plan)

    def copies(cins, couts, sems):
        send_sems, recv_sems, local_sems = sems
        x, y, c = _mesh_pos()
        me, sibling = (x, y, c), (x, y, 1 - c)
        chips = [(1 - x, y), (x, 1 - y), (1 - x, 1 - y)]
        local, first, relay, recv_ici, recv_d2d = [], [], [], [], []
        for p, (i_shard, i_out, rows, row0, nrows) in enumerate(plan):
            src = cins[i_shard].at[pl.ds(row0, nrows), :]

            def slot(dev, i_out=i_out, rows=rows, row0=row0, nrows=nrows):
                return couts[i_out].at[pl.ds(_dev_index(*dev) * rows + row0, nrows), :]

            def copy(g, dev, to, src=None, p=p, slot=slot):
                return pltpu.make_async_remote_copy(
                    src_ref=slot(dev) if src is None else src, dst_ref=slot(dev),
                    send_sem=send_sems.at[g, p], recv_sem=recv_sems.at[g, p], device_id=to, device_id_type=MESH)

            local.append(pltpu.make_async_copy(src, slot(me), local_sems.at[p]))
            first.append(copy(0, me, sibling, src=src))
            recv_d2d.append(copy(0, sibling, me))
            for j, chip in enumerate(chips):
                first.append(copy(1 + j, me, (*chip, c), src=src))
                recv_ici.append(copy(1 + j, (*chip, c), me))
                relay.append(copy(4 + j, (*chip, c), sibling))
                recv_d2d.append(copy(4 + j, (*chip, 1 - c), me))
        return local, first, relay, recv_ici, recv_d2d

    def start(cins, couts, sems):
        local, first, _, _, _ = copies(cins, couts, sems)
        for cp in local + first:
            cp.start()

    def finish(cins, couts, sems):
        local, first, relay, recv_ici, recv_d2d = copies(cins, couts, sems)
        for cp in recv_ici:
            cp.wait_recv()
        for cp in relay:
            cp.start()
        for cp in recv_d2d:
            cp.wait_recv()
        for cp in first + relay:
            cp.wait_send()
        for cp in local:
            cp.wait()

    sem_shapes = [pltpu.SemaphoreType.DMA((7, n)), pltpu.SemaphoreType.DMA((7, n)), pltpu.SemaphoreType.DMA((n,))]
    return CommJob(inputs, aliases, out_shapes, sem_shapes, start, finish)


def sibling_exchange_job(grads):
    n = len(grads)

    def copies(cins, couts, sems):
        send_sems, recv_sems = sems
        x, y, c = _mesh_pos()
        return [pltpu.make_async_remote_copy(
            src_ref=cins[q].at[:, 1 - c], dst_ref=couts[q], send_sem=send_sems.at[q], recv_sem=recv_sems.at[q],
            device_id=(x, y, 1 - c), device_id_type=MESH) for q in range(n)]

    def start(cins, couts, sems):
        for cp in copies(cins, couts, sems):
            cp.start()

    def finish(cins, couts, sems):
        cps = copies(cins, couts, sems)
        for cp in cps:
            cp.wait_recv()
        for cp in cps:
            cp.wait_send()

    return CommJob(grads, {}, [_sds((4,) + g.shape[2:], g.dtype) for g in grads],
                   [pltpu.SemaphoreType.DMA((n,)), pltpu.SemaphoreType.DMA((n,))], start, finish)


def chip_exchange_job(pieces):
    inputs, aliases, out_shapes, plan, where = [], {}, [], [], {}
    for partial, contrib, key, layer, row0, nrows, col0, cols in pieces:
        if key not in where:
            where[key] = len(out_shapes)
            out_shapes.append(_sds((4, DEPTH, partial.shape[1], cols), partial.dtype))
            if contrib is not None:
                aliases[len(inputs)] = where[key]
                inputs.append(contrib)
        plan.append((len(inputs), where[key], layer, row0, nrows, col0, partial.shape[2]))
        inputs.append(partial)
    n = len(plan)

    def copies(cins, couts, sems):
        send_sems, recv_sems, local_sems = sems
        x, y, c = _mesh_pos()
        mine = 2 * x + y
        local, sends, recvs = [], [], []
        for p, (i_in, i_out, layer, row0, nrows, col0, ncols) in enumerate(plan):
            rows, lanes = pl.ds(row0, nrows), pl.ds(col0, ncols)
            local.append(pltpu.make_async_copy(cins[i_in].at[mine, rows, :], couts[i_out].at[mine, layer, rows, lanes],
                                               local_sems.at[p]))
            for j, (cx, cy) in enumerate([(1 - x, y), (x, 1 - y), (1 - x, 1 - y)]):
                theirs = 2 * cx + cy

                def copy(slot_there, j=j, p=p, cx=cx, cy=cy, theirs=theirs, i_in=i_in, i_out=i_out, layer=layer,
                         rows=rows, lanes=lanes):
                    return pltpu.make_async_remote_copy(
                        src_ref=cins[i_in].at[theirs, rows, :], dst_ref=couts[i_out].at[slot_there, layer, rows, lanes],
                        send_sem=send_sems.at[j, p], recv_sem=recv_sems.at[j, p], device_id=(cx, cy, c), device_id_type=MESH)
                sends.append(copy(mine))
                recvs.append(copy(theirs))
        return local, sends, recvs

    def start(cins, couts, sems):
        local, sends, _ = copies(cins, couts, sems)
        for cp in local + sends:
            cp.start()

    def finish(cins, couts, sems):
        local, sends, recvs = copies(cins, couts, sems)
        for cp in recvs:
            cp.wait_recv()
        for cp in sends:
            cp.wait_send()
        for cp in local:
            cp.wait()

    sem_shapes = [pltpu.SemaphoreType.DMA((3, n)), pltpu.SemaphoreType.DMA((3, n)), pltpu.SemaphoreType.DMA((n,))]
    return CommJob(inputs, aliases, out_shapes, sem_shapes, start, finish)


def fwd_proj(x, g1, wt_in, l, comm=None):
    s = x.shape[0]
    tm = min(512, s)
    tn = 1408

    def body(x_ref, g_ref, w_ref, o_ref, xn_ref):
        @pl.when(pl.program_id(1) == 0)
        def _():
            xv = x_ref[...]
            r = lax.rsqrt(jnp.mean(xv * xv, axis=-1, keepdims=True) + EPS)
            xn_ref[...] = (xv * r * g_ref[l:l + 1, :]).astype(BF16)

        o_ref[...] = _dot(xn_ref[...], w_ref[...], _NT).astype(BF16)

    return _call(
        body, comm, (x, g1, wt_in), grid=(s // tm, IN_W // tn),
        in_specs=[pl.BlockSpec((tm, D), lambda i, j: (i, 0)),
                  pl.BlockSpec((DEPTH, D), lambda i, j: (0, 0)),
                  pl.BlockSpec((tn, D), lambda i, j: (j, 0))],
        out_specs=pl.BlockSpec((tm, tn), lambda i, j: (i, j)),
        out_shape=_sds((s, IN_W), BF16),
        scratch_shapes=[pltpu.VMEM((tm, D), BF16)], name=f"fwd_proj{l}")


def _scan_fwd(a_ref, u_ref, h_ref, h0, n_rows):
    row = lax.broadcasted_iota(jnp.int32, (8, BW), 0)

    def body(g, hprev):
        r = pl.multiple_of(g * 8, 8)
        a = a_ref[pl.ds(r, 8), :]
        u = u_ref[pl.ds(r, 8), :]
        for sft in (1, 2, 4):
            a_sh = jnp.where(row >= sft, pltpu.roll(a, sft, 0), 1.0)
            u_sh = jnp.where(row >= sft, pltpu.roll(u, sft, 0), 0.0)
            u = u + a * u_sh
            a = a * a_sh
        h = u + a * hprev
        h_ref[pl.ds(r, 8), :] = h
        return h[7:8, :]

    return lax.fori_loop(0, n_rows // 8, body, h0)


def _scan_bwd(b_ref, g_ref, o_ref, c0, n_rows):
    row = lax.broadcasted_iota(jnp.int32, (8, BW), 0)

    def body(k, cnext):
        r = pl.multiple_of((n_rows // 8 - 1 - k) * 8, 8)
        b = b_ref[pl.ds(r, 8), :]
        g = g_ref[pl.ds(r, 8), :]
        for sft in (1, 2, 4):
            b_sh = jnp.where(row < 8 - sft, pltpu.roll(b, 8 - sft, 0), 1.0)
            g_sh = jnp.where(row < 8 - sft, pltpu.roll(g, 8 - sft, 0), 0.0)
            g = g + b * g_sh
            b = b * b_sh
        o = g + b * cnext
        o_ref[pl.ds(r, 8), :] = o
        return o[0:1, :]

    return lax.fori_loop(0, n_rows // 8, body, c0)


def _shifted_copies(buf, shifted, n_rows):
    for r in range(1, 8):
        shifted[r - 1, 0:n_rows - 8, :] = buf[pl.ds(r, n_rows - 8), :]


def _window(buf, shifted, off, t):
    r = off % 8
    return buf[pl.ds(off, t), :] if r == 0 else shifted[r - 1, pl.ds(off - r, t), :]


def _branch_fwd_math(cur_ref, halo_ref, cw_ref, vec_ref, wx_ref, wa_ref, bufa, bufb, bufd, xd, first, t):
    def halo(c0):
        v = halo_ref[:, c0:c0 + BW].astype(F32)
        return jnp.where(first, 0.0, v)

    def cur(c0):
        return cur_ref[:, c0:c0 + BW].astype(F32)

    out = {}
    bufa[0:HALO, :] = halo(C_AX)
    bufa[HALO:HALO + t, :] = cur(C_AX)
    ca = jnp.zeros((t, BW), F32) + vec_ref[V_CAB:V_CAB + 1, :]
    for k in range(CONV_A):
        ca = ca + cw_ref[CW_A + k:CW_A + k + 1, :] * bufa[pl.ds(HALO - (CONV_A - 1) + k, t), :]
    gi = _sigmoid(_dot(ca, wx_ref[...], _NN) + vec_ref[V_BX:V_BX + 1, :])
    gr = _sigmoid(_dot(ca, wa_ref[...], _NN) + vec_ref[V_BA:V_BA + 1, :])
    sp = _softplus(-vec_ref[V_LAM:V_LAM + 1, :])
    la = -LRU_C * sp * gr
    a = jnp.exp(la)
    mult = jnp.sqrt(_neg_expm1(2.0 * la))
    out.update(ca=ca, gi=gi, gr=gr, sp=sp, a=a, mult=mult)
    bufb[0:HALO, :] = halo(C_BC) * halo(C_BV)
    bufb[HALO:HALO + t, :] = cur(C_BC) * cur(C_BV)
    cb = jnp.zeros((t, BW), F32)
    for k in range(CONV_B):
        cb = cb + cw_ref[CW_B + k:CW_B + k + 1, :] * bufb[pl.ds(HALO - (CONV_B - 1) + k, t), :]
    out.update(cb=cb)
    bufd[0:HALO, :] = halo(C_D1) * _sigmoid(halo(C_D2))
    s2 = _sigmoid(cur(C_D2))
    bufd[HALO:HALO + t, :] = cur(C_D1) * s2
    _shifted_copies(bufd, xd, t + HALO)
    cd = jnp.zeros((t, BW), F32) + vec_ref[V_CDB:V_CDB + 1, :]
    for k in range(CONV_D):
        cd = cd + cw_ref[CW_D + k:CW_D + k + 1, :] * _window(bufd, xd, HALO - (CONV_D - 1) + k, t)
    mu = jnp.mean(cd, axis=-1, keepdims=True)
    xc = cd - mu
    rstd = lax.rsqrt(jnp.mean(xc * xc, axis=-1, keepdims=True) + EPS)
    xh = xc * rstd
    ln = xh * vec_ref[V_LNG:V_LNG + 1, :] + vec_ref[V_LNB:V_LNB + 1, :]
    out.update(s2=s2, xh=xh, rstd=rstd, ln=ln)
    return out


def fwd_branch(proj, convw, vecs, wx_bd, wa_bd, l, comm=None):
    s = proj.shape[0]
    t = min(256, s)

    def body(cur_ref, halo_ref, cw_ref, vec_ref, wx_ref, wa_ref, pre_ref, h_ref, bufa, bufb, bufd, xd, a_s, u_s, hcar):
        first = pl.program_id(0) == 0

        @pl.when(first)
        def _():
            hcar[...] = jnp.zeros((1, BW), F32)

        v = _branch_fwd_math(cur_ref, halo_ref, cw_ref, vec_ref, wx_ref, wa_ref, bufa, bufb, bufd, xd, first, t)
        a_s[...] = v["a"]
        u_s[...] = v["ca"] * v["gi"] * v["mult"]
        hcar[...] = _scan_fwd(a_s, u_s, h_ref, hcar[...], t)
        gg, _ = _gelu_and_grad(cur_ref[:, C_AG:C_AG + BW].astype(F32))
        pre_ref[:, 0:BW] = (h_ref[...] * gg).astype(BF16)
        pre_ref[:, BW:2 * BW] = (cur_ref[:, C_BB:C_BB + BW].astype(F32) * v["cb"]).astype(BF16)
        ln = v["ln"]
        pre_ref[:, 2 * BW:3 * BW] = (ln * _sigmoid(ln)).astype(BF16)

    hb = t // HALO
    return _call(
        body, comm, (proj, proj, convw, vecs, wx_bd, wa_bd), grid=(s // t,),
        in_specs=[pl.BlockSpec((t, GL0), lambda i: (i, 0)),
                  pl.BlockSpec((HALO, GL0), lambda i: (jnp.maximum(i * hb - 1, 0), 0)),
                  pl.BlockSpec((None, CW_ROWS, BW), lambda i: (l, 0, 0)),
                  pl.BlockSpec((None, V_ROWS, BW), lambda i: (l, 0, 0)),
                  pl.BlockSpec((None, BW, BW), lambda i: (l, 0, 0)),
                  pl.BlockSpec((None, BW, BW), lambda i: (l, 0, 0))],
        out_specs=[pl.BlockSpec((t, 3 * BW), lambda i: (i, 0)), pl.BlockSpec((t, BW), lambda i: (i, 0))],
        out_shape=[_sds((s, 3 * BW), BF16), _sds((s, BW), F32)],
        scratch_shapes=[pltpu.VMEM((t + HALO, BW), F32)] * 3 + [pltpu.VMEM((7, t + HALO - 8, BW), F32)]
        + [pltpu.VMEM((t, BW), F32)] * 2 + [pltpu.VMEM((1, BW), F32)],
        name=f"fwd_branch{l}")


GRP = N_HEADS // N_KV


def _attn_mask_bias(first_block):
    shape = (GRP * ATT_BLK, 2 * ATT_BLK)
    qi = lax.broadcasted_iota(jnp.int32, shape, 0) & (ATT_BLK - 1)
    ki = lax.broadcasted_iota(jnp.int32, shape, 1)
    dist = qi + ATT_BLK - ki
    valid = (dist >= 0) & (dist < ATT_BLK) & (jnp.logical_not(first_block) | (ki >= ATT_BLK))
    return dist.astype(F32), valid


def _per_head(hk, values):
    hl = lax.broadcasted_iota(jnp.int32, (GRP * ATT_BLK, 1), 0) // ATT_BLK
    out = values[GRP - 1]
    for j in range(GRP - 2, -1, -1):
        out = jnp.where(hl == j, values[j], out)
    return out


def _attn_probs(q_ref, kvp_ref, kvc_ref, vec_ref, distf, valid):
    kvs = range(N_KV)
    heads = [range(hk * GRP, (hk + 1) * GRP) for hk in kvs]
    q4 = [jnp.concatenate([q_ref[:, h * HD:(h + 1) * HD] for h in heads[hk]], axis=0) for hk in kvs]
    k2 = [jnp.concatenate([kvp_ref[:, hk * HD:(hk + 1) * HD], kvc_ref[:, hk * HD:(hk + 1) * HD]], axis=0) for hk in kvs]
    v2 = [jnp.concatenate([kvp_ref[:, (N_KV + hk) * HD:(N_KV + hk + 1) * HD],
                           kvc_ref[:, (N_KV + hk) * HD:(N_KV + hk + 1) * HD]], axis=0) for hk in kvs]
    slope = [_per_head(hk, [2.0 ** (-8.0 * (h + 1) / N_HEADS) for h in heads[hk]]) for hk in kvs]
    sink = [_per_head(hk, [vec_ref[V_SINK:V_SINK + 1, h:h + 1] for h in heads[hk]]) for hk in kvs]
    sc = [_dot(q4[hk], k2[hk], _NT) for hk in kvs]
    sc = [jnp.where(valid, sc[hk] * (HD ** -0.5) - slope[hk] * distf, NEG_INF) for hk in kvs]
    m = [jnp.maximum(jnp.max(sc[hk], axis=-1, keepdims=True), sink[hk]) for hk in kvs]
    p = [jnp.exp(sc[hk] - m[hk]) for hk in kvs]
    es = [jnp.exp(sink[hk] - m[hk]) for hk in kvs]
    inv = [1.0 / (jnp.sum(p[hk], axis=-1, keepdims=True) + es[hk]) for hk in kvs]
    return [(q4[hk], k2[hk], v2[hk], p[hk] * inv[hk], es[hk] * inv[hk]) for hk in kvs]


def fwd_attn(proj, vecs, l, comm=None):
    s = proj.shape[0]
    nb = s // ATT_BLK

    def body(q_ref, kvp_ref, kvc_ref, vec_ref, o_ref):
        distf, valid = _attn_mask_bias(pl.program_id(0) == 0)
        groups = _attn_probs(q_ref, kvp_ref, kvc_ref, vec_ref, distf, valid)
        outs = [_dot(p, v2, _NN).astype(BF16) for _, _, v2, p, _ in groups]
        for hk, out in enumerate(outs):
            for j in range(GRP):
                h = hk * GRP + j
                o_ref[:, h * HD:(h + 1) * HD] = out[j * ATT_BLK:(j + 1) * ATT_BLK]

    return _call(
        body, comm, (proj, proj, proj, vecs), grid=(nb,),
        in_specs=[pl.BlockSpec((ATT_BLK, BW), lambda i: (i, C_Q // BW)),
                  pl.BlockSpec((ATT_BLK, 256), lambda i: (jnp.maximum(i - 1, 0), C_K // 256)),
                  pl.BlockSpec((ATT_BLK, 256), lambda i: (i, C_K // 256)),
                  pl.BlockSpec((None, V_ROWS, BW), lambda i: (l, 0, 0))],
        out_specs=pl.BlockSpec((ATT_BLK, BW), lambda i: (i, 0)),
        out_shape=_sds((s, BW), BF16), name=f"fwd_attn{l}")


def fwd_merge(x, proj, pre_abd, pre_c, wt_a, wt_b, wt_c, wt_d, w_o, l, comm=None):
    s = x.shape[0]
    tm = min(256, s)

    def body(x_ref, gl_ref, pabd_ref, pc_ref, wa_ref, wb_ref, wc_ref, wd_ref, wo_ref, y_ref, mg_ref, h1_ref):
        pres = (pabd_ref[:, 0:BW], pabd_ref[:, BW:2 * BW], pc_ref[...], pabd_ref[:, 2 * BW:3 * BW])
        merged = jnp.zeros((tm, D), F32)
        for k, (pre, w_ref) in enumerate(zip(pres, (wa_ref, wb_ref, wc_ref, wd_ref))):
            yk = _dot(pre, w_ref[...], _NT)
            y_ref[:, k * D:(k + 1) * D] = yk.astype(BF16)
            merged = merged + _sigmoid(gl_ref[:, k * D:(k + 1) * D].astype(F32)) * yk
        mg_ref[...] = merged.astype(BF16)
        h1_ref[...] = x_ref[...] + _dot(merged, wo_ref[...], _NN)

    wspec = pl.BlockSpec((D, BW), lambda i: (0, 0))
    return _call(
        body, comm, (x, proj, pre_abd, pre_c, wt_a, wt_b, wt_c, wt_d, w_o), grid=(s // tm,),
        in_specs=[pl.BlockSpec((tm, D), lambda i: (i, 0)),
                  pl.BlockSpec((E(tm), E(4 * D)), lambda i: (i * tm, GL0)),
                  pl.BlockSpec((tm, 3 * BW), lambda i: (i, 0)),
                  pl.BlockSpec((tm, BW), lambda i: (i, 0)),
                  wspec, wspec, wspec, wspec,
                  pl.BlockSpec((D, D), lambda i: (0, 0))],
        out_specs=[pl.BlockSpec((tm, 4 * D), lambda i: (i, 0)), pl.BlockSpec((tm, D), lambda i: (i, 0)),
                   pl.BlockSpec((tm, D), lambda i: (i, 0))],
        out_shape=[_sds((s, 4 * D), BF16), _sds((s, D), BF16), _sds((s, D), F32)], name=f"fwd_merge{l}")


def fwd_ffn(h1, g2, wt_gate, wt_up, w_down, l, comm=None):
    s = h1.shape[0]
    tm = min(512, s)
    fc = FF // 2

    def body(h_ref, g_ref, wg_ref, wu_ref, wd_ref, xo_ref, fg_ref, fu_ref, hn_ref, acc_ref):
        j = pl.program_id(1)

        @pl.when(j == 0)
        def _():
            hv = h_ref[...]
            r = lax.rsqrt(jnp.mean(hv * hv, axis=-1, keepdims=True) + EPS)
            hn_ref[...] = (hv * r * g_ref[l:l + 1, :]).astype(BF16)
            acc_ref[...] = hv

        fg = _dot(hn_ref[...], wg_ref[...], _NT)
        fu = _dot(hn_ref[...], wu_ref[...], _NT)
        fg_ref[...] = fg.astype(BF16)
        fu_ref[...] = fu.astype(BF16)
        acc_ref[...] += _dot(fg * _sigmoid(fg) * fu, wd_ref[...], _NN)

        @pl.when(j == pl.num_programs(1) - 1)
        def _():
            xo_ref[...] = acc_ref[...]

    wspec = pl.BlockSpec((fc, D), lambda i, j: (j, 0))
    return _call(
        body, comm, (h1, g2, wt_gate, wt_up, w_down), grid=(s // tm, FF // fc),
        in_specs=[pl.BlockSpec((tm, D), lambda i, j: (i, 0)), pl.BlockSpec((DEPTH, D), lambda i, j: (0, 0)),
                  wspec, wspec, wspec],
        out_specs=[pl.BlockSpec((tm, D), lambda i, j: (i, 0)), pl.BlockSpec((tm, fc), lambda i, j: (i, j)),
                   pl.BlockSpec((tm, fc), lambda i, j: (i, j))],
        out_shape=[_sds((s, D), F32), _sds((s, FF), BF16), _sds((s, FF), BF16)],
        scratch_shapes=[pltpu.VMEM((tm, D), BF16), pltpu.VMEM((tm, D), F32)], name=f"fwd_ffn{l}")


def loss_head(x, gf, target):
    s = x.shape[0]
    tm = min(512, s)

    def body(x_ref, g_ref, t_ref, dx_ref, st_ref):
        @pl.when(pl.program_id(0) == 0)
        def _():
            st_ref[...] = jnp.zeros((8, D), F32)

        xv = x_ref[...]
        g = g_ref[...]
        r = lax.rsqrt(jnp.mean(xv * xv, axis=-1, keepdims=True) + EPS)
        n = xv * r
        err = n * g - t_ref[...]
        dy = err * (1.0 / D)
        dn = dy * g
        dx_ref[...] = r * (dn - n * jnp.mean(dn * n, axis=-1, keepdims=True))
        st_ref[0:1, :] += jnp.sum(dy * n, axis=0, keepdims=True)
        lsum = 0.5 * jnp.sum(jnp.mean(err * err, axis=-1, keepdims=True), axis=0, keepdims=True)
        st_ref[1:2, :] += jnp.broadcast_to(lsum, (1, D))

    return pl.pallas_call(
        body, grid=(s // tm,),
        in_specs=[pl.BlockSpec((tm, D), lambda i: (i, 0)), pl.BlockSpec((1, D), lambda i: (0, 0)),
                  pl.BlockSpec((tm, D), lambda i: (i, 0))],
        out_specs=[pl.BlockSpec((tm, D), lambda i: (i, 0)), pl.BlockSpec((8, D), lambda i: (0, 0))],
        out_shape=[_sds((s, D), F32), _sds((8, D), F32)],
        compiler_params=_cparams(1), name="loss_head")(x, gf, target)


def _edge_index(j, i, n_j, n_i):
    return jnp.where((j == 0) | (j == n_j - 1), i, n_i - 1)


def bwd_ffn(dxo, h1, fg, fu, g2, wt_gate, wt_up, w_down, l, comm=None):
    s = h1.shape[0]
    tm = min(512, s)
    fc = 256
    n_j, n_i = FF // fc, s // tm

    def body(dxo_ref, h_ref, fg_ref, fu_ref, g_ref, wg_ref, wu_ref, wd_ref,
             dh_ref, dwg_ref, dwu_ref, dwd_ref, st_ref, dhn, dxo_b, hn_b, ag, au, ad):
        j, i = pl.program_id(0), pl.program_id(1)
        rows = pl.ds(pl.multiple_of(i * tm, tm), tm)
        g = g_ref[l:l + 1, :]

        @pl.when(j == 0)
        def _():
            hv = h_ref[...]
            r = lax.rsqrt(jnp.mean(hv * hv, axis=-1, keepdims=True) + EPS)
            hn_b[rows, :] = (hv * r * g).astype(BF16)
            dxo_b[rows, :] = dxo_ref[...].astype(BF16)
            dhn[rows, :] = jnp.zeros((tm, D), F32)

        @pl.when((j == 0) & (i == 0))
        def _():
            st_ref[...] = jnp.zeros((8, D), F32)

        @pl.when(i == 0)
        def _():
            ag[...] = jnp.zeros((fc, D), F32)
            au[...] = jnp.zeros((fc, D), F32)
            ad[...] = jnp.zeros((fc, D), F32)

        fgv = fg_ref[...].astype(F32)
        fuv = fu_ref[...].astype(F32)
        sg = _sigmoid(fgv)
        sil = fgv * sg
        dxb = dxo_b[rows, :]
        hnb = hn_b[rows, :]
        d_act = _dot(dxb, wd_ref[...], _NT)
        ad[...] += _dot(sil * fuv, dxb, _TN)
        d_fg = (d_act * fuv * (sg * (1.0 + fgv * (1.0 - sg)))).astype(BF16)
        d_fu = (d_act * sil).astype(BF16)
        ag[...] += _dot(d_fg, hnb, _TN)
        au[...] += _dot(d_fu, hnb, _TN)
        dhn[rows, :] += _dot(d_fg, wg_ref[...], _NN) + _dot(d_fu, wu_ref[...], _NN)

        @pl.when(i == n_i - 1)
        def _():
            dwg_ref[...] = ag[...].astype(BF16)
            dwu_ref[...] = au[...].astype(BF16)
            dwd_ref[...] = ad[...].astype(BF16)

        @pl.when(j == n_j - 1)
        def _():
            hv = h_ref[...]
            r = lax.rsqrt(jnp.mean(hv * hv, axis=-1, keepdims=True) + EPS)
            n = hv * r
            dv = dhn[rows, :]
            dn = dv * g
            dh_ref[...] = dxo_ref[...] + r * (dn - n * jnp.mean(dn * n, axis=-1, keepdims=True))
            st_ref[0:1, :] += jnp.sum(dv * n, axis=0, keepdims=True)

    edge = lambda j, i: (_edge_index(j, i, n_j, n_i), 0)
    wspec = pl.BlockSpec((fc, D), lambda j, i: (j, 0))
    dwspec = pl.BlockSpec((fc, D), lambda j, i: (j, 0))
    return _call(
        body, comm, (dxo, h1, fg, fu, g2, wt_gate, wt_up, w_down), grid=(n_j, n_i),
        in_specs=[pl.BlockSpec((tm, D), edge), pl.BlockSpec((tm, D), edge),
                  pl.BlockSpec((tm, fc), lambda j, i: (i, j)), pl.BlockSpec((tm, fc), lambda j, i: (i, j)),
                  pl.BlockSpec((DEPTH, D), lambda j, i: (0, 0)), wspec, wspec, wspec],
        out_specs=[pl.BlockSpec((tm, D), lambda j, i: (jnp.where(j == n_j - 1, i, 0), 0)),
                   dwspec, dwspec, dwspec, pl.BlockSpec((8, D), lambda j, i: (0, 0))],
        out_shape=[_sds((s, D), F32), _sds((FF, D), BF16), _sds((FF, D), BF16), _sds((FF, D), BF16), _sds((8, D), F32)],
        scratch_shapes=[pltpu.VMEM((s, D), F32), pltpu.VMEM((s, D), BF16), pltpu.VMEM((s, D), BF16),
                        pltpu.VMEM((fc, D), F32), pltpu.VMEM((fc, D), F32), pltpu.VMEM((fc, D), F32)],
        name=f"bwd_ffn{l}")


def bwd_merge(dh1, y4, proj, merged, pre_abd, pre_c, wt_a, wt_b, wt_c, wt_d, w_o, l, comm=None):
    s = dh1.shape[0]
    tm = min(256, s)
    n_i = s // tm

    def body(dh_ref, y_ref, gl_ref, mg_ref, pabd_ref, pc_ref, wa_ref, wb_ref, wc_ref, wd_ref, wo_ref,
             dgl_ref, dpre_ref, dwo_ref, dwa_ref, dwb_ref, dwc_ref, dwd_ref, ao, aa, ab, ac, ad):
        i = pl.program_id(0)
        accs = (aa, ab, ac, ad)

        @pl.when(i == 0)
        def _():
            ao[...] = jnp.zeros((D, D), F32)
            for acc in accs:
                acc[...] = jnp.zeros((D, BW), F32)

        dhb = dh_ref[...].astype(BF16)
        dmg = _dot(dhb, wo_ref[...], _NT)
        ao[...] += _dot(mg_ref[...], dhb, _TN)
        pres = (pabd_ref[:, 0:BW], pabd_ref[:, BW:2 * BW], pc_ref[...], pabd_ref[:, 2 * BW:3 * BW])
        for k, (pre, w_ref, acc) in enumerate(zip(pres, (wa_ref, wb_ref, wc_ref, wd_ref), accs)):
            gk = _sigmoid(gl_ref[:, k * D:(k + 1) * D].astype(F32))
            yk = y_ref[:, k * D:(k + 1) * D].astype(F32)
            dgl_ref[:, k * D:(k + 1) * D] = (dmg * yk * gk * (1.0 - gk)).astype(BF16)
            dyk = (dmg * gk).astype(BF16)
            dpre_ref[:, k * BW:(k + 1) * BW] = _dot(dyk, w_ref[...], _NN).astype(BF16)
            acc[...] += _dot(dyk, pre, _TN)

        @pl.when(i == n_i - 1)
        def _():
            dwo_ref[...] = ao[...].astype(BF16)
            for o_ref, acc in zip((dwa_ref, dwb_ref, dwc_ref, dwd_ref), accs):
                o_ref[...] = acc[...].astype(BF16)

    wspec = pl.BlockSpec((D, BW), lambda i: (0, 0))
    dwspec = pl.BlockSpec((D, BW), lambda i: (0, 0))
    return _call(
        body, comm, (dh1, y4, proj, merged, pre_abd, pre_c, wt_a, wt_b, wt_c, wt_d, w_o), grid=(n_i,),
        in_specs=[pl.BlockSpec((tm, D), lambda i: (i, 0)),
                  pl.BlockSpec((tm, 4 * D), lambda i: (i, 0)),
                  pl.BlockSpec((E(tm), E(4 * D)), lambda i: (i * tm, GL0)),
                  pl.BlockSpec((tm, D), lambda i: (i, 0)),
                  pl.BlockSpec((tm, 3 * BW), lambda i: (i, 0)),
                  pl.BlockSpec((tm, BW), lambda i: (i, 0)),
                  wspec, wspec, wspec, wspec,
                  pl.BlockSpec((D, D), lambda i: (0, 0))],
        out_specs=[pl.BlockSpec((E(tm), E(4 * D)), lambda i: (i * tm, GL0)),
                   pl.BlockSpec((tm, 4 * BW), lambda i: (i, 0)),
                   pl.BlockSpec((D, D), lambda i: (0, 0)), dwspec, dwspec, dwspec, dwspec],
        out_shape=[_sds((s, IN_W), BF16), _sds((s, 4 * BW), BF16), _sds((D, D), BF16)] + [_sds((D, BW), BF16)] * 4,
        scratch_shapes=[pltpu.VMEM((D, D), F32)] + [pltpu.VMEM((D, BW), F32)] * 4, name=f"bwd_merge{l}")


def bwd_attn(proj, dpre, vecs, l, comm=None):
    s = proj.shape[0]
    nb = s // ATT_BLK
    grp = N_HEADS // N_KV

    def body(q_ref, kvp_ref, kvc_ref, do_ref, vec_ref, dq_ref, dkc_ref, dkp_ref, st_ref):
        @pl.when(pl.program_id(0) == 0)
        def _():
            st_ref[...] = jnp.zeros((8, 128), F32)

        distf, valid = _attn_mask_bias(pl.program_id(0) == 0)
        lane = lax.broadcasted_iota(jnp.int32, (1, 128), 1)
        dsink = jnp.zeros((1, 128), F32)
        groups = _attn_probs(q_ref, kvp_ref, kvc_ref, vec_ref, distf, valid)
        kvs = range(N_KV)
        do4s = [jnp.concatenate([do_ref[:, h * HD:(h + 1) * HD] for h in range(hk * grp, (hk + 1) * grp)], axis=0) for hk in kvs]
        dps = [_dot(do4s[hk], groups[hk][2], _NT) for hk in kvs]
        deltas = [jnp.sum(groups[hk][3] * dps[hk], axis=-1, keepdims=True) for hk in kvs]
        dss = [groups[hk][3] * (dps[hk] - deltas[hk]) * (HD ** -0.5) for hk in kvs]
        for hk in kvs:
            q4, k2, v2, p, ps = groups[hk]
            do4, delta, ds = do4s[hk], deltas[hk], dss[hk]
            dq4 = _dot(ds, k2, _NN).astype(BF16)
            dk2 = _dot(ds, q4, _TN)
            dv2 = _dot(p, do4, _TN)
            psd = ps * delta
            for j in range(grp):
                h = hk * grp + j
                rows = slice(j * ATT_BLK, (j + 1) * ATT_BLK)
                dq_ref[:, h * HD:(h + 1) * HD] = dq4[rows]
                dsink = dsink + jnp.where(lane == h, -jnp.sum(psd[rows], axis=0, keepdims=True), 0.0)
            dkp_ref[:, hk * HD:(hk + 1) * HD] = dk2[0:ATT_BLK].astype(BF16)
            dkc_ref[:, hk * HD:(hk + 1) * HD] = dk2[ATT_BLK:].astype(BF16)
            dkp_ref[:, (N_KV + hk) * HD:(N_KV + hk + 1) * HD] = dv2[0:ATT_BLK].astype(BF16)
            dkc_ref[:, (N_KV + hk) * HD:(N_KV + hk + 1) * HD] = dv2[ATT_BLK:].astype(BF16)
        st_ref[0:1, :] += dsink

    return _call(
        body, comm, (proj, proj, proj, dpre, vecs), grid=(nb,),
        in_specs=[pl.BlockSpec((ATT_BLK, BW), lambda i: (i, C_Q // BW)),
                  pl.BlockSpec((ATT_BLK, 256), lambda i: (jnp.maximum(i - 1, 0), C_K // 256)),
                  pl.BlockSpec((ATT_BLK, 256), lambda i: (i, C_K // 256)),
                  pl.BlockSpec((ATT_BLK, BW), lambda i: (i, 2)),
                  pl.BlockSpec((None, V_ROWS, BW), lambda i: (l, 0, 0))],
        out_specs=[pl.BlockSpec((ATT_BLK, BW), lambda i: (i, 0)), pl.BlockSpec((ATT_BLK, 256), lambda i: (i, 0)),
                   pl.BlockSpec((ATT_BLK, 256), lambda i: (i, 0)), pl.BlockSpec((8, 128), lambda i: (0, 0))],
        out_shape=[_sds((s, BW), BF16), _sds((s, 256), BF16), _sds((s, 256), BF16), _sds((8, 128), F32)],
        name=f"bwd_attn{l}")


def bwd_branch(proj, dproj, dpre, h, dq, dkc, dkp, convw, vecs, wx_bd, wa_bd, l, comm=None):
    s = proj.shape[0]
    t = 2 * ATT_BLK
    nt = s // t
    nb = s // ATT_BLK
    hb = t // HALO

    def body(cur_ref, halo_ref, dpre_ref, h_ref, hp_ref, dq_ref, dkc_ref, dkp1_ref, dkp2_ref,
             cw_ref, vec_ref, wx_ref, wa_ref, dproj_in, dp_ref, dcw_ref, dvec_ref, dwx_ref, dwa_ref,
             bufa, bufb, bufd, xd, xg, a_ext, hbuf, b_s, g_s, dh_s, ga, gb, gd, dhcar):
        del dproj_in
        step = pl.program_id(0)
        ti = nt - 1 - step
        first = ti == 0

        @pl.when(step == 0)
        def _():
            dcw_ref[...] = jnp.zeros((CW_ROWS, BW), F32)
            dvec_ref[...] = jnp.zeros((V_ROWS, BW), F32)
            dwx_ref[...] = jnp.zeros((BW, BW), F32)
            dwa_ref[...] = jnp.zeros((BW, BW), F32)
            dhcar[...] = jnp.zeros((1, BW), F32)
            a_ext[t:t + 8, :] = jnp.zeros((8, BW), F32)
            ga[t:t + 8, :] = jnp.zeros((8, BW), F32)
            gb[t:t + 8, :] = jnp.zeros((8, BW), F32)
            gd[t:t + HALO, :] = jnp.zeros((HALO, BW), F32)

        def cur(c0):
            return cur_ref[:, c0:c0 + BW].astype(F32)

        def rsum(v):
            return jnp.sum(v, axis=0, keepdims=True)

        def put(c0, v):
            dp_ref[:, c0:c0 + BW] = v.astype(BF16)

        v = _branch_fwd_math(cur_ref, halo_ref, cw_ref, vec_ref, wx_ref, wa_ref, bufa, bufb, bufd, xd, first, t)
        ca, gi, gr, sp, a, mult = v["ca"], v["gi"], v["gr"], v["sp"], v["a"], v["mult"]
        dpa = dpre_ref[:, 0:BW].astype(F32)
        gg, dgg = _gelu_and_grad(cur(C_AG))
        hv = h_ref[...]
        put(C_AG, dpa * hv * dgg)
        a_ext[0:t, :] = a
        b_s[...] = a_ext[pl.ds(1, t), :]
        g_s[...] = dpa * gg
        dhcar[...] = _scan_bwd(b_s, g_s, dh_s, dhcar[...], t)
        a_ext[t:t + 1, :] = a[0:1, :]
        dh = dh_s[...]
        hbuf[0:8, :] = jnp.where(first, 0.0, hp_ref[...])
        hbuf[8:8 + t, :] = hv
        da = dh * hbuf[pl.ds(7, t), :]
        d_ca = dh * gi * mult
        d_gi = dh * ca * mult
        d_mult = dh * ca * gi
        d_la = da * a - d_mult * (a * a) / mult
        lam = vec_ref[V_LAM:V_LAM + 1, :]
        dvec_ref[V_LAM:V_LAM + 1, :] += rsum(d_la * gr) * (LRU_C * _sigmoid(-lam))
        d_gr = d_la * (-LRU_C * sp)
        d_zr = d_gr * gr * (1.0 - gr)
        d_zi = d_gi * gi * (1.0 - gi)
        dvec_ref[V_BA:V_BA + 1, :] += rsum(d_zr)
        dvec_ref[V_BX:V_BX + 1, :] += rsum(d_zi)
        dwa_ref[...] += _dot(ca, d_zr, _TN)
        dwx_ref[...] += _dot(ca, d_zi, _TN)
        d_ca = d_ca + _dot(d_zi, wx_ref[...], _NT) + _dot(d_zr, wa_ref[...], _NT)
        dvec_ref[V_CAB:V_CAB + 1, :] += rsum(d_ca)
        ga[0:t, :] = d_ca
        d_ax = jnp.zeros((t, BW), F32)
        for k in range(CONV_A):
            d_ax = d_ax + cw_ref[CW_A + k:CW_A + k + 1, :] * ga[pl.ds(CONV_A - 1 - k, t), :]
            dcw_ref[CW_A + k:CW_A + k + 1, :] += rsum(d_ca * bufa[pl.ds(HALO - (CONV_A - 1) + k, t), :])
        ga[t:t + 8, :] = d_ca[0:8, :]
        put(C_AX, d_ax)
        dpb = dpre_ref[:, BW:2 * BW].astype(F32)
        put(C_BB, dpb * v["cb"])
        d_cb = dpb * cur(C_BB)
        gb[0:t, :] = d_cb
        d_cbin = jnp.zeros((t, BW), F32)
        for k in range(CONV_B):
            d_cbin = d_cbin + cw_ref[CW_B + k:CW_B + k + 1, :] * gb[pl.ds(CONV_B - 1 - k, t), :]
            dcw_ref[CW_B + k:CW_B + k + 1, :] += rsum(d_cb * bufb[pl.ds(HALO - (CONV_B - 1) + k, t), :])
        gb[t:t + 8, :] = d_cb[0:8, :]
        put(C_BC, d_cbin * cur(C_BV))
        put(C_BV, d_cbin * cur(C_BC))
        dpd = dpre_ref[:, 3 * BW:4 * BW].astype(F32)
        ln, xh, rstd, s2 = v["ln"], v["xh"], v["rstd"], v["s2"]
        sg = _sigmoid(ln)
        d_ln = dpd * sg * (1.0 + ln * (1.0 - sg))
        dvec_ref[V_LNG:V_LNG + 1, :] += rsum(d_ln * xh)
        dvec_ref[V_LNB:V_LNB + 1, :] += rsum(d_ln)
        d_xh = d_ln * vec_ref[V_LNG:V_LNG + 1, :]
        d_cd = rstd * (d_xh - jnp.mean(d_xh, axis=-1, keepdims=True)
                       - xh * jnp.mean(d_xh * xh, axis=-1, keepdims=True))
        dvec_ref[V_CDB:V_CDB + 1, :] += rsum(d_cd)
        gd[0:t, :] = d_cd
        _shifted_copies(gd, xg, t + HALO)
        d_dg = jnp.zeros((t, BW), F32)
        for k in range(CONV_D):
            d_dg = d_dg + cw_ref[CW_D + k:CW_D + k + 1, :] * _window(gd, xg, CONV_D - 1 - k, t)
            dcw_ref[CW_D + k:CW_D + k + 1, :] += rsum(d_cd * _window(bufd, xd, HALO - (CONV_D - 1) + k, t))
        gd[t:t + HALO, :] = d_cd[0:HALO, :]
        put(C_D1, d_dg * s2)
        put(C_D2, d_dg * cur(C_D1) * s2 * (1.0 - s2))
        dp_ref[:, C_Q:C_Q + BW] = dq_ref[...]
        dkp2 = jnp.where(step == 0, 0.0, dkp2_ref[...].astype(F32))
        dp_ref[0:ATT_BLK, C_K:C_K + 256] = (dkc_ref[0:ATT_BLK, :].astype(F32) + dkp1_ref[...].astype(F32)).astype(BF16)
        dp_ref[ATT_BLK:t, C_K:C_K + 256] = (dkc_ref[ATT_BLK:t, :].astype(F32) + dkp2).astype(BF16)

    rev = lambda i: nt - 1 - i
    full = lambda r, c: pl.BlockSpec((r, c), lambda i: (0, 0))
    return _call(
        body, comm, (proj, proj, dpre, h, h, dq, dkc, dkp, dkp, convw, vecs, wx_bd, wa_bd, dproj), grid=(nt,),
        in_specs=[pl.BlockSpec((t, GL0), lambda i: (rev(i), 0)),
                  pl.BlockSpec((HALO, GL0), lambda i: (jnp.maximum(rev(i) * hb - 1, 0), 0)),
                  pl.BlockSpec((t, 4 * BW), lambda i: (rev(i), 0)),
                  pl.BlockSpec((t, BW), lambda i: (rev(i), 0)),
                  pl.BlockSpec((8, BW), lambda i: (jnp.maximum(rev(i) * (t // 8) - 1, 0), 0)),
                  pl.BlockSpec((t, BW), lambda i: (rev(i), 0)),
                  pl.BlockSpec((t, 256), lambda i: (rev(i), 0)),
                  pl.BlockSpec((ATT_BLK, 256), lambda i: (2 * rev(i) + 1, 0)),
                  pl.BlockSpec((ATT_BLK, 256), lambda i: (jnp.minimum(2 * rev(i) + 2, nb - 1), 0)),
                  pl.BlockSpec((None, CW_ROWS, BW), lambda i: (l, 0, 0)),
                  pl.BlockSpec((None, V_ROWS, BW), lambda i: (l, 0, 0)),
                  pl.BlockSpec((None, BW, BW), lambda i: (l, 0, 0)),
                  pl.BlockSpec((None, BW, BW), lambda i: (l, 0, 0)),
                  pl.BlockSpec(memory_space=pl.ANY)],
        out_specs=[pl.BlockSpec((t, GL0), lambda i: (rev(i), 0)),
                   full(CW_ROWS, BW), full(V_ROWS, BW), full(BW, BW), full(BW, BW)],
        out_shape=[_sds((s, IN_W), BF16), _sds((CW_ROWS, BW), F32), _sds((V_ROWS, BW), F32),
                   _sds((BW, BW), F32), _sds((BW, BW), F32)],
        scratch_shapes=[pltpu.VMEM((t + HALO, BW), F32)] * 3 + [pltpu.VMEM((7, t + HALO - 8, BW), F32)] * 2
        + [pltpu.VMEM((t + 8, BW), F32), pltpu.VMEM((t + 8, BW), F32)]
        + [pltpu.VMEM((t, BW), F32)] * 3
        + [pltpu.VMEM((t + 8, BW), F32), pltpu.VMEM((t + 8, BW), F32), pltpu.VMEM((t + HALO, BW), F32),
           pltpu.VMEM((1, BW), F32)],
        aliases={13: 0}, name=f"bwd_branch{l}")


def bwd_proj(dproj, x, dh1, g1, wt_in, l, comm=None):
    s = x.shape[0]
    tm = min(512, s)
    ck = 1408
    n_j, n_i = IN_W // ck, s // tm

    def body(dp_ref, x_ref, dh_ref, g_ref, w_ref, dx_ref, dw_ref, st_ref, dxn, xn_b, acc):
        j, i = pl.program_id(0), pl.program_id(1)
        rows = pl.ds(pl.multiple_of(i * tm, tm), tm)
        g = g_ref[l:l + 1, :]

        @pl.when(j == 0)
        def _():
            xv = x_ref[...]
            r = lax.rsqrt(jnp.mean(xv * xv, axis=-1, keepdims=True) + EPS)
            xn_b[rows, :] = (xv * r * g).astype(BF16)
            dxn[rows, :] = jnp.zeros((tm, D), F32)

        @pl.when((j == 0) & (i == 0))
        def _():
            st_ref[...] = jnp.zeros((8, D), F32)

        @pl.when(i == 0)
        def _():
            acc[...] = jnp.zeros((ck, D), F32)

        dp = dp_ref[...]
        dxn[rows, :] += _dot(dp, w_ref[...], _NN)
        acc[...] += _dot(dp, xn_b[rows, :], _TN)

        @pl.when(i == n_i - 1)
        def _():
            dw_ref[...] = acc[...].astype(BF16)

        @pl.when(j == n_j - 1)
        def _():
            xv = x_ref[...]
            r = lax.rsqrt(jnp.mean(xv * xv, axis=-1, keepdims=True) + EPS)
            n = xv * r
            dv = dxn[rows, :]
            dn = dv * g
            dx_ref[...] = dh_ref[...] + r * (dn - n * jnp.mean(dn * n, axis=-1, keepdims=True))
            st_ref[0:1, :] += jnp.sum(dv * n, axis=0, keepdims=True)

    lastrow = lambda j, i: (jnp.where(j == n_j - 1, i, 0), 0)
    return _call(
        body, comm, (dproj, x, dh1, g1, wt_in), grid=(n_j, n_i),
        in_specs=[pl.BlockSpec((tm, ck), lambda j, i: (i, j)),
                  pl.BlockSpec((tm, D), lambda j, i: (_edge_index(j, i, n_j, n_i), 0)),
                  pl.BlockSpec((tm, D), lastrow),
                  pl.BlockSpec((DEPTH, D), lambda j, i: (0, 0)),
                  pl.BlockSpec((ck, D), lambda j, i: (j, 0))],
        out_specs=[pl.BlockSpec((tm, D), lastrow), pl.BlockSpec((ck, D), lambda j, i: (j, 0)),
                   pl.BlockSpec((8, D), lambda j, i: (0, 0))],
        out_shape=[_sds((s, D), F32), _sds((IN_W, D), BF16), _sds((8, D), F32)],
        scratch_shapes=[pltpu.VMEM((s, D), F32), pltpu.VMEM((s, D), BF16), pltpu.VMEM((ck, D), F32)],
        name=f"bwd_proj{l}")


def bwd_proj_w(dproj, x, g1, l, half, comm=None):
    s = x.shape[0]
    tm = min(512, s)
    ck = 1408
    hw = D // 2
    n_j, n_i = IN_W // ck, s // tm

    def body(dp_ref, x_ref, g_ref, dw_ref, xn_b, acc):
        j, i = pl.program_id(0), pl.program_id(1)
        rows = pl.ds(pl.multiple_of(i * tm, tm), tm)

        @pl.when(j == 0)
        def _():
            xv = x_ref[...]
            r = lax.rsqrt(jnp.mean(xv * xv, axis=-1, keepdims=True) + EPS)
            xn_b[rows, :] = (xv * r * g_ref[l:l + 1, :])[:, half * hw:(half + 1) * hw].astype(BF16)

        @pl.when(i == 0)
        def _():
            acc[...] = jnp.zeros((ck, hw), F32)

        acc[...] += _dot(dp_ref[...], xn_b[rows, :], _TN)

        @pl.when(i == n_i - 1)
        def _():
            dw_ref[...] = acc[...].astype(BF16)

    return _call(
        body, comm, (dproj, x, g1), grid=(n_j, n_i),
        in_specs=[pl.BlockSpec((tm, ck), lambda j, i: (i, j)),
                  pl.BlockSpec((tm, D), lambda j, i: (jnp.where(j == 0, i, n_i - 1), 0)),
                  pl.BlockSpec((DEPTH, D), lambda j, i: (0, 0))],
        out_specs=pl.BlockSpec((ck, hw), lambda j, i: (j, 0)),
        out_shape=_sds((IN_W, hw), BF16),
        scratch_shapes=[pltpu.VMEM((s, hw), BF16), pltpu.VMEM((ck, hw), F32)],
        name=f"bwd_proj_w{half}_{l}")


def bwd_proj_x(dproj, x, dh1, g1, wt_in, l, comm=None):
    s = x.shape[0]
    tm = min(512, s)
    ck = 1408
    n_j, n_i = IN_W // ck, s // tm

    def body(dp_ref, x_ref, dh_ref, g_ref, w_ref, dx_ref, st_ref, dxn):
        j, i = pl.program_id(0), pl.program_id(1)
        rows = pl.ds(pl.multiple_of(i * tm, tm), tm)
        g = g_ref[l:l + 1, :]

        @pl.when((j == 0) & (i == 0))
        def _():
            st_ref[...] = jnp.zeros((8, D), F32)

        part = _dot(dp_ref[...], w_ref[...], _NN)

        @pl.when(j == 0)
        def _():
            dxn[rows, :] = part

        @pl.when(j > 0)
        def _():
            dxn[rows, :] += part

        @pl.when(j == n_j - 1)
        def _():
            xv = x_ref[...]
            r = lax.rsqrt(jnp.mean(xv * xv, axis=-1, keepdims=True) + EPS)
            n = xv * r
            dv = dxn[rows, :]
            dn = dv * g
            dx_ref[...] = dh_ref[...] + r * (dn - n * jnp.mean(dn * n, axis=-1, keepdims=True))
            st_ref[0:1, :] += jnp.sum(dv * n, axis=0, keepdims=True)

    lastrow = lambda j, i: (jnp.where(j == n_j - 1, i, 0), 0)
    return _call(
        body, comm, (dproj, x, dh1, g1, wt_in), grid=(n_j, n_i),
        in_specs=[pl.BlockSpec((tm, ck), lambda j, i: (i, j)), pl.BlockSpec((tm, D), lastrow), pl.BlockSpec((tm, D), lastrow),
                  pl.BlockSpec((DEPTH, D), lambda j, i: (0, 0)), pl.BlockSpec((ck, D), lambda j, i: (j, 0))],
        out_specs=[pl.BlockSpec((tm, D), lastrow), pl.BlockSpec((8, D), lambda j, i: (0, 0))],
        out_shape=[_sds((s, D), F32), _sds((8, D), F32)],
        scratch_shapes=[pltpu.VMEM((s, D), F32)], name=f"bwd_proj_x{l}")


def _block_diag(w):
    nl, nb, bw, _ = w.shape
    eye = jnp.eye(nb, dtype=w.dtype)
    return jnp.einsum("lhij,hk->lhikj", w, eye).reshape(nl, nb * bw, nb * bw).astype(BF16)


class NoOverlap:
    def __init__(self, big):
        self.big = big

    def weights(self, l):
        return self.big[l]

    def job(self, slot, l):
        return None

    def done(self, slot, l, results):
        pass

    def new_grads(self, group, l, grads):
        pass


def local_step(x, target, norm1_g, norm2_g, final_g, convw, vecs, lru_wx, lru_wa, plan):
    wx_bd, wa_bd = _block_diag(lru_wx), _block_diag(lru_wa)

    def run(fn, slot, l, *args):
        res, cres = fn(*args, l, comm=plan.job(slot, l))
        plan.done(slot, l, cres)
        return res

    saved = []
    for l in range(DEPTH):
        proj = run(fwd_proj, "fwd_proj", l, x, norm1_g, plan.weights(l)["in_t"])
        pre_abd, h = run(fwd_branch, "fwd_branch", l, proj, convw, vecs, wx_bd, wa_bd)
        pre_c = run(fwd_attn, "fwd_attn", l, proj, vecs)
        w = plan.weights(l)
        y4, merged, h1 = run(fwd_merge, "fwd_merge", l, x, proj, pre_abd, pre_c, w["a_t"], w["b_t"], w["c_t"], w["d_t"], w["o"])
        w = plan.weights(l)
        x_out, fg, fu = run(fwd_ffn, "fwd_ffn", l, h1, norm2_g, w["gate_t"], w["up_t"], w["down"])
        saved.append((x, proj, pre_abd, h, pre_c, y4, merged, h1, fg, fu))
        x = x_out
    dx, head_stats = loss_head(x, final_g.reshape(1, D), target)
    small = [None] * DEPTH
    for l in reversed(range(DEPTH)):
        x_in, proj, pre_abd, h, pre_c, y4, merged, h1, fg, fu = saved[l]
        w = plan.weights(l)
        dh1, d_gate, d_up, d_down, st_ffn = run(bwd_ffn, "bwd_ffn", l, dx, h1, fg, fu, norm2_g, w["gate_t"], w["up_t"], w["down"])
        plan.new_grads("ffn", l, dict(gate_t=d_gate, up_t=d_up, down=d_down))
        dproj, dpre, d_o, d_a, d_b, d_c, d_d = run(
            bwd_merge, "bwd_merge", l, dh1, y4, proj, merged, pre_abd, pre_c, w["a_t"], w["b_t"], w["c_t"], w["d_t"], w["o"])
        plan.new_grads("out", l, dict(a_t=d_a, b_t=d_b, c_t=d_c, d_t=d_d, o=d_o))
        dq, dkc, dkp, st_attn = run(bwd_attn, "bwd_attn", l, proj, dpre, vecs)
        dproj, dcw, dvec, dwx, dwa = run(bwd_branch, "bwd_branch", l, proj, dproj, dpre, h, dq, dkc, dkp, convw, vecs, wx_bd, wa_bd)
        if l > 0:
            dx, d_in, st_proj = run(bwd_proj, "bwd_proj", l, dproj, x_in, dh1, norm1_g, w["in_t"])
            plan.new_grads("in", l, dict(in_t=d_in))
        else:
            for half, name in enumerate(("in_a", "in_b")):
                d_half = run(functools.partial(bwd_proj_w, half=half), f"bwd_proj_w{half}", l, dproj, x_in, norm1_g)
                plan.new_grads(name, l, {name: d_half})
            dx, st_proj = run(bwd_proj_x, "bwd_proj_x", l, dproj, x_in, dh1, norm1_g, w["in_t"])
        small[l] = (st_proj, st_ffn, dvec, st_attn, dcw, dwx, dwa)
    return head_stats, dx, small


BIG = dict(in_t=("w_in", "view"), a_t=("w_a_out", "transpose"), b_t=("w_b_out", "transpose"), c_t=("w_c_out", "transpose"),
           d_t=("w_d_out", "transpose"), o=("w_o", "plain"), gate_t=("w_ffn_gate", "view"), up_t=("w_ffn_up", "view"),
           down=("w_ffn_down", "plain"))


def cast_transpose(w, name):
    nl, a, b = w.shape
    ta = min(256, a)

    def body(w_ref, o_ref):
        o_ref[...] = w_ref[...].T.astype(BF16)

    return pl.pallas_call(
        body, grid=(nl, a // ta),
        in_specs=[pl.BlockSpec((None, ta, b), lambda l, i: (l, i, 0))],
        out_specs=pl.BlockSpec((None, b, ta), lambda l, i: (l, 0, i)),
        out_shape=_sds((nl, b, a), BF16), compiler_params=_cparams(2), name=name)(w)


def add_partials(mine, recv, core, name):
    n = len(mine)

    def body(core_ref, *refs):
        del core_ref
        for a_ref, b_ref, o_ref in zip(refs[:n], refs[n:2 * n], refs[2 * n:]):
            o_ref[...] = (a_ref[...].astype(F32) + b_ref[...].astype(F32)).astype(BF16)

    return pl.pallas_call(
        body,
        grid_spec=pltpu.PrefetchScalarGridSpec(
            num_scalar_prefetch=1, grid=(4,),
            in_specs=[pl.BlockSpec((None, None) + a.shape[2:], lambda i, cr: (i, cr[0], 0, 0)) for a in mine]
            + [pl.BlockSpec((None,) + b.shape[1:], lambda i, cr: (i, 0, 0)) for b in recv],
            out_specs=[pl.BlockSpec((None,) + b.shape[1:], lambda i, cr: (i, 0, 0)) for b in recv]),
        out_shape=[_sds(b.shape, BF16) for b in recv], compiler_params=_cparams(1), name=name)(core, *mine, *recv)


def _adamw(w, g, m, v):
    m = ADAM_B1 * m + (1.0 - ADAM_B1) * g
    v = ADAM_B2 * v + (1.0 - ADAM_B2) * (g * g)
    m_hat = m / (1.0 - ADAM_B1 ** ADAM_STEP)
    v_hat = v / (1.0 - ADAM_B2 ** ADAM_STEP)
    delta = -ADAM_LR * (m_hat / (jnp.sqrt(v_hat) + ADAM_EPS) + ADAM_WD * w)
    return delta, m, v


def adamw_big(contrib, w, m, v, transposed, name, comm=None):
    nsrc, nl, rows, cols = contrib.shape
    ct = 256

    def body(c_ref, w_ref, m_ref, v_ref, g_out, d_out, m_out, v_out):
        g = c_ref[0].astype(F32)
        for src in range(1, nsrc):
            g = g + c_ref[src].astype(F32)
        if transposed:
            g = g.T
        delta, mn, vn = _adamw(w_ref[...], g, m_ref[...], v_ref[...])
        g_out[...] = g
        d_out[...] = delta
        m_out[...] = mn
        v_out[...] = vn

    if transposed:
        wspec = pl.BlockSpec((None, ct, rows), lambda l, j: (l, j, 0))
    else:
        wspec = pl.BlockSpec((None, rows, ct), lambda l, j: (l, 0, j))
    return _call(
        body, comm, (contrib, w, m, v), grid=(nl, cols // ct),
        in_specs=[pl.BlockSpec((nsrc, None, rows, ct), lambda l, j: (0, l, 0, j)), wspec, wspec, wspec],
        out_specs=[wspec] * 4, out_shape=[_sds(w.shape, F32)] * 4, name=name)


VEC_NAMES = ("conv_a_b", "lru_bx", "lru_ba", "lru_lambda", "conv_d_b", "ln_d_g", "ln_d_b")
P_N1, P_N2, P_VEC, P_CONV, P_LRU = 0, 1, 2, 6, 6 + CW_ROWS
P_LAYER = P_LRU + HD
P_FINAL, P_LOSS, P_ROWS = DEPTH * P_LAYER, DEPTH * P_LAYER + 1, 8 * ((DEPTH * P_LAYER + 2 + 7) // 8)
SMALL = ("norm1_g", "conv_a_w", "conv_a_b", "lru_wx", "lru_bx", "lru_wa", "lru_ba", "lru_lambda", "conv_b_w", "sinks",
         "conv_d_w", "conv_d_b", "ln_d_g", "ln_d_b", "norm2_g", "final_g")
VMEM_FULL = pl.BlockSpec(memory_space=pltpu.VMEM)


def _stack_vecs(p):
    rows = [p[n] for n in VEC_NAMES] + [jnp.pad(p["sinks"], ((0, 0), (0, BW - N_HEADS)))]
    return jnp.stack(rows, axis=1)


def _stack_convs(p):
    nl, _, ch = p["conv_a_w"].shape
    z = jnp.zeros((nl, 1, ch), F32)
    return jnp.concatenate([p["conv_a_w"], p["conv_b_w"], z, p["conv_d_w"], z], axis=1)


def _vec_place(l, r):
    return l * P_LAYER + P_VEC + r // 2, (r % 2) * BW


def pack_small(per_layer, head_stats):
    n = len(per_layer[0])

    def body(*refs):
        head_ref, pack = refs[DEPTH * n], refs[DEPTH * n + 1]
        pack[...] = jnp.zeros((P_ROWS, D), F32)
        lane = lax.broadcasted_iota(jnp.int32, (HD, BW), 1)
        for l in range(DEPTH):
            st_proj, st_ffn, dvec, st_attn, dcw, dwx, dwa = refs[l * n:(l + 1) * n]
            b = l * P_LAYER
            pack[b + P_N1:b + P_N1 + 1, :] = st_proj[0:1, :]
            pack[b + P_N2:b + P_N2 + 1, :] = st_ffn[0:1, :]
            for r in range(len(VEC_NAMES)):
                row, c0 = _vec_place(l, r)
                pack[row:row + 1, c0:c0 + BW] = dvec[r:r + 1, :]
            row, c0 = _vec_place(l, V_SINK)
            pack[row:row + 1, c0:c0 + 128] = st_attn[0:1, :]
            pack[b + P_CONV:b + P_CONV + CW_ROWS, 0:BW] = dcw[...]
            for mat, c0 in ((dwx, 0), (dwa, BW)):
                blocks = jnp.zeros((HD, BW), F32)
                for h in range(BW // HD):
                    blocks = jnp.where((lane >= HD * h) & (lane < HD * (h + 1)), mat[HD * h:HD * (h + 1), :], blocks)
                pack[b + P_LRU:b + P_LRU + HD, c0:c0 + BW] = blocks
        pack[P_FINAL:P_FINAL + 1, :] = head_ref[0:1, :]
        pack[P_LOSS:P_LOSS + 1, :] = head_ref[1:2, :]

    flat = [a for layer in per_layer for a in layer] + [head_stats]
    return pl.pallas_call(body, out_shape=_sds((P_ROWS, D), F32), in_specs=[VMEM_FULL] * len(flat), out_specs=VMEM_FULL,
                          name="pack_small", compiler_params=pltpu.CompilerParams(vmem_limit_bytes=VMEM_LIMIT))(*flat)


def adamw_small(gathered, me, w, m, v):
    ns = len(SMALL)

    def body(me_ref, c_ref, *refs):
        w_refs, m_refs, v_refs = refs[:ns], refs[ns:2 * ns], refs[2 * ns:3 * ns]
        loss_ref, outs, gs = refs[3 * ns], refs[3 * ns + 1:3 * ns + 1 + 4 * ns], refs[-1]
        gs[...] = c_ref[0]
        for dev in range(1, NDEV):
            gs[...] += c_ref[dev]
        loss_ref[...] = gs[P_LOSS:P_LOSS + 1, 0:128]

        def update(name, sel, g):
            i = SMALL.index(name)
            delta, mn, vn = _adamw(w_refs[i][sel], g, m_refs[i][sel], v_refs[i][sel])
            for o_ref, val in zip(outs[4 * i:4 * i + 4], (g, delta, mn, vn)):
                o_ref[sel] = val

        update("final_g", (slice(0, 1), slice(None)), gs[P_FINAL:P_FINAL + 1, :])
        shift = (BW - me_ref[0] * (BW // NDEV)) & (BW - 1)
        for l in range(DEPTH):
            b = l * P_LAYER
            row = (slice(l, l + 1), slice(None))
            update("norm1_g", row, gs[b + P_N1:b + P_N1 + 1, :])
            update("norm2_g", row, gs[b + P_N2:b + P_N2 + 1, :])
            for r, name in enumerate(VEC_NAMES):
                prow, c0 = _vec_place(l, r)
                update(name, row, gs[prow:prow + 1, c0:c0 + BW])
            prow, c0 = _vec_place(l, V_SINK)
            update("sinks", row, gs[prow:prow + 1, c0:c0 + N_HEADS])
            mine = pltpu.roll(gs[b + P_CONV:b + P_CONV + CW_ROWS, 0:BW], shift, 1)[:, 0:BW // NDEV]
            update("conv_a_w", (l,), mine[CW_A:CW_A + CONV_A])
            update("conv_b_w", (l,), mine[CW_B:CW_B + CONV_B])
            update("conv_d_w", (l,), mine[CW_D:CW_D + CONV_D])
            for h in range(BW // HD):
                update("lru_wx", (l, h), gs[b + P_LRU:b + P_LRU + HD, HD * h:HD * (h + 1)])
                update("lru_wa", (l, h), gs[b + P_LRU:b + P_LRU + HD, BW + HD * h:BW + HD * (h + 1)])

    args = [p[n] for p in (w, m, v) for n in SMALL]
    full = lambda a: pl.BlockSpec(a.shape, lambda i, me_ref: (0,) * a.ndim)
    out_shape = [_sds((1, 128), F32)] + [_sds(w[n].shape, F32) for n in SMALL for _ in range(4)]
    outs = pl.pallas_call(
        body,
        grid_spec=pltpu.PrefetchScalarGridSpec(
            num_scalar_prefetch=1, grid=(1,),
            in_specs=[full(gathered)] + [full(a) for a in args], out_specs=[full(o) for o in out_shape],
            scratch_shapes=[pltpu.VMEM((P_ROWS, D), F32)]),
        out_shape=out_shape, name="adamw_small", compiler_params=_cparams(1))(me, gathered, *args)
    return outs[0], {n: outs[1 + 4 * i:5 + 4 * i] for i, n in enumerate(SMALL)}


def merge_jobs(jobs):
    jobs = [j for j in jobs if j is not None]
    if not jobs:
        return None, []
    inputs, aliases, outs, sems, cuts = [], {}, [], [], []
    for j in jobs:
        i0, o0, s0 = len(inputs), len(outs), len(sems)
        aliases.update({i0 + i: o0 + o for i, o in j.aliases.items()})
        inputs += j.inputs
        outs += j.out_shapes
        sems += j.sem_shapes
        cuts.append((i0, len(inputs), o0, len(outs), s0, len(sems)))

    def each(which):
        def go(cins, couts, s):
            for j, (i0, i1, o0, o1, s0, s1) in zip(jobs, cuts):
                getattr(j, which)(cins[i0:i1], couts[o0:o1], s[s0:s1])
        return go

    return CommJob(inputs, aliases, outs, sems, each("start"), each("finish")), [(c[2], c[3]) for c in cuts]


SIXTHS = 6
OUT_KINDS = ("a_t", "b_t", "c_t", "d_t", "o")
GATHER_PLAN = {
    "fwd_proj": [(k, 0, 0, 6) for k in OUT_KINDS] + [("gate_t", 0, 0, 3)],
    "fwd_branch": [("gate_t", 0, 3, 6), ("up_t", 0, 0, 3)],
    "fwd_attn": [("up_t", 0, 3, 6), ("down", 0, 0, 6)],
    "fwd_merge": [("in_t", 1, 0, 2)],
    "fwd_ffn": [("in_t", 1, 2, 6)],
}
SIBLING_PLAN = {"bwd_merge": ("ffn", 0), "bwd_branch": ("out", 0), "bwd_ffn": ("in", 1),
                "bwd_proj_w1": ("in_a", 0), "bwd_proj_x": ("in_b", 0)}
GROUPS = dict(ffn=("gate_t", "up_t", "down"), out=OUT_KINDS, in_a=("in_a",), in_b=("in_b",))
GROUPS["in"] = ("in_t",)
COLUMN_HALF = dict(in_a=("in_t", 0), in_b=("in_t", D // 2))
CHIP_PLAN = {
    "bwd_attn": [("in_t", 1, 3, 6), ("gate_t", 0, 0, 3)],
    "bwd_branch": [("gate_t", 0, 3, 6), ("up_t", 0, 0, 6), ("down", 0, 0, 6)],
    "bwd_proj": [(k, 0, 0, 6) for k in OUT_KINDS],
    "bwd_proj_w0": [(k, 0, 0, 6) for k in OUT_KINDS[:3]],
    "bwd_proj_w1": [(k, 0, 0, 6) for k in OUT_KINDS[3:]],
    "bwd_merge": [("in_t", 1, 0, 3)],
    "bwd_proj_x": [("in_a", 0, 0, 6)],
    "adamw_gate_t": [("in_b", 0, 0, 3)], "adamw_up_t": [("in_b", 0, 3, 6)],
}
SMALL_GATHER_SLOT = "adamw_down"


class Overlap:
    def __init__(self, shards, core):
        self.shards = shards
        self.core = core
        self.gathered = [dict.fromkeys(BIG) for _ in range(DEPTH)]
        self.views = {}
        self.partial = {}
        self.contrib = dict.fromkeys(BIG)
        self.small_pack = self.small_gathered = None
        self._open = None

    def weights(self, l):
        return self.gathered[l]

    def new_grads(self, group, l, grads):
        for k, g in grads.items():
            self.views[k, l] = g.reshape(4, 2, g.shape[0] // NDEV, g.shape[1])

    @staticmethod
    def _rows(shard_rows, f0, f1):
        return shard_rows * f0 // SIXTHS, shard_rows * (f1 - f0) // SIXTHS

    def job(self, slot, l):
        jobs, notes = [], []
        pieces = [(k, l + dl, f0, f1) for k, dl, f0, f1 in GATHER_PLAN.get(slot, []) if l + dl < DEPTH]
        if pieces:
            jobs.append(gather_job([((k, ll), self.shards[ll][k], self.gathered[ll][k],
                                     *self._rows(self.shards[ll][k].shape[0], f0, f1)) for k, ll, f0, f1 in pieces]))
            notes.append(("gather", list(dict.fromkeys((k, ll) for k, ll, _, _ in pieces))))
        if slot in SIBLING_PLAN and l + SIBLING_PLAN[slot][1] < DEPTH:
            group, dl = SIBLING_PLAN[slot]
            keys = [(k, l + dl) for k in GROUPS[group]]
            jobs.append(sibling_exchange_job([self.views[key] for key in keys]))
            notes.append(("sibling", keys))
        pieces = [(k, l + dl, f0, f1) for k, dl, f0, f1 in CHIP_PLAN.get(slot, []) if l + dl < DEPTH]
        if pieces:
            whole = [(*COLUMN_HALF.get(k, (k, 0)), k, ll, f0, f1) for k, ll, f0, f1 in pieces]
            jobs.append(chip_exchange_job([(self.partial[k, ll], self.contrib[kind], kind, ll,
                                            *self._rows(self.partial[k, ll].shape[1], f0, f1), col0, self.shards[ll][kind].shape[1])
                                           for kind, col0, k, ll, f0, f1 in whole]))
            notes.append(("chips", list(dict.fromkeys(kind for kind, *_ in whole))))
        if slot == SMALL_GATHER_SLOT:
            jobs.append(gather_job([("small", self.small_pack, None, 0, self.small_pack.shape[0])]))
            notes.append(("small", None))
        job, spans = merge_jobs(jobs)
        self._open = (slot, l, notes, spans)
        return job

    def done(self, slot, l, results):
        open_slot, open_l, notes, spans = self._open
        assert (open_slot, open_l) == (slot, l)
        for (what, keys), (r0, r1) in zip(notes, spans):
            res = results[r0:r1]
            if what == "gather":
                for (k, ll), g in zip(keys, res):
                    self.gathered[ll][k] = g
            elif what == "sibling":
                sums = add_partials([self.views[key] for key in keys], list(res), self.core, f"chip_sum_{keys[0][0]}{keys[0][1]}")
                self.partial.update(zip(keys, sums))
            elif what == "chips":
                for k, c in zip(keys, res):
                    self.contrib[k] = c
            else:
                self.small_gathered, = res


SMALL = ("norm1_g", "conv_a_w", "conv_a_b", "lru_wx", "lru_bx", "lru_wa", "lru_ba", "lru_lambda", "conv_b_w", "sinks",
         "conv_d_w", "conv_d_b", "ln_d_g", "ln_d_b", "norm2_g", "final_g")
WEIGHTS = ("norm1_g", "w_in", "conv_a_w", "conv_a_b", "lru_wx", "lru_bx", "lru_wa", "lru_ba", "lru_lambda", "w_a_out",
           "conv_b_w", "w_b_out", "sinks", "w_c_out", "conv_d_w", "conv_d_b", "ln_d_g", "ln_d_b", "w_d_out", "w_o",
           "norm2_g", "w_ffn_gate", "w_ffn_up", "w_ffn_down", "final_g")


def kernel(x, norm1_g, w_in, conv_a_w, conv_a_b, lru_wx, lru_bx, lru_wa, lru_ba, lru_lambda, w_a_out, conv_b_w, w_b_out, sinks, w_c_out, conv_d_w, conv_d_b, ln_d_g, ln_d_b, w_d_out, w_o, norm2_g, w_ffn_gate, w_ffn_up, w_ffn_down, final_g, loss_target, m_norm1_g, m_w_in, m_conv_a_w, m_conv_a_b, m_lru_wx, m_lru_bx, m_lru_wa, m_lru_ba, m_lru_lambda, m_w_a_out, m_conv_b_w, m_w_b_out, m_sinks, m_w_c_out, m_conv_d_w, m_conv_d_b, m_ln_d_g, m_ln_d_b, m_w_d_out, m_w_o, m_norm2_g, m_w_ffn_gate, m_w_ffn_up, m_w_ffn_down, m_final_g, v_norm1_g, v_w_in, v_conv_a_w, v_conv_a_b, v_lru_wx, v_lru_bx, v_lru_wa, v_lru_ba, v_lru_lambda, v_w_a_out, v_conv_b_w, v_w_b_out, v_sinks, v_w_c_out, v_conv_d_w, v_conv_d_b, v_ln_d_g, v_ln_d_b, v_w_d_out, v_w_o, v_norm2_g, v_w_ffn_gate, v_w_ffn_up, v_w_ffn_down, v_final_g):
    args = dict(locals())
    w = {n: args[n] for n in WEIGHTS}
    m = {n: args["m_" + n] for n in WEIGHTS}
    v = {n: args["v_" + n] for n in WEIGHTS}
    me = _dev_index(*_mesh_pos())

    def rows_major(a, how):
        return jnp.swapaxes(a, 1, 2) if how == "view" else a

    stacked = {k: cast_transpose(w[n], "prep_" + k) if how == "transpose" else rows_major(w[n], how).astype(BF16)
               for k, (n, how) in BIG.items()}
    plan = Overlap([{k: stacked[k][l] for k in BIG} for l in range(DEPTH)], lax.axis_index("c").astype(jnp.int32).reshape(1))
    convs = _stack_convs(w).reshape(DEPTH * CW_ROWS, BW // NDEV)
    g_in0, g_conv = _comm_only(gather_job([(("in_t", 0), plan.shards[0]["in_t"], None, 0, plan.shards[0]["in_t"].shape[0]),
                                           ("convs", convs, None, 0, convs.shape[0])]), "gather_first")
    plan.gathered[0]["in_t"] = g_in0
    convw = g_conv.reshape(NDEV, DEPTH, CW_ROWS, BW // NDEV).transpose(1, 2, 0, 3).reshape(DEPTH, CW_ROWS, BW)

    vecs = _stack_vecs(w)
    head_stats, grad_x, grads = local_step(x[0], loss_target[0], norm1_g, norm2_g, final_g, convw, vecs, lru_wx, lru_wa, plan)

    plan.small_pack = pack_small(grads, head_stats)

    out = {}
    for k in ("down", "gate_t", "up_t", "o", "a_t", "b_t", "c_t", "d_t", "in_t"):
        n, how = BIG[k]
        res, cres = adamw_big(plan.contrib[k], rows_major(w[n], how), rows_major(m[n], how), rows_major(v[n], how),
                              how == "transpose", "adamw_" + k, comm=plan.job("adamw_" + k, 0))
        plan.done("adamw_" + k, 0, cres)
        out[n] = [rows_major(r, how) for r in res]

    def own_shapes(p):
        return {n: p[n].reshape(1, D) if n == "final_g" else p[n] for n in SMALL}

    loss, small = adamw_small(plan.small_gathered.reshape(NDEV, P_ROWS, D), me.astype(jnp.int32).reshape(1),
                              own_shapes(w), own_shapes(m), own_shapes(v))
    for n in SMALL:
        out[n] = [r.reshape(w[n].shape) for r in small[n]]
    loss = loss[0, 0]
    return (loss, grad_x[None], *[out[n][0] for n in WEIGHTS], *[out[n][1] for n in WEIGHTS],
            *[out[n][2] for n in WEIGHTS], *[out[n][3] for n in WEIGHTS])
```

```python
import functools

import jax
import jax.numpy as jnp
from jax import lax
from jax.experimental import pallas as pl
from jax.experimental.pallas import tpu as pltpu

F32 = jnp.float32
BF16 = jnp.bfloat16
E = pl.Element

D = 1024
BW = 512
IN_W = 8448
GL0 = 4352
FF = 2816
N_HEADS = 8
N_KV = 2
HD = 64
ATT_BLK = 128
EPS = 1e-6
LRU_C = 8.0
NEG_INF = -1e30
DEPTH = 2
NDEV = 8
CONV_A, CONV_B, CONV_D = 4, 3, 31
C_AX, C_AG, C_BV, C_BC, C_BB, C_Q, C_K, C_V, C_D1, C_D2 = 0, 512, 1024, 1536, 2048, 2560, 3072, 3200, 3328, 3840
CW_A, CW_B, CW_D, CW_ROWS = 0, 4, 8, 40
V_CAB, V_BX, V_BA, V_LAM, V_CDB, V_LNG, V_LNB, V_SINK, V_ROWS = 0, 1, 2, 3, 4, 5, 6, 7, 8
HALO = 32

ADAM_LR, ADAM_B1, ADAM_B2, ADAM_EPS, ADAM_WD, ADAM_STEP = 0.001, 0.9, 0.999, 1e-08, 0.01, 10

VMEM_LIMIT = 56 * 1024 * 1024

_NN = (((1,), (0,)), ((), ()))
_NT = (((1,), (1,)), ((), ()))
_TN = (((0,), (0,)), ((), ()))


def _dot(a, b, dims):
    return lax.dot_general(a.astype(BF16), b.astype(BF16), dims, preferred_element_type=F32)


def _cparams(n_axes):
    return pltpu.CompilerParams(dimension_semantics=("arbitrary",) * n_axes, vmem_limit_bytes=VMEM_LIMIT)


def _sds(shape, dtype):
    return jax.ShapeDtypeStruct(tuple(shape), dtype)


def _sigmoid(x):
    return jax.nn.sigmoid(x)


def _neg_expm1(x):
    p = x * (1.0 + x * (0.5 + x * (1.0 / 6.0 + x * (1.0 / 24.0 + x * (1.0 / 120.0)))))
    return jnp.where(x > -0.1, -p, 1.0 - jnp.exp(x))


def _softplus(z):
    return jnp.maximum(z, 0.0) + jnp.log1p(jnp.exp(-jnp.abs(z)))


def _gelu_and_grad(x):
    c = 0.7978845608028654
    inner = c * (x + 0.044715 * x * x * x)
    t = jnp.tanh(inner)
    g = 0.5 * x * (1.0 + t)
    dg = 0.5 * (1.0 + t) + 0.5 * x * (1.0 - t * t) * c * (1.0 + 3.0 * 0.044715 * x * x)
    return g, dg


ANY = pl.BlockSpec(memory_space=pl.ANY)
MESH = pl.DeviceIdType.MESH


def _mesh_pos():
    return lax.axis_index("x"), lax.axis_index("y"), lax.axis_index("c")


def _dev_index(px, py, pc):
    return 4 * px + 2 * py + pc


class CommJob:
    def __init__(self, inputs, aliases, out_shapes, sem_shapes, start, finish):
        self.inputs, self.aliases, self.out_shapes, self.sem_shapes = list(inputs), dict(aliases), list(out_shapes), list(sem_shapes)
        self.start, self.finish = start, finish


def _call(body, comm, args, *, grid, in_specs, out_specs, out_shape, scratch_shapes=(), name, aliases=None):
    single = not isinstance(out_shape, (list, tuple))
    out_specs = [out_specs] if single else list(out_specs)
    out_shape = [out_shape] if single else list(out_shape)
    scratch_shapes = list(scratch_shapes)
    n_in, n_out, n_scr, n_axes = len(in_specs), len(out_shape), len(scratch_shapes), len(grid)
    params = pltpu.CompilerParams(dimension_semantics=("arbitrary",) * n_axes, vmem_limit_bytes=VMEM_LIMIT)
    io_aliases = dict(aliases or {})
    if comm is None:
        outs = pl.pallas_call(body, grid=grid, in_specs=in_specs, out_specs=out_specs, out_shape=out_shape,
                              scratch_shapes=scratch_shapes, input_output_aliases=io_aliases, compiler_params=params,
                              name=name)(*args)
        return (outs[0] if single else outs), []
    c_in, c_out = len(comm.inputs), len(comm.out_shapes)
    io_aliases.update({n_in + i: n_out + o for i, o in comm.aliases.items()})

    def wrapped(*refs):
        ins, cins = refs[:n_in], refs[n_in:n_in + c_in]
        outs = refs[n_in + c_in:n_in + c_in + n_out]
        couts = refs[n_in + c_in + n_out:n_in + c_in + n_out + c_out]
        rest = refs[n_in + c_in + n_out + c_out:]
        scr, sems = rest[:n_scr], rest[n_scr:]
        first = functools.reduce(lambda a, b: a & b, [pl.program_id(a) == 0 for a in range(n_axes)])
        last = functools.reduce(lambda a, b: a & b, [pl.program_id(a) == pl.num_programs(a) - 1 for a in range(n_axes)])

        @pl.when(first)
        def _():
            comm.start(cins, couts, sems)

        body(*ins, *outs, *scr)

        @pl.when(last)
        def _():
            comm.finish(cins, couts, sems)

    outs = pl.pallas_call(
        wrapped, grid=grid, in_specs=list(in_specs) + [ANY] * c_in, out_specs=out_specs + [ANY] * c_out,
        out_shape=out_shape + comm.out_shapes, scratch_shapes=scratch_shapes + comm.sem_shapes,
        input_output_aliases=io_aliases, compiler_params=params, name=name)(*args, *comm.inputs)
    res, cres = outs[:n_out], outs[n_out:]
    return (res[0] if single else res), cres


def _comm_only(comm, name):
    c_in, c_out = len(comm.inputs), len(comm.out_shapes)

    def body(*refs):
        cins, couts, sems = refs[:c_in], refs[c_in:c_in + c_out], refs[c_in + c_out:]
        comm.start(cins, couts, sems)
        comm.finish(cins, couts, sems)

    return pl.pallas_call(body, in_specs=[ANY] * c_in, out_specs=[ANY] * c_out, out_shape=comm.out_shapes,
                          scratch_shapes=comm.sem_shapes, input_output_aliases=comm.aliases, name=name)(*comm.inputs)


def gather_job(pieces):
    inputs, aliases, out_shapes, plan, where = [], {}, [], [], {}
    for key, shard, gathered, row0, nrows in pieces:
        if key not in where:
            where[key] = (len(inputs), len(out_shapes))
            inputs.append(shard)
            if gathered is not None:
                aliases[len(inputs)] = len(out_shapes)
                inputs.append(gathered)
            out_shapes.append(_sds((NDEV * shard.shape[0], shard.shape[1]), shard.dtype))
        plan.append((*where[key], shard.shape[0], row0, nrows))
    n = len(plan)

    def copies(cins, couts, sems):
        send_sems, recv_sems, local_sems = sems
        x, y, c = _mesh_pos()
        me, sibling = (x, y, c), (x, y, 1 - c)
        chips = [(1 - x, y), (x, 1 - y), (1 - x, 1 - y)]
        local, first, relay, recv_ici, recv_d2d = [], [], [], [], []
        for p, (i_shard, i_out, rows, row0, nrows) in enumerate(plan):
            src = cins[i_shard].at[pl.ds(row0, nrows), :]

            def slot(dev, i_out=i_out, rows=rows, row0=row0, nrows=nrows):
                return couts[i_out].at[pl.ds(_dev_index(*dev) * rows + row0, nrows), :]

            def copy(g, dev, to, src=None, p=p, slot=slot):
                return pltpu.make_async_remote_copy(
                    src_ref=slot(dev) if src is None else src, dst_ref=slot(dev),
                    send_sem=send_sems.at[g, p], recv_sem=recv_sems.at[g, p], device_id=to, device_id_type=MESH)

            local.append(pltpu.make_async_copy(src, slot(me), local_sems.at[p]))
            first.append(copy(0, me, sibling, src=src))
            recv_d2d.append(copy(0, sibling, me))
            for j, chip in enumerate(chips):
                first.append(copy(1 + j, me, (*chip, c), src=src))
                recv_ici.append(copy(1 + j, (*chip, c), me))
                relay.append(copy(4 + j, (*chip, c), sibling))
                recv_d2d.append(copy(4 + j, (*chip, 1 - c), me))
        return local, first, relay, recv_ici, recv_d2d

    def start(cins, couts, sems):
        local, first, _, _, _ = copies(cins, couts, sems)
        for cp in local + first:
            cp.start()

    def finish(cins, couts, sems):
        local, first, relay, recv_ici, recv_d2d = copies(cins, couts, sems)
        for cp in recv_ici:
            cp.wait_recv()
        for cp in relay:
            cp.start()
        for cp in recv_d2d:
            cp.wait_recv()
        for cp in first + relay:
            cp.wait_send()
        for cp in local:
            cp.wait()

    sem_shapes = [pltpu.SemaphoreType.DMA((7, n)), pltpu.SemaphoreType.DMA((7, n)), pltpu.SemaphoreType.DMA((n,))]
    return CommJob(inputs, aliases, out_shapes, sem_shapes, start, finish)


def sibling_exchange_job(grads):
    n = len(grads)

    def copies(cins, couts, sems):
        send_sems, recv_sems = sems
        x, y, c = _mesh_pos()
        return [pltpu.make_async_remote_copy(
            src_ref=cins[q].at[:, 1 - c], dst_ref=couts[q], send_sem=send_sems.at[q], recv_sem=recv_sems.at[q],
            device_id=(x, y, 1 - c), device_id_type=MESH) for q in range(n)]

    def start(cins, couts, sems):
        for cp in copies(cins, couts, sems):
            cp.start()

    def finish(cins, couts, sems):
        cps = copies(cins, couts, sems)
        for cp in cps:
            cp.wait_recv()
        for cp in cps:
            cp.wait_send()

    return CommJob(grads, {}, [_sds((4,) + g.shape[2:], g.dtype) for g in grads],
                   [pltpu.SemaphoreType.DMA((n,)), pltpu.SemaphoreType.DMA((n,))], start, finish)


def chip_exchange_job(pieces):
    inputs, aliases, out_shapes, plan, where = [], {}, [], [], {}
    for partial, contrib, key, layer, row0, nrows, col0, cols in pieces:
        if key not in where:
            where[key] = len(out_shapes)
            out_shapes.append(_sds((4, DEPTH, partial.shape[1], cols), partial.dtype))
            if contrib is not None:
                aliases[len(inputs)] = where[key]
                inputs.append(contrib)
        plan.append((len(inputs), where[key], layer, row0, nrows, col0, partial.shape[2]))
        inputs.append(partial)
    n = len(plan)

    def copies(cins, couts, sems):
        send_sems, recv_sems, local_sems = sems
        x, y, c = _mesh_pos()
        mine = 2 * x + y
        local, sends, recvs = [], [], []
        for p, (i_in, i_out, layer, row0, nrows, col0, ncols) in enumerate(plan):
            rows, lanes = pl.ds(row0, nrows), pl.ds(col0, ncols)
            local.append(pltpu.make_async_copy(cins[i_in].at[mine, rows, :], couts[i_out].at[mine, layer, rows, lanes],
                                               local_sems.at[p]))
            for j, (cx, cy) in enumerate([(1 - x, y), (x, 1 - y), (1 - x, 1 - y)]):
                theirs = 2 * cx + cy

                def copy(slot_there, j=j, p=p, cx=cx, cy=cy, theirs=theirs, i_in=i_in, i_out=i_out, layer=layer,
                         rows=rows, lanes=lanes):
                    return pltpu.make_async_remote_copy(
                        src_ref=cins[i_in].at[theirs, rows, :], dst_ref=couts[i_out].at[slot_there, layer, rows, lanes],
                        send_sem=send_sems.at[j, p], recv_sem=recv_sems.at[j, p], device_id=(cx, cy, c), device_id_type=MESH)
                sends.append(copy(mine))
                recvs.append(copy(theirs))
        return local, sends, recvs

    def start(cins, couts, sems):
        local, sends, _ = copies(cins, couts, sems)
        for cp in local + sends:
            cp.start()

    def finish(cins, couts, sems):
        local, sends, recvs = copies(cins, couts, sems)
        for cp in recvs:
            cp.wait_recv()
        for cp in sends:
            cp.wait_send()
        for cp in local:
            cp.wait()

    sem_shapes = [pltpu.SemaphoreType.DMA((3, n)), pltpu.SemaphoreType.DMA((3, n)), pltpu.SemaphoreType.DMA((n,))]
    return CommJob(inputs, aliases, out_shapes, sem_shapes, start, finish)


def fwd_proj(x, g1, wt_in, l, comm=None):
    s = x.shape[0]
    tm = min(512, s)
    tn = 1408

    def body(x_ref, g_ref, w_ref, o_ref, xn_ref):
        @pl.when(pl.program_id(1) == 0)
        def _():
            xv = x_ref[...]
            r = lax.rsqrt(jnp.mean(xv * xv, axis=-1, keepdims=True) + EPS)
            xn_ref[...] = (xv * r * g_ref[l:l + 1, :]).astype(BF16)

        o_ref[...] = _dot(xn_ref[...], w_ref[...], _NT).astype(BF16)

    return _call(
        body, comm, (x, g1, wt_in), grid=(s // tm, IN_W // tn),
        in_specs=[pl.BlockSpec((tm, D), lambda i, j: (i, 0)),
                  pl.BlockSpec((DEPTH, D), lambda i, j: (0, 0)),
                  pl.BlockSpec((tn, D), lambda i, j: (j, 0))],
        out_specs=pl.BlockSpec((tm, tn), lambda i, j: (i, j)),
        out_shape=_sds((s, IN_W), BF16),
        scratch_shapes=[pltpu.VMEM((tm, D), BF16)], name=f"fwd_proj{l}")


def _scan_fwd(a_ref, u_ref, h_ref, h0, n_rows):
    row = lax.broadcasted_iota(jnp.int32, (8, BW), 0)

    def body(g, hprev):
        r = pl.multiple_of(g * 8, 8)
        a = a_ref[pl.ds(r, 8), :]
        u = u_ref[pl.ds(r, 8), :]
        for sft in (1, 2, 4):
            a_sh = jnp.where(row >= sft, pltpu.roll(a, sft, 0), 1.0)
            u_sh = jnp.where(row >= sft, pltpu.roll(u, sft, 0), 0.0)
            u = u + a * u_sh
            a = a * a_sh
        h = u + a * hprev
        h_ref[pl.ds(r, 8), :] = h
        return h[7:8, :]

    return lax.fori_loop(0, n_rows // 8, body, h0)


def _scan_bwd(b_ref, g_ref, o_ref, c0, n_rows):
    row = lax.broadcasted_iota(jnp.int32, (8, BW), 0)

    def body(k, cnext):
        r = pl.multiple_of((n_rows // 8 - 1 - k) * 8, 8)
        b = b_ref[pl.ds(r, 8), :]
        g = g_ref[pl.ds(r, 8), :]
        for sft in (1, 2, 4):
            b_sh = jnp.where(row < 8 - sft, pltpu.roll(b, 8 - sft, 0), 1.0)
            g_sh = jnp.where(row < 8 - sft, pltpu.roll(g, 8 - sft, 0), 0.0)
            g = g + b * g_sh
            b = b * b_sh
        o = g + b * cnext
        o_ref[pl.ds(r, 8), :] = o
        return o[0:1, :]

    return lax.fori_loop(0, n_rows // 8, body, c0)


def _shifted_copies(buf, shifted, n_rows):
    for r in range(1, 8):
        shifted[r - 1, 0:n_rows - 8, :] = buf[pl.ds(r, n_rows - 8), :]


def _window(buf, shifted, off, t):
    r = off % 8
    return buf[pl.ds(off, t), :] if r == 0 else shifted[r - 1, pl.ds(off - r, t), :]


def _branch_fwd_math(cur_ref, halo_ref, cw_ref, vec_ref, wx_ref, wa_ref, bufa, bufb, bufd, xd, first, t):
    def halo(c0):
        v = halo_ref[:, c0:c0 + BW].astype(F32)
        return jnp.where(first, 0.0, v)

    def cur(c0):
        return cur_ref[:, c0:c0 + BW].astype(F32)

    out = {}
    bufa[0:HALO, :] = halo(C_AX)
    bufa[HALO:HALO + t, :] = cur(C_AX)
    ca = jnp.zeros((t, BW), F32) + vec_ref[V_CAB:V_CAB + 1, :]
    for k in range(CONV_A):
        ca = ca + cw_ref[CW_A + k:CW_A + k + 1, :] * bufa[pl.ds(HALO - (CONV_A - 1) + k, t), :]
    gi = _sigmoid(_dot(ca, wx_ref[...], _NN) + vec_ref[V_BX:V_BX + 1, :])
    gr = _sigmoid(_dot(ca, wa_ref[...], _NN) + vec_ref[V_BA:V_BA + 1, :])
    sp = _softplus(-vec_ref[V_LAM:V_LAM + 1, :])
    la = -LRU_C * sp * gr
    a = jnp.exp(la)
    mult = jnp.sqrt(_neg_expm1(2.0 * la))
    out.update(ca=ca, gi=gi, gr=gr, sp=sp, a=a, mult=mult)
    bufb[0:HALO, :] = halo(C_BC) * halo(C_BV)
    bufb[HALO:HALO + t, :] = cur(C_BC) * cur(C_BV)
    cb = jnp.zeros((t, BW), F32)
    for k in range(CONV_B):
        cb = cb + cw_ref[CW_B + k:CW_B + k + 1, :] * bufb[pl.ds(HALO - (CONV_B - 1) + k, t), :]
    out.update(cb=cb)
    bufd[0:HALO, :] = halo(C_D1) * _sigmoid(halo(C_D2))
    s2 = _sigmoid(cur(C_D2))
    bufd[HALO:HALO + t, :] = cur(C_D1) * s2
    _shifted_copies(bufd, xd, t + HALO)
    cd = jnp.zeros((t, BW), F32) + vec_ref[V_CDB:V_CDB + 1, :]
    for k in range(CONV_D):
        cd = cd + cw_ref[CW_D + k:CW_D + k + 1, :] * _window(bufd, xd, HALO - (CONV_D - 1) + k, t)
    mu = jnp.mean(cd, axis=-1, keepdims=True)
    xc = cd - mu
    rstd = lax.rsqrt(jnp.mean(xc * xc, axis=-1, keepdims=True) + EPS)
    xh = xc * rstd
    ln = xh * vec_ref[V_LNG:V_LNG + 1, :] + vec_ref[V_LNB:V_LNB + 1, :]
    out.update(s2=s2, xh=xh, rstd=rstd, ln=ln)
    return out


def fwd_branch(proj, convw, vecs, wx_bd, wa_bd, l, comm=None):
    s = proj.shape[0]
    t = min(256, s)

    def body(cur_ref, halo_ref, cw_ref, vec_ref, wx_ref, wa_ref, pre_ref, h_ref, bufa, bufb, bufd, xd, a_s, u_s, hcar):
        first = pl.program_id(0) == 0

        @pl.when(first)
        def _():
            hcar[...] = jnp.zeros((1, BW), F32)

        v = _branch_fwd_math(cur_ref, halo_ref, cw_ref, vec_ref, wx_ref, wa_ref, bufa, bufb, bufd, xd, first, t)
        a_s[...] = v["a"]
        u_s[...] = v["ca"] * v["gi"] * v["mult"]
        hcar[...] = _scan_fwd(a_s, u_s, h_ref, hcar[...], t)
        gg, _ = _gelu_and_grad(cur_ref[:, C_AG:C_AG + BW].astype(F32))
        pre_ref[:, 0:BW] = (h_ref[...] * gg).astype(BF16)
        pre_ref[:, BW:2 * BW] = (cur_ref[:, C_BB:C_BB + BW].astype(F32) * v["cb"]).astype(BF16)
        ln = v["ln"]
        pre_ref[:, 2 * BW:3 * BW] = (ln * _sigmoid(ln)).astype(BF16)

    hb = t // HALO
    return _call(
        body, comm, (proj, proj, convw, vecs, wx_bd, wa_bd), grid=(s // t,),
        in_specs=[pl.BlockSpec((t, GL0), lambda i: (i, 0)),
                  pl.BlockSpec((HALO, GL0), lambda i: (jnp.maximum(i * hb - 1, 0), 0)),
                  pl.BlockSpec((None, CW_ROWS, BW), lambda i: (l, 0, 0)),
                  pl.BlockSpec((None, V_ROWS, BW), lambda i: (l, 0, 0)),
                  pl.BlockSpec((None, BW, BW), lambda i: (l, 0, 0)),
                  pl.BlockSpec((None, BW, BW), lambda i: (l, 0, 0))],
        out_specs=[pl.BlockSpec((t, 3 * BW), lambda i: (i, 0)), pl.BlockSpec((t, BW), lambda i: (i, 0))],
        out_shape=[_sds((s, 3 * BW), BF16), _sds((s, BW), F32)],
        scratch_shapes=[pltpu.VMEM((t + HALO, BW), F32)] * 3 + [pltpu.VMEM((7, t + HALO - 8, BW), F32)]
        + [pltpu.VMEM((t, BW), F32)] * 2 + [pltpu.VMEM((1, BW), F32)],
        name=f"fwd_branch{l}")


GRP = N_HEADS // N_KV


def _attn_mask_bias(first_block):
    shape = (GRP * ATT_BLK, 2 * ATT_BLK)
    qi = lax.broadcasted_iota(jnp.int32, shape, 0) & (ATT_BLK - 1)
    ki = lax.broadcasted_iota(jnp.int32, shape, 1)
    dist = qi + ATT_BLK - ki
    valid = (dist >= 0) & (dist < ATT_BLK) & (jnp.logical_not(first_block) | (ki >= ATT_BLK))
    return dist.astype(F32), valid


def _per_head(hk, values):
    hl = lax.broadcasted_iota(jnp.int32, (GRP * ATT_BLK, 1), 0) // ATT_BLK
    out = values[GRP - 1]
    for j in range(GRP - 2, -1, -1):
        out = jnp.where(hl == j, values[j], out)
    return out


def _attn_probs(q_ref, kvp_ref, kvc_ref, vec_ref, distf, valid):
    kvs = range(N_KV)
    heads = [range(hk * GRP, (hk + 1) * GRP) for hk in kvs]
    q4 = [jnp.concatenate([q_ref[:, h * HD:(h + 1) * HD] for h in heads[hk]], axis=0) for hk in kvs]
    k2 = [jnp.concatenate([kvp_ref[:, hk * HD:(hk + 1) * HD], kvc_ref[:, hk * HD:(hk + 1) * HD]], axis=0) for hk in kvs]
    v2 = [jnp.concatenate([kvp_ref[:, (N_KV + hk) * HD:(N_KV + hk + 1) * HD],
                           kvc_ref[:, (N_KV + hk) * HD:(N_KV + hk + 1) * HD]], axis=0) for hk in kvs]
    slope = [_per_head(hk, [2.0 ** (-8.0 * (h + 1) / N_HEADS) for h in heads[hk]]) for hk in kvs]
    sink = [_per_head(hk, [vec_ref[V_SINK:V_SINK + 1, h:h + 1] for h in heads[hk]]) for hk in kvs]
    sc = [_dot(q4[hk], k2[hk], _NT) for hk in kvs]
    sc = [jnp.where(valid, sc[hk] * (HD ** -0.5) - slope[hk] * distf, NEG_INF) for hk in kvs]
    m = [jnp.maximum(jnp.max(sc[hk], axis=-1, keepdims=True), sink[hk]) for hk in kvs]
    p = [jnp.exp(sc[hk] - m[hk]) for hk in kvs]
    es = [jnp.exp(sink[hk] - m[hk]) for hk in kvs]
    inv = [1.0 / (jnp.sum(p[hk], axis=-1, keepdims=True) + es[hk]) for hk in kvs]
    return [(q4[hk], k2[hk], v2[hk], p[hk] * inv[hk], es[hk] * inv[hk]) for hk in kvs]


def fwd_attn(proj, vecs, l, comm=None):
    s = proj.shape[0]
    nb = s // ATT_BLK

    def body(q_ref, kvp_ref, kvc_ref, vec_ref, o_ref):
        distf, valid = _attn_mask_bias(pl.program_id(0) == 0)
        groups = _attn_probs(q_ref, kvp_ref, kvc_ref, vec_ref, distf, valid)
        outs = [_dot(p, v2, _NN).astype(BF16) for _, _, v2, p, _ in groups]
        for hk, out in enumerate(outs):
            for j in range(GRP):
                h = hk * GRP + j
                o_ref[:, h * HD:(h + 1) * HD] = out[j * ATT_BLK:(j + 1) * ATT_BLK]

    return _call(
        body, comm, (proj, proj, proj, vecs), grid=(nb,),
        in_specs=[pl.BlockSpec((ATT_BLK, BW), lambda i: (i, C_Q // BW)),
                  pl.BlockSpec((ATT_BLK, 256), lambda i: (jnp.maximum(i - 1, 0), C_K // 256)),
                  pl.BlockSpec((ATT_BLK, 256), lambda i: (i, C_K // 256)),
                  pl.BlockSpec((None, V_ROWS, BW), lambda i: (l, 0, 0))],
        out_specs=pl.BlockSpec((ATT_BLK, BW), lambda i: (i, 0)),
        out_shape=_sds((s, BW), BF16), name=f"fwd_attn{l}")


def fwd_merge(x, proj, pre_abd, pre_c, wt_a, wt_b, wt_c, wt_d, w_o, l, comm=None):
    s = x.shape[0]
    tm = min(256, s)

    def body(x_ref, gl_ref, pabd_ref, pc_ref, wa_ref, wb_ref, wc_ref, wd_ref, wo_ref, y_ref, mg_ref, h1_ref):
        pres = (pabd_ref[:, 0:BW], pabd_ref[:, BW:2 * BW], pc_ref[...], pabd_ref[:, 2 * BW:3 * BW])
        merged = jnp.zeros((tm, D), F32)
        for k, (pre, w_ref) in enumerate(zip(pres, (wa_ref, wb_ref, wc_ref, wd_ref))):
            yk = _dot(pre, w_ref[...], _NT)
            y_ref[:, k * D:(k + 1) * D] = yk.astype(BF16)
            merged = merged + _sigmoid(gl_ref[:, k * D:(k + 1) * D].astype(F32)) * yk
        mg_ref[...] = merged.astype(BF16)
        h1_ref[...] = x_ref[...] + _dot(merged, wo_ref[...], _NN)

    wspec = pl.BlockSpec((D, BW), lambda i: (0, 0))
    return _call(
        body, comm, (x, proj, pre_abd, pre_c, wt_a, wt_b, wt_c, wt_d, w_o), grid=(s // tm,),
        in_specs=[pl.BlockSpec((tm, D), lambda i: (i, 0)),
                  pl.BlockSpec((E(tm), E(4 * D)), lambda i: (i * tm, GL0)),
                  pl.BlockSpec((tm, 3 * BW), lambda i: (i, 0)),
                  pl.BlockSpec((tm, BW), lambda i: (i, 0)),
                  wspec, wspec, wspec, wspec,
                  pl.BlockSpec((D, D), lambda i: (0, 0))],
        out_specs=[pl.BlockSpec((tm, 4 * D), lambda i: (i, 0)), pl.BlockSpec((tm, D), lambda i: (i, 0)),
                   pl.BlockSpec((tm, D), lambda i: (i, 0))],
        out_shape=[_sds((s, 4 * D), BF16), _sds((s, D), BF16), _sds((s, D), F32)], name=f"fwd_merge{l}")


def fwd_ffn(h1, g2, wt_gate, wt_up, w_down, l, comm=None):
    s = h1.shape[0]
    tm = min(512, s)
    fc = FF // 2

    def body(h_ref, g_ref, wg_ref, wu_ref, wd_ref, xo_ref, fg_ref, fu_ref, hn_ref, acc_ref):
        j = pl.program_id(1)

        @pl.when(j == 0)
        def _():
            hv = h_ref[...]
            r = lax.rsqrt(jnp.mean(hv * hv, axis=-1, keepdims=True) + EPS)
            hn_ref[...] = (hv * r * g_ref[l:l + 1, :]).astype(BF16)
            acc_ref[...] = hv

        fg = _dot(hn_ref[...], wg_ref[...], _NT)
        fu = _dot(hn_ref[...], wu_ref[...], _NT)
        fg_ref[...] = fg.astype(BF16)
        fu_ref[...] = fu.astype(BF16)
        acc_ref[...] += _dot(fg * _sigmoid(fg) * fu, wd_ref[...], _NN)

        @pl.when(j == pl.num_programs(1) - 1)
        def _():
            xo_ref[...] = acc_ref[...]

    wspec = pl.BlockSpec((fc, D), lambda i, j: (j, 0))
    return _call(
        body, comm, (h1, g2, wt_gate, wt_up, w_down), grid=(s // tm, FF // fc),
        in_specs=[pl.BlockSpec((tm, D), lambda i, j: (i, 0)), pl.BlockSpec((DEPTH, D), lambda i, j: (0, 0)),
                  wspec, wspec, wspec],
        out_specs=[pl.BlockSpec((tm, D), lambda i, j: (i, 0)), pl.BlockSpec((tm, fc), lambda i, j: (i, j)),
                   pl.BlockSpec((tm, fc), lambda i, j: (i, j))],
        out_shape=[_sds((s, D), F32), _sds((s, FF), BF16), _sds((s, FF), BF16)],
        scratch_shapes=[pltpu.VMEM((tm, D), BF16), pltpu.VMEM((tm, D), F32)], name=f"fwd_ffn{l}")


def loss_head(x, gf, target):
    s = x.shape[0]
    tm = min(512, s)

    def body(x_ref, g_ref, t_ref, dx_ref, st_ref):
        @pl.when(pl.program_id(0) == 0)
        def _():
            st_ref[...] = jnp.zeros((8, D), F32)

        xv = x_ref[...]
        g = g_ref[...]
        r = lax.rsqrt(jnp.mean(xv * xv, axis=-1, keepdims=True) + EPS)
        n = xv * r
        err = n * g - t_ref[...]
        dy = err * (1.0 / D)
        dn = dy * g
        dx_ref[...] = r * (dn - n * jnp.mean(dn * n, axis=-1, keepdims=True))
        st_ref[0:1, :] += jnp.sum(dy * n, axis=0, keepdims=True)
        lsum = 0.5 * jnp.sum(jnp.mean(err * err, axis=-1, keepdims=True), axis=0, keepdims=True)
        st_ref[1:2, :] += jnp.broadcast_to(lsum, (1, D))

    return pl.pallas_call(
        body, grid=(s // tm,),
        in_specs=[pl.BlockSpec((tm, D), lambda i: (i, 0)), pl.BlockSpec((1, D), lambda i: (0, 0)),
                  pl.BlockSpec((tm, D), lambda i: (i, 0))],
        out_specs=[pl.BlockSpec((tm, D), lambda i: (i, 0)), pl.BlockSpec((8, D), lambda i: (0, 0))],
        out_shape=[_sds((s, D), F32), _sds((8, D), F32)],
        compiler_params=_cparams(1), name="loss_head")(x, gf, target)


def _edge_index(j, i, n_j, n_i):
    return jnp.where((j == 0) | (j == n_j - 1), i, n_i - 1)


def bwd_ffn(dxo, h1, fg, fu, g2, wt_gate, wt_up, w_down, l, comm=None):
    s = h1.shape[0]
    tm = min(1024, s)
    fc = 256
    n_j, n_i = FF // fc, s // tm

    def body(dxo_ref, h_ref, fg_ref, fu_ref, g_ref, wg_ref, wu_ref, wd_ref,
             dh_ref, dwg_ref, dwu_ref, dwd_ref, st_ref, dhn, dxo_b, hn_b, ag, au, ad):
        j, i = pl.program_id(0), pl.program_id(1)
        rows = pl.ds(pl.multiple_of(i * tm, tm), tm)
        g = g_ref[l:l + 1, :]

        @pl.when(j == 0)
        def _():
            hv = h_ref[...]
            r = lax.rsqrt(jnp.mean(hv * hv, axis=-1, keepdims=True) + EPS)
            hn_b[rows, :] = (hv * r * g).astype(BF16)
            dxo_b[rows, :] = dxo_ref[...].astype(BF16)
            dhn[rows, :] = jnp.zeros((tm, D), F32)

        @pl.when((j == 0) & (i == 0))
        def _():
            st_ref[...] = jnp.zeros((8, D), F32)

        @pl.when(i == 0)
        def _():
            ag[...] = jnp.zeros((fc, D), F32)
            au[...] = jnp.zeros((fc, D), F32)
            ad[...] = jnp.zeros((fc, D), F32)

        fgv = fg_ref[...].astype(F32)
        fuv = fu_ref[...].astype(F32)
        sg = _sigmoid(fgv)
        sil = fgv * sg
        dxb = dxo_b[rows, :]
        hnb = hn_b[rows, :]
        d_act = _dot(dxb, wd_ref[...], _NT)
        ad[...] += _dot(sil * fuv, dxb, _TN)
        d_fg = (d_act * fuv * (sg * (1.0 + fgv * (1.0 - sg)))).astype(BF16)
        d_fu = (d_act * sil).astype(BF16)
        ag[...] += _dot(d_fg, hnb, _TN)
        au[...] += _dot(d_fu, hnb, _TN)
        dhn[rows, :] += _dot(d_fg, wg_ref[...], _NN) + _dot(d_fu, wu_ref[...], _NN)

        @pl.when(i == n_i - 1)
        def _():
            dwg_ref[...] = ag[...].astype(BF16)
            dwu_ref[...] = au[...].astype(BF16)
            dwd_ref[...] = ad[...].astype(BF16)

        @pl.when(j == n_j - 1)
        def _():
            hv = h_ref[...]
            r = lax.rsqrt(jnp.mean(hv * hv, axis=-1, keepdims=True) + EPS)
            n = hv * r
            dv = dhn[rows, :]
            dn = dv * g
            dh_ref[...] = dxo_ref[...] + r * (dn - n * jnp.mean(dn * n, axis=-1, keepdims=True))
            st_ref[0:1, :] += jnp.sum(dv * n, axis=0, keepdims=True)

    edge = lambda j, i: (_edge_index(j, i, n_j, n_i), 0)
    wspec = pl.BlockSpec((fc, D), lambda j, i: (j, 0))
    dwspec = pl.BlockSpec((fc, D), lambda j, i: (j, 0))
    return _call(
        body, comm, (dxo, h1, fg, fu, g2, wt_gate, wt_up, w_down), grid=(n_j, n_i),
        in_specs=[pl.BlockSpec((tm, D), edge, pipeline_mode=pl.Buffered(1)),
                  pl.BlockSpec((tm, D), edge, pipeline_mode=pl.Buffered(1)),
                  pl.BlockSpec((tm, fc), lambda j, i: (i, j)), pl.BlockSpec((tm, fc), lambda j, i: (i, j)),
                  pl.BlockSpec((DEPTH, D), lambda j, i: (0, 0)), wspec, wspec, wspec],
        out_specs=[pl.BlockSpec((tm, D), lambda j, i: (jnp.where(j == n_j - 1, i, 0), 0)),
                   dwspec, dwspec, dwspec, pl.BlockSpec((8, D), lambda j, i: (0, 0))],
        out_shape=[_sds((s, D), F32), _sds((FF, D), BF16), _sds((FF, D), BF16), _sds((FF, D), BF16), _sds((8, D), F32)],
        scratch_shapes=[pltpu.VMEM((s, D), F32), pltpu.VMEM((s, D), BF16), pltpu.VMEM((s, D), BF16),
                        pltpu.VMEM((fc, D), F32), pltpu.VMEM((fc, D), F32), pltpu.VMEM((fc, D), F32)],
        name=f"bwd_ffn{l}")


def bwd_merge(dh1, y4, proj, merged, pre_abd, pre_c, wt_a, wt_b, wt_c, wt_d, w_o, l, comm=None):
    s = dh1.shape[0]
    tm = min(256, s)
    n_i = s // tm

    def body(dh_ref, y_ref, gl_ref, mg_ref, pabd_ref, pc_ref, wa_ref, wb_ref, wc_ref, wd_ref, wo_ref,
             dgl_ref, dpre_ref, dwo_ref, dwa_ref, dwb_ref, dwc_ref, dwd_ref, ao, aa, ab, ac, ad):
        i = pl.program_id(0)
        accs = (aa, ab, ac, ad)

        @pl.when(i == 0)
        def _():
            ao[...] = jnp.zeros((D, D), F32)
            for acc in accs:
                acc[...] = jnp.zeros((D, BW), F32)

        dhb = dh_ref[...].astype(BF16)
        dmg = _dot(dhb, wo_ref[...], _NT)
        ao[...] += _dot(mg_ref[...], dhb, _TN)
        pres = (pabd_ref[:, 0:BW], pabd_ref[:, BW:2 * BW], pc_ref[...], pabd_ref[:, 2 * BW:3 * BW])
        for k, (pre, w_ref, acc) in enumerate(zip(pres, (wa_ref, wb_ref, wc_ref, wd_ref), accs)):
            gk = _sigmoid(gl_ref[:, k * D:(k + 1) * D].astype(F32))
            yk = y_ref[:, k * D:(k + 1) * D].astype(F32)
            dgl_ref[:, k * D:(k + 1) * D] = (dmg * yk * gk * (1.0 - gk)).astype(BF16)
            dyk = (dmg * gk).astype(BF16)
            dpre_ref[:, k * BW:(k + 1) * BW] = _dot(dyk, w_ref[...], _NN).astype(BF16)
            acc[...] += _dot(dyk, pre, _TN)

        @pl.when(i == n_i - 1)
        def _():
            dwo_ref[...] = ao[...].astype(BF16)
            for o_ref, acc in zip((dwa_ref, dwb_ref, dwc_ref, dwd_ref), accs):
                o_ref[...] = acc[...].astype(BF16)

    wspec = pl.BlockSpec((D, BW), lambda i: (0, 0))
    dwspec = pl.BlockSpec((D, BW), lambda i: (0, 0))
    return _call(
        body, comm, (dh1, y4, proj, merged, pre_abd, pre_c, wt_a, wt_b, wt_c, wt_d, w_o), grid=(n_i,),
        in_specs=[pl.BlockSpec((tm, D), lambda i: (i, 0)),
                  pl.BlockSpec((tm, 4 * D), lambda i: (i, 0)),
                  pl.BlockSpec((E(tm), E(4 * D)), lambda i: (i * tm, GL0)),
                  pl.BlockSpec((tm, D), lambda i: (i, 0)),
                  pl.BlockSpec((tm, 3 * BW), lambda i: (i, 0)),
                  pl.BlockSpec((tm, BW), lambda i: (i, 0)),
                  wspec, wspec, wspec, wspec,
                  pl.BlockSpec((D, D), lambda i: (0, 0))],
        out_specs=[pl.BlockSpec((E(tm), E(4 * D)), lambda i: (i * tm, GL0)),
                   pl.BlockSpec((tm, 4 * BW), lambda i: (i, 0)),
                   pl.BlockSpec((D, D), lambda i: (0, 0)), dwspec, dwspec, dwspec, dwspec],
        out_shape=[_sds((s, IN_W), BF16), _sds((s, 4 * BW), BF16), _sds((D, D), BF16)] + [_sds((D, BW), BF16)] * 4,
        scratch_shapes=[pltpu.VMEM((D, D), F32)] + [pltpu.VMEM((D, BW), F32)] * 4, name=f"bwd_merge{l}")


def bwd_attn(proj, dpre, vecs, l, comm=None):
    s = proj.shape[0]
    nb = s // ATT_BLK
    grp = N_HEADS // N_KV

    def body(q_ref, kvp_ref, kvc_ref, do_ref, vec_ref, dq_ref, dkc_ref, dkp_ref, st_ref):
        @pl.when(pl.program_id(0) == 0)
        def _():
            st_ref[...] = jnp.zeros((8, 128), F32)

        distf, valid = _attn_mask_bias(pl.program_id(0) == 0)
        lane = lax.broadcasted_iota(jnp.int32, (1, 128), 1)
        dsink = jnp.zeros((1, 128), F32)
        groups = _attn_probs(q_ref, kvp_ref, kvc_ref, vec_ref, distf, valid)
        kvs = range(N_KV)
        do4s = [jnp.concatenate([do_ref[:, h * HD:(h + 1) * HD] for h in range(hk * grp, (hk + 1) * grp)], axis=0) for hk in kvs]
        dps = [_dot(do4s[hk], groups[hk][2], _NT) for hk in kvs]
        deltas = [jnp.sum(groups[hk][3] * dps[hk], axis=-1, keepdims=True) for hk in kvs]
        dss = [groups[hk][3] * (dps[hk] - deltas[hk]) * (HD ** -0.5) for hk in kvs]
        for hk in kvs:
            q4, k2, v2, p, ps = groups[hk]
            do4, delta, ds = do4s[hk], deltas[hk], dss[hk]
            dq4 = _dot(ds, k2, _NN).astype(BF16)
            dk2 = _dot(ds, q4, _TN)
            dv2 = _dot(p, do4, _TN)
            psd = ps * delta
            for j in range(grp):
                h = hk * grp + j
                rows = slice(j * ATT_BLK, (j + 1) * ATT_BLK)
                dq_ref[:, h * HD:(h + 1) * HD] = dq4[rows]
                dsink = dsink + jnp.where(lane == h, -jnp.sum(psd[rows], axis=0, keepdims=True), 0.0)
            dkp_ref[:, hk * HD:(hk + 1) * HD] = dk2[0:ATT_BLK].astype(BF16)
            dkc_ref[:, hk * HD:(hk + 1) * HD] = dk2[ATT_BLK:].astype(BF16)
            dkp_ref[:, (N_KV + hk) * HD:(N_KV + hk + 1) * HD] = dv2[0:ATT_BLK].astype(BF16)
            dkc_ref[:, (N_KV + hk) * HD:(N_KV + hk + 1) * HD] = dv2[ATT_BLK:].astype(BF16)
        st_ref[0:1, :] += dsink

    return _call(
        body, comm, (proj, proj, proj, dpre, vecs), grid=(nb,),
        in_specs=[pl.BlockSpec((ATT_BLK, BW), lambda i: (i, C_Q // BW)),
                  pl.BlockSpec((ATT_BLK, 256), lambda i: (jnp.maximum(i - 1, 0), C_K // 256)),
                  pl.BlockSpec((ATT_BLK, 256), lambda i: (i, C_K // 256)),
                  pl.BlockSpec((ATT_BLK, BW), lambda i: (i, 2)),
                  pl.BlockSpec((None, V_ROWS, BW), lambda i: (l, 0, 0))],
        out_specs=[pl.BlockSpec((ATT_BLK, BW), lambda i: (i, 0)), pl.BlockSpec((ATT_BLK, 256), lambda i: (i, 0)),
                   pl.BlockSpec((ATT_BLK, 256), lambda i: (i, 0)), pl.BlockSpec((8, 128), lambda i: (0, 0))],
        out_shape=[_sds((s, BW), BF16), _sds((s, 256), BF16), _sds((s, 256), BF16), _sds((8, 128), F32)],
        name=f"bwd_attn{l}")


def bwd_branch(proj, dproj, dpre, h, dq, dkc, dkp, convw, vecs, wx_bd, wa_bd, l, comm=None):
    s = proj.shape[0]
    t = 2 * ATT_BLK
    nt = s // t
    nb = s // ATT_BLK
    hb = t // HALO

    def body(cur_ref, halo_ref, dpre_ref, h_ref, hp_ref, dq_ref, dkc_ref, dkp1_ref, dkp2_ref,
             cw_ref, vec_ref, wx_ref, wa_ref, dproj_in, dp_ref, dcw_ref, dvec_ref, dwx_ref, dwa_ref,
             bufa, bufb, bufd, xd, xg, a_ext, hbuf, b_s, g_s, dh_s, ga, gb, gd, dhcar):
        del dproj_in
        step = pl.program_id(0)
        ti = nt - 1 - step
        first = ti == 0

        @pl.when(step == 0)
        def _():
            dcw_ref[...] = jnp.zeros((CW_ROWS, BW), F32)
            dvec_ref[...] = jnp.zeros((V_ROWS, BW), F32)
            dwx_ref[...] = jnp.zeros((BW, BW), F32)
            dwa_ref[...] = jnp.zeros((BW, BW), F32)
            dhcar[...] = jnp.zeros((1, BW), F32)
            a_ext[t:t + 8, :] = jnp.zeros((8, BW), F32)
            ga[t:t + 8, :] = jnp.zeros((8, BW), F32)
            gb[t:t + 8, :] = jnp.zeros((8, BW), F32)
            gd[t:t + HALO, :] = jnp.zeros((HALO, BW), F32)

        def cur(c0):
            return cur_ref[:, c0:c0 + BW].astype(F32)

        def rsum(v):
            return jnp.sum(v, axis=0, keepdims=True)

        def put(c0, v):
            dp_ref[:, c0:c0 + BW] = v.astype(BF16)

        v = _branch_fwd_math(cur_ref, halo_ref, cw_ref, vec_ref, wx_ref, wa_ref, bufa, bufb, bufd, xd, first, t)
        ca, gi, gr, sp, a, mult = v["ca"], v["gi"], v["gr"], v["sp"], v["a"], v["mult"]
        dpa = dpre_ref[:, 0:BW].astype(F32)
        gg, dgg = _gelu_and_grad(cur(C_AG))
        hv = h_ref[...]
        put(C_AG, dpa * hv * dgg)
        a_ext[0:t, :] = a
        b_s[...] = a_ext[pl.ds(1, t), :]
        g_s[...] = dpa * gg
        dhcar[...] = _scan_bwd(b_s, g_s, dh_s, dhcar[...], t)
        a_ext[t:t + 1, :] = a[0:1, :]
        dh = dh_s[...]
        hbuf[0:8, :] = jnp.where(first, 0.0, hp_ref[...])
        hbuf[8:8 + t, :] = hv
        da = dh * hbuf[pl.ds(7, t), :]
        d_ca = dh * gi * mult
        d_gi = dh * ca * mult
        d_mult = dh * ca * gi
        d_la = da * a - d_mult * (a * a) / mult
        lam = vec_ref[V_LAM:V_LAM + 1, :]
        dvec_ref[V_LAM:V_LAM + 1, :] += rsum(d_la * gr) * (LRU_C * _sigmoid(-lam))
        d_gr = d_la * (-LRU_C * sp)
        d_zr = d_gr * gr * (1.0 - gr)
        d_zi = d_gi * gi * (1.0 - gi)
        dvec_ref[V_BA:V_BA + 1, :] += rsum(d_zr)
        dvec_ref[V_BX:V_BX + 1, :] += rsum(d_zi)
        dwa_ref[...] += _dot(ca, d_zr, _TN)
        dwx_ref[...] += _dot(ca, d_zi, _TN)
        d_ca = d_ca + _dot(d_zi, wx_ref[...], _NT) + _dot(d_zr, wa_ref[...], _NT)
        dvec_ref[V_CAB:V_CAB + 1, :] += rsum(d_ca)
        ga[0:t, :] = d_ca
        d_ax = jnp.zeros((t, BW), F32)
        for k in range(CONV_A):
            d_ax = d_ax + cw_ref[CW_A + k:CW_A + k + 1, :] * ga[pl.ds(CONV_A - 1 - k, t), :]
            dcw_ref[CW_A + k:CW_A + k + 1, :] += rsum(d_ca * bufa[pl.ds(HALO - (CONV_A - 1) + k, t), :])
        ga[t:t + 8, :] = d_ca[0:8, :]
        put(C_AX, d_ax)
        dpb = dpre_ref[:, BW:2 * BW].astype(F32)
        put(C_BB, dpb * v["cb"])
        d_cb = dpb * cur(C_BB)
        gb[0:t, :] = d_cb
        d_cbin = jnp.zeros((t, BW), F32)
        for k in range(CONV_B):
            d_cbin = d_cbin + cw_ref[CW_B + k:CW_B + k + 1, :] * gb[pl.ds(CONV_B - 1 - k, t), :]
            dcw_ref[CW_B + k:CW_B + k + 1, :] += rsum(d_cb * bufb[pl.ds(HALO - (CONV_B - 1) + k, t), :])
        gb[t:t + 8, :] = d_cb[0:8, :]
        put(C_BC, d_cbin * cur(C_BV))
        put(C_BV, d_cbin * cur(C_BC))
        dpd = dpre_ref[:, 3 * BW:4 * BW].astype(F32)
        ln, xh, rstd, s2 = v["ln"], v["xh"], v["rstd"], v["s2"]
        sg = _sigmoid(ln)
        d_ln = dpd * sg * (1.0 + ln * (1.0 - sg))
        dvec_ref[V_LNG:V_LNG + 1, :] += rsum(d_ln * xh)
        dvec_ref[V_LNB:V_LNB + 1, :] += rsum(d_ln)
        d_xh = d_ln * vec_ref[V_LNG:V_LNG + 1, :]
        d_cd = rstd * (d_xh - jnp.mean(d_xh, axis=-1, keepdims=True)
                       - xh * jnp.mean(d_xh * xh, axis=-1, keepdims=True))
        dvec_ref[V_CDB:V_CDB + 1, :] += rsum(d_cd)
        gd[0:t, :] = d_cd
        _shifted_copies(gd, xg, t + HALO)
        d_dg = jnp.zeros((t, BW), F32)
        for k in range(CONV_D):
            d_dg = d_dg + cw_ref[CW_D + k:CW_D + k + 1, :] * _window(gd, xg, CONV_D - 1 - k, t)
            dcw_ref[CW_D + k:CW_D + k + 1, :] += rsum(d_cd * _window(bufd, xd, HALO - (CONV_D - 1) + k, t))
        gd[t:t + HALO, :] = d_cd[0:HALO, :]
        put(C_D1, d_dg * s2)
        put(C_D2, d_dg * cur(C_D1) * s2 * (1.0 - s2))
        dp_ref[:, C_Q:C_Q + BW] = dq_ref[...]
        dkp2 = jnp.where(step == 0, 0.0, dkp2_ref[...].astype(F32))
        dp_ref[0:ATT_BLK, C_K:C_K + 256] = (dkc_ref[0:ATT_BLK, :].astype(F32) + dkp1_ref[...].astype(F32)).astype(BF16)
        dp_ref[ATT_BLK:t, C_K:C_K + 256] = (dkc_ref[ATT_BLK:t, :].astype(F32) + dkp2).astype(BF16)

    rev = lambda i: nt - 1 - i
    full = lambda r, c: pl.BlockSpec((r, c), lambda i: (0, 0))
    return _call(
        body, comm, (proj, proj, dpre, h, h, dq, dkc, dkp, dkp, convw, vecs, wx_bd, wa_bd, dproj), grid=(nt,),
        in_specs=[pl.BlockSpec((t, GL0), lambda i: (rev(i), 0)),
                  pl.BlockSpec((HALO, GL0), lambda i: (jnp.maximum(rev(i) * hb - 1, 0), 0)),
                  pl.BlockSpec((t, 4 * BW), lambda i: (rev(i), 0)),
                  pl.BlockSpec((t, BW), lambda i: (rev(i), 0)),
                  pl.BlockSpec((8, BW), lambda i: (jnp.maximum(rev(i) * (t // 8) - 1, 0), 0)),
                  pl.BlockSpec((t, BW), lambda i: (rev(i), 0)),
                  pl.BlockSpec((t, 256), lambda i: (rev(i), 0)),
                  pl.BlockSpec((ATT_BLK, 256), lambda i: (2 * rev(i) + 1, 0)),
                  pl.BlockSpec((ATT_BLK, 256), lambda i: (jnp.minimum(2 * rev(i) + 2, nb - 1), 0)),
                  pl.BlockSpec((None, CW_ROWS, BW), lambda i: (l, 0, 0)),
                  pl.BlockSpec((None, V_ROWS, BW), lambda i: (l, 0, 0)),
                  pl.BlockSpec((None, BW, BW), lambda i: (l, 0, 0)),
                  pl.BlockSpec((None, BW, BW), lambda i: (l, 0, 0)),
                  pl.BlockSpec(memory_space=pl.ANY)],
        out_specs=[pl.BlockSpec((t, GL0), lambda i: (rev(i), 0)),
                   full(CW_ROWS, BW), full(V_ROWS, BW), full(BW, BW), full(BW, BW)],
        out_shape=[_sds((s, IN_W), BF16), _sds((CW_ROWS, BW), F32), _sds((V_ROWS, BW), F32),
                   _sds((BW, BW), F32), _sds((BW, BW), F32)],
        scratch_shapes=[pltpu.VMEM((t + HALO, BW), F32)] * 3 + [pltpu.VMEM((7, t + HALO - 8, BW), F32)] * 2
        + [pltpu.VMEM((t + 8, BW), F32), pltpu.VMEM((t + 8, BW), F32)]
        + [pltpu.VMEM((t, BW), F32)] * 3
        + [pltpu.VMEM((t + 8, BW), F32), pltpu.VMEM((t + 8, BW), F32), pltpu.VMEM((t + HALO, BW), F32),
           pltpu.VMEM((1, BW), F32)],
        aliases={13: 0}, name=f"bwd_branch{l}")


def bwd_proj(dproj, x, dh1, g1, wt_in, l, comm=None):
    s = x.shape[0]
    tm = min(1024, s)
    ck = 768
    n_j, n_i = IN_W // ck, s // tm

    def body(dp_ref, x_ref, dh_ref, g_ref, w_ref, dx_ref, dw_ref, st_ref, dxn, xn_b, acc):
        j, i = pl.program_id(0), pl.program_id(1)
        rows = pl.ds(pl.multiple_of(i * tm, tm), tm)
        g = g_ref[l:l + 1, :]

        @pl.when(j == 0)
        def _():
            xv = x_ref[...]
            r = lax.rsqrt(jnp.mean(xv * xv, axis=-1, keepdims=True) + EPS)
            xn_b[rows, :] = (xv * r * g).astype(BF16)
            dxn[rows, :] = jnp.zeros((tm, D), F32)

        @pl.when((j == 0) & (i == 0))
        def _():
            st_ref[...] = jnp.zeros((8, D), F32)

        @pl.when(i == 0)
        def _():
            acc[...] = jnp.zeros((ck, D), F32)

        dp = dp_ref[...]
        dxn[rows, :] += _dot(dp, w_ref[...], _NN)
        acc[...] += _dot(dp, xn_b[rows, :], _TN)

        @pl.when(i == n_i - 1)
        def _():
            dw_ref[...] = acc[...].astype(BF16)

        @pl.when(j == n_j - 1)
        def _():
            xv = x_ref[...]
            r = lax.rsqrt(jnp.mean(xv * xv, axis=-1, keepdims=True) + EPS)
            n = xv * r
            dv = dxn[rows, :]
            dn = dv * g
            dx_ref[...] = dh_ref[...] + r * (dn - n * jnp.mean(dn * n, axis=-1, keepdims=True))
            st_ref[0:1, :] += jnp.sum(dv * n, axis=0, keepdims=True)

    lastrow = lambda j, i: (jnp.where(j == n_j - 1, i, 0), 0)
    return _call(
        body, comm, (dproj, x, dh1, g1, wt_in), grid=(n_j, n_i),
        in_specs=[pl.BlockSpec((tm, ck), lambda j, i: (i, j)),
                  pl.BlockSpec((tm, D), lambda j, i: (_edge_index(j, i, n_j, n_i), 0), pipeline_mode=pl.Buffered(1)),
                  pl.BlockSpec((tm, D), lastrow, pipeline_mode=pl.Buffered(1)),
                  pl.BlockSpec((DEPTH, D), lambda j, i: (0, 0)),
                  pl.BlockSpec((ck, D), lambda j, i: (j, 0))],
        out_specs=[pl.BlockSpec((tm, D), lastrow), pl.BlockSpec((ck, D), lambda j, i: (j, 0)),
                   pl.BlockSpec((8, D), lambda j, i: (0, 0))],
        out_shape=[_sds((s, D), F32), _sds((IN_W, D), BF16), _sds((8, D), F32)],
        scratch_shapes=[pltpu.VMEM((s, D), F32), pltpu.VMEM((s, D), BF16), pltpu.VMEM((ck, D), F32)],
        name=f"bwd_proj{l}")


def bwd_proj_w(dproj, x, g1, l, half, comm=None):
    s = x.shape[0]
    tm = min(1024, s)
    ck = 1408
    hw = D // 2
    n_j, n_i = IN_W // ck, s // tm

    def body(dp_ref, x_ref, g_ref, dw_ref, xn_b, acc):
        j, i = pl.program_id(0), pl.program_id(1)
        rows = pl.ds(pl.multiple_of(i * tm, tm), tm)

        @pl.when(j == 0)
        def _():
            xv = x_ref[...]
            r = lax.rsqrt(jnp.mean(xv * xv, axis=-1, keepdims=True) + EPS)
            xn_b[rows, :] = (xv * r * g_ref[l:l + 1, :])[:, half * hw:(half + 1) * hw].astype(BF16)

        @pl.when(i == 0)
        def _():
            acc[...] = jnp.zeros((ck, hw), F32)

        acc[...] += _dot(dp_ref[...], xn_b[rows, :], _TN)

        @pl.when(i == n_i - 1)
        def _():
            dw_ref[...] = acc[...].astype(BF16)

    return _call(
        body, comm, (dproj, x, g1), grid=(n_j, n_i),
        in_specs=[pl.BlockSpec((tm, ck), lambda j, i: (i, j)),
                  pl.BlockSpec((tm, D), lambda j, i: (jnp.where(j == 0, i, n_i - 1), 0), pipeline_mode=pl.Buffered(1)),
                  pl.BlockSpec((DEPTH, D), lambda j, i: (0, 0))],
        out_specs=pl.BlockSpec((ck, hw), lambda j, i: (j, 0)),
        out_shape=_sds((IN_W, hw), BF16),
        scratch_shapes=[pltpu.VMEM((s, hw), BF16), pltpu.VMEM((ck, hw), F32)],
        name=f"bwd_proj_w{half}_{l}")


def bwd_proj_x(dproj, x, dh1, g1, wt_in, l, comm=None):
    s = x.shape[0]
    tm = min(1024, s)
    ck = 1408
    n_j, n_i = IN_W // ck, s // tm

    def body(dp_ref, x_ref, dh_ref, g_ref, w_ref, dx_ref, st_ref, dxn):
        j, i = pl.program_id(0), pl.program_id(1)
        rows = pl.ds(pl.multiple_of(i * tm, tm), tm)
        g = g_ref[l:l + 1, :]

        @pl.when((j == 0) & (i == 0))
        def _():
            st_ref[...] = jnp.zeros((8, D), F32)

        part = _dot(dp_ref[...], w_ref[...], _NN)

        @pl.when(j == 0)
        def _():
            dxn[rows, :] = part

        @pl.when(j > 0)
        def _():
            dxn[rows, :] += part

        @pl.when(j == n_j - 1)
        def _():
            xv = x_ref[...]
            r = lax.rsqrt(jnp.mean(xv * xv, axis=-1, keepdims=True) + EPS)
            n = xv * r
            dv = dxn[rows, :]
            dn = dv * g
            dx_ref[...] = dh_ref[...] + r * (dn - n * jnp.mean(dn * n, axis=-1, keepdims=True))
            st_ref[0:1, :] += jnp.sum(dv * n, axis=0, keepdims=True)

    lastrow = lambda j, i: (jnp.where(j == n_j - 1, i, 0), 0)
    return _call(
        body, comm, (dproj, x, dh1, g1, wt_in), grid=(n_j, n_i),
        in_specs=[pl.BlockSpec((tm, ck), lambda j, i: (i, j)), pl.BlockSpec((tm, D), lastrow, pipeline_mode=pl.Buffered(1)),
                  pl.BlockSpec((tm, D), lastrow, pipeline_mode=pl.Buffered(1)),
                  pl.BlockSpec((DEPTH, D), lambda j, i: (0, 0)), pl.BlockSpec((ck, D), lambda j, i: (j, 0))],
        out_specs=[pl.BlockSpec((tm, D), lastrow), pl.BlockSpec((8, D), lambda j, i: (0, 0))],
        out_shape=[_sds((s, D), F32), _sds((8, D), F32)],
        scratch_shapes=[pltpu.VMEM((s, D), F32)], name=f"bwd_proj_x{l}")


def _block_diag(w):
    nl, nb, bw, _ = w.shape
    eye = jnp.eye(nb, dtype=w.dtype)
    return jnp.einsum("lhij,hk->lhikj", w, eye).reshape(nl, nb * bw, nb * bw).astype(BF16)


class NoOverlap:
    def __init__(self, big):
        self.big = big

    def weights(self, l):
        return self.big[l]

    def job(self, slot, l):
        return None

    def done(self, slot, l, results):
        pass

    def new_grads(self, group, l, grads):
        pass


def local_step(x, target, norm1_g, norm2_g, final_g, convw, vecs, lru_wx, lru_wa, plan):
    wx_bd, wa_bd = _block_diag(lru_wx), _block_diag(lru_wa)

    def run(fn, slot, l, *args):
        res, cres = fn(*args, l, comm=plan.job(slot, l))
        plan.done(slot, l, cres)
        return res

    saved = []
    for l in range(DEPTH):
        proj = run(fwd_proj, "fwd_proj", l, x, norm1_g, plan.weights(l)["in_t"])
        pre_abd, h = run(fwd_branch, "fwd_branch", l, proj, convw, vecs, wx_bd, wa_bd)
        pre_c = run(fwd_attn, "fwd_attn", l, proj, vecs)
        w = plan.weights(l)
        y4, merged, h1 = run(fwd_merge, "fwd_merge", l, x, proj, pre_abd, pre_c, w["a_t"], w["b_t"], w["c_t"], w["d_t"], w["o"])
        w = plan.weights(l)
        x_out, fg, fu = run(fwd_ffn, "fwd_ffn", l, h1, norm2_g, w["gate_t"], w["up_t"], w["down"])
        saved.append((x, proj, pre_abd, h, pre_c, y4, merged, h1, fg, fu))
        x = x_out
    dx, head_stats = loss_head(x, final_g.reshape(1, D), target)
    small = [None] * DEPTH
    for l in reversed(range(DEPTH)):
        x_in, proj, pre_abd, h, pre_c, y4, merged, h1, fg, fu = saved[l]
        w = plan.weights(l)
        dh1, d_gate, d_up, d_down, st_ffn = run(bwd_ffn, "bwd_ffn", l, dx, h1, fg, fu, norm2_g, w["gate_t"], w["up_t"], w["down"])
        plan.new_grads("ffn", l, dict(gate_t=d_gate, up_t=d_up, down=d_down))
        dproj, dpre, d_o, d_a, d_b, d_c, d_d = run(
            bwd_merge, "bwd_merge", l, dh1, y4, proj, merged, pre_abd, pre_c, w["a_t"], w["b_t"], w["c_t"], w["d_t"], w["o"])
        plan.new_grads("out", l, dict(a_t=d_a, b_t=d_b, c_t=d_c, d_t=d_d, o=d_o))
        dq, dkc, dkp, st_attn = run(bwd_attn, "bwd_attn", l, proj, dpre, vecs)
        dproj, dcw, dvec, dwx, dwa = run(bwd_branch, "bwd_branch", l, proj, dproj, dpre, h, dq, dkc, dkp, convw, vecs, wx_bd, wa_bd)
        if l > 0:
            dx, d_in, st_proj = run(bwd_proj, "bwd_proj", l, dproj, x_in, dh1, norm1_g, w["in_t"])
            plan.new_grads("in", l, dict(in_t=d_in))
        else:
            for half, name in enumerate(("in_a", "in_b")):
                d_half = run(functools.partial(bwd_proj_w, half=half), f"bwd_proj_w{half}", l, dproj, x_in, norm1_g)
                plan.new_grads(name, l, {name: d_half})
            dx, st_proj = run(bwd_proj_x, "bwd_proj_x", l, dproj, x_in, dh1, norm1_g, w["in_t"])
        small[l] = (st_proj, st_ffn, dvec, st_attn, dcw, dwx, dwa)
    return head_stats, dx, small


BIG = dict(in_t=("w_in", "view"), a_t=("w_a_out", "transpose"), b_t=("w_b_out", "transpose"), c_t=("w_c_out", "transpose"),
           d_t=("w_d_out", "transpose"), o=("w_o", "plain"), gate_t=("w_ffn_gate", "view"), up_t=("w_ffn_up", "view"),
           down=("w_ffn_down", "plain"))


def cast_transpose(w, name):
    nl, a, b = w.shape
    ta = min(256, a)

    def body(w_ref, o_ref):
        o_ref[...] = w_ref[...].T.astype(BF16)

    return pl.pallas_call(
        body, grid=(nl, a // ta),
        in_specs=[pl.BlockSpec((None, ta, b), lambda l, i: (l, i, 0))],
        out_specs=pl.BlockSpec((None, b, ta), lambda l, i: (l, 0, i)),
        out_shape=_sds((nl, b, a), BF16), compiler_params=_cparams(2), name=name)(w)


def add_partials(mine, recv, core, name):
    n = len(mine)

    def body(core_ref, *refs):
        del core_ref
        for a_ref, b_ref, o_ref in zip(refs[:n], refs[n:2 * n], refs[2 * n:]):
            o_ref[...] = (a_ref[...].astype(F32) + b_ref[...].astype(F32)).astype(BF16)

    return pl.pallas_call(
        body,
        grid_spec=pltpu.PrefetchScalarGridSpec(
            num_scalar_prefetch=1, grid=(4,),
            in_specs=[pl.BlockSpec((None, None) + a.shape[2:], lambda i, cr: (i, cr[0], 0, 0)) for a in mine]
            + [pl.BlockSpec((None,) + b.shape[1:], lambda i, cr: (i, 0, 0)) for b in recv],
            out_specs=[pl.BlockSpec((None,) + b.shape[1:], lambda i, cr: (i, 0, 0)) for b in recv]),
        out_shape=[_sds(b.shape, BF16) for b in recv], compiler_params=_cparams(1), name=name)(core, *mine, *recv)


def _adamw(w, g, m, v):
    m = ADAM_B1 * m + (1.0 - ADAM_B1) * g
    v = ADAM_B2 * v + (1.0 - ADAM_B2) * (g * g)
    m_hat = m / (1.0 - ADAM_B1 ** ADAM_STEP)
    v_hat = v / (1.0 - ADAM_B2 ** ADAM_STEP)
    delta = -ADAM_LR * (m_hat / (jnp.sqrt(v_hat) + ADAM_EPS) + ADAM_WD * w)
    return delta, m, v


def adamw_big(contrib, w, m, v, transposed, name, comm=None):
    nsrc, nl, rows, cols = contrib.shape
    ct = 256

    def body(c_ref, w_ref, m_ref, v_ref, g_out, d_out, m_out, v_out):
        g = c_ref[0].astype(F32)
        for src in range(1, nsrc):
            g = g + c_ref[src].astype(F32)
        if transposed:
            g = g.T
        delta, mn, vn = _adamw(w_ref[...], g, m_ref[...], v_ref[...])
        g_out[...] = g
        d_out[...] = delta
        m_out[...] = mn
        v_out[...] = vn

    if transposed:
        wspec = pl.BlockSpec((None, ct, rows), lambda l, j: (l, j, 0))
    else:
        wspec = pl.BlockSpec((None, rows, ct), lambda l, j: (l, 0, j))
    return _call(
        body, comm, (contrib, w, m, v), grid=(nl, cols // ct),
        in_specs=[pl.BlockSpec((nsrc, None, rows, ct), lambda l, j: (0, l, 0, j)), wspec, wspec, wspec],
        out_specs=[wspec] * 4, out_shape=[_sds(w.shape, F32)] * 4, name=name)


VEC_NAMES = ("conv_a_b", "lru_bx", "lru_ba", "lru_lambda", "conv_d_b", "ln_d_g", "ln_d_b")
P_N1, P_N2, P_VEC, P_CONV, P_LRU = 0, 1, 2, 6, 6 + CW_ROWS
P_LAYER = P_LRU + HD
P_FINAL, P_LOSS, P_ROWS = DEPTH * P_LAYER, DEPTH * P_LAYER + 1, 8 * ((DEPTH * P_LAYER + 2 + 7) // 8)
SMALL = ("norm1_g", "conv_a_w", "conv_a_b", "lru_wx", "lru_bx", "lru_wa", "lru_ba", "lru_lambda", "conv_b_w", "sinks",
         "conv_d_w", "conv_d_b", "ln_d_g", "ln_d_b", "norm2_g", "final_g")
VMEM_FULL = pl.BlockSpec(memory_space=pltpu.VMEM)


def _stack_vecs(p):
    rows = [p[n] for n in VEC_NAMES] + [jnp.pad(p["sinks"], ((0, 0), (0, BW - N_HEADS)))]
    return jnp.stack(rows, axis=1)


def _stack_convs(p):
    nl, _, ch = p["conv_a_w"].shape
    z = jnp.zeros((nl, 1, ch), F32)
    return jnp.concatenate([p["conv_a_w"], p["conv_b_w"], z, p["conv_d_w"], z], axis=1)


def _vec_place(l, r):
    return l * P_LAYER + P_VEC + r // 2, (r % 2) * BW


def pack_small(per_layer, head_stats):
    n = len(per_layer[0])

    def body(*refs):
        head_ref, pack = refs[DEPTH * n], refs[DEPTH * n + 1]
        pack[...] = jnp.zeros((P_ROWS, D), F32)
        lane = lax.broadcasted_iota(jnp.int32, (HD, BW), 1)
        for l in range(DEPTH):
            st_proj, st_ffn, dvec, st_attn, dcw, dwx, dwa = refs[l * n:(l + 1) * n]
            b = l * P_LAYER
            pack[b + P_N1:b + P_N1 + 1, :] = st_proj[0:1, :]
            pack[b + P_N2:b + P_N2 + 1, :] = st_ffn[0:1, :]
            for r in range(len(VEC_NAMES)):
                row, c0 = _vec_place(l, r)
                pack[row:row + 1, c0:c0 + BW] = dvec[r:r + 1, :]
            row, c0 = _vec_place(l, V_SINK)
            pack[row:row + 1, c0:c0 + 128] = st_attn[0:1, :]
            pack[b + P_CONV:b + P_CONV + CW_ROWS, 0:BW] = dcw[...]
            for mat, c0 in ((dwx, 0), (dwa, BW)):
                blocks = jnp.zeros((HD, BW), F32)
                for h in range(BW // HD):
                    blocks = jnp.where((lane >= HD * h) & (lane < HD * (h + 1)), mat[HD * h:HD * (h + 1), :], blocks)
                pack[b + P_LRU:b + P_LRU + HD, c0:c0 + BW] = blocks
        pack[P_FINAL:P_FINAL + 1, :] = head_ref[0:1, :]
        pack[P_LOSS:P_LOSS + 1, :] = head_ref[1:2, :]

    flat = [a for layer in per_layer for a in layer] + [head_stats]
    return pl.pallas_call(body, out_shape=_sds((P_ROWS, D), F32), in_specs=[VMEM_FULL] * len(flat), out_specs=VMEM_FULL,
                          name="pack_small", compiler_params=pltpu.CompilerParams(vmem_limit_bytes=VMEM_LIMIT))(*flat)


def adamw_small(gathered, me, w, m, v):
    ns = len(SMALL)

    def body(me_ref, c_ref, *refs):
        w_refs, m_refs, v_refs = refs[:ns], refs[ns:2 * ns], refs[2 * ns:3 * ns]
        loss_ref, outs, gs = refs[3 * ns], refs[3 * ns + 1:3 * ns + 1 + 4 * ns], refs[-1]
        gs[...] = c_ref[0]
        for dev in range(1, NDEV):
            gs[...] += c_ref[dev]
        loss_ref[...] = gs[P_LOSS:P_LOSS + 1, 0:128]

        def update(name, sel, g):
            i = SMALL.index(name)
            delta, mn, vn = _adamw(w_refs[i][sel], g, m_refs[i][sel], v_refs[i][sel])
            for o_ref, val in zip(outs[4 * i:4 * i + 4], (g, delta, mn, vn)):
                o_ref[sel] = val

        update("final_g", (slice(0, 1), slice(None)), gs[P_FINAL:P_FINAL + 1, :])
        shift = (BW - me_ref[0] * (BW // NDEV)) & (BW - 1)
        for l in range(DEPTH):
            b = l * P_LAYER
            row = (slice(l, l + 1), slice(None))
            update("norm1_g", row, gs[b + P_N1:b + P_N1 + 1, :])
            update("norm2_g", row, gs[b + P_N2:b + P_N2 + 1, :])
            for r, name in enumerate(VEC_NAMES):
                prow, c0 = _vec_place(l, r)
                update(name, row, gs[prow:prow + 1, c0:c0 + BW])
            prow, c0 = _vec_place(l, V_SINK)
            update("sinks", row, gs[prow:prow + 1, c0:c0 + N_HEADS])
            mine = pltpu.roll(gs[b + P_CONV:b + P_CONV + CW_ROWS, 0:BW], shift, 1)[:, 0:BW // NDEV]
            update("conv_a_w", (l,), mine[CW_A:CW_A + CONV_A])
            update("conv_b_w", (l,), mine[CW_B:CW_B + CONV_B])
            update("conv_d_w", (l,), mine[CW_D:CW_D + CONV_D])
            for h in range(BW // HD):
                update("lru_wx", (l, h), gs[b + P_LRU:b + P_LRU + HD, HD * h:HD * (h + 1)])
                update("lru_wa", (l, h), gs[b + P_LRU:b + P_LRU + HD, BW + HD * h:BW + HD * (h + 1)])

    args = [p[n] for p in (w, m, v) for n in SMALL]
    full = lambda a: pl.BlockSpec(a.shape, lambda i, me_ref: (0,) * a.ndim)
    out_shape = [_sds((1, 128), F32)] + [_sds(w[n].shape, F32) for n in SMALL for _ in range(4)]
    outs = pl.pallas_call(
        body,
        grid_spec=pltpu.PrefetchScalarGridSpec(
            num_scalar_prefetch=1, grid=(1,),
            in_specs=[full(gathered)] + [full(a) for a in args], out_specs=[full(o) for o in out_shape],
            scratch_shapes=[pltpu.VMEM((P_ROWS, D), F32)]),
        out_shape=out_shape, name="adamw_small", compiler_params=_cparams(1))(me, gathered, *args)
    return outs[0], {n: outs[1 + 4 * i:5 + 4 * i] for i, n in enumerate(SMALL)}


def merge_jobs(jobs):
    jobs = [j for j in jobs if j is not None]
    if not jobs:
        return None, []
    inputs, aliases, outs, sems, cuts = [], {}, [], [], []
    for j in jobs:
        i0, o0, s0 = len(inputs), len(outs), len(sems)
        aliases.update({i0 + i: o0 + o for i, o in j.aliases.items()})
        inputs += j.inputs
        outs += j.out_shapes
        sems += j.sem_shapes
        cuts.append((i0, len(inputs), o0, len(outs), s0, len(sems)))

    def each(which):
        def go(cins, couts, s):
            for j, (i0, i1, o0, o1, s0, s1) in zip(jobs, cuts):
                getattr(j, which)(cins[i0:i1], couts[o0:o1], s[s0:s1])
        return go

    return CommJob(inputs, aliases, outs, sems, each("start"), each("finish")), [(c[2], c[3]) for c in cuts]


SIXTHS = 6
OUT_KINDS = ("a_t", "b_t", "c_t", "d_t", "o")
GATHER_PLAN = {
    "fwd_proj": [(k, 0, 0, 6) for k in OUT_KINDS] + [("gate_t", 0, 0, 3)],
    "fwd_branch": [("gate_t", 0, 3, 6), ("up_t", 0, 0, 3)],
    "fwd_attn": [("up_t", 0, 3, 6), ("down", 0, 0, 6)],
    "fwd_merge": [("in_t", 1, 0, 2)],
    "fwd_ffn": [("in_t", 1, 2, 6)],
}
SIBLING_PLAN = {"bwd_merge": ("ffn", 0), "bwd_branch": ("out", 0), "bwd_ffn": ("in", 1),
                "bwd_proj_w1": ("in_a", 0), "bwd_proj_x": ("in_b", 0)}
GROUPS = dict(ffn=("gate_t", "up_t", "down"), out=OUT_KINDS, in_a=("in_a",), in_b=("in_b",))
GROUPS["in"] = ("in_t",)
COLUMN_HALF = dict(in_a=("in_t", 0), in_b=("in_t", D // 2))
CHIP_PLAN = {
    "bwd_attn": [("in_t", 1, 3, 6), ("gate_t", 0, 0, 3)],
    "bwd_branch": [("gate_t", 0, 3, 6), ("up_t", 0, 0, 6), ("down", 0, 0, 6)],
    "bwd_proj": [(k, 0, 0, 6) for k in OUT_KINDS],
    "bwd_proj_w0": [(k, 0, 0, 6) for k in OUT_KINDS[:3]],
    "bwd_proj_w1": [(k, 0, 0, 6) for k in OUT_KINDS[3:]],
    "bwd_merge": [("in_t", 1, 0, 3)],
    "bwd_proj_x": [("in_a", 0, 0, 6)],
    "adamw_gate_t": [("in_b", 0, 0, 3)], "adamw_up_t": [("in_b", 0, 3, 6)],
}
SMALL_GATHER_SLOT = "adamw_down"


class Overlap:
    def __init__(self, shards, core):
        self.shards = shards
        self.core = core
        self.gathered = [dict.fromkeys(BIG) for _ in range(DEPTH)]
        self.views = {}
        self.partial = {}
        self.contrib = dict.fromkeys(BIG)
        self.small_pack = self.small_gathered = None
        self._open = None

    def weights(self, l):
        return self.gathered[l]

    def new_grads(self, group, l, grads):
        for k, g in grads.items():
            self.views[k, l] = g.reshape(4, 2, g.shape[0] // NDEV, g.shape[1])

    @staticmethod
    def _rows(shard_rows, f0, f1):
        return shard_rows * f0 // SIXTHS, shard_rows * (f1 - f0) // SIXTHS

    def job(self, slot, l):
        jobs, notes = [], []
        pieces = [(k, l + dl, f0, f1) for k, dl, f0, f1 in GATHER_PLAN.get(slot, []) if l + dl < DEPTH]
        if pieces:
            jobs.append(gather_job([((k, ll), self.shards[ll][k], self.gathered[ll][k],
                                     *self._rows(self.shards[ll][k].shape[0], f0, f1)) for k, ll, f0, f1 in pieces]))
            notes.append(("gather", list(dict.fromkeys((k, ll) for k, ll, _, _ in pieces))))
        if slot in SIBLING_PLAN and l + SIBLING_PLAN[slot][1] < DEPTH:
            group, dl = SIBLING_PLAN[slot]
            keys = [(k, l + dl) for k in GROUPS[group]]
            jobs.append(sibling_exchange_job([self.views[key] for key in keys]))
            notes.append(("sibling", keys))
        pieces = [(k, l + dl, f0, f1) for k, dl, f0, f1 in CHIP_PLAN.get(slot, []) if l + dl < DEPTH]
        if pieces:
            whole = [(*COLUMN_HALF.get(k, (k, 0)), k, ll, f0, f1) for k, ll, f0, f1 in pieces]
            jobs.append(chip_exchange_job([(self.partial[k, ll], self.contrib[kind], kind, ll,
                                            *self._rows(self.partial[k, ll].shape[1], f0, f1), col0, self.shards[ll][kind].shape[1])
                                           for kind, col0, k, ll, f0, f1 in whole]))
            notes.append(("chips", list(dict.fromkeys(kind for kind, *_ in whole))))
        if slot == SMALL_GATHER_SLOT:
            jobs.append(gather_job([("small", self.small_pack, None, 0, self.small_pack.shape[0])]))
            notes.append(("small", None))
        job, spans = merge_jobs(jobs)
        self._open = (slot, l, notes, spans)
        return job

    def done(self, slot, l, results):
        open_slot, open_l, notes, spans = self._open
        assert (open_slot, open_l) == (slot, l)
        for (what, keys), (r0, r1) in zip(notes, spans):
            res = results[r0:r1]
            if what == "gather":
                for (k, ll), g in zip(keys, res):
                    self.gathered[ll][k] = g
            elif what == "sibling":
                sums = add_partials([self.views[key] for key in keys], list(res), self.core, f"chip_sum_{keys[0][0]}{keys[0][1]}")
                self.partial.update(zip(keys, sums))
            elif what == "chips":
                for k, c in zip(keys, res):
                    self.contrib[k] = c
            else:
                self.small_gathered, = res


SMALL = ("norm1_g", "conv_a_w", "conv_a_b", "lru_wx", "lru_bx", "lru_wa", "lru_ba", "lru_lambda", "conv_b_w", "sinks",
         "conv_d_w", "conv_d_b", "ln_d_g", "ln_d_b", "norm2_g", "final_g")
WEIGHTS = ("norm1_g", "w_in", "conv_a_w", "conv_a_b", "lru_wx", "lru_bx", "lru_wa", "lru_ba", "lru_lambda", "w_a_out",
           "conv_b_w", "w_b_out", "sinks", "w_c_out", "conv_d_w", "conv_d_b", "ln_d_g", "ln_d_b", "w_d_out", "w_o",
           "norm2_g", "w_ffn_gate", "w_ffn_up", "w_ffn_down", "final_g")


def kernel(x, norm1_g, w_in, conv_a_w, conv_a_b, lru_wx, lru_bx, lru_wa, lru_ba, lru_lambda, w_a_out, conv_b_w, w_b_out, sinks, w_c_out, conv_d_w, conv_d_b, ln_d_g, ln_d_b, w_d_out, w_o, norm2_g, w_ffn_gate, w_ffn_up, w_ffn_down, final_g, loss_target, m_norm1_g, m_w_in, m_conv_a_w, m_conv_a_b, m_lru_wx, m_lru_bx, m_lru_wa, m_lru_ba, m_lru_lambda, m_w_a_out, m_conv_b_w, m_w_b_out, m_sinks, m_w_c_out, m_conv_d_w, m_conv_d_b, m_ln_d_g, m_ln_d_b, m_w_d_out, m_w_o, m_norm2_g, m_w_ffn_gate, m_w_ffn_up, m_w_ffn_down, m_final_g, v_norm1_g, v_w_in, v_conv_a_w, v_conv_a_b, v_lru_wx, v_lru_bx, v_lru_wa, v_lru_ba, v_lru_lambda, v_w_a_out, v_conv_b_w, v_w_b_out, v_sinks, v_w_c_out, v_conv_d_w, v_conv_d_b, v_ln_d_g, v_ln_d_b, v_w_d_out, v_w_o, v_norm2_g, v_w_ffn_gate, v_w_ffn_up, v_w_ffn_down, v_final_g):
    args = dict(locals())
    w = {n: args[n] for n in WEIGHTS}
    m = {n: args["m_" + n] for n in WEIGHTS}
    v = {n: args["v_" + n] for n in WEIGHTS}
    me = _dev_index(*_mesh_pos())

    def rows_major(a, how):
        return jnp.swapaxes(a, 1, 2) if how == "view" else a

    stacked = {k: cast_transpose(w[n], "prep_" + k) if how == "transpose" else rows_major(w[n], how).astype(BF16)
               for k, (n, how) in BIG.items()}
    plan = Overlap([{k: stacked[k][l] for k in BIG} for l in range(DEPTH)], lax.axis_index("c").astype(jnp.int32).reshape(1))
    convs = _stack_convs(w).reshape(DEPTH * CW_ROWS, BW // NDEV)
    g_in0, g_conv = _comm_only(gather_job([(("in_t", 0), plan.shards[0]["in_t"], None, 0, plan.shards[0]["in_t"].shape[0]),
                                           ("convs", convs, None, 0, convs.shape[0])]), "gather_first")
    plan.gathered[0]["in_t"] = g_in0
    convw = g_conv.reshape(NDEV, DEPTH, CW_ROWS, BW // NDEV).transpose(1, 2, 0, 3).reshape(DEPTH, CW_ROWS, BW)

    vecs = _stack_vecs(w)
    head_stats, grad_x, grads = local_step(x[0], loss_target[0], norm1_g, norm2_g, final_g, convw, vecs, lru_wx, lru_wa, plan)

    plan.small_pack = pack_small(grads, head_stats)

    out = {}
    for k in ("down", "gate_t", "up_t", "o", "a_t", "b_t", "c_t", "d_t", "in_t"):
        n, how = BIG[k]
        res, cres = adamw_big(plan.contrib[k], rows_major(w[n], how), rows_major(m[n], how), rows_major(v[n], how),
                              how == "transpose", "adamw_" + k, comm=plan.job("adamw_" + k, 0))
        plan.done("adamw_" + k, 0, cres)
        out[n] = [rows_major(r, how) for r in res]

    def own_shapes(p):
        return {n: p[n].reshape(1, D) if n == "final_g" else p[n] for n in SMALL}

    loss, small = adamw_small(plan.small_gathered.reshape(NDEV, P_ROWS, D), me.astype(jnp.int32).reshape(1),
                              own_shapes(w), own_shapes(m), own_shapes(v))
    for n in SMALL:
        out[n] = [r.reshape(w[n].shape) for r in small[n]]
    loss = loss[0, 0]
    return (loss, grad_x[None], *[out[n][0] for n in WEIGHTS], *[out[n][1] for n in WEIGHTS],
            *[out[n][2] for n in WEIGHTS], *[out[n][3] for n in WEIGHTS])
```

```python
import functools

import jax
import jax.numpy as jnp
from jax import lax
from jax.experimental import pallas as pl
from jax.experimental.pallas import tpu as pltpu

F32 = jnp.float32
BF16 = jnp.bfloat16
E = pl.Element

D = 1024
BW = 512
IN_W = 8448
GL0 = 4352
FF = 2816
N_HEADS = 8
N_KV = 2
HD = 64
ATT_BLK = 128
EPS = 1e-6
LRU_C = 8.0
NEG_INF = -1e30
DEPTH = 2
NDEV = 8
CONV_A, CONV_B, CONV_D = 4, 3, 31
C_AX, C_AG, C_BV, C_BC, C_BB, C_Q, C_K, C_V, C_D1, C_D2 = 0, 512, 1024, 1536, 2048, 2560, 3072, 3200, 3328, 3840
CW_A, CW_B, CW_D, CW_ROWS = 0, 4, 8, 40
V_CAB, V_BX, V_BA, V_LAM, V_CDB, V_LNG, V_LNB, V_SINK, V_ROWS = 0, 1, 2, 3, 4, 5, 6, 7, 8
HALO = 32
W_IN_PARTS = ((0, 768), (768, 256))

ADAM_LR, ADAM_B1, ADAM_B2, ADAM_EPS, ADAM_WD, ADAM_STEP = 0.001, 0.9, 0.999, 1e-08, 0.01, 10

VMEM_LIMIT = 56 * 1024 * 1024

_NN = (((1,), (0,)), ((), ()))
_NT = (((1,), (1,)), ((), ()))
_TN = (((0,), (0,)), ((), ()))


def _dot(a, b, dims):
    return lax.dot_general(a.astype(BF16), b.astype(BF16), dims, preferred_element_type=F32)


def _cparams(n_axes):
    return pltpu.CompilerParams(dimension_semantics=("arbitrary",) * n_axes, vmem_limit_bytes=VMEM_LIMIT)


def _sds(shape, dtype):
    return jax.ShapeDtypeStruct(tuple(shape), dtype)


def _sigmoid(x):
    return jax.nn.sigmoid(x)


def _neg_expm1(x):
    p = x * (1.0 + x * (0.5 + x * (1.0 / 6.0 + x * (1.0 / 24.0 + x * (1.0 / 120.0)))))
    return jnp.where(x > -0.1, -p, 1.0 - jnp.exp(x))


def _softplus(z):
    return jnp.maximum(z, 0.0) + jnp.log1p(jnp.exp(-jnp.abs(z)))


def _gelu_and_grad(x):
    c = 0.7978845608028654
    inner = c * (x + 0.044715 * x * x * x)
    t = jnp.tanh(inner)
    g = 0.5 * x * (1.0 + t)
    dg = 0.5 * (1.0 + t) + 0.5 * x * (1.0 - t * t) * c * (1.0 + 3.0 * 0.044715 * x * x)
    return g, dg


ANY = pl.BlockSpec(memory_space=pl.ANY)
MESH = pl.DeviceIdType.MESH


def _mesh_pos():
    return lax.axis_index("x"), lax.axis_index("y"), lax.axis_index("c")


def _dev_index(px, py, pc):
    return 4 * px + 2 * py + pc


class CommJob:
    def __init__(self, inputs, aliases, out_shapes, sem_shapes, start, finish):
        self.inputs, self.aliases, self.out_shapes, self.sem_shapes = list(inputs), dict(aliases), list(out_shapes), list(sem_shapes)
        self.start, self.finish = start, finish


def _call(body, comm, args, *, grid, in_specs, out_specs, out_shape, scratch_shapes=(), name, aliases=None):
    single = not isinstance(out_shape, (list, tuple))
    out_specs = [out_specs] if single else list(out_specs)
    out_shape = [out_shape] if single else list(out_shape)
    scratch_shapes = list(scratch_shapes)
    n_in, n_out, n_scr, n_axes = len(in_specs), len(out_shape), len(scratch_shapes), len(grid)
    params = pltpu.CompilerParams(dimension_semantics=("arbitrary",) * n_axes, vmem_limit_bytes=VMEM_LIMIT)
    io_aliases = dict(aliases or {})
    if comm is None:
        outs = pl.pallas_call(body, grid=grid, in_specs=in_specs, out_specs=out_specs, out_shape=out_shape,
                              scratch_shapes=scratch_shapes, input_output_aliases=io_aliases, compiler_params=params,
                              name=name)(*args)
        return (outs[0] if single else outs), []
    c_in, c_out = len(comm.inputs), len(comm.out_shapes)
    io_aliases.update({n_in + i: n_out + o for i, o in comm.aliases.items()})

    def wrapped(*refs):
        ins, cins = refs[:n_in], refs[n_in:n_in + c_in]
        outs = refs[n_in + c_in:n_in + c_in + n_out]
        couts = refs[n_in + c_in + n_out:n_in + c_in + n_out + c_out]
        rest = refs[n_in + c_in + n_out + c_out:]
        scr, sems = rest[:n_scr], rest[n_scr:]
        first = functools.reduce(lambda a, b: a & b, [pl.program_id(a) == 0 for a in range(n_axes)])
        last = functools.reduce(lambda a, b: a & b, [pl.program_id(a) == pl.num_programs(a) - 1 for a in range(n_axes)])

        @pl.when(first)
        def _():
            comm.start(cins, couts, sems)

        body(*ins, *outs, *scr)

        @pl.when(last)
        def _():
            comm.finish(cins, couts, sems)

    outs = pl.pallas_call(
        wrapped, grid=grid, in_specs=list(in_specs) + [ANY] * c_in, out_specs=out_specs + [ANY] * c_out,
        out_shape=out_shape + comm.out_shapes, scratch_shapes=scratch_shapes + comm.sem_shapes,
        input_output_aliases=io_aliases, compiler_params=params, name=name)(*args, *comm.inputs)
    res, cres = outs[:n_out], outs[n_out:]
    return (res[0] if single else res), cres


def _comm_only(comm, name):
    c_in, c_out = len(comm.inputs), len(comm.out_shapes)

    def body(*refs):
        cins, couts, sems = refs[:c_in], refs[c_in:c_in + c_out], refs[c_in + c_out:]
        comm.start(cins, couts, sems)
        comm.finish(cins, couts, sems)

    return pl.pallas_call(body, in_specs=[ANY] * c_in, out_specs=[ANY] * c_out, out_shape=comm.out_shapes,
                          scratch_shapes=comm.sem_shapes, input_output_aliases=comm.aliases, name=name)(*comm.inputs)


def gather_job(pieces):
    inputs, aliases, out_shapes, plan, where = [], {}, [], [], {}
    for key, shard, gathered, row0, nrows in pieces:
        if key not in where:
            where[key] = (len(inputs), len(out_shapes))
            inputs.append(shard)
            if gathered is not None:
                aliases[len(inputs)] = len(out_shapes)
                inputs.append(gathered)
            out_shapes.append(_sds((NDEV * shard.shape[0], shard.shape[1]), shard.dtype))
        plan.append((*where[key], shard.shape[0], row0, nrows))
    n = len(plan)

    def copies(cins, couts, sems):
        send_sems, recv_sems, local_sems = sems
        x, y, c = _mesh_pos()
        me, sibling = (x, y, c), (x, y, 1 - c)
        chips = [(1 - x, y), (x, 1 - y), (1 - x, 1 - y)]
        local, first, relay, recv_ici, recv_d2d = [], [], [], [], []
        for p, (i_shard, i_out, rows, row0, nrows) in enumerate(plan):
            src = cins[i_shard].at[pl.ds(row0, nrows), :]

            def slot(dev, i_out=i_out, rows=rows, row0=row0, nrows=nrows):
                return couts[i_out].at[pl.ds(_dev_index(*dev) * rows + row0, nrows), :]

            def copy(g, dev, to, src=None, p=p, slot=slot):
                return pltpu.make_async_remote_copy(
                    src_ref=slot(dev) if src is None else src, dst_ref=slot(dev),
                    send_sem=send_sems.at[g, p], recv_sem=recv_sems.at[g, p], device_id=to, device_id_type=MESH)

            local.append(pltpu.make_async_copy(src, slot(me), local_sems.at[p]))
            first.append(copy(0, me, sibling, src=src))
            recv_d2d.append(copy(0, sibling, me))
            for j, chip in enumerate(chips):
                first.append(copy(1 + j, me, (*chip, c), src=src))
                recv_ici.append(copy(1 + j, (*chip, c), me))
                relay.append(copy(4 + j, (*chip, c), sibling))
                recv_d2d.append(copy(4 + j, (*chip, 1 - c), me))
        return local, first, relay, recv_ici, recv_d2d

    def start(cins, couts, sems):
        local, first, _, _, _ = copies(cins, couts, sems)
        for cp in local + first:
            cp.start()

    def finish(cins, couts, sems):
        local, first, relay, recv_ici, recv_d2d = copies(cins, couts, sems)
        for cp in recv_ici:
            cp.wait_recv()
        for cp in relay:
            cp.start()
        for cp in recv_d2d:
            cp.wait_recv()
        for cp in first + relay:
            cp.wait_send()
        for cp in local:
            cp.wait()

    sem_shapes = [pltpu.SemaphoreType.DMA((7, n)), pltpu.SemaphoreType.DMA((7, n)), pltpu.SemaphoreType.DMA((n,))]
    return CommJob(inputs, aliases, out_shapes, sem_shapes, start, finish)


def sibling_exchange_job(grads):
    n = len(grads)

    def copies(cins, couts, sems):
        send_sems, recv_sems = sems
        x, y, c = _mesh_pos()
        return [pltpu.make_async_remote_copy(
            src_ref=cins[q].at[:, 1 - c], dst_ref=couts[q], send_sem=send_sems.at[q], recv_sem=recv_sems.at[q],
            device_id=(x, y, 1 - c), device_id_type=MESH) for q in range(n)]

    def start(cins, couts, sems):
        for cp in copies(cins, couts, sems):
            cp.start()

    def finish(cins, couts, sems):
        cps = copies(cins, couts, sems)
        for cp in cps:
            cp.wait_recv()
        for cp in cps:
            cp.wait_send()

    return CommJob(grads, {}, [_sds((4,) + g.shape[2:], g.dtype) for g in grads],
                   [pltpu.SemaphoreType.DMA((n,)), pltpu.SemaphoreType.DMA((n,))], start, finish)


def chip_exchange_job(pieces):
    inputs, aliases, out_shapes, plan, where = [], {}, [], [], {}
    for partial, contrib, key, layer, row0, nrows, col0, cols in pieces:
        if key not in where:
            where[key] = len(out_shapes)
            out_shapes.append(_sds((4, DEPTH, partial.shape[1], cols), partial.dtype))
            if contrib is not None:
                aliases[len(inputs)] = where[key]
                inputs.append(contrib)
        plan.append((len(inputs), where[key], layer, row0, nrows, col0, partial.shape[2]))
        inputs.append(partial)
    n = len(plan)

    def copies(cins, couts, sems):
        send_sems, recv_sems, local_sems = sems
        x, y, c = _mesh_pos()
        mine = 2 * x + y
        local, sends, recvs = [], [], []
        for p, (i_in, i_out, layer, row0, nrows, col0, ncols) in enumerate(plan):
            rows, lanes = pl.ds(row0, nrows), pl.ds(col0, ncols)
            local.append(pltpu.make_async_copy(cins[i_in].at[mine, rows, :], couts[i_out].at[mine, layer, rows, lanes],
                                               local_sems.at[p]))
            for j, (cx, cy) in enumerate([(1 - x, y), (x, 1 - y), (1 - x, 1 - y)]):
                theirs = 2 * cx + cy

                def copy(slot_there, j=j, p=p, cx=cx, cy=cy, theirs=theirs, i_in=i_in, i_out=i_out, layer=layer,
                         rows=rows, lanes=lanes):
                    return pltpu.make_async_remote_copy(
                        src_ref=cins[i_in].at[theirs, rows, :], dst_ref=couts[i_out].at[slot_there, layer, rows, lanes],
                        send_sem=send_sems.at[j, p], recv_sem=recv_sems.at[j, p], device_id=(cx, cy, c), device_id_type=MESH)
                sends.append(copy(mine))
                recvs.append(copy(theirs))
        return local, sends, recvs

    def start(cins, couts, sems):
        local, sends, _ = copies(cins, couts, sems)
        for cp in local + sends:
            cp.start()

    def finish(cins, couts, sems):
        local, sends, recvs = copies(cins, couts, sems)
        for cp in recvs:
            cp.wait_recv()
        for cp in sends:
            cp.wait_send()
        for cp in local:
            cp.wait()

    sem_shapes = [pltpu.SemaphoreType.DMA((3, n)), pltpu.SemaphoreType.DMA((3, n)), pltpu.SemaphoreType.DMA((n,))]
    return CommJob(inputs, aliases, out_shapes, sem_shapes, start, finish)


def fwd_proj(x, g1, wt_in, l, comm=None):
    s = x.shape[0]
    tm = min(512, s)
    tn = 1408

    def body(x_ref, g_ref, w_ref, o_ref, xn_ref):
        @pl.when(pl.program_id(1) == 0)
        def _():
            xv = x_ref[...]
            r = lax.rsqrt(jnp.mean(xv * xv, axis=-1, keepdims=True) + EPS)
            xn_ref[...] = (xv * r * g_ref[l:l + 1, :]).astype(BF16)

        o_ref[...] = _dot(xn_ref[...], w_ref[...], _NT).astype(BF16)

    return _call(
        body, comm, (x, g1, wt_in), grid=(s // tm, IN_W // tn),
        in_specs=[pl.BlockSpec((tm, D), lambda i, j: (i, 0)),
                  pl.BlockSpec((DEPTH, D), lambda i, j: (0, 0)),
                  pl.BlockSpec((tn, D), lambda i, j: (j, 0))],
        out_specs=pl.BlockSpec((tm, tn), lambda i, j: (i, j)),
        out_shape=_sds((s, IN_W), BF16),
        scratch_shapes=[pltpu.VMEM((tm, D), BF16)], name=f"fwd_proj{l}")


def _scan_fwd(a_ref, u_ref, h_ref, h0, n_rows):
    row = lax.broadcasted_iota(jnp.int32, (8, BW), 0)

    def body(g, hprev):
        r = pl.multiple_of(g * 8, 8)
        a = a_ref[pl.ds(r, 8), :]
        u = u_ref[pl.ds(r, 8), :]
        for sft in (1, 2, 4):
            a_sh = jnp.where(row >= sft, pltpu.roll(a, sft, 0), 1.0)
            u_sh = jnp.where(row >= sft, pltpu.roll(u, sft, 0), 0.0)
            u = u + a * u_sh
            a = a * a_sh
        h = u + a * hprev
        h_ref[pl.ds(r, 8), :] = h
        return h[7:8, :]

    return lax.fori_loop(0, n_rows // 8, body, h0)


def _scan_bwd(b_ref, g_ref, o_ref, c0, n_rows):
    row = lax.broadcasted_iota(jnp.int32, (8, BW), 0)

    def body(k, cnext):
        r = pl.multiple_of((n_rows // 8 - 1 - k) * 8, 8)
        b = b_ref[pl.ds(r, 8), :]
        g = g_ref[pl.ds(r, 8), :]
        for sft in (1, 2, 4):
            b_sh = jnp.where(row < 8 - sft, pltpu.roll(b, 8 - sft, 0), 1.0)
            g_sh = jnp.where(row < 8 - sft, pltpu.roll(g, 8 - sft, 0), 0.0)
            g = g + b * g_sh
            b = b * b_sh
        o = g + b * cnext
        o_ref[pl.ds(r, 8), :] = o
        return o[0:1, :]

    return lax.fori_loop(0, n_rows // 8, body, c0)


def _shifted_copies(buf, shifted, n_rows):
    for r in range(1, 8):
        shifted[r - 1, 0:n_rows - 8, :] = buf[pl.ds(r, n_rows - 8), :]


def _window(buf, shifted, off, t):
    r = off % 8
    return buf[pl.ds(off, t), :] if r == 0 else shifted[r - 1, pl.ds(off - r, t), :]


def _branch_fwd_math(cur_ref, halo_ref, cw_ref, vec_ref, wx_ref, wa_ref, bufa, bufb, bufd, xd, first, t):
    def halo(c0):
        v = halo_ref[:, c0:c0 + BW].astype(F32)
        return jnp.where(first, 0.0, v)

    def cur(c0):
        return cur_ref[:, c0:c0 + BW].astype(F32)

    out = {}
    bufa[0:HALO, :] = halo(C_AX)
    bufa[HALO:HALO + t, :] = cur(C_AX)
    ca = jnp.zeros((t, BW), F32) + vec_ref[V_CAB:V_CAB + 1, :]
    for k in range(CONV_A):
        ca = ca + cw_ref[CW_A + k:CW_A + k + 1, :] * bufa[pl.ds(HALO - (CONV_A - 1) + k, t), :]
    gi = _sigmoid(_dot(ca, wx_ref[...], _NN) + vec_ref[V_BX:V_BX + 1, :])
    gr = _sigmoid(_dot(ca, wa_ref[...], _NN) + vec_ref[V_BA:V_BA + 1, :])
    sp = _softplus(-vec_ref[V_LAM:V_LAM + 1, :])
    la = -LRU_C * sp * gr
    a = jnp.exp(la)
    mult = jnp.sqrt(_neg_expm1(2.0 * la))
    out.update(ca=ca, gi=gi, gr=gr, sp=sp, a=a, mult=mult)
    bufb[0:HALO, :] = halo(C_BC) * halo(C_BV)
    bufb[HALO:HALO + t, :] = cur(C_BC) * cur(C_BV)
    cb = jnp.zeros((t, BW), F32)
    for k in range(CONV_B):
        cb = cb + cw_ref[CW_B + k:CW_B + k + 1, :] * bufb[pl.ds(HALO - (CONV_B - 1) + k, t), :]
    out.update(cb=cb)
    bufd[0:HALO, :] = halo(C_D1) * _sigmoid(halo(C_D2))
    s2 = _sigmoid(cur(C_D2))
    bufd[HALO:HALO + t, :] = cur(C_D1) * s2
    _shifted_copies(bufd, xd, t + HALO)
    cd = jnp.zeros((t, BW), F32) + vec_ref[V_CDB:V_CDB + 1, :]
    for k in range(CONV_D):
        cd = cd + cw_ref[CW_D + k:CW_D + k + 1, :] * _window(bufd, xd, HALO - (CONV_D - 1) + k, t)
    mu = jnp.mean(cd, axis=-1, keepdims=True)
    xc = cd - mu
    rstd = lax.rsqrt(jnp.mean(xc * xc, axis=-1, keepdims=True) + EPS)
    xh = xc * rstd
    ln = xh * vec_ref[V_LNG:V_LNG + 1, :] + vec_ref[V_LNB:V_LNB + 1, :]
    out.update(s2=s2, xh=xh, rstd=rstd, ln=ln)
    return out


def fwd_branch(proj, convw, vecs, wx_bd, wa_bd, l, comm=None):
    s = proj.shape[0]
    t = min(256, s)

    def body(cur_ref, halo_ref, cw_ref, vec_ref, wx_ref, wa_ref, pre_ref, h_ref, bufa, bufb, bufd, xd, a_s, u_s, hcar):
        first = pl.program_id(0) == 0

        @pl.when(first)
        def _():
            hcar[...] = jnp.zeros((1, BW), F32)

        v = _branch_fwd_math(cur_ref, halo_ref, cw_ref, vec_ref, wx_ref, wa_ref, bufa, bufb, bufd, xd, first, t)
        a_s[...] = v["a"]
        u_s[...] = v["ca"] * v["gi"] * v["mult"]
        hcar[...] = _scan_fwd(a_s, u_s, h_ref, hcar[...], t)
        gg, _ = _gelu_and_grad(cur_ref[:, C_AG:C_AG + BW].astype(F32))
        pre_ref[:, 0:BW] = (h_ref[...] * gg).astype(BF16)
        pre_ref[:, BW:2 * BW] = (cur_ref[:, C_BB:C_BB + BW].astype(F32) * v["cb"]).astype(BF16)
        ln = v["ln"]
        pre_ref[:, 2 * BW:3 * BW] = (ln * _sigmoid(ln)).astype(BF16)

    hb = t // HALO
    return _call(
        body, comm, (proj, proj, convw, vecs, wx_bd, wa_bd), grid=(s // t,),
        in_specs=[pl.BlockSpec((t, GL0), lambda i: (i, 0)),
                  pl.BlockSpec((HALO, GL0), lambda i: (jnp.maximum(i * hb - 1, 0), 0)),
                  pl.BlockSpec((None, CW_ROWS, BW), lambda i: (l, 0, 0)),
                  pl.BlockSpec((None, V_ROWS, BW), lambda i: (l, 0, 0)),
                  pl.BlockSpec((None, BW, BW), lambda i: (l, 0, 0)),
                  pl.BlockSpec((None, BW, BW), lambda i: (l, 0, 0))],
        out_specs=[pl.BlockSpec((t, 3 * BW), lambda i: (i, 0)), pl.BlockSpec((t, BW), lambda i: (i, 0))],
        out_shape=[_sds((s, 3 * BW), BF16), _sds((s, BW), F32)],
        scratch_shapes=[pltpu.VMEM((t + HALO, BW), F32)] * 3 + [pltpu.VMEM((7, t + HALO - 8, BW), F32)]
        + [pltpu.VMEM((t, BW), F32)] * 2 + [pltpu.VMEM((1, BW), F32)],
        name=f"fwd_branch{l}")


GRP = N_HEADS // N_KV


def _attn_mask_bias(first_block):
    shape = (GRP * ATT_BLK, 2 * ATT_BLK)
    qi = lax.broadcasted_iota(jnp.int32, shape, 0) & (ATT_BLK - 1)
    ki = lax.broadcasted_iota(jnp.int32, shape, 1)
    dist = qi + ATT_BLK - ki
    valid = (dist >= 0) & (dist < ATT_BLK) & (jnp.logical_not(first_block) | (ki >= ATT_BLK))
    return dist.astype(F32), valid


def _per_head(hk, values):
    hl = lax.broadcasted_iota(jnp.int32, (GRP * ATT_BLK, 1), 0) // ATT_BLK
    out = values[GRP - 1]
    for j in range(GRP - 2, -1, -1):
        out = jnp.where(hl == j, values[j], out)
    return out


def _attn_probs(q_ref, kvp_ref, kvc_ref, vec_ref, distf, valid):
    kvs = range(N_KV)
    heads = [range(hk * GRP, (hk + 1) * GRP) for hk in kvs]
    q4 = [jnp.concatenate([q_ref[:, h * HD:(h + 1) * HD] for h in heads[hk]], axis=0) for hk in kvs]
    k2 = [jnp.concatenate([kvp_ref[:, hk * HD:(hk + 1) * HD], kvc_ref[:, hk * HD:(hk + 1) * HD]], axis=0) for hk in kvs]
    v2 = [jnp.concatenate([kvp_ref[:, (N_KV + hk) * HD:(N_KV + hk + 1) * HD],
                           kvc_ref[:, (N_KV + hk) * HD:(N_KV + hk + 1) * HD]], axis=0) for hk in kvs]
    slope = [_per_head(hk, [2.0 ** (-8.0 * (h + 1) / N_HEADS) for h in heads[hk]]) for hk in kvs]
    sink = [_per_head(hk, [vec_ref[V_SINK:V_SINK + 1, h:h + 1] for h in heads[hk]]) for hk in kvs]
    sc = [_dot(q4[hk], k2[hk], _NT) for hk in kvs]
    sc = [jnp.where(valid, sc[hk] * (HD ** -0.5) - slope[hk] * distf, NEG_INF) for hk in kvs]
    m = [jnp.maximum(jnp.max(sc[hk], axis=-1, keepdims=True), sink[hk]) for hk in kvs]
    p = [jnp.exp(sc[hk] - m[hk]) for hk in kvs]
    es = [jnp.exp(sink[hk] - m[hk]) for hk in kvs]
    inv = [1.0 / (jnp.sum(p[hk], axis=-1, keepdims=True) + es[hk]) for hk in kvs]
    return [(q4[hk], k2[hk], v2[hk], p[hk] * inv[hk], es[hk] * inv[hk]) for hk in kvs]


def fwd_attn(proj, vecs, l, comm=None):
    s = proj.shape[0]
    nb = s // ATT_BLK

    def body(q_ref, kvp_ref, kvc_ref, vec_ref, o_ref):
        distf, valid = _attn_mask_bias(pl.program_id(0) == 0)
        groups = _attn_probs(q_ref, kvp_ref, kvc_ref, vec_ref, distf, valid)
        outs = [_dot(p, v2, _NN).astype(BF16) for _, _, v2, p, _ in groups]
        for hk, out in enumerate(outs):
            for j in range(GRP):
                h = hk * GRP + j
                o_ref[:, h * HD:(h + 1) * HD] = out[j * ATT_BLK:(j + 1) * ATT_BLK]

    return _call(
        body, comm, (proj, proj, proj, vecs), grid=(nb,),
        in_specs=[pl.BlockSpec((ATT_BLK, BW), lambda i: (i, C_Q // BW)),
                  pl.BlockSpec((ATT_BLK, 256), lambda i: (jnp.maximum(i - 1, 0), C_K // 256)),
                  pl.BlockSpec((ATT_BLK, 256), lambda i: (i, C_K // 256)),
                  pl.BlockSpec((None, V_ROWS, BW), lambda i: (l, 0, 0))],
        out_specs=pl.BlockSpec((ATT_BLK, BW), lambda i: (i, 0)),
        out_shape=_sds((s, BW), BF16), name=f"fwd_attn{l}")


def fwd_merge(x, proj, pre_abd, pre_c, wt_a, wt_b, wt_c, wt_d, w_o, l, comm=None):
    s = x.shape[0]
    tm = min(256, s)

    def body(x_ref, gl_ref, pabd_ref, pc_ref, wa_ref, wb_ref, wc_ref, wd_ref, wo_ref, y_ref, mg_ref, h1_ref):
        pres = (pabd_ref[:, 0:BW], pabd_ref[:, BW:2 * BW], pc_ref[...], pabd_ref[:, 2 * BW:3 * BW])
        merged = jnp.zeros((tm, D), F32)
        for k, (pre, w_ref) in enumerate(zip(pres, (wa_ref, wb_ref, wc_ref, wd_ref))):
            yk = _dot(pre, w_ref[...], _NT)
            y_ref[:, k * D:(k + 1) * D] = yk.astype(BF16)
            merged = merged + _sigmoid(gl_ref[:, k * D:(k + 1) * D].astype(F32)) * yk
        mg_ref[...] = merged.astype(BF16)
        h1_ref[...] = x_ref[...] + _dot(merged, wo_ref[...], _NN)

    wspec = pl.BlockSpec((D, BW), lambda i: (0, 0))
    return _call(
        body, comm, (x, proj, pre_abd, pre_c, wt_a, wt_b, wt_c, wt_d, w_o), grid=(s // tm,),
        in_specs=[pl.BlockSpec((tm, D), lambda i: (i, 0)),
                  pl.BlockSpec((E(tm), E(4 * D)), lambda i: (i * tm, GL0)),
                  pl.BlockSpec((tm, 3 * BW), lambda i: (i, 0)),
                  pl.BlockSpec((tm, BW), lambda i: (i, 0)),
                  wspec, wspec, wspec, wspec,
                  pl.BlockSpec((D, D), lambda i: (0, 0))],
        out_specs=[pl.BlockSpec((tm, 4 * D), lambda i: (i, 0)), pl.BlockSpec((tm, D), lambda i: (i, 0)),
                   pl.BlockSpec((tm, D), lambda i: (i, 0))],
        out_shape=[_sds((s, 4 * D), BF16), _sds((s, D), BF16), _sds((s, D), F32)], name=f"fwd_merge{l}")


def fwd_ffn(h1, g2, wt_gate, wt_up, w_down, l, comm=None):
    s = h1.shape[0]
    tm = min(512, s)
    fc = FF // 2

    def body(h_ref, g_ref, wg_ref, wu_ref, wd_ref, xo_ref, fg_ref, fu_ref, hn_ref, acc_ref):
        j = pl.program_id(1)

        @pl.when(j == 0)
        def _():
            hv = h_ref[...]
            r = lax.rsqrt(jnp.mean(hv * hv, axis=-1, keepdims=True) + EPS)
            hn_ref[...] = (hv * r * g_ref[l:l + 1, :]).astype(BF16)
            acc_ref[...] = hv

        fg = _dot(hn_ref[...], wg_ref[...], _NT)
        fu = _dot(hn_ref[...], wu_ref[...], _NT)
        fg_ref[...] = fg.astype(BF16)
        fu_ref[...] = fu.astype(BF16)
        acc_ref[...] += _dot(fg * _sigmoid(fg) * fu, wd_ref[...], _NN)

        @pl.when(j == pl.num_programs(1) - 1)
        def _():
            xo_ref[...] = acc_ref[...]

    wspec = pl.BlockSpec((fc, D), lambda i, j: (j, 0))
    return _call(
        body, comm, (h1, g2, wt_gate, wt_up, w_down), grid=(s // tm, FF // fc),
        in_specs=[pl.BlockSpec((tm, D), lambda i, j: (i, 0)), pl.BlockSpec((DEPTH, D), lambda i, j: (0, 0)),
                  wspec, wspec, wspec],
        out_specs=[pl.BlockSpec((tm, D), lambda i, j: (i, 0)), pl.BlockSpec((tm, fc), lambda i, j: (i, j)),
                   pl.BlockSpec((tm, fc), lambda i, j: (i, j))],
        out_shape=[_sds((s, D), F32), _sds((s, FF), BF16), _sds((s, FF), BF16)],
        scratch_shapes=[pltpu.VMEM((tm, D), BF16), pltpu.VMEM((tm, D), F32)], name=f"fwd_ffn{l}")


def loss_head(x, gf, target):
    s = x.shape[0]
    tm = min(512, s)

    def body(x_ref, g_ref, t_ref, dx_ref, st_ref):
        @pl.when(pl.program_id(0) == 0)
        def _():
            st_ref[...] = jnp.zeros((8, D), F32)

        xv = x_ref[...]
        g = g_ref[...]
        r = lax.rsqrt(jnp.mean(xv * xv, axis=-1, keepdims=True) + EPS)
        n = xv * r
        err = n * g - t_ref[...]
        dy = err * (1.0 / D)
        dn = dy * g
        dx_ref[...] = r * (dn - n * jnp.mean(dn * n, axis=-1, keepdims=True))
        st_ref[0:1, :] += jnp.sum(dy * n, axis=0, keepdims=True)
        lsum = 0.5 * jnp.sum(jnp.mean(err * err, axis=-1, keepdims=True), axis=0, keepdims=True)
        st_ref[1:2, :] += jnp.broadcast_to(lsum, (1, D))

    return pl.pallas_call(
        body, grid=(s // tm,),
        in_specs=[pl.BlockSpec((tm, D), lambda i: (i, 0)), pl.BlockSpec((1, D), lambda i: (0, 0)),
                  pl.BlockSpec((tm, D), lambda i: (i, 0))],
        out_specs=[pl.BlockSpec((tm, D), lambda i: (i, 0)), pl.BlockSpec((8, D), lambda i: (0, 0))],
        out_shape=[_sds((s, D), F32), _sds((8, D), F32)],
        compiler_params=_cparams(1), name="loss_head")(x, gf, target)


def _edge_index(j, i, n_j, n_i):
    return jnp.where((j == 0) | (j == n_j - 1), i, n_i - 1)


def bwd_ffn(dxo, h1, fg, fu, g2, wt_gate, wt_up, w_down, l, comm=None):
    s = h1.shape[0]
    tm = min(512, s)
    fc = 256
    n_j, n_i = FF // fc, s // tm

    def body(dxo_ref, h_ref, fg_ref, fu_ref, g_ref, wg_ref, wu_ref, wd_ref,
             dh_ref, dwg_ref, dwu_ref, dwd_ref, st_ref, dhn, dxo_b, hn_b, ag, au, ad):
        j, i = pl.program_id(0), pl.program_id(1)
        rows = pl.ds(pl.multiple_of(i * tm, tm), tm)
        g = g_ref[l:l + 1, :]

        @pl.when(j == 0)
        def _():
            hv = h_ref[...]
            r = lax.rsqrt(jnp.mean(hv * hv, axis=-1, keepdims=True) + EPS)
            hn_b[rows, :] = (hv * r * g).astype(BF16)
            dxo_b[rows, :] = dxo_ref[...].astype(BF16)
            dhn[rows, :] = jnp.zeros((tm, D), F32)

        @pl.when((j == 0) & (i == 0))
        def _():
            st_ref[...] = jnp.zeros((8, D), F32)

        @pl.when(i == 0)
        def _():
            ag[...] = jnp.zeros((fc, D), F32)
            au[...] = jnp.zeros((fc, D), F32)
            ad[...] = jnp.zeros((fc, D), F32)

        fgv = fg_ref[...].astype(F32)
        fuv = fu_ref[...].astype(F32)
        sg = _sigmoid(fgv)
        sil = fgv * sg
        dxb = dxo_b[rows, :]
        hnb = hn_b[rows, :]
        d_act = _dot(dxb, wd_ref[...], _NT)
        ad[...] += _dot(sil * fuv, dxb, _TN)
        d_fg = (d_act * fuv * (sg * (1.0 + fgv * (1.0 - sg)))).astype(BF16)
        d_fu = (d_act * sil).astype(BF16)
        ag[...] += _dot(d_fg, hnb, _TN)
        au[...] += _dot(d_fu, hnb, _TN)
        dhn[rows, :] += _dot(d_fg, wg_ref[...], _NN) + _dot(d_fu, wu_ref[...], _NN)

        @pl.when(i == n_i - 1)
        def _():
            dwg_ref[...] = ag[...].astype(BF16)
            dwu_ref[...] = au[...].astype(BF16)
            dwd_ref[...] = ad[...].astype(BF16)

        @pl.when(j == n_j - 1)
        def _():
            hv = h_ref[...]
            r = lax.rsqrt(jnp.mean(hv * hv, axis=-1, keepdims=True) + EPS)
            n = hv * r
            dv = dhn[rows, :]
            dn = dv * g
            dh_ref[...] = dxo_ref[...] + r * (dn - n * jnp.mean(dn * n, axis=-1, keepdims=True))
            st_ref[0:1, :] += jnp.sum(dv * n, axis=0, keepdims=True)

    edge = lambda j, i: (_edge_index(j, i, n_j, n_i), 0)
    wspec = pl.BlockSpec((fc, D), lambda j, i: (j, 0))
    dwspec = pl.BlockSpec((fc, D), lambda j, i: (j, 0))
    return _call(
        body, comm, (dxo, h1, fg, fu, g2, wt_gate, wt_up, w_down), grid=(n_j, n_i),
        in_specs=[pl.BlockSpec((tm, D), edge, pipeline_mode=pl.Buffered(1)),
                  pl.BlockSpec((tm, D), edge, pipeline_mode=pl.Buffered(1)),
                  pl.BlockSpec((tm, fc), lambda j, i: (i, j)), pl.BlockSpec((tm, fc), lambda j, i: (i, j)),
                  pl.BlockSpec((DEPTH, D), lambda j, i: (0, 0)), wspec, wspec, wspec],
        out_specs=[pl.BlockSpec((tm, D), lambda j, i: (jnp.where(j == n_j - 1, i, 0), 0)),
                   dwspec, dwspec, dwspec, pl.BlockSpec((8, D), lambda j, i: (0, 0))],
        out_shape=[_sds((s, D), F32), _sds((FF, D), BF16), _sds((FF, D), BF16), _sds((FF, D), BF16), _sds((8, D), F32)],
        scratch_shapes=[pltpu.VMEM((s, D), F32), pltpu.VMEM((s, D), BF16), pltpu.VMEM((s, D), BF16),
                        pltpu.VMEM((fc, D), F32), pltpu.VMEM((fc, D), F32), pltpu.VMEM((fc, D), F32)],
        name=f"bwd_ffn{l}")


def bwd_merge(dh1, y4, proj, merged, pre_abd, pre_c, wt_a, wt_b, wt_c, wt_d, w_o, l, comm=None):
    s = dh1.shape[0]
    tm = min(256, s)
    n_i = s // tm

    def body(dh_ref, y_ref, gl_ref, mg_ref, pabd_ref, pc_ref, wa_ref, wb_ref, wc_ref, wd_ref, wo_ref,
             dgl_ref, dpre_ref, dwo_ref, dwa_ref, dwb_ref, dwc_ref, dwd_ref, ao, aa, ab, ac, ad):
        i = pl.program_id(0)
        accs = (aa, ab, ac, ad)

        @pl.when(i == 0)
        def _():
            ao[...] = jnp.zeros((D, D), F32)
            for acc in accs:
                acc[...] = jnp.zeros((D, BW), F32)

        dhb = dh_ref[...].astype(BF16)
        dmg = _dot(dhb, wo_ref[...], _NT)
        ao[...] += _dot(mg_ref[...], dhb, _TN)
        pres = (pabd_ref[:, 0:BW], pabd_ref[:, BW:2 * BW], pc_ref[...], pabd_ref[:, 2 * BW:3 * BW])
        for k, (pre, w_ref, acc) in enumerate(zip(pres, (wa_ref, wb_ref, wc_ref, wd_ref), accs)):
            gk = _sigmoid(gl_ref[:, k * D:(k + 1) * D].astype(F32))
            yk = y_ref[:, k * D:(k + 1) * D].astype(F32)
            dgl_ref[:, k * D:(k + 1) * D] = (dmg * yk * gk * (1.0 - gk)).astype(BF16)
            dyk = (dmg * gk).astype(BF16)
            dpre_ref[:, k * BW:(k + 1) * BW] = _dot(dyk, w_ref[...], _NN).astype(BF16)
            acc[...] += _dot(dyk, pre, _TN)

        @pl.when(i == n_i - 1)
        def _():
            dwo_ref[...] = ao[...].astype(BF16)
            for o_ref, acc in zip((dwa_ref, dwb_ref, dwc_ref, dwd_ref), accs):
                o_ref[...] = acc[...].astype(BF16)

    wspec = pl.BlockSpec((D, BW), lambda i: (0, 0))
    dwspec = pl.BlockSpec((D, BW), lambda i: (0, 0))
    return _call(
        body, comm, (dh1, y4, proj, merged, pre_abd, pre_c, wt_a, wt_b, wt_c, wt_d, w_o), grid=(n_i,),
        in_specs=[pl.BlockSpec((tm, D), lambda i: (i, 0)),
                  pl.BlockSpec((tm, 4 * D), lambda i: (i, 0)),
                  pl.BlockSpec((E(tm), E(4 * D)), lambda i: (i * tm, GL0)),
                  pl.BlockSpec((tm, D), lambda i: (i, 0)),
                  pl.BlockSpec((tm, 3 * BW), lambda i: (i, 0)),
                  pl.BlockSpec((tm, BW), lambda i: (i, 0)),
                  wspec, wspec, wspec, wspec,
                  pl.BlockSpec((D, D), lambda i: (0, 0))],
        out_specs=[pl.BlockSpec((E(tm), E(4 * D)), lambda i: (i * tm, GL0)),
                   pl.BlockSpec((tm, 4 * BW), lambda i: (i, 0)),
                   pl.BlockSpec((D, D), lambda i: (0, 0)), dwspec, dwspec, dwspec, dwspec],
        out_shape=[_sds((s, IN_W), BF16), _sds((s, 4 * BW), BF16), _sds((D, D), BF16)] + [_sds((D, BW), BF16)] * 4,
        scratch_shapes=[pltpu.VMEM((D, D), F32)] + [pltpu.VMEM((D, BW), F32)] * 4, name=f"bwd_merge{l}")


def bwd_attn(proj, dpre, vecs, l, comm=None):
    s = proj.shape[0]
    nb = s // ATT_BLK
    grp = N_HEADS // N_KV

    def body(q_ref, kvp_ref, kvc_ref, do_ref, vec_ref, dq_ref, dkc_ref, dkp_ref, st_ref):
        @pl.when(pl.program_id(0) == 0)
        def _():
            st_ref[...] = jnp.zeros((8, 128), F32)

        distf, valid = _attn_mask_bias(pl.program_id(0) == 0)
        lane = lax.broadcasted_iota(jnp.int32, (1, 128), 1)
        dsink = jnp.zeros((1, 128), F32)
        groups = _attn_probs(q_ref, kvp_ref, kvc_ref, vec_ref, distf, valid)
        kvs = range(N_KV)
        do4s = [jnp.concatenate([do_ref[:, h * HD:(h + 1) * HD] for h in range(hk * grp, (hk + 1) * grp)], axis=0) for hk in kvs]
        dps = [_dot(do4s[hk], groups[hk][2], _NT) for hk in kvs]
        deltas = [jnp.sum(groups[hk][3] * dps[hk], axis=-1, keepdims=True) for hk in kvs]
        dss = [groups[hk][3] * (dps[hk] - deltas[hk]) * (HD ** -0.5) for hk in kvs]
        for hk in kvs:
            q4, k2, v2, p, ps = groups[hk]
            do4, delta, ds = do4s[hk], deltas[hk], dss[hk]
            dq4 = _dot(ds, k2, _NN).astype(BF16)
            dk2 = _dot(ds, q4, _TN)
            dv2 = _dot(p, do4, _TN)
            psd = ps * delta
            for j in range(grp):
                h = hk * grp + j
                rows = slice(j * ATT_BLK, (j + 1) * ATT_BLK)
                dq_ref[:, h * HD:(h + 1) * HD] = dq4[rows]
                dsink = dsink + jnp.where(lane == h, -jnp.sum(psd[rows], axis=0, keepdims=True), 0.0)
            dkp_ref[:, hk * HD:(hk + 1) * HD] = dk2[0:ATT_BLK].astype(BF16)
            dkc_ref[:, hk * HD:(hk + 1) * HD] = dk2[ATT_BLK:].astype(BF16)
            dkp_ref[:, (N_KV + hk) * HD:(N_KV + hk + 1) * HD] = dv2[0:ATT_BLK].astype(BF16)
            dkc_ref[:, (N_KV + hk) * HD:(N_KV + hk + 1) * HD] = dv2[ATT_BLK:].astype(BF16)
        st_ref[0:1, :] += dsink

    return _call(
        body, comm, (proj, proj, proj, dpre, vecs), grid=(nb,),
        in_specs=[pl.BlockSpec((ATT_BLK, BW), lambda i: (i, C_Q // BW)),
                  pl.BlockSpec((ATT_BLK, 256), lambda i: (jnp.maximum(i - 1, 0), C_K // 256)),
                  pl.BlockSpec((ATT_BLK, 256), lambda i: (i, C_K // 256)),
                  pl.BlockSpec((ATT_BLK, BW), lambda i: (i, 2)),
                  pl.BlockSpec((None, V_ROWS, BW), lambda i: (l, 0, 0))],
        out_specs=[pl.BlockSpec((ATT_BLK, BW), lambda i: (i, 0)), pl.BlockSpec((ATT_BLK, 256), lambda i: (i, 0)),
                   pl.BlockSpec((ATT_BLK, 256), lambda i: (i, 0)), pl.BlockSpec((8, 128), lambda i: (0, 0))],
        out_shape=[_sds((s, BW), BF16), _sds((s, 256), BF16), _sds((s, 256), BF16), _sds((8, 128), F32)],
        name=f"bwd_attn{l}")


def bwd_branch(proj, dproj, dpre, h, dq, dkc, dkp, convw, vecs, wx_bd, wa_bd, l, comm=None):
    s = proj.shape[0]
    t = 2 * ATT_BLK
    nt = s // t
    nb = s // ATT_BLK
    hb = t // HALO

    def body(cur_ref, halo_ref, dpre_ref, h_ref, hp_ref, dq_ref, dkc_ref, dkp1_ref, dkp2_ref,
             cw_ref, vec_ref, wx_ref, wa_ref, dproj_in, dp_ref, dcw_ref, dvec_ref, dwx_ref, dwa_ref,
             bufa, bufb, bufd, xd, xg, a_ext, hbuf, b_s, g_s, dh_s, ga, gb, gd, dhcar):
        del dproj_in
        step = pl.program_id(0)
        ti = nt - 1 - step
        first = ti == 0

        @pl.when(step == 0)
        def _():
            dcw_ref[...] = jnp.zeros((CW_ROWS, BW), F32)
            dvec_ref[...] = jnp.zeros((V_ROWS, BW), F32)
            dwx_ref[...] = jnp.zeros((BW, BW), F32)
            dwa_ref[...] = jnp.zeros((BW, BW), F32)
            dhcar[...] = jnp.zeros((1, BW), F32)
            a_ext[t:t + 8, :] = jnp.zeros((8, BW), F32)
            ga[t:t + 8, :] = jnp.zeros((8, BW), F32)
            gb[t:t + 8, :] = jnp.zeros((8, BW), F32)
            gd[t:t + HALO, :] = jnp.zeros((HALO, BW), F32)

        def cur(c0):
            return cur_ref[:, c0:c0 + BW].astype(F32)

        def rsum(v):
            return jnp.sum(v, axis=0, keepdims=True)

        def put(c0, v):
            dp_ref[:, c0:c0 + BW] = v.astype(BF16)

        v = _branch_fwd_math(cur_ref, halo_ref, cw_ref, vec_ref, wx_ref, wa_ref, bufa, bufb, bufd, xd, first, t)
        ca, gi, gr, sp, a, mult = v["ca"], v["gi"], v["gr"], v["sp"], v["a"], v["mult"]
        dpa = dpre_ref[:, 0:BW].astype(F32)
        gg, dgg = _gelu_and_grad(cur(C_AG))
        hv = h_ref[...]
        put(C_AG, dpa * hv * dgg)
        a_ext[0:t, :] = a
        b_s[...] = a_ext[pl.ds(1, t), :]
        g_s[...] = dpa * gg
        dhcar[...] = _scan_bwd(b_s, g_s, dh_s, dhcar[...], t)
        a_ext[t:t + 1, :] = a[0:1, :]
        dh = dh_s[...]
        hbuf[0:8, :] = jnp.where(first, 0.0, hp_ref[...])
        hbuf[8:8 + t, :] = hv
        da = dh * hbuf[pl.ds(7, t), :]
        d_ca = dh * gi * mult
        d_gi = dh * ca * mult
        d_mult = dh * ca * gi
        d_la = da * a - d_mult * (a * a) / mult
        lam = vec_ref[V_LAM:V_LAM + 1, :]
        dvec_ref[V_LAM:V_LAM + 1, :] += rsum(d_la * gr) * (LRU_C * _sigmoid(-lam))
        d_gr = d_la * (-LRU_C * sp)
        d_zr = d_gr * gr * (1.0 - gr)
        d_zi = d_gi * gi * (1.0 - gi)
        dvec_ref[V_BA:V_BA + 1, :] += rsum(d_zr)
        dvec_ref[V_BX:V_BX + 1, :] += rsum(d_zi)
        dwa_ref[...] += _dot(ca, d_zr, _TN)
        dwx_ref[...] += _dot(ca, d_zi, _TN)
        d_ca = d_ca + _dot(d_zi, wx_ref[...], _NT) + _dot(d_zr, wa_ref[...], _NT)
        dvec_ref[V_CAB:V_CAB + 1, :] += rsum(d_ca)
        ga[0:t, :] = d_ca
        d_ax = jnp.zeros((t, BW), F32)
        for k in range(CONV_A):
            d_ax = d_ax + cw_ref[CW_A + k:CW_A + k + 1, :] * ga[pl.ds(CONV_A - 1 - k, t), :]
            dcw_ref[CW_A + k:CW_A + k + 1, :] += rsum(d_ca * bufa[pl.ds(HALO - (CONV_A - 1) + k, t), :])
        ga[t:t + 8, :] = d_ca[0:8, :]
        put(C_AX, d_ax)
        dpb = dpre_ref[:, BW:2 * BW].astype(F32)
        put(C_BB, dpb * v["cb"])
        d_cb = dpb * cur(C_BB)
        gb[0:t, :] = d_cb
        d_cbin = jnp.zeros((t, BW), F32)
        for k in range(CONV_B):
            d_cbin = d_cbin + cw_ref[CW_B + k:CW_B + k + 1, :] * gb[pl.ds(CONV_B - 1 - k, t), :]
            dcw_ref[CW_B + k:CW_B + k + 1, :] += rsum(d_cb * bufb[pl.ds(HALO - (CONV_B - 1) + k, t), :])
        gb[t:t + 8, :] = d_cb[0:8, :]
        put(C_BC, d_cbin * cur(C_BV))
        put(C_BV, d_cbin * cur(C_BC))
        dpd = dpre_ref[:, 3 * BW:4 * BW].astype(F32)
        ln, xh, rstd, s2 = v["ln"], v["xh"], v["rstd"], v["s2"]
        sg = _sigmoid(ln)
        d_ln = dpd * sg * (1.0 + ln * (1.0 - sg))
        dvec_ref[V_LNG:V_LNG + 1, :] += rsum(d_ln * xh)
        dvec_ref[V_LNB:V_LNB + 1, :] += rsum(d_ln)
        d_xh = d_ln * vec_ref[V_LNG:V_LNG + 1, :]
        d_cd = rstd * (d_xh - jnp.mean(d_xh, axis=-1, keepdims=True)
                       - xh * jnp.mean(d_xh * xh, axis=-1, keepdims=True))
        dvec_ref[V_CDB:V_CDB + 1, :] += rsum(d_cd)
        gd[0:t, :] = d_cd
        _shifted_copies(gd, xg, t + HALO)
        d_dg = jnp.zeros((t, BW), F32)
        for k in range(CONV_D):
            d_dg = d_dg + cw_ref[CW_D + k:CW_D + k + 1, :] * _window(gd, xg, CONV_D - 1 - k, t)
            dcw_ref[CW_D + k:CW_D + k + 1, :] += rsum(d_cd * _window(bufd, xd, HALO - (CONV_D - 1) + k, t))
        gd[t:t + HALO, :] = d_cd[0:HALO, :]
        put(C_D1, d_dg * s2)
        put(C_D2, d_dg * cur(C_D1) * s2 * (1.0 - s2))
        dp_ref[:, C_Q:C_Q + BW] = dq_ref[...]
        dkp2 = jnp.where(step == 0, 0.0, dkp2_ref[...].astype(F32))
        dp_ref[0:ATT_BLK, C_K:C_K + 256] = (dkc_ref[0:ATT_BLK, :].astype(F32) + dkp1_ref[...].astype(F32)).astype(BF16)
        dp_ref[ATT_BLK:t, C_K:C_K + 256] = (dkc_ref[ATT_BLK:t, :].astype(F32) + dkp2).astype(BF16)

    rev = lambda i: nt - 1 - i
    full = lambda r, c: pl.BlockSpec((r, c), lambda i: (0, 0))
    return _call(
        body, comm, (proj, proj, dpre, h, h, dq, dkc, dkp, dkp, convw, vecs, wx_bd, wa_bd, dproj), grid=(nt,),
        in_specs=[pl.BlockSpec((t, GL0), lambda i: (rev(i), 0)),
                  pl.BlockSpec((HALO, GL0), lambda i: (jnp.maximum(rev(i) * hb - 1, 0), 0)),
                  pl.BlockSpec((t, 4 * BW), lambda i: (rev(i), 0)),
                  pl.BlockSpec((t, BW), lambda i: (rev(i), 0)),
                  pl.BlockSpec((8, BW), lambda i: (jnp.maximum(rev(i) * (t // 8) - 1, 0), 0)),
                  pl.BlockSpec((t, BW), lambda i: (rev(i), 0)),
                  pl.BlockSpec((t, 256), lambda i: (rev(i), 0)),
                  pl.BlockSpec((ATT_BLK, 256), lambda i: (2 * rev(i) + 1, 0)),
                  pl.BlockSpec((ATT_BLK, 256), lambda i: (jnp.minimum(2 * rev(i) + 2, nb - 1), 0)),
                  pl.BlockSpec((None, CW_ROWS, BW), lambda i: (l, 0, 0)),
                  pl.BlockSpec((None, V_ROWS, BW), lambda i: (l, 0, 0)),
                  pl.BlockSpec((None, BW, BW), lambda i: (l, 0, 0)),
                  pl.BlockSpec((None, BW, BW), lambda i: (l, 0, 0)),
                  pl.BlockSpec(memory_space=pl.ANY)],
        out_specs=[pl.BlockSpec((t, GL0), lambda i: (rev(i), 0)),
                   full(CW_ROWS, BW), full(V_ROWS, BW), full(BW, BW), full(BW, BW)],
        out_shape=[_sds((s, IN_W), BF16), _sds((CW_ROWS, BW), F32), _sds((V_ROWS, BW), F32),
                   _sds((BW, BW), F32), _sds((BW, BW), F32)],
        scratch_shapes=[pltpu.VMEM((t + HALO, BW), F32)] * 3 + [pltpu.VMEM((7, t + HALO - 8, BW), F32)] * 2
        + [pltpu.VMEM((t + 8, BW), F32), pltpu.VMEM((t + 8, BW), F32)]
        + [pltpu.VMEM((t, BW), F32)] * 3
        + [pltpu.VMEM((t + 8, BW), F32), pltpu.VMEM((t + 8, BW), F32), pltpu.VMEM((t + HALO, BW), F32),
           pltpu.VMEM((1, BW), F32)],
        aliases={13: 0}, name=f"bwd_branch{l}")


def bwd_proj(dproj, x, dh1, g1, wt_in, l, comm=None):
    s = x.shape[0]
    tm = min(512, s)
    ck = 1408
    n_j, n_i = IN_W // ck, s // tm

    def body(dp_ref, x_ref, dh_ref, g_ref, w_ref, dx_ref, dw_ref, st_ref, dxn, xn_b, acc):
        j, i = pl.program_id(0), pl.program_id(1)
        rows = pl.ds(pl.multiple_of(i * tm, tm), tm)
        g = g_ref[l:l + 1, :]

        @pl.when(j == 0)
        def _():
            xv = x_ref[...]
            r = lax.rsqrt(jnp.mean(xv * xv, axis=-1, keepdims=True) + EPS)
            xn_b[rows, :] = (xv * r * g).astype(BF16)
            dxn[rows, :] = jnp.zeros((tm, D), F32)

        @pl.when((j == 0) & (i == 0))
        def _():
            st_ref[...] = jnp.zeros((8, D), F32)

        @pl.when(i == 0)
        def _():
            acc[...] = jnp.zeros((ck, D), F32)

        dp = dp_ref[...]
        dxn[rows, :] += _dot(dp, w_ref[...], _NN)
        acc[...] += _dot(dp, xn_b[rows, :], _TN)

        @pl.when(i == n_i - 1)
        def _():
            dw_ref[...] = acc[...].astype(BF16)

        @pl.when(j == n_j - 1)
        def _():
            xv = x_ref[...]
            r = lax.rsqrt(jnp.mean(xv * xv, axis=-1, keepdims=True) + EPS)
            n = xv * r
            dv = dxn[rows, :]
            dn = dv * g
            dx_ref[...] = dh_ref[...] + r * (dn - n * jnp.mean(dn * n, axis=-1, keepdims=True))
            st_ref[0:1, :] += jnp.sum(dv * n, axis=0, keepdims=True)

    lastrow = lambda j, i: (jnp.where(j == n_j - 1, i, 0), 0)
    return _call(
        body, comm, (dproj, x, dh1, g1, wt_in), grid=(n_j, n_i),
        in_specs=[pl.BlockSpec((tm, ck), lambda j, i: (i, j)),
                  pl.BlockSpec((tm, D), lambda j, i: (_edge_index(j, i, n_j, n_i), 0), pipeline_mode=pl.Buffered(1)),
                  pl.BlockSpec((tm, D), lastrow, pipeline_mode=pl.Buffered(1)),
                  pl.BlockSpec((DEPTH, D), lambda j, i: (0, 0)),
                  pl.BlockSpec((ck, D), lambda j, i: (j, 0))],
        out_specs=[pl.BlockSpec((tm, D), lastrow), pl.BlockSpec((ck, D), lambda j, i: (j, 0)),
                   pl.BlockSpec((8, D), lambda j, i: (0, 0))],
        out_shape=[_sds((s, D), F32), _sds((IN_W, D), BF16), _sds((8, D), F32)],
        scratch_shapes=[pltpu.VMEM((s, D), F32), pltpu.VMEM((s, D), BF16), pltpu.VMEM((ck, D), F32)],
        name=f"bwd_proj{l}")


def bwd_proj_w(dproj, x, g1, l, half, comm=None):
    s = x.shape[0]
    tm = min(1024, s)
    ck = 1408
    c0, hw = W_IN_PARTS[half]
    n_j, n_i = IN_W // ck, s // tm

    def body(dp_ref, x_ref, g_ref, dw_ref, xn_b, acc):
        j, i = pl.program_id(0), pl.program_id(1)
        rows = pl.ds(pl.multiple_of(i * tm, tm), tm)

        @pl.when(j == 0)
        def _():
            xv = x_ref[...]
            r = lax.rsqrt(jnp.mean(xv * xv, axis=-1, keepdims=True) + EPS)
            xn_b[rows, :] = (xv * r * g_ref[l:l + 1, :])[:, c0:c0 + hw].astype(BF16)

        @pl.when(i == 0)
        def _():
            acc[...] = jnp.zeros((ck, hw), F32)

        acc[...] += _dot(dp_ref[...], xn_b[rows, :], _TN)

        @pl.when(i == n_i - 1)
        def _():
            dw_ref[...] = acc[...].astype(BF16)

    return _call(
        body, comm, (dproj, x, g1), grid=(n_j, n_i),
        in_specs=[pl.BlockSpec((tm, ck), lambda j, i: (i, j)),
                  pl.BlockSpec((tm, D), lambda j, i: (jnp.where(j == 0, i, n_i - 1), 0), pipeline_mode=pl.Buffered(1)),
                  pl.BlockSpec((DEPTH, D), lambda j, i: (0, 0))],
        out_specs=pl.BlockSpec((ck, hw), lambda j, i: (j, 0)),
        out_shape=_sds((IN_W, hw), BF16),
        scratch_shapes=[pltpu.VMEM((s, hw), BF16), pltpu.VMEM((ck, hw), F32)],
        name=f"bwd_proj_w{half}_{l}")


def bwd_proj_x(dproj, x, dh1, g1, wt_in, l, comm=None):
    s = x.shape[0]
    tm = min(512, s)
    ck = 1408
    n_j, n_i = IN_W // ck, s // tm

    def body(dp_ref, x_ref, dh_ref, g_ref, w_ref, dx_ref, st_ref, dxn):
        j, i = pl.program_id(0), pl.program_id(1)
        rows = pl.ds(pl.multiple_of(i * tm, tm), tm)
        g = g_ref[l:l + 1, :]

        @pl.when((j == 0) & (i == 0))
        def _():
            st_ref[...] = jnp.zeros((8, D), F32)

        part = _dot(dp_ref[...], w_ref[...], _NN)

        @pl.when(j == 0)
        def _():
            dxn[rows, :] = part

        @pl.when(j > 0)
        def _():
            dxn[rows, :] += part

        @pl.when(j == n_j - 1)
        def _():
            xv = x_ref[...]
            r = lax.rsqrt(jnp.mean(xv * xv, axis=-1, keepdims=True) + EPS)
            n = xv * r
            dv = dxn[rows, :]
            dn = dv * g
            dx_ref[...] = dh_ref[...] + r * (dn - n * jnp.mean(dn * n, axis=-1, keepdims=True))
            st_ref[0:1, :] += jnp.sum(dv * n, axis=0, keepdims=True)

    lastrow = lambda j, i: (jnp.where(j == n_j - 1, i, 0), 0)
    return _call(
        body, comm, (dproj, x, dh1, g1, wt_in), grid=(n_j, n_i),
        in_specs=[pl.BlockSpec((tm, ck), lambda j, i: (i, j)), pl.BlockSpec((tm, D), lastrow, pipeline_mode=pl.Buffered(1)),
                  pl.BlockSpec((tm, D), lastrow, pipeline_mode=pl.Buffered(1)),
                  pl.BlockSpec((DEPTH, D), lambda j, i: (0, 0)), pl.BlockSpec((ck, D), lambda j, i: (j, 0))],
        out_specs=[pl.BlockSpec((tm, D), lastrow), pl.BlockSpec((8, D), lambda j, i: (0, 0))],
        out_shape=[_sds((s, D), F32), _sds((8, D), F32)],
        scratch_shapes=[pltpu.VMEM((s, D), F32)], name=f"bwd_proj_x{l}")


def _block_diag(w):
    nl, nb, bw, _ = w.shape
    eye = jnp.eye(nb, dtype=w.dtype)
    return jnp.einsum("lhij,hk->lhikj", w, eye).reshape(nl, nb * bw, nb * bw).astype(BF16)


class NoOverlap:
    def __init__(self, big):
        self.big = big

    def weights(self, l):
        return self.big[l]

    def job(self, slot, l):
        return None

    def done(self, slot, l, results):
        pass

    def new_grads(self, group, l, grads):
        pass


def local_step(x, target, norm1_g, norm2_g, final_g, convw, vecs, lru_wx, lru_wa, plan):
    wx_bd, wa_bd = _block_diag(lru_wx), _block_diag(lru_wa)

    def run(fn, slot, l, *args):
        res, cres = fn(*args, l, comm=plan.job(slot, l))
        plan.done(slot, l, cres)
        return res

    saved = []
    for l in range(DEPTH):
        proj = run(fwd_proj, "fwd_proj", l, x, norm1_g, plan.weights(l)["in_t"])
        pre_abd, h = run(fwd_branch, "fwd_branch", l, proj, convw, vecs, wx_bd, wa_bd)
        pre_c = run(fwd_attn, "fwd_attn", l, proj, vecs)
        w = plan.weights(l)
        y4, merged, h1 = run(fwd_merge, "fwd_merge", l, x, proj, pre_abd, pre_c, w["a_t"], w["b_t"], w["c_t"], w["d_t"], w["o"])
        w = plan.weights(l)
        x_out, fg, fu = run(fwd_ffn, "fwd_ffn", l, h1, norm2_g, w["gate_t"], w["up_t"], w["down"])
        saved.append((x, proj, pre_abd, h, pre_c, y4, merged, h1, fg, fu))
        x = x_out
    dx, head_stats = loss_head(x, final_g.reshape(1, D), target)
    small = [None] * DEPTH
    for l in reversed(range(DEPTH)):
        x_in, proj, pre_abd, h, pre_c, y4, merged, h1, fg, fu = saved[l]
        w = plan.weights(l)
        dh1, d_gate, d_up, d_down, st_ffn = run(bwd_ffn, "bwd_ffn", l, dx, h1, fg, fu, norm2_g, w["gate_t"], w["up_t"], w["down"])
        plan.new_grads("ffn", l, dict(gate_t=d_gate, up_t=d_up, down=d_down))
        dproj, dpre, d_o, d_a, d_b, d_c, d_d = run(
            bwd_merge, "bwd_merge", l, dh1, y4, proj, merged, pre_abd, pre_c, w["a_t"], w["b_t"], w["c_t"], w["d_t"], w["o"])
        plan.new_grads("out", l, dict(a_t=d_a, b_t=d_b, c_t=d_c, d_t=d_d, o=d_o))
        dq, dkc, dkp, st_attn = run(bwd_attn, "bwd_attn", l, proj, dpre, vecs)
        dproj, dcw, dvec, dwx, dwa = run(bwd_branch, "bwd_branch", l, proj, dproj, dpre, h, dq, dkc, dkp, convw, vecs, wx_bd, wa_bd)
        if l > 0:
            dx, d_in, st_proj = run(bwd_proj, "bwd_proj", l, dproj, x_in, dh1, norm1_g, w["in_t"])
            plan.new_grads("in", l, dict(in_t=d_in))
        else:
            for half, name in enumerate(("in_a", "in_b")):
                d_half = run(functools.partial(bwd_proj_w, half=half), f"bwd_proj_w{half}", l, dproj, x_in, norm1_g)
                plan.new_grads(name, l, {name: d_half})
            dx, st_proj = run(bwd_proj_x, "bwd_proj_x", l, dproj, x_in, dh1, norm1_g, w["in_t"])
        small[l] = (st_proj, st_ffn, dvec, st_attn, dcw, dwx, dwa)
    return head_stats, dx, small


BIG = dict(in_t=("w_in", "view"), a_t=("w_a_out", "transpose"), b_t=("w_b_out", "transpose"), c_t=("w_c_out", "transpose"),
           d_t=("w_d_out", "transpose"), o=("w_o", "plain"), gate_t=("w_ffn_gate", "view"), up_t=("w_ffn_up", "view"),
           down=("w_ffn_down", "plain"))


def cast_transpose(w, name):
    nl, a, b = w.shape
    ta = min(256, a)

    def body(w_ref, o_ref):
        o_ref[...] = w_ref[...].T.astype(BF16)

    return pl.pallas_call(
        body, grid=(nl, a // ta),
        in_specs=[pl.BlockSpec((None, ta, b), lambda l, i: (l, i, 0))],
        out_specs=pl.BlockSpec((None, b, ta), lambda l, i: (l, 0, i)),
        out_shape=_sds((nl, b, a), BF16), compiler_params=_cparams(2), name=name)(w)


def add_partials(mine, recv, core, name):
    n = len(mine)

    def body(core_ref, *refs):
        del core_ref
        for a_ref, b_ref, o_ref in zip(refs[:n], refs[n:2 * n], refs[2 * n:]):
            o_ref[...] = (a_ref[...].astype(F32) + b_ref[...].astype(F32)).astype(BF16)

    return pl.pallas_call(
        body,
        grid_spec=pltpu.PrefetchScalarGridSpec(
            num_scalar_prefetch=1, grid=(4,),
            in_specs=[pl.BlockSpec((None, None) + a.shape[2:], lambda i, cr: (i, cr[0], 0, 0)) for a in mine]
            + [pl.BlockSpec((None,) + b.shape[1:], lambda i, cr: (i, 0, 0)) for b in recv],
            out_specs=[pl.BlockSpec((None,) + b.shape[1:], lambda i, cr: (i, 0, 0)) for b in recv]),
        out_shape=[_sds(b.shape, BF16) for b in recv], compiler_params=_cparams(1), name=name)(core, *mine, *recv)


def _adamw(w, g, m, v):
    m = ADAM_B1 * m + (1.0 - ADAM_B1) * g
    v = ADAM_B2 * v + (1.0 - ADAM_B2) * (g * g)
    m_hat = m / (1.0 - ADAM_B1 ** ADAM_STEP)
    v_hat = v / (1.0 - ADAM_B2 ** ADAM_STEP)
    delta = -ADAM_LR * (m_hat / (jnp.sqrt(v_hat) + ADAM_EPS) + ADAM_WD * w)
    return delta, m, v


def adamw_big(contrib, w, m, v, transposed, name, comm=None):
    nsrc, nl, rows, cols = contrib.shape
    ct = 256

    def body(c_ref, w_ref, m_ref, v_ref, g_out, d_out, m_out, v_out):
        g = c_ref[0].astype(F32)
        for src in range(1, nsrc):
            g = g + c_ref[src].astype(F32)
        if transposed:
            g = g.T
        delta, mn, vn = _adamw(w_ref[...], g, m_ref[...], v_ref[...])
        g_out[...] = g
        d_out[...] = delta
        m_out[...] = mn
        v_out[...] = vn

    if transposed:
        wspec = pl.BlockSpec((None, ct, rows), lambda l, j: (l, j, 0))
    else:
        wspec = pl.BlockSpec((None, rows, ct), lambda l, j: (l, 0, j))
    return _call(
        body, comm, (contrib, w, m, v), grid=(nl, cols // ct),
        in_specs=[pl.BlockSpec((nsrc, None, rows, ct), lambda l, j: (0, l, 0, j)), wspec, wspec, wspec],
        out_specs=[wspec] * 4, out_shape=[_sds(w.shape, F32)] * 4, name=name)


VEC_NAMES = ("conv_a_b", "lru_bx", "lru_ba", "lru_lambda", "conv_d_b", "ln_d_g", "ln_d_b")
P_N1, P_N2, P_VEC, P_CONV, P_LRU = 0, 1, 2, 6, 6 + CW_ROWS
P_LAYER = P_LRU + HD
P_FINAL, P_LOSS, P_ROWS = DEPTH * P_LAYER, DEPTH * P_LAYER + 1, 8 * ((DEPTH * P_LAYER + 2 + 7) // 8)
SMALL = ("norm1_g", "conv_a_w", "conv_a_b", "lru_wx", "lru_bx", "lru_wa", "lru_ba", "lru_lambda", "conv_b_w", "sinks",
         "conv_d_w", "conv_d_b", "ln_d_g", "ln_d_b", "norm2_g", "final_g")
VMEM_FULL = pl.BlockSpec(memory_space=pltpu.VMEM)


def _stack_vecs(p):
    rows = [p[n] for n in VEC_NAMES] + [jnp.pad(p["sinks"], ((0, 0), (0, BW - N_HEADS)))]
    return jnp.stack(rows, axis=1)


def _stack_convs(p):
    nl, _, ch = p["conv_a_w"].shape
    z = jnp.zeros((nl, 1, ch), F32)
    return jnp.concatenate([p["conv_a_w"], p["conv_b_w"], z, p["conv_d_w"], z], axis=1)


def _vec_place(l, r):
    return l * P_LAYER + P_VEC + r // 2, (r % 2) * BW


def pack_small(per_layer, head_stats):
    n = len(per_layer[0])

    def body(*refs):
        head_ref, pack = refs[DEPTH * n], refs[DEPTH * n + 1]
        pack[...] = jnp.zeros((P_ROWS, D), F32)
        lane = lax.broadcasted_iota(jnp.int32, (HD, BW), 1)
        for l in range(DEPTH):
            st_proj, st_ffn, dvec, st_attn, dcw, dwx, dwa = refs[l * n:(l + 1) * n]
            b = l * P_LAYER
            pack[b + P_N1:b + P_N1 + 1, :] = st_proj[0:1, :]
            pack[b + P_N2:b + P_N2 + 1, :] = st_ffn[0:1, :]
            for r in range(len(VEC_NAMES)):
                row, c0 = _vec_place(l, r)
                pack[row:row + 1, c0:c0 + BW] = dvec[r:r + 1, :]
            row, c0 = _vec_place(l, V_SINK)
            pack[row:row + 1, c0:c0 + 128] = st_attn[0:1, :]
            pack[b + P_CONV:b + P_CONV + CW_ROWS, 0:BW] = dcw[...]
            for mat, c0 in ((dwx, 0), (dwa, BW)):
                blocks = jnp.zeros((HD, BW), F32)
                for h in range(BW // HD):
                    blocks = jnp.where((lane >= HD * h) & (lane < HD * (h + 1)), mat[HD * h:HD * (h + 1), :], blocks)
                pack[b + P_LRU:b + P_LRU + HD, c0:c0 + BW] = blocks
        pack[P_FINAL:P_FINAL + 1, :] = head_ref[0:1, :]
        pack[P_LOSS:P_LOSS + 1, :] = head_ref[1:2, :]

    flat = [a for layer in per_layer for a in layer] + [head_stats]
    return pl.pallas_call(body, out_shape=_sds((P_ROWS, D), F32), in_specs=[VMEM_FULL] * len(flat), out_specs=VMEM_FULL,
                          name="pack_small", compiler_params=pltpu.CompilerParams(vmem_limit_bytes=VMEM_LIMIT))(*flat)


def adamw_small(gathered, me, w, m, v):
    ns = len(SMALL)

    def body(me_ref, c_ref, *refs):
        w_refs, m_refs, v_refs = refs[:ns], refs[ns:2 * ns], refs[2 * ns:3 * ns]
        loss_ref, outs, gs = refs[3 * ns], refs[3 * ns + 1:3 * ns + 1 + 4 * ns], refs[-1]
        gs[...] = c_ref[0]
        for dev in range(1, NDEV):
            gs[...] += c_ref[dev]
        loss_ref[...] = gs[P_LOSS:P_LOSS + 1, 0:128]

        def update(name, sel, g):
            i = SMALL.index(name)
            delta, mn, vn = _adamw(w_refs[i][sel], g, m_refs[i][sel], v_refs[i][sel])
            for o_ref, val in zip(outs[4 * i:4 * i + 4], (g, delta, mn, vn)):
                o_ref[sel] = val

        update("final_g", (slice(0, 1), slice(None)), gs[P_FINAL:P_FINAL + 1, :])
        shift = (BW - me_ref[0] * (BW // NDEV)) & (BW - 1)
        for l in range(DEPTH):
            b = l * P_LAYER
            row = (slice(l, l + 1), slice(None))
            update("norm1_g", row, gs[b + P_N1:b + P_N1 + 1, :])
            update("norm2_g", row, gs[b + P_N2:b + P_N2 + 1, :])
            for r, name in enumerate(VEC_NAMES):
                prow, c0 = _vec_place(l, r)
                update(name, row, gs[prow:prow + 1, c0:c0 + BW])
            prow, c0 = _vec_place(l, V_SINK)
            update("sinks", row, gs[prow:prow + 1, c0:c0 + N_HEADS])
            mine = pltpu.roll(gs[b + P_CONV:b + P_CONV + CW_ROWS, 0:BW], shift, 1)[:, 0:BW // NDEV]
            update("conv_a_w", (l,), mine[CW_A:CW_A + CONV_A])
            update("conv_b_w", (l,), mine[CW_B:CW_B + CONV_B])
            update("conv_d_w", (l,), mine[CW_D:CW_D + CONV_D])
            for h in range(BW // HD):
                update("lru_wx", (l, h), gs[b + P_LRU:b + P_LRU + HD, HD * h:HD * (h + 1)])
                update("lru_wa", (l, h), gs[b + P_LRU:b + P_LRU + HD, BW + HD * h:BW + HD * (h + 1)])

    args = [p[n] for p in (w, m, v) for n in SMALL]
    full = lambda a: pl.BlockSpec(a.shape, lambda i, me_ref: (0,) * a.ndim)
    out_shape = [_sds((1, 128), F32)] + [_sds(w[n].shape, F32) for n in SMALL for _ in range(4)]
    outs = pl.pallas_call(
        body,
        grid_spec=pltpu.PrefetchScalarGridSpec(
            num_scalar_prefetch=1, grid=(1,),
            in_specs=[full(gathered)] + [full(a) for a in args], out_specs=[full(o) for o in out_shape],
            scratch_shapes=[pltpu.VMEM((P_ROWS, D), F32)]),
        out_shape=out_shape, name="adamw_small", compiler_params=_cparams(1))(me, gathered, *args)
    return outs[0], {n: outs[1 + 4 * i:5 + 4 * i] for i, n in enumerate(SMALL)}


def merge_jobs(jobs):
    jobs = [j for j in jobs if j is not None]
    if not jobs:
        return None, []
    inputs, aliases, outs, sems, cuts = [], {}, [], [], []
    for j in jobs:
        i0, o0, s0 = len(inputs), len(outs), len(sems)
        aliases.update({i0 + i: o0 + o for i, o in j.aliases.items()})
        inputs += j.inputs
        outs += j.out_shapes
        sems += j.sem_shapes
        cuts.append((i0, len(inputs), o0, len(outs), s0, len(sems)))

    def each(which):
        def go(cins, couts, s):
            for j, (i0, i1, o0, o1, s0, s1) in zip(jobs, cuts):
                getattr(j, which)(cins[i0:i1], couts[o0:o1], s[s0:s1])
        return go

    return CommJob(inputs, aliases, outs, sems, each("start"), each("finish")), [(c[2], c[3]) for c in cuts]


SIXTHS = 6
OUT_KINDS = ("a_t", "b_t", "c_t", "d_t", "o")
GATHER_PLAN = {
    "fwd_proj": [(k, 0, 0, 6) for k in OUT_KINDS] + [("gate_t", 0, 0, 3)],
    "fwd_branch": [("gate_t", 0, 3, 6), ("up_t", 0, 0, 3)],
    "fwd_attn": [("up_t", 0, 3, 6), ("down", 0, 0, 6)],
    "fwd_merge": [("in_t", 1, 0, 2)],
    "fwd_ffn": [("in_t", 1, 2, 6)],
}
SIBLING_PLAN = {"bwd_merge": ("ffn", 0), "bwd_branch": ("out", 0), "bwd_ffn": ("in", 1),
                "bwd_proj_w1": ("in_a", 0), "bwd_proj_x": ("in_b", 0)}
GROUPS = dict(ffn=("gate_t", "up_t", "down"), out=OUT_KINDS, in_a=("in_a",), in_b=("in_b",))
GROUPS["in"] = ("in_t",)
COLUMN_HALF = dict(in_a=("in_t", W_IN_PARTS[0][0]), in_b=("in_t", W_IN_PARTS[1][0]))
CHIP_PLAN = {
    "bwd_attn": [("in_t", 1, 3, 6), ("gate_t", 0, 0, 3)],
    "bwd_branch": [("gate_t", 0, 3, 6), ("up_t", 0, 0, 6), ("down", 0, 0, 6)],
    "bwd_proj": [(k, 0, 0, 6) for k in OUT_KINDS],
    "bwd_proj_w0": [(k, 0, 0, 6) for k in OUT_KINDS[:3]],
    "bwd_proj_w1": [(k, 0, 0, 6) for k in OUT_KINDS[3:]],
    "bwd_merge": [("in_t", 1, 0, 3)],
    "bwd_proj_x": [("in_a", 0, 0, 6)],
    "adamw_gate_t": [("in_b", 0, 0, 6)],
}
SMALL_GATHER_SLOT = "adamw_down"


class Overlap:
    def __init__(self, shards, core):
        self.shards = shards
        self.core = core
        self.gathered = [dict.fromkeys(BIG) for _ in range(DEPTH)]
        self.views = {}
        self.partial = {}
        self.contrib = dict.fromkeys(BIG)
        self.small_pack = self.small_gathered = None
        self._open = None

    def weights(self, l):
        return self.gathered[l]

    def new_grads(self, group, l, grads):
        for k, g in grads.items():
            self.views[k, l] = g.reshape(4, 2, g.shape[0] // NDEV, g.shape[1])

    @staticmethod
    def _rows(shard_rows, f0, f1):
        return shard_rows * f0 // SIXTHS, shard_rows * (f1 - f0) // SIXTHS

    def job(self, slot, l):
        jobs, notes = [], []
        pieces = [(k, l + dl, f0, f1) for k, dl, f0, f1 in GATHER_PLAN.get(slot, []) if l + dl < DEPTH]
        if pieces:
            jobs.append(gather_job([((k, ll), self.shards[ll][k], self.gathered[ll][k],
                                     *self._rows(self.shards[ll][k].shape[0], f0, f1)) for k, ll, f0, f1 in pieces]))
            notes.append(("gather", list(dict.fromkeys((k, ll) for k, ll, _, _ in pieces))))
        if slot in SIBLING_PLAN and l + SIBLING_PLAN[slot][1] < DEPTH:
            group, dl = SIBLING_PLAN[slot]
            keys = [(k, l + dl) for k in GROUPS[group]]
            jobs.append(sibling_exchange_job([self.views[key] for key in keys]))
            notes.append(("sibling", keys))
        pieces = [(k, l + dl, f0, f1) for k, dl, f0, f1 in CHIP_PLAN.get(slot, []) if l + dl < DEPTH]
        if pieces:
            whole = [(*COLUMN_HALF.get(k, (k, 0)), k, ll, f0, f1) for k, ll, f0, f1 in pieces]
            jobs.append(chip_exchange_job([(self.partial[k, ll], self.contrib[kind], kind, ll,
                                            *self._rows(self.partial[k, ll].shape[1], f0, f1), col0, self.shards[ll][kind].shape[1])
                                           for kind, col0, k, ll, f0, f1 in whole]))
            notes.append(("chips", list(dict.fromkeys(kind for kind, *_ in whole))))
        if slot == SMALL_GATHER_SLOT:
            jobs.append(gather_job([("small", self.small_pack, None, 0, self.small_pack.shape[0])]))
            notes.append(("small", None))
        job, spans = merge_jobs(jobs)
        self._open = (slot, l, notes, spans)
        return job

    def done(self, slot, l, results):
        open_slot, open_l, notes, spans = self._open
        assert (open_slot, open_l) == (slot, l)
        for (what, keys), (r0, r1) in zip(notes, spans):
            res = results[r0:r1]
            if what == "gather":
                for (k, ll), g in zip(keys, res):
                    self.gathered[ll][k] = g
            elif what == "sibling":
                sums = add_partials([self.views[key] for key in keys], list(res), self.core, f"chip_sum_{keys[0][0]}{keys[0][1]}")
                self.partial.update(zip(keys, sums))
            elif what == "chips":
                for k, c in zip(keys, res):
                    self.contrib[k] = c
            else:
                self.small_gathered, = res


SMALL = ("norm1_g", "conv_a_w", "conv_a_b", "lru_wx", "lru_bx", "lru_wa", "lru_ba", "lru_lambda", "conv_b_w", "sinks",
         "conv_d_w", "conv_d_b", "ln_d_g", "ln_d_b", "norm2_g", "final_g")
WEIGHTS = ("norm1_g", "w_in", "conv_a_w", "conv_a_b", "lru_wx", "lru_bx", "lru_wa", "lru_ba", "lru_lambda", "w_a_out",
           "conv_b_w", "w_b_out", "sinks", "w_c_out", "conv_d_w", "conv_d_b", "ln_d_g", "ln_d_b", "w_d_out", "w_o",
           "norm2_g", "w_ffn_gate", "w_ffn_up", "w_ffn_down", "final_g")


def kernel(x, norm1_g, w_in, conv_a_w, conv_a_b, lru_wx, lru_bx, lru_wa, lru_ba, lru_lambda, w_a_out, conv_b_w, w_b_out, sinks, w_c_out, conv_d_w, conv_d_b, ln_d_g, ln_d_b, w_d_out, w_o, norm2_g, w_ffn_gate, w_ffn_up, w_ffn_down, final_g, loss_target, m_norm1_g, m_w_in, m_conv_a_w, m_conv_a_b, m_lru_wx, m_lru_bx, m_lru_wa, m_lru_ba, m_lru_lambda, m_w_a_out, m_conv_b_w, m_w_b_out, m_sinks, m_w_c_out, m_conv_d_w, m_conv_d_b, m_ln_d_g, m_ln_d_b, m_w_d_out, m_w_o, m_norm2_g, m_w_ffn_gate, m_w_ffn_up, m_w_ffn_down, m_final_g, v_norm1_g, v_w_in, v_conv_a_w, v_conv_a_b, v_lru_wx, v_lru_bx, v_lru_wa, v_lru_ba, v_lru_lambda, v_w_a_out, v_conv_b_w, v_w_b_out, v_sinks, v_w_c_out, v_conv_d_w, v_conv_d_b, v_ln_d_g, v_ln_d_b, v_w_d_out, v_w_o, v_norm2_g, v_w_ffn_gate, v_w_ffn_up, v_w_ffn_down, v_final_g):
    args = dict(locals())
    w = {n: args[n] for n in WEIGHTS}
    m = {n: args["m_" + n] for n in WEIGHTS}
    v = {n: args["v_" + n] for n in WEIGHTS}
    me = _dev_index(*_mesh_pos())

    def rows_major(a, how):
        return jnp.swapaxes(a, 1, 2) if how == "view" else a

    stacked = {k: cast_transpose(w[n], "prep_" + k) if how == "transpose" else rows_major(w[n], how).astype(BF16)
               for k, (n, how) in BIG.items()}
    plan = Overlap([{k: stacked[k][l] for k in BIG} for l in range(DEPTH)], lax.axis_index("c").astype(jnp.int32).reshape(1))
    convs = _stack_convs(w).reshape(DEPTH * CW_ROWS, BW // NDEV)
    g_in0, g_conv = _comm_only(gather_job([(("in_t", 0), plan.shards[0]["in_t"], None, 0, plan.shards[0]["in_t"].shape[0]),
                                           ("convs", convs, None, 0, convs.shape[0])]), "gather_first")
    plan.gathered[0]["in_t"] = g_in0
    convw = g_conv.reshape(NDEV, DEPTH, CW_ROWS, BW // NDEV).transpose(1, 2, 0, 3).reshape(DEPTH, CW_ROWS, BW)

    vecs = _stack_vecs(w)
    head_stats, grad_x, grads = local_step(x[0], loss_target[0], norm1_g, norm2_g, final_g, convw, vecs, lru_wx, lru_wa, plan)

    plan.small_pack = pack_small(grads, head_stats)

    out = {}
    for k in ("down", "gate_t", "up_t", "o", "a_t", "b_t", "c_t", "d_t", "in_t"):
        n, how = BIG[k]
        res, cres = adamw_big(plan.contrib[k], rows_major(w[n], how), rows_major(m[n], how), rows_major(v[n], how),
                              how == "transpose", "adamw_" + k, comm=plan.job("adamw_" + k, 0))
        plan.done("adamw_" + k, 0, cres)
        out[n] = [rows_major(r, how) for r in res]

    def own_shapes(p):
        return {n: p[n].reshape(1, D) if n == "final_g" else p[n] for n in SMALL}

    loss, small = adamw_small(plan.small_gathered.reshape(NDEV, P_ROWS, D), me.astype(jnp.int32).reshape(1),
                              own_shapes(w), own_shapes(m), own_shapes(v))
    for n in SMALL:
        out[n] = [r.reshape(w[n].shape) for r in small[n]]
    loss = loss[0, 0]
    return (loss, grad_x[None], *[out[n][0] for n in WEIGHTS], *[out[n][1] for n in WEIGHTS],
            *[out[n][2] for n in WEIGHTS], *[out[n][3] for n in WEIGHTS])
```

```python
import functools

import jax
import jax.numpy as jnp
from jax import lax
from jax.experimental import pallas as pl
from jax.experimental.pallas import tpu as pltpu

F32 = jnp.float32
BF16 = jnp.bfloat16
E = pl.Element

D = 1024
BW = 512
IN_W = 8448
GL0 = 4352
FF = 2816
N_HEADS = 8
N_KV = 2
HD = 64
ATT_BLK = 128
EPS = 1e-6
LRU_C = 8.0
NEG_INF = -1e30
DEPTH = 2
NDEV = 8
CONV_A, CONV_B, CONV_D = 4, 3, 31
C_AX, C_AG, C_BV, C_BC, C_BB, C_Q, C_K, C_V, C_D1, C_D2 = 0, 512, 1024, 1536, 2048, 2560, 3072, 3200, 3328, 3840
CW_A, CW_B, CW_D, CW_ROWS = 0, 4, 8, 40
V_CAB, V_BX, V_BA, V_LAM, V_CDB, V_LNG, V_LNB, V_SINK, V_ROWS = 0, 1, 2, 3, 4, 5, 6, 7, 8
HALO = 32
W_IN_PARTS = ((0, 768), (768, 256))

ADAM_LR, ADAM_B1, ADAM_B2, ADAM_EPS, ADAM_WD, ADAM_STEP = 0.001, 0.9, 0.999, 1e-08, 0.01, 10

VMEM_LIMIT = 56 * 1024 * 1024

_NN = (((1,), (0,)), ((), ()))
_NT = (((1,), (1,)), ((), ()))
_TN = (((0,), (0,)), ((), ()))


def _dot(a, b, dims):
    return lax.dot_general(a.astype(BF16), b.astype(BF16), dims, preferred_element_type=F32)


def _cparams(n_axes):
    return pltpu.CompilerParams(dimension_semantics=("arbitrary",) * n_axes, vmem_limit_bytes=VMEM_LIMIT)


def _sds(shape, dtype):
    return jax.ShapeDtypeStruct(tuple(shape), dtype)


def _sigmoid(x):
    return jax.nn.sigmoid(x)


def _neg_expm1(x):
    p = x * (1.0 + x * (0.5 + x * (1.0 / 6.0 + x * (1.0 / 24.0 + x * (1.0 / 120.0)))))
    return jnp.where(x > -0.1, -p, 1.0 - jnp.exp(x))


def _softplus(z):
    return jnp.maximum(z, 0.0) + jnp.log1p(jnp.exp(-jnp.abs(z)))


def _gelu_and_grad(x):
    c = 0.7978845608028654
    inner = c * (x + 0.044715 * x * x * x)
    t = jnp.tanh(inner)
    g = 0.5 * x * (1.0 + t)
    dg = 0.5 * (1.0 + t) + 0.5 * x * (1.0 - t * t) * c * (1.0 + 3.0 * 0.044715 * x * x)
    return g, dg


ANY = pl.BlockSpec(memory_space=pl.ANY)
MESH = pl.DeviceIdType.MESH


def _mesh_pos():
    return lax.axis_index("x"), lax.axis_index("y"), lax.axis_index("c")


def _dev_index(px, py, pc):
    return 4 * px + 2 * py + pc


class CommJob:
    def __init__(self, inputs, aliases, out_shapes, sem_shapes, start, finish):
        self.inputs, self.aliases, self.out_shapes, self.sem_shapes = list(inputs), dict(aliases), list(out_shapes), list(sem_shapes)
        self.start, self.finish = start, finish


def _call(body, comm, args, *, grid, in_specs, out_specs, out_shape, scratch_shapes=(), name, aliases=None):
    single = not isinstance(out_shape, (list, tuple))
    out_specs = [out_specs] if single else list(out_specs)
    out_shape = [out_shape] if single else list(out_shape)
    scratch_shapes = list(scratch_shapes)
    n_in, n_out, n_scr, n_axes = len(in_specs), len(out_shape), len(scratch_shapes), len(grid)
    params = pltpu.CompilerParams(dimension_semantics=("arbitrary",) * n_axes, vmem_limit_bytes=VMEM_LIMIT)
    io_aliases = dict(aliases or {})
    if comm is None:
        outs = pl.pallas_call(body, grid=grid, in_specs=in_specs, out_specs=out_specs, out_shape=out_shape,
                              scratch_shapes=scratch_shapes, input_output_aliases=io_aliases, compiler_params=params,
                              name=name)(*args)
        return (outs[0] if single else outs), []
    c_in, c_out = len(comm.inputs), len(comm.out_shapes)
    io_aliases.update({n_in + i: n_out + o for i, o in comm.aliases.items()})

    def wrapped(*refs):
        ins, cins = refs[:n_in], refs[n_in:n_in + c_in]
        outs = refs[n_in + c_in:n_in + c_in + n_out]
        couts = refs[n_in + c_in + n_out:n_in + c_in + n_out + c_out]
        rest = refs[n_in + c_in + n_out + c_out:]
        scr, sems = rest[:n_scr], rest[n_scr:]
        first = functools.reduce(lambda a, b: a & b, [pl.program_id(a) == 0 for a in range(n_axes)])
        last = functools.reduce(lambda a, b: a & b, [pl.program_id(a) == pl.num_programs(a) - 1 for a in range(n_axes)])

        @pl.when(first)
        def _():
            comm.start(cins, couts, sems)

        body(*ins, *outs, *scr)

        @pl.when(last)
        def _():
            comm.finish(cins, couts, sems)

    outs = pl.pallas_call(
        wrapped, grid=grid, in_specs=list(in_specs) + [ANY] * c_in, out_specs=out_specs + [ANY] * c_out,
        out_shape=out_shape + comm.out_shapes, scratch_shapes=scratch_shapes + comm.sem_shapes,
        input_output_aliases=io_aliases, compiler_params=params, name=name)(*args, *comm.inputs)
    res, cres = outs[:n_out], outs[n_out:]
    return (res[0] if single else res), cres


def _comm_only(comm, name):
    c_in, c_out = len(comm.inputs), len(comm.out_shapes)

    def body(*refs):
        cins, couts, sems = refs[:c_in], refs[c_in:c_in + c_out], refs[c_in + c_out:]
        comm.start(cins, couts, sems)
        comm.finish(cins, couts, sems)

    return pl.pallas_call(body, in_specs=[ANY] * c_in, out_specs=[ANY] * c_out, out_shape=comm.out_shapes,
                          scratch_shapes=comm.sem_shapes, input_output_aliases=comm.aliases, name=name)(*comm.inputs)


def gather_job(pieces):
    inputs, aliases, out_shapes, plan, where = [], {}, [], [], {}
    for key, shard, gathered, row0, nrows in pieces:
        if key not in where:
            where[key] = (len(inputs), len(out_shapes))
            inputs.append(shard)
            if gathered is not None:
                aliases[len(inputs)] = len(out_shapes)
                inputs.append(gathered)
            out_shapes.append(_sds((NDEV * shard.shape[0], shard.shape[1]), shard.dtype))
        plan.append((*where[key], shard.shape[0], row0, nrows))
    n = len(plan)

    def copies(cins, couts, sems):
        send_sems, recv_sems, local_sems = sems
        x, y, c = _mesh_pos()
        me, sibling = (x, y, c), (x, y, 1 - c)
        chips = [(1 - x, y), (x, 1 - y), (1 - x, 1 - y)]
        local, first, relay, recv_ici, recv_d2d = [], [], [], [], []
        for p, (i_shard, i_out, rows, row0, nrows) in enumerate(plan):
            src = cins[i_shard].at[pl.ds(row0, nrows), :]

            def slot(dev, i_out=i_out, rows=rows, row0=row0, nrows=nrows):
                return couts[i_out].at[pl.ds(_dev_index(*dev) * rows + row0, nrows), :]

            def copy(g, dev, to, src=None, p=p, slot=slot):
                return pltpu.make_async_remote_copy(
                    src_ref=slot(dev) if src is None else src, dst_ref=slot(dev),
                    send_sem=send_sems.at[g, p], recv_sem=recv_sems.at[g, p], device_id=to, device_id_type=MESH)

            local.append(pltpu.make_async_copy(src, slot(me), local_sems.at[p]))
            first.append(copy(0, me, sibling, src=src))
            recv_d2d.append(copy(0, sibling, me))
            for j, chip in enumerate(chips):
                first.append(copy(1 + j, me, (*chip, c), src=src))
                recv_ici.append(copy(1 + j, (*chip, c), me))
                relay.append(copy(4 + j, (*chip, c), sibling))
                recv_d2d.append(copy(4 + j, (*chip, 1 - c), me))
        return local, first, relay, recv_ici, recv_d2d

    def start(cins, couts, sems):
        local, first, _, _, _ = copies(cins, couts, sems)
        for cp in local + first:
            cp.start()

    def finish(cins, couts, sems):
        local, first, relay, recv_ici, recv_d2d = copies(cins, couts, sems)
        for cp in recv_ici:
            cp.wait_recv()
        for cp in relay:
            cp.start()
        for cp in recv_d2d:
            cp.wait_recv()
        for cp in first + relay:
            cp.wait_send()
        for cp in local:
            cp.wait()

    sem_shapes = [pltpu.SemaphoreType.DMA((7, n)), pltpu.SemaphoreType.DMA((7, n)), pltpu.SemaphoreType.DMA((n,))]
    return CommJob(inputs, aliases, out_shapes, sem_shapes, start, finish)


def sibling_exchange_job(grads):
    n = len(grads)

    def copies(cins, couts, sems):
        send_sems, recv_sems = sems
        x, y, c = _mesh_pos()
        return [pltpu.make_async_remote_copy(
            src_ref=cins[q].at[:, 1 - c], dst_ref=couts[q], send_sem=send_sems.at[q], recv_sem=recv_sems.at[q],
            device_id=(x, y, 1 - c), device_id_type=MESH) for q in range(n)]

    def start(cins, couts, sems):
        for cp in copies(cins, couts, sems):
            cp.start()

    def finish(cins, couts, sems):
        cps = copies(cins, couts, sems)
        for cp in cps:
            cp.wait_recv()
        for cp in cps:
            cp.wait_send()

    return CommJob(grads, {}, [_sds((4,) + g.shape[2:], g.dtype) for g in grads],
                   [pltpu.SemaphoreType.DMA((n,)), pltpu.SemaphoreType.DMA((n,))], start, finish)


def chip_exchange_job(pieces):
    inputs, aliases, out_shapes, plan, where = [], {}, [], [], {}
    for partial, contrib, key, layer, row0, nrows, col0, cols in pieces:
        if key not in where:
            where[key] = len(out_shapes)
            out_shapes.append(_sds((4, DEPTH, partial.shape[1], cols), partial.dtype))
            if contrib is not None:
                aliases[len(inputs)] = where[key]
                inputs.append(contrib)
        plan.append((len(inputs), where[key], layer, row0, nrows, col0, partial.shape[2]))
        inputs.append(partial)
    n = len(plan)

    def copies(cins, couts, sems):
        send_sems, recv_sems, local_sems = sems
        x, y, c = _mesh_pos()
        mine = 2 * x + y
        local, sends, recvs = [], [], []
        for p, (i_in, i_out, layer, row0, nrows, col0, ncols) in enumerate(plan):
            rows, lanes = pl.ds(row0, nrows), pl.ds(col0, ncols)
            local.append(pltpu.make_async_copy(cins[i_in].at[mine, rows, :], couts[i_out].at[mine, layer, rows, lanes],
                                               local_sems.at[p]))
            for j, (cx, cy) in enumerate([(1 - x, y), (x, 1 - y), (1 - x, 1 - y)]):
                theirs = 2 * cx + cy

                def copy(slot_there, j=j, p=p, cx=cx, cy=cy, theirs=theirs, i_in=i_in, i_out=i_out, layer=layer,
                         rows=rows, lanes=lanes):
                    return pltpu.make_async_remote_copy(
                        src_ref=cins[i_in].at[theirs, rows, :], dst_ref=couts[i_out].at[slot_there, layer, rows, lanes],
                        send_sem=send_sems.at[j, p], recv_sem=recv_sems.at[j, p], device_id=(cx, cy, c), device_id_type=MESH)
                sends.append(copy(mine))
                recvs.append(copy(theirs))
        return local, sends, recvs

    def start(cins, couts, sems):
        local, sends, _ = copies(cins, couts, sems)
        for cp in local + sends:
            cp.start()

    def finish(cins, couts, sems):
        local, sends, recvs = copies(cins, couts, sems)
        for cp in recvs:
            cp.wait_recv()
        for cp in sends:
            cp.wait_send()
        for cp in local:
            cp.wait()

    sem_shapes = [pltpu.SemaphoreType.DMA((3, n)), pltpu.SemaphoreType.DMA((3, n)), pltpu.SemaphoreType.DMA((n,))]
    return CommJob(inputs, aliases, out_shapes, sem_shapes, start, finish)


def fwd_proj(x, g1, wt_in, l, comm=None):
    s = x.shape[0]
    tm = min(512, s)
    tn = 1408

    def body(x_ref, g_ref, w_ref, o_ref, xn_ref):
        @pl.when(pl.program_id(1) == 0)
        def _():
            xv = x_ref[...]
            r = lax.rsqrt(jnp.mean(xv * xv, axis=-1, keepdims=True) + EPS)
            xn_ref[...] = (xv * r * g_ref[l:l + 1, :]).astype(BF16)

        o_ref[...] = _dot(xn_ref[...], w_ref[...], _NT).astype(BF16)

    return _call(
        body, comm, (x, g1, wt_in), grid=(s // tm, IN_W // tn),
        in_specs=[pl.BlockSpec((tm, D), lambda i, j: (i, 0)),
                  pl.BlockSpec((DEPTH, D), lambda i, j: (0, 0)),
                  pl.BlockSpec((tn, D), lambda i, j: (j, 0))],
        out_specs=pl.BlockSpec((tm, tn), lambda i, j: (i, j)),
        out_shape=_sds((s, IN_W), BF16),
        scratch_shapes=[pltpu.VMEM((tm, D), BF16)], name=f"fwd_proj{l}")


def _scan_fwd(a_ref, u_ref, h_ref, h0, n_rows):
    row = lax.broadcasted_iota(jnp.int32, (8, BW), 0)

    def body(g, hprev):
        r = pl.multiple_of(g * 8, 8)
        a = a_ref[pl.ds(r, 8), :]
        u = u_ref[pl.ds(r, 8), :]
        for sft in (1, 2, 4):
            a_sh = jnp.where(row >= sft, pltpu.roll(a, sft, 0), 1.0)
            u_sh = jnp.where(row >= sft, pltpu.roll(u, sft, 0), 0.0)
            u = u + a * u_sh
            a = a * a_sh
        h = u + a * hprev
        h_ref[pl.ds(r, 8), :] = h
        return h[7:8, :]

    return lax.fori_loop(0, n_rows // 8, body, h0)


def _scan_bwd(b_ref, g_ref, o_ref, c0, n_rows):
    row = lax.broadcasted_iota(jnp.int32, (8, BW), 0)

    def body(k, cnext):
        r = pl.multiple_of((n_rows // 8 - 1 - k) * 8, 8)
        b = b_ref[pl.ds(r, 8), :]
        g = g_ref[pl.ds(r, 8), :]
        for sft in (1, 2, 4):
            b_sh = jnp.where(row < 8 - sft, pltpu.roll(b, 8 - sft, 0), 1.0)
            g_sh = jnp.where(row < 8 - sft, pltpu.roll(g, 8 - sft, 0), 0.0)
            g = g + b * g_sh
            b = b * b_sh
        o = g + b * cnext
        o_ref[pl.ds(r, 8), :] = o
        return o[0:1, :]

    return lax.fori_loop(0, n_rows // 8, body, c0)


def _shifted_copies(buf, shifted, n_rows):
    for r in range(1, 8):
        shifted[r - 1, 0:n_rows - 8, :] = buf[pl.ds(r, n_rows - 8), :]


def _window(buf, shifted, off, t):
    r = off % 8
    return buf[pl.ds(off, t), :] if r == 0 else shifted[r - 1, pl.ds(off - r, t), :]


def _branch_fwd_math(cur_ref, halo_ref, cw_ref, vec_ref, wx_ref, wa_ref, bufa, bufb, bufd, xd, first, t):
    def halo(c0):
        v = halo_ref[:, c0:c0 + BW].astype(F32)
        return jnp.where(first, 0.0, v)

    def cur(c0):
        return cur_ref[:, c0:c0 + BW].astype(F32)

    out = {}
    bufa[0:HALO, :] = halo(C_AX)
    bufa[HALO:HALO + t, :] = cur(C_AX)
    ca = jnp.zeros((t, BW), F32) + vec_ref[V_CAB:V_CAB + 1, :]
    for k in range(CONV_A):
        ca = ca + cw_ref[CW_A + k:CW_A + k + 1, :] * bufa[pl.ds(HALO - (CONV_A - 1) + k, t), :]
    gi = _sigmoid(_dot(ca, wx_ref[...], _NN) + vec_ref[V_BX:V_BX + 1, :])
    gr = _sigmoid(_dot(ca, wa_ref[...], _NN) + vec_ref[V_BA:V_BA + 1, :])
    sp = _softplus(-vec_ref[V_LAM:V_LAM + 1, :])
    la = -LRU_C * sp * gr
    a = jnp.exp(la)
    mult = jnp.sqrt(_neg_expm1(2.0 * la))
    out.update(ca=ca, gi=gi, gr=gr, sp=sp, a=a, mult=mult)
    bufb[0:HALO, :] = halo(C_BC) * halo(C_BV)
    bufb[HALO:HALO + t, :] = cur(C_BC) * cur(C_BV)
    cb = jnp.zeros((t, BW), F32)
    for k in range(CONV_B):
        cb = cb + cw_ref[CW_B + k:CW_B + k + 1, :] * bufb[pl.ds(HALO - (CONV_B - 1) + k, t), :]
    out.update(cb=cb)
    bufd[0:HALO, :] = halo(C_D1) * _sigmoid(halo(C_D2))
    s2 = _sigmoid(cur(C_D2))
    bufd[HALO:HALO + t, :] = cur(C_D1) * s2
    _shifted_copies(bufd, xd, t + HALO)
    cd = jnp.zeros((t, BW), F32) + vec_ref[V_CDB:V_CDB + 1, :]
    for k in range(CONV_D):
        cd = cd + cw_ref[CW_D + k:CW_D + k + 1, :] * _window(bufd, xd, HALO - (CONV_D - 1) + k, t)
    mu = jnp.mean(cd, axis=-1, keepdims=True)
    xc = cd - mu
    rstd = lax.rsqrt(jnp.mean(xc * xc, axis=-1, keepdims=True) + EPS)
    xh = xc * rstd
    ln = xh * vec_ref[V_LNG:V_LNG + 1, :] + vec_ref[V_LNB:V_LNB + 1, :]
    out.update(s2=s2, xh=xh, rstd=rstd, ln=ln)
    return out


def fwd_branch(proj, convw, vecs, wx_bd, wa_bd, l, comm=None):
    s = proj.shape[0]
    t = min(256, s)

    def body(cur_ref, halo_ref, cw_ref, vec_ref, wx_ref, wa_ref, pre_ref, h_ref, bufa, bufb, bufd, xd, a_s, u_s, hcar):
        first = pl.program_id(0) == 0

        @pl.when(first)
        def _():
            hcar[...] = jnp.zeros((1, BW), F32)

        v = _branch_fwd_math(cur_ref, halo_ref, cw_ref, vec_ref, wx_ref, wa_ref, bufa, bufb, bufd, xd, first, t)
        a_s[...] = v["a"]
        u_s[...] = v["ca"] * v["gi"] * v["mult"]
        hcar[...] = _scan_fwd(a_s, u_s, h_ref, hcar[...], t)
        gg, _ = _gelu_and_grad(cur_ref[:, C_AG:C_AG + BW].astype(F32))
        pre_ref[:, 0:BW] = (h_ref[...] * gg).astype(BF16)
        pre_ref[:, BW:2 * BW] = (cur_ref[:, C_BB:C_BB + BW].astype(F32) * v["cb"]).astype(BF16)
        ln = v["ln"]
        pre_ref[:, 2 * BW:3 * BW] = (ln * _sigmoid(ln)).astype(BF16)

    hb = t // HALO
    return _call(
        body, comm, (proj, proj, convw, vecs, wx_bd, wa_bd), grid=(s // t,),
        in_specs=[pl.BlockSpec((t, GL0), lambda i: (i, 0)),
                  pl.BlockSpec((HALO, GL0), lambda i: (jnp.maximum(i * hb - 1, 0), 0)),
                  pl.BlockSpec((None, CW_ROWS, BW), lambda i: (l, 0, 0)),
                  pl.BlockSpec((None, V_ROWS, BW), lambda i: (l, 0, 0)),
                  pl.BlockSpec((None, BW, BW), lambda i: (l, 0, 0)),
                  pl.BlockSpec((None, BW, BW), lambda i: (l, 0, 0))],
        out_specs=[pl.BlockSpec((t, 3 * BW), lambda i: (i, 0)), pl.BlockSpec((t, BW), lambda i: (i, 0))],
        out_shape=[_sds((s, 3 * BW), BF16), _sds((s, BW), F32)],
        scratch_shapes=[pltpu.VMEM((t + HALO, BW), F32)] * 3 + [pltpu.VMEM((7, t + HALO - 8, BW), F32)]
        + [pltpu.VMEM((t, BW), F32)] * 2 + [pltpu.VMEM((1, BW), F32)],
        name=f"fwd_branch{l}")


GRP = N_HEADS // N_KV


def _attn_mask_bias(first_block):
    shape = (GRP * ATT_BLK, 2 * ATT_BLK)
    qi = lax.broadcasted_iota(jnp.int32, shape, 0) & (ATT_BLK - 1)
    ki = lax.broadcasted_iota(jnp.int32, shape, 1)
    dist = qi + ATT_BLK - ki
    valid = (dist >= 0) & (dist < ATT_BLK) & (jnp.logical_not(first_block) | (ki >= ATT_BLK))
    return dist.astype(F32), valid


def _per_head(hk, values):
    hl = lax.broadcasted_iota(jnp.int32, (GRP * ATT_BLK, 1), 0) // ATT_BLK
    out = values[GRP - 1]
    for j in range(GRP - 2, -1, -1):
        out = jnp.where(hl == j, values[j], out)
    return out


def _attn_probs(q_ref, kvp_ref, kvc_ref, vec_ref, distf, valid):
    kvs = range(N_KV)
    heads = [range(hk * GRP, (hk + 1) * GRP) for hk in kvs]
    q4 = [jnp.concatenate([q_ref[:, h * HD:(h + 1) * HD] for h in heads[hk]], axis=0) for hk in kvs]
    k2 = [jnp.concatenate([kvp_ref[:, hk * HD:(hk + 1) * HD], kvc_ref[:, hk * HD:(hk + 1) * HD]], axis=0) for hk in kvs]
    v2 = [jnp.concatenate([kvp_ref[:, (N_KV + hk) * HD:(N_KV + hk + 1) * HD],
                           kvc_ref[:, (N_KV + hk) * HD:(N_KV + hk + 1) * HD]], axis=0) for hk in kvs]
    slope = [_per_head(hk, [2.0 ** (-8.0 * (h + 1) / N_HEADS) for h in heads[hk]]) for hk in kvs]
    sink = [_per_head(hk, [vec_ref[V_SINK:V_SINK + 1, h:h + 1] for h in heads[hk]]) for hk in kvs]
    sc = [_dot(q4[hk], k2[hk], _NT) for hk in kvs]
    sc = [jnp.where(valid, sc[hk] * (HD ** -0.5) - slope[hk] * distf, NEG_INF) for hk in kvs]
    m = [jnp.maximum(jnp.max(sc[hk], axis=-1, keepdims=True), sink[hk]) for hk in kvs]
    p = [jnp.exp(sc[hk] - m[hk]) for hk in kvs]
    es = [jnp.exp(sink[hk] - m[hk]) for hk in kvs]
    inv = [1.0 / (jnp.sum(p[hk], axis=-1, keepdims=True) + es[hk]) for hk in kvs]
    return [(q4[hk], k2[hk], v2[hk], p[hk] * inv[hk], es[hk] * inv[hk]) for hk in kvs]


def fwd_attn(proj, vecs, l, comm=None):
    s = proj.shape[0]
    nb = s // ATT_BLK

    def body(q_ref, kvp_ref, kvc_ref, vec_ref, o_ref):
        distf, valid = _attn_mask_bias(pl.program_id(0) == 0)
        groups = _attn_probs(q_ref, kvp_ref, kvc_ref, vec_ref, distf, valid)
        outs = [_dot(p, v2, _NN).astype(BF16) for _, _, v2, p, _ in groups]
        for hk, out in enumerate(outs):
            for j in range(GRP):
                h = hk * GRP + j
                o_ref[:, h * HD:(h + 1) * HD] = out[j * ATT_BLK:(j + 1) * ATT_BLK]

    return _call(
        body, comm, (proj, proj, proj, vecs), grid=(nb,),
        in_specs=[pl.BlockSpec((ATT_BLK, BW), lambda i: (i, C_Q // BW)),
                  pl.BlockSpec((ATT_BLK, 256), lambda i: (jnp.maximum(i - 1, 0), C_K // 256)),
                  pl.BlockSpec((ATT_BLK, 256), lambda i: (i, C_K // 256)),
                  pl.BlockSpec((None, V_ROWS, BW), lambda i: (l, 0, 0))],
        out_specs=pl.BlockSpec((ATT_BLK, BW), lambda i: (i, 0)),
        out_shape=_sds((s, BW), BF16), name=f"fwd_attn{l}")


def fwd_merge(x, proj, pre_abd, pre_c, wt_a, wt_b, wt_c, wt_d, w_o, l, comm=None):
    s = x.shape[0]
    tm = min(256, s)

    def body(x_ref, gl_ref, pabd_ref, pc_ref, wa_ref, wb_ref, wc_ref, wd_ref, wo_ref, y_ref, mg_ref, h1_ref):
        pres = (pabd_ref[:, 0:BW], pabd_ref[:, BW:2 * BW], pc_ref[...], pabd_ref[:, 2 * BW:3 * BW])
        merged = jnp.zeros((tm, D), F32)
        for k, (pre, w_ref) in enumerate(zip(pres, (wa_ref, wb_ref, wc_ref, wd_ref))):
            yk = _dot(pre, w_ref[...], _NT)
            y_ref[:, k * D:(k + 1) * D] = yk.astype(BF16)
            merged = merged + _sigmoid(gl_ref[:, k * D:(k + 1) * D].astype(F32)) * yk
        mg_ref[...] = merged.astype(BF16)
        h1_ref[...] = x_ref[...] + _dot(merged, wo_ref[...], _NN)

    wspec = pl.BlockSpec((D, BW), lambda i: (0, 0))
    return _call(
        body, comm, (x, proj, pre_abd, pre_c, wt_a, wt_b, wt_c, wt_d, w_o), grid=(s // tm,),
        in_specs=[pl.BlockSpec((tm, D), lambda i: (i, 0)),
                  pl.BlockSpec((E(tm), E(4 * D)), lambda i: (i * tm, GL0)),
                  pl.BlockSpec((tm, 3 * BW), lambda i: (i, 0)),
                  pl.BlockSpec((tm, BW), lambda i: (i, 0)),
                  wspec, wspec, wspec, wspec,
                  pl.BlockSpec((D, D), lambda i: (0, 0))],
        out_specs=[pl.BlockSpec((tm, 4 * D), lambda i: (i, 0)), pl.BlockSpec((tm, D), lambda i: (i, 0)),
                   pl.BlockSpec((tm, D), lambda i: (i, 0))],
        out_shape=[_sds((s, 4 * D), BF16), _sds((s, D), BF16), _sds((s, D), F32)], name=f"fwd_merge{l}")


def fwd_ffn(h1, g2, wt_gate, wt_up, w_down, l, comm=None):
    s = h1.shape[0]
    tm = min(512, s)
    fc = FF // 2

    def body(h_ref, g_ref, wg_ref, wu_ref, wd_ref, xo_ref, fg_ref, fu_ref, hn_ref, acc_ref):
        j = pl.program_id(1)

        @pl.when(j == 0)
        def _():
            hv = h_ref[...]
            r = lax.rsqrt(jnp.mean(hv * hv, axis=-1, keepdims=True) + EPS)
            hn_ref[...] = (hv * r * g_ref[l:l + 1, :]).astype(BF16)
            acc_ref[...] = hv

        fg = _dot(hn_ref[...], wg_ref[...], _NT)
        fu = _dot(hn_ref[...], wu_ref[...], _NT)
        fg_ref[...] = fg.astype(BF16)
        fu_ref[...] = fu.astype(BF16)
        acc_ref[...] += _dot(fg * _sigmoid(fg) * fu, wd_ref[...], _NN)

        @pl.when(j == pl.num_programs(1) - 1)
        def _():
            xo_ref[...] = acc_ref[...]

    wspec = pl.BlockSpec((fc, D), lambda i, j: (j, 0))
    return _call(
        body, comm, (h1, g2, wt_gate, wt_up, w_down), grid=(s // tm, FF // fc),
        in_specs=[pl.BlockSpec((tm, D), lambda i, j: (i, 0)), pl.BlockSpec((DEPTH, D), lambda i, j: (0, 0)),
                  wspec, wspec, wspec],
        out_specs=[pl.BlockSpec((tm, D), lambda i, j: (i, 0)), pl.BlockSpec((tm, fc), lambda i, j: (i, j)),
                   pl.BlockSpec((tm, fc), lambda i, j: (i, j))],
        out_shape=[_sds((s, D), F32), _sds((s, FF), BF16), _sds((s, FF), BF16)],
        scratch_shapes=[pltpu.VMEM((tm, D), BF16), pltpu.VMEM((tm, D), F32)], name=f"fwd_ffn{l}")


def loss_head(x, gf, target):
    s = x.shape[0]
    tm = min(512, s)

    def body(x_ref, g_ref, t_ref, dx_ref, st_ref):
        @pl.when(pl.program_id(0) == 0)
        def _():
            st_ref[...] = jnp.zeros((8, D), F32)

        xv = x_ref[...]
        g = g_ref[...]
        r = lax.rsqrt(jnp.mean(xv * xv, axis=-1, keepdims=True) + EPS)
        n = xv * r
        err = n * g - t_ref[...]
        dy = err * (1.0 / D)
        dn = dy * g
        dx_ref[...] = r * (dn - n * jnp.mean(dn * n, axis=-1, keepdims=True))
        st_ref[0:1, :] += jnp.sum(dy * n, axis=0, keepdims=True)
        lsum = 0.5 * jnp.sum(jnp.mean(err * err, axis=-1, keepdims=True), axis=0, keepdims=True)
        st_ref[1:2, :] += jnp.broadcast_to(lsum, (1, D))

    return pl.pallas_call(
        body, grid=(s // tm,),
        in_specs=[pl.BlockSpec((tm, D), lambda i: (i, 0)), pl.BlockSpec((1, D), lambda i: (0, 0)),
                  pl.BlockSpec((tm, D), lambda i: (i, 0))],
        out_specs=[pl.BlockSpec((tm, D), lambda i: (i, 0)), pl.BlockSpec((8, D), lambda i: (0, 0))],
        out_shape=[_sds((s, D), F32), _sds((8, D), F32)],
        compiler_params=_cparams(1), name="loss_head")(x, gf, target)


def _edge_index(j, i, n_j, n_i):
    return jnp.where((j == 0) | (j == n_j - 1), i, n_i - 1)


def bwd_ffn(dxo, h1, fg, fu, g2, wt_gate, wt_up, w_down, l, comm=None):
    s = h1.shape[0]
    tm = min(512, s)
    fc = 256
    n_j, n_i = FF // fc, s // tm

    def body(dxo_ref, h_ref, fg_ref, fu_ref, g_ref, wg_ref, wu_ref, wd_ref,
             dh_ref, dwg_ref, dwu_ref, dwd_ref, st_ref, dhn, dxo_b, hn_b, ag, au, ad):
        j, i = pl.program_id(0), pl.program_id(1)
        rows = pl.ds(pl.multiple_of(i * tm, tm), tm)
        g = g_ref[l:l + 1, :]

        @pl.when(j == 0)
        def _():
            hv = h_ref[...]
            r = lax.rsqrt(jnp.mean(hv * hv, axis=-1, keepdims=True) + EPS)
            hn_b[rows, :] = (hv * r * g).astype(BF16)
            dxo_b[rows, :] = dxo_ref[...].astype(BF16)
            dhn[rows, :] = jnp.zeros((tm, D), F32)

        @pl.when((j == 0) & (i == 0))
        def _():
            st_ref[...] = jnp.zeros((8, D), F32)

        @pl.when(i == 0)
        def _():
            ag[...] = jnp.zeros((fc, D), F32)
            au[...] = jnp.zeros((fc, D), F32)
            ad[...] = jnp.zeros((fc, D), F32)

        fgv = fg_ref[...].astype(F32)
        fuv = fu_ref[...].astype(F32)
        sg = _sigmoid(fgv)
        sil = fgv * sg
        dxb = dxo_b[rows, :]
        hnb = hn_b[rows, :]
        d_act = _dot(dxb, wd_ref[...], _NT)
        ad[...] += _dot(sil * fuv, dxb, _TN)
        d_fg = (d_act * fuv * (sg * (1.0 + fgv * (1.0 - sg)))).astype(BF16)
        d_fu = (d_act * sil).astype(BF16)
        ag[...] += _dot(d_fg, hnb, _TN)
        au[...] += _dot(d_fu, hnb, _TN)
        dhn[rows, :] += _dot(d_fg, wg_ref[...], _NN) + _dot(d_fu, wu_ref[...], _NN)

        @pl.when(i == n_i - 1)
        def _():
            dwg_ref[...] = ag[...].astype(BF16)
            dwu_ref[...] = au[...].astype(BF16)
            dwd_ref[...] = ad[...].astype(BF16)

        @pl.when(j == n_j - 1)
        def _():
            hv = h_ref[...]
            r = lax.rsqrt(jnp.mean(hv * hv, axis=-1, keepdims=True) + EPS)
            n = hv * r
            dv = dhn[rows, :]
            dn = dv * g
            dh_ref[...] = dxo_ref[...] + r * (dn - n * jnp.mean(dn * n, axis=-1, keepdims=True))
            st_ref[0:1, :] += jnp.sum(dv * n, axis=0, keepdims=True)

    edge = lambda j, i: (_edge_index(j, i, n_j, n_i), 0)
    wspec = pl.BlockSpec((fc, D), lambda j, i: (j, 0))
    dwspec = pl.BlockSpec((fc, D), lambda j, i: (j, 0))
    return _call(
        body, comm, (dxo, h1, fg, fu, g2, wt_gate, wt_up, w_down), grid=(n_j, n_i),
        in_specs=[pl.BlockSpec((tm, D), edge),
                  pl.BlockSpec((tm, D), edge),
                  pl.BlockSpec((tm, fc), lambda j, i: (i, j)), pl.BlockSpec((tm, fc), lambda j, i: (i, j)),
                  pl.BlockSpec((DEPTH, D), lambda j, i: (0, 0)), wspec, wspec, wspec],
        out_specs=[pl.BlockSpec((tm, D), lambda j, i: (jnp.where(j == n_j - 1, i, 0), 0)),
                   dwspec, dwspec, dwspec, pl.BlockSpec((8, D), lambda j, i: (0, 0))],
        out_shape=[_sds((s, D), F32), _sds((FF, D), BF16), _sds((FF, D), BF16), _sds((FF, D), BF16), _sds((8, D), F32)],
        scratch_shapes=[pltpu.VMEM((s, D), F32), pltpu.VMEM((s, D), BF16), pltpu.VMEM((s, D), BF16),
                        pltpu.VMEM((fc, D), F32), pltpu.VMEM((fc, D), F32), pltpu.VMEM((fc, D), F32)],
        name=f"bwd_ffn{l}")


def bwd_merge(dh1, y4, proj, merged, pre_abd, pre_c, wt_a, wt_b, wt_c, wt_d, w_o, l, comm=None):
    s = dh1.shape[0]
    tm = min(256, s)
    n_i = s // tm

    def body(dh_ref, y_ref, gl_ref, mg_ref, pabd_ref, pc_ref, wa_ref, wb_ref, wc_ref, wd_ref, wo_ref,
             dgl_ref, dpre_ref, dwo_ref, dwa_ref, dwb_ref, dwc_ref, dwd_ref, ao, aa, ab, ac, ad):
        i = pl.program_id(0)
        accs = (aa, ab, ac, ad)

        @pl.when(i == 0)
        def _():
            ao[...] = jnp.zeros((D, D), F32)
            for acc in accs:
                acc[...] = jnp.zeros((D, BW), F32)

        dhb = dh_ref[...].astype(BF16)
        dmg = _dot(dhb, wo_ref[...], _NT)
        ao[...] += _dot(mg_ref[...], dhb, _TN)
        pres = (pabd_ref[:, 0:BW], pabd_ref[:, BW:2 * BW], pc_ref[...], pabd_ref[:, 2 * BW:3 * BW])
        for k, (pre, w_ref, acc) in enumerate(zip(pres, (wa_ref, wb_ref, wc_ref, wd_ref), accs)):
            gk = _sigmoid(gl_ref[:, k * D:(k + 1) * D].astype(F32))
            yk = y_ref[:, k * D:(k + 1) * D].astype(F32)
            dgl_ref[:, k * D:(k + 1) * D] = (dmg * yk * gk * (1.0 - gk)).astype(BF16)
            dyk = (dmg * gk).astype(BF16)
            dpre_ref[:, k * BW:(k + 1) * BW] = _dot(dyk, w_ref[...], _NN).astype(BF16)
            acc[...] += _dot(dyk, pre, _TN)

        @pl.when(i == n_i - 1)
        def _():
            dwo_ref[...] = ao[...].astype(BF16)
            for o_ref, acc in zip((dwa_ref, dwb_ref, dwc_ref, dwd_ref), accs):
                o_ref[...] = acc[...].astype(BF16)

    wspec = pl.BlockSpec((D, BW), lambda i: (0, 0))
    dwspec = pl.BlockSpec((D, BW), lambda i: (0, 0))
    return _call(
        body, comm, (dh1, y4, proj, merged, pre_abd, pre_c, wt_a, wt_b, wt_c, wt_d, w_o), grid=(n_i,),
        in_specs=[pl.BlockSpec((tm, D), lambda i: (i, 0)),
                  pl.BlockSpec((tm, 4 * D), lambda i: (i, 0)),
                  pl.BlockSpec((E(tm), E(4 * D)), lambda i: (i * tm, GL0)),
                  pl.BlockSpec((tm, D), lambda i: (i, 0)),
                  pl.BlockSpec((tm, 3 * BW), lambda i: (i, 0)),
                  pl.BlockSpec((tm, BW), lambda i: (i, 0)),
                  wspec, wspec, wspec, wspec,
                  pl.BlockSpec((D, D), lambda i: (0, 0))],
        out_specs=[pl.BlockSpec((E(tm), E(4 * D)), lambda i: (i * tm, GL0)),
                   pl.BlockSpec((tm, 4 * BW), lambda i: (i, 0)),
                   pl.BlockSpec((D, D), lambda i: (0, 0)), dwspec, dwspec, dwspec, dwspec],
        out_shape=[_sds((s, IN_W), BF16), _sds((s, 4 * BW), BF16), _sds((D, D), BF16)] + [_sds((D, BW), BF16)] * 4,
        scratch_shapes=[pltpu.VMEM((D, D), F32)] + [pltpu.VMEM((D, BW), F32)] * 4, name=f"bwd_merge{l}")


def bwd_attn(proj, dpre, vecs, l, comm=None):
    s = proj.shape[0]
    nb = s // ATT_BLK
    grp = N_HEADS // N_KV

    def body(q_ref, kvp_ref, kvc_ref, do_ref, vec_ref, dq_ref, dkc_ref, dkp_ref, st_ref):
        @pl.when(pl.program_id(0) == 0)
        def _():
            st_ref[...] = jnp.zeros((8, 128), F32)

        distf, valid = _attn_mask_bias(pl.program_id(0) == 0)
        lane = lax.broadcasted_iota(jnp.int32, (1, 128), 1)
        dsink = jnp.zeros((1, 128), F32)
        groups = _attn_probs(q_ref, kvp_ref, kvc_ref, vec_ref, distf, valid)
        kvs = range(N_KV)
        do4s = [jnp.concatenate([do_ref[:, h * HD:(h + 1) * HD] for h in range(hk * grp, (hk + 1) * grp)], axis=0) for hk in kvs]
        dps = [_dot(do4s[hk], groups[hk][2], _NT) for hk in kvs]
        deltas = [jnp.sum(groups[hk][3] * dps[hk], axis=-1, keepdims=True) for hk in kvs]
        dss = [groups[hk][3] * (dps[hk] - deltas[hk]) * (HD ** -0.5) for hk in kvs]
        for hk in kvs:
            q4, k2, v2, p, ps = groups[hk]
            do4, delta, ds = do4s[hk], deltas[hk], dss[hk]
            dq4 = _dot(ds, k2, _NN).astype(BF16)
            dk2 = _dot(ds, q4, _TN)
            dv2 = _dot(p, do4, _TN)
            psd = ps * delta
            for j in range(grp):
                h = hk * grp + j
                rows = slice(j * ATT_BLK, (j + 1) * ATT_BLK)
                dq_ref[:, h * HD:(h + 1) * HD] = dq4[rows]
                dsink = dsink + jnp.where(lane == h, -jnp.sum(psd[rows], axis=0, keepdims=True), 0.0)
            dkp_ref[:, hk * HD:(hk + 1) * HD] = dk2[0:ATT_BLK].astype(BF16)
            dkc_ref[:, hk * HD:(hk + 1) * HD] = dk2[ATT_BLK:].astype(BF16)
            dkp_ref[:, (N_KV + hk) * HD:(N_KV + hk + 1) * HD] = dv2[0:ATT_BLK].astype(BF16)
            dkc_ref[:, (N_KV + hk) * HD:(N_KV + hk + 1) * HD] = dv2[ATT_BLK:].astype(BF16)
        st_ref[0:1, :] += dsink

    return _call(
        body, comm, (proj, proj, proj, dpre, vecs), grid=(nb,),
        in_specs=[pl.BlockSpec((ATT_BLK, BW), lambda i: (i, C_Q // BW)),
                  pl.BlockSpec((ATT_BLK, 256), lambda i: (jnp.maximum(i - 1, 0), C_K // 256)),
                  pl.BlockSpec((ATT_BLK, 256), lambda i: (i, C_K // 256)),
                  pl.BlockSpec((ATT_BLK, BW), lambda i: (i, 2)),
                  pl.BlockSpec((None, V_ROWS, BW), lambda i: (l, 0, 0))],
        out_specs=[pl.BlockSpec((ATT_BLK, BW), lambda i: (i, 0)), pl.BlockSpec((ATT_BLK, 256), lambda i: (i, 0)),
                   pl.BlockSpec((ATT_BLK, 256), lambda i: (i, 0)), pl.BlockSpec((8, 128), lambda i: (0, 0))],
        out_shape=[_sds((s, BW), BF16), _sds((s, 256), BF16), _sds((s, 256), BF16), _sds((8, 128), F32)],
        name=f"bwd_attn{l}")


def bwd_branch(proj, dproj, dpre, h, dq, dkc, dkp, convw, vecs, wx_bd, wa_bd, l, comm=None):
    s = proj.shape[0]
    t = 2 * ATT_BLK
    nt = s // t
    nb = s // ATT_BLK
    hb = t // HALO

    def body(cur_ref, halo_ref, dpre_ref, h_ref, hp_ref, dq_ref, dkc_ref, dkp1_ref, dkp2_ref,
             cw_ref, vec_ref, wx_ref, wa_ref, dproj_in, dp_ref, dcw_ref, dvec_ref, dwx_ref, dwa_ref,
             bufa, bufb, bufd, xd, xg, a_ext, hbuf, b_s, g_s, dh_s, ga, gb, gd, dhcar):
        del dproj_in
        step = pl.program_id(0)
        ti = nt - 1 - step
        first = ti == 0

        @pl.when(step == 0)
        def _():
            dcw_ref[...] = jnp.zeros((CW_ROWS, BW), F32)
            dvec_ref[...] = jnp.zeros((V_ROWS, BW), F32)
            dwx_ref[...] = jnp.zeros((BW, BW), F32)
            dwa_ref[...] = jnp.zeros((BW, BW), F32)
            dhcar[...] = jnp.zeros((1, BW), F32)
            a_ext[t:t + 8, :] = jnp.zeros((8, BW), F32)
            ga[t:t + 8, :] = jnp.zeros((8, BW), F32)
            gb[t:t + 8, :] = jnp.zeros((8, BW), F32)
            gd[t:t + HALO, :] = jnp.zeros((HALO, BW), F32)

        def cur(c0):
            return cur_ref[:, c0:c0 + BW].astype(F32)

        def rsum(v):
            return jnp.sum(v, axis=0, keepdims=True)

        def put(c0, v):
            dp_ref[:, c0:c0 + BW] = v.astype(BF16)

        v = _branch_fwd_math(cur_ref, halo_ref, cw_ref, vec_ref, wx_ref, wa_ref, bufa, bufb, bufd, xd, first, t)
        ca, gi, gr, sp, a, mult = v["ca"], v["gi"], v["gr"], v["sp"], v["a"], v["mult"]
        dpa = dpre_ref[:, 0:BW].astype(F32)
        gg, dgg = _gelu_and_grad(cur(C_AG))
        hv = h_ref[...]
        put(C_AG, dpa * hv * dgg)
        a_ext[0:t, :] = a
        b_s[...] = a_ext[pl.ds(1, t), :]
        g_s[...] = dpa * gg
        dhcar[...] = _scan_bwd(b_s, g_s, dh_s, dhcar[...], t)
        a_ext[t:t + 1, :] = a[0:1, :]
        dh = dh_s[...]
        hbuf[0:8, :] = jnp.where(first, 0.0, hp_ref[...])
        hbuf[8:8 + t, :] = hv
        da = dh * hbuf[pl.ds(7, t), :]
        d_ca = dh * gi * mult
        d_gi = dh * ca * mult
        d_mult = dh * ca * gi
        d_la = da * a - d_mult * (a * a) / mult
        lam = vec_ref[V_LAM:V_LAM + 1, :]
        dvec_ref[V_LAM:V_LAM + 1, :] += rsum(d_la * gr) * (LRU_C * _sigmoid(-lam))
        d_gr = d_la * (-LRU_C * sp)
        d_zr = d_gr * gr * (1.0 - gr)
        d_zi = d_gi * gi * (1.0 - gi)
        dvec_ref[V_BA:V_BA + 1, :] += rsum(d_zr)
        dvec_ref[V_BX:V_BX + 1, :] += rsum(d_zi)
        dwa_ref[...] += _dot(ca, d_zr, _TN)
        dwx_ref[...] += _dot(ca, d_zi, _TN)
        d_ca = d_ca + _dot(d_zi, wx_ref[...], _NT) + _dot(d_zr, wa_ref[...], _NT)
        dvec_ref[V_CAB:V_CAB + 1, :] += rsum(d_ca)
        ga[0:t, :] = d_ca
        d_ax = jnp.zeros((t, BW), F32)
        for k in range(CONV_A):
            d_ax = d_ax + cw_ref[CW_A + k:CW_A + k + 1, :] * ga[pl.ds(CONV_A - 1 - k, t), :]
            dcw_ref[CW_A + k:CW_A + k + 1, :] += rsum(d_ca * bufa[pl.ds(HALO - (CONV_A - 1) + k, t), :])
        ga[t:t + 8, :] = d_ca[0:8, :]
        put(C_AX, d_ax)
        dpb = dpre_ref[:, BW:2 * BW].astype(F32)
        put(C_BB, dpb * v["cb"])
        d_cb = dpb * cur(C_BB)
        gb[0:t, :] = d_cb
        d_cbin = jnp.zeros((t, BW), F32)
        for k in range(CONV_B):
            d_cbin = d_cbin + cw_ref[CW_B + k:CW_B + k + 1, :] * gb[pl.ds(CONV_B - 1 - k, t), :]
            dcw_ref[CW_B + k:CW_B + k + 1, :] += rsum(d_cb * bufb[pl.ds(HALO - (CONV_B - 1) + k, t), :])
        gb[t:t + 8, :] = d_cb[0:8, :]
        put(C_BC, d_cbin * cur(C_BV))
        put(C_BV, d_cbin * cur(C_BC))
        dpd = dpre_ref[:, 3 * BW:4 * BW].astype(F32)
        ln, xh, rstd, s2 = v["ln"], v["xh"], v["rstd"], v["s2"]
        sg = _sigmoid(ln)
        d_ln = dpd * sg * (1.0 + ln * (1.0 - sg))
        dvec_ref[V_LNG:V_LNG + 1, :] += rsum(d_ln * xh)
        dvec_ref[V_LNB:V_LNB + 1, :] += rsum(d_ln)
        d_xh = d_ln * vec_ref[V_LNG:V_LNG + 1, :]
        d_cd = rstd * (d_xh - jnp.mean(d_xh, axis=-1, keepdims=True)
                       - xh * jnp.mean(d_xh * xh, axis=-1, keepdims=True))
        dvec_ref[V_CDB:V_CDB + 1, :] += rsum(d_cd)
        gd[0:t, :] = d_cd
        _shifted_copies(gd, xg, t + HALO)
        d_dg = jnp.zeros((t, BW), F32)
        for k in range(CONV_D):
            d_dg = d_dg + cw_ref[CW_D + k:CW_D + k + 1, :] * _window(gd, xg, CONV_D - 1 - k, t)
            dcw_ref[CW_D + k:CW_D + k + 1, :] += rsum(d_cd * _window(bufd, xd, HALO - (CONV_D - 1) + k, t))
        gd[t:t + HALO, :] = d_cd[0:HALO, :]
        put(C_D1, d_dg * s2)
        put(C_D2, d_dg * cur(C_D1) * s2 * (1.0 - s2))
        dp_ref[:, C_Q:C_Q + BW] = dq_ref[...]
        dkp2 = jnp.where(step == 0, 0.0, dkp2_ref[...].astype(F32))
        dp_ref[0:ATT_BLK, C_K:C_K + 256] = (dkc_ref[0:ATT_BLK, :].astype(F32) + dkp1_ref[...].astype(F32)).astype(BF16)
        dp_ref[ATT_BLK:t, C_K:C_K + 256] = (dkc_ref[ATT_BLK:t, :].astype(F32) + dkp2).astype(BF16)

    rev = lambda i: nt - 1 - i
    full = lambda r, c: pl.BlockSpec((r, c), lambda i: (0, 0))
    return _call(
        body, comm, (proj, proj, dpre, h, h, dq, dkc, dkp, dkp, convw, vecs, wx_bd, wa_bd, dproj), grid=(nt,),
        in_specs=[pl.BlockSpec((t, GL0), lambda i: (rev(i), 0)),
                  pl.BlockSpec((HALO, GL0), lambda i: (jnp.maximum(rev(i) * hb - 1, 0), 0)),
                  pl.BlockSpec((t, 4 * BW), lambda i: (rev(i), 0)),
                  pl.BlockSpec((t, BW), lambda i: (rev(i), 0)),
                  pl.BlockSpec((8, BW), lambda i: (jnp.maximum(rev(i) * (t // 8) - 1, 0), 0)),
                  pl.BlockSpec((t, BW), lambda i: (rev(i), 0)),
                  pl.BlockSpec((t, 256), lambda i: (rev(i), 0)),
                  pl.BlockSpec((ATT_BLK, 256), lambda i: (2 * rev(i) + 1, 0)),
                  pl.BlockSpec((ATT_BLK, 256), lambda i: (jnp.minimum(2 * rev(i) + 2, nb - 1), 0)),
                  pl.BlockSpec((None, CW_ROWS, BW), lambda i: (l, 0, 0)),
                  pl.BlockSpec((None, V_ROWS, BW), lambda i: (l, 0, 0)),
                  pl.BlockSpec((None, BW, BW), lambda i: (l, 0, 0)),
                  pl.BlockSpec((None, BW, BW), lambda i: (l, 0, 0)),
                  pl.BlockSpec(memory_space=pl.ANY)],
        out_specs=[pl.BlockSpec((t, GL0), lambda i: (rev(i), 0)),
                   full(CW_ROWS, BW), full(V_ROWS, BW), full(BW, BW), full(BW, BW)],
        out_shape=[_sds((s, IN_W), BF16), _sds((CW_ROWS, BW), F32), _sds((V_ROWS, BW), F32),
                   _sds((BW, BW), F32), _sds((BW, BW), F32)],
        scratch_shapes=[pltpu.VMEM((t + HALO, BW), F32)] * 3 + [pltpu.VMEM((7, t + HALO - 8, BW), F32)] * 2
        + [pltpu.VMEM((t + 8, BW), F32), pltpu.VMEM((t + 8, BW), F32)]
        + [pltpu.VMEM((t, BW), F32)] * 3
        + [pltpu.VMEM((t + 8, BW), F32), pltpu.VMEM((t + 8, BW), F32), pltpu.VMEM((t + HALO, BW), F32),
           pltpu.VMEM((1, BW), F32)],
        aliases={13: 0}, name=f"bwd_branch{l}")


def bwd_proj(dproj, x, dh1, g1, wt_in, l, comm=None):
    s = x.shape[0]
    tm = min(512, s)
    ck = 1408
    n_j, n_i = IN_W // ck, s // tm

    def body(dp_ref, x_ref, dh_ref, g_ref, w_ref, dx_ref, dw_ref, st_ref, dxn, xn_b, acc):
        j, i = pl.program_id(0), pl.program_id(1)
        rows = pl.ds(pl.multiple_of(i * tm, tm), tm)
        g = g_ref[l:l + 1, :]

        @pl.when(j == 0)
        def _():
            xv = x_ref[...]
            r = lax.rsqrt(jnp.mean(xv * xv, axis=-1, keepdims=True) + EPS)
            xn_b[rows, :] = (xv * r * g).astype(BF16)
            dxn[rows, :] = jnp.zeros((tm, D), F32)

        @pl.when((j == 0) & (i == 0))
        def _():
            st_ref[...] = jnp.zeros((8, D), F32)

        @pl.when(i == 0)
        def _():
            acc[...] = jnp.zeros((ck, D), F32)

        dp = dp_ref[...]
        dxn[rows, :] += _dot(dp, w_ref[...], _NN)
        acc[...] += _dot(dp, xn_b[rows, :], _TN)

        @pl.when(i == n_i - 1)
        def _():
            dw_ref[...] = acc[...].astype(BF16)

        @pl.when(j == n_j - 1)
        def _():
            xv = x_ref[...]
            r = lax.rsqrt(jnp.mean(xv * xv, axis=-1, keepdims=True) + EPS)
            n = xv * r
            dv = dxn[rows, :]
            dn = dv * g
            dx_ref[...] = dh_ref[...] + r * (dn - n * jnp.mean(dn * n, axis=-1, keepdims=True))
            st_ref[0:1, :] += jnp.sum(dv * n, axis=0, keepdims=True)

    lastrow = lambda j, i: (jnp.where(j == n_j - 1, i, 0), 0)
    return _call(
        body, comm, (dproj, x, dh1, g1, wt_in), grid=(n_j, n_i),
        in_specs=[pl.BlockSpec((tm, ck), lambda j, i: (i, j)),
                  pl.BlockSpec((tm, D), lambda j, i: (_edge_index(j, i, n_j, n_i), 0)),
                  pl.BlockSpec((tm, D), lastrow),
                  pl.BlockSpec((DEPTH, D), lambda j, i: (0, 0)),
                  pl.BlockSpec((ck, D), lambda j, i: (j, 0))],
        out_specs=[pl.BlockSpec((tm, D), lastrow), pl.BlockSpec((ck, D), lambda j, i: (j, 0)),
                   pl.BlockSpec((8, D), lambda j, i: (0, 0))],
        out_shape=[_sds((s, D), F32), _sds((IN_W, D), BF16), _sds((8, D), F32)],
        scratch_shapes=[pltpu.VMEM((s, D), F32), pltpu.VMEM((s, D), BF16), pltpu.VMEM((ck, D), F32)],
        name=f"bwd_proj{l}")


def bwd_proj_w(dproj, x, g1, l, half, comm=None):
    s = x.shape[0]
    tm = min(1024, s)
    ck = 1408
    c0, hw = W_IN_PARTS[half]
    n_j, n_i = IN_W // ck, s // tm

    def body(dp_ref, x_ref, g_ref, dw_ref, xn_b, acc):
        j, i = pl.program_id(0), pl.program_id(1)
        rows = pl.ds(pl.multiple_of(i * tm, tm), tm)

        @pl.when(j == 0)
        def _():
            xv = x_ref[...]
            r = lax.rsqrt(jnp.mean(xv * xv, axis=-1, keepdims=True) + EPS)
            xn_b[rows, :] = (xv * r * g_ref[l:l + 1, :])[:, c0:c0 + hw].astype(BF16)

        @pl.when(i == 0)
        def _():
            acc[...] = jnp.zeros((ck, hw), F32)

        acc[...] += _dot(dp_ref[...], xn_b[rows, :], _TN)

        @pl.when(i == n_i - 1)
        def _():
            dw_ref[...] = acc[...].astype(BF16)

    return _call(
        body, comm, (dproj, x, g1), grid=(n_j, n_i),
        in_specs=[pl.BlockSpec((tm, ck), lambda j, i: (i, j)),
                  pl.BlockSpec((tm, D), lambda j, i: (jnp.where(j == 0, i, n_i - 1), 0)),
                  pl.BlockSpec((DEPTH, D), lambda j, i: (0, 0))],
        out_specs=pl.BlockSpec((ck, hw), lambda j, i: (j, 0)),
        out_shape=_sds((IN_W, hw), BF16),
        scratch_shapes=[pltpu.VMEM((s, hw), BF16), pltpu.VMEM((ck, hw), F32)],
        name=f"bwd_proj_w{half}_{l}")


def bwd_proj_x(dproj, x, dh1, g1, wt_in, l, comm=None):
    s = x.shape[0]
    tm = min(512, s)
    ck = 1408
    n_j, n_i = IN_W // ck, s // tm

    def body(dp_ref, x_ref, dh_ref, g_ref, w_ref, dx_ref, st_ref, dxn):
        j, i = pl.program_id(0), pl.program_id(1)
        rows = pl.ds(pl.multiple_of(i * tm, tm), tm)
        g = g_ref[l:l + 1, :]

        @pl.when((j == 0) & (i == 0))
        def _():
            st_ref[...] = jnp.zeros((8, D), F32)

        part = _dot(dp_ref[...], w_ref[...], _NN)

        @pl.when(j == 0)
        def _():
            dxn[rows, :] = part

        @pl.when(j > 0)
        def _():
            dxn[rows, :] += part

        @pl.when(j == n_j - 1)
        def _():
            xv = x_ref[...]
            r = lax.rsqrt(jnp.mean(xv * xv, axis=-1, keepdims=True) + EPS)
            n = xv * r
            dv = dxn[rows, :]
            dn = dv * g
            dx_ref[...] = dh_ref[...] + r * (dn - n * jnp.mean(dn * n, axis=-1, keepdims=True))
            st_ref[0:1, :] += jnp.sum(dv * n, axis=0, keepdims=True)

    lastrow = lambda j, i: (jnp.where(j == n_j - 1, i, 0), 0)
    return _call(
        body, comm, (dproj, x, dh1, g1, wt_in), grid=(n_j, n_i),
        in_specs=[pl.BlockSpec((tm, ck), lambda j, i: (i, j)), pl.BlockSpec((tm, D), lastrow),
                  pl.BlockSpec((tm, D), lastrow),
                  pl.BlockSpec((DEPTH, D), lambda j, i: (0, 0)), pl.BlockSpec((ck, D), lambda j, i: (j, 0))],
        out_specs=[pl.BlockSpec((tm, D), lastrow), pl.BlockSpec((8, D), lambda j, i: (0, 0))],
        out_shape=[_sds((s, D), F32), _sds((8, D), F32)],
        scratch_shapes=[pltpu.VMEM((s, D), F32)], name=f"bwd_proj_x{l}")


def _block_diag(w):
    nl, nb, bw, _ = w.shape
    eye = jnp.eye(nb, dtype=w.dtype)
    return jnp.einsum("lhij,hk->lhikj", w, eye).reshape(nl, nb * bw, nb * bw).astype(BF16)


class NoOverlap:
    def __init__(self, big):
        self.big = big

    def weights(self, l):
        return self.big[l]

    def job(self, slot, l):
        return None

    def done(self, slot, l, results):
        pass

    def new_grads(self, group, l, grads):
        pass

    def new_small(self, l, arrays, head_stats):
        pass


def local_step(x, target, norm1_g, norm2_g, final_g, convw, vecs, lru_wx, lru_wa, plan):
    wx_bd, wa_bd = _block_diag(lru_wx), _block_diag(lru_wa)

    def run(fn, slot, l, *args):
        res, cres = fn(*args, l, comm=plan.job(slot, l))
        plan.done(slot, l, cres)
        return res

    saved = []
    for l in range(DEPTH):
        proj = run(fwd_proj, "fwd_proj", l, x, norm1_g, plan.weights(l)["in_t"])
        pre_abd, h = run(fwd_branch, "fwd_branch", l, proj, convw, vecs, wx_bd, wa_bd)
        pre_c = run(fwd_attn, "fwd_attn", l, proj, vecs)
        w = plan.weights(l)
        y4, merged, h1 = run(fwd_merge, "fwd_merge", l, x, proj, pre_abd, pre_c, w["a_t"], w["b_t"], w["c_t"], w["d_t"], w["o"])
        w = plan.weights(l)
        x_out, fg, fu = run(fwd_ffn, "fwd_ffn", l, h1, norm2_g, w["gate_t"], w["up_t"], w["down"])
        saved.append((x, proj, pre_abd, h, pre_c, y4, merged, h1, fg, fu))
        x = x_out
    dx, head_stats = loss_head(x, final_g.reshape(1, D), target)
    small = [None] * DEPTH
    for l in reversed(range(DEPTH)):
        x_in, proj, pre_abd, h, pre_c, y4, merged, h1, fg, fu = saved[l]
        w = plan.weights(l)
        dh1, d_gate, d_up, d_down, st_ffn = run(bwd_ffn, "bwd_ffn", l, dx, h1, fg, fu, norm2_g, w["gate_t"], w["up_t"], w["down"])
        plan.new_grads("ffn", l, dict(gate_t=d_gate, up_t=d_up, down=d_down))
        dproj, dpre, d_o, d_a, d_b, d_c, d_d = run(
            bwd_merge, "bwd_merge", l, dh1, y4, proj, merged, pre_abd, pre_c, w["a_t"], w["b_t"], w["c_t"], w["d_t"], w["o"])
        plan.new_grads("out", l, dict(a_t=d_a, b_t=d_b, c_t=d_c, d_t=d_d, o=d_o))
        dq, dkc, dkp, st_attn = run(bwd_attn, "bwd_attn", l, proj, dpre, vecs)
        dproj, dcw, dvec, dwx, dwa = run(bwd_branch, "bwd_branch", l, proj, dproj, dpre, h, dq, dkc, dkp, convw, vecs, wx_bd, wa_bd)
        if l > 0:
            dx, d_in, st_proj = run(bwd_proj, "bwd_proj", l, dproj, x_in, dh1, norm1_g, w["in_t"])
            plan.new_grads("in", l, dict(in_t=d_in))
        else:
            for half, name in enumerate(("in_a", "in_b")):
                d_half = run(functools.partial(bwd_proj_w, half=half), f"bwd_proj_w{half}", l, dproj, x_in, norm1_g)
                plan.new_grads(name, l, {name: d_half})
            dx, st_proj = run(bwd_proj_x, "bwd_proj_x", l, dproj, x_in, dh1, norm1_g, w["in_t"])
        small[l] = (st_proj, st_ffn, dvec, st_attn, dcw, dwx, dwa)
        plan.new_small(l, small[l], head_stats)
    return head_stats, dx, small


BIG = dict(in_t=("w_in", "view"), a_t=("w_a_out", "transpose"), b_t=("w_b_out", "transpose"), c_t=("w_c_out", "transpose"),
           d_t=("w_d_out", "transpose"), o=("w_o", "plain"), gate_t=("w_ffn_gate", "view"), up_t=("w_ffn_up", "view"),
           down=("w_ffn_down", "plain"))


def cast_transpose(w, name):
    nl, a, b = w.shape
    ta = min(256, a)

    def body(w_ref, o_ref):
        o_ref[...] = w_ref[...].T.astype(BF16)

    return pl.pallas_call(
        body, grid=(nl, a // ta),
        in_specs=[pl.BlockSpec((None, ta, b), lambda l, i: (l, i, 0))],
        out_specs=pl.BlockSpec((None, b, ta), lambda l, i: (l, 0, i)),
        out_shape=_sds((nl, b, a), BF16), compiler_params=_cparams(2), name=name)(w)


def add_partials(mine, recv, core, name):
    n = len(mine)

    def body(core_ref, *refs):
        del core_ref
        for a_ref, b_ref, o_ref in zip(refs[:n], refs[n:2 * n], refs[2 * n:]):
            o_ref[...] = (a_ref[...].astype(F32) + b_ref[...].astype(F32)).astype(BF16)

    return pl.pallas_call(
        body,
        grid_spec=pltpu.PrefetchScalarGridSpec(
            num_scalar_prefetch=1, grid=(4,),
            in_specs=[pl.BlockSpec((None, None) + a.shape[2:], lambda i, cr: (i, cr[0], 0, 0)) for a in mine]
            + [pl.BlockSpec((None,) + b.shape[1:], lambda i, cr: (i, 0, 0)) for b in recv],
            out_specs=[pl.BlockSpec((None,) + b.shape[1:], lambda i, cr: (i, 0, 0)) for b in recv]),
        out_shape=[_sds(b.shape, BF16) for b in recv], compiler_params=_cparams(1), name=name)(core, *mine, *recv)


def _adamw(w, g, m, v):
    m = ADAM_B1 * m + (1.0 - ADAM_B1) * g
    v = ADAM_B2 * v + (1.0 - ADAM_B2) * (g * g)
    m_hat = m / (1.0 - ADAM_B1 ** ADAM_STEP)
    v_hat = v / (1.0 - ADAM_B2 ** ADAM_STEP)
    delta = -ADAM_LR * (m_hat / (jnp.sqrt(v_hat) + ADAM_EPS) + ADAM_WD * w)
    return delta, m, v


def adamw_big(contrib, w, m, v, transposed, name, comm=None):
    nsrc, nl, rows, cols = contrib.shape
    ct = 256

    def body(c_ref, w_ref, m_ref, v_ref, g_out, d_out, m_out, v_out):
        g = c_ref[0].astype(F32)
        for src in range(1, nsrc):
            g = g + c_ref[src].astype(F32)
        if transposed:
            g = g.T
        delta, mn, vn = _adamw(w_ref[...], g, m_ref[...], v_ref[...])
        g_out[...] = g
        d_out[...] = delta
        m_out[...] = mn
        v_out[...] = vn

    if transposed:
        wspec = pl.BlockSpec((None, ct, rows), lambda l, j: (l, j, 0))
    else:
        wspec = pl.BlockSpec((None, rows, ct), lambda l, j: (l, 0, j))
    return _call(
        body, comm, (contrib, w, m, v), grid=(nl, cols // ct),
        in_specs=[pl.BlockSpec((nsrc, None, rows, ct), lambda l, j: (0, l, 0, j)), wspec, wspec, wspec],
        out_specs=[wspec] * 4, out_shape=[_sds(w.shape, F32)] * 4, name=name)


VEC_NAMES = ("conv_a_b", "lru_bx", "lru_ba", "lru_lambda", "conv_d_b", "ln_d_g", "ln_d_b")
P_N1, P_N2, P_VEC, P_CONV, P_LRU = 0, 1, 2, 6, 6 + CW_ROWS
P_FINAL, P_LOSS, P_ROWS = P_LRU + HD, P_LRU + HD + 1, P_LRU + HD + 2
SMALL = ("norm1_g", "conv_a_w", "conv_a_b", "lru_wx", "lru_bx", "lru_wa", "lru_ba", "lru_lambda", "conv_b_w", "sinks",
         "conv_d_w", "conv_d_b", "ln_d_g", "ln_d_b", "norm2_g", "final_g")
VMEM_FULL = pl.BlockSpec(memory_space=pltpu.VMEM)


def _stack_vecs(p):
    rows = [p[n] for n in VEC_NAMES] + [jnp.pad(p["sinks"], ((0, 0), (0, BW - N_HEADS)))]
    return jnp.stack(rows, axis=1)


def _stack_convs(p):
    nl, _, ch = p["conv_a_w"].shape
    z = jnp.zeros((nl, 1, ch), F32)
    return jnp.concatenate([p["conv_a_w"], p["conv_b_w"], z, p["conv_d_w"], z], axis=1)


def _vec_place(r):
    return P_VEC + r // 2, (r % 2) * BW


def pack_small(arrays, head_stats, l):
    n = len(arrays)

    def body(*refs):
        st_proj, st_ffn, dvec, st_attn, dcw, dwx, dwa = refs[:n]
        pack = refs[-1]
        pack[...] = jnp.zeros((P_ROWS, D), F32)
        lane = lax.broadcasted_iota(jnp.int32, (HD, BW), 1)
        pack[P_N1:P_N1 + 1, :] = st_proj[0:1, :]
        pack[P_N2:P_N2 + 1, :] = st_ffn[0:1, :]
        for r in range(len(VEC_NAMES)):
            row, c0 = _vec_place(r)
            pack[row:row + 1, c0:c0 + BW] = dvec[r:r + 1, :]
        row, c0 = _vec_place(V_SINK)
        pack[row:row + 1, c0:c0 + 128] = st_attn[0:1, :]
        pack[P_CONV:P_CONV + CW_ROWS, 0:BW] = dcw[...]
        for mat, c0 in ((dwx, 0), (dwa, BW)):
            blocks = jnp.zeros((HD, BW), F32)
            for h in range(BW // HD):
                blocks = jnp.where((lane >= HD * h) & (lane < HD * (h + 1)), mat[HD * h:HD * (h + 1), :], blocks)
            pack[P_LRU:P_LRU + HD, c0:c0 + BW] = blocks
        if head_stats is not None:
            pack[P_FINAL:P_LOSS + 1, :] = refs[n][0:2, :]

    flat = list(arrays) + ([] if head_stats is None else [head_stats])
    return pl.pallas_call(body, out_shape=_sds((P_ROWS, D), F32), in_specs=[VMEM_FULL] * len(flat), out_specs=VMEM_FULL,
                          name=f"pack_small{l}", compiler_params=pltpu.CompilerParams(vmem_limit_bytes=VMEM_LIMIT))(*flat)


def adamw_small(gathered, me, w, m, v):
    ns = len(SMALL)

    def body(me_ref, *refs):
        c_refs, refs = refs[:DEPTH], refs[DEPTH:]
        w_refs, m_refs, v_refs = refs[:ns], refs[ns:2 * ns], refs[2 * ns:3 * ns]
        loss_ref, outs, gs = refs[3 * ns], refs[3 * ns + 1:3 * ns + 1 + 4 * ns], refs[-1]
        for l in range(DEPTH):
            gs[l] = c_refs[l][0]
            for dev in range(1, NDEV):
                gs[l] += c_refs[l][dev]
        loss_ref[...] = gs[DEPTH - 1, P_LOSS:P_LOSS + 1, 0:128]

        def update(name, sel, g):
            i = SMALL.index(name)
            delta, mn, vn = _adamw(w_refs[i][sel], g, m_refs[i][sel], v_refs[i][sel])
            for o_ref, val in zip(outs[4 * i:4 * i + 4], (g, delta, mn, vn)):
                o_ref[sel] = val

        update("final_g", (slice(0, 1), slice(None)), gs[DEPTH - 1, P_FINAL:P_FINAL + 1, :])
        shift = (BW - me_ref[0] * (BW // NDEV)) & (BW - 1)
        for l in range(DEPTH):
            row = (slice(l, l + 1), slice(None))
            update("norm1_g", row, gs[l, P_N1:P_N1 + 1, :])
            update("norm2_g", row, gs[l, P_N2:P_N2 + 1, :])
            for r, name in enumerate(VEC_NAMES):
                prow, c0 = _vec_place(r)
                update(name, row, gs[l, prow:prow + 1, c0:c0 + BW])
            prow, c0 = _vec_place(V_SINK)
            update("sinks", row, gs[l, prow:prow + 1, c0:c0 + N_HEADS])
            mine = pltpu.roll(gs[l, P_CONV:P_CONV + CW_ROWS, 0:BW], shift, 1)[:, 0:BW // NDEV]
            update("conv_a_w", (l,), mine[CW_A:CW_A + CONV_A])
            update("conv_b_w", (l,), mine[CW_B:CW_B + CONV_B])
            update("conv_d_w", (l,), mine[CW_D:CW_D + CONV_D])
            for h in range(BW // HD):
                update("lru_wx", (l, h), gs[l, P_LRU:P_LRU + HD, HD * h:HD * (h + 1)])
                update("lru_wa", (l, h), gs[l, P_LRU:P_LRU + HD, BW + HD * h:BW + HD * (h + 1)])

    args = [p[n] for p in (w, m, v) for n in SMALL]
    full = lambda a: pl.BlockSpec(a.shape, lambda i, me_ref: (0,) * a.ndim)
    out_shape = [_sds((1, 128), F32)] + [_sds(w[n].shape, F32) for n in SMALL for _ in range(4)]
    outs = pl.pallas_call(
        body,
        grid_spec=pltpu.PrefetchScalarGridSpec(
            num_scalar_prefetch=1, grid=(1,),
            in_specs=[full(a) for a in list(gathered) + args], out_specs=[full(o) for o in out_shape],
            scratch_shapes=[pltpu.VMEM((DEPTH, P_ROWS, D), F32)]),
        out_shape=out_shape, name="adamw_small", compiler_params=_cparams(1))(me, *gathered, *args)
    return outs[0], {n: outs[1 + 4 * i:5 + 4 * i] for i, n in enumerate(SMALL)}


def merge_jobs(jobs):
    jobs = [j for j in jobs if j is not None]
    if not jobs:
        return None, []
    inputs, aliases, outs, sems, cuts = [], {}, [], [], []
    for j in jobs:
        i0, o0, s0 = len(inputs), len(outs), len(sems)
        aliases.update({i0 + i: o0 + o for i, o in j.aliases.items()})
        inputs += j.inputs
        outs += j.out_shapes
        sems += j.sem_shapes
        cuts.append((i0, len(inputs), o0, len(outs), s0, len(sems)))

    def each(which):
        def go(cins, couts, s):
            for j, (i0, i1, o0, o1, s0, s1) in zip(jobs, cuts):
                getattr(j, which)(cins[i0:i1], couts[o0:o1], s[s0:s1])
        return go

    return CommJob(inputs, aliases, outs, sems, each("start"), each("finish")), [(c[2], c[3]) for c in cuts]


SIXTHS = 6
OUT_KINDS = ("a_t", "b_t", "c_t", "d_t", "o")
GATHER_PLAN = {
    "fwd_proj": [(k, 0, 0, 6) for k in OUT_KINDS] + [("gate_t", 0, 0, 3)],
    "fwd_branch": [("gate_t", 0, 3, 6), ("up_t", 0, 0, 3)],
    "fwd_attn": [("up_t", 0, 3, 6), ("down", 0, 0, 6)],
    "fwd_merge": [("in_t", 1, 0, 2)],
    "fwd_ffn": [("in_t", 1, 2, 6)],
}
SIBLING_PLAN = {"bwd_merge": ("ffn", 0), "bwd_branch": ("out", 0), "bwd_ffn": ("in", 1),
                "bwd_proj_w1": ("in_a", 0), "bwd_proj_x": ("in_b", 0)}
GROUPS = dict(ffn=("gate_t", "up_t", "down"), out=OUT_KINDS, in_a=("in_a",), in_b=("in_b",))
GROUPS["in"] = ("in_t",)
COLUMN_HALF = dict(in_a=("in_t", W_IN_PARTS[0][0]), in_b=("in_t", W_IN_PARTS[1][0]))
CHIP_PLAN = {
    "bwd_attn": [("in_t", 1, 3, 6), ("gate_t", 0, 0, 3)],
    "bwd_branch": [("gate_t", 0, 3, 6), ("up_t", 0, 0, 6), ("down", 0, 0, 6)],
    "bwd_proj": [(k, 0, 0, 6) for k in OUT_KINDS],
    "bwd_proj_w0": [(k, 0, 0, 6) for k in OUT_KINDS[:3]],
    "bwd_proj_w1": [(k, 0, 0, 6) for k in OUT_KINDS[3:]],
    "bwd_merge": [("in_t", 1, 0, 3)],
    "bwd_proj_x": [("in_a", 0, 0, 6)],
    "adamw_gate_t": [("in_b", 0, 0, 6)],
}
SMALL_GATHER_PLAN = {"bwd_ffn": 1, "adamw_down": 0}


class Overlap:
    def __init__(self, shards, core):
        self.shards = shards
        self.core = core
        self.gathered = [dict.fromkeys(BIG) for _ in range(DEPTH)]
        self.views = {}
        self.partial = {}
        self.contrib = dict.fromkeys(BIG)
        self.small_packs = [None] * DEPTH
        self.small_gathered = [None] * DEPTH
        self._open = None

    def weights(self, l):
        return self.gathered[l]

    def new_grads(self, group, l, grads):
        for k, g in grads.items():
            self.views[k, l] = g.reshape(4, 2, g.shape[0] // NDEV, g.shape[1])

    def new_small(self, l, arrays, head_stats):
        self.small_packs[l] = pack_small(arrays, head_stats if l == DEPTH - 1 else None, l)

    @staticmethod
    def _rows(shard_rows, f0, f1):
        return shard_rows * f0 // SIXTHS, shard_rows * (f1 - f0) // SIXTHS

    def job(self, slot, l):
        jobs, notes = [], []
        pieces = [(k, l + dl, f0, f1) for k, dl, f0, f1 in GATHER_PLAN.get(slot, []) if l + dl < DEPTH]
        if pieces:
            jobs.append(gather_job([((k, ll), self.shards[ll][k], self.gathered[ll][k],
                                     *self._rows(self.shards[ll][k].shape[0], f0, f1)) for k, ll, f0, f1 in pieces]))
            notes.append(("gather", list(dict.fromkeys((k, ll) for k, ll, _, _ in pieces))))
        if slot in SIBLING_PLAN and l + SIBLING_PLAN[slot][1] < DEPTH:
            group, dl = SIBLING_PLAN[slot]
            keys = [(k, l + dl) for k in GROUPS[group]]
            jobs.append(sibling_exchange_job([self.views[key] for key in keys]))
            notes.append(("sibling", keys))
        pieces = [(k, l + dl, f0, f1) for k, dl, f0, f1 in CHIP_PLAN.get(slot, []) if l + dl < DEPTH]
        if pieces:
            whole = [(*COLUMN_HALF.get(k, (k, 0)), k, ll, f0, f1) for k, ll, f0, f1 in pieces]
            jobs.append(chip_exchange_job([(self.partial[k, ll], self.contrib[kind], kind, ll,
                                            *self._rows(self.partial[k, ll].shape[1], f0, f1), col0, self.shards[ll][kind].shape[1])
                                           for kind, col0, k, ll, f0, f1 in whole]))
            notes.append(("chips", list(dict.fromkeys(kind for kind, *_ in whole))))
        if slot in SMALL_GATHER_PLAN and l + SMALL_GATHER_PLAN[slot] < DEPTH:
            ll = l + SMALL_GATHER_PLAN[slot]
            jobs.append(gather_job([("small", self.small_packs[ll], None, 0, P_ROWS)]))
            notes.append(("small", ll))
        job, spans = merge_jobs(jobs)
        self._open = (slot, l, notes, spans)
        return job

    def done(self, slot, l, results):
        open_slot, open_l, notes, spans = self._open
        assert (open_slot, open_l) == (slot, l)
        for (what, keys), (r0, r1) in zip(notes, spans):
            res = results[r0:r1]
            if what == "gather":
                for (k, ll), g in zip(keys, res):
                    self.gathered[ll][k] = g
            elif what == "sibling":
                sums = add_partials([self.views[key] for key in keys], list(res), self.core, f"chip_sum_{keys[0][0]}{keys[0][1]}")
                self.partial.update(zip(keys, sums))
            elif what == "chips":
                for k, c in zip(keys, res):
                    self.contrib[k] = c
            else:
                self.small_gathered[keys], = res


SMALL = ("norm1_g", "conv_a_w", "conv_a_b", "lru_wx", "lru_bx", "lru_wa", "lru_ba", "lru_lambda", "conv_b_w", "sinks",
         "conv_d_w", "conv_d_b", "ln_d_g", "ln_d_b", "norm2_g", "final_g")
WEIGHTS = ("norm1_g", "w_in", "conv_a_w", "conv_a_b", "lru_wx", "lru_bx", "lru_wa", "lru_ba", "lru_lambda", "w_a_out",
           "conv_b_w", "w_b_out", "sinks", "w_c_out", "conv_d_w", "conv_d_b", "ln_d_g", "ln_d_b", "w_d_out", "w_o",
           "norm2_g", "w_ffn_gate", "w_ffn_up", "w_ffn_down", "final_g")


def kernel(x, norm1_g, w_in, conv_a_w, conv_a_b, lru_wx, lru_bx, lru_wa, lru_ba, lru_lambda, w_a_out, conv_b_w, w_b_out, sinks, w_c_out, conv_d_w, conv_d_b, ln_d_g, ln_d_b, w_d_out, w_o, norm2_g, w_ffn_gate, w_ffn_up, w_ffn_down, final_g, loss_target, m_norm1_g, m_w_in, m_conv_a_w, m_conv_a_b, m_lru_wx, m_lru_bx, m_lru_wa, m_lru_ba, m_lru_lambda, m_w_a_out, m_conv_b_w, m_w_b_out, m_sinks, m_w_c_out, m_conv_d_w, m_conv_d_b, m_ln_d_g, m_ln_d_b, m_w_d_out, m_w_o, m_norm2_g, m_w_ffn_gate, m_w_ffn_up, m_w_ffn_down, m_final_g, v_norm1_g, v_w_in, v_conv_a_w, v_conv_a_b, v_lru_wx, v_lru_bx, v_lru_wa, v_lru_ba, v_lru_lambda, v_w_a_out, v_conv_b_w, v_w_b_out, v_sinks, v_w_c_out, v_conv_d_w, v_conv_d_b, v_ln_d_g, v_ln_d_b, v_w_d_out, v_w_o, v_norm2_g, v_w_ffn_gate, v_w_ffn_up, v_w_ffn_down, v_final_g):
    args = dict(locals())
    w = {n: args[n] for n in WEIGHTS}
    m = {n: args["m_" + n] for n in WEIGHTS}
    v = {n: args["v_" + n] for n in WEIGHTS}
    me = _dev_index(*_mesh_pos())

    def rows_major(a, how):
        return jnp.swapaxes(a, 1, 2) if how == "view" else a

    stacked = {k: cast_transpose(w[n], "prep_" + k) if how == "transpose" else rows_major(w[n], how).astype(BF16)
               for k, (n, how) in BIG.items()}
    plan = Overlap([{k: stacked[k][l] for k in BIG} for l in range(DEPTH)], lax.axis_index("c").astype(jnp.int32).reshape(1))
    convs = _stack_convs(w).reshape(DEPTH * CW_ROWS, BW // NDEV)
    g_in0, g_conv = _comm_only(gather_job([(("in_t", 0), plan.shards[0]["in_t"], None, 0, plan.shards[0]["in_t"].shape[0]),
                                           ("convs", convs, None, 0, convs.shape[0])]), "gather_first")
    plan.gathered[0]["in_t"] = g_in0
    convw = g_conv.reshape(NDEV, DEPTH, CW_ROWS, BW // NDEV).transpose(1, 2, 0, 3).reshape(DEPTH, CW_ROWS, BW)

    vecs = _stack_vecs(w)
    head_stats, grad_x, grads = local_step(x[0], loss_target[0], norm1_g, norm2_g, final_g, convw, vecs, lru_wx, lru_wa, plan)


    out = {}
    for k in ("down", "gate_t", "up_t", "o", "a_t", "b_t", "c_t", "d_t", "in_t"):
        n, how = BIG[k]
        res, cres = adamw_big(plan.contrib[k], rows_major(w[n], how), rows_major(m[n], how), rows_major(v[n], how),
                              how == "transpose", "adamw_" + k, comm=plan.job("adamw_" + k, 0))
        plan.done("adamw_" + k, 0, cres)
        out[n] = [rows_major(r, how) for r in res]

    def own_shapes(p):
        return {n: p[n].reshape(1, D) if n == "final_g" else p[n] for n in SMALL}

    loss, small = adamw_small([g.reshape(NDEV, P_ROWS, D) for g in plan.small_gathered], me.astype(jnp.int32).reshape(1),
                              own_shapes(w), own_shapes(m), own_shapes(v))
    for n in SMALL:
        out[n] = [r.reshape(w[n].shape) for r in small[n]]
    loss = loss[0, 0]
    return (loss, grad_x[None], *[out[n][0] for n in WEIGHTS], *[out[n][1] for n in WEIGHTS],
            *[out[n][2] for n in WEIGHTS], *[out[n][3] for n in WEIGHTS])
```

```python
import functools

import jax
import jax.numpy as jnp
from jax import lax
from jax.experimental import pallas as pl
from jax.experimental.pallas import tpu as pltpu

F32 = jnp.float32
BF16 = jnp.bfloat16
E = pl.Element

D = 1024
BW = 512
IN_W = 8448
GL0 = 4352
FF = 2816
N_HEADS = 8
N_KV = 2
HD = 64
ATT_BLK = 128
EPS = 1e-6
LRU_C = 8.0
NEG_INF = -1e30
DEPTH = 2
NDEV = 8
CONV_A, CONV_B, CONV_D = 4, 3, 31
C_AX, C_AG, C_BV, C_BC, C_BB, C_Q, C_K, C_V, C_D1, C_D2 = 0, 512, 1024, 1536, 2048, 2560, 3072, 3200, 3328, 3840
CW_A, CW_B, CW_D, CW_ROWS = 0, 4, 8, 40
V_CAB, V_BX, V_BA, V_LAM, V_CDB, V_LNG, V_LNB, V_SINK, V_ROWS = 0, 1, 2, 3, 4, 5, 6, 7, 8
HALO = 32
W_IN_PARTS = ((0, 768), (768, 256))

ADAM_LR, ADAM_B1, ADAM_B2, ADAM_EPS, ADAM_WD, ADAM_STEP = 0.001, 0.9, 0.999, 1e-08, 0.01, 10

VMEM_LIMIT = 56 * 1024 * 1024

_NN = (((1,), (0,)), ((), ()))
_NT = (((1,), (1,)), ((), ()))
_TN = (((0,), (0,)), ((), ()))


def _dot(a, b, dims):
    return lax.dot_general(a.astype(BF16), b.astype(BF16), dims, preferred_element_type=F32)


def _cparams(n_axes):
    return pltpu.CompilerParams(dimension_semantics=("arbitrary",) * n_axes, vmem_limit_bytes=VMEM_LIMIT)


def _sds(shape, dtype):
    return jax.ShapeDtypeStruct(tuple(shape), dtype)


def _sigmoid(x):
    return jax.nn.sigmoid(x)


def _neg_expm1(x):
    p = x * (1.0 + x * (0.5 + x * (1.0 / 6.0 + x * (1.0 / 24.0 + x * (1.0 / 120.0)))))
    return jnp.where(x > -0.1, -p, 1.0 - jnp.exp(x))


def _softplus(z):
    return jnp.maximum(z, 0.0) + jnp.log1p(jnp.exp(-jnp.abs(z)))


def _gelu_and_grad(x):
    c = 0.7978845608028654
    inner = c * (x + 0.044715 * x * x * x)
    t = jnp.tanh(inner)
    g = 0.5 * x * (1.0 + t)
    dg = 0.5 * (1.0 + t) + 0.5 * x * (1.0 - t * t) * c * (1.0 + 3.0 * 0.044715 * x * x)
    return g, dg


ANY = pl.BlockSpec(memory_space=pl.ANY)
MESH = pl.DeviceIdType.MESH


def _mesh_pos():
    return lax.axis_index("x"), lax.axis_index("y"), lax.axis_index("c")


def _dev_index(px, py, pc):
    return 4 * px + 2 * py + pc


class CommJob:
    def __init__(self, inputs, aliases, out_shapes, sem_shapes, start, finish, relay=None):
        self.inputs, self.aliases, self.out_shapes, self.sem_shapes = list(inputs), dict(aliases), list(out_shapes), list(sem_shapes)
        self.start, self.finish, self.relay = start, finish, relay


def _call(body, comm, args, *, grid, in_specs, out_specs, out_shape, scratch_shapes=(), name, aliases=None):
    single = not isinstance(out_shape, (list, tuple))
    out_specs = [out_specs] if single else list(out_specs)
    out_shape = [out_shape] if single else list(out_shape)
    scratch_shapes = list(scratch_shapes)
    n_in, n_out, n_scr, n_axes = len(in_specs), len(out_shape), len(scratch_shapes), len(grid)
    params = pltpu.CompilerParams(dimension_semantics=("arbitrary",) * n_axes, vmem_limit_bytes=VMEM_LIMIT)
    io_aliases = dict(aliases or {})
    if comm is None:
        outs = pl.pallas_call(body, grid=grid, in_specs=in_specs, out_specs=out_specs, out_shape=out_shape,
                              scratch_shapes=scratch_shapes, input_output_aliases=io_aliases, compiler_params=params,
                              name=name)(*args)
        return (outs[0] if single else outs), []
    c_in, c_out = len(comm.inputs), len(comm.out_shapes)
    io_aliases.update({n_in + i: n_out + o for i, o in comm.aliases.items()})

    def wrapped(*refs):
        ins, cins = refs[:n_in], refs[n_in:n_in + c_in]
        outs = refs[n_in + c_in:n_in + c_in + n_out]
        couts = refs[n_in + c_in + n_out:n_in + c_in + n_out + c_out]
        rest = refs[n_in + c_in + n_out + c_out:]
        scr, sems = rest[:n_scr], rest[n_scr:]
        first = functools.reduce(lambda a, b: a & b, [pl.program_id(a) == 0 for a in range(n_axes)])
        last = functools.reduce(lambda a, b: a & b, [pl.program_id(a) == pl.num_programs(a) - 1 for a in range(n_axes)])

        @pl.when(first)
        def _():
            comm.start(cins, couts, sems)

        if comm.relay is not None:
            step = functools.reduce(lambda a, b: a * grid[b] + pl.program_id(b), range(1, n_axes), pl.program_id(0))
            n_steps = functools.reduce(lambda a, b: a * b, grid)

            @pl.when(step == 3 * n_steps // 4)
            def _():
                comm.relay(cins, couts, sems)

        body(*ins, *outs, *scr)

        @pl.when(last)
        def _():
            comm.finish(cins, couts, sems)

    outs = pl.pallas_call(
        wrapped, grid=grid, in_specs=list(in_specs) + [ANY] * c_in, out_specs=out_specs + [ANY] * c_out,
        out_shape=out_shape + comm.out_shapes, scratch_shapes=scratch_shapes + comm.sem_shapes,
        input_output_aliases=io_aliases, compiler_params=params, name=name)(*args, *comm.inputs)
    res, cres = outs[:n_out], outs[n_out:]
    return (res[0] if single else res), cres


def _comm_only(comm, name):
    c_in, c_out = len(comm.inputs), len(comm.out_shapes)

    def body(*refs):
        cins, couts, sems = refs[:c_in], refs[c_in:c_in + c_out], refs[c_in + c_out:]
        comm.start(cins, couts, sems)
        if comm.relay is not None:
            comm.relay(cins, couts, sems)
        comm.finish(cins, couts, sems)

    return pl.pallas_call(body, in_specs=[ANY] * c_in, out_specs=[ANY] * c_out, out_shape=comm.out_shapes,
                          scratch_shapes=comm.sem_shapes, input_output_aliases=comm.aliases, name=name)(*comm.inputs)


def gather_job(pieces):
    inputs, aliases, out_shapes, plan, where = [], {}, [], [], {}
    for key, shard, gathered, row0, nrows in pieces:
        if key not in where:
            where[key] = (len(inputs), len(out_shapes))
            inputs.append(shard)
            if gathered is not None:
                aliases[len(inputs)] = len(out_shapes)
                inputs.append(gathered)
            out_shapes.append(_sds((NDEV * shard.shape[0], shard.shape[1]), shard.dtype))
        plan.append((*where[key], shard.shape[0], row0, nrows))
    n = len(plan)

    def copies(cins, couts, sems):
        send_sems, recv_sems, local_sems = sems
        x, y, c = _mesh_pos()
        me, sibling = (x, y, c), (x, y, 1 - c)
        chips = [(1 - x, y), (x, 1 - y), (1 - x, 1 - y)]
        local, first, relay, recv_ici, recv_d2d = [], [], [], [], []
        for p, (i_shard, i_out, rows, row0, nrows) in enumerate(plan):
            src = cins[i_shard].at[pl.ds(row0, nrows), :]

            def slot(dev, i_out=i_out, rows=rows, row0=row0, nrows=nrows):
                return couts[i_out].at[pl.ds(_dev_index(*dev) * rows + row0, nrows), :]

            def copy(g, dev, to, src=None, p=p, slot=slot):
                return pltpu.make_async_remote_copy(
                    src_ref=slot(dev) if src is None else src, dst_ref=slot(dev),
                    send_sem=send_sems.at[g, p], recv_sem=recv_sems.at[g, p], device_id=to, device_id_type=MESH)

            local.append(pltpu.make_async_copy(src, slot(me), local_sems.at[p]))
            first.append(copy(0, me, sibling, src=src))
            recv_d2d.append(copy(0, sibling, me))
            for j, chip in enumerate(chips):
                first.append(copy(1 + j, me, (*chip, c), src=src))
                recv_ici.append(copy(1 + j, (*chip, c), me))
                relay.append(copy(4 + j, (*chip, c), sibling))
                recv_d2d.append(copy(4 + j, (*chip, 1 - c), me))
        return local, first, relay, recv_ici, recv_d2d

    def start(cins, couts, sems):
        local, first, _, _, _ = copies(cins, couts, sems)
        for cp in local + first:
            cp.start()

    def pass_on(cins, couts, sems):
        _, _, relay, recv_ici, _ = copies(cins, couts, sems)
        for cp in recv_ici:
            cp.wait_recv()
        for cp in relay:
            cp.start()

    def finish(cins, couts, sems):
        local, first, relay, _, recv_d2d = copies(cins, couts, sems)
        for cp in recv_d2d:
            cp.wait_recv()
        for cp in first + relay:
            cp.wait_send()
        for cp in local:
            cp.wait()

    sem_shapes = [pltpu.SemaphoreType.DMA((7, n)), pltpu.SemaphoreType.DMA((7, n)), pltpu.SemaphoreType.DMA((n,))]
    return CommJob(inputs, aliases, out_shapes, sem_shapes, start, finish, relay=pass_on)


def sibling_exchange_job(grads):
    n = len(grads)

    def copies(cins, couts, sems):
        send_sems, recv_sems = sems
        x, y, c = _mesh_pos()
        return [pltpu.make_async_remote_copy(
            src_ref=cins[q].at[:, 1 - c], dst_ref=couts[q], send_sem=send_sems.at[q], recv_sem=recv_sems.at[q],
            device_id=(x, y, 1 - c), device_id_type=MESH) for q in range(n)]

    def start(cins, couts, sems):
        for cp in copies(cins, couts, sems):
            cp.start()

    def finish(cins, couts, sems):
        cps = copies(cins, couts, sems)
        for cp in cps:
            cp.wait_recv()
        for cp in cps:
            cp.wait_send()

    return CommJob(grads, {}, [_sds((4,) + g.shape[2:], g.dtype) for g in grads],
                   [pltpu.SemaphoreType.DMA((n,)), pltpu.SemaphoreType.DMA((n,))], start, finish)


def chip_exchange_job(pieces):
    inputs, aliases, out_shapes, plan, where = [], {}, [], [], {}
    for partial, contrib, key, layer, row0, nrows, col0, cols in pieces:
        if key not in where:
            where[key] = len(out_shapes)
            out_shapes.append(_sds((4, DEPTH, partial.shape[1], cols), partial.dtype))
            if contrib is not None:
                aliases[len(inputs)] = where[key]
                inputs.append(contrib)
        plan.append((len(inputs), where[key], layer, row0, nrows, col0, partial.shape[2]))
        inputs.append(partial)
    n = len(plan)

    def copies(cins, couts, sems):
        send_sems, recv_sems, local_sems = sems
        x, y, c = _mesh_pos()
        mine = 2 * x + y
        local, sends, recvs = [], [], []
        for p, (i_in, i_out, layer, row0, nrows, col0, ncols) in enumerate(plan):
            rows, lanes = pl.ds(row0, nrows), pl.ds(col0, ncols)
            local.append(pltpu.make_async_copy(cins[i_in].at[mine, rows, :], couts[i_out].at[mine, layer, rows, lanes],
                                               local_sems.at[p]))
            for j, (cx, cy) in enumerate([(1 - x, y), (x, 1 - y), (1 - x, 1 - y)]):
                theirs = 2 * cx + cy

                def copy(slot_there, j=j, p=p, cx=cx, cy=cy, theirs=theirs, i_in=i_in, i_out=i_out, layer=layer,
                         rows=rows, lanes=lanes):
                    return pltpu.make_async_remote_copy(
                        src_ref=cins[i_in].at[theirs, rows, :], dst_ref=couts[i_out].at[slot_there, layer, rows, lanes],
                        send_sem=send_sems.at[j, p], recv_sem=recv_sems.at[j, p], device_id=(cx, cy, c), device_id_type=MESH)
                sends.append(copy(mine))
                recvs.append(copy(theirs))
        return local, sends, recvs

    def start(cins, couts, sems):
        local, sends, _ = copies(cins, couts, sems)
        for cp in local + sends:
            cp.start()

    def finish(cins, couts, sems):
        local, sends, recvs = copies(cins, couts, sems)
        for cp in recvs:
            cp.wait_recv()
        for cp in sends:
            cp.wait_send()
        for cp in local:
            cp.wait()

    sem_shapes = [pltpu.SemaphoreType.DMA((3, n)), pltpu.SemaphoreType.DMA((3, n)), pltpu.SemaphoreType.DMA((n,))]
    return CommJob(inputs, aliases, out_shapes, sem_shapes, start, finish)


def fwd_proj(x, g1, wt_in, l, comm=None):
    s = x.shape[0]
    tm = min(512, s)
    tn = 1408

    def body(x_ref, g_ref, w_ref, o_ref, xn_ref):
        @pl.when(pl.program_id(1) == 0)
        def _():
            xv = x_ref[...]
            r = lax.rsqrt(jnp.mean(xv * xv, axis=-1, keepdims=True) + EPS)
            xn_ref[...] = (xv * r * g_ref[l:l + 1, :]).astype(BF16)

        o_ref[...] = _dot(xn_ref[...], w_ref[...], _NT).astype(BF16)

    return _call(
        body, comm, (x, g1, wt_in), grid=(s // tm, IN_W // tn),
        in_specs=[pl.BlockSpec((tm, D), lambda i, j: (i, 0)),
                  pl.BlockSpec((DEPTH, D), lambda i, j: (0, 0)),
                  pl.BlockSpec((tn, D), lambda i, j: (j, 0))],
        out_specs=pl.BlockSpec((tm, tn), lambda i, j: (i, j)),
        out_shape=_sds((s, IN_W), BF16),
        scratch_shapes=[pltpu.VMEM((tm, D), BF16)], name=f"fwd_proj{l}")


def _scan_fwd(a_ref, u_ref, h_ref, h0, n_rows):
    row = lax.broadcasted_iota(jnp.int32, (8, BW), 0)

    def body(g, hprev):
        r = pl.multiple_of(g * 8, 8)
        a = a_ref[pl.ds(r, 8), :]
        u = u_ref[pl.ds(r, 8), :]
        for sft in (1, 2, 4):
            a_sh = jnp.where(row >= sft, pltpu.roll(a, sft, 0), 1.0)
            u_sh = jnp.where(row >= sft, pltpu.roll(u, sft, 0), 0.0)
            u = u + a * u_sh
            a = a * a_sh
        h = u + a * hprev
        h_ref[pl.ds(r, 8), :] = h
        return h[7:8, :]

    return lax.fori_loop(0, n_rows // 8, body, h0)


def _scan_bwd(b_ref, g_ref, o_ref, c0, n_rows):
    row = lax.broadcasted_iota(jnp.int32, (8, BW), 0)

    def body(k, cnext):
        r = pl.multiple_of((n_rows // 8 - 1 - k) * 8, 8)
        b = b_ref[pl.ds(r, 8), :]
        g = g_ref[pl.ds(r, 8), :]
        for sft in (1, 2, 4):
            b_sh = jnp.where(row < 8 - sft, pltpu.roll(b, 8 - sft, 0), 1.0)
            g_sh = jnp.where(row < 8 - sft, pltpu.roll(g, 8 - sft, 0), 0.0)
            g = g + b * g_sh
            b = b * b_sh
        o = g + b * cnext
        o_ref[pl.ds(r, 8), :] = o
        return o[0:1, :]

    return lax.fori_loop(0, n_rows // 8, body, c0)


def _shifted_copies(buf, shifted, n_rows):
    for r in range(1, 8):
        shifted[r - 1, 0:n_rows - 8, :] = buf[pl.ds(r, n_rows - 8), :]


def _window(buf, shifted, off, t):
    r = off % 8
    return buf[pl.ds(off, t), :] if r == 0 else shifted[r - 1, pl.ds(off - r, t), :]


def _branch_fwd_math(cur_ref, halo_ref, cw_ref, vec_ref, wx_ref, wa_ref, bufa, bufb, bufd, xd, first, t):
    def halo(c0):
        v = halo_ref[:, c0:c0 + BW].astype(F32)
        return jnp.where(first, 0.0, v)

    def cur(c0):
        return cur_ref[:, c0:c0 + BW].astype(F32)

    out = {}
    bufa[0:HALO, :] = halo(C_AX)
    bufa[HALO:HALO + t, :] = cur(C_AX)
    ca = jnp.zeros((t, BW), F32) + vec_ref[V_CAB:V_CAB + 1, :]
    for k in range(CONV_A):
        ca = ca + cw_ref[CW_A + k:CW_A + k + 1, :] * bufa[pl.ds(HALO - (CONV_A - 1) + k, t), :]
    gi = _sigmoid(_dot(ca, wx_ref[...], _NN) + vec_ref[V_BX:V_BX + 1, :])
    gr = _sigmoid(_dot(ca, wa_ref[...], _NN) + vec_ref[V_BA:V_BA + 1, :])
    sp = _softplus(-vec_ref[V_LAM:V_LAM + 1, :])
    la = -LRU_C * sp * gr
    a = jnp.exp(la)
    mult = jnp.sqrt(_neg_expm1(2.0 * la))
    out.update(ca=ca, gi=gi, gr=gr, sp=sp, a=a, mult=mult)
    bufb[0:HALO, :] = halo(C_BC) * halo(C_BV)
    bufb[HALO:HALO + t, :] = cur(C_BC) * cur(C_BV)
    cb = jnp.zeros((t, BW), F32)
    for k in range(CONV_B):
        cb = cb + cw_ref[CW_B + k:CW_B + k + 1, :] * bufb[pl.ds(HALO - (CONV_B - 1) + k, t), :]
    out.update(cb=cb)
    bufd[0:HALO, :] = halo(C_D1) * _sigmoid(halo(C_D2))
    s2 = _sigmoid(cur(C_D2))
    bufd[HALO:HALO + t, :] = cur(C_D1) * s2
    _shifted_copies(bufd, xd, t + HALO)
    cd = jnp.zeros((t, BW), F32) + vec_ref[V_CDB:V_CDB + 1, :]
    for k in range(CONV_D):
        cd = cd + cw_ref[CW_D + k:CW_D + k + 1, :] * _window(bufd, xd, HALO - (CONV_D - 1) + k, t)
    mu = jnp.mean(cd, axis=-1, keepdims=True)
    xc = cd - mu
    rstd = lax.rsqrt(jnp.mean(xc * xc, axis=-1, keepdims=True) + EPS)
    xh = xc * rstd
    ln = xh * vec_ref[V_LNG:V_LNG + 1, :] + vec_ref[V_LNB:V_LNB + 1, :]
    out.update(s2=s2, xh=xh, rstd=rstd, ln=ln)
    return out


def fwd_branch(proj, convw, vecs, wx_bd, wa_bd, l, comm=None):
    s = proj.shape[0]
    t = min(256, s)

    def body(cur_ref, halo_ref, cw_ref, vec_ref, wx_ref, wa_ref, pre_ref, h_ref, bufa, bufb, bufd, xd, a_s, u_s, hcar):
        first = pl.program_id(0) == 0

        @pl.when(first)
        def _():
            hcar[...] = jnp.zeros((1, BW), F32)

        v = _branch_fwd_math(cur_ref, halo_ref, cw_ref, vec_ref, wx_ref, wa_ref, bufa, bufb, bufd, xd, first, t)
        a_s[...] = v["a"]
        u_s[...] = v["ca"] * v["gi"] * v["mult"]
        hcar[...] = _scan_fwd(a_s, u_s, h_ref, hcar[...], t)
        gg, _ = _gelu_and_grad(cur_ref[:, C_AG:C_AG + BW].astype(F32))
        pre_ref[:, 0:BW] = (h_ref[...] * gg).astype(BF16)
        pre_ref[:, BW:2 * BW] = (cur_ref[:, C_BB:C_BB + BW].astype(F32) * v["cb"]).astype(BF16)
        ln = v["ln"]
        pre_ref[:, 2 * BW:3 * BW] = (ln * _sigmoid(ln)).astype(BF16)

    hb = t // HALO
    return _call(
        body, comm, (proj, proj, convw, vecs, wx_bd, wa_bd), grid=(s // t,),
        in_specs=[pl.BlockSpec((t, GL0), lambda i: (i, 0)),
                  pl.BlockSpec((HALO, GL0), lambda i: (jnp.maximum(i * hb - 1, 0), 0)),
                  pl.BlockSpec((None, CW_ROWS, BW), lambda i: (l, 0, 0)),
                  pl.BlockSpec((None, V_ROWS, BW), lambda i: (l, 0, 0)),
                  pl.BlockSpec((None, BW, BW), lambda i: (l, 0, 0)),
                  pl.BlockSpec((None, BW, BW), lambda i: (l, 0, 0))],
        out_specs=[pl.BlockSpec((t, 3 * BW), lambda i: (i, 0)), pl.BlockSpec((t, BW), lambda i: (i, 0))],
        out_shape=[_sds((s, 3 * BW), BF16), _sds((s, BW), F32)],
        scratch_shapes=[pltpu.VMEM((t + HALO, BW), F32)] * 3 + [pltpu.VMEM((7, t + HALO - 8, BW), F32)]
        + [pltpu.VMEM((t, BW), F32)] * 2 + [pltpu.VMEM((1, BW), F32)],
        name=f"fwd_branch{l}")


GRP = N_HEADS // N_KV


def _attn_mask_bias(first_block):
    shape = (GRP * ATT_BLK, 2 * ATT_BLK)
    qi = lax.broadcasted_iota(jnp.int32, shape, 0) & (ATT_BLK - 1)
    ki = lax.broadcasted_iota(jnp.int32, shape, 1)
    dist = qi + ATT_BLK - ki
    valid = (dist >= 0) & (dist < ATT_BLK) & (jnp.logical_not(first_block) | (ki >= ATT_BLK))
    return dist.astype(F32), valid


def _per_head(hk, values):
    hl = lax.broadcasted_iota(jnp.int32, (GRP * ATT_BLK, 1), 0) // ATT_BLK
    out = values[GRP - 1]
    for j in range(GRP - 2, -1, -1):
        out = jnp.where(hl == j, values[j], out)
    return out


def _attn_probs(q_ref, kvp_ref, kvc_ref, vec_ref, distf, valid):
    kvs = range(N_KV)
    heads = [range(hk * GRP, (hk + 1) * GRP) for hk in kvs]
    q4 = [jnp.concatenate([q_ref[:, h * HD:(h + 1) * HD] for h in heads[hk]], axis=0) for hk in kvs]
    k2 = [jnp.concatenate([kvp_ref[:, hk * HD:(hk + 1) * HD], kvc_ref[:, hk * HD:(hk + 1) * HD]], axis=0) for hk in kvs]
    v2 = [jnp.concatenate([kvp_ref[:, (N_KV + hk) * HD:(N_KV + hk + 1) * HD],
                           kvc_ref[:, (N_KV + hk) * HD:(N_KV + hk + 1) * HD]], axis=0) for hk in kvs]
    slope = [_per_head(hk, [2.0 ** (-8.0 * (h + 1) / N_HEADS) for h in heads[hk]]) for hk in kvs]
    sink = [_per_head(hk, [vec_ref[V_SINK:V_SINK + 1, h:h + 1] for h in heads[hk]]) for hk in kvs]
    sc = [_dot(q4[hk], k2[hk], _NT) for hk in kvs]
    sc = [jnp.where(valid, sc[hk] * (HD ** -0.5) - slope[hk] * distf, NEG_INF) for hk in kvs]
    m = [jnp.maximum(jnp.max(sc[hk], axis=-1, keepdims=True), sink[hk]) for hk in kvs]
    p = [jnp.exp(sc[hk] - m[hk]) for hk in kvs]
    es = [jnp.exp(sink[hk] - m[hk]) for hk in kvs]
    inv = [1.0 / (jnp.sum(p[hk], axis=-1, keepdims=True) + es[hk]) for hk in kvs]
    return [(q4[hk], k2[hk], v2[hk], p[hk] * inv[hk], es[hk] * inv[hk]) for hk in kvs]


def fwd_attn(proj, vecs, l, comm=None):
    s = proj.shape[0]
    nb = s // ATT_BLK

    def body(q_ref, kvp_ref, kvc_ref, vec_ref, o_ref):
        distf, valid = _attn_mask_bias(pl.program_id(0) == 0)
        groups = _attn_probs(q_ref, kvp_ref, kvc_ref, vec_ref, distf, valid)
        outs = [_dot(p, v2, _NN).astype(BF16) for _, _, v2, p, _ in groups]
        for hk, out in enumerate(outs):
            for j in range(GRP):
                h = hk * GRP + j
                o_ref[:, h * HD:(h + 1) * HD] = out[j * ATT_BLK:(j + 1) * ATT_BLK]

    return _call(
        body, comm, (proj, proj, proj, vecs), grid=(nb,),
        in_specs=[pl.BlockSpec((ATT_BLK, BW), lambda i: (i, C_Q // BW)),
                  pl.BlockSpec((ATT_BLK, 256), lambda i: (jnp.maximum(i - 1, 0), C_K // 256)),
                  pl.BlockSpec((ATT_BLK, 256), lambda i: (i, C_K // 256)),
                  pl.BlockSpec((None, V_ROWS, BW), lambda i: (l, 0, 0))],
        out_specs=pl.BlockSpec((ATT_BLK, BW), lambda i: (i, 0)),
        out_shape=_sds((s, BW), BF16), name=f"fwd_attn{l}")


def fwd_merge(x, proj, pre_abd, pre_c, wt_a, wt_b, wt_c, wt_d, w_o, l, comm=None):
    s = x.shape[0]
    tm = min(256, s)

    def body(x_ref, gl_ref, pabd_ref, pc_ref, wa_ref, wb_ref, wc_ref, wd_ref, wo_ref, y_ref, mg_ref, h1_ref):
        pres = (pabd_ref[:, 0:BW], pabd_ref[:, BW:2 * BW], pc_ref[...], pabd_ref[:, 2 * BW:3 * BW])
        merged = jnp.zeros((tm, D), F32)
        for k, (pre, w_ref) in enumerate(zip(pres, (wa_ref, wb_ref, wc_ref, wd_ref))):
            yk = _dot(pre, w_ref[...], _NT)
            y_ref[:, k * D:(k + 1) * D] = yk.astype(BF16)
            merged = merged + _sigmoid(gl_ref[:, k * D:(k + 1) * D].astype(F32)) * yk
        mg_ref[...] = merged.astype(BF16)
        h1_ref[...] = x_ref[...] + _dot(merged, wo_ref[...], _NN)

    wspec = pl.BlockSpec((D, BW), lambda i: (0, 0))
    return _call(
        body, comm, (x, proj, pre_abd, pre_c, wt_a, wt_b, wt_c, wt_d, w_o), grid=(s // tm,),
        in_specs=[pl.BlockSpec((tm, D), lambda i: (i, 0)),
                  pl.BlockSpec((E(tm), E(4 * D)), lambda i: (i * tm, GL0)),
                  pl.BlockSpec((tm, 3 * BW), lambda i: (i, 0)),
                  pl.BlockSpec((tm, BW), lambda i: (i, 0)),
                  wspec, wspec, wspec, wspec,
                  pl.BlockSpec((D, D), lambda i: (0, 0))],
        out_specs=[pl.BlockSpec((tm, 4 * D), lambda i: (i, 0)), pl.BlockSpec((tm, D), lambda i: (i, 0)),
                   pl.BlockSpec((tm, D), lambda i: (i, 0))],
        out_shape=[_sds((s, 4 * D), BF16), _sds((s, D), BF16), _sds((s, D), F32)], name=f"fwd_merge{l}")


def fwd_ffn(h1, g2, wt_gate, wt_up, w_down, l, comm=None):
    s = h1.shape[0]
    tm = min(512, s)
    fc = FF // 2

    def body(h_ref, g_ref, wg_ref, wu_ref, wd_ref, xo_ref, fg_ref, fu_ref, hn_ref, acc_ref):
        j = pl.program_id(1)

        @pl.when(j == 0)
        def _():
            hv = h_ref[...]
            r = lax.rsqrt(jnp.mean(hv * hv, axis=-1, keepdims=True) + EPS)
            hn_ref[...] = (hv * r * g_ref[l:l + 1, :]).astype(BF16)
            acc_ref[...] = hv

        fg = _dot(hn_ref[...], wg_ref[...], _NT)
        fu = _dot(hn_ref[...], wu_ref[...], _NT)
        fg_ref[...] = fg.astype(BF16)
        fu_ref[...] = fu.astype(BF16)
        acc_ref[...] += _dot(fg * _sigmoid(fg) * fu, wd_ref[...], _NN)

        @pl.when(j == pl.num_programs(1) - 1)
        def _():
            xo_ref[...] = acc_ref[...]

    wspec = pl.BlockSpec((fc, D), lambda i, j: (j, 0))
    return _call(
        body, comm, (h1, g2, wt_gate, wt_up, w_down), grid=(s // tm, FF // fc),
        in_specs=[pl.BlockSpec((tm, D), lambda i, j: (i, 0)), pl.BlockSpec((DEPTH, D), lambda i, j: (0, 0)),
                  wspec, wspec, wspec],
        out_specs=[pl.BlockSpec((tm, D), lambda i, j: (i, 0)), pl.BlockSpec((tm, fc), lambda i, j: (i, j)),
                   pl.BlockSpec((tm, fc), lambda i, j: (i, j))],
        out_shape=[_sds((s, D), F32), _sds((s, FF), BF16), _sds((s, FF), BF16)],
        scratch_shapes=[pltpu.VMEM((tm, D), BF16), pltpu.VMEM((tm, D), F32)], name=f"fwd_ffn{l}")


def loss_head(x, gf, target):
    s = x.shape[0]
    tm = min(512, s)

    def body(x_ref, g_ref, t_ref, dx_ref, st_ref):
        @pl.when(pl.program_id(0) == 0)
        def _():
            st_ref[...] = jnp.zeros((8, D), F32)

        xv = x_ref[...]
        g = g_ref[...]
        r = lax.rsqrt(jnp.mean(xv * xv, axis=-1, keepdims=True) + EPS)
        n = xv * r
        err = n * g - t_ref[...]
        dy = err * (1.0 / D)
        dn = dy * g
        dx_ref[...] = r * (dn - n * jnp.mean(dn * n, axis=-1, keepdims=True))
        st_ref[0:1, :] += jnp.sum(dy * n, axis=0, keepdims=True)
        lsum = 0.5 * jnp.sum(jnp.mean(err * err, axis=-1, keepdims=True), axis=0, keepdims=True)
        st_ref[1:2, :] += jnp.broadcast_to(lsum, (1, D))

    return pl.pallas_call(
        body, grid=(s // tm,),
        in_specs=[pl.BlockSpec((tm, D), lambda i: (i, 0)), pl.BlockSpec((1, D), lambda i: (0, 0)),
                  pl.BlockSpec((tm, D), lambda i: (i, 0))],
        out_specs=[pl.BlockSpec((tm, D), lambda i: (i, 0)), pl.BlockSpec((8, D), lambda i: (0, 0))],
        out_shape=[_sds((s, D), F32), _sds((8, D), F32)],
        compiler_params=_cparams(1), name="loss_head")(x, gf, target)


def _edge_index(j, i, n_j, n_i):
    return jnp.where((j == 0) | (j == n_j - 1), i, n_i - 1)


def bwd_ffn(dxo, h1, fg, fu, g2, wt_gate, wt_up, w_down, l, comm=None):
    s = h1.shape[0]
    tm = min(512, s)
    fc = 256
    n_j, n_i = FF // fc, s // tm

    def body(dxo_ref, h_ref, fg_ref, fu_ref, g_ref, wg_ref, wu_ref, wd_ref,
             dh_ref, dwg_ref, dwu_ref, dwd_ref, st_ref, dhn, dxo_b, hn_b, ag, au, ad):
        j, i = pl.program_id(0), pl.program_id(1)
        rows = pl.ds(pl.multiple_of(i * tm, tm), tm)
        g = g_ref[l:l + 1, :]

        @pl.when(j == 0)
        def _():
            hv = h_ref[...]
            r = lax.rsqrt(jnp.mean(hv * hv, axis=-1, keepdims=True) + EPS)
            hn_b[rows, :] = (hv * r * g).astype(BF16)
            dxo_b[rows, :] = dxo_ref[...].astype(BF16)
            dhn[rows, :] = jnp.zeros((tm, D), F32)

        @pl.when((j == 0) & (i == 0))
        def _():
            st_ref[...] = jnp.zeros((8, D), F32)

        @pl.when(i == 0)
        def _():
            ag[...] = jnp.zeros((fc, D), F32)
            au[...] = jnp.zeros((fc, D), F32)
            ad[...] = jnp.zeros((fc, D), F32)

        fgv = fg_ref[...].astype(F32)
        fuv = fu_ref[...].astype(F32)
        sg = _sigmoid(fgv)
        sil = fgv * sg
        dxb = dxo_b[rows, :]
        hnb = hn_b[rows, :]
        d_act = _dot(dxb, wd_ref[...], _NT)
        ad[...] += _dot(sil * fuv, dxb, _TN)
        d_fg = (d_act * fuv * (sg * (1.0 + fgv * (1.0 - sg)))).astype(BF16)
        d_fu = (d_act * sil).astype(BF16)
        ag[...] += _dot(d_fg, hnb, _TN)
        au[...] += _dot(d_fu, hnb, _TN)
        dhn[rows, :] += _dot(d_fg, wg_ref[...], _NN) + _dot(d_fu, wu_ref[...], _NN)

        @pl.when(i == n_i - 1)
        def _():
            dwg_ref[...] = ag[...].astype(BF16)
            dwu_ref[...] = au[...].astype(BF16)
            dwd_ref[...] = ad[...].astype(BF16)

        @pl.when(j == n_j - 1)
        def _():
            hv = h_ref[...]
            r = lax.rsqrt(jnp.mean(hv * hv, axis=-1, keepdims=True) + EPS)
            n = hv * r
            dv = dhn[rows, :]
            dn = dv * g
            dh_ref[...] = dxo_ref[...] + r * (dn - n * jnp.mean(dn * n, axis=-1, keepdims=True))
            st_ref[0:1, :] += jnp.sum(dv * n, axis=0, keepdims=True)

    edge = lambda j, i: (_edge_index(j, i, n_j, n_i), 0)
    wspec = pl.BlockSpec((fc, D), lambda j, i: (j, 0))
    dwspec = pl.BlockSpec((fc, D), lambda j, i: (j, 0))
    return _call(
        body, comm, (dxo, h1, fg, fu, g2, wt_gate, wt_up, w_down), grid=(n_j, n_i),
        in_specs=[pl.BlockSpec((tm, D), edge),
                  pl.BlockSpec((tm, D), edge),
                  pl.BlockSpec((tm, fc), lambda j, i: (i, j)), pl.BlockSpec((tm, fc), lambda j, i: (i, j)),
                  pl.BlockSpec((DEPTH, D), lambda j, i: (0, 0)), wspec, wspec, wspec],
        out_specs=[pl.BlockSpec((tm, D), lambda j, i: (jnp.where(j == n_j - 1, i, 0), 0)),
                   dwspec, dwspec, dwspec, pl.BlockSpec((8, D), lambda j, i: (0, 0))],
        out_shape=[_sds((s, D), F32), _sds((FF, D), BF16), _sds((FF, D), BF16), _sds((FF, D), BF16), _sds((8, D), F32)],
        scratch_shapes=[pltpu.VMEM((s, D), F32), pltpu.VMEM((s, D), BF16), pltpu.VMEM((s, D), BF16),
                        pltpu.VMEM((fc, D), F32), pltpu.VMEM((fc, D), F32), pltpu.VMEM((fc, D), F32)],
        name=f"bwd_ffn{l}")


def bwd_merge(dh1, y4, proj, merged, pre_abd, pre_c, wt_a, wt_b, wt_c, wt_d, w_o, l, comm=None):
    s = dh1.shape[0]
    tm = min(256, s)
    n_i = s // tm

    def body(dh_ref, y_ref, gl_ref, mg_ref, pabd_ref, pc_ref, wa_ref, wb_ref, wc_ref, wd_ref, wo_ref,
             dgl_ref, dpre_ref, dwo_ref, dwa_ref, dwb_ref, dwc_ref, dwd_ref, ao, aa, ab, ac, ad):
        i = pl.program_id(0)
        accs = (aa, ab, ac, ad)

        @pl.when(i == 0)
        def _():
            ao[...] = jnp.zeros((D, D), F32)
            for acc in accs:
                acc[...] = jnp.zeros((D, BW), F32)

        dhb = dh_ref[...].astype(BF16)
        dmg = _dot(dhb, wo_ref[...], _NT)
        ao[...] += _dot(mg_ref[...], dhb, _TN)
        pres = (pabd_ref[:, 0:BW], pabd_ref[:, BW:2 * BW], pc_ref[...], pabd_ref[:, 2 * BW:3 * BW])
        for k, (pre, w_ref, acc) in enumerate(zip(pres, (wa_ref, wb_ref, wc_ref, wd_ref), accs)):
            gk = _sigmoid(gl_ref[:, k * D:(k + 1) * D].astype(F32))
            yk = y_ref[:, k * D:(k + 1) * D].astype(F32)
            dgl_ref[:, k * D:(k + 1) * D] = (dmg * yk * gk * (1.0 - gk)).astype(BF16)
            dyk = (dmg * gk).astype(BF16)
            dpre_ref[:, k * BW:(k + 1) * BW] = _dot(dyk, w_ref[...], _NN).astype(BF16)
            acc[...] += _dot(dyk, pre, _TN)

        @pl.when(i == n_i - 1)
        def _():
            dwo_ref[...] = ao[...].astype(BF16)
            for o_ref, acc in zip((dwa_ref, dwb_ref, dwc_ref, dwd_ref), accs):
                o_ref[...] = acc[...].astype(BF16)

    wspec = pl.BlockSpec((D, BW), lambda i: (0, 0))
    dwspec = pl.BlockSpec((D, BW), lambda i: (0, 0))
    return _call(
        body, comm, (dh1, y4, proj, merged, pre_abd, pre_c, wt_a, wt_b, wt_c, wt_d, w_o), grid=(n_i,),
        in_specs=[pl.BlockSpec((tm, D), lambda i: (i, 0)),
                  pl.BlockSpec((tm, 4 * D), lambda i: (i, 0)),
                  pl.BlockSpec((E(tm), E(4 * D)), lambda i: (i * tm, GL0)),
                  pl.BlockSpec((tm, D), lambda i: (i, 0)),
                  pl.BlockSpec((tm, 3 * BW), lambda i: (i, 0)),
                  pl.BlockSpec((tm, BW), lambda i: (i, 0)),
                  wspec, wspec, wspec, wspec,
                  pl.BlockSpec((D, D), lambda i: (0, 0))],
        out_specs=[pl.BlockSpec((E(tm), E(4 * D)), lambda i: (i * tm, GL0)),
                   pl.BlockSpec((tm, 4 * BW), lambda i: (i, 0)),
                   pl.BlockSpec((D, D), lambda i: (0, 0)), dwspec, dwspec, dwspec, dwspec],
        out_shape=[_sds((s, IN_W), BF16), _sds((s, 4 * BW), BF16), _sds((D, D), BF16)] + [_sds((D, BW), BF16)] * 4,
        scratch_shapes=[pltpu.VMEM((D, D), F32)] + [pltpu.VMEM((D, BW), F32)] * 4, name=f"bwd_merge{l}")


def bwd_attn(proj, dpre, vecs, l, comm=None):
    s = proj.shape[0]
    nb = s // ATT_BLK
    grp = N_HEADS // N_KV

    def body(q_ref, kvp_ref, kvc_ref, do_ref, vec_ref, dq_ref, dkc_ref, dkp_ref, st_ref):
        @pl.when(pl.program_id(0) == 0)
        def _():
            st_ref[...] = jnp.zeros((8, 128), F32)

        distf, valid = _attn_mask_bias(pl.program_id(0) == 0)
        lane = lax.broadcasted_iota(jnp.int32, (1, 128), 1)
        dsink = jnp.zeros((1, 128), F32)
        groups = _attn_probs(q_ref, kvp_ref, kvc_ref, vec_ref, distf, valid)
        kvs = range(N_KV)
        do4s = [jnp.concatenate([do_ref[:, h * HD:(h + 1) * HD] for h in range(hk * grp, (hk + 1) * grp)], axis=0) for hk in kvs]
        dps = [_dot(do4s[hk], groups[hk][2], _NT) for hk in kvs]
        deltas = [jnp.sum(groups[hk][3] * dps[hk], axis=-1, keepdims=True) for hk in kvs]
        dss = [groups[hk][3] * (dps[hk] - deltas[hk]) * (HD ** -0.5) for hk in kvs]
        for hk in kvs:
            q4, k2, v2, p, ps = groups[hk]
            do4, delta, ds = do4s[hk], deltas[hk], dss[hk]
            dq4 = _dot(ds, k2, _NN).astype(BF16)
            dk2 = _dot(ds, q4, _TN)
            dv2 = _dot(p, do4, _TN)
            psd = ps * delta
            for j in range(grp):
                h = hk * grp + j
                rows = slice(j * ATT_BLK, (j + 1) * ATT_BLK)
                dq_ref[:, h * HD:(h + 1) * HD] = dq4[rows]
                dsink = dsink + jnp.where(lane == h, -jnp.sum(psd[rows], axis=0, keepdims=True), 0.0)
            dkp_ref[:, hk * HD:(hk + 1) * HD] = dk2[0:ATT_BLK].astype(BF16)
            dkc_ref[:, hk * HD:(hk + 1) * HD] = dk2[ATT_BLK:].astype(BF16)
            dkp_ref[:, (N_KV + hk) * HD:(N_KV + hk + 1) * HD] = dv2[0:ATT_BLK].astype(BF16)
            dkc_ref[:, (N_KV + hk) * HD:(N_KV + hk + 1) * HD] = dv2[ATT_BLK:].astype(BF16)
        st_ref[0:1, :] += dsink

    return _call(
        body, comm, (proj, proj, proj, dpre, vecs), grid=(nb,),
        in_specs=[pl.BlockSpec((ATT_BLK, BW), lambda i: (i, C_Q // BW)),
                  pl.BlockSpec((ATT_BLK, 256), lambda i: (jnp.maximum(i - 1, 0), C_K // 256)),
                  pl.BlockSpec((ATT_BLK, 256), lambda i: (i, C_K // 256)),
                  pl.BlockSpec((ATT_BLK, BW), lambda i: (i, 2)),
                  pl.BlockSpec((None, V_ROWS, BW), lambda i: (l, 0, 0))],
        out_specs=[pl.BlockSpec((ATT_BLK, BW), lambda i: (i, 0)), pl.BlockSpec((ATT_BLK, 256), lambda i: (i, 0)),
                   pl.BlockSpec((ATT_BLK, 256), lambda i: (i, 0)), pl.BlockSpec((8, 128), lambda i: (0, 0))],
        out_shape=[_sds((s, BW), BF16), _sds((s, 256), BF16), _sds((s, 256), BF16), _sds((8, 128), F32)],
        name=f"bwd_attn{l}")


def bwd_branch(proj, dproj, dpre, h, dq, dkc, dkp, convw, vecs, wx_bd, wa_bd, l, comm=None):
    s = proj.shape[0]
    t = 2 * ATT_BLK
    nt = s // t
    nb = s // ATT_BLK
    hb = t // HALO

    def body(cur_ref, halo_ref, dpre_ref, h_ref, hp_ref, dq_ref, dkc_ref, dkp1_ref, dkp2_ref,
             cw_ref, vec_ref, wx_ref, wa_ref, dproj_in, dp_ref, dcw_ref, dvec_ref, dwx_ref, dwa_ref,
             bufa, bufb, bufd, xd, xg, a_ext, hbuf, b_s, g_s, dh_s, ga, gb, gd, dhcar):
        del dproj_in
        step = pl.program_id(0)
        ti = nt - 1 - step
        first = ti == 0

        @pl.when(step == 0)
        def _():
            dcw_ref[...] = jnp.zeros((CW_ROWS, BW), F32)
            dvec_ref[...] = jnp.zeros((V_ROWS, BW), F32)
            dwx_ref[...] = jnp.zeros((BW, BW), F32)
            dwa_ref[...] = jnp.zeros((BW, BW), F32)
            dhcar[...] = jnp.zeros((1, BW), F32)
            a_ext[t:t + 8, :] = jnp.zeros((8, BW), F32)
            ga[t:t + 8, :] = jnp.zeros((8, BW), F32)
            gb[t:t + 8, :] = jnp.zeros((8, BW), F32)
            gd[t:t + HALO, :] = jnp.zeros((HALO, BW), F32)

        def cur(c0):
            return cur_ref[:, c0:c0 + BW].astype(F32)

        def rsum(v):
            return jnp.sum(v, axis=0, keepdims=True)

        def put(c0, v):
            dp_ref[:, c0:c0 + BW] = v.astype(BF16)

        v = _branch_fwd_math(cur_ref, halo_ref, cw_ref, vec_ref, wx_ref, wa_ref, bufa, bufb, bufd, xd, first, t)
        ca, gi, gr, sp, a, mult = v["ca"], v["gi"], v["gr"], v["sp"], v["a"], v["mult"]
        dpa = dpre_ref[:, 0:BW].astype(F32)
        gg, dgg = _gelu_and_grad(cur(C_AG))
        hv = h_ref[...]
        put(C_AG, dpa * hv * dgg)
        a_ext[0:t, :] = a
        b_s[...] = a_ext[pl.ds(1, t), :]
        g_s[...] = dpa * gg
        dhcar[...] = _scan_bwd(b_s, g_s, dh_s, dhcar[...], t)
        a_ext[t:t + 1, :] = a[0:1, :]
        dh = dh_s[...]
        hbuf[0:8, :] = jnp.where(first, 0.0, hp_ref[...])
        hbuf[8:8 + t, :] = hv
        da = dh * hbuf[pl.ds(7, t), :]
        d_ca = dh * gi * mult
        d_gi = dh * ca * mult
        d_mult = dh * ca * gi
        d_la = da * a - d_mult * (a * a) / mult
        lam = vec_ref[V_LAM:V_LAM + 1, :]
        dvec_ref[V_LAM:V_LAM + 1, :] += rsum(d_la * gr) * (LRU_C * _sigmoid(-lam))
        d_gr = d_la * (-LRU_C * sp)
        d_zr = d_gr * gr * (1.0 - gr)
        d_zi = d_gi * gi * (1.0 - gi)
        dvec_ref[V_BA:V_BA + 1, :] += rsum(d_zr)
        dvec_ref[V_BX:V_BX + 1, :] += rsum(d_zi)
        dwa_ref[...] += _dot(ca, d_zr, _TN)
        dwx_ref[...] += _dot(ca, d_zi, _TN)
        d_ca = d_ca + _dot(d_zi, wx_ref[...], _NT) + _dot(d_zr, wa_ref[...], _NT)
        dvec_ref[V_CAB:V_CAB + 1, :] += rsum(d_ca)
        ga[0:t, :] = d_ca
        d_ax = jnp.zeros((t, BW), F32)
        for k in range(CONV_A):
            d_ax = d_ax + cw_ref[CW_A + k:CW_A + k + 1, :] * ga[pl.ds(CONV_A - 1 - k, t), :]
            dcw_ref[CW_A + k:CW_A + k + 1, :] += rsum(d_ca * bufa[pl.ds(HALO - (CONV_A - 1) + k, t), :])
        ga[t:t + 8, :] = d_ca[0:8, :]
        put(C_AX, d_ax)
        dpb = dpre_ref[:, BW:2 * BW].astype(F32)
        put(C_BB, dpb * v["cb"])
        d_cb = dpb * cur(C_BB)
        gb[0:t, :] = d_cb
        d_cbin = jnp.zeros((t, BW), F32)
        for k in range(CONV_B):
            d_cbin = d_cbin + cw_ref[CW_B + k:CW_B + k + 1, :] * gb[pl.ds(CONV_B - 1 - k, t), :]
            dcw_ref[CW_B + k:CW_B + k + 1, :] += rsum(d_cb * bufb[pl.ds(HALO - (CONV_B - 1) + k, t), :])
        gb[t:t + 8, :] = d_cb[0:8, :]
        put(C_BC, d_cbin * cur(C_BV))
        put(C_BV, d_cbin * cur(C_BC))
        dpd = dpre_ref[:, 3 * BW:4 * BW].astype(F32)
        ln, xh, rstd, s2 = v["ln"], v["xh"], v["rstd"], v["s2"]
        sg = _sigmoid(ln)
        d_ln = dpd * sg * (1.0 + ln * (1.0 - sg))
        dvec_ref[V_LNG:V_LNG + 1, :] += rsum(d_ln * xh)
        dvec_ref[V_LNB:V_LNB + 1, :] += rsum(d_ln)
        d_xh = d_ln * vec_ref[V_LNG:V_LNG + 1, :]
        d_cd = rstd * (d_xh - jnp.mean(d_xh, axis=-1, keepdims=True)
                       - xh * jnp.mean(d_xh * xh, axis=-1, keepdims=True))
        dvec_ref[V_CDB:V_CDB + 1, :] += rsum(d_cd)
        gd[0:t, :] = d_cd
        _shifted_copies(gd, xg, t + HALO)
        d_dg = jnp.zeros((t, BW), F32)
        for k in range(CONV_D):
            d_dg = d_dg + cw_ref[CW_D + k:CW_D + k + 1, :] * _window(gd, xg, CONV_D - 1 - k, t)
            dcw_ref[CW_D + k:CW_D + k + 1, :] += rsum(d_cd * _window(bufd, xd, HALO - (CONV_D - 1) + k, t))
        gd[t:t + HALO, :] = d_cd[0:HALO, :]
        put(C_D1, d_dg * s2)
        put(C_D2, d_dg * cur(C_D1) * s2 * (1.0 - s2))
        dp_ref[:, C_Q:C_Q + BW] = dq_ref[...]
        dkp2 = jnp.where(step == 0, 0.0, dkp2_ref[...].astype(F32))
        dp_ref[0:ATT_BLK, C_K:C_K + 256] = (dkc_ref[0:ATT_BLK, :].astype(F32) + dkp1_ref[...].astype(F32)).astype(BF16)
        dp_ref[ATT_BLK:t, C_K:C_K + 256] = (dkc_ref[ATT_BLK:t, :].astype(F32) + dkp2).astype(BF16)

    rev = lambda i: nt - 1 - i
    full = lambda r, c: pl.BlockSpec((r, c), lambda i: (0, 0))
    return _call(
        body, comm, (proj, proj, dpre, h, h, dq, dkc, dkp, dkp, convw, vecs, wx_bd, wa_bd, dproj), grid=(nt,),
        in_specs=[pl.BlockSpec((t, GL0), lambda i: (rev(i), 0)),
                  pl.BlockSpec((HALO, GL0), lambda i: (jnp.maximum(rev(i) * hb - 1, 0), 0)),
                  pl.BlockSpec((t, 4 * BW), lambda i: (rev(i), 0)),
                  pl.BlockSpec((t, BW), lambda i: (rev(i), 0)),
                  pl.BlockSpec((8, BW), lambda i: (jnp.maximum(rev(i) * (t // 8) - 1, 0), 0)),
                  pl.BlockSpec((t, BW), lambda i: (rev(i), 0)),
                  pl.BlockSpec((t, 256), lambda i: (rev(i), 0)),
                  pl.BlockSpec((ATT_BLK, 256), lambda i: (2 * rev(i) + 1, 0)),
                  pl.BlockSpec((ATT_BLK, 256), lambda i: (jnp.minimum(2 * rev(i) + 2, nb - 1), 0)),
                  pl.BlockSpec((None, CW_ROWS, BW), lambda i: (l, 0, 0)),
                  pl.BlockSpec((None, V_ROWS, BW), lambda i: (l, 0, 0)),
                  pl.BlockSpec((None, BW, BW), lambda i: (l, 0, 0)),
                  pl.BlockSpec((None, BW, BW), lambda i: (l, 0, 0)),
                  pl.BlockSpec(memory_space=pl.ANY)],
        out_specs=[pl.BlockSpec((t, GL0), lambda i: (rev(i), 0)),
                   full(CW_ROWS, BW), full(V_ROWS, BW), full(BW, BW), full(BW, BW)],
        out_shape=[_sds((s, IN_W), BF16), _sds((CW_ROWS, BW), F32), _sds((V_ROWS, BW), F32),
                   _sds((BW, BW), F32), _sds((BW, BW), F32)],
        scratch_shapes=[pltpu.VMEM((t + HALO, BW), F32)] * 3 + [pltpu.VMEM((7, t + HALO - 8, BW), F32)] * 2
        + [pltpu.VMEM((t + 8, BW), F32), pltpu.VMEM((t + 8, BW), F32)]
        + [pltpu.VMEM((t, BW), F32)] * 3
        + [pltpu.VMEM((t + 8, BW), F32), pltpu.VMEM((t + 8, BW), F32), pltpu.VMEM((t + HALO, BW), F32),
           pltpu.VMEM((1, BW), F32)],
        aliases={13: 0}, name=f"bwd_branch{l}")


def bwd_proj(dproj, x, dh1, g1, wt_in, l, comm=None):
    s = x.shape[0]
    tm = min(512, s)
    ck = 1408
    n_j, n_i = IN_W // ck, s // tm

    def body(dp_ref, x_ref, dh_ref, g_ref, w_ref, dx_ref, dw_ref, st_ref, dxn, xn_b, acc):
        j, i = pl.program_id(0), pl.program_id(1)
        rows = pl.ds(pl.multiple_of(i * tm, tm), tm)
        g = g_ref[l:l + 1, :]

        @pl.when(j == 0)
        def _():
            xv = x_ref[...]
            r = lax.rsqrt(jnp.mean(xv * xv, axis=-1, keepdims=True) + EPS)
            xn_b[rows, :] = (xv * r * g).astype(BF16)
            dxn[rows, :] = jnp.zeros((tm, D), F32)

        @pl.when((j == 0) & (i == 0))
        def _():
            st_ref[...] = jnp.zeros((8, D), F32)

        @pl.when(i == 0)
        def _():
            acc[...] = jnp.zeros((ck, D), F32)

        dp = dp_ref[...]
        dxn[rows, :] += _dot(dp, w_ref[...], _NN)
        acc[...] += _dot(dp, xn_b[rows, :], _TN)

        @pl.when(i == n_i - 1)
        def _():
            dw_ref[...] = acc[...].astype(BF16)

        @pl.when(j == n_j - 1)
        def _():
            xv = x_ref[...]
            r = lax.rsqrt(jnp.mean(xv * xv, axis=-1, keepdims=True) + EPS)
            n = xv * r
            dv = dxn[rows, :]
            dn = dv * g
            dx_ref[...] = dh_ref[...] + r * (dn - n * jnp.mean(dn * n, axis=-1, keepdims=True))
            st_ref[0:1, :] += jnp.sum(dv * n, axis=0, keepdims=True)

    lastrow = lambda j, i: (jnp.where(j == n_j - 1, i, 0), 0)
    return _call(
        body, comm, (dproj, x, dh1, g1, wt_in), grid=(n_j, n_i),
        in_specs=[pl.BlockSpec((tm, ck), lambda j, i: (i, j)),
                  pl.BlockSpec((tm, D), lambda j, i: (_edge_index(j, i, n_j, n_i), 0)),
                  pl.BlockSpec((tm, D), lastrow),
                  pl.BlockSpec((DEPTH, D), lambda j, i: (0, 0)),
                  pl.BlockSpec((ck, D), lambda j, i: (j, 0))],
        out_specs=[pl.BlockSpec((tm, D), lastrow), pl.BlockSpec((ck, D), lambda j, i: (j, 0)),
                   pl.BlockSpec((8, D), lambda j, i: (0, 0))],
        out_shape=[_sds((s, D), F32), _sds((IN_W, D), BF16), _sds((8, D), F32)],
        scratch_shapes=[pltpu.VMEM((s, D), F32), pltpu.VMEM((s, D), BF16), pltpu.VMEM((ck, D), F32)],
        name=f"bwd_proj{l}")


def bwd_proj_w(dproj, x, g1, l, half, comm=None):
    s = x.shape[0]
    tm = min(1024, s)
    ck = 1408
    c0, hw = W_IN_PARTS[half]
    n_j, n_i = IN_W // ck, s // tm

    def body(dp_ref, x_ref, g_ref, dw_ref, xn_b, acc):
        j, i = pl.program_id(0), pl.program_id(1)
        rows = pl.ds(pl.multiple_of(i * tm, tm), tm)

        @pl.when(j == 0)
        def _():
            xv = x_ref[...]
            r = lax.rsqrt(jnp.mean(xv * xv, axis=-1, keepdims=True) + EPS)
            xn_b[rows, :] = (xv * r * g_ref[l:l + 1, :])[:, c0:c0 + hw].astype(BF16)

        @pl.when(i == 0)
        def _():
            acc[...] = jnp.zeros((ck, hw), F32)

        acc[...] += _dot(dp_ref[...], xn_b[rows, :], _TN)

        @pl.when(i == n_i - 1)
        def _():
            dw_ref[...] = acc[...].astype(BF16)

    return _call(
        body, comm, (dproj, x, g1), grid=(n_j, n_i),
        in_specs=[pl.BlockSpec((tm, ck), lambda j, i: (i, j)),
                  pl.BlockSpec((tm, D), lambda j, i: (jnp.where(j == 0, i, n_i - 1), 0)),
                  pl.BlockSpec((DEPTH, D), lambda j, i: (0, 0))],
        out_specs=pl.BlockSpec((ck, hw), lambda j, i: (j, 0)),
        out_shape=_sds((IN_W, hw), BF16),
        scratch_shapes=[pltpu.VMEM((s, hw), BF16), pltpu.VMEM((ck, hw), F32)],
        name=f"bwd_proj_w{half}_{l}")


def bwd_proj_x(dproj, x, dh1, g1, wt_in, l, comm=None):
    s = x.shape[0]
    tm = min(512, s)
    ck = 1408
    n_j, n_i = IN_W // ck, s // tm

    def body(dp_ref, x_ref, dh_ref, g_ref, w_ref, dx_ref, st_ref, dxn):
        j, i = pl.program_id(0), pl.program_id(1)
        rows = pl.ds(pl.multiple_of(i * tm, tm), tm)
        g = g_ref[l:l + 1, :]

        @pl.when((j == 0) & (i == 0))
        def _():
            st_ref[...] = jnp.zeros((8, D), F32)

        part = _dot(dp_ref[...], w_ref[...], _NN)

        @pl.when(j == 0)
        def _():
            dxn[rows, :] = part

        @pl.when(j > 0)
        def _():
            dxn[rows, :] += part

        @pl.when(j == n_j - 1)
        def _():
            xv = x_ref[...]
            r = lax.rsqrt(jnp.mean(xv * xv, axis=-1, keepdims=True) + EPS)
            n = xv * r
            dv = dxn[rows, :]
            dn = dv * g
            dx_ref[...] = dh_ref[...] + r * (dn - n * jnp.mean(dn * n, axis=-1, keepdims=True))
            st_ref[0:1, :] += jnp.sum(dv * n, axis=0, keepdims=True)

    lastrow = lambda j, i: (jnp.where(j == n_j - 1, i, 0), 0)
    return _call(
        body, comm, (dproj, x, dh1, g1, wt_in), grid=(n_j, n_i),
        in_specs=[pl.BlockSpec((tm, ck), lambda j, i: (i, j)), pl.BlockSpec((tm, D), lastrow),
                  pl.BlockSpec((tm, D), lastrow),
                  pl.BlockSpec((DEPTH, D), lambda j, i: (0, 0)), pl.BlockSpec((ck, D), lambda j, i: (j, 0))],
        out_specs=[pl.BlockSpec((tm, D), lastrow), pl.BlockSpec((8, D), lambda j, i: (0, 0))],
        out_shape=[_sds((s, D), F32), _sds((8, D), F32)],
        scratch_shapes=[pltpu.VMEM((s, D), F32)], name=f"bwd_proj_x{l}")


def _block_diag(w):
    nl, nb, bw, _ = w.shape
    eye = jnp.eye(nb, dtype=w.dtype)
    return jnp.einsum("lhij,hk->lhikj", w, eye).reshape(nl, nb * bw, nb * bw).astype(BF16)


class NoOverlap:
    def __init__(self, big):
        self.big = big

    def weights(self, l):
        return self.big[l]

    def job(self, slot, l):
        return None

    def done(self, slot, l, results):
        pass

    def new_grads(self, group, l, grads):
        pass

    def new_small(self, l, arrays, head_stats):
        pass


def local_step(x, target, norm1_g, norm2_g, final_g, convw, vecs, lru_wx, lru_wa, plan):
    wx_bd, wa_bd = _block_diag(lru_wx), _block_diag(lru_wa)

    def run(fn, slot, l, *args):
        res, cres = fn(*args, l, comm=plan.job(slot, l))
        plan.done(slot, l, cres)
        return res

    saved = []
    for l in range(DEPTH):
        proj = run(fwd_proj, "fwd_proj", l, x, norm1_g, plan.weights(l)["in_t"])
        pre_abd, h = run(fwd_branch, "fwd_branch", l, proj, convw, vecs, wx_bd, wa_bd)
        pre_c = run(fwd_attn, "fwd_attn", l, proj, vecs)
        w = plan.weights(l)
        y4, merged, h1 = run(fwd_merge, "fwd_merge", l, x, proj, pre_abd, pre_c, w["a_t"], w["b_t"], w["c_t"], w["d_t"], w["o"])
        w = plan.weights(l)
        x_out, fg, fu = run(fwd_ffn, "fwd_ffn", l, h1, norm2_g, w["gate_t"], w["up_t"], w["down"])
        saved.append((x, proj, pre_abd, h, pre_c, y4, merged, h1, fg, fu))
        x = x_out
    dx, head_stats = loss_head(x, final_g.reshape(1, D), target)
    small = [None] * DEPTH
    for l in reversed(range(DEPTH)):
        x_in, proj, pre_abd, h, pre_c, y4, merged, h1, fg, fu = saved[l]
        w = plan.weights(l)
        dh1, d_gate, d_up, d_down, st_ffn = run(bwd_ffn, "bwd_ffn", l, dx, h1, fg, fu, norm2_g, w["gate_t"], w["up_t"], w["down"])
        plan.new_grads("ffn", l, dict(gate_t=d_gate, up_t=d_up, down=d_down))
        dproj, dpre, d_o, d_a, d_b, d_c, d_d = run(
            bwd_merge, "bwd_merge", l, dh1, y4, proj, merged, pre_abd, pre_c, w["a_t"], w["b_t"], w["c_t"], w["d_t"], w["o"])
        plan.new_grads("out", l, dict(a_t=d_a, b_t=d_b, c_t=d_c, d_t=d_d, o=d_o))
        dq, dkc, dkp, st_attn = run(bwd_attn, "bwd_attn", l, proj, dpre, vecs)
        dproj, dcw, dvec, dwx, dwa = run(bwd_branch, "bwd_branch", l, proj, dproj, dpre, h, dq, dkc, dkp, convw, vecs, wx_bd, wa_bd)
        if l > 0:
            dx, d_in, st_proj = run(bwd_proj, "bwd_proj", l, dproj, x_in, dh1, norm1_g, w["in_t"])
            plan.new_grads("in", l, dict(in_t=d_in))
        else:
            for half, name in enumerate(("in_a", "in_b")):
                d_half = run(functools.partial(bwd_proj_w, half=half), f"bwd_proj_w{half}", l, dproj, x_in, norm1_g)
                plan.new_grads(name, l, {name: d_half})
            dx, st_proj = run(bwd_proj_x, "bwd_proj_x", l, dproj, x_in, dh1, norm1_g, w["in_t"])
        small[l] = (st_proj, st_ffn, dvec, st_attn, dcw, dwx, dwa)
        plan.new_small(l, small[l], head_stats)
    return head_stats, dx, small


BIG = dict(in_t=("w_in", "view"), a_t=("w_a_out", "transpose"), b_t=("w_b_out", "transpose"), c_t=("w_c_out", "transpose"),
           d_t=("w_d_out", "transpose"), o=("w_o", "plain"), gate_t=("w_ffn_gate", "view"), up_t=("w_ffn_up", "view"),
           down=("w_ffn_down", "plain"))


def cast_transpose(w, name):
    nl, a, b = w.shape
    ta = min(256, a)

    def body(w_ref, o_ref):
        o_ref[...] = w_ref[...].T.astype(BF16)

    return pl.pallas_call(
        body, grid=(nl, a // ta),
        in_specs=[pl.BlockSpec((None, ta, b), lambda l, i: (l, i, 0))],
        out_specs=pl.BlockSpec((None, b, ta), lambda l, i: (l, 0, i)),
        out_shape=_sds((nl, b, a), BF16), compiler_params=_cparams(2), name=name)(w)


def add_partials(mine, recv, core, name):
    n = len(mine)

    def body(core_ref, *refs):
        del core_ref
        for a_ref, b_ref, o_ref in zip(refs[:n], refs[n:2 * n], refs[2 * n:]):
            o_ref[...] = (a_ref[...].astype(F32) + b_ref[...].astype(F32)).astype(BF16)

    return pl.pallas_call(
        body,
        grid_spec=pltpu.PrefetchScalarGridSpec(
            num_scalar_prefetch=1, grid=(4,),
            in_specs=[pl.BlockSpec((None, None) + a.shape[2:], lambda i, cr: (i, cr[0], 0, 0)) for a in mine]
            + [pl.BlockSpec((None,) + b.shape[1:], lambda i, cr: (i, 0, 0)) for b in recv],
            out_specs=[pl.BlockSpec((None,) + b.shape[1:], lambda i, cr: (i, 0, 0)) for b in recv]),
        out_shape=[_sds(b.shape, BF16) for b in recv], compiler_params=_cparams(1), name=name)(core, *mine, *recv)


def _adamw(w, g, m, v):
    m = ADAM_B1 * m + (1.0 - ADAM_B1) * g
    v = ADAM_B2 * v + (1.0 - ADAM_B2) * (g * g)
    m_hat = m / (1.0 - ADAM_B1 ** ADAM_STEP)
    v_hat = v / (1.0 - ADAM_B2 ** ADAM_STEP)
    delta = -ADAM_LR * (m_hat / (jnp.sqrt(v_hat) + ADAM_EPS) + ADAM_WD * w)
    return delta, m, v


def adamw_big(contrib, w, m, v, transposed, name, comm=None):
    nsrc, nl, rows, cols = contrib.shape
    ct = 256

    def body(c_ref, w_ref, m_ref, v_ref, g_out, d_out, m_out, v_out):
        g = c_ref[0].astype(F32)
        for src in range(1, nsrc):
            g = g + c_ref[src].astype(F32)
        if transposed:
            g = g.T
        delta, mn, vn = _adamw(w_ref[...], g, m_ref[...], v_ref[...])
        g_out[...] = g
        d_out[...] = delta
        m_out[...] = mn
        v_out[...] = vn

    if transposed:
        wspec = pl.BlockSpec((None, ct, rows), lambda l, j: (l, j, 0))
    else:
        wspec = pl.BlockSpec((None, rows, ct), lambda l, j: (l, 0, j))
    return _call(
        body, comm, (contrib, w, m, v), grid=(nl, cols // ct),
        in_specs=[pl.BlockSpec((nsrc, None, rows, ct), lambda l, j: (0, l, 0, j)), wspec, wspec, wspec],
        out_specs=[wspec] * 4, out_shape=[_sds(w.shape, F32)] * 4, name=name)


VEC_NAMES = ("conv_a_b", "lru_bx", "lru_ba", "lru_lambda", "conv_d_b", "ln_d_g", "ln_d_b")
P_N1, P_N2, P_VEC, P_CONV, P_LRU = 0, 1, 2, 6, 6 + CW_ROWS
P_FINAL, P_LOSS, P_ROWS = P_LRU + HD, P_LRU + HD + 1, P_LRU + HD + 2
SMALL = ("norm1_g", "conv_a_w", "conv_a_b", "lru_wx", "lru_bx", "lru_wa", "lru_ba", "lru_lambda", "conv_b_w", "sinks",
         "conv_d_w", "conv_d_b", "ln_d_g", "ln_d_b", "norm2_g", "final_g")
VMEM_FULL = pl.BlockSpec(memory_space=pltpu.VMEM)


def _stack_vecs(p):
    rows = [p[n] for n in VEC_NAMES] + [jnp.pad(p["sinks"], ((0, 0), (0, BW - N_HEADS)))]
    return jnp.stack(rows, axis=1)


def _stack_convs(p):
    nl, _, ch = p["conv_a_w"].shape
    z = jnp.zeros((nl, 1, ch), F32)
    return jnp.concatenate([p["conv_a_w"], p["conv_b_w"], z, p["conv_d_w"], z], axis=1)


def _vec_place(r):
    return P_VEC + r // 2, (r % 2) * BW


def pack_small(arrays, head_stats, l):
    n = len(arrays)

    def body(*refs):
        st_proj, st_ffn, dvec, st_attn, dcw, dwx, dwa = refs[:n]
        pack = refs[-1]
        pack[...] = jnp.zeros((P_ROWS, D), F32)
        lane = lax.broadcasted_iota(jnp.int32, (HD, BW), 1)
        pack[P_N1:P_N1 + 1, :] = st_proj[0:1, :]
        pack[P_N2:P_N2 + 1, :] = st_ffn[0:1, :]
        for r in range(len(VEC_NAMES)):
            row, c0 = _vec_place(r)
            pack[row:row + 1, c0:c0 + BW] = dvec[r:r + 1, :]
        row, c0 = _vec_place(V_SINK)
        pack[row:row + 1, c0:c0 + 128] = st_attn[0:1, :]
        pack[P_CONV:P_CONV + CW_ROWS, 0:BW] = dcw[...]
        for mat, c0 in ((dwx, 0), (dwa, BW)):
            blocks = jnp.zeros((HD, BW), F32)
            for h in range(BW // HD):
                blocks = jnp.where((lane >= HD * h) & (lane < HD * (h + 1)), mat[HD * h:HD * (h + 1), :], blocks)
            pack[P_LRU:P_LRU + HD, c0:c0 + BW] = blocks
        if head_stats is not None:
            pack[P_FINAL:P_LOSS + 1, :] = refs[n][0:2, :]

    flat = list(arrays) + ([] if head_stats is None else [head_stats])
    return pl.pallas_call(body, out_shape=_sds((P_ROWS, D), F32), in_specs=[VMEM_FULL] * len(flat), out_specs=VMEM_FULL,
                          name=f"pack_small{l}", compiler_params=pltpu.CompilerParams(vmem_limit_bytes=VMEM_LIMIT))(*flat)


def adamw_small(gathered, me, w, m, v):
    ns = len(SMALL)

    def body(me_ref, *refs):
        c_refs, refs = refs[:DEPTH], refs[DEPTH:]
        w_refs, m_refs, v_refs = refs[:ns], refs[ns:2 * ns], refs[2 * ns:3 * ns]
        loss_ref, outs, gs = refs[3 * ns], refs[3 * ns + 1:3 * ns + 1 + 4 * ns], refs[-1]
        for l in range(DEPTH):
            gs[l] = c_refs[l][0]
            for dev in range(1, NDEV):
                gs[l] += c_refs[l][dev]
        loss_ref[...] = gs[DEPTH - 1, P_LOSS:P_LOSS + 1, 0:128]

        def update(name, sel, g):
            i = SMALL.index(name)
            delta, mn, vn = _adamw(w_refs[i][sel], g, m_refs[i][sel], v_refs[i][sel])
            for o_ref, val in zip(outs[4 * i:4 * i + 4], (g, delta, mn, vn)):
                o_ref[sel] = val

        update("final_g", (slice(0, 1), slice(None)), gs[DEPTH - 1, P_FINAL:P_FINAL + 1, :])
        shift = (BW - me_ref[0] * (BW // NDEV)) & (BW - 1)
        for l in range(DEPTH):
            row = (slice(l, l + 1), slice(None))
            update("norm1_g", row, gs[l, P_N1:P_N1 + 1, :])
            update("norm2_g", row, gs[l, P_N2:P_N2 + 1, :])
            for r, name in enumerate(VEC_NAMES):
                prow, c0 = _vec_place(r)
                update(name, row, gs[l, prow:prow + 1, c0:c0 + BW])
            prow, c0 = _vec_place(V_SINK)
            update("sinks", row, gs[l, prow:prow + 1, c0:c0 + N_HEADS])
            mine = pltpu.roll(gs[l, P_CONV:P_CONV + CW_ROWS, 0:BW], shift, 1)[:, 0:BW // NDEV]
            update("conv_a_w", (l,), mine[CW_A:CW_A + CONV_A])
            update("conv_b_w", (l,), mine[CW_B:CW_B + CONV_B])
            update("conv_d_w", (l,), mine[CW_D:CW_D + CONV_D])
            for h in range(BW // HD):
                update("lru_wx", (l, h), gs[l, P_LRU:P_LRU + HD, HD * h:HD * (h + 1)])
                update("lru_wa", (l, h), gs[l, P_LRU:P_LRU + HD, BW + HD * h:BW + HD * (h + 1)])

    args = [p[n] for p in (w, m, v) for n in SMALL]
    full = lambda a: pl.BlockSpec(a.shape, lambda i, me_ref: (0,) * a.ndim)
    out_shape = [_sds((1, 128), F32)] + [_sds(w[n].shape, F32) for n in SMALL for _ in range(4)]
    outs = pl.pallas_call(
        body,
        grid_spec=pltpu.PrefetchScalarGridSpec(
            num_scalar_prefetch=1, grid=(1,),
            in_specs=[full(a) for a in list(gathered) + args], out_specs=[full(o) for o in out_shape],
            scratch_shapes=[pltpu.VMEM((DEPTH, P_ROWS, D), F32)]),
        out_shape=out_shape, name="adamw_small", compiler_params=_cparams(1))(me, *gathered, *args)
    return outs[0], {n: outs[1 + 4 * i:5 + 4 * i] for i, n in enumerate(SMALL)}


def merge_jobs(jobs):
    jobs = [j for j in jobs if j is not None]
    if not jobs:
        return None, []
    inputs, aliases, outs, sems, cuts = [], {}, [], [], []
    for j in jobs:
        i0, o0, s0 = len(inputs), len(outs), len(sems)
        aliases.update({i0 + i: o0 + o for i, o in j.aliases.items()})
        inputs += j.inputs
        outs += j.out_shapes
        sems += j.sem_shapes
        cuts.append((i0, len(inputs), o0, len(outs), s0, len(sems)))

    def each(which):
        def go(cins, couts, s):
            for j, (i0, i1, o0, o1, s0, s1) in zip(jobs, cuts):
                if getattr(j, which) is not None:
                    getattr(j, which)(cins[i0:i1], couts[o0:o1], s[s0:s1])
        return go

    relay = each("relay") if any(j.relay is not None for j in jobs) else None
    return CommJob(inputs, aliases, outs, sems, each("start"), each("finish"), relay), [(c[2], c[3]) for c in cuts]


SIXTHS = 6
OUT_KINDS = ("a_t", "b_t", "c_t", "d_t", "o")
GATHER_PLAN = {
    "fwd_proj": [(k, 0, 0, 6) for k in OUT_KINDS] + [("gate_t", 0, 0, 3)],
    "fwd_branch": [("gate_t", 0, 3, 6), ("up_t", 0, 0, 3)],
    "fwd_attn": [("up_t", 0, 3, 6), ("down", 0, 0, 6)],
    "fwd_merge": [("in_t", 1, 0, 2)],
    "fwd_ffn": [("in_t", 1, 2, 6)],
}
SIBLING_PLAN = {"bwd_merge": ("ffn", 0), "bwd_branch": ("out", 0), "bwd_ffn": ("in", 1),
                "bwd_proj_w1": ("in_a", 0), "bwd_proj_x": ("in_b", 0)}
GROUPS = dict(ffn=("gate_t", "up_t", "down"), out=OUT_KINDS, in_a=("in_a",), in_b=("in_b",))
GROUPS["in"] = ("in_t",)
COLUMN_HALF = dict(in_a=("in_t", W_IN_PARTS[0][0]), in_b=("in_t", W_IN_PARTS[1][0]))
CHIP_PLAN = {
    "bwd_attn": [("in_t", 1, 3, 6), ("gate_t", 0, 0, 3)],
    "bwd_branch": [("gate_t", 0, 3, 6), ("up_t", 0, 0, 6), ("down", 0, 0, 6)],
    "bwd_proj": [(k, 0, 0, 6) for k in OUT_KINDS],
    "bwd_proj_w0": [(k, 0, 0, 6) for k in OUT_KINDS[:3]],
    "bwd_proj_w1": [(k, 0, 0, 6) for k in OUT_KINDS[3:]],
    "bwd_merge": [("in_t", 1, 0, 3)],
    "bwd_proj_x": [("in_a", 0, 0, 6)],
    "adamw_gate_t": [("in_b", 0, 0, 6)],
}
SMALL_GATHER_PLAN = {"bwd_ffn": 1, "adamw_down": 0}


class Overlap:
    def __init__(self, shards, core):
        self.shards = shards
        self.core = core
        self.gathered = [dict.fromkeys(BIG) for _ in range(DEPTH)]
        self.views = {}
        self.partial = {}
        self.contrib = dict.fromkeys(BIG)
        self.small_packs = [None] * DEPTH
        self.small_gathered = [None] * DEPTH
        self._open = None

    def weights(self, l):
        return self.gathered[l]

    def new_grads(self, group, l, grads):
        for k, g in grads.items():
            self.views[k, l] = g.reshape(4, 2, g.shape[0] // NDEV, g.shape[1])

    def new_small(self, l, arrays, head_stats):
        self.small_packs[l] = pack_small(arrays, head_stats if l == DEPTH - 1 else None, l)

    @staticmethod
    def _rows(shard_rows, f0, f1):
        return shard_rows * f0 // SIXTHS, shard_rows * (f1 - f0) // SIXTHS

    def job(self, slot, l):
        jobs, notes = [], []
        pieces = [(k, l + dl, f0, f1) for k, dl, f0, f1 in GATHER_PLAN.get(slot, []) if l + dl < DEPTH]
        if pieces:
            jobs.append(gather_job([((k, ll), self.shards[ll][k], self.gathered[ll][k],
                                     *self._rows(self.shards[ll][k].shape[0], f0, f1)) for k, ll, f0, f1 in pieces]))
            notes.append(("gather", list(dict.fromkeys((k, ll) for k, ll, _, _ in pieces))))
        if slot in SIBLING_PLAN and l + SIBLING_PLAN[slot][1] < DEPTH:
            group, dl = SIBLING_PLAN[slot]
            keys = [(k, l + dl) for k in GROUPS[group]]
            jobs.append(sibling_exchange_job([self.views[key] for key in keys]))
            notes.append(("sibling", keys))
        pieces = [(k, l + dl, f0, f1) for k, dl, f0, f1 in CHIP_PLAN.get(slot, []) if l + dl < DEPTH]
        if pieces:
            whole = [(*COLUMN_HALF.get(k, (k, 0)), k, ll, f0, f1) for k, ll, f0, f1 in pieces]
            jobs.append(chip_exchange_job([(self.partial[k, ll], self.contrib[kind], kind, ll,
                                            *self._rows(self.partial[k, ll].shape[1], f0, f1), col0, self.shards[ll][kind].shape[1])
                                           for kind, col0, k, ll, f0, f1 in whole]))
            notes.append(("chips", list(dict.fromkeys(kind for kind, *_ in whole))))
        if slot in SMALL_GATHER_PLAN and l + SMALL_GATHER_PLAN[slot] < DEPTH:
            ll = l + SMALL_GATHER_PLAN[slot]
            jobs.append(gather_job([("small", self.small_packs[ll], None, 0, P_ROWS)]))
            notes.append(("small", ll))
        job, spans = merge_jobs(jobs)
        self._open = (slot, l, notes, spans)
        return job

    def done(self, slot, l, results):
        open_slot, open_l, notes, spans = self._open
        assert (open_slot, open_l) == (slot, l)
        for (what, keys), (r0, r1) in zip(notes, spans):
            res = results[r0:r1]
            if what == "gather":
                for (k, ll), g in zip(keys, res):
                    self.gathered[ll][k] = g
            elif what == "sibling":
                sums = add_partials([self.views[key] for key in keys], list(res), self.core, f"chip_sum_{keys[0][0]}{keys[0][1]}")
                self.partial.update(zip(keys, sums))
            elif what == "chips":
                for k, c in zip(keys, res):
                    self.contrib[k] = c
            else:
                self.small_gathered[keys], = res


SMALL = ("norm1_g", "conv_a_w", "conv_a_b", "lru_wx", "lru_bx", "lru_wa", "lru_ba", "lru_lambda", "conv_b_w", "sinks",
         "conv_d_w", "conv_d_b", "ln_d_g", "ln_d_b", "norm2_g", "final_g")
WEIGHTS = ("norm1_g", "w_in", "conv_a_w", "conv_a_b", "lru_wx", "lru_bx", "lru_wa", "lru_ba", "lru_lambda", "w_a_out",
           "conv_b_w", "w_b_out", "sinks", "w_c_out", "conv_d_w", "conv_d_b", "ln_d_g", "ln_d_b", "w_d_out", "w_o",
           "norm2_g", "w_ffn_gate", "w_ffn_up", "w_ffn_down", "final_g")


def kernel(x, norm1_g, w_in, conv_a_w, conv_a_b, lru_wx, lru_bx, lru_wa, lru_ba, lru_lambda, w_a_out, conv_b_w, w_b_out, sinks, w_c_out, conv_d_w, conv_d_b, ln_d_g, ln_d_b, w_d_out, w_o, norm2_g, w_ffn_gate, w_ffn_up, w_ffn_down, final_g, loss_target, m_norm1_g, m_w_in, m_conv_a_w, m_conv_a_b, m_lru_wx, m_lru_bx, m_lru_wa, m_lru_ba, m_lru_lambda, m_w_a_out, m_conv_b_w, m_w_b_out, m_sinks, m_w_c_out, m_conv_d_w, m_conv_d_b, m_ln_d_g, m_ln_d_b, m_w_d_out, m_w_o, m_norm2_g, m_w_ffn_gate, m_w_ffn_up, m_w_ffn_down, m_final_g, v_norm1_g, v_w_in, v_conv_a_w, v_conv_a_b, v_lru_wx, v_lru_bx, v_lru_wa, v_lru_ba, v_lru_lambda, v_w_a_out, v_conv_b_w, v_w_b_out, v_sinks, v_w_c_out, v_conv_d_w, v_conv_d_b, v_ln_d_g, v_ln_d_b, v_w_d_out, v_w_o, v_norm2_g, v_w_ffn_gate, v_w_ffn_up, v_w_ffn_down, v_final_g):
    args = dict(locals())
    w = {n: args[n] for n in WEIGHTS}
    m = {n: args["m_" + n] for n in WEIGHTS}
    v = {n: args["v_" + n] for n in WEIGHTS}
    me = _dev_index(*_mesh_pos())

    def rows_major(a, how):
        return jnp.swapaxes(a, 1, 2) if how == "view" else a

    stacked = {k: cast_transpose(w[n], "prep_" + k) if how == "transpose" else rows_major(w[n], how).astype(BF16)
               for k, (n, how) in BIG.items()}
    plan = Overlap([{k: stacked[k][l] for k in BIG} for l in range(DEPTH)], lax.axis_index("c").astype(jnp.int32).reshape(1))
    convs = _stack_convs(w).reshape(DEPTH * CW_ROWS, BW // NDEV)
    g_in0, g_conv = _comm_only(gather_job([(("in_t", 0), plan.shards[0]["in_t"], None, 0, plan.shards[0]["in_t"].shape[0]),
                                           ("convs", convs, None, 0, convs.shape[0])]), "gather_first")
    plan.gathered[0]["in_t"] = g_in0
    convw = g_conv.reshape(NDEV, DEPTH, CW_ROWS, BW // NDEV).transpose(1, 2, 0, 3).reshape(DEPTH, CW_ROWS, BW)

    vecs = _stack_vecs(w)
    head_stats, grad_x, grads = local_step(x[0], loss_target[0], norm1_g, norm2_g, final_g, convw, vecs, lru_wx, lru_wa, plan)


    out = {}
    for k in ("down", "gate_t", "up_t", "o", "a_t", "b_t", "c_t", "d_t", "in_t"):
        n, how = BIG[k]
        res, cres = adamw_big(plan.contrib[k], rows_major(w[n], how), rows_major(m[n], how), rows_major(v[n], how),
                              how == "transpose", "adamw_" + k, comm=plan.job("adamw_" + k, 0))
        plan.done("adamw_" + k, 0, cres)
        out[n] = [rows_major(r, how) for r in res]

    def own_shapes(p):
        return {n: p[n].reshape(1, D) if n == "final_g" else p[n] for n in SMALL}

    loss, small = adamw_small([g.reshape(NDEV, P_ROWS, D) for g in plan.small_gathered], me.astype(jnp.int32).reshape(1),
                              own_shapes(w), own_shapes(m), own_shapes(v))
    for n in SMALL:
        out[n] = [r.reshape(w[n].shape) for r in small[n]]
    loss = loss[0, 0]
    return (loss, grad_x[None], *[out[n][0] for n in WEIGHTS], *[out[n][1] for n in WEIGHTS],
            *[out[n][2] for n in WEIGHTS], *[out[n][3] for n in WEIGHTS])
```

```python
import functools

import jax
import jax.numpy as jnp
from jax import lax
from jax.experimental import pallas as pl
from jax.experimental.pallas import tpu as pltpu

F32 = jnp.float32
BF16 = jnp.bfloat16
E = pl.Element

D = 1024
BW = 512
IN_W = 8448
GL0 = 4352
FF = 2816
N_HEADS = 8
N_KV = 2
HD = 64
ATT_BLK = 128
EPS = 1e-6
LRU_C = 8.0
NEG_INF = -1e30
DEPTH = 2
NDEV = 8
CONV_A, CONV_B, CONV_D = 4, 3, 31
C_AX, C_AG, C_BV, C_BC, C_BB, C_Q, C_K, C_V, C_D1, C_D2 = 0, 512, 1024, 1536, 2048, 2560, 3072, 3200, 3328, 3840
CW_A, CW_B, CW_D, CW_ROWS = 0, 4, 8, 40
V_CAB, V_BX, V_BA, V_LAM, V_CDB, V_LNG, V_LNB, V_SINK, V_ROWS = 0, 1, 2, 3, 4, 5, 6, 7, 8
HALO = 32
W_IN_PARTS = ((0, 768), (768, 256))

ADAM_LR, ADAM_B1, ADAM_B2, ADAM_EPS, ADAM_WD, ADAM_STEP = 0.001, 0.9, 0.999, 1e-08, 0.01, 10

VMEM_LIMIT = 56 * 1024 * 1024

_NN = (((1,), (0,)), ((), ()))
_NT = (((1,), (1,)), ((), ()))
_TN = (((0,), (0,)), ((), ()))


def _dot(a, b, dims):
    return lax.dot_general(a.astype(BF16), b.astype(BF16), dims, preferred_element_type=F32)


def _cparams(n_axes):
    return pltpu.CompilerParams(dimension_semantics=("arbitrary",) * n_axes, vmem_limit_bytes=VMEM_LIMIT)


def _sds(shape, dtype):
    return jax.ShapeDtypeStruct(tuple(shape), dtype)


def _sigmoid(x):
    return jax.nn.sigmoid(x)


def _neg_expm1(x):
    p = x * (1.0 + x * (0.5 + x * (1.0 / 6.0 + x * (1.0 / 24.0 + x * (1.0 / 120.0)))))
    return jnp.where(x > -0.1, -p, 1.0 - jnp.exp(x))


def _softplus(z):
    return jnp.maximum(z, 0.0) + jnp.log1p(jnp.exp(-jnp.abs(z)))


def _gelu_and_grad(x):
    c = 0.7978845608028654
    inner = c * (x + 0.044715 * x * x * x)
    t = jnp.tanh(inner)
    g = 0.5 * x * (1.0 + t)
    dg = 0.5 * (1.0 + t) + 0.5 * x * (1.0 - t * t) * c * (1.0 + 3.0 * 0.044715 * x * x)
    return g, dg


ANY = pl.BlockSpec(memory_space=pl.ANY)
MESH = pl.DeviceIdType.MESH


def _mesh_pos():
    return lax.axis_index("x"), lax.axis_index("y"), lax.axis_index("c")


def _dev_index(px, py, pc):
    return 4 * px + 2 * py + pc


class CommJob:
    def __init__(self, inputs, aliases, out_shapes, sem_shapes, start, finish, relay=None):
        self.inputs, self.aliases, self.out_shapes, self.sem_shapes = list(inputs), dict(aliases), list(out_shapes), list(sem_shapes)
        self.start, self.finish, self.relay = start, finish, relay


def _call(body, comm, args, *, grid, in_specs, out_specs, out_shape, scratch_shapes=(), name, aliases=None):
    single = not isinstance(out_shape, (list, tuple))
    out_specs = [out_specs] if single else list(out_specs)
    out_shape = [out_shape] if single else list(out_shape)
    scratch_shapes = list(scratch_shapes)
    n_in, n_out, n_scr, n_axes = len(in_specs), len(out_shape), len(scratch_shapes), len(grid)
    params = pltpu.CompilerParams(dimension_semantics=("arbitrary",) * n_axes, vmem_limit_bytes=VMEM_LIMIT)
    io_aliases = dict(aliases or {})
    if comm is None:
        outs = pl.pallas_call(body, grid=grid, in_specs=in_specs, out_specs=out_specs, out_shape=out_shape,
                              scratch_shapes=scratch_shapes, input_output_aliases=io_aliases, compiler_params=params,
                              name=name)(*args)
        return (outs[0] if single else outs), []
    c_in, c_out = len(comm.inputs), len(comm.out_shapes)
    io_aliases.update({n_in + i: n_out + o for i, o in comm.aliases.items()})

    def wrapped(*refs):
        ins, cins = refs[:n_in], refs[n_in:n_in + c_in]
        outs = refs[n_in + c_in:n_in + c_in + n_out]
        couts = refs[n_in + c_in + n_out:n_in + c_in + n_out + c_out]
        rest = refs[n_in + c_in + n_out + c_out:]
        scr, sems = rest[:n_scr], rest[n_scr:]
        first = functools.reduce(lambda a, b: a & b, [pl.program_id(a) == 0 for a in range(n_axes)])
        last = functools.reduce(lambda a, b: a & b, [pl.program_id(a) == pl.num_programs(a) - 1 for a in range(n_axes)])

        @pl.when(first)
        def _():
            comm.start(cins, couts, sems)

        if comm.relay is not None:
            step = functools.reduce(lambda a, b: a * grid[b] + pl.program_id(b), range(1, n_axes), pl.program_id(0))
            n_steps = functools.reduce(lambda a, b: a * b, grid)

            @pl.when(step == n_steps // 2)
            def _():
                comm.relay(cins, couts, sems)

        body(*ins, *outs, *scr)

        @pl.when(last)
        def _():
            comm.finish(cins, couts, sems)

    outs = pl.pallas_call(
        wrapped, grid=grid, in_specs=list(in_specs) + [ANY] * c_in, out_specs=out_specs + [ANY] * c_out,
        out_shape=out_shape + comm.out_shapes, scratch_shapes=scratch_shapes + comm.sem_shapes,
        input_output_aliases=io_aliases, compiler_params=params, name=name)(*args, *comm.inputs)
    res, cres = outs[:n_out], outs[n_out:]
    return (res[0] if single else res), cres


def _comm_only(comm, name):
    c_in, c_out = len(comm.inputs), len(comm.out_shapes)

    def body(*refs):
        cins, couts, sems = refs[:c_in], refs[c_in:c_in + c_out], refs[c_in + c_out:]
        comm.start(cins, couts, sems)
        if comm.relay is not None:
            comm.relay(cins, couts, sems)
        comm.finish(cins, couts, sems)

    return pl.pallas_call(body, in_specs=[ANY] * c_in, out_specs=[ANY] * c_out, out_shape=comm.out_shapes,
                          scratch_shapes=comm.sem_shapes, input_output_aliases=comm.aliases, name=name)(*comm.inputs)


def gather_job(pieces):
    inputs, aliases, out_shapes, plan, where = [], {}, [], [], {}
    for key, shard, gathered, row0, nrows in pieces:
        if key not in where:
            where[key] = (len(inputs), len(out_shapes))
            inputs.append(shard)
            if gathered is not None:
                aliases[len(inputs)] = len(out_shapes)
                inputs.append(gathered)
            out_shapes.append(_sds((NDEV * shard.shape[0], shard.shape[1]), shard.dtype))
        plan.append((*where[key], shard.shape[0], row0, nrows))
    n = len(plan)

    def copies(cins, couts, sems):
        send_sems, recv_sems, local_sems = sems
        x, y, c = _mesh_pos()
        me, sibling = (x, y, c), (x, y, 1 - c)
        xn, yn, dg = (1 - x, y), (x, 1 - y), (1 - x, 1 - y)
        local, first, pass1, pass2, got_ici, got_fwd, got_d2d = [], [], [], [], [], [], []
        for p, (i_shard, i_out, rows, row0, nrows) in enumerate(plan):
            src = cins[i_shard].at[pl.ds(row0, nrows), :]
            half = cins[i_shard].shape[1] // 2
            left, right, whole = pl.ds(0, half), pl.ds(half, half), slice(None)

            def slot(dev, lanes, i_out=i_out, rows=rows, row0=row0, nrows=nrows):
                return couts[i_out].at[pl.ds(_dev_index(*dev) * rows + row0, nrows), lanes]

            def copy(g, dev, to, lanes=whole, src=None, p=p, slot=slot):
                return pltpu.make_async_remote_copy(
                    src_ref=slot(dev, lanes) if src is None else src, dst_ref=slot(dev, lanes),
                    send_sem=send_sems.at[g, p], recv_sem=recv_sems.at[g, p], device_id=to, device_id_type=MESH)

            local.append(pltpu.make_async_copy(src, slot(me, whole), local_sems.at[p]))
            first += [copy(0, me, sibling, src=src), copy(1, me, (*xn, c), src=src), copy(2, me, (*yn, c), src=src)]
            got_ici += [copy(1, (*xn, c), me), copy(2, (*yn, c), me)]
            pass1 += [copy(3, (*xn, c), (*yn, c), left), copy(4, (*yn, c), (*xn, c), right),
                      copy(5, (*xn, c), sibling), copy(6, (*yn, c), sibling)]
            got_fwd += [copy(3, (*dg, c), me, left), copy(4, (*dg, c), me, right)]
            pass2 += [copy(7, (*dg, c), sibling, left), copy(8, (*dg, c), sibling, right)]
            got_d2d += [copy(0, sibling, me), copy(5, (*xn, 1 - c), me), copy(6, (*yn, 1 - c), me),
                        copy(7, (*dg, 1 - c), me, left), copy(8, (*dg, 1 - c), me, right)]
        return local, first, pass1, pass2, got_ici, got_fwd, got_d2d

    def start(cins, couts, sems):
        local, first, *_ = copies(cins, couts, sems)
        for cp in local + first:
            cp.start()

    def pass_on(cins, couts, sems):
        _, _, pass1, _, got_ici, _, _ = copies(cins, couts, sems)
        for cp in got_ici:
            cp.wait_recv()
        for cp in pass1:
            cp.start()

    def finish(cins, couts, sems):
        local, first, pass1, pass2, _, got_fwd, got_d2d = copies(cins, couts, sems)
        for cp in got_fwd:
            cp.wait_recv()
        for cp in pass2:
            cp.start()
        for cp in got_d2d:
            cp.wait_recv()
        for cp in first + pass1 + pass2:
            cp.wait_send()
        for cp in local:
            cp.wait()

    sem_shapes = [pltpu.SemaphoreType.DMA((9, n)), pltpu.SemaphoreType.DMA((9, n)), pltpu.SemaphoreType.DMA((n,))]
    return CommJob(inputs, aliases, out_shapes, sem_shapes, start, finish, relay=pass_on)


def sibling_exchange_job(grads):
    n = len(grads)

    def copies(cins, couts, sems):
        send_sems, recv_sems = sems
        x, y, c = _mesh_pos()
        return [pltpu.make_async_remote_copy(
            src_ref=cins[q].at[:, 1 - c], dst_ref=couts[q], send_sem=send_sems.at[q], recv_sem=recv_sems.at[q],
            device_id=(x, y, 1 - c), device_id_type=MESH) for q in range(n)]

    def start(cins, couts, sems):
        for cp in copies(cins, couts, sems):
            cp.start()

    def finish(cins, couts, sems):
        cps = copies(cins, couts, sems)
        for cp in cps:
            cp.wait_recv()
        for cp in cps:
            cp.wait_send()

    return CommJob(grads, {}, [_sds((4,) + g.shape[2:], g.dtype) for g in grads],
                   [pltpu.SemaphoreType.DMA((n,)), pltpu.SemaphoreType.DMA((n,))], start, finish)


def chip_exchange_job(pieces):
    inputs, aliases, out_shapes, plan, where = [], {}, [], [], {}
    for partial, contrib, key, layer, row0, nrows, col0, cols in pieces:
        if key not in where:
            where[key] = len(out_shapes)
            out_shapes.append(_sds((4, DEPTH, partial.shape[1], cols), partial.dtype))
            if contrib is not None:
                aliases[len(inputs)] = where[key]
                inputs.append(contrib)
        plan.append((len(inputs), where[key], layer, row0, nrows, col0, partial.shape[2]))
        inputs.append(partial)
    n = len(plan)

    def copies(cins, couts, sems):
        send_sems, recv_sems, local_sems = sems
        x, y, c = _mesh_pos()
        mine = 2 * x + y
        local, sends, recvs = [], [], []
        for p, (i_in, i_out, layer, row0, nrows, col0, ncols) in enumerate(plan):
            rows, lanes = pl.ds(row0, nrows), pl.ds(col0, ncols)
            local.append(pltpu.make_async_copy(cins[i_in].at[mine, rows, :], couts[i_out].at[mine, layer, rows, lanes],
                                               local_sems.at[p]))
            for j, (cx, cy) in enumerate([(1 - x, y), (x, 1 - y), (1 - x, 1 - y)]):
                theirs = 2 * cx + cy

                def copy(slot_there, j=j, p=p, cx=cx, cy=cy, theirs=theirs, i_in=i_in, i_out=i_out, layer=layer,
                         rows=rows, lanes=lanes):
                    return pltpu.make_async_remote_copy(
                        src_ref=cins[i_in].at[theirs, rows, :], dst_ref=couts[i_out].at[slot_there, layer, rows, lanes],
                        send_sem=send_sems.at[j, p], recv_sem=recv_sems.at[j, p], device_id=(cx, cy, c), device_id_type=MESH)
                sends.append(copy(mine))
                recvs.append(copy(theirs))
        return local, sends, recvs

    def start(cins, couts, sems):
        local, sends, _ = copies(cins, couts, sems)
        for cp in local + sends:
            cp.start()

    def finish(cins, couts, sems):
        local, sends, recvs = copies(cins, couts, sems)
        for cp in recvs:
            cp.wait_recv()
        for cp in sends:
            cp.wait_send()
        for cp in local:
            cp.wait()

    sem_shapes = [pltpu.SemaphoreType.DMA((3, n)), pltpu.SemaphoreType.DMA((3, n)), pltpu.SemaphoreType.DMA((n,))]
    return CommJob(inputs, aliases, out_shapes, sem_shapes, start, finish)


def fwd_proj(x, g1, wt_in, l, comm=None):
    s = x.shape[0]
    tm = min(512, s)
    tn = 1408

    def body(x_ref, g_ref, w_ref, o_ref, xn_ref):
        @pl.when(pl.program_id(1) == 0)
        def _():
            xv = x_ref[...]
            r = lax.rsqrt(jnp.mean(xv * xv, axis=-1, keepdims=True) + EPS)
            xn_ref[...] = (xv * r * g_ref[l:l + 1, :]).astype(BF16)

        o_ref[...] = _dot(xn_ref[...], w_ref[...], _NT).astype(BF16)

    return _call(
        body, comm, (x, g1, wt_in), grid=(s // tm, IN_W // tn),
        in_specs=[pl.BlockSpec((tm, D), lambda i, j: (i, 0)),
                  pl.BlockSpec((DEPTH, D), lambda i, j: (0, 0)),
                  pl.BlockSpec((tn, D), lambda i, j: (j, 0))],
        out_specs=pl.BlockSpec((tm, tn), lambda i, j: (i, j)),
        out_shape=_sds((s, IN_W), BF16),
        scratch_shapes=[pltpu.VMEM((tm, D), BF16)], name=f"fwd_proj{l}")


def _scan_fwd(a_ref, u_ref, h_ref, h0, n_rows):
    row = lax.broadcasted_iota(jnp.int32, (8, BW), 0)

    def body(g, hprev):
        r = pl.multiple_of(g * 8, 8)
        a = a_ref[pl.ds(r, 8), :]
        u = u_ref[pl.ds(r, 8), :]
        for sft in (1, 2, 4):
            a_sh = jnp.where(row >= sft, pltpu.roll(a, sft, 0), 1.0)
            u_sh = jnp.where(row >= sft, pltpu.roll(u, sft, 0), 0.0)
            u = u + a * u_sh
            a = a * a_sh
        h = u + a * hprev
        h_ref[pl.ds(r, 8), :] = h
        return h[7:8, :]

    return lax.fori_loop(0, n_rows // 8, body, h0)


def _scan_bwd(b_ref, g_ref, o_ref, c0, n_rows):
    row = lax.broadcasted_iota(jnp.int32, (8, BW), 0)

    def body(k, cnext):
        r = pl.multiple_of((n_rows // 8 - 1 - k) * 8, 8)
        b = b_ref[pl.ds(r, 8), :]
        g = g_ref[pl.ds(r, 8), :]
        for sft in (1, 2, 4):
            b_sh = jnp.where(row < 8 - sft, pltpu.roll(b, 8 - sft, 0), 1.0)
            g_sh = jnp.where(row < 8 - sft, pltpu.roll(g, 8 - sft, 0), 0.0)
            g = g + b * g_sh
            b = b * b_sh
        o = g + b * cnext
        o_ref[pl.ds(r, 8), :] = o
        return o[0:1, :]

    return lax.fori_loop(0, n_rows // 8, body, c0)


def _shifted_copies(buf, shifted, n_rows):
    for r in range(1, 8):
        shifted[r - 1, 0:n_rows - 8, :] = buf[pl.ds(r, n_rows - 8), :]


def _window(buf, shifted, off, t):
    r = off % 8
    return buf[pl.ds(off, t), :] if r == 0 else shifted[r - 1, pl.ds(off - r, t), :]


def _branch_fwd_math(cur_ref, halo_ref, cw_ref, vec_ref, wx_ref, wa_ref, bufa, bufb, bufd, xd, first, t):
    def halo(c0):
        v = halo_ref[:, c0:c0 + BW].astype(F32)
        return jnp.where(first, 0.0, v)

    def cur(c0):
        return cur_ref[:, c0:c0 + BW].astype(F32)

    out = {}
    bufa[0:HALO, :] = halo(C_AX)
    bufa[HALO:HALO + t, :] = cur(C_AX)
    ca = jnp.zeros((t, BW), F32) + vec_ref[V_CAB:V_CAB + 1, :]
    for k in range(CONV_A):
        ca = ca + cw_ref[CW_A + k:CW_A + k + 1, :] * bufa[pl.ds(HALO - (CONV_A - 1) + k, t), :]
    gi = _sigmoid(_dot(ca, wx_ref[...], _NN) + vec_ref[V_BX:V_BX + 1, :])
    gr = _sigmoid(_dot(ca, wa_ref[...], _NN) + vec_ref[V_BA:V_BA + 1, :])
    sp = _softplus(-vec_ref[V_LAM:V_LAM + 1, :])
    la = -LRU_C * sp * gr
    a = jnp.exp(la)
    mult = jnp.sqrt(_neg_expm1(2.0 * la))
    out.update(ca=ca, gi=gi, gr=gr, sp=sp, a=a, mult=mult)
    bufb[0:HALO, :] = halo(C_BC) * halo(C_BV)
    bufb[HALO:HALO + t, :] = cur(C_BC) * cur(C_BV)
    cb = jnp.zeros((t, BW), F32)
    for k in range(CONV_B):
        cb = cb + cw_ref[CW_B + k:CW_B + k + 1, :] * bufb[pl.ds(HALO - (CONV_B - 1) + k, t), :]
    out.update(cb=cb)
    bufd[0:HALO, :] = halo(C_D1) * _sigmoid(halo(C_D2))
    s2 = _sigmoid(cur(C_D2))
    bufd[HALO:HALO + t, :] = cur(C_D1) * s2
    _shifted_copies(bufd, xd, t + HALO)
    cd = jnp.zeros((t, BW), F32) + vec_ref[V_CDB:V_CDB + 1, :]
    for k in range(CONV_D):
        cd = cd + cw_ref[CW_D + k:CW_D + k + 1, :] * _window(bufd, xd, HALO - (CONV_D - 1) + k, t)
    mu = jnp.mean(cd, axis=-1, keepdims=True)
    xc = cd - mu
    rstd = lax.rsqrt(jnp.mean(xc * xc, axis=-1, keepdims=True) + EPS)
    xh = xc * rstd
    ln = xh * vec_ref[V_LNG:V_LNG + 1, :] + vec_ref[V_LNB:V_LNB + 1, :]
    out.update(s2=s2, xh=xh, rstd=rstd, ln=ln)
    return out


def fwd_branch(proj, convw, vecs, wx_bd, wa_bd, l, comm=None):
    s = proj.shape[0]
    t = min(256, s)

    def body(cur_ref, halo_ref, cw_ref, vec_ref, wx_ref, wa_ref, pre_ref, h_ref, bufa, bufb, bufd, xd, a_s, u_s, hcar):
        first = pl.program_id(0) == 0

        @pl.when(first)
        def _():
            hcar[...] = jnp.zeros((1, BW), F32)

        v = _branch_fwd_math(cur_ref, halo_ref, cw_ref, vec_ref, wx_ref, wa_ref, bufa, bufb, bufd, xd, first, t)
        a_s[...] = v["a"]
        u_s[...] = v["ca"] * v["gi"] * v["mult"]
        hcar[...] = _scan_fwd(a_s, u_s, h_ref, hcar[...], t)
        gg, _ = _gelu_and_grad(cur_ref[:, C_AG:C_AG + BW].astype(F32))
        pre_ref[:, 0:BW] = (h_ref[...] * gg).astype(BF16)
        pre_ref[:, BW:2 * BW] = (cur_ref[:, C_BB:C_BB + BW].astype(F32) * v["cb"]).astype(BF16)
        ln = v["ln"]
        pre_ref[:, 2 * BW:3 * BW] = (ln * _sigmoid(ln)).astype(BF16)

    hb = t // HALO
    return _call(
        body, comm, (proj, proj, convw, vecs, wx_bd, wa_bd), grid=(s // t,),
        in_specs=[pl.BlockSpec((t, GL0), lambda i: (i, 0)),
                  pl.BlockSpec((HALO, GL0), lambda i: (jnp.maximum(i * hb - 1, 0), 0)),
                  pl.BlockSpec((None, CW_ROWS, BW), lambda i: (l, 0, 0)),
                  pl.BlockSpec((None, V_ROWS, BW), lambda i: (l, 0, 0)),
                  pl.BlockSpec((None, BW, BW), lambda i: (l, 0, 0)),
                  pl.BlockSpec((None, BW, BW), lambda i: (l, 0, 0))],
        out_specs=[pl.BlockSpec((t, 3 * BW), lambda i: (i, 0)), pl.BlockSpec((t, BW), lambda i: (i, 0))],
        out_shape=[_sds((s, 3 * BW), BF16), _sds((s, BW), F32)],
        scratch_shapes=[pltpu.VMEM((t + HALO, BW), F32)] * 3 + [pltpu.VMEM((7, t + HALO - 8, BW), F32)]
        + [pltpu.VMEM((t, BW), F32)] * 2 + [pltpu.VMEM((1, BW), F32)],
        name=f"fwd_branch{l}")


GRP = N_HEADS // N_KV


def _attn_mask_bias(first_block):
    shape = (GRP * ATT_BLK, 2 * ATT_BLK)
    qi = lax.broadcasted_iota(jnp.int32, shape, 0) & (ATT_BLK - 1)
    ki = lax.broadcasted_iota(jnp.int32, shape, 1)
    dist = qi + ATT_BLK - ki
    valid = (dist >= 0) & (dist < ATT_BLK) & (jnp.logical_not(first_block) | (ki >= ATT_BLK))
    return dist.astype(F32), valid


def _per_head(hk, values):
    hl = lax.broadcasted_iota(jnp.int32, (GRP * ATT_BLK, 1), 0) // ATT_BLK
    out = values[GRP - 1]
    for j in range(GRP - 2, -1, -1):
        out = jnp.where(hl == j, values[j], out)
    return out


def _attn_probs(q_ref, kvp_ref, kvc_ref, vec_ref, distf, valid):
    kvs = range(N_KV)
    heads = [range(hk * GRP, (hk + 1) * GRP) for hk in kvs]
    q4 = [jnp.concatenate([q_ref[:, h * HD:(h + 1) * HD] for h in heads[hk]], axis=0) for hk in kvs]
    k2 = [jnp.concatenate([kvp_ref[:, hk * HD:(hk + 1) * HD], kvc_ref[:, hk * HD:(hk + 1) * HD]], axis=0) for hk in kvs]
    v2 = [jnp.concatenate([kvp_ref[:, (N_KV + hk) * HD:(N_KV + hk + 1) * HD],
                           kvc_ref[:, (N_KV + hk) * HD:(N_KV + hk + 1) * HD]], axis=0) for hk in kvs]
    slope = [_per_head(hk, [2.0 ** (-8.0 * (h + 1) / N_HEADS) for h in heads[hk]]) for hk in kvs]
    sink = [_per_head(hk, [vec_ref[V_SINK:V_SINK + 1, h:h + 1] for h in heads[hk]]) for hk in kvs]
    sc = [_dot(q4[hk], k2[hk], _NT) for hk in kvs]
    sc = [jnp.where(valid, sc[hk] * (HD ** -0.5) - slope[hk] * distf, NEG_INF) for hk in kvs]
    m = [jnp.maximum(jnp.max(sc[hk], axis=-1, keepdims=True), sink[hk]) for hk in kvs]
    p = [jnp.exp(sc[hk] - m[hk]) for hk in kvs]
    es = [jnp.exp(sink[hk] - m[hk]) for hk in kvs]
    inv = [1.0 / (jnp.sum(p[hk], axis=-1, keepdims=True) + es[hk]) for hk in kvs]
    return [(q4[hk], k2[hk], v2[hk], p[hk] * inv[hk], es[hk] * inv[hk]) for hk in kvs]


def fwd_attn(proj, vecs, l, comm=None):
    s = proj.shape[0]
    nb = s // ATT_BLK

    def body(q_ref, kvp_ref, kvc_ref, vec_ref, o_ref):
        distf, valid = _attn_mask_bias(pl.program_id(0) == 0)
        groups = _attn_probs(q_ref, kvp_ref, kvc_ref, vec_ref, distf, valid)
        outs = [_dot(p, v2, _NN).astype(BF16) for _, _, v2, p, _ in groups]
        for hk, out in enumerate(outs):
            for j in range(GRP):
                h = hk * GRP + j
                o_ref[:, h * HD:(h + 1) * HD] = out[j * ATT_BLK:(j + 1) * ATT_BLK]

    return _call(
        body, comm, (proj, proj, proj, vecs), grid=(nb,),
        in_specs=[pl.BlockSpec((ATT_BLK, BW), lambda i: (i, C_Q // BW)),
                  pl.BlockSpec((ATT_BLK, 256), lambda i: (jnp.maximum(i - 1, 0), C_K // 256)),
                  pl.BlockSpec((ATT_BLK, 256), lambda i: (i, C_K // 256)),
                  pl.BlockSpec((None, V_ROWS, BW), lambda i: (l, 0, 0))],
        out_specs=pl.BlockSpec((ATT_BLK, BW), lambda i: (i, 0)),
        out_shape=_sds((s, BW), BF16), name=f"fwd_attn{l}")


def fwd_merge(x, proj, pre_abd, pre_c, wt_a, wt_b, wt_c, wt_d, w_o, l, comm=None):
    s = x.shape[0]
    tm = min(256, s)

    def body(x_ref, gl_ref, pabd_ref, pc_ref, wa_ref, wb_ref, wc_ref, wd_ref, wo_ref, y_ref, mg_ref, h1_ref):
        pres = (pabd_ref[:, 0:BW], pabd_ref[:, BW:2 * BW], pc_ref[...], pabd_ref[:, 2 * BW:3 * BW])
        merged = jnp.zeros((tm, D), F32)
        for k, (pre, w_ref) in enumerate(zip(pres, (wa_ref, wb_ref, wc_ref, wd_ref))):
            yk = _dot(pre, w_ref[...], _NT)
            y_ref[:, k * D:(k + 1) * D] = yk.astype(BF16)
            merged = merged + _sigmoid(gl_ref[:, k * D:(k + 1) * D].astype(F32)) * yk
        mg_ref[...] = merged.astype(BF16)
        h1_ref[...] = x_ref[...] + _dot(merged, wo_ref[...], _NN)

    wspec = pl.BlockSpec((D, BW), lambda i: (0, 0))
    return _call(
        body, comm, (x, proj, pre_abd, pre_c, wt_a, wt_b, wt_c, wt_d, w_o), grid=(s // tm,),
        in_specs=[pl.BlockSpec((tm, D), lambda i: (i, 0)),
                  pl.BlockSpec((E(tm), E(4 * D)), lambda i: (i * tm, GL0)),
                  pl.BlockSpec((tm, 3 * BW), lambda i: (i, 0)),
                  pl.BlockSpec((tm, BW), lambda i: (i, 0)),
                  wspec, wspec, wspec, wspec,
                  pl.BlockSpec((D, D), lambda i: (0, 0))],
        out_specs=[pl.BlockSpec((tm, 4 * D), lambda i: (i, 0)), pl.BlockSpec((tm, D), lambda i: (i, 0)),
                   pl.BlockSpec((tm, D), lambda i: (i, 0))],
        out_shape=[_sds((s, 4 * D), BF16), _sds((s, D), BF16), _sds((s, D), F32)], name=f"fwd_merge{l}")


def fwd_ffn(h1, g2, wt_gate, wt_up, w_down, l, comm=None):
    s = h1.shape[0]
    tm = min(512, s)
    fc = FF // 2

    def body(h_ref, g_ref, wg_ref, wu_ref, wd_ref, xo_ref, fg_ref, fu_ref, hn_ref, acc_ref):
        j = pl.program_id(1)

        @pl.when(j == 0)
        def _():
            hv = h_ref[...]
            r = lax.rsqrt(jnp.mean(hv * hv, axis=-1, keepdims=True) + EPS)
            hn_ref[...] = (hv * r * g_ref[l:l + 1, :]).astype(BF16)
            acc_ref[...] = hv

        fg = _dot(hn_ref[...], wg_ref[...], _NT)
        fu = _dot(hn_ref[...], wu_ref[...], _NT)
        fg_ref[...] = fg.astype(BF16)
        fu_ref[...] = fu.astype(BF16)
        acc_ref[...] += _dot(fg * _sigmoid(fg) * fu, wd_ref[...], _NN)

        @pl.when(j == pl.num_programs(1) - 1)
        def _():
            xo_ref[...] = acc_ref[...]

    wspec = pl.BlockSpec((fc, D), lambda i, j: (j, 0))
    return _call(
        body, comm, (h1, g2, wt_gate, wt_up, w_down), grid=(s // tm, FF // fc),
        in_specs=[pl.BlockSpec((tm, D), lambda i, j: (i, 0)), pl.BlockSpec((DEPTH, D), lambda i, j: (0, 0)),
                  wspec, wspec, wspec],
        out_specs=[pl.BlockSpec((tm, D), lambda i, j: (i, 0)), pl.BlockSpec((tm, fc), lambda i, j: (i, j)),
                   pl.BlockSpec((tm, fc), lambda i, j: (i, j))],
        out_shape=[_sds((s, D), F32), _sds((s, FF), BF16), _sds((s, FF), BF16)],
        scratch_shapes=[pltpu.VMEM((tm, D), BF16), pltpu.VMEM((tm, D), F32)], name=f"fwd_ffn{l}")


def loss_head(x, gf, target):
    s = x.shape[0]
    tm = min(512, s)

    def body(x_ref, g_ref, t_ref, dx_ref, st_ref):
        @pl.when(pl.program_id(0) == 0)
        def _():
            st_ref[...] = jnp.zeros((8, D), F32)

        xv = x_ref[...]
        g = g_ref[...]
        r = lax.rsqrt(jnp.mean(xv * xv, axis=-1, keepdims=True) + EPS)
        n = xv * r
        err = n * g - t_ref[...]
        dy = err * (1.0 / D)
        dn = dy * g
        dx_ref[...] = r * (dn - n * jnp.mean(dn * n, axis=-1, keepdims=True))
        st_ref[0:1, :] += jnp.sum(dy * n, axis=0, keepdims=True)
        lsum = 0.5 * jnp.sum(jnp.mean(err * err, axis=-1, keepdims=True), axis=0, keepdims=True)
        st_ref[1:2, :] += jnp.broadcast_to(lsum, (1, D))

    return pl.pallas_call(
        body, grid=(s // tm,),
        in_specs=[pl.BlockSpec((tm, D), lambda i: (i, 0)), pl.BlockSpec((1, D), lambda i: (0, 0)),
                  pl.BlockSpec((tm, D), lambda i: (i, 0))],
        out_specs=[pl.BlockSpec((tm, D), lambda i: (i, 0)), pl.BlockSpec((8, D), lambda i: (0, 0))],
        out_shape=[_sds((s, D), F32), _sds((8, D), F32)],
        compiler_params=_cparams(1), name="loss_head")(x, gf, target)


def _edge_index(j, i, n_j, n_i):
    return jnp.where((j == 0) | (j == n_j - 1), i, n_i - 1)


def bwd_ffn(dxo, h1, fg, fu, g2, wt_gate, wt_up, w_down, l, comm=None):
    s = h1.shape[0]
    tm = min(512, s)
    fc = 256
    n_j, n_i = FF // fc, s // tm

    def body(dxo_ref, h_ref, fg_ref, fu_ref, g_ref, wg_ref, wu_ref, wd_ref,
             dh_ref, dwg_ref, dwu_ref, dwd_ref, st_ref, dhn, dxo_b, hn_b, ag, au, ad):
        j, i = pl.program_id(0), pl.program_id(1)
        rows = pl.ds(pl.multiple_of(i * tm, tm), tm)
        g = g_ref[l:l + 1, :]

        @pl.when(j == 0)
        def _():
            hv = h_ref[...]
            r = lax.rsqrt(jnp.mean(hv * hv, axis=-1, keepdims=True) + EPS)
            hn_b[rows, :] = (hv * r * g).astype(BF16)
            dxo_b[rows, :] = dxo_ref[...].astype(BF16)
            dhn[rows, :] = jnp.zeros((tm, D), F32)

        @pl.when((j == 0) & (i == 0))
        def _():
            st_ref[...] = jnp.zeros((8, D), F32)

        @pl.when(i == 0)
        def _():
            ag[...] = jnp.zeros((fc, D), F32)
            au[...] = jnp.zeros((fc, D), F32)
            ad[...] = jnp.zeros((fc, D), F32)

        fgv = fg_ref[...].astype(F32)
        fuv = fu_ref[...].astype(F32)
        sg = _sigmoid(fgv)
        sil = fgv * sg
        dxb = dxo_b[rows, :]
        hnb = hn_b[rows, :]
        d_act = _dot(dxb, wd_ref[...], _NT)
        ad[...] += _dot(sil * fuv, dxb, _TN)
        d_fg = (d_act * fuv * (sg * (1.0 + fgv * (1.0 - sg)))).astype(BF16)
        d_fu = (d_act * sil).astype(BF16)
        ag[...] += _dot(d_fg, hnb, _TN)
        au[...] += _dot(d_fu, hnb, _TN)
        dhn[rows, :] += _dot(d_fg, wg_ref[...], _NN) + _dot(d_fu, wu_ref[...], _NN)

        @pl.when(i == n_i - 1)
        def _():
            dwg_ref[...] = ag[...].astype(BF16)
            dwu_ref[...] = au[...].astype(BF16)
            dwd_ref[...] = ad[...].astype(BF16)

        @pl.when(j == n_j - 1)
        def _():
            hv = h_ref[...]
            r = lax.rsqrt(jnp.mean(hv * hv, axis=-1, keepdims=True) + EPS)
            n = hv * r
            dv = dhn[rows, :]
            dn = dv * g
            dh_ref[...] = dxo_ref[...] + r * (dn - n * jnp.mean(dn * n, axis=-1, keepdims=True))
            st_ref[0:1, :] += jnp.sum(dv * n, axis=0, keepdims=True)

    edge = lambda j, i: (_edge_index(j, i, n_j, n_i), 0)
    wspec = pl.BlockSpec((fc, D), lambda j, i: (j, 0))
    dwspec = pl.BlockSpec((fc, D), lambda j, i: (j, 0))
    return _call(
        body, comm, (dxo, h1, fg, fu, g2, wt_gate, wt_up, w_down), grid=(n_j, n_i),
        in_specs=[pl.BlockSpec((tm, D), edge),
                  pl.BlockSpec((tm, D), edge),
                  pl.BlockSpec((tm, fc), lambda j, i: (i, j)), pl.BlockSpec((tm, fc), lambda j, i: (i, j)),
                  pl.BlockSpec((DEPTH, D), lambda j, i: (0, 0)), wspec, wspec, wspec],
        out_specs=[pl.BlockSpec((tm, D), lambda j, i: (jnp.where(j == n_j - 1, i, 0), 0)),
                   dwspec, dwspec, dwspec, pl.BlockSpec((8, D), lambda j, i: (0, 0))],
        out_shape=[_sds((s, D), F32), _sds((FF, D), BF16), _sds((FF, D), BF16), _sds((FF, D), BF16), _sds((8, D), F32)],
        scratch_shapes=[pltpu.VMEM((s, D), F32), pltpu.VMEM((s, D), BF16), pltpu.VMEM((s, D), BF16),
                        pltpu.VMEM((fc, D), F32), pltpu.VMEM((fc, D), F32), pltpu.VMEM((fc, D), F32)],
        name=f"bwd_ffn{l}")


def bwd_merge(dh1, y4, proj, merged, pre_abd, pre_c, wt_a, wt_b, wt_c, wt_d, w_o, l, comm=None):
    s = dh1.shape[0]
    tm = min(256, s)
    n_i = s // tm

    def body(dh_ref, y_ref, gl_ref, mg_ref, pabd_ref, pc_ref, wa_ref, wb_ref, wc_ref, wd_ref, wo_ref,
             dgl_ref, dpre_ref, dwo_ref, dwa_ref, dwb_ref, dwc_ref, dwd_ref, ao, aa, ab, ac, ad):
        i = pl.program_id(0)
        accs = (aa, ab, ac, ad)

        @pl.when(i == 0)
        def _():
            ao[...] = jnp.zeros((D, D), F32)
            for acc in accs:
                acc[...] = jnp.zeros((D, BW), F32)

        dhb = dh_ref[...].astype(BF16)
        dmg = _dot(dhb, wo_ref[...], _NT)
        ao[...] += _dot(mg_ref[...], dhb, _TN)
        pres = (pabd_ref[:, 0:BW], pabd_ref[:, BW:2 * BW], pc_ref[...], pabd_ref[:, 2 * BW:3 * BW])
        for k, (pre, w_ref, acc) in enumerate(zip(pres, (wa_ref, wb_ref, wc_ref, wd_ref), accs)):
            gk = _sigmoid(gl_ref[:, k * D:(k + 1) * D].astype(F32))
            yk = y_ref[:, k * D:(k + 1) * D].astype(F32)
            dgl_ref[:, k * D:(k + 1) * D] = (dmg * yk * gk * (1.0 - gk)).astype(BF16)
            dyk = (dmg * gk).astype(BF16)
            dpre_ref[:, k * BW:(k + 1) * BW] = _dot(dyk, w_ref[...], _NN).astype(BF16)
            acc[...] += _dot(dyk, pre, _TN)

        @pl.when(i == n_i - 1)
        def _():
            dwo_ref[...] = ao[...].astype(BF16)
            for o_ref, acc in zip((dwa_ref, dwb_ref, dwc_ref, dwd_ref), accs):
                o_ref[...] = acc[...].astype(BF16)

    wspec = pl.BlockSpec((D, BW), lambda i: (0, 0))
    dwspec = pl.BlockSpec((D, BW), lambda i: (0, 0))
    return _call(
        body, comm, (dh1, y4, proj, merged, pre_abd, pre_c, wt_a, wt_b, wt_c, wt_d, w_o), grid=(n_i,),
        in_specs=[pl.BlockSpec((tm, D), lambda i: (i, 0)),
                  pl.BlockSpec((tm, 4 * D), lambda i: (i, 0)),
                  pl.BlockSpec((E(tm), E(4 * D)), lambda i: (i * tm, GL0)),
                  pl.BlockSpec((tm, D), lambda i: (i, 0)),
                  pl.BlockSpec((tm, 3 * BW), lambda i: (i, 0)),
                  pl.BlockSpec((tm, BW), lambda i: (i, 0)),
                  wspec, wspec, wspec, wspec,
                  pl.BlockSpec((D, D), lambda i: (0, 0))],
        out_specs=[pl.BlockSpec((E(tm), E(4 * D)), lambda i: (i * tm, GL0)),
                   pl.BlockSpec((tm, 4 * BW), lambda i: (i, 0)),
                   pl.BlockSpec((D, D), lambda i: (0, 0)), dwspec, dwspec, dwspec, dwspec],
        out_shape=[_sds((s, IN_W), BF16), _sds((s, 4 * BW), BF16), _sds((D, D), BF16)] + [_sds((D, BW), BF16)] * 4,
        scratch_shapes=[pltpu.VMEM((D, D), F32)] + [pltpu.VMEM((D, BW), F32)] * 4, name=f"bwd_merge{l}")


def bwd_attn(proj, dpre, vecs, l, comm=None):
    s = proj.shape[0]
    nb = s // ATT_BLK
    grp = N_HEADS // N_KV

    def body(q_ref, kvp_ref, kvc_ref, do_ref, vec_ref, dq_ref, dkc_ref, dkp_ref, st_ref):
        @pl.when(pl.program_id(0) == 0)
        def _():
            st_ref[...] = jnp.zeros((8, 128), F32)

        distf, valid = _attn_mask_bias(pl.program_id(0) == 0)
        lane = lax.broadcasted_iota(jnp.int32, (1, 128), 1)
        dsink = jnp.zeros((1, 128), F32)
        groups = _attn_probs(q_ref, kvp_ref, kvc_ref, vec_ref, distf, valid)
        kvs = range(N_KV)
        do4s = [jnp.concatenate([do_ref[:, h * HD:(h + 1) * HD] for h in range(hk * grp, (hk + 1) * grp)], axis=0) for hk in kvs]
        dps = [_dot(do4s[hk], groups[hk][2], _NT) for hk in kvs]
        deltas = [jnp.sum(groups[hk][3] * dps[hk], axis=-1, keepdims=True) for hk in kvs]
        dss = [groups[hk][3] * (dps[hk] - deltas[hk]) * (HD ** -0.5) for hk in kvs]
        for hk in kvs:
            q4, k2, v2, p, ps = groups[hk]
            do4, delta, ds = do4s[hk], deltas[hk], dss[hk]
            dq4 = _dot(ds, k2, _NN).astype(BF16)
            dk2 = _dot(ds, q4, _TN)
            dv2 = _dot(p, do4, _TN)
            psd = ps * delta
            for j in range(grp):
                h = hk * grp + j
                rows = slice(j * ATT_BLK, (j + 1) * ATT_BLK)
                dq_ref[:, h * HD:(h + 1) * HD] = dq4[rows]
                dsink = dsink + jnp.where(lane == h, -jnp.sum(psd[rows], axis=0, keepdims=True), 0.0)
            dkp_ref[:, hk * HD:(hk + 1) * HD] = dk2[0:ATT_BLK].astype(BF16)
            dkc_ref[:, hk * HD:(hk + 1) * HD] = dk2[ATT_BLK:].astype(BF16)
            dkp_ref[:, (N_KV + hk) * HD:(N_KV + hk + 1) * HD] = dv2[0:ATT_BLK].astype(BF16)
            dkc_ref[:, (N_KV + hk) * HD:(N_KV + hk + 1) * HD] = dv2[ATT_BLK:].astype(BF16)
        st_ref[0:1, :] += dsink

    return _call(
        body, comm, (proj, proj, proj, dpre, vecs), grid=(nb,),
        in_specs=[pl.BlockSpec((ATT_BLK, BW), lambda i: (i, C_Q // BW)),
                  pl.BlockSpec((ATT_BLK, 256), lambda i: (jnp.maximum(i - 1, 0), C_K // 256)),
                  pl.BlockSpec((ATT_BLK, 256), lambda i: (i, C_K // 256)),
                  pl.BlockSpec((ATT_BLK, BW), lambda i: (i, 2)),
                  pl.BlockSpec((None, V_ROWS, BW), lambda i: (l, 0, 0))],
        out_specs=[pl.BlockSpec((ATT_BLK, BW), lambda i: (i, 0)), pl.BlockSpec((ATT_BLK, 256), lambda i: (i, 0)),
                   pl.BlockSpec((ATT_BLK, 256), lambda i: (i, 0)), pl.BlockSpec((8, 128), lambda i: (0, 0))],
        out_shape=[_sds((s, BW), BF16), _sds((s, 256), BF16), _sds((s, 256), BF16), _sds((8, 128), F32)],
        name=f"bwd_attn{l}")


def bwd_branch(proj, dproj, dpre, h, dq, dkc, dkp, convw, vecs, wx_bd, wa_bd, l, comm=None):
    s = proj.shape[0]
    t = 2 * ATT_BLK
    nt = s // t
    nb = s // ATT_BLK
    hb = t // HALO

    def body(cur_ref, halo_ref, dpre_ref, h_ref, hp_ref, dq_ref, dkc_ref, dkp1_ref, dkp2_ref,
             cw_ref, vec_ref, wx_ref, wa_ref, dproj_in, dp_ref, dcw_ref, dvec_ref, dwx_ref, dwa_ref,
             bufa, bufb, bufd, xd, xg, a_ext, hbuf, b_s, g_s, dh_s, ga, gb, gd, dhcar):
        del dproj_in
        step = pl.program_id(0)
        ti = nt - 1 - step
        first = ti == 0

        @pl.when(step == 0)
        def _():
            dcw_ref[...] = jnp.zeros((CW_ROWS, BW), F32)
            dvec_ref[...] = jnp.zeros((V_ROWS, BW), F32)
            dwx_ref[...] = jnp.zeros((BW, BW), F32)
            dwa_ref[...] = jnp.zeros((BW, BW), F32)
            dhcar[...] = jnp.zeros((1, BW), F32)
            a_ext[t:t + 8, :] = jnp.zeros((8, BW), F32)
            ga[t:t + 8, :] = jnp.zeros((8, BW), F32)
            gb[t:t + 8, :] = jnp.zeros((8, BW), F32)
            gd[t:t + HALO, :] = jnp.zeros((HALO, BW), F32)

        def cur(c0):
            return cur_ref[:, c0:c0 + BW].astype(F32)

        def rsum(v):
            return jnp.sum(v, axis=0, keepdims=True)

        def put(c0, v):
            dp_ref[:, c0:c0 + BW] = v.astype(BF16)

        v = _branch_fwd_math(cur_ref, halo_ref, cw_ref, vec_ref, wx_ref, wa_ref, bufa, bufb, bufd, xd, first, t)
        ca, gi, gr, sp, a, mult = v["ca"], v["gi"], v["gr"], v["sp"], v["a"], v["mult"]
        dpa = dpre_ref[:, 0:BW].astype(F32)
        gg, dgg = _gelu_and_grad(cur(C_AG))
        hv = h_ref[...]
        put(C_AG, dpa * hv * dgg)
        a_ext[0:t, :] = a
        b_s[...] = a_ext[pl.ds(1, t), :]
        g_s[...] = dpa * gg
        dhcar[...] = _scan_bwd(b_s, g_s, dh_s, dhcar[...], t)
        a_ext[t:t + 1, :] = a[0:1, :]
        dh = dh_s[...]
        hbuf[0:8, :] = jnp.where(first, 0.0, hp_ref[...])
        hbuf[8:8 + t, :] = hv
        da = dh * hbuf[pl.ds(7, t), :]
        d_ca = dh * gi * mult
        d_gi = dh * ca * mult
        d_mult = dh * ca * gi
        d_la = da * a - d_mult * (a * a) / mult
        lam = vec_ref[V_LAM:V_LAM + 1, :]
        dvec_ref[V_LAM:V_LAM + 1, :] += rsum(d_la * gr) * (LRU_C * _sigmoid(-lam))
        d_gr = d_la * (-LRU_C * sp)
        d_zr = d_gr * gr * (1.0 - gr)
        d_zi = d_gi * gi * (1.0 - gi)
        dvec_ref[V_BA:V_BA + 1, :] += rsum(d_zr)
        dvec_ref[V_BX:V_BX + 1, :] += rsum(d_zi)
        dwa_ref[...] += _dot(ca, d_zr, _TN)
        dwx_ref[...] += _dot(ca, d_zi, _TN)
        d_ca = d_ca + _dot(d_zi, wx_ref[...], _NT) + _dot(d_zr, wa_ref[...], _NT)
        dvec_ref[V_CAB:V_CAB + 1, :] += rsum(d_ca)
        ga[0:t, :] = d_ca
        d_ax = jnp.zeros((t, BW), F32)
        for k in range(CONV_A):
            d_ax = d_ax + cw_ref[CW_A + k:CW_A + k + 1, :] * ga[pl.ds(CONV_A - 1 - k, t), :]
            dcw_ref[CW_A + k:CW_A + k + 1, :] += rsum(d_ca * bufa[pl.ds(HALO - (CONV_A - 1) + k, t), :])
        ga[t:t + 8, :] = d_ca[0:8, :]
        put(C_AX, d_ax)
        dpb = dpre_ref[:, BW:2 * BW].astype(F32)
        put(C_BB, dpb * v["cb"])
        d_cb = dpb * cur(C_BB)
        gb[0:t, :] = d_cb
        d_cbin = jnp.zeros((t, BW), F32)
        for k in range(CONV_B):
            d_cbin = d_cbin + cw_ref[CW_B + k:CW_B + k + 1, :] * gb[pl.ds(CONV_B - 1 - k, t), :]
            dcw_ref[CW_B + k:CW_B + k + 1, :] += rsum(d_cb * bufb[pl.ds(HALO - (CONV_B - 1) + k, t), :])
        gb[t:t + 8, :] = d_cb[0:8, :]
        put(C_BC, d_cbin * cur(C_BV))
        put(C_BV, d_cbin * cur(C_BC))
        dpd = dpre_ref[:, 3 * BW:4 * BW].astype(F32)
        ln, xh, rstd, s2 = v["ln"], v["xh"], v["rstd"], v["s2"]
        sg = _sigmoid(ln)
        d_ln = dpd * sg * (1.0 + ln * (1.0 - sg))
        dvec_ref[V_LNG:V_LNG + 1, :] += rsum(d_ln * xh)
        dvec_ref[V_LNB:V_LNB + 1, :] += rsum(d_ln)
        d_xh = d_ln * vec_ref[V_LNG:V_LNG + 1, :]
        d_cd = rstd * (d_xh - jnp.mean(d_xh, axis=-1, keepdims=True)
                       - xh * jnp.mean(d_xh * xh, axis=-1, keepdims=True))
        dvec_ref[V_CDB:V_CDB + 1, :] += rsum(d_cd)
        gd[0:t, :] = d_cd
        _shifted_copies(gd, xg, t + HALO)
        d_dg = jnp.zeros((t, BW), F32)
        for k in range(CONV_D):
            d_dg = d_dg + cw_ref[CW_D + k:CW_D + k + 1, :] * _window(gd, xg, CONV_D - 1 - k, t)
            dcw_ref[CW_D + k:CW_D + k + 1, :] += rsum(d_cd * _window(bufd, xd, HALO - (CONV_D - 1) + k, t))
        gd[t:t + HALO, :] = d_cd[0:HALO, :]
        put(C_D1, d_dg * s2)
        put(C_D2, d_dg * cur(C_D1) * s2 * (1.0 - s2))
        dp_ref[:, C_Q:C_Q + BW] = dq_ref[...]
        dkp2 = jnp.where(step == 0, 0.0, dkp2_ref[...].astype(F32))
        dp_ref[0:ATT_BLK, C_K:C_K + 256] = (dkc_ref[0:ATT_BLK, :].astype(F32) + dkp1_ref[...].astype(F32)).astype(BF16)
        dp_ref[ATT_BLK:t, C_K:C_K + 256] = (dkc_ref[ATT_BLK:t, :].astype(F32) + dkp2).astype(BF16)

    rev = lambda i: nt - 1 - i
    full = lambda r, c: pl.BlockSpec((r, c), lambda i: (0, 0))
    return _call(
        body, comm, (proj, proj, dpre, h, h, dq, dkc, dkp, dkp, convw, vecs, wx_bd, wa_bd, dproj), grid=(nt,),
        in_specs=[pl.BlockSpec((t, GL0), lambda i: (rev(i), 0)),
                  pl.BlockSpec((HALO, GL0), lambda i: (jnp.maximum(rev(i) * hb - 1, 0), 0)),
                  pl.BlockSpec((t, 4 * BW), lambda i: (rev(i), 0)),
                  pl.BlockSpec((t, BW), lambda i: (rev(i), 0)),
                  pl.BlockSpec((8, BW), lambda i: (jnp.maximum(rev(i) * (t // 8) - 1, 0), 0)),
                  pl.BlockSpec((t, BW), lambda i: (rev(i), 0)),
                  pl.BlockSpec((t, 256), lambda i: (rev(i), 0)),
                  pl.BlockSpec((ATT_BLK, 256), lambda i: (2 * rev(i) + 1, 0)),
                  pl.BlockSpec((ATT_BLK, 256), lambda i: (jnp.minimum(2 * rev(i) + 2, nb - 1), 0)),
                  pl.BlockSpec((None, CW_ROWS, BW), lambda i: (l, 0, 0)),
                  pl.BlockSpec((None, V_ROWS, BW), lambda i: (l, 0, 0)),
                  pl.BlockSpec((None, BW, BW), lambda i: (l, 0, 0)),
                  pl.BlockSpec((None, BW, BW), lambda i: (l, 0, 0)),
                  pl.BlockSpec(memory_space=pl.ANY)],
        out_specs=[pl.BlockSpec((t, GL0), lambda i: (rev(i), 0)),
                   full(CW_ROWS, BW), full(V_ROWS, BW), full(BW, BW), full(BW, BW)],
        out_shape=[_sds((s, IN_W), BF16), _sds((CW_ROWS, BW), F32), _sds((V_ROWS, BW), F32),
                   _sds((BW, BW), F32), _sds((BW, BW), F32)],
        scratch_shapes=[pltpu.VMEM((t + HALO, BW), F32)] * 3 + [pltpu.VMEM((7, t + HALO - 8, BW), F32)] * 2
        + [pltpu.VMEM((t + 8, BW), F32), pltpu.VMEM((t + 8, BW), F32)]
        + [pltpu.VMEM((t, BW), F32)] * 3
        + [pltpu.VMEM((t + 8, BW), F32), pltpu.VMEM((t + 8, BW), F32), pltpu.VMEM((t + HALO, BW), F32),
           pltpu.VMEM((1, BW), F32)],
        aliases={13: 0}, name=f"bwd_branch{l}")


def bwd_proj(dproj, x, dh1, g1, wt_in, l, comm=None):
    s = x.shape[0]
    tm = min(512, s)
    ck = 1408
    n_j, n_i = IN_W // ck, s // tm

    def body(dp_ref, x_ref, dh_ref, g_ref, w_ref, dx_ref, dw_ref, st_ref, dxn, xn_b, acc):
        j, i = pl.program_id(0), pl.program_id(1)
        rows = pl.ds(pl.multiple_of(i * tm, tm), tm)
        g = g_ref[l:l + 1, :]

        @pl.when(j == 0)
        def _():
            xv = x_ref[...]
            r = lax.rsqrt(jnp.mean(xv * xv, axis=-1, keepdims=True) + EPS)
            xn_b[rows, :] = (xv * r * g).astype(BF16)
            dxn[rows, :] = jnp.zeros((tm, D), F32)

        @pl.when((j == 0) & (i == 0))
        def _():
            st_ref[...] = jnp.zeros((8, D), F32)

        @pl.when(i == 0)
        def _():
            acc[...] = jnp.zeros((ck, D), F32)

        dp = dp_ref[...]
        dxn[rows, :] += _dot(dp, w_ref[...], _NN)
        acc[...] += _dot(dp, xn_b[rows, :], _TN)

        @pl.when(i == n_i - 1)
        def _():
            dw_ref[...] = acc[...].astype(BF16)

        @pl.when(j == n_j - 1)
        def _():
            xv = x_ref[...]
            r = lax.rsqrt(jnp.mean(xv * xv, axis=-1, keepdims=True) + EPS)
            n = xv * r
            dv = dxn[rows, :]
            dn = dv * g
            dx_ref[...] = dh_ref[...] + r * (dn - n * jnp.mean(dn * n, axis=-1, keepdims=True))
            st_ref[0:1, :] += jnp.sum(dv * n, axis=0, keepdims=True)

    lastrow = lambda j, i: (jnp.where(j == n_j - 1, i, 0), 0)
    return _call(
        body, comm, (dproj, x, dh1, g1, wt_in), grid=(n_j, n_i),
        in_specs=[pl.BlockSpec((tm, ck), lambda j, i: (i, j)),
                  pl.BlockSpec((tm, D), lambda j, i: (_edge_index(j, i, n_j, n_i), 0)),
                  pl.BlockSpec((tm, D), lastrow),
                  pl.BlockSpec((DEPTH, D), lambda j, i: (0, 0)),
                  pl.BlockSpec((ck, D), lambda j, i: (j, 0))],
        out_specs=[pl.BlockSpec((tm, D), lastrow), pl.BlockSpec((ck, D), lambda j, i: (j, 0)),
                   pl.BlockSpec((8, D), lambda j, i: (0, 0))],
        out_shape=[_sds((s, D), F32), _sds((IN_W, D), BF16), _sds((8, D), F32)],
        scratch_shapes=[pltpu.VMEM((s, D), F32), pltpu.VMEM((s, D), BF16), pltpu.VMEM((ck, D), F32)],
        name=f"bwd_proj{l}")


def bwd_proj_w(dproj, x, g1, l, half, comm=None):
    s = x.shape[0]
    tm = min(1024, s)
    ck = 1408
    c0, hw = W_IN_PARTS[half]
    n_j, n_i = IN_W // ck, s // tm

    def body(dp_ref, x_ref, g_ref, dw_ref, xn_b, acc):
        j, i = pl.program_id(0), pl.program_id(1)
        rows = pl.ds(pl.multiple_of(i * tm, tm), tm)

        @pl.when(j == 0)
        def _():
            xv = x_ref[...]
            r = lax.rsqrt(jnp.mean(xv * xv, axis=-1, keepdims=True) + EPS)
            xn_b[rows, :] = (xv * r * g_ref[l:l + 1, :])[:, c0:c0 + hw].astype(BF16)

        @pl.when(i == 0)
        def _():
            acc[...] = jnp.zeros((ck, hw), F32)

        acc[...] += _dot(dp_ref[...], xn_b[rows, :], _TN)

        @pl.when(i == n_i - 1)
        def _():
            dw_ref[...] = acc[...].astype(BF16)

    return _call(
        body, comm, (dproj, x, g1), grid=(n_j, n_i),
        in_specs=[pl.BlockSpec((tm, ck), lambda j, i: (i, j)),
                  pl.BlockSpec((tm, D), lambda j, i: (jnp.where(j == 0, i, n_i - 1), 0)),
                  pl.BlockSpec((DEPTH, D), lambda j, i: (0, 0))],
        out_specs=pl.BlockSpec((ck, hw), lambda j, i: (j, 0)),
        out_shape=_sds((IN_W, hw), BF16),
        scratch_shapes=[pltpu.VMEM((s, hw), BF16), pltpu.VMEM((ck, hw), F32)],
        name=f"bwd_proj_w{half}_{l}")


def bwd_proj_x(dproj, x, dh1, g1, wt_in, l, comm=None):
    s = x.shape[0]
    tm = min(512, s)
    ck = 1408
    n_j, n_i = IN_W // ck, s // tm

    def body(dp_ref, x_ref, dh_ref, g_ref, w_ref, dx_ref, st_ref, dxn):
        j, i = pl.program_id(0), pl.program_id(1)
        rows = pl.ds(pl.multiple_of(i * tm, tm), tm)
        g = g_ref[l:l + 1, :]

        @pl.when((j == 0) & (i == 0))
        def _():
            st_ref[...] = jnp.zeros((8, D), F32)

        part = _dot(dp_ref[...], w_ref[...], _NN)

        @pl.when(j == 0)
        def _():
            dxn[rows, :] = part

        @pl.when(j > 0)
        def _():
            dxn[rows, :] += part

        @pl.when(j == n_j - 1)
        def _():
            xv = x_ref[...]
            r = lax.rsqrt(jnp.mean(xv * xv, axis=-1, keepdims=True) + EPS)
            n = xv * r
            dv = dxn[rows, :]
            dn = dv * g
            dx_ref[...] = dh_ref[...] + r * (dn - n * jnp.mean(dn * n, axis=-1, keepdims=True))
            st_ref[0:1, :] += jnp.sum(dv * n, axis=0, keepdims=True)

    lastrow = lambda j, i: (jnp.where(j == n_j - 1, i, 0), 0)
    return _call(
        body, comm, (dproj, x, dh1, g1, wt_in), grid=(n_j, n_i),
        in_specs=[pl.BlockSpec((tm, ck), lambda j, i: (i, j)), pl.BlockSpec((tm, D), lastrow),
                  pl.BlockSpec((tm, D), lastrow),
                  pl.BlockSpec((DEPTH, D), lambda j, i: (0, 0)), pl.BlockSpec((ck, D), lambda j, i: (j, 0))],
        out_specs=[pl.BlockSpec((tm, D), lastrow), pl.BlockSpec((8, D), lambda j, i: (0, 0))],
        out_shape=[_sds((s, D), F32), _sds((8, D), F32)],
        scratch_shapes=[pltpu.VMEM((s, D), F32)], name=f"bwd_proj_x{l}")


def _block_diag(w):
    nl, nb, bw, _ = w.shape
    eye = jnp.eye(nb, dtype=w.dtype)
    return jnp.einsum("lhij,hk->lhikj", w, eye).reshape(nl, nb * bw, nb * bw).astype(BF16)


class NoOverlap:
    def __init__(self, big):
        self.big = big

    def weights(self, l):
        return self.big[l]

    def job(self, slot, l):
        return None

    def done(self, slot, l, results):
        pass

    def new_grads(self, group, l, grads):
        pass

    def new_small(self, l, arrays, head_stats):
        pass


def local_step(x, target, norm1_g, norm2_g, final_g, convw, vecs, lru_wx, lru_wa, plan):
    wx_bd, wa_bd = _block_diag(lru_wx), _block_diag(lru_wa)

    def run(fn, slot, l, *args):
        res, cres = fn(*args, l, comm=plan.job(slot, l))
        plan.done(slot, l, cres)
        return res

    saved = []
    for l in range(DEPTH):
        proj = run(fwd_proj, "fwd_proj", l, x, norm1_g, plan.weights(l)["in_t"])
        pre_abd, h = run(fwd_branch, "fwd_branch", l, proj, convw, vecs, wx_bd, wa_bd)
        pre_c = run(fwd_attn, "fwd_attn", l, proj, vecs)
        w = plan.weights(l)
        y4, merged, h1 = run(fwd_merge, "fwd_merge", l, x, proj, pre_abd, pre_c, w["a_t"], w["b_t"], w["c_t"], w["d_t"], w["o"])
        w = plan.weights(l)
        x_out, fg, fu = run(fwd_ffn, "fwd_ffn", l, h1, norm2_g, w["gate_t"], w["up_t"], w["down"])
        saved.append((x, proj, pre_abd, h, pre_c, y4, merged, h1, fg, fu))
        x = x_out
    dx, head_stats = loss_head(x, final_g.reshape(1, D), target)
    small = [None] * DEPTH
    for l in reversed(range(DEPTH)):
        x_in, proj, pre_abd, h, pre_c, y4, merged, h1, fg, fu = saved[l]
        w = plan.weights(l)
        dh1, d_gate, d_up, d_down, st_ffn = run(bwd_ffn, "bwd_ffn", l, dx, h1, fg, fu, norm2_g, w["gate_t"], w["up_t"], w["down"])
        plan.new_grads("ffn", l, dict(gate_t=d_gate, up_t=d_up, down=d_down))
        dproj, dpre, d_o, d_a, d_b, d_c, d_d = run(
            bwd_merge, "bwd_merge", l, dh1, y4, proj, merged, pre_abd, pre_c, w["a_t"], w["b_t"], w["c_t"], w["d_t"], w["o"])
        plan.new_grads("out", l, dict(a_t=d_a, b_t=d_b, c_t=d_c, d_t=d_d, o=d_o))
        dq, dkc, dkp, st_attn = run(bwd_attn, "bwd_attn", l, proj, dpre, vecs)
        dproj, dcw, dvec, dwx, dwa = run(bwd_branch, "bwd_branch", l, proj, dproj, dpre, h, dq, dkc, dkp, convw, vecs, wx_bd, wa_bd)
        if l > 0:
            dx, d_in, st_proj = run(bwd_proj, "bwd_proj", l, dproj, x_in, dh1, norm1_g, w["in_t"])
            plan.new_grads("in", l, dict(in_t=d_in))
        else:
            for half, name in enumerate(("in_a", "in_b")):
                d_half = run(functools.partial(bwd_proj_w, half=half), f"bwd_proj_w{half}", l, dproj, x_in, norm1_g)
                plan.new_grads(name, l, {name: d_half})
            dx, st_proj = run(bwd_proj_x, "bwd_proj_x", l, dproj, x_in, dh1, norm1_g, w["in_t"])
        small[l] = (st_proj, st_ffn, dvec, st_attn, dcw, dwx, dwa)
        plan.new_small(l, small[l], head_stats)
    return head_stats, dx, small


BIG = dict(in_t=("w_in", "view"), a_t=("w_a_out", "transpose"), b_t=("w_b_out", "transpose"), c_t=("w_c_out", "transpose"),
           d_t=("w_d_out", "transpose"), o=("w_o", "plain"), gate_t=("w_ffn_gate", "view"), up_t=("w_ffn_up", "view"),
           down=("w_ffn_down", "plain"))


def cast_transpose(w, name):
    nl, a, b = w.shape
    ta = min(256, a)

    def body(w_ref, o_ref):
        o_ref[...] = w_ref[...].T.astype(BF16)

    return pl.pallas_call(
        body, grid=(nl, a // ta),
        in_specs=[pl.BlockSpec((None, ta, b), lambda l, i: (l, i, 0))],
        out_specs=pl.BlockSpec((None, b, ta), lambda l, i: (l, 0, i)),
        out_shape=_sds((nl, b, a), BF16), compiler_params=_cparams(2), name=name)(w)


def add_partials(mine, recv, core, name):
    n = len(mine)

    def body(core_ref, *refs):
        del core_ref
        for a_ref, b_ref, o_ref in zip(refs[:n], refs[n:2 * n], refs[2 * n:]):
            o_ref[...] = (a_ref[...].astype(F32) + b_ref[...].astype(F32)).astype(BF16)

    return pl.pallas_call(
        body,
        grid_spec=pltpu.PrefetchScalarGridSpec(
            num_scalar_prefetch=1, grid=(4,),
            in_specs=[pl.BlockSpec((None, None) + a.shape[2:], lambda i, cr: (i, cr[0], 0, 0)) for a in mine]
            + [pl.BlockSpec((None,) + b.shape[1:], lambda i, cr: (i, 0, 0)) for b in recv],
            out_specs=[pl.BlockSpec((None,) + b.shape[1:], lambda i, cr: (i, 0, 0)) for b in recv]),
        out_shape=[_sds(b.shape, BF16) for b in recv], compiler_params=_cparams(1), name=name)(core, *mine, *recv)


def _adamw(w, g, m, v):
    m = ADAM_B1 * m + (1.0 - ADAM_B1) * g
    v = ADAM_B2 * v + (1.0 - ADAM_B2) * (g * g)
    m_hat = m / (1.0 - ADAM_B1 ** ADAM_STEP)
    v_hat = v / (1.0 - ADAM_B2 ** ADAM_STEP)
    delta = -ADAM_LR * (m_hat / (jnp.sqrt(v_hat) + ADAM_EPS) + ADAM_WD * w)
    return delta, m, v


def adamw_big(contrib, w, m, v, transposed, name, comm=None):
    nsrc, nl, rows, cols = contrib.shape
    ct = 256

    def body(c_ref, w_ref, m_ref, v_ref, g_out, d_out, m_out, v_out):
        g = c_ref[0].astype(F32)
        for src in range(1, nsrc):
            g = g + c_ref[src].astype(F32)
        if transposed:
            g = g.T
        delta, mn, vn = _adamw(w_ref[...], g, m_ref[...], v_ref[...])
        g_out[...] = g
        d_out[...] = delta
        m_out[...] = mn
        v_out[...] = vn

    if transposed:
        wspec = pl.BlockSpec((None, ct, rows), lambda l, j: (l, j, 0))
    else:
        wspec = pl.BlockSpec((None, rows, ct), lambda l, j: (l, 0, j))
    return _call(
        body, comm, (contrib, w, m, v), grid=(nl, cols // ct),
        in_specs=[pl.BlockSpec((nsrc, None, rows, ct), lambda l, j: (0, l, 0, j)), wspec, wspec, wspec],
        out_specs=[wspec] * 4, out_shape=[_sds(w.shape, F32)] * 4, name=name)


VEC_NAMES = ("conv_a_b", "lru_bx", "lru_ba", "lru_lambda", "conv_d_b", "ln_d_g", "ln_d_b")
P_N1, P_N2, P_VEC, P_CONV, P_LRU = 0, 1, 2, 6, 6 + CW_ROWS
P_FINAL, P_LOSS, P_ROWS = P_LRU + HD, P_LRU + HD + 1, P_LRU + HD + 2
SMALL = ("norm1_g", "conv_a_w", "conv_a_b", "lru_wx", "lru_bx", "lru_wa", "lru_ba", "lru_lambda", "conv_b_w", "sinks",
         "conv_d_w", "conv_d_b", "ln_d_g", "ln_d_b", "norm2_g", "final_g")
VMEM_FULL = pl.BlockSpec(memory_space=pltpu.VMEM)


def _stack_vecs(p):
    rows = [p[n] for n in VEC_NAMES] + [jnp.pad(p["sinks"], ((0, 0), (0, BW - N_HEADS)))]
    return jnp.stack(rows, axis=1)


def _stack_convs(p):
    nl, _, ch = p["conv_a_w"].shape
    z = jnp.zeros((nl, 1, ch), F32)
    return jnp.concatenate([p["conv_a_w"], p["conv_b_w"], z, p["conv_d_w"], z], axis=1)


def _vec_place(r):
    return P_VEC + r // 2, (r % 2) * BW


def pack_small(arrays, head_stats, l):
    n = len(arrays)

    def body(*refs):
        st_proj, st_ffn, dvec, st_attn, dcw, dwx, dwa = refs[:n]
        pack = refs[-1]
        pack[...] = jnp.zeros((P_ROWS, D), F32)
        lane = lax.broadcasted_iota(jnp.int32, (HD, BW), 1)
        pack[P_N1:P_N1 + 1, :] = st_proj[0:1, :]
        pack[P_N2:P_N2 + 1, :] = st_ffn[0:1, :]
        for r in range(len(VEC_NAMES)):
            row, c0 = _vec_place(r)
            pack[row:row + 1, c0:c0 + BW] = dvec[r:r + 1, :]
        row, c0 = _vec_place(V_SINK)
        pack[row:row + 1, c0:c0 + 128] = st_attn[0:1, :]
        pack[P_CONV:P_CONV + CW_ROWS, 0:BW] = dcw[...]
        for mat, c0 in ((dwx, 0), (dwa, BW)):
            blocks = jnp.zeros((HD, BW), F32)
            for h in range(BW // HD):
                blocks = jnp.where((lane >= HD * h) & (lane < HD * (h + 1)), mat[HD * h:HD * (h + 1), :], blocks)
            pack[P_LRU:P_LRU + HD, c0:c0 + BW] = blocks
        if head_stats is not None:
            pack[P_FINAL:P_LOSS + 1, :] = refs[n][0:2, :]

    flat = list(arrays) + ([] if head_stats is None else [head_stats])
    return pl.pallas_call(body, out_shape=_sds((P_ROWS, D), F32), in_specs=[VMEM_FULL] * len(flat), out_specs=VMEM_FULL,
                          name=f"pack_small{l}", compiler_params=pltpu.CompilerParams(vmem_limit_bytes=VMEM_LIMIT))(*flat)


def adamw_small(gathered, me, w, m, v):
    ns = len(SMALL)

    def body(me_ref, *refs):
        c_refs, refs = refs[:DEPTH], refs[DEPTH:]
        w_refs, m_refs, v_refs = refs[:ns], refs[ns:2 * ns], refs[2 * ns:3 * ns]
        loss_ref, outs, gs = refs[3 * ns], refs[3 * ns + 1:3 * ns + 1 + 4 * ns], refs[-1]
        for l in range(DEPTH):
            gs[l] = c_refs[l][0]
            for dev in range(1, NDEV):
                gs[l] += c_refs[l][dev]
        loss_ref[...] = gs[DEPTH - 1, P_LOSS:P_LOSS + 1, 0:128]

        def update(name, sel, g):
            i = SMALL.index(name)
            delta, mn, vn = _adamw(w_refs[i][sel], g, m_refs[i][sel], v_refs[i][sel])
            for o_ref, val in zip(outs[4 * i:4 * i + 4], (g, delta, mn, vn)):
                o_ref[sel] = val

        update("final_g", (slice(0, 1), slice(None)), gs[DEPTH - 1, P_FINAL:P_FINAL + 1, :])
        shift = (BW - me_ref[0] * (BW // NDEV)) & (BW - 1)
        for l in range(DEPTH):
            row = (slice(l, l + 1), slice(None))
            update("norm1_g", row, gs[l, P_N1:P_N1 + 1, :])
            update("norm2_g", row, gs[l, P_N2:P_N2 + 1, :])
            for r, name in enumerate(VEC_NAMES):
                prow, c0 = _vec_place(r)
                update(name, row, gs[l, prow:prow + 1, c0:c0 + BW])
            prow, c0 = _vec_place(V_SINK)
            update("sinks", row, gs[l, prow:prow + 1, c0:c0 + N_HEADS])
            mine = pltpu.roll(gs[l, P_CONV:P_CONV + CW_ROWS, 0:BW], shift, 1)[:, 0:BW // NDEV]
            update("conv_a_w", (l,), mine[CW_A:CW_A + CONV_A])
            update("conv_b_w", (l,), mine[CW_B:CW_B + CONV_B])
            update("conv_d_w", (l,), mine[CW_D:CW_D + CONV_D])
            for h in range(BW // HD):
                update("lru_wx", (l, h), gs[l, P_LRU:P_LRU + HD, HD * h:HD * (h + 1)])
                update("lru_wa", (l, h), gs[l, P_LRU:P_LRU + HD, BW + HD * h:BW + HD * (h + 1)])

    args = [p[n] for p in (w, m, v) for n in SMALL]
    full = lambda a: pl.BlockSpec(a.shape, lambda i, me_ref: (0,) * a.ndim)
    out_shape = [_sds((1, 128), F32)] + [_sds(w[n].shape, F32) for n in SMALL for _ in range(4)]
    outs = pl.pallas_call(
        body,
        grid_spec=pltpu.PrefetchScalarGridSpec(
            num_scalar_prefetch=1, grid=(1,),
            in_specs=[full(a) for a in list(gathered) + args], out_specs=[full(o) for o in out_shape],
            scratch_shapes=[pltpu.VMEM((DEPTH, P_ROWS, D), F32)]),
        out_shape=out_shape, name="adamw_small", compiler_params=_cparams(1))(me, *gathered, *args)
    return outs[0], {n: outs[1 + 4 * i:5 + 4 * i] for i, n in enumerate(SMALL)}


def merge_jobs(jobs):
    jobs = [j for j in jobs if j is not None]
    if not jobs:
        return None, []
    inputs, aliases, outs, sems, cuts = [], {}, [], [], []
    for j in jobs:
        i0, o0, s0 = len(inputs), len(outs), len(sems)
        aliases.update({i0 + i: o0 + o for i, o in j.aliases.items()})
        inputs += j.inputs
        outs += j.out_shapes
        sems += j.sem_shapes
        cuts.append((i0, len(inputs), o0, len(outs), s0, len(sems)))

    def each(which):
        def go(cins, couts, s):
            for j, (i0, i1, o0, o1, s0, s1) in zip(jobs, cuts):
                if getattr(j, which) is not None:
                    getattr(j, which)(cins[i0:i1], couts[o0:o1], s[s0:s1])
        return go

    relay = each("relay") if any(j.relay is not None for j in jobs) else None
    return CommJob(inputs, aliases, outs, sems, each("start"), each("finish"), relay), [(c[2], c[3]) for c in cuts]


SIXTHS = 6
OUT_KINDS = ("a_t", "b_t", "c_t", "d_t", "o")
GATHER_PLAN = {
    "fwd_proj": [(k, 0, 0, 6) for k in OUT_KINDS] + [("gate_t", 0, 0, 3)],
    "fwd_branch": [("gate_t", 0, 3, 6), ("up_t", 0, 0, 3)],
    "fwd_attn": [("up_t", 0, 3, 6), ("down", 0, 0, 6)],
    "fwd_merge": [("in_t", 1, 0, 2)],
    "fwd_ffn": [("in_t", 1, 2, 6)],
}
SIBLING_PLAN = {"bwd_merge": ("ffn", 0), "bwd_branch": ("out", 0), "bwd_ffn": ("in", 1),
                "bwd_proj_w1": ("in_a", 0), "bwd_proj_x": ("in_b", 0)}
GROUPS = dict(ffn=("gate_t", "up_t", "down"), out=OUT_KINDS, in_a=("in_a",), in_b=("in_b",))
GROUPS["in"] = ("in_t",)
COLUMN_HALF = dict(in_a=("in_t", W_IN_PARTS[0][0]), in_b=("in_t", W_IN_PARTS[1][0]))
CHIP_PLAN = {
    "bwd_attn": [("in_t", 1, 3, 6), ("gate_t", 0, 0, 3)],
    "bwd_branch": [("gate_t", 0, 3, 6), ("up_t", 0, 0, 6), ("down", 0, 0, 6)],
    "bwd_proj": [(k, 0, 0, 6) for k in OUT_KINDS],
    "bwd_proj_w0": [(k, 0, 0, 6) for k in OUT_KINDS[:3]],
    "bwd_proj_w1": [(k, 0, 0, 6) for k in OUT_KINDS[3:]],
    "bwd_merge": [("in_t", 1, 0, 3)],
    "bwd_proj_x": [("in_a", 0, 0, 6)],
    "adamw_gate_t": [("in_b", 0, 0, 6)],
}
SMALL_GATHER_PLAN = {"bwd_ffn": 1, "adamw_down": 0}


class Overlap:
    def __init__(self, shards, core):
        self.shards = shards
        self.core = core
        self.gathered = [dict.fromkeys(BIG) for _ in range(DEPTH)]
        self.views = {}
        self.partial = {}
        self.contrib = dict.fromkeys(BIG)
        self.small_packs = [None] * DEPTH
        self.small_gathered = [None] * DEPTH
        self._open = None

    def weights(self, l):
        return self.gathered[l]

    def new_grads(self, group, l, grads):
        for k, g in grads.items():
            self.views[k, l] = g.reshape(4, 2, g.shape[0] // NDEV, g.shape[1])

    def new_small(self, l, arrays, head_stats):
        self.small_packs[l] = pack_small(arrays, head_stats if l == DEPTH - 1 else None, l)

    @staticmethod
    def _rows(shard_rows, f0, f1):
        return shard_rows * f0 // SIXTHS, shard_rows * (f1 - f0) // SIXTHS

    def job(self, slot, l):
        jobs, notes = [], []
        pieces = [(k, l + dl, f0, f1) for k, dl, f0, f1 in GATHER_PLAN.get(slot, []) if l + dl < DEPTH]
        if pieces:
            jobs.append(gather_job([((k, ll), self.shards[ll][k], self.gathered[ll][k],
                                     *self._rows(self.shards[ll][k].shape[0], f0, f1)) for k, ll, f0, f1 in pieces]))
            notes.append(("gather", list(dict.fromkeys((k, ll) for k, ll, _, _ in pieces))))
        if slot in SIBLING_PLAN and l + SIBLING_PLAN[slot][1] < DEPTH:
            group, dl = SIBLING_PLAN[slot]
            keys = [(k, l + dl) for k in GROUPS[group]]
            jobs.append(sibling_exchange_job([self.views[key] for key in keys]))
            notes.append(("sibling", keys))
        pieces = [(k, l + dl, f0, f1) for k, dl, f0, f1 in CHIP_PLAN.get(slot, []) if l + dl < DEPTH]
        if pieces:
            whole = [(*COLUMN_HALF.get(k, (k, 0)), k, ll, f0, f1) for k, ll, f0, f1 in pieces]
            jobs.append(chip_exchange_job([(self.partial[k, ll], self.contrib[kind], kind, ll,
                                            *self._rows(self.partial[k, ll].shape[1], f0, f1), col0, self.shards[ll][kind].shape[1])
                                           for kind, col0, k, ll, f0, f1 in whole]))
            notes.append(("chips", list(dict.fromkeys(kind for kind, *_ in whole))))
        if slot in SMALL_GATHER_PLAN and l + SMALL_GATHER_PLAN[slot] < DEPTH:
            ll = l + SMALL_GATHER_PLAN[slot]
            jobs.append(gather_job([("small", self.small_packs[ll], None, 0, P_ROWS)]))
            notes.append(("small", ll))
        job, spans = merge_jobs(jobs)
        self._open = (slot, l, notes, spans)
        return job

    def done(self, slot, l, results):
        open_slot, open_l, notes, spans = self._open
        assert (open_slot, open_l) == (slot, l)
        for (what, keys), (r0, r1) in zip(notes, spans):
            res = results[r0:r1]
            if what == "gather":
                for (k, ll), g in zip(keys, res):
                    self.gathered[ll][k] = g
            elif what == "sibling":
                sums = add_partials([self.views[key] for key in keys], list(res), self.core, f"chip_sum_{keys[0][0]}{keys[0][1]}")
                self.partial.update(zip(keys, sums))
            elif what == "chips":
                for k, c in zip(keys, res):
                    self.contrib[k] = c
            else:
                self.small_gathered[keys], = res


SMALL = ("norm1_g", "conv_a_w", "conv_a_b", "lru_wx", "lru_bx", "lru_wa", "lru_ba", "lru_lambda", "conv_b_w", "sinks",
         "conv_d_w", "conv_d_b", "ln_d_g", "ln_d_b", "norm2_g", "final_g")
WEIGHTS = ("norm1_g", "w_in", "conv_a_w", "conv_a_b", "lru_wx", "lru_bx", "lru_wa", "lru_ba", "lru_lambda", "w_a_out",
           "conv_b_w", "w_b_out", "sinks", "w_c_out", "conv_d_w", "conv_d_b", "ln_d_g", "ln_d_b", "w_d_out", "w_o",
           "norm2_g", "w_ffn_gate", "w_ffn_up", "w_ffn_down", "final_g")


def kernel(x, norm1_g, w_in, conv_a_w, conv_a_b, lru_wx, lru_bx, lru_wa, lru_ba, lru_lambda, w_a_out, conv_b_w, w_b_out, sinks, w_c_out, conv_d_w, conv_d_b, ln_d_g, ln_d_b, w_d_out, w_o, norm2_g, w_ffn_gate, w_ffn_up, w_ffn_down, final_g, loss_target, m_norm1_g, m_w_in, m_conv_a_w, m_conv_a_b, m_lru_wx, m_lru_bx, m_lru_wa, m_lru_ba, m_lru_lambda, m_w_a_out, m_conv_b_w, m_w_b_out, m_sinks, m_w_c_out, m_conv_d_w, m_conv_d_b, m_ln_d_g, m_ln_d_b, m_w_d_out, m_w_o, m_norm2_g, m_w_ffn_gate, m_w_ffn_up, m_w_ffn_down, m_final_g, v_norm1_g, v_w_in, v_conv_a_w, v_conv_a_b, v_lru_wx, v_lru_bx, v_lru_wa, v_lru_ba, v_lru_lambda, v_w_a_out, v_conv_b_w, v_w_b_out, v_sinks, v_w_c_out, v_conv_d_w, v_conv_d_b, v_ln_d_g, v_ln_d_b, v_w_d_out, v_w_o, v_norm2_g, v_w_ffn_gate, v_w_ffn_up, v_w_ffn_down, v_final_g):
    args = dict(locals())
    w = {n: args[n] for n in WEIGHTS}
    m = {n: args["m_" + n] for n in WEIGHTS}
    v = {n: args["v_" + n] for n in WEIGHTS}
    me = _dev_index(*_mesh_pos())

    def rows_major(a, how):
        return jnp.swapaxes(a, 1, 2) if how == "view" else a

    stacked = {k: cast_transpose(w[n], "prep_" + k) if how == "transpose" else rows_major(w[n], how).astype(BF16)
               for k, (n, how) in BIG.items()}
    plan = Overlap([{k: stacked[k][l] for k in BIG} for l in range(DEPTH)], lax.axis_index("c").astype(jnp.int32).reshape(1))
    convs = jnp.pad(_stack_convs(w).reshape(DEPTH * CW_ROWS, BW // NDEV), ((0, 0), (0, 256 - BW // NDEV)))
    g_in0, g_conv = _comm_only(gather_job([(("in_t", 0), plan.shards[0]["in_t"], None, 0, plan.shards[0]["in_t"].shape[0]),
                                           ("convs", convs, None, 0, convs.shape[0])]), "gather_first")
    plan.gathered[0]["in_t"] = g_in0
    convw = g_conv[:, :BW // NDEV].reshape(NDEV, DEPTH, CW_ROWS, BW // NDEV).transpose(1, 2, 0, 3).reshape(DEPTH, CW_ROWS, BW)

    vecs = _stack_vecs(w)
    head_stats, grad_x, grads = local_step(x[0], loss_target[0], norm1_g, norm2_g, final_g, convw, vecs, lru_wx, lru_wa, plan)


    out = {}
    for k in ("down", "gate_t", "up_t", "o", "a_t", "b_t", "c_t", "d_t", "in_t"):
        n, how = BIG[k]
        res, cres = adamw_big(plan.contrib[k], rows_major(w[n], how), rows_major(m[n], how), rows_major(v[n], how),
                              how == "transpose", "adamw_" + k, comm=plan.job("adamw_" + k, 0))
        plan.done("adamw_" + k, 0, cres)
        out[n] = [rows_major(r, how) for r in res]

    def own_shapes(p):
        return {n: p[n].reshape(1, D) if n == "final_g" else p[n] for n in SMALL}

    loss, small = adamw_small([g.reshape(NDEV, P_ROWS, D) for g in plan.small_gathered], me.astype(jnp.int32).reshape(1),
                              own_shapes(w), own_shapes(m), own_shapes(v))
    for n in SMALL:
        out[n] = [r.reshape(w[n].shape) for r in small[n]]
    loss = loss[0, 0]
    return (loss, grad_x[None], *[out[n][0] for n in WEIGHTS], *[out[n][1] for n in WEIGHTS],
            *[out[n][2] for n in WEIGHTS], *[out[n][3] for n in WEIGHTS])
```

```python
import functools

import jax
import jax.numpy as jnp
from jax import lax
from jax.experimental import pallas as pl
from jax.experimental.pallas import tpu as pltpu

F32 = jnp.float32
BF16 = jnp.bfloat16
E = pl.Element

D = 1024
BW = 512
IN_W = 8448
GL0 = 4352
FF = 2816
N_HEADS = 8
N_KV = 2
HD = 64
ATT_BLK = 128
EPS = 1e-6
LRU_C = 8.0
NEG_INF = -1e30
DEPTH = 2
NDEV = 8
CONV_A, CONV_B, CONV_D = 4, 3, 31
C_AX, C_AG, C_BV, C_BC, C_BB, C_Q, C_K, C_V, C_D1, C_D2 = 0, 512, 1024, 1536, 2048, 2560, 3072, 3200, 3328, 3840
CW_A, CW_B, CW_D, CW_ROWS = 0, 4, 8, 40
V_CAB, V_BX, V_BA, V_LAM, V_CDB, V_LNG, V_LNB, V_SINK, V_ROWS = 0, 1, 2, 3, 4, 5, 6, 7, 8
HALO = 32
W_IN_PARTS = ((0, 768), (768, 256))

ADAM_LR, ADAM_B1, ADAM_B2, ADAM_EPS, ADAM_WD, ADAM_STEP = 0.001, 0.9, 0.999, 1e-08, 0.01, 10

VMEM_LIMIT = 56 * 1024 * 1024

_NN = (((1,), (0,)), ((), ()))
_NT = (((1,), (1,)), ((), ()))
_TN = (((0,), (0,)), ((), ()))


def _dot(a, b, dims):
    return lax.dot_general(a.astype(BF16), b.astype(BF16), dims, preferred_element_type=F32)


def _cparams(n_axes):
    return pltpu.CompilerParams(dimension_semantics=("arbitrary",) * n_axes, vmem_limit_bytes=VMEM_LIMIT)


def _sds(shape, dtype):
    return jax.ShapeDtypeStruct(tuple(shape), dtype)


def _sigmoid(x):
    return jax.nn.sigmoid(x)


def _neg_expm1(x):
    p = x * (1.0 + x * (0.5 + x * (1.0 / 6.0 + x * (1.0 / 24.0 + x * (1.0 / 120.0)))))
    return jnp.where(x > -0.1, -p, 1.0 - jnp.exp(x))


def _softplus(z):
    return jnp.maximum(z, 0.0) + jnp.log1p(jnp.exp(-jnp.abs(z)))


def _gelu_and_grad(x):
    c = 0.7978845608028654
    inner = c * (x + 0.044715 * x * x * x)
    t = jnp.tanh(inner)
    g = 0.5 * x * (1.0 + t)
    dg = 0.5 * (1.0 + t) + 0.5 * x * (1.0 - t * t) * c * (1.0 + 3.0 * 0.044715 * x * x)
    return g, dg


ANY = pl.BlockSpec(memory_space=pl.ANY)
MESH = pl.DeviceIdType.MESH


def _mesh_pos():
    return lax.axis_index("x"), lax.axis_index("y"), lax.axis_index("c")


def _dev_index(px, py, pc):
    return 4 * px + 2 * py + pc


class CommJob:
    def __init__(self, inputs, aliases, out_shapes, sem_shapes, start, finish, relay=None):
        self.inputs, self.aliases, self.out_shapes, self.sem_shapes = list(inputs), dict(aliases), list(out_shapes), list(sem_shapes)
        self.start, self.finish, self.relay = start, finish, relay


def _call(body, comm, args, *, grid, in_specs, out_specs, out_shape, scratch_shapes=(), name, aliases=None):
    single = not isinstance(out_shape, (list, tuple))
    out_specs = [out_specs] if single else list(out_specs)
    out_shape = [out_shape] if single else list(out_shape)
    scratch_shapes = list(scratch_shapes)
    n_in, n_out, n_scr, n_axes = len(in_specs), len(out_shape), len(scratch_shapes), len(grid)
    params = pltpu.CompilerParams(dimension_semantics=("arbitrary",) * n_axes, vmem_limit_bytes=VMEM_LIMIT)
    io_aliases = dict(aliases or {})
    if comm is None:
        outs = pl.pallas_call(body, grid=grid, in_specs=in_specs, out_specs=out_specs, out_shape=out_shape,
                              scratch_shapes=scratch_shapes, input_output_aliases=io_aliases, compiler_params=params,
                              name=name)(*args)
        return (outs[0] if single else outs), []
    c_in, c_out = len(comm.inputs), len(comm.out_shapes)
    io_aliases.update({n_in + i: n_out + o for i, o in comm.aliases.items()})

    def wrapped(*refs):
        ins, cins = refs[:n_in], refs[n_in:n_in + c_in]
        outs = refs[n_in + c_in:n_in + c_in + n_out]
        couts = refs[n_in + c_in + n_out:n_in + c_in + n_out + c_out]
        rest = refs[n_in + c_in + n_out + c_out:]
        scr, sems = rest[:n_scr], rest[n_scr:]
        first = functools.reduce(lambda a, b: a & b, [pl.program_id(a) == 0 for a in range(n_axes)])
        last = functools.reduce(lambda a, b: a & b, [pl.program_id(a) == pl.num_programs(a) - 1 for a in range(n_axes)])

        @pl.when(first)
        def _():
            comm.start(cins, couts, sems)

        if comm.relay is not None:
            step = functools.reduce(lambda a, b: a * grid[b] + pl.program_id(b), range(1, n_axes), pl.program_id(0))
            n_steps = functools.reduce(lambda a, b: a * b, grid)

            @pl.when(step == 2 * n_steps // 3)
            def _():
                comm.relay(cins, couts, sems)

        body(*ins, *outs, *scr)

        @pl.when(last)
        def _():
            comm.finish(cins, couts, sems)

    outs = pl.pallas_call(
        wrapped, grid=grid, in_specs=list(in_specs) + [ANY] * c_in, out_specs=out_specs + [ANY] * c_out,
        out_shape=out_shape + comm.out_shapes, scratch_shapes=scratch_shapes + comm.sem_shapes,
        input_output_aliases=io_aliases, compiler_params=params, name=name)(*args, *comm.inputs)
    res, cres = outs[:n_out], outs[n_out:]
    return (res[0] if single else res), cres


def _comm_only(comm, name):
    c_in, c_out = len(comm.inputs), len(comm.out_shapes)

    def body(*refs):
        cins, couts, sems = refs[:c_in], refs[c_in:c_in + c_out], refs[c_in + c_out:]
        comm.start(cins, couts, sems)
        if comm.relay is not None:
            comm.relay(cins, couts, sems)
        comm.finish(cins, couts, sems)

    return pl.pallas_call(body, in_specs=[ANY] * c_in, out_specs=[ANY] * c_out, out_shape=comm.out_shapes,
                          scratch_shapes=comm.sem_shapes, input_output_aliases=comm.aliases, name=name)(*comm.inputs)


def gather_job(pieces):
    inputs, aliases, out_shapes, plan, where = [], {}, [], [], {}
    for key, shard, gathered, row0, nrows in pieces:
        if key not in where:
            where[key] = (len(inputs), len(out_shapes))
            inputs.append(shard)
            if gathered is not None:
                aliases[len(inputs)] = len(out_shapes)
                inputs.append(gathered)
            out_shapes.append(_sds((NDEV * shard.shape[0], shard.shape[1]), shard.dtype))
        plan.append((*where[key], shard.shape[0], row0, nrows))
    n = len(plan)

    def copies(cins, couts, sems):
        send_sems, recv_sems, local_sems = sems
        x, y, c = _mesh_pos()
        me, sibling = (x, y, c), (x, y, 1 - c)
        xn, yn, dg = (1 - x, y), (x, 1 - y), (1 - x, 1 - y)
        local, first, pass1, pass2, got_ici, got_fwd, got_d2d = [], [], [], [], [], [], []
        for p, (i_shard, i_out, rows, row0, nrows) in enumerate(plan):
            src = cins[i_shard].at[pl.ds(row0, nrows), :]
            half = cins[i_shard].shape[1] // 2
            left, right, whole = pl.ds(0, half), pl.ds(half, half), slice(None)

            def slot(dev, lanes, i_out=i_out, rows=rows, row0=row0, nrows=nrows):
                return couts[i_out].at[pl.ds(_dev_index(*dev) * rows + row0, nrows), lanes]

            def copy(g, dev, to, lanes=whole, src=None, p=p, slot=slot):
                return pltpu.make_async_remote_copy(
                    src_ref=slot(dev, lanes) if src is None else src, dst_ref=slot(dev, lanes),
                    send_sem=send_sems.at[g, p], recv_sem=recv_sems.at[g, p], device_id=to, device_id_type=MESH)

            local.append(pltpu.make_async_copy(src, slot(me, whole), local_sems.at[p]))
            first += [copy(0, me, sibling, src=src), copy(1, me, (*xn, c), src=src), copy(2, me, (*yn, c), src=src)]
            got_ici += [copy(1, (*xn, c), me), copy(2, (*yn, c), me)]
            pass1 += [copy(3, (*xn, c), (*yn, c), left), copy(4, (*yn, c), (*xn, c), right),
                      copy(5, (*xn, c), sibling), copy(6, (*yn, c), sibling)]
            got_fwd += [copy(3, (*dg, c), me, left), copy(4, (*dg, c), me, right)]
            pass2 += [copy(7, (*dg, c), sibling, left), copy(8, (*dg, c), sibling, right)]
            got_d2d += [copy(0, sibling, me), copy(5, (*xn, 1 - c), me), copy(6, (*yn, 1 - c), me),
                        copy(7, (*dg, 1 - c), me, left), copy(8, (*dg, 1 - c), me, right)]
        return local, first, pass1, pass2, got_ici, got_fwd, got_d2d

    def start(cins, couts, sems):
        local, first, *_ = copies(cins, couts, sems)
        for cp in local + first:
            cp.start()

    def pass_on(cins, couts, sems):
        _, _, pass1, _, got_ici, _, _ = copies(cins, couts, sems)
        for cp in got_ici:
            cp.wait_recv()
        for cp in pass1:
            cp.start()

    def finish(cins, couts, sems):
        local, first, pass1, pass2, _, got_fwd, got_d2d = copies(cins, couts, sems)
        for cp in got_fwd:
            cp.wait_recv()
        for cp in pass2:
            cp.start()
        for cp in got_d2d:
            cp.wait_recv()
        for cp in first + pass1 + pass2:
            cp.wait_send()
        for cp in local:
            cp.wait()

    sem_shapes = [pltpu.SemaphoreType.DMA((9, n)), pltpu.SemaphoreType.DMA((9, n)), pltpu.SemaphoreType.DMA((n,))]
    return CommJob(inputs, aliases, out_shapes, sem_shapes, start, finish, relay=pass_on)


def sibling_exchange_job(grads):
    n = len(grads)

    def copies(cins, couts, sems):
        send_sems, recv_sems = sems
        x, y, c = _mesh_pos()
        return [pltpu.make_async_remote_copy(
            src_ref=cins[q].at[:, 1 - c], dst_ref=couts[q], send_sem=send_sems.at[q], recv_sem=recv_sems.at[q],
            device_id=(x, y, 1 - c), device_id_type=MESH) for q in range(n)]

    def start(cins, couts, sems):
        for cp in copies(cins, couts, sems):
            cp.start()

    def finish(cins, couts, sems):
        cps = copies(cins, couts, sems)
        for cp in cps:
            cp.wait_recv()
        for cp in cps:
            cp.wait_send()

    return CommJob(grads, {}, [_sds((4,) + g.shape[2:], g.dtype) for g in grads],
                   [pltpu.SemaphoreType.DMA((n,)), pltpu.SemaphoreType.DMA((n,))], start, finish)


def chip_exchange_job(pieces):
    inputs, aliases, out_shapes, plan, where = [], {}, [], [], {}
    for partial, contrib, key, layer, row0, nrows, col0, cols in pieces:
        if key not in where:
            where[key] = len(out_shapes)
            out_shapes.append(_sds((4, DEPTH, partial.shape[1], cols), partial.dtype))
            if contrib is not None:
                aliases[len(inputs)] = where[key]
                inputs.append(contrib)
        plan.append((len(inputs), where[key], layer, row0, nrows, col0, partial.shape[2]))
        inputs.append(partial)
    n = len(plan)

    def copies(cins, couts, sems):
        send_sems, recv_sems, local_sems = sems
        x, y, c = _mesh_pos()
        mine = 2 * x + y
        local, sends, recvs = [], [], []
        for p, (i_in, i_out, layer, row0, nrows, col0, ncols) in enumerate(plan):
            rows, lanes = pl.ds(row0, nrows), pl.ds(col0, ncols)
            local.append(pltpu.make_async_copy(cins[i_in].at[mine, rows, :], couts[i_out].at[mine, layer, rows, lanes],
                                               local_sems.at[p]))
            for j, (cx, cy) in enumerate([(1 - x, y), (x, 1 - y), (1 - x, 1 - y)]):
                theirs = 2 * cx + cy

                def copy(slot_there, j=j, p=p, cx=cx, cy=cy, theirs=theirs, i_in=i_in, i_out=i_out, layer=layer,
                         rows=rows, lanes=lanes):
                    return pltpu.make_async_remote_copy(
                        src_ref=cins[i_in].at[theirs, rows, :], dst_ref=couts[i_out].at[slot_there, layer, rows, lanes],
                        send_sem=send_sems.at[j, p], recv_sem=recv_sems.at[j, p], device_id=(cx, cy, c), device_id_type=MESH)
                sends.append(copy(mine))
                recvs.append(copy(theirs))
        return local, sends, recvs

    def start(cins, couts, sems):
        local, sends, _ = copies(cins, couts, sems)
        for cp in local + sends:
            cp.start()

    def finish(cins, couts, sems):
        local, sends, recvs = copies(cins, couts, sems)
        for cp in recvs:
            cp.wait_recv()
        for cp in sends:
            cp.wait_send()
        for cp in local:
            cp.wait()

    sem_shapes = [pltpu.SemaphoreType.DMA((3, n)), pltpu.SemaphoreType.DMA((3, n)), pltpu.SemaphoreType.DMA((n,))]
    return CommJob(inputs, aliases, out_shapes, sem_shapes, start, finish)


def fwd_proj(x, g1, wt_in, l, comm=None):
    s = x.shape[0]
    tm = min(512, s)
    tn = 1408

    def body(x_ref, g_ref, w_ref, o_ref, xn_ref):
        @pl.when(pl.program_id(1) == 0)
        def _():
            xv = x_ref[...]
            r = lax.rsqrt(jnp.mean(xv * xv, axis=-1, keepdims=True) + EPS)
            xn_ref[...] = (xv * r * g_ref[l:l + 1, :]).astype(BF16)

        o_ref[...] = _dot(xn_ref[...], w_ref[...], _NT).astype(BF16)

    return _call(
        body, comm, (x, g1, wt_in), grid=(s // tm, IN_W // tn),
        in_specs=[pl.BlockSpec((tm, D), lambda i, j: (i, 0)),
                  pl.BlockSpec((DEPTH, D), lambda i, j: (0, 0)),
                  pl.BlockSpec((tn, D), lambda i, j: (j, 0))],
        out_specs=pl.BlockSpec((tm, tn), lambda i, j: (i, j)),
        out_shape=_sds((s, IN_W), BF16),
        scratch_shapes=[pltpu.VMEM((tm, D), BF16)], name=f"fwd_proj{l}")


def _scan_fwd(a_ref, u_ref, h_ref, h0, n_rows):
    row = lax.broadcasted_iota(jnp.int32, (8, BW), 0)

    def body(g, hprev):
        r = pl.multiple_of(g * 8, 8)
        a = a_ref[pl.ds(r, 8), :]
        u = u_ref[pl.ds(r, 8), :]
        for sft in (1, 2, 4):
            a_sh = jnp.where(row >= sft, pltpu.roll(a, sft, 0), 1.0)
            u_sh = jnp.where(row >= sft, pltpu.roll(u, sft, 0), 0.0)
            u = u + a * u_sh
            a = a * a_sh
        h = u + a * hprev
        h_ref[pl.ds(r, 8), :] = h
        return h[7:8, :]

    return lax.fori_loop(0, n_rows // 8, body, h0)


def _scan_bwd(b_ref, g_ref, o_ref, c0, n_rows):
    row = lax.broadcasted_iota(jnp.int32, (8, BW), 0)

    def body(k, cnext):
        r = pl.multiple_of((n_rows // 8 - 1 - k) * 8, 8)
        b = b_ref[pl.ds(r, 8), :]
        g = g_ref[pl.ds(r, 8), :]
        for sft in (1, 2, 4):
            b_sh = jnp.where(row < 8 - sft, pltpu.roll(b, 8 - sft, 0), 1.0)
            g_sh = jnp.where(row < 8 - sft, pltpu.roll(g, 8 - sft, 0), 0.0)
            g = g + b * g_sh
            b = b * b_sh
        o = g + b * cnext
        o_ref[pl.ds(r, 8), :] = o
        return o[0:1, :]

    return lax.fori_loop(0, n_rows // 8, body, c0)


def _shifted_copies(buf, shifted, n_rows):
    for r in range(1, 8):
        shifted[r - 1, 0:n_rows - 8, :] = buf[pl.ds(r, n_rows - 8), :]


def _window(buf, shifted, off, t):
    r = off % 8
    return buf[pl.ds(off, t), :] if r == 0 else shifted[r - 1, pl.ds(off - r, t), :]


def _branch_fwd_math(cur_ref, halo_ref, cw_ref, vec_ref, wx_ref, wa_ref, bufa, bufb, bufd, xd, first, t):
    def halo(c0):
        v = halo_ref[:, c0:c0 + BW].astype(F32)
        return jnp.where(first, 0.0, v)

    def cur(c0):
        return cur_ref[:, c0:c0 + BW].astype(F32)

    out = {}
    bufa[0:HALO, :] = halo(C_AX)
    bufa[HALO:HALO + t, :] = cur(C_AX)
    ca = jnp.zeros((t, BW), F32) + vec_ref[V_CAB:V_CAB + 1, :]
    for k in range(CONV_A):
        ca = ca + cw_ref[CW_A + k:CW_A + k + 1, :] * bufa[pl.ds(HALO - (CONV_A - 1) + k, t), :]
    gi = _sigmoid(_dot(ca, wx_ref[...], _NN) + vec_ref[V_BX:V_BX + 1, :])
    gr = _sigmoid(_dot(ca, wa_ref[...], _NN) + vec_ref[V_BA:V_BA + 1, :])
    sp = _softplus(-vec_ref[V_LAM:V_LAM + 1, :])
    la = -LRU_C * sp * gr
    a = jnp.exp(la)
    mult = jnp.sqrt(_neg_expm1(2.0 * la))
    out.update(ca=ca, gi=gi, gr=gr, sp=sp, a=a, mult=mult)
    bufb[0:HALO, :] = halo(C_BC) * halo(C_BV)
    bufb[HALO:HALO + t, :] = cur(C_BC) * cur(C_BV)
    cb = jnp.zeros((t, BW), F32)
    for k in range(CONV_B):
        cb = cb + cw_ref[CW_B + k:CW_B + k + 1, :] * bufb[pl.ds(HALO - (CONV_B - 1) + k, t), :]
    out.update(cb=cb)
    bufd[0:HALO, :] = halo(C_D1) * _sigmoid(halo(C_D2))
    s2 = _sigmoid(cur(C_D2))
    bufd[HALO:HALO + t, :] = cur(C_D1) * s2
    _shifted_copies(bufd, xd, t + HALO)
    cd = jnp.zeros((t, BW), F32) + vec_ref[V_CDB:V_CDB + 1, :]
    for k in range(CONV_D):
        cd = cd + cw_ref[CW_D + k:CW_D + k + 1, :] * _window(bufd, xd, HALO - (CONV_D - 1) + k, t)
    mu = jnp.mean(cd, axis=-1, keepdims=True)
    xc = cd - mu
    rstd = lax.rsqrt(jnp.mean(xc * xc, axis=-1, keepdims=True) + EPS)
    xh = xc * rstd
    ln = xh * vec_ref[V_LNG:V_LNG + 1, :] + vec_ref[V_LNB:V_LNB + 1, :]
    out.update(s2=s2, xh=xh, rstd=rstd, ln=ln)
    return out


def fwd_branch(proj, convw, vecs, wx_bd, wa_bd, l, comm=None):
    s = proj.shape[0]
    t = min(256, s)

    def body(cur_ref, halo_ref, cw_ref, vec_ref, wx_ref, wa_ref, pre_ref, h_ref, bufa, bufb, bufd, xd, a_s, u_s, hcar):
        first = pl.program_id(0) == 0

        @pl.when(first)
        def _():
            hcar[...] = jnp.zeros((1, BW), F32)

        v = _branch_fwd_math(cur_ref, halo_ref, cw_ref, vec_ref, wx_ref, wa_ref, bufa, bufb, bufd, xd, first, t)
        a_s[...] = v["a"]
        u_s[...] = v["ca"] * v["gi"] * v["mult"]
        hcar[...] = _scan_fwd(a_s, u_s, h_ref, hcar[...], t)
        gg, _ = _gelu_and_grad(cur_ref[:, C_AG:C_AG + BW].astype(F32))
        pre_ref[:, 0:BW] = (h_ref[...] * gg).astype(BF16)
        pre_ref[:, BW:2 * BW] = (cur_ref[:, C_BB:C_BB + BW].astype(F32) * v["cb"]).astype(BF16)
        ln = v["ln"]
        pre_ref[:, 2 * BW:3 * BW] = (ln * _sigmoid(ln)).astype(BF16)

    hb = t // HALO
    return _call(
        body, comm, (proj, proj, convw, vecs, wx_bd, wa_bd), grid=(s // t,),
        in_specs=[pl.BlockSpec((t, GL0), lambda i: (i, 0)),
                  pl.BlockSpec((HALO, GL0), lambda i: (jnp.maximum(i * hb - 1, 0), 0)),
                  pl.BlockSpec((None, CW_ROWS, BW), lambda i: (l, 0, 0)),
                  pl.BlockSpec((None, V_ROWS, BW), lambda i: (l, 0, 0)),
                  pl.BlockSpec((None, BW, BW), lambda i: (l, 0, 0)),
                  pl.BlockSpec((None, BW, BW), lambda i: (l, 0, 0))],
        out_specs=[pl.BlockSpec((t, 3 * BW), lambda i: (i, 0)), pl.BlockSpec((t, BW), lambda i: (i, 0))],
        out_shape=[_sds((s, 3 * BW), BF16), _sds((s, BW), F32)],
        scratch_shapes=[pltpu.VMEM((t + HALO, BW), F32)] * 3 + [pltpu.VMEM((7, t + HALO - 8, BW), F32)]
        + [pltpu.VMEM((t, BW), F32)] * 2 + [pltpu.VMEM((1, BW), F32)],
        name=f"fwd_branch{l}")


GRP = N_HEADS // N_KV


def _attn_mask_bias(first_block):
    shape = (GRP * ATT_BLK, 2 * ATT_BLK)
    qi = lax.broadcasted_iota(jnp.int32, shape, 0) & (ATT_BLK - 1)
    ki = lax.broadcasted_iota(jnp.int32, shape, 1)
    dist = qi + ATT_BLK - ki
    valid = (dist >= 0) & (dist < ATT_BLK) & (jnp.logical_not(first_block) | (ki >= ATT_BLK))
    return dist.astype(F32), valid


def _per_head(hk, values):
    hl = lax.broadcasted_iota(jnp.int32, (GRP * ATT_BLK, 1), 0) // ATT_BLK
    out = values[GRP - 1]
    for j in range(GRP - 2, -1, -1):
        out = jnp.where(hl == j, values[j], out)
    return out


def _attn_probs(q_ref, kvp_ref, kvc_ref, vec_ref, distf, valid):
    kvs = range(N_KV)
    heads = [range(hk * GRP, (hk + 1) * GRP) for hk in kvs]
    q4 = [jnp.concatenate([q_ref[:, h * HD:(h + 1) * HD] for h in heads[hk]], axis=0) for hk in kvs]
    k2 = [jnp.concatenate([kvp_ref[:, hk * HD:(hk + 1) * HD], kvc_ref[:, hk * HD:(hk + 1) * HD]], axis=0) for hk in kvs]
    v2 = [jnp.concatenate([kvp_ref[:, (N_KV + hk) * HD:(N_KV + hk + 1) * HD],
                           kvc_ref[:, (N_KV + hk) * HD:(N_KV + hk + 1) * HD]], axis=0) for hk in kvs]
    slope = [_per_head(hk, [2.0 ** (-8.0 * (h + 1) / N_HEADS) for h in heads[hk]]) for hk in kvs]
    sink = [_per_head(hk, [vec_ref[V_SINK:V_SINK + 1, h:h + 1] for h in heads[hk]]) for hk in kvs]
    sc = [_dot(q4[hk], k2[hk], _NT) for hk in kvs]
    sc = [jnp.where(valid, sc[hk] * (HD ** -0.5) - slope[hk] * distf, NEG_INF) for hk in kvs]
    m = [jnp.maximum(jnp.max(sc[hk], axis=-1, keepdims=True), sink[hk]) for hk in kvs]
    p = [jnp.exp(sc[hk] - m[hk]) for hk in kvs]
    es = [jnp.exp(sink[hk] - m[hk]) for hk in kvs]
    inv = [1.0 / (jnp.sum(p[hk], axis=-1, keepdims=True) + es[hk]) for hk in kvs]
    return [(q4[hk], k2[hk], v2[hk], p[hk] * inv[hk], es[hk] * inv[hk]) for hk in kvs]


def fwd_attn(proj, vecs, l, comm=None):
    s = proj.shape[0]
    nb = s // ATT_BLK

    def body(q_ref, kvp_ref, kvc_ref, vec_ref, o_ref):
        distf, valid = _attn_mask_bias(pl.program_id(0) == 0)
        groups = _attn_probs(q_ref, kvp_ref, kvc_ref, vec_ref, distf, valid)
        outs = [_dot(p, v2, _NN).astype(BF16) for _, _, v2, p, _ in groups]
        for hk, out in enumerate(outs):
            for j in range(GRP):
                h = hk * GRP + j
                o_ref[:, h * HD:(h + 1) * HD] = out[j * ATT_BLK:(j + 1) * ATT_BLK]

    return _call(
        body, comm, (proj, proj, proj, vecs), grid=(nb,),
        in_specs=[pl.BlockSpec((ATT_BLK, BW), lambda i: (i, C_Q // BW)),
                  pl.BlockSpec((ATT_BLK, 256), lambda i: (jnp.maximum(i - 1, 0), C_K // 256)),
                  pl.BlockSpec((ATT_BLK, 256), lambda i: (i, C_K // 256)),
                  pl.BlockSpec((None, V_ROWS, BW), lambda i: (l, 0, 0))],
        out_specs=pl.BlockSpec((ATT_BLK, BW), lambda i: (i, 0)),
        out_shape=_sds((s, BW), BF16), name=f"fwd_attn{l}")


def fwd_merge(x, proj, pre_abd, pre_c, wt_a, wt_b, wt_c, wt_d, w_o, l, comm=None):
    s = x.shape[0]
    tm = min(256, s)

    def body(x_ref, gl_ref, pabd_ref, pc_ref, wa_ref, wb_ref, wc_ref, wd_ref, wo_ref, y_ref, mg_ref, h1_ref):
        pres = (pabd_ref[:, 0:BW], pabd_ref[:, BW:2 * BW], pc_ref[...], pabd_ref[:, 2 * BW:3 * BW])
        merged = jnp.zeros((tm, D), F32)
        for k, (pre, w_ref) in enumerate(zip(pres, (wa_ref, wb_ref, wc_ref, wd_ref))):
            yk = _dot(pre, w_ref[...], _NT)
            y_ref[:, k * D:(k + 1) * D] = yk.astype(BF16)
            merged = merged + _sigmoid(gl_ref[:, k * D:(k + 1) * D].astype(F32)) * yk
        mg_ref[...] = merged.astype(BF16)
        h1_ref[...] = x_ref[...] + _dot(merged, wo_ref[...], _NN)

    wspec = pl.BlockSpec((D, BW), lambda i: (0, 0))
    return _call(
        body, comm, (x, proj, pre_abd, pre_c, wt_a, wt_b, wt_c, wt_d, w_o), grid=(s // tm,),
        in_specs=[pl.BlockSpec((tm, D), lambda i: (i, 0)),
                  pl.BlockSpec((E(tm), E(4 * D)), lambda i: (i * tm, GL0)),
                  pl.BlockSpec((tm, 3 * BW), lambda i: (i, 0)),
                  pl.BlockSpec((tm, BW), lambda i: (i, 0)),
                  wspec, wspec, wspec, wspec,
                  pl.BlockSpec((D, D), lambda i: (0, 0))],
        out_specs=[pl.BlockSpec((tm, 4 * D), lambda i: (i, 0)), pl.BlockSpec((tm, D), lambda i: (i, 0)),
                   pl.BlockSpec((tm, D), lambda i: (i, 0))],
        out_shape=[_sds((s, 4 * D), BF16), _sds((s, D), BF16), _sds((s, D), F32)], name=f"fwd_merge{l}")


def fwd_ffn(h1, g2, wt_gate, wt_up, w_down, l, comm=None):
    s = h1.shape[0]
    tm = min(512, s)
    fc = FF // 2

    def body(h_ref, g_ref, wg_ref, wu_ref, wd_ref, xo_ref, fg_ref, fu_ref, hn_ref, acc_ref):
        j = pl.program_id(1)

        @pl.when(j == 0)
        def _():
            hv = h_ref[...]
            r = lax.rsqrt(jnp.mean(hv * hv, axis=-1, keepdims=True) + EPS)
            hn_ref[...] = (hv * r * g_ref[l:l + 1, :]).astype(BF16)
            acc_ref[...] = hv

        fg = _dot(hn_ref[...], wg_ref[...], _NT)
        fu = _dot(hn_ref[...], wu_ref[...], _NT)
        fg_ref[...] = fg.astype(BF16)
        fu_ref[...] = fu.astype(BF16)
        acc_ref[...] += _dot(fg * _sigmoid(fg) * fu, wd_ref[...], _NN)

        @pl.when(j == pl.num_programs(1) - 1)
        def _():
            xo_ref[...] = acc_ref[...]

    wspec = pl.BlockSpec((fc, D), lambda i, j: (j, 0))
    return _call(
        body, comm, (h1, g2, wt_gate, wt_up, w_down), grid=(s // tm, FF // fc),
        in_specs=[pl.BlockSpec((tm, D), lambda i, j: (i, 0)), pl.BlockSpec((DEPTH, D), lambda i, j: (0, 0)),
                  wspec, wspec, wspec],
        out_specs=[pl.BlockSpec((tm, D), lambda i, j: (i, 0)), pl.BlockSpec((tm, fc), lambda i, j: (i, j)),
                   pl.BlockSpec((tm, fc), lambda i, j: (i, j))],
        out_shape=[_sds((s, D), F32), _sds((s, FF), BF16), _sds((s, FF), BF16)],
        scratch_shapes=[pltpu.VMEM((tm, D), BF16), pltpu.VMEM((tm, D), F32)], name=f"fwd_ffn{l}")


def loss_head(x, gf, target):
    s = x.shape[0]
    tm = min(512, s)

    def body(x_ref, g_ref, t_ref, dx_ref, st_ref):
        @pl.when(pl.program_id(0) == 0)
        def _():
            st_ref[...] = jnp.zeros((8, D), F32)

        xv = x_ref[...]
        g = g_ref[...]
        r = lax.rsqrt(jnp.mean(xv * xv, axis=-1, keepdims=True) + EPS)
        n = xv * r
        err = n * g - t_ref[...]
        dy = err * (1.0 / D)
        dn = dy * g
        dx_ref[...] = r * (dn - n * jnp.mean(dn * n, axis=-1, keepdims=True))
        st_ref[0:1, :] += jnp.sum(dy * n, axis=0, keepdims=True)
        lsum = 0.5 * jnp.sum(jnp.mean(err * err, axis=-1, keepdims=True), axis=0, keepdims=True)
        st_ref[1:2, :] += jnp.broadcast_to(lsum, (1, D))

    return pl.pallas_call(
        body, grid=(s // tm,),
        in_specs=[pl.BlockSpec((tm, D), lambda i: (i, 0)), pl.BlockSpec((1, D), lambda i: (0, 0)),
                  pl.BlockSpec((tm, D), lambda i: (i, 0))],
        out_specs=[pl.BlockSpec((tm, D), lambda i: (i, 0)), pl.BlockSpec((8, D), lambda i: (0, 0))],
        out_shape=[_sds((s, D), F32), _sds((8, D), F32)],
        compiler_params=_cparams(1), name="loss_head")(x, gf, target)


def _edge_index(j, i, n_j, n_i):
    return jnp.where((j == 0) | (j == n_j - 1), i, n_i - 1)


def bwd_ffn(dxo, h1, fg, fu, g2, wt_gate, wt_up, w_down, l, comm=None):
    s = h1.shape[0]
    tm = min(512, s)
    fc = 256
    n_j, n_i = FF // fc, s // tm

    def body(dxo_ref, h_ref, fg_ref, fu_ref, g_ref, wg_ref, wu_ref, wd_ref,
             dh_ref, dwg_ref, dwu_ref, dwd_ref, st_ref, dhn, dxo_b, hn_b, ag, au, ad):
        j, i = pl.program_id(0), pl.program_id(1)
        rows = pl.ds(pl.multiple_of(i * tm, tm), tm)
        g = g_ref[l:l + 1, :]

        @pl.when(j == 0)
        def _():
            hv = h_ref[...]
            r = lax.rsqrt(jnp.mean(hv * hv, axis=-1, keepdims=True) + EPS)
            hn_b[rows, :] = (hv * r * g).astype(BF16)
            dxo_b[rows, :] = dxo_ref[...].astype(BF16)
            dhn[rows, :] = jnp.zeros((tm, D), F32)

        @pl.when((j == 0) & (i == 0))
        def _():
            st_ref[...] = jnp.zeros((8, D), F32)

        @pl.when(i == 0)
        def _():
            ag[...] = jnp.zeros((fc, D), F32)
            au[...] = jnp.zeros((fc, D), F32)
            ad[...] = jnp.zeros((fc, D), F32)

        fgv = fg_ref[...].astype(F32)
        fuv = fu_ref[...].astype(F32)
        sg = _sigmoid(fgv)
        sil = fgv * sg
        dxb = dxo_b[rows, :]
        hnb = hn_b[rows, :]
        d_act = _dot(dxb, wd_ref[...], _NT)
        ad[...] += _dot(sil * fuv, dxb, _TN)
        d_fg = (d_act * fuv * (sg * (1.0 + fgv * (1.0 - sg)))).astype(BF16)
        d_fu = (d_act * sil).astype(BF16)
        ag[...] += _dot(d_fg, hnb, _TN)
        au[...] += _dot(d_fu, hnb, _TN)
        dhn[rows, :] += _dot(d_fg, wg_ref[...], _NN) + _dot(d_fu, wu_ref[...], _NN)

        @pl.when(i == n_i - 1)
        def _():
            dwg_ref[...] = ag[...].astype(BF16)
            dwu_ref[...] = au[...].astype(BF16)
            dwd_ref[...] = ad[...].astype(BF16)

        @pl.when(j == n_j - 1)
        def _():
            hv = h_ref[...]
            r = lax.rsqrt(jnp.mean(hv * hv, axis=-1, keepdims=True) + EPS)
            n = hv * r
            dv = dhn[rows, :]
            dn = dv * g
            dh_ref[...] = dxo_ref[...] + r * (dn - n * jnp.mean(dn * n, axis=-1, keepdims=True))
            st_ref[0:1, :] += jnp.sum(dv * n, axis=0, keepdims=True)

    edge = lambda j, i: (_edge_index(j, i, n_j, n_i), 0)
    wspec = pl.BlockSpec((fc, D), lambda j, i: (j, 0))
    dwspec = pl.BlockSpec((fc, D), lambda j, i: (j, 0))
    return _call(
        body, comm, (dxo, h1, fg, fu, g2, wt_gate, wt_up, w_down), grid=(n_j, n_i),
        in_specs=[pl.BlockSpec((tm, D), edge),
                  pl.BlockSpec((tm, D), edge),
                  pl.BlockSpec((tm, fc), lambda j, i: (i, j)), pl.BlockSpec((tm, fc), lambda j, i: (i, j)),
                  pl.BlockSpec((DEPTH, D), lambda j, i: (0, 0)), wspec, wspec, wspec],
        out_specs=[pl.BlockSpec((tm, D), lambda j, i: (jnp.where(j == n_j - 1, i, 0), 0)),
                   dwspec, dwspec, dwspec, pl.BlockSpec((8, D), lambda j, i: (0, 0))],
        out_shape=[_sds((s, D), F32), _sds((FF, D), BF16), _sds((FF, D), BF16), _sds((FF, D), BF16), _sds((8, D), F32)],
        scratch_shapes=[pltpu.VMEM((s, D), F32), pltpu.VMEM((s, D), BF16), pltpu.VMEM((s, D), BF16),
                        pltpu.VMEM((fc, D), F32), pltpu.VMEM((fc, D), F32), pltpu.VMEM((fc, D), F32)],
        name=f"bwd_ffn{l}")


def bwd_merge(dh1, y4, proj, merged, pre_abd, pre_c, wt_a, wt_b, wt_c, wt_d, w_o, l, comm=None):
    s = dh1.shape[0]
    tm = min(256, s)
    n_i = s // tm

    def body(dh_ref, y_ref, gl_ref, mg_ref, pabd_ref, pc_ref, wa_ref, wb_ref, wc_ref, wd_ref, wo_ref,
             dgl_ref, dpre_ref, dwo_ref, dwa_ref, dwb_ref, dwc_ref, dwd_ref, ao, aa, ab, ac, ad):
        i = pl.program_id(0)
        accs = (aa, ab, ac, ad)

        @pl.when(i == 0)
        def _():
            ao[...] = jnp.zeros((D, D), F32)
            for acc in accs:
                acc[...] = jnp.zeros((D, BW), F32)

        dhb = dh_ref[...].astype(BF16)
        dmg = _dot(dhb, wo_ref[...], _NT)
        ao[...] += _dot(mg_ref[...], dhb, _TN)
        pres = (pabd_ref[:, 0:BW], pabd_ref[:, BW:2 * BW], pc_ref[...], pabd_ref[:, 2 * BW:3 * BW])
        for k, (pre, w_ref, acc) in enumerate(zip(pres, (wa_ref, wb_ref, wc_ref, wd_ref), accs)):
            gk = _sigmoid(gl_ref[:, k * D:(k + 1) * D].astype(F32))
            yk = y_ref[:, k * D:(k + 1) * D].astype(F32)
            dgl_ref[:, k * D:(k + 1) * D] = (dmg * yk * gk * (1.0 - gk)).astype(BF16)
            dyk = (dmg * gk).astype(BF16)
            dpre_ref[:, k * BW:(k + 1) * BW] = _dot(dyk, w_ref[...], _NN).astype(BF16)
            acc[...] += _dot(dyk, pre, _TN)

        @pl.when(i == n_i - 1)
        def _():
            dwo_ref[...] = ao[...].astype(BF16)
            for o_ref, acc in zip((dwa_ref, dwb_ref, dwc_ref, dwd_ref), accs):
                o_ref[...] = acc[...].astype(BF16)

    wspec = pl.BlockSpec((D, BW), lambda i: (0, 0))
    dwspec = pl.BlockSpec((D, BW), lambda i: (0, 0))
    return _call(
        body, comm, (dh1, y4, proj, merged, pre_abd, pre_c, wt_a, wt_b, wt_c, wt_d, w_o), grid=(n_i,),
        in_specs=[pl.BlockSpec((tm, D), lambda i: (i, 0)),
                  pl.BlockSpec((tm, 4 * D), lambda i: (i, 0)),
                  pl.BlockSpec((E(tm), E(4 * D)), lambda i: (i * tm, GL0)),
                  pl.BlockSpec((tm, D), lambda i: (i, 0)),
                  pl.BlockSpec((tm, 3 * BW), lambda i: (i, 0)),
                  pl.BlockSpec((tm, BW), lambda i: (i, 0)),
                  wspec, wspec, wspec, wspec,
                  pl.BlockSpec((D, D), lambda i: (0, 0))],
        out_specs=[pl.BlockSpec((E(tm), E(4 * D)), lambda i: (i * tm, GL0)),
                   pl.BlockSpec((tm, 4 * BW), lambda i: (i, 0)),
                   pl.BlockSpec((D, D), lambda i: (0, 0)), dwspec, dwspec, dwspec, dwspec],
        out_shape=[_sds((s, IN_W), BF16), _sds((s, 4 * BW), BF16), _sds((D, D), BF16)] + [_sds((D, BW), BF16)] * 4,
        scratch_shapes=[pltpu.VMEM((D, D), F32)] + [pltpu.VMEM((D, BW), F32)] * 4, name=f"bwd_merge{l}")


def bwd_attn(proj, dpre, vecs, l, comm=None):
    s = proj.shape[0]
    nb = s // ATT_BLK
    grp = N_HEADS // N_KV

    def body(q_ref, kvp_ref, kvc_ref, do_ref, vec_ref, dq_ref, dkc_ref, dkp_ref, st_ref):
        @pl.when(pl.program_id(0) == 0)
        def _():
            st_ref[...] = jnp.zeros((8, 128), F32)

        distf, valid = _attn_mask_bias(pl.program_id(0) == 0)
        lane = lax.broadcasted_iota(jnp.int32, (1, 128), 1)
        dsink = jnp.zeros((1, 128), F32)
        groups = _attn_probs(q_ref, kvp_ref, kvc_ref, vec_ref, distf, valid)
        kvs = range(N_KV)
        do4s = [jnp.concatenate([do_ref[:, h * HD:(h + 1) * HD] for h in range(hk * grp, (hk + 1) * grp)], axis=0) for hk in kvs]
        dps = [_dot(do4s[hk], groups[hk][2], _NT) for hk in kvs]
        deltas = [jnp.sum(groups[hk][3] * dps[hk], axis=-1, keepdims=True) for hk in kvs]
        dss = [groups[hk][3] * (dps[hk] - deltas[hk]) * (HD ** -0.5) for hk in kvs]
        for hk in kvs:
            q4, k2, v2, p, ps = groups[hk]
            do4, delta, ds = do4s[hk], deltas[hk], dss[hk]
            dq4 = _dot(ds, k2, _NN).astype(BF16)
            dk2 = _dot(ds, q4, _TN)
            dv2 = _dot(p, do4, _TN)
            psd = ps * delta
            for j in range(grp):
                h = hk * grp + j
                rows = slice(j * ATT_BLK, (j + 1) * ATT_BLK)
                dq_ref[:, h * HD:(h + 1) * HD] = dq4[rows]
                dsink = dsink + jnp.where(lane == h, -jnp.sum(psd[rows], axis=0, keepdims=True), 0.0)
            dkp_ref[:, hk * HD:(hk + 1) * HD] = dk2[0:ATT_BLK].astype(BF16)
            dkc_ref[:, hk * HD:(hk + 1) * HD] = dk2[ATT_BLK:].astype(BF16)
            dkp_ref[:, (N_KV + hk) * HD:(N_KV + hk + 1) * HD] = dv2[0:ATT_BLK].astype(BF16)
            dkc_ref[:, (N_KV + hk) * HD:(N_KV + hk + 1) * HD] = dv2[ATT_BLK:].astype(BF16)
        st_ref[0:1, :] += dsink

    return _call(
        body, comm, (proj, proj, proj, dpre, vecs), grid=(nb,),
        in_specs=[pl.BlockSpec((ATT_BLK, BW), lambda i: (i, C_Q // BW)),
                  pl.BlockSpec((ATT_BLK, 256), lambda i: (jnp.maximum(i - 1, 0), C_K // 256)),
                  pl.BlockSpec((ATT_BLK, 256), lambda i: (i, C_K // 256)),
                  pl.BlockSpec((ATT_BLK, BW), lambda i: (i, 2)),
                  pl.BlockSpec((None, V_ROWS, BW), lambda i: (l, 0, 0))],
        out_specs=[pl.BlockSpec((ATT_BLK, BW), lambda i: (i, 0)), pl.BlockSpec((ATT_BLK, 256), lambda i: (i, 0)),
                   pl.BlockSpec((ATT_BLK, 256), lambda i: (i, 0)), pl.BlockSpec((8, 128), lambda i: (0, 0))],
        out_shape=[_sds((s, BW), BF16), _sds((s, 256), BF16), _sds((s, 256), BF16), _sds((8, 128), F32)],
        name=f"bwd_attn{l}")


def bwd_branch(proj, dproj, dpre, h, dq, dkc, dkp, convw, vecs, wx_bd, wa_bd, l, comm=None):
    s = proj.shape[0]
    t = 2 * ATT_BLK
    nt = s // t
    nb = s // ATT_BLK
    hb = t // HALO

    def body(cur_ref, halo_ref, dpre_ref, h_ref, hp_ref, dq_ref, dkc_ref, dkp1_ref, dkp2_ref,
             cw_ref, vec_ref, wx_ref, wa_ref, dproj_in, dp_ref, dcw_ref, dvec_ref, dwx_ref, dwa_ref,
             bufa, bufb, bufd, xd, xg, a_ext, hbuf, b_s, g_s, dh_s, ga, gb, gd, dhcar):
        del dproj_in
        step = pl.program_id(0)
        ti = nt - 1 - step
        first = ti == 0

        @pl.when(step == 0)
        def _():
            dcw_ref[...] = jnp.zeros((CW_ROWS, BW), F32)
            dvec_ref[...] = jnp.zeros((V_ROWS, BW), F32)
            dwx_ref[...] = jnp.zeros((BW, BW), F32)
            dwa_ref[...] = jnp.zeros((BW, BW), F32)
            dhcar[...] = jnp.zeros((1, BW), F32)
            a_ext[t:t + 8, :] = jnp.zeros((8, BW), F32)
            ga[t:t + 8, :] = jnp.zeros((8, BW), F32)
            gb[t:t + 8, :] = jnp.zeros((8, BW), F32)
            gd[t:t + HALO, :] = jnp.zeros((HALO, BW), F32)

        def cur(c0):
            return cur_ref[:, c0:c0 + BW].astype(F32)

        def rsum(v):
            return jnp.sum(v, axis=0, keepdims=True)

        def put(c0, v):
            dp_ref[:, c0:c0 + BW] = v.astype(BF16)

        v = _branch_fwd_math(cur_ref, halo_ref, cw_ref, vec_ref, wx_ref, wa_ref, bufa, bufb, bufd, xd, first, t)
        ca, gi, gr, sp, a, mult = v["ca"], v["gi"], v["gr"], v["sp"], v["a"], v["mult"]
        dpa = dpre_ref[:, 0:BW].astype(F32)
        gg, dgg = _gelu_and_grad(cur(C_AG))
        hv = h_ref[...]
        put(C_AG, dpa * hv * dgg)
        a_ext[0:t, :] = a
        b_s[...] = a_ext[pl.ds(1, t), :]
        g_s[...] = dpa * gg
        dhcar[...] = _scan_bwd(b_s, g_s, dh_s, dhcar[...], t)
        a_ext[t:t + 1, :] = a[0:1, :]
        dh = dh_s[...]
        hbuf[0:8, :] = jnp.where(first, 0.0, hp_ref[...])
        hbuf[8:8 + t, :] = hv
        da = dh * hbuf[pl.ds(7, t), :]
        d_ca = dh * gi * mult
        d_gi = dh * ca * mult
        d_mult = dh * ca * gi
        d_la = da * a - d_mult * (a * a) / mult
        lam = vec_ref[V_LAM:V_LAM + 1, :]
        dvec_ref[V_LAM:V_LAM + 1, :] += rsum(d_la * gr) * (LRU_C * _sigmoid(-lam))
        d_gr = d_la * (-LRU_C * sp)
        d_zr = d_gr * gr * (1.0 - gr)
        d_zi = d_gi * gi * (1.0 - gi)
        dvec_ref[V_BA:V_BA + 1, :] += rsum(d_zr)
        dvec_ref[V_BX:V_BX + 1, :] += rsum(d_zi)
        dwa_ref[...] += _dot(ca, d_zr, _TN)
        dwx_ref[...] += _dot(ca, d_zi, _TN)
        d_ca = d_ca + _dot(d_zi, wx_ref[...], _NT) + _dot(d_zr, wa_ref[...], _NT)
        dvec_ref[V_CAB:V_CAB + 1, :] += rsum(d_ca)
        ga[0:t, :] = d_ca
        d_ax = jnp.zeros((t, BW), F32)
        for k in range(CONV_A):
            d_ax = d_ax + cw_ref[CW_A + k:CW_A + k + 1, :] * ga[pl.ds(CONV_A - 1 - k, t), :]
            dcw_ref[CW_A + k:CW_A + k + 1, :] += rsum(d_ca * bufa[pl.ds(HALO - (CONV_A - 1) + k, t), :])
        ga[t:t + 8, :] = d_ca[0:8, :]
        put(C_AX, d_ax)
        dpb = dpre_ref[:, BW:2 * BW].astype(F32)
        put(C_BB, dpb * v["cb"])
        d_cb = dpb * cur(C_BB)
        gb[0:t, :] = d_cb
        d_cbin = jnp.zeros((t, BW), F32)
        for k in range(CONV_B):
            d_cbin = d_cbin + cw_ref[CW_B + k:CW_B + k + 1, :] * gb[pl.ds(CONV_B - 1 - k, t), :]
            dcw_ref[CW_B + k:CW_B + k + 1, :] += rsum(d_cb * bufb[pl.ds(HALO - (CONV_B - 1) + k, t), :])
        gb[t:t + 8, :] = d_cb[0:8, :]
        put(C_BC, d_cbin * cur(C_BV))
        put(C_BV, d_cbin * cur(C_BC))
        dpd = dpre_ref[:, 3 * BW:4 * BW].astype(F32)
        ln, xh, rstd, s2 = v["ln"], v["xh"], v["rstd"], v["s2"]
        sg = _sigmoid(ln)
        d_ln = dpd * sg * (1.0 + ln * (1.0 - sg))
        dvec_ref[V_LNG:V_LNG + 1, :] += rsum(d_ln * xh)
        dvec_ref[V_LNB:V_LNB + 1, :] += rsum(d_ln)
        d_xh = d_ln * vec_ref[V_LNG:V_LNG + 1, :]
        d_cd = rstd * (d_xh - jnp.mean(d_xh, axis=-1, keepdims=True)
                       - xh * jnp.mean(d_xh * xh, axis=-1, keepdims=True))
        dvec_ref[V_CDB:V_CDB + 1, :] += rsum(d_cd)
        gd[0:t, :] = d_cd
        _shifted_copies(gd, xg, t + HALO)
        d_dg = jnp.zeros((t, BW), F32)
        for k in range(CONV_D):
            d_dg = d_dg + cw_ref[CW_D + k:CW_D + k + 1, :] * _window(gd, xg, CONV_D - 1 - k, t)
            dcw_ref[CW_D + k:CW_D + k + 1, :] += rsum(d_cd * _window(bufd, xd, HALO - (CONV_D - 1) + k, t))
        gd[t:t + HALO, :] = d_cd[0:HALO, :]
        put(C_D1, d_dg * s2)
        put(C_D2, d_dg * cur(C_D1) * s2 * (1.0 - s2))
        dp_ref[:, C_Q:C_Q + BW] = dq_ref[...]
        dkp2 = jnp.where(step == 0, 0.0, dkp2_ref[...].astype(F32))
        dp_ref[0:ATT_BLK, C_K:C_K + 256] = (dkc_ref[0:ATT_BLK, :].astype(F32) + dkp1_ref[...].astype(F32)).astype(BF16)
        dp_ref[ATT_BLK:t, C_K:C_K + 256] = (dkc_ref[ATT_BLK:t, :].astype(F32) + dkp2).astype(BF16)

    rev = lambda i: nt - 1 - i
    full = lambda r, c: pl.BlockSpec((r, c), lambda i: (0, 0))
    return _call(
        body, comm, (proj, proj, dpre, h, h, dq, dkc, dkp, dkp, convw, vecs, wx_bd, wa_bd, dproj), grid=(nt,),
        in_specs=[pl.BlockSpec((t, GL0), lambda i: (rev(i), 0)),
                  pl.BlockSpec((HALO, GL0), lambda i: (jnp.maximum(rev(i) * hb - 1, 0), 0)),
                  pl.BlockSpec((t, 4 * BW), lambda i: (rev(i), 0)),
                  pl.BlockSpec((t, BW), lambda i: (rev(i), 0)),
                  pl.BlockSpec((8, BW), lambda i: (jnp.maximum(rev(i) * (t // 8) - 1, 0), 0)),
                  pl.BlockSpec((t, BW), lambda i: (rev(i), 0)),
                  pl.BlockSpec((t, 256), lambda i: (rev(i), 0)),
                  pl.BlockSpec((ATT_BLK, 256), lambda i: (2 * rev(i) + 1, 0)),
                  pl.BlockSpec((ATT_BLK, 256), lambda i: (jnp.minimum(2 * rev(i) + 2, nb - 1), 0)),
                  pl.BlockSpec((None, CW_ROWS, BW), lambda i: (l, 0, 0)),
                  pl.BlockSpec((None, V_ROWS, BW), lambda i: (l, 0, 0)),
                  pl.BlockSpec((None, BW, BW), lambda i: (l, 0, 0)),
                  pl.BlockSpec((None, BW, BW), lambda i: (l, 0, 0)),
                  pl.BlockSpec(memory_space=pl.ANY)],
        out_specs=[pl.BlockSpec((t, GL0), lambda i: (rev(i), 0)),
                   full(CW_ROWS, BW), full(V_ROWS, BW), full(BW, BW), full(BW, BW)],
        out_shape=[_sds((s, IN_W), BF16), _sds((CW_ROWS, BW), F32), _sds((V_ROWS, BW), F32),
                   _sds((BW, BW), F32), _sds((BW, BW), F32)],
        scratch_shapes=[pltpu.VMEM((t + HALO, BW), F32)] * 3 + [pltpu.VMEM((7, t + HALO - 8, BW), F32)] * 2
        + [pltpu.VMEM((t + 8, BW), F32), pltpu.VMEM((t + 8, BW), F32)]
        + [pltpu.VMEM((t, BW), F32)] * 3
        + [pltpu.VMEM((t + 8, BW), F32), pltpu.VMEM((t + 8, BW), F32), pltpu.VMEM((t + HALO, BW), F32),
           pltpu.VMEM((1, BW), F32)],
        aliases={13: 0}, name=f"bwd_branch{l}")


def bwd_proj(dproj, x, dh1, g1, wt_in, l, comm=None):
    s = x.shape[0]
    tm = min(512, s)
    ck = 1408
    n_j, n_i = IN_W // ck, s // tm

    def body(dp_ref, x_ref, dh_ref, g_ref, w_ref, dx_ref, dw_ref, st_ref, dxn, xn_b, acc):
        j, i = pl.program_id(0), pl.program_id(1)
        rows = pl.ds(pl.multiple_of(i * tm, tm), tm)
        g = g_ref[l:l + 1, :]

        @pl.when(j == 0)
        def _():
            xv = x_ref[...]
            r = lax.rsqrt(jnp.mean(xv * xv, axis=-1, keepdims=True) + EPS)
            xn_b[rows, :] = (xv * r * g).astype(BF16)
            dxn[rows, :] = jnp.zeros((tm, D), F32)

        @pl.when((j == 0) & (i == 0))
        def _():
            st_ref[...] = jnp.zeros((8, D), F32)

        @pl.when(i == 0)
        def _():
            acc[...] = jnp.zeros((ck, D), F32)

        dp = dp_ref[...]
        dxn[rows, :] += _dot(dp, w_ref[...], _NN)
        acc[...] += _dot(dp, xn_b[rows, :], _TN)

        @pl.when(i == n_i - 1)
        def _():
            dw_ref[...] = acc[...].astype(BF16)

        @pl.when(j == n_j - 1)
        def _():
            xv = x_ref[...]
            r = lax.rsqrt(jnp.mean(xv * xv, axis=-1, keepdims=True) + EPS)
            n = xv * r
            dv = dxn[rows, :]
            dn = dv * g
            dx_ref[...] = dh_ref[...] + r * (dn - n * jnp.mean(dn * n, axis=-1, keepdims=True))
            st_ref[0:1, :] += jnp.sum(dv * n, axis=0, keepdims=True)

    lastrow = lambda j, i: (jnp.where(j == n_j - 1, i, 0), 0)
    return _call(
        body, comm, (dproj, x, dh1, g1, wt_in), grid=(n_j, n_i),
        in_specs=[pl.BlockSpec((tm, ck), lambda j, i: (i, j)),
                  pl.BlockSpec((tm, D), lambda j, i: (_edge_index(j, i, n_j, n_i), 0)),
                  pl.BlockSpec((tm, D), lastrow),
                  pl.BlockSpec((DEPTH, D), lambda j, i: (0, 0)),
                  pl.BlockSpec((ck, D), lambda j, i: (j, 0))],
        out_specs=[pl.BlockSpec((tm, D), lastrow), pl.BlockSpec((ck, D), lambda j, i: (j, 0)),
                   pl.BlockSpec((8, D), lambda j, i: (0, 0))],
        out_shape=[_sds((s, D), F32), _sds((IN_W, D), BF16), _sds((8, D), F32)],
        scratch_shapes=[pltpu.VMEM((s, D), F32), pltpu.VMEM((s, D), BF16), pltpu.VMEM((ck, D), F32)],
        name=f"bwd_proj{l}")


def bwd_proj_w(dproj, x, g1, l, half, comm=None):
    s = x.shape[0]
    tm = min(1024, s)
    ck = 1408
    c0, hw = W_IN_PARTS[half]
    n_j, n_i = IN_W // ck, s // tm

    def body(dp_ref, x_ref, g_ref, dw_ref, xn_b, acc):
        j, i = pl.program_id(0), pl.program_id(1)
        rows = pl.ds(pl.multiple_of(i * tm, tm), tm)

        @pl.when(j == 0)
        def _():
            xv = x_ref[...]
            r = lax.rsqrt(jnp.mean(xv * xv, axis=-1, keepdims=True) + EPS)
            xn_b[rows, :] = (xv * r * g_ref[l:l + 1, :])[:, c0:c0 + hw].astype(BF16)

        @pl.when(i == 0)
        def _():
            acc[...] = jnp.zeros((ck, hw), F32)

        acc[...] += _dot(dp_ref[...], xn_b[rows, :], _TN)

        @pl.when(i == n_i - 1)
        def _():
            dw_ref[...] = acc[...].astype(BF16)

    return _call(
        body, comm, (dproj, x, g1), grid=(n_j, n_i),
        in_specs=[pl.BlockSpec((tm, ck), lambda j, i: (i, j)),
                  pl.BlockSpec((tm, D), lambda j, i: (jnp.where(j == 0, i, n_i - 1), 0)),
                  pl.BlockSpec((DEPTH, D), lambda j, i: (0, 0))],
        out_specs=pl.BlockSpec((ck, hw), lambda j, i: (j, 0)),
        out_shape=_sds((IN_W, hw), BF16),
        scratch_shapes=[pltpu.VMEM((s, hw), BF16), pltpu.VMEM((ck, hw), F32)],
        name=f"bwd_proj_w{half}_{l}")


def bwd_proj_x(dproj, x, dh1, g1, wt_in, l, comm=None):
    s = x.shape[0]
    tm = min(512, s)
    ck = 1408
    n_j, n_i = IN_W // ck, s // tm

    def body(dp_ref, x_ref, dh_ref, g_ref, w_ref, dx_ref, st_ref, dxn):
        j, i = pl.program_id(0), pl.program_id(1)
        rows = pl.ds(pl.multiple_of(i * tm, tm), tm)
        g = g_ref[l:l + 1, :]

        @pl.when((j == 0) & (i == 0))
        def _():
            st_ref[...] = jnp.zeros((8, D), F32)

        part = _dot(dp_ref[...], w_ref[...], _NN)

        @pl.when(j == 0)
        def _():
            dxn[rows, :] = part

        @pl.when(j > 0)
        def _():
            dxn[rows, :] += part

        @pl.when(j == n_j - 1)
        def _():
            xv = x_ref[...]
            r = lax.rsqrt(jnp.mean(xv * xv, axis=-1, keepdims=True) + EPS)
            n = xv * r
            dv = dxn[rows, :]
            dn = dv * g
            dx_ref[...] = dh_ref[...] + r * (dn - n * jnp.mean(dn * n, axis=-1, keepdims=True))
            st_ref[0:1, :] += jnp.sum(dv * n, axis=0, keepdims=True)

    lastrow = lambda j, i: (jnp.where(j == n_j - 1, i, 0), 0)
    return _call(
        body, comm, (dproj, x, dh1, g1, wt_in), grid=(n_j, n_i),
        in_specs=[pl.BlockSpec((tm, ck), lambda j, i: (i, j)), pl.BlockSpec((tm, D), lastrow),
                  pl.BlockSpec((tm, D), lastrow),
                  pl.BlockSpec((DEPTH, D), lambda j, i: (0, 0)), pl.BlockSpec((ck, D), lambda j, i: (j, 0))],
        out_specs=[pl.BlockSpec((tm, D), lastrow), pl.BlockSpec((8, D), lambda j, i: (0, 0))],
        out_shape=[_sds((s, D), F32), _sds((8, D), F32)],
        scratch_shapes=[pltpu.VMEM((s, D), F32)], name=f"bwd_proj_x{l}")


def _block_diag(w):
    nl, nb, bw, _ = w.shape
    eye = jnp.eye(nb, dtype=w.dtype)
    return jnp.einsum("lhij,hk->lhikj", w, eye).reshape(nl, nb * bw, nb * bw).astype(BF16)


class NoOverlap:
    def __init__(self, big):
        self.big = big

    def weights(self, l):
        return self.big[l]

    def job(self, slot, l):
        return None

    def done(self, slot, l, results):
        pass

    def new_grads(self, group, l, grads):
        pass

    def new_small(self, l, arrays, head_stats):
        pass


def local_step(x, target, norm1_g, norm2_g, final_g, convw, vecs, lru_wx, lru_wa, plan):
    wx_bd, wa_bd = _block_diag(lru_wx), _block_diag(lru_wa)

    def run(fn, slot, l, *args):
        res, cres = fn(*args, l, comm=plan.job(slot, l))
        plan.done(slot, l, cres)
        return res

    saved = []
    for l in range(DEPTH):
        proj = run(fwd_proj, "fwd_proj", l, x, norm1_g, plan.weights(l)["in_t"])
        pre_abd, h = run(fwd_branch, "fwd_branch", l, proj, convw, vecs, wx_bd, wa_bd)
        pre_c = run(fwd_attn, "fwd_attn", l, proj, vecs)
        w = plan.weights(l)
        y4, merged, h1 = run(fwd_merge, "fwd_merge", l, x, proj, pre_abd, pre_c, w["a_t"], w["b_t"], w["c_t"], w["d_t"], w["o"])
        w = plan.weights(l)
        x_out, fg, fu = run(fwd_ffn, "fwd_ffn", l, h1, norm2_g, w["gate_t"], w["up_t"], w["down"])
        saved.append((x, proj, pre_abd, h, pre_c, y4, merged, h1, fg, fu))
        x = x_out
    dx, head_stats = loss_head(x, final_g.reshape(1, D), target)
    small = [None] * DEPTH
    for l in reversed(range(DEPTH)):
        x_in, proj, pre_abd, h, pre_c, y4, merged, h1, fg, fu = saved[l]
        w = plan.weights(l)
        dh1, d_gate, d_up, d_down, st_ffn = run(bwd_ffn, "bwd_ffn", l, dx, h1, fg, fu, norm2_g, w["gate_t"], w["up_t"], w["down"])
        plan.new_grads("ffn", l, dict(gate_t=d_gate, up_t=d_up, down=d_down))
        dproj, dpre, d_o, d_a, d_b, d_c, d_d = run(
            bwd_merge, "bwd_merge", l, dh1, y4, proj, merged, pre_abd, pre_c, w["a_t"], w["b_t"], w["c_t"], w["d_t"], w["o"])
        plan.new_grads("out", l, dict(a_t=d_a, b_t=d_b, c_t=d_c, d_t=d_d, o=d_o))
        dq, dkc, dkp, st_attn = run(bwd_attn, "bwd_attn", l, proj, dpre, vecs)
        dproj, dcw, dvec, dwx, dwa = run(bwd_branch, "bwd_branch", l, proj, dproj, dpre, h, dq, dkc, dkp, convw, vecs, wx_bd, wa_bd)
        if l > 0:
            dx, d_in, st_proj = run(bwd_proj, "bwd_proj", l, dproj, x_in, dh1, norm1_g, w["in_t"])
            plan.new_grads("in", l, dict(in_t=d_in))
        else:
            for half, name in enumerate(("in_a", "in_b")):
                d_half = run(functools.partial(bwd_proj_w, half=half), f"bwd_proj_w{half}", l, dproj, x_in, norm1_g)
                plan.new_grads(name, l, {name: d_half})
            dx, st_proj = run(bwd_proj_x, "bwd_proj_x", l, dproj, x_in, dh1, norm1_g, w["in_t"])
        small[l] = (st_proj, st_ffn, dvec, st_attn, dcw, dwx, dwa)
        plan.new_small(l, small[l], head_stats)
    return head_stats, dx, small


BIG = dict(in_t=("w_in", "view"), a_t=("w_a_out", "transpose"), b_t=("w_b_out", "transpose"), c_t=("w_c_out", "transpose"),
           d_t=("w_d_out", "transpose"), o=("w_o", "plain"), gate_t=("w_ffn_gate", "view"), up_t=("w_ffn_up", "view"),
           down=("w_ffn_down", "plain"))


def cast_transpose(w, name):
    nl, a, b = w.shape
    ta = min(256, a)

    def body(w_ref, o_ref):
        o_ref[...] = w_ref[...].T.astype(BF16)

    return pl.pallas_call(
        body, grid=(nl, a // ta),
        in_specs=[pl.BlockSpec((None, ta, b), lambda l, i: (l, i, 0))],
        out_specs=pl.BlockSpec((None, b, ta), lambda l, i: (l, 0, i)),
        out_shape=_sds((nl, b, a), BF16), compiler_params=_cparams(2), name=name)(w)


def add_partials(mine, recv, core, name):
    n = len(mine)

    def body(core_ref, *refs):
        del core_ref
        for a_ref, b_ref, o_ref in zip(refs[:n], refs[n:2 * n], refs[2 * n:]):
            o_ref[...] = (a_ref[...].astype(F32) + b_ref[...].astype(F32)).astype(BF16)

    return pl.pallas_call(
        body,
        grid_spec=pltpu.PrefetchScalarGridSpec(
            num_scalar_prefetch=1, grid=(4,),
            in_specs=[pl.BlockSpec((None, None) + a.shape[2:], lambda i, cr: (i, cr[0], 0, 0)) for a in mine]
            + [pl.BlockSpec((None,) + b.shape[1:], lambda i, cr: (i, 0, 0)) for b in recv],
            out_specs=[pl.BlockSpec((None,) + b.shape[1:], lambda i, cr: (i, 0, 0)) for b in recv]),
        out_shape=[_sds(b.shape, BF16) for b in recv], compiler_params=_cparams(1), name=name)(core, *mine, *recv)


def _adamw(w, g, m, v):
    m = ADAM_B1 * m + (1.0 - ADAM_B1) * g
    v = ADAM_B2 * v + (1.0 - ADAM_B2) * (g * g)
    m_hat = m / (1.0 - ADAM_B1 ** ADAM_STEP)
    v_hat = v / (1.0 - ADAM_B2 ** ADAM_STEP)
    delta = -ADAM_LR * (m_hat / (jnp.sqrt(v_hat) + ADAM_EPS) + ADAM_WD * w)
    return delta, m, v


def adamw_big(contrib, w, m, v, transposed, name, comm=None):
    nsrc, nl, rows, cols = contrib.shape
    ct = 256

    def body(c_ref, w_ref, m_ref, v_ref, g_out, d_out, m_out, v_out):
        g = c_ref[0].astype(F32)
        for src in range(1, nsrc):
            g = g + c_ref[src].astype(F32)
        if transposed:
            g = g.T
        delta, mn, vn = _adamw(w_ref[...], g, m_ref[...], v_ref[...])
        g_out[...] = g
        d_out[...] = delta
        m_out[...] = mn
        v_out[...] = vn

    if transposed:
        wspec = pl.BlockSpec((None, ct, rows), lambda l, j: (l, j, 0))
    else:
        wspec = pl.BlockSpec((None, rows, ct), lambda l, j: (l, 0, j))
    return _call(
        body, comm, (contrib, w, m, v), grid=(nl, cols // ct),
        in_specs=[pl.BlockSpec((nsrc, None, rows, ct), lambda l, j: (0, l, 0, j)), wspec, wspec, wspec],
        out_specs=[wspec] * 4, out_shape=[_sds(w.shape, F32)] * 4, name=name)


VEC_NAMES = ("conv_a_b", "lru_bx", "lru_ba", "lru_lambda", "conv_d_b", "ln_d_g", "ln_d_b")
P_N1, P_N2, P_VEC, P_CONV, P_LRU = 0, 1, 2, 6, 6 + CW_ROWS
P_FINAL, P_LOSS, P_ROWS = P_LRU + HD, P_LRU + HD + 1, P_LRU + HD + 2
SMALL = ("norm1_g", "conv_a_w", "conv_a_b", "lru_wx", "lru_bx", "lru_wa", "lru_ba", "lru_lambda", "conv_b_w", "sinks",
         "conv_d_w", "conv_d_b", "ln_d_g", "ln_d_b", "norm2_g", "final_g")
VMEM_FULL = pl.BlockSpec(memory_space=pltpu.VMEM)


def _stack_vecs(p):
    rows = [p[n] for n in VEC_NAMES] + [jnp.pad(p["sinks"], ((0, 0), (0, BW - N_HEADS)))]
    return jnp.stack(rows, axis=1)


def _stack_convs(p):
    nl, _, ch = p["conv_a_w"].shape
    z = jnp.zeros((nl, 1, ch), F32)
    return jnp.concatenate([p["conv_a_w"], p["conv_b_w"], z, p["conv_d_w"], z], axis=1)


def _vec_place(r):
    return P_VEC + r // 2, (r % 2) * BW


def pack_small(arrays, head_stats, l):
    n = len(arrays)

    def body(*refs):
        st_proj, st_ffn, dvec, st_attn, dcw, dwx, dwa = refs[:n]
        pack = refs[-1]
        pack[...] = jnp.zeros((P_ROWS, D), F32)
        lane = lax.broadcasted_iota(jnp.int32, (HD, BW), 1)
        pack[P_N1:P_N1 + 1, :] = st_proj[0:1, :]
        pack[P_N2:P_N2 + 1, :] = st_ffn[0:1, :]
        for r in range(len(VEC_NAMES)):
            row, c0 = _vec_place(r)
            pack[row:row + 1, c0:c0 + BW] = dvec[r:r + 1, :]
        row, c0 = _vec_place(V_SINK)
        pack[row:row + 1, c0:c0 + 128] = st_attn[0:1, :]
        pack[P_CONV:P_CONV + CW_ROWS, 0:BW] = dcw[...]
        for mat, c0 in ((dwx, 0), (dwa, BW)):
            blocks = jnp.zeros((HD, BW), F32)
            for h in range(BW // HD):
                blocks = jnp.where((lane >= HD * h) & (lane < HD * (h + 1)), mat[HD * h:HD * (h + 1), :], blocks)
            pack[P_LRU:P_LRU + HD, c0:c0 + BW] = blocks
        if head_stats is not None:
            pack[P_FINAL:P_LOSS + 1, :] = refs[n][0:2, :]

    flat = list(arrays) + ([] if head_stats is None else [head_stats])
    return pl.pallas_call(body, out_shape=_sds((P_ROWS, D), F32), in_specs=[VMEM_FULL] * len(flat), out_specs=VMEM_FULL,
                          name=f"pack_small{l}", compiler_params=pltpu.CompilerParams(vmem_limit_bytes=VMEM_LIMIT))(*flat)


def adamw_small(gathered, me, w, m, v):
    ns = len(SMALL)

    def body(me_ref, *refs):
        c_refs, refs = refs[:DEPTH], refs[DEPTH:]
        w_refs, m_refs, v_refs = refs[:ns], refs[ns:2 * ns], refs[2 * ns:3 * ns]
        loss_ref, outs, gs = refs[3 * ns], refs[3 * ns + 1:3 * ns + 1 + 4 * ns], refs[-1]
        for l in range(DEPTH):
            gs[l] = c_refs[l][0]
            for dev in range(1, NDEV):
                gs[l] += c_refs[l][dev]
        loss_ref[...] = gs[DEPTH - 1, P_LOSS:P_LOSS + 1, 0:128]

        def update(name, sel, g):
            i = SMALL.index(name)
            delta, mn, vn = _adamw(w_refs[i][sel], g, m_refs[i][sel], v_refs[i][sel])
            for o_ref, val in zip(outs[4 * i:4 * i + 4], (g, delta, mn, vn)):
                o_ref[sel] = val

        update("final_g", (slice(0, 1), slice(None)), gs[DEPTH - 1, P_FINAL:P_FINAL + 1, :])
        shift = (BW - me_ref[0] * (BW // NDEV)) & (BW - 1)
        for l in range(DEPTH):
            row = (slice(l, l + 1), slice(None))
            update("norm1_g", row, gs[l, P_N1:P_N1 + 1, :])
            update("norm2_g", row, gs[l, P_N2:P_N2 + 1, :])
            for r, name in enumerate(VEC_NAMES):
                prow, c0 = _vec_place(r)
                update(name, row, gs[l, prow:prow + 1, c0:c0 + BW])
            prow, c0 = _vec_place(V_SINK)
            update("sinks", row, gs[l, prow:prow + 1, c0:c0 + N_HEADS])
            mine = pltpu.roll(gs[l, P_CONV:P_CONV + CW_ROWS, 0:BW], shift, 1)[:, 0:BW // NDEV]
            update("conv_a_w", (l,), mine[CW_A:CW_A + CONV_A])
            update("conv_b_w", (l,), mine[CW_B:CW_B + CONV_B])
            update("conv_d_w", (l,), mine[CW_D:CW_D + CONV_D])
            for h in range(BW // HD):
                update("lru_wx", (l, h), gs[l, P_LRU:P_LRU + HD, HD * h:HD * (h + 1)])
                update("lru_wa", (l, h), gs[l, P_LRU:P_LRU + HD, BW + HD * h:BW + HD * (h + 1)])

    args = [p[n] for p in (w, m, v) for n in SMALL]
    full = lambda a: pl.BlockSpec(a.shape, lambda i, me_ref: (0,) * a.ndim)
    out_shape = [_sds((1, 128), F32)] + [_sds(w[n].shape, F32) for n in SMALL for _ in range(4)]
    outs = pl.pallas_call(
        body,
        grid_spec=pltpu.PrefetchScalarGridSpec(
            num_scalar_prefetch=1, grid=(1,),
            in_specs=[full(a) for a in list(gathered) + args], out_specs=[full(o) for o in out_shape],
            scratch_shapes=[pltpu.VMEM((DEPTH, P_ROWS, D), F32)]),
        out_shape=out_shape, name="adamw_small", compiler_params=_cparams(1))(me, *gathered, *args)
    return outs[0], {n: outs[1 + 4 * i:5 + 4 * i] for i, n in enumerate(SMALL)}


def merge_jobs(jobs):
    jobs = [j for j in jobs if j is not None]
    if not jobs:
        return None, []
    inputs, aliases, outs, sems, cuts = [], {}, [], [], []
    for j in jobs:
        i0, o0, s0 = len(inputs), len(outs), len(sems)
        aliases.update({i0 + i: o0 + o for i, o in j.aliases.items()})
        inputs += j.inputs
        outs += j.out_shapes
        sems += j.sem_shapes
        cuts.append((i0, len(inputs), o0, len(outs), s0, len(sems)))

    def each(which):
        def go(cins, couts, s):
            for j, (i0, i1, o0, o1, s0, s1) in zip(jobs, cuts):
                if getattr(j, which) is not None:
                    getattr(j, which)(cins[i0:i1], couts[o0:o1], s[s0:s1])
        return go

    relay = each("relay") if any(j.relay is not None for j in jobs) else None
    return CommJob(inputs, aliases, outs, sems, each("start"), each("finish"), relay), [(c[2], c[3]) for c in cuts]


SIXTHS = 6
OUT_KINDS = ("a_t", "b_t", "c_t", "d_t", "o")
GATHER_PLAN = {
    "fwd_proj": [(k, 0, 0, 6) for k in OUT_KINDS] + [("gate_t", 0, 0, 6)],
    "fwd_branch": [("up_t", 0, 0, 6)],
    "fwd_attn": [("down", 0, 0, 6)],
    "fwd_merge": [("in_t", 1, 0, 2)],
    "fwd_ffn": [("in_t", 1, 2, 6)],
}
SIBLING_PLAN = {"bwd_merge": ("ffn", 0), "bwd_branch": ("out", 0), "bwd_ffn": ("in", 1),
                "bwd_proj_w1": ("in_a", 0), "bwd_proj_x": ("in_b", 0)}
GROUPS = dict(ffn=("gate_t", "up_t", "down"), out=OUT_KINDS, in_a=("in_a",), in_b=("in_b",))
GROUPS["in"] = ("in_t",)
COLUMN_HALF = dict(in_a=("in_t", W_IN_PARTS[0][0]), in_b=("in_t", W_IN_PARTS[1][0]))
CHIP_PLAN = {
    "bwd_attn": [("in_t", 1, 3, 6), ("gate_t", 0, 0, 3)],
    "bwd_branch": [("gate_t", 0, 3, 6), ("up_t", 0, 0, 6), ("down", 0, 0, 6)],
    "bwd_proj": [(k, 0, 0, 6) for k in OUT_KINDS],
    "bwd_proj_w0": [(k, 0, 0, 6) for k in OUT_KINDS[:3]],
    "bwd_proj_w1": [(k, 0, 0, 6) for k in OUT_KINDS[3:]],
    "bwd_merge": [("in_t", 1, 0, 3)],
    "bwd_proj_x": [("in_a", 0, 0, 6)],
    "adamw_gate_t": [("in_b", 0, 0, 6)],
}
SMALL_GATHER_PLAN = {"bwd_ffn": 1, "adamw_down": 0}


class Overlap:
    def __init__(self, shards, core):
        self.shards = shards
        self.core = core
        self.gathered = [dict.fromkeys(BIG) for _ in range(DEPTH)]
        self.views = {}
        self.partial = {}
        self.contrib = dict.fromkeys(BIG)
        self.small_packs = [None] * DEPTH
        self.small_gathered = [None] * DEPTH
        self._open = None

    def weights(self, l):
        return self.gathered[l]

    def new_grads(self, group, l, grads):
        for k, g in grads.items():
            self.views[k, l] = g.reshape(4, 2, g.shape[0] // NDEV, g.shape[1])

    def new_small(self, l, arrays, head_stats):
        self.small_packs[l] = pack_small(arrays, head_stats if l == DEPTH - 1 else None, l)

    @staticmethod
    def _rows(shard_rows, f0, f1):
        return shard_rows * f0 // SIXTHS, shard_rows * (f1 - f0) // SIXTHS

    def job(self, slot, l):
        jobs, notes = [], []
        pieces = [(k, l + dl, f0, f1) for k, dl, f0, f1 in GATHER_PLAN.get(slot, []) if l + dl < DEPTH]
        if pieces:
            jobs.append(gather_job([((k, ll), self.shards[ll][k], self.gathered[ll][k],
                                     *self._rows(self.shards[ll][k].shape[0], f0, f1)) for k, ll, f0, f1 in pieces]))
            notes.append(("gather", list(dict.fromkeys((k, ll) for k, ll, _, _ in pieces))))
        if slot in SIBLING_PLAN and l + SIBLING_PLAN[slot][1] < DEPTH:
            group, dl = SIBLING_PLAN[slot]
            keys = [(k, l + dl) for k in GROUPS[group]]
            jobs.append(sibling_exchange_job([self.views[key] for key in keys]))
            notes.append(("sibling", keys))
        pieces = [(k, l + dl, f0, f1) for k, dl, f0, f1 in CHIP_PLAN.get(slot, []) if l + dl < DEPTH]
        if pieces:
            whole = [(*COLUMN_HALF.get(k, (k, 0)), k, ll, f0, f1) for k, ll, f0, f1 in pieces]
            jobs.append(chip_exchange_job([(self.partial[k, ll], self.contrib[kind], kind, ll,
                                            *self._rows(self.partial[k, ll].shape[1], f0, f1), col0, self.shards[ll][kind].shape[1])
                                           for kind, col0, k, ll, f0, f1 in whole]))
            notes.append(("chips", list(dict.fromkeys(kind for kind, *_ in whole))))
        if slot in SMALL_GATHER_PLAN and l + SMALL_GATHER_PLAN[slot] < DEPTH:
            ll = l + SMALL_GATHER_PLAN[slot]
            jobs.append(gather_job([("small", self.small_packs[ll], None, 0, P_ROWS)]))
            notes.append(("small", ll))
        job, spans = merge_jobs(jobs)
        self._open = (slot, l, notes, spans)
        return job

    def done(self, slot, l, results):
        open_slot, open_l, notes, spans = self._open
        assert (open_slot, open_l) == (slot, l)
        for (what, keys), (r0, r1) in zip(notes, spans):
            res = results[r0:r1]
            if what == "gather":
                for (k, ll), g in zip(keys, res):
                    self.gathered[ll][k] = g
            elif what == "sibling":
                sums = add_partials([self.views[key] for key in keys], list(res), self.core, f"chip_sum_{keys[0][0]}{keys[0][1]}")
                self.partial.update(zip(keys, sums))
            elif what == "chips":
                for k, c in zip(keys, res):
                    self.contrib[k] = c
            else:
                self.small_gathered[keys], = res


SMALL = ("norm1_g", "conv_a_w", "conv_a_b", "lru_wx", "lru_bx", "lru_wa", "lru_ba", "lru_lambda", "conv_b_w", "sinks",
         "conv_d_w", "conv_d_b", "ln_d_g", "ln_d_b", "norm2_g", "final_g")
WEIGHTS = ("norm1_g", "w_in", "conv_a_w", "conv_a_b", "lru_wx", "lru_bx", "lru_wa", "lru_ba", "lru_lambda", "w_a_out",
           "conv_b_w", "w_b_out", "sinks", "w_c_out", "conv_d_w", "conv_d_b", "ln_d_g", "ln_d_b", "w_d_out", "w_o",
           "norm2_g", "w_ffn_gate", "w_ffn_up", "w_ffn_down", "final_g")


def kernel(x, norm1_g, w_in, conv_a_w, conv_a_b, lru_wx, lru_bx, lru_wa, lru_ba, lru_lambda, w_a_out, conv_b_w, w_b_out, sinks, w_c_out, conv_d_w, conv_d_b, ln_d_g, ln_d_b, w_d_out, w_o, norm2_g, w_ffn_gate, w_ffn_up, w_ffn_down, final_g, loss_target, m_norm1_g, m_w_in, m_conv_a_w, m_conv_a_b, m_lru_wx, m_lru_bx, m_lru_wa, m_lru_ba, m_lru_lambda, m_w_a_out, m_conv_b_w, m_w_b_out, m_sinks, m_w_c_out, m_conv_d_w, m_conv_d_b, m_ln_d_g, m_ln_d_b, m_w_d_out, m_w_o, m_norm2_g, m_w_ffn_gate, m_w_ffn_up, m_w_ffn_down, m_final_g, v_norm1_g, v_w_in, v_conv_a_w, v_conv_a_b, v_lru_wx, v_lru_bx, v_lru_wa, v_lru_ba, v_lru_lambda, v_w_a_out, v_conv_b_w, v_w_b_out, v_sinks, v_w_c_out, v_conv_d_w, v_conv_d_b, v_ln_d_g, v_ln_d_b, v_w_d_out, v_w_o, v_norm2_g, v_w_ffn_gate, v_w_ffn_up, v_w_ffn_down, v_final_g):
    args = dict(locals())
    w = {n: args[n] for n in WEIGHTS}
    m = {n: args["m_" + n] for n in WEIGHTS}
    v = {n: args["v_" + n] for n in WEIGHTS}
    me = _dev_index(*_mesh_pos())

    def rows_major(a, how):
        return jnp.swapaxes(a, 1, 2) if how == "view" else a

    stacked = {k: cast_transpose(w[n], "prep_" + k) if how == "transpose" else rows_major(w[n], how).astype(BF16)
               for k, (n, how) in BIG.items()}
    plan = Overlap([{k: stacked[k][l] for k in BIG} for l in range(DEPTH)], lax.axis_index("c").astype(jnp.int32).reshape(1))
    convs = jnp.pad(_stack_convs(w).reshape(DEPTH * CW_ROWS, BW // NDEV), ((0, 0), (0, 256 - BW // NDEV)))
    g_in0, g_conv = _comm_only(gather_job([(("in_t", 0), plan.shards[0]["in_t"], None, 0, plan.shards[0]["in_t"].shape[0]),
                                           ("convs", convs, None, 0, convs.shape[0])]), "gather_first")
    plan.gathered[0]["in_t"] = g_in0
    convw = g_conv[:, :BW // NDEV].reshape(NDEV, DEPTH, CW_ROWS, BW // NDEV).transpose(1, 2, 0, 3).reshape(DEPTH, CW_ROWS, BW)

    vecs = _stack_vecs(w)
    head_stats, grad_x, grads = local_step(x[0], loss_target[0], norm1_g, norm2_g, final_g, convw, vecs, lru_wx, lru_wa, plan)


    out = {}
    for k in ("down", "gate_t", "up_t", "o", "a_t", "b_t", "c_t", "d_t", "in_t"):
        n, how = BIG[k]
        res, cres = adamw_big(plan.contrib[k], rows_major(w[n], how), rows_major(m[n], how), rows_major(v[n], how),
                              how == "transpose", "adamw_" + k, comm=plan.job("adamw_" + k, 0))
        plan.done("adamw_" + k, 0, cres)
        out[n] = [rows_major(r, how) for r in res]

    def own_shapes(p):
        return {n: p[n].reshape(1, D) if n == "final_g" else p[n] for n in SMALL}

    loss, small = adamw_small([g.reshape(NDEV, P_ROWS, D) for g in plan.small_gathered], me.astype(jnp.int32).reshape(1),
                              own_shapes(w), own_shapes(m), own_shapes(v))
    for n in SMALL:
        out[n] = [r.reshape(w[n].shape) for r in small[n]]
    loss = loss[0, 0]
    return (loss, grad_x[None], *[out[n][0] for n in WEIGHTS], *[out[n][1] for n in WEIGHTS],
            *[out[n][2] for n in WEIGHTS], *[out[n][3] for n in WEIGHTS])
```

```python
import functools

import jax
import jax.numpy as jnp
from jax import lax
from jax.experimental import pallas as pl
from jax.experimental.pallas import tpu as pltpu

F32 = jnp.float32
BF16 = jnp.bfloat16
E = pl.Element

D = 1024
BW = 512
IN_W = 8448
GL0 = 4352
FF = 2816
N_HEADS = 8
N_KV = 2
HD = 64
ATT_BLK = 128
EPS = 1e-6
LRU_C = 8.0
NEG_INF = -1e30
DEPTH = 2
NDEV = 8
CONV_A, CONV_B, CONV_D = 4, 3, 31
C_AX, C_AG, C_BV, C_BC, C_BB, C_Q, C_K, C_V, C_D1, C_D2 = 0, 512, 1024, 1536, 2048, 2560, 3072, 3200, 3328, 3840
CW_A, CW_B, CW_D, CW_ROWS = 0, 4, 8, 40
V_CAB, V_BX, V_BA, V_LAM, V_CDB, V_LNG, V_LNB, V_SINK, V_ROWS = 0, 1, 2, 3, 4, 5, 6, 7, 8
HALO = 32
W_IN_PARTS = ((0, 768), (768, 256))

ADAM_LR, ADAM_B1, ADAM_B2, ADAM_EPS, ADAM_WD, ADAM_STEP = 0.001, 0.9, 0.999, 1e-08, 0.01, 10

VMEM_LIMIT = 56 * 1024 * 1024

_NN = (((1,), (0,)), ((), ()))
_NT = (((1,), (1,)), ((), ()))
_TN = (((0,), (0,)), ((), ()))


def _dot(a, b, dims):
    return lax.dot_general(a.astype(BF16), b.astype(BF16), dims, preferred_element_type=F32)


def _cparams(n_axes):
    return pltpu.CompilerParams(dimension_semantics=("arbitrary",) * n_axes, vmem_limit_bytes=VMEM_LIMIT)


def _sds(shape, dtype):
    return jax.ShapeDtypeStruct(tuple(shape), dtype)


def _sigmoid(x):
    return jax.nn.sigmoid(x)


def _neg_expm1(x):
    p = x * (1.0 + x * (0.5 + x * (1.0 / 6.0 + x * (1.0 / 24.0 + x * (1.0 / 120.0)))))
    return jnp.where(x > -0.1, -p, 1.0 - jnp.exp(x))


def _softplus(z):
    return jnp.maximum(z, 0.0) + jnp.log1p(jnp.exp(-jnp.abs(z)))


def _gelu_and_grad(x):
    c = 0.7978845608028654
    inner = c * (x + 0.044715 * x * x * x)
    t = jnp.tanh(inner)
    g = 0.5 * x * (1.0 + t)
    dg = 0.5 * (1.0 + t) + 0.5 * x * (1.0 - t * t) * c * (1.0 + 3.0 * 0.044715 * x * x)
    return g, dg


ANY = pl.BlockSpec(memory_space=pl.ANY)
MESH = pl.DeviceIdType.MESH


def _mesh_pos():
    return lax.axis_index("x"), lax.axis_index("y"), lax.axis_index("c")


def _dev_index(px, py, pc):
    return 4 * px + 2 * py + pc


class CommJob:
    def __init__(self, inputs, aliases, out_shapes, sem_shapes, start, finish, relay=None):
        self.inputs, self.aliases, self.out_shapes, self.sem_shapes = list(inputs), dict(aliases), list(out_shapes), list(sem_shapes)
        self.start, self.finish, self.relay = start, finish, relay


def _call(body, comm, args, *, grid, in_specs, out_specs, out_shape, scratch_shapes=(), name, aliases=None):
    single = not isinstance(out_shape, (list, tuple))
    out_specs = [out_specs] if single else list(out_specs)
    out_shape = [out_shape] if single else list(out_shape)
    scratch_shapes = list(scratch_shapes)
    n_in, n_out, n_scr, n_axes = len(in_specs), len(out_shape), len(scratch_shapes), len(grid)
    params = pltpu.CompilerParams(dimension_semantics=("arbitrary",) * n_axes, vmem_limit_bytes=VMEM_LIMIT)
    io_aliases = dict(aliases or {})
    if comm is None:
        outs = pl.pallas_call(body, grid=grid, in_specs=in_specs, out_specs=out_specs, out_shape=out_shape,
                              scratch_shapes=scratch_shapes, input_output_aliases=io_aliases, compiler_params=params,
                              name=name)(*args)
        return (outs[0] if single else outs), []
    c_in, c_out = len(comm.inputs), len(comm.out_shapes)
    io_aliases.update({n_in + i: n_out + o for i, o in comm.aliases.items()})

    def wrapped(*refs):
        ins, cins = refs[:n_in], refs[n_in:n_in + c_in]
        outs = refs[n_in + c_in:n_in + c_in + n_out]
        couts = refs[n_in + c_in + n_out:n_in + c_in + n_out + c_out]
        rest = refs[n_in + c_in + n_out + c_out:]
        scr, sems = rest[:n_scr], rest[n_scr:]
        first = functools.reduce(lambda a, b: a & b, [pl.program_id(a) == 0 for a in range(n_axes)])
        last = functools.reduce(lambda a, b: a & b, [pl.program_id(a) == pl.num_programs(a) - 1 for a in range(n_axes)])

        @pl.when(first)
        def _():
            comm.start(cins, couts, sems)

        if comm.relay is not None:
            step = functools.reduce(lambda a, b: a * grid[b] + pl.program_id(b), range(1, n_axes), pl.program_id(0))
            n_steps = functools.reduce(lambda a, b: a * b, grid)

            @pl.when(step == 2 * n_steps // 3)
            def _():
                comm.relay(cins, couts, sems)

        body(*ins, *outs, *scr)

        @pl.when(last)
        def _():
            comm.finish(cins, couts, sems)

    outs = pl.pallas_call(
        wrapped, grid=grid, in_specs=list(in_specs) + [ANY] * c_in, out_specs=out_specs + [ANY] * c_out,
        out_shape=out_shape + comm.out_shapes, scratch_shapes=scratch_shapes + comm.sem_shapes,
        input_output_aliases=io_aliases, compiler_params=params, name=name)(*args, *comm.inputs)
    res, cres = outs[:n_out], outs[n_out:]
    return (res[0] if single else res), cres


def _comm_only(comm, name):
    c_in, c_out = len(comm.inputs), len(comm.out_shapes)

    def body(*refs):
        cins, couts, sems = refs[:c_in], refs[c_in:c_in + c_out], refs[c_in + c_out:]
        comm.start(cins, couts, sems)
        if comm.relay is not None:
            comm.relay(cins, couts, sems)
        comm.finish(cins, couts, sems)

    return pl.pallas_call(body, in_specs=[ANY] * c_in, out_specs=[ANY] * c_out, out_shape=comm.out_shapes,
                          scratch_shapes=comm.sem_shapes, input_output_aliases=comm.aliases, name=name)(*comm.inputs)


def gather_job(pieces):
    inputs, aliases, out_shapes, plan, where = [], {}, [], [], {}
    for key, shard, gathered, row0, nrows in pieces:
        if key not in where:
            where[key] = (len(inputs), len(out_shapes))
            inputs.append(shard)
            if gathered is not None:
                aliases[len(inputs)] = len(out_shapes)
                inputs.append(gathered)
            out_shapes.append(_sds((NDEV * shard.shape[0], shard.shape[1]), shard.dtype))
        plan.append((*where[key], shard.shape[0], row0, nrows))
    n = len(plan)

    def copies(cins, couts, sems):
        send_sems, recv_sems, local_sems = sems
        x, y, c = _mesh_pos()
        me, sibling = (x, y, c), (x, y, 1 - c)
        xn, yn, dg = (1 - x, y), (x, 1 - y), (1 - x, 1 - y)
        local, first, pass1, pass2, got_ici, got_fwd, got_d2d = [], [], [], [], [], [], []
        for p, (i_shard, i_out, rows, row0, nrows) in enumerate(plan):
            src = cins[i_shard].at[pl.ds(row0, nrows), :]
            half = cins[i_shard].shape[1] // 2
            left, right, whole = pl.ds(0, half), pl.ds(half, half), slice(None)

            def slot(dev, lanes, i_out=i_out, rows=rows, row0=row0, nrows=nrows):
                return couts[i_out].at[pl.ds(_dev_index(*dev) * rows + row0, nrows), lanes]

            def copy(g, dev, to, lanes=whole, src=None, p=p, slot=slot):
                return pltpu.make_async_remote_copy(
                    src_ref=slot(dev, lanes) if src is None else src, dst_ref=slot(dev, lanes),
                    send_sem=send_sems.at[g, p], recv_sem=recv_sems.at[g, p], device_id=to, device_id_type=MESH)

            local.append(pltpu.make_async_copy(src, slot(me, whole), local_sems.at[p]))
            first += [copy(0, me, sibling, src=src), copy(1, me, (*xn, c), src=src), copy(2, me, (*yn, c), src=src)]
            got_ici += [copy(1, (*xn, c), me), copy(2, (*yn, c), me)]
            pass1 += [copy(3, (*xn, c), (*yn, c), left), copy(4, (*yn, c), (*xn, c), right),
                      copy(5, (*xn, c), sibling), copy(6, (*yn, c), sibling)]
            got_fwd += [copy(3, (*dg, c), me, left), copy(4, (*dg, c), me, right)]
            pass2 += [copy(7, (*dg, c), sibling, left), copy(8, (*dg, c), sibling, right)]
            got_d2d += [copy(0, sibling, me), copy(5, (*xn, 1 - c), me), copy(6, (*yn, 1 - c), me),
                        copy(7, (*dg, 1 - c), me, left), copy(8, (*dg, 1 - c), me, right)]
        return local, first, pass1, pass2, got_ici, got_fwd, got_d2d

    def start(cins, couts, sems):
        local, first, *_ = copies(cins, couts, sems)
        for cp in local + first:
            cp.start()

    def pass_on(cins, couts, sems):
        _, _, pass1, _, got_ici, _, _ = copies(cins, couts, sems)
        for cp in got_ici:
            cp.wait_recv()
        for cp in pass1:
            cp.start()

    def finish(cins, couts, sems):
        local, first, pass1, pass2, _, got_fwd, got_d2d = copies(cins, couts, sems)
        for cp in got_fwd:
            cp.wait_recv()
        for cp in pass2:
            cp.start()
        for cp in got_d2d:
            cp.wait_recv()
        for cp in first + pass1 + pass2:
            cp.wait_send()
        for cp in local:
            cp.wait()

    sem_shapes = [pltpu.SemaphoreType.DMA((9, n)), pltpu.SemaphoreType.DMA((9, n)), pltpu.SemaphoreType.DMA((n,))]
    return CommJob(inputs, aliases, out_shapes, sem_shapes, start, finish, relay=pass_on)


def sibling_exchange_job(grads):
    n = len(grads)

    def copies(cins, couts, sems):
        send_sems, recv_sems = sems
        x, y, c = _mesh_pos()
        return [pltpu.make_async_remote_copy(
            src_ref=cins[q].at[:, 1 - c], dst_ref=couts[q], send_sem=send_sems.at[q], recv_sem=recv_sems.at[q],
            device_id=(x, y, 1 - c), device_id_type=MESH) for q in range(n)]

    def start(cins, couts, sems):
        for cp in copies(cins, couts, sems):
            cp.start()

    def finish(cins, couts, sems):
        cps = copies(cins, couts, sems)
        for cp in cps:
            cp.wait_recv()
        for cp in cps:
            cp.wait_send()

    return CommJob(grads, {}, [_sds((4,) + g.shape[2:], g.dtype) for g in grads],
                   [pltpu.SemaphoreType.DMA((n,)), pltpu.SemaphoreType.DMA((n,))], start, finish)


def chip_exchange_job(pieces):
    inputs, aliases, out_shapes, plan, where = [], {}, [], [], {}
    for partial, contrib, key, layer, row0, nrows, col0, cols in pieces:
        if key not in where:
            where[key] = len(out_shapes)
            out_shapes.append(_sds((4, DEPTH, partial.shape[1], cols), partial.dtype))
            if contrib is not None:
                aliases[len(inputs)] = where[key]
                inputs.append(contrib)
        plan.append((len(inputs), where[key], layer, row0, nrows, col0, partial.shape[2]))
        inputs.append(partial)
    n = len(plan)

    def copies(cins, couts, sems):
        send_sems, recv_sems, local_sems = sems
        x, y, c = _mesh_pos()
        mine = 2 * x + y
        local, sends, recvs = [], [], []
        for p, (i_in, i_out, layer, row0, nrows, col0, ncols) in enumerate(plan):
            rows, lanes = pl.ds(row0, nrows), pl.ds(col0, ncols)
            local.append(pltpu.make_async_copy(cins[i_in].at[mine, rows, :], couts[i_out].at[mine, layer, rows, lanes],
                                               local_sems.at[p]))
            for j, (cx, cy) in enumerate([(1 - x, y), (x, 1 - y), (1 - x, 1 - y)]):
                theirs = 2 * cx + cy

                def copy(slot_there, j=j, p=p, cx=cx, cy=cy, theirs=theirs, i_in=i_in, i_out=i_out, layer=layer,
                         rows=rows, lanes=lanes):
                    return pltpu.make_async_remote_copy(
                        src_ref=cins[i_in].at[theirs, rows, :], dst_ref=couts[i_out].at[slot_there, layer, rows, lanes],
                        send_sem=send_sems.at[j, p], recv_sem=recv_sems.at[j, p], device_id=(cx, cy, c), device_id_type=MESH)
                sends.append(copy(mine))
                recvs.append(copy(theirs))
        return local, sends, recvs

    def start(cins, couts, sems):
        local, sends, _ = copies(cins, couts, sems)
        for cp in local + sends:
            cp.start()

    def finish(cins, couts, sems):
        local, sends, recvs = copies(cins, couts, sems)
        for cp in recvs:
            cp.wait_recv()
        for cp in sends:
            cp.wait_send()
        for cp in local:
            cp.wait()

    sem_shapes = [pltpu.SemaphoreType.DMA((3, n)), pltpu.SemaphoreType.DMA((3, n)), pltpu.SemaphoreType.DMA((n,))]
    return CommJob(inputs, aliases, out_shapes, sem_shapes, start, finish)


def fwd_proj(x, g1, wt_in, l, comm=None):
    s = x.shape[0]
    tm = min(512, s)
    tn = 1408

    def body(x_ref, g_ref, w_ref, o_ref, xn_ref):
        @pl.when(pl.program_id(1) == 0)
        def _():
            xv = x_ref[...]
            r = lax.rsqrt(jnp.mean(xv * xv, axis=-1, keepdims=True) + EPS)
            xn_ref[...] = (xv * r * g_ref[l:l + 1, :]).astype(BF16)

        o_ref[...] = _dot(xn_ref[...], w_ref[...], _NT).astype(BF16)

    return _call(
        body, comm, (x, g1, wt_in), grid=(s // tm, IN_W // tn),
        in_specs=[pl.BlockSpec((tm, D), lambda i, j: (i, 0)),
                  pl.BlockSpec((DEPTH, D), lambda i, j: (0, 0)),
                  pl.BlockSpec((tn, D), lambda i, j: (j, 0))],
        out_specs=pl.BlockSpec((tm, tn), lambda i, j: (i, j)),
        out_shape=_sds((s, IN_W), BF16),
        scratch_shapes=[pltpu.VMEM((tm, D), BF16)], name=f"fwd_proj{l}")


def _scan_fwd(a_ref, u_ref, h_ref, h0, n_rows):
    row = lax.broadcasted_iota(jnp.int32, (8, BW), 0)

    def body(g, hprev):
        r = pl.multiple_of(g * 8, 8)
        a = a_ref[pl.ds(r, 8), :]
        u = u_ref[pl.ds(r, 8), :]
        for sft in (1, 2, 4):
            a_sh = jnp.where(row >= sft, pltpu.roll(a, sft, 0), 1.0)
            u_sh = jnp.where(row >= sft, pltpu.roll(u, sft, 0), 0.0)
            u = u + a * u_sh
            a = a * a_sh
        h = u + a * hprev
        h_ref[pl.ds(r, 8), :] = h
        return h[7:8, :]

    return lax.fori_loop(0, n_rows // 8, body, h0)


def _scan_bwd(b_ref, g_ref, o_ref, c0, n_rows):
    row = lax.broadcasted_iota(jnp.int32, (8, BW), 0)

    def body(k, cnext):
        r = pl.multiple_of((n_rows // 8 - 1 - k) * 8, 8)
        b = b_ref[pl.ds(r, 8), :]
        g = g_ref[pl.ds(r, 8), :]
        for sft in (1, 2, 4):
            b_sh = jnp.where(row < 8 - sft, pltpu.roll(b, 8 - sft, 0), 1.0)
            g_sh = jnp.where(row < 8 - sft, pltpu.roll(g, 8 - sft, 0), 0.0)
            g = g + b * g_sh
            b = b * b_sh
        o = g + b * cnext
        o_ref[pl.ds(r, 8), :] = o
        return o[0:1, :]

    return lax.fori_loop(0, n_rows // 8, body, c0)


def _shifted_copies(buf, shifted, n_rows):
    for r in range(1, 8):
        shifted[r - 1, 0:n_rows - 8, :] = buf[pl.ds(r, n_rows - 8), :]


def _window(buf, shifted, off, t):
    r = off % 8
    return buf[pl.ds(off, t), :] if r == 0 else shifted[r - 1, pl.ds(off - r, t), :]


def _branch_fwd_math(cur_ref, halo_ref, cw_ref, vec_ref, wx_ref, wa_ref, bufa, bufb, bufd, xd, first, t, saved_ref=None):
    def halo(c0):
        v = halo_ref[:, c0:c0 + BW].astype(F32)
        return jnp.where(first, 0.0, v)

    def cur(c0):
        return cur_ref[:, c0:c0 + BW].astype(F32)

    out = {}
    bufa[0:HALO, :] = halo(C_AX)
    bufa[HALO:HALO + t, :] = cur(C_AX)
    ca = jnp.zeros((t, BW), F32) + vec_ref[V_CAB:V_CAB + 1, :]
    for k in range(CONV_A):
        ca = ca + cw_ref[CW_A + k:CW_A + k + 1, :] * bufa[pl.ds(HALO - (CONV_A - 1) + k, t), :]
    if saved_ref is None:
        gi = _sigmoid(_dot(ca, wx_ref[...], _NN) + vec_ref[V_BX:V_BX + 1, :])
        gr = _sigmoid(_dot(ca, wa_ref[...], _NN) + vec_ref[V_BA:V_BA + 1, :])
    else:
        gi, gr = saved_ref[:, BW:2 * BW], saved_ref[:, 2 * BW:3 * BW]
    sp = _softplus(-vec_ref[V_LAM:V_LAM + 1, :])
    la = -LRU_C * sp * gr
    a = jnp.exp(la)
    mult = jnp.sqrt(_neg_expm1(2.0 * la))
    out.update(ca=ca, gi=gi, gr=gr, sp=sp, a=a, mult=mult)
    bufb[0:HALO, :] = halo(C_BC) * halo(C_BV)
    bufb[HALO:HALO + t, :] = cur(C_BC) * cur(C_BV)
    cb = jnp.zeros((t, BW), F32)
    for k in range(CONV_B):
        cb = cb + cw_ref[CW_B + k:CW_B + k + 1, :] * bufb[pl.ds(HALO - (CONV_B - 1) + k, t), :]
    out.update(cb=cb)
    bufd[0:HALO, :] = halo(C_D1) * _sigmoid(halo(C_D2))
    s2 = _sigmoid(cur(C_D2))
    bufd[HALO:HALO + t, :] = cur(C_D1) * s2
    _shifted_copies(bufd, xd, t + HALO)
    if saved_ref is None:
        cd = jnp.zeros((t, BW), F32) + vec_ref[V_CDB:V_CDB + 1, :]
        for k in range(CONV_D):
            cd = cd + cw_ref[CW_D + k:CW_D + k + 1, :] * _window(bufd, xd, HALO - (CONV_D - 1) + k, t)
    else:
        cd = saved_ref[:, 0:BW]
    mu = jnp.mean(cd, axis=-1, keepdims=True)
    xc = cd - mu
    rstd = lax.rsqrt(jnp.mean(xc * xc, axis=-1, keepdims=True) + EPS)
    xh = xc * rstd
    ln = xh * vec_ref[V_LNG:V_LNG + 1, :] + vec_ref[V_LNB:V_LNB + 1, :]
    out.update(s2=s2, xh=xh, rstd=rstd, ln=ln, cd=cd)
    return out


def fwd_branch(proj, convw, vecs, wx_bd, wa_bd, l, comm=None):
    s = proj.shape[0]
    t = min(256, s)

    def body(cur_ref, halo_ref, cw_ref, vec_ref, wx_ref, wa_ref, pre_ref, h_ref, sv_ref, bufa, bufb, bufd, xd, a_s, u_s, hcar):
        first = pl.program_id(0) == 0

        @pl.when(first)
        def _():
            hcar[...] = jnp.zeros((1, BW), F32)

        v = _branch_fwd_math(cur_ref, halo_ref, cw_ref, vec_ref, wx_ref, wa_ref, bufa, bufb, bufd, xd, first, t)
        a_s[...] = v["a"]
        u_s[...] = v["ca"] * v["gi"] * v["mult"]
        sv_ref[:, 0:BW] = v["cd"]
        sv_ref[:, BW:2 * BW] = v["gi"]
        sv_ref[:, 2 * BW:3 * BW] = v["gr"]
        hcar[...] = _scan_fwd(a_s, u_s, h_ref, hcar[...], t)
        gg, _ = _gelu_and_grad(cur_ref[:, C_AG:C_AG + BW].astype(F32))
        pre_ref[:, 0:BW] = (h_ref[...] * gg).astype(BF16)
        pre_ref[:, BW:2 * BW] = (cur_ref[:, C_BB:C_BB + BW].astype(F32) * v["cb"]).astype(BF16)
        ln = v["ln"]
        pre_ref[:, 2 * BW:3 * BW] = (ln * _sigmoid(ln)).astype(BF16)

    hb = t // HALO
    return _call(
        body, comm, (proj, proj, convw, vecs, wx_bd, wa_bd), grid=(s // t,),
        in_specs=[pl.BlockSpec((t, GL0), lambda i: (i, 0)),
                  pl.BlockSpec((HALO, GL0), lambda i: (jnp.maximum(i * hb - 1, 0), 0)),
                  pl.BlockSpec((None, CW_ROWS, BW), lambda i: (l, 0, 0)),
                  pl.BlockSpec((None, V_ROWS, BW), lambda i: (l, 0, 0)),
                  pl.BlockSpec((None, BW, BW), lambda i: (l, 0, 0)),
                  pl.BlockSpec((None, BW, BW), lambda i: (l, 0, 0))],
        out_specs=[pl.BlockSpec((t, 3 * BW), lambda i: (i, 0)), pl.BlockSpec((t, BW), lambda i: (i, 0)),
                   pl.BlockSpec((t, 3 * BW), lambda i: (i, 0))],
        out_shape=[_sds((s, 3 * BW), BF16), _sds((s, BW), F32), _sds((s, 3 * BW), F32)],
        scratch_shapes=[pltpu.VMEM((t + HALO, BW), F32)] * 3 + [pltpu.VMEM((7, t + HALO - 8, BW), F32)]
        + [pltpu.VMEM((t, BW), F32)] * 2 + [pltpu.VMEM((1, BW), F32)],
        name=f"fwd_branch{l}")


GRP = N_HEADS // N_KV


def _attn_mask_bias(first_block):
    shape = (GRP * ATT_BLK, 2 * ATT_BLK)
    qi = lax.broadcasted_iota(jnp.int32, shape, 0) & (ATT_BLK - 1)
    ki = lax.broadcasted_iota(jnp.int32, shape, 1)
    dist = qi + ATT_BLK - ki
    valid = (dist >= 0) & (dist < ATT_BLK) & (jnp.logical_not(first_block) | (ki >= ATT_BLK))
    return dist.astype(F32), valid


def _per_head(hk, values):
    hl = lax.broadcasted_iota(jnp.int32, (GRP * ATT_BLK, 1), 0) // ATT_BLK
    out = values[GRP - 1]
    for j in range(GRP - 2, -1, -1):
        out = jnp.where(hl == j, values[j], out)
    return out


def _attn_probs(q_ref, kvp_ref, kvc_ref, vec_ref, distf, valid):
    kvs = range(N_KV)
    heads = [range(hk * GRP, (hk + 1) * GRP) for hk in kvs]
    q4 = [jnp.concatenate([q_ref[:, h * HD:(h + 1) * HD] for h in heads[hk]], axis=0) for hk in kvs]
    k2 = [jnp.concatenate([kvp_ref[:, hk * HD:(hk + 1) * HD], kvc_ref[:, hk * HD:(hk + 1) * HD]], axis=0) for hk in kvs]
    v2 = [jnp.concatenate([kvp_ref[:, (N_KV + hk) * HD:(N_KV + hk + 1) * HD],
                           kvc_ref[:, (N_KV + hk) * HD:(N_KV + hk + 1) * HD]], axis=0) for hk in kvs]
    slope = [_per_head(hk, [2.0 ** (-8.0 * (h + 1) / N_HEADS) for h in heads[hk]]) for hk in kvs]
    sink = [_per_head(hk, [vec_ref[V_SINK:V_SINK + 1, h:h + 1] for h in heads[hk]]) for hk in kvs]
    sc = [_dot(q4[hk], k2[hk], _NT) for hk in kvs]
    sc = [jnp.where(valid, sc[hk] * (HD ** -0.5) - slope[hk] * distf, NEG_INF) for hk in kvs]
    m = [jnp.maximum(jnp.max(sc[hk], axis=-1, keepdims=True), sink[hk]) for hk in kvs]
    p = [jnp.exp(sc[hk] - m[hk]) for hk in kvs]
    es = [jnp.exp(sink[hk] - m[hk]) for hk in kvs]
    inv = [1.0 / (jnp.sum(p[hk], axis=-1, keepdims=True) + es[hk]) for hk in kvs]
    return [(q4[hk], k2[hk], v2[hk], p[hk] * inv[hk], es[hk] * inv[hk]) for hk in kvs]


def fwd_attn(proj, vecs, l, comm=None):
    s = proj.shape[0]
    nb = s // ATT_BLK

    def body(q_ref, kvp_ref, kvc_ref, vec_ref, o_ref):
        distf, valid = _attn_mask_bias(pl.program_id(0) == 0)
        groups = _attn_probs(q_ref, kvp_ref, kvc_ref, vec_ref, distf, valid)
        outs = [_dot(p, v2, _NN).astype(BF16) for _, _, v2, p, _ in groups]
        for hk, out in enumerate(outs):
            for j in range(GRP):
                h = hk * GRP + j
                o_ref[:, h * HD:(h + 1) * HD] = out[j * ATT_BLK:(j + 1) * ATT_BLK]

    return _call(
        body, comm, (proj, proj, proj, vecs), grid=(nb,),
        in_specs=[pl.BlockSpec((ATT_BLK, BW), lambda i: (i, C_Q // BW)),
                  pl.BlockSpec((ATT_BLK, 256), lambda i: (jnp.maximum(i - 1, 0), C_K // 256)),
                  pl.BlockSpec((ATT_BLK, 256), lambda i: (i, C_K // 256)),
                  pl.BlockSpec((None, V_ROWS, BW), lambda i: (l, 0, 0))],
        out_specs=pl.BlockSpec((ATT_BLK, BW), lambda i: (i, 0)),
        out_shape=_sds((s, BW), BF16), name=f"fwd_attn{l}")


def fwd_merge(x, proj, pre_abd, pre_c, wt_a, wt_b, wt_c, wt_d, w_o, l, comm=None):
    s = x.shape[0]
    tm = min(256, s)

    def body(x_ref, gl_ref, pabd_ref, pc_ref, wa_ref, wb_ref, wc_ref, wd_ref, wo_ref, y_ref, mg_ref, h1_ref):
        pres = (pabd_ref[:, 0:BW], pabd_ref[:, BW:2 * BW], pc_ref[...], pabd_ref[:, 2 * BW:3 * BW])
        merged = jnp.zeros((tm, D), F32)
        for k, (pre, w_ref) in enumerate(zip(pres, (wa_ref, wb_ref, wc_ref, wd_ref))):
            yk = _dot(pre, w_ref[...], _NT)
            y_ref[:, k * D:(k + 1) * D] = yk.astype(BF16)
            merged = merged + _sigmoid(gl_ref[:, k * D:(k + 1) * D].astype(F32)) * yk
        mg_ref[...] = merged.astype(BF16)
        h1_ref[...] = x_ref[...] + _dot(merged, wo_ref[...], _NN)

    wspec = pl.BlockSpec((D, BW), lambda i: (0, 0))
    return _call(
        body, comm, (x, proj, pre_abd, pre_c, wt_a, wt_b, wt_c, wt_d, w_o), grid=(s // tm,),
        in_specs=[pl.BlockSpec((tm, D), lambda i: (i, 0)),
                  pl.BlockSpec((E(tm), E(4 * D)), lambda i: (i * tm, GL0)),
                  pl.BlockSpec((tm, 3 * BW), lambda i: (i, 0)),
                  pl.BlockSpec((tm, BW), lambda i: (i, 0)),
                  wspec, wspec, wspec, wspec,
                  pl.BlockSpec((D, D), lambda i: (0, 0))],
        out_specs=[pl.BlockSpec((tm, 4 * D), lambda i: (i, 0)), pl.BlockSpec((tm, D), lambda i: (i, 0)),
                   pl.BlockSpec((tm, D), lambda i: (i, 0))],
        out_shape=[_sds((s, 4 * D), BF16), _sds((s, D), BF16), _sds((s, D), F32)], name=f"fwd_merge{l}")


def fwd_ffn(h1, g2, wt_gate, wt_up, w_down, l, comm=None):
    s = h1.shape[0]
    tm = min(512, s)
    fc = FF // 2

    def body(h_ref, g_ref, wg_ref, wu_ref, wd_ref, xo_ref, fg_ref, fu_ref, hn_ref, acc_ref):
        j = pl.program_id(1)

        @pl.when(j == 0)
        def _():
            hv = h_ref[...]
            r = lax.rsqrt(jnp.mean(hv * hv, axis=-1, keepdims=True) + EPS)
            hn_ref[...] = (hv * r * g_ref[l:l + 1, :]).astype(BF16)
            acc_ref[...] = hv

        fg = _dot(hn_ref[...], wg_ref[...], _NT)
        fu = _dot(hn_ref[...], wu_ref[...], _NT)
        fg_ref[...] = fg.astype(BF16)
        fu_ref[...] = fu.astype(BF16)
        acc_ref[...] += _dot(fg * _sigmoid(fg) * fu, wd_ref[...], _NN)

        @pl.when(j == pl.num_programs(1) - 1)
        def _():
            xo_ref[...] = acc_ref[...]

    wspec = pl.BlockSpec((fc, D), lambda i, j: (j, 0))
    return _call(
        body, comm, (h1, g2, wt_gate, wt_up, w_down), grid=(s // tm, FF // fc),
        in_specs=[pl.BlockSpec((tm, D), lambda i, j: (i, 0)), pl.BlockSpec((DEPTH, D), lambda i, j: (0, 0)),
                  wspec, wspec, wspec],
        out_specs=[pl.BlockSpec((tm, D), lambda i, j: (i, 0)), pl.BlockSpec((tm, fc), lambda i, j: (i, j)),
                   pl.BlockSpec((tm, fc), lambda i, j: (i, j))],
        out_shape=[_sds((s, D), F32), _sds((s, FF), BF16), _sds((s, FF), BF16)],
        scratch_shapes=[pltpu.VMEM((tm, D), BF16), pltpu.VMEM((tm, D), F32)], name=f"fwd_ffn{l}")


def loss_head(x, gf, target):
    s = x.shape[0]
    tm = min(512, s)

    def body(x_ref, g_ref, t_ref, dx_ref, st_ref):
        @pl.when(pl.program_id(0) == 0)
        def _():
            st_ref[...] = jnp.zeros((8, D), F32)

        xv = x_ref[...]
        g = g_ref[...]
        r = lax.rsqrt(jnp.mean(xv * xv, axis=-1, keepdims=True) + EPS)
        n = xv * r
        err = n * g - t_ref[...]
        dy = err * (1.0 / D)
        dn = dy * g
        dx_ref[...] = r * (dn - n * jnp.mean(dn * n, axis=-1, keepdims=True))
        st_ref[0:1, :] += jnp.sum(dy * n, axis=0, keepdims=True)
        lsum = 0.5 * jnp.sum(jnp.mean(err * err, axis=-1, keepdims=True), axis=0, keepdims=True)
        st_ref[1:2, :] += jnp.broadcast_to(lsum, (1, D))

    return pl.pallas_call(
        body, grid=(s // tm,),
        in_specs=[pl.BlockSpec((tm, D), lambda i: (i, 0)), pl.BlockSpec((1, D), lambda i: (0, 0)),
                  pl.BlockSpec((tm, D), lambda i: (i, 0))],
        out_specs=[pl.BlockSpec((tm, D), lambda i: (i, 0)), pl.BlockSpec((8, D), lambda i: (0, 0))],
        out_shape=[_sds((s, D), F32), _sds((8, D), F32)],
        compiler_params=_cparams(1), name="loss_head")(x, gf, target)


def _edge_index(j, i, n_j, n_i):
    return jnp.where((j == 0) | (j == n_j - 1), i, n_i - 1)


def bwd_ffn(dxo, h1, fg, fu, g2, wt_gate, wt_up, w_down, l, comm=None):
    s = h1.shape[0]
    tm = min(512, s)
    fc = 256
    n_j, n_i = FF // fc, s // tm

    def body(dxo_ref, h_ref, fg_ref, fu_ref, g_ref, wg_ref, wu_ref, wd_ref,
             dh_ref, dwg_ref, dwu_ref, dwd_ref, st_ref, dhn, dxo_b, hn_b, ag, au, ad):
        j, i = pl.program_id(0), pl.program_id(1)
        rows = pl.ds(pl.multiple_of(i * tm, tm), tm)
        g = g_ref[l:l + 1, :]

        @pl.when(j == 0)
        def _():
            hv = h_ref[...]
            r = lax.rsqrt(jnp.mean(hv * hv, axis=-1, keepdims=True) + EPS)
            hn_b[rows, :] = (hv * r * g).astype(BF16)
            dxo_b[rows, :] = dxo_ref[...].astype(BF16)
            dhn[rows, :] = jnp.zeros((tm, D), F32)

        @pl.when((j == 0) & (i == 0))
        def _():
            st_ref[...] = jnp.zeros((8, D), F32)

        @pl.when(i == 0)
        def _():
            ag[...] = jnp.zeros((fc, D), F32)
            au[...] = jnp.zeros((fc, D), F32)
            ad[...] = jnp.zeros((fc, D), F32)

        fgv = fg_ref[...].astype(F32)
        fuv = fu_ref[...].astype(F32)
        sg = _sigmoid(fgv)
        sil = fgv * sg
        dxb = dxo_b[rows, :]
        hnb = hn_b[rows, :]
        d_act = _dot(dxb, wd_ref[...], _NT)
        ad[...] += _dot(sil * fuv, dxb, _TN)
        d_fg = (d_act * fuv * (sg * (1.0 + fgv * (1.0 - sg)))).astype(BF16)
        d_fu = (d_act * sil).astype(BF16)
        ag[...] += _dot(d_fg, hnb, _TN)
        au[...] += _dot(d_fu, hnb, _TN)
        dhn[rows, :] += _dot(d_fg, wg_ref[...], _NN) + _dot(d_fu, wu_ref[...], _NN)

        @pl.when(i == n_i - 1)
        def _():
            dwg_ref[...] = ag[...].astype(BF16)
            dwu_ref[...] = au[...].astype(BF16)
            dwd_ref[...] = ad[...].astype(BF16)

        @pl.when(j == n_j - 1)
        def _():
            hv = h_ref[...]
            r = lax.rsqrt(jnp.mean(hv * hv, axis=-1, keepdims=True) + EPS)
            n = hv * r
            dv = dhn[rows, :]
            dn = dv * g
            dh_ref[...] = dxo_ref[...] + r * (dn - n * jnp.mean(dn * n, axis=-1, keepdims=True))
            st_ref[0:1, :] += jnp.sum(dv * n, axis=0, keepdims=True)

    edge = lambda j, i: (_edge_index(j, i, n_j, n_i), 0)
    wspec = pl.BlockSpec((fc, D), lambda j, i: (j, 0))
    dwspec = pl.BlockSpec((fc, D), lambda j, i: (j, 0))
    return _call(
        body, comm, (dxo, h1, fg, fu, g2, wt_gate, wt_up, w_down), grid=(n_j, n_i),
        in_specs=[pl.BlockSpec((tm, D), edge),
                  pl.BlockSpec((tm, D), edge),
                  pl.BlockSpec((tm, fc), lambda j, i: (i, j)), pl.BlockSpec((tm, fc), lambda j, i: (i, j)),
                  pl.BlockSpec((DEPTH, D), lambda j, i: (0, 0)), wspec, wspec, wspec],
        out_specs=[pl.BlockSpec((tm, D), lambda j, i: (jnp.where(j == n_j - 1, i, 0), 0)),
                   dwspec, dwspec, dwspec, pl.BlockSpec((8, D), lambda j, i: (0, 0))],
        out_shape=[_sds((s, D), F32), _sds((FF, D), BF16), _sds((FF, D), BF16), _sds((FF, D), BF16), _sds((8, D), F32)],
        scratch_shapes=[pltpu.VMEM((s, D), F32), pltpu.VMEM((s, D), BF16), pltpu.VMEM((s, D), BF16),
                        pltpu.VMEM((fc, D), F32), pltpu.VMEM((fc, D), F32), pltpu.VMEM((fc, D), F32)],
        name=f"bwd_ffn{l}")


def bwd_merge(dh1, y4, proj, merged, pre_abd, pre_c, wt_a, wt_b, wt_c, wt_d, w_o, l, comm=None):
    s = dh1.shape[0]
    tm = min(256, s)
    n_i = s // tm

    def body(dh_ref, y_ref, gl_ref, mg_ref, pabd_ref, pc_ref, wa_ref, wb_ref, wc_ref, wd_ref, wo_ref,
             dgl_ref, dpre_ref, dwo_ref, dwa_ref, dwb_ref, dwc_ref, dwd_ref, ao, aa, ab, ac, ad):
        i = pl.program_id(0)
        accs = (aa, ab, ac, ad)

        @pl.when(i == 0)
        def _():
            ao[...] = jnp.zeros((D, D), F32)
            for acc in accs:
                acc[...] = jnp.zeros((D, BW), F32)

        dhb = dh_ref[...].astype(BF16)
        dmg = _dot(dhb, wo_ref[...], _NT)
        ao[...] += _dot(mg_ref[...], dhb, _TN)
        pres = (pabd_ref[:, 0:BW], pabd_ref[:, BW:2 * BW], pc_ref[...], pabd_ref[:, 2 * BW:3 * BW])
        for k, (pre, w_ref, acc) in enumerate(zip(pres, (wa_ref, wb_ref, wc_ref, wd_ref), accs)):
            gk = _sigmoid(gl_ref[:, k * D:(k + 1) * D].astype(F32))
            yk = y_ref[:, k * D:(k + 1) * D].astype(F32)
            dgl_ref[:, k * D:(k + 1) * D] = (dmg * yk * gk * (1.0 - gk)).astype(BF16)
            dyk = (dmg * gk).astype(BF16)
            dpre_ref[:, k * BW:(k + 1) * BW] = _dot(dyk, w_ref[...], _NN).astype(BF16)
            acc[...] += _dot(dyk, pre, _TN)

        @pl.when(i == n_i - 1)
        def _():
            dwo_ref[...] = ao[...].astype(BF16)
            for o_ref, acc in zip((dwa_ref, dwb_ref, dwc_ref, dwd_ref), accs):
                o_ref[...] = acc[...].astype(BF16)

    wspec = pl.BlockSpec((D, BW), lambda i: (0, 0))
    dwspec = pl.BlockSpec((D, BW), lambda i: (0, 0))
    return _call(
        body, comm, (dh1, y4, proj, merged, pre_abd, pre_c, wt_a, wt_b, wt_c, wt_d, w_o), grid=(n_i,),
        in_specs=[pl.BlockSpec((tm, D), lambda i: (i, 0)),
                  pl.BlockSpec((tm, 4 * D), lambda i: (i, 0)),
                  pl.BlockSpec((E(tm), E(4 * D)), lambda i: (i * tm, GL0)),
                  pl.BlockSpec((tm, D), lambda i: (i, 0)),
                  pl.BlockSpec((tm, 3 * BW), lambda i: (i, 0)),
                  pl.BlockSpec((tm, BW), lambda i: (i, 0)),
                  wspec, wspec, wspec, wspec,
                  pl.BlockSpec((D, D), lambda i: (0, 0))],
        out_specs=[pl.BlockSpec((E(tm), E(4 * D)), lambda i: (i * tm, GL0)),
                   pl.BlockSpec((tm, 4 * BW), lambda i: (i, 0)),
                   pl.BlockSpec((D, D), lambda i: (0, 0)), dwspec, dwspec, dwspec, dwspec],
        out_shape=[_sds((s, IN_W), BF16), _sds((s, 4 * BW), BF16), _sds((D, D), BF16)] + [_sds((D, BW), BF16)] * 4,
        scratch_shapes=[pltpu.VMEM((D, D), F32)] + [pltpu.VMEM((D, BW), F32)] * 4, name=f"bwd_merge{l}")


def bwd_attn(proj, dpre, vecs, l, comm=None):
    s = proj.shape[0]
    nb = s // ATT_BLK
    grp = N_HEADS // N_KV

    def body(q_ref, kvp_ref, kvc_ref, do_ref, vec_ref, dq_ref, dkc_ref, dkp_ref, st_ref):
        @pl.when(pl.program_id(0) == 0)
        def _():
            st_ref[...] = jnp.zeros((8, 128), F32)

        distf, valid = _attn_mask_bias(pl.program_id(0) == 0)
        lane = lax.broadcasted_iota(jnp.int32, (1, 128), 1)
        dsink = jnp.zeros((1, 128), F32)
        groups = _attn_probs(q_ref, kvp_ref, kvc_ref, vec_ref, distf, valid)
        kvs = range(N_KV)
        do4s = [jnp.concatenate([do_ref[:, h * HD:(h + 1) * HD] for h in range(hk * grp, (hk + 1) * grp)], axis=0) for hk in kvs]
        dps = [_dot(do4s[hk], groups[hk][2], _NT) for hk in kvs]
        deltas = [jnp.sum(groups[hk][3] * dps[hk], axis=-1, keepdims=True) for hk in kvs]
        dss = [groups[hk][3] * (dps[hk] - deltas[hk]) * (HD ** -0.5) for hk in kvs]
        for hk in kvs:
            q4, k2, v2, p, ps = groups[hk]
            do4, delta, ds = do4s[hk], deltas[hk], dss[hk]
            dq4 = _dot(ds, k2, _NN).astype(BF16)
            dk2 = _dot(ds, q4, _TN)
            dv2 = _dot(p, do4, _TN)
            psd = ps * delta
            for j in range(grp):
                h = hk * grp + j
                rows = slice(j * ATT_BLK, (j + 1) * ATT_BLK)
                dq_ref[:, h * HD:(h + 1) * HD] = dq4[rows]
                dsink = dsink + jnp.where(lane == h, -jnp.sum(psd[rows], axis=0, keepdims=True), 0.0)
            dkp_ref[:, hk * HD:(hk + 1) * HD] = dk2[0:ATT_BLK].astype(BF16)
            dkc_ref[:, hk * HD:(hk + 1) * HD] = dk2[ATT_BLK:].astype(BF16)
            dkp_ref[:, (N_KV + hk) * HD:(N_KV + hk + 1) * HD] = dv2[0:ATT_BLK].astype(BF16)
            dkc_ref[:, (N_KV + hk) * HD:(N_KV + hk + 1) * HD] = dv2[ATT_BLK:].astype(BF16)
        st_ref[0:1, :] += dsink

    return _call(
        body, comm, (proj, proj, proj, dpre, vecs), grid=(nb,),
        in_specs=[pl.BlockSpec((ATT_BLK, BW), lambda i: (i, C_Q // BW)),
                  pl.BlockSpec((ATT_BLK, 256), lambda i: (jnp.maximum(i - 1, 0), C_K // 256)),
                  pl.BlockSpec((ATT_BLK, 256), lambda i: (i, C_K // 256)),
                  pl.BlockSpec((ATT_BLK, BW), lambda i: (i, 2)),
                  pl.BlockSpec((None, V_ROWS, BW), lambda i: (l, 0, 0))],
        out_specs=[pl.BlockSpec((ATT_BLK, BW), lambda i: (i, 0)), pl.BlockSpec((ATT_BLK, 256), lambda i: (i, 0)),
                   pl.BlockSpec((ATT_BLK, 256), lambda i: (i, 0)), pl.BlockSpec((8, 128), lambda i: (0, 0))],
        out_shape=[_sds((s, BW), BF16), _sds((s, 256), BF16), _sds((s, 256), BF16), _sds((8, 128), F32)],
        name=f"bwd_attn{l}")


def bwd_branch(proj, dproj, dpre, h, saved, dq, dkc, dkp, convw, vecs, wx_bd, wa_bd, l, comm=None):
    s = proj.shape[0]
    t = 2 * ATT_BLK
    nt = s // t
    nb = s // ATT_BLK
    hb = t // HALO

    def body(cur_ref, halo_ref, dpre_ref, h_ref, hp_ref, dq_ref, dkc_ref, dkp1_ref, dkp2_ref,
             cw_ref, vec_ref, wx_ref, wa_ref, sv_ref, dproj_in, dp_ref, dcw_ref, dvec_ref, dwx_ref, dwa_ref,
             bufa, bufb, bufd, xd, xg, a_ext, hbuf, b_s, g_s, dh_s, ga, gb, gd, dhcar):
        del dproj_in
        step = pl.program_id(0)
        ti = nt - 1 - step
        first = ti == 0

        @pl.when(step == 0)
        def _():
            dcw_ref[...] = jnp.zeros((CW_ROWS, BW), F32)
            dvec_ref[...] = jnp.zeros((V_ROWS, BW), F32)
            dwx_ref[...] = jnp.zeros((BW, BW), F32)
            dwa_ref[...] = jnp.zeros((BW, BW), F32)
            dhcar[...] = jnp.zeros((1, BW), F32)
            a_ext[t:t + 8, :] = jnp.zeros((8, BW), F32)
            ga[t:t + 8, :] = jnp.zeros((8, BW), F32)
            gb[t:t + 8, :] = jnp.zeros((8, BW), F32)
            gd[t:t + HALO, :] = jnp.zeros((HALO, BW), F32)

        def cur(c0):
            return cur_ref[:, c0:c0 + BW].astype(F32)

        def rsum(v):
            return jnp.sum(v, axis=0, keepdims=True)

        def put(c0, v):
            dp_ref[:, c0:c0 + BW] = v.astype(BF16)

        v = _branch_fwd_math(cur_ref, halo_ref, cw_ref, vec_ref, wx_ref, wa_ref, bufa, bufb, bufd, xd, first, t, sv_ref)
        ca, gi, gr, sp, a, mult = v["ca"], v["gi"], v["gr"], v["sp"], v["a"], v["mult"]
        dpa = dpre_ref[:, 0:BW].astype(F32)
        gg, dgg = _gelu_and_grad(cur(C_AG))
        hv = h_ref[...]
        put(C_AG, dpa * hv * dgg)
        a_ext[0:t, :] = a
        b_s[...] = a_ext[pl.ds(1, t), :]
        g_s[...] = dpa * gg
        dhcar[...] = _scan_bwd(b_s, g_s, dh_s, dhcar[...], t)
        a_ext[t:t + 1, :] = a[0:1, :]
        dh = dh_s[...]
        hbuf[0:8, :] = jnp.where(first, 0.0, hp_ref[...])
        hbuf[8:8 + t, :] = hv
        da = dh * hbuf[pl.ds(7, t), :]
        d_ca = dh * gi * mult
        d_gi = dh * ca * mult
        d_mult = dh * ca * gi
        d_la = da * a - d_mult * (a * a) / mult
        lam = vec_ref[V_LAM:V_LAM + 1, :]
        dvec_ref[V_LAM:V_LAM + 1, :] += rsum(d_la * gr) * (LRU_C * _sigmoid(-lam))
        d_gr = d_la * (-LRU_C * sp)
        d_zr = d_gr * gr * (1.0 - gr)
        d_zi = d_gi * gi * (1.0 - gi)
        dvec_ref[V_BA:V_BA + 1, :] += rsum(d_zr)
        dvec_ref[V_BX:V_BX + 1, :] += rsum(d_zi)
        dwa_ref[...] += _dot(ca, d_zr, _TN)
        dwx_ref[...] += _dot(ca, d_zi, _TN)
        d_ca = d_ca + _dot(d_zi, wx_ref[...], _NT) + _dot(d_zr, wa_ref[...], _NT)
        dvec_ref[V_CAB:V_CAB + 1, :] += rsum(d_ca)
        ga[0:t, :] = d_ca
        d_ax = jnp.zeros((t, BW), F32)
        for k in range(CONV_A):
            d_ax = d_ax + cw_ref[CW_A + k:CW_A + k + 1, :] * ga[pl.ds(CONV_A - 1 - k, t), :]
            dcw_ref[CW_A + k:CW_A + k + 1, :] += rsum(d_ca * bufa[pl.ds(HALO - (CONV_A - 1) + k, t), :])
        ga[t:t + 8, :] = d_ca[0:8, :]
        put(C_AX, d_ax)
        dpb = dpre_ref[:, BW:2 * BW].astype(F32)
        put(C_BB, dpb * v["cb"])
        d_cb = dpb * cur(C_BB)
        gb[0:t, :] = d_cb
        d_cbin = jnp.zeros((t, BW), F32)
        for k in range(CONV_B):
            d_cbin = d_cbin + cw_ref[CW_B + k:CW_B + k + 1, :] * gb[pl.ds(CONV_B - 1 - k, t), :]
            dcw_ref[CW_B + k:CW_B + k + 1, :] += rsum(d_cb * bufb[pl.ds(HALO - (CONV_B - 1) + k, t), :])
        gb[t:t + 8, :] = d_cb[0:8, :]
        put(C_BC, d_cbin * cur(C_BV))
        put(C_BV, d_cbin * cur(C_BC))
        dpd = dpre_ref[:, 3 * BW:4 * BW].astype(F32)
        ln, xh, rstd, s2 = v["ln"], v["xh"], v["rstd"], v["s2"]
        sg = _sigmoid(ln)
        d_ln = dpd * sg * (1.0 + ln * (1.0 - sg))
        dvec_ref[V_LNG:V_LNG + 1, :] += rsum(d_ln * xh)
        dvec_ref[V_LNB:V_LNB + 1, :] += rsum(d_ln)
        d_xh = d_ln * vec_ref[V_LNG:V_LNG + 1, :]
        d_cd = rstd * (d_xh - jnp.mean(d_xh, axis=-1, keepdims=True)
                       - xh * jnp.mean(d_xh * xh, axis=-1, keepdims=True))
        dvec_ref[V_CDB:V_CDB + 1, :] += rsum(d_cd)
        gd[0:t, :] = d_cd
        _shifted_copies(gd, xg, t + HALO)
        d_dg = jnp.zeros((t, BW), F32)
        for k in range(CONV_D):
            d_dg = d_dg + cw_ref[CW_D + k:CW_D + k + 1, :] * _window(gd, xg, CONV_D - 1 - k, t)
            dcw_ref[CW_D + k:CW_D + k + 1, :] += rsum(d_cd * _window(bufd, xd, HALO - (CONV_D - 1) + k, t))
        gd[t:t + HALO, :] = d_cd[0:HALO, :]
        put(C_D1, d_dg * s2)
        put(C_D2, d_dg * cur(C_D1) * s2 * (1.0 - s2))
        dp_ref[:, C_Q:C_Q + BW] = dq_ref[...]
        dkp2 = jnp.where(step == 0, 0.0, dkp2_ref[...].astype(F32))
        dp_ref[0:ATT_BLK, C_K:C_K + 256] = (dkc_ref[0:ATT_BLK, :].astype(F32) + dkp1_ref[...].astype(F32)).astype(BF16)
        dp_ref[ATT_BLK:t, C_K:C_K + 256] = (dkc_ref[ATT_BLK:t, :].astype(F32) + dkp2).astype(BF16)

    rev = lambda i: nt - 1 - i
    full = lambda r, c: pl.BlockSpec((r, c), lambda i: (0, 0))
    return _call(
        body, comm, (proj, proj, dpre, h, h, dq, dkc, dkp, dkp, convw, vecs, wx_bd, wa_bd, saved, dproj), grid=(nt,),
        in_specs=[pl.BlockSpec((t, GL0), lambda i: (rev(i), 0)),
                  pl.BlockSpec((HALO, GL0), lambda i: (jnp.maximum(rev(i) * hb - 1, 0), 0)),
                  pl.BlockSpec((t, 4 * BW), lambda i: (rev(i), 0)),
                  pl.BlockSpec((t, BW), lambda i: (rev(i), 0)),
                  pl.BlockSpec((8, BW), lambda i: (jnp.maximum(rev(i) * (t // 8) - 1, 0), 0)),
                  pl.BlockSpec((t, BW), lambda i: (rev(i), 0)),
                  pl.BlockSpec((t, 256), lambda i: (rev(i), 0)),
                  pl.BlockSpec((ATT_BLK, 256), lambda i: (2 * rev(i) + 1, 0)),
                  pl.BlockSpec((ATT_BLK, 256), lambda i: (jnp.minimum(2 * rev(i) + 2, nb - 1), 0)),
                  pl.BlockSpec((None, CW_ROWS, BW), lambda i: (l, 0, 0)),
                  pl.BlockSpec((None, V_ROWS, BW), lambda i: (l, 0, 0)),
                  pl.BlockSpec((None, BW, BW), lambda i: (l, 0, 0)),
                  pl.BlockSpec((None, BW, BW), lambda i: (l, 0, 0)),
                  pl.BlockSpec((t, 3 * BW), lambda i: (rev(i), 0)),
                  pl.BlockSpec(memory_space=pl.ANY)],
        out_specs=[pl.BlockSpec((t, GL0), lambda i: (rev(i), 0)),
                   full(CW_ROWS, BW), full(V_ROWS, BW), full(BW, BW), full(BW, BW)],
        out_shape=[_sds((s, IN_W), BF16), _sds((CW_ROWS, BW), F32), _sds((V_ROWS, BW), F32),
                   _sds((BW, BW), F32), _sds((BW, BW), F32)],
        scratch_shapes=[pltpu.VMEM((t + HALO, BW), F32)] * 3 + [pltpu.VMEM((7, t + HALO - 8, BW), F32)] * 2
        + [pltpu.VMEM((t + 8, BW), F32), pltpu.VMEM((t + 8, BW), F32)]
        + [pltpu.VMEM((t, BW), F32)] * 3
        + [pltpu.VMEM((t + 8, BW), F32), pltpu.VMEM((t + 8, BW), F32), pltpu.VMEM((t + HALO, BW), F32),
           pltpu.VMEM((1, BW), F32)],
        aliases={14: 0}, name=f"bwd_branch{l}")


def bwd_proj(dproj, x, dh1, g1, wt_in, l, comm=None):
    s = x.shape[0]
    tm = min(512, s)
    ck = 1408
    n_j, n_i = IN_W // ck, s // tm

    def body(dp_ref, x_ref, dh_ref, g_ref, w_ref, dx_ref, dw_ref, st_ref, dxn, xn_b, acc):
        j, i = pl.program_id(0), pl.program_id(1)
        rows = pl.ds(pl.multiple_of(i * tm, tm), tm)
        g = g_ref[l:l + 1, :]

        @pl.when(j == 0)
        def _():
            xv = x_ref[...]
            r = lax.rsqrt(jnp.mean(xv * xv, axis=-1, keepdims=True) + EPS)
            xn_b[rows, :] = (xv * r * g).astype(BF16)
            dxn[rows, :] = jnp.zeros((tm, D), F32)

        @pl.when((j == 0) & (i == 0))
        def _():
            st_ref[...] = jnp.zeros((8, D), F32)

        @pl.when(i == 0)
        def _():
            acc[...] = jnp.zeros((ck, D), F32)

        dp = dp_ref[...]
        dxn[rows, :] += _dot(dp, w_ref[...], _NN)
        acc[...] += _dot(dp, xn_b[rows, :], _TN)

        @pl.when(i == n_i - 1)
        def _():
            dw_ref[...] = acc[...].astype(BF16)

        @pl.when(j == n_j - 1)
        def _():
            xv = x_ref[...]
            r = lax.rsqrt(jnp.mean(xv * xv, axis=-1, keepdims=True) + EPS)
            n = xv * r
            dv = dxn[rows, :]
            dn = dv * g
            dx_ref[...] = dh_ref[...] + r * (dn - n * jnp.mean(dn * n, axis=-1, keepdims=True))
            st_ref[0:1, :] += jnp.sum(dv * n, axis=0, keepdims=True)

    lastrow = lambda j, i: (jnp.where(j == n_j - 1, i, 0), 0)
    return _call(
        body, comm, (dproj, x, dh1, g1, wt_in), grid=(n_j, n_i),
        in_specs=[pl.BlockSpec((tm, ck), lambda j, i: (i, j)),
                  pl.BlockSpec((tm, D), lambda j, i: (_edge_index(j, i, n_j, n_i), 0)),
                  pl.BlockSpec((tm, D), lastrow),
                  pl.BlockSpec((DEPTH, D), lambda j, i: (0, 0)),
                  pl.BlockSpec((ck, D), lambda j, i: (j, 0))],
        out_specs=[pl.BlockSpec((tm, D), lastrow), pl.BlockSpec((ck, D), lambda j, i: (j, 0)),
                   pl.BlockSpec((8, D), lambda j, i: (0, 0))],
        out_shape=[_sds((s, D), F32), _sds((IN_W, D), BF16), _sds((8, D), F32)],
        scratch_shapes=[pltpu.VMEM((s, D), F32), pltpu.VMEM((s, D), BF16), pltpu.VMEM((ck, D), F32)],
        name=f"bwd_proj{l}")


def bwd_proj_w(dproj, x, g1, l, half, comm=None):
    s = x.shape[0]
    tm = min(1024, s)
    ck = 1408
    c0, hw = W_IN_PARTS[half]
    n_j, n_i = IN_W // ck, s // tm

    def body(dp_ref, x_ref, g_ref, dw_ref, xn_b, acc):
        j, i = pl.program_id(0), pl.program_id(1)
        rows = pl.ds(pl.multiple_of(i * tm, tm), tm)

        @pl.when(j == 0)
        def _():
            xv = x_ref[...]
            r = lax.rsqrt(jnp.mean(xv * xv, axis=-1, keepdims=True) + EPS)
            xn_b[rows, :] = (xv * r * g_ref[l:l + 1, :])[:, c0:c0 + hw].astype(BF16)

        @pl.when(i == 0)
        def _():
            acc[...] = jnp.zeros((ck, hw), F32)

        acc[...] += _dot(dp_ref[...], xn_b[rows, :], _TN)

        @pl.when(i == n_i - 1)
        def _():
            dw_ref[...] = acc[...].astype(BF16)

    return _call(
        body, comm, (dproj, x, g1), grid=(n_j, n_i),
        in_specs=[pl.BlockSpec((tm, ck), lambda j, i: (i, j)),
                  pl.BlockSpec((tm, D), lambda j, i: (jnp.where(j == 0, i, n_i - 1), 0)),
                  pl.BlockSpec((DEPTH, D), lambda j, i: (0, 0))],
        out_specs=pl.BlockSpec((ck, hw), lambda j, i: (j, 0)),
        out_shape=_sds((IN_W, hw), BF16),
        scratch_shapes=[pltpu.VMEM((s, hw), BF16), pltpu.VMEM((ck, hw), F32)],
        name=f"bwd_proj_w{half}_{l}")


def bwd_proj_x(dproj, x, dh1, g1, wt_in, l, comm=None):
    s = x.shape[0]
    tm = min(512, s)
    ck = 1408
    n_j, n_i = IN_W // ck, s // tm

    def body(dp_ref, x_ref, dh_ref, g_ref, w_ref, dx_ref, st_ref, dxn):
        j, i = pl.program_id(0), pl.program_id(1)
        rows = pl.ds(pl.multiple_of(i * tm, tm), tm)
        g = g_ref[l:l + 1, :]

        @pl.when((j == 0) & (i == 0))
        def _():
            st_ref[...] = jnp.zeros((8, D), F32)

        part = _dot(dp_ref[...], w_ref[...], _NN)

        @pl.when(j == 0)
        def _():
            dxn[rows, :] = part

        @pl.when(j > 0)
        def _():
            dxn[rows, :] += part

        @pl.when(j == n_j - 1)
        def _():
            xv = x_ref[...]
            r = lax.rsqrt(jnp.mean(xv * xv, axis=-1, keepdims=True) + EPS)
            n = xv * r
            dv = dxn[rows, :]
            dn = dv * g
            dx_ref[...] = dh_ref[...] + r * (dn - n * jnp.mean(dn * n, axis=-1, keepdims=True))
            st_ref[0:1, :] += jnp.sum(dv * n, axis=0, keepdims=True)

    lastrow = lambda j, i: (jnp.where(j == n_j - 1, i, 0), 0)
    return _call(
        body, comm, (dproj, x, dh1, g1, wt_in), grid=(n_j, n_i),
        in_specs=[pl.BlockSpec((tm, ck), lambda j, i: (i, j)), pl.BlockSpec((tm, D), lastrow),
                  pl.BlockSpec((tm, D), lastrow),
                  pl.BlockSpec((DEPTH, D), lambda j, i: (0, 0)), pl.BlockSpec((ck, D), lambda j, i: (j, 0))],
        out_specs=[pl.BlockSpec((tm, D), lastrow), pl.BlockSpec((8, D), lambda j, i: (0, 0))],
        out_shape=[_sds((s, D), F32), _sds((8, D), F32)],
        scratch_shapes=[pltpu.VMEM((s, D), F32)], name=f"bwd_proj_x{l}")


def _block_diag(w):
    nl, nb, bw, _ = w.shape
    eye = jnp.eye(nb, dtype=w.dtype)
    return jnp.einsum("lhij,hk->lhikj", w, eye).reshape(nl, nb * bw, nb * bw).astype(BF16)


class NoOverlap:
    def __init__(self, big):
        self.big = big

    def weights(self, l):
        return self.big[l]

    def job(self, slot, l):
        return None

    def done(self, slot, l, results):
        pass

    def new_grads(self, group, l, grads):
        pass

    def new_small(self, l, arrays, head_stats):
        pass


def local_step(x, target, norm1_g, norm2_g, final_g, convw, vecs, lru_wx, lru_wa, plan):
    wx_bd, wa_bd = _block_diag(lru_wx), _block_diag(lru_wa)

    def run(fn, slot, l, *args):
        res, cres = fn(*args, l, comm=plan.job(slot, l))
        plan.done(slot, l, cres)
        return res

    saved = []
    for l in range(DEPTH):
        proj = run(fwd_proj, "fwd_proj", l, x, norm1_g, plan.weights(l)["in_t"])
        pre_abd, h, kept = run(fwd_branch, "fwd_branch", l, proj, convw, vecs, wx_bd, wa_bd)
        pre_c = run(fwd_attn, "fwd_attn", l, proj, vecs)
        w = plan.weights(l)
        y4, merged, h1 = run(fwd_merge, "fwd_merge", l, x, proj, pre_abd, pre_c, w["a_t"], w["b_t"], w["c_t"], w["d_t"], w["o"])
        w = plan.weights(l)
        x_out, fg, fu = run(fwd_ffn, "fwd_ffn", l, h1, norm2_g, w["gate_t"], w["up_t"], w["down"])
        saved.append((x, proj, pre_abd, h, kept, pre_c, y4, merged, h1, fg, fu))
        x = x_out
    dx, head_stats = loss_head(x, final_g.reshape(1, D), target)
    small = [None] * DEPTH
    for l in reversed(range(DEPTH)):
        x_in, proj, pre_abd, h, kept, pre_c, y4, merged, h1, fg, fu = saved[l]
        w = plan.weights(l)
        dh1, d_gate, d_up, d_down, st_ffn = run(bwd_ffn, "bwd_ffn", l, dx, h1, fg, fu, norm2_g, w["gate_t"], w["up_t"], w["down"])
        plan.new_grads("ffn", l, dict(gate_t=d_gate, up_t=d_up, down=d_down))
        dproj, dpre, d_o, d_a, d_b, d_c, d_d = run(
            bwd_merge, "bwd_merge", l, dh1, y4, proj, merged, pre_abd, pre_c, w["a_t"], w["b_t"], w["c_t"], w["d_t"], w["o"])
        plan.new_grads("out", l, dict(a_t=d_a, b_t=d_b, c_t=d_c, d_t=d_d, o=d_o))
        dq, dkc, dkp, st_attn = run(bwd_attn, "bwd_attn", l, proj, dpre, vecs)
        dproj, dcw, dvec, dwx, dwa = run(bwd_branch, "bwd_branch", l, proj, dproj, dpre, h, kept, dq, dkc, dkp, convw, vecs, wx_bd, wa_bd)
        if l > 0:
            dx, d_in, st_proj = run(bwd_proj, "bwd_proj", l, dproj, x_in, dh1, norm1_g, w["in_t"])
            plan.new_grads("in", l, dict(in_t=d_in))
        else:
            for half, name in enumerate(("in_a", "in_b")):
                d_half = run(functools.partial(bwd_proj_w, half=half), f"bwd_proj_w{half}", l, dproj, x_in, norm1_g)
                plan.new_grads(name, l, {name: d_half})
            dx, st_proj = run(bwd_proj_x, "bwd_proj_x", l, dproj, x_in, dh1, norm1_g, w["in_t"])
        small[l] = (st_proj, st_ffn, dvec, st_attn, dcw, dwx, dwa)
        plan.new_small(l, small[l], head_stats)
    return head_stats, dx, small


BIG = dict(in_t=("w_in", "view"), a_t=("w_a_out", "transpose"), b_t=("w_b_out", "transpose"), c_t=("w_c_out", "transpose"),
           d_t=("w_d_out", "transpose"), o=("w_o", "plain"), gate_t=("w_ffn_gate", "view"), up_t=("w_ffn_up", "view"),
           down=("w_ffn_down", "plain"))


def cast_transpose(w, name):
    nl, a, b = w.shape
    ta = min(256, a)

    def body(w_ref, o_ref):
        o_ref[...] = w_ref[...].T.astype(BF16)

    return pl.pallas_call(
        body, grid=(nl, a // ta),
        in_specs=[pl.BlockSpec((None, ta, b), lambda l, i: (l, i, 0))],
        out_specs=pl.BlockSpec((None, b, ta), lambda l, i: (l, 0, i)),
        out_shape=_sds((nl, b, a), BF16), compiler_params=_cparams(2), name=name)(w)


def add_partials(mine, recv, core, name):
    n = len(mine)

    def body(core_ref, *refs):
        del core_ref
        for a_ref, b_ref, o_ref in zip(refs[:n], refs[n:2 * n], refs[2 * n:]):
            o_ref[...] = (a_ref[...].astype(F32) + b_ref[...].astype(F32)).astype(BF16)

    return pl.pallas_call(
        body,
        grid_spec=pltpu.PrefetchScalarGridSpec(
            num_scalar_prefetch=1, grid=(4,),
            in_specs=[pl.BlockSpec((None, None) + a.shape[2:], lambda i, cr: (i, cr[0], 0, 0)) for a in mine]
            + [pl.BlockSpec((None,) + b.shape[1:], lambda i, cr: (i, 0, 0)) for b in recv],
            out_specs=[pl.BlockSpec((None,) + b.shape[1:], lambda i, cr: (i, 0, 0)) for b in recv]),
        out_shape=[_sds(b.shape, BF16) for b in recv], compiler_params=_cparams(1), name=name)(core, *mine, *recv)


def _adamw(w, g, m, v):
    m = ADAM_B1 * m + (1.0 - ADAM_B1) * g
    v = ADAM_B2 * v + (1.0 - ADAM_B2) * (g * g)
    m_hat = m / (1.0 - ADAM_B1 ** ADAM_STEP)
    v_hat = v / (1.0 - ADAM_B2 ** ADAM_STEP)
    delta = -ADAM_LR * (m_hat / (jnp.sqrt(v_hat) + ADAM_EPS) + ADAM_WD * w)
    return delta, m, v


def adamw_big(contrib, w, m, v, transposed, name, comm=None):
    nsrc, nl, rows, cols = contrib.shape
    ct = 256

    def body(c_ref, w_ref, m_ref, v_ref, g_out, d_out, m_out, v_out):
        g = c_ref[0].astype(F32)
        for src in range(1, nsrc):
            g = g + c_ref[src].astype(F32)
        if transposed:
            g = g.T
        delta, mn, vn = _adamw(w_ref[...], g, m_ref[...], v_ref[...])
        g_out[...] = g
        d_out[...] = delta
        m_out[...] = mn
        v_out[...] = vn

    if transposed:
        wspec = pl.BlockSpec((None, ct, rows), lambda l, j: (l, j, 0))
    else:
        wspec = pl.BlockSpec((None, rows, ct), lambda l, j: (l, 0, j))
    return _call(
        body, comm, (contrib, w, m, v), grid=(nl, cols // ct),
        in_specs=[pl.BlockSpec((nsrc, None, rows, ct), lambda l, j: (0, l, 0, j)), wspec, wspec, wspec],
        out_specs=[wspec] * 4, out_shape=[_sds(w.shape, F32)] * 4, name=name)


VEC_NAMES = ("conv_a_b", "lru_bx", "lru_ba", "lru_lambda", "conv_d_b", "ln_d_g", "ln_d_b")
P_N1, P_N2, P_VEC, P_CONV, P_LRU = 0, 1, 2, 6, 6 + CW_ROWS
P_FINAL, P_LOSS, P_ROWS = P_LRU + HD, P_LRU + HD + 1, P_LRU + HD + 2
SMALL = ("norm1_g", "conv_a_w", "conv_a_b", "lru_wx", "lru_bx", "lru_wa", "lru_ba", "lru_lambda", "conv_b_w", "sinks",
         "conv_d_w", "conv_d_b", "ln_d_g", "ln_d_b", "norm2_g", "final_g")
VMEM_FULL = pl.BlockSpec(memory_space=pltpu.VMEM)


def _stack_vecs(p):
    rows = [p[n] for n in VEC_NAMES] + [jnp.pad(p["sinks"], ((0, 0), (0, BW - N_HEADS)))]
    return jnp.stack(rows, axis=1)


def _stack_convs(p):
    nl, _, ch = p["conv_a_w"].shape
    z = jnp.zeros((nl, 1, ch), F32)
    return jnp.concatenate([p["conv_a_w"], p["conv_b_w"], z, p["conv_d_w"], z], axis=1)


def _vec_place(r):
    return P_VEC + r // 2, (r % 2) * BW


def pack_small(arrays, head_stats, l):
    n = len(arrays)

    def body(*refs):
        st_proj, st_ffn, dvec, st_attn, dcw, dwx, dwa = refs[:n]
        pack = refs[-1]
        pack[...] = jnp.zeros((P_ROWS, D), F32)
        lane = lax.broadcasted_iota(jnp.int32, (HD, BW), 1)
        pack[P_N1:P_N1 + 1, :] = st_proj[0:1, :]
        pack[P_N2:P_N2 + 1, :] = st_ffn[0:1, :]
        for r in range(len(VEC_NAMES)):
            row, c0 = _vec_place(r)
            pack[row:row + 1, c0:c0 + BW] = dvec[r:r + 1, :]
        row, c0 = _vec_place(V_SINK)
        pack[row:row + 1, c0:c0 + 128] = st_attn[0:1, :]
        pack[P_CONV:P_CONV + CW_ROWS, 0:BW] = dcw[...]
        for mat, c0 in ((dwx, 0), (dwa, BW)):
            blocks = jnp.zeros((HD, BW), F32)
            for h in range(BW // HD):
                blocks = jnp.where((lane >= HD * h) & (lane < HD * (h + 1)), mat[HD * h:HD * (h + 1), :], blocks)
            pack[P_LRU:P_LRU + HD, c0:c0 + BW] = blocks
        if head_stats is not None:
            pack[P_FINAL:P_LOSS + 1, :] = refs[n][0:2, :]

    flat = list(arrays) + ([] if head_stats is None else [head_stats])
    return pl.pallas_call(body, out_shape=_sds((P_ROWS, D), F32), in_specs=[VMEM_FULL] * len(flat), out_specs=VMEM_FULL,
                          name=f"pack_small{l}", compiler_params=pltpu.CompilerParams(vmem_limit_bytes=VMEM_LIMIT))(*flat)


def adamw_small(gathered, me, w, m, v):
    ns = len(SMALL)

    def body(me_ref, *refs):
        c_refs, refs = refs[:DEPTH], refs[DEPTH:]
        w_refs, m_refs, v_refs = refs[:ns], refs[ns:2 * ns], refs[2 * ns:3 * ns]
        loss_ref, outs, gs = refs[3 * ns], refs[3 * ns + 1:3 * ns + 1 + 4 * ns], refs[-1]
        for l in range(DEPTH):
            gs[l] = c_refs[l][0]
            for dev in range(1, NDEV):
                gs[l] += c_refs[l][dev]
        loss_ref[...] = gs[DEPTH - 1, P_LOSS:P_LOSS + 1, 0:128]

        def update(name, sel, g):
            i = SMALL.index(name)
            delta, mn, vn = _adamw(w_refs[i][sel], g, m_refs[i][sel], v_refs[i][sel])
            for o_ref, val in zip(outs[4 * i:4 * i + 4], (g, delta, mn, vn)):
                o_ref[sel] = val

        update("final_g", (slice(0, 1), slice(None)), gs[DEPTH - 1, P_FINAL:P_FINAL + 1, :])
        shift = (BW - me_ref[0] * (BW // NDEV)) & (BW - 1)
        for l in range(DEPTH):
            row = (slice(l, l + 1), slice(None))
            update("norm1_g", row, gs[l, P_N1:P_N1 + 1, :])
            update("norm2_g", row, gs[l, P_N2:P_N2 + 1, :])
            for r, name in enumerate(VEC_NAMES):
                prow, c0 = _vec_place(r)
                update(name, row, gs[l, prow:prow + 1, c0:c0 + BW])
            prow, c0 = _vec_place(V_SINK)
            update("sinks", row, gs[l, prow:prow + 1, c0:c0 + N_HEADS])
            mine = pltpu.roll(gs[l, P_CONV:P_CONV + CW_ROWS, 0:BW], shift, 1)[:, 0:BW // NDEV]
            update("conv_a_w", (l,), mine[CW_A:CW_A + CONV_A])
            update("conv_b_w", (l,), mine[CW_B:CW_B + CONV_B])
            update("conv_d_w", (l,), mine[CW_D:CW_D + CONV_D])
            for h in range(BW // HD):
                update("lru_wx", (l, h), gs[l, P_LRU:P_LRU + HD, HD * h:HD * (h + 1)])
                update("lru_wa", (l, h), gs[l, P_LRU:P_LRU + HD, BW + HD * h:BW + HD * (h + 1)])

    args = [p[n] for p in (w, m, v) for n in SMALL]
    full = lambda a: pl.BlockSpec(a.shape, lambda i, me_ref: (0,) * a.ndim)
    out_shape = [_sds((1, 128), F32)] + [_sds(w[n].shape, F32) for n in SMALL for _ in range(4)]
    outs = pl.pallas_call(
        body,
        grid_spec=pltpu.PrefetchScalarGridSpec(
            num_scalar_prefetch=1, grid=(1,),
            in_specs=[full(a) for a in list(gathered) + args], out_specs=[full(o) for o in out_shape],
            scratch_shapes=[pltpu.VMEM((DEPTH, P_ROWS, D), F32)]),
        out_shape=out_shape, name="adamw_small", compiler_params=_cparams(1))(me, *gathered, *args)
    return outs[0], {n: outs[1 + 4 * i:5 + 4 * i] for i, n in enumerate(SMALL)}


def merge_jobs(jobs):
    jobs = [j for j in jobs if j is not None]
    if not jobs:
        return None, []
    inputs, aliases, outs, sems, cuts = [], {}, [], [], []
    for j in jobs:
        i0, o0, s0 = len(inputs), len(outs), len(sems)
        aliases.update({i0 + i: o0 + o for i, o in j.aliases.items()})
        inputs += j.inputs
        outs += j.out_shapes
        sems += j.sem_shapes
        cuts.append((i0, len(inputs), o0, len(outs), s0, len(sems)))

    def each(which):
        def go(cins, couts, s):
            for j, (i0, i1, o0, o1, s0, s1) in zip(jobs, cuts):
                if getattr(j, which) is not None:
                    getattr(j, which)(cins[i0:i1], couts[o0:o1], s[s0:s1])
        return go

    relay = each("relay") if any(j.relay is not None for j in jobs) else None
    return CommJob(inputs, aliases, outs, sems, each("start"), each("finish"), relay), [(c[2], c[3]) for c in cuts]


SIXTHS = 6
OUT_KINDS = ("a_t", "b_t", "c_t", "d_t", "o")
GATHER_PLAN = {
    "fwd_proj": [(k, 0, 0, 6) for k in OUT_KINDS] + [("gate_t", 0, 0, 6)],
    "fwd_branch": [("up_t", 0, 0, 6)],
    "fwd_attn": [("down", 0, 0, 6)],
    "fwd_merge": [("in_t", 1, 0, 2)],
    "fwd_ffn": [("in_t", 1, 2, 6)],
}
SIBLING_PLAN = {"bwd_merge": ("ffn", 0), "bwd_branch": ("out", 0), "bwd_ffn": ("in", 1),
                "bwd_proj_w1": ("in_a", 0), "bwd_proj_x": ("in_b", 0)}
GROUPS = dict(ffn=("gate_t", "up_t", "down"), out=OUT_KINDS, in_a=("in_a",), in_b=("in_b",))
GROUPS["in"] = ("in_t",)
COLUMN_HALF = dict(in_a=("in_t", W_IN_PARTS[0][0]), in_b=("in_t", W_IN_PARTS[1][0]))
CHIP_PLAN = {
    "bwd_attn": [("in_t", 1, 3, 6)],
    "bwd_branch": [("gate_t", 0, 0, 6), ("up_t", 0, 0, 6), ("down", 0, 0, 3)],
    "bwd_proj": [(k, 0, 0, 6) for k in OUT_KINDS] + [("down", 0, 3, 6)],
    "bwd_proj_w0": [(k, 0, 0, 6) for k in OUT_KINDS[:3]] + [("down", 0, 3, 6)],
    "bwd_proj_w1": [(k, 0, 0, 6) for k in OUT_KINDS[3:]],
    "bwd_merge": [("in_t", 1, 0, 3)],
    "bwd_proj_x": [("in_a", 0, 0, 6)],
    "adamw_gate_t": [("in_b", 0, 0, 6)],
}
SMALL_GATHER_PLAN = {"bwd_ffn": 1, "adamw_down": 0}


class Overlap:
    def __init__(self, shards, core):
        self.shards = shards
        self.core = core
        self.gathered = [dict.fromkeys(BIG) for _ in range(DEPTH)]
        self.views = {}
        self.partial = {}
        self.contrib = dict.fromkeys(BIG)
        self.small_packs = [None] * DEPTH
        self.small_gathered = [None] * DEPTH
        self._open = None

    def weights(self, l):
        return self.gathered[l]

    def new_grads(self, group, l, grads):
        for k, g in grads.items():
            self.views[k, l] = g.reshape(4, 2, g.shape[0] // NDEV, g.shape[1])

    def new_small(self, l, arrays, head_stats):
        self.small_packs[l] = pack_small(arrays, head_stats if l == DEPTH - 1 else None, l)

    @staticmethod
    def _rows(shard_rows, f0, f1):
        return shard_rows * f0 // SIXTHS, shard_rows * (f1 - f0) // SIXTHS

    def job(self, slot, l):
        jobs, notes = [], []
        pieces = [(k, l + dl, f0, f1) for k, dl, f0, f1 in GATHER_PLAN.get(slot, []) if l + dl < DEPTH]
        if pieces:
            jobs.append(gather_job([((k, ll), self.shards[ll][k], self.gathered[ll][k],
                                     *self._rows(self.shards[ll][k].shape[0], f0, f1)) for k, ll, f0, f1 in pieces]))
            notes.append(("gather", list(dict.fromkeys((k, ll) for k, ll, _, _ in pieces))))
        if slot in SIBLING_PLAN and l + SIBLING_PLAN[slot][1] < DEPTH:
            group, dl = SIBLING_PLAN[slot]
            keys = [(k, l + dl) for k in GROUPS[group]]
            jobs.append(sibling_exchange_job([self.views[key] for key in keys]))
            notes.append(("sibling", keys))
        pieces = [(k, l + dl, f0, f1) for k, dl, f0, f1 in CHIP_PLAN.get(slot, []) if l + dl < DEPTH]
        if pieces:
            whole = [(*COLUMN_HALF.get(k, (k, 0)), k, ll, f0, f1) for k, ll, f0, f1 in pieces]
            jobs.append(chip_exchange_job([(self.partial[k, ll], self.contrib[kind], kind, ll,
                                            *self._rows(self.partial[k, ll].shape[1], f0, f1), col0, self.shards[ll][kind].shape[1])
                                           for kind, col0, k, ll, f0, f1 in whole]))
            notes.append(("chips", list(dict.fromkeys(kind for kind, *_ in whole))))
        if slot in SMALL_GATHER_PLAN and l + SMALL_GATHER_PLAN[slot] < DEPTH:
            ll = l + SMALL_GATHER_PLAN[slot]
            jobs.append(gather_job([("small", self.small_packs[ll], None, 0, P_ROWS)]))
            notes.append(("small", ll))
        job, spans = merge_jobs(jobs)
        self._open = (slot, l, notes, spans)
        return job

    def done(self, slot, l, results):
        open_slot, open_l, notes, spans = self._open
        assert (open_slot, open_l) == (slot, l)
        for (what, keys), (r0, r1) in zip(notes, spans):
            res = results[r0:r1]
            if what == "gather":
                for (k, ll), g in zip(keys, res):
                    self.gathered[ll][k] = g
            elif what == "sibling":
                sums = add_partials([self.views[key] for key in keys], list(res), self.core, f"chip_sum_{keys[0][0]}{keys[0][1]}")
                self.partial.update(zip(keys, sums))
            elif what == "chips":
                for k, c in zip(keys, res):
                    self.contrib[k] = c
            else:
                self.small_gathered[keys], = res


SMALL = ("norm1_g", "conv_a_w", "conv_a_b", "lru_wx", "lru_bx", "lru_wa", "lru_ba", "lru_lambda", "conv_b_w", "sinks",
         "conv_d_w", "conv_d_b", "ln_d_g", "ln_d_b", "norm2_g", "final_g")
WEIGHTS = ("norm1_g", "w_in", "conv_a_w", "conv_a_b", "lru_wx", "lru_bx", "lru_wa", "lru_ba", "lru_lambda", "w_a_out",
           "conv_b_w", "w_b_out", "sinks", "w_c_out", "conv_d_w", "conv_d_b", "ln_d_g", "ln_d_b", "w_d_out", "w_o",
           "norm2_g", "w_ffn_gate", "w_ffn_up", "w_ffn_down", "final_g")


def kernel(x, norm1_g, w_in, conv_a_w, conv_a_b, lru_wx, lru_bx, lru_wa, lru_ba, lru_lambda, w_a_out, conv_b_w, w_b_out, sinks, w_c_out, conv_d_w, conv_d_b, ln_d_g, ln_d_b, w_d_out, w_o, norm2_g, w_ffn_gate, w_ffn_up, w_ffn_down, final_g, loss_target, m_norm1_g, m_w_in, m_conv_a_w, m_conv_a_b, m_lru_wx, m_lru_bx, m_lru_wa, m_lru_ba, m_lru_lambda, m_w_a_out, m_conv_b_w, m_w_b_out, m_sinks, m_w_c_out, m_conv_d_w, m_conv_d_b, m_ln_d_g, m_ln_d_b, m_w_d_out, m_w_o, m_norm2_g, m_w_ffn_gate, m_w_ffn_up, m_w_ffn_down, m_final_g, v_norm1_g, v_w_in, v_conv_a_w, v_conv_a_b, v_lru_wx, v_lru_bx, v_lru_wa, v_lru_ba, v_lru_lambda, v_w_a_out, v_conv_b_w, v_w_b_out, v_sinks, v_w_c_out, v_conv_d_w, v_conv_d_b, v_ln_d_g, v_ln_d_b, v_w_d_out, v_w_o, v_norm2_g, v_w_ffn_gate, v_w_ffn_up, v_w_ffn_down, v_final_g):
    args = dict(locals())
    w = {n: args[n] for n in WEIGHTS}
    m = {n: args["m_" + n] for n in WEIGHTS}
    v = {n: args["v_" + n] for n in WEIGHTS}
    me = _dev_index(*_mesh_pos())

    def rows_major(a, how):
        return jnp.swapaxes(a, 1, 2) if how == "view" else a

    stacked = {k: cast_transpose(w[n], "prep_" + k) if how == "transpose" else rows_major(w[n], how).astype(BF16)
               for k, (n, how) in BIG.items()}
    plan = Overlap([{k: stacked[k][l] for k in BIG} for l in range(DEPTH)], lax.axis_index("c").astype(jnp.int32).reshape(1))
    convs = jnp.pad(_stack_convs(w).reshape(DEPTH * CW_ROWS, BW // NDEV), ((0, 0), (0, 256 - BW // NDEV)))
    g_in0, g_conv = _comm_only(gather_job([(("in_t", 0), plan.shards[0]["in_t"], None, 0, plan.shards[0]["in_t"].shape[0]),
                                           ("convs", convs, None, 0, convs.shape[0])]), "gather_first")
    plan.gathered[0]["in_t"] = g_in0
    convw = g_conv[:, :BW // NDEV].reshape(NDEV, DEPTH, CW_ROWS, BW // NDEV).transpose(1, 2, 0, 3).reshape(DEPTH, CW_ROWS, BW)

    vecs = _stack_vecs(w)
    head_stats, grad_x, grads = local_step(x[0], loss_target[0], norm1_g, norm2_g, final_g, convw, vecs, lru_wx, lru_wa, plan)


    out = {}
    for k in ("down", "gate_t", "up_t", "o", "a_t", "b_t", "c_t", "d_t", "in_t"):
        n, how = BIG[k]
        res, cres = adamw_big(plan.contrib[k], rows_major(w[n], how), rows_major(m[n], how), rows_major(v[n], how),
                              how == "transpose", "adamw_" + k, comm=plan.job("adamw_" + k, 0))
        plan.done("adamw_" + k, 0, cres)
        out[n] = [rows_major(r, how) for r in res]

    def own_shapes(p):
        return {n: p[n].reshape(1, D) if n == "final_g" else p[n] for n in SMALL}

    loss, small = adamw_small([g.reshape(NDEV, P_ROWS, D) for g in plan.small_gathered], me.astype(jnp.int32).reshape(1),
                              own_shapes(w), own_shapes(m), own_shapes(v))
    for n in SMALL:
        out[n] = [r.reshape(w[n].shape) for r in small[n]]
    loss = loss[0, 0]
    return (loss, grad_x[None], *[out[n][0] for n in WEIGHTS], *[out[n][1] for n in WEIGHTS],
            *[out[n][2] for n in WEIGHTS], *[out[n][3] for n in WEIGHTS])
```

```python
import functools

import jax
import jax.numpy as jnp
from jax import lax
from jax.experimental import pallas as pl
from jax.experimental.pallas import tpu as pltpu

F32 = jnp.float32
BF16 = jnp.bfloat16
E = pl.Element

D = 1024
BW = 512
IN_W = 8448
GL0 = 4352
FF = 2816
N_HEADS = 8
N_KV = 2
HD = 64
ATT_BLK = 128
EPS = 1e-6
LRU_C = 8.0
NEG_INF = -1e30
DEPTH = 2
NDEV = 8
CONV_A, CONV_B, CONV_D = 4, 3, 31
C_AX, C_AG, C_BV, C_BC, C_BB, C_Q, C_K, C_V, C_D1, C_D2 = 0, 512, 1024, 1536, 2048, 2560, 3072, 3200, 3328, 3840
CW_A, CW_B, CW_D, CW_ROWS = 0, 4, 8, 40
V_CAB, V_BX, V_BA, V_LAM, V_CDB, V_LNG, V_LNB, V_SINK, V_ROWS = 0, 1, 2, 3, 4, 5, 6, 7, 8
HALO = 32
W_IN_PARTS = ((0, 768), (768, 256))

ADAM_LR, ADAM_B1, ADAM_B2, ADAM_EPS, ADAM_WD, ADAM_STEP = 0.001, 0.9, 0.999, 1e-08, 0.01, 10

VMEM_LIMIT = 56 * 1024 * 1024

_NN = (((1,), (0,)), ((), ()))
_NT = (((1,), (1,)), ((), ()))
_TN = (((0,), (0,)), ((), ()))


def _dot(a, b, dims):
    return lax.dot_general(a.astype(BF16), b.astype(BF16), dims, preferred_element_type=F32)


def _cparams(n_axes):
    return pltpu.CompilerParams(dimension_semantics=("arbitrary",) * n_axes, vmem_limit_bytes=VMEM_LIMIT)


def _sds(shape, dtype):
    return jax.ShapeDtypeStruct(tuple(shape), dtype)


def _sigmoid(x):
    return jax.nn.sigmoid(x)


def _neg_expm1(x):
    p = x * (1.0 + x * (0.5 + x * (1.0 / 6.0 + x * (1.0 / 24.0 + x * (1.0 / 120.0)))))
    return jnp.where(x > -0.1, -p, 1.0 - jnp.exp(x))


def _softplus(z):
    return jnp.maximum(z, 0.0) + jnp.log1p(jnp.exp(-jnp.abs(z)))


def _gelu_and_grad(x):
    c = 0.7978845608028654
    inner = c * (x + 0.044715 * x * x * x)
    t = jnp.tanh(inner)
    g = 0.5 * x * (1.0 + t)
    dg = 0.5 * (1.0 + t) + 0.5 * x * (1.0 - t * t) * c * (1.0 + 3.0 * 0.044715 * x * x)
    return g, dg


ANY = pl.BlockSpec(memory_space=pl.ANY)
MESH = pl.DeviceIdType.MESH


def _mesh_pos():
    return lax.axis_index("x"), lax.axis_index("y"), lax.axis_index("c")


def _dev_index(px, py, pc):
    return 4 * px + 2 * py + pc


class CommJob:
    def __init__(self, inputs, aliases, out_shapes, sem_shapes, start, finish, relay=None):
        self.inputs, self.aliases, self.out_shapes, self.sem_shapes = list(inputs), dict(aliases), list(out_shapes), list(sem_shapes)
        self.start, self.finish, self.relay = start, finish, relay


def _call(body, comm, args, *, grid, in_specs, out_specs, out_shape, scratch_shapes=(), name, aliases=None):
    single = not isinstance(out_shape, (list, tuple))
    out_specs = [out_specs] if single else list(out_specs)
    out_shape = [out_shape] if single else list(out_shape)
    scratch_shapes = list(scratch_shapes)
    n_in, n_out, n_scr, n_axes = len(in_specs), len(out_shape), len(scratch_shapes), len(grid)
    params = pltpu.CompilerParams(dimension_semantics=("arbitrary",) * n_axes, vmem_limit_bytes=VMEM_LIMIT)
    io_aliases = dict(aliases or {})
    if comm is None:
        outs = pl.pallas_call(body, grid=grid, in_specs=in_specs, out_specs=out_specs, out_shape=out_shape,
                              scratch_shapes=scratch_shapes, input_output_aliases=io_aliases, compiler_params=params,
                              name=name)(*args)
        return (outs[0] if single else outs), []
    c_in, c_out = len(comm.inputs), len(comm.out_shapes)
    io_aliases.update({n_in + i: n_out + o for i, o in comm.aliases.items()})

    def wrapped(*refs):
        ins, cins = refs[:n_in], refs[n_in:n_in + c_in]
        outs = refs[n_in + c_in:n_in + c_in + n_out]
        couts = refs[n_in + c_in + n_out:n_in + c_in + n_out + c_out]
        rest = refs[n_in + c_in + n_out + c_out:]
        scr, sems = rest[:n_scr], rest[n_scr:]
        first = functools.reduce(lambda a, b: a & b, [pl.program_id(a) == 0 for a in range(n_axes)])
        last = functools.reduce(lambda a, b: a & b, [pl.program_id(a) == pl.num_programs(a) - 1 for a in range(n_axes)])

        @pl.when(first)
        def _():
            comm.start(cins, couts, sems)

        if comm.relay is not None:
            step = functools.reduce(lambda a, b: a * grid[b] + pl.program_id(b), range(1, n_axes), pl.program_id(0))
            n_steps = functools.reduce(lambda a, b: a * b, grid)

            @pl.when(step == 2 * n_steps // 3)
            def _():
                comm.relay(cins, couts, sems)

        body(*ins, *outs, *scr)

        @pl.when(last)
        def _():
            comm.finish(cins, couts, sems)

    outs = pl.pallas_call(
        wrapped, grid=grid, in_specs=list(in_specs) + [ANY] * c_in, out_specs=out_specs + [ANY] * c_out,
        out_shape=out_shape + comm.out_shapes, scratch_shapes=scratch_shapes + comm.sem_shapes,
        input_output_aliases=io_aliases, compiler_params=params, name=name)(*args, *comm.inputs)
    res, cres = outs[:n_out], outs[n_out:]
    return (res[0] if single else res), cres


def _comm_only(comm, name):
    c_in, c_out = len(comm.inputs), len(comm.out_shapes)

    def body(*refs):
        cins, couts, sems = refs[:c_in], refs[c_in:c_in + c_out], refs[c_in + c_out:]
        comm.start(cins, couts, sems)
        if comm.relay is not None:
            comm.relay(cins, couts, sems)
        comm.finish(cins, couts, sems)

    return pl.pallas_call(body, in_specs=[ANY] * c_in, out_specs=[ANY] * c_out, out_shape=comm.out_shapes,
                          scratch_shapes=comm.sem_shapes, input_output_aliases=comm.aliases, name=name)(*comm.inputs)


def gather_job(pieces):
    inputs, aliases, out_shapes, plan, where = [], {}, [], [], {}
    for key, shard, gathered, row0, nrows in pieces:
        if key not in where:
            where[key] = (len(inputs), len(out_shapes))
            inputs.append(shard)
            if gathered is not None:
                aliases[len(inputs)] = len(out_shapes)
                inputs.append(gathered)
            out_shapes.append(_sds((NDEV * shard.shape[0], shard.shape[1]), shard.dtype))
        plan.append((*where[key], shard.shape[0], row0, nrows))
    n = len(plan)

    def copies(cins, couts, sems):
        send_sems, recv_sems, local_sems = sems
        x, y, c = _mesh_pos()
        me, sibling = (x, y, c), (x, y, 1 - c)
        xn, yn, dg = (1 - x, y), (x, 1 - y), (1 - x, 1 - y)
        local, first, pass1, pass2, got_ici, got_fwd, got_d2d = [], [], [], [], [], [], []
        for p, (i_shard, i_out, rows, row0, nrows) in enumerate(plan):
            src = cins[i_shard].at[pl.ds(row0, nrows), :]
            half = cins[i_shard].shape[1] // 2
            left, right, whole = pl.ds(0, half), pl.ds(half, half), slice(None)

            def slot(dev, lanes, i_out=i_out, rows=rows, row0=row0, nrows=nrows):
                return couts[i_out].at[pl.ds(_dev_index(*dev) * rows + row0, nrows), lanes]

            def copy(g, dev, to, lanes=whole, src=None, p=p, slot=slot):
                return pltpu.make_async_remote_copy(
                    src_ref=slot(dev, lanes) if src is None else src, dst_ref=slot(dev, lanes),
                    send_sem=send_sems.at[g, p], recv_sem=recv_sems.at[g, p], device_id=to, device_id_type=MESH)

            local.append(pltpu.make_async_copy(src, slot(me, whole), local_sems.at[p]))
            first += [copy(0, me, sibling, src=src), copy(1, me, (*xn, c), src=src), copy(2, me, (*yn, c), src=src)]
            got_ici += [copy(1, (*xn, c), me), copy(2, (*yn, c), me)]
            pass1 += [copy(3, (*xn, c), (*yn, c), left), copy(4, (*yn, c), (*xn, c), right),
                      copy(5, (*xn, c), sibling), copy(6, (*yn, c), sibling)]
            got_fwd += [copy(3, (*dg, c), me, left), copy(4, (*dg, c), me, right)]
            pass2 += [copy(7, (*dg, c), sibling, left), copy(8, (*dg, c), sibling, right)]
            got_d2d += [copy(0, sibling, me), copy(5, (*xn, 1 - c), me), copy(6, (*yn, 1 - c), me),
                        copy(7, (*dg, 1 - c), me, left), copy(8, (*dg, 1 - c), me, right)]
        return local, first, pass1, pass2, got_ici, got_fwd, got_d2d

    def start(cins, couts, sems):
        local, first, *_ = copies(cins, couts, sems)
        for cp in local + first:
            cp.start()

    def pass_on(cins, couts, sems):
        _, _, pass1, _, got_ici, _, _ = copies(cins, couts, sems)
        for cp in got_ici:
            cp.wait_recv()
        for cp in pass1:
            cp.start()

    def finish(cins, couts, sems):
        local, first, pass1, pass2, _, got_fwd, got_d2d = copies(cins, couts, sems)
        for cp in got_fwd:
            cp.wait_recv()
        for cp in pass2:
            cp.start()
        for cp in got_d2d:
            cp.wait_recv()
        for cp in first + pass1 + pass2:
            cp.wait_send()
        for cp in local:
            cp.wait()

    sem_shapes = [pltpu.SemaphoreType.DMA((9, n)), pltpu.SemaphoreType.DMA((9, n)), pltpu.SemaphoreType.DMA((n,))]
    return CommJob(inputs, aliases, out_shapes, sem_shapes, start, finish, relay=pass_on)


def sibling_exchange_job(grads):
    n = len(grads)

    def copies(cins, couts, sems):
        send_sems, recv_sems = sems
        x, y, c = _mesh_pos()
        return [pltpu.make_async_remote_copy(
            src_ref=cins[q].at[:, 1 - c], dst_ref=couts[q], send_sem=send_sems.at[q], recv_sem=recv_sems.at[q],
            device_id=(x, y, 1 - c), device_id_type=MESH) for q in range(n)]

    def start(cins, couts, sems):
        for cp in copies(cins, couts, sems):
            cp.start()

    def finish(cins, couts, sems):
        cps = copies(cins, couts, sems)
        for cp in cps:
            cp.wait_recv()
        for cp in cps:
            cp.wait_send()

    return CommJob(grads, {}, [_sds((4,) + g.shape[2:], g.dtype) for g in grads],
                   [pltpu.SemaphoreType.DMA((n,)), pltpu.SemaphoreType.DMA((n,))], start, finish)


def chip_exchange_job(pieces):
    inputs, aliases, out_shapes, plan, where = [], {}, [], [], {}
    for partial, contrib, key, layer, row0, nrows, col0, cols in pieces:
        if key not in where:
            where[key] = len(out_shapes)
            out_shapes.append(_sds((4, DEPTH, partial.shape[1], cols), partial.dtype))
            if contrib is not None:
                aliases[len(inputs)] = where[key]
                inputs.append(contrib)
        plan.append((len(inputs), where[key], layer, row0, nrows, col0, partial.shape[2]))
        inputs.append(partial)
    n = len(plan)

    def copies(cins, couts, sems):
        send_sems, recv_sems, local_sems = sems
        x, y, c = _mesh_pos()
        mine = 2 * x + y
        local, sends, recvs = [], [], []
        for p, (i_in, i_out, layer, row0, nrows, col0, ncols) in enumerate(plan):
            rows, lanes = pl.ds(row0, nrows), pl.ds(col0, ncols)
            local.append(pltpu.make_async_copy(cins[i_in].at[mine, rows, :], couts[i_out].at[mine, layer, rows, lanes],
                                               local_sems.at[p]))
            for j, (cx, cy) in enumerate([(1 - x, y), (x, 1 - y), (1 - x, 1 - y)]):
                theirs = 2 * cx + cy

                def copy(slot_there, j=j, p=p, cx=cx, cy=cy, theirs=theirs, i_in=i_in, i_out=i_out, layer=layer,
                         rows=rows, lanes=lanes):
                    return pltpu.make_async_remote_copy(
                        src_ref=cins[i_in].at[theirs, rows, :], dst_ref=couts[i_out].at[slot_there, layer, rows, lanes],
                        send_sem=send_sems.at[j, p], recv_sem=recv_sems.at[j, p], device_id=(cx, cy, c), device_id_type=MESH)
                sends.append(copy(mine))
                recvs.append(copy(theirs))
        return local, sends, recvs

    def start(cins, couts, sems):
        local, sends, _ = copies(cins, couts, sems)
        for cp in local + sends:
            cp.start()

    def finish(cins, couts, sems):
        local, sends, recvs = copies(cins, couts, sems)
        for cp in recvs:
            cp.wait_recv()
        for cp in sends:
            cp.wait_send()
        for cp in local:
            cp.wait()

    sem_shapes = [pltpu.SemaphoreType.DMA((3, n)), pltpu.SemaphoreType.DMA((3, n)), pltpu.SemaphoreType.DMA((n,))]
    return CommJob(inputs, aliases, out_shapes, sem_shapes, start, finish)


def fwd_proj(x, g1, wt_in, l, comm=None):
    s = x.shape[0]
    tm = min(512, s)
    tn = 1408

    def body(x_ref, g_ref, w_ref, o_ref, xn_ref):
        @pl.when(pl.program_id(1) == 0)
        def _():
            xv = x_ref[...]
            r = lax.rsqrt(jnp.mean(xv * xv, axis=-1, keepdims=True) + EPS)
            xn_ref[...] = (xv * r * g_ref[l:l + 1, :]).astype(BF16)

        o_ref[...] = _dot(xn_ref[...], w_ref[...], _NT).astype(BF16)

    return _call(
        body, comm, (x, g1, wt_in), grid=(s // tm, IN_W // tn),
        in_specs=[pl.BlockSpec((tm, D), lambda i, j: (i, 0)),
                  pl.BlockSpec((DEPTH, D), lambda i, j: (0, 0)),
                  pl.BlockSpec((tn, D), lambda i, j: (j, 0))],
        out_specs=pl.BlockSpec((tm, tn), lambda i, j: (i, j)),
        out_shape=_sds((s, IN_W), BF16),
        scratch_shapes=[pltpu.VMEM((tm, D), BF16)], name=f"fwd_proj{l}")


def _scan_fwd(a_ref, u_ref, h_ref, h0, n_rows):
    row = lax.broadcasted_iota(jnp.int32, (8, BW), 0)

    def body(g, hprev):
        r = pl.multiple_of(g * 8, 8)
        a = a_ref[pl.ds(r, 8), :]
        u = u_ref[pl.ds(r, 8), :]
        for sft in (1, 2, 4):
            a_sh = jnp.where(row >= sft, pltpu.roll(a, sft, 0), 1.0)
            u_sh = jnp.where(row >= sft, pltpu.roll(u, sft, 0), 0.0)
            u = u + a * u_sh
            a = a * a_sh
        h = u + a * hprev
        h_ref[pl.ds(r, 8), :] = h
        return h[7:8, :]

    return lax.fori_loop(0, n_rows // 8, body, h0)


def _scan_bwd(b_ref, g_ref, o_ref, c0, n_rows):
    row = lax.broadcasted_iota(jnp.int32, (8, BW), 0)

    def body(k, cnext):
        r = pl.multiple_of((n_rows // 8 - 1 - k) * 8, 8)
        b = b_ref[pl.ds(r, 8), :]
        g = g_ref[pl.ds(r, 8), :]
        for sft in (1, 2, 4):
            b_sh = jnp.where(row < 8 - sft, pltpu.roll(b, 8 - sft, 0), 1.0)
            g_sh = jnp.where(row < 8 - sft, pltpu.roll(g, 8 - sft, 0), 0.0)
            g = g + b * g_sh
            b = b * b_sh
        o = g + b * cnext
        o_ref[pl.ds(r, 8), :] = o
        return o[0:1, :]

    return lax.fori_loop(0, n_rows // 8, body, c0)


def _shifted_copies(buf, shifted, n_rows):
    for r in range(1, 8):
        shifted[r - 1, 0:n_rows - 8, :] = buf[pl.ds(r, n_rows - 8), :]


def _window(buf, shifted, off, t):
    r = off % 8
    return buf[pl.ds(off, t), :] if r == 0 else shifted[r - 1, pl.ds(off - r, t), :]


def _branch_fwd_math(cur_ref, halo_ref, cw_ref, vec_ref, wx_ref, wa_ref, bufa, bufb, bufd, xd, first, t, saved_ref=None):
    def halo(c0):
        v = halo_ref[:, c0:c0 + BW].astype(F32)
        return jnp.where(first, 0.0, v)

    def cur(c0):
        return cur_ref[:, c0:c0 + BW].astype(F32)

    out = {}
    bufa[0:HALO, :] = halo(C_AX)
    bufa[HALO:HALO + t, :] = cur(C_AX)
    ca = jnp.zeros((t, BW), F32) + vec_ref[V_CAB:V_CAB + 1, :]
    for k in range(CONV_A):
        ca = ca + cw_ref[CW_A + k:CW_A + k + 1, :] * bufa[pl.ds(HALO - (CONV_A - 1) + k, t), :]
    if saved_ref is None:
        gi = _sigmoid(_dot(ca, wx_ref[...], _NN) + vec_ref[V_BX:V_BX + 1, :])
        gr = _sigmoid(_dot(ca, wa_ref[...], _NN) + vec_ref[V_BA:V_BA + 1, :])
    else:
        gi, gr = saved_ref[:, BW:2 * BW], saved_ref[:, 2 * BW:3 * BW]
    sp = _softplus(-vec_ref[V_LAM:V_LAM + 1, :])
    la = -LRU_C * sp * gr
    a = jnp.exp(la)
    mult = jnp.sqrt(_neg_expm1(2.0 * la))
    out.update(ca=ca, gi=gi, gr=gr, sp=sp, a=a, mult=mult)
    bufb[0:HALO, :] = halo(C_BC) * halo(C_BV)
    bufb[HALO:HALO + t, :] = cur(C_BC) * cur(C_BV)
    cb = jnp.zeros((t, BW), F32)
    for k in range(CONV_B):
        cb = cb + cw_ref[CW_B + k:CW_B + k + 1, :] * bufb[pl.ds(HALO - (CONV_B - 1) + k, t), :]
    out.update(cb=cb)
    bufd[0:HALO, :] = halo(C_D1) * _sigmoid(halo(C_D2))
    s2 = _sigmoid(cur(C_D2))
    bufd[HALO:HALO + t, :] = cur(C_D1) * s2
    _shifted_copies(bufd, xd, t + HALO)
    if saved_ref is None:
        cd = jnp.zeros((t, BW), F32) + vec_ref[V_CDB:V_CDB + 1, :]
        for k in range(CONV_D):
            cd = cd + cw_ref[CW_D + k:CW_D + k + 1, :] * _window(bufd, xd, HALO - (CONV_D - 1) + k, t)
    else:
        cd = saved_ref[:, 0:BW]
    mu = jnp.mean(cd, axis=-1, keepdims=True)
    xc = cd - mu
    rstd = lax.rsqrt(jnp.mean(xc * xc, axis=-1, keepdims=True) + EPS)
    xh = xc * rstd
    ln = xh * vec_ref[V_LNG:V_LNG + 1, :] + vec_ref[V_LNB:V_LNB + 1, :]
    out.update(s2=s2, xh=xh, rstd=rstd, ln=ln, cd=cd)
    return out


def fwd_branch(proj, convw, vecs, wx_bd, wa_bd, l, comm=None):
    s = proj.shape[0]
    t = min(256, s)

    def body(cur_ref, halo_ref, cw_ref, vec_ref, wx_ref, wa_ref, pre_ref, h_ref, sv_ref, bufa, bufb, bufd, xd, a_s, u_s, hcar):
        first = pl.program_id(0) == 0

        @pl.when(first)
        def _():
            hcar[...] = jnp.zeros((1, BW), F32)

        v = _branch_fwd_math(cur_ref, halo_ref, cw_ref, vec_ref, wx_ref, wa_ref, bufa, bufb, bufd, xd, first, t)
        a_s[...] = v["a"]
        u_s[...] = v["ca"] * v["gi"] * v["mult"]
        sv_ref[:, 0:BW] = v["cd"]
        sv_ref[:, BW:2 * BW] = v["gi"]
        sv_ref[:, 2 * BW:3 * BW] = v["gr"]
        hcar[...] = _scan_fwd(a_s, u_s, h_ref, hcar[...], t)
        gg, _ = _gelu_and_grad(cur_ref[:, C_AG:C_AG + BW].astype(F32))
        pre_ref[:, 0:BW] = (h_ref[...] * gg).astype(BF16)
        pre_ref[:, BW:2 * BW] = (cur_ref[:, C_BB:C_BB + BW].astype(F32) * v["cb"]).astype(BF16)
        ln = v["ln"]
        pre_ref[:, 2 * BW:3 * BW] = (ln * _sigmoid(ln)).astype(BF16)

    hb = t // HALO
    return _call(
        body, comm, (proj, proj, convw, vecs, wx_bd, wa_bd), grid=(s // t,),
        in_specs=[pl.BlockSpec((t, GL0), lambda i: (i, 0)),
                  pl.BlockSpec((HALO, GL0), lambda i: (jnp.maximum(i * hb - 1, 0), 0)),
                  pl.BlockSpec((None, CW_ROWS, BW), lambda i: (l, 0, 0)),
                  pl.BlockSpec((None, V_ROWS, BW), lambda i: (l, 0, 0)),
                  pl.BlockSpec((None, BW, BW), lambda i: (l, 0, 0)),
                  pl.BlockSpec((None, BW, BW), lambda i: (l, 0, 0))],
        out_specs=[pl.BlockSpec((t, 3 * BW), lambda i: (i, 0)), pl.BlockSpec((t, BW), lambda i: (i, 0)),
                   pl.BlockSpec((t, 3 * BW), lambda i: (i, 0))],
        out_shape=[_sds((s, 3 * BW), BF16), _sds((s, BW), F32), _sds((s, 3 * BW), F32)],
        scratch_shapes=[pltpu.VMEM((t + HALO, BW), F32)] * 3 + [pltpu.VMEM((7, t + HALO - 8, BW), F32)]
        + [pltpu.VMEM((t, BW), F32)] * 2 + [pltpu.VMEM((1, BW), F32)],
        name=f"fwd_branch{l}")


GRP = N_HEADS // N_KV


ATT_SUB = 2


def _attn_mask_bias(first_block):
    shape = (GRP * ATT_BLK, 2 * ATT_BLK)
    qi = lax.broadcasted_iota(jnp.int32, shape, 0) & (ATT_BLK - 1)
    ki = lax.broadcasted_iota(jnp.int32, shape, 1)
    dist = qi + ATT_BLK - ki
    valid = (dist >= 0) & (dist < ATT_BLK)
    if first_block is not None:
        valid = valid & (jnp.logical_not(first_block) | (ki >= ATT_BLK))
    return dist.astype(F32), valid


def _attn_units(q_ref, kvp_ref, kvc_ref, first_step):
    units = []
    for b in range(ATT_SUB):
        rows = slice(b * ATT_BLK, (b + 1) * ATT_BLK)
        if b == 0:
            prev = lambda c0, c1: kvp_ref[:, c0:c1]
        else:
            prev = lambda c0, c1, b=b: kvc_ref[(b - 1) * ATT_BLK:b * ATT_BLK, c0:c1]
        for hk in range(N_KV):
            units.append(dict(b=b, hk=hk, rows=rows, q=lambda c0, c1, rows=rows: q_ref[rows, c0:c1], prev=prev,
                              cur=lambda c0, c1, rows=rows: kvc_ref[rows, c0:c1], first=first_step if b == 0 else None))
    return units


def _per_head(hk, values):
    hl = lax.broadcasted_iota(jnp.int32, (GRP * ATT_BLK, 1), 0) // ATT_BLK
    out = values[GRP - 1]
    for j in range(GRP - 2, -1, -1):
        out = jnp.where(hl == j, values[j], out)
    return out


def _attn_probs(units, vec_ref):
    us = range(len(units))
    heads = [range(u["hk"] * GRP, (u["hk"] + 1) * GRP) for u in units]
    masks = {id(u["first"]): _attn_mask_bias(u["first"]) for u in units}
    distf = [masks[id(u["first"])][0] for u in units]
    valid = [masks[id(u["first"])][1] for u in units]
    q4 = [jnp.concatenate([units[i]["q"](h * HD, (h + 1) * HD) for h in heads[i]], axis=0) for i in us]
    kcol = [(u["hk"] * HD, (u["hk"] + 1) * HD) for u in units]
    vcol = [((N_KV + u["hk"]) * HD, (N_KV + u["hk"] + 1) * HD) for u in units]
    k2 = [jnp.concatenate([units[i]["prev"](*kcol[i]), units[i]["cur"](*kcol[i])], axis=0) for i in us]
    v2 = [jnp.concatenate([units[i]["prev"](*vcol[i]), units[i]["cur"](*vcol[i])], axis=0) for i in us]
    slope = [_per_head(units[i]["hk"], [2.0 ** (-8.0 * (h + 1) / N_HEADS) for h in heads[i]]) for i in us]
    sink = [_per_head(units[i]["hk"], [vec_ref[V_SINK:V_SINK + 1, h:h + 1] for h in heads[i]]) for i in us]
    sc = [_dot(q4[i], k2[i], _NT) for i in us]
    sc = [jnp.where(valid[i], sc[i] * (HD ** -0.5) - slope[i] * distf[i], NEG_INF) for i in us]
    m = [jnp.maximum(jnp.max(sc[i], axis=-1, keepdims=True), sink[i]) for i in us]
    p = [jnp.exp(sc[i] - m[i]) for i in us]
    es = [jnp.exp(sink[i] - m[i]) for i in us]
    inv = [1.0 / (jnp.sum(p[i], axis=-1, keepdims=True) + es[i]) for i in us]
    return [(q4[i], k2[i], v2[i], p[i] * inv[i], es[i] * inv[i]) for i in us]


def fwd_attn(proj, vecs, l, comm=None):
    s = proj.shape[0]
    t = ATT_SUB * ATT_BLK

    def body(q_ref, kvp_ref, kvc_ref, vec_ref, o_ref):
        units = _attn_units(q_ref, kvp_ref, kvc_ref, pl.program_id(0) == 0)
        groups = _attn_probs(units, vec_ref)
        outs = [_dot(p, v2, _NN).astype(BF16) for _, _, v2, p, _ in groups]
        for u, out in zip(units, outs):
            for j in range(GRP):
                h = u["hk"] * GRP + j
                o_ref[u["rows"], h * HD:(h + 1) * HD] = out[j * ATT_BLK:(j + 1) * ATT_BLK]

    return _call(
        body, comm, (proj, proj, proj, vecs), grid=(s // t,),
        in_specs=[pl.BlockSpec((t, BW), lambda i: (i, C_Q // BW)),
                  pl.BlockSpec((ATT_BLK, 256), lambda i: (jnp.maximum(ATT_SUB * i - 1, 0), C_K // 256)),
                  pl.BlockSpec((t, 256), lambda i: (i, C_K // 256)),
                  pl.BlockSpec((None, V_ROWS, BW), lambda i: (l, 0, 0))],
        out_specs=pl.BlockSpec((t, BW), lambda i: (i, 0)),
        out_shape=_sds((s, BW), BF16), name=f"fwd_attn{l}")


def fwd_merge(x, proj, pre_abd, pre_c, wt_a, wt_b, wt_c, wt_d, w_o, l, comm=None):
    s = x.shape[0]
    tm = min(256, s)

    def body(x_ref, gl_ref, pabd_ref, pc_ref, wa_ref, wb_ref, wc_ref, wd_ref, wo_ref, y_ref, mg_ref, h1_ref):
        pres = (pabd_ref[:, 0:BW], pabd_ref[:, BW:2 * BW], pc_ref[...], pabd_ref[:, 2 * BW:3 * BW])
        merged = jnp.zeros((tm, D), F32)
        for k, (pre, w_ref) in enumerate(zip(pres, (wa_ref, wb_ref, wc_ref, wd_ref))):
            yk = _dot(pre, w_ref[...], _NT)
            y_ref[:, k * D:(k + 1) * D] = yk.astype(BF16)
            merged = merged + _sigmoid(gl_ref[:, k * D:(k + 1) * D].astype(F32)) * yk
        mg_ref[...] = merged.astype(BF16)
        h1_ref[...] = x_ref[...] + _dot(merged, wo_ref[...], _NN)

    wspec = pl.BlockSpec((D, BW), lambda i: (0, 0))
    return _call(
        body, comm, (x, proj, pre_abd, pre_c, wt_a, wt_b, wt_c, wt_d, w_o), grid=(s // tm,),
        in_specs=[pl.BlockSpec((tm, D), lambda i: (i, 0)),
                  pl.BlockSpec((E(tm), E(4 * D)), lambda i: (i * tm, GL0)),
                  pl.BlockSpec((tm, 3 * BW), lambda i: (i, 0)),
                  pl.BlockSpec((tm, BW), lambda i: (i, 0)),
                  wspec, wspec, wspec, wspec,
                  pl.BlockSpec((D, D), lambda i: (0, 0))],
        out_specs=[pl.BlockSpec((tm, 4 * D), lambda i: (i, 0)), pl.BlockSpec((tm, D), lambda i: (i, 0)),
                   pl.BlockSpec((tm, D), lambda i: (i, 0))],
        out_shape=[_sds((s, 4 * D), BF16), _sds((s, D), BF16), _sds((s, D), F32)], name=f"fwd_merge{l}")


def fwd_ffn(h1, g2, wt_gate, wt_up, w_down, l, comm=None):
    s = h1.shape[0]
    tm = min(512, s)
    fc = FF // 2

    def body(h_ref, g_ref, wg_ref, wu_ref, wd_ref, xo_ref, fg_ref, fu_ref, hn_ref, acc_ref):
        j = pl.program_id(1)

        @pl.when(j == 0)
        def _():
            hv = h_ref[...]
            r = lax.rsqrt(jnp.mean(hv * hv, axis=-1, keepdims=True) + EPS)
            hn_ref[...] = (hv * r * g_ref[l:l + 1, :]).astype(BF16)
            acc_ref[...] = hv

        fg = _dot(hn_ref[...], wg_ref[...], _NT)
        fu = _dot(hn_ref[...], wu_ref[...], _NT)
        fg_ref[...] = fg.astype(BF16)
        fu_ref[...] = fu.astype(BF16)
        acc_ref[...] += _dot(fg * _sigmoid(fg) * fu, wd_ref[...], _NN)

        @pl.when(j == pl.num_programs(1) - 1)
        def _():
            xo_ref[...] = acc_ref[...]

    wspec = pl.BlockSpec((fc, D), lambda i, j: (j, 0))
    return _call(
        body, comm, (h1, g2, wt_gate, wt_up, w_down), grid=(s // tm, FF // fc),
        in_specs=[pl.BlockSpec((tm, D), lambda i, j: (i, 0)), pl.BlockSpec((DEPTH, D), lambda i, j: (0, 0)),
                  wspec, wspec, wspec],
        out_specs=[pl.BlockSpec((tm, D), lambda i, j: (i, 0)), pl.BlockSpec((tm, fc), lambda i, j: (i, j)),
                   pl.BlockSpec((tm, fc), lambda i, j: (i, j))],
        out_shape=[_sds((s, D), F32), _sds((s, FF), BF16), _sds((s, FF), BF16)],
        scratch_shapes=[pltpu.VMEM((tm, D), BF16), pltpu.VMEM((tm, D), F32)], name=f"fwd_ffn{l}")


def loss_head(x, gf, target):
    s = x.shape[0]
    tm = min(512, s)

    def body(x_ref, g_ref, t_ref, dx_ref, st_ref):
        @pl.when(pl.program_id(0) == 0)
        def _():
            st_ref[...] = jnp.zeros((8, D), F32)

        xv = x_ref[...]
        g = g_ref[...]
        r = lax.rsqrt(jnp.mean(xv * xv, axis=-1, keepdims=True) + EPS)
        n = xv * r
        err = n * g - t_ref[...]
        dy = err * (1.0 / D)
        dn = dy * g
        dx_ref[...] = r * (dn - n * jnp.mean(dn * n, axis=-1, keepdims=True))
        st_ref[0:1, :] += jnp.sum(dy * n, axis=0, keepdims=True)
        lsum = 0.5 * jnp.sum(jnp.mean(err * err, axis=-1, keepdims=True), axis=0, keepdims=True)
        st_ref[1:2, :] += jnp.broadcast_to(lsum, (1, D))

    return pl.pallas_call(
        body, grid=(s // tm,),
        in_specs=[pl.BlockSpec((tm, D), lambda i: (i, 0)), pl.BlockSpec((1, D), lambda i: (0, 0)),
                  pl.BlockSpec((tm, D), lambda i: (i, 0))],
        out_specs=[pl.BlockSpec((tm, D), lambda i: (i, 0)), pl.BlockSpec((8, D), lambda i: (0, 0))],
        out_shape=[_sds((s, D), F32), _sds((8, D), F32)],
        compiler_params=_cparams(1), name="loss_head")(x, gf, target)


def _edge_index(j, i, n_j, n_i):
    return jnp.where((j == 0) | (j == n_j - 1), i, n_i - 1)


def bwd_ffn(dxo, h1, fg, fu, g2, wt_gate, wt_up, w_down, l, comm=None):
    s = h1.shape[0]
    tm = min(512, s)
    fc = 256
    n_j, n_i = FF // fc, s // tm

    def body(dxo_ref, h_ref, fg_ref, fu_ref, g_ref, wg_ref, wu_ref, wd_ref,
             dh_ref, dwg_ref, dwu_ref, dwd_ref, st_ref, dhn, dxo_b, hn_b, ag, au, ad):
        j, i = pl.program_id(0), pl.program_id(1)
        rows = pl.ds(pl.multiple_of(i * tm, tm), tm)
        g = g_ref[l:l + 1, :]

        @pl.when(j == 0)
        def _():
            hv = h_ref[...]
            r = lax.rsqrt(jnp.mean(hv * hv, axis=-1, keepdims=True) + EPS)
            hn_b[rows, :] = (hv * r * g).astype(BF16)
            dxo_b[rows, :] = dxo_ref[...].astype(BF16)
            dhn[rows, :] = jnp.zeros((tm, D), F32)

        @pl.when((j == 0) & (i == 0))
        def _():
            st_ref[...] = jnp.zeros((8, D), F32)

        @pl.when(i == 0)
        def _():
            ag[...] = jnp.zeros((fc, D), F32)
            au[...] = jnp.zeros((fc, D), F32)
            ad[...] = jnp.zeros((fc, D), F32)

        fgv = fg_ref[...].astype(F32)
        fuv = fu_ref[...].astype(F32)
        sg = _sigmoid(fgv)
        sil = fgv * sg
        dxb = dxo_b[rows, :]
        hnb = hn_b[rows, :]
        d_act = _dot(dxb, wd_ref[...], _NT)
        ad[...] += _dot(sil * fuv, dxb, _TN)
        d_fg = (d_act * fuv * (sg * (1.0 + fgv * (1.0 - sg)))).astype(BF16)
        d_fu = (d_act * sil).astype(BF16)
        ag[...] += _dot(d_fg, hnb, _TN)
        au[...] += _dot(d_fu, hnb, _TN)
        dhn[rows, :] += _dot(d_fg, wg_ref[...], _NN) + _dot(d_fu, wu_ref[...], _NN)

        @pl.when(i == n_i - 1)
        def _():
            dwg_ref[...] = ag[...].astype(BF16)
            dwu_ref[...] = au[...].astype(BF16)
            dwd_ref[...] = ad[...].astype(BF16)

        @pl.when(j == n_j - 1)
        def _():
            hv = h_ref[...]
            r = lax.rsqrt(jnp.mean(hv * hv, axis=-1, keepdims=True) + EPS)
            n = hv * r
            dv = dhn[rows, :]
            dn = dv * g
            dh_ref[...] = dxo_ref[...] + r * (dn - n * jnp.mean(dn * n, axis=-1, keepdims=True))
            st_ref[0:1, :] += jnp.sum(dv * n, axis=0, keepdims=True)

    edge = lambda j, i: (_edge_index(j, i, n_j, n_i), 0)
    wspec = pl.BlockSpec((fc, D), lambda j, i: (j, 0))
    dwspec = pl.BlockSpec((fc, D), lambda j, i: (j, 0))
    return _call(
        body, comm, (dxo, h1, fg, fu, g2, wt_gate, wt_up, w_down), grid=(n_j, n_i),
        in_specs=[pl.BlockSpec((tm, D), edge),
                  pl.BlockSpec((tm, D), edge),
                  pl.BlockSpec((tm, fc), lambda j, i: (i, j)), pl.BlockSpec((tm, fc), lambda j, i: (i, j)),
                  pl.BlockSpec((DEPTH, D), lambda j, i: (0, 0)), wspec, wspec, wspec],
        out_specs=[pl.BlockSpec((tm, D), lambda j, i: (jnp.where(j == n_j - 1, i, 0), 0)),
                   dwspec, dwspec, dwspec, pl.BlockSpec((8, D), lambda j, i: (0, 0))],
        out_shape=[_sds((s, D), F32), _sds((FF, D), BF16), _sds((FF, D), BF16), _sds((FF, D), BF16), _sds((8, D), F32)],
        scratch_shapes=[pltpu.VMEM((s, D), F32), pltpu.VMEM((s, D), BF16), pltpu.VMEM((s, D), BF16),
                        pltpu.VMEM((fc, D), F32), pltpu.VMEM((fc, D), F32), pltpu.VMEM((fc, D), F32)],
        name=f"bwd_ffn{l}")


def bwd_merge(dh1, y4, proj, merged, pre_abd, pre_c, wt_a, wt_b, wt_c, wt_d, w_o, l, comm=None):
    s = dh1.shape[0]
    tm = min(256, s)
    n_i = s // tm

    def body(dh_ref, y_ref, gl_ref, mg_ref, pabd_ref, pc_ref, wa_ref, wb_ref, wc_ref, wd_ref, wo_ref,
             dgl_ref, dpre_ref, dwo_ref, dwa_ref, dwb_ref, dwc_ref, dwd_ref, ao, aa, ab, ac, ad):
        i = pl.program_id(0)
        accs = (aa, ab, ac, ad)

        @pl.when(i == 0)
        def _():
            ao[...] = jnp.zeros((D, D), F32)
            for acc in accs:
                acc[...] = jnp.zeros((D, BW), F32)

        dhb = dh_ref[...].astype(BF16)
        dmg = _dot(dhb, wo_ref[...], _NT)
        ao[...] += _dot(mg_ref[...], dhb, _TN)
        pres = (pabd_ref[:, 0:BW], pabd_ref[:, BW:2 * BW], pc_ref[...], pabd_ref[:, 2 * BW:3 * BW])
        for k, (pre, w_ref, acc) in enumerate(zip(pres, (wa_ref, wb_ref, wc_ref, wd_ref), accs)):
            gk = _sigmoid(gl_ref[:, k * D:(k + 1) * D].astype(F32))
            yk = y_ref[:, k * D:(k + 1) * D].astype(F32)
            dgl_ref[:, k * D:(k + 1) * D] = (dmg * yk * gk * (1.0 - gk)).astype(BF16)
            dyk = (dmg * gk).astype(BF16)
            dpre_ref[:, k * BW:(k + 1) * BW] = _dot(dyk, w_ref[...], _NN).astype(BF16)
            acc[...] += _dot(dyk, pre, _TN)

        @pl.when(i == n_i - 1)
        def _():
            dwo_ref[...] = ao[...].astype(BF16)
            for o_ref, acc in zip((dwa_ref, dwb_ref, dwc_ref, dwd_ref), accs):
                o_ref[...] = acc[...].astype(BF16)

    wspec = pl.BlockSpec((D, BW), lambda i: (0, 0))
    dwspec = pl.BlockSpec((D, BW), lambda i: (0, 0))
    return _call(
        body, comm, (dh1, y4, proj, merged, pre_abd, pre_c, wt_a, wt_b, wt_c, wt_d, w_o), grid=(n_i,),
        in_specs=[pl.BlockSpec((tm, D), lambda i: (i, 0)),
                  pl.BlockSpec((tm, 4 * D), lambda i: (i, 0)),
                  pl.BlockSpec((E(tm), E(4 * D)), lambda i: (i * tm, GL0)),
                  pl.BlockSpec((tm, D), lambda i: (i, 0)),
                  pl.BlockSpec((tm, 3 * BW), lambda i: (i, 0)),
                  pl.BlockSpec((tm, BW), lambda i: (i, 0)),
                  wspec, wspec, wspec, wspec,
                  pl.BlockSpec((D, D), lambda i: (0, 0))],
        out_specs=[pl.BlockSpec((E(tm), E(4 * D)), lambda i: (i * tm, GL0)),
                   pl.BlockSpec((tm, 4 * BW), lambda i: (i, 0)),
                   pl.BlockSpec((D, D), lambda i: (0, 0)), dwspec, dwspec, dwspec, dwspec],
        out_shape=[_sds((s, IN_W), BF16), _sds((s, 4 * BW), BF16), _sds((D, D), BF16)] + [_sds((D, BW), BF16)] * 4,
        scratch_shapes=[pltpu.VMEM((D, D), F32)] + [pltpu.VMEM((D, BW), F32)] * 4, name=f"bwd_merge{l}")


def bwd_attn(proj, dpre, vecs, l, comm=None):
    s = proj.shape[0]
    t = ATT_SUB * ATT_BLK
    grp = N_HEADS // N_KV

    def body(q_ref, kvp_ref, kvc_ref, do_ref, vec_ref, dq_ref, dkc_ref, dkp_ref, st_ref):
        @pl.when(pl.program_id(0) == 0)
        def _():
            st_ref[...] = jnp.zeros((8, 128), F32)

        lane = lax.broadcasted_iota(jnp.int32, (1, 128), 1)
        dsink = jnp.zeros((1, 128), F32)
        units = _attn_units(q_ref, kvp_ref, kvc_ref, pl.program_id(0) == 0)
        groups = _attn_probs(units, vec_ref)
        us = range(len(units))
        do4s = [jnp.concatenate([do_ref[u["rows"], h * HD:(h + 1) * HD] for h in range(u["hk"] * grp, (u["hk"] + 1) * grp)],
                                axis=0) for u in units]
        dps = [_dot(do4s[i], groups[i][2], _NT) for i in us]
        deltas = [jnp.sum(groups[i][3] * dps[i], axis=-1, keepdims=True) for i in us]
        dss = [groups[i][3] * (dps[i] - deltas[i]) * (HD ** -0.5) for i in us]
        dq4s = [_dot(dss[i], groups[i][1], _NN).astype(BF16) for i in us]
        dk2s = [_dot(dss[i], groups[i][0], _TN) for i in us]
        dv2s = [_dot(groups[i][3], do4s[i], _TN) for i in us]
        for i, u in enumerate(units):
            psd = groups[i][4] * deltas[i]
            for j in range(grp):
                h = u["hk"] * grp + j
                rows = slice(j * ATT_BLK, (j + 1) * ATT_BLK)
                dq_ref[u["rows"], h * HD:(h + 1) * HD] = dq4s[i][rows]
                dsink = dsink + jnp.where(lane == h, -jnp.sum(psd[rows], axis=0, keepdims=True), 0.0)
        for i, u in enumerate(units):
            nxt = [k for k, w in enumerate(units) if w["hk"] == u["hk"] and w["b"] == u["b"] + 1]
            for grad, c0 in ((dk2s, u["hk"] * HD), (dv2s, (N_KV + u["hk"]) * HD)):
                own = grad[i][ATT_BLK:]
                if nxt:
                    own = own + grad[nxt[0]][0:ATT_BLK]
                dkc_ref[u["rows"], c0:c0 + HD] = own.astype(BF16)
                if u["b"] == 0:
                    dkp_ref[:, c0:c0 + HD] = grad[i][0:ATT_BLK].astype(BF16)
        st_ref[0:1, :] += dsink

    return _call(
        body, comm, (proj, proj, proj, dpre, vecs), grid=(s // t,),
        in_specs=[pl.BlockSpec((t, BW), lambda i: (i, C_Q // BW)),
                  pl.BlockSpec((ATT_BLK, 256), lambda i: (jnp.maximum(ATT_SUB * i - 1, 0), C_K // 256)),
                  pl.BlockSpec((t, 256), lambda i: (i, C_K // 256)),
                  pl.BlockSpec((t, BW), lambda i: (i, 2)),
                  pl.BlockSpec((None, V_ROWS, BW), lambda i: (l, 0, 0))],
        out_specs=[pl.BlockSpec((t, BW), lambda i: (i, 0)), pl.BlockSpec((t, 256), lambda i: (i, 0)),
                   pl.BlockSpec((ATT_BLK, 256), lambda i: (i, 0)), pl.BlockSpec((8, 128), lambda i: (0, 0))],
        out_shape=[_sds((s, BW), BF16), _sds((s, 256), BF16), _sds((s // ATT_SUB, 256), BF16), _sds((8, 128), F32)],
        name=f"bwd_attn{l}")


def bwd_branch(proj, dproj, dpre, h, saved, dq, dkc, dkp, convw, vecs, wx_bd, wa_bd, l, comm=None):
    s = proj.shape[0]
    t = 2 * ATT_BLK
    nt = s // t
    nb = s // ATT_BLK
    hb = t // HALO

    def body(cur_ref, halo_ref, dpre_ref, h_ref, hp_ref, dq_ref, dkc_ref, dkp_ref,
             cw_ref, vec_ref, wx_ref, wa_ref, sv_ref, dproj_in, dp_ref, dcw_ref, dvec_ref, dwx_ref, dwa_ref,
             bufa, bufb, bufd, xd, xg, a_ext, hbuf, b_s, g_s, dh_s, ga, gb, gd, dhcar):
        del dproj_in
        step = pl.program_id(0)
        ti = nt - 1 - step
        first = ti == 0

        @pl.when(step == 0)
        def _():
            dcw_ref[...] = jnp.zeros((CW_ROWS, BW), F32)
            dvec_ref[...] = jnp.zeros((V_ROWS, BW), F32)
            dwx_ref[...] = jnp.zeros((BW, BW), F32)
            dwa_ref[...] = jnp.zeros((BW, BW), F32)
            dhcar[...] = jnp.zeros((1, BW), F32)
            a_ext[t:t + 8, :] = jnp.zeros((8, BW), F32)
            ga[t:t + 8, :] = jnp.zeros((8, BW), F32)
            gb[t:t + 8, :] = jnp.zeros((8, BW), F32)
            gd[t:t + HALO, :] = jnp.zeros((HALO, BW), F32)

        def cur(c0):
            return cur_ref[:, c0:c0 + BW].astype(F32)

        def rsum(v):
            return jnp.sum(v, axis=0, keepdims=True)

        def put(c0, v):
            dp_ref[:, c0:c0 + BW] = v.astype(BF16)

        v = _branch_fwd_math(cur_ref, halo_ref, cw_ref, vec_ref, wx_ref, wa_ref, bufa, bufb, bufd, xd, first, t, sv_ref)
        ca, gi, gr, sp, a, mult = v["ca"], v["gi"], v["gr"], v["sp"], v["a"], v["mult"]
        dpa = dpre_ref[:, 0:BW].astype(F32)
        gg, dgg = _gelu_and_grad(cur(C_AG))
        hv = h_ref[...]
        put(C_AG, dpa * hv * dgg)
        a_ext[0:t, :] = a
        b_s[...] = a_ext[pl.ds(1, t), :]
        g_s[...] = dpa * gg
        dhcar[...] = _scan_bwd(b_s, g_s, dh_s, dhcar[...], t)
        a_ext[t:t + 1, :] = a[0:1, :]
        dh = dh_s[...]
        hbuf[0:8, :] = jnp.where(first, 0.0, hp_ref[...])
        hbuf[8:8 + t, :] = hv
        da = dh * hbuf[pl.ds(7, t), :]
        d_ca = dh * gi * mult
        d_gi = dh * ca * mult
        d_mult = dh * ca * gi
        d_la = da * a - d_mult * (a * a) / mult
        lam = vec_ref[V_LAM:V_LAM + 1, :]
        dvec_ref[V_LAM:V_LAM + 1, :] += rsum(d_la * gr) * (LRU_C * _sigmoid(-lam))
        d_gr = d_la * (-LRU_C * sp)
        d_zr = d_gr * gr * (1.0 - gr)
        d_zi = d_gi * gi * (1.0 - gi)
        dvec_ref[V_BA:V_BA + 1, :] += rsum(d_zr)
        dvec_ref[V_BX:V_BX + 1, :] += rsum(d_zi)
        dwa_ref[...] += _dot(ca, d_zr, _TN)
        dwx_ref[...] += _dot(ca, d_zi, _TN)
        d_ca = d_ca + _dot(d_zi, wx_ref[...], _NT) + _dot(d_zr, wa_ref[...], _NT)
        dvec_ref[V_CAB:V_CAB + 1, :] += rsum(d_ca)
        ga[0:t, :] = d_ca
        d_ax = jnp.zeros((t, BW), F32)
        for k in range(CONV_A):
            d_ax = d_ax + cw_ref[CW_A + k:CW_A + k + 1, :] * ga[pl.ds(CONV_A - 1 - k, t), :]
            dcw_ref[CW_A + k:CW_A + k + 1, :] += rsum(d_ca * bufa[pl.ds(HALO - (CONV_A - 1) + k, t), :])
        ga[t:t + 8, :] = d_ca[0:8, :]
        put(C_AX, d_ax)
        dpb = dpre_ref[:, BW:2 * BW].astype(F32)
        put(C_BB, dpb * v["cb"])
        d_cb = dpb * cur(C_BB)
        gb[0:t, :] = d_cb
        d_cbin = jnp.zeros((t, BW), F32)
        for k in range(CONV_B):
            d_cbin = d_cbin + cw_ref[CW_B + k:CW_B + k + 1, :] * gb[pl.ds(CONV_B - 1 - k, t), :]
            dcw_ref[CW_B + k:CW_B + k + 1, :] += rsum(d_cb * bufb[pl.ds(HALO - (CONV_B - 1) + k, t), :])
        gb[t:t + 8, :] = d_cb[0:8, :]
        put(C_BC, d_cbin * cur(C_BV))
        put(C_BV, d_cbin * cur(C_BC))
        dpd = dpre_ref[:, 3 * BW:4 * BW].astype(F32)
        ln, xh, rstd, s2 = v["ln"], v["xh"], v["rstd"], v["s2"]
        sg = _sigmoid(ln)
        d_ln = dpd * sg * (1.0 + ln * (1.0 - sg))
        dvec_ref[V_LNG:V_LNG + 1, :] += rsum(d_ln * xh)
        dvec_ref[V_LNB:V_LNB + 1, :] += rsum(d_ln)
        d_xh = d_ln * vec_ref[V_LNG:V_LNG + 1, :]
        d_cd = rstd * (d_xh - jnp.mean(d_xh, axis=-1, keepdims=True)
                       - xh * jnp.mean(d_xh * xh, axis=-1, keepdims=True))
        dvec_ref[V_CDB:V_CDB + 1, :] += rsum(d_cd)
        gd[0:t, :] = d_cd
        _shifted_copies(gd, xg, t + HALO)
        d_dg = jnp.zeros((t, BW), F32)
        for k in range(CONV_D):
            d_dg = d_dg + cw_ref[CW_D + k:CW_D + k + 1, :] * _window(gd, xg, CONV_D - 1 - k, t)
            dcw_ref[CW_D + k:CW_D + k + 1, :] += rsum(d_cd * _window(bufd, xd, HALO - (CONV_D - 1) + k, t))
        gd[t:t + HALO, :] = d_cd[0:HALO, :]
        put(C_D1, d_dg * s2)
        put(C_D2, d_dg * cur(C_D1) * s2 * (1.0 - s2))
        dp_ref[:, C_Q:C_Q + BW] = dq_ref[...]
        dkp = jnp.where(step == 0, 0.0, dkp_ref[...].astype(F32))
        dp_ref[0:t - ATT_BLK, C_K:C_K + 256] = dkc_ref[0:t - ATT_BLK, :]
        dp_ref[t - ATT_BLK:t, C_K:C_K + 256] = (dkc_ref[t - ATT_BLK:t, :].astype(F32) + dkp).astype(BF16)

    rev = lambda i: nt - 1 - i
    full = lambda r, c: pl.BlockSpec((r, c), lambda i: (0, 0))
    return _call(
        body, comm, (proj, proj, dpre, h, h, dq, dkc, dkp, convw, vecs, wx_bd, wa_bd, saved, dproj), grid=(nt,),
        in_specs=[pl.BlockSpec((t, GL0), lambda i: (rev(i), 0)),
                  pl.BlockSpec((HALO, GL0), lambda i: (jnp.maximum(rev(i) * hb - 1, 0), 0)),
                  pl.BlockSpec((t, 4 * BW), lambda i: (rev(i), 0)),
                  pl.BlockSpec((t, BW), lambda i: (rev(i), 0)),
                  pl.BlockSpec((8, BW), lambda i: (jnp.maximum(rev(i) * (t // 8) - 1, 0), 0)),
                  pl.BlockSpec((t, BW), lambda i: (rev(i), 0)),
                  pl.BlockSpec((t, 256), lambda i: (rev(i), 0)),
                  pl.BlockSpec((ATT_BLK, 256), lambda i: (jnp.minimum(rev(i) + 1, nt - 1), 0)),
                  pl.BlockSpec((None, CW_ROWS, BW), lambda i: (l, 0, 0)),
                  pl.BlockSpec((None, V_ROWS, BW), lambda i: (l, 0, 0)),
                  pl.BlockSpec((None, BW, BW), lambda i: (l, 0, 0)),
                  pl.BlockSpec((None, BW, BW), lambda i: (l, 0, 0)),
                  pl.BlockSpec((t, 3 * BW), lambda i: (rev(i), 0)),
                  pl.BlockSpec(memory_space=pl.ANY)],
        out_specs=[pl.BlockSpec((t, GL0), lambda i: (rev(i), 0)),
                   full(CW_ROWS, BW), full(V_ROWS, BW), full(BW, BW), full(BW, BW)],
        out_shape=[_sds((s, IN_W), BF16), _sds((CW_ROWS, BW), F32), _sds((V_ROWS, BW), F32),
                   _sds((BW, BW), F32), _sds((BW, BW), F32)],
        scratch_shapes=[pltpu.VMEM((t + HALO, BW), F32)] * 3 + [pltpu.VMEM((7, t + HALO - 8, BW), F32)] * 2
        + [pltpu.VMEM((t + 8, BW), F32), pltpu.VMEM((t + 8, BW), F32)]
        + [pltpu.VMEM((t, BW), F32)] * 3
        + [pltpu.VMEM((t + 8, BW), F32), pltpu.VMEM((t + 8, BW), F32), pltpu.VMEM((t + HALO, BW), F32),
           pltpu.VMEM((1, BW), F32)],
        aliases={13: 0}, name=f"bwd_branch{l}")


def bwd_proj(dproj, x, dh1, g1, wt_in, l, comm=None):
    s = x.shape[0]
    tm = min(512, s)
    ck = 1408
    n_j, n_i = IN_W // ck, s // tm

    def body(dp_ref, x_ref, dh_ref, g_ref, w_ref, dx_ref, dw_ref, st_ref, dxn, xn_b, acc):
        j, i = pl.program_id(0), pl.program_id(1)
        rows = pl.ds(pl.multiple_of(i * tm, tm), tm)
        g = g_ref[l:l + 1, :]

        @pl.when(j == 0)
        def _():
            xv = x_ref[...]
            r = lax.rsqrt(jnp.mean(xv * xv, axis=-1, keepdims=True) + EPS)
            xn_b[rows, :] = (xv * r * g).astype(BF16)
            dxn[rows, :] = jnp.zeros((tm, D), F32)

        @pl.when((j == 0) & (i == 0))
        def _():
            st_ref[...] = jnp.zeros((8, D), F32)

        @pl.when(i == 0)
        def _():
            acc[...] = jnp.zeros((ck, D), F32)

        dp = dp_ref[...]
        dxn[rows, :] += _dot(dp, w_ref[...], _NN)
        acc[...] += _dot(dp, xn_b[rows, :], _TN)

        @pl.when(i == n_i - 1)
        def _():
            dw_ref[...] = acc[...].astype(BF16)

        @pl.when(j == n_j - 1)
        def _():
            xv = x_ref[...]
            r = lax.rsqrt(jnp.mean(xv * xv, axis=-1, keepdims=True) + EPS)
            n = xv * r
            dv = dxn[rows, :]
            dn = dv * g
            dx_ref[...] = dh_ref[...] + r * (dn - n * jnp.mean(dn * n, axis=-1, keepdims=True))
            st_ref[0:1, :] += jnp.sum(dv * n, axis=0, keepdims=True)

    lastrow = lambda j, i: (jnp.where(j == n_j - 1, i, 0), 0)
    return _call(
        body, comm, (dproj, x, dh1, g1, wt_in), grid=(n_j, n_i),
        in_specs=[pl.BlockSpec((tm, ck), lambda j, i: (i, j)),
                  pl.BlockSpec((tm, D), lambda j, i: (_edge_index(j, i, n_j, n_i), 0)),
                  pl.BlockSpec((tm, D), lastrow),
                  pl.BlockSpec((DEPTH, D), lambda j, i: (0, 0)),
                  pl.BlockSpec((ck, D), lambda j, i: (j, 0))],
        out_specs=[pl.BlockSpec((tm, D), lastrow), pl.BlockSpec((ck, D), lambda j, i: (j, 0)),
                   pl.BlockSpec((8, D), lambda j, i: (0, 0))],
        out_shape=[_sds((s, D), F32), _sds((IN_W, D), BF16), _sds((8, D), F32)],
        scratch_shapes=[pltpu.VMEM((s, D), F32), pltpu.VMEM((s, D), BF16), pltpu.VMEM((ck, D), F32)],
        name=f"bwd_proj{l}")


def bwd_proj_w(dproj, x, g1, l, half, comm=None):
    s = x.shape[0]
    tm = min(1024, s)
    ck = 1408
    c0, hw = W_IN_PARTS[half]
    n_j, n_i = IN_W // ck, s // tm

    def body(dp_ref, x_ref, g_ref, dw_ref, xn_b, acc):
        j, i = pl.program_id(0), pl.program_id(1)
        rows = pl.ds(pl.multiple_of(i * tm, tm), tm)

        @pl.when(j == 0)
        def _():
            xv = x_ref[...]
            r = lax.rsqrt(jnp.mean(xv * xv, axis=-1, keepdims=True) + EPS)
            xn_b[rows, :] = (xv * r * g_ref[l:l + 1, :])[:, c0:c0 + hw].astype(BF16)

        @pl.when(i == 0)
        def _():
            acc[...] = jnp.zeros((ck, hw), F32)

        acc[...] += _dot(dp_ref[...], xn_b[rows, :], _TN)

        @pl.when(i == n_i - 1)
        def _():
            dw_ref[...] = acc[...].astype(BF16)

    return _call(
        body, comm, (dproj, x, g1), grid=(n_j, n_i),
        in_specs=[pl.BlockSpec((tm, ck), lambda j, i: (i, j)),
                  pl.BlockSpec((tm, D), lambda j, i: (jnp.where(j == 0, i, n_i - 1), 0)),
                  pl.BlockSpec((DEPTH, D), lambda j, i: (0, 0))],
        out_specs=pl.BlockSpec((ck, hw), lambda j, i: (j, 0)),
        out_shape=_sds((IN_W, hw), BF16),
        scratch_shapes=[pltpu.VMEM((s, hw), BF16), pltpu.VMEM((ck, hw), F32)],
        name=f"bwd_proj_w{half}_{l}")


def bwd_proj_x(dproj, x, dh1, g1, wt_in, l, comm=None):
    s = x.shape[0]
    tm = min(512, s)
    ck = 1408
    n_j, n_i = IN_W // ck, s // tm

    def body(dp_ref, x_ref, dh_ref, g_ref, w_ref, dx_ref, st_ref, dxn):
        j, i = pl.program_id(0), pl.program_id(1)
        rows = pl.ds(pl.multiple_of(i * tm, tm), tm)
        g = g_ref[l:l + 1, :]

        @pl.when((j == 0) & (i == 0))
        def _():
            st_ref[...] = jnp.zeros((8, D), F32)

        part = _dot(dp_ref[...], w_ref[...], _NN)

        @pl.when(j == 0)
        def _():
            dxn[rows, :] = part

        @pl.when(j > 0)
        def _():
            dxn[rows, :] += part

        @pl.when(j == n_j - 1)
        def _():
            xv = x_ref[...]
            r = lax.rsqrt(jnp.mean(xv * xv, axis=-1, keepdims=True) + EPS)
            n = xv * r
            dv = dxn[rows, :]
            dn = dv * g
            dx_ref[...] = dh_ref[...] + r * (dn - n * jnp.mean(dn * n, axis=-1, keepdims=True))
            st_ref[0:1, :] += jnp.sum(dv * n, axis=0, keepdims=True)

    lastrow = lambda j, i: (jnp.where(j == n_j - 1, i, 0), 0)
    return _call(
        body, comm, (dproj, x, dh1, g1, wt_in), grid=(n_j, n_i),
        in_specs=[pl.BlockSpec((tm, ck), lambda j, i: (i, j)), pl.BlockSpec((tm, D), lastrow),
                  pl.BlockSpec((tm, D), lastrow),
                  pl.BlockSpec((DEPTH, D), lambda j, i: (0, 0)), pl.BlockSpec((ck, D), lambda j, i: (j, 0))],
        out_specs=[pl.BlockSpec((tm, D), lastrow), pl.BlockSpec((8, D), lambda j, i: (0, 0))],
        out_shape=[_sds((s, D), F32), _sds((8, D), F32)],
        scratch_shapes=[pltpu.VMEM((s, D), F32)], name=f"bwd_proj_x{l}")


def _block_diag(w):
    nl, nb, bw, _ = w.shape
    eye = jnp.eye(nb, dtype=w.dtype)
    return jnp.einsum("lhij,hk->lhikj", w, eye).reshape(nl, nb * bw, nb * bw).astype(BF16)


class NoOverlap:
    def __init__(self, big):
        self.big = big

    def weights(self, l):
        return self.big[l]

    def job(self, slot, l):
        return None

    def done(self, slot, l, results):
        pass

    def new_grads(self, group, l, grads):
        pass

    def new_small(self, l, arrays, head_stats):
        pass


def local_step(x, target, norm1_g, norm2_g, final_g, convw, vecs, lru_wx, lru_wa, plan):
    wx_bd, wa_bd = _block_diag(lru_wx), _block_diag(lru_wa)

    def run(fn, slot, l, *args):
        res, cres = fn(*args, l, comm=plan.job(slot, l))
        plan.done(slot, l, cres)
        return res

    saved = []
    for l in range(DEPTH):
        proj = run(fwd_proj, "fwd_proj", l, x, norm1_g, plan.weights(l)["in_t"])
        pre_abd, h, kept = run(fwd_branch, "fwd_branch", l, proj, convw, vecs, wx_bd, wa_bd)
        pre_c = run(fwd_attn, "fwd_attn", l, proj, vecs)
        w = plan.weights(l)
        y4, merged, h1 = run(fwd_merge, "fwd_merge", l, x, proj, pre_abd, pre_c, w["a_t"], w["b_t"], w["c_t"], w["d_t"], w["o"])
        w = plan.weights(l)
        x_out, fg, fu = run(fwd_ffn, "fwd_ffn", l, h1, norm2_g, w["gate_t"], w["up_t"], w["down"])
        saved.append((x, proj, pre_abd, h, kept, pre_c, y4, merged, h1, fg, fu))
        x = x_out
    dx, head_stats = loss_head(x, final_g.reshape(1, D), target)
    small = [None] * DEPTH
    for l in reversed(range(DEPTH)):
        x_in, proj, pre_abd, h, kept, pre_c, y4, merged, h1, fg, fu = saved[l]
        w = plan.weights(l)
        dh1, d_gate, d_up, d_down, st_ffn = run(bwd_ffn, "bwd_ffn", l, dx, h1, fg, fu, norm2_g, w["gate_t"], w["up_t"], w["down"])
        plan.new_grads("ffn", l, dict(gate_t=d_gate, up_t=d_up, down=d_down))
        dproj, dpre, d_o, d_a, d_b, d_c, d_d = run(
            bwd_merge, "bwd_merge", l, dh1, y4, proj, merged, pre_abd, pre_c, w["a_t"], w["b_t"], w["c_t"], w["d_t"], w["o"])
        plan.new_grads("out", l, dict(a_t=d_a, b_t=d_b, c_t=d_c, d_t=d_d, o=d_o))
        dq, dkc, dkp, st_attn = run(bwd_attn, "bwd_attn", l, proj, dpre, vecs)
        dproj, dcw, dvec, dwx, dwa = run(bwd_branch, "bwd_branch", l, proj, dproj, dpre, h, kept, dq, dkc, dkp, convw, vecs, wx_bd, wa_bd)
        if l > 0:
            dx, d_in, st_proj = run(bwd_proj, "bwd_proj", l, dproj, x_in, dh1, norm1_g, w["in_t"])
            plan.new_grads("in", l, dict(in_t=d_in))
        else:
            for half, name in enumerate(("in_a", "in_b")):
                d_half = run(functools.partial(bwd_proj_w, half=half), f"bwd_proj_w{half}", l, dproj, x_in, norm1_g)
                plan.new_grads(name, l, {name: d_half})
            dx, st_proj = run(bwd_proj_x, "bwd_proj_x", l, dproj, x_in, dh1, norm1_g, w["in_t"])
        small[l] = (st_proj, st_ffn, dvec, st_attn, dcw, dwx, dwa)
        plan.new_small(l, small[l], head_stats)
    return head_stats, dx, small


BIG = dict(in_t=("w_in", "view"), a_t=("w_a_out", "transpose"), b_t=("w_b_out", "transpose"), c_t=("w_c_out", "transpose"),
           d_t=("w_d_out", "transpose"), o=("w_o", "plain"), gate_t=("w_ffn_gate", "view"), up_t=("w_ffn_up", "view"),
           down=("w_ffn_down", "plain"))


def cast_transpose(w, name):
    nl, a, b = w.shape
    ta = min(256, a)

    def body(w_ref, o_ref):
        o_ref[...] = w_ref[...].T.astype(BF16)

    return pl.pallas_call(
        body, grid=(nl, a // ta),
        in_specs=[pl.BlockSpec((None, ta, b), lambda l, i: (l, i, 0))],
        out_specs=pl.BlockSpec((None, b, ta), lambda l, i: (l, 0, i)),
        out_shape=_sds((nl, b, a), BF16), compiler_params=_cparams(2), name=name)(w)


def add_partials(mine, recv, core, name):
    n = len(mine)

    def body(core_ref, *refs):
        del core_ref
        for a_ref, b_ref, o_ref in zip(refs[:n], refs[n:2 * n], refs[2 * n:]):
            o_ref[...] = (a_ref[...].astype(F32) + b_ref[...].astype(F32)).astype(BF16)

    return pl.pallas_call(
        body,
        grid_spec=pltpu.PrefetchScalarGridSpec(
            num_scalar_prefetch=1, grid=(4,),
            in_specs=[pl.BlockSpec((None, None) + a.shape[2:], lambda i, cr: (i, cr[0], 0, 0)) for a in mine]
            + [pl.BlockSpec((None,) + b.shape[1:], lambda i, cr: (i, 0, 0)) for b in recv],
            out_specs=[pl.BlockSpec((None,) + b.shape[1:], lambda i, cr: (i, 0, 0)) for b in recv]),
        out_shape=[_sds(b.shape, BF16) for b in recv], compiler_params=_cparams(1), name=name)(core, *mine, *recv)


def _adamw(w, g, m, v):
    m = ADAM_B1 * m + (1.0 - ADAM_B1) * g
    v = ADAM_B2 * v + (1.0 - ADAM_B2) * (g * g)
    m_hat = m / (1.0 - ADAM_B1 ** ADAM_STEP)
    v_hat = v / (1.0 - ADAM_B2 ** ADAM_STEP)
    delta = -ADAM_LR * (m_hat / (jnp.sqrt(v_hat) + ADAM_EPS) + ADAM_WD * w)
    return delta, m, v


def adamw_big(contrib, w, m, v, transposed, name, comm=None):
    nsrc, nl, rows, cols = contrib.shape
    ct = 256

    def body(c_ref, w_ref, m_ref, v_ref, g_out, d_out, m_out, v_out):
        g = c_ref[0].astype(F32)
        for src in range(1, nsrc):
            g = g + c_ref[src].astype(F32)
        if transposed:
            g = g.T
        delta, mn, vn = _adamw(w_ref[...], g, m_ref[...], v_ref[...])
        g_out[...] = g
        d_out[...] = delta
        m_out[...] = mn
        v_out[...] = vn

    if transposed:
        wspec = pl.BlockSpec((None, ct, rows), lambda l, j: (l, j, 0))
    else:
        wspec = pl.BlockSpec((None, rows, ct), lambda l, j: (l, 0, j))
    return _call(
        body, comm, (contrib, w, m, v), grid=(nl, cols // ct),
        in_specs=[pl.BlockSpec((nsrc, None, rows, ct), lambda l, j: (0, l, 0, j)), wspec, wspec, wspec],
        out_specs=[wspec] * 4, out_shape=[_sds(w.shape, F32)] * 4, name=name)


VEC_NAMES = ("conv_a_b", "lru_bx", "lru_ba", "lru_lambda", "conv_d_b", "ln_d_g", "ln_d_b")
P_N1, P_N2, P_VEC, P_CONV, P_LRU = 0, 1, 2, 6, 6 + CW_ROWS
P_FINAL, P_LOSS, P_ROWS = P_LRU + HD, P_LRU + HD + 1, P_LRU + HD + 2
SMALL = ("norm1_g", "conv_a_w", "conv_a_b", "lru_wx", "lru_bx", "lru_wa", "lru_ba", "lru_lambda", "conv_b_w", "sinks",
         "conv_d_w", "conv_d_b", "ln_d_g", "ln_d_b", "norm2_g", "final_g")
VMEM_FULL = pl.BlockSpec(memory_space=pltpu.VMEM)


def _stack_vecs(p):
    rows = [p[n] for n in VEC_NAMES] + [jnp.pad(p["sinks"], ((0, 0), (0, BW - N_HEADS)))]
    return jnp.stack(rows, axis=1)


def _stack_convs(p):
    nl, _, ch = p["conv_a_w"].shape
    z = jnp.zeros((nl, 1, ch), F32)
    return jnp.concatenate([p["conv_a_w"], p["conv_b_w"], z, p["conv_d_w"], z], axis=1)


def _vec_place(r):
    return P_VEC + r // 2, (r % 2) * BW


def pack_small(arrays, head_stats, l):
    n = len(arrays)

    def body(*refs):
        st_proj, st_ffn, dvec, st_attn, dcw, dwx, dwa = refs[:n]
        pack = refs[-1]
        pack[...] = jnp.zeros((P_ROWS, D), F32)
        lane = lax.broadcasted_iota(jnp.int32, (HD, BW), 1)
        pack[P_N1:P_N1 + 1, :] = st_proj[0:1, :]
        pack[P_N2:P_N2 + 1, :] = st_ffn[0:1, :]
        for r in range(len(VEC_NAMES)):
            row, c0 = _vec_place(r)
            pack[row:row + 1, c0:c0 + BW] = dvec[r:r + 1, :]
        row, c0 = _vec_place(V_SINK)
        pack[row:row + 1, c0:c0 + 128] = st_attn[0:1, :]
        pack[P_CONV:P_CONV + CW_ROWS, 0:BW] = dcw[...]
        for mat, c0 in ((dwx, 0), (dwa, BW)):
            blocks = jnp.zeros((HD, BW), F32)
            for h in range(BW // HD):
                blocks = jnp.where((lane >= HD * h) & (lane < HD * (h + 1)), mat[HD * h:HD * (h + 1), :], blocks)
            pack[P_LRU:P_LRU + HD, c0:c0 + BW] = blocks
        if head_stats is not None:
            pack[P_FINAL:P_LOSS + 1, :] = refs[n][0:2, :]

    flat = list(arrays) + ([] if head_stats is None else [head_stats])
    return pl.pallas_call(body, out_shape=_sds((P_ROWS, D), F32), in_specs=[VMEM_FULL] * len(flat), out_specs=VMEM_FULL,
                          name=f"pack_small{l}", compiler_params=pltpu.CompilerParams(vmem_limit_bytes=VMEM_LIMIT))(*flat)


def adamw_small(gathered, me, w, m, v):
    ns = len(SMALL)

    def body(me_ref, *refs):
        c_refs, refs = refs[:DEPTH], refs[DEPTH:]
        w_refs, m_refs, v_refs = refs[:ns], refs[ns:2 * ns], refs[2 * ns:3 * ns]
        loss_ref, outs, gs = refs[3 * ns], refs[3 * ns + 1:3 * ns + 1 + 4 * ns], refs[-1]
        for l in range(DEPTH):
            gs[l] = c_refs[l][0]
            for dev in range(1, NDEV):
                gs[l] += c_refs[l][dev]
        loss_ref[...] = gs[DEPTH - 1, P_LOSS:P_LOSS + 1, 0:128]

        def update(name, sel, g):
            i = SMALL.index(name)
            delta, mn, vn = _adamw(w_refs[i][sel], g, m_refs[i][sel], v_refs[i][sel])
            for o_ref, val in zip(outs[4 * i:4 * i + 4], (g, delta, mn, vn)):
                o_ref[sel] = val

        update("final_g", (slice(0, 1), slice(None)), gs[DEPTH - 1, P_FINAL:P_FINAL + 1, :])
        shift = (BW - me_ref[0] * (BW // NDEV)) & (BW - 1)
        for l in range(DEPTH):
            row = (slice(l, l + 1), slice(None))
            update("norm1_g", row, gs[l, P_N1:P_N1 + 1, :])
            update("norm2_g", row, gs[l, P_N2:P_N2 + 1, :])
            for r, name in enumerate(VEC_NAMES):
                prow, c0 = _vec_place(r)
                update(name, row, gs[l, prow:prow + 1, c0:c0 + BW])
            prow, c0 = _vec_place(V_SINK)
            update("sinks", row, gs[l, prow:prow + 1, c0:c0 + N_HEADS])
            mine = pltpu.roll(gs[l, P_CONV:P_CONV + CW_ROWS, 0:BW], shift, 1)[:, 0:BW // NDEV]
            update("conv_a_w", (l,), mine[CW_A:CW_A + CONV_A])
            update("conv_b_w", (l,), mine[CW_B:CW_B + CONV_B])
            update("conv_d_w", (l,), mine[CW_D:CW_D + CONV_D])
            for h in range(BW // HD):
                update("lru_wx", (l, h), gs[l, P_LRU:P_LRU + HD, HD * h:HD * (h + 1)])
                update("lru_wa", (l, h), gs[l, P_LRU:P_LRU + HD, BW + HD * h:BW + HD * (h + 1)])

    args = [p[n] for p in (w, m, v) for n in SMALL]
    full = lambda a: pl.BlockSpec(a.shape, lambda i, me_ref: (0,) * a.ndim)
    out_shape = [_sds((1, 128), F32)] + [_sds(w[n].shape, F32) for n in SMALL for _ in range(4)]
    outs = pl.pallas_call(
        body,
        grid_spec=pltpu.PrefetchScalarGridSpec(
            num_scalar_prefetch=1, grid=(1,),
            in_specs=[full(a) for a in list(gathered) + args], out_specs=[full(o) for o in out_shape],
            scratch_shapes=[pltpu.VMEM((DEPTH, P_ROWS, D), F32)]),
        out_shape=out_shape, name="adamw_small", compiler_params=_cparams(1))(me, *gathered, *args)
    return outs[0], {n: outs[1 + 4 * i:5 + 4 * i] for i, n in enumerate(SMALL)}


def merge_jobs(jobs):
    jobs = [j for j in jobs if j is not None]
    if not jobs:
        return None, []
    inputs, aliases, outs, sems, cuts = [], {}, [], [], []
    for j in jobs:
        i0, o0, s0 = len(inputs), len(outs), len(sems)
        aliases.update({i0 + i: o0 + o for i, o in j.aliases.items()})
        inputs += j.inputs
        outs += j.out_shapes
        sems += j.sem_shapes
        cuts.append((i0, len(inputs), o0, len(outs), s0, len(sems)))

    def each(which):
        def go(cins, couts, s):
            for j, (i0, i1, o0, o1, s0, s1) in zip(jobs, cuts):
                if getattr(j, which) is not None:
                    getattr(j, which)(cins[i0:i1], couts[o0:o1], s[s0:s1])
        return go

    relay = each("relay") if any(j.relay is not None for j in jobs) else None
    return CommJob(inputs, aliases, outs, sems, each("start"), each("finish"), relay), [(c[2], c[3]) for c in cuts]


SIXTHS = 6
OUT_KINDS = ("a_t", "b_t", "c_t", "d_t", "o")
GATHER_PLAN = {
    "fwd_proj": [(k, 0, 0, 6) for k in OUT_KINDS] + [("gate_t", 0, 0, 6)],
    "fwd_branch": [("up_t", 0, 0, 6)],
    "fwd_attn": [("down", 0, 0, 6)],
    "fwd_merge": [("in_t", 1, 0, 2)],
    "fwd_ffn": [("in_t", 1, 2, 6)],
}
SIBLING_PLAN = {"bwd_merge": ("ffn", 0), "bwd_branch": ("out", 0), "bwd_ffn": ("in", 1),
                "bwd_proj_w1": ("in_a", 0), "bwd_proj_x": ("in_b", 0)}
GROUPS = dict(ffn=("gate_t", "up_t", "down"), out=OUT_KINDS, in_a=("in_a",), in_b=("in_b",))
GROUPS["in"] = ("in_t",)
COLUMN_HALF = dict(in_a=("in_t", W_IN_PARTS[0][0]), in_b=("in_t", W_IN_PARTS[1][0]))
CHIP_PLAN = {
    "bwd_attn": [("in_t", 1, 3, 6)],
    "bwd_branch": [("gate_t", 0, 0, 6), ("up_t", 0, 0, 6), ("down", 0, 0, 3)],
    "bwd_proj": [(k, 0, 0, 6) for k in OUT_KINDS] + [("down", 0, 3, 6)],
    "bwd_proj_w0": [(k, 0, 0, 6) for k in OUT_KINDS[:3]] + [("down", 0, 3, 6)],
    "bwd_proj_w1": [(k, 0, 0, 6) for k in OUT_KINDS[3:]],
    "bwd_merge": [("in_t", 1, 0, 3)],
    "bwd_proj_x": [("in_a", 0, 0, 6)],
    "adamw_gate_t": [("in_b", 0, 0, 6)],
}
SMALL_GATHER_PLAN = {"bwd_ffn": 1, "adamw_down": 0}


class Overlap:
    def __init__(self, shards, core):
        self.shards = shards
        self.core = core
        self.gathered = [dict.fromkeys(BIG) for _ in range(DEPTH)]
        self.views = {}
        self.partial = {}
        self.contrib = dict.fromkeys(BIG)
        self.small_packs = [None] * DEPTH
        self.small_gathered = [None] * DEPTH
        self._open = None

    def weights(self, l):
        return self.gathered[l]

    def new_grads(self, group, l, grads):
        for k, g in grads.items():
            self.views[k, l] = g.reshape(4, 2, g.shape[0] // NDEV, g.shape[1])

    def new_small(self, l, arrays, head_stats):
        self.small_packs[l] = pack_small(arrays, head_stats if l == DEPTH - 1 else None, l)

    @staticmethod
    def _rows(shard_rows, f0, f1):
        return shard_rows * f0 // SIXTHS, shard_rows * (f1 - f0) // SIXTHS

    def job(self, slot, l):
        jobs, notes = [], []
        pieces = [(k, l + dl, f0, f1) for k, dl, f0, f1 in GATHER_PLAN.get(slot, []) if l + dl < DEPTH]
        if pieces:
            jobs.append(gather_job([((k, ll), self.shards[ll][k], self.gathered[ll][k],
                                     *self._rows(self.shards[ll][k].shape[0], f0, f1)) for k, ll, f0, f1 in pieces]))
            notes.append(("gather", list(dict.fromkeys((k, ll) for k, ll, _, _ in pieces))))
        if slot in SIBLING_PLAN and l + SIBLING_PLAN[slot][1] < DEPTH:
            group, dl = SIBLING_PLAN[slot]
            keys = [(k, l + dl) for k in GROUPS[group]]
            jobs.append(sibling_exchange_job([self.views[key] for key in keys]))
            notes.append(("sibling", keys))
        pieces = [(k, l + dl, f0, f1) for k, dl, f0, f1 in CHIP_PLAN.get(slot, []) if l + dl < DEPTH]
        if pieces:
            whole = [(*COLUMN_HALF.get(k, (k, 0)), k, ll, f0, f1) for k, ll, f0, f1 in pieces]
            jobs.append(chip_exchange_job([(self.partial[k, ll], self.contrib[kind], kind, ll,
                                            *self._rows(self.partial[k, ll].shape[1], f0, f1), col0, self.shards[ll][kind].shape[1])
                                           for kind, col0, k, ll, f0, f1 in whole]))
            notes.append(("chips", list(dict.fromkeys(kind for kind, *_ in whole))))
        if slot in SMALL_GATHER_PLAN and l + SMALL_GATHER_PLAN[slot] < DEPTH:
            ll = l + SMALL_GATHER_PLAN[slot]
            jobs.append(gather_job([("small", self.small_packs[ll], None, 0, P_ROWS)]))
            notes.append(("small", ll))
        job, spans = merge_jobs(jobs)
        self._open = (slot, l, notes, spans)
        return job

    def done(self, slot, l, results):
        open_slot, open_l, notes, spans = self._open
        assert (open_slot, open_l) == (slot, l)
        for (what, keys), (r0, r1) in zip(notes, spans):
            res = results[r0:r1]
            if what == "gather":
                for (k, ll), g in zip(keys, res):
                    self.gathered[ll][k] = g
            elif what == "sibling":
                sums = add_partials([self.views[key] for key in keys], list(res), self.core, f"chip_sum_{keys[0][0]}{keys[0][1]}")
                self.partial.update(zip(keys, sums))
            elif what == "chips":
                for k, c in zip(keys, res):
                    self.contrib[k] = c
            else:
                self.small_gathered[keys], = res


SMALL = ("norm1_g", "conv_a_w", "conv_a_b", "lru_wx", "lru_bx", "lru_wa", "lru_ba", "lru_lambda", "conv_b_w", "sinks",
         "conv_d_w", "conv_d_b", "ln_d_g", "ln_d_b", "norm2_g", "final_g")
WEIGHTS = ("norm1_g", "w_in", "conv_a_w", "conv_a_b", "lru_wx", "lru_bx", "lru_wa", "lru_ba", "lru_lambda", "w_a_out",
           "conv_b_w", "w_b_out", "sinks", "w_c_out", "conv_d_w", "conv_d_b", "ln_d_g", "ln_d_b", "w_d_out", "w_o",
           "norm2_g", "w_ffn_gate", "w_ffn_up", "w_ffn_down", "final_g")


def kernel(x, norm1_g, w_in, conv_a_w, conv_a_b, lru_wx, lru_bx, lru_wa, lru_ba, lru_lambda, w_a_out, conv_b_w, w_b_out, sinks, w_c_out, conv_d_w, conv_d_b, ln_d_g, ln_d_b, w_d_out, w_o, norm2_g, w_ffn_gate, w_ffn_up, w_ffn_down, final_g, loss_target, m_norm1_g, m_w_in, m_conv_a_w, m_conv_a_b, m_lru_wx, m_lru_bx, m_lru_wa, m_lru_ba, m_lru_lambda, m_w_a_out, m_conv_b_w, m_w_b_out, m_sinks, m_w_c_out, m_conv_d_w, m_conv_d_b, m_ln_d_g, m_ln_d_b, m_w_d_out, m_w_o, m_norm2_g, m_w_ffn_gate, m_w_ffn_up, m_w_ffn_down, m_final_g, v_norm1_g, v_w_in, v_conv_a_w, v_conv_a_b, v_lru_wx, v_lru_bx, v_lru_wa, v_lru_ba, v_lru_lambda, v_w_a_out, v_conv_b_w, v_w_b_out, v_sinks, v_w_c_out, v_conv_d_w, v_conv_d_b, v_ln_d_g, v_ln_d_b, v_w_d_out, v_w_o, v_norm2_g, v_w_ffn_gate, v_w_ffn_up, v_w_ffn_down, v_final_g):
    args = dict(locals())
    w = {n: args[n] for n in WEIGHTS}
    m = {n: args["m_" + n] for n in WEIGHTS}
    v = {n: args["v_" + n] for n in WEIGHTS}
    me = _dev_index(*_mesh_pos())

    def rows_major(a, how):
        return jnp.swapaxes(a, 1, 2) if how == "view" else a

    stacked = {k: cast_transpose(w[n], "prep_" + k) if how == "transpose" else rows_major(w[n], how).astype(BF16)
               for k, (n, how) in BIG.items()}
    plan = Overlap([{k: stacked[k][l] for k in BIG} for l in range(DEPTH)], lax.axis_index("c").astype(jnp.int32).reshape(1))
    convs = jnp.pad(_stack_convs(w).reshape(DEPTH * CW_ROWS, BW // NDEV), ((0, 0), (0, 256 - BW // NDEV)))
    g_in0, g_conv = _comm_only(gather_job([(("in_t", 0), plan.shards[0]["in_t"], None, 0, plan.shards[0]["in_t"].shape[0]),
                                           ("convs", convs, None, 0, convs.shape[0])]), "gather_first")
    plan.gathered[0]["in_t"] = g_in0
    convw = g_conv[:, :BW // NDEV].reshape(NDEV, DEPTH, CW_ROWS, BW // NDEV).transpose(1, 2, 0, 3).reshape(DEPTH, CW_ROWS, BW)

    vecs = _stack_vecs(w)
    head_stats, grad_x, grads = local_step(x[0], loss_target[0], norm1_g, norm2_g, final_g, convw, vecs, lru_wx, lru_wa, plan)


    out = {}
    for k in ("down", "gate_t", "up_t", "o", "a_t", "b_t", "c_t", "d_t", "in_t"):
        n, how = BIG[k]
        res, cres = adamw_big(plan.contrib[k], rows_major(w[n], how), rows_major(m[n], how), rows_major(v[n], how),
                              how == "transpose", "adamw_" + k, comm=plan.job("adamw_" + k, 0))
        plan.done("adamw_" + k, 0, cres)
        out[n] = [rows_major(r, how) for r in res]

    def own_shapes(p):
        return {n: p[n].reshape(1, D) if n == "final_g" else p[n] for n in SMALL}

    loss, small = adamw_small([g.reshape(NDEV, P_ROWS, D) for g in plan.small_gathered], me.astype(jnp.int32).reshape(1),
                              own_shapes(w), own_shapes(m), own_shapes(v))
    for n in SMALL:
        out[n] = [r.reshape(w[n].shape) for r in small[n]]
    loss = loss[0, 0]
    return (loss, grad_x[None], *[out[n][0] for n in WEIGHTS], *[out[n][1] for n in WEIGHTS],
            *[out[n][2] for n in WEIGHTS], *[out[n][3] for n in WEIGHTS])
```

```python
import functools

import jax
import jax.numpy as jnp
from jax import lax
from jax.experimental import pallas as pl
from jax.experimental.pallas import tpu as pltpu

F32 = jnp.float32
BF16 = jnp.bfloat16
E = pl.Element

D = 1024
BW = 512
IN_W = 8448
GL0 = 4352
FF = 2816
N_HEADS = 8
N_KV = 2
HD = 64
ATT_BLK = 128
EPS = 1e-6
LRU_C = 8.0
NEG_INF = -1e30
DEPTH = 2
NDEV = 8
CONV_A, CONV_B, CONV_D = 4, 3, 31
C_AX, C_AG, C_BV, C_BC, C_BB, C_Q, C_K, C_V, C_D1, C_D2 = 0, 512, 1024, 1536, 2048, 2560, 3072, 3200, 3328, 3840
CW_A, CW_B, CW_D, CW_ROWS = 0, 4, 8, 40
V_CAB, V_BX, V_BA, V_LAM, V_CDB, V_LNG, V_LNB, V_SINK, V_ROWS = 0, 1, 2, 3, 4, 5, 6, 7, 8
HALO = 32
W_IN_PARTS = ((0, 768), (768, 256))

ADAM_LR, ADAM_B1, ADAM_B2, ADAM_EPS, ADAM_WD, ADAM_STEP = 0.001, 0.9, 0.999, 1e-08, 0.01, 10

VMEM_LIMIT = 56 * 1024 * 1024

_NN = (((1,), (0,)), ((), ()))
_NT = (((1,), (1,)), ((), ()))
_TN = (((0,), (0,)), ((), ()))


def _dot(a, b, dims):
    return lax.dot_general(a.astype(BF16), b.astype(BF16), dims, preferred_element_type=F32)


def _cparams(n_axes):
    return pltpu.CompilerParams(dimension_semantics=("arbitrary",) * n_axes, vmem_limit_bytes=VMEM_LIMIT)


def _sds(shape, dtype):
    return jax.ShapeDtypeStruct(tuple(shape), dtype)


def _sigmoid(x):
    return jax.nn.sigmoid(x)


def _neg_expm1(x):
    p = x * (1.0 + x * (0.5 + x * (1.0 / 6.0 + x * (1.0 / 24.0 + x * (1.0 / 120.0)))))
    return jnp.where(x > -0.1, -p, 1.0 - jnp.exp(x))


def _softplus(z):
    return jnp.maximum(z, 0.0) + jnp.log1p(jnp.exp(-jnp.abs(z)))


def _gelu_and_grad(x):
    c = 0.7978845608028654
    inner = c * (x + 0.044715 * x * x * x)
    t = jnp.tanh(inner)
    g = 0.5 * x * (1.0 + t)
    dg = 0.5 * (1.0 + t) + 0.5 * x * (1.0 - t * t) * c * (1.0 + 3.0 * 0.044715 * x * x)
    return g, dg


ANY = pl.BlockSpec(memory_space=pl.ANY)
MESH = pl.DeviceIdType.MESH


def _mesh_pos():
    return lax.axis_index("x"), lax.axis_index("y"), lax.axis_index("c")


def _dev_index(px, py, pc):
    return 4 * px + 2 * py + pc


class CommJob:
    def __init__(self, inputs, aliases, out_shapes, sem_shapes, start, finish, relay=None):
        self.inputs, self.aliases, self.out_shapes, self.sem_shapes = list(inputs), dict(aliases), list(out_shapes), list(sem_shapes)
        self.start, self.finish, self.relay = start, finish, relay


def _call(body, comm, args, *, grid, in_specs, out_specs, out_shape, scratch_shapes=(), name, aliases=None):
    single = not isinstance(out_shape, (list, tuple))
    out_specs = [out_specs] if single else list(out_specs)
    out_shape = [out_shape] if single else list(out_shape)
    scratch_shapes = list(scratch_shapes)
    n_in, n_out, n_scr, n_axes = len(in_specs), len(out_shape), len(scratch_shapes), len(grid)
    params = pltpu.CompilerParams(dimension_semantics=("arbitrary",) * n_axes, vmem_limit_bytes=VMEM_LIMIT)
    io_aliases = dict(aliases or {})
    if comm is None:
        outs = pl.pallas_call(body, grid=grid, in_specs=in_specs, out_specs=out_specs, out_shape=out_shape,
                              scratch_shapes=scratch_shapes, input_output_aliases=io_aliases, compiler_params=params,
                              name=name)(*args)
        return (outs[0] if single else outs), []
    c_in, c_out = len(comm.inputs), len(comm.out_shapes)
    io_aliases.update({n_in + i: n_out + o for i, o in comm.aliases.items()})

    def wrapped(*refs):
        ins, cins = refs[:n_in], refs[n_in:n_in + c_in]
        outs = refs[n_in + c_in:n_in + c_in + n_out]
        couts = refs[n_in + c_in + n_out:n_in + c_in + n_out + c_out]
        rest = refs[n_in + c_in + n_out + c_out:]
        scr, sems = rest[:n_scr], rest[n_scr:]
        first = functools.reduce(lambda a, b: a & b, [pl.program_id(a) == 0 for a in range(n_axes)])
        last = functools.reduce(lambda a, b: a & b, [pl.program_id(a) == pl.num_programs(a) - 1 for a in range(n_axes)])

        @pl.when(first)
        def _():
            comm.start(cins, couts, sems)

        if comm.relay is not None:
            step = functools.reduce(lambda a, b: a * grid[b] + pl.program_id(b), range(1, n_axes), pl.program_id(0))
            n_steps = functools.reduce(lambda a, b: a * b, grid)

            @pl.when(step == 2 * n_steps // 3)
            def _():
                comm.relay(cins, couts, sems)

        body(*ins, *outs, *scr)

        @pl.when(last)
        def _():
            comm.finish(cins, couts, sems)

    outs = pl.pallas_call(
        wrapped, grid=grid, in_specs=list(in_specs) + [ANY] * c_in, out_specs=out_specs + [ANY] * c_out,
        out_shape=out_shape + comm.out_shapes, scratch_shapes=scratch_shapes + comm.sem_shapes,
        input_output_aliases=io_aliases, compiler_params=params, name=name)(*args, *comm.inputs)
    res, cres = outs[:n_out], outs[n_out:]
    return (res[0] if single else res), cres


def _comm_only(comm, name):
    c_in, c_out = len(comm.inputs), len(comm.out_shapes)

    def body(*refs):
        cins, couts, sems = refs[:c_in], refs[c_in:c_in + c_out], refs[c_in + c_out:]
        comm.start(cins, couts, sems)
        if comm.relay is not None:
            comm.relay(cins, couts, sems)
        comm.finish(cins, couts, sems)

    return pl.pallas_call(body, in_specs=[ANY] * c_in, out_specs=[ANY] * c_out, out_shape=comm.out_shapes,
                          scratch_shapes=comm.sem_shapes, input_output_aliases=comm.aliases, name=name)(*comm.inputs)


def gather_job(pieces):
    inputs, aliases, out_shapes, plan, where = [], {}, [], [], {}
    for key, shard, gathered, row0, nrows in pieces:
        if key not in where:
            where[key] = (len(inputs), len(out_shapes))
            inputs.append(shard)
            if gathered is not None:
                aliases[len(inputs)] = len(out_shapes)
                inputs.append(gathered)
            out_shapes.append(_sds((NDEV * shard.shape[0], shard.shape[1]), shard.dtype))
        plan.append((*where[key], shard.shape[0], row0, nrows))
    n = len(plan)

    def copies(cins, couts, sems):
        send_sems, recv_sems, local_sems = sems
        x, y, c = _mesh_pos()
        me, sibling = (x, y, c), (x, y, 1 - c)
        xn, yn, dg = (1 - x, y), (x, 1 - y), (1 - x, 1 - y)
        local, first, pass1, pass2, got_ici, got_fwd, got_d2d = [], [], [], [], [], [], []
        for p, (i_shard, i_out, rows, row0, nrows) in enumerate(plan):
            src = cins[i_shard].at[pl.ds(row0, nrows), :]
            half = cins[i_shard].shape[1] // 2
            left, right, whole = pl.ds(0, half), pl.ds(half, half), slice(None)

            def slot(dev, lanes, i_out=i_out, rows=rows, row0=row0, nrows=nrows):
                return couts[i_out].at[pl.ds(_dev_index(*dev) * rows + row0, nrows), lanes]

            def copy(g, dev, to, lanes=whole, src=None, p=p, slot=slot):
                return pltpu.make_async_remote_copy(
                    src_ref=slot(dev, lanes) if src is None else src, dst_ref=slot(dev, lanes),
                    send_sem=send_sems.at[g, p], recv_sem=recv_sems.at[g, p], device_id=to, device_id_type=MESH)

            local.append(pltpu.make_async_copy(src, slot(me, whole), local_sems.at[p]))
            first += [copy(0, me, sibling, src=src), copy(1, me, (*xn, c), src=src), copy(2, me, (*yn, c), src=src)]
            got_ici += [copy(1, (*xn, c), me), copy(2, (*yn, c), me)]
            pass1 += [copy(3, (*xn, c), (*yn, c), left), copy(4, (*yn, c), (*xn, c), right),
                      copy(5, (*xn, c), sibling), copy(6, (*yn, c), sibling)]
            got_fwd += [copy(3, (*dg, c), me, left), copy(4, (*dg, c), me, right)]
            pass2 += [copy(7, (*dg, c), sibling, left), copy(8, (*dg, c), sibling, right)]
            got_d2d += [copy(0, sibling, me), copy(5, (*xn, 1 - c), me), copy(6, (*yn, 1 - c), me),
                        copy(7, (*dg, 1 - c), me, left), copy(8, (*dg, 1 - c), me, right)]
        return local, first, pass1, pass2, got_ici, got_fwd, got_d2d

    def start(cins, couts, sems):
        local, first, *_ = copies(cins, couts, sems)
        for cp in local + first:
            cp.start()

    def pass_on(cins, couts, sems):
        _, _, pass1, _, got_ici, _, _ = copies(cins, couts, sems)
        for cp in got_ici:
            cp.wait_recv()
        for cp in pass1:
            cp.start()

    def finish(cins, couts, sems):
        local, first, pass1, pass2, _, got_fwd, got_d2d = copies(cins, couts, sems)
        for cp in got_fwd:
            cp.wait_recv()
        for cp in pass2:
            cp.start()
        for cp in got_d2d:
            cp.wait_recv()
        for cp in first + pass1 + pass2:
            cp.wait_send()
        for cp in local:
            cp.wait()

    sem_shapes = [pltpu.SemaphoreType.DMA((9, n)), pltpu.SemaphoreType.DMA((9, n)), pltpu.SemaphoreType.DMA((n,))]
    return CommJob(inputs, aliases, out_shapes, sem_shapes, start, finish, relay=pass_on)


def sibling_exchange_job(grads):
    n = len(grads)

    def copies(cins, couts, sems):
        send_sems, recv_sems = sems
        x, y, c = _mesh_pos()
        return [pltpu.make_async_remote_copy(
            src_ref=cins[q].at[:, 1 - c], dst_ref=couts[q], send_sem=send_sems.at[q], recv_sem=recv_sems.at[q],
            device_id=(x, y, 1 - c), device_id_type=MESH) for q in range(n)]

    def start(cins, couts, sems):
        for cp in copies(cins, couts, sems):
            cp.start()

    def finish(cins, couts, sems):
        cps = copies(cins, couts, sems)
        for cp in cps:
            cp.wait_recv()
        for cp in cps:
            cp.wait_send()

    return CommJob(grads, {}, [_sds((4,) + g.shape[2:], g.dtype) for g in grads],
                   [pltpu.SemaphoreType.DMA((n,)), pltpu.SemaphoreType.DMA((n,))], start, finish)


def chip_exchange_job(pieces):
    inputs, aliases, out_shapes, plan, where = [], {}, [], [], {}
    for partial, contrib, key, layer, row0, nrows, col0, cols in pieces:
        if key not in where:
            where[key] = len(out_shapes)
            out_shapes.append(_sds((4, DEPTH, partial.shape[1], cols), partial.dtype))
            if contrib is not None:
                aliases[len(inputs)] = where[key]
                inputs.append(contrib)
        plan.append((len(inputs), where[key], layer, row0, nrows, col0, partial.shape[2]))
        inputs.append(partial)
    n = len(plan)

    def copies(cins, couts, sems):
        send_sems, recv_sems, local_sems = sems
        x, y, c = _mesh_pos()
        mine = 2 * x + y
        local, sends, recvs = [], [], []
        for p, (i_in, i_out, layer, row0, nrows, col0, ncols) in enumerate(plan):
            rows, lanes = pl.ds(row0, nrows), pl.ds(col0, ncols)
            local.append(pltpu.make_async_copy(cins[i_in].at[mine, rows, :], couts[i_out].at[mine, layer, rows, lanes],
                                               local_sems.at[p]))
            for j, (cx, cy) in enumerate([(1 - x, y), (x, 1 - y), (1 - x, 1 - y)]):
                theirs = 2 * cx + cy

                def copy(slot_there, j=j, p=p, cx=cx, cy=cy, theirs=theirs, i_in=i_in, i_out=i_out, layer=layer,
                         rows=rows, lanes=lanes):
                    return pltpu.make_async_remote_copy(
                        src_ref=cins[i_in].at[theirs, rows, :], dst_ref=couts[i_out].at[slot_there, layer, rows, lanes],
                        send_sem=send_sems.at[j, p], recv_sem=recv_sems.at[j, p], device_id=(cx, cy, c), device_id_type=MESH)
                sends.append(copy(mine))
                recvs.append(copy(theirs))
        return local, sends, recvs

    def start(cins, couts, sems):
        local, sends, _ = copies(cins, couts, sems)
        for cp in local + sends:
            cp.start()

    def finish(cins, couts, sems):
        local, sends, recvs = copies(cins, couts, sems)
        for cp in recvs:
            cp.wait_recv()
        for cp in sends:
            cp.wait_send()
        for cp in local:
            cp.wait()

    sem_shapes = [pltpu.SemaphoreType.DMA((3, n)), pltpu.SemaphoreType.DMA((3, n)), pltpu.SemaphoreType.DMA((n,))]
    return CommJob(inputs, aliases, out_shapes, sem_shapes, start, finish)


def fwd_proj(x, g1, wt_in, l, comm=None):
    s = x.shape[0]
    tm = min(512, s)
    tn = 1408

    def body(x_ref, g_ref, w_ref, o_ref, xn_ref):
        @pl.when(pl.program_id(1) == 0)
        def _():
            xv = x_ref[...]
            r = lax.rsqrt(jnp.mean(xv * xv, axis=-1, keepdims=True) + EPS)
            xn_ref[...] = (xv * r * g_ref[l:l + 1, :]).astype(BF16)

        o_ref[...] = _dot(xn_ref[...], w_ref[...], _NT).astype(BF16)

    return _call(
        body, comm, (x, g1, wt_in), grid=(s // tm, IN_W // tn),
        in_specs=[pl.BlockSpec((tm, D), lambda i, j: (i, 0)),
                  pl.BlockSpec((DEPTH, D), lambda i, j: (0, 0)),
                  pl.BlockSpec((tn, D), lambda i, j: (j, 0))],
        out_specs=pl.BlockSpec((tm, tn), lambda i, j: (i, j)),
        out_shape=_sds((s, IN_W), BF16),
        scratch_shapes=[pltpu.VMEM((tm, D), BF16)], name=f"fwd_proj{l}")


def _scan_fwd(a_ref, u_ref, h_ref, h0, n_rows):
    row = lax.broadcasted_iota(jnp.int32, (8, BW), 0)

    def body(g, hprev):
        r = pl.multiple_of(g * 8, 8)
        a = a_ref[pl.ds(r, 8), :]
        u = u_ref[pl.ds(r, 8), :]
        for sft in (1, 2, 4):
            a_sh = jnp.where(row >= sft, pltpu.roll(a, sft, 0), 1.0)
            u_sh = jnp.where(row >= sft, pltpu.roll(u, sft, 0), 0.0)
            u = u + a * u_sh
            a = a * a_sh
        h = u + a * hprev
        h_ref[pl.ds(r, 8), :] = h
        return h[7:8, :]

    return lax.fori_loop(0, n_rows // 8, body, h0)


def _scan_bwd(b_ref, g_ref, o_ref, c0, n_rows):
    row = lax.broadcasted_iota(jnp.int32, (8, BW), 0)

    def body(k, cnext):
        r = pl.multiple_of((n_rows // 8 - 1 - k) * 8, 8)
        b = b_ref[pl.ds(r, 8), :]
        g = g_ref[pl.ds(r, 8), :]
        for sft in (1, 2, 4):
            b_sh = jnp.where(row < 8 - sft, pltpu.roll(b, 8 - sft, 0), 1.0)
            g_sh = jnp.where(row < 8 - sft, pltpu.roll(g, 8 - sft, 0), 0.0)
            g = g + b * g_sh
            b = b * b_sh
        o = g + b * cnext
        o_ref[pl.ds(r, 8), :] = o
        return o[0:1, :]

    return lax.fori_loop(0, n_rows // 8, body, c0)


def _shifted_copies(buf, shifted, n_rows):
    for r in range(1, 8):
        shifted[r - 1, 0:n_rows - 8, :] = buf[pl.ds(r, n_rows - 8), :]


def _window(buf, shifted, off, t):
    r = off % 8
    return buf[pl.ds(off, t), :] if r == 0 else shifted[r - 1, pl.ds(off - r, t), :]


def _branch_fwd_math(cur_ref, halo_ref, cw_ref, vec_ref, wx_ref, wa_ref, bufa, bufb, bufd, xd, first, t, saved_ref=None):
    def halo(c0):
        v = halo_ref[:, c0:c0 + BW].astype(F32)
        return jnp.where(first, 0.0, v)

    def cur(c0):
        return cur_ref[:, c0:c0 + BW].astype(F32)

    out = {}
    bufa[0:HALO, :] = halo(C_AX)
    bufa[HALO:HALO + t, :] = cur(C_AX)
    ca = jnp.zeros((t, BW), F32) + vec_ref[V_CAB:V_CAB + 1, :]
    for k in range(CONV_A):
        ca = ca + cw_ref[CW_A + k:CW_A + k + 1, :] * bufa[pl.ds(HALO - (CONV_A - 1) + k, t), :]
    if saved_ref is None:
        gi = _sigmoid(_dot(ca, wx_ref[...], _NN) + vec_ref[V_BX:V_BX + 1, :])
        gr = _sigmoid(_dot(ca, wa_ref[...], _NN) + vec_ref[V_BA:V_BA + 1, :])
    else:
        gi, gr = saved_ref[:, BW:2 * BW], saved_ref[:, 2 * BW:3 * BW]
    sp = _softplus(-vec_ref[V_LAM:V_LAM + 1, :])
    la = -LRU_C * sp * gr
    a = jnp.exp(la)
    mult = jnp.sqrt(_neg_expm1(2.0 * la))
    out.update(ca=ca, gi=gi, gr=gr, sp=sp, a=a, mult=mult)
    bufb[0:HALO, :] = halo(C_BC) * halo(C_BV)
    bufb[HALO:HALO + t, :] = cur(C_BC) * cur(C_BV)
    cb = jnp.zeros((t, BW), F32)
    for k in range(CONV_B):
        cb = cb + cw_ref[CW_B + k:CW_B + k + 1, :] * bufb[pl.ds(HALO - (CONV_B - 1) + k, t), :]
    out.update(cb=cb)
    bufd[0:HALO, :] = halo(C_D1) * _sigmoid(halo(C_D2))
    s2 = _sigmoid(cur(C_D2))
    bufd[HALO:HALO + t, :] = cur(C_D1) * s2
    _shifted_copies(bufd, xd, t + HALO)
    if saved_ref is None:
        cd = jnp.zeros((t, BW), F32) + vec_ref[V_CDB:V_CDB + 1, :]
        for k in range(CONV_D):
            cd = cd + cw_ref[CW_D + k:CW_D + k + 1, :] * _window(bufd, xd, HALO - (CONV_D - 1) + k, t)
    else:
        cd = saved_ref[:, 0:BW]
    mu = jnp.mean(cd, axis=-1, keepdims=True)
    xc = cd - mu
    rstd = lax.rsqrt(jnp.mean(xc * xc, axis=-1, keepdims=True) + EPS)
    xh = xc * rstd
    ln = xh * vec_ref[V_LNG:V_LNG + 1, :] + vec_ref[V_LNB:V_LNB + 1, :]
    out.update(s2=s2, xh=xh, rstd=rstd, ln=ln, cd=cd)
    return out


def fwd_branch(proj, convw, vecs, wx_bd, wa_bd, l, comm=None):
    s = proj.shape[0]
    t = min(256, s)

    def body(cur_ref, halo_ref, cw_ref, vec_ref, wx_ref, wa_ref, pre_ref, h_ref, sv_ref, bufa, bufb, bufd, xd, a_s, u_s, hcar):
        first = pl.program_id(0) == 0

        @pl.when(first)
        def _():
            hcar[...] = jnp.zeros((1, BW), F32)

        v = _branch_fwd_math(cur_ref, halo_ref, cw_ref, vec_ref, wx_ref, wa_ref, bufa, bufb, bufd, xd, first, t)
        a_s[...] = v["a"]
        u_s[...] = v["ca"] * v["gi"] * v["mult"]
        sv_ref[:, 0:BW] = v["cd"]
        sv_ref[:, BW:2 * BW] = v["gi"]
        sv_ref[:, 2 * BW:3 * BW] = v["gr"]
        hcar[...] = _scan_fwd(a_s, u_s, h_ref, hcar[...], t)
        gg, _ = _gelu_and_grad(cur_ref[:, C_AG:C_AG + BW].astype(F32))
        pre_ref[:, 0:BW] = (h_ref[...] * gg).astype(BF16)
        pre_ref[:, BW:2 * BW] = (cur_ref[:, C_BB:C_BB + BW].astype(F32) * v["cb"]).astype(BF16)
        ln = v["ln"]
        pre_ref[:, 2 * BW:3 * BW] = (ln * _sigmoid(ln)).astype(BF16)

    hb = t // HALO
    return _call(
        body, comm, (proj, proj, convw, vecs, wx_bd, wa_bd), grid=(s // t,),
        in_specs=[pl.BlockSpec((t, GL0), lambda i: (i, 0)),
                  pl.BlockSpec((HALO, GL0), lambda i: (jnp.maximum(i * hb - 1, 0), 0)),
                  pl.BlockSpec((None, CW_ROWS, BW), lambda i: (l, 0, 0)),
                  pl.BlockSpec((None, V_ROWS, BW), lambda i: (l, 0, 0)),
                  pl.BlockSpec((None, BW, BW), lambda i: (l, 0, 0)),
                  pl.BlockSpec((None, BW, BW), lambda i: (l, 0, 0))],
        out_specs=[pl.BlockSpec((t, 3 * BW), lambda i: (i, 0)), pl.BlockSpec((t, BW), lambda i: (i, 0)),
                   pl.BlockSpec((t, 3 * BW), lambda i: (i, 0))],
        out_shape=[_sds((s, 3 * BW), BF16), _sds((s, BW), F32), _sds((s, 3 * BW), F32)],
        scratch_shapes=[pltpu.VMEM((t + HALO, BW), F32)] * 3 + [pltpu.VMEM((7, t + HALO - 8, BW), F32)]
        + [pltpu.VMEM((t, BW), F32)] * 2 + [pltpu.VMEM((1, BW), F32)],
        name=f"fwd_branch{l}")


GRP = N_HEADS // N_KV


ATT_SUB = 2


def _attn_mask_bias(first_block):
    shape = (GRP * ATT_BLK, 2 * ATT_BLK)
    qi = lax.broadcasted_iota(jnp.int32, shape, 0) & (ATT_BLK - 1)
    ki = lax.broadcasted_iota(jnp.int32, shape, 1)
    dist = qi + ATT_BLK - ki
    valid = (dist >= 0) & (dist < ATT_BLK)
    if first_block is not None:
        valid = valid & (jnp.logical_not(first_block) | (ki >= ATT_BLK))
    return dist.astype(F32), valid


def _attn_units(q_ref, kvp_ref, kvc_ref, first_step):
    units = []
    for b in range(ATT_SUB):
        rows = slice(b * ATT_BLK, (b + 1) * ATT_BLK)
        if b == 0:
            prev = lambda c0, c1: kvp_ref[:, c0:c1]
        else:
            prev = lambda c0, c1, b=b: kvc_ref[(b - 1) * ATT_BLK:b * ATT_BLK, c0:c1]
        for hk in range(N_KV):
            units.append(dict(b=b, hk=hk, rows=rows, q=lambda c0, c1, rows=rows: q_ref[rows, c0:c1], prev=prev,
                              cur=lambda c0, c1, rows=rows: kvc_ref[rows, c0:c1], first=first_step if b == 0 else None))
    return units


def _per_head(hk, values):
    hl = lax.broadcasted_iota(jnp.int32, (GRP * ATT_BLK, 1), 0) // ATT_BLK
    out = values[GRP - 1]
    for j in range(GRP - 2, -1, -1):
        out = jnp.where(hl == j, values[j], out)
    return out


def _attn_probs(units, vec_ref):
    us = range(len(units))
    heads = [range(u["hk"] * GRP, (u["hk"] + 1) * GRP) for u in units]
    masks = {id(u["first"]): _attn_mask_bias(u["first"]) for u in units}
    distf = [masks[id(u["first"])][0] for u in units]
    valid = [masks[id(u["first"])][1] for u in units]
    q4 = [jnp.concatenate([units[i]["q"](h * HD, (h + 1) * HD) for h in heads[i]], axis=0) for i in us]
    kcol = [(u["hk"] * HD, (u["hk"] + 1) * HD) for u in units]
    vcol = [((N_KV + u["hk"]) * HD, (N_KV + u["hk"] + 1) * HD) for u in units]
    k2 = [jnp.concatenate([units[i]["prev"](*kcol[i]), units[i]["cur"](*kcol[i])], axis=0) for i in us]
    v2 = [jnp.concatenate([units[i]["prev"](*vcol[i]), units[i]["cur"](*vcol[i])], axis=0) for i in us]
    slope = [_per_head(units[i]["hk"], [2.0 ** (-8.0 * (h + 1) / N_HEADS) for h in heads[i]]) for i in us]
    sink = [_per_head(units[i]["hk"], [vec_ref[V_SINK:V_SINK + 1, h:h + 1] for h in heads[i]]) for i in us]
    sc = [_dot(q4[i], k2[i], _NT) for i in us]
    sc = [jnp.where(valid[i], sc[i] * (HD ** -0.5) - slope[i] * distf[i], NEG_INF) for i in us]
    m = [jnp.maximum(jnp.max(sc[i], axis=-1, keepdims=True), sink[i]) for i in us]
    p = [jnp.exp(sc[i] - m[i]) for i in us]
    es = [jnp.exp(sink[i] - m[i]) for i in us]
    inv = [1.0 / (jnp.sum(p[i], axis=-1, keepdims=True) + es[i]) for i in us]
    return [(q4[i], k2[i], v2[i], p[i] * inv[i], es[i] * inv[i]) for i in us]


def fwd_attn(proj, vecs, l, comm=None):
    s = proj.shape[0]
    t = ATT_SUB * ATT_BLK

    def body(q_ref, kvp_ref, kvc_ref, vec_ref, o_ref):
        units = _attn_units(q_ref, kvp_ref, kvc_ref, pl.program_id(0) == 0)
        groups = _attn_probs(units, vec_ref)
        outs = [_dot(p, v2, _NN).astype(BF16) for _, _, v2, p, _ in groups]
        for u, out in zip(units, outs):
            for j in range(GRP):
                h = u["hk"] * GRP + j
                o_ref[u["rows"], h * HD:(h + 1) * HD] = out[j * ATT_BLK:(j + 1) * ATT_BLK]

    return _call(
        body, comm, (proj, proj, proj, vecs), grid=(s // t,),
        in_specs=[pl.BlockSpec((t, BW), lambda i: (i, C_Q // BW)),
                  pl.BlockSpec((ATT_BLK, 256), lambda i: (jnp.maximum(ATT_SUB * i - 1, 0), C_K // 256)),
                  pl.BlockSpec((t, 256), lambda i: (i, C_K // 256)),
                  pl.BlockSpec((None, V_ROWS, BW), lambda i: (l, 0, 0))],
        out_specs=pl.BlockSpec((t, BW), lambda i: (i, 0)),
        out_shape=_sds((s, BW), BF16), name=f"fwd_attn{l}")


def fwd_merge(x, proj, pre_abd, pre_c, wt_a, wt_b, wt_c, wt_d, w_o, l, comm=None):
    s = x.shape[0]
    tm = min(256, s)

    def body(x_ref, gl_ref, pabd_ref, pc_ref, wa_ref, wb_ref, wc_ref, wd_ref, wo_ref, y_ref, mg_ref, h1_ref):
        pres = (pabd_ref[:, 0:BW], pabd_ref[:, BW:2 * BW], pc_ref[...], pabd_ref[:, 2 * BW:3 * BW])
        merged = jnp.zeros((tm, D), F32)
        for k, (pre, w_ref) in enumerate(zip(pres, (wa_ref, wb_ref, wc_ref, wd_ref))):
            yk = _dot(pre, w_ref[...], _NT)
            y_ref[:, k * D:(k + 1) * D] = yk.astype(BF16)
            merged = merged + _sigmoid(gl_ref[:, k * D:(k + 1) * D].astype(F32)) * yk
        mg_ref[...] = merged.astype(BF16)
        h1_ref[...] = x_ref[...] + _dot(merged, wo_ref[...], _NN)

    wspec = pl.BlockSpec((D, BW), lambda i: (0, 0))
    return _call(
        body, comm, (x, proj, pre_abd, pre_c, wt_a, wt_b, wt_c, wt_d, w_o), grid=(s // tm,),
        in_specs=[pl.BlockSpec((tm, D), lambda i: (i, 0)),
                  pl.BlockSpec((E(tm), E(4 * D)), lambda i: (i * tm, GL0)),
                  pl.BlockSpec((tm, 3 * BW), lambda i: (i, 0)),
                  pl.BlockSpec((tm, BW), lambda i: (i, 0)),
                  wspec, wspec, wspec, wspec,
                  pl.BlockSpec((D, D), lambda i: (0, 0))],
        out_specs=[pl.BlockSpec((tm, 4 * D), lambda i: (i, 0)), pl.BlockSpec((tm, D), lambda i: (i, 0)),
                   pl.BlockSpec((tm, D), lambda i: (i, 0))],
        out_shape=[_sds((s, 4 * D), BF16), _sds((s, D), BF16), _sds((s, D), F32)], name=f"fwd_merge{l}")


def fwd_ffn(h1, g2, wt_gate, wt_up, w_down, l, comm=None):
    s = h1.shape[0]
    tm = min(512, s)
    fc = FF // 2

    def body(h_ref, g_ref, wg_ref, wu_ref, wd_ref, xo_ref, fg_ref, fu_ref, hn_ref, acc_ref):
        j = pl.program_id(1)

        @pl.when(j == 0)
        def _():
            hv = h_ref[...]
            r = lax.rsqrt(jnp.mean(hv * hv, axis=-1, keepdims=True) + EPS)
            hn_ref[...] = (hv * r * g_ref[l:l + 1, :]).astype(BF16)
            acc_ref[...] = hv

        fg = _dot(hn_ref[...], wg_ref[...], _NT)
        fu = _dot(hn_ref[...], wu_ref[...], _NT)
        fg_ref[...] = fg.astype(BF16)
        fu_ref[...] = fu.astype(BF16)
        acc_ref[...] += _dot(fg * _sigmoid(fg) * fu, wd_ref[...], _NN)

        @pl.when(j == pl.num_programs(1) - 1)
        def _():
            xo_ref[...] = acc_ref[...]

    wspec = pl.BlockSpec((fc, D), lambda i, j: (j, 0))
    return _call(
        body, comm, (h1, g2, wt_gate, wt_up, w_down), grid=(s // tm, FF // fc),
        in_specs=[pl.BlockSpec((tm, D), lambda i, j: (i, 0)), pl.BlockSpec((DEPTH, D), lambda i, j: (0, 0)),
                  wspec, wspec, wspec],
        out_specs=[pl.BlockSpec((tm, D), lambda i, j: (i, 0)), pl.BlockSpec((tm, fc), lambda i, j: (i, j)),
                   pl.BlockSpec((tm, fc), lambda i, j: (i, j))],
        out_shape=[_sds((s, D), F32), _sds((s, FF), BF16), _sds((s, FF), BF16)],
        scratch_shapes=[pltpu.VMEM((tm, D), BF16), pltpu.VMEM((tm, D), F32)], name=f"fwd_ffn{l}")


def loss_head(x, gf, target):
    s = x.shape[0]
    tm = min(512, s)

    def body(x_ref, g_ref, t_ref, dx_ref, st_ref):
        @pl.when(pl.program_id(0) == 0)
        def _():
            st_ref[...] = jnp.zeros((8, D), F32)

        xv = x_ref[...]
        g = g_ref[...]
        r = lax.rsqrt(jnp.mean(xv * xv, axis=-1, keepdims=True) + EPS)
        n = xv * r
        err = n * g - t_ref[...]
        dy = err * (1.0 / D)
        dn = dy * g
        dx_ref[...] = r * (dn - n * jnp.mean(dn * n, axis=-1, keepdims=True))
        st_ref[0:1, :] += jnp.sum(dy * n, axis=0, keepdims=True)
        lsum = 0.5 * jnp.sum(jnp.mean(err * err, axis=-1, keepdims=True), axis=0, keepdims=True)
        st_ref[1:2, :] += jnp.broadcast_to(lsum, (1, D))

    return pl.pallas_call(
        body, grid=(s // tm,),
        in_specs=[pl.BlockSpec((tm, D), lambda i: (i, 0)), pl.BlockSpec((1, D), lambda i: (0, 0)),
                  pl.BlockSpec((tm, D), lambda i: (i, 0))],
        out_specs=[pl.BlockSpec((tm, D), lambda i: (i, 0)), pl.BlockSpec((8, D), lambda i: (0, 0))],
        out_shape=[_sds((s, D), F32), _sds((8, D), F32)],
        compiler_params=_cparams(1), name="loss_head")(x, gf, target)


def _edge_index(j, i, n_j, n_i):
    return jnp.where((j == 0) | (j == n_j - 1), i, n_i - 1)


def bwd_ffn(dxo, h1, fg, fu, g2, wt_gate, wt_up, w_down, l, comm=None):
    s = h1.shape[0]
    tm = min(512, s)
    fc = 256
    n_j, n_i = FF // fc, s // tm

    def body(dxo_ref, h_ref, fg_ref, fu_ref, g_ref, wg_ref, wu_ref, wd_ref,
             dh_ref, dwg_ref, dwu_ref, dwd_ref, st_ref, dhn, dxo_b, hn_b, ag, au, ad):
        j, i = pl.program_id(0), pl.program_id(1)
        rows = pl.ds(pl.multiple_of(i * tm, tm), tm)
        g = g_ref[l:l + 1, :]

        @pl.when(j == 0)
        def _():
            hv = h_ref[...]
            r = lax.rsqrt(jnp.mean(hv * hv, axis=-1, keepdims=True) + EPS)
            hn_b[rows, :] = (hv * r * g).astype(BF16)
            dxo_b[rows, :] = dxo_ref[...].astype(BF16)
            dhn[rows, :] = jnp.zeros((tm, D), F32)

        @pl.when((j == 0) & (i == 0))
        def _():
            st_ref[...] = jnp.zeros((8, D), F32)

        @pl.when(i == 0)
        def _():
            ag[...] = jnp.zeros((fc, D), F32)
            au[...] = jnp.zeros((fc, D), F32)
            ad[...] = jnp.zeros((fc, D), F32)

        fgv = fg_ref[...].astype(F32)
        fuv = fu_ref[...].astype(F32)
        sg = _sigmoid(fgv)
        sil = fgv * sg
        dxb = dxo_b[rows, :]
        hnb = hn_b[rows, :]
        d_act = _dot(dxb, wd_ref[...], _NT)
        ad[...] += _dot(sil * fuv, dxb, _TN)
        d_fg = (d_act * fuv * (sg * (1.0 + fgv * (1.0 - sg)))).astype(BF16)
        d_fu = (d_act * sil).astype(BF16)
        ag[...] += _dot(d_fg, hnb, _TN)
        au[...] += _dot(d_fu, hnb, _TN)
        dhn[rows, :] += _dot(d_fg, wg_ref[...], _NN) + _dot(d_fu, wu_ref[...], _NN)

        @pl.when(i == n_i - 1)
        def _():
            dwg_ref[...] = ag[...].astype(BF16)
            dwu_ref[...] = au[...].astype(BF16)
            dwd_ref[...] = ad[...].astype(BF16)

        @pl.when(j == n_j - 1)
        def _():
            hv = h_ref[...]
            r = lax.rsqrt(jnp.mean(hv * hv, axis=-1, keepdims=True) + EPS)
            n = hv * r
            dv = dhn[rows, :]
            dn = dv * g
            dh_ref[...] = dxo_ref[...] + r * (dn - n * jnp.mean(dn * n, axis=-1, keepdims=True))
            st_ref[0:1, :] += jnp.sum(dv * n, axis=0, keepdims=True)

    edge = lambda j, i: (_edge_index(j, i, n_j, n_i), 0)
    wspec = pl.BlockSpec((fc, D), lambda j, i: (j, 0))
    dwspec = pl.BlockSpec((fc, D), lambda j, i: (j, 0))
    return _call(
        body, comm, (dxo, h1, fg, fu, g2, wt_gate, wt_up, w_down), grid=(n_j, n_i),
        in_specs=[pl.BlockSpec((tm, D), edge),
                  pl.BlockSpec((tm, D), edge),
                  pl.BlockSpec((tm, fc), lambda j, i: (i, j)), pl.BlockSpec((tm, fc), lambda j, i: (i, j)),
                  pl.BlockSpec((DEPTH, D), lambda j, i: (0, 0)), wspec, wspec, wspec],
        out_specs=[pl.BlockSpec((tm, D), lambda j, i: (jnp.where(j == n_j - 1, i, 0), 0)),
                   dwspec, dwspec, dwspec, pl.BlockSpec((8, D), lambda j, i: (0, 0))],
        out_shape=[_sds((s, D), F32), _sds((FF, D), BF16), _sds((FF, D), BF16), _sds((FF, D), BF16), _sds((8, D), F32)],
        scratch_shapes=[pltpu.VMEM((s, D), F32), pltpu.VMEM((s, D), BF16), pltpu.VMEM((s, D), BF16),
                        pltpu.VMEM((fc, D), F32), pltpu.VMEM((fc, D), F32), pltpu.VMEM((fc, D), F32)],
        name=f"bwd_ffn{l}")


def bwd_merge(dh1, y4, proj, merged, pre_abd, pre_c, wt_a, wt_b, wt_c, wt_d, w_o, l, comm=None):
    s = dh1.shape[0]
    tm = min(256, s)
    n_i = s // tm

    def body(dh_ref, y_ref, gl_ref, mg_ref, pabd_ref, pc_ref, wa_ref, wb_ref, wc_ref, wd_ref, wo_ref,
             dgl_ref, dpre_ref, dwo_ref, dwa_ref, dwb_ref, dwc_ref, dwd_ref, ao, aa, ab, ac, ad):
        i = pl.program_id(0)
        accs = (aa, ab, ac, ad)

        @pl.when(i == 0)
        def _():
            ao[...] = jnp.zeros((D, D), F32)
            for acc in accs:
                acc[...] = jnp.zeros((D, BW), F32)

        dhb = dh_ref[...].astype(BF16)
        dmg = _dot(dhb, wo_ref[...], _NT)
        ao[...] += _dot(mg_ref[...], dhb, _TN)
        pres = (pabd_ref[:, 0:BW], pabd_ref[:, BW:2 * BW], pc_ref[...], pabd_ref[:, 2 * BW:3 * BW])
        for k, (pre, w_ref, acc) in enumerate(zip(pres, (wa_ref, wb_ref, wc_ref, wd_ref), accs)):
            gk = _sigmoid(gl_ref[:, k * D:(k + 1) * D].astype(F32))
            yk = y_ref[:, k * D:(k + 1) * D].astype(F32)
            dgl_ref[:, k * D:(k + 1) * D] = (dmg * yk * gk * (1.0 - gk)).astype(BF16)
            dyk = (dmg * gk).astype(BF16)
            dpre_ref[:, k * BW:(k + 1) * BW] = _dot(dyk, w_ref[...], _NN).astype(BF16)
            acc[...] += _dot(dyk, pre, _TN)

        @pl.when(i == n_i - 1)
        def _():
            dwo_ref[...] = ao[...].astype(BF16)
            for o_ref, acc in zip((dwa_ref, dwb_ref, dwc_ref, dwd_ref), accs):
                o_ref[...] = acc[...].astype(BF16)

    wspec = pl.BlockSpec((D, BW), lambda i: (0, 0))
    dwspec = pl.BlockSpec((D, BW), lambda i: (0, 0))
    return _call(
        body, comm, (dh1, y4, proj, merged, pre_abd, pre_c, wt_a, wt_b, wt_c, wt_d, w_o), grid=(n_i,),
        in_specs=[pl.BlockSpec((tm, D), lambda i: (i, 0)),
                  pl.BlockSpec((tm, 4 * D), lambda i: (i, 0)),
                  pl.BlockSpec((E(tm), E(4 * D)), lambda i: (i * tm, GL0)),
                  pl.BlockSpec((tm, D), lambda i: (i, 0)),
                  pl.BlockSpec((tm, 3 * BW), lambda i: (i, 0)),
                  pl.BlockSpec((tm, BW), lambda i: (i, 0)),
                  wspec, wspec, wspec, wspec,
                  pl.BlockSpec((D, D), lambda i: (0, 0))],
        out_specs=[pl.BlockSpec((E(tm), E(4 * D)), lambda i: (i * tm, GL0)),
                   pl.BlockSpec((tm, 4 * BW), lambda i: (i, 0)),
                   pl.BlockSpec((D, D), lambda i: (0, 0)), dwspec, dwspec, dwspec, dwspec],
        out_shape=[_sds((s, IN_W), BF16), _sds((s, 4 * BW), BF16), _sds((D, D), BF16)] + [_sds((D, BW), BF16)] * 4,
        scratch_shapes=[pltpu.VMEM((D, D), F32)] + [pltpu.VMEM((D, BW), F32)] * 4, name=f"bwd_merge{l}")


def bwd_attn(proj, dpre, vecs, l, comm=None):
    s = proj.shape[0]
    t = ATT_SUB * ATT_BLK
    grp = N_HEADS // N_KV

    def body(q_ref, kvp_ref, kvc_ref, do_ref, vec_ref, dq_ref, dkc_ref, dkp_ref, st_ref):
        @pl.when(pl.program_id(0) == 0)
        def _():
            st_ref[...] = jnp.zeros((8, 128), F32)

        lane = lax.broadcasted_iota(jnp.int32, (1, 128), 1)
        dsink = jnp.zeros((1, 128), F32)
        units = _attn_units(q_ref, kvp_ref, kvc_ref, pl.program_id(0) == 0)
        groups = _attn_probs(units, vec_ref)
        us = range(len(units))
        do4s = [jnp.concatenate([do_ref[u["rows"], h * HD:(h + 1) * HD] for h in range(u["hk"] * grp, (u["hk"] + 1) * grp)],
                                axis=0) for u in units]
        dps = [_dot(do4s[i], groups[i][2], _NT) for i in us]
        deltas = [jnp.sum(groups[i][3] * dps[i], axis=-1, keepdims=True) for i in us]
        dss = [groups[i][3] * (dps[i] - deltas[i]) * (HD ** -0.5) for i in us]
        dq4s = [_dot(dss[i], groups[i][1], _NN).astype(BF16) for i in us]
        dk2s = [_dot(dss[i], groups[i][0], _TN) for i in us]
        dv2s = [_dot(groups[i][3], do4s[i], _TN) for i in us]
        for i, u in enumerate(units):
            psd = groups[i][4] * deltas[i]
            for j in range(grp):
                h = u["hk"] * grp + j
                rows = slice(j * ATT_BLK, (j + 1) * ATT_BLK)
                dq_ref[u["rows"], h * HD:(h + 1) * HD] = dq4s[i][rows]
                dsink = dsink + jnp.where(lane == h, -jnp.sum(psd[rows], axis=0, keepdims=True), 0.0)
        for i, u in enumerate(units):
            nxt = [k for k, w in enumerate(units) if w["hk"] == u["hk"] and w["b"] == u["b"] + 1]
            for grad, c0 in ((dk2s, u["hk"] * HD), (dv2s, (N_KV + u["hk"]) * HD)):
                own = grad[i][ATT_BLK:]
                if nxt:
                    own = own + grad[nxt[0]][0:ATT_BLK]
                dkc_ref[u["rows"], c0:c0 + HD] = own.astype(BF16)
                if u["b"] == 0:
                    dkp_ref[:, c0:c0 + HD] = grad[i][0:ATT_BLK].astype(BF16)
        st_ref[0:1, :] += dsink

    return _call(
        body, comm, (proj, proj, proj, dpre, vecs), grid=(s // t,),
        in_specs=[pl.BlockSpec((t, BW), lambda i: (i, C_Q // BW)),
                  pl.BlockSpec((ATT_BLK, 256), lambda i: (jnp.maximum(ATT_SUB * i - 1, 0), C_K // 256)),
                  pl.BlockSpec((t, 256), lambda i: (i, C_K // 256)),
                  pl.BlockSpec((t, BW), lambda i: (i, 2)),
                  pl.BlockSpec((None, V_ROWS, BW), lambda i: (l, 0, 0))],
        out_specs=[pl.BlockSpec((t, BW), lambda i: (i, 0)), pl.BlockSpec((t, 256), lambda i: (i, 0)),
                   pl.BlockSpec((ATT_BLK, 256), lambda i: (i, 0)), pl.BlockSpec((8, 128), lambda i: (0, 0))],
        out_shape=[_sds((s, BW), BF16), _sds((s, 256), BF16), _sds((s // ATT_SUB, 256), BF16), _sds((8, 128), F32)],
        name=f"bwd_attn{l}")


def bwd_branch(proj, dproj, dpre, h, saved, dq, dkc, dkp, convw, vecs, wx_bd, wa_bd, l, comm=None):
    s = proj.shape[0]
    t = 2 * ATT_BLK
    nt = s // t
    nb = s // ATT_BLK
    hb = t // HALO

    def body(cur_ref, halo_ref, dpre_ref, h_ref, hp_ref, dq_ref, dkc_ref, dkp_ref,
             cw_ref, vec_ref, wx_ref, wa_ref, sv_ref, dproj_in, dp_ref, dcw_ref, dvec_ref, dwx_ref, dwa_ref,
             bufa, bufb, bufd, xd, xg, a_ext, hbuf, b_s, g_s, dh_s, ga, gb, gd, dhcar):
        del dproj_in
        step = pl.program_id(0)
        ti = nt - 1 - step
        first = ti == 0

        @pl.when(step == 0)
        def _():
            dcw_ref[...] = jnp.zeros((CW_ROWS, BW), F32)
            dvec_ref[...] = jnp.zeros((V_ROWS, BW), F32)
            dwx_ref[...] = jnp.zeros((BW, BW), F32)
            dwa_ref[...] = jnp.zeros((BW, BW), F32)
            dhcar[...] = jnp.zeros((1, BW), F32)
            a_ext[t:t + 8, :] = jnp.zeros((8, BW), F32)
            ga[t:t + 8, :] = jnp.zeros((8, BW), F32)
            gb[t:t + 8, :] = jnp.zeros((8, BW), F32)
            gd[t:t + HALO, :] = jnp.zeros((HALO, BW), F32)

        def cur(c0):
            return cur_ref[:, c0:c0 + BW].astype(F32)

        def rsum(v):
            return jnp.sum(v, axis=0, keepdims=True)

        def put(c0, v):
            dp_ref[:, c0:c0 + BW] = v.astype(BF16)

        v = _branch_fwd_math(cur_ref, halo_ref, cw_ref, vec_ref, wx_ref, wa_ref, bufa, bufb, bufd, xd, first, t, sv_ref)
        ca, gi, gr, sp, a, mult = v["ca"], v["gi"], v["gr"], v["sp"], v["a"], v["mult"]
        dpa = dpre_ref[:, 0:BW].astype(F32)
        gg, dgg = _gelu_and_grad(cur(C_AG))
        hv = h_ref[...]
        put(C_AG, dpa * hv * dgg)
        a_ext[0:t, :] = a
        b_s[...] = a_ext[pl.ds(1, t), :]
        g_s[...] = dpa * gg
        dhcar[...] = _scan_bwd(b_s, g_s, dh_s, dhcar[...], t)
        a_ext[t:t + 1, :] = a[0:1, :]
        dh = dh_s[...]
        hbuf[0:8, :] = jnp.where(first, 0.0, hp_ref[...])
        hbuf[8:8 + t, :] = hv
        da = dh * hbuf[pl.ds(7, t), :]
        d_ca = dh * gi * mult
        d_gi = dh * ca * mult
        d_mult = dh * ca * gi
        d_la = da * a - d_mult * (a * a) / mult
        lam = vec_ref[V_LAM:V_LAM + 1, :]
        dvec_ref[V_LAM:V_LAM + 1, :] += rsum(d_la * gr) * (LRU_C * _sigmoid(-lam))
        d_gr = d_la * (-LRU_C * sp)
        d_zr = d_gr * gr * (1.0 - gr)
        d_zi = d_gi * gi * (1.0 - gi)
        dvec_ref[V_BA:V_BA + 1, :] += rsum(d_zr)
        dvec_ref[V_BX:V_BX + 1, :] += rsum(d_zi)
        dwa_ref[...] += _dot(ca, d_zr, _TN)
        dwx_ref[...] += _dot(ca, d_zi, _TN)
        d_ca = d_ca + _dot(d_zi, wx_ref[...], _NT) + _dot(d_zr, wa_ref[...], _NT)
        dvec_ref[V_CAB:V_CAB + 1, :] += rsum(d_ca)
        ga[0:t, :] = d_ca
        d_ax = jnp.zeros((t, BW), F32)
        for k in range(CONV_A):
            d_ax = d_ax + cw_ref[CW_A + k:CW_A + k + 1, :] * ga[pl.ds(CONV_A - 1 - k, t), :]
            dcw_ref[CW_A + k:CW_A + k + 1, :] += rsum(d_ca * bufa[pl.ds(HALO - (CONV_A - 1) + k, t), :])
        ga[t:t + 8, :] = d_ca[0:8, :]
        put(C_AX, d_ax)
        dpb = dpre_ref[:, BW:2 * BW].astype(F32)
        put(C_BB, dpb * v["cb"])
        d_cb = dpb * cur(C_BB)
        gb[0:t, :] = d_cb
        d_cbin = jnp.zeros((t, BW), F32)
        for k in range(CONV_B):
            d_cbin = d_cbin + cw_ref[CW_B + k:CW_B + k + 1, :] * gb[pl.ds(CONV_B - 1 - k, t), :]
            dcw_ref[CW_B + k:CW_B + k + 1, :] += rsum(d_cb * bufb[pl.ds(HALO - (CONV_B - 1) + k, t), :])
        gb[t:t + 8, :] = d_cb[0:8, :]
        put(C_BC, d_cbin * cur(C_BV))
        put(C_BV, d_cbin * cur(C_BC))
        dpd = dpre_ref[:, 3 * BW:4 * BW].astype(F32)
        ln, xh, rstd, s2 = v["ln"], v["xh"], v["rstd"], v["s2"]
        sg = _sigmoid(ln)
        d_ln = dpd * sg * (1.0 + ln * (1.0 - sg))
        dvec_ref[V_LNG:V_LNG + 1, :] += rsum(d_ln * xh)
        dvec_ref[V_LNB:V_LNB + 1, :] += rsum(d_ln)
        d_xh = d_ln * vec_ref[V_LNG:V_LNG + 1, :]
        d_cd = rstd * (d_xh - jnp.mean(d_xh, axis=-1, keepdims=True)
                       - xh * jnp.mean(d_xh * xh, axis=-1, keepdims=True))
        dvec_ref[V_CDB:V_CDB + 1, :] += rsum(d_cd)
        gd[0:t, :] = d_cd
        _shifted_copies(gd, xg, t + HALO)
        d_dg = jnp.zeros((t, BW), F32)
        for k in range(CONV_D):
            d_dg = d_dg + cw_ref[CW_D + k:CW_D + k + 1, :] * _window(gd, xg, CONV_D - 1 - k, t)
            dcw_ref[CW_D + k:CW_D + k + 1, :] += rsum(d_cd * _window(bufd, xd, HALO - (CONV_D - 1) + k, t))
        gd[t:t + HALO, :] = d_cd[0:HALO, :]
        put(C_D1, d_dg * s2)
        put(C_D2, d_dg * cur(C_D1) * s2 * (1.0 - s2))
        dp_ref[:, C_Q:C_Q + BW] = dq_ref[...]
        dkp = jnp.where(step == 0, 0.0, dkp_ref[...].astype(F32))
        dp_ref[0:t - ATT_BLK, C_K:C_K + 256] = dkc_ref[0:t - ATT_BLK, :]
        dp_ref[t - ATT_BLK:t, C_K:C_K + 256] = (dkc_ref[t - ATT_BLK:t, :].astype(F32) + dkp).astype(BF16)

    rev = lambda i: nt - 1 - i
    full = lambda r, c: pl.BlockSpec((r, c), lambda i: (0, 0))
    return _call(
        body, comm, (proj, proj, dpre, h, h, dq, dkc, dkp, convw, vecs, wx_bd, wa_bd, saved, dproj), grid=(nt,),
        in_specs=[pl.BlockSpec((t, GL0), lambda i: (rev(i), 0)),
                  pl.BlockSpec((HALO, GL0), lambda i: (jnp.maximum(rev(i) * hb - 1, 0), 0)),
                  pl.BlockSpec((t, 4 * BW), lambda i: (rev(i), 0)),
                  pl.BlockSpec((t, BW), lambda i: (rev(i), 0)),
                  pl.BlockSpec((8, BW), lambda i: (jnp.maximum(rev(i) * (t // 8) - 1, 0), 0)),
                  pl.BlockSpec((t, BW), lambda i: (rev(i), 0)),
                  pl.BlockSpec((t, 256), lambda i: (rev(i), 0)),
                  pl.BlockSpec((ATT_BLK, 256), lambda i: (jnp.minimum(rev(i) + 1, nt - 1), 0)),
                  pl.BlockSpec((None, CW_ROWS, BW), lambda i: (l, 0, 0)),
                  pl.BlockSpec((None, V_ROWS, BW), lambda i: (l, 0, 0)),
                  pl.BlockSpec((None, BW, BW), lambda i: (l, 0, 0)),
                  pl.BlockSpec((None, BW, BW), lambda i: (l, 0, 0)),
                  pl.BlockSpec((t, 3 * BW), lambda i: (rev(i), 0)),
                  pl.BlockSpec(memory_space=pl.ANY)],
        out_specs=[pl.BlockSpec((t, GL0), lambda i: (rev(i), 0)),
                   full(CW_ROWS, BW), full(V_ROWS, BW), full(BW, BW), full(BW, BW)],
        out_shape=[_sds((s, IN_W), BF16), _sds((CW_ROWS, BW), F32), _sds((V_ROWS, BW), F32),
                   _sds((BW, BW), F32), _sds((BW, BW), F32)],
        scratch_shapes=[pltpu.VMEM((t + HALO, BW), F32)] * 3 + [pltpu.VMEM((7, t + HALO - 8, BW), F32)] * 2
        + [pltpu.VMEM((t + 8, BW), F32), pltpu.VMEM((t + 8, BW), F32)]
        + [pltpu.VMEM((t, BW), F32)] * 3
        + [pltpu.VMEM((t + 8, BW), F32), pltpu.VMEM((t + 8, BW), F32), pltpu.VMEM((t + HALO, BW), F32),
           pltpu.VMEM((1, BW), F32)],
        aliases={13: 0}, name=f"bwd_branch{l}")


def bwd_proj(dproj, x, dh1, g1, wt_in, l, comm=None):
    s = x.shape[0]
    tm = min(512, s)
    ck = 1408
    n_j, n_i = IN_W // ck, s // tm

    def body(dp_ref, x_ref, dh_ref, g_ref, w_ref, dx_ref, dw_ref, st_ref, dxn, xn_b, acc):
        j, i = pl.program_id(0), pl.program_id(1)
        rows = pl.ds(pl.multiple_of(i * tm, tm), tm)
        g = g_ref[l:l + 1, :]

        @pl.when(j == 0)
        def _():
            xv = x_ref[...]
            r = lax.rsqrt(jnp.mean(xv * xv, axis=-1, keepdims=True) + EPS)
            xn_b[rows, :] = (xv * r * g).astype(BF16)
            dxn[rows, :] = jnp.zeros((tm, D), F32)

        @pl.when((j == 0) & (i == 0))
        def _():
            st_ref[...] = jnp.zeros((8, D), F32)

        @pl.when(i == 0)
        def _():
            acc[...] = jnp.zeros((ck, D), F32)

        dp = dp_ref[...]
        dxn[rows, :] += _dot(dp, w_ref[...], _NN)
        acc[...] += _dot(dp, xn_b[rows, :], _TN)

        @pl.when(i == n_i - 1)
        def _():
            dw_ref[...] = acc[...].astype(BF16)

        @pl.when(j == n_j - 1)
        def _():
            xv = x_ref[...]
            r = lax.rsqrt(jnp.mean(xv * xv, axis=-1, keepdims=True) + EPS)
            n = xv * r
            dv = dxn[rows, :]
            dn = dv * g
            dx_ref[...] = dh_ref[...] + r * (dn - n * jnp.mean(dn * n, axis=-1, keepdims=True))
            st_ref[0:1, :] += jnp.sum(dv * n, axis=0, keepdims=True)

    lastrow = lambda j, i: (jnp.where(j == n_j - 1, i, 0), 0)
    return _call(
        body, comm, (dproj, x, dh1, g1, wt_in), grid=(n_j, n_i),
        in_specs=[pl.BlockSpec((tm, ck), lambda j, i: (i, j)),
                  pl.BlockSpec((tm, D), lambda j, i: (_edge_index(j, i, n_j, n_i), 0)),
                  pl.BlockSpec((tm, D), lastrow),
                  pl.BlockSpec((DEPTH, D), lambda j, i: (0, 0)),
                  pl.BlockSpec((ck, D), lambda j, i: (j, 0))],
        out_specs=[pl.BlockSpec((tm, D), lastrow), pl.BlockSpec((ck, D), lambda j, i: (j, 0)),
                   pl.BlockSpec((8, D), lambda j, i: (0, 0))],
        out_shape=[_sds((s, D), F32), _sds((IN_W, D), BF16), _sds((8, D), F32)],
        scratch_shapes=[pltpu.VMEM((s, D), F32), pltpu.VMEM((s, D), BF16), pltpu.VMEM((ck, D), F32)],
        name=f"bwd_proj{l}")


def bwd_proj_w(dproj, x, g1, l, half, comm=None):
    s = x.shape[0]
    tm = min(1024, s)
    ck = 1408
    c0, hw = W_IN_PARTS[half]
    n_j, n_i = IN_W // ck, s // tm

    def body(dp_ref, x_ref, g_ref, dw_ref, xn_b, acc):
        j, i = pl.program_id(0), pl.program_id(1)
        rows = pl.ds(pl.multiple_of(i * tm, tm), tm)

        @pl.when(j == 0)
        def _():
            xv = x_ref[...]
            r = lax.rsqrt(jnp.mean(xv * xv, axis=-1, keepdims=True) + EPS)
            xn_b[rows, :] = (xv * r * g_ref[l:l + 1, :])[:, c0:c0 + hw].astype(BF16)

        @pl.when(i == 0)
        def _():
            acc[...] = jnp.zeros((ck, hw), F32)

        acc[...] += _dot(dp_ref[...], xn_b[rows, :], _TN)

        @pl.when(i == n_i - 1)
        def _():
            dw_ref[...] = acc[...].astype(BF16)

    return _call(
        body, comm, (dproj, x, g1), grid=(n_j, n_i),
        in_specs=[pl.BlockSpec((tm, ck), lambda j, i: (i, j)),
                  pl.BlockSpec((tm, D), lambda j, i: (jnp.where(j == 0, i, n_i - 1), 0)),
                  pl.BlockSpec((DEPTH, D), lambda j, i: (0, 0))],
        out_specs=pl.BlockSpec((ck, hw), lambda j, i: (j, 0)),
        out_shape=_sds((IN_W, hw), BF16),
        scratch_shapes=[pltpu.VMEM((s, hw), BF16), pltpu.VMEM((ck, hw), F32)],
        name=f"bwd_proj_w{half}_{l}")


def bwd_proj_x(dproj, x, dh1, g1, wt_in, l, comm=None):
    s = x.shape[0]
    tm = min(512, s)
    ck = 1408
    n_j, n_i = IN_W // ck, s // tm

    def body(dp_ref, x_ref, dh_ref, g_ref, w_ref, dx_ref, st_ref, dxn):
        j, i = pl.program_id(0), pl.program_id(1)
        rows = pl.ds(pl.multiple_of(i * tm, tm), tm)
        g = g_ref[l:l + 1, :]

        @pl.when((j == 0) & (i == 0))
        def _():
            st_ref[...] = jnp.zeros((8, D), F32)

        part = _dot(dp_ref[...], w_ref[...], _NN)

        @pl.when(j == 0)
        def _():
            dxn[rows, :] = part

        @pl.when(j > 0)
        def _():
            dxn[rows, :] += part

        @pl.when(j == n_j - 1)
        def _():
            xv = x_ref[...]
            r = lax.rsqrt(jnp.mean(xv * xv, axis=-1, keepdims=True) + EPS)
            n = xv * r
            dv = dxn[rows, :]
            dn = dv * g
            dx_ref[...] = dh_ref[...] + r * (dn - n * jnp.mean(dn * n, axis=-1, keepdims=True))
            st_ref[0:1, :] += jnp.sum(dv * n, axis=0, keepdims=True)

    lastrow = lambda j, i: (jnp.where(j == n_j - 1, i, 0), 0)
    return _call(
        body, comm, (dproj, x, dh1, g1, wt_in), grid=(n_j, n_i),
        in_specs=[pl.BlockSpec((tm, ck), lambda j, i: (i, j)), pl.BlockSpec((tm, D), lastrow),
                  pl.BlockSpec((tm, D), lastrow),
                  pl.BlockSpec((DEPTH, D), lambda j, i: (0, 0)), pl.BlockSpec((ck, D), lambda j, i: (j, 0))],
        out_specs=[pl.BlockSpec((tm, D), lastrow), pl.BlockSpec((8, D), lambda j, i: (0, 0))],
        out_shape=[_sds((s, D), F32), _sds((8, D), F32)],
        scratch_shapes=[pltpu.VMEM((s, D), F32)], name=f"bwd_proj_x{l}")


def _block_diag(w):
    nl, nb, bw, _ = w.shape
    eye = jnp.eye(nb, dtype=w.dtype)
    return jnp.einsum("lhij,hk->lhikj", w, eye).reshape(nl, nb * bw, nb * bw).astype(BF16)


class NoOverlap:
    def __init__(self, big):
        self.big = big

    def weights(self, l):
        return self.big[l]

    def job(self, slot, l):
        return None

    def done(self, slot, l, results):
        pass

    def new_grads(self, group, l, grads):
        pass

    def new_small(self, l, arrays, head_stats):
        pass


def local_step(x, target, norm1_g, norm2_g, final_g, convw, vecs, lru_wx, lru_wa, plan):
    wx_bd, wa_bd = _block_diag(lru_wx), _block_diag(lru_wa)

    def run(fn, slot, l, *args):
        res, cres = fn(*args, l, comm=plan.job(slot, l))
        plan.done(slot, l, cres)
        return res

    saved = []
    for l in range(DEPTH):
        proj = run(fwd_proj, "fwd_proj", l, x, norm1_g, plan.weights(l)["in_t"])
        pre_abd, h, kept = run(fwd_branch, "fwd_branch", l, proj, convw, vecs, wx_bd, wa_bd)
        pre_c = run(fwd_attn, "fwd_attn", l, proj, vecs)
        w = plan.weights(l)
        y4, merged, h1 = run(fwd_merge, "fwd_merge", l, x, proj, pre_abd, pre_c, w["a_t"], w["b_t"], w["c_t"], w["d_t"], w["o"])
        w = plan.weights(l)
        x_out, fg, fu = run(fwd_ffn, "fwd_ffn", l, h1, norm2_g, w["gate_t"], w["up_t"], w["down"])
        saved.append((x, proj, pre_abd, h, kept, pre_c, y4, merged, h1, fg, fu))
        x = x_out
    dx, head_stats = loss_head(x, final_g.reshape(1, D), target)
    small = [None] * DEPTH
    for l in reversed(range(DEPTH)):
        x_in, proj, pre_abd, h, kept, pre_c, y4, merged, h1, fg, fu = saved[l]
        w = plan.weights(l)
        dh1, d_gate, d_up, d_down, st_ffn = run(bwd_ffn, "bwd_ffn", l, dx, h1, fg, fu, norm2_g, w["gate_t"], w["up_t"], w["down"])
        plan.new_grads("ffn", l, dict(gate_t=d_gate, up_t=d_up, down=d_down))
        dproj, dpre, d_o, d_a, d_b, d_c, d_d = run(
            bwd_merge, "bwd_merge", l, dh1, y4, proj, merged, pre_abd, pre_c, w["a_t"], w["b_t"], w["c_t"], w["d_t"], w["o"])
        plan.new_grads("out", l, dict(a_t=d_a, b_t=d_b, c_t=d_c, d_t=d_d, o=d_o))
        dq, dkc, dkp, st_attn = run(bwd_attn, "bwd_attn", l, proj, dpre, vecs)
        dproj, dcw, dvec, dwx, dwa = run(bwd_branch, "bwd_branch", l, proj, dproj, dpre, h, kept, dq, dkc, dkp, convw, vecs, wx_bd, wa_bd)
        if l > 0:
            dx, d_in, st_proj = run(bwd_proj, "bwd_proj", l, dproj, x_in, dh1, norm1_g, w["in_t"])
            plan.new_grads("in", l, dict(in_t=d_in))
        else:
            for half, name in enumerate(("in_a", "in_b")):
                d_half = run(functools.partial(bwd_proj_w, half=half), f"bwd_proj_w{half}", l, dproj, x_in, norm1_g)
                plan.new_grads(name, l, {name: d_half})
            dx, st_proj = run(bwd_proj_x, "bwd_proj_x", l, dproj, x_in, dh1, norm1_g, w["in_t"])
        small[l] = (st_proj, st_ffn, dvec, st_attn, dcw, dwx, dwa)
        plan.new_small(l, small[l], head_stats)
    return head_stats, dx, small


BIG = dict(in_t=("w_in", "view"), a_t=("w_a_out", "transpose"), b_t=("w_b_out", "transpose"), c_t=("w_c_out", "transpose"),
           d_t=("w_d_out", "transpose"), o=("w_o", "plain"), gate_t=("w_ffn_gate", "view"), up_t=("w_ffn_up", "view"),
           down=("w_ffn_down", "plain"))


def cast_transpose(w, name):
    nl, a, b = w.shape
    ta = min(256, a)

    def body(w_ref, o_ref):
        o_ref[...] = w_ref[...].T.astype(BF16)

    return pl.pallas_call(
        body, grid=(nl, a // ta),
        in_specs=[pl.BlockSpec((None, ta, b), lambda l, i: (l, i, 0))],
        out_specs=pl.BlockSpec((None, b, ta), lambda l, i: (l, 0, i)),
        out_shape=_sds((nl, b, a), BF16), compiler_params=_cparams(2), name=name)(w)


def add_partials(mine, recv, core, name):
    n = len(mine)

    def body(core_ref, *refs):
        del core_ref
        for a_ref, b_ref, o_ref in zip(refs[:n], refs[n:2 * n], refs[2 * n:]):
            o_ref[...] = (a_ref[...].astype(F32) + b_ref[...].astype(F32)).astype(BF16)

    return pl.pallas_call(
        body,
        grid_spec=pltpu.PrefetchScalarGridSpec(
            num_scalar_prefetch=1, grid=(4,),
            in_specs=[pl.BlockSpec((None, None) + a.shape[2:], lambda i, cr: (i, cr[0], 0, 0)) for a in mine]
            + [pl.BlockSpec((None,) + b.shape[1:], lambda i, cr: (i, 0, 0)) for b in recv],
            out_specs=[pl.BlockSpec((None,) + b.shape[1:], lambda i, cr: (i, 0, 0)) for b in recv]),
        out_shape=[_sds(b.shape, BF16) for b in recv], compiler_params=_cparams(1), name=name)(core, *mine, *recv)


def _adamw(w, g, m, v):
    m = ADAM_B1 * m + (1.0 - ADAM_B1) * g
    v = ADAM_B2 * v + (1.0 - ADAM_B2) * (g * g)
    m_hat = m / (1.0 - ADAM_B1 ** ADAM_STEP)
    v_hat = v / (1.0 - ADAM_B2 ** ADAM_STEP)
    delta = -ADAM_LR * (m_hat / (jnp.sqrt(v_hat) + ADAM_EPS) + ADAM_WD * w)
    return delta, m, v


def adamw_big(contrib, w, m, v, transposed, name, comm=None):
    nsrc, nl, rows, cols = contrib.shape
    ct = 256

    def body(c_ref, w_ref, m_ref, v_ref, g_out, d_out, m_out, v_out):
        g = c_ref[0].astype(F32)
        for src in range(1, nsrc):
            g = g + c_ref[src].astype(F32)
        if transposed:
            g = g.T
        delta, mn, vn = _adamw(w_ref[...], g, m_ref[...], v_ref[...])
        g_out[...] = g
        d_out[...] = delta
        m_out[...] = mn
        v_out[...] = vn

    if transposed:
        wspec = pl.BlockSpec((None, ct, rows), lambda l, j: (l, j, 0))
    else:
        wspec = pl.BlockSpec((None, rows, ct), lambda l, j: (l, 0, j))
    return _call(
        body, comm, (contrib, w, m, v), grid=(nl, cols // ct),
        in_specs=[pl.BlockSpec((nsrc, None, rows, ct), lambda l, j: (0, l, 0, j)), wspec, wspec, wspec],
        out_specs=[wspec] * 4, out_shape=[_sds(w.shape, F32)] * 4, name=name)


VEC_NAMES = ("conv_a_b", "lru_bx", "lru_ba", "lru_lambda", "conv_d_b", "ln_d_g", "ln_d_b")
P_N1, P_N2, P_VEC, P_CONV, P_LRU = 0, 1, 2, 6, 6 + CW_ROWS
P_FINAL, P_LOSS, P_ROWS = P_LRU + HD, P_LRU + HD + 1, P_LRU + HD + 2
SMALL = ("norm1_g", "conv_a_w", "conv_a_b", "lru_wx", "lru_bx", "lru_wa", "lru_ba", "lru_lambda", "conv_b_w", "sinks",
         "conv_d_w", "conv_d_b", "ln_d_g", "ln_d_b", "norm2_g", "final_g")
VMEM_FULL = pl.BlockSpec(memory_space=pltpu.VMEM)


def _stack_vecs(p):
    rows = [p[n] for n in VEC_NAMES] + [jnp.pad(p["sinks"], ((0, 0), (0, BW - N_HEADS)))]
    return jnp.stack(rows, axis=1)


def _stack_convs(p):
    nl, _, ch = p["conv_a_w"].shape
    z = jnp.zeros((nl, 1, ch), F32)
    return jnp.concatenate([p["conv_a_w"], p["conv_b_w"], z, p["conv_d_w"], z], axis=1)


def _vec_place(r):
    return P_VEC + r // 2, (r % 2) * BW


def pack_small(arrays, head_stats, l):
    n = len(arrays)

    def body(*refs):
        st_proj, st_ffn, dvec, st_attn, dcw, dwx, dwa = refs[:n]
        pack = refs[-1]
        pack[...] = jnp.zeros((P_ROWS, D), F32)
        lane = lax.broadcasted_iota(jnp.int32, (HD, BW), 1)
        pack[P_N1:P_N1 + 1, :] = st_proj[0:1, :]
        pack[P_N2:P_N2 + 1, :] = st_ffn[0:1, :]
        for r in range(len(VEC_NAMES)):
            row, c0 = _vec_place(r)
            pack[row:row + 1, c0:c0 + BW] = dvec[r:r + 1, :]
        row, c0 = _vec_place(V_SINK)
        pack[row:row + 1, c0:c0 + 128] = st_attn[0:1, :]
        pack[P_CONV:P_CONV + CW_ROWS, 0:BW] = dcw[...]
        for mat, c0 in ((dwx, 0), (dwa, BW)):
            blocks = jnp.zeros((HD, BW), F32)
            for h in range(BW // HD):
                blocks = jnp.where((lane >= HD * h) & (lane < HD * (h + 1)), mat[HD * h:HD * (h + 1), :], blocks)
            pack[P_LRU:P_LRU + HD, c0:c0 + BW] = blocks
        if head_stats is not None:
            pack[P_FINAL:P_LOSS + 1, :] = refs[n][0:2, :]

    flat = list(arrays) + ([] if head_stats is None else [head_stats])
    return pl.pallas_call(body, out_shape=_sds((P_ROWS, D), F32), in_specs=[VMEM_FULL] * len(flat), out_specs=VMEM_FULL,
                          name=f"pack_small{l}", compiler_params=pltpu.CompilerParams(vmem_limit_bytes=VMEM_LIMIT))(*flat)


def adamw_small(gathered, me, w, m, v):
    ns = len(SMALL)

    def body(me_ref, *refs):
        c_refs, refs = refs[:DEPTH], refs[DEPTH:]
        w_refs, m_refs, v_refs = refs[:ns], refs[ns:2 * ns], refs[2 * ns:3 * ns]
        loss_ref, outs, gs = refs[3 * ns], refs[3 * ns + 1:3 * ns + 1 + 4 * ns], refs[-1]
        for l in range(DEPTH):
            gs[l] = c_refs[l][0]
            for dev in range(1, NDEV):
                gs[l] += c_refs[l][dev]
        loss_ref[...] = gs[DEPTH - 1, P_LOSS:P_LOSS + 1, 0:128]

        def update(name, sel, g):
            i = SMALL.index(name)
            delta, mn, vn = _adamw(w_refs[i][sel], g, m_refs[i][sel], v_refs[i][sel])
            for o_ref, val in zip(outs[4 * i:4 * i + 4], (g, delta, mn, vn)):
                o_ref[sel] = val

        update("final_g", (slice(0, 1), slice(None)), gs[DEPTH - 1, P_FINAL:P_FINAL + 1, :])
        shift = (BW - me_ref[0] * (BW // NDEV)) & (BW - 1)
        for l in range(DEPTH):
            row = (slice(l, l + 1), slice(None))
            update("norm1_g", row, gs[l, P_N1:P_N1 + 1, :])
            update("norm2_g", row, gs[l, P_N2:P_N2 + 1, :])
            for r, name in enumerate(VEC_NAMES):
                prow, c0 = _vec_place(r)
                update(name, row, gs[l, prow:prow + 1, c0:c0 + BW])
            prow, c0 = _vec_place(V_SINK)
            update("sinks", row, gs[l, prow:prow + 1, c0:c0 + N_HEADS])
            mine = pltpu.roll(gs[l, P_CONV:P_CONV + CW_ROWS, 0:BW], shift, 1)[:, 0:BW // NDEV]
            update("conv_a_w", (l,), mine[CW_A:CW_A + CONV_A])
            update("conv_b_w", (l,), mine[CW_B:CW_B + CONV_B])
            update("conv_d_w", (l,), mine[CW_D:CW_D + CONV_D])
            for h in range(BW // HD):
                update("lru_wx", (l, h), gs[l, P_LRU:P_LRU + HD, HD * h:HD * (h + 1)])
                update("lru_wa", (l, h), gs[l, P_LRU:P_LRU + HD, BW + HD * h:BW + HD * (h + 1)])

    args = [p[n] for p in (w, m, v) for n in SMALL]
    full = lambda a: pl.BlockSpec(a.shape, lambda i, me_ref: (0,) * a.ndim)
    out_shape = [_sds((1, 128), F32)] + [_sds(w[n].shape, F32) for n in SMALL for _ in range(4)]
    outs = pl.pallas_call(
        body,
        grid_spec=pltpu.PrefetchScalarGridSpec(
            num_scalar_prefetch=1, grid=(1,),
            in_specs=[full(a) for a in list(gathered) + args], out_specs=[full(o) for o in out_shape],
            scratch_shapes=[pltpu.VMEM((DEPTH, P_ROWS, D), F32)]),
        out_shape=out_shape, name="adamw_small", compiler_params=_cparams(1))(me, *gathered, *args)
    return outs[0], {n: outs[1 + 4 * i:5 + 4 * i] for i, n in enumerate(SMALL)}


def merge_jobs(jobs):
    jobs = [j for j in jobs if j is not None]
    if not jobs:
        return None, []
    inputs, aliases, outs, sems, cuts = [], {}, [], [], []
    for j in jobs:
        i0, o0, s0 = len(inputs), len(outs), len(sems)
        aliases.update({i0 + i: o0 + o for i, o in j.aliases.items()})
        inputs += j.inputs
        outs += j.out_shapes
        sems += j.sem_shapes
        cuts.append((i0, len(inputs), o0, len(outs), s0, len(sems)))

    def each(which):
        def go(cins, couts, s):
            for j, (i0, i1, o0, o1, s0, s1) in zip(jobs, cuts):
                if getattr(j, which) is not None:
                    getattr(j, which)(cins[i0:i1], couts[o0:o1], s[s0:s1])
        return go

    relay = each("relay") if any(j.relay is not None for j in jobs) else None
    return CommJob(inputs, aliases, outs, sems, each("start"), each("finish"), relay), [(c[2], c[3]) for c in cuts]


SIXTHS = 6
OUT_KINDS = ("a_t", "b_t", "c_t", "d_t", "o")
GATHER_PLAN = {
    "fwd_proj": [(k, 0, 0, 6) for k in OUT_KINDS] + [("gate_t", 0, 0, 6)],
    "fwd_branch": [("up_t", 0, 0, 6)],
    "fwd_attn": [("down", 0, 0, 6)],
    "fwd_merge": [("in_t", 1, 0, 2)],
    "fwd_ffn": [("in_t", 1, 2, 6)],
}
SIBLING_PLAN = {"bwd_merge": ("ffn", 0), "bwd_branch": ("out", 0), "bwd_ffn": ("in", 1),
                "bwd_proj_w1": ("in_a", 0), "bwd_proj_x": ("in_b", 0)}
GROUPS = dict(ffn=("gate_t", "up_t", "down"), out=OUT_KINDS, in_a=("in_a",), in_b=("in_b",))
GROUPS["in"] = ("in_t",)
COLUMN_HALF = dict(in_a=("in_t", W_IN_PARTS[0][0]), in_b=("in_t", W_IN_PARTS[1][0]))
CHIP_PLAN = {
    "bwd_attn": [("in_t", 1, 3, 5)],
    "bwd_branch": [("in_t", 1, 5, 6), ("gate_t", 0, 0, 6), ("up_t", 0, 0, 6)],
    "bwd_proj": [(k, 0, 0, 6) for k in OUT_KINDS] + [("down", 0, 0, 6)],
    "bwd_proj_w0": [(k, 0, 0, 6) for k in OUT_KINDS[:3]] + [("down", 0, 0, 6)],
    "bwd_proj_w1": [(k, 0, 0, 6) for k in OUT_KINDS[3:]],
    "bwd_merge": [("in_t", 1, 0, 3)],
    "bwd_proj_x": [("in_a", 0, 0, 6)],
    "adamw_gate_t": [("in_b", 0, 0, 6)],
}
SMALL_GATHER_PLAN = {"bwd_ffn": 1, "adamw_down": 0}


class Overlap:
    def __init__(self, shards, core):
        self.shards = shards
        self.core = core
        self.gathered = [dict.fromkeys(BIG) for _ in range(DEPTH)]
        self.views = {}
        self.partial = {}
        self.contrib = dict.fromkeys(BIG)
        self.small_packs = [None] * DEPTH
        self.small_gathered = [None] * DEPTH
        self._open = None

    def weights(self, l):
        return self.gathered[l]

    def new_grads(self, group, l, grads):
        for k, g in grads.items():
            self.views[k, l] = g.reshape(4, 2, g.shape[0] // NDEV, g.shape[1])

    def new_small(self, l, arrays, head_stats):
        self.small_packs[l] = pack_small(arrays, head_stats if l == DEPTH - 1 else None, l)

    @staticmethod
    def _rows(shard_rows, f0, f1):
        return shard_rows * f0 // SIXTHS, shard_rows * (f1 - f0) // SIXTHS

    def job(self, slot, l):
        jobs, notes = [], []
        pieces = [(k, l + dl, f0, f1) for k, dl, f0, f1 in GATHER_PLAN.get(slot, []) if l + dl < DEPTH]
        if pieces:
            jobs.append(gather_job([((k, ll), self.shards[ll][k], self.gathered[ll][k],
                                     *self._rows(self.shards[ll][k].shape[0], f0, f1)) for k, ll, f0, f1 in pieces]))
            notes.append(("gather", list(dict.fromkeys((k, ll) for k, ll, _, _ in pieces))))
        if slot in SIBLING_PLAN and l + SIBLING_PLAN[slot][1] < DEPTH:
            group, dl = SIBLING_PLAN[slot]
            keys = [(k, l + dl) for k in GROUPS[group]]
            jobs.append(sibling_exchange_job([self.views[key] for key in keys]))
            notes.append(("sibling", keys))
        pieces = [(k, l + dl, f0, f1) for k, dl, f0, f1 in CHIP_PLAN.get(slot, []) if l + dl < DEPTH]
        if pieces:
            whole = [(*COLUMN_HALF.get(k, (k, 0)), k, ll, f0, f1) for k, ll, f0, f1 in pieces]
            jobs.append(chip_exchange_job([(self.partial[k, ll], self.contrib[kind], kind, ll,
                                            *self._rows(self.partial[k, ll].shape[1], f0, f1), col0, self.shards[ll][kind].shape[1])
                                           for kind, col0, k, ll, f0, f1 in whole]))
            notes.append(("chips", list(dict.fromkeys(kind for kind, *_ in whole))))
        if slot in SMALL_GATHER_PLAN and l + SMALL_GATHER_PLAN[slot] < DEPTH:
            ll = l + SMALL_GATHER_PLAN[slot]
            jobs.append(gather_job([("small", self.small_packs[ll], None, 0, P_ROWS)]))
            notes.append(("small", ll))
        job, spans = merge_jobs(jobs)
        self._open = (slot, l, notes, spans)
        return job

    def done(self, slot, l, results):
        open_slot, open_l, notes, spans = self._open
        assert (open_slot, open_l) == (slot, l)
        for (what, keys), (r0, r1) in zip(notes, spans):
            res = results[r0:r1]
            if what == "gather":
                for (k, ll), g in zip(keys, res):
                    self.gathered[ll][k] = g
            elif what == "sibling":
                sums = add_partials([self.views[key] for key in keys], list(res), self.core, f"chip_sum_{keys[0][0]}{keys[0][1]}")
                self.partial.update(zip(keys, sums))
            elif what == "chips":
                for k, c in zip(keys, res):
                    self.contrib[k] = c
            else:
                self.small_gathered[keys], = res


SMALL = ("norm1_g", "conv_a_w", "conv_a_b", "lru_wx", "lru_bx", "lru_wa", "lru_ba", "lru_lambda", "conv_b_w", "sinks",
         "conv_d_w", "conv_d_b", "ln_d_g", "ln_d_b", "norm2_g", "final_g")
WEIGHTS = ("norm1_g", "w_in", "conv_a_w", "conv_a_b", "lru_wx", "lru_bx", "lru_wa", "lru_ba", "lru_lambda", "w_a_out",
           "conv_b_w", "w_b_out", "sinks", "w_c_out", "conv_d_w", "conv_d_b", "ln_d_g", "ln_d_b", "w_d_out", "w_o",
           "norm2_g", "w_ffn_gate", "w_ffn_up", "w_ffn_down", "final_g")


def kernel(x, norm1_g, w_in, conv_a_w, conv_a_b, lru_wx, lru_bx, lru_wa, lru_ba, lru_lambda, w_a_out, conv_b_w, w_b_out, sinks, w_c_out, conv_d_w, conv_d_b, ln_d_g, ln_d_b, w_d_out, w_o, norm2_g, w_ffn_gate, w_ffn_up, w_ffn_down, final_g, loss_target, m_norm1_g, m_w_in, m_conv_a_w, m_conv_a_b, m_lru_wx, m_lru_bx, m_lru_wa, m_lru_ba, m_lru_lambda, m_w_a_out, m_conv_b_w, m_w_b_out, m_sinks, m_w_c_out, m_conv_d_w, m_conv_d_b, m_ln_d_g, m_ln_d_b, m_w_d_out, m_w_o, m_norm2_g, m_w_ffn_gate, m_w_ffn_up, m_w_ffn_down, m_final_g, v_norm1_g, v_w_in, v_conv_a_w, v_conv_a_b, v_lru_wx, v_lru_bx, v_lru_wa, v_lru_ba, v_lru_lambda, v_w_a_out, v_conv_b_w, v_w_b_out, v_sinks, v_w_c_out, v_conv_d_w, v_conv_d_b, v_ln_d_g, v_ln_d_b, v_w_d_out, v_w_o, v_norm2_g, v_w_ffn_gate, v_w_ffn_up, v_w_ffn_down, v_final_g):
    args = dict(locals())
    w = {n: args[n] for n in WEIGHTS}
    m = {n: args["m_" + n] for n in WEIGHTS}
    v = {n: args["v_" + n] for n in WEIGHTS}
    me = _dev_index(*_mesh_pos())

    def rows_major(a, how):
        return jnp.swapaxes(a, 1, 2) if how == "view" else a

    stacked = {k: cast_transpose(w[n], "prep_" + k) if how == "transpose" else rows_major(w[n], how).astype(BF16)
               for k, (n, how) in BIG.items()}
    plan = Overlap([{k: stacked[k][l] for k in BIG} for l in range(DEPTH)], lax.axis_index("c").astype(jnp.int32).reshape(1))
    convs = jnp.pad(_stack_convs(w).reshape(DEPTH * CW_ROWS, BW // NDEV), ((0, 0), (0, 256 - BW // NDEV)))
    g_in0, g_conv = _comm_only(gather_job([(("in_t", 0), plan.shards[0]["in_t"], None, 0, plan.shards[0]["in_t"].shape[0]),
                                           ("convs", convs, None, 0, convs.shape[0])]), "gather_first")
    plan.gathered[0]["in_t"] = g_in0
    convw = g_conv[:, :BW // NDEV].reshape(NDEV, DEPTH, CW_ROWS, BW // NDEV).transpose(1, 2, 0, 3).reshape(DEPTH, CW_ROWS, BW)

    vecs = _stack_vecs(w)
    head_stats, grad_x, grads = local_step(x[0], loss_target[0], norm1_g, norm2_g, final_g, convw, vecs, lru_wx, lru_wa, plan)


    out = {}
    for k in ("down", "gate_t", "up_t", "o", "a_t", "b_t", "c_t", "d_t", "in_t"):
        n, how = BIG[k]
        res, cres = adamw_big(plan.contrib[k], rows_major(w[n], how), rows_major(m[n], how), rows_major(v[n], how),
                              how == "transpose", "adamw_" + k, comm=plan.job("adamw_" + k, 0))
        plan.done("adamw_" + k, 0, cres)
        out[n] = [rows_major(r, how) for r in res]

    def own_shapes(p):
        return {n: p[n].reshape(1, D) if n == "final_g" else p[n] for n in SMALL}

    loss, small = adamw_small([g.reshape(NDEV, P_ROWS, D) for g in plan.small_gathered], me.astype(jnp.int32).reshape(1),
                              own_shapes(w), own_shapes(m), own_shapes(v))
    for n in SMALL:
        out[n] = [r.reshape(w[n].shape) for r in small[n]]
    loss = loss[0, 0]
    return (loss, grad_x[None], *[out[n][0] for n in WEIGHTS], *[out[n][1] for n in WEIGHTS],
            *[out[n][2] for n in WEIGHTS], *[out[n][3] for n in WEIGHTS])
```

```python
import functools

import jax
import jax.numpy as jnp
from jax import lax
from jax.experimental import pallas as pl
from jax.experimental.pallas import tpu as pltpu

F32 = jnp.float32
BF16 = jnp.bfloat16
E = pl.Element

D = 1024
BW = 512
IN_W = 8448
GL0 = 4352
FF = 2816
N_HEADS = 8
N_KV = 2
HD = 64
ATT_BLK = 128
EPS = 1e-6
LRU_C = 8.0
NEG_INF = -1e30
DEPTH = 2
NDEV = 8
CONV_A, CONV_B, CONV_D = 4, 3, 31
C_AX, C_AG, C_BV, C_BC, C_BB, C_Q, C_K, C_V, C_D1, C_D2 = 0, 512, 1024, 1536, 2048, 2560, 3072, 3200, 3328, 3840
CW_A, CW_B, CW_D, CW_ROWS = 0, 4, 8, 40
V_CAB, V_BX, V_BA, V_LAM, V_CDB, V_LNG, V_LNB, V_SINK, V_ROWS = 0, 1, 2, 3, 4, 5, 6, 7, 8
HALO = 32
W_IN_PARTS = ((0, 768), (768, 256))

ADAM_LR, ADAM_B1, ADAM_B2, ADAM_EPS, ADAM_WD, ADAM_STEP = 0.001, 0.9, 0.999, 1e-08, 0.01, 10

VMEM_LIMIT = 56 * 1024 * 1024

_NN = (((1,), (0,)), ((), ()))
_NT = (((1,), (1,)), ((), ()))
_TN = (((0,), (0,)), ((), ()))


def _dot(a, b, dims):
    return lax.dot_general(a.astype(BF16), b.astype(BF16), dims, preferred_element_type=F32)


def _cparams(n_axes):
    return pltpu.CompilerParams(dimension_semantics=("arbitrary",) * n_axes, vmem_limit_bytes=VMEM_LIMIT)


def _sds(shape, dtype):
    return jax.ShapeDtypeStruct(tuple(shape), dtype)


def _sigmoid(x):
    return jax.nn.sigmoid(x)


def _neg_expm1(x):
    p = x * (1.0 + x * (0.5 + x * (1.0 / 6.0 + x * (1.0 / 24.0 + x * (1.0 / 120.0)))))
    return jnp.where(x > -0.1, -p, 1.0 - jnp.exp(x))


def _softplus(z):
    return jnp.maximum(z, 0.0) + jnp.log1p(jnp.exp(-jnp.abs(z)))


def _gelu_and_grad(x):
    c = 0.7978845608028654
    inner = c * (x + 0.044715 * x * x * x)
    t = jnp.tanh(inner)
    g = 0.5 * x * (1.0 + t)
    dg = 0.5 * (1.0 + t) + 0.5 * x * (1.0 - t * t) * c * (1.0 + 3.0 * 0.044715 * x * x)
    return g, dg


ANY = pl.BlockSpec(memory_space=pl.ANY)
MESH = pl.DeviceIdType.MESH


def _mesh_pos():
    return lax.axis_index("x"), lax.axis_index("y"), lax.axis_index("c")


def _dev_index(px, py, pc):
    return 4 * px + 2 * py + pc


class CommJob:
    def __init__(self, inputs, aliases, out_shapes, sem_shapes, start, finish, relay=None):
        self.inputs, self.aliases, self.out_shapes, self.sem_shapes = list(inputs), dict(aliases), list(out_shapes), list(sem_shapes)
        self.start, self.finish, self.relay = start, finish, relay


def _call(body, comm, args, *, grid, in_specs, out_specs, out_shape, scratch_shapes=(), name, aliases=None):
    single = not isinstance(out_shape, (list, tuple))
    out_specs = [out_specs] if single else list(out_specs)
    out_shape = [out_shape] if single else list(out_shape)
    scratch_shapes = list(scratch_shapes)
    n_in, n_out, n_scr, n_axes = len(in_specs), len(out_shape), len(scratch_shapes), len(grid)
    params = pltpu.CompilerParams(dimension_semantics=("arbitrary",) * n_axes, vmem_limit_bytes=VMEM_LIMIT)
    io_aliases = dict(aliases or {})
    if comm is None:
        outs = pl.pallas_call(body, grid=grid, in_specs=in_specs, out_specs=out_specs, out_shape=out_shape,
                              scratch_shapes=scratch_shapes, input_output_aliases=io_aliases, compiler_params=params,
                              name=name)(*args)
        return (outs[0] if single else outs), []
    c_in, c_out = len(comm.inputs), len(comm.out_shapes)
    io_aliases.update({n_in + i: n_out + o for i, o in comm.aliases.items()})

    def wrapped(*refs):
        ins, cins = refs[:n_in], refs[n_in:n_in + c_in]
        outs = refs[n_in + c_in:n_in + c_in + n_out]
        couts = refs[n_in + c_in + n_out:n_in + c_in + n_out + c_out]
        rest = refs[n_in + c_in + n_out + c_out:]
        scr, sems = rest[:n_scr], rest[n_scr:]
        first = functools.reduce(lambda a, b: a & b, [pl.program_id(a) == 0 for a in range(n_axes)])
        last = functools.reduce(lambda a, b: a & b, [pl.program_id(a) == pl.num_programs(a) - 1 for a in range(n_axes)])

        @pl.when(first)
        def _():
            comm.start(cins, couts, sems)

        if comm.relay is not None:
            step = functools.reduce(lambda a, b: a * grid[b] + pl.program_id(b), range(1, n_axes), pl.program_id(0))
            n_steps = functools.reduce(lambda a, b: a * b, grid)

            @pl.when(step == 2 * n_steps // 3)
            def _():
                comm.relay(cins, couts, sems)

        body(*ins, *outs, *scr)

        @pl.when(last)
        def _():
            comm.finish(cins, couts, sems)

    outs = pl.pallas_call(
        wrapped, grid=grid, in_specs=list(in_specs) + [ANY] * c_in, out_specs=out_specs + [ANY] * c_out,
        out_shape=out_shape + comm.out_shapes, scratch_shapes=scratch_shapes + comm.sem_shapes,
        input_output_aliases=io_aliases, compiler_params=params, name=name)(*args, *comm.inputs)
    res, cres = outs[:n_out], outs[n_out:]
    return (res[0] if single else res), cres


def _comm_only(comm, name):
    c_in, c_out = len(comm.inputs), len(comm.out_shapes)

    def body(*refs):
        cins, couts, sems = refs[:c_in], refs[c_in:c_in + c_out], refs[c_in + c_out:]
        comm.start(cins, couts, sems)
        if comm.relay is not None:
            comm.relay(cins, couts, sems)
        comm.finish(cins, couts, sems)

    return pl.pallas_call(body, in_specs=[ANY] * c_in, out_specs=[ANY] * c_out, out_shape=comm.out_shapes,
                          scratch_shapes=comm.sem_shapes, input_output_aliases=comm.aliases, name=name)(*comm.inputs)


def gather_job(pieces):
    inputs, aliases, out_shapes, plan, where = [], {}, [], [], {}
    for key, shard, gathered, row0, nrows in pieces:
        if key not in where:
            where[key] = (len(inputs), len(out_shapes))
            inputs.append(shard)
            if gathered is not None:
                aliases[len(inputs)] = len(out_shapes)
                inputs.append(gathered)
            out_shapes.append(_sds((NDEV * shard.shape[0], shard.shape[1]), shard.dtype))
        plan.append((*where[key], shard.shape[0], row0, nrows))
    n = len(plan)

    def copies(cins, couts, sems):
        send_sems, recv_sems, local_sems = sems
        x, y, c = _mesh_pos()
        me, sibling = (x, y, c), (x, y, 1 - c)
        xn, yn, dg = (1 - x, y), (x, 1 - y), (1 - x, 1 - y)
        local, first, pass1, pass2, got_ici, got_fwd, got_d2d = [], [], [], [], [], [], []
        for p, (i_shard, i_out, rows, row0, nrows) in enumerate(plan):
            src = cins[i_shard].at[pl.ds(row0, nrows), :]
            half = cins[i_shard].shape[1] // 2
            left, right, whole = pl.ds(0, half), pl.ds(half, half), slice(None)

            def slot(dev, lanes, i_out=i_out, rows=rows, row0=row0, nrows=nrows):
                return couts[i_out].at[pl.ds(_dev_index(*dev) * rows + row0, nrows), lanes]

            def copy(g, dev, to, lanes=whole, src=None, p=p, slot=slot):
                return pltpu.make_async_remote_copy(
                    src_ref=slot(dev, lanes) if src is None else src, dst_ref=slot(dev, lanes),
                    send_sem=send_sems.at[g, p], recv_sem=recv_sems.at[g, p], device_id=to, device_id_type=MESH)

            local.append(pltpu.make_async_copy(src, slot(me, whole), local_sems.at[p]))
            first += [copy(0, me, sibling, src=src), copy(1, me, (*xn, c), src=src), copy(2, me, (*yn, c), src=src)]
            got_ici += [copy(1, (*xn, c), me), copy(2, (*yn, c), me)]
            pass1 += [copy(3, (*xn, c), (*yn, c), left), copy(4, (*yn, c), (*xn, c), right),
                      copy(5, (*xn, c), sibling), copy(6, (*yn, c), sibling)]
            got_fwd += [copy(3, (*dg, c), me, left), copy(4, (*dg, c), me, right)]
            pass2 += [copy(7, (*dg, c), sibling, left), copy(8, (*dg, c), sibling, right)]
            got_d2d += [copy(0, sibling, me), copy(5, (*xn, 1 - c), me), copy(6, (*yn, 1 - c), me),
                        copy(7, (*dg, 1 - c), me, left), copy(8, (*dg, 1 - c), me, right)]
        return local, first, pass1, pass2, got_ici, got_fwd, got_d2d

    def start(cins, couts, sems):
        local, first, *_ = copies(cins, couts, sems)
        for cp in local + first:
            cp.start()

    def pass_on(cins, couts, sems):
        _, _, pass1, _, got_ici, _, _ = copies(cins, couts, sems)
        for cp in got_ici:
            cp.wait_recv()
        for cp in pass1:
            cp.start()

    def finish(cins, couts, sems):
        local, first, pass1, pass2, _, got_fwd, got_d2d = copies(cins, couts, sems)
        for cp in got_fwd:
            cp.wait_recv()
        for cp in pass2:
            cp.start()
        for cp in got_d2d:
            cp.wait_recv()
        for cp in first + pass1 + pass2:
            cp.wait_send()
        for cp in local:
            cp.wait()

    sem_shapes = [pltpu.SemaphoreType.DMA((9, n)), pltpu.SemaphoreType.DMA((9, n)), pltpu.SemaphoreType.DMA((n,))]
    return CommJob(inputs, aliases, out_shapes, sem_shapes, start, finish, relay=pass_on)


def sibling_exchange_job(grads):
    n = len(grads)

    def copies(cins, couts, sems):
        send_sems, recv_sems = sems
        x, y, c = _mesh_pos()
        return [pltpu.make_async_remote_copy(
            src_ref=cins[q].at[:, 1 - c], dst_ref=couts[q], send_sem=send_sems.at[q], recv_sem=recv_sems.at[q],
            device_id=(x, y, 1 - c), device_id_type=MESH) for q in range(n)]

    def start(cins, couts, sems):
        for cp in copies(cins, couts, sems):
            cp.start()

    def finish(cins, couts, sems):
        cps = copies(cins, couts, sems)
        for cp in cps:
            cp.wait_recv()
        for cp in cps:
            cp.wait_send()

    return CommJob(grads, {}, [_sds((4,) + g.shape[2:], g.dtype) for g in grads],
                   [pltpu.SemaphoreType.DMA((n,)), pltpu.SemaphoreType.DMA((n,))], start, finish)


def chip_exchange_job(pieces):
    inputs, aliases, out_shapes, plan, where = [], {}, [], [], {}
    for partial, contrib, key, layer, row0, nrows, col0, cols in pieces:
        if key not in where:
            where[key] = len(out_shapes)
            out_shapes.append(_sds((4, DEPTH, partial.shape[1], cols), partial.dtype))
            if contrib is not None:
                aliases[len(inputs)] = where[key]
                inputs.append(contrib)
        plan.append((len(inputs), where[key], layer, row0, nrows, col0, partial.shape[2]))
        inputs.append(partial)
    n = len(plan)

    def copies(cins, couts, sems):
        send_sems, recv_sems, local_sems = sems
        x, y, c = _mesh_pos()
        mine = 2 * x + y
        local, sends, recvs = [], [], []
        for p, (i_in, i_out, layer, row0, nrows, col0, ncols) in enumerate(plan):
            rows, lanes = pl.ds(row0, nrows), pl.ds(col0, ncols)
            local.append(pltpu.make_async_copy(cins[i_in].at[mine, rows, :], couts[i_out].at[mine, layer, rows, lanes],
                                               local_sems.at[p]))
            for j, (cx, cy) in enumerate([(1 - x, y), (x, 1 - y), (1 - x, 1 - y)]):
                theirs = 2 * cx + cy

                def copy(slot_there, j=j, p=p, cx=cx, cy=cy, theirs=theirs, i_in=i_in, i_out=i_out, layer=layer,
                         rows=rows, lanes=lanes):
                    return pltpu.make_async_remote_copy(
                        src_ref=cins[i_in].at[theirs, rows, :], dst_ref=couts[i_out].at[slot_there, layer, rows, lanes],
                        send_sem=send_sems.at[j, p], recv_sem=recv_sems.at[j, p], device_id=(cx, cy, c), device_id_type=MESH)
                sends.append(copy(mine))
                recvs.append(copy(theirs))
        return local, sends, recvs

    def start(cins, couts, sems):
        local, sends, _ = copies(cins, couts, sems)
        for cp in local + sends:
            cp.start()

    def finish(cins, couts, sems):
        local, sends, recvs = copies(cins, couts, sems)
        for cp in recvs:
            cp.wait_recv()
        for cp in sends:
            cp.wait_send()
        for cp in local:
            cp.wait()

    sem_shapes = [pltpu.SemaphoreType.DMA((3, n)), pltpu.SemaphoreType.DMA((3, n)), pltpu.SemaphoreType.DMA((n,))]
    return CommJob(inputs, aliases, out_shapes, sem_shapes, start, finish)


def fwd_proj(x, g1, wt_in, l, comm=None):
    s = x.shape[0]
    tm = min(512, s)
    tn = 1408

    def body(x_ref, g_ref, w_ref, o_ref, xn_ref):
        @pl.when(pl.program_id(1) == 0)
        def _():
            xv = x_ref[...]
            r = lax.rsqrt(jnp.mean(xv * xv, axis=-1, keepdims=True) + EPS)
            xn_ref[...] = (xv * r * g_ref[l:l + 1, :]).astype(BF16)

        o_ref[...] = _dot(xn_ref[...], w_ref[...], _NT).astype(BF16)

    return _call(
        body, comm, (x, g1, wt_in), grid=(s // tm, IN_W // tn),
        in_specs=[pl.BlockSpec((tm, D), lambda i, j: (i, 0)),
                  pl.BlockSpec((DEPTH, D), lambda i, j: (0, 0)),
                  pl.BlockSpec((tn, D), lambda i, j: (j, 0))],
        out_specs=pl.BlockSpec((tm, tn), lambda i, j: (i, j)),
        out_shape=_sds((s, IN_W), BF16),
        scratch_shapes=[pltpu.VMEM((tm, D), BF16)], name=f"fwd_proj{l}")


def _scan_fwd(a_ref, u_ref, h_ref, h0, n_rows):
    row = lax.broadcasted_iota(jnp.int32, (8, BW), 0)

    def body(g, hprev):
        r = pl.multiple_of(g * 8, 8)
        a = a_ref[pl.ds(r, 8), :]
        u = u_ref[pl.ds(r, 8), :]
        for sft in (1, 2, 4):
            a_sh = jnp.where(row >= sft, pltpu.roll(a, sft, 0), 1.0)
            u_sh = jnp.where(row >= sft, pltpu.roll(u, sft, 0), 0.0)
            u = u + a * u_sh
            a = a * a_sh
        h = u + a * hprev
        h_ref[pl.ds(r, 8), :] = h
        return h[7:8, :]

    return lax.fori_loop(0, n_rows // 8, body, h0)


def _scan_bwd(b_ref, g_ref, o_ref, c0, n_rows):
    row = lax.broadcasted_iota(jnp.int32, (8, BW), 0)

    def body(k, cnext):
        r = pl.multiple_of((n_rows // 8 - 1 - k) * 8, 8)
        b = b_ref[pl.ds(r, 8), :]
        g = g_ref[pl.ds(r, 8), :]
        for sft in (1, 2, 4):
            b_sh = jnp.where(row < 8 - sft, pltpu.roll(b, 8 - sft, 0), 1.0)
            g_sh = jnp.where(row < 8 - sft, pltpu.roll(g, 8 - sft, 0), 0.0)
            g = g + b * g_sh
            b = b * b_sh
        o = g + b * cnext
        o_ref[pl.ds(r, 8), :] = o
        return o[0:1, :]

    return lax.fori_loop(0, n_rows // 8, body, c0)


def _shifted_copies(buf, shifted, n_rows):
    for r in range(1, 8):
        shifted[r - 1, 0:n_rows - 8, :] = buf[pl.ds(r, n_rows - 8), :]


def _window(buf, shifted, off, t):
    r = off % 8
    return buf[pl.ds(off, t), :] if r == 0 else shifted[r - 1, pl.ds(off - r, t), :]


def _branch_fwd_math(cur_ref, halo_ref, cw_ref, vec_ref, wx_ref, wa_ref, bufa, bufb, bufd, xd, first, t, saved_ref=None):
    def halo(c0):
        v = halo_ref[:, c0:c0 + BW].astype(F32)
        return jnp.where(first, 0.0, v)

    def cur(c0):
        return cur_ref[:, c0:c0 + BW].astype(F32)

    out = {}
    bufa[0:HALO, :] = halo(C_AX)
    bufa[HALO:HALO + t, :] = cur(C_AX)
    ca = jnp.zeros((t, BW), F32) + vec_ref[V_CAB:V_CAB + 1, :]
    for k in range(CONV_A):
        ca = ca + cw_ref[CW_A + k:CW_A + k + 1, :] * bufa[pl.ds(HALO - (CONV_A - 1) + k, t), :]
    if saved_ref is None:
        gi = _sigmoid(_dot(ca, wx_ref[...], _NN) + vec_ref[V_BX:V_BX + 1, :])
        gr = _sigmoid(_dot(ca, wa_ref[...], _NN) + vec_ref[V_BA:V_BA + 1, :])
    else:
        gi, gr = saved_ref[:, BW:2 * BW], saved_ref[:, 2 * BW:3 * BW]
    sp = _softplus(-vec_ref[V_LAM:V_LAM + 1, :])
    la = -LRU_C * sp * gr
    a = jnp.exp(la)
    mult = jnp.sqrt(_neg_expm1(2.0 * la))
    out.update(ca=ca, gi=gi, gr=gr, sp=sp, a=a, mult=mult)
    bufb[0:HALO, :] = halo(C_BC) * halo(C_BV)
    bufb[HALO:HALO + t, :] = cur(C_BC) * cur(C_BV)
    cb = jnp.zeros((t, BW), F32)
    for k in range(CONV_B):
        cb = cb + cw_ref[CW_B + k:CW_B + k + 1, :] * bufb[pl.ds(HALO - (CONV_B - 1) + k, t), :]
    out.update(cb=cb)
    bufd[0:HALO, :] = halo(C_D1) * _sigmoid(halo(C_D2))
    s2 = _sigmoid(cur(C_D2))
    bufd[HALO:HALO + t, :] = cur(C_D1) * s2
    _shifted_copies(bufd, xd, t + HALO)
    if saved_ref is None:
        cd = jnp.zeros((t, BW), F32) + vec_ref[V_CDB:V_CDB + 1, :]
        for k in range(CONV_D):
            cd = cd + cw_ref[CW_D + k:CW_D + k + 1, :] * _window(bufd, xd, HALO - (CONV_D - 1) + k, t)
    else:
        cd = saved_ref[:, 0:BW]
    mu = jnp.mean(cd, axis=-1, keepdims=True)
    xc = cd - mu
    rstd = lax.rsqrt(jnp.mean(xc * xc, axis=-1, keepdims=True) + EPS)
    xh = xc * rstd
    ln = xh * vec_ref[V_LNG:V_LNG + 1, :] + vec_ref[V_LNB:V_LNB + 1, :]
    out.update(s2=s2, xh=xh, rstd=rstd, ln=ln, cd=cd)
    return out


def fwd_branch(proj, convw, vecs, wx_bd, wa_bd, l, comm=None):
    s = proj.shape[0]
    t = min(256, s)

    def body(cur_ref, halo_ref, cw_ref, vec_ref, wx_ref, wa_ref, pre_ref, h_ref, sv_ref, bufa, bufb, bufd, xd, a_s, u_s, hcar):
        first = pl.program_id(0) == 0

        @pl.when(first)
        def _():
            hcar[...] = jnp.zeros((1, BW), F32)

        v = _branch_fwd_math(cur_ref, halo_ref, cw_ref, vec_ref, wx_ref, wa_ref, bufa, bufb, bufd, xd, first, t)
        a_s[...] = v["a"]
        u_s[...] = v["ca"] * v["gi"] * v["mult"]
        sv_ref[:, 0:BW] = v["cd"]
        sv_ref[:, BW:2 * BW] = v["gi"]
        sv_ref[:, 2 * BW:3 * BW] = v["gr"]
        hcar[...] = _scan_fwd(a_s, u_s, h_ref, hcar[...], t)
        gg, _ = _gelu_and_grad(cur_ref[:, C_AG:C_AG + BW].astype(F32))
        pre_ref[:, 0:BW] = (h_ref[...] * gg).astype(BF16)
        pre_ref[:, BW:2 * BW] = (cur_ref[:, C_BB:C_BB + BW].astype(F32) * v["cb"]).astype(BF16)
        ln = v["ln"]
        pre_ref[:, 2 * BW:3 * BW] = (ln * _sigmoid(ln)).astype(BF16)

    hb = t // HALO
    return _call(
        body, comm, (proj, proj, convw, vecs, wx_bd, wa_bd), grid=(s // t,),
        in_specs=[pl.BlockSpec((t, GL0), lambda i: (i, 0)),
                  pl.BlockSpec((HALO, GL0), lambda i: (jnp.maximum(i * hb - 1, 0), 0)),
                  pl.BlockSpec((None, CW_ROWS, BW), lambda i: (l, 0, 0)),
                  pl.BlockSpec((None, V_ROWS, BW), lambda i: (l, 0, 0)),
                  pl.BlockSpec((None, BW, BW), lambda i: (l, 0, 0)),
                  pl.BlockSpec((None, BW, BW), lambda i: (l, 0, 0))],
        out_specs=[pl.BlockSpec((t, 3 * BW), lambda i: (i, 0)), pl.BlockSpec((t, BW), lambda i: (i, 0)),
                   pl.BlockSpec((t, 3 * BW), lambda i: (i, 0))],
        out_shape=[_sds((s, 3 * BW), BF16), _sds((s, BW), F32), _sds((s, 3 * BW), F32)],
        scratch_shapes=[pltpu.VMEM((t + HALO, BW), F32)] * 3 + [pltpu.VMEM((7, t + HALO - 8, BW), F32)]
        + [pltpu.VMEM((t, BW), F32)] * 2 + [pltpu.VMEM((1, BW), F32)],
        name=f"fwd_branch{l}")


GRP = N_HEADS // N_KV


ATT_SUB = 2


def _attn_mask_bias(first_block):
    shape = (GRP * ATT_BLK, 2 * ATT_BLK)
    qi = lax.broadcasted_iota(jnp.int32, shape, 0) & (ATT_BLK - 1)
    ki = lax.broadcasted_iota(jnp.int32, shape, 1)
    dist = qi + ATT_BLK - ki
    valid = (dist >= 0) & (dist < ATT_BLK)
    if first_block is not None:
        valid = valid & (jnp.logical_not(first_block) | (ki >= ATT_BLK))
    return dist.astype(F32), valid


def _attn_units(q_ref, kvp_ref, kvc_ref, first_step):
    units = []
    for b in range(ATT_SUB):
        rows = slice(b * ATT_BLK, (b + 1) * ATT_BLK)
        if b == 0:
            prev = lambda c0, c1: kvp_ref[:, c0:c1]
        else:
            prev = lambda c0, c1, b=b: kvc_ref[(b - 1) * ATT_BLK:b * ATT_BLK, c0:c1]
        for hk in range(N_KV):
            units.append(dict(b=b, hk=hk, rows=rows, q=lambda c0, c1, rows=rows: q_ref[rows, c0:c1], prev=prev,
                              cur=lambda c0, c1, rows=rows: kvc_ref[rows, c0:c1], first=first_step if b == 0 else None))
    return units


def _per_head(hk, values):
    hl = lax.broadcasted_iota(jnp.int32, (GRP * ATT_BLK, 1), 0) // ATT_BLK
    out = values[GRP - 1]
    for j in range(GRP - 2, -1, -1):
        out = jnp.where(hl == j, values[j], out)
    return out


def _attn_probs(units, vec_ref):
    us = range(len(units))
    heads = [range(u["hk"] * GRP, (u["hk"] + 1) * GRP) for u in units]
    masks = {id(u["first"]): _attn_mask_bias(u["first"]) for u in units}
    distf = [masks[id(u["first"])][0] for u in units]
    valid = [masks[id(u["first"])][1] for u in units]
    q4 = [jnp.concatenate([units[i]["q"](h * HD, (h + 1) * HD) for h in heads[i]], axis=0) for i in us]
    kcol = [(u["hk"] * HD, (u["hk"] + 1) * HD) for u in units]
    vcol = [((N_KV + u["hk"]) * HD, (N_KV + u["hk"] + 1) * HD) for u in units]
    k2 = [jnp.concatenate([units[i]["prev"](*kcol[i]), units[i]["cur"](*kcol[i])], axis=0) for i in us]
    v2 = [jnp.concatenate([units[i]["prev"](*vcol[i]), units[i]["cur"](*vcol[i])], axis=0) for i in us]
    slope = [_per_head(units[i]["hk"], [2.0 ** (-8.0 * (h + 1) / N_HEADS) for h in heads[i]]) for i in us]
    sink = [_per_head(units[i]["hk"], [vec_ref[V_SINK:V_SINK + 1, h:h + 1] for h in heads[i]]) for i in us]
    sc = [_dot(q4[i], k2[i], _NT) for i in us]
    sc = [jnp.where(valid[i], sc[i] * (HD ** -0.5) - slope[i] * distf[i], NEG_INF) for i in us]
    m = [jnp.maximum(jnp.max(sc[i], axis=-1, keepdims=True), sink[i]) for i in us]
    p = [jnp.exp(sc[i] - m[i]) for i in us]
    es = [jnp.exp(sink[i] - m[i]) for i in us]
    inv = [1.0 / (jnp.sum(p[i], axis=-1, keepdims=True) + es[i]) for i in us]
    return [(q4[i], k2[i], v2[i], p[i] * inv[i], es[i] * inv[i]) for i in us]


def fwd_attn(proj, vecs, l, comm=None):
    s = proj.shape[0]
    t = ATT_SUB * ATT_BLK

    def body(q_ref, kvp_ref, kvc_ref, vec_ref, o_ref):
        units = _attn_units(q_ref, kvp_ref, kvc_ref, pl.program_id(0) == 0)
        groups = _attn_probs(units, vec_ref)
        outs = [_dot(p, v2, _NN).astype(BF16) for _, _, v2, p, _ in groups]
        for u, out in zip(units, outs):
            for j in range(GRP):
                h = u["hk"] * GRP + j
                o_ref[u["rows"], h * HD:(h + 1) * HD] = out[j * ATT_BLK:(j + 1) * ATT_BLK]

    return _call(
        body, comm, (proj, proj, proj, vecs), grid=(s // t,),
        in_specs=[pl.BlockSpec((t, BW), lambda i: (i, C_Q // BW)),
                  pl.BlockSpec((ATT_BLK, 256), lambda i: (jnp.maximum(ATT_SUB * i - 1, 0), C_K // 256)),
                  pl.BlockSpec((t, 256), lambda i: (i, C_K // 256)),
                  pl.BlockSpec((None, V_ROWS, BW), lambda i: (l, 0, 0))],
        out_specs=pl.BlockSpec((t, BW), lambda i: (i, 0)),
        out_shape=_sds((s, BW), BF16), name=f"fwd_attn{l}")


def fwd_merge(x, proj, pre_abd, pre_c, wt_a, wt_b, wt_c, wt_d, w_o, l, comm=None):
    s = x.shape[0]
    tm = min(256, s)

    def body(x_ref, gl_ref, pabd_ref, pc_ref, wa_ref, wb_ref, wc_ref, wd_ref, wo_ref, y_ref, mg_ref, h1_ref):
        pres = (pabd_ref[:, 0:BW], pabd_ref[:, BW:2 * BW], pc_ref[...], pabd_ref[:, 2 * BW:3 * BW])
        merged = jnp.zeros((tm, D), F32)
        for k, (pre, w_ref) in enumerate(zip(pres, (wa_ref, wb_ref, wc_ref, wd_ref))):
            yk = _dot(pre, w_ref[...], _NT)
            y_ref[:, k * D:(k + 1) * D] = yk.astype(BF16)
            merged = merged + _sigmoid(gl_ref[:, k * D:(k + 1) * D].astype(F32)) * yk
        mg_ref[...] = merged.astype(BF16)
        h1_ref[...] = x_ref[...] + _dot(merged, wo_ref[...], _NN)

    wspec = pl.BlockSpec((D, BW), lambda i: (0, 0))
    return _call(
        body, comm, (x, proj, pre_abd, pre_c, wt_a, wt_b, wt_c, wt_d, w_o), grid=(s // tm,),
        in_specs=[pl.BlockSpec((tm, D), lambda i: (i, 0)),
                  pl.BlockSpec((E(tm), E(4 * D)), lambda i: (i * tm, GL0)),
                  pl.BlockSpec((tm, 3 * BW), lambda i: (i, 0)),
                  pl.BlockSpec((tm, BW), lambda i: (i, 0)),
                  wspec, wspec, wspec, wspec,
                  pl.BlockSpec((D, D), lambda i: (0, 0))],
        out_specs=[pl.BlockSpec((tm, 4 * D), lambda i: (i, 0)), pl.BlockSpec((tm, D), lambda i: (i, 0)),
                   pl.BlockSpec((tm, D), lambda i: (i, 0))],
        out_shape=[_sds((s, 4 * D), BF16), _sds((s, D), BF16), _sds((s, D), F32)], name=f"fwd_merge{l}")


def fwd_ffn(h1, g2, wt_gate, wt_up, w_down, l, comm=None):
    s = h1.shape[0]
    tm = min(512, s)
    fc = FF // 2

    def body(h_ref, g_ref, wg_ref, wu_ref, wd_ref, xo_ref, fg_ref, fu_ref, hn_ref, acc_ref):
        j = pl.program_id(1)

        @pl.when(j == 0)
        def _():
            hv = h_ref[...]
            r = lax.rsqrt(jnp.mean(hv * hv, axis=-1, keepdims=True) + EPS)
            hn_ref[...] = (hv * r * g_ref[l:l + 1, :]).astype(BF16)
            acc_ref[...] = hv

        fg = _dot(hn_ref[...], wg_ref[...], _NT)
        fu = _dot(hn_ref[...], wu_ref[...], _NT)
        fg_ref[...] = fg.astype(BF16)
        fu_ref[...] = fu.astype(BF16)
        acc_ref[...] += _dot(fg * _sigmoid(fg) * fu, wd_ref[...], _NN)

        @pl.when(j == pl.num_programs(1) - 1)
        def _():
            xo_ref[...] = acc_ref[...]

    wspec = pl.BlockSpec((fc, D), lambda i, j: (j, 0))
    return _call(
        body, comm, (h1, g2, wt_gate, wt_up, w_down), grid=(s // tm, FF // fc),
        in_specs=[pl.BlockSpec((tm, D), lambda i, j: (i, 0)), pl.BlockSpec((DEPTH, D), lambda i, j: (0, 0)),
                  wspec, wspec, wspec],
        out_specs=[pl.BlockSpec((tm, D), lambda i, j: (i, 0)), pl.BlockSpec((tm, fc), lambda i, j: (i, j)),
                   pl.BlockSpec((tm, fc), lambda i, j: (i, j))],
        out_shape=[_sds((s, D), F32), _sds((s, FF), BF16), _sds((s, FF), BF16)],
        scratch_shapes=[pltpu.VMEM((tm, D), BF16), pltpu.VMEM((tm, D), F32)], name=f"fwd_ffn{l}")


def loss_head(x, gf, target):
    s = x.shape[0]
    tm = min(512, s)

    def body(x_ref, g_ref, t_ref, dx_ref, st_ref):
        @pl.when(pl.program_id(0) == 0)
        def _():
            st_ref[...] = jnp.zeros((8, D), F32)

        xv = x_ref[...]
        g = g_ref[...]
        r = lax.rsqrt(jnp.mean(xv * xv, axis=-1, keepdims=True) + EPS)
        n = xv * r
        err = n * g - t_ref[...]
        dy = err * (1.0 / D)
        dn = dy * g
        dx_ref[...] = r * (dn - n * jnp.mean(dn * n, axis=-1, keepdims=True))
        st_ref[0:1, :] += jnp.sum(dy * n, axis=0, keepdims=True)
        lsum = 0.5 * jnp.sum(jnp.mean(err * err, axis=-1, keepdims=True), axis=0, keepdims=True)
        st_ref[1:2, :] += jnp.broadcast_to(lsum, (1, D))

    return pl.pallas_call(
        body, grid=(s // tm,),
        in_specs=[pl.BlockSpec((tm, D), lambda i: (i, 0)), pl.BlockSpec((1, D), lambda i: (0, 0)),
                  pl.BlockSpec((tm, D), lambda i: (i, 0))],
        out_specs=[pl.BlockSpec((tm, D), lambda i: (i, 0)), pl.BlockSpec((8, D), lambda i: (0, 0))],
        out_shape=[_sds((s, D), F32), _sds((8, D), F32)],
        compiler_params=_cparams(1), name="loss_head")(x, gf, target)


def _edge_index(j, i, n_j, n_i):
    return jnp.where((j == 0) | (j == n_j - 1), i, n_i - 1)


def bwd_ffn(dxo, h1, fg, fu, g2, wt_gate, wt_up, w_down, l, comm=None):
    s = h1.shape[0]
    tm = min(512, s)
    fc = 256
    n_j, n_i = FF // fc, s // tm

    def body(dxo_ref, h_ref, fg_ref, fu_ref, g_ref, wg_ref, wu_ref, wd_ref,
             dh_ref, dwg_ref, dwu_ref, dwd_ref, st_ref, dhn, dxo_b, hn_b, ag, au, ad):
        j, i = pl.program_id(0), pl.program_id(1)
        rows = pl.ds(pl.multiple_of(i * tm, tm), tm)
        g = g_ref[l:l + 1, :]

        @pl.when(j == 0)
        def _():
            hv = h_ref[...]
            r = lax.rsqrt(jnp.mean(hv * hv, axis=-1, keepdims=True) + EPS)
            hn_b[rows, :] = (hv * r * g).astype(BF16)
            dxo_b[rows, :] = dxo_ref[...].astype(BF16)
            dhn[rows, :] = jnp.zeros((tm, D), F32)

        @pl.when((j == 0) & (i == 0))
        def _():
            st_ref[...] = jnp.zeros((8, D), F32)

        @pl.when(i == 0)
        def _():
            ag[...] = jnp.zeros((fc, D), F32)
            au[...] = jnp.zeros((fc, D), F32)
            ad[...] = jnp.zeros((fc, D), F32)

        fgv = fg_ref[...].astype(F32)
        fuv = fu_ref[...].astype(F32)
        sg = _sigmoid(fgv)
        sil = fgv * sg
        dxb = dxo_b[rows, :]
        hnb = hn_b[rows, :]
        d_act = _dot(dxb, wd_ref[...], _NT)
        ad[...] += _dot(sil * fuv, dxb, _TN)
        d_fg = (d_act * fuv * (sg * (1.0 + fgv * (1.0 - sg)))).astype(BF16)
        d_fu = (d_act * sil).astype(BF16)
        ag[...] += _dot(d_fg, hnb, _TN)
        au[...] += _dot(d_fu, hnb, _TN)
        dhn[rows, :] += _dot(d_fg, wg_ref[...], _NN) + _dot(d_fu, wu_ref[...], _NN)

        @pl.when(i == n_i - 1)
        def _():
            dwg_ref[...] = ag[...].astype(BF16)
            dwu_ref[...] = au[...].astype(BF16)
            dwd_ref[...] = ad[...].astype(BF16)

        @pl.when(j == n_j - 1)
        def _():
            hv = h_ref[...]
            r = lax.rsqrt(jnp.mean(hv * hv, axis=-1, keepdims=True) + EPS)
            n = hv * r
            dv = dhn[rows, :]
            dn = dv * g
            dh_ref[...] = dxo_ref[...] + r * (dn - n * jnp.mean(dn * n, axis=-1, keepdims=True))
            st_ref[0:1, :] += jnp.sum(dv * n, axis=0, keepdims=True)

    edge = lambda j, i: (_edge_index(j, i, n_j, n_i), 0)
    wspec = pl.BlockSpec((fc, D), lambda j, i: (j, 0))
    dwspec = pl.BlockSpec((fc, D), lambda j, i: (j, 0))
    return _call(
        body, comm, (dxo, h1, fg, fu, g2, wt_gate, wt_up, w_down), grid=(n_j, n_i),
        in_specs=[pl.BlockSpec((tm, D), edge),
                  pl.BlockSpec((tm, D), edge),
                  pl.BlockSpec((tm, fc), lambda j, i: (i, j)), pl.BlockSpec((tm, fc), lambda j, i: (i, j)),
                  pl.BlockSpec((DEPTH, D), lambda j, i: (0, 0)), wspec, wspec, wspec],
        out_specs=[pl.BlockSpec((tm, D), lambda j, i: (jnp.where(j == n_j - 1, i, 0), 0)),
                   dwspec, dwspec, dwspec, pl.BlockSpec((8, D), lambda j, i: (0, 0))],
        out_shape=[_sds((s, D), F32), _sds((FF, D), BF16), _sds((FF, D), BF16), _sds((FF, D), BF16), _sds((8, D), F32)],
        scratch_shapes=[pltpu.VMEM((s, D), F32), pltpu.VMEM((s, D), BF16), pltpu.VMEM((s, D), BF16),
                        pltpu.VMEM((fc, D), F32), pltpu.VMEM((fc, D), F32), pltpu.VMEM((fc, D), F32)],
        name=f"bwd_ffn{l}")


def bwd_merge(dh1, y4, proj, merged, pre_abd, pre_c, wt_a, wt_b, wt_c, wt_d, w_o, l, comm=None):
    s = dh1.shape[0]
    tm = min(256, s)
    n_i = s // tm

    def body(dh_ref, y_ref, gl_ref, mg_ref, pabd_ref, pc_ref, wa_ref, wb_ref, wc_ref, wd_ref, wo_ref,
             dgl_ref, dpre_ref, dwo_ref, dwa_ref, dwb_ref, dwc_ref, dwd_ref, ao, aa, ab, ac, ad):
        i = pl.program_id(0)
        accs = (aa, ab, ac, ad)

        @pl.when(i == 0)
        def _():
            ao[...] = jnp.zeros((D, D), F32)
            for acc in accs:
                acc[...] = jnp.zeros((D, BW), F32)

        dhb = dh_ref[...].astype(BF16)
        dmg = _dot(dhb, wo_ref[...], _NT)
        ao[...] += _dot(mg_ref[...], dhb, _TN)
        pres = (pabd_ref[:, 0:BW], pabd_ref[:, BW:2 * BW], pc_ref[...], pabd_ref[:, 2 * BW:3 * BW])
        for k, (pre, w_ref, acc) in enumerate(zip(pres, (wa_ref, wb_ref, wc_ref, wd_ref), accs)):
            gk = _sigmoid(gl_ref[:, k * D:(k + 1) * D].astype(F32))
            yk = y_ref[:, k * D:(k + 1) * D].astype(F32)
            dgl_ref[:, k * D:(k + 1) * D] = (dmg * yk * gk * (1.0 - gk)).astype(BF16)
            dyk = (dmg * gk).astype(BF16)
            dpre_ref[:, k * BW:(k + 1) * BW] = _dot(dyk, w_ref[...], _NN).astype(BF16)
            acc[...] += _dot(dyk, pre, _TN)

        @pl.when(i == n_i - 1)
        def _():
            dwo_ref[...] = ao[...].astype(BF16)
            for o_ref, acc in zip((dwa_ref, dwb_ref, dwc_ref, dwd_ref), accs):
                o_ref[...] = acc[...].astype(BF16)

    wspec = pl.BlockSpec((D, BW), lambda i: (0, 0))
    dwspec = pl.BlockSpec((D, BW), lambda i: (0, 0))
    return _call(
        body, comm, (dh1, y4, proj, merged, pre_abd, pre_c, wt_a, wt_b, wt_c, wt_d, w_o), grid=(n_i,),
        in_specs=[pl.BlockSpec((tm, D), lambda i: (i, 0)),
                  pl.BlockSpec((tm, 4 * D), lambda i: (i, 0)),
                  pl.BlockSpec((E(tm), E(4 * D)), lambda i: (i * tm, GL0)),
                  pl.BlockSpec((tm, D), lambda i: (i, 0)),
                  pl.BlockSpec((tm, 3 * BW), lambda i: (i, 0)),
                  pl.BlockSpec((tm, BW), lambda i: (i, 0)),
                  wspec, wspec, wspec, wspec,
                  pl.BlockSpec((D, D), lambda i: (0, 0))],
        out_specs=[pl.BlockSpec((E(tm), E(4 * D)), lambda i: (i * tm, GL0)),
                   pl.BlockSpec((tm, 4 * BW), lambda i: (i, 0)),
                   pl.BlockSpec((D, D), lambda i: (0, 0)), dwspec, dwspec, dwspec, dwspec],
        out_shape=[_sds((s, IN_W), BF16), _sds((s, 4 * BW), BF16), _sds((D, D), BF16)] + [_sds((D, BW), BF16)] * 4,
        scratch_shapes=[pltpu.VMEM((D, D), F32)] + [pltpu.VMEM((D, BW), F32)] * 4, name=f"bwd_merge{l}")


def bwd_attn(proj, dpre, vecs, l, comm=None):
    s = proj.shape[0]
    t = ATT_SUB * ATT_BLK
    grp = N_HEADS // N_KV

    def body(q_ref, kvp_ref, kvc_ref, do_ref, vec_ref, dq_ref, dkc_ref, dkp_ref, st_ref):
        @pl.when(pl.program_id(0) == 0)
        def _():
            st_ref[...] = jnp.zeros((8, 128), F32)

        lane = lax.broadcasted_iota(jnp.int32, (1, 128), 1)
        dsink = jnp.zeros((1, 128), F32)
        units = _attn_units(q_ref, kvp_ref, kvc_ref, pl.program_id(0) == 0)
        groups = _attn_probs(units, vec_ref)
        us = range(len(units))
        do4s = [jnp.concatenate([do_ref[u["rows"], h * HD:(h + 1) * HD] for h in range(u["hk"] * grp, (u["hk"] + 1) * grp)],
                                axis=0) for u in units]
        dps = [_dot(do4s[i], groups[i][2], _NT) for i in us]
        deltas = [jnp.sum(groups[i][3] * dps[i], axis=-1, keepdims=True) for i in us]
        dss = [groups[i][3] * (dps[i] - deltas[i]) * (HD ** -0.5) for i in us]
        dq4s = [_dot(dss[i], groups[i][1], _NN).astype(BF16) for i in us]
        dk2s = [_dot(dss[i], groups[i][0], _TN) for i in us]
        dv2s = [_dot(groups[i][3], do4s[i], _TN) for i in us]
        for i, u in enumerate(units):
            psd = groups[i][4] * deltas[i]
            for j in range(grp):
                h = u["hk"] * grp + j
                rows = slice(j * ATT_BLK, (j + 1) * ATT_BLK)
                dq_ref[u["rows"], h * HD:(h + 1) * HD] = dq4s[i][rows]
                dsink = dsink + jnp.where(lane == h, -jnp.sum(psd[rows], axis=0, keepdims=True), 0.0)
        for i, u in enumerate(units):
            nxt = [k for k, w in enumerate(units) if w["hk"] == u["hk"] and w["b"] == u["b"] + 1]
            for grad, c0 in ((dk2s, u["hk"] * HD), (dv2s, (N_KV + u["hk"]) * HD)):
                own = grad[i][ATT_BLK:]
                if nxt:
                    own = own + grad[nxt[0]][0:ATT_BLK]
                dkc_ref[u["rows"], c0:c0 + HD] = own.astype(BF16)
                if u["b"] == 0:
                    dkp_ref[:, c0:c0 + HD] = grad[i][0:ATT_BLK].astype(BF16)
        st_ref[0:1, :] += dsink

    return _call(
        body, comm, (proj, proj, proj, dpre, vecs), grid=(s // t,),
        in_specs=[pl.BlockSpec((t, BW), lambda i: (i, C_Q // BW)),
                  pl.BlockSpec((ATT_BLK, 256), lambda i: (jnp.maximum(ATT_SUB * i - 1, 0), C_K // 256)),
                  pl.BlockSpec((t, 256), lambda i: (i, C_K // 256)),
                  pl.BlockSpec((t, BW), lambda i: (i, 2)),
                  pl.BlockSpec((None, V_ROWS, BW), lambda i: (l, 0, 0))],
        out_specs=[pl.BlockSpec((t, BW), lambda i: (i, 0)), pl.BlockSpec((t, 256), lambda i: (i, 0)),
                   pl.BlockSpec((ATT_BLK, 256), lambda i: (i, 0)), pl.BlockSpec((8, 128), lambda i: (0, 0))],
        out_shape=[_sds((s, BW), BF16), _sds((s, 256), BF16), _sds((s // ATT_SUB, 256), BF16), _sds((8, 128), F32)],
        name=f"bwd_attn{l}")


def bwd_branch(proj, dproj, dpre, h, saved, dq, dkc, dkp, convw, vecs, wx_bd, wa_bd, l, comm=None):
    s = proj.shape[0]
    t = 2 * ATT_BLK
    nt = s // t
    nb = s // ATT_BLK
    hb = t // HALO

    def body(cur_ref, halo_ref, dpre_ref, h_ref, hp_ref, dq_ref, dkc_ref, dkp_ref,
             cw_ref, vec_ref, wx_ref, wa_ref, sv_ref, dproj_in, dp_ref, dcw_ref, dvec_ref, dwx_ref, dwa_ref,
             bufa, bufb, bufd, xd, xg, a_ext, hbuf, b_s, g_s, dh_s, ga, gb, gd, dhcar):
        del dproj_in
        step = pl.program_id(0)
        ti = nt - 1 - step
        first = ti == 0

        @pl.when(step == 0)
        def _():
            dcw_ref[...] = jnp.zeros((CW_ROWS, BW), F32)
            dvec_ref[...] = jnp.zeros((V_ROWS, BW), F32)
            dwx_ref[...] = jnp.zeros((BW, BW), F32)
            dwa_ref[...] = jnp.zeros((BW, BW), F32)
            dhcar[...] = jnp.zeros((1, BW), F32)
            a_ext[t:t + 8, :] = jnp.zeros((8, BW), F32)
            ga[t:t + 8, :] = jnp.zeros((8, BW), F32)
            gb[t:t + 8, :] = jnp.zeros((8, BW), F32)
            gd[t:t + HALO, :] = jnp.zeros((HALO, BW), F32)

        def cur(c0):
            return cur_ref[:, c0:c0 + BW].astype(F32)

        def rsum(v):
            return jnp.sum(v, axis=0, keepdims=True)

        def put(c0, v):
            dp_ref[:, c0:c0 + BW] = v.astype(BF16)

        v = _branch_fwd_math(cur_ref, halo_ref, cw_ref, vec_ref, wx_ref, wa_ref, bufa, bufb, bufd, xd, first, t, sv_ref)
        ca, gi, gr, sp, a, mult = v["ca"], v["gi"], v["gr"], v["sp"], v["a"], v["mult"]
        dpa = dpre_ref[:, 0:BW].astype(F32)
        gg, dgg = _gelu_and_grad(cur(C_AG))
        hv = h_ref[...]
        put(C_AG, dpa * hv * dgg)
        a_ext[0:t, :] = a
        b_s[...] = a_ext[pl.ds(1, t), :]
        g_s[...] = dpa * gg
        dhcar[...] = _scan_bwd(b_s, g_s, dh_s, dhcar[...], t)
        a_ext[t:t + 1, :] = a[0:1, :]
        dh = dh_s[...]
        hbuf[0:8, :] = jnp.where(first, 0.0, hp_ref[...])
        hbuf[8:8 + t, :] = hv
        da = dh * hbuf[pl.ds(7, t), :]
        d_ca = dh * gi * mult
        d_gi = dh * ca * mult
        d_mult = dh * ca * gi
        d_la = da * a - d_mult * (a * a) / mult
        lam = vec_ref[V_LAM:V_LAM + 1, :]
        dvec_ref[V_LAM:V_LAM + 1, :] += rsum(d_la * gr) * (LRU_C * _sigmoid(-lam))
        d_gr = d_la * (-LRU_C * sp)
        d_zr = d_gr * gr * (1.0 - gr)
        d_zi = d_gi * gi * (1.0 - gi)
        dvec_ref[V_BA:V_BA + 1, :] += rsum(d_zr)
        dvec_ref[V_BX:V_BX + 1, :] += rsum(d_zi)
        dwa_ref[...] += _dot(ca, d_zr, _TN)
        dwx_ref[...] += _dot(ca, d_zi, _TN)
        d_ca = d_ca + _dot(d_zi, wx_ref[...], _NT) + _dot(d_zr, wa_ref[...], _NT)
        dvec_ref[V_CAB:V_CAB + 1, :] += rsum(d_ca)
        ga[0:t, :] = d_ca
        d_ax = jnp.zeros((t, BW), F32)
        for k in range(CONV_A):
            d_ax = d_ax + cw_ref[CW_A + k:CW_A + k + 1, :] * ga[pl.ds(CONV_A - 1 - k, t), :]
            dcw_ref[CW_A + k:CW_A + k + 1, :] += rsum(d_ca * bufa[pl.ds(HALO - (CONV_A - 1) + k, t), :])
        ga[t:t + 8, :] = d_ca[0:8, :]
        put(C_AX, d_ax)
        dpb = dpre_ref[:, BW:2 * BW].astype(F32)
        put(C_BB, dpb * v["cb"])
        d_cb = dpb * cur(C_BB)
        gb[0:t, :] = d_cb
        d_cbin = jnp.zeros((t, BW), F32)
        for k in range(CONV_B):
            d_cbin = d_cbin + cw_ref[CW_B + k:CW_B + k + 1, :] * gb[pl.ds(CONV_B - 1 - k, t), :]
            dcw_ref[CW_B + k:CW_B + k + 1, :] += rsum(d_cb * bufb[pl.ds(HALO - (CONV_B - 1) + k, t), :])
        gb[t:t + 8, :] = d_cb[0:8, :]
        put(C_BC, d_cbin * cur(C_BV))
        put(C_BV, d_cbin * cur(C_BC))
        dpd = dpre_ref[:, 3 * BW:4 * BW].astype(F32)
        ln, xh, rstd, s2 = v["ln"], v["xh"], v["rstd"], v["s2"]
        sg = _sigmoid(ln)
        d_ln = dpd * sg * (1.0 + ln * (1.0 - sg))
        dvec_ref[V_LNG:V_LNG + 1, :] += rsum(d_ln * xh)
        dvec_ref[V_LNB:V_LNB + 1, :] += rsum(d_ln)
        d_xh = d_ln * vec_ref[V_LNG:V_LNG + 1, :]
        d_cd = rstd * (d_xh - jnp.mean(d_xh, axis=-1, keepdims=True)
                       - xh * jnp.mean(d_xh * xh, axis=-1, keepdims=True))
        dvec_ref[V_CDB:V_CDB + 1, :] += rsum(d_cd)
        gd[0:t, :] = d_cd
        _shifted_copies(gd, xg, t + HALO)
        d_dg = jnp.zeros((t, BW), F32)
        for k in range(CONV_D):
            d_dg = d_dg + cw_ref[CW_D + k:CW_D + k + 1, :] * _window(gd, xg, CONV_D - 1 - k, t)
            dcw_ref[CW_D + k:CW_D + k + 1, :] += rsum(d_cd * _window(bufd, xd, HALO - (CONV_D - 1) + k, t))
        gd[t:t + HALO, :] = d_cd[0:HALO, :]
        put(C_D1, d_dg * s2)
        put(C_D2, d_dg * cur(C_D1) * s2 * (1.0 - s2))
        dp_ref[:, C_Q:C_Q + BW] = dq_ref[...]
        dkp = jnp.where(step == 0, 0.0, dkp_ref[...].astype(F32))
        dp_ref[0:t - ATT_BLK, C_K:C_K + 256] = dkc_ref[0:t - ATT_BLK, :]
        dp_ref[t - ATT_BLK:t, C_K:C_K + 256] = (dkc_ref[t - ATT_BLK:t, :].astype(F32) + dkp).astype(BF16)

    rev = lambda i: nt - 1 - i
    full = lambda r, c: pl.BlockSpec((r, c), lambda i: (0, 0))
    return _call(
        body, comm, (proj, proj, dpre, h, h, dq, dkc, dkp, convw, vecs, wx_bd, wa_bd, saved, dproj), grid=(nt,),
        in_specs=[pl.BlockSpec((t, GL0), lambda i: (rev(i), 0)),
                  pl.BlockSpec((HALO, GL0), lambda i: (jnp.maximum(rev(i) * hb - 1, 0), 0)),
                  pl.BlockSpec((t, 4 * BW), lambda i: (rev(i), 0)),
                  pl.BlockSpec((t, BW), lambda i: (rev(i), 0)),
                  pl.BlockSpec((8, BW), lambda i: (jnp.maximum(rev(i) * (t // 8) - 1, 0), 0)),
                  pl.BlockSpec((t, BW), lambda i: (rev(i), 0)),
                  pl.BlockSpec((t, 256), lambda i: (rev(i), 0)),
                  pl.BlockSpec((ATT_BLK, 256), lambda i: (jnp.minimum(rev(i) + 1, nt - 1), 0)),
                  pl.BlockSpec((None, CW_ROWS, BW), lambda i: (l, 0, 0)),
                  pl.BlockSpec((None, V_ROWS, BW), lambda i: (l, 0, 0)),
                  pl.BlockSpec((None, BW, BW), lambda i: (l, 0, 0)),
                  pl.BlockSpec((None, BW, BW), lambda i: (l, 0, 0)),
                  pl.BlockSpec((t, 3 * BW), lambda i: (rev(i), 0)),
                  pl.BlockSpec(memory_space=pl.ANY)],
        out_specs=[pl.BlockSpec((t, GL0), lambda i: (rev(i), 0)),
                   full(CW_ROWS, BW), full(V_ROWS, BW), full(BW, BW), full(BW, BW)],
        out_shape=[_sds((s, IN_W), BF16), _sds((CW_ROWS, BW), F32), _sds((V_ROWS, BW), F32),
                   _sds((BW, BW), F32), _sds((BW, BW), F32)],
        scratch_shapes=[pltpu.VMEM((t + HALO, BW), F32)] * 3 + [pltpu.VMEM((7, t + HALO - 8, BW), F32)] * 2
        + [pltpu.VMEM((t + 8, BW), F32), pltpu.VMEM((t + 8, BW), F32)]
        + [pltpu.VMEM((t, BW), F32)] * 3
        + [pltpu.VMEM((t + 8, BW), F32), pltpu.VMEM((t + 8, BW), F32), pltpu.VMEM((t + HALO, BW), F32),
           pltpu.VMEM((1, BW), F32)],
        aliases={13: 0}, name=f"bwd_branch{l}")


def bwd_proj(dproj, x, dh1, g1, wt_in, l, comm=None):
    s = x.shape[0]
    tm = min(512, s)
    ck = 1408
    n_j, n_i = IN_W // ck, s // tm

    def body(dp_ref, x_ref, dh_ref, g_ref, w_ref, dx_ref, dw_ref, st_ref, dxn, xn_b, acc):
        j, i = pl.program_id(0), pl.program_id(1)
        rows = pl.ds(pl.multiple_of(i * tm, tm), tm)
        g = g_ref[l:l + 1, :]

        @pl.when(j == 0)
        def _():
            xv = x_ref[...]
            r = lax.rsqrt(jnp.mean(xv * xv, axis=-1, keepdims=True) + EPS)
            xn_b[rows, :] = (xv * r * g).astype(BF16)
            dxn[rows, :] = jnp.zeros((tm, D), F32)

        @pl.when((j == 0) & (i == 0))
        def _():
            st_ref[...] = jnp.zeros((8, D), F32)

        @pl.when(i == 0)
        def _():
            acc[...] = jnp.zeros((ck, D), F32)

        dp = dp_ref[...]
        dxn[rows, :] += _dot(dp, w_ref[...], _NN)
        acc[...] += _dot(dp, xn_b[rows, :], _TN)

        @pl.when(i == n_i - 1)
        def _():
            dw_ref[...] = acc[...].astype(BF16)

        @pl.when(j == n_j - 1)
        def _():
            xv = x_ref[...]
            r = lax.rsqrt(jnp.mean(xv * xv, axis=-1, keepdims=True) + EPS)
            n = xv * r
            dv = dxn[rows, :]
            dn = dv * g
            dx_ref[...] = dh_ref[...] + r * (dn - n * jnp.mean(dn * n, axis=-1, keepdims=True))
            st_ref[0:1, :] += jnp.sum(dv * n, axis=0, keepdims=True)

    lastrow = lambda j, i: (jnp.where(j == n_j - 1, i, 0), 0)
    return _call(
        body, comm, (dproj, x, dh1, g1, wt_in), grid=(n_j, n_i),
        in_specs=[pl.BlockSpec((tm, ck), lambda j, i: (i, j)),
                  pl.BlockSpec((tm, D), lambda j, i: (_edge_index(j, i, n_j, n_i), 0)),
                  pl.BlockSpec((tm, D), lastrow),
                  pl.BlockSpec((DEPTH, D), lambda j, i: (0, 0)),
                  pl.BlockSpec((ck, D), lambda j, i: (j, 0))],
        out_specs=[pl.BlockSpec((tm, D), lastrow), pl.BlockSpec((ck, D), lambda j, i: (j, 0)),
                   pl.BlockSpec((8, D), lambda j, i: (0, 0))],
        out_shape=[_sds((s, D), F32), _sds((IN_W, D), BF16), _sds((8, D), F32)],
        scratch_shapes=[pltpu.VMEM((s, D), F32), pltpu.VMEM((s, D), BF16), pltpu.VMEM((ck, D), F32)],
        name=f"bwd_proj{l}")


def bwd_proj_w(dproj, x, g1, l, half, comm=None):
    s = x.shape[0]
    tm = min(1024, s)
    ck = 1408
    c0, hw = W_IN_PARTS[half]
    n_j, n_i = IN_W // ck, s // tm

    def body(dp_ref, x_ref, g_ref, dw_ref, xn_b, acc):
        j, i = pl.program_id(0), pl.program_id(1)
        rows = pl.ds(pl.multiple_of(i * tm, tm), tm)

        @pl.when(j == 0)
        def _():
            xv = x_ref[...]
            r = lax.rsqrt(jnp.mean(xv * xv, axis=-1, keepdims=True) + EPS)
            xn_b[rows, :] = (xv * r * g_ref[l:l + 1, :])[:, c0:c0 + hw].astype(BF16)

        @pl.when(i == 0)
        def _():
            acc[...] = jnp.zeros((ck, hw), F32)

        acc[...] += _dot(dp_ref[...], xn_b[rows, :], _TN)

        @pl.when(i == n_i - 1)
        def _():
            dw_ref[...] = acc[...].astype(BF16)

    return _call(
        body, comm, (dproj, x, g1), grid=(n_j, n_i),
        in_specs=[pl.BlockSpec((tm, ck), lambda j, i: (i, j)),
                  pl.BlockSpec((tm, D), lambda j, i: (jnp.where(j == 0, i, n_i - 1), 0)),
                  pl.BlockSpec((DEPTH, D), lambda j, i: (0, 0))],
        out_specs=pl.BlockSpec((ck, hw), lambda j, i: (j, 0)),
        out_shape=_sds((IN_W, hw), BF16),
        scratch_shapes=[pltpu.VMEM((s, hw), BF16), pltpu.VMEM((ck, hw), F32)],
        name=f"bwd_proj_w{half}_{l}")


def bwd_proj_x(dproj, x, dh1, g1, wt_in, l, comm=None):
    s = x.shape[0]
    tm = min(512, s)
    ck = 1408
    n_j, n_i = IN_W // ck, s // tm

    def body(dp_ref, x_ref, dh_ref, g_ref, w_ref, dx_ref, st_ref, dxn):
        j, i = pl.program_id(0), pl.program_id(1)
        rows = pl.ds(pl.multiple_of(i * tm, tm), tm)
        g = g_ref[l:l + 1, :]

        @pl.when((j == 0) & (i == 0))
        def _():
            st_ref[...] = jnp.zeros((8, D), F32)

        part = _dot(dp_ref[...], w_ref[...], _NN)

        @pl.when(j == 0)
        def _():
            dxn[rows, :] = part

        @pl.when(j > 0)
        def _():
            dxn[rows, :] += part

        @pl.when(j == n_j - 1)
        def _():
            xv = x_ref[...]
            r = lax.rsqrt(jnp.mean(xv * xv, axis=-1, keepdims=True) + EPS)
            n = xv * r
            dv = dxn[rows, :]
            dn = dv * g
            dx_ref[...] = dh_ref[...] + r * (dn - n * jnp.mean(dn * n, axis=-1, keepdims=True))
            st_ref[0:1, :] += jnp.sum(dv * n, axis=0, keepdims=True)

    lastrow = lambda j, i: (jnp.where(j == n_j - 1, i, 0), 0)
    return _call(
        body, comm, (dproj, x, dh1, g1, wt_in), grid=(n_j, n_i),
        in_specs=[pl.BlockSpec((tm, ck), lambda j, i: (i, j)), pl.BlockSpec((tm, D), lastrow),
                  pl.BlockSpec((tm, D), lastrow),
                  pl.BlockSpec((DEPTH, D), lambda j, i: (0, 0)), pl.BlockSpec((ck, D), lambda j, i: (j, 0))],
        out_specs=[pl.BlockSpec((tm, D), lastrow), pl.BlockSpec((8, D), lambda j, i: (0, 0))],
        out_shape=[_sds((s, D), F32), _sds((8, D), F32)],
        scratch_shapes=[pltpu.VMEM((s, D), F32)], name=f"bwd_proj_x{l}")


def _block_diag(w):
    nl, nb, bw, _ = w.shape
    eye = jnp.eye(nb, dtype=w.dtype)
    return jnp.einsum("lhij,hk->lhikj", w, eye).reshape(nl, nb * bw, nb * bw).astype(BF16)


class NoOverlap:
    def __init__(self, big):
        self.big = big

    def weights(self, l):
        return self.big[l]

    def job(self, slot, l):
        return None

    def done(self, slot, l, results):
        pass

    def new_grads(self, group, l, grads):
        pass

    def new_small(self, l, arrays, head_stats):
        pass


def local_step(x, target, norm1_g, norm2_g, final_g, convw, vecs, lru_wx, lru_wa, plan):
    wx_bd, wa_bd = _block_diag(lru_wx), _block_diag(lru_wa)

    def run(fn, slot, l, *args):
        res, cres = fn(*args, l, comm=plan.job(slot, l))
        plan.done(slot, l, cres)
        return res

    saved = []
    for l in range(DEPTH):
        proj = run(fwd_proj, "fwd_proj", l, x, norm1_g, plan.weights(l)["in_t"])
        pre_abd, h, kept = run(fwd_branch, "fwd_branch", l, proj, convw, vecs, wx_bd, wa_bd)
        pre_c = run(fwd_attn, "fwd_attn", l, proj, vecs)
        w = plan.weights(l)
        y4, merged, h1 = run(fwd_merge, "fwd_merge", l, x, proj, pre_abd, pre_c, w["a_t"], w["b_t"], w["c_t"], w["d_t"], w["o"])
        w = plan.weights(l)
        x_out, fg, fu = run(fwd_ffn, "fwd_ffn", l, h1, norm2_g, w["gate_t"], w["up_t"], w["down"])
        saved.append((x, proj, pre_abd, h, kept, pre_c, y4, merged, h1, fg, fu))
        x = x_out
    dx, head_stats = loss_head(x, final_g.reshape(1, D), target)
    small = [None] * DEPTH
    for l in reversed(range(DEPTH)):
        x_in, proj, pre_abd, h, kept, pre_c, y4, merged, h1, fg, fu = saved[l]
        w = plan.weights(l)
        dh1, d_gate, d_up, d_down, st_ffn = run(bwd_ffn, "bwd_ffn", l, dx, h1, fg, fu, norm2_g, w["gate_t"], w["up_t"], w["down"])
        plan.new_grads("ffn", l, dict(gate_t=d_gate, up_t=d_up, down=d_down))
        dproj, dpre, d_o, d_a, d_b, d_c, d_d = run(
            bwd_merge, "bwd_merge", l, dh1, y4, proj, merged, pre_abd, pre_c, w["a_t"], w["b_t"], w["c_t"], w["d_t"], w["o"])
        plan.new_grads("out", l, dict(a_t=d_a, b_t=d_b, c_t=d_c, d_t=d_d, o=d_o))
        dq, dkc, dkp, st_attn = run(bwd_attn, "bwd_attn", l, proj, dpre, vecs)
        dproj, dcw, dvec, dwx, dwa = run(bwd_branch, "bwd_branch", l, proj, dproj, dpre, h, kept, dq, dkc, dkp, convw, vecs, wx_bd, wa_bd)
        if l > 0:
            dx, d_in, st_proj = run(bwd_proj, "bwd_proj", l, dproj, x_in, dh1, norm1_g, w["in_t"])
            plan.new_grads("in", l, dict(in_t=d_in))
        else:
            for half, name in enumerate(("in_a", "in_b")):
                d_half = run(functools.partial(bwd_proj_w, half=half), f"bwd_proj_w{half}", l, dproj, x_in, norm1_g)
                plan.new_grads(name, l, {name: d_half})
            dx, st_proj = run(bwd_proj_x, "bwd_proj_x", l, dproj, x_in, dh1, norm1_g, w["in_t"])
        small[l] = (st_proj, st_ffn, dvec, st_attn, dcw, dwx, dwa)
        plan.new_small(l, small[l], head_stats)
    return head_stats, dx, small


BIG = dict(in_t=("w_in", "view"), a_t=("w_a_out", "transpose"), b_t=("w_b_out", "transpose"), c_t=("w_c_out", "transpose"),
           d_t=("w_d_out", "transpose"), o=("w_o", "plain"), gate_t=("w_ffn_gate", "view"), up_t=("w_ffn_up", "view"),
           down=("w_ffn_down", "plain"))


def cast_transpose(ws, name):
    n = len(ws)
    nl, a, b = ws[0].shape
    ta = min(256, a)

    def body(*refs):
        for w_ref, o_ref in zip(refs[:n], refs[n:]):
            o_ref[...] = w_ref[...].T.astype(BF16)

    return pl.pallas_call(
        body, grid=(nl, a // ta),
        in_specs=[pl.BlockSpec((None, ta, b), lambda l, i: (l, i, 0))] * n,
        out_specs=[pl.BlockSpec((None, b, ta), lambda l, i: (l, 0, i))] * n,
        out_shape=[_sds((nl, b, a), BF16)] * n, compiler_params=_cparams(2), name=name)(*ws)


def add_partials(mine, recv, core, name):
    n = len(mine)

    def body(core_ref, *refs):
        del core_ref
        for a_ref, b_ref, o_ref in zip(refs[:n], refs[n:2 * n], refs[2 * n:]):
            o_ref[...] = (a_ref[...].astype(F32) + b_ref[...].astype(F32)).astype(BF16)

    return pl.pallas_call(
        body,
        grid_spec=pltpu.PrefetchScalarGridSpec(
            num_scalar_prefetch=1, grid=(4,),
            in_specs=[pl.BlockSpec((None, None) + a.shape[2:], lambda i, cr: (i, cr[0], 0, 0)) for a in mine]
            + [pl.BlockSpec((None,) + b.shape[1:], lambda i, cr: (i, 0, 0)) for b in recv],
            out_specs=[pl.BlockSpec((None,) + b.shape[1:], lambda i, cr: (i, 0, 0)) for b in recv]),
        out_shape=[_sds(b.shape, BF16) for b in recv], compiler_params=_cparams(1), name=name)(core, *mine, *recv)


def _adamw(w, g, m, v):
    m = ADAM_B1 * m + (1.0 - ADAM_B1) * g
    v = ADAM_B2 * v + (1.0 - ADAM_B2) * (g * g)
    m_hat = m / (1.0 - ADAM_B1 ** ADAM_STEP)
    v_hat = v / (1.0 - ADAM_B2 ** ADAM_STEP)
    delta = -ADAM_LR * (m_hat / (jnp.sqrt(v_hat) + ADAM_EPS) + ADAM_WD * w)
    return delta, m, v


def adamw_big(contrib, w, m, v, transposed, name, comm=None):
    nsrc, nl, rows, cols = contrib.shape
    ct = 256

    def body(c_ref, w_ref, m_ref, v_ref, g_out, d_out, m_out, v_out):
        g = c_ref[0].astype(F32)
        for src in range(1, nsrc):
            g = g + c_ref[src].astype(F32)
        if transposed:
            g = g.T
        delta, mn, vn = _adamw(w_ref[...], g, m_ref[...], v_ref[...])
        g_out[...] = g
        d_out[...] = delta
        m_out[...] = mn
        v_out[...] = vn

    if transposed:
        wspec = pl.BlockSpec((None, ct, rows), lambda l, j: (l, j, 0))
    else:
        wspec = pl.BlockSpec((None, rows, ct), lambda l, j: (l, 0, j))
    return _call(
        body, comm, (contrib, w, m, v), grid=(nl, cols // ct),
        in_specs=[pl.BlockSpec((nsrc, None, rows, ct), lambda l, j: (0, l, 0, j)), wspec, wspec, wspec],
        out_specs=[wspec] * 4, out_shape=[_sds(w.shape, F32)] * 4, name=name)


VEC_NAMES = ("conv_a_b", "lru_bx", "lru_ba", "lru_lambda", "conv_d_b", "ln_d_g", "ln_d_b")
P_N1, P_N2, P_VEC, P_CONV, P_LRU = 0, 1, 2, 6, 6 + CW_ROWS
P_FINAL, P_LOSS, P_ROWS = P_LRU + HD, P_LRU + HD + 1, P_LRU + HD + 2
SMALL = ("norm1_g", "conv_a_w", "conv_a_b", "lru_wx", "lru_bx", "lru_wa", "lru_ba", "lru_lambda", "conv_b_w", "sinks",
         "conv_d_w", "conv_d_b", "ln_d_g", "ln_d_b", "norm2_g", "final_g")
VMEM_FULL = pl.BlockSpec(memory_space=pltpu.VMEM)


def _stack_vecs(p):
    rows = [p[n] for n in VEC_NAMES] + [jnp.pad(p["sinks"], ((0, 0), (0, BW - N_HEADS)))]
    return jnp.stack(rows, axis=1)


def _stack_convs(p):
    nl, _, ch = p["conv_a_w"].shape
    z = jnp.zeros((nl, 1, ch), F32)
    return jnp.concatenate([p["conv_a_w"], p["conv_b_w"], z, p["conv_d_w"], z], axis=1)


def _vec_place(r):
    return P_VEC + r // 2, (r % 2) * BW


def pack_small(arrays, head_stats, l):
    n = len(arrays)

    def body(*refs):
        st_proj, st_ffn, dvec, st_attn, dcw, dwx, dwa = refs[:n]
        pack = refs[-1]
        pack[...] = jnp.zeros((P_ROWS, D), F32)
        lane = lax.broadcasted_iota(jnp.int32, (HD, BW), 1)
        pack[P_N1:P_N1 + 1, :] = st_proj[0:1, :]
        pack[P_N2:P_N2 + 1, :] = st_ffn[0:1, :]
        for r in range(len(VEC_NAMES)):
            row, c0 = _vec_place(r)
            pack[row:row + 1, c0:c0 + BW] = dvec[r:r + 1, :]
        row, c0 = _vec_place(V_SINK)
        pack[row:row + 1, c0:c0 + 128] = st_attn[0:1, :]
        pack[P_CONV:P_CONV + CW_ROWS, 0:BW] = dcw[...]
        for mat, c0 in ((dwx, 0), (dwa, BW)):
            blocks = jnp.zeros((HD, BW), F32)
            for h in range(BW // HD):
                blocks = jnp.where((lane >= HD * h) & (lane < HD * (h + 1)), mat[HD * h:HD * (h + 1), :], blocks)
            pack[P_LRU:P_LRU + HD, c0:c0 + BW] = blocks
        if head_stats is not None:
            pack[P_FINAL:P_LOSS + 1, :] = refs[n][0:2, :]

    flat = list(arrays) + ([] if head_stats is None else [head_stats])
    return pl.pallas_call(body, out_shape=_sds((P_ROWS, D), F32), in_specs=[VMEM_FULL] * len(flat), out_specs=VMEM_FULL,
                          name=f"pack_small{l}", compiler_params=pltpu.CompilerParams(vmem_limit_bytes=VMEM_LIMIT))(*flat)


def adamw_small(gathered, me, w, m, v):
    ns = len(SMALL)

    def body(me_ref, *refs):
        c_refs, refs = refs[:DEPTH], refs[DEPTH:]
        w_refs, m_refs, v_refs = refs[:ns], refs[ns:2 * ns], refs[2 * ns:3 * ns]
        loss_ref, outs, gs = refs[3 * ns], refs[3 * ns + 1:3 * ns + 1 + 4 * ns], refs[-1]
        for l in range(DEPTH):
            gs[l] = c_refs[l][0]
            for dev in range(1, NDEV):
                gs[l] += c_refs[l][dev]
        loss_ref[...] = gs[DEPTH - 1, P_LOSS:P_LOSS + 1, 0:128]

        def update(name, sel, g):
            i = SMALL.index(name)
            delta, mn, vn = _adamw(w_refs[i][sel], g, m_refs[i][sel], v_refs[i][sel])
            for o_ref, val in zip(outs[4 * i:4 * i + 4], (g, delta, mn, vn)):
                o_ref[sel] = val

        update("final_g", (slice(0, 1), slice(None)), gs[DEPTH - 1, P_FINAL:P_FINAL + 1, :])
        shift = (BW - me_ref[0] * (BW // NDEV)) & (BW - 1)
        for l in range(DEPTH):
            row = (slice(l, l + 1), slice(None))
            update("norm1_g", row, gs[l, P_N1:P_N1 + 1, :])
            update("norm2_g", row, gs[l, P_N2:P_N2 + 1, :])
            for r, name in enumerate(VEC_NAMES):
                prow, c0 = _vec_place(r)
                update(name, row, gs[l, prow:prow + 1, c0:c0 + BW])
            prow, c0 = _vec_place(V_SINK)
            update("sinks", row, gs[l, prow:prow + 1, c0:c0 + N_HEADS])
            mine = pltpu.roll(gs[l, P_CONV:P_CONV + CW_ROWS, 0:BW], shift, 1)[:, 0:BW // NDEV]
            update("conv_a_w", (l,), mine[CW_A:CW_A + CONV_A])
            update("conv_b_w", (l,), mine[CW_B:CW_B + CONV_B])
            update("conv_d_w", (l,), mine[CW_D:CW_D + CONV_D])
            for h in range(BW // HD):
                update("lru_wx", (l, h), gs[l, P_LRU:P_LRU + HD, HD * h:HD * (h + 1)])
                update("lru_wa", (l, h), gs[l, P_LRU:P_LRU + HD, BW + HD * h:BW + HD * (h + 1)])

    args = [p[n] for p in (w, m, v) for n in SMALL]
    full = lambda a: pl.BlockSpec(a.shape, lambda i, me_ref: (0,) * a.ndim)
    out_shape = [_sds((1, 128), F32)] + [_sds(w[n].shape, F32) for n in SMALL for _ in range(4)]
    outs = pl.pallas_call(
        body,
        grid_spec=pltpu.PrefetchScalarGridSpec(
            num_scalar_prefetch=1, grid=(1,),
            in_specs=[full(a) for a in list(gathered) + args], out_specs=[full(o) for o in out_shape],
            scratch_shapes=[pltpu.VMEM((DEPTH, P_ROWS, D), F32)]),
        out_shape=out_shape, name="adamw_small", compiler_params=_cparams(1))(me, *gathered, *args)
    return outs[0], {n: outs[1 + 4 * i:5 + 4 * i] for i, n in enumerate(SMALL)}


def merge_jobs(jobs):
    jobs = [j for j in jobs if j is not None]
    if not jobs:
        return None, []
    inputs, aliases, outs, sems, cuts = [], {}, [], [], []
    for j in jobs:
        i0, o0, s0 = len(inputs), len(outs), len(sems)
        aliases.update({i0 + i: o0 + o for i, o in j.aliases.items()})
        inputs += j.inputs
        outs += j.out_shapes
        sems += j.sem_shapes
        cuts.append((i0, len(inputs), o0, len(outs), s0, len(sems)))

    def each(which):
        def go(cins, couts, s):
            for j, (i0, i1, o0, o1, s0, s1) in zip(jobs, cuts):
                if getattr(j, which) is not None:
                    getattr(j, which)(cins[i0:i1], couts[o0:o1], s[s0:s1])
        return go

    relay = each("relay") if any(j.relay is not None for j in jobs) else None
    return CommJob(inputs, aliases, outs, sems, each("start"), each("finish"), relay), [(c[2], c[3]) for c in cuts]


SIXTHS = 6
OUT_KINDS = ("a_t", "b_t", "c_t", "d_t", "o")
GATHER_PLAN = {
    "fwd_proj": [(k, 0, 0, 6) for k in OUT_KINDS] + [("gate_t", 0, 0, 6)],
    "fwd_branch": [("up_t", 0, 0, 6)],
    "fwd_attn": [("down", 0, 0, 6)],
    "fwd_merge": [("in_t", 1, 0, 2)],
    "fwd_ffn": [("in_t", 1, 2, 6)],
}
SIBLING_PLAN = {"bwd_merge": ("ffn", 0), "bwd_branch": ("out", 0), "bwd_ffn": ("in", 1),
                "bwd_proj_w1": ("in_a", 0), "bwd_proj_x": ("in_b", 0)}
GROUPS = dict(ffn=("gate_t", "up_t", "down"), out=OUT_KINDS, in_a=("in_a",), in_b=("in_b",))
GROUPS["in"] = ("in_t",)
COLUMN_HALF = dict(in_a=("in_t", W_IN_PARTS[0][0]), in_b=("in_t", W_IN_PARTS[1][0]))
CHIP_PLAN = {
    "bwd_attn": [("in_t", 1, 3, 5)],
    "bwd_branch": [("in_t", 1, 5, 6), ("gate_t", 0, 0, 6), ("up_t", 0, 0, 6), ("down", 0, 0, 3)],
    "bwd_proj": [(k, 0, 0, 6) for k in OUT_KINDS] + [("down", 0, 3, 6)],
    "bwd_proj_w0": [(k, 0, 0, 6) for k in OUT_KINDS[:3]] + [("down", 0, 3, 6)],
    "bwd_proj_w1": [(k, 0, 0, 6) for k in OUT_KINDS[3:]],
    "bwd_merge": [("in_t", 1, 0, 3)],
    "bwd_proj_x": [("in_a", 0, 0, 6)],
    "adamw_gate_t": [("in_b", 0, 0, 6)],
}
SMALL_GATHER_PLAN = {"bwd_ffn": 1, "adamw_down": 0}


class Overlap:
    def __init__(self, shards, core):
        self.shards = shards
        self.core = core
        self.gathered = [dict.fromkeys(BIG) for _ in range(DEPTH)]
        self.views = {}
        self.partial = {}
        self.contrib = dict.fromkeys(BIG)
        self.small_packs = [None] * DEPTH
        self.small_gathered = [None] * DEPTH
        self._open = None

    def weights(self, l):
        return self.gathered[l]

    def new_grads(self, group, l, grads):
        for k, g in grads.items():
            self.views[k, l] = g.reshape(4, 2, g.shape[0] // NDEV, g.shape[1])

    def new_small(self, l, arrays, head_stats):
        self.small_packs[l] = pack_small(arrays, head_stats if l == DEPTH - 1 else None, l)

    @staticmethod
    def _rows(shard_rows, f0, f1):
        return shard_rows * f0 // SIXTHS, shard_rows * (f1 - f0) // SIXTHS

    def job(self, slot, l):
        jobs, notes = [], []
        pieces = [(k, l + dl, f0, f1) for k, dl, f0, f1 in GATHER_PLAN.get(slot, []) if l + dl < DEPTH]
        if pieces:
            jobs.append(gather_job([((k, ll), self.shards[ll][k], self.gathered[ll][k],
                                     *self._rows(self.shards[ll][k].shape[0], f0, f1)) for k, ll, f0, f1 in pieces]))
            notes.append(("gather", list(dict.fromkeys((k, ll) for k, ll, _, _ in pieces))))
        if slot in SIBLING_PLAN and l + SIBLING_PLAN[slot][1] < DEPTH:
            group, dl = SIBLING_PLAN[slot]
            keys = [(k, l + dl) for k in GROUPS[group]]
            jobs.append(sibling_exchange_job([self.views[key] for key in keys]))
            notes.append(("sibling", keys))
        pieces = [(k, l + dl, f0, f1) for k, dl, f0, f1 in CHIP_PLAN.get(slot, []) if l + dl < DEPTH]
        if pieces:
            whole = [(*COLUMN_HALF.get(k, (k, 0)), k, ll, f0, f1) for k, ll, f0, f1 in pieces]
            jobs.append(chip_exchange_job([(self.partial[k, ll], self.contrib[kind], kind, ll,
                                            *self._rows(self.partial[k, ll].shape[1], f0, f1), col0, self.shards[ll][kind].shape[1])
                                           for kind, col0, k, ll, f0, f1 in whole]))
            notes.append(("chips", list(dict.fromkeys(kind for kind, *_ in whole))))
        if slot in SMALL_GATHER_PLAN and l + SMALL_GATHER_PLAN[slot] < DEPTH:
            ll = l + SMALL_GATHER_PLAN[slot]
            jobs.append(gather_job([("small", self.small_packs[ll], None, 0, P_ROWS)]))
            notes.append(("small", ll))
        job, spans = merge_jobs(jobs)
        self._open = (slot, l, notes, spans)
        return job

    def done(self, slot, l, results):
        open_slot, open_l, notes, spans = self._open
        assert (open_slot, open_l) == (slot, l)
        for (what, keys), (r0, r1) in zip(notes, spans):
            res = results[r0:r1]
            if what == "gather":
                for (k, ll), g in zip(keys, res):
                    self.gathered[ll][k] = g
            elif what == "sibling":
                sums = add_partials([self.views[key] for key in keys], list(res), self.core, f"chip_sum_{keys[0][0]}{keys[0][1]}")
                self.partial.update(zip(keys, sums))
            elif what == "chips":
                for k, c in zip(keys, res):
                    self.contrib[k] = c
            else:
                self.small_gathered[keys], = res


SMALL = ("norm1_g", "conv_a_w", "conv_a_b", "lru_wx", "lru_bx", "lru_wa", "lru_ba", "lru_lambda", "conv_b_w", "sinks",
         "conv_d_w", "conv_d_b", "ln_d_g", "ln_d_b", "norm2_g", "final_g")
WEIGHTS = ("norm1_g", "w_in", "conv_a_w", "conv_a_b", "lru_wx", "lru_bx", "lru_wa", "lru_ba", "lru_lambda", "w_a_out",
           "conv_b_w", "w_b_out", "sinks", "w_c_out", "conv_d_w", "conv_d_b", "ln_d_g", "ln_d_b", "w_d_out", "w_o",
           "norm2_g", "w_ffn_gate", "w_ffn_up", "w_ffn_down", "final_g")


def kernel(x, norm1_g, w_in, conv_a_w, conv_a_b, lru_wx, lru_bx, lru_wa, lru_ba, lru_lambda, w_a_out, conv_b_w, w_b_out, sinks, w_c_out, conv_d_w, conv_d_b, ln_d_g, ln_d_b, w_d_out, w_o, norm2_g, w_ffn_gate, w_ffn_up, w_ffn_down, final_g, loss_target, m_norm1_g, m_w_in, m_conv_a_w, m_conv_a_b, m_lru_wx, m_lru_bx, m_lru_wa, m_lru_ba, m_lru_lambda, m_w_a_out, m_conv_b_w, m_w_b_out, m_sinks, m_w_c_out, m_conv_d_w, m_conv_d_b, m_ln_d_g, m_ln_d_b, m_w_d_out, m_w_o, m_norm2_g, m_w_ffn_gate, m_w_ffn_up, m_w_ffn_down, m_final_g, v_norm1_g, v_w_in, v_conv_a_w, v_conv_a_b, v_lru_wx, v_lru_bx, v_lru_wa, v_lru_ba, v_lru_lambda, v_w_a_out, v_conv_b_w, v_w_b_out, v_sinks, v_w_c_out, v_conv_d_w, v_conv_d_b, v_ln_d_g, v_ln_d_b, v_w_d_out, v_w_o, v_norm2_g, v_w_ffn_gate, v_w_ffn_up, v_w_ffn_down, v_final_g):
    args = dict(locals())
    w = {n: args[n] for n in WEIGHTS}
    m = {n: args["m_" + n] for n in WEIGHTS}
    v = {n: args["v_" + n] for n in WEIGHTS}
    me = _dev_index(*_mesh_pos())

    def rows_major(a, how):
        return jnp.swapaxes(a, 1, 2) if how == "view" else a

    stacked = {k: rows_major(w[n], how).astype(BF16) for k, (n, how) in BIG.items() if how != "transpose"}
    turned = [k for k, (n, how) in BIG.items() if how == "transpose"]
    stacked.update(zip(turned, cast_transpose([w[BIG[k][0]] for k in turned], "prep_transposed")))
    plan = Overlap([{k: stacked[k][l] for k in BIG} for l in range(DEPTH)], lax.axis_index("c").astype(jnp.int32).reshape(1))
    convs = jnp.pad(_stack_convs(w).reshape(DEPTH * CW_ROWS, BW // NDEV), ((0, 0), (0, 256 - BW // NDEV)))
    g_in0, g_conv = _comm_only(gather_job([(("in_t", 0), plan.shards[0]["in_t"], None, 0, plan.shards[0]["in_t"].shape[0]),
                                           ("convs", convs, None, 0, convs.shape[0])]), "gather_first")
    plan.gathered[0]["in_t"] = g_in0
    convw = g_conv[:, :BW // NDEV].reshape(NDEV, DEPTH, CW_ROWS, BW // NDEV).transpose(1, 2, 0, 3).reshape(DEPTH, CW_ROWS, BW)

    vecs = _stack_vecs(w)
    head_stats, grad_x, grads = local_step(x[0], loss_target[0], norm1_g, norm2_g, final_g, convw, vecs, lru_wx, lru_wa, plan)


    out = {}
    for k in ("down", "gate_t", "up_t", "o", "a_t", "b_t", "c_t", "d_t", "in_t"):
        n, how = BIG[k]
        res, cres = adamw_big(plan.contrib[k], rows_major(w[n], how), rows_major(m[n], how), rows_major(v[n], how),
                              how == "transpose", "adamw_" + k, comm=plan.job("adamw_" + k, 0))
        plan.done("adamw_" + k, 0, cres)
        out[n] = [rows_major(r, how) for r in res]

    def own_shapes(p):
        return {n: p[n].reshape(1, D) if n == "final_g" else p[n] for n in SMALL}

    loss, small = adamw_small([g.reshape(NDEV, P_ROWS, D) for g in plan.small_gathered], me.astype(jnp.int32).reshape(1),
                              own_shapes(w), own_shapes(m), own_shapes(v))
    for n in SMALL:
        out[n] = [r.reshape(w[n].shape) for r in small[n]]
    loss = loss[0, 0]
    return (loss, grad_x[None], *[out[n][0] for n in WEIGHTS], *[out[n][1] for n in WEIGHTS],
            *[out[n][2] for n in WEIGHTS], *[out[n][3] for n in WEIGHTS])
```

```python
import functools

import jax
import jax.numpy as jnp
from jax import lax
from jax.experimental import pallas as pl
from jax.experimental.pallas import tpu as pltpu

F32 = jnp.float32
BF16 = jnp.bfloat16
E = pl.Element

D = 1024
BW = 512
IN_W = 8448
GL0 = 4352
FF = 2816
N_HEADS = 8
N_KV = 2
HD = 64
ATT_BLK = 128
EPS = 1e-6
LRU_C = 8.0
NEG_INF = -1e30
DEPTH = 2
NDEV = 8
CONV_A, CONV_B, CONV_D = 4, 3, 31
C_AX, C_AG, C_BV, C_BC, C_BB, C_Q, C_K, C_V, C_D1, C_D2 = 0, 512, 1024, 1536, 2048, 2560, 3072, 3200, 3328, 3840
CW_A, CW_B, CW_D, CW_ROWS = 0, 4, 8, 40
V_CAB, V_BX, V_BA, V_LAM, V_CDB, V_LNG, V_LNB, V_SINK, V_ROWS = 0, 1, 2, 3, 4, 5, 6, 7, 8
HALO = 32
W_IN_PARTS = ((0, 768), (768, 256))

ADAM_LR, ADAM_B1, ADAM_B2, ADAM_EPS, ADAM_WD, ADAM_STEP = 0.001, 0.9, 0.999, 1e-08, 0.01, 10

VMEM_LIMIT = 56 * 1024 * 1024

_NN = (((1,), (0,)), ((), ()))
_NT = (((1,), (1,)), ((), ()))
_TN = (((0,), (0,)), ((), ()))


def _dot(a, b, dims):
    return lax.dot_general(a.astype(BF16), b.astype(BF16), dims, preferred_element_type=F32)


def _cparams(n_axes):
    return pltpu.CompilerParams(dimension_semantics=("arbitrary",) * n_axes, vmem_limit_bytes=VMEM_LIMIT)


def _sds(shape, dtype):
    return jax.ShapeDtypeStruct(tuple(shape), dtype)


def _sigmoid(x):
    return jax.nn.sigmoid(x)


def _neg_expm1(x):
    p = x * (1.0 + x * (0.5 + x * (1.0 / 6.0 + x * (1.0 / 24.0 + x * (1.0 / 120.0)))))
    return jnp.where(x > -0.1, -p, 1.0 - jnp.exp(x))


def _softplus(z):
    return jnp.maximum(z, 0.0) + jnp.log1p(jnp.exp(-jnp.abs(z)))


def _gelu_and_grad(x):
    c = 0.7978845608028654
    inner = c * (x + 0.044715 * x * x * x)
    t = jnp.tanh(inner)
    g = 0.5 * x * (1.0 + t)
    dg = 0.5 * (1.0 + t) + 0.5 * x * (1.0 - t * t) * c * (1.0 + 3.0 * 0.044715 * x * x)
    return g, dg


ANY = pl.BlockSpec(memory_space=pl.ANY)
MESH = pl.DeviceIdType.MESH


def _mesh_pos():
    return lax.axis_index("x"), lax.axis_index("y"), lax.axis_index("c")


def _dev_index(px, py, pc):
    return 4 * px + 2 * py + pc


class CommJob:
    def __init__(self, inputs, aliases, out_shapes, sem_shapes, start, finish, relay=None):
        self.inputs, self.aliases, self.out_shapes, self.sem_shapes = list(inputs), dict(aliases), list(out_shapes), list(sem_shapes)
        self.start, self.finish, self.relay = start, finish, relay


def _call(body, comm, args, *, grid, in_specs, out_specs, out_shape, scratch_shapes=(), name, aliases=None):
    single = not isinstance(out_shape, (list, tuple))
    out_specs = [out_specs] if single else list(out_specs)
    out_shape = [out_shape] if single else list(out_shape)
    scratch_shapes = list(scratch_shapes)
    n_in, n_out, n_scr, n_axes = len(in_specs), len(out_shape), len(scratch_shapes), len(grid)
    params = pltpu.CompilerParams(dimension_semantics=("arbitrary",) * n_axes, vmem_limit_bytes=VMEM_LIMIT)
    io_aliases = dict(aliases or {})
    if comm is None:
        outs = pl.pallas_call(body, grid=grid, in_specs=in_specs, out_specs=out_specs, out_shape=out_shape,
                              scratch_shapes=scratch_shapes, input_output_aliases=io_aliases, compiler_params=params,
                              name=name)(*args)
        return (outs[0] if single else outs), []
    c_in, c_out = len(comm.inputs), len(comm.out_shapes)
    io_aliases.update({n_in + i: n_out + o for i, o in comm.aliases.items()})

    def wrapped(*refs):
        ins, cins = refs[:n_in], refs[n_in:n_in + c_in]
        outs = refs[n_in + c_in:n_in + c_in + n_out]
        couts = refs[n_in + c_in + n_out:n_in + c_in + n_out + c_out]
        rest = refs[n_in + c_in + n_out + c_out:]
        scr, sems = rest[:n_scr], rest[n_scr:]
        first = functools.reduce(lambda a, b: a & b, [pl.program_id(a) == 0 for a in range(n_axes)])
        last = functools.reduce(lambda a, b: a & b, [pl.program_id(a) == pl.num_programs(a) - 1 for a in range(n_axes)])

        @pl.when(first)
        def _():
            comm.start(cins, couts, sems)

        if comm.relay is not None:
            step = functools.reduce(lambda a, b: a * grid[b] + pl.program_id(b), range(1, n_axes), pl.program_id(0))
            n_steps = functools.reduce(lambda a, b: a * b, grid)

            @pl.when(step == 2 * n_steps // 3)
            def _():
                comm.relay(cins, couts, sems)

        body(*ins, *outs, *scr)

        @pl.when(last)
        def _():
            comm.finish(cins, couts, sems)

    outs = pl.pallas_call(
        wrapped, grid=grid, in_specs=list(in_specs) + [ANY] * c_in, out_specs=out_specs + [ANY] * c_out,
        out_shape=out_shape + comm.out_shapes, scratch_shapes=scratch_shapes + comm.sem_shapes,
        input_output_aliases=io_aliases, compiler_params=params, name=name)(*args, *comm.inputs)
    res, cres = outs[:n_out], outs[n_out:]
    return (res[0] if single else res), cres


def _comm_only(comm, name):
    c_in, c_out = len(comm.inputs), len(comm.out_shapes)

    def body(*refs):
        cins, couts, sems = refs[:c_in], refs[c_in:c_in + c_out], refs[c_in + c_out:]
        comm.start(cins, couts, sems)
        if comm.relay is not None:
            comm.relay(cins, couts, sems)
        comm.finish(cins, couts, sems)

    return pl.pallas_call(body, in_specs=[ANY] * c_in, out_specs=[ANY] * c_out, out_shape=comm.out_shapes,
                          scratch_shapes=comm.sem_shapes, input_output_aliases=comm.aliases, name=name)(*comm.inputs)


def gather_job(pieces):
    inputs, aliases, out_shapes, plan, where = [], {}, [], [], {}
    for key, shard, gathered, row0, nrows in pieces:
        if key not in where:
            where[key] = (len(inputs), len(out_shapes))
            inputs.append(shard)
            if gathered is not None:
                aliases[len(inputs)] = len(out_shapes)
                inputs.append(gathered)
            out_shapes.append(_sds((NDEV * shard.shape[0], shard.shape[1]), shard.dtype))
        plan.append((*where[key], shard.shape[0], row0, nrows))
    n = len(plan)

    def copies(cins, couts, sems):
        send_sems, recv_sems, local_sems = sems
        x, y, c = _mesh_pos()
        me, sibling = (x, y, c), (x, y, 1 - c)
        xn, yn, dg = (1 - x, y), (x, 1 - y), (1 - x, 1 - y)
        local, first, pass1, pass2, got_ici, got_fwd, got_d2d = [], [], [], [], [], [], []
        for p, (i_shard, i_out, rows, row0, nrows) in enumerate(plan):
            src = cins[i_shard].at[pl.ds(row0, nrows), :]
            half = cins[i_shard].shape[1] // 2
            left, right, whole = pl.ds(0, half), pl.ds(half, half), slice(None)

            def slot(dev, lanes, i_out=i_out, rows=rows, row0=row0, nrows=nrows):
                return couts[i_out].at[pl.ds(_dev_index(*dev) * rows + row0, nrows), lanes]

            def copy(g, dev, to, lanes=whole, src=None, p=p, slot=slot):
                return pltpu.make_async_remote_copy(
                    src_ref=slot(dev, lanes) if src is None else src, dst_ref=slot(dev, lanes),
                    send_sem=send_sems.at[g, p], recv_sem=recv_sems.at[g, p], device_id=to, device_id_type=MESH)

            local.append(pltpu.make_async_copy(src, slot(me, whole), local_sems.at[p]))
            first += [copy(0, me, sibling, src=src), copy(1, me, (*xn, c), src=src), copy(2, me, (*yn, c), src=src)]
            got_ici += [copy(1, (*xn, c), me), copy(2, (*yn, c), me)]
            pass1 += [copy(3, (*xn, c), (*yn, c), left), copy(4, (*yn, c), (*xn, c), right),
                      copy(5, (*xn, c), sibling), copy(6, (*yn, c), sibling)]
            got_fwd += [copy(3, (*dg, c), me, left), copy(4, (*dg, c), me, right)]
            pass2 += [copy(7, (*dg, c), sibling, left), copy(8, (*dg, c), sibling, right)]
            got_d2d += [copy(0, sibling, me), copy(5, (*xn, 1 - c), me), copy(6, (*yn, 1 - c), me),
                        copy(7, (*dg, 1 - c), me, left), copy(8, (*dg, 1 - c), me, right)]
        return local, first, pass1, pass2, got_ici, got_fwd, got_d2d

    def start(cins, couts, sems):
        local, first, *_ = copies(cins, couts, sems)
        for cp in local + first:
            cp.start()

    def pass_on(cins, couts, sems):
        _, _, pass1, _, got_ici, _, _ = copies(cins, couts, sems)
        for cp in got_ici:
            cp.wait_recv()
        for cp in pass1:
            cp.start()

    def finish(cins, couts, sems):
        local, first, pass1, pass2, _, got_fwd, got_d2d = copies(cins, couts, sems)
        for cp in got_fwd:
            cp.wait_recv()
        for cp in pass2:
            cp.start()
        for cp in got_d2d:
            cp.wait_recv()
        for cp in first + pass1 + pass2:
            cp.wait_send()
        for cp in local:
            cp.wait()

    sem_shapes = [pltpu.SemaphoreType.DMA((9, n)), pltpu.SemaphoreType.DMA((9, n)), pltpu.SemaphoreType.DMA((n,))]
    return CommJob(inputs, aliases, out_shapes, sem_shapes, start, finish, relay=pass_on)


def sibling_exchange_job(grads):
    n = len(grads)

    def copies(cins, couts, sems):
        send_sems, recv_sems = sems
        x, y, c = _mesh_pos()
        return [pltpu.make_async_remote_copy(
            src_ref=cins[q].at[:, 1 - c], dst_ref=couts[q], send_sem=send_sems.at[q], recv_sem=recv_sems.at[q],
            device_id=(x, y, 1 - c), device_id_type=MESH) for q in range(n)]

    def start(cins, couts, sems):
        for cp in copies(cins, couts, sems):
            cp.start()

    def finish(cins, couts, sems):
        cps = copies(cins, couts, sems)
        for cp in cps:
            cp.wait_recv()
        for cp in cps:
            cp.wait_send()

    return CommJob(grads, {}, [_sds((4,) + g.shape[2:], g.dtype) for g in grads],
                   [pltpu.SemaphoreType.DMA((n,)), pltpu.SemaphoreType.DMA((n,))], start, finish)


def chip_exchange_job(pieces):
    inputs, aliases, out_shapes, plan, where = [], {}, [], [], {}
    for partial, contrib, key, layer, row0, nrows, col0, cols in pieces:
        if key not in where:
            where[key] = len(out_shapes)
            out_shapes.append(_sds((4, DEPTH, partial.shape[1], cols), partial.dtype))
            if contrib is not None:
                aliases[len(inputs)] = where[key]
                inputs.append(contrib)
        plan.append((len(inputs), where[key], layer, row0, nrows, col0, partial.shape[2]))
        inputs.append(partial)
    n = len(plan)

    def copies(cins, couts, sems):
        send_sems, recv_sems, local_sems = sems
        x, y, c = _mesh_pos()
        mine = 2 * x + y
        local, sends, recvs = [], [], []
        for p, (i_in, i_out, layer, row0, nrows, col0, ncols) in enumerate(plan):
            rows, lanes = pl.ds(row0, nrows), pl.ds(col0, ncols)
            local.append(pltpu.make_async_copy(cins[i_in].at[mine, rows, :], couts[i_out].at[mine, layer, rows, lanes],
                                               local_sems.at[p]))
            for j, (cx, cy) in enumerate([(1 - x, y), (x, 1 - y), (1 - x, 1 - y)]):
                theirs = 2 * cx + cy

                def copy(slot_there, j=j, p=p, cx=cx, cy=cy, theirs=theirs, i_in=i_in, i_out=i_out, layer=layer,
                         rows=rows, lanes=lanes):
                    return pltpu.make_async_remote_copy(
                        src_ref=cins[i_in].at[theirs, rows, :], dst_ref=couts[i_out].at[slot_there, layer, rows, lanes],
                        send_sem=send_sems.at[j, p], recv_sem=recv_sems.at[j, p], device_id=(cx, cy, c), device_id_type=MESH)
                sends.append(copy(mine))
                recvs.append(copy(theirs))
        return local, sends, recvs

    def start(cins, couts, sems):
        local, sends, _ = copies(cins, couts, sems)
        for cp in local + sends:
            cp.start()

    def finish(cins, couts, sems):
        local, sends, recvs = copies(cins, couts, sems)
        for cp in recvs:
            cp.wait_recv()
        for cp in sends:
            cp.wait_send()
        for cp in local:
            cp.wait()

    sem_shapes = [pltpu.SemaphoreType.DMA((3, n)), pltpu.SemaphoreType.DMA((3, n)), pltpu.SemaphoreType.DMA((n,))]
    return CommJob(inputs, aliases, out_shapes, sem_shapes, start, finish)


def fwd_proj(x, g1, wt_in, l, comm=None):
    s = x.shape[0]
    tm = min(512, s)
    tn = 1408

    def body(x_ref, g_ref, w_ref, o_ref, xn_ref):
        @pl.when(pl.program_id(1) == 0)
        def _():
            xv = x_ref[...]
            r = lax.rsqrt(jnp.mean(xv * xv, axis=-1, keepdims=True) + EPS)
            xn_ref[...] = (xv * r * g_ref[l:l + 1, :]).astype(BF16)

        o_ref[...] = _dot(xn_ref[...], w_ref[...], _NT).astype(BF16)

    return _call(
        body, comm, (x, g1, wt_in), grid=(s // tm, IN_W // tn),
        in_specs=[pl.BlockSpec((tm, D), lambda i, j: (i, 0)),
                  pl.BlockSpec((DEPTH, D), lambda i, j: (0, 0)),
                  pl.BlockSpec((tn, D), lambda i, j: (j, 0))],
        out_specs=pl.BlockSpec((tm, tn), lambda i, j: (i, j)),
        out_shape=_sds((s, IN_W), BF16),
        scratch_shapes=[pltpu.VMEM((tm, D), BF16)], name=f"fwd_proj{l}")


def _scan_fwd(a_ref, u_ref, h_ref, h0, n_rows):
    row = lax.broadcasted_iota(jnp.int32, (8, BW), 0)

    def body(g, hprev):
        r = pl.multiple_of(g * 8, 8)
        a = a_ref[pl.ds(r, 8), :]
        u = u_ref[pl.ds(r, 8), :]
        for sft in (1, 2, 4):
            a_sh = jnp.where(row >= sft, pltpu.roll(a, sft, 0), 1.0)
            u_sh = jnp.where(row >= sft, pltpu.roll(u, sft, 0), 0.0)
            u = u + a * u_sh
            a = a * a_sh
        h = u + a * hprev
        h_ref[pl.ds(r, 8), :] = h
        return h[7:8, :]

    return lax.fori_loop(0, n_rows // 8, body, h0)


def _scan_bwd(b_ref, g_ref, o_ref, c0, n_rows):
    row = lax.broadcasted_iota(jnp.int32, (8, BW), 0)

    def body(k, cnext):
        r = pl.multiple_of((n_rows // 8 - 1 - k) * 8, 8)
        b = b_ref[pl.ds(r, 8), :]
        g = g_ref[pl.ds(r, 8), :]
        for sft in (1, 2, 4):
            b_sh = jnp.where(row < 8 - sft, pltpu.roll(b, 8 - sft, 0), 1.0)
            g_sh = jnp.where(row < 8 - sft, pltpu.roll(g, 8 - sft, 0), 0.0)
            g = g + b * g_sh
            b = b * b_sh
        o = g + b * cnext
        o_ref[pl.ds(r, 8), :] = o
        return o[0:1, :]

    return lax.fori_loop(0, n_rows // 8, body, c0)


def _shifted_copies(buf, shifted, n_rows):
    for r in range(1, 8):
        shifted[r - 1, 0:n_rows - 8, :] = buf[pl.ds(r, n_rows - 8), :]


def _window(buf, shifted, off, t):
    r = off % 8
    return buf[pl.ds(off, t), :] if r == 0 else shifted[r - 1, pl.ds(off - r, t), :]


def _branch_fwd_math(cur_ref, halo_ref, cw_ref, vec_ref, wx_ref, wa_ref, bufa, bufb, bufd, xd, first, t, saved_ref=None):
    def halo(c0):
        v = halo_ref[:, c0:c0 + BW].astype(F32)
        return jnp.where(first, 0.0, v)

    def cur(c0):
        return cur_ref[:, c0:c0 + BW].astype(F32)

    out = {}
    bufa[0:HALO, :] = halo(C_AX)
    bufa[HALO:HALO + t, :] = cur(C_AX)
    ca = jnp.zeros((t, BW), F32) + vec_ref[V_CAB:V_CAB + 1, :]
    for k in range(CONV_A):
        ca = ca + cw_ref[CW_A + k:CW_A + k + 1, :] * bufa[pl.ds(HALO - (CONV_A - 1) + k, t), :]
    if saved_ref is None:
        gi = _sigmoid(_dot(ca, wx_ref[...], _NN) + vec_ref[V_BX:V_BX + 1, :])
        gr = _sigmoid(_dot(ca, wa_ref[...], _NN) + vec_ref[V_BA:V_BA + 1, :])
    else:
        gi, gr = saved_ref[:, BW:2 * BW], saved_ref[:, 2 * BW:3 * BW]
    sp = _softplus(-vec_ref[V_LAM:V_LAM + 1, :])
    la = -LRU_C * sp * gr
    a = jnp.exp(la)
    mult = jnp.sqrt(_neg_expm1(2.0 * la))
    out.update(ca=ca, gi=gi, gr=gr, sp=sp, a=a, mult=mult)
    bufb[0:HALO, :] = halo(C_BC) * halo(C_BV)
    bufb[HALO:HALO + t, :] = cur(C_BC) * cur(C_BV)
    cb = jnp.zeros((t, BW), F32)
    for k in range(CONV_B):
        cb = cb + cw_ref[CW_B + k:CW_B + k + 1, :] * bufb[pl.ds(HALO - (CONV_B - 1) + k, t), :]
    out.update(cb=cb)
    bufd[0:HALO, :] = halo(C_D1) * _sigmoid(halo(C_D2))
    s2 = _sigmoid(cur(C_D2))
    bufd[HALO:HALO + t, :] = cur(C_D1) * s2
    _shifted_copies(bufd, xd, t + HALO)
    if saved_ref is None:
        cd = jnp.zeros((t, BW), F32) + vec_ref[V_CDB:V_CDB + 1, :]
        for k in range(CONV_D):
            cd = cd + cw_ref[CW_D + k:CW_D + k + 1, :] * _window(bufd, xd, HALO - (CONV_D - 1) + k, t)
    else:
        cd = saved_ref[:, 0:BW]
    mu = jnp.mean(cd, axis=-1, keepdims=True)
    xc = cd - mu
    rstd = lax.rsqrt(jnp.mean(xc * xc, axis=-1, keepdims=True) + EPS)
    xh = xc * rstd
    ln = xh * vec_ref[V_LNG:V_LNG + 1, :] + vec_ref[V_LNB:V_LNB + 1, :]
    out.update(s2=s2, xh=xh, rstd=rstd, ln=ln, cd=cd)
    return out


def fwd_branch(proj, convw, vecs, wx_bd, wa_bd, l, comm=None):
    s = proj.shape[0]
    t = min(256, s)

    def body(cur_ref, halo_ref, cw_ref, vec_ref, wx_ref, wa_ref, pre_ref, h_ref, sv_ref, bufa, bufb, bufd, xd, a_s, u_s, hcar):
        first = pl.program_id(0) == 0

        @pl.when(first)
        def _():
            hcar[...] = jnp.zeros((1, BW), F32)

        v = _branch_fwd_math(cur_ref, halo_ref, cw_ref, vec_ref, wx_ref, wa_ref, bufa, bufb, bufd, xd, first, t)
        a_s[...] = v["a"]
        u_s[...] = v["ca"] * v["gi"] * v["mult"]
        sv_ref[:, 0:BW] = v["cd"]
        sv_ref[:, BW:2 * BW] = v["gi"]
        sv_ref[:, 2 * BW:3 * BW] = v["gr"]
        hcar[...] = _scan_fwd(a_s, u_s, h_ref, hcar[...], t)
        gg, _ = _gelu_and_grad(cur_ref[:, C_AG:C_AG + BW].astype(F32))
        pre_ref[:, 0:BW] = (h_ref[...] * gg).astype(BF16)
        pre_ref[:, BW:2 * BW] = (cur_ref[:, C_BB:C_BB + BW].astype(F32) * v["cb"]).astype(BF16)
        ln = v["ln"]
        pre_ref[:, 2 * BW:3 * BW] = (ln * _sigmoid(ln)).astype(BF16)

    hb = t // HALO
    return _call(
        body, comm, (proj, proj, convw, vecs, wx_bd, wa_bd), grid=(s // t,),
        in_specs=[pl.BlockSpec((t, GL0), lambda i: (i, 0)),
                  pl.BlockSpec((HALO, GL0), lambda i: (jnp.maximum(i * hb - 1, 0), 0)),
                  pl.BlockSpec((None, CW_ROWS, BW), lambda i: (l, 0, 0)),
                  pl.BlockSpec((None, V_ROWS, BW), lambda i: (l, 0, 0)),
                  pl.BlockSpec((None, BW, BW), lambda i: (l, 0, 0)),
                  pl.BlockSpec((None, BW, BW), lambda i: (l, 0, 0))],
        out_specs=[pl.BlockSpec((t, 3 * BW), lambda i: (i, 0)), pl.BlockSpec((t, BW), lambda i: (i, 0)),
                   pl.BlockSpec((t, 3 * BW), lambda i: (i, 0))],
        out_shape=[_sds((s, 3 * BW), BF16), _sds((s, BW), F32), _sds((s, 3 * BW), F32)],
        scratch_shapes=[pltpu.VMEM((t + HALO, BW), F32)] * 3 + [pltpu.VMEM((7, t + HALO - 8, BW), F32)]
        + [pltpu.VMEM((t, BW), F32)] * 2 + [pltpu.VMEM((1, BW), F32)],
        name=f"fwd_branch{l}")


GRP = N_HEADS // N_KV


ATT_SUB = 2


def _attn_mask_bias(first_block):
    shape = (GRP * ATT_BLK, 2 * ATT_BLK)
    qi = lax.broadcasted_iota(jnp.int32, shape, 0) & (ATT_BLK - 1)
    ki = lax.broadcasted_iota(jnp.int32, shape, 1)
    dist = qi + ATT_BLK - ki
    valid = (dist >= 0) & (dist < ATT_BLK)
    if first_block is not None:
        valid = valid & (jnp.logical_not(first_block) | (ki >= ATT_BLK))
    return dist.astype(F32), valid


def _attn_units(q_ref, kvp_ref, kvc_ref, first_step):
    units = []
    for b in range(ATT_SUB):
        rows = slice(b * ATT_BLK, (b + 1) * ATT_BLK)
        if b == 0:
            prev = lambda c0, c1: kvp_ref[:, c0:c1]
        else:
            prev = lambda c0, c1, b=b: kvc_ref[(b - 1) * ATT_BLK:b * ATT_BLK, c0:c1]
        for hk in range(N_KV):
            units.append(dict(b=b, hk=hk, rows=rows, q=lambda c0, c1, rows=rows: q_ref[rows, c0:c1], prev=prev,
                              cur=lambda c0, c1, rows=rows: kvc_ref[rows, c0:c1], first=first_step if b == 0 else None))
    return units


def _per_head(hk, values):
    hl = lax.broadcasted_iota(jnp.int32, (GRP * ATT_BLK, 1), 0) // ATT_BLK
    out = values[GRP - 1]
    for j in range(GRP - 2, -1, -1):
        out = jnp.where(hl == j, values[j], out)
    return out


def _attn_probs(units, vec_ref):
    us = range(len(units))
    heads = [range(u["hk"] * GRP, (u["hk"] + 1) * GRP) for u in units]
    masks = {id(u["first"]): _attn_mask_bias(u["first"]) for u in units}
    distf = [masks[id(u["first"])][0] for u in units]
    valid = [masks[id(u["first"])][1] for u in units]
    q4 = [jnp.concatenate([units[i]["q"](h * HD, (h + 1) * HD) for h in heads[i]], axis=0) for i in us]
    kcol = [(u["hk"] * HD, (u["hk"] + 1) * HD) for u in units]
    vcol = [((N_KV + u["hk"]) * HD, (N_KV + u["hk"] + 1) * HD) for u in units]
    k2 = [jnp.concatenate([units[i]["prev"](*kcol[i]), units[i]["cur"](*kcol[i])], axis=0) for i in us]
    v2 = [jnp.concatenate([units[i]["prev"](*vcol[i]), units[i]["cur"](*vcol[i])], axis=0) for i in us]
    slope = [_per_head(units[i]["hk"], [2.0 ** (-8.0 * (h + 1) / N_HEADS) for h in heads[i]]) for i in us]
    sink = [_per_head(units[i]["hk"], [vec_ref[V_SINK:V_SINK + 1, h:h + 1] for h in heads[i]]) for i in us]
    sc = [_dot(q4[i], k2[i], _NT) for i in us]
    sc = [jnp.where(valid[i], sc[i] * (HD ** -0.5) - slope[i] * distf[i], NEG_INF) for i in us]
    m = [jnp.maximum(jnp.max(sc[i], axis=-1, keepdims=True), sink[i]) for i in us]
    p = [jnp.exp(sc[i] - m[i]) for i in us]
    es = [jnp.exp(sink[i] - m[i]) for i in us]
    inv = [1.0 / (jnp.sum(p[i], axis=-1, keepdims=True) + es[i]) for i in us]
    return [(q4[i], k2[i], v2[i], p[i] * inv[i], es[i] * inv[i]) for i in us]


def fwd_attn(proj, vecs, l, comm=None):
    s = proj.shape[0]
    t = ATT_SUB * ATT_BLK

    def body(q_ref, kvp_ref, kvc_ref, vec_ref, o_ref):
        units = _attn_units(q_ref, kvp_ref, kvc_ref, pl.program_id(0) == 0)
        groups = _attn_probs(units, vec_ref)
        outs = [_dot(p, v2, _NN).astype(BF16) for _, _, v2, p, _ in groups]
        for u, out in zip(units, outs):
            for j in range(GRP):
                h = u["hk"] * GRP + j
                o_ref[u["rows"], h * HD:(h + 1) * HD] = out[j * ATT_BLK:(j + 1) * ATT_BLK]

    return _call(
        body, comm, (proj, proj, proj, vecs), grid=(s // t,),
        in_specs=[pl.BlockSpec((t, BW), lambda i: (i, C_Q // BW)),
                  pl.BlockSpec((ATT_BLK, 256), lambda i: (jnp.maximum(ATT_SUB * i - 1, 0), C_K // 256)),
                  pl.BlockSpec((t, 256), lambda i: (i, C_K // 256)),
                  pl.BlockSpec((None, V_ROWS, BW), lambda i: (l, 0, 0))],
        out_specs=pl.BlockSpec((t, BW), lambda i: (i, 0)),
        out_shape=_sds((s, BW), BF16), name=f"fwd_attn{l}")


def fwd_merge(x, proj, pre_abd, pre_c, wt_a, wt_b, wt_c, wt_d, w_o, l, comm=None):
    s = x.shape[0]
    tm = min(256, s)

    def body(x_ref, gl_ref, pabd_ref, pc_ref, wa_ref, wb_ref, wc_ref, wd_ref, wo_ref, y_ref, mg_ref, h1_ref):
        pres = (pabd_ref[:, 0:BW], pabd_ref[:, BW:2 * BW], pc_ref[...], pabd_ref[:, 2 * BW:3 * BW])
        merged = jnp.zeros((tm, D), F32)
        for k, (pre, w_ref) in enumerate(zip(pres, (wa_ref, wb_ref, wc_ref, wd_ref))):
            yk = _dot(pre, w_ref[...], _NT)
            y_ref[:, k * D:(k + 1) * D] = yk.astype(BF16)
            merged = merged + _sigmoid(gl_ref[:, k * D:(k + 1) * D].astype(F32)) * yk
        mg_ref[...] = merged.astype(BF16)
        h1_ref[...] = x_ref[...] + _dot(merged, wo_ref[...], _NN)

    wspec = pl.BlockSpec((D, BW), lambda i: (0, 0))
    return _call(
        body, comm, (x, proj, pre_abd, pre_c, wt_a, wt_b, wt_c, wt_d, w_o), grid=(s // tm,),
        in_specs=[pl.BlockSpec((tm, D), lambda i: (i, 0)),
                  pl.BlockSpec((E(tm), E(4 * D)), lambda i: (i * tm, GL0)),
                  pl.BlockSpec((tm, 3 * BW), lambda i: (i, 0)),
                  pl.BlockSpec((tm, BW), lambda i: (i, 0)),
                  wspec, wspec, wspec, wspec,
                  pl.BlockSpec((D, D), lambda i: (0, 0))],
        out_specs=[pl.BlockSpec((tm, 4 * D), lambda i: (i, 0)), pl.BlockSpec((tm, D), lambda i: (i, 0)),
                   pl.BlockSpec((tm, D), lambda i: (i, 0))],
        out_shape=[_sds((s, 4 * D), BF16), _sds((s, D), BF16), _sds((s, D), F32)], name=f"fwd_merge{l}")


def fwd_ffn(h1, g2, wt_gate, wt_up, w_down, l, comm=None):
    s = h1.shape[0]
    tm = min(512, s)
    fc = FF // 2

    def body(h_ref, g_ref, wg_ref, wu_ref, wd_ref, xo_ref, fg_ref, fu_ref, hn_ref, acc_ref):
        j = pl.program_id(1)

        @pl.when(j == 0)
        def _():
            hv = h_ref[...]
            r = lax.rsqrt(jnp.mean(hv * hv, axis=-1, keepdims=True) + EPS)
            hn_ref[...] = (hv * r * g_ref[l:l + 1, :]).astype(BF16)
            acc_ref[...] = hv

        fg = _dot(hn_ref[...], wg_ref[...], _NT)
        fu = _dot(hn_ref[...], wu_ref[...], _NT)
        fg_ref[...] = fg.astype(BF16)
        fu_ref[...] = fu.astype(BF16)
        acc_ref[...] += _dot(fg * _sigmoid(fg) * fu, wd_ref[...], _NN)

        @pl.when(j == pl.num_programs(1) - 1)
        def _():
            xo_ref[...] = acc_ref[...]

    wspec = pl.BlockSpec((fc, D), lambda i, j: (j, 0))
    return _call(
        body, comm, (h1, g2, wt_gate, wt_up, w_down), grid=(s // tm, FF // fc),
        in_specs=[pl.BlockSpec((tm, D), lambda i, j: (i, 0)), pl.BlockSpec((DEPTH, D), lambda i, j: (0, 0)),
                  wspec, wspec, wspec],
        out_specs=[pl.BlockSpec((tm, D), lambda i, j: (i, 0)), pl.BlockSpec((tm, fc), lambda i, j: (i, j)),
                   pl.BlockSpec((tm, fc), lambda i, j: (i, j))],
        out_shape=[_sds((s, D), F32), _sds((s, FF), BF16), _sds((s, FF), BF16)],
        scratch_shapes=[pltpu.VMEM((tm, D), BF16), pltpu.VMEM((tm, D), F32)], name=f"fwd_ffn{l}")


def loss_head(x, gf, target):
    s = x.shape[0]
    tm = min(512, s)

    def body(x_ref, g_ref, t_ref, dx_ref, st_ref):
        @pl.when(pl.program_id(0) == 0)
        def _():
            st_ref[...] = jnp.zeros((8, D), F32)

        xv = x_ref[...]
        g = g_ref[...]
        r = lax.rsqrt(jnp.mean(xv * xv, axis=-1, keepdims=True) + EPS)
        n = xv * r
        err = n * g - t_ref[...]
        dy = err * (1.0 / D)
        dn = dy * g
        dx_ref[...] = r * (dn - n * jnp.mean(dn * n, axis=-1, keepdims=True))
        st_ref[0:1, :] += jnp.sum(dy * n, axis=0, keepdims=True)
        lsum = 0.5 * jnp.sum(jnp.mean(err * err, axis=-1, keepdims=True), axis=0, keepdims=True)
        st_ref[1:2, :] += jnp.broadcast_to(lsum, (1, D))

    return pl.pallas_call(
        body, grid=(s // tm,),
        in_specs=[pl.BlockSpec((tm, D), lambda i: (i, 0)), pl.BlockSpec((1, D), lambda i: (0, 0)),
                  pl.BlockSpec((tm, D), lambda i: (i, 0))],
        out_specs=[pl.BlockSpec((tm, D), lambda i: (i, 0)), pl.BlockSpec((8, D), lambda i: (0, 0))],
        out_shape=[_sds((s, D), F32), _sds((8, D), F32)],
        compiler_params=_cparams(1), name="loss_head")(x, gf, target)


def _edge_index(j, i, n_j, n_i):
    return jnp.where((j == 0) | (j == n_j - 1), i, n_i - 1)


def bwd_ffn(dxo, h1, fg, fu, g2, wt_gate, wt_up, w_down, l, comm=None):
    s = h1.shape[0]
    tm = min(512, s)
    fc = 256
    n_j, n_i = FF // fc, s // tm

    def body(dxo_ref, h_ref, fg_ref, fu_ref, g_ref, wg_ref, wu_ref, wd_ref,
             dh_ref, dwg_ref, dwu_ref, dwd_ref, st_ref, dhn, dxo_b, hn_b, ag, au, ad):
        j, i = pl.program_id(0), pl.program_id(1)
        rows = pl.ds(pl.multiple_of(i * tm, tm), tm)
        g = g_ref[l:l + 1, :]

        @pl.when(j == 0)
        def _():
            hv = h_ref[...]
            r = lax.rsqrt(jnp.mean(hv * hv, axis=-1, keepdims=True) + EPS)
            hn_b[rows, :] = (hv * r * g).astype(BF16)
            dxo_b[rows, :] = dxo_ref[...].astype(BF16)
            dhn[rows, :] = jnp.zeros((tm, D), F32)

        @pl.when((j == 0) & (i == 0))
        def _():
            st_ref[...] = jnp.zeros((8, D), F32)

        @pl.when(i == 0)
        def _():
            ag[...] = jnp.zeros((fc, D), F32)
            au[...] = jnp.zeros((fc, D), F32)
            ad[...] = jnp.zeros((fc, D), F32)

        fgv = fg_ref[...].astype(F32)
        fuv = fu_ref[...].astype(F32)
        sg = _sigmoid(fgv)
        sil = fgv * sg
        dxb = dxo_b[rows, :]
        hnb = hn_b[rows, :]
        d_act = _dot(dxb, wd_ref[...], _NT)
        ad[...] += _dot(sil * fuv, dxb, _TN)
        d_fg = (d_act * fuv * (sg * (1.0 + fgv * (1.0 - sg)))).astype(BF16)
        d_fu = (d_act * sil).astype(BF16)
        ag[...] += _dot(d_fg, hnb, _TN)
        au[...] += _dot(d_fu, hnb, _TN)
        dhn[rows, :] += _dot(d_fg, wg_ref[...], _NN) + _dot(d_fu, wu_ref[...], _NN)

        @pl.when(i == n_i - 1)
        def _():
            dwg_ref[...] = ag[...].astype(BF16)
            dwu_ref[...] = au[...].astype(BF16)
            dwd_ref[...] = ad[...].astype(BF16)

        @pl.when(j == n_j - 1)
        def _():
            hv = h_ref[...]
            r = lax.rsqrt(jnp.mean(hv * hv, axis=-1, keepdims=True) + EPS)
            n = hv * r
            dv = dhn[rows, :]
            dn = dv * g
            dh_ref[...] = dxo_ref[...] + r * (dn - n * jnp.mean(dn * n, axis=-1, keepdims=True))
            st_ref[0:1, :] += jnp.sum(dv * n, axis=0, keepdims=True)

    edge = lambda j, i: (_edge_index(j, i, n_j, n_i), 0)
    wspec = pl.BlockSpec((fc, D), lambda j, i: (j, 0))
    dwspec = pl.BlockSpec((fc, D), lambda j, i: (j, 0))
    return _call(
        body, comm, (dxo, h1, fg, fu, g2, wt_gate, wt_up, w_down), grid=(n_j, n_i),
        in_specs=[pl.BlockSpec((tm, D), edge),
                  pl.BlockSpec((tm, D), edge),
                  pl.BlockSpec((tm, fc), lambda j, i: (i, j)), pl.BlockSpec((tm, fc), lambda j, i: (i, j)),
                  pl.BlockSpec((DEPTH, D), lambda j, i: (0, 0)), wspec, wspec, wspec],
        out_specs=[pl.BlockSpec((tm, D), lambda j, i: (jnp.where(j == n_j - 1, i, 0), 0)),
                   dwspec, dwspec, dwspec, pl.BlockSpec((8, D), lambda j, i: (0, 0))],
        out_shape=[_sds((s, D), F32), _sds((FF, D), BF16), _sds((FF, D), BF16), _sds((FF, D), BF16), _sds((8, D), F32)],
        scratch_shapes=[pltpu.VMEM((s, D), F32), pltpu.VMEM((s, D), BF16), pltpu.VMEM((s, D), BF16),
                        pltpu.VMEM((fc, D), F32), pltpu.VMEM((fc, D), F32), pltpu.VMEM((fc, D), F32)],
        name=f"bwd_ffn{l}")


def bwd_merge(dh1, y4, proj, merged, pre_abd, pre_c, wt_a, wt_b, wt_c, wt_d, w_o, l, comm=None):
    s = dh1.shape[0]
    tm = min(256, s)
    n_i = s // tm

    def body(dh_ref, y_ref, gl_ref, mg_ref, pabd_ref, pc_ref, wa_ref, wb_ref, wc_ref, wd_ref, wo_ref,
             dgl_ref, dpre_ref, dwo_ref, dwa_ref, dwb_ref, dwc_ref, dwd_ref, ao, aa, ab, ac, ad):
        i = pl.program_id(0)
        accs = (aa, ab, ac, ad)

        @pl.when(i == 0)
        def _():
            ao[...] = jnp.zeros((D, D), F32)
            for acc in accs:
                acc[...] = jnp.zeros((D, BW), F32)

        dhb = dh_ref[...].astype(BF16)
        dmg = _dot(dhb, wo_ref[...], _NT)
        ao[...] += _dot(mg_ref[...], dhb, _TN)
        pres = (pabd_ref[:, 0:BW], pabd_ref[:, BW:2 * BW], pc_ref[...], pabd_ref[:, 2 * BW:3 * BW])
        for k, (pre, w_ref, acc) in enumerate(zip(pres, (wa_ref, wb_ref, wc_ref, wd_ref), accs)):
            gk = _sigmoid(gl_ref[:, k * D:(k + 1) * D].astype(F32))
            yk = y_ref[:, k * D:(k + 1) * D].astype(F32)
            dgl_ref[:, k * D:(k + 1) * D] = (dmg * yk * gk * (1.0 - gk)).astype(BF16)
            dyk = (dmg * gk).astype(BF16)
            dpre_ref[:, k * BW:(k + 1) * BW] = _dot(dyk, w_ref[...], _NN).astype(BF16)
            acc[...] += _dot(dyk, pre, _TN)

        @pl.when(i == n_i - 1)
        def _():
            dwo_ref[...] = ao[...].astype(BF16)
            for o_ref, acc in zip((dwa_ref, dwb_ref, dwc_ref, dwd_ref), accs):
                o_ref[...] = acc[...].astype(BF16)

    wspec = pl.BlockSpec((D, BW), lambda i: (0, 0))
    dwspec = pl.BlockSpec((D, BW), lambda i: (0, 0))
    return _call(
        body, comm, (dh1, y4, proj, merged, pre_abd, pre_c, wt_a, wt_b, wt_c, wt_d, w_o), grid=(n_i,),
        in_specs=[pl.BlockSpec((tm, D), lambda i: (i, 0)),
                  pl.BlockSpec((tm, 4 * D), lambda i: (i, 0)),
                  pl.BlockSpec((E(tm), E(4 * D)), lambda i: (i * tm, GL0)),
                  pl.BlockSpec((tm, D), lambda i: (i, 0)),
                  pl.BlockSpec((tm, 3 * BW), lambda i: (i, 0)),
                  pl.BlockSpec((tm, BW), lambda i: (i, 0)),
                  wspec, wspec, wspec, wspec,
                  pl.BlockSpec((D, D), lambda i: (0, 0))],
        out_specs=[pl.BlockSpec((E(tm), E(4 * D)), lambda i: (i * tm, GL0)),
                   pl.BlockSpec((tm, 4 * BW), lambda i: (i, 0)),
                   pl.BlockSpec((D, D), lambda i: (0, 0)), dwspec, dwspec, dwspec, dwspec],
        out_shape=[_sds((s, IN_W), BF16), _sds((s, 4 * BW), BF16), _sds((D, D), BF16)] + [_sds((D, BW), BF16)] * 4,
        scratch_shapes=[pltpu.VMEM((D, D), F32)] + [pltpu.VMEM((D, BW), F32)] * 4, name=f"bwd_merge{l}")


def bwd_attn(proj, dpre, vecs, l, comm=None):
    s = proj.shape[0]
    t = ATT_SUB * ATT_BLK
    grp = N_HEADS // N_KV

    def body(q_ref, kvp_ref, kvc_ref, do_ref, vec_ref, dq_ref, dkc_ref, dkp_ref, st_ref):
        @pl.when(pl.program_id(0) == 0)
        def _():
            st_ref[...] = jnp.zeros((8, 128), F32)

        lane = lax.broadcasted_iota(jnp.int32, (1, 128), 1)
        dsink = jnp.zeros((1, 128), F32)
        units = _attn_units(q_ref, kvp_ref, kvc_ref, pl.program_id(0) == 0)
        groups = _attn_probs(units, vec_ref)
        us = range(len(units))
        do4s = [jnp.concatenate([do_ref[u["rows"], h * HD:(h + 1) * HD] for h in range(u["hk"] * grp, (u["hk"] + 1) * grp)],
                                axis=0) for u in units]
        dps = [_dot(do4s[i], groups[i][2], _NT) for i in us]
        deltas = [jnp.sum(groups[i][3] * dps[i], axis=-1, keepdims=True) for i in us]
        dss = [groups[i][3] * (dps[i] - deltas[i]) * (HD ** -0.5) for i in us]
        dq4s = [_dot(dss[i], groups[i][1], _NN).astype(BF16) for i in us]
        dk2s = [_dot(dss[i], groups[i][0], _TN) for i in us]
        dv2s = [_dot(groups[i][3], do4s[i], _TN) for i in us]
        for i, u in enumerate(units):
            psd = groups[i][4] * deltas[i]
            for j in range(grp):
                h = u["hk"] * grp + j
                rows = slice(j * ATT_BLK, (j + 1) * ATT_BLK)
                dq_ref[u["rows"], h * HD:(h + 1) * HD] = dq4s[i][rows]
                dsink = dsink + jnp.where(lane == h, -jnp.sum(psd[rows], axis=0, keepdims=True), 0.0)
        for i, u in enumerate(units):
            nxt = [k for k, w in enumerate(units) if w["hk"] == u["hk"] and w["b"] == u["b"] + 1]
            for grad, c0 in ((dk2s, u["hk"] * HD), (dv2s, (N_KV + u["hk"]) * HD)):
                own = grad[i][ATT_BLK:]
                if nxt:
                    own = own + grad[nxt[0]][0:ATT_BLK]
                dkc_ref[u["rows"], c0:c0 + HD] = own.astype(BF16)
                if u["b"] == 0:
                    dkp_ref[:, c0:c0 + HD] = grad[i][0:ATT_BLK].astype(BF16)
        st_ref[0:1, :] += dsink

    return _call(
        body, comm, (proj, proj, proj, dpre, vecs), grid=(s // t,),
        in_specs=[pl.BlockSpec((t, BW), lambda i: (i, C_Q // BW)),
                  pl.BlockSpec((ATT_BLK, 256), lambda i: (jnp.maximum(ATT_SUB * i - 1, 0), C_K // 256)),
                  pl.BlockSpec((t, 256), lambda i: (i, C_K // 256)),
                  pl.BlockSpec((t, BW), lambda i: (i, 2)),
                  pl.BlockSpec((None, V_ROWS, BW), lambda i: (l, 0, 0))],
        out_specs=[pl.BlockSpec((t, BW), lambda i: (i, 0)), pl.BlockSpec((t, 256), lambda i: (i, 0)),
                   pl.BlockSpec((ATT_BLK, 256), lambda i: (i, 0)), pl.BlockSpec((8, 128), lambda i: (0, 0))],
        out_shape=[_sds((s, BW), BF16), _sds((s, 256), BF16), _sds((s // ATT_SUB, 256), BF16), _sds((8, 128), F32)],
        name=f"bwd_attn{l}")


def bwd_branch(proj, dproj, dpre, h, saved, dq, dkc, dkp, convw, vecs, wx_bd, wa_bd, l, comm=None):
    s = proj.shape[0]
    t = 2 * ATT_BLK
    nt = s // t
    nb = s // ATT_BLK
    hb = t // HALO

    def body(cur_ref, halo_ref, dpre_ref, h_ref, hp_ref, dq_ref, dkc_ref, dkp_ref,
             cw_ref, vec_ref, wx_ref, wa_ref, sv_ref, dproj_in, dp_ref, dcw_ref, dvec_ref, dwx_ref, dwa_ref,
             bufa, bufb, bufd, xd, xg, a_ext, hbuf, b_s, g_s, dh_s, ga, gb, gd, dhcar):
        del dproj_in
        step = pl.program_id(0)
        ti = nt - 1 - step
        first = ti == 0

        @pl.when(step == 0)
        def _():
            dcw_ref[...] = jnp.zeros((CW_ROWS, BW), F32)
            dvec_ref[...] = jnp.zeros((V_ROWS, BW), F32)
            dwx_ref[...] = jnp.zeros((BW, BW), F32)
            dwa_ref[...] = jnp.zeros((BW, BW), F32)
            dhcar[...] = jnp.zeros((1, BW), F32)
            a_ext[t:t + 8, :] = jnp.zeros((8, BW), F32)
            ga[t:t + 8, :] = jnp.zeros((8, BW), F32)
            gb[t:t + 8, :] = jnp.zeros((8, BW), F32)
            gd[t:t + HALO, :] = jnp.zeros((HALO, BW), F32)

        def cur(c0):
            return cur_ref[:, c0:c0 + BW].astype(F32)

        def rsum(v):
            return jnp.sum(v, axis=0, keepdims=True)

        def put(c0, v):
            dp_ref[:, c0:c0 + BW] = v.astype(BF16)

        v = _branch_fwd_math(cur_ref, halo_ref, cw_ref, vec_ref, wx_ref, wa_ref, bufa, bufb, bufd, xd, first, t, sv_ref)
        ca, gi, gr, sp, a, mult = v["ca"], v["gi"], v["gr"], v["sp"], v["a"], v["mult"]
        dpa = dpre_ref[:, 0:BW].astype(F32)
        gg, dgg = _gelu_and_grad(cur(C_AG))
        hv = h_ref[...]
        put(C_AG, dpa * hv * dgg)
        a_ext[0:t, :] = a
        b_s[...] = a_ext[pl.ds(1, t), :]
        g_s[...] = dpa * gg
        dhcar[...] = _scan_bwd(b_s, g_s, dh_s, dhcar[...], t)
        a_ext[t:t + 1, :] = a[0:1, :]
        dh = dh_s[...]
        hbuf[0:8, :] = jnp.where(first, 0.0, hp_ref[...])
        hbuf[8:8 + t, :] = hv
        da = dh * hbuf[pl.ds(7, t), :]
        d_ca = dh * gi * mult
        d_gi = dh * ca * mult
        d_mult = dh * ca * gi
        d_la = da * a - d_mult * (a * a) / mult
        lam = vec_ref[V_LAM:V_LAM + 1, :]
        dvec_ref[V_LAM:V_LAM + 1, :] += rsum(d_la * gr) * (LRU_C * _sigmoid(-lam))
        d_gr = d_la * (-LRU_C * sp)
        d_zr = d_gr * gr * (1.0 - gr)
        d_zi = d_gi * gi * (1.0 - gi)
        dvec_ref[V_BA:V_BA + 1, :] += rsum(d_zr)
        dvec_ref[V_BX:V_BX + 1, :] += rsum(d_zi)
        dwa_ref[...] += _dot(ca, d_zr, _TN)
        dwx_ref[...] += _dot(ca, d_zi, _TN)
        d_ca = d_ca + _dot(d_zi, wx_ref[...], _NT) + _dot(d_zr, wa_ref[...], _NT)
        dvec_ref[V_CAB:V_CAB + 1, :] += rsum(d_ca)
        ga[0:t, :] = d_ca
        d_ax = jnp.zeros((t, BW), F32)
        for k in range(CONV_A):
            d_ax = d_ax + cw_ref[CW_A + k:CW_A + k + 1, :] * ga[pl.ds(CONV_A - 1 - k, t), :]
            dcw_ref[CW_A + k:CW_A + k + 1, :] += rsum(d_ca * bufa[pl.ds(HALO - (CONV_A - 1) + k, t), :])
        ga[t:t + 8, :] = d_ca[0:8, :]
        put(C_AX, d_ax)
        dpb = dpre_ref[:, BW:2 * BW].astype(F32)
        put(C_BB, dpb * v["cb"])
        d_cb = dpb * cur(C_BB)
        gb[0:t, :] = d_cb
        d_cbin = jnp.zeros((t, BW), F32)
        for k in range(CONV_B):
            d_cbin = d_cbin + cw_ref[CW_B + k:CW_B + k + 1, :] * gb[pl.ds(CONV_B - 1 - k, t), :]
            dcw_ref[CW_B + k:CW_B + k + 1, :] += rsum(d_cb * bufb[pl.ds(HALO - (CONV_B - 1) + k, t), :])
        gb[t:t + 8, :] = d_cb[0:8, :]
        put(C_BC, d_cbin * cur(C_BV))
        put(C_BV, d_cbin * cur(C_BC))
        dpd = dpre_ref[:, 3 * BW:4 * BW].astype(F32)
        ln, xh, rstd, s2 = v["ln"], v["xh"], v["rstd"], v["s2"]
        sg = _sigmoid(ln)
        d_ln = dpd * sg * (1.0 + ln * (1.0 - sg))
        dvec_ref[V_LNG:V_LNG + 1, :] += rsum(d_ln * xh)
        dvec_ref[V_LNB:V_LNB + 1, :] += rsum(d_ln)
        d_xh = d_ln * vec_ref[V_LNG:V_LNG + 1, :]
        d_cd = rstd * (d_xh - jnp.mean(d_xh, axis=-1, keepdims=True)
                       - xh * jnp.mean(d_xh * xh, axis=-1, keepdims=True))
        dvec_ref[V_CDB:V_CDB + 1, :] += rsum(d_cd)
        gd[0:t, :] = d_cd
        _shifted_copies(gd, xg, t + HALO)
        d_dg = jnp.zeros((t, BW), F32)
        for k in range(CONV_D):
            d_dg = d_dg + cw_ref[CW_D + k:CW_D + k + 1, :] * _window(gd, xg, CONV_D - 1 - k, t)
            dcw_ref[CW_D + k:CW_D + k + 1, :] += rsum(d_cd * _window(bufd, xd, HALO - (CONV_D - 1) + k, t))
        gd[t:t + HALO, :] = d_cd[0:HALO, :]
        put(C_D1, d_dg * s2)
        put(C_D2, d_dg * cur(C_D1) * s2 * (1.0 - s2))
        dp_ref[:, C_Q:C_Q + BW] = dq_ref[...]
        dkp = jnp.where(step == 0, 0.0, dkp_ref[...].astype(F32))
        dp_ref[0:t - ATT_BLK, C_K:C_K + 256] = dkc_ref[0:t - ATT_BLK, :]
        dp_ref[t - ATT_BLK:t, C_K:C_K + 256] = (dkc_ref[t - ATT_BLK:t, :].astype(F32) + dkp).astype(BF16)

    rev = lambda i: nt - 1 - i
    full = lambda r, c: pl.BlockSpec((r, c), lambda i: (0, 0))
    return _call(
        body, comm, (proj, proj, dpre, h, h, dq, dkc, dkp, convw, vecs, wx_bd, wa_bd, saved, dproj), grid=(nt,),
        in_specs=[pl.BlockSpec((t, GL0), lambda i: (rev(i), 0)),
                  pl.BlockSpec((HALO, GL0), lambda i: (jnp.maximum(rev(i) * hb - 1, 0), 0)),
                  pl.BlockSpec((t, 4 * BW), lambda i: (rev(i), 0)),
                  pl.BlockSpec((t, BW), lambda i: (rev(i), 0)),
                  pl.BlockSpec((8, BW), lambda i: (jnp.maximum(rev(i) * (t // 8) - 1, 0), 0)),
                  pl.BlockSpec((t, BW), lambda i: (rev(i), 0)),
                  pl.BlockSpec((t, 256), lambda i: (rev(i), 0)),
                  pl.BlockSpec((ATT_BLK, 256), lambda i: (jnp.minimum(rev(i) + 1, nt - 1), 0)),
                  pl.BlockSpec((None, CW_ROWS, BW), lambda i: (l, 0, 0)),
                  pl.BlockSpec((None, V_ROWS, BW), lambda i: (l, 0, 0)),
                  pl.BlockSpec((None, BW, BW), lambda i: (l, 0, 0)),
                  pl.BlockSpec((None, BW, BW), lambda i: (l, 0, 0)),
                  pl.BlockSpec((t, 3 * BW), lambda i: (rev(i), 0)),
                  pl.BlockSpec(memory_space=pl.ANY)],
        out_specs=[pl.BlockSpec((t, GL0), lambda i: (rev(i), 0)),
                   full(CW_ROWS, BW), full(V_ROWS, BW), full(BW, BW), full(BW, BW)],
        out_shape=[_sds((s, IN_W), BF16), _sds((CW_ROWS, BW), F32), _sds((V_ROWS, BW), F32),
                   _sds((BW, BW), F32), _sds((BW, BW), F32)],
        scratch_shapes=[pltpu.VMEM((t + HALO, BW), F32)] * 3 + [pltpu.VMEM((7, t + HALO - 8, BW), F32)] * 2
        + [pltpu.VMEM((t + 8, BW), F32), pltpu.VMEM((t + 8, BW), F32)]
        + [pltpu.VMEM((t, BW), F32)] * 3
        + [pltpu.VMEM((t + 8, BW), F32), pltpu.VMEM((t + 8, BW), F32), pltpu.VMEM((t + HALO, BW), F32),
           pltpu.VMEM((1, BW), F32)],
        aliases={13: 0}, name=f"bwd_branch{l}")


def bwd_proj(dproj, x, dh1, g1, wt_in, l, comm=None):
    s = x.shape[0]
    tm = min(512, s)
    ck = 1408
    n_j, n_i = IN_W // ck, s // tm

    def body(dp_ref, x_ref, dh_ref, g_ref, w_ref, dx_ref, dw_ref, st_ref, dxn, xn_b, acc):
        j, i = pl.program_id(0), pl.program_id(1)
        rows = pl.ds(pl.multiple_of(i * tm, tm), tm)
        g = g_ref[l:l + 1, :]

        @pl.when(j == 0)
        def _():
            xv = x_ref[...]
            r = lax.rsqrt(jnp.mean(xv * xv, axis=-1, keepdims=True) + EPS)
            xn_b[rows, :] = (xv * r * g).astype(BF16)
            dxn[rows, :] = jnp.zeros((tm, D), F32)

        @pl.when((j == 0) & (i == 0))
        def _():
            st_ref[...] = jnp.zeros((8, D), F32)

        @pl.when(i == 0)
        def _():
            acc[...] = jnp.zeros((ck, D), F32)

        dp = dp_ref[...]
        dxn[rows, :] += _dot(dp, w_ref[...], _NN)
        acc[...] += _dot(dp, xn_b[rows, :], _TN)

        @pl.when(i == n_i - 1)
        def _():
            dw_ref[...] = acc[...].astype(BF16)

        @pl.when(j == n_j - 1)
        def _():
            xv = x_ref[...]
            r = lax.rsqrt(jnp.mean(xv * xv, axis=-1, keepdims=True) + EPS)
            n = xv * r
            dv = dxn[rows, :]
            dn = dv * g
            dx_ref[...] = dh_ref[...] + r * (dn - n * jnp.mean(dn * n, axis=-1, keepdims=True))
            st_ref[0:1, :] += jnp.sum(dv * n, axis=0, keepdims=True)

    lastrow = lambda j, i: (jnp.where(j == n_j - 1, i, 0), 0)
    return _call(
        body, comm, (dproj, x, dh1, g1, wt_in), grid=(n_j, n_i),
        in_specs=[pl.BlockSpec((tm, ck), lambda j, i: (i, j)),
                  pl.BlockSpec((tm, D), lambda j, i: (_edge_index(j, i, n_j, n_i), 0)),
                  pl.BlockSpec((tm, D), lastrow),
                  pl.BlockSpec((DEPTH, D), lambda j, i: (0, 0)),
                  pl.BlockSpec((ck, D), lambda j, i: (j, 0))],
        out_specs=[pl.BlockSpec((tm, D), lastrow), pl.BlockSpec((ck, D), lambda j, i: (j, 0)),
                   pl.BlockSpec((8, D), lambda j, i: (0, 0))],
        out_shape=[_sds((s, D), F32), _sds((IN_W, D), BF16), _sds((8, D), F32)],
        scratch_shapes=[pltpu.VMEM((s, D), F32), pltpu.VMEM((s, D), BF16), pltpu.VMEM((ck, D), F32)],
        name=f"bwd_proj{l}")


def bwd_proj_w(dproj, x, g1, l, half, comm=None):
    s = x.shape[0]
    tm = min(1024, s)
    ck = 1408
    c0, hw = W_IN_PARTS[half]
    n_j, n_i = IN_W // ck, s // tm

    def body(dp_ref, x_ref, g_ref, dw_ref, xn_b, acc):
        j, i = pl.program_id(0), pl.program_id(1)
        rows = pl.ds(pl.multiple_of(i * tm, tm), tm)

        @pl.when(j == 0)
        def _():
            xv = x_ref[...]
            r = lax.rsqrt(jnp.mean(xv * xv, axis=-1, keepdims=True) + EPS)
            xn_b[rows, :] = (xv * r * g_ref[l:l + 1, :])[:, c0:c0 + hw].astype(BF16)

        @pl.when(i == 0)
        def _():
            acc[...] = jnp.zeros((ck, hw), F32)

        acc[...] += _dot(dp_ref[...], xn_b[rows, :], _TN)

        @pl.when(i == n_i - 1)
        def _():
            dw_ref[...] = acc[...].astype(BF16)

    return _call(
        body, comm, (dproj, x, g1), grid=(n_j, n_i),
        in_specs=[pl.BlockSpec((tm, ck), lambda j, i: (i, j)),
                  pl.BlockSpec((tm, D), lambda j, i: (jnp.where(j == 0, i, n_i - 1), 0)),
                  pl.BlockSpec((DEPTH, D), lambda j, i: (0, 0))],
        out_specs=pl.BlockSpec((ck, hw), lambda j, i: (j, 0)),
        out_shape=_sds((IN_W, hw), BF16),
        scratch_shapes=[pltpu.VMEM((s, hw), BF16), pltpu.VMEM((ck, hw), F32)],
        name=f"bwd_proj_w{half}_{l}")


def bwd_proj_x(dproj, x, dh1, g1, wt_in, l, comm=None):
    s = x.shape[0]
    tm = min(512, s)
    ck = 1408
    n_j, n_i = IN_W // ck, s // tm

    def body(dp_ref, x_ref, dh_ref, g_ref, w_ref, dx_ref, st_ref, dxn):
        j, i = pl.program_id(0), pl.program_id(1)
        rows = pl.ds(pl.multiple_of(i * tm, tm), tm)
        g = g_ref[l:l + 1, :]

        @pl.when((j == 0) & (i == 0))
        def _():
            st_ref[...] = jnp.zeros((8, D), F32)

        part = _dot(dp_ref[...], w_ref[...], _NN)

        @pl.when(j == 0)
        def _():
            dxn[rows, :] = part

        @pl.when(j > 0)
        def _():
            dxn[rows, :] += part

        @pl.when(j == n_j - 1)
        def _():
            xv = x_ref[...]
            r = lax.rsqrt(jnp.mean(xv * xv, axis=-1, keepdims=True) + EPS)
            n = xv * r
            dv = dxn[rows, :]
            dn = dv * g
            dx_ref[...] = dh_ref[...] + r * (dn - n * jnp.mean(dn * n, axis=-1, keepdims=True))
            st_ref[0:1, :] += jnp.sum(dv * n, axis=0, keepdims=True)

    lastrow = lambda j, i: (jnp.where(j == n_j - 1, i, 0), 0)
    return _call(
        body, comm, (dproj, x, dh1, g1, wt_in), grid=(n_j, n_i),
        in_specs=[pl.BlockSpec((tm, ck), lambda j, i: (i, j)), pl.BlockSpec((tm, D), lastrow),
                  pl.BlockSpec((tm, D), lastrow),
                  pl.BlockSpec((DEPTH, D), lambda j, i: (0, 0)), pl.BlockSpec((ck, D), lambda j, i: (j, 0))],
        out_specs=[pl.BlockSpec((tm, D), lastrow), pl.BlockSpec((8, D), lambda j, i: (0, 0))],
        out_shape=[_sds((s, D), F32), _sds((8, D), F32)],
        scratch_shapes=[pltpu.VMEM((s, D), F32)], name=f"bwd_proj_x{l}")


def _block_diag(w):
    nl, nb, bw, _ = w.shape
    eye = jnp.eye(nb, dtype=w.dtype)
    return jnp.einsum("lhij,hk->lhikj", w, eye).reshape(nl, nb * bw, nb * bw).astype(BF16)


class NoOverlap:
    def __init__(self, big):
        self.big = big

    def weights(self, l):
        return self.big[l]

    def job(self, slot, l):
        return None

    def done(self, slot, l, results):
        pass

    def new_grads(self, group, l, grads):
        pass

    def new_small(self, l, arrays, head_stats):
        pass


def local_step(x, target, norm1_g, norm2_g, final_g, convw, vecs, lru_wx, lru_wa, plan):
    wx_bd, wa_bd = _block_diag(lru_wx), _block_diag(lru_wa)

    def run(fn, slot, l, *args):
        res, cres = fn(*args, l, comm=plan.job(slot, l))
        plan.done(slot, l, cres)
        return res

    saved = []
    for l in range(DEPTH):
        proj = run(fwd_proj, "fwd_proj", l, x, norm1_g, plan.weights(l)["in_t"])
        pre_abd, h, kept = run(fwd_branch, "fwd_branch", l, proj, convw, vecs, wx_bd, wa_bd)
        pre_c = run(fwd_attn, "fwd_attn", l, proj, vecs)
        w = plan.weights(l)
        y4, merged, h1 = run(fwd_merge, "fwd_merge", l, x, proj, pre_abd, pre_c, w["a_t"], w["b_t"], w["c_t"], w["d_t"], w["o"])
        w = plan.weights(l)
        x_out, fg, fu = run(fwd_ffn, "fwd_ffn", l, h1, norm2_g, w["gate_t"], w["up_t"], w["down"])
        saved.append((x, proj, pre_abd, h, kept, pre_c, y4, merged, h1, fg, fu))
        x = x_out
    dx, head_stats = loss_head(x, final_g.reshape(1, D), target)
    small = [None] * DEPTH
    for l in reversed(range(DEPTH)):
        x_in, proj, pre_abd, h, kept, pre_c, y4, merged, h1, fg, fu = saved[l]
        w = plan.weights(l)
        dh1, d_gate, d_up, d_down, st_ffn = run(bwd_ffn, "bwd_ffn", l, dx, h1, fg, fu, norm2_g, w["gate_t"], w["up_t"], w["down"])
        plan.new_grads("ffn", l, dict(gate_t=d_gate, up_t=d_up, down=d_down))
        dproj, dpre, d_o, d_a, d_b, d_c, d_d = run(
            bwd_merge, "bwd_merge", l, dh1, y4, proj, merged, pre_abd, pre_c, w["a_t"], w["b_t"], w["c_t"], w["d_t"], w["o"])
        plan.new_grads("out", l, dict(a_t=d_a, b_t=d_b, c_t=d_c, d_t=d_d, o=d_o))
        dq, dkc, dkp, st_attn = run(bwd_attn, "bwd_attn", l, proj, dpre, vecs)
        dproj, dcw, dvec, dwx, dwa = run(bwd_branch, "bwd_branch", l, proj, dproj, dpre, h, kept, dq, dkc, dkp, convw, vecs, wx_bd, wa_bd)
        if l > 0:
            dx, d_in, st_proj = run(bwd_proj, "bwd_proj", l, dproj, x_in, dh1, norm1_g, w["in_t"])
            plan.new_grads("in", l, dict(in_t=d_in))
        else:
            for half, name in enumerate(("in_a", "in_b")):
                d_half = run(functools.partial(bwd_proj_w, half=half), f"bwd_proj_w{half}", l, dproj, x_in, norm1_g)
                plan.new_grads(name, l, {name: d_half})
            dx, st_proj = run(bwd_proj_x, "bwd_proj_x", l, dproj, x_in, dh1, norm1_g, w["in_t"])
        small[l] = (st_proj, st_ffn, dvec, st_attn, dcw, dwx, dwa)
        plan.new_small(l, small[l], head_stats)
    return head_stats, dx, small


BIG = dict(in_t=("w_in", "view"), a_t=("w_a_out", "transpose"), b_t=("w_b_out", "transpose"), c_t=("w_c_out", "transpose"),
           d_t=("w_d_out", "transpose"), o=("w_o", "plain"), gate_t=("w_ffn_gate", "view"), up_t=("w_ffn_up", "view"),
           down=("w_ffn_down", "plain"))


def cast_transpose(ws, name):
    n = len(ws)
    nl, a, b = ws[0].shape
    ta = min(256, a)

    def body(*refs):
        for w_ref, o_ref in zip(refs[:n], refs[n:]):
            o_ref[...] = w_ref[...].T.astype(BF16)

    return pl.pallas_call(
        body, grid=(nl, a // ta),
        in_specs=[pl.BlockSpec((None, ta, b), lambda l, i: (l, i, 0))] * n,
        out_specs=[pl.BlockSpec((None, b, ta), lambda l, i: (l, 0, i))] * n,
        out_shape=[_sds((nl, b, a), BF16)] * n, compiler_params=_cparams(2), name=name)(*ws)


def add_partials(mine, recv, core, name):
    n = len(mine)

    def body(core_ref, *refs):
        del core_ref
        for a_ref, b_ref, o_ref in zip(refs[:n], refs[n:2 * n], refs[2 * n:]):
            o_ref[...] = (a_ref[...].astype(F32) + b_ref[...].astype(F32)).astype(BF16)

    return pl.pallas_call(
        body,
        grid_spec=pltpu.PrefetchScalarGridSpec(
            num_scalar_prefetch=1, grid=(4,),
            in_specs=[pl.BlockSpec((None, None) + a.shape[2:], lambda i, cr: (i, cr[0], 0, 0)) for a in mine]
            + [pl.BlockSpec((None,) + b.shape[1:], lambda i, cr: (i, 0, 0)) for b in recv],
            out_specs=[pl.BlockSpec((None,) + b.shape[1:], lambda i, cr: (i, 0, 0)) for b in recv]),
        out_shape=[_sds(b.shape, BF16) for b in recv], compiler_params=_cparams(1), name=name)(core, *mine, *recv)


def _adamw(w, g, m, v):
    m = ADAM_B1 * m + (1.0 - ADAM_B1) * g
    v = ADAM_B2 * v + (1.0 - ADAM_B2) * (g * g)
    m_hat = m / (1.0 - ADAM_B1 ** ADAM_STEP)
    v_hat = v / (1.0 - ADAM_B2 ** ADAM_STEP)
    delta = -ADAM_LR * (m_hat / (jnp.sqrt(v_hat) + ADAM_EPS) + ADAM_WD * w)
    return delta, m, v


def adamw_big(items, name, comm=None):
    n_tiles = 4
    n = len(items)
    nl = items[0][0].shape[1]

    def body(*refs):
        ins, outs = refs[:4 * n], refs[4 * n:]
        for k, (contrib, _, _, _, transposed) in enumerate(items):
            c_ref, w_ref, m_ref, v_ref = ins[4 * k:4 * k + 4]
            g = c_ref[0].astype(F32)
            for src in range(1, contrib.shape[0]):
                g = g + c_ref[src].astype(F32)
            if transposed:
                g = g.T
            delta, mn, vn = _adamw(w_ref[...], g, m_ref[...], v_ref[...])
            for o_ref, val in zip(outs[4 * k:4 * k + 4], (g, delta, mn, vn)):
                o_ref[...] = val

    in_specs, out_specs, out_shape, args = [], [], [], []
    for contrib, w, m, v, transposed in items:
        nsrc, _, rows, cols = contrib.shape
        ct = cols // n_tiles
        if transposed:
            wspec = pl.BlockSpec((None, ct, rows), lambda l, j: (l, j, 0))
        else:
            wspec = pl.BlockSpec((None, rows, ct), lambda l, j: (l, 0, j))
        in_specs += [pl.BlockSpec((nsrc, None, rows, ct), lambda l, j: (0, l, 0, j)), wspec, wspec, wspec]
        out_specs += [wspec] * 4
        out_shape += [_sds(w.shape, F32)] * 4
        args += [contrib, w, m, v]
    res, cres = _call(body, comm, tuple(args), grid=(nl, n_tiles), in_specs=in_specs, out_specs=out_specs,
                      out_shape=out_shape, name=name)
    return [res[4 * k:4 * k + 4] for k in range(n)], cres


VEC_NAMES = ("conv_a_b", "lru_bx", "lru_ba", "lru_lambda", "conv_d_b", "ln_d_g", "ln_d_b")
P_N1, P_N2, P_VEC, P_CONV, P_LRU = 0, 1, 2, 6, 6 + CW_ROWS
P_FINAL, P_LOSS, P_ROWS = P_LRU + HD, P_LRU + HD + 1, P_LRU + HD + 2
SMALL = ("norm1_g", "conv_a_w", "conv_a_b", "lru_wx", "lru_bx", "lru_wa", "lru_ba", "lru_lambda", "conv_b_w", "sinks",
         "conv_d_w", "conv_d_b", "ln_d_g", "ln_d_b", "norm2_g", "final_g")
VMEM_FULL = pl.BlockSpec(memory_space=pltpu.VMEM)


def _stack_vecs(p):
    rows = [p[n] for n in VEC_NAMES] + [jnp.pad(p["sinks"], ((0, 0), (0, BW - N_HEADS)))]
    return jnp.stack(rows, axis=1)


def _stack_convs(p):
    nl, _, ch = p["conv_a_w"].shape
    z = jnp.zeros((nl, 1, ch), F32)
    return jnp.concatenate([p["conv_a_w"], p["conv_b_w"], z, p["conv_d_w"], z], axis=1)


def _vec_place(r):
    return P_VEC + r // 2, (r % 2) * BW


def pack_small(arrays, head_stats, l):
    n = len(arrays)

    def body(*refs):
        st_proj, st_ffn, dvec, st_attn, dcw, dwx, dwa = refs[:n]
        pack = refs[-1]
        pack[...] = jnp.zeros((P_ROWS, D), F32)
        lane = lax.broadcasted_iota(jnp.int32, (HD, BW), 1)
        pack[P_N1:P_N1 + 1, :] = st_proj[0:1, :]
        pack[P_N2:P_N2 + 1, :] = st_ffn[0:1, :]
        for r in range(len(VEC_NAMES)):
            row, c0 = _vec_place(r)
            pack[row:row + 1, c0:c0 + BW] = dvec[r:r + 1, :]
        row, c0 = _vec_place(V_SINK)
        pack[row:row + 1, c0:c0 + 128] = st_attn[0:1, :]
        pack[P_CONV:P_CONV + CW_ROWS, 0:BW] = dcw[...]
        for mat, c0 in ((dwx, 0), (dwa, BW)):
            blocks = jnp.zeros((HD, BW), F32)
            for h in range(BW // HD):
                blocks = jnp.where((lane >= HD * h) & (lane < HD * (h + 1)), mat[HD * h:HD * (h + 1), :], blocks)
            pack[P_LRU:P_LRU + HD, c0:c0 + BW] = blocks
        if head_stats is not None:
            pack[P_FINAL:P_LOSS + 1, :] = refs[n][0:2, :]

    flat = list(arrays) + ([] if head_stats is None else [head_stats])
    return pl.pallas_call(body, out_shape=_sds((P_ROWS, D), F32), in_specs=[VMEM_FULL] * len(flat), out_specs=VMEM_FULL,
                          name=f"pack_small{l}", compiler_params=pltpu.CompilerParams(vmem_limit_bytes=VMEM_LIMIT))(*flat)


def adamw_small(gathered, me, w, m, v):
    ns = len(SMALL)

    def body(me_ref, *refs):
        c_refs, refs = refs[:DEPTH], refs[DEPTH:]
        w_refs, m_refs, v_refs = refs[:ns], refs[ns:2 * ns], refs[2 * ns:3 * ns]
        loss_ref, outs, gs = refs[3 * ns], refs[3 * ns + 1:3 * ns + 1 + 4 * ns], refs[-1]
        for l in range(DEPTH):
            gs[l] = c_refs[l][0]
            for dev in range(1, NDEV):
                gs[l] += c_refs[l][dev]
        loss_ref[...] = gs[DEPTH - 1, P_LOSS:P_LOSS + 1, 0:128]

        def update(name, sel, g):
            i = SMALL.index(name)
            delta, mn, vn = _adamw(w_refs[i][sel], g, m_refs[i][sel], v_refs[i][sel])
            for o_ref, val in zip(outs[4 * i:4 * i + 4], (g, delta, mn, vn)):
                o_ref[sel] = val

        update("final_g", (slice(0, 1), slice(None)), gs[DEPTH - 1, P_FINAL:P_FINAL + 1, :])
        shift = (BW - me_ref[0] * (BW // NDEV)) & (BW - 1)
        for l in range(DEPTH):
            row = (slice(l, l + 1), slice(None))
            update("norm1_g", row, gs[l, P_N1:P_N1 + 1, :])
            update("norm2_g", row, gs[l, P_N2:P_N2 + 1, :])
            for r, name in enumerate(VEC_NAMES):
                prow, c0 = _vec_place(r)
                update(name, row, gs[l, prow:prow + 1, c0:c0 + BW])
            prow, c0 = _vec_place(V_SINK)
            update("sinks", row, gs[l, prow:prow + 1, c0:c0 + N_HEADS])
            mine = pltpu.roll(gs[l, P_CONV:P_CONV + CW_ROWS, 0:BW], shift, 1)[:, 0:BW // NDEV]
            update("conv_a_w", (l,), mine[CW_A:CW_A + CONV_A])
            update("conv_b_w", (l,), mine[CW_B:CW_B + CONV_B])
            update("conv_d_w", (l,), mine[CW_D:CW_D + CONV_D])
            for h in range(BW // HD):
                update("lru_wx", (l, h), gs[l, P_LRU:P_LRU + HD, HD * h:HD * (h + 1)])
                update("lru_wa", (l, h), gs[l, P_LRU:P_LRU + HD, BW + HD * h:BW + HD * (h + 1)])

    args = [p[n] for p in (w, m, v) for n in SMALL]
    full = lambda a: pl.BlockSpec(a.shape, lambda i, me_ref: (0,) * a.ndim)
    out_shape = [_sds((1, 128), F32)] + [_sds(w[n].shape, F32) for n in SMALL for _ in range(4)]
    outs = pl.pallas_call(
        body,
        grid_spec=pltpu.PrefetchScalarGridSpec(
            num_scalar_prefetch=1, grid=(1,),
            in_specs=[full(a) for a in list(gathered) + args], out_specs=[full(o) for o in out_shape],
            scratch_shapes=[pltpu.VMEM((DEPTH, P_ROWS, D), F32)]),
        out_shape=out_shape, name="adamw_small", compiler_params=_cparams(1))(me, *gathered, *args)
    return outs[0], {n: outs[1 + 4 * i:5 + 4 * i] for i, n in enumerate(SMALL)}


def merge_jobs(jobs):
    jobs = [j for j in jobs if j is not None]
    if not jobs:
        return None, []
    inputs, aliases, outs, sems, cuts = [], {}, [], [], []
    for j in jobs:
        i0, o0, s0 = len(inputs), len(outs), len(sems)
        aliases.update({i0 + i: o0 + o for i, o in j.aliases.items()})
        inputs += j.inputs
        outs += j.out_shapes
        sems += j.sem_shapes
        cuts.append((i0, len(inputs), o0, len(outs), s0, len(sems)))

    def each(which):
        def go(cins, couts, s):
            for j, (i0, i1, o0, o1, s0, s1) in zip(jobs, cuts):
                if getattr(j, which) is not None:
                    getattr(j, which)(cins[i0:i1], couts[o0:o1], s[s0:s1])
        return go

    relay = each("relay") if any(j.relay is not None for j in jobs) else None
    return CommJob(inputs, aliases, outs, sems, each("start"), each("finish"), relay), [(c[2], c[3]) for c in cuts]


SIXTHS = 6
OUT_KINDS = ("a_t", "b_t", "c_t", "d_t", "o")
GATHER_PLAN = {
    "fwd_proj": [(k, 0, 0, 6) for k in OUT_KINDS] + [("gate_t", 0, 0, 6)],
    "fwd_branch": [("up_t", 0, 0, 6)],
    "fwd_attn": [("down", 0, 0, 6)],
    "fwd_merge": [("in_t", 1, 0, 2)],
    "fwd_ffn": [("in_t", 1, 2, 6)],
}
SIBLING_PLAN = {"bwd_merge": ("ffn", 0), "bwd_branch": ("out", 0), "bwd_ffn": ("in", 1),
                "bwd_proj_w1": ("in_a", 0), "bwd_proj_x": ("in_b", 0)}
GROUPS = dict(ffn=("gate_t", "up_t", "down"), out=OUT_KINDS, in_a=("in_a",), in_b=("in_b",))
GROUPS["in"] = ("in_t",)
COLUMN_HALF = dict(in_a=("in_t", W_IN_PARTS[0][0]), in_b=("in_t", W_IN_PARTS[1][0]))
CHIP_PLAN = {
    "bwd_attn": [("in_t", 1, 3, 5)],
    "bwd_branch": [("in_t", 1, 5, 6), ("gate_t", 0, 0, 6), ("up_t", 0, 0, 6), ("down", 0, 0, 3)],
    "bwd_proj": [(k, 0, 0, 6) for k in OUT_KINDS] + [("down", 0, 3, 6)],
    "bwd_proj_w0": [(k, 0, 0, 6) for k in OUT_KINDS[:3]] + [("down", 0, 3, 6)],
    "bwd_proj_w1": [(k, 0, 0, 6) for k in OUT_KINDS[3:]],
    "bwd_merge": [("in_t", 1, 0, 3)],
    "bwd_proj_x": [("in_a", 0, 0, 6)],
    "adamw_rest": [("in_b", 0, 0, 6)],
}
SMALL_GATHER_PLAN = {"bwd_ffn": 1, "adamw_rest": 0}


class Overlap:
    def __init__(self, shards, core):
        self.shards = shards
        self.core = core
        self.gathered = [dict.fromkeys(BIG) for _ in range(DEPTH)]
        self.views = {}
        self.partial = {}
        self.contrib = dict.fromkeys(BIG)
        self.small_packs = [None] * DEPTH
        self.small_gathered = [None] * DEPTH
        self._open = None

    def weights(self, l):
        return self.gathered[l]

    def new_grads(self, group, l, grads):
        for k, g in grads.items():
            self.views[k, l] = g.reshape(4, 2, g.shape[0] // NDEV, g.shape[1])

    def new_small(self, l, arrays, head_stats):
        self.small_packs[l] = pack_small(arrays, head_stats if l == DEPTH - 1 else None, l)

    @staticmethod
    def _rows(shard_rows, f0, f1):
        return shard_rows * f0 // SIXTHS, shard_rows * (f1 - f0) // SIXTHS

    def job(self, slot, l):
        jobs, notes = [], []
        pieces = [(k, l + dl, f0, f1) for k, dl, f0, f1 in GATHER_PLAN.get(slot, []) if l + dl < DEPTH]
        if pieces:
            jobs.append(gather_job([((k, ll), self.shards[ll][k], self.gathered[ll][k],
                                     *self._rows(self.shards[ll][k].shape[0], f0, f1)) for k, ll, f0, f1 in pieces]))
            notes.append(("gather", list(dict.fromkeys((k, ll) for k, ll, _, _ in pieces))))
        if slot in SIBLING_PLAN and l + SIBLING_PLAN[slot][1] < DEPTH:
            group, dl = SIBLING_PLAN[slot]
            keys = [(k, l + dl) for k in GROUPS[group]]
            jobs.append(sibling_exchange_job([self.views[key] for key in keys]))
            notes.append(("sibling", keys))
        pieces = [(k, l + dl, f0, f1) for k, dl, f0, f1 in CHIP_PLAN.get(slot, []) if l + dl < DEPTH]
        if pieces:
            whole = [(*COLUMN_HALF.get(k, (k, 0)), k, ll, f0, f1) for k, ll, f0, f1 in pieces]
            jobs.append(chip_exchange_job([(self.partial[k, ll], self.contrib[kind], kind, ll,
                                            *self._rows(self.partial[k, ll].shape[1], f0, f1), col0, self.shards[ll][kind].shape[1])
                                           for kind, col0, k, ll, f0, f1 in whole]))
            notes.append(("chips", list(dict.fromkeys(kind for kind, *_ in whole))))
        if slot in SMALL_GATHER_PLAN and l + SMALL_GATHER_PLAN[slot] < DEPTH:
            ll = l + SMALL_GATHER_PLAN[slot]
            jobs.append(gather_job([("small", self.small_packs[ll], None, 0, P_ROWS)]))
            notes.append(("small", ll))
        job, spans = merge_jobs(jobs)
        self._open = (slot, l, notes, spans)
        return job

    def done(self, slot, l, results):
        open_slot, open_l, notes, spans = self._open
        assert (open_slot, open_l) == (slot, l)
        for (what, keys), (r0, r1) in zip(notes, spans):
            res = results[r0:r1]
            if what == "gather":
                for (k, ll), g in zip(keys, res):
                    self.gathered[ll][k] = g
            elif what == "sibling":
                sums = add_partials([self.views[key] for key in keys], list(res), self.core, f"chip_sum_{keys[0][0]}{keys[0][1]}")
                self.partial.update(zip(keys, sums))
            elif what == "chips":
                for k, c in zip(keys, res):
                    self.contrib[k] = c
            else:
                self.small_gathered[keys], = res


SMALL = ("norm1_g", "conv_a_w", "conv_a_b", "lru_wx", "lru_bx", "lru_wa", "lru_ba", "lru_lambda", "conv_b_w", "sinks",
         "conv_d_w", "conv_d_b", "ln_d_g", "ln_d_b", "norm2_g", "final_g")
WEIGHTS = ("norm1_g", "w_in", "conv_a_w", "conv_a_b", "lru_wx", "lru_bx", "lru_wa", "lru_ba", "lru_lambda", "w_a_out",
           "conv_b_w", "w_b_out", "sinks", "w_c_out", "conv_d_w", "conv_d_b", "ln_d_g", "ln_d_b", "w_d_out", "w_o",
           "norm2_g", "w_ffn_gate", "w_ffn_up", "w_ffn_down", "final_g")


def kernel(x, norm1_g, w_in, conv_a_w, conv_a_b, lru_wx, lru_bx, lru_wa, lru_ba, lru_lambda, w_a_out, conv_b_w, w_b_out, sinks, w_c_out, conv_d_w, conv_d_b, ln_d_g, ln_d_b, w_d_out, w_o, norm2_g, w_ffn_gate, w_ffn_up, w_ffn_down, final_g, loss_target, m_norm1_g, m_w_in, m_conv_a_w, m_conv_a_b, m_lru_wx, m_lru_bx, m_lru_wa, m_lru_ba, m_lru_lambda, m_w_a_out, m_conv_b_w, m_w_b_out, m_sinks, m_w_c_out, m_conv_d_w, m_conv_d_b, m_ln_d_g, m_ln_d_b, m_w_d_out, m_w_o, m_norm2_g, m_w_ffn_gate, m_w_ffn_up, m_w_ffn_down, m_final_g, v_norm1_g, v_w_in, v_conv_a_w, v_conv_a_b, v_lru_wx, v_lru_bx, v_lru_wa, v_lru_ba, v_lru_lambda, v_w_a_out, v_conv_b_w, v_w_b_out, v_sinks, v_w_c_out, v_conv_d_w, v_conv_d_b, v_ln_d_g, v_ln_d_b, v_w_d_out, v_w_o, v_norm2_g, v_w_ffn_gate, v_w_ffn_up, v_w_ffn_down, v_final_g):
    args = dict(locals())
    w = {n: args[n] for n in WEIGHTS}
    m = {n: args["m_" + n] for n in WEIGHTS}
    v = {n: args["v_" + n] for n in WEIGHTS}
    me = _dev_index(*_mesh_pos())

    def rows_major(a, how):
        return jnp.swapaxes(a, 1, 2) if how == "view" else a

    stacked = {k: rows_major(w[n], how).astype(BF16) for k, (n, how) in BIG.items() if how != "transpose"}
    turned = [k for k, (n, how) in BIG.items() if how == "transpose"]
    stacked.update(zip(turned, cast_transpose([w[BIG[k][0]] for k in turned], "prep_transposed")))
    plan = Overlap([{k: stacked[k][l] for k in BIG} for l in range(DEPTH)], lax.axis_index("c").astype(jnp.int32).reshape(1))
    convs = jnp.pad(_stack_convs(w).reshape(DEPTH * CW_ROWS, BW // NDEV), ((0, 0), (0, 256 - BW // NDEV)))
    g_in0, g_conv = _comm_only(gather_job([(("in_t", 0), plan.shards[0]["in_t"], None, 0, plan.shards[0]["in_t"].shape[0]),
                                           ("convs", convs, None, 0, convs.shape[0])]), "gather_first")
    plan.gathered[0]["in_t"] = g_in0
    convw = g_conv[:, :BW // NDEV].reshape(NDEV, DEPTH, CW_ROWS, BW // NDEV).transpose(1, 2, 0, 3).reshape(DEPTH, CW_ROWS, BW)

    vecs = _stack_vecs(w)
    head_stats, grad_x, grads = local_step(x[0], loss_target[0], norm1_g, norm2_g, final_g, convw, vecs, lru_wx, lru_wa, plan)


    out = {}
    for slot, kinds in (("adamw_rest", [k for k in BIG if k != "in_t"]), ("adamw_in_t", ["in_t"])):
        job = plan.job(slot, 0)
        items = [(plan.contrib[k], *[rows_major(p[BIG[k][0]], BIG[k][1]) for p in (w, m, v)], BIG[k][1] == "transpose")
                 for k in kinds]
        results, cres = adamw_big(items, slot, comm=job)
        plan.done(slot, 0, cres)
        for k, res in zip(kinds, results):
            out[BIG[k][0]] = [rows_major(r, BIG[k][1]) for r in res]

    def own_shapes(p):
        return {n: p[n].reshape(1, D) if n == "final_g" else p[n] for n in SMALL}

    loss, small = adamw_small([g.reshape(NDEV, P_ROWS, D) for g in plan.small_gathered], me.astype(jnp.int32).reshape(1),
                              own_shapes(w), own_shapes(m), own_shapes(v))
    for n in SMALL:
        out[n] = [r.reshape(w[n].shape) for r in small[n]]
    loss = loss[0, 0]
    return (loss, grad_x[None], *[out[n][0] for n in WEIGHTS], *[out[n][1] for n in WEIGHTS],
            *[out[n][2] for n in WEIGHTS], *[out[n][3] for n in WEIGHTS])
```

```python
import functools

import jax
import jax.numpy as jnp
from jax import lax
from jax.experimental import pallas as pl
from jax.experimental.pallas import tpu as pltpu

F32 = jnp.float32
BF16 = jnp.bfloat16
E = pl.Element

D = 1024
BW = 512
IN_W = 8448
GL0 = 4352
FF = 2816
N_HEADS = 8
N_KV = 2
HD = 64
ATT_BLK = 128
EPS = 1e-6
LRU_C = 8.0
NEG_INF = -1e30
DEPTH = 2
NDEV = 8
CONV_A, CONV_B, CONV_D = 4, 3, 31
C_AX, C_AG, C_BV, C_BC, C_BB, C_Q, C_K, C_V, C_D1, C_D2 = 0, 512, 1024, 1536, 2048, 2560, 3072, 3200, 3328, 3840
CW_A, CW_B, CW_D, CW_ROWS = 0, 4, 8, 40
V_CAB, V_BX, V_BA, V_LAM, V_CDB, V_LNG, V_LNB, V_SINK, V_ROWS = 0, 1, 2, 3, 4, 5, 6, 7, 8
HALO = 32
W_IN_PARTS = ((0, 768), (768, 256))

ADAM_LR, ADAM_B1, ADAM_B2, ADAM_EPS, ADAM_WD, ADAM_STEP = 0.001, 0.9, 0.999, 1e-08, 0.01, 10

VMEM_LIMIT = 56 * 1024 * 1024

_NN = (((1,), (0,)), ((), ()))
_NT = (((1,), (1,)), ((), ()))
_TN = (((0,), (0,)), ((), ()))


def _dot(a, b, dims):
    return lax.dot_general(a.astype(BF16), b.astype(BF16), dims, preferred_element_type=F32)


def _cparams(n_axes):
    return pltpu.CompilerParams(dimension_semantics=("arbitrary",) * n_axes, vmem_limit_bytes=VMEM_LIMIT)


def _sds(shape, dtype):
    return jax.ShapeDtypeStruct(tuple(shape), dtype)


def _sigmoid(x):
    return jax.nn.sigmoid(x)


def _neg_expm1(x):
    p = x * (1.0 + x * (0.5 + x * (1.0 / 6.0 + x * (1.0 / 24.0 + x * (1.0 / 120.0)))))
    return jnp.where(x > -0.1, -p, 1.0 - jnp.exp(x))


def _softplus(z):
    return jnp.maximum(z, 0.0) + jnp.log1p(jnp.exp(-jnp.abs(z)))


def _gelu_and_grad(x):
    c = 0.7978845608028654
    inner = c * (x + 0.044715 * x * x * x)
    t = jnp.tanh(inner)
    g = 0.5 * x * (1.0 + t)
    dg = 0.5 * (1.0 + t) + 0.5 * x * (1.0 - t * t) * c * (1.0 + 3.0 * 0.044715 * x * x)
    return g, dg


ANY = pl.BlockSpec(memory_space=pl.ANY)
MESH = pl.DeviceIdType.MESH


def _mesh_pos():
    return lax.axis_index("x"), lax.axis_index("y"), lax.axis_index("c")


def _dev_index(px, py, pc):
    return 4 * px + 2 * py + pc


class CommJob:
    def __init__(self, inputs, aliases, out_shapes, sem_shapes, start, finish, relay=None):
        self.inputs, self.aliases, self.out_shapes, self.sem_shapes = list(inputs), dict(aliases), list(out_shapes), list(sem_shapes)
        self.start, self.finish, self.relay = start, finish, relay


def _call(body, comm, args, *, grid, in_specs, out_specs, out_shape, scratch_shapes=(), name, aliases=None):
    single = not isinstance(out_shape, (list, tuple))
    out_specs = [out_specs] if single else list(out_specs)
    out_shape = [out_shape] if single else list(out_shape)
    scratch_shapes = list(scratch_shapes)
    n_in, n_out, n_scr, n_axes = len(in_specs), len(out_shape), len(scratch_shapes), len(grid)
    params = pltpu.CompilerParams(dimension_semantics=("arbitrary",) * n_axes, vmem_limit_bytes=VMEM_LIMIT)
    io_aliases = dict(aliases or {})
    if comm is None:
        outs = pl.pallas_call(body, grid=grid, in_specs=in_specs, out_specs=out_specs, out_shape=out_shape,
                              scratch_shapes=scratch_shapes, input_output_aliases=io_aliases, compiler_params=params,
                              name=name)(*args)
        return (outs[0] if single else outs), []
    c_in, c_out = len(comm.inputs), len(comm.out_shapes)
    io_aliases.update({n_in + i: n_out + o for i, o in comm.aliases.items()})

    def wrapped(*refs):
        ins, cins = refs[:n_in], refs[n_in:n_in + c_in]
        outs = refs[n_in + c_in:n_in + c_in + n_out]
        couts = refs[n_in + c_in + n_out:n_in + c_in + n_out + c_out]
        rest = refs[n_in + c_in + n_out + c_out:]
        scr, sems = rest[:n_scr], rest[n_scr:]
        first = functools.reduce(lambda a, b: a & b, [pl.program_id(a) == 0 for a in range(n_axes)])
        last = functools.reduce(lambda a, b: a & b, [pl.program_id(a) == pl.num_programs(a) - 1 for a in range(n_axes)])

        @pl.when(first)
        def _():
            comm.start(cins, couts, sems)

        if comm.relay is not None:
            step = functools.reduce(lambda a, b: a * grid[b] + pl.program_id(b), range(1, n_axes), pl.program_id(0))
            n_steps = functools.reduce(lambda a, b: a * b, grid)

            @pl.when(step == 2 * n_steps // 3)
            def _():
                comm.relay(cins, couts, sems)

        body(*ins, *outs, *scr)

        @pl.when(last)
        def _():
            comm.finish(cins, couts, sems)

    outs = pl.pallas_call(
        wrapped, grid=grid, in_specs=list(in_specs) + [ANY] * c_in, out_specs=out_specs + [ANY] * c_out,
        out_shape=out_shape + comm.out_shapes, scratch_shapes=scratch_shapes + comm.sem_shapes,
        input_output_aliases=io_aliases, compiler_params=params, name=name)(*args, *comm.inputs)
    res, cres = outs[:n_out], outs[n_out:]
    return (res[0] if single else res), cres


def _comm_only(comm, name):
    c_in, c_out = len(comm.inputs), len(comm.out_shapes)

    def body(*refs):
        cins, couts, sems = refs[:c_in], refs[c_in:c_in + c_out], refs[c_in + c_out:]
        comm.start(cins, couts, sems)
        if comm.relay is not None:
            comm.relay(cins, couts, sems)
        comm.finish(cins, couts, sems)

    return pl.pallas_call(body, in_specs=[ANY] * c_in, out_specs=[ANY] * c_out, out_shape=comm.out_shapes,
                          scratch_shapes=comm.sem_shapes, input_output_aliases=comm.aliases, name=name)(*comm.inputs)


def gather_job(pieces):
    inputs, aliases, out_shapes, plan, where = [], {}, [], [], {}
    for key, shard, gathered, row0, nrows in pieces:
        if key not in where:
            where[key] = (len(inputs), len(out_shapes))
            inputs.append(shard)
            if gathered is not None:
                aliases[len(inputs)] = len(out_shapes)
                inputs.append(gathered)
            out_shapes.append(_sds((NDEV * shard.shape[0], shard.shape[1]), shard.dtype))
        plan.append((*where[key], shard.shape[0], row0, nrows))
    n = len(plan)

    def copies(cins, couts, sems):
        send_sems, recv_sems, local_sems = sems
        x, y, c = _mesh_pos()
        me, sibling = (x, y, c), (x, y, 1 - c)
        xn, yn, dg = (1 - x, y), (x, 1 - y), (1 - x, 1 - y)
        local, first, pass1, pass2, got_ici, got_fwd, got_d2d = [], [], [], [], [], [], []
        for p, (i_shard, i_out, rows, row0, nrows) in enumerate(plan):
            src = cins[i_shard].at[pl.ds(row0, nrows), :]
            half = cins[i_shard].shape[1] // 2
            left, right, whole = pl.ds(0, half), pl.ds(half, half), slice(None)

            def slot(dev, lanes, i_out=i_out, rows=rows, row0=row0, nrows=nrows):
                return couts[i_out].at[pl.ds(_dev_index(*dev) * rows + row0, nrows), lanes]

            def copy(g, dev, to, lanes=whole, src=None, p=p, slot=slot):
                return pltpu.make_async_remote_copy(
                    src_ref=slot(dev, lanes) if src is None else src, dst_ref=slot(dev, lanes),
                    send_sem=send_sems.at[g, p], recv_sem=recv_sems.at[g, p], device_id=to, device_id_type=MESH)

            local.append(pltpu.make_async_copy(src, slot(me, whole), local_sems.at[p]))
            first += [copy(0, me, sibling, src=src), copy(1, me, (*xn, c), src=src), copy(2, me, (*yn, c), src=src)]
            got_ici += [copy(1, (*xn, c), me), copy(2, (*yn, c), me)]
            pass1 += [copy(3, (*xn, c), (*yn, c), left), copy(4, (*yn, c), (*xn, c), right),
                      copy(5, (*xn, c), sibling), copy(6, (*yn, c), sibling)]
            got_fwd += [copy(3, (*dg, c), me, left), copy(4, (*dg, c), me, right)]
            pass2 += [copy(7, (*dg, c), sibling, left), copy(8, (*dg, c), sibling, right)]
            got_d2d += [copy(0, sibling, me), copy(5, (*xn, 1 - c), me), copy(6, (*yn, 1 - c), me),
                        copy(7, (*dg, 1 - c), me, left), copy(8, (*dg, 1 - c), me, right)]
        return local, first, pass1, pass2, got_ici, got_fwd, got_d2d

    def start(cins, couts, sems):
        local, first, *_ = copies(cins, couts, sems)
        for cp in local + first:
            cp.start()

    def pass_on(cins, couts, sems):
        _, _, pass1, _, got_ici, _, _ = copies(cins, couts, sems)
        for cp in got_ici:
            cp.wait_recv()
        for cp in pass1:
            cp.start()

    def finish(cins, couts, sems):
        local, first, pass1, pass2, _, got_fwd, got_d2d = copies(cins, couts, sems)
        for cp in got_fwd:
            cp.wait_recv()
        for cp in pass2:
            cp.start()
        for cp in got_d2d:
            cp.wait_recv()
        for cp in first + pass1 + pass2:
            cp.wait_send()
        for cp in local:
            cp.wait()

    sem_shapes = [pltpu.SemaphoreType.DMA((9, n)), pltpu.SemaphoreType.DMA((9, n)), pltpu.SemaphoreType.DMA((n,))]
    return CommJob(inputs, aliases, out_shapes, sem_shapes, start, finish, relay=pass_on)


def sibling_exchange_job(grads):
    n = len(grads)

    def copies(cins, couts, sems):
        send_sems, recv_sems = sems
        x, y, c = _mesh_pos()
        return [pltpu.make_async_remote_copy(
            src_ref=cins[q].at[:, 1 - c], dst_ref=couts[q], send_sem=send_sems.at[q], recv_sem=recv_sems.at[q],
            device_id=(x, y, 1 - c), device_id_type=MESH) for q in range(n)]

    def start(cins, couts, sems):
        for cp in copies(cins, couts, sems):
            cp.start()

    def finish(cins, couts, sems):
        cps = copies(cins, couts, sems)
        for cp in cps:
            cp.wait_recv()
        for cp in cps:
            cp.wait_send()

    return CommJob(grads, {}, [_sds((4,) + g.shape[2:], g.dtype) for g in grads],
                   [pltpu.SemaphoreType.DMA((n,)), pltpu.SemaphoreType.DMA((n,))], start, finish)


def chip_exchange_job(pieces):
    inputs, aliases, out_shapes, plan, where = [], {}, [], [], {}
    for partial, contrib, key, layer, row0, nrows, col0, cols in pieces:
        if key not in where:
            where[key] = len(out_shapes)
            out_shapes.append(_sds((4, DEPTH, partial.shape[1], cols), partial.dtype))
            if contrib is not None:
                aliases[len(inputs)] = where[key]
                inputs.append(contrib)
        plan.append((len(inputs), where[key], layer, row0, nrows, col0, partial.shape[2]))
        inputs.append(partial)
    n = len(plan)

    def copies(cins, couts, sems):
        send_sems, recv_sems, local_sems = sems
        x, y, c = _mesh_pos()
        mine = 2 * x + y
        local, sends, recvs = [], [], []
        for p, (i_in, i_out, layer, row0, nrows, col0, ncols) in enumerate(plan):
            rows, lanes = pl.ds(row0, nrows), pl.ds(col0, ncols)
            local.append(pltpu.make_async_copy(cins[i_in].at[mine, rows, :], couts[i_out].at[mine, layer, rows, lanes],
                                               local_sems.at[p]))
            for j, (cx, cy) in enumerate([(1 - x, y), (x, 1 - y), (1 - x, 1 - y)]):
                theirs = 2 * cx + cy

                def copy(slot_there, j=j, p=p, cx=cx, cy=cy, theirs=theirs, i_in=i_in, i_out=i_out, layer=layer,
                         rows=rows, lanes=lanes):
                    return pltpu.make_async_remote_copy(
                        src_ref=cins[i_in].at[theirs, rows, :], dst_ref=couts[i_out].at[slot_there, layer, rows, lanes],
                        send_sem=send_sems.at[j, p], recv_sem=recv_sems.at[j, p], device_id=(cx, cy, c), device_id_type=MESH)
                sends.append(copy(mine))
                recvs.append(copy(theirs))
        return local, sends, recvs

    def start(cins, couts, sems):
        local, sends, _ = copies(cins, couts, sems)
        for cp in local + sends:
            cp.start()

    def finish(cins, couts, sems):
        local, sends, recvs = copies(cins, couts, sems)
        for cp in recvs:
            cp.wait_recv()
        for cp in sends:
            cp.wait_send()
        for cp in local:
            cp.wait()

    sem_shapes = [pltpu.SemaphoreType.DMA((3, n)), pltpu.SemaphoreType.DMA((3, n)), pltpu.SemaphoreType.DMA((n,))]
    return CommJob(inputs, aliases, out_shapes, sem_shapes, start, finish)


def fwd_proj(x, g1, wt_in, l, comm=None):
    s = x.shape[0]
    tm = min(512, s)
    tn = 1408

    def body(x_ref, g_ref, w_ref, o_ref, xn_ref):
        @pl.when(pl.program_id(1) == 0)
        def _():
            xv = x_ref[...]
            r = lax.rsqrt(jnp.mean(xv * xv, axis=-1, keepdims=True) + EPS)
            xn_ref[...] = (xv * r * g_ref[l:l + 1, :]).astype(BF16)

        o_ref[...] = _dot(xn_ref[...], w_ref[...], _NT).astype(BF16)

    return _call(
        body, comm, (x, g1, wt_in), grid=(s // tm, IN_W // tn),
        in_specs=[pl.BlockSpec((tm, D), lambda i, j: (i, 0)),
                  pl.BlockSpec((DEPTH, D), lambda i, j: (0, 0)),
                  pl.BlockSpec((tn, D), lambda i, j: (j, 0))],
        out_specs=pl.BlockSpec((tm, tn), lambda i, j: (i, j)),
        out_shape=_sds((s, IN_W), BF16),
        scratch_shapes=[pltpu.VMEM((tm, D), BF16)], name=f"fwd_proj{l}")


def _scan_fwd(a_ref, u_ref, h_ref, h0, n_rows):
    row = lax.broadcasted_iota(jnp.int32, (8, BW), 0)

    def body(g, hprev):
        r = pl.multiple_of(g * 8, 8)
        a = a_ref[pl.ds(r, 8), :]
        u = u_ref[pl.ds(r, 8), :]
        for sft in (1, 2, 4):
            a_sh = jnp.where(row >= sft, pltpu.roll(a, sft, 0), 1.0)
            u_sh = jnp.where(row >= sft, pltpu.roll(u, sft, 0), 0.0)
            u = u + a * u_sh
            a = a * a_sh
        h = u + a * hprev
        h_ref[pl.ds(r, 8), :] = h
        return h[7:8, :]

    return lax.fori_loop(0, n_rows // 8, body, h0)


def _scan_bwd(b_ref, g_ref, o_ref, c0, n_rows):
    row = lax.broadcasted_iota(jnp.int32, (8, BW), 0)

    def body(k, cnext):
        r = pl.multiple_of((n_rows // 8 - 1 - k) * 8, 8)
        b = b_ref[pl.ds(r, 8), :]
        g = g_ref[pl.ds(r, 8), :]
        for sft in (1, 2, 4):
            b_sh = jnp.where(row < 8 - sft, pltpu.roll(b, 8 - sft, 0), 1.0)
            g_sh = jnp.where(row < 8 - sft, pltpu.roll(g, 8 - sft, 0), 0.0)
            g = g + b * g_sh
            b = b * b_sh
        o = g + b * cnext
        o_ref[pl.ds(r, 8), :] = o
        return o[0:1, :]

    return lax.fori_loop(0, n_rows // 8, body, c0)


def _shifted_copies(buf, shifted, n_rows):
    for r in range(1, 8):
        shifted[r - 1, 0:n_rows - 8, :] = buf[pl.ds(r, n_rows - 8), :]


def _window(buf, shifted, off, t):
    r = off % 8
    return buf[pl.ds(off, t), :] if r == 0 else shifted[r - 1, pl.ds(off - r, t), :]


def _branch_fwd_math(cur_ref, halo_ref, cw_ref, vec_ref, wx_ref, wa_ref, bufa, bufb, bufd, xd, first, t, saved_ref=None):
    def halo(c0):
        v = halo_ref[:, c0:c0 + BW].astype(F32)
        return jnp.where(first, 0.0, v)

    def cur(c0):
        return cur_ref[:, c0:c0 + BW].astype(F32)

    out = {}
    bufa[0:HALO, :] = halo(C_AX)
    bufa[HALO:HALO + t, :] = cur(C_AX)
    ca = jnp.zeros((t, BW), F32) + vec_ref[V_CAB:V_CAB + 1, :]
    for k in range(CONV_A):
        ca = ca + cw_ref[CW_A + k:CW_A + k + 1, :] * bufa[pl.ds(HALO - (CONV_A - 1) + k, t), :]
    if saved_ref is None:
        gi = _sigmoid(_dot(ca, wx_ref[...], _NN) + vec_ref[V_BX:V_BX + 1, :])
        gr = _sigmoid(_dot(ca, wa_ref[...], _NN) + vec_ref[V_BA:V_BA + 1, :])
    else:
        gi, gr = saved_ref[:, BW:2 * BW], saved_ref[:, 2 * BW:3 * BW]
    sp = _softplus(-vec_ref[V_LAM:V_LAM + 1, :])
    la = -LRU_C * sp * gr
    a = jnp.exp(la)
    mult = jnp.sqrt(_neg_expm1(2.0 * la))
    out.update(ca=ca, gi=gi, gr=gr, sp=sp, a=a, mult=mult)
    bufb[0:HALO, :] = halo(C_BC) * halo(C_BV)
    bufb[HALO:HALO + t, :] = cur(C_BC) * cur(C_BV)
    cb = jnp.zeros((t, BW), F32)
    for k in range(CONV_B):
        cb = cb + cw_ref[CW_B + k:CW_B + k + 1, :] * bufb[pl.ds(HALO - (CONV_B - 1) + k, t), :]
    out.update(cb=cb)
    bufd[0:HALO, :] = halo(C_D1) * _sigmoid(halo(C_D2))
    s2 = _sigmoid(cur(C_D2))
    bufd[HALO:HALO + t, :] = cur(C_D1) * s2
    _shifted_copies(bufd, xd, t + HALO)
    if saved_ref is None:
        cd = jnp.zeros((t, BW), F32) + vec_ref[V_CDB:V_CDB + 1, :]
        for k in range(CONV_D):
            cd = cd + cw_ref[CW_D + k:CW_D + k + 1, :] * _window(bufd, xd, HALO - (CONV_D - 1) + k, t)
    else:
        cd = saved_ref[:, 0:BW]
    mu = jnp.mean(cd, axis=-1, keepdims=True)
    xc = cd - mu
    rstd = lax.rsqrt(jnp.mean(xc * xc, axis=-1, keepdims=True) + EPS)
    xh = xc * rstd
    ln = xh * vec_ref[V_LNG:V_LNG + 1, :] + vec_ref[V_LNB:V_LNB + 1, :]
    out.update(s2=s2, xh=xh, rstd=rstd, ln=ln, cd=cd)
    return out


def fwd_branch(proj, convw, vecs, wx_bd, wa_bd, l, comm=None):
    s = proj.shape[0]
    t = min(256, s)

    def body(cur_ref, halo_ref, cw_ref, vec_ref, wx_ref, wa_ref, pre_ref, h_ref, sv_ref, bufa, bufb, bufd, xd, a_s, u_s, hcar):
        first = pl.program_id(0) == 0

        @pl.when(first)
        def _():
            hcar[...] = jnp.zeros((1, BW), F32)

        v = _branch_fwd_math(cur_ref, halo_ref, cw_ref, vec_ref, wx_ref, wa_ref, bufa, bufb, bufd, xd, first, t)
        a_s[...] = v["a"]
        u_s[...] = v["ca"] * v["gi"] * v["mult"]
        sv_ref[:, 0:BW] = v["cd"]
        sv_ref[:, BW:2 * BW] = v["gi"]
        sv_ref[:, 2 * BW:3 * BW] = v["gr"]
        hcar[...] = _scan_fwd(a_s, u_s, h_ref, hcar[...], t)
        gg, _ = _gelu_and_grad(cur_ref[:, C_AG:C_AG + BW].astype(F32))
        pre_ref[:, 0:BW] = (h_ref[...] * gg).astype(BF16)
        pre_ref[:, BW:2 * BW] = (cur_ref[:, C_BB:C_BB + BW].astype(F32) * v["cb"]).astype(BF16)
        ln = v["ln"]
        pre_ref[:, 2 * BW:3 * BW] = (ln * _sigmoid(ln)).astype(BF16)

    hb = t // HALO
    return _call(
        body, comm, (proj, proj, convw, vecs, wx_bd, wa_bd), grid=(s // t,),
        in_specs=[pl.BlockSpec((t, GL0), lambda i: (i, 0)),
                  pl.BlockSpec((HALO, GL0), lambda i: (jnp.maximum(i * hb - 1, 0), 0)),
                  pl.BlockSpec((None, CW_ROWS, BW), lambda i: (l, 0, 0)),
                  pl.BlockSpec((None, V_ROWS, BW), lambda i: (l, 0, 0)),
                  pl.BlockSpec((None, BW, BW), lambda i: (l, 0, 0)),
                  pl.BlockSpec((None, BW, BW), lambda i: (l, 0, 0))],
        out_specs=[pl.BlockSpec((t, 3 * BW), lambda i: (i, 0)), pl.BlockSpec((t, BW), lambda i: (i, 0)),
                   pl.BlockSpec((t, 3 * BW), lambda i: (i, 0))],
        out_shape=[_sds((s, 3 * BW), BF16), _sds((s, BW), F32), _sds((s, 3 * BW), F32)],
        scratch_shapes=[pltpu.VMEM((t + HALO, BW), F32)] * 3 + [pltpu.VMEM((7, t + HALO - 8, BW), F32)]
        + [pltpu.VMEM((t, BW), F32)] * 2 + [pltpu.VMEM((1, BW), F32)],
        name=f"fwd_branch{l}")


GRP = N_HEADS // N_KV


ATT_SUB = 2


def _attn_mask_bias(first_block):
    shape = (GRP * ATT_BLK, 2 * ATT_BLK)
    qi = lax.broadcasted_iota(jnp.int32, shape, 0) & (ATT_BLK - 1)
    ki = lax.broadcasted_iota(jnp.int32, shape, 1)
    dist = qi + ATT_BLK - ki
    valid = (dist >= 0) & (dist < ATT_BLK)
    if first_block is not None:
        valid = valid & (jnp.logical_not(first_block) | (ki >= ATT_BLK))
    return dist.astype(F32), valid


def _attn_units(q_ref, kvp_ref, kvc_ref, first_step):
    units = []
    for b in range(ATT_SUB):
        rows = slice(b * ATT_BLK, (b + 1) * ATT_BLK)
        if b == 0:
            prev = lambda c0, c1: kvp_ref[:, c0:c1]
        else:
            prev = lambda c0, c1, b=b: kvc_ref[(b - 1) * ATT_BLK:b * ATT_BLK, c0:c1]
        for hk in range(N_KV):
            units.append(dict(b=b, hk=hk, rows=rows, q=lambda c0, c1, rows=rows: q_ref[rows, c0:c1], prev=prev,
                              cur=lambda c0, c1, rows=rows: kvc_ref[rows, c0:c1], first=first_step if b == 0 else None))
    return units


def _per_head(hk, values):
    hl = lax.broadcasted_iota(jnp.int32, (GRP * ATT_BLK, 1), 0) // ATT_BLK
    out = values[GRP - 1]
    for j in range(GRP - 2, -1, -1):
        out = jnp.where(hl == j, values[j], out)
    return out


def _attn_probs(units, vec_ref):
    us = range(len(units))
    heads = [range(u["hk"] * GRP, (u["hk"] + 1) * GRP) for u in units]
    masks = {id(u["first"]): _attn_mask_bias(u["first"]) for u in units}
    distf = [masks[id(u["first"])][0] for u in units]
    valid = [masks[id(u["first"])][1] for u in units]
    q4 = [jnp.concatenate([units[i]["q"](h * HD, (h + 1) * HD) for h in heads[i]], axis=0) for i in us]
    kcol = [(u["hk"] * HD, (u["hk"] + 1) * HD) for u in units]
    vcol = [((N_KV + u["hk"]) * HD, (N_KV + u["hk"] + 1) * HD) for u in units]
    k2 = [jnp.concatenate([units[i]["prev"](*kcol[i]), units[i]["cur"](*kcol[i])], axis=0) for i in us]
    v2 = [jnp.concatenate([units[i]["prev"](*vcol[i]), units[i]["cur"](*vcol[i])], axis=0) for i in us]
    slope = [_per_head(units[i]["hk"], [2.0 ** (-8.0 * (h + 1) / N_HEADS) for h in heads[i]]) for i in us]
    sink = [_per_head(units[i]["hk"], [vec_ref[V_SINK:V_SINK + 1, h:h + 1] for h in heads[i]]) for i in us]
    sc = [_dot(q4[i], k2[i], _NT) for i in us]
    sc = [jnp.where(valid[i], sc[i] * (HD ** -0.5) - slope[i] * distf[i], NEG_INF) for i in us]
    m = [jnp.maximum(jnp.max(sc[i], axis=-1, keepdims=True), sink[i]) for i in us]
    p = [jnp.exp(sc[i] - m[i]) for i in us]
    es = [jnp.exp(sink[i] - m[i]) for i in us]
    inv = [1.0 / (jnp.sum(p[i], axis=-1, keepdims=True) + es[i]) for i in us]
    return [(q4[i], k2[i], v2[i], p[i] * inv[i], es[i] * inv[i]) for i in us]


def fwd_attn(proj, vecs, l, comm=None):
    s = proj.shape[0]
    t = ATT_SUB * ATT_BLK

    def body(q_ref, kvp_ref, kvc_ref, vec_ref, o_ref):
        units = _attn_units(q_ref, kvp_ref, kvc_ref, pl.program_id(0) == 0)
        groups = _attn_probs(units, vec_ref)
        outs = [_dot(p, v2, _NN).astype(BF16) for _, _, v2, p, _ in groups]
        for u, out in zip(units, outs):
            for j in range(GRP):
                h = u["hk"] * GRP + j
                o_ref[u["rows"], h * HD:(h + 1) * HD] = out[j * ATT_BLK:(j + 1) * ATT_BLK]

    return _call(
        body, comm, (proj, proj, proj, vecs), grid=(s // t,),
        in_specs=[pl.BlockSpec((t, BW), lambda i: (i, C_Q // BW)),
                  pl.BlockSpec((ATT_BLK, 256), lambda i: (jnp.maximum(ATT_SUB * i - 1, 0), C_K // 256)),
                  pl.BlockSpec((t, 256), lambda i: (i, C_K // 256)),
                  pl.BlockSpec((None, V_ROWS, BW), lambda i: (l, 0, 0))],
        out_specs=pl.BlockSpec((t, BW), lambda i: (i, 0)),
        out_shape=_sds((s, BW), BF16), name=f"fwd_attn{l}")


def fwd_merge(x, proj, pre_abd, pre_c, wt_a, wt_b, wt_c, wt_d, w_o, l, comm=None):
    s = x.shape[0]
    tm = min(256, s)

    def body(x_ref, gl_ref, pabd_ref, pc_ref, wa_ref, wb_ref, wc_ref, wd_ref, wo_ref, y_ref, mg_ref, h1_ref):
        pres = (pabd_ref[:, 0:BW], pabd_ref[:, BW:2 * BW], pc_ref[...], pabd_ref[:, 2 * BW:3 * BW])
        merged = jnp.zeros((tm, D), F32)
        for k, (pre, w_ref) in enumerate(zip(pres, (wa_ref, wb_ref, wc_ref, wd_ref))):
            yk = _dot(pre, w_ref[...], _NT)
            y_ref[:, k * D:(k + 1) * D] = yk.astype(BF16)
            merged = merged + _sigmoid(gl_ref[:, k * D:(k + 1) * D].astype(F32)) * yk
        mg_ref[...] = merged.astype(BF16)
        h1_ref[...] = x_ref[...] + _dot(merged, wo_ref[...], _NN)

    wspec = pl.BlockSpec((D, BW), lambda i: (0, 0))
    return _call(
        body, comm, (x, proj, pre_abd, pre_c, wt_a, wt_b, wt_c, wt_d, w_o), grid=(s // tm,),
        in_specs=[pl.BlockSpec((tm, D), lambda i: (i, 0)),
                  pl.BlockSpec((E(tm), E(4 * D)), lambda i: (i * tm, GL0)),
                  pl.BlockSpec((tm, 3 * BW), lambda i: (i, 0)),
                  pl.BlockSpec((tm, BW), lambda i: (i, 0)),
                  wspec, wspec, wspec, wspec,
                  pl.BlockSpec((D, D), lambda i: (0, 0))],
        out_specs=[pl.BlockSpec((tm, 4 * D), lambda i: (i, 0)), pl.BlockSpec((tm, D), lambda i: (i, 0)),
                   pl.BlockSpec((tm, D), lambda i: (i, 0))],
        out_shape=[_sds((s, 4 * D), BF16), _sds((s, D), BF16), _sds((s, D), F32)], name=f"fwd_merge{l}")


def fwd_ffn(h1, g2, wt_gate, wt_up, w_down, l, comm=None):
    s = h1.shape[0]
    tm = min(512, s)
    fc = FF // 2

    def body(h_ref, g_ref, wg_ref, wu_ref, wd_ref, xo_ref, fg_ref, fu_ref, hn_ref, acc_ref):
        j = pl.program_id(1)

        @pl.when(j == 0)
        def _():
            hv = h_ref[...]
            r = lax.rsqrt(jnp.mean(hv * hv, axis=-1, keepdims=True) + EPS)
            hn_ref[...] = (hv * r * g_ref[l:l + 1, :]).astype(BF16)
            acc_ref[...] = hv

        fg = _dot(hn_ref[...], wg_ref[...], _NT)
        fu = _dot(hn_ref[...], wu_ref[...], _NT)
        fg_ref[...] = fg.astype(BF16)
        fu_ref[...] = fu.astype(BF16)
        acc_ref[...] += _dot(fg * _sigmoid(fg) * fu, wd_ref[...], _NN)

        @pl.when(j == pl.num_programs(1) - 1)
        def _():
            xo_ref[...] = acc_ref[...]

    wspec = pl.BlockSpec((fc, D), lambda i, j: (j, 0))
    return _call(
        body, comm, (h1, g2, wt_gate, wt_up, w_down), grid=(s // tm, FF // fc),
        in_specs=[pl.BlockSpec((tm, D), lambda i, j: (i, 0)), pl.BlockSpec((DEPTH, D), lambda i, j: (0, 0)),
                  wspec, wspec, wspec],
        out_specs=[pl.BlockSpec((tm, D), lambda i, j: (i, 0)), pl.BlockSpec((tm, fc), lambda i, j: (i, j)),
                   pl.BlockSpec((tm, fc), lambda i, j: (i, j))],
        out_shape=[_sds((s, D), F32), _sds((s, FF), BF16), _sds((s, FF), BF16)],
        scratch_shapes=[pltpu.VMEM((tm, D), BF16), pltpu.VMEM((tm, D), F32)], name=f"fwd_ffn{l}")


def loss_head(x, gf, target):
    s = x.shape[0]
    tm = min(512, s)

    def body(x_ref, g_ref, t_ref, dx_ref, st_ref):
        @pl.when(pl.program_id(0) == 0)
        def _():
            st_ref[...] = jnp.zeros((8, D), F32)

        xv = x_ref[...]
        g = g_ref[...]
        r = lax.rsqrt(jnp.mean(xv * xv, axis=-1, keepdims=True) + EPS)
        n = xv * r
        err = n * g - t_ref[...]
        dy = err * (1.0 / D)
        dn = dy * g
        dx_ref[...] = r * (dn - n * jnp.mean(dn * n, axis=-1, keepdims=True))
        st_ref[0:1, :] += jnp.sum(dy * n, axis=0, keepdims=True)
        lsum = 0.5 * jnp.sum(jnp.mean(err * err, axis=-1, keepdims=True), axis=0, keepdims=True)
        st_ref[1:2, :] += jnp.broadcast_to(lsum, (1, D))

    return pl.pallas_call(
        body, grid=(s // tm,),
        in_specs=[pl.BlockSpec((tm, D), lambda i: (i, 0)), pl.BlockSpec((1, D), lambda i: (0, 0)),
                  pl.BlockSpec((tm, D), lambda i: (i, 0))],
        out_specs=[pl.BlockSpec((tm, D), lambda i: (i, 0)), pl.BlockSpec((8, D), lambda i: (0, 0))],
        out_shape=[_sds((s, D), F32), _sds((8, D), F32)],
        compiler_params=_cparams(1), name="loss_head")(x, gf, target)


def _edge_index(j, i, n_j, n_i):
    return jnp.where((j == 0) | (j == n_j - 1), i, n_i - 1)


def bwd_ffn(dxo, h1, fg, fu, g2, wt_gate, wt_up, w_down, l, comm=None):
    s = h1.shape[0]
    tm = min(512, s)
    fc = 256
    n_j, n_i = FF // fc, s // tm

    def body(dxo_ref, h_ref, fg_ref, fu_ref, g_ref, wg_ref, wu_ref, wd_ref,
             dh_ref, dwg_ref, dwu_ref, dwd_ref, st_ref, dhn, dxo_b, hn_b, ag, au, ad):
        j, i = pl.program_id(0), pl.program_id(1)
        rows = pl.ds(pl.multiple_of(i * tm, tm), tm)
        g = g_ref[l:l + 1, :]

        @pl.when(j == 0)
        def _():
            hv = h_ref[...]
            r = lax.rsqrt(jnp.mean(hv * hv, axis=-1, keepdims=True) + EPS)
            hn_b[rows, :] = (hv * r * g).astype(BF16)
            dxo_b[rows, :] = dxo_ref[...].astype(BF16)
            dhn[rows, :] = jnp.zeros((tm, D), F32)

        @pl.when((j == 0) & (i == 0))
        def _():
            st_ref[...] = jnp.zeros((8, D), F32)

        @pl.when(i == 0)
        def _():
            ag[...] = jnp.zeros((fc, D), F32)
            au[...] = jnp.zeros((fc, D), F32)
            ad[...] = jnp.zeros((fc, D), F32)

        fgv = fg_ref[...].astype(F32)
        fuv = fu_ref[...].astype(F32)
        sg = _sigmoid(fgv)
        sil = fgv * sg
        dxb = dxo_b[rows, :]
        hnb = hn_b[rows, :]
        d_act = _dot(dxb, wd_ref[...], _NT)
        ad[...] += _dot(sil * fuv, dxb, _TN)
        d_fg = (d_act * fuv * (sg * (1.0 + fgv * (1.0 - sg)))).astype(BF16)
        d_fu = (d_act * sil).astype(BF16)
        ag[...] += _dot(d_fg, hnb, _TN)
        au[...] += _dot(d_fu, hnb, _TN)
        dhn[rows, :] += _dot(d_fg, wg_ref[...], _NN) + _dot(d_fu, wu_ref[...], _NN)

        @pl.when(i == n_i - 1)
        def _():
            dwg_ref[...] = ag[...].astype(BF16)
            dwu_ref[...] = au[...].astype(BF16)
            dwd_ref[...] = ad[...].astype(BF16)

        @pl.when(j == n_j - 1)
        def _():
            hv = h_ref[...]
            r = lax.rsqrt(jnp.mean(hv * hv, axis=-1, keepdims=True) + EPS)
            n = hv * r
            dv = dhn[rows, :]
            dn = dv * g
            dh_ref[...] = dxo_ref[...] + r * (dn - n * jnp.mean(dn * n, axis=-1, keepdims=True))
            st_ref[0:1, :] += jnp.sum(dv * n, axis=0, keepdims=True)

    edge = lambda j, i: (_edge_index(j, i, n_j, n_i), 0)
    wspec = pl.BlockSpec((fc, D), lambda j, i: (j, 0))
    dwspec = pl.BlockSpec((fc, D), lambda j, i: (j, 0))
    return _call(
        body, comm, (dxo, h1, fg, fu, g2, wt_gate, wt_up, w_down), grid=(n_j, n_i),
        in_specs=[pl.BlockSpec((tm, D), edge),
                  pl.BlockSpec((tm, D), edge),
                  pl.BlockSpec((tm, fc), lambda j, i: (i, j)), pl.BlockSpec((tm, fc), lambda j, i: (i, j)),
                  pl.BlockSpec((DEPTH, D), lambda j, i: (0, 0)), wspec, wspec, wspec],
        out_specs=[pl.BlockSpec((tm, D), lambda j, i: (jnp.where(j == n_j - 1, i, 0), 0)),
                   dwspec, dwspec, dwspec, pl.BlockSpec((8, D), lambda j, i: (0, 0))],
        out_shape=[_sds((s, D), F32), _sds((FF, D), BF16), _sds((FF, D), BF16), _sds((FF, D), BF16), _sds((8, D), F32)],
        scratch_shapes=[pltpu.VMEM((s, D), F32), pltpu.VMEM((s, D), BF16), pltpu.VMEM((s, D), BF16),
                        pltpu.VMEM((fc, D), F32), pltpu.VMEM((fc, D), F32), pltpu.VMEM((fc, D), F32)],
        name=f"bwd_ffn{l}")


def bwd_merge(dh1, y4, proj, merged, pre_abd, pre_c, wt_a, wt_b, wt_c, wt_d, w_o, l, comm=None):
    s = dh1.shape[0]
    tm = min(256, s)
    n_i = s // tm

    def body(dh_ref, y_ref, gl_ref, mg_ref, pabd_ref, pc_ref, wa_ref, wb_ref, wc_ref, wd_ref, wo_ref,
             dgl_ref, dpre_ref, dwo_ref, dwa_ref, dwb_ref, dwc_ref, dwd_ref, ao, aa, ab, ac, ad):
        i = pl.program_id(0)
        accs = (aa, ab, ac, ad)

        @pl.when(i == 0)
        def _():
            ao[...] = jnp.zeros((D, D), F32)
            for acc in accs:
                acc[...] = jnp.zeros((D, BW), F32)

        dhb = dh_ref[...].astype(BF16)
        dmg = _dot(dhb, wo_ref[...], _NT)
        ao[...] += _dot(mg_ref[...], dhb, _TN)
        pres = (pabd_ref[:, 0:BW], pabd_ref[:, BW:2 * BW], pc_ref[...], pabd_ref[:, 2 * BW:3 * BW])
        for k, (pre, w_ref, acc) in enumerate(zip(pres, (wa_ref, wb_ref, wc_ref, wd_ref), accs)):
            gk = _sigmoid(gl_ref[:, k * D:(k + 1) * D].astype(F32))
            yk = y_ref[:, k * D:(k + 1) * D].astype(F32)
            dgl_ref[:, k * D:(k + 1) * D] = (dmg * yk * gk * (1.0 - gk)).astype(BF16)
            dyk = (dmg * gk).astype(BF16)
            dpre_ref[:, k * BW:(k + 1) * BW] = _dot(dyk, w_ref[...], _NN).astype(BF16)
            acc[...] += _dot(dyk, pre, _TN)

        @pl.when(i == n_i - 1)
        def _():
            dwo_ref[...] = ao[...].astype(BF16)
            for o_ref, acc in zip((dwa_ref, dwb_ref, dwc_ref, dwd_ref), accs):
                o_ref[...] = acc[...].astype(BF16)

    wspec = pl.BlockSpec((D, BW), lambda i: (0, 0))
    dwspec = pl.BlockSpec((D, BW), lambda i: (0, 0))
    return _call(
        body, comm, (dh1, y4, proj, merged, pre_abd, pre_c, wt_a, wt_b, wt_c, wt_d, w_o), grid=(n_i,),
        in_specs=[pl.BlockSpec((tm, D), lambda i: (i, 0)),
                  pl.BlockSpec((tm, 4 * D), lambda i: (i, 0)),
                  pl.BlockSpec((E(tm), E(4 * D)), lambda i: (i * tm, GL0)),
                  pl.BlockSpec((tm, D), lambda i: (i, 0)),
                  pl.BlockSpec((tm, 3 * BW), lambda i: (i, 0)),
                  pl.BlockSpec((tm, BW), lambda i: (i, 0)),
                  wspec, wspec, wspec, wspec,
                  pl.BlockSpec((D, D), lambda i: (0, 0))],
        out_specs=[pl.BlockSpec((E(tm), E(4 * D)), lambda i: (i * tm, GL0)),
                   pl.BlockSpec((tm, 4 * BW), lambda i: (i, 0)),
                   pl.BlockSpec((D, D), lambda i: (0, 0)), dwspec, dwspec, dwspec, dwspec],
        out_shape=[_sds((s, IN_W), BF16), _sds((s, 4 * BW), BF16), _sds((D, D), BF16)] + [_sds((D, BW), BF16)] * 4,
        scratch_shapes=[pltpu.VMEM((D, D), F32)] + [pltpu.VMEM((D, BW), F32)] * 4, name=f"bwd_merge{l}")


def bwd_attn(proj, dpre, vecs, l, comm=None):
    s = proj.shape[0]
    t = ATT_SUB * ATT_BLK
    grp = N_HEADS // N_KV

    def body(q_ref, kvp_ref, kvc_ref, do_ref, vec_ref, dq_ref, dkc_ref, dkp_ref, st_ref):
        @pl.when(pl.program_id(0) == 0)
        def _():
            st_ref[...] = jnp.zeros((8, 128), F32)

        lane = lax.broadcasted_iota(jnp.int32, (1, 128), 1)
        dsink = jnp.zeros((1, 128), F32)
        units = _attn_units(q_ref, kvp_ref, kvc_ref, pl.program_id(0) == 0)
        groups = _attn_probs(units, vec_ref)
        us = range(len(units))
        do4s = [jnp.concatenate([do_ref[u["rows"], h * HD:(h + 1) * HD] for h in range(u["hk"] * grp, (u["hk"] + 1) * grp)],
                                axis=0) for u in units]
        dps = [_dot(do4s[i], groups[i][2], _NT) for i in us]
        deltas = [jnp.sum(groups[i][3] * dps[i], axis=-1, keepdims=True) for i in us]
        dss = [groups[i][3] * (dps[i] - deltas[i]) * (HD ** -0.5) for i in us]
        dq4s = [_dot(dss[i], groups[i][1], _NN).astype(BF16) for i in us]
        dk2s = [_dot(dss[i], groups[i][0], _TN) for i in us]
        dv2s = [_dot(groups[i][3], do4s[i], _TN) for i in us]
        for i, u in enumerate(units):
            psd = groups[i][4] * deltas[i]
            for j in range(grp):
                h = u["hk"] * grp + j
                rows = slice(j * ATT_BLK, (j + 1) * ATT_BLK)
                dq_ref[u["rows"], h * HD:(h + 1) * HD] = dq4s[i][rows]
                dsink = dsink + jnp.where(lane == h, -jnp.sum(psd[rows], axis=0, keepdims=True), 0.0)
        for i, u in enumerate(units):
            nxt = [k for k, w in enumerate(units) if w["hk"] == u["hk"] and w["b"] == u["b"] + 1]
            for grad, c0 in ((dk2s, u["hk"] * HD), (dv2s, (N_KV + u["hk"]) * HD)):
                own = grad[i][ATT_BLK:]
                if nxt:
                    own = own + grad[nxt[0]][0:ATT_BLK]
                dkc_ref[u["rows"], c0:c0 + HD] = own.astype(BF16)
                if u["b"] == 0:
                    dkp_ref[:, c0:c0 + HD] = grad[i][0:ATT_BLK].astype(BF16)
        st_ref[0:1, :] += dsink

    return _call(
        body, comm, (proj, proj, proj, dpre, vecs), grid=(s // t,),
        in_specs=[pl.BlockSpec((t, BW), lambda i: (i, C_Q // BW)),
                  pl.BlockSpec((ATT_BLK, 256), lambda i: (jnp.maximum(ATT_SUB * i - 1, 0), C_K // 256)),
                  pl.BlockSpec((t, 256), lambda i: (i, C_K // 256)),
                  pl.BlockSpec((t, BW), lambda i: (i, 2)),
                  pl.BlockSpec((None, V_ROWS, BW), lambda i: (l, 0, 0))],
        out_specs=[pl.BlockSpec((t, BW), lambda i: (i, 0)), pl.BlockSpec((t, 256), lambda i: (i, 0)),
                   pl.BlockSpec((ATT_BLK, 256), lambda i: (i, 0)), pl.BlockSpec((8, 128), lambda i: (0, 0))],
        out_shape=[_sds((s, BW), BF16), _sds((s, 256), BF16), _sds((s // ATT_SUB, 256), BF16), _sds((8, 128), F32)],
        name=f"bwd_attn{l}")


def bwd_branch(proj, dproj, dpre, h, saved, dq, dkc, dkp, convw, vecs, wx_bd, wa_bd, l, comm=None):
    s = proj.shape[0]
    t = 2 * ATT_BLK
    nt = s // t
    nb = s // ATT_BLK
    hb = t // HALO

    def body(cur_ref, halo_ref, dpre_ref, h_ref, hp_ref, dq_ref, dkc_ref, dkp_ref,
             cw_ref, vec_ref, wx_ref, wa_ref, sv_ref, dproj_in, dp_ref, dcw_ref, dvec_ref, dwx_ref, dwa_ref,
             bufa, bufb, bufd, xd, xg, a_ext, hbuf, b_s, g_s, dh_s, ga, gb, gd, dhcar):
        del dproj_in
        step = pl.program_id(0)
        ti = nt - 1 - step
        first = ti == 0

        @pl.when(step == 0)
        def _():
            dcw_ref[...] = jnp.zeros((CW_ROWS, BW), F32)
            dvec_ref[...] = jnp.zeros((V_ROWS, BW), F32)
            dwx_ref[...] = jnp.zeros((BW, BW), F32)
            dwa_ref[...] = jnp.zeros((BW, BW), F32)
            dhcar[...] = jnp.zeros((1, BW), F32)
            a_ext[t:t + 8, :] = jnp.zeros((8, BW), F32)
            ga[t:t + 8, :] = jnp.zeros((8, BW), F32)
            gb[t:t + 8, :] = jnp.zeros((8, BW), F32)
            gd[t:t + HALO, :] = jnp.zeros((HALO, BW), F32)

        def cur(c0):
            return cur_ref[:, c0:c0 + BW].astype(F32)

        def rsum(v):
            return jnp.sum(v, axis=0, keepdims=True)

        def put(c0, v):
            dp_ref[:, c0:c0 + BW] = v.astype(BF16)

        v = _branch_fwd_math(cur_ref, halo_ref, cw_ref, vec_ref, wx_ref, wa_ref, bufa, bufb, bufd, xd, first, t, sv_ref)
        ca, gi, gr, sp, a, mult = v["ca"], v["gi"], v["gr"], v["sp"], v["a"], v["mult"]
        dpa = dpre_ref[:, 0:BW].astype(F32)
        gg, dgg = _gelu_and_grad(cur(C_AG))
        hv = h_ref[...]
        put(C_AG, dpa * hv * dgg)
        a_ext[0:t, :] = a
        b_s[...] = a_ext[pl.ds(1, t), :]
        g_s[...] = dpa * gg
        dhcar[...] = _scan_bwd(b_s, g_s, dh_s, dhcar[...], t)
        a_ext[t:t + 1, :] = a[0:1, :]
        dh = dh_s[...]
        hbuf[0:8, :] = jnp.where(first, 0.0, hp_ref[...])
        hbuf[8:8 + t, :] = hv
        da = dh * hbuf[pl.ds(7, t), :]
        d_ca = dh * gi * mult
        d_gi = dh * ca * mult
        d_mult = dh * ca * gi
        d_la = da * a - d_mult * (a * a) / mult
        lam = vec_ref[V_LAM:V_LAM + 1, :]
        dvec_ref[V_LAM:V_LAM + 1, :] += rsum(d_la * gr) * (LRU_C * _sigmoid(-lam))
        d_gr = d_la * (-LRU_C * sp)
        d_zr = d_gr * gr * (1.0 - gr)
        d_zi = d_gi * gi * (1.0 - gi)
        dvec_ref[V_BA:V_BA + 1, :] += rsum(d_zr)
        dvec_ref[V_BX:V_BX + 1, :] += rsum(d_zi)
        dwa_ref[...] += _dot(ca, d_zr, _TN)
        dwx_ref[...] += _dot(ca, d_zi, _TN)
        d_ca = d_ca + _dot(d_zi, wx_ref[...], _NT) + _dot(d_zr, wa_ref[...], _NT)
        dvec_ref[V_CAB:V_CAB + 1, :] += rsum(d_ca)
        ga[0:t, :] = d_ca
        d_ax = jnp.zeros((t, BW), F32)
        for k in range(CONV_A):
            d_ax = d_ax + cw_ref[CW_A + k:CW_A + k + 1, :] * ga[pl.ds(CONV_A - 1 - k, t), :]
            dcw_ref[CW_A + k:CW_A + k + 1, :] += rsum(d_ca * bufa[pl.ds(HALO - (CONV_A - 1) + k, t), :])
        ga[t:t + 8, :] = d_ca[0:8, :]
        put(C_AX, d_ax)
        dpb = dpre_ref[:, BW:2 * BW].astype(F32)
        put(C_BB, dpb * v["cb"])
        d_cb = dpb * cur(C_BB)
        gb[0:t, :] = d_cb
        d_cbin = jnp.zeros((t, BW), F32)
        for k in range(CONV_B):
            d_cbin = d_cbin + cw_ref[CW_B + k:CW_B + k + 1, :] * gb[pl.ds(CONV_B - 1 - k, t), :]
            dcw_ref[CW_B + k:CW_B + k + 1, :] += rsum(d_cb * bufb[pl.ds(HALO - (CONV_B - 1) + k, t), :])
        gb[t:t + 8, :] = d_cb[0:8, :]
        put(C_BC, d_cbin * cur(C_BV))
        put(C_BV, d_cbin * cur(C_BC))
        dpd = dpre_ref[:, 3 * BW:4 * BW].astype(F32)
        ln, xh, rstd, s2 = v["ln"], v["xh"], v["rstd"], v["s2"]
        sg = _sigmoid(ln)
        d_ln = dpd * sg * (1.0 + ln * (1.0 - sg))
        dvec_ref[V_LNG:V_LNG + 1, :] += rsum(d_ln * xh)
        dvec_ref[V_LNB:V_LNB + 1, :] += rsum(d_ln)
        d_xh = d_ln * vec_ref[V_LNG:V_LNG + 1, :]
        d_cd = rstd * (d_xh - jnp.mean(d_xh, axis=-1, keepdims=True)
                       - xh * jnp.mean(d_xh * xh, axis=-1, keepdims=True))
        dvec_ref[V_CDB:V_CDB + 1, :] += rsum(d_cd)
        gd[0:t, :] = d_cd
        _shifted_copies(gd, xg, t + HALO)
        d_dg = jnp.zeros((t, BW), F32)
        for k in range(CONV_D):
            d_dg = d_dg + cw_ref[CW_D + k:CW_D + k + 1, :] * _window(gd, xg, CONV_D - 1 - k, t)
            dcw_ref[CW_D + k:CW_D + k + 1, :] += rsum(d_cd * _window(bufd, xd, HALO - (CONV_D - 1) + k, t))
        gd[t:t + HALO, :] = d_cd[0:HALO, :]
        put(C_D1, d_dg * s2)
        put(C_D2, d_dg * cur(C_D1) * s2 * (1.0 - s2))
        dp_ref[:, C_Q:C_Q + BW] = dq_ref[...]
        dkp = jnp.where(step == 0, 0.0, dkp_ref[...].astype(F32))
        dp_ref[0:t - ATT_BLK, C_K:C_K + 256] = dkc_ref[0:t - ATT_BLK, :]
        dp_ref[t - ATT_BLK:t, C_K:C_K + 256] = (dkc_ref[t - ATT_BLK:t, :].astype(F32) + dkp).astype(BF16)

    rev = lambda i: nt - 1 - i
    full = lambda r, c: pl.BlockSpec((r, c), lambda i: (0, 0))
    return _call(
        body, comm, (proj, proj, dpre, h, h, dq, dkc, dkp, convw, vecs, wx_bd, wa_bd, saved, dproj), grid=(nt,),
        in_specs=[pl.BlockSpec((t, GL0), lambda i: (rev(i), 0)),
                  pl.BlockSpec((HALO, GL0), lambda i: (jnp.maximum(rev(i) * hb - 1, 0), 0)),
                  pl.BlockSpec((t, 4 * BW), lambda i: (rev(i), 0)),
                  pl.BlockSpec((t, BW), lambda i: (rev(i), 0)),
                  pl.BlockSpec((8, BW), lambda i: (jnp.maximum(rev(i) * (t // 8) - 1, 0), 0)),
                  pl.BlockSpec((t, BW), lambda i: (rev(i), 0)),
                  pl.BlockSpec((t, 256), lambda i: (rev(i), 0)),
                  pl.BlockSpec((ATT_BLK, 256), lambda i: (jnp.minimum(rev(i) + 1, nt - 1), 0)),
                  pl.BlockSpec((None, CW_ROWS, BW), lambda i: (l, 0, 0)),
                  pl.BlockSpec((None, V_ROWS, BW), lambda i: (l, 0, 0)),
                  pl.BlockSpec((None, BW, BW), lambda i: (l, 0, 0)),
                  pl.BlockSpec((None, BW, BW), lambda i: (l, 0, 0)),
                  pl.BlockSpec((t, 3 * BW), lambda i: (rev(i), 0)),
                  pl.BlockSpec(memory_space=pl.ANY)],
        out_specs=[pl.BlockSpec((t, GL0), lambda i: (rev(i), 0)),
                   full(CW_ROWS, BW), full(V_ROWS, BW), full(BW, BW), full(BW, BW)],
        out_shape=[_sds((s, IN_W), BF16), _sds((CW_ROWS, BW), F32), _sds((V_ROWS, BW), F32),
                   _sds((BW, BW), F32), _sds((BW, BW), F32)],
        scratch_shapes=[pltpu.VMEM((t + HALO, BW), F32)] * 3 + [pltpu.VMEM((7, t + HALO - 8, BW), F32)] * 2
        + [pltpu.VMEM((t + 8, BW), F32), pltpu.VMEM((t + 8, BW), F32)]
        + [pltpu.VMEM((t, BW), F32)] * 3
        + [pltpu.VMEM((t + 8, BW), F32), pltpu.VMEM((t + 8, BW), F32), pltpu.VMEM((t + HALO, BW), F32),
           pltpu.VMEM((1, BW), F32)],
        aliases={13: 0}, name=f"bwd_branch{l}")


def bwd_proj(dproj, x, dh1, g1, wt_in, l, comm=None):
    s = x.shape[0]
    tm = min(512, s)
    ck = 1408
    n_j, n_i = IN_W // ck, s // tm

    def body(dp_ref, x_ref, dh_ref, g_ref, w_ref, dx_ref, dw_ref, st_ref, dxn, xn_b, acc):
        j, i = pl.program_id(0), pl.program_id(1)
        rows = pl.ds(pl.multiple_of(i * tm, tm), tm)
        g = g_ref[l:l + 1, :]

        @pl.when(j == 0)
        def _():
            xv = x_ref[...]
            r = lax.rsqrt(jnp.mean(xv * xv, axis=-1, keepdims=True) + EPS)
            xn_b[rows, :] = (xv * r * g).astype(BF16)
            dxn[rows, :] = jnp.zeros((tm, D), F32)

        @pl.when((j == 0) & (i == 0))
        def _():
            st_ref[...] = jnp.zeros((8, D), F32)

        @pl.when(i == 0)
        def _():
            acc[...] = jnp.zeros((ck, D), F32)

        dp = dp_ref[...]
        dxn[rows, :] += _dot(dp, w_ref[...], _NN)
        acc[...] += _dot(dp, xn_b[rows, :], _TN)

        @pl.when(i == n_i - 1)
        def _():
            dw_ref[...] = acc[...].astype(BF16)

        @pl.when(j == n_j - 1)
        def _():
            xv = x_ref[...]
            r = lax.rsqrt(jnp.mean(xv * xv, axis=-1, keepdims=True) + EPS)
            n = xv * r
            dv = dxn[rows, :]
            dn = dv * g
            dx_ref[...] = dh_ref[...] + r * (dn - n * jnp.mean(dn * n, axis=-1, keepdims=True))
            st_ref[0:1, :] += jnp.sum(dv * n, axis=0, keepdims=True)

    lastrow = lambda j, i: (jnp.where(j == n_j - 1, i, 0), 0)
    return _call(
        body, comm, (dproj, x, dh1, g1, wt_in), grid=(n_j, n_i),
        in_specs=[pl.BlockSpec((tm, ck), lambda j, i: (i, j)),
                  pl.BlockSpec((tm, D), lambda j, i: (_edge_index(j, i, n_j, n_i), 0)),
                  pl.BlockSpec((tm, D), lastrow),
                  pl.BlockSpec((DEPTH, D), lambda j, i: (0, 0)),
                  pl.BlockSpec((ck, D), lambda j, i: (j, 0))],
        out_specs=[pl.BlockSpec((tm, D), lastrow), pl.BlockSpec((ck, D), lambda j, i: (j, 0)),
                   pl.BlockSpec((8, D), lambda j, i: (0, 0))],
        out_shape=[_sds((s, D), F32), _sds((IN_W, D), BF16), _sds((8, D), F32)],
        scratch_shapes=[pltpu.VMEM((s, D), F32), pltpu.VMEM((s, D), BF16), pltpu.VMEM((ck, D), F32)],
        name=f"bwd_proj{l}")


def bwd_proj_w(dproj, x, g1, l, half, comm=None):
    s = x.shape[0]
    tm = min(1024, s)
    ck = 1408
    c0, hw = W_IN_PARTS[half]
    n_j, n_i = IN_W // ck, s // tm

    def body(dp_ref, x_ref, g_ref, dw_ref, xn_b, acc):
        j, i = pl.program_id(0), pl.program_id(1)
        rows = pl.ds(pl.multiple_of(i * tm, tm), tm)

        @pl.when(j == 0)
        def _():
            xv = x_ref[...]
            r = lax.rsqrt(jnp.mean(xv * xv, axis=-1, keepdims=True) + EPS)
            xn_b[rows, :] = (xv * r * g_ref[l:l + 1, :])[:, c0:c0 + hw].astype(BF16)

        @pl.when(i == 0)
        def _():
            acc[...] = jnp.zeros((ck, hw), F32)

        acc[...] += _dot(dp_ref[...], xn_b[rows, :], _TN)

        @pl.when(i == n_i - 1)
        def _():
            dw_ref[...] = acc[...].astype(BF16)

    return _call(
        body, comm, (dproj, x, g1), grid=(n_j, n_i),
        in_specs=[pl.BlockSpec((tm, ck), lambda j, i: (i, j)),
                  pl.BlockSpec((tm, D), lambda j, i: (jnp.where(j == 0, i, n_i - 1), 0)),
                  pl.BlockSpec((DEPTH, D), lambda j, i: (0, 0))],
        out_specs=pl.BlockSpec((ck, hw), lambda j, i: (j, 0)),
        out_shape=_sds((IN_W, hw), BF16),
        scratch_shapes=[pltpu.VMEM((s, hw), BF16), pltpu.VMEM((ck, hw), F32)],
        name=f"bwd_proj_w{half}_{l}")


def bwd_proj_x(dproj, x, dh1, g1, wt_in, l, comm=None):
    s = x.shape[0]
    tm = min(512, s)
    ck = 1408
    n_j, n_i = IN_W // ck, s // tm

    def body(dp_ref, x_ref, dh_ref, g_ref, w_ref, dx_ref, st_ref, dxn):
        j, i = pl.program_id(0), pl.program_id(1)
        rows = pl.ds(pl.multiple_of(i * tm, tm), tm)
        g = g_ref[l:l + 1, :]

        @pl.when((j == 0) & (i == 0))
        def _():
            st_ref[...] = jnp.zeros((8, D), F32)

        part = _dot(dp_ref[...], w_ref[...], _NN)

        @pl.when(j == 0)
        def _():
            dxn[rows, :] = part

        @pl.when(j > 0)
        def _():
            dxn[rows, :] += part

        @pl.when(j == n_j - 1)
        def _():
            xv = x_ref[...]
            r = lax.rsqrt(jnp.mean(xv * xv, axis=-1, keepdims=True) + EPS)
            n = xv * r
            dv = dxn[rows, :]
            dn = dv * g
            dx_ref[...] = dh_ref[...] + r * (dn - n * jnp.mean(dn * n, axis=-1, keepdims=True))
            st_ref[0:1, :] += jnp.sum(dv * n, axis=0, keepdims=True)

    lastrow = lambda j, i: (jnp.where(j == n_j - 1, i, 0), 0)
    return _call(
        body, comm, (dproj, x, dh1, g1, wt_in), grid=(n_j, n_i),
        in_specs=[pl.BlockSpec((tm, ck), lambda j, i: (i, j)), pl.BlockSpec((tm, D), lastrow),
                  pl.BlockSpec((tm, D), lastrow),
                  pl.BlockSpec((DEPTH, D), lambda j, i: (0, 0)), pl.BlockSpec((ck, D), lambda j, i: (j, 0))],
        out_specs=[pl.BlockSpec((tm, D), lastrow), pl.BlockSpec((8, D), lambda j, i: (0, 0))],
        out_shape=[_sds((s, D), F32), _sds((8, D), F32)],
        scratch_shapes=[pltpu.VMEM((s, D), F32)], name=f"bwd_proj_x{l}")


def _block_diag(w):
    nl, nb, bw, _ = w.shape
    eye = jnp.eye(nb, dtype=w.dtype)
    return jnp.einsum("lhij,hk->lhikj", w, eye).reshape(nl, nb * bw, nb * bw).astype(BF16)


class NoOverlap:
    def __init__(self, big):
        self.big = big

    def weights(self, l):
        return self.big[l]

    def job(self, slot, l):
        return None

    def done(self, slot, l, results):
        pass

    def new_grads(self, group, l, grads):
        pass

    def new_small(self, l, arrays, head_stats):
        pass


def local_step(x, target, norm1_g, norm2_g, final_g, convw, vecs, lru_wx, lru_wa, plan):
    wx_bd, wa_bd = _block_diag(lru_wx), _block_diag(lru_wa)

    def run(fn, slot, l, *args):
        res, cres = fn(*args, l, comm=plan.job(slot, l))
        plan.done(slot, l, cres)
        return res

    saved = []
    for l in range(DEPTH):
        proj = run(fwd_proj, "fwd_proj", l, x, norm1_g, plan.weights(l)["in_t"])
        pre_abd, h, kept = run(fwd_branch, "fwd_branch", l, proj, convw, vecs, wx_bd, wa_bd)
        pre_c = run(fwd_attn, "fwd_attn", l, proj, vecs)
        w = plan.weights(l)
        y4, merged, h1 = run(fwd_merge, "fwd_merge", l, x, proj, pre_abd, pre_c, w["a_t"], w["b_t"], w["c_t"], w["d_t"], w["o"])
        w = plan.weights(l)
        x_out, fg, fu = run(fwd_ffn, "fwd_ffn", l, h1, norm2_g, w["gate_t"], w["up_t"], w["down"])
        saved.append((x, proj, pre_abd, h, kept, pre_c, y4, merged, h1, fg, fu))
        x = x_out
    dx, head_stats = loss_head(x, final_g.reshape(1, D), target)
    small = [None] * DEPTH
    for l in reversed(range(DEPTH)):
        x_in, proj, pre_abd, h, kept, pre_c, y4, merged, h1, fg, fu = saved[l]
        w = plan.weights(l)
        dh1, d_gate, d_up, d_down, st_ffn = run(bwd_ffn, "bwd_ffn", l, dx, h1, fg, fu, norm2_g, w["gate_t"], w["up_t"], w["down"])
        plan.new_grads("ffn", l, dict(gate_t=d_gate, up_t=d_up, down=d_down))
        dproj, dpre, d_o, d_a, d_b, d_c, d_d = run(
            bwd_merge, "bwd_merge", l, dh1, y4, proj, merged, pre_abd, pre_c, w["a_t"], w["b_t"], w["c_t"], w["d_t"], w["o"])
        plan.new_grads("out", l, dict(a_t=d_a, b_t=d_b, c_t=d_c, d_t=d_d, o=d_o))
        dq, dkc, dkp, st_attn = run(bwd_attn, "bwd_attn", l, proj, dpre, vecs)
        dproj, dcw, dvec, dwx, dwa = run(bwd_branch, "bwd_branch", l, proj, dproj, dpre, h, kept, dq, dkc, dkp, convw, vecs, wx_bd, wa_bd)
        if l > 0:
            dx, d_in, st_proj = run(bwd_proj, "bwd_proj", l, dproj, x_in, dh1, norm1_g, w["in_t"])
            plan.new_grads("in", l, dict(in_t=d_in))
        else:
            for half, name in enumerate(("in_a", "in_b")):
                d_half = run(functools.partial(bwd_proj_w, half=half), f"bwd_proj_w{half}", l, dproj, x_in, norm1_g)
                plan.new_grads(name, l, {name: d_half})
            dx, st_proj = run(bwd_proj_x, "bwd_proj_x", l, dproj, x_in, dh1, norm1_g, w["in_t"])
        small[l] = (st_proj, st_ffn, dvec, st_attn, dcw, dwx, dwa)
        plan.new_small(l, small[l], head_stats)
    return head_stats, dx, small


BIG = dict(in_t=("w_in", "view"), a_t=("w_a_out", "transpose"), b_t=("w_b_out", "transpose"), c_t=("w_c_out", "transpose"),
           d_t=("w_d_out", "transpose"), o=("w_o", "plain"), gate_t=("w_ffn_gate", "view"), up_t=("w_ffn_up", "view"),
           down=("w_ffn_down", "plain"))


def cast_transpose(ws, name):
    n = len(ws)
    nl, a, b = ws[0].shape
    ta = min(256, a)

    def body(*refs):
        for w_ref, o_ref in zip(refs[:n], refs[n:]):
            o_ref[...] = w_ref[...].T.astype(BF16)

    return pl.pallas_call(
        body, grid=(nl, a // ta),
        in_specs=[pl.BlockSpec((None, ta, b), lambda l, i: (l, i, 0))] * n,
        out_specs=[pl.BlockSpec((None, b, ta), lambda l, i: (l, 0, i))] * n,
        out_shape=[_sds((nl, b, a), BF16)] * n, compiler_params=_cparams(2), name=name)(*ws)


def add_partials(mine, recv, core, name):
    n = len(mine)

    def body(core_ref, *refs):
        del core_ref
        for a_ref, b_ref, o_ref in zip(refs[:n], refs[n:2 * n], refs[2 * n:]):
            o_ref[...] = (a_ref[...].astype(F32) + b_ref[...].astype(F32)).astype(BF16)

    return pl.pallas_call(
        body,
        grid_spec=pltpu.PrefetchScalarGridSpec(
            num_scalar_prefetch=1, grid=(4,),
            in_specs=[pl.BlockSpec((None, None) + a.shape[2:], lambda i, cr: (i, cr[0], 0, 0)) for a in mine]
            + [pl.BlockSpec((None,) + b.shape[1:], lambda i, cr: (i, 0, 0)) for b in recv],
            out_specs=[pl.BlockSpec((None,) + b.shape[1:], lambda i, cr: (i, 0, 0)) for b in recv]),
        out_shape=[_sds(b.shape, BF16) for b in recv], compiler_params=_cparams(1), name=name)(core, *mine, *recv)


def _adamw(w, g, m, v):
    m = ADAM_B1 * m + (1.0 - ADAM_B1) * g
    v = ADAM_B2 * v + (1.0 - ADAM_B2) * (g * g)
    m_hat = m / (1.0 - ADAM_B1 ** ADAM_STEP)
    v_hat = v / (1.0 - ADAM_B2 ** ADAM_STEP)
    delta = -ADAM_LR * (m_hat / (jnp.sqrt(v_hat) + ADAM_EPS) + ADAM_WD * w)
    return delta, m, v


def adamw_big(items, name, comm=None):
    n_tiles = 4
    n = len(items)
    nl = items[0][0].shape[1]

    def body(*refs):
        ins, outs = refs[:4 * n], refs[4 * n:]
        for k, (contrib, _, _, _, transposed) in enumerate(items):
            c_ref, w_ref, m_ref, v_ref = ins[4 * k:4 * k + 4]
            g = c_ref[0].astype(F32)
            for src in range(1, contrib.shape[0]):
                g = g + c_ref[src].astype(F32)
            if transposed:
                g = g.T
            delta, mn, vn = _adamw(w_ref[...], g, m_ref[...], v_ref[...])
            for o_ref, val in zip(outs[4 * k:4 * k + 4], (g, delta, mn, vn)):
                o_ref[...] = val

    in_specs, out_specs, out_shape, args = [], [], [], []
    for contrib, w, m, v, transposed in items:
        nsrc, _, rows, cols = contrib.shape
        ct = cols // n_tiles
        if transposed:
            wspec = pl.BlockSpec((None, ct, rows), lambda l, j: (l, j, 0))
        else:
            wspec = pl.BlockSpec((None, rows, ct), lambda l, j: (l, 0, j))
        in_specs += [pl.BlockSpec((nsrc, None, rows, ct), lambda l, j: (0, l, 0, j)), wspec, wspec, wspec]
        out_specs += [wspec] * 4
        out_shape += [_sds(w.shape, F32)] * 4
        args += [contrib, w, m, v]
    res, cres = _call(body, comm, tuple(args), grid=(nl, n_tiles), in_specs=in_specs, out_specs=out_specs,
                      out_shape=out_shape, name=name)
    return [res[4 * k:4 * k + 4] for k in range(n)], cres


VEC_NAMES = ("conv_a_b", "lru_bx", "lru_ba", "lru_lambda", "conv_d_b", "ln_d_g", "ln_d_b")
P_N1, P_N2, P_VEC, P_CONV, P_LRU = 0, 1, 2, 6, 6 + CW_ROWS
P_FINAL, P_LOSS, P_ROWS = P_LRU + HD, P_LRU + HD + 1, P_LRU + HD + 2
SMALL = ("norm1_g", "conv_a_w", "conv_a_b", "lru_wx", "lru_bx", "lru_wa", "lru_ba", "lru_lambda", "conv_b_w", "sinks",
         "conv_d_w", "conv_d_b", "ln_d_g", "ln_d_b", "norm2_g", "final_g")
VMEM_FULL = pl.BlockSpec(memory_space=pltpu.VMEM)


def _stack_vecs(p):
    rows = [p[n] for n in VEC_NAMES] + [jnp.pad(p["sinks"], ((0, 0), (0, BW - N_HEADS)))]
    return jnp.stack(rows, axis=1)


def _stack_convs(p):
    nl, _, ch = p["conv_a_w"].shape
    z = jnp.zeros((nl, 1, ch), F32)
    return jnp.concatenate([p["conv_a_w"], p["conv_b_w"], z, p["conv_d_w"], z], axis=1)


def _vec_place(r):
    return P_VEC + r // 2, (r % 2) * BW


def pack_small(arrays, head_stats, l):
    n = len(arrays)

    def body(*refs):
        st_proj, st_ffn, dvec, st_attn, dcw, dwx, dwa = refs[:n]
        pack = refs[-1]
        pack[...] = jnp.zeros((P_ROWS, D), F32)
        lane = lax.broadcasted_iota(jnp.int32, (HD, BW), 1)
        pack[P_N1:P_N1 + 1, :] = st_proj[0:1, :]
        pack[P_N2:P_N2 + 1, :] = st_ffn[0:1, :]
        for r in range(len(VEC_NAMES)):
            row, c0 = _vec_place(r)
            pack[row:row + 1, c0:c0 + BW] = dvec[r:r + 1, :]
        row, c0 = _vec_place(V_SINK)
        pack[row:row + 1, c0:c0 + 128] = st_attn[0:1, :]
        pack[P_CONV:P_CONV + CW_ROWS, 0:BW] = dcw[...]
        for mat, c0 in ((dwx, 0), (dwa, BW)):
            blocks = jnp.zeros((HD, BW), F32)
            for h in range(BW // HD):
                blocks = jnp.where((lane >= HD * h) & (lane < HD * (h + 1)), mat[HD * h:HD * (h + 1), :], blocks)
            pack[P_LRU:P_LRU + HD, c0:c0 + BW] = blocks
        if head_stats is not None:
            pack[P_FINAL:P_LOSS + 1, :] = refs[n][0:2, :]

    flat = list(arrays) + ([] if head_stats is None else [head_stats])
    return pl.pallas_call(body, out_shape=_sds((P_ROWS, D), F32), in_specs=[VMEM_FULL] * len(flat), out_specs=VMEM_FULL,
                          name=f"pack_small{l}", compiler_params=pltpu.CompilerParams(vmem_limit_bytes=VMEM_LIMIT))(*flat)


def adamw_small(gathered, me, w, m, v):
    ns = len(SMALL)

    def body(me_ref, *refs):
        c_refs, refs = refs[:DEPTH], refs[DEPTH:]
        w_refs, m_refs, v_refs = refs[:ns], refs[ns:2 * ns], refs[2 * ns:3 * ns]
        loss_ref, outs, gs = refs[3 * ns], refs[3 * ns + 1:3 * ns + 1 + 4 * ns], refs[-1]
        for l in range(DEPTH):
            gs[l] = c_refs[l][0]
            for dev in range(1, NDEV):
                gs[l] += c_refs[l][dev]
        loss_ref[...] = gs[DEPTH - 1, P_LOSS:P_LOSS + 1, 0:128]

        def update(name, sel, g):
            i = SMALL.index(name)
            delta, mn, vn = _adamw(w_refs[i][sel], g, m_refs[i][sel], v_refs[i][sel])
            for o_ref, val in zip(outs[4 * i:4 * i + 4], (g, delta, mn, vn)):
                o_ref[sel] = val

        update("final_g", (slice(0, 1), slice(None)), gs[DEPTH - 1, P_FINAL:P_FINAL + 1, :])
        shift = (BW - me_ref[0] * (BW // NDEV)) & (BW - 1)
        for l in range(DEPTH):
            row = (slice(l, l + 1), slice(None))
            update("norm1_g", row, gs[l, P_N1:P_N1 + 1, :])
            update("norm2_g", row, gs[l, P_N2:P_N2 + 1, :])
            for r, name in enumerate(VEC_NAMES):
                prow, c0 = _vec_place(r)
                update(name, row, gs[l, prow:prow + 1, c0:c0 + BW])
            prow, c0 = _vec_place(V_SINK)
            update("sinks", row, gs[l, prow:prow + 1, c0:c0 + N_HEADS])
            mine = pltpu.roll(gs[l, P_CONV:P_CONV + CW_ROWS, 0:BW], shift, 1)[:, 0:BW // NDEV]
            update("conv_a_w", (l,), mine[CW_A:CW_A + CONV_A])
            update("conv_b_w", (l,), mine[CW_B:CW_B + CONV_B])
            update("conv_d_w", (l,), mine[CW_D:CW_D + CONV_D])
            for h in range(BW // HD):
                update("lru_wx", (l, h), gs[l, P_LRU:P_LRU + HD, HD * h:HD * (h + 1)])
                update("lru_wa", (l, h), gs[l, P_LRU:P_LRU + HD, BW + HD * h:BW + HD * (h + 1)])

    args = [p[n] for p in (w, m, v) for n in SMALL]
    full = lambda a: pl.BlockSpec(a.shape, lambda i, me_ref: (0,) * a.ndim)
    out_shape = [_sds((1, 128), F32)] + [_sds(w[n].shape, F32) for n in SMALL for _ in range(4)]
    outs = pl.pallas_call(
        body,
        grid_spec=pltpu.PrefetchScalarGridSpec(
            num_scalar_prefetch=1, grid=(1,),
            in_specs=[full(a) for a in list(gathered) + args], out_specs=[full(o) for o in out_shape],
            scratch_shapes=[pltpu.VMEM((DEPTH, P_ROWS, D), F32)]),
        out_shape=out_shape, name="adamw_small", compiler_params=_cparams(1))(me, *gathered, *args)
    return outs[0], {n: outs[1 + 4 * i:5 + 4 * i] for i, n in enumerate(SMALL)}


def merge_jobs(jobs):
    jobs = [j for j in jobs if j is not None]
    if not jobs:
        return None, []
    inputs, aliases, outs, sems, cuts = [], {}, [], [], []
    for j in jobs:
        i0, o0, s0 = len(inputs), len(outs), len(sems)
        aliases.update({i0 + i: o0 + o for i, o in j.aliases.items()})
        inputs += j.inputs
        outs += j.out_shapes
        sems += j.sem_shapes
        cuts.append((i0, len(inputs), o0, len(outs), s0, len(sems)))

    def each(which):
        def go(cins, couts, s):
            for j, (i0, i1, o0, o1, s0, s1) in zip(jobs, cuts):
                if getattr(j, which) is not None:
                    getattr(j, which)(cins[i0:i1], couts[o0:o1], s[s0:s1])
        return go

    relay = each("relay") if any(j.relay is not None for j in jobs) else None
    return CommJob(inputs, aliases, outs, sems, each("start"), each("finish"), relay), [(c[2], c[3]) for c in cuts]


SIXTHS = 6
OUT_KINDS = ("a_t", "b_t", "c_t", "d_t", "o")
GATHER_PLAN = {
    "fwd_proj": [(k, 0, 0, 6) for k in OUT_KINDS] + [("gate_t", 0, 0, 6)],
    "fwd_branch": [("up_t", 0, 0, 6)],
    "fwd_attn": [("down", 0, 0, 6)],
    "fwd_merge": [("in_t", 1, 0, 2)],
    "fwd_ffn": [("in_t", 1, 2, 6)],
}
SIBLING_PLAN = {"bwd_merge": ("ffn", 0), "bwd_branch": ("out", 0), "bwd_ffn": ("in", 1),
                "bwd_proj_w1": ("in_a", 0), "bwd_proj_x": ("in_b", 0)}
GROUPS = dict(ffn=("gate_t", "up_t", "down"), out=OUT_KINDS, in_a=("in_a",), in_b=("in_b",))
GROUPS["in"] = ("in_t",)
COLUMN_HALF = dict(in_a=("in_t", W_IN_PARTS[0][0]), in_b=("in_t", W_IN_PARTS[1][0]))
CHIP_PLAN = {
    "bwd_attn": [("in_t", 1, 3, 5)],
    "bwd_branch": [("in_t", 1, 5, 6), ("gate_t", 0, 0, 6), ("up_t", 0, 0, 6), ("down", 0, 0, 3)],
    "bwd_proj": [(k, 0, 0, 6) for k in OUT_KINDS] + [("down", 0, 3, 6)],
    "bwd_proj_w0": [(k, 0, 0, 6) for k in OUT_KINDS[:3]] + [("down", 0, 3, 6)],
    "bwd_proj_w1": [(k, 0, 0, 6) for k in OUT_KINDS[3:]],
    "bwd_merge": [("in_t", 1, 0, 3)],
    "bwd_proj_x": [("in_a", 0, 0, 5)],
    "adamw_rest": [("in_a", 0, 5, 6), ("in_b", 0, 0, 6)],
}
SMALL_GATHER_PLAN = {"bwd_ffn": 1, "adamw_rest": 0}


class Overlap:
    def __init__(self, shards, core):
        self.shards = shards
        self.core = core
        self.gathered = [dict.fromkeys(BIG) for _ in range(DEPTH)]
        self.views = {}
        self.partial = {}
        self.contrib = dict.fromkeys(BIG)
        self.small_packs = [None] * DEPTH
        self.small_gathered = [None] * DEPTH
        self._open = None

    def weights(self, l):
        return self.gathered[l]

    def new_grads(self, group, l, grads):
        for k, g in grads.items():
            self.views[k, l] = g.reshape(4, 2, g.shape[0] // NDEV, g.shape[1])

    def new_small(self, l, arrays, head_stats):
        self.small_packs[l] = pack_small(arrays, head_stats if l == DEPTH - 1 else None, l)

    @staticmethod
    def _rows(shard_rows, f0, f1):
        return shard_rows * f0 // SIXTHS, shard_rows * (f1 - f0) // SIXTHS

    def job(self, slot, l):
        jobs, notes = [], []
        pieces = [(k, l + dl, f0, f1) for k, dl, f0, f1 in GATHER_PLAN.get(slot, []) if l + dl < DEPTH]
        if pieces:
            jobs.append(gather_job([((k, ll), self.shards[ll][k], self.gathered[ll][k],
                                     *self._rows(self.shards[ll][k].shape[0], f0, f1)) for k, ll, f0, f1 in pieces]))
            notes.append(("gather", list(dict.fromkeys((k, ll) for k, ll, _, _ in pieces))))
        if slot in SIBLING_PLAN and l + SIBLING_PLAN[slot][1] < DEPTH:
            group, dl = SIBLING_PLAN[slot]
            keys = [(k, l + dl) for k in GROUPS[group]]
            jobs.append(sibling_exchange_job([self.views[key] for key in keys]))
            notes.append(("sibling", keys))
        pieces = [(k, l + dl, f0, f1) for k, dl, f0, f1 in CHIP_PLAN.get(slot, []) if l + dl < DEPTH]
        if pieces:
            whole = [(*COLUMN_HALF.get(k, (k, 0)), k, ll, f0, f1) for k, ll, f0, f1 in pieces]
            jobs.append(chip_exchange_job([(self.partial[k, ll], self.contrib[kind], kind, ll,
                                            *self._rows(self.partial[k, ll].shape[1], f0, f1), col0, self.shards[ll][kind].shape[1])
                                           for kind, col0, k, ll, f0, f1 in whole]))
            notes.append(("chips", list(dict.fromkeys(kind for kind, *_ in whole))))
        if slot in SMALL_GATHER_PLAN and l + SMALL_GATHER_PLAN[slot] < DEPTH:
            ll = l + SMALL_GATHER_PLAN[slot]
            jobs.append(gather_job([("small", self.small_packs[ll], None, 0, P_ROWS)]))
            notes.append(("small", ll))
        job, spans = merge_jobs(jobs)
        self._open = (slot, l, notes, spans)
        return job

    def done(self, slot, l, results):
        open_slot, open_l, notes, spans = self._open
        assert (open_slot, open_l) == (slot, l)
        for (what, keys), (r0, r1) in zip(notes, spans):
            res = results[r0:r1]
            if what == "gather":
                for (k, ll), g in zip(keys, res):
                    self.gathered[ll][k] = g
            elif what == "sibling":
                sums = add_partials([self.views[key] for key in keys], list(res), self.core, f"chip_sum_{keys[0][0]}{keys[0][1]}")
                self.partial.update(zip(keys, sums))
            elif what == "chips":
                for k, c in zip(keys, res):
                    self.contrib[k] = c
            else:
                self.small_gathered[keys], = res


SMALL = ("norm1_g", "conv_a_w", "conv_a_b", "lru_wx", "lru_bx", "lru_wa", "lru_ba", "lru_lambda", "conv_b_w", "sinks",
         "conv_d_w", "conv_d_b", "ln_d_g", "ln_d_b", "norm2_g", "final_g")
WEIGHTS = ("norm1_g", "w_in", "conv_a_w", "conv_a_b", "lru_wx", "lru_bx", "lru_wa", "lru_ba", "lru_lambda", "w_a_out",
           "conv_b_w", "w_b_out", "sinks", "w_c_out", "conv_d_w", "conv_d_b", "ln_d_g", "ln_d_b", "w_d_out", "w_o",
           "norm2_g", "w_ffn_gate", "w_ffn_up", "w_ffn_down", "final_g")


def kernel(x, norm1_g, w_in, conv_a_w, conv_a_b, lru_wx, lru_bx, lru_wa, lru_ba, lru_lambda, w_a_out, conv_b_w, w_b_out, sinks, w_c_out, conv_d_w, conv_d_b, ln_d_g, ln_d_b, w_d_out, w_o, norm2_g, w_ffn_gate, w_ffn_up, w_ffn_down, final_g, loss_target, m_norm1_g, m_w_in, m_conv_a_w, m_conv_a_b, m_lru_wx, m_lru_bx, m_lru_wa, m_lru_ba, m_lru_lambda, m_w_a_out, m_conv_b_w, m_w_b_out, m_sinks, m_w_c_out, m_conv_d_w, m_conv_d_b, m_ln_d_g, m_ln_d_b, m_w_d_out, m_w_o, m_norm2_g, m_w_ffn_gate, m_w_ffn_up, m_w_ffn_down, m_final_g, v_norm1_g, v_w_in, v_conv_a_w, v_conv_a_b, v_lru_wx, v_lru_bx, v_lru_wa, v_lru_ba, v_lru_lambda, v_w_a_out, v_conv_b_w, v_w_b_out, v_sinks, v_w_c_out, v_conv_d_w, v_conv_d_b, v_ln_d_g, v_ln_d_b, v_w_d_out, v_w_o, v_norm2_g, v_w_ffn_gate, v_w_ffn_up, v_w_ffn_down, v_final_g):
    args = dict(locals())
    w = {n: args[n] for n in WEIGHTS}
    m = {n: args["m_" + n] for n in WEIGHTS}
    v = {n: args["v_" + n] for n in WEIGHTS}
    me = _dev_index(*_mesh_pos())

    def rows_major(a, how):
        return jnp.swapaxes(a, 1, 2) if how == "view" else a

    stacked = {k: rows_major(w[n], how).astype(BF16) for k, (n, how) in BIG.items() if how != "transpose"}
    turned = [k for k, (n, how) in BIG.items() if how == "transpose"]
    stacked.update(zip(turned, cast_transpose([w[BIG[k][0]] for k in turned], "prep_transposed")))
    plan = Overlap([{k: stacked[k][l] for k in BIG} for l in range(DEPTH)], lax.axis_index("c").astype(jnp.int32).reshape(1))
    convs = jnp.pad(_stack_convs(w).reshape(DEPTH * CW_ROWS, BW // NDEV), ((0, 0), (0, 256 - BW // NDEV)))
    g_in0, g_conv = _comm_only(gather_job([(("in_t", 0), plan.shards[0]["in_t"], None, 0, plan.shards[0]["in_t"].shape[0]),
                                           ("convs", convs, None, 0, convs.shape[0])]), "gather_first")
    plan.gathered[0]["in_t"] = g_in0
    convw = g_conv[:, :BW // NDEV].reshape(NDEV, DEPTH, CW_ROWS, BW // NDEV).transpose(1, 2, 0, 3).reshape(DEPTH, CW_ROWS, BW)

    vecs = _stack_vecs(w)
    head_stats, grad_x, grads = local_step(x[0], loss_target[0], norm1_g, norm2_g, final_g, convw, vecs, lru_wx, lru_wa, plan)


    out = {}
    for slot, kinds in (("adamw_rest", [k for k in BIG if k != "in_t"]), ("adamw_in_t", ["in_t"])):
        job = plan.job(slot, 0)
        items = [(plan.contrib[k], *[rows_major(p[BIG[k][0]], BIG[k][1]) for p in (w, m, v)], BIG[k][1] == "transpose")
                 for k in kinds]
        results, cres = adamw_big(items, slot, comm=job)
        plan.done(slot, 0, cres)
        for k, res in zip(kinds, results):
            out[BIG[k][0]] = [rows_major(r, BIG[k][1]) for r in res]

    def own_shapes(p):
        return {n: p[n].reshape(1, D) if n == "final_g" else p[n] for n in SMALL}

    loss, small = adamw_small([g.reshape(NDEV, P_ROWS, D) for g in plan.small_gathered], me.astype(jnp.int32).reshape(1),
                              own_shapes(w), own_shapes(m), own_shapes(v))
    for n in SMALL:
        out[n] = [r.reshape(w[n].shape) for r in small[n]]
    loss = loss[0, 0]
    return (loss, grad_x[None], *[out[n][0] for n in WEIGHTS], *[out[n][1] for n in WEIGHTS],
            *[out[n][2] for n in WEIGHTS], *[out[n][3] for n in WEIGHTS])
```

```python
import functools

import jax
import jax.numpy as jnp
from jax import lax
from jax.experimental import pallas as pl
from jax.experimental.pallas import tpu as pltpu

F32 = jnp.float32
BF16 = jnp.bfloat16
E = pl.Element

D = 1024
BW = 512
IN_W = 8448
GL0 = 4352
FF = 2816
N_HEADS = 8
N_KV = 2
HD = 64
ATT_BLK = 128
EPS = 1e-6
LRU_C = 8.0
NEG_INF = -1e30
DEPTH = 2
NDEV = 8
CONV_A, CONV_B, CONV_D = 4, 3, 31
C_AX, C_AG, C_BV, C_BC, C_BB, C_Q, C_K, C_V, C_D1, C_D2 = 0, 512, 1024, 1536, 2048, 2560, 3072, 3200, 3328, 3840
CW_A, CW_B, CW_D, CW_ROWS = 0, 4, 8, 40
V_CAB, V_BX, V_BA, V_LAM, V_CDB, V_LNG, V_LNB, V_SINK, V_ROWS = 0, 1, 2, 3, 4, 5, 6, 7, 8
HALO = 32
W_IN_PARTS = ((0, 768), (768, 256))

ADAM_LR, ADAM_B1, ADAM_B2, ADAM_EPS, ADAM_WD, ADAM_STEP = 0.001, 0.9, 0.999, 1e-08, 0.01, 10

VMEM_LIMIT = 56 * 1024 * 1024

_NN = (((1,), (0,)), ((), ()))
_NT = (((1,), (1,)), ((), ()))
_TN = (((0,), (0,)), ((), ()))


def _dot(a, b, dims):
    return lax.dot_general(a.astype(BF16), b.astype(BF16), dims, preferred_element_type=F32)


def _cparams(n_axes):
    return pltpu.CompilerParams(dimension_semantics=("arbitrary",) * n_axes, vmem_limit_bytes=VMEM_LIMIT)


def _sds(shape, dtype):
    return jax.ShapeDtypeStruct(tuple(shape), dtype)


def _sigmoid(x):
    return jax.nn.sigmoid(x)


def _neg_expm1(x):
    p = x * (1.0 + x * (0.5 + x * (1.0 / 6.0 + x * (1.0 / 24.0 + x * (1.0 / 120.0)))))
    return jnp.where(x > -0.1, -p, 1.0 - jnp.exp(x))


def _softplus(z):
    return jnp.maximum(z, 0.0) + jnp.log1p(jnp.exp(-jnp.abs(z)))


def _gelu_and_grad(x):
    c = 0.7978845608028654
    inner = c * (x + 0.044715 * x * x * x)
    t = jnp.tanh(inner)
    g = 0.5 * x * (1.0 + t)
    dg = 0.5 * (1.0 + t) + 0.5 * x * (1.0 - t * t) * c * (1.0 + 3.0 * 0.044715 * x * x)
    return g, dg


ANY = pl.BlockSpec(memory_space=pl.ANY)
MESH = pl.DeviceIdType.MESH


def _mesh_pos():
    return lax.axis_index("x"), lax.axis_index("y"), lax.axis_index("c")


def _dev_index(px, py, pc):
    return 4 * px + 2 * py + pc


class CommJob:
    def __init__(self, inputs, aliases, out_shapes, sem_shapes, start, finish, relay=None):
        self.inputs, self.aliases, self.out_shapes, self.sem_shapes = list(inputs), dict(aliases), list(out_shapes), list(sem_shapes)
        self.start, self.finish, self.relay = start, finish, relay


def _call(body, comm, args, *, grid, in_specs, out_specs, out_shape, scratch_shapes=(), name, aliases=None):
    single = not isinstance(out_shape, (list, tuple))
    out_specs = [out_specs] if single else list(out_specs)
    out_shape = [out_shape] if single else list(out_shape)
    scratch_shapes = list(scratch_shapes)
    n_in, n_out, n_scr, n_axes = len(in_specs), len(out_shape), len(scratch_shapes), len(grid)
    params = pltpu.CompilerParams(dimension_semantics=("arbitrary",) * n_axes, vmem_limit_bytes=VMEM_LIMIT)
    io_aliases = dict(aliases or {})
    if comm is None:
        outs = pl.pallas_call(body, grid=grid, in_specs=in_specs, out_specs=out_specs, out_shape=out_shape,
                              scratch_shapes=scratch_shapes, input_output_aliases=io_aliases, compiler_params=params,
                              name=name)(*args)
        return (outs[0] if single else outs), []
    c_in, c_out = len(comm.inputs), len(comm.out_shapes)
    io_aliases.update({n_in + i: n_out + o for i, o in comm.aliases.items()})

    def wrapped(*refs):
        ins, cins = refs[:n_in], refs[n_in:n_in + c_in]
        outs = refs[n_in + c_in:n_in + c_in + n_out]
        couts = refs[n_in + c_in + n_out:n_in + c_in + n_out + c_out]
        rest = refs[n_in + c_in + n_out + c_out:]
        scr, sems = rest[:n_scr], rest[n_scr:]
        first = functools.reduce(lambda a, b: a & b, [pl.program_id(a) == 0 for a in range(n_axes)])
        last = functools.reduce(lambda a, b: a & b, [pl.program_id(a) == pl.num_programs(a) - 1 for a in range(n_axes)])

        @pl.when(first)
        def _():
            comm.start(cins, couts, sems)

        if comm.relay is not None:
            step = functools.reduce(lambda a, b: a * grid[b] + pl.program_id(b), range(1, n_axes), pl.program_id(0))
            n_steps = functools.reduce(lambda a, b: a * b, grid)

            @pl.when(step == 2 * n_steps // 3)
            def _():
                comm.relay(cins, couts, sems)

        body(*ins, *outs, *scr)

        @pl.when(last)
        def _():
            comm.finish(cins, couts, sems)

    outs = pl.pallas_call(
        wrapped, grid=grid, in_specs=list(in_specs) + [ANY] * c_in, out_specs=out_specs + [ANY] * c_out,
        out_shape=out_shape + comm.out_shapes, scratch_shapes=scratch_shapes + comm.sem_shapes,
        input_output_aliases=io_aliases, compiler_params=params, name=name)(*args, *comm.inputs)
    res, cres = outs[:n_out], outs[n_out:]
    return (res[0] if single else res), cres


def _comm_only(comm, name):
    c_in, c_out = len(comm.inputs), len(comm.out_shapes)

    def body(*refs):
        cins, couts, sems = refs[:c_in], refs[c_in:c_in + c_out], refs[c_in + c_out:]
        comm.start(cins, couts, sems)
        if comm.relay is not None:
            comm.relay(cins, couts, sems)
        comm.finish(cins, couts, sems)

    return pl.pallas_call(body, in_specs=[ANY] * c_in, out_specs=[ANY] * c_out, out_shape=comm.out_shapes,
                          scratch_shapes=comm.sem_shapes, input_output_aliases=comm.aliases, name=name)(*comm.inputs)


def gather_job(pieces):
    inputs, aliases, out_shapes, plan, where = [], {}, [], [], {}
    for key, shard, gathered, row0, nrows in pieces:
        if key not in where:
            where[key] = (len(inputs), len(out_shapes))
            inputs.append(shard)
            if gathered is not None:
                aliases[len(inputs)] = len(out_shapes)
                inputs.append(gathered)
            out_shapes.append(_sds((NDEV * shard.shape[0], shard.shape[1]), shard.dtype))
        plan.append((*where[key], shard.shape[0], row0, nrows))
    n = len(plan)

    def copies(cins, couts, sems):
        send_sems, recv_sems, local_sems = sems
        x, y, c = _mesh_pos()
        me, sibling = (x, y, c), (x, y, 1 - c)
        xn, yn, dg = (1 - x, y), (x, 1 - y), (1 - x, 1 - y)
        local, first, pass1, pass2, got_ici, got_fwd, got_d2d = [], [], [], [], [], [], []
        for p, (i_shard, i_out, rows, row0, nrows) in enumerate(plan):
            src = cins[i_shard].at[pl.ds(row0, nrows), :]
            half = cins[i_shard].shape[1] // 2
            left, right, whole = pl.ds(0, half), pl.ds(half, half), slice(None)

            def slot(dev, lanes, i_out=i_out, rows=rows, row0=row0, nrows=nrows):
                return couts[i_out].at[pl.ds(_dev_index(*dev) * rows + row0, nrows), lanes]

            def copy(g, dev, to, lanes=whole, src=None, p=p, slot=slot):
                return pltpu.make_async_remote_copy(
                    src_ref=slot(dev, lanes) if src is None else src, dst_ref=slot(dev, lanes),
                    send_sem=send_sems.at[g, p], recv_sem=recv_sems.at[g, p], device_id=to, device_id_type=MESH)

            local.append(pltpu.make_async_copy(src, slot(me, whole), local_sems.at[p]))
            first += [copy(0, me, sibling, src=src), copy(1, me, (*xn, c), src=src), copy(2, me, (*yn, c), src=src)]
            got_ici += [copy(1, (*xn, c), me), copy(2, (*yn, c), me)]
            pass1 += [copy(3, (*xn, c), (*yn, c), left), copy(4, (*yn, c), (*xn, c), right),
                      copy(5, (*xn, c), sibling), copy(6, (*yn, c), sibling)]
            got_fwd += [copy(3, (*dg, c), me, left), copy(4, (*dg, c), me, right)]
            pass2 += [copy(7, (*dg, c), sibling, left), copy(8, (*dg, c), sibling, right)]
            got_d2d += [copy(0, sibling, me), copy(5, (*xn, 1 - c), me), copy(6, (*yn, 1 - c), me),
                        copy(7, (*dg, 1 - c), me, left), copy(8, (*dg, 1 - c), me, right)]
        return local, first, pass1, pass2, got_ici, got_fwd, got_d2d

    def start(cins, couts, sems):
        local, first, *_ = copies(cins, couts, sems)
        for cp in local + first:
            cp.start()

    def pass_on(cins, couts, sems):
        _, _, pass1, _, got_ici, _, _ = copies(cins, couts, sems)
        for cp in got_ici:
            cp.wait_recv()
        for cp in pass1:
            cp.start()

    def finish(cins, couts, sems):
        local, first, pass1, pass2, _, got_fwd, got_d2d = copies(cins, couts, sems)
        for cp in got_fwd:
            cp.wait_recv()
        for cp in pass2:
            cp.start()
        for cp in got_d2d:
            cp.wait_recv()
        for cp in first + pass1 + pass2:
            cp.wait_send()
        for cp in local:
            cp.wait()

    sem_shapes = [pltpu.SemaphoreType.DMA((9, n)), pltpu.SemaphoreType.DMA((9, n)), pltpu.SemaphoreType.DMA((n,))]
    return CommJob(inputs, aliases, out_shapes, sem_shapes, start, finish, relay=pass_on)


def sibling_exchange_job(grads):
    n = len(grads)

    def copies(cins, couts, sems):
        send_sems, recv_sems = sems
        x, y, c = _mesh_pos()
        return [pltpu.make_async_remote_copy(
            src_ref=cins[q].at[:, 1 - c], dst_ref=couts[q], send_sem=send_sems.at[q], recv_sem=recv_sems.at[q],
            device_id=(x, y, 1 - c), device_id_type=MESH) for q in range(n)]

    def start(cins, couts, sems):
        for cp in copies(cins, couts, sems):
            cp.start()

    def finish(cins, couts, sems):
        cps = copies(cins, couts, sems)
        for cp in cps:
            cp.wait_recv()
        for cp in cps:
            cp.wait_send()

    return CommJob(grads, {}, [_sds((4,) + g.shape[2:], g.dtype) for g in grads],
                   [pltpu.SemaphoreType.DMA((n,)), pltpu.SemaphoreType.DMA((n,))], start, finish)


def chip_exchange_job(pieces):
    inputs, aliases, out_shapes, plan, where = [], {}, [], [], {}
    for partial, contrib, key, layer, row0, nrows, col0, cols in pieces:
        if key not in where:
            where[key] = len(out_shapes)
            out_shapes.append(_sds((4, DEPTH, partial.shape[1], cols), partial.dtype))
            if contrib is not None:
                aliases[len(inputs)] = where[key]
                inputs.append(contrib)
        plan.append((len(inputs), where[key], layer, row0, nrows, col0, partial.shape[2]))
        inputs.append(partial)
    n = len(plan)

    def copies(cins, couts, sems):
        send_sems, recv_sems, local_sems = sems
        x, y, c = _mesh_pos()
        mine = 2 * x + y
        local, sends, recvs = [], [], []
        for p, (i_in, i_out, layer, row0, nrows, col0, ncols) in enumerate(plan):
            rows, lanes = pl.ds(row0, nrows), pl.ds(col0, ncols)
            local.append(pltpu.make_async_copy(cins[i_in].at[mine, rows, :], couts[i_out].at[mine, layer, rows, lanes],
                                               local_sems.at[p]))
            for j, (cx, cy) in enumerate([(1 - x, y), (x, 1 - y), (1 - x, 1 - y)]):
                theirs = 2 * cx + cy

                def copy(slot_there, j=j, p=p, cx=cx, cy=cy, theirs=theirs, i_in=i_in, i_out=i_out, layer=layer,
                         rows=rows, lanes=lanes):
                    return pltpu.make_async_remote_copy(
                        src_ref=cins[i_in].at[theirs, rows, :], dst_ref=couts[i_out].at[slot_there, layer, rows, lanes],
                        send_sem=send_sems.at[j, p], recv_sem=recv_sems.at[j, p], device_id=(cx, cy, c), device_id_type=MESH)
                sends.append(copy(mine))
                recvs.append(copy(theirs))
        return local, sends, recvs

    def start(cins, couts, sems):
        local, sends, _ = copies(cins, couts, sems)
        for cp in local + sends:
            cp.start()

    def finish(cins, couts, sems):
        local, sends, recvs = copies(cins, couts, sems)
        for cp in recvs:
            cp.wait_recv()
        for cp in sends:
            cp.wait_send()
        for cp in local:
            cp.wait()

    sem_shapes = [pltpu.SemaphoreType.DMA((3, n)), pltpu.SemaphoreType.DMA((3, n)), pltpu.SemaphoreType.DMA((n,))]
    return CommJob(inputs, aliases, out_shapes, sem_shapes, start, finish)


def fwd_proj(x, g1, wt_in, l, comm=None):
    s = x.shape[0]
    tm = min(512, s)
    tn = 1408

    def body(x_ref, g_ref, w_ref, o_ref, xn_ref):
        @pl.when(pl.program_id(1) == 0)
        def _():
            xv = x_ref[...]
            r = lax.rsqrt(jnp.mean(xv * xv, axis=-1, keepdims=True) + EPS)
            xn_ref[...] = (xv * r * g_ref[l:l + 1, :]).astype(BF16)

        o_ref[...] = _dot(xn_ref[...], w_ref[...], _NT).astype(BF16)

    return _call(
        body, comm, (x, g1, wt_in), grid=(s // tm, IN_W // tn),
        in_specs=[pl.BlockSpec((tm, D), lambda i, j: (i, 0)),
                  pl.BlockSpec((DEPTH, D), lambda i, j: (0, 0)),
                  pl.BlockSpec((tn, D), lambda i, j: (j, 0))],
        out_specs=pl.BlockSpec((tm, tn), lambda i, j: (i, j)),
        out_shape=_sds((s, IN_W), BF16),
        scratch_shapes=[pltpu.VMEM((tm, D), BF16)], name=f"fwd_proj{l}")


def _scan_fwd(a_ref, u_ref, h_ref, h0, n_rows):
    row = lax.broadcasted_iota(jnp.int32, (8, BW), 0)

    def body(g, hprev):
        r = pl.multiple_of(g * 8, 8)
        a = a_ref[pl.ds(r, 8), :]
        u = u_ref[pl.ds(r, 8), :]
        for sft in (1, 2, 4):
            a_sh = jnp.where(row >= sft, pltpu.roll(a, sft, 0), 1.0)
            u_sh = jnp.where(row >= sft, pltpu.roll(u, sft, 0), 0.0)
            u = u + a * u_sh
            a = a * a_sh
        h = u + a * hprev
        h_ref[pl.ds(r, 8), :] = h
        return h[7:8, :]

    return lax.fori_loop(0, n_rows // 8, body, h0)


def _scan_bwd(b_ref, g_ref, o_ref, c0, n_rows):
    row = lax.broadcasted_iota(jnp.int32, (8, BW), 0)

    def body(k, cnext):
        r = pl.multiple_of((n_rows // 8 - 1 - k) * 8, 8)
        b = b_ref[pl.ds(r, 8), :]
        g = g_ref[pl.ds(r, 8), :]
        for sft in (1, 2, 4):
            b_sh = jnp.where(row < 8 - sft, pltpu.roll(b, 8 - sft, 0), 1.0)
            g_sh = jnp.where(row < 8 - sft, pltpu.roll(g, 8 - sft, 0), 0.0)
            g = g + b * g_sh
            b = b * b_sh
        o = g + b * cnext
        o_ref[pl.ds(r, 8), :] = o
        return o[0:1, :]

    return lax.fori_loop(0, n_rows // 8, body, c0)


def _shifted_copies(buf, shifted, n_rows):
    for r in range(1, 8):
        shifted[r - 1, 0:n_rows - 8, :] = buf[pl.ds(r, n_rows - 8), :]


def _window(buf, shifted, off, t):
    r = off % 8
    return buf[pl.ds(off, t), :] if r == 0 else shifted[r - 1, pl.ds(off - r, t), :]


def _branch_fwd_math(cur_ref, halo_ref, cw_ref, vec_ref, wx_ref, wa_ref, bufa, bufb, bufd, xd, first, t, saved_ref=None):
    def halo(c0):
        v = halo_ref[:, c0:c0 + BW].astype(F32)
        return jnp.where(first, 0.0, v)

    def cur(c0):
        return cur_ref[:, c0:c0 + BW].astype(F32)

    out = {}
    bufa[0:HALO, :] = halo(C_AX)
    bufa[HALO:HALO + t, :] = cur(C_AX)
    ca = jnp.zeros((t, BW), F32) + vec_ref[V_CAB:V_CAB + 1, :]
    for k in range(CONV_A):
        ca = ca + cw_ref[CW_A + k:CW_A + k + 1, :] * bufa[pl.ds(HALO - (CONV_A - 1) + k, t), :]
    if saved_ref is None:
        gi = _sigmoid(_dot(ca, wx_ref[...], _NN) + vec_ref[V_BX:V_BX + 1, :])
        gr = _sigmoid(_dot(ca, wa_ref[...], _NN) + vec_ref[V_BA:V_BA + 1, :])
    else:
        gi, gr = saved_ref[:, BW:2 * BW], saved_ref[:, 2 * BW:3 * BW]
    sp = _softplus(-vec_ref[V_LAM:V_LAM + 1, :])
    la = -LRU_C * sp * gr
    a = jnp.exp(la)
    mult = jnp.sqrt(_neg_expm1(2.0 * la))
    out.update(ca=ca, gi=gi, gr=gr, sp=sp, a=a, mult=mult)
    bufb[0:HALO, :] = halo(C_BC) * halo(C_BV)
    bufb[HALO:HALO + t, :] = cur(C_BC) * cur(C_BV)
    cb = jnp.zeros((t, BW), F32)
    for k in range(CONV_B):
        cb = cb + cw_ref[CW_B + k:CW_B + k + 1, :] * bufb[pl.ds(HALO - (CONV_B - 1) + k, t), :]
    out.update(cb=cb)
    bufd[0:HALO, :] = halo(C_D1) * _sigmoid(halo(C_D2))
    s2 = _sigmoid(cur(C_D2))
    bufd[HALO:HALO + t, :] = cur(C_D1) * s2
    _shifted_copies(bufd, xd, t + HALO)
    if saved_ref is None:
        cd = jnp.zeros((t, BW), F32) + vec_ref[V_CDB:V_CDB + 1, :]
        for k in range(CONV_D):
            cd = cd + cw_ref[CW_D + k:CW_D + k + 1, :] * _window(bufd, xd, HALO - (CONV_D - 1) + k, t)
    else:
        cd = saved_ref[:, 0:BW]
    mu = jnp.mean(cd, axis=-1, keepdims=True)
    xc = cd - mu
    rstd = lax.rsqrt(jnp.mean(xc * xc, axis=-1, keepdims=True) + EPS)
    xh = xc * rstd
    ln = xh * vec_ref[V_LNG:V_LNG + 1, :] + vec_ref[V_LNB:V_LNB + 1, :]
    out.update(s2=s2, xh=xh, rstd=rstd, ln=ln, cd=cd)
    return out


def fwd_branch(proj, convw, vecs, wx_bd, wa_bd, l, comm=None):
    s = proj.shape[0]
    t = min(256, s)

    def body(cur_ref, halo_ref, cw_ref, vec_ref, wx_ref, wa_ref, pre_ref, h_ref, sv_ref, bufa, bufb, bufd, xd, a_s, u_s, hcar):
        first = pl.program_id(0) == 0

        @pl.when(first)
        def _():
            hcar[...] = jnp.zeros((1, BW), F32)

        v = _branch_fwd_math(cur_ref, halo_ref, cw_ref, vec_ref, wx_ref, wa_ref, bufa, bufb, bufd, xd, first, t)
        a_s[...] = v["a"]
        u_s[...] = v["ca"] * v["gi"] * v["mult"]
        sv_ref[:, 0:BW] = v["cd"]
        sv_ref[:, BW:2 * BW] = v["gi"]
        sv_ref[:, 2 * BW:3 * BW] = v["gr"]
        hcar[...] = _scan_fwd(a_s, u_s, h_ref, hcar[...], t)
        gg, _ = _gelu_and_grad(cur_ref[:, C_AG:C_AG + BW].astype(F32))
        pre_ref[:, 0:BW] = (h_ref[...] * gg).astype(BF16)
        pre_ref[:, BW:2 * BW] = (cur_ref[:, C_BB:C_BB + BW].astype(F32) * v["cb"]).astype(BF16)
        ln = v["ln"]
        pre_ref[:, 2 * BW:3 * BW] = (ln * _sigmoid(ln)).astype(BF16)

    hb = t // HALO
    return _call(
        body, comm, (proj, proj, convw, vecs, wx_bd, wa_bd), grid=(s // t,),
        in_specs=[pl.BlockSpec((t, GL0), lambda i: (i, 0)),
                  pl.BlockSpec((HALO, GL0), lambda i: (jnp.maximum(i * hb - 1, 0), 0)),
                  pl.BlockSpec((None, CW_ROWS, BW), lambda i: (l, 0, 0)),
                  pl.BlockSpec((None, V_ROWS, BW), lambda i: (l, 0, 0)),
                  pl.BlockSpec((None, BW, BW), lambda i: (l, 0, 0)),
                  pl.BlockSpec((None, BW, BW), lambda i: (l, 0, 0))],
        out_specs=[pl.BlockSpec((t, 3 * BW), lambda i: (i, 0)), pl.BlockSpec((t, BW), lambda i: (i, 0)),
                   pl.BlockSpec((t, 3 * BW), lambda i: (i, 0))],
        out_shape=[_sds((s, 3 * BW), BF16), _sds((s, BW), F32), _sds((s, 3 * BW), F32)],
        scratch_shapes=[pltpu.VMEM((t + HALO, BW), F32)] * 3 + [pltpu.VMEM((7, t + HALO - 8, BW), F32)]
        + [pltpu.VMEM((t, BW), F32)] * 2 + [pltpu.VMEM((1, BW), F32)],
        name=f"fwd_branch{l}")


GRP = N_HEADS // N_KV


ATT_SUB = 2


def _attn_mask_bias(first_block):
    shape = (GRP * ATT_BLK, 2 * ATT_BLK)
    qi = lax.broadcasted_iota(jnp.int32, shape, 0) & (ATT_BLK - 1)
    ki = lax.broadcasted_iota(jnp.int32, shape, 1)
    dist = qi + ATT_BLK - ki
    valid = (dist >= 0) & (dist < ATT_BLK)
    if first_block is not None:
        valid = valid & (jnp.logical_not(first_block) | (ki >= ATT_BLK))
    return dist.astype(F32), valid


def _attn_units(q_ref, kvp_ref, kvc_ref, first_step):
    units = []
    for b in range(ATT_SUB):
        rows = slice(b * ATT_BLK, (b + 1) * ATT_BLK)
        if b == 0:
            prev = lambda c0, c1: kvp_ref[:, c0:c1]
        else:
            prev = lambda c0, c1, b=b: kvc_ref[(b - 1) * ATT_BLK:b * ATT_BLK, c0:c1]
        for hk in range(N_KV):
            units.append(dict(b=b, hk=hk, rows=rows, q=lambda c0, c1, rows=rows: q_ref[rows, c0:c1], prev=prev,
                              cur=lambda c0, c1, rows=rows: kvc_ref[rows, c0:c1], first=first_step if b == 0 else None))
    return units


def _per_head(hk, values):
    hl = lax.broadcasted_iota(jnp.int32, (GRP * ATT_BLK, 1), 0) // ATT_BLK
    out = values[GRP - 1]
    for j in range(GRP - 2, -1, -1):
        out = jnp.where(hl == j, values[j], out)
    return out


def _attn_probs(units, vec_ref):
    us = range(len(units))
    heads = [range(u["hk"] * GRP, (u["hk"] + 1) * GRP) for u in units]
    masks = {id(u["first"]): _attn_mask_bias(u["first"]) for u in units}
    distf = [masks[id(u["first"])][0] for u in units]
    valid = [masks[id(u["first"])][1] for u in units]
    q4 = [jnp.concatenate([units[i]["q"](h * HD, (h + 1) * HD) for h in heads[i]], axis=0) for i in us]
    kcol = [(u["hk"] * HD, (u["hk"] + 1) * HD) for u in units]
    vcol = [((N_KV + u["hk"]) * HD, (N_KV + u["hk"] + 1) * HD) for u in units]
    k2 = [jnp.concatenate([units[i]["prev"](*kcol[i]), units[i]["cur"](*kcol[i])], axis=0) for i in us]
    v2 = [jnp.concatenate([units[i]["prev"](*vcol[i]), units[i]["cur"](*vcol[i])], axis=0) for i in us]
    slope = [_per_head(units[i]["hk"], [2.0 ** (-8.0 * (h + 1) / N_HEADS) for h in heads[i]]) for i in us]
    sink = [_per_head(units[i]["hk"], [vec_ref[V_SINK:V_SINK + 1, h:h + 1] for h in heads[i]]) for i in us]
    sc = [_dot(q4[i], k2[i], _NT) for i in us]
    sc = [jnp.where(valid[i], sc[i] * (HD ** -0.5) - slope[i] * distf[i], NEG_INF) for i in us]
    m = [jnp.maximum(jnp.max(sc[i], axis=-1, keepdims=True), sink[i]) for i in us]
    p = [jnp.exp(sc[i] - m[i]) for i in us]
    es = [jnp.exp(sink[i] - m[i]) for i in us]
    inv = [1.0 / (jnp.sum(p[i], axis=-1, keepdims=True) + es[i]) for i in us]
    return [(q4[i], k2[i], v2[i], p[i] * inv[i], es[i] * inv[i]) for i in us]


def fwd_attn(proj, vecs, l, comm=None):
    s = proj.shape[0]
    t = ATT_SUB * ATT_BLK

    def body(q_ref, kvp_ref, kvc_ref, vec_ref, o_ref):
        units = _attn_units(q_ref, kvp_ref, kvc_ref, pl.program_id(0) == 0)
        groups = _attn_probs(units, vec_ref)
        outs = [_dot(p, v2, _NN).astype(BF16) for _, _, v2, p, _ in groups]
        for u, out in zip(units, outs):
            for j in range(GRP):
                h = u["hk"] * GRP + j
                o_ref[u["rows"], h * HD:(h + 1) * HD] = out[j * ATT_BLK:(j + 1) * ATT_BLK]

    return _call(
        body, comm, (proj, proj, proj, vecs), grid=(s // t,),
        in_specs=[pl.BlockSpec((t, BW), lambda i: (i, C_Q // BW)),
                  pl.BlockSpec((ATT_BLK, 256), lambda i: (jnp.maximum(ATT_SUB * i - 1, 0), C_K // 256)),
                  pl.BlockSpec((t, 256), lambda i: (i, C_K // 256)),
                  pl.BlockSpec((None, V_ROWS, BW), lambda i: (l, 0, 0))],
        out_specs=pl.BlockSpec((t, BW), lambda i: (i, 0)),
        out_shape=_sds((s, BW), BF16), name=f"fwd_attn{l}")


def fwd_merge(x, proj, pre_abd, pre_c, wt_a, wt_b, wt_c, wt_d, w_o, l, comm=None):
    s = x.shape[0]
    tm = min(256, s)

    def body(x_ref, gl_ref, pabd_ref, pc_ref, wa_ref, wb_ref, wc_ref, wd_ref, wo_ref, y_ref, mg_ref, h1_ref):
        pres = (pabd_ref[:, 0:BW], pabd_ref[:, BW:2 * BW], pc_ref[...], pabd_ref[:, 2 * BW:3 * BW])
        merged = jnp.zeros((tm, D), F32)
        for k, (pre, w_ref) in enumerate(zip(pres, (wa_ref, wb_ref, wc_ref, wd_ref))):
            yk = _dot(pre, w_ref[...], _NT)
            y_ref[:, k * D:(k + 1) * D] = yk.astype(BF16)
            merged = merged + _sigmoid(gl_ref[:, k * D:(k + 1) * D].astype(F32)) * yk
        mg_ref[...] = merged.astype(BF16)
        h1_ref[...] = x_ref[...] + _dot(merged, wo_ref[...], _NN)

    wspec = pl.BlockSpec((D, BW), lambda i: (0, 0))
    return _call(
        body, comm, (x, proj, pre_abd, pre_c, wt_a, wt_b, wt_c, wt_d, w_o), grid=(s // tm,),
        in_specs=[pl.BlockSpec((tm, D), lambda i: (i, 0)),
                  pl.BlockSpec((E(tm), E(4 * D)), lambda i: (i * tm, GL0)),
                  pl.BlockSpec((tm, 3 * BW), lambda i: (i, 0)),
                  pl.BlockSpec((tm, BW), lambda i: (i, 0)),
                  wspec, wspec, wspec, wspec,
                  pl.BlockSpec((D, D), lambda i: (0, 0))],
        out_specs=[pl.BlockSpec((tm, 4 * D), lambda i: (i, 0)), pl.BlockSpec((tm, D), lambda i: (i, 0)),
                   pl.BlockSpec((tm, D), lambda i: (i, 0))],
        out_shape=[_sds((s, 4 * D), BF16), _sds((s, D), BF16), _sds((s, D), F32)], name=f"fwd_merge{l}")


def fwd_ffn(h1, g2, wt_gate, wt_up, w_down, l, comm=None):
    s = h1.shape[0]
    tm = min(512, s)
    fc = FF // 2

    def body(h_ref, g_ref, wg_ref, wu_ref, wd_ref, xo_ref, fg_ref, fu_ref, hn_ref, acc_ref):
        j = pl.program_id(1)

        @pl.when(j == 0)
        def _():
            hv = h_ref[...]
            r = lax.rsqrt(jnp.mean(hv * hv, axis=-1, keepdims=True) + EPS)
            hn_ref[...] = (hv * r * g_ref[l:l + 1, :]).astype(BF16)
            acc_ref[...] = hv

        fg = _dot(hn_ref[...], wg_ref[...], _NT)
        fu = _dot(hn_ref[...], wu_ref[...], _NT)
        fg_ref[...] = fg.astype(BF16)
        fu_ref[...] = fu.astype(BF16)
        acc_ref[...] += _dot(fg * _sigmoid(fg) * fu, wd_ref[...], _NN)

        @pl.when(j == pl.num_programs(1) - 1)
        def _():
            xo_ref[...] = acc_ref[...]

    wspec = pl.BlockSpec((fc, D), lambda i, j: (j, 0))
    return _call(
        body, comm, (h1, g2, wt_gate, wt_up, w_down), grid=(s // tm, FF // fc),
        in_specs=[pl.BlockSpec((tm, D), lambda i, j: (i, 0)), pl.BlockSpec((DEPTH, D), lambda i, j: (0, 0)),
                  wspec, wspec, wspec],
        out_specs=[pl.BlockSpec((tm, D), lambda i, j: (i, 0)), pl.BlockSpec((tm, fc), lambda i, j: (i, j)),
                   pl.BlockSpec((tm, fc), lambda i, j: (i, j))],
        out_shape=[_sds((s, D), F32), _sds((s, FF), BF16), _sds((s, FF), BF16)],
        scratch_shapes=[pltpu.VMEM((tm, D), BF16), pltpu.VMEM((tm, D), F32)], name=f"fwd_ffn{l}")


def loss_head(x, gf, target):
    s = x.shape[0]
    tm = min(512, s)

    def body(x_ref, g_ref, t_ref, dx_ref, st_ref):
        @pl.when(pl.program_id(0) == 0)
        def _():
            st_ref[...] = jnp.zeros((8, D), F32)

        xv = x_ref[...]
        g = g_ref[...]
        r = lax.rsqrt(jnp.mean(xv * xv, axis=-1, keepdims=True) + EPS)
        n = xv * r
        err = n * g - t_ref[...]
        dy = err * (1.0 / D)
        dn = dy * g
        dx_ref[...] = r * (dn - n * jnp.mean(dn * n, axis=-1, keepdims=True))
        st_ref[0:1, :] += jnp.sum(dy * n, axis=0, keepdims=True)
        lsum = 0.5 * jnp.sum(jnp.mean(err * err, axis=-1, keepdims=True), axis=0, keepdims=True)
        st_ref[1:2, :] += jnp.broadcast_to(lsum, (1, D))

    return pl.pallas_call(
        body, grid=(s // tm,),
        in_specs=[pl.BlockSpec((tm, D), lambda i: (i, 0)), pl.BlockSpec((1, D), lambda i: (0, 0)),
                  pl.BlockSpec((tm, D), lambda i: (i, 0))],
        out_specs=[pl.BlockSpec((tm, D), lambda i: (i, 0)), pl.BlockSpec((8, D), lambda i: (0, 0))],
        out_shape=[_sds((s, D), F32), _sds((8, D), F32)],
        compiler_params=_cparams(1), name="loss_head")(x, gf, target)


def _edge_index(j, i, n_j, n_i):
    return jnp.where((j == 0) | (j == n_j - 1), i, n_i - 1)


def bwd_ffn(dxo, h1, fg, fu, g2, wt_gate, wt_up, w_down, l, comm=None):
    s = h1.shape[0]
    tm = min(512, s)
    fc = 256
    n_j, n_i = FF // fc, s // tm

    def body(dxo_ref, h_ref, fg_ref, fu_ref, g_ref, wg_ref, wu_ref, wd_ref,
             dh_ref, dwg_ref, dwu_ref, dwd_ref, st_ref, dhn, dxo_b, hn_b, ag, au, ad):
        j, i = pl.program_id(0), pl.program_id(1)
        rows = pl.ds(pl.multiple_of(i * tm, tm), tm)
        g = g_ref[l:l + 1, :]

        @pl.when(j == 0)
        def _():
            hv = h_ref[...]
            r = lax.rsqrt(jnp.mean(hv * hv, axis=-1, keepdims=True) + EPS)
            hn_b[rows, :] = (hv * r * g).astype(BF16)
            dxo_b[rows, :] = dxo_ref[...].astype(BF16)
            dhn[rows, :] = jnp.zeros((tm, D), F32)

        @pl.when((j == 0) & (i == 0))
        def _():
            st_ref[...] = jnp.zeros((8, D), F32)

        @pl.when(i == 0)
        def _():
            ag[...] = jnp.zeros((fc, D), F32)
            au[...] = jnp.zeros((fc, D), F32)
            ad[...] = jnp.zeros((fc, D), F32)

        fgv = fg_ref[...].astype(F32)
        fuv = fu_ref[...].astype(F32)
        sg = _sigmoid(fgv)
        sil = fgv * sg
        dxb = dxo_b[rows, :]
        hnb = hn_b[rows, :]
        d_act = _dot(dxb, wd_ref[...], _NT)
        ad[...] += _dot(sil * fuv, dxb, _TN)
        d_fg = (d_act * fuv * (sg * (1.0 + fgv * (1.0 - sg)))).astype(BF16)
        d_fu = (d_act * sil).astype(BF16)
        ag[...] += _dot(d_fg, hnb, _TN)
        au[...] += _dot(d_fu, hnb, _TN)
        dhn[rows, :] += _dot(d_fg, wg_ref[...], _NN) + _dot(d_fu, wu_ref[...], _NN)

        @pl.when(i == n_i - 1)
        def _():
            dwg_ref[...] = ag[...].astype(BF16)
            dwu_ref[...] = au[...].astype(BF16)
            dwd_ref[...] = ad[...].astype(BF16)

        @pl.when(j == n_j - 1)
        def _():
            hv = h_ref[...]
            r = lax.rsqrt(jnp.mean(hv * hv, axis=-1, keepdims=True) + EPS)
            n = hv * r
            dv = dhn[rows, :]
            dn = dv * g
            dh_ref[...] = dxo_ref[...] + r * (dn - n * jnp.mean(dn * n, axis=-1, keepdims=True))
            st_ref[0:1, :] += jnp.sum(dv * n, axis=0, keepdims=True)

    edge = lambda j, i: (_edge_index(j, i, n_j, n_i), 0)
    wspec = pl.BlockSpec((fc, D), lambda j, i: (j, 0))
    dwspec = pl.BlockSpec((fc, D), lambda j, i: (j, 0))
    return _call(
        body, comm, (dxo, h1, fg, fu, g2, wt_gate, wt_up, w_down), grid=(n_j, n_i),
        in_specs=[pl.BlockSpec((tm, D), edge),
                  pl.BlockSpec((tm, D), edge),
                  pl.BlockSpec((tm, fc), lambda j, i: (i, j)), pl.BlockSpec((tm, fc), lambda j, i: (i, j)),
                  pl.BlockSpec((DEPTH, D), lambda j, i: (0, 0)), wspec, wspec, wspec],
        out_specs=[pl.BlockSpec((tm, D), lambda j, i: (jnp.where(j == n_j - 1, i, 0), 0)),
                   dwspec, dwspec, dwspec, pl.BlockSpec((8, D), lambda j, i: (0, 0))],
        out_shape=[_sds((s, D), F32), _sds((FF, D), BF16), _sds((FF, D), BF16), _sds((FF, D), BF16), _sds((8, D), F32)],
        scratch_shapes=[pltpu.VMEM((s, D), F32), pltpu.VMEM((s, D), BF16), pltpu.VMEM((s, D), BF16),
                        pltpu.VMEM((fc, D), F32), pltpu.VMEM((fc, D), F32), pltpu.VMEM((fc, D), F32)],
        name=f"bwd_ffn{l}")


def bwd_merge(dh1, y4, proj, merged, pre_abd, pre_c, wt_a, wt_b, wt_c, wt_d, w_o, l, comm=None):
    s = dh1.shape[0]
    tm = min(256, s)
    n_i = s // tm

    def body(dh_ref, y_ref, gl_ref, mg_ref, pabd_ref, pc_ref, wa_ref, wb_ref, wc_ref, wd_ref, wo_ref,
             dgl_ref, dpre_ref, dwo_ref, dwa_ref, dwb_ref, dwc_ref, dwd_ref, ao, aa, ab, ac, ad):
        i = pl.program_id(0)
        accs = (aa, ab, ac, ad)

        @pl.when(i == 0)
        def _():
            ao[...] = jnp.zeros((D, D), F32)
            for acc in accs:
                acc[...] = jnp.zeros((D, BW), F32)

        dhb = dh_ref[...].astype(BF16)
        dmg = _dot(dhb, wo_ref[...], _NT)
        ao[...] += _dot(mg_ref[...], dhb, _TN)
        pres = (pabd_ref[:, 0:BW], pabd_ref[:, BW:2 * BW], pc_ref[...], pabd_ref[:, 2 * BW:3 * BW])
        for k, (pre, w_ref, acc) in enumerate(zip(pres, (wa_ref, wb_ref, wc_ref, wd_ref), accs)):
            gk = _sigmoid(gl_ref[:, k * D:(k + 1) * D].astype(F32))
            yk = y_ref[:, k * D:(k + 1) * D].astype(F32)
            dgl_ref[:, k * D:(k + 1) * D] = (dmg * yk * gk * (1.0 - gk)).astype(BF16)
            dyk = (dmg * gk).astype(BF16)
            dpre_ref[:, k * BW:(k + 1) * BW] = _dot(dyk, w_ref[...], _NN).astype(BF16)
            acc[...] += _dot(dyk, pre, _TN)

        @pl.when(i == n_i - 1)
        def _():
            dwo_ref[...] = ao[...].astype(BF16)
            for o_ref, acc in zip((dwa_ref, dwb_ref, dwc_ref, dwd_ref), accs):
                o_ref[...] = acc[...].astype(BF16)

    wspec = pl.BlockSpec((D, BW), lambda i: (0, 0))
    dwspec = pl.BlockSpec((D, BW), lambda i: (0, 0))
    return _call(
        body, comm, (dh1, y4, proj, merged, pre_abd, pre_c, wt_a, wt_b, wt_c, wt_d, w_o), grid=(n_i,),
        in_specs=[pl.BlockSpec((tm, D), lambda i: (i, 0)),
                  pl.BlockSpec((tm, 4 * D), lambda i: (i, 0)),
                  pl.BlockSpec((E(tm), E(4 * D)), lambda i: (i * tm, GL0)),
                  pl.BlockSpec((tm, D), lambda i: (i, 0)),
                  pl.BlockSpec((tm, 3 * BW), lambda i: (i, 0)),
                  pl.BlockSpec((tm, BW), lambda i: (i, 0)),
                  wspec, wspec, wspec, wspec,
                  pl.BlockSpec((D, D), lambda i: (0, 0))],
        out_specs=[pl.BlockSpec((E(tm), E(4 * D)), lambda i: (i * tm, GL0)),
                   pl.BlockSpec((tm, 4 * BW), lambda i: (i, 0)),
                   pl.BlockSpec((D, D), lambda i: (0, 0)), dwspec, dwspec, dwspec, dwspec],
        out_shape=[_sds((s, IN_W), BF16), _sds((s, 4 * BW), BF16), _sds((D, D), BF16)] + [_sds((D, BW), BF16)] * 4,
        scratch_shapes=[pltpu.VMEM((D, D), F32)] + [pltpu.VMEM((D, BW), F32)] * 4, name=f"bwd_merge{l}")


def bwd_attn(proj, dpre, vecs, l, comm=None):
    s = proj.shape[0]
    t = ATT_SUB * ATT_BLK
    grp = N_HEADS // N_KV

    def body(q_ref, kvp_ref, kvc_ref, do_ref, vec_ref, dq_ref, dkc_ref, dkp_ref, st_ref):
        @pl.when(pl.program_id(0) == 0)
        def _():
            st_ref[...] = jnp.zeros((8, 128), F32)

        lane = lax.broadcasted_iota(jnp.int32, (1, 128), 1)
        dsink = jnp.zeros((1, 128), F32)
        units = _attn_units(q_ref, kvp_ref, kvc_ref, pl.program_id(0) == 0)
        groups = _attn_probs(units, vec_ref)
        us = range(len(units))
        do4s = [jnp.concatenate([do_ref[u["rows"], h * HD:(h + 1) * HD] for h in range(u["hk"] * grp, (u["hk"] + 1) * grp)],
                                axis=0) for u in units]
        dps = [_dot(do4s[i], groups[i][2], _NT) for i in us]
        deltas = [jnp.sum(groups[i][3] * dps[i], axis=-1, keepdims=True) for i in us]
        dss = [groups[i][3] * (dps[i] - deltas[i]) * (HD ** -0.5) for i in us]
        dq4s = [_dot(dss[i], groups[i][1], _NN).astype(BF16) for i in us]
        dk2s = [_dot(dss[i], groups[i][0], _TN) for i in us]
        dv2s = [_dot(groups[i][3], do4s[i], _TN) for i in us]
        for i, u in enumerate(units):
            psd = groups[i][4] * deltas[i]
            for j in range(grp):
                h = u["hk"] * grp + j
                rows = slice(j * ATT_BLK, (j + 1) * ATT_BLK)
                dq_ref[u["rows"], h * HD:(h + 1) * HD] = dq4s[i][rows]
                dsink = dsink + jnp.where(lane == h, -jnp.sum(psd[rows], axis=0, keepdims=True), 0.0)
        for i, u in enumerate(units):
            nxt = [k for k, w in enumerate(units) if w["hk"] == u["hk"] and w["b"] == u["b"] + 1]
            for grad, c0 in ((dk2s, u["hk"] * HD), (dv2s, (N_KV + u["hk"]) * HD)):
                own = grad[i][ATT_BLK:]
                if nxt:
                    own = own + grad[nxt[0]][0:ATT_BLK]
                dkc_ref[u["rows"], c0:c0 + HD] = own.astype(BF16)
                if u["b"] == 0:
                    dkp_ref[:, c0:c0 + HD] = grad[i][0:ATT_BLK].astype(BF16)
        st_ref[0:1, :] += dsink

    return _call(
        body, comm, (proj, proj, proj, dpre, vecs), grid=(s // t,),
        in_specs=[pl.BlockSpec((t, BW), lambda i: (i, C_Q // BW)),
                  pl.BlockSpec((ATT_BLK, 256), lambda i: (jnp.maximum(ATT_SUB * i - 1, 0), C_K // 256)),
                  pl.BlockSpec((t, 256), lambda i: (i, C_K // 256)),
                  pl.BlockSpec((t, BW), lambda i: (i, 2)),
                  pl.BlockSpec((None, V_ROWS, BW), lambda i: (l, 0, 0))],
        out_specs=[pl.BlockSpec((t, BW), lambda i: (i, 0)), pl.BlockSpec((t, 256), lambda i: (i, 0)),
                   pl.BlockSpec((ATT_BLK, 256), lambda i: (i, 0)), pl.BlockSpec((8, 128), lambda i: (0, 0))],
        out_shape=[_sds((s, BW), BF16), _sds((s, 256), BF16), _sds((s // ATT_SUB, 256), BF16), _sds((8, 128), F32)],
        name=f"bwd_attn{l}")


def bwd_branch(proj, dproj, dpre, h, saved, dq, dkc, dkp, convw, vecs, wx_bd, wa_bd, l, comm=None):
    s = proj.shape[0]
    t = 2 * ATT_BLK
    nt = s // t
    nb = s // ATT_BLK
    hb = t // HALO

    def body(cur_ref, halo_ref, dpre_ref, h_ref, hp_ref, dq_ref, dkc_ref, dkp_ref,
             cw_ref, vec_ref, wx_ref, wa_ref, sv_ref, dproj_in, dp_ref, dcw_ref, dvec_ref, dwx_ref, dwa_ref,
             bufa, bufb, bufd, xd, xg, a_ext, hbuf, b_s, g_s, dh_s, ga, gb, gd, dhcar):
        del dproj_in
        step = pl.program_id(0)
        ti = nt - 1 - step
        first = ti == 0

        @pl.when(step == 0)
        def _():
            dcw_ref[...] = jnp.zeros((CW_ROWS, BW), F32)
            dvec_ref[...] = jnp.zeros((V_ROWS, BW), F32)
            dwx_ref[...] = jnp.zeros((BW, BW), F32)
            dwa_ref[...] = jnp.zeros((BW, BW), F32)
            dhcar[...] = jnp.zeros((1, BW), F32)
            a_ext[t:t + 8, :] = jnp.zeros((8, BW), F32)
            ga[t:t + 8, :] = jnp.zeros((8, BW), F32)
            gb[t:t + 8, :] = jnp.zeros((8, BW), F32)
            gd[t:t + HALO, :] = jnp.zeros((HALO, BW), F32)

        def cur(c0):
            return cur_ref[:, c0:c0 + BW].astype(F32)

        def rsum(v):
            return jnp.sum(v, axis=0, keepdims=True)

        def put(c0, v):
            dp_ref[:, c0:c0 + BW] = v.astype(BF16)

        v = _branch_fwd_math(cur_ref, halo_ref, cw_ref, vec_ref, wx_ref, wa_ref, bufa, bufb, bufd, xd, first, t, sv_ref)
        ca, gi, gr, sp, a, mult = v["ca"], v["gi"], v["gr"], v["sp"], v["a"], v["mult"]
        dpa = dpre_ref[:, 0:BW].astype(F32)
        gg, dgg = _gelu_and_grad(cur(C_AG))
        hv = h_ref[...]
        put(C_AG, dpa * hv * dgg)
        a_ext[0:t, :] = a
        b_s[...] = a_ext[pl.ds(1, t), :]
        g_s[...] = dpa * gg
        dhcar[...] = _scan_bwd(b_s, g_s, dh_s, dhcar[...], t)
        a_ext[t:t + 1, :] = a[0:1, :]
        dh = dh_s[...]
        hbuf[0:8, :] = jnp.where(first, 0.0, hp_ref[...])
        hbuf[8:8 + t, :] = hv
        da = dh * hbuf[pl.ds(7, t), :]
        d_ca = dh * gi * mult
        d_gi = dh * ca * mult
        d_mult = dh * ca * gi
        d_la = da * a - d_mult * (a * a) / mult
        lam = vec_ref[V_LAM:V_LAM + 1, :]
        dvec_ref[V_LAM:V_LAM + 1, :] += rsum(d_la * gr) * (LRU_C * _sigmoid(-lam))
        d_gr = d_la * (-LRU_C * sp)
        d_zr = d_gr * gr * (1.0 - gr)
        d_zi = d_gi * gi * (1.0 - gi)
        dvec_ref[V_BA:V_BA + 1, :] += rsum(d_zr)
        dvec_ref[V_BX:V_BX + 1, :] += rsum(d_zi)
        dwa_ref[...] += _dot(ca, d_zr, _TN)
        dwx_ref[...] += _dot(ca, d_zi, _TN)
        d_ca = d_ca + _dot(d_zi, wx_ref[...], _NT) + _dot(d_zr, wa_ref[...], _NT)
        dvec_ref[V_CAB:V_CAB + 1, :] += rsum(d_ca)
        ga[0:t, :] = d_ca
        d_ax = jnp.zeros((t, BW), F32)
        for k in range(CONV_A):
            d_ax = d_ax + cw_ref[CW_A + k:CW_A + k + 1, :] * ga[pl.ds(CONV_A - 1 - k, t), :]
            dcw_ref[CW_A + k:CW_A + k + 1, :] += rsum(d_ca * bufa[pl.ds(HALO - (CONV_A - 1) + k, t), :])
        ga[t:t + 8, :] = d_ca[0:8, :]
        put(C_AX, d_ax)
        dpb = dpre_ref[:, BW:2 * BW].astype(F32)
        put(C_BB, dpb * v["cb"])
        d_cb = dpb * cur(C_BB)
        gb[0:t, :] = d_cb
        d_cbin = jnp.zeros((t, BW), F32)
        for k in range(CONV_B):
            d_cbin = d_cbin + cw_ref[CW_B + k:CW_B + k + 1, :] * gb[pl.ds(CONV_B - 1 - k, t), :]
            dcw_ref[CW_B + k:CW_B + k + 1, :] += rsum(d_cb * bufb[pl.ds(HALO - (CONV_B - 1) + k, t), :])
        gb[t:t + 8, :] = d_cb[0:8, :]
        put(C_BC, d_cbin * cur(C_BV))
        put(C_BV, d_cbin * cur(C_BC))
        dpd = dpre_ref[:, 3 * BW:4 * BW].astype(F32)
        ln, xh, rstd, s2 = v["ln"], v["xh"], v["rstd"], v["s2"]
        sg = _sigmoid(ln)
        d_ln = dpd * sg * (1.0 + ln * (1.0 - sg))
        dvec_ref[V_LNG:V_LNG + 1, :] += rsum(d_ln * xh)
        dvec_ref[V_LNB:V_LNB + 1, :] += rsum(d_ln)
        d_xh = d_ln * vec_ref[V_LNG:V_LNG + 1, :]
        d_cd = rstd * (d_xh - jnp.mean(d_xh, axis=-1, keepdims=True)
                       - xh * jnp.mean(d_xh * xh, axis=-1, keepdims=True))
        dvec_ref[V_CDB:V_CDB + 1, :] += rsum(d_cd)
        gd[0:t, :] = d_cd
        _shifted_copies(gd, xg, t + HALO)
        d_dg = jnp.zeros((t, BW), F32)
        for k in range(CONV_D):
            d_dg = d_dg + cw_ref[CW_D + k:CW_D + k + 1, :] * _window(gd, xg, CONV_D - 1 - k, t)
            dcw_ref[CW_D + k:CW_D + k + 1, :] += rsum(d_cd * _window(bufd, xd, HALO - (CONV_D - 1) + k, t))
        gd[t:t + HALO, :] = d_cd[0:HALO, :]
        put(C_D1, d_dg * s2)
        put(C_D2, d_dg * cur(C_D1) * s2 * (1.0 - s2))
        dp_ref[:, C_Q:C_Q + BW] = dq_ref[...]
        dkp = jnp.where(step == 0, 0.0, dkp_ref[...].astype(F32))
        dp_ref[0:t - ATT_BLK, C_K:C_K + 256] = dkc_ref[0:t - ATT_BLK, :]
        dp_ref[t - ATT_BLK:t, C_K:C_K + 256] = (dkc_ref[t - ATT_BLK:t, :].astype(F32) + dkp).astype(BF16)

    rev = lambda i: nt - 1 - i
    full = lambda r, c: pl.BlockSpec((r, c), lambda i: (0, 0))
    return _call(
        body, comm, (proj, proj, dpre, h, h, dq, dkc, dkp, convw, vecs, wx_bd, wa_bd, saved, dproj), grid=(nt,),
        in_specs=[pl.BlockSpec((t, GL0), lambda i: (rev(i), 0)),
                  pl.BlockSpec((HALO, GL0), lambda i: (jnp.maximum(rev(i) * hb - 1, 0), 0)),
                  pl.BlockSpec((t, 4 * BW), lambda i: (rev(i), 0)),
                  pl.BlockSpec((t, BW), lambda i: (rev(i), 0)),
                  pl.BlockSpec((8, BW), lambda i: (jnp.maximum(rev(i) * (t // 8) - 1, 0), 0)),
                  pl.BlockSpec((t, BW), lambda i: (rev(i), 0)),
                  pl.BlockSpec((t, 256), lambda i: (rev(i), 0)),
                  pl.BlockSpec((ATT_BLK, 256), lambda i: (jnp.minimum(rev(i) + 1, nt - 1), 0)),
                  pl.BlockSpec((None, CW_ROWS, BW), lambda i: (l, 0, 0)),
                  pl.BlockSpec((None, V_ROWS, BW), lambda i: (l, 0, 0)),
                  pl.BlockSpec((None, BW, BW), lambda i: (l, 0, 0)),
                  pl.BlockSpec((None, BW, BW), lambda i: (l, 0, 0)),
                  pl.BlockSpec((t, 3 * BW), lambda i: (rev(i), 0)),
                  pl.BlockSpec(memory_space=pl.ANY)],
        out_specs=[pl.BlockSpec((t, GL0), lambda i: (rev(i), 0)),
                   full(CW_ROWS, BW), full(V_ROWS, BW), full(BW, BW), full(BW, BW)],
        out_shape=[_sds((s, IN_W), BF16), _sds((CW_ROWS, BW), F32), _sds((V_ROWS, BW), F32),
                   _sds((BW, BW), F32), _sds((BW, BW), F32)],
        scratch_shapes=[pltpu.VMEM((t + HALO, BW), F32)] * 3 + [pltpu.VMEM((7, t + HALO - 8, BW), F32)] * 2
        + [pltpu.VMEM((t + 8, BW), F32), pltpu.VMEM((t + 8, BW), F32)]
        + [pltpu.VMEM((t, BW), F32)] * 3
        + [pltpu.VMEM((t + 8, BW), F32), pltpu.VMEM((t + 8, BW), F32), pltpu.VMEM((t + HALO, BW), F32),
           pltpu.VMEM((1, BW), F32)],
        aliases={13: 0}, name=f"bwd_branch{l}")


def bwd_proj(dproj, x, dh1, g1, wt_in, l, comm=None):
    s = x.shape[0]
    tm = min(512, s)
    ck = 1408
    n_j, n_i = IN_W // ck, s // tm

    def body(dp_ref, x_ref, dh_ref, g_ref, w_ref, dx_ref, dw_ref, st_ref, dxn, xn_b, acc):
        j, i = pl.program_id(0), pl.program_id(1)
        rows = pl.ds(pl.multiple_of(i * tm, tm), tm)
        g = g_ref[l:l + 1, :]

        @pl.when(j == 0)
        def _():
            xv = x_ref[...]
            r = lax.rsqrt(jnp.mean(xv * xv, axis=-1, keepdims=True) + EPS)
            xn_b[rows, :] = (xv * r * g).astype(BF16)
            dxn[rows, :] = jnp.zeros((tm, D), F32)

        @pl.when((j == 0) & (i == 0))
        def _():
            st_ref[...] = jnp.zeros((8, D), F32)

        @pl.when(i == 0)
        def _():
            acc[...] = jnp.zeros((ck, D), F32)

        dp = dp_ref[...]
        dxn[rows, :] += _dot(dp, w_ref[...], _NN)
        acc[...] += _dot(dp, xn_b[rows, :], _TN)

        @pl.when(i == n_i - 1)
        def _():
            dw_ref[...] = acc[...].astype(BF16)

        @pl.when(j == n_j - 1)
        def _():
            xv = x_ref[...]
            r = lax.rsqrt(jnp.mean(xv * xv, axis=-1, keepdims=True) + EPS)
            n = xv * r
            dv = dxn[rows, :]
            dn = dv * g
            dx_ref[...] = dh_ref[...] + r * (dn - n * jnp.mean(dn * n, axis=-1, keepdims=True))
            st_ref[0:1, :] += jnp.sum(dv * n, axis=0, keepdims=True)

    lastrow = lambda j, i: (jnp.where(j == n_j - 1, i, 0), 0)
    return _call(
        body, comm, (dproj, x, dh1, g1, wt_in), grid=(n_j, n_i),
        in_specs=[pl.BlockSpec((tm, ck), lambda j, i: (i, j)),
                  pl.BlockSpec((tm, D), lambda j, i: (_edge_index(j, i, n_j, n_i), 0)),
                  pl.BlockSpec((tm, D), lastrow),
                  pl.BlockSpec((DEPTH, D), lambda j, i: (0, 0)),
                  pl.BlockSpec((ck, D), lambda j, i: (j, 0))],
        out_specs=[pl.BlockSpec((tm, D), lastrow), pl.BlockSpec((ck, D), lambda j, i: (j, 0)),
                   pl.BlockSpec((8, D), lambda j, i: (0, 0))],
        out_shape=[_sds((s, D), F32), _sds((IN_W, D), BF16), _sds((8, D), F32)],
        scratch_shapes=[pltpu.VMEM((s, D), F32), pltpu.VMEM((s, D), BF16), pltpu.VMEM((ck, D), F32)],
        name=f"bwd_proj{l}")


def bwd_proj_w(dproj, x, g1, l, half, comm=None):
    s = x.shape[0]
    tm = min(1024, s)
    ck = 1408
    c0, hw = W_IN_PARTS[half]
    n_j, n_i = IN_W // ck, s // tm

    def body(dp_ref, x_ref, g_ref, dw_ref, xn_b, acc):
        j, i = pl.program_id(0), pl.program_id(1)
        rows = pl.ds(pl.multiple_of(i * tm, tm), tm)

        @pl.when(j == 0)
        def _():
            xv = x_ref[...]
            r = lax.rsqrt(jnp.mean(xv * xv, axis=-1, keepdims=True) + EPS)
            xn_b[rows, :] = (xv * r * g_ref[l:l + 1, :])[:, c0:c0 + hw].astype(BF16)

        @pl.when(i == 0)
        def _():
            acc[...] = jnp.zeros((ck, hw), F32)

        acc[...] += _dot(dp_ref[...], xn_b[rows, :], _TN)

        @pl.when(i == n_i - 1)
        def _():
            dw_ref[...] = acc[...].astype(BF16)

    return _call(
        body, comm, (dproj, x, g1), grid=(n_j, n_i),
        in_specs=[pl.BlockSpec((tm, ck), lambda j, i: (i, j)),
                  pl.BlockSpec((tm, D), lambda j, i: (jnp.where(j == 0, i, n_i - 1), 0)),
                  pl.BlockSpec((DEPTH, D), lambda j, i: (0, 0))],
        out_specs=pl.BlockSpec((ck, hw), lambda j, i: (j, 0)),
        out_shape=_sds((IN_W, hw), BF16),
        scratch_shapes=[pltpu.VMEM((s, hw), BF16), pltpu.VMEM((ck, hw), F32)],
        name=f"bwd_proj_w{half}_{l}")


def bwd_proj_x(dproj, x, dh1, g1, wt_in, l, comm=None):
    s = x.shape[0]
    tm = min(512, s)
    ck = 1408
    n_j, n_i = IN_W // ck, s // tm

    def body(dp_ref, x_ref, dh_ref, g_ref, w_ref, dx_ref, st_ref, dxn):
        j, i = pl.program_id(0), pl.program_id(1)
        rows = pl.ds(pl.multiple_of(i * tm, tm), tm)
        g = g_ref[l:l + 1, :]

        @pl.when((j == 0) & (i == 0))
        def _():
            st_ref[...] = jnp.zeros((8, D), F32)

        part = _dot(dp_ref[...], w_ref[...], _NN)

        @pl.when(j == 0)
        def _():
            dxn[rows, :] = part

        @pl.when(j > 0)
        def _():
            dxn[rows, :] += part

        @pl.when(j == n_j - 1)
        def _():
            xv = x_ref[...]
            r = lax.rsqrt(jnp.mean(xv * xv, axis=-1, keepdims=True) + EPS)
            n = xv * r
            dv = dxn[rows, :]
            dn = dv * g
            dx_ref[...] = dh_ref[...] + r * (dn - n * jnp.mean(dn * n, axis=-1, keepdims=True))
            st_ref[0:1, :] += jnp.sum(dv * n, axis=0, keepdims=True)

    lastrow = lambda j, i: (jnp.where(j == n_j - 1, i, 0), 0)
    return _call(
        body, comm, (dproj, x, dh1, g1, wt_in), grid=(n_j, n_i),
        in_specs=[pl.BlockSpec((tm, ck), lambda j, i: (i, j)), pl.BlockSpec((tm, D), lastrow),
                  pl.BlockSpec((tm, D), lastrow),
                  pl.BlockSpec((DEPTH, D), lambda j, i: (0, 0)), pl.BlockSpec((ck, D), lambda j, i: (j, 0))],
        out_specs=[pl.BlockSpec((tm, D), lastrow), pl.BlockSpec((8, D), lambda j, i: (0, 0))],
        out_shape=[_sds((s, D), F32), _sds((8, D), F32)],
        scratch_shapes=[pltpu.VMEM((s, D), F32)], name=f"bwd_proj_x{l}")


def _block_diag(w):
    nl, nb, bw, _ = w.shape
    eye = jnp.eye(nb, dtype=w.dtype)
    return jnp.einsum("lhij,hk->lhikj", w, eye).reshape(nl, nb * bw, nb * bw).astype(BF16)


class NoOverlap:
    def __init__(self, big):
        self.big = big

    def weights(self, l):
        return self.big[l]

    def job(self, slot, l):
        return None

    def done(self, slot, l, results):
        pass

    def new_grads(self, group, l, grads):
        pass

    def new_small(self, l, arrays, head_stats):
        pass


def local_step(x, target, norm1_g, norm2_g, final_g, convw, vecs, lru_wx, lru_wa, plan):
    wx_bd, wa_bd = _block_diag(lru_wx), _block_diag(lru_wa)

    def run(fn, slot, l, *args):
        res, cres = fn(*args, l, comm=plan.job(slot, l))
        plan.done(slot, l, cres)
        return res

    saved = []
    for l in range(DEPTH):
        proj = run(fwd_proj, "fwd_proj", l, x, norm1_g, plan.weights(l)["in_t"])
        pre_abd, h, kept = run(fwd_branch, "fwd_branch", l, proj, convw, vecs, wx_bd, wa_bd)
        pre_c = run(fwd_attn, "fwd_attn", l, proj, vecs)
        w = plan.weights(l)
        y4, merged, h1 = run(fwd_merge, "fwd_merge", l, x, proj, pre_abd, pre_c, w["a_t"], w["b_t"], w["c_t"], w["d_t"], w["o"])
        w = plan.weights(l)
        x_out, fg, fu = run(fwd_ffn, "fwd_ffn", l, h1, norm2_g, w["gate_t"], w["up_t"], w["down"])
        saved.append((x, proj, pre_abd, h, kept, pre_c, y4, merged, h1, fg, fu))
        x = x_out
    dx, head_stats = loss_head(x, final_g.reshape(1, D), target)
    small = [None] * DEPTH
    for l in reversed(range(DEPTH)):
        x_in, proj, pre_abd, h, kept, pre_c, y4, merged, h1, fg, fu = saved[l]
        w = plan.weights(l)
        dh1, d_gate, d_up, d_down, st_ffn = run(bwd_ffn, "bwd_ffn", l, dx, h1, fg, fu, norm2_g, w["gate_t"], w["up_t"], w["down"])
        plan.new_grads("ffn", l, dict(gate_t=d_gate, up_t=d_up, down=d_down))
        dproj, dpre, d_o, d_a, d_b, d_c, d_d = run(
            bwd_merge, "bwd_merge", l, dh1, y4, proj, merged, pre_abd, pre_c, w["a_t"], w["b_t"], w["c_t"], w["d_t"], w["o"])
        plan.new_grads("out", l, dict(a_t=d_a, b_t=d_b, c_t=d_c, d_t=d_d, o=d_o))
        dq, dkc, dkp, st_attn = run(bwd_attn, "bwd_attn", l, proj, dpre, vecs)
        dproj, dcw, dvec, dwx, dwa = run(bwd_branch, "bwd_branch", l, proj, dproj, dpre, h, kept, dq, dkc, dkp, convw, vecs, wx_bd, wa_bd)
        if l > 0:
            dx, d_in, st_proj = run(bwd_proj, "bwd_proj", l, dproj, x_in, dh1, norm1_g, w["in_t"])
            plan.new_grads("in", l, dict(in_t=d_in))
        else:
            for half, name in enumerate(("in_a", "in_b")):
                d_half = run(functools.partial(bwd_proj_w, half=half), f"bwd_proj_w{half}", l, dproj, x_in, norm1_g)
                plan.new_grads(name, l, {name: d_half})
            dx, st_proj = run(bwd_proj_x, "bwd_proj_x", l, dproj, x_in, dh1, norm1_g, w["in_t"])
        small[l] = (st_proj, st_ffn, dvec, st_attn, dcw, dwx, dwa)
        plan.new_small(l, small[l], head_stats)
    return head_stats, dx, small


BIG = dict(in_t=("w_in", "view"), a_t=("w_a_out", "transpose"), b_t=("w_b_out", "transpose"), c_t=("w_c_out", "transpose"),
           d_t=("w_d_out", "transpose"), o=("w_o", "plain"), gate_t=("w_ffn_gate", "view"), up_t=("w_ffn_up", "view"),
           down=("w_ffn_down", "plain"))


def cast_transpose(ws, name):
    n = len(ws)
    nl, a, b = ws[0].shape
    ta = min(256, a)

    def body(*refs):
        for w_ref, o_ref in zip(refs[:n], refs[n:]):
            o_ref[...] = w_ref[...].T.astype(BF16)

    return pl.pallas_call(
        body, grid=(nl, a // ta),
        in_specs=[pl.BlockSpec((None, ta, b), lambda l, i: (l, i, 0))] * n,
        out_specs=[pl.BlockSpec((None, b, ta), lambda l, i: (l, 0, i))] * n,
        out_shape=[_sds((nl, b, a), BF16)] * n, compiler_params=_cparams(2), name=name)(*ws)


def add_partials(mine, recv, core, name):
    n = len(mine)

    def body(core_ref, *refs):
        del core_ref
        for a_ref, b_ref, o_ref in zip(refs[:n], refs[n:2 * n], refs[2 * n:]):
            o_ref[...] = (a_ref[...].astype(F32) + b_ref[...].astype(F32)).astype(BF16)

    return pl.pallas_call(
        body,
        grid_spec=pltpu.PrefetchScalarGridSpec(
            num_scalar_prefetch=1, grid=(4,),
            in_specs=[pl.BlockSpec((None, None) + a.shape[2:], lambda i, cr: (i, cr[0], 0, 0)) for a in mine]
            + [pl.BlockSpec((None,) + b.shape[1:], lambda i, cr: (i, 0, 0)) for b in recv],
            out_specs=[pl.BlockSpec((None,) + b.shape[1:], lambda i, cr: (i, 0, 0)) for b in recv]),
        out_shape=[_sds(b.shape, BF16) for b in recv], compiler_params=_cparams(1), name=name)(core, *mine, *recv)


def _adamw(w, g, m, v):
    m = ADAM_B1 * m + (1.0 - ADAM_B1) * g
    v = ADAM_B2 * v + (1.0 - ADAM_B2) * (g * g)
    m_hat = m / (1.0 - ADAM_B1 ** ADAM_STEP)
    v_hat = v / (1.0 - ADAM_B2 ** ADAM_STEP)
    delta = -ADAM_LR * (m_hat / (jnp.sqrt(v_hat) + ADAM_EPS) + ADAM_WD * w)
    return delta, m, v


def adamw_big(items, name, comm=None):
    n_tiles = 4
    n = len(items)
    nl = items[0][0].shape[1]

    def body(*refs):
        ins, outs = refs[:4 * n], refs[4 * n:]
        for k, (contrib, _, _, _, transposed) in enumerate(items):
            c_ref, w_ref, m_ref, v_ref = ins[4 * k:4 * k + 4]
            g = c_ref[0].astype(F32)
            for src in range(1, contrib.shape[0]):
                g = g + c_ref[src].astype(F32)
            if transposed:
                g = g.T
            delta, mn, vn = _adamw(w_ref[...], g, m_ref[...], v_ref[...])
            for o_ref, val in zip(outs[4 * k:4 * k + 4], (g, delta, mn, vn)):
                o_ref[...] = val

    in_specs, out_specs, out_shape, args = [], [], [], []
    for contrib, w, m, v, transposed in items:
        nsrc, _, rows, cols = contrib.shape
        ct = cols // n_tiles
        if transposed:
            wspec = pl.BlockSpec((None, ct, rows), lambda l, j: (l, j, 0))
        else:
            wspec = pl.BlockSpec((None, rows, ct), lambda l, j: (l, 0, j))
        in_specs += [pl.BlockSpec((nsrc, None, rows, ct), lambda l, j: (0, l, 0, j)), wspec, wspec, wspec]
        out_specs += [wspec] * 4
        out_shape += [_sds(w.shape, F32)] * 4
        args += [contrib, w, m, v]
    res, cres = _call(body, comm, tuple(args), grid=(nl, n_tiles), in_specs=in_specs, out_specs=out_specs,
                      out_shape=out_shape, name=name)
    return [res[4 * k:4 * k + 4] for k in range(n)], cres


VEC_NAMES = ("conv_a_b", "lru_bx", "lru_ba", "lru_lambda", "conv_d_b", "ln_d_g", "ln_d_b")
P_N1, P_N2, P_VEC, P_CONV, P_LRU = 0, 1, 2, 6, 6 + CW_ROWS
P_FINAL, P_LOSS, P_ROWS = P_LRU + HD, P_LRU + HD + 1, P_LRU + HD + 2
SMALL = ("norm1_g", "conv_a_w", "conv_a_b", "lru_wx", "lru_bx", "lru_wa", "lru_ba", "lru_lambda", "conv_b_w", "sinks",
         "conv_d_w", "conv_d_b", "ln_d_g", "ln_d_b", "norm2_g", "final_g")
VMEM_FULL = pl.BlockSpec(memory_space=pltpu.VMEM)


def _stack_vecs(p):
    rows = [p[n] for n in VEC_NAMES] + [jnp.pad(p["sinks"], ((0, 0), (0, BW - N_HEADS)))]
    return jnp.stack(rows, axis=1)


def _stack_convs(p):
    nl, _, ch = p["conv_a_w"].shape
    z = jnp.zeros((nl, 1, ch), F32)
    return jnp.concatenate([p["conv_a_w"], p["conv_b_w"], z, p["conv_d_w"], z], axis=1)


def _vec_place(r):
    return P_VEC + r // 2, (r % 2) * BW


def pack_small(arrays, head_stats, l):
    n = len(arrays)

    def body(*refs):
        st_proj, st_ffn, dvec, st_attn, dcw, dwx, dwa = refs[:n]
        pack = refs[-1]
        pack[...] = jnp.zeros((P_ROWS, D), F32)
        lane = lax.broadcasted_iota(jnp.int32, (HD, BW), 1)
        pack[P_N1:P_N1 + 1, :] = st_proj[0:1, :]
        pack[P_N2:P_N2 + 1, :] = st_ffn[0:1, :]
        for r in range(len(VEC_NAMES)):
            row, c0 = _vec_place(r)
            pack[row:row + 1, c0:c0 + BW] = dvec[r:r + 1, :]
        row, c0 = _vec_place(V_SINK)
        pack[row:row + 1, c0:c0 + 128] = st_attn[0:1, :]
        pack[P_CONV:P_CONV + CW_ROWS, 0:BW] = dcw[...]
        for mat, c0 in ((dwx, 0), (dwa, BW)):
            blocks = jnp.zeros((HD, BW), F32)
            for h in range(BW // HD):
                blocks = jnp.where((lane >= HD * h) & (lane < HD * (h + 1)), mat[HD * h:HD * (h + 1), :], blocks)
            pack[P_LRU:P_LRU + HD, c0:c0 + BW] = blocks
        if head_stats is not None:
            pack[P_FINAL:P_LOSS + 1, :] = refs[n][0:2, :]

    flat = list(arrays) + ([] if head_stats is None else [head_stats])
    return pl.pallas_call(body, out_shape=_sds((P_ROWS, D), F32), in_specs=[VMEM_FULL] * len(flat), out_specs=VMEM_FULL,
                          name=f"pack_small{l}", compiler_params=pltpu.CompilerParams(vmem_limit_bytes=VMEM_LIMIT))(*flat)


def adamw_small(gathered, me, w, m, v):
    ns = len(SMALL)

    def body(me_ref, *refs):
        c_refs, refs = refs[:DEPTH], refs[DEPTH:]
        w_refs, m_refs, v_refs = refs[:ns], refs[ns:2 * ns], refs[2 * ns:3 * ns]
        loss_ref, outs, gs = refs[3 * ns], refs[3 * ns + 1:3 * ns + 1 + 4 * ns], refs[-1]
        for l in range(DEPTH):
            gs[l] = c_refs[l][0]
            for dev in range(1, NDEV):
                gs[l] += c_refs[l][dev]
        loss_ref[...] = gs[DEPTH - 1, P_LOSS:P_LOSS + 1, 0:128]

        def update(name, sel, g):
            i = SMALL.index(name)
            delta, mn, vn = _adamw(w_refs[i][sel], g, m_refs[i][sel], v_refs[i][sel])
            for o_ref, val in zip(outs[4 * i:4 * i + 4], (g, delta, mn, vn)):
                o_ref[sel] = val

        update("final_g", (slice(0, 1), slice(None)), gs[DEPTH - 1, P_FINAL:P_FINAL + 1, :])
        shift = (BW - me_ref[0] * (BW // NDEV)) & (BW - 1)
        for l in range(DEPTH):
            row = (slice(l, l + 1), slice(None))
            update("norm1_g", row, gs[l, P_N1:P_N1 + 1, :])
            update("norm2_g", row, gs[l, P_N2:P_N2 + 1, :])
            for r, name in enumerate(VEC_NAMES):
                prow, c0 = _vec_place(r)
                update(name, row, gs[l, prow:prow + 1, c0:c0 + BW])
            prow, c0 = _vec_place(V_SINK)
            update("sinks", row, gs[l, prow:prow + 1, c0:c0 + N_HEADS])
            mine = pltpu.roll(gs[l, P_CONV:P_CONV + CW_ROWS, 0:BW], shift, 1)[:, 0:BW // NDEV]
            update("conv_a_w", (l,), mine[CW_A:CW_A + CONV_A])
            update("conv_b_w", (l,), mine[CW_B:CW_B + CONV_B])
            update("conv_d_w", (l,), mine[CW_D:CW_D + CONV_D])
            for h in range(BW // HD):
                update("lru_wx", (l, h), gs[l, P_LRU:P_LRU + HD, HD * h:HD * (h + 1)])
                update("lru_wa", (l, h), gs[l, P_LRU:P_LRU + HD, BW + HD * h:BW + HD * (h + 1)])

    args = [p[n] for p in (w, m, v) for n in SMALL]
    full = lambda a: pl.BlockSpec(a.shape, lambda i, me_ref: (0,) * a.ndim)
    out_shape = [_sds((1, 128), F32)] + [_sds(w[n].shape, F32) for n in SMALL for _ in range(4)]
    outs = pl.pallas_call(
        body,
        grid_spec=pltpu.PrefetchScalarGridSpec(
            num_scalar_prefetch=1, grid=(1,),
            in_specs=[full(a) for a in list(gathered) + args], out_specs=[full(o) for o in out_shape],
            scratch_shapes=[pltpu.VMEM((DEPTH, P_ROWS, D), F32)]),
        out_shape=out_shape, name="adamw_small", compiler_params=_cparams(1))(me, *gathered, *args)
    return outs[0], {n: outs[1 + 4 * i:5 + 4 * i] for i, n in enumerate(SMALL)}


def merge_jobs(jobs):
    jobs = [j for j in jobs if j is not None]
    if not jobs:
        return None, []
    inputs, aliases, outs, sems, cuts = [], {}, [], [], []
    for j in jobs:
        i0, o0, s0 = len(inputs), len(outs), len(sems)
        aliases.update({i0 + i: o0 + o for i, o in j.aliases.items()})
        inputs += j.inputs
        outs += j.out_shapes
        sems += j.sem_shapes
        cuts.append((i0, len(inputs), o0, len(outs), s0, len(sems)))

    def each(which):
        def go(cins, couts, s):
            for j, (i0, i1, o0, o1, s0, s1) in zip(jobs, cuts):
                if getattr(j, which) is not None:
                    getattr(j, which)(cins[i0:i1], couts[o0:o1], s[s0:s1])
        return go

    relay = each("relay") if any(j.relay is not None for j in jobs) else None
    return CommJob(inputs, aliases, outs, sems, each("start"), each("finish"), relay), [(c[2], c[3]) for c in cuts]


SIXTHS = 6
OUT_KINDS = ("a_t", "b_t", "c_t", "d_t", "o")
GATHER_PLAN = {
    "fwd_proj": [(k, 0, 0, 6) for k in OUT_KINDS] + [("gate_t", 0, 0, 6)],
    "fwd_branch": [("up_t", 0, 0, 6)],
    "fwd_attn": [("down", 0, 0, 6)],
    "fwd_merge": [("in_t", 1, 0, 2)],
    "fwd_ffn": [("in_t", 1, 2, 6)],
}
SIBLING_PLAN = {"bwd_merge": ("ffn", 0), "bwd_branch": ("out", 0), "bwd_ffn": ("in", 1),
                "bwd_proj_w1": ("in_a", 0), "bwd_proj_x": ("in_b", 0)}
GROUPS = dict(ffn=("gate_t", "up_t", "down"), out=OUT_KINDS, in_a=("in_a",), in_b=("in_b",))
GROUPS["in"] = ("in_t",)
COLUMN_HALF = dict(in_a=("in_t", W_IN_PARTS[0][0]), in_b=("in_t", W_IN_PARTS[1][0]))
CHIP_PLAN = {
    "bwd_attn": [("in_t", 1, 3, 5)],
    "bwd_branch": [("in_t", 1, 5, 6), ("gate_t", 0, 0, 6), ("up_t", 0, 0, 6), ("down", 0, 0, 3)],
    "bwd_proj": [(k, 0, 0, 6) for k in OUT_KINDS] + [("down", 0, 3, 6)],
    "bwd_proj_w0": [(k, 0, 0, 6) for k in OUT_KINDS[:3]] + [("down", 0, 3, 6)],
    "bwd_proj_w1": [(k, 0, 0, 6) for k in OUT_KINDS[3:]],
    "bwd_merge": [("in_t", 1, 0, 3)],
    "bwd_proj_x": [("in_a", 0, 0, 5)],
    "adamw_rest": [("in_a", 0, 5, 6), ("in_b", 0, 0, 6)],
}
SMALL_GATHER_PLAN = {"bwd_ffn": 1, "adamw_rest": 0}


class Overlap:
    def __init__(self, shards, core):
        self.shards = shards
        self.core = core
        self.gathered = [dict.fromkeys(BIG) for _ in range(DEPTH)]
        self.views = {}
        self.partial = {}
        self.contrib = dict.fromkeys(BIG)
        self.small_packs = [None] * DEPTH
        self.small_gathered = [None] * DEPTH
        self._open = None

    def weights(self, l):
        return self.gathered[l]

    def new_grads(self, group, l, grads):
        for k, g in grads.items():
            self.views[k, l] = g.reshape(4, 2, g.shape[0] // NDEV, g.shape[1])

    def new_small(self, l, arrays, head_stats):
        self.small_packs[l] = pack_small(arrays, head_stats if l == DEPTH - 1 else None, l)

    @staticmethod
    def _rows(shard_rows, f0, f1):
        return shard_rows * f0 // SIXTHS, shard_rows * (f1 - f0) // SIXTHS

    def job(self, slot, l):
        jobs, notes = [], []
        pieces = [(k, l + dl, f0, f1) for k, dl, f0, f1 in GATHER_PLAN.get(slot, []) if l + dl < DEPTH]
        if pieces:
            jobs.append(gather_job([((k, ll), self.shards[ll][k], self.gathered[ll][k],
                                     *self._rows(self.shards[ll][k].shape[0], f0, f1)) for k, ll, f0, f1 in pieces]))
            notes.append(("gather", list(dict.fromkeys((k, ll) for k, ll, _, _ in pieces))))
        if slot in SIBLING_PLAN and l + SIBLING_PLAN[slot][1] < DEPTH:
            group, dl = SIBLING_PLAN[slot]
            keys = [(k, l + dl) for k in GROUPS[group]]
            jobs.append(sibling_exchange_job([self.views[key] for key in keys]))
            notes.append(("sibling", keys))
        pieces = [(k, l + dl, f0, f1) for k, dl, f0, f1 in CHIP_PLAN.get(slot, []) if l + dl < DEPTH]
        if pieces:
            whole = [(*COLUMN_HALF.get(k, (k, 0)), k, ll, f0, f1) for k, ll, f0, f1 in pieces]
            jobs.append(chip_exchange_job([(self.partial[k, ll], self.contrib[kind], kind, ll,
                                            *self._rows(self.partial[k, ll].shape[1], f0, f1), col0, self.shards[ll][kind].shape[1])
                                           for kind, col0, k, ll, f0, f1 in whole]))
            notes.append(("chips", list(dict.fromkeys(kind for kind, *_ in whole))))
        if slot in SMALL_GATHER_PLAN and l + SMALL_GATHER_PLAN[slot] < DEPTH:
            ll = l + SMALL_GATHER_PLAN[slot]
            jobs.append(gather_job([("small", self.small_packs[ll], None, 0, P_ROWS)]))
            notes.append(("small", ll))
        job, spans = merge_jobs(jobs)
        self._open = (slot, l, notes, spans)
        return job

    def done(self, slot, l, results):
        open_slot, open_l, notes, spans = self._open
        assert (open_slot, open_l) == (slot, l)
        for (what, keys), (r0, r1) in zip(notes, spans):
            res = results[r0:r1]
            if what == "gather":
                for (k, ll), g in zip(keys, res):
                    self.gathered[ll][k] = g
            elif what == "sibling":
                sums = add_partials([self.views[key] for key in keys], list(res), self.core, f"chip_sum_{keys[0][0]}{keys[0][1]}")
                self.partial.update(zip(keys, sums))
            elif what == "chips":
                for k, c in zip(keys, res):
                    self.contrib[k] = c
            else:
                self.small_gathered[keys], = res


SMALL = ("norm1_g", "conv_a_w", "conv_a_b", "lru_wx", "lru_bx", "lru_wa", "lru_ba", "lru_lambda", "conv_b_w", "sinks",
         "conv_d_w", "conv_d_b", "ln_d_g", "ln_d_b", "norm2_g", "final_g")
WEIGHTS = ("norm1_g", "w_in", "conv_a_w", "conv_a_b", "lru_wx", "lru_bx", "lru_wa", "lru_ba", "lru_lambda", "w_a_out",
           "conv_b_w", "w_b_out", "sinks", "w_c_out", "conv_d_w", "conv_d_b", "ln_d_g", "ln_d_b", "w_d_out", "w_o",
           "norm2_g", "w_ffn_gate", "w_ffn_up", "w_ffn_down", "final_g")


def kernel(x, norm1_g, w_in, conv_a_w, conv_a_b, lru_wx, lru_bx, lru_wa, lru_ba, lru_lambda, w_a_out, conv_b_w, w_b_out, sinks, w_c_out, conv_d_w, conv_d_b, ln_d_g, ln_d_b, w_d_out, w_o, norm2_g, w_ffn_gate, w_ffn_up, w_ffn_down, final_g, loss_target, m_norm1_g, m_w_in, m_conv_a_w, m_conv_a_b, m_lru_wx, m_lru_bx, m_lru_wa, m_lru_ba, m_lru_lambda, m_w_a_out, m_conv_b_w, m_w_b_out, m_sinks, m_w_c_out, m_conv_d_w, m_conv_d_b, m_ln_d_g, m_ln_d_b, m_w_d_out, m_w_o, m_norm2_g, m_w_ffn_gate, m_w_ffn_up, m_w_ffn_down, m_final_g, v_norm1_g, v_w_in, v_conv_a_w, v_conv_a_b, v_lru_wx, v_lru_bx, v_lru_wa, v_lru_ba, v_lru_lambda, v_w_a_out, v_conv_b_w, v_w_b_out, v_sinks, v_w_c_out, v_conv_d_w, v_conv_d_b, v_ln_d_g, v_ln_d_b, v_w_d_out, v_w_o, v_norm2_g, v_w_ffn_gate, v_w_ffn_up, v_w_ffn_down, v_final_g):
    args = dict(locals())
    w = {n: args[n] for n in WEIGHTS}
    m = {n: args["m_" + n] for n in WEIGHTS}
    v = {n: args["v_" + n] for n in WEIGHTS}
    me = _dev_index(*_mesh_pos())

    def rows_major(a, how):
        return jnp.swapaxes(a, 1, 2) if how == "view" else a

    turned = [k for k, (n, how) in BIG.items() if how == "transpose"]
    stacked = dict(zip(turned, cast_transpose([w[BIG[k][0]] for k in turned], "prep_transposed")))
    shards = [{k: stacked[k][l] if k in stacked else rows_major(w[n], how)[l].astype(BF16) for k, (n, how) in BIG.items()}
              for l in range(DEPTH)]
    plan = Overlap(shards, lax.axis_index("c").astype(jnp.int32).reshape(1))
    convs = jnp.pad(_stack_convs(w).reshape(DEPTH * CW_ROWS, BW // NDEV), ((0, 0), (0, 256 - BW // NDEV)))
    g_in0, g_conv = _comm_only(gather_job([(("in_t", 0), plan.shards[0]["in_t"], None, 0, plan.shards[0]["in_t"].shape[0]),
                                           ("convs", convs, None, 0, convs.shape[0])]), "gather_first")
    plan.gathered[0]["in_t"] = g_in0
    convw = g_conv[:, :BW // NDEV].reshape(NDEV, DEPTH, CW_ROWS, BW // NDEV).transpose(1, 2, 0, 3).reshape(DEPTH, CW_ROWS, BW)

    vecs = _stack_vecs(w)
    head_stats, grad_x, grads = local_step(x[0], loss_target[0], norm1_g, norm2_g, final_g, convw, vecs, lru_wx, lru_wa, plan)


    out = {}
    for slot, kinds in (("adamw_rest", [k for k in BIG if k != "in_t"]), ("adamw_in_t", ["in_t"])):
        job = plan.job(slot, 0)
        items = [(plan.contrib[k], *[rows_major(p[BIG[k][0]], BIG[k][1]) for p in (w, m, v)], BIG[k][1] == "transpose")
                 for k in kinds]
        results, cres = adamw_big(items, slot, comm=job)
        plan.done(slot, 0, cres)
        for k, res in zip(kinds, results):
            out[BIG[k][0]] = [rows_major(r, BIG[k][1]) for r in res]

    def own_shapes(p):
        return {n: p[n].reshape(1, D) if n == "final_g" else p[n] for n in SMALL}

    loss, small = adamw_small([g.reshape(NDEV, P_ROWS, D) for g in plan.small_gathered], me.astype(jnp.int32).reshape(1),
                              own_shapes(w), own_shapes(m), own_shapes(v))
    for n in SMALL:
        out[n] = [r.reshape(w[n].shape) for r in small[n]]
    loss = loss[0, 0]
    return (loss, grad_x[None], *[out[n][0] for n in WEIGHTS], *[out[n][1] for n in WEIGHTS],
            *[out[n][2] for n in WEIGHTS], *[out[n][3] for n in WEIGHTS])
```

```python
import functools

import jax
import jax.numpy as jnp
from jax import lax
from jax.experimental import pallas as pl
from jax.experimental.pallas import tpu as pltpu

F32 = jnp.float32
BF16 = jnp.bfloat16
E = pl.Element

D = 1024
BW = 512
IN_W = 8448
GL0 = 4352
FF = 2816
N_HEADS = 8
N_KV = 2
HD = 64
ATT_BLK = 128
EPS = 1e-6
LRU_C = 8.0
NEG_INF = -1e30
DEPTH = 2
NDEV = 8
CONV_A, CONV_B, CONV_D = 4, 3, 31
C_AX, C_AG, C_BV, C_BC, C_BB, C_Q, C_K, C_V, C_D1, C_D2 = 0, 512, 1024, 1536, 2048, 2560, 3072, 3200, 3328, 3840
CW_A, CW_B, CW_D, CW_ROWS = 0, 4, 8, 40
V_CAB, V_BX, V_BA, V_LAM, V_CDB, V_LNG, V_LNB, V_SINK, V_ROWS = 0, 1, 2, 3, 4, 5, 6, 7, 8
HALO = 32
W_IN_PARTS = ((0, 768), (768, 256))

ADAM_LR, ADAM_B1, ADAM_B2, ADAM_EPS, ADAM_WD, ADAM_STEP = 0.001, 0.9, 0.999, 1e-08, 0.01, 10

VMEM_LIMIT = 56 * 1024 * 1024

_NN = (((1,), (0,)), ((), ()))
_NT = (((1,), (1,)), ((), ()))
_TN = (((0,), (0,)), ((), ()))


def _dot(a, b, dims):
    return lax.dot_general(a.astype(BF16), b.astype(BF16), dims, preferred_element_type=F32)


def _cparams(n_axes):
    return pltpu.CompilerParams(dimension_semantics=("arbitrary",) * n_axes, vmem_limit_bytes=VMEM_LIMIT)


def _sds(shape, dtype):
    return jax.ShapeDtypeStruct(tuple(shape), dtype)


def _sigmoid(x):
    return jax.nn.sigmoid(x)


def _neg_expm1(x):
    p = x * (1.0 + x * (0.5 + x * (1.0 / 6.0 + x * (1.0 / 24.0 + x * (1.0 / 120.0)))))
    return jnp.where(x > -0.1, -p, 1.0 - jnp.exp(x))


def _softplus(z):
    return jnp.maximum(z, 0.0) + jnp.log1p(jnp.exp(-jnp.abs(z)))


def _gelu_and_grad(x):
    c = 0.7978845608028654
    inner = c * (x + 0.044715 * x * x * x)
    t = jnp.tanh(inner)
    g = 0.5 * x * (1.0 + t)
    dg = 0.5 * (1.0 + t) + 0.5 * x * (1.0 - t * t) * c * (1.0 + 3.0 * 0.044715 * x * x)
    return g, dg


ANY = pl.BlockSpec(memory_space=pl.ANY)
MESH = pl.DeviceIdType.MESH


def _mesh_pos():
    return lax.axis_index("x"), lax.axis_index("y"), lax.axis_index("c")


def _dev_index(px, py, pc):
    return 4 * px + 2 * py + pc


class CommJob:
    def __init__(self, inputs, aliases, out_shapes, sem_shapes, start, finish, relay=None):
        self.inputs, self.aliases, self.out_shapes, self.sem_shapes = list(inputs), dict(aliases), list(out_shapes), list(sem_shapes)
        self.start, self.finish, self.relay = start, finish, relay


def _call(body, comm, args, *, grid, in_specs, out_specs, out_shape, scratch_shapes=(), name, aliases=None):
    single = not isinstance(out_shape, (list, tuple))
    out_specs = [out_specs] if single else list(out_specs)
    out_shape = [out_shape] if single else list(out_shape)
    scratch_shapes = list(scratch_shapes)
    n_in, n_out, n_scr, n_axes = len(in_specs), len(out_shape), len(scratch_shapes), len(grid)
    params = pltpu.CompilerParams(dimension_semantics=("arbitrary",) * n_axes, vmem_limit_bytes=VMEM_LIMIT)
    io_aliases = dict(aliases or {})
    if comm is None:
        outs = pl.pallas_call(body, grid=grid, in_specs=in_specs, out_specs=out_specs, out_shape=out_shape,
                              scratch_shapes=scratch_shapes, input_output_aliases=io_aliases, compiler_params=params,
                              name=name)(*args)
        return (outs[0] if single else outs), []
    c_in, c_out = len(comm.inputs), len(comm.out_shapes)
    io_aliases.update({n_in + i: n_out + o for i, o in comm.aliases.items()})

    def wrapped(*refs):
        ins, cins = refs[:n_in], refs[n_in:n_in + c_in]
        outs = refs[n_in + c_in:n_in + c_in + n_out]
        couts = refs[n_in + c_in + n_out:n_in + c_in + n_out + c_out]
        rest = refs[n_in + c_in + n_out + c_out:]
        scr, sems = rest[:n_scr], rest[n_scr:]
        first = functools.reduce(lambda a, b: a & b, [pl.program_id(a) == 0 for a in range(n_axes)])
        last = functools.reduce(lambda a, b: a & b, [pl.program_id(a) == pl.num_programs(a) - 1 for a in range(n_axes)])

        @pl.when(first)
        def _():
            comm.start(cins, couts, sems)

        if comm.relay is not None:
            step = functools.reduce(lambda a, b: a * grid[b] + pl.program_id(b), range(1, n_axes), pl.program_id(0))
            n_steps = functools.reduce(lambda a, b: a * b, grid)

            @pl.when(step == 2 * n_steps // 3)
            def _():
                comm.relay(cins, couts, sems)

        body(*ins, *outs, *scr)

        @pl.when(last)
        def _():
            comm.finish(cins, couts, sems)

    outs = pl.pallas_call(
        wrapped, grid=grid, in_specs=list(in_specs) + [ANY] * c_in, out_specs=out_specs + [ANY] * c_out,
        out_shape=out_shape + comm.out_shapes, scratch_shapes=scratch_shapes + comm.sem_shapes,
        input_output_aliases=io_aliases, compiler_params=params, name=name)(*args, *comm.inputs)
    res, cres = outs[:n_out], outs[n_out:]
    return (res[0] if single else res), cres


def _comm_only(comm, name):
    c_in, c_out = len(comm.inputs), len(comm.out_shapes)

    def body(*refs):
        cins, couts, sems = refs[:c_in], refs[c_in:c_in + c_out], refs[c_in + c_out:]
        comm.start(cins, couts, sems)
        if comm.relay is not None:
            comm.relay(cins, couts, sems)
        comm.finish(cins, couts, sems)

    return pl.pallas_call(body, in_specs=[ANY] * c_in, out_specs=[ANY] * c_out, out_shape=comm.out_shapes,
                          scratch_shapes=comm.sem_shapes, input_output_aliases=comm.aliases, name=name)(*comm.inputs)


def gather_job(pieces):
    inputs, aliases, out_shapes, plan, where = [], {}, [], [], {}
    for key, shard, gathered, row0, nrows in pieces:
        if key not in where:
            where[key] = (len(inputs), len(out_shapes))
            inputs.append(shard)
            if gathered is not None:
                aliases[len(inputs)] = len(out_shapes)
                inputs.append(gathered)
            out_shapes.append(_sds((NDEV * shard.shape[0], shard.shape[1]), shard.dtype))
        plan.append((*where[key], shard.shape[0], row0, nrows))
    n = len(plan)

    def copies(cins, couts, sems):
        send_sems, recv_sems, local_sems = sems
        x, y, c = _mesh_pos()
        me, sibling = (x, y, c), (x, y, 1 - c)
        xn, yn, dg = (1 - x, y), (x, 1 - y), (1 - x, 1 - y)
        local, first, pass1, pass2, got_ici, got_fwd, got_d2d = [], [], [], [], [], [], []
        for p, (i_shard, i_out, rows, row0, nrows) in enumerate(plan):
            src = cins[i_shard].at[pl.ds(row0, nrows), :]
            half = cins[i_shard].shape[1] // 2
            left, right, whole = pl.ds(0, half), pl.ds(half, half), slice(None)

            def slot(dev, lanes, i_out=i_out, rows=rows, row0=row0, nrows=nrows):
                return couts[i_out].at[pl.ds(_dev_index(*dev) * rows + row0, nrows), lanes]

            def copy(g, dev, to, lanes=whole, src=None, p=p, slot=slot):
                return pltpu.make_async_remote_copy(
                    src_ref=slot(dev, lanes) if src is None else src, dst_ref=slot(dev, lanes),
                    send_sem=send_sems.at[g, p], recv_sem=recv_sems.at[g, p], device_id=to, device_id_type=MESH)

            local.append(pltpu.make_async_copy(src, slot(me, whole), local_sems.at[p]))
            first += [copy(0, me, sibling, src=src), copy(1, me, (*xn, c), src=src), copy(2, me, (*yn, c), src=src)]
            got_ici += [copy(1, (*xn, c), me), copy(2, (*yn, c), me)]
            pass1 += [copy(3, (*xn, c), (*yn, c), left), copy(4, (*yn, c), (*xn, c), right),
                      copy(5, (*xn, c), sibling), copy(6, (*yn, c), sibling)]
            got_fwd += [copy(3, (*dg, c), me, left), copy(4, (*dg, c), me, right)]
            pass2 += [copy(7, (*dg, c), sibling, left), copy(8, (*dg, c), sibling, right)]
            got_d2d += [copy(0, sibling, me), copy(5, (*xn, 1 - c), me), copy(6, (*yn, 1 - c), me),
                        copy(7, (*dg, 1 - c), me, left), copy(8, (*dg, 1 - c), me, right)]
        return local, first, pass1, pass2, got_ici, got_fwd, got_d2d

    def start(cins, couts, sems):
        local, first, *_ = copies(cins, couts, sems)
        for cp in local + first:
            cp.start()

    def pass_on(cins, couts, sems):
        _, _, pass1, _, got_ici, _, _ = copies(cins, couts, sems)
        for cp in got_ici:
            cp.wait_recv()
        for cp in pass1:
            cp.start()

    def finish(cins, couts, sems):
        local, first, pass1, pass2, _, got_fwd, got_d2d = copies(cins, couts, sems)
        for cp in got_fwd:
            cp.wait_recv()
        for cp in pass2:
            cp.start()
        for cp in got_d2d:
            cp.wait_recv()
        for cp in first + pass1 + pass2:
            cp.wait_send()
        for cp in local:
            cp.wait()

    sem_shapes = [pltpu.SemaphoreType.DMA((9, n)), pltpu.SemaphoreType.DMA((9, n)), pltpu.SemaphoreType.DMA((n,))]
    return CommJob(inputs, aliases, out_shapes, sem_shapes, start, finish, relay=pass_on)


def sibling_exchange_job(grads):
    n = len(grads)

    def copies(cins, couts, sems):
        send_sems, recv_sems = sems
        x, y, c = _mesh_pos()
        return [pltpu.make_async_remote_copy(
            src_ref=cins[q].at[:, 1 - c], dst_ref=couts[q], send_sem=send_sems.at[q], recv_sem=recv_sems.at[q],
            device_id=(x, y, 1 - c), device_id_type=MESH) for q in range(n)]

    def start(cins, couts, sems):
        for cp in copies(cins, couts, sems):
            cp.start()

    def finish(cins, couts, sems):
        cps = copies(cins, couts, sems)
        for cp in cps:
            cp.wait_recv()
        for cp in cps:
            cp.wait_send()

    return CommJob(grads, {}, [_sds((4,) + g.shape[2:], g.dtype) for g in grads],
                   [pltpu.SemaphoreType.DMA((n,)), pltpu.SemaphoreType.DMA((n,))], start, finish)


def chip_exchange_job(pieces):
    inputs, aliases, out_shapes, plan, where = [], {}, [], [], {}
    for partial, contrib, key, layer, row0, nrows, col0, cols in pieces:
        if key not in where:
            where[key] = len(out_shapes)
            out_shapes.append(_sds((4, DEPTH, partial.shape[1], cols), partial.dtype))
            if contrib is not None:
                aliases[len(inputs)] = where[key]
                inputs.append(contrib)
        plan.append((len(inputs), where[key], layer, row0, nrows, col0, partial.shape[2]))
        inputs.append(partial)
    n = len(plan)

    def copies(cins, couts, sems):
        send_sems, recv_sems, local_sems = sems
        x, y, c = _mesh_pos()
        mine = 2 * x + y
        local, sends, recvs = [], [], []
        for p, (i_in, i_out, layer, row0, nrows, col0, ncols) in enumerate(plan):
            rows, lanes = pl.ds(row0, nrows), pl.ds(col0, ncols)
            local.append(pltpu.make_async_copy(cins[i_in].at[mine, rows, :], couts[i_out].at[mine, layer, rows, lanes],
                                               local_sems.at[p]))
            for j, (cx, cy) in enumerate([(1 - x, y), (x, 1 - y), (1 - x, 1 - y)]):
                theirs = 2 * cx + cy

                def copy(slot_there, j=j, p=p, cx=cx, cy=cy, theirs=theirs, i_in=i_in, i_out=i_out, layer=layer,
                         rows=rows, lanes=lanes):
                    return pltpu.make_async_remote_copy(
                        src_ref=cins[i_in].at[theirs, rows, :], dst_ref=couts[i_out].at[slot_there, layer, rows, lanes],
                        send_sem=send_sems.at[j, p], recv_sem=recv_sems.at[j, p], device_id=(cx, cy, c), device_id_type=MESH)
                sends.append(copy(mine))
                recvs.append(copy(theirs))
        return local, sends, recvs

    def start(cins, couts, sems):
        local, sends, _ = copies(cins, couts, sems)
        for cp in local + sends:
            cp.start()

    def finish(cins, couts, sems):
        local, sends, recvs = copies(cins, couts, sems)
        for cp in recvs:
            cp.wait_recv()
        for cp in sends:
            cp.wait_send()
        for cp in local:
            cp.wait()

    sem_shapes = [pltpu.SemaphoreType.DMA((3, n)), pltpu.SemaphoreType.DMA((3, n)), pltpu.SemaphoreType.DMA((n,))]
    return CommJob(inputs, aliases, out_shapes, sem_shapes, start, finish)


def fwd_proj(x, g1, wt_in, l, comm=None):
    s = x.shape[0]
    tm = min(1024, s)
    tn = 1408

    def body(x_ref, g_ref, w_ref, o_ref, xn_ref):
        @pl.when(pl.program_id(1) == 0)
        def _():
            xv = x_ref[...]
            r = lax.rsqrt(jnp.mean(xv * xv, axis=-1, keepdims=True) + EPS)
            xn_ref[...] = (xv * r * g_ref[l:l + 1, :]).astype(BF16)

        o_ref[...] = _dot(xn_ref[...], w_ref[...], _NT).astype(BF16)

    return _call(
        body, comm, (x, g1, wt_in), grid=(s // tm, IN_W // tn),
        in_specs=[pl.BlockSpec((tm, D), lambda i, j: (i, 0)),
                  pl.BlockSpec((DEPTH, D), lambda i, j: (0, 0)),
                  pl.BlockSpec((tn, D), lambda i, j: (j, 0))],
        out_specs=pl.BlockSpec((tm, tn), lambda i, j: (i, j)),
        out_shape=_sds((s, IN_W), BF16),
        scratch_shapes=[pltpu.VMEM((tm, D), BF16)], name=f"fwd_proj{l}")


def _scan_fwd(a_ref, u_ref, h_ref, h0, n_rows):
    row = lax.broadcasted_iota(jnp.int32, (8, BW), 0)

    def body(g, hprev):
        r = pl.multiple_of(g * 8, 8)
        a = a_ref[pl.ds(r, 8), :]
        u = u_ref[pl.ds(r, 8), :]
        for sft in (1, 2, 4):
            a_sh = jnp.where(row >= sft, pltpu.roll(a, sft, 0), 1.0)
            u_sh = jnp.where(row >= sft, pltpu.roll(u, sft, 0), 0.0)
            u = u + a * u_sh
            a = a * a_sh
        h = u + a * hprev
        h_ref[pl.ds(r, 8), :] = h
        return h[7:8, :]

    return lax.fori_loop(0, n_rows // 8, body, h0)


def _scan_bwd(b_ref, g_ref, o_ref, c0, n_rows):
    row = lax.broadcasted_iota(jnp.int32, (8, BW), 0)

    def body(k, cnext):
        r = pl.multiple_of((n_rows // 8 - 1 - k) * 8, 8)
        b = b_ref[pl.ds(r, 8), :]
        g = g_ref[pl.ds(r, 8), :]
        for sft in (1, 2, 4):
            b_sh = jnp.where(row < 8 - sft, pltpu.roll(b, 8 - sft, 0), 1.0)
            g_sh = jnp.where(row < 8 - sft, pltpu.roll(g, 8 - sft, 0), 0.0)
            g = g + b * g_sh
            b = b * b_sh
        o = g + b * cnext
        o_ref[pl.ds(r, 8), :] = o
        return o[0:1, :]

    return lax.fori_loop(0, n_rows // 8, body, c0)


def _shifted_copies(buf, shifted, n_rows):
    for r in range(1, 8):
        shifted[r - 1, 0:n_rows - 8, :] = buf[pl.ds(r, n_rows - 8), :]


def _window(buf, shifted, off, t):
    r = off % 8
    return buf[pl.ds(off, t), :] if r == 0 else shifted[r - 1, pl.ds(off - r, t), :]


def _branch_fwd_math(cur_ref, halo_ref, cw_ref, vec_ref, wx_ref, wa_ref, bufa, bufb, bufd, xd, first, t, saved_ref=None):
    def halo(c0):
        v = halo_ref[:, c0:c0 + BW].astype(F32)
        return jnp.where(first, 0.0, v)

    def cur(c0):
        return cur_ref[:, c0:c0 + BW].astype(F32)

    out = {}
    bufa[0:HALO, :] = halo(C_AX)
    bufa[HALO:HALO + t, :] = cur(C_AX)
    ca = jnp.zeros((t, BW), F32) + vec_ref[V_CAB:V_CAB + 1, :]
    for k in range(CONV_A):
        ca = ca + cw_ref[CW_A + k:CW_A + k + 1, :] * bufa[pl.ds(HALO - (CONV_A - 1) + k, t), :]
    if saved_ref is None:
        gi = _sigmoid(_dot(ca, wx_ref[...], _NN) + vec_ref[V_BX:V_BX + 1, :])
        gr = _sigmoid(_dot(ca, wa_ref[...], _NN) + vec_ref[V_BA:V_BA + 1, :])
    else:
        gi, gr = saved_ref[:, BW:2 * BW], saved_ref[:, 2 * BW:3 * BW]
    sp = _softplus(-vec_ref[V_LAM:V_LAM + 1, :])
    la = -LRU_C * sp * gr
    a = jnp.exp(la)
    mult = jnp.sqrt(_neg_expm1(2.0 * la))
    out.update(ca=ca, gi=gi, gr=gr, sp=sp, a=a, mult=mult)
    bufb[0:HALO, :] = halo(C_BC) * halo(C_BV)
    bufb[HALO:HALO + t, :] = cur(C_BC) * cur(C_BV)
    cb = jnp.zeros((t, BW), F32)
    for k in range(CONV_B):
        cb = cb + cw_ref[CW_B + k:CW_B + k + 1, :] * bufb[pl.ds(HALO - (CONV_B - 1) + k, t), :]
    out.update(cb=cb)
    bufd[0:HALO, :] = halo(C_D1) * _sigmoid(halo(C_D2))
    s2 = _sigmoid(cur(C_D2))
    bufd[HALO:HALO + t, :] = cur(C_D1) * s2
    _shifted_copies(bufd, xd, t + HALO)
    if saved_ref is None:
        cd = jnp.zeros((t, BW), F32) + vec_ref[V_CDB:V_CDB + 1, :]
        for k in range(CONV_D):
            cd = cd + cw_ref[CW_D + k:CW_D + k + 1, :] * _window(bufd, xd, HALO - (CONV_D - 1) + k, t)
    else:
        cd = saved_ref[:, 0:BW]
    mu = jnp.mean(cd, axis=-1, keepdims=True)
    xc = cd - mu
    rstd = lax.rsqrt(jnp.mean(xc * xc, axis=-1, keepdims=True) + EPS)
    xh = xc * rstd
    ln = xh * vec_ref[V_LNG:V_LNG + 1, :] + vec_ref[V_LNB:V_LNB + 1, :]
    out.update(s2=s2, xh=xh, rstd=rstd, ln=ln, cd=cd)
    return out


def fwd_branch(proj, convw, vecs, wx_bd, wa_bd, l, comm=None):
    s = proj.shape[0]
    t = min(256, s)

    def body(cur_ref, halo_ref, cw_ref, vec_ref, wx_ref, wa_ref, pre_ref, h_ref, sv_ref, bufa, bufb, bufd, xd, a_s, u_s, hcar):
        first = pl.program_id(0) == 0

        @pl.when(first)
        def _():
            hcar[...] = jnp.zeros((1, BW), F32)

        v = _branch_fwd_math(cur_ref, halo_ref, cw_ref, vec_ref, wx_ref, wa_ref, bufa, bufb, bufd, xd, first, t)
        a_s[...] = v["a"]
        u_s[...] = v["ca"] * v["gi"] * v["mult"]
        sv_ref[:, 0:BW] = v["cd"]
        sv_ref[:, BW:2 * BW] = v["gi"]
        sv_ref[:, 2 * BW:3 * BW] = v["gr"]
        hcar[...] = _scan_fwd(a_s, u_s, h_ref, hcar[...], t)
        gg, _ = _gelu_and_grad(cur_ref[:, C_AG:C_AG + BW].astype(F32))
        pre_ref[:, 0:BW] = (h_ref[...] * gg).astype(BF16)
        pre_ref[:, BW:2 * BW] = (cur_ref[:, C_BB:C_BB + BW].astype(F32) * v["cb"]).astype(BF16)
        ln = v["ln"]
        pre_ref[:, 2 * BW:3 * BW] = (ln * _sigmoid(ln)).astype(BF16)

    hb = t // HALO
    return _call(
        body, comm, (proj, proj, convw, vecs, wx_bd, wa_bd), grid=(s // t,),
        in_specs=[pl.BlockSpec((t, GL0), lambda i: (i, 0)),
                  pl.BlockSpec((HALO, GL0), lambda i: (jnp.maximum(i * hb - 1, 0), 0)),
                  pl.BlockSpec((None, CW_ROWS, BW), lambda i: (l, 0, 0)),
                  pl.BlockSpec((None, V_ROWS, BW), lambda i: (l, 0, 0)),
                  pl.BlockSpec((None, BW, BW), lambda i: (l, 0, 0)),
                  pl.BlockSpec((None, BW, BW), lambda i: (l, 0, 0))],
        out_specs=[pl.BlockSpec((t, 3 * BW), lambda i: (i, 0)), pl.BlockSpec((t, BW), lambda i: (i, 0)),
                   pl.BlockSpec((t, 3 * BW), lambda i: (i, 0))],
        out_shape=[_sds((s, 3 * BW), BF16), _sds((s, BW), F32), _sds((s, 3 * BW), F32)],
        scratch_shapes=[pltpu.VMEM((t + HALO, BW), F32)] * 3 + [pltpu.VMEM((7, t + HALO - 8, BW), F32)]
        + [pltpu.VMEM((t, BW), F32)] * 2 + [pltpu.VMEM((1, BW), F32)],
        name=f"fwd_branch{l}")


GRP = N_HEADS // N_KV


ATT_SUB = 2


def _attn_mask_bias(first_block):
    shape = (GRP * ATT_BLK, 2 * ATT_BLK)
    qi = lax.broadcasted_iota(jnp.int32, shape, 0) & (ATT_BLK - 1)
    ki = lax.broadcasted_iota(jnp.int32, shape, 1)
    dist = qi + ATT_BLK - ki
    valid = (dist >= 0) & (dist < ATT_BLK)
    if first_block is not None:
        valid = valid & (jnp.logical_not(first_block) | (ki >= ATT_BLK))
    return dist.astype(F32), valid


def _attn_units(q_ref, kvp_ref, kvc_ref, first_step):
    units = []
    for b in range(ATT_SUB):
        rows = slice(b * ATT_BLK, (b + 1) * ATT_BLK)
        if b == 0:
            prev = lambda c0, c1: kvp_ref[:, c0:c1]
        else:
            prev = lambda c0, c1, b=b: kvc_ref[(b - 1) * ATT_BLK:b * ATT_BLK, c0:c1]
        for hk in range(N_KV):
            units.append(dict(b=b, hk=hk, rows=rows, q=lambda c0, c1, rows=rows: q_ref[rows, c0:c1], prev=prev,
                              cur=lambda c0, c1, rows=rows: kvc_ref[rows, c0:c1], first=first_step if b == 0 else None))
    return units


def _per_head(hk, values):
    hl = lax.broadcasted_iota(jnp.int32, (GRP * ATT_BLK, 1), 0) // ATT_BLK
    out = values[GRP - 1]
    for j in range(GRP - 2, -1, -1):
        out = jnp.where(hl == j, values[j], out)
    return out


def _attn_probs(units, vec_ref):
    us = range(len(units))
    heads = [range(u["hk"] * GRP, (u["hk"] + 1) * GRP) for u in units]
    masks = {id(u["first"]): _attn_mask_bias(u["first"]) for u in units}
    distf = [masks[id(u["first"])][0] for u in units]
    valid = [masks[id(u["first"])][1] for u in units]
    q4 = [jnp.concatenate([units[i]["q"](h * HD, (h + 1) * HD) for h in heads[i]], axis=0) for i in us]
    kcol = [(u["hk"] * HD, (u["hk"] + 1) * HD) for u in units]
    vcol = [((N_KV + u["hk"]) * HD, (N_KV + u["hk"] + 1) * HD) for u in units]
    k2 = [jnp.concatenate([units[i]["prev"](*kcol[i]), units[i]["cur"](*kcol[i])], axis=0) for i in us]
    v2 = [jnp.concatenate([units[i]["prev"](*vcol[i]), units[i]["cur"](*vcol[i])], axis=0) for i in us]
    slope = [_per_head(units[i]["hk"], [2.0 ** (-8.0 * (h + 1) / N_HEADS) for h in heads[i]]) for i in us]
    sink = [_per_head(units[i]["hk"], [vec_ref[V_SINK:V_SINK + 1, h:h + 1] for h in heads[i]]) for i in us]
    sc = [_dot(q4[i], k2[i], _NT) for i in us]
    sc = [jnp.where(valid[i], sc[i] * (HD ** -0.5) - slope[i] * distf[i], NEG_INF) for i in us]
    m = [jnp.maximum(jnp.max(sc[i], axis=-1, keepdims=True), sink[i]) for i in us]
    p = [jnp.exp(sc[i] - m[i]) for i in us]
    es = [jnp.exp(sink[i] - m[i]) for i in us]
    inv = [1.0 / (jnp.sum(p[i], axis=-1, keepdims=True) + es[i]) for i in us]
    return [(q4[i], k2[i], v2[i], p[i] * inv[i], es[i] * inv[i]) for i in us]


def fwd_attn(proj, vecs, l, comm=None):
    s = proj.shape[0]
    t = ATT_SUB * ATT_BLK

    def body(q_ref, kvp_ref, kvc_ref, vec_ref, o_ref):
        units = _attn_units(q_ref, kvp_ref, kvc_ref, pl.program_id(0) == 0)
        groups = _attn_probs(units, vec_ref)
        outs = [_dot(p, v2, _NN).astype(BF16) for _, _, v2, p, _ in groups]
        for u, out in zip(units, outs):
            for j in range(GRP):
                h = u["hk"] * GRP + j
                o_ref[u["rows"], h * HD:(h + 1) * HD] = out[j * ATT_BLK:(j + 1) * ATT_BLK]

    return _call(
        body, comm, (proj, proj, proj, vecs), grid=(s // t,),
        in_specs=[pl.BlockSpec((t, BW), lambda i: (i, C_Q // BW)),
                  pl.BlockSpec((ATT_BLK, 256), lambda i: (jnp.maximum(ATT_SUB * i - 1, 0), C_K // 256)),
                  pl.BlockSpec((t, 256), lambda i: (i, C_K // 256)),
                  pl.BlockSpec((None, V_ROWS, BW), lambda i: (l, 0, 0))],
        out_specs=pl.BlockSpec((t, BW), lambda i: (i, 0)),
        out_shape=_sds((s, BW), BF16), name=f"fwd_attn{l}")


def fwd_merge(x, proj, pre_abd, pre_c, wt_a, wt_b, wt_c, wt_d, w_o, l, comm=None):
    s = x.shape[0]
    tm = min(256, s)

    def body(x_ref, gl_ref, pabd_ref, pc_ref, wa_ref, wb_ref, wc_ref, wd_ref, wo_ref, y_ref, mg_ref, h1_ref):
        pres = (pabd_ref[:, 0:BW], pabd_ref[:, BW:2 * BW], pc_ref[...], pabd_ref[:, 2 * BW:3 * BW])
        merged = jnp.zeros((tm, D), F32)
        for k, (pre, w_ref) in enumerate(zip(pres, (wa_ref, wb_ref, wc_ref, wd_ref))):
            yk = _dot(pre, w_ref[...], _NT)
            y_ref[:, k * D:(k + 1) * D] = yk.astype(BF16)
            merged = merged + _sigmoid(gl_ref[:, k * D:(k + 1) * D].astype(F32)) * yk
        mg_ref[...] = merged.astype(BF16)
        h1_ref[...] = x_ref[...] + _dot(merged, wo_ref[...], _NN)

    wspec = pl.BlockSpec((D, BW), lambda i: (0, 0))
    return _call(
        body, comm, (x, proj, pre_abd, pre_c, wt_a, wt_b, wt_c, wt_d, w_o), grid=(s // tm,),
        in_specs=[pl.BlockSpec((tm, D), lambda i: (i, 0)),
                  pl.BlockSpec((E(tm), E(4 * D)), lambda i: (i * tm, GL0)),
                  pl.BlockSpec((tm, 3 * BW), lambda i: (i, 0)),
                  pl.BlockSpec((tm, BW), lambda i: (i, 0)),
                  wspec, wspec, wspec, wspec,
                  pl.BlockSpec((D, D), lambda i: (0, 0))],
        out_specs=[pl.BlockSpec((tm, 4 * D), lambda i: (i, 0)), pl.BlockSpec((tm, D), lambda i: (i, 0)),
                   pl.BlockSpec((tm, D), lambda i: (i, 0))],
        out_shape=[_sds((s, 4 * D), BF16), _sds((s, D), BF16), _sds((s, D), F32)], name=f"fwd_merge{l}")


def fwd_ffn(h1, g2, wt_gate, wt_up, w_down, l, comm=None):
    s = h1.shape[0]
    tm = min(512, s)
    fc = FF // 2

    def body(h_ref, g_ref, wg_ref, wu_ref, wd_ref, xo_ref, fg_ref, fu_ref, hn_ref, acc_ref):
        j = pl.program_id(1)

        @pl.when(j == 0)
        def _():
            hv = h_ref[...]
            r = lax.rsqrt(jnp.mean(hv * hv, axis=-1, keepdims=True) + EPS)
            hn_ref[...] = (hv * r * g_ref[l:l + 1, :]).astype(BF16)
            acc_ref[...] = hv

        fg = _dot(hn_ref[...], wg_ref[...], _NT)
        fu = _dot(hn_ref[...], wu_ref[...], _NT)
        fg_ref[...] = fg.astype(BF16)
        fu_ref[...] = fu.astype(BF16)
        acc_ref[...] += _dot(fg * _sigmoid(fg) * fu, wd_ref[...], _NN)

        @pl.when(j == pl.num_programs(1) - 1)
        def _():
            xo_ref[...] = acc_ref[...]

    wspec = pl.BlockSpec((fc, D), lambda i, j: (j, 0))
    return _call(
        body, comm, (h1, g2, wt_gate, wt_up, w_down), grid=(s // tm, FF // fc),
        in_specs=[pl.BlockSpec((tm, D), lambda i, j: (i, 0)), pl.BlockSpec((DEPTH, D), lambda i, j: (0, 0)),
                  wspec, wspec, wspec],
        out_specs=[pl.BlockSpec((tm, D), lambda i, j: (i, 0)), pl.BlockSpec((tm, fc), lambda i, j: (i, j)),
                   pl.BlockSpec((tm, fc), lambda i, j: (i, j))],
        out_shape=[_sds((s, D), F32), _sds((s, FF), BF16), _sds((s, FF), BF16)],
        scratch_shapes=[pltpu.VMEM((tm, D), BF16), pltpu.VMEM((tm, D), F32)], name=f"fwd_ffn{l}")


def loss_head(x, gf, target):
    s = x.shape[0]
    tm = min(512, s)

    def body(x_ref, g_ref, t_ref, dx_ref, st_ref):
        @pl.when(pl.program_id(0) == 0)
        def _():
            st_ref[...] = jnp.zeros((8, D), F32)

        xv = x_ref[...]
        g = g_ref[...]
        r = lax.rsqrt(jnp.mean(xv * xv, axis=-1, keepdims=True) + EPS)
        n = xv * r
        err = n * g - t_ref[...]
        dy = err * (1.0 / D)
        dn = dy * g
        dx_ref[...] = r * (dn - n * jnp.mean(dn * n, axis=-1, keepdims=True))
        st_ref[0:1, :] += jnp.sum(dy * n, axis=0, keepdims=True)
        lsum = 0.5 * jnp.sum(jnp.mean(err * err, axis=-1, keepdims=True), axis=0, keepdims=True)
        st_ref[1:2, :] += jnp.broadcast_to(lsum, (1, D))

    return pl.pallas_call(
        body, grid=(s // tm,),
        in_specs=[pl.BlockSpec((tm, D), lambda i: (i, 0)), pl.BlockSpec((1, D), lambda i: (0, 0)),
                  pl.BlockSpec((tm, D), lambda i: (i, 0))],
        out_specs=[pl.BlockSpec((tm, D), lambda i: (i, 0)), pl.BlockSpec((8, D), lambda i: (0, 0))],
        out_shape=[_sds((s, D), F32), _sds((8, D), F32)],
        compiler_params=_cparams(1), name="loss_head")(x, gf, target)


def _edge_index(j, i, n_j, n_i):
    return jnp.where((j == 0) | (j == n_j - 1), i, n_i - 1)


def bwd_ffn(dxo, h1, fg, fu, g2, wt_gate, wt_up, w_down, l, comm=None):
    s = h1.shape[0]
    tm = min(512, s)
    fc = 256
    n_j, n_i = FF // fc, s // tm

    def body(dxo_ref, h_ref, fg_ref, fu_ref, g_ref, wg_ref, wu_ref, wd_ref,
             dh_ref, dwg_ref, dwu_ref, dwd_ref, st_ref, dhn, dxo_b, hn_b, ag, au, ad):
        j, i = pl.program_id(0), pl.program_id(1)
        rows = pl.ds(pl.multiple_of(i * tm, tm), tm)
        g = g_ref[l:l + 1, :]

        @pl.when(j == 0)
        def _():
            hv = h_ref[...]
            r = lax.rsqrt(jnp.mean(hv * hv, axis=-1, keepdims=True) + EPS)
            hn_b[rows, :] = (hv * r * g).astype(BF16)
            dxo_b[rows, :] = dxo_ref[...].astype(BF16)
            dhn[rows, :] = jnp.zeros((tm, D), F32)

        @pl.when((j == 0) & (i == 0))
        def _():
            st_ref[...] = jnp.zeros((8, D), F32)

        @pl.when(i == 0)
        def _():
            ag[...] = jnp.zeros((fc, D), F32)
            au[...] = jnp.zeros((fc, D), F32)
            ad[...] = jnp.zeros((fc, D), F32)

        fgv = fg_ref[...].astype(F32)
        fuv = fu_ref[...].astype(F32)
        sg = _sigmoid(fgv)
        sil = fgv * sg
        dxb = dxo_b[rows, :]
        hnb = hn_b[rows, :]
        d_act = _dot(dxb, wd_ref[...], _NT)
        ad[...] += _dot(sil * fuv, dxb, _TN)
        d_fg = (d_act * fuv * (sg * (1.0 + fgv * (1.0 - sg)))).astype(BF16)
        d_fu = (d_act * sil).astype(BF16)
        ag[...] += _dot(d_fg, hnb, _TN)
        au[...] += _dot(d_fu, hnb, _TN)
        dhn[rows, :] += _dot(d_fg, wg_ref[...], _NN) + _dot(d_fu, wu_ref[...], _NN)

        @pl.when(i == n_i - 1)
        def _():
            dwg_ref[...] = ag[...].astype(BF16)
            dwu_ref[...] = au[...].astype(BF16)
            dwd_ref[...] = ad[...].astype(BF16)

        @pl.when(j == n_j - 1)
        def _():
            hv = h_ref[...]
            r = lax.rsqrt(jnp.mean(hv * hv, axis=-1, keepdims=True) + EPS)
            n = hv * r
            dv = dhn[rows, :]
            dn = dv * g
            dh_ref[...] = dxo_ref[...] + r * (dn - n * jnp.mean(dn * n, axis=-1, keepdims=True))
            st_ref[0:1, :] += jnp.sum(dv * n, axis=0, keepdims=True)

    edge = lambda j, i: (_edge_index(j, i, n_j, n_i), 0)
    wspec = pl.BlockSpec((fc, D), lambda j, i: (j, 0))
    dwspec = pl.BlockSpec((fc, D), lambda j, i: (j, 0))
    return _call(
        body, comm, (dxo, h1, fg, fu, g2, wt_gate, wt_up, w_down), grid=(n_j, n_i),
        in_specs=[pl.BlockSpec((tm, D), edge),
                  pl.BlockSpec((tm, D), edge),
                  pl.BlockSpec((tm, fc), lambda j, i: (i, j)), pl.BlockSpec((tm, fc), lambda j, i: (i, j)),
                  pl.BlockSpec((DEPTH, D), lambda j, i: (0, 0)), wspec, wspec, wspec],
        out_specs=[pl.BlockSpec((tm, D), lambda j, i: (jnp.where(j == n_j - 1, i, 0), 0)),
                   dwspec, dwspec, dwspec, pl.BlockSpec((8, D), lambda j, i: (0, 0))],
        out_shape=[_sds((s, D), F32), _sds((FF, D), BF16), _sds((FF, D), BF16), _sds((FF, D), BF16), _sds((8, D), F32)],
        scratch_shapes=[pltpu.VMEM((s, D), F32), pltpu.VMEM((s, D), BF16), pltpu.VMEM((s, D), BF16),
                        pltpu.VMEM((fc, D), F32), pltpu.VMEM((fc, D), F32), pltpu.VMEM((fc, D), F32)],
        name=f"bwd_ffn{l}")


def bwd_merge(dh1, y4, proj, merged, pre_abd, pre_c, wt_a, wt_b, wt_c, wt_d, w_o, l, comm=None):
    s = dh1.shape[0]
    tm = min(256, s)
    n_i = s // tm

    def body(dh_ref, y_ref, gl_ref, mg_ref, pabd_ref, pc_ref, wa_ref, wb_ref, wc_ref, wd_ref, wo_ref,
             dgl_ref, dpre_ref, dwo_ref, dwa_ref, dwb_ref, dwc_ref, dwd_ref, ao, aa, ab, ac, ad):
        i = pl.program_id(0)
        accs = (aa, ab, ac, ad)

        @pl.when(i == 0)
        def _():
            ao[...] = jnp.zeros((D, D), F32)
            for acc in accs:
                acc[...] = jnp.zeros((D, BW), F32)

        dhb = dh_ref[...].astype(BF16)
        dmg = _dot(dhb, wo_ref[...], _NT)
        ao[...] += _dot(mg_ref[...], dhb, _TN)
        pres = (pabd_ref[:, 0:BW], pabd_ref[:, BW:2 * BW], pc_ref[...], pabd_ref[:, 2 * BW:3 * BW])
        for k, (pre, w_ref, acc) in enumerate(zip(pres, (wa_ref, wb_ref, wc_ref, wd_ref), accs)):
            gk = _sigmoid(gl_ref[:, k * D:(k + 1) * D].astype(F32))
            yk = y_ref[:, k * D:(k + 1) * D].astype(F32)
            dgl_ref[:, k * D:(k + 1) * D] = (dmg * yk * gk * (1.0 - gk)).astype(BF16)
            dyk = (dmg * gk).astype(BF16)
            dpre_ref[:, k * BW:(k + 1) * BW] = _dot(dyk, w_ref[...], _NN).astype(BF16)
            acc[...] += _dot(dyk, pre, _TN)

        @pl.when(i == n_i - 1)
        def _():
            dwo_ref[...] = ao[...].astype(BF16)
            for o_ref, acc in zip((dwa_ref, dwb_ref, dwc_ref, dwd_ref), accs):
                o_ref[...] = acc[...].astype(BF16)

    wspec = pl.BlockSpec((D, BW), lambda i: (0, 0))
    dwspec = pl.BlockSpec((D, BW), lambda i: (0, 0))
    return _call(
        body, comm, (dh1, y4, proj, merged, pre_abd, pre_c, wt_a, wt_b, wt_c, wt_d, w_o), grid=(n_i,),
        in_specs=[pl.BlockSpec((tm, D), lambda i: (i, 0)),
                  pl.BlockSpec((tm, 4 * D), lambda i: (i, 0)),
                  pl.BlockSpec((E(tm), E(4 * D)), lambda i: (i * tm, GL0)),
                  pl.BlockSpec((tm, D), lambda i: (i, 0)),
                  pl.BlockSpec((tm, 3 * BW), lambda i: (i, 0)),
                  pl.BlockSpec((tm, BW), lambda i: (i, 0)),
                  wspec, wspec, wspec, wspec,
                  pl.BlockSpec((D, D), lambda i: (0, 0))],
        out_specs=[pl.BlockSpec((E(tm), E(4 * D)), lambda i: (i * tm, GL0)),
                   pl.BlockSpec((tm, 4 * BW), lambda i: (i, 0)),
                   pl.BlockSpec((D, D), lambda i: (0, 0)), dwspec, dwspec, dwspec, dwspec],
        out_shape=[_sds((s, IN_W), BF16), _sds((s, 4 * BW), BF16), _sds((D, D), BF16)] + [_sds((D, BW), BF16)] * 4,
        scratch_shapes=[pltpu.VMEM((D, D), F32)] + [pltpu.VMEM((D, BW), F32)] * 4, name=f"bwd_merge{l}")


def bwd_attn(proj, dpre, vecs, l, comm=None):
    s = proj.shape[0]
    t = ATT_SUB * ATT_BLK
    grp = N_HEADS // N_KV

    def body(q_ref, kvp_ref, kvc_ref, do_ref, vec_ref, dq_ref, dkc_ref, dkp_ref, st_ref):
        @pl.when(pl.program_id(0) == 0)
        def _():
            st_ref[...] = jnp.zeros((8, 128), F32)

        lane = lax.broadcasted_iota(jnp.int32, (1, 128), 1)
        dsink = jnp.zeros((1, 128), F32)
        units = _attn_units(q_ref, kvp_ref, kvc_ref, pl.program_id(0) == 0)
        groups = _attn_probs(units, vec_ref)
        us = range(len(units))
        do4s = [jnp.concatenate([do_ref[u["rows"], h * HD:(h + 1) * HD] for h in range(u["hk"] * grp, (u["hk"] + 1) * grp)],
                                axis=0) for u in units]
        dps = [_dot(do4s[i], groups[i][2], _NT) for i in us]
        deltas = [jnp.sum(groups[i][3] * dps[i], axis=-1, keepdims=True) for i in us]
        dss = [groups[i][3] * (dps[i] - deltas[i]) * (HD ** -0.5) for i in us]
        dq4s = [_dot(dss[i], groups[i][1], _NN).astype(BF16) for i in us]
        dk2s = [_dot(dss[i], groups[i][0], _TN) for i in us]
        dv2s = [_dot(groups[i][3], do4s[i], _TN) for i in us]
        for i, u in enumerate(units):
            psd = groups[i][4] * deltas[i]
            for j in range(grp):
                h = u["hk"] * grp + j
                rows = slice(j * ATT_BLK, (j + 1) * ATT_BLK)
                dq_ref[u["rows"], h * HD:(h + 1) * HD] = dq4s[i][rows]
                dsink = dsink + jnp.where(lane == h, -jnp.sum(psd[rows], axis=0, keepdims=True), 0.0)
        for i, u in enumerate(units):
            nxt = [k for k, w in enumerate(units) if w["hk"] == u["hk"] and w["b"] == u["b"] + 1]
            for grad, c0 in ((dk2s, u["hk"] * HD), (dv2s, (N_KV + u["hk"]) * HD)):
                own = grad[i][ATT_BLK:]
                if nxt:
                    own = own + grad[nxt[0]][0:ATT_BLK]
                dkc_ref[u["rows"], c0:c0 + HD] = own.astype(BF16)
                if u["b"] == 0:
                    dkp_ref[:, c0:c0 + HD] = grad[i][0:ATT_BLK].astype(BF16)
        st_ref[0:1, :] += dsink

    return _call(
        body, comm, (proj, proj, proj, dpre, vecs), grid=(s // t,),
        in_specs=[pl.BlockSpec((t, BW), lambda i: (i, C_Q // BW)),
                  pl.BlockSpec((ATT_BLK, 256), lambda i: (jnp.maximum(ATT_SUB * i - 1, 0), C_K // 256)),
                  pl.BlockSpec((t, 256), lambda i: (i, C_K // 256)),
                  pl.BlockSpec((t, BW), lambda i: (i, 2)),
                  pl.BlockSpec((None, V_ROWS, BW), lambda i: (l, 0, 0))],
        out_specs=[pl.BlockSpec((t, BW), lambda i: (i, 0)), pl.BlockSpec((t, 256), lambda i: (i, 0)),
                   pl.BlockSpec((ATT_BLK, 256), lambda i: (i, 0)), pl.BlockSpec((8, 128), lambda i: (0, 0))],
        out_shape=[_sds((s, BW), BF16), _sds((s, 256), BF16), _sds((s // ATT_SUB, 256), BF16), _sds((8, 128), F32)],
        name=f"bwd_attn{l}")


def bwd_branch(proj, dproj, dpre, h, saved, dq, dkc, dkp, convw, vecs, wx_bd, wa_bd, l, comm=None):
    s = proj.shape[0]
    t = 2 * ATT_BLK
    nt = s // t
    nb = s // ATT_BLK
    hb = t // HALO

    def body(cur_ref, halo_ref, dpre_ref, h_ref, hp_ref, dq_ref, dkc_ref, dkp_ref,
             cw_ref, vec_ref, wx_ref, wa_ref, sv_ref, dproj_in, dp_ref, dcw_ref, dvec_ref, dwx_ref, dwa_ref,
             bufa, bufb, bufd, xd, xg, a_ext, hbuf, b_s, g_s, dh_s, ga, gb, gd, dhcar):
        del dproj_in
        step = pl.program_id(0)
        ti = nt - 1 - step
        first = ti == 0

        @pl.when(step == 0)
        def _():
            dcw_ref[...] = jnp.zeros((CW_ROWS, BW), F32)
            dvec_ref[...] = jnp.zeros((V_ROWS, BW), F32)
            dwx_ref[...] = jnp.zeros((BW, BW), F32)
            dwa_ref[...] = jnp.zeros((BW, BW), F32)
            dhcar[...] = jnp.zeros((1, BW), F32)
            a_ext[t:t + 8, :] = jnp.zeros((8, BW), F32)
            ga[t:t + 8, :] = jnp.zeros((8, BW), F32)
            gb[t:t + 8, :] = jnp.zeros((8, BW), F32)
            gd[t:t + HALO, :] = jnp.zeros((HALO, BW), F32)

        def cur(c0):
            return cur_ref[:, c0:c0 + BW].astype(F32)

        def rsum(v):
            return jnp.sum(v, axis=0, keepdims=True)

        def put(c0, v):
            dp_ref[:, c0:c0 + BW] = v.astype(BF16)

        v = _branch_fwd_math(cur_ref, halo_ref, cw_ref, vec_ref, wx_ref, wa_ref, bufa, bufb, bufd, xd, first, t, sv_ref)
        ca, gi, gr, sp, a, mult = v["ca"], v["gi"], v["gr"], v["sp"], v["a"], v["mult"]
        dpa = dpre_ref[:, 0:BW].astype(F32)
        gg, dgg = _gelu_and_grad(cur(C_AG))
        hv = h_ref[...]
        put(C_AG, dpa * hv * dgg)
        a_ext[0:t, :] = a
        b_s[...] = a_ext[pl.ds(1, t), :]
        g_s[...] = dpa * gg
        dhcar[...] = _scan_bwd(b_s, g_s, dh_s, dhcar[...], t)
        a_ext[t:t + 1, :] = a[0:1, :]
        dh = dh_s[...]
        hbuf[0:8, :] = jnp.where(first, 0.0, hp_ref[...])
        hbuf[8:8 + t, :] = hv
        da = dh * hbuf[pl.ds(7, t), :]
        d_ca = dh * gi * mult
        d_gi = dh * ca * mult
        d_mult = dh * ca * gi
        d_la = da * a - d_mult * (a * a) / mult
        lam = vec_ref[V_LAM:V_LAM + 1, :]
        dvec_ref[V_LAM:V_LAM + 1, :] += rsum(d_la * gr) * (LRU_C * _sigmoid(-lam))
        d_gr = d_la * (-LRU_C * sp)
        d_zr = d_gr * gr * (1.0 - gr)
        d_zi = d_gi * gi * (1.0 - gi)
        dvec_ref[V_BA:V_BA + 1, :] += rsum(d_zr)
        dvec_ref[V_BX:V_BX + 1, :] += rsum(d_zi)
        dwa_ref[...] += _dot(ca, d_zr, _TN)
        dwx_ref[...] += _dot(ca, d_zi, _TN)
        d_ca = d_ca + _dot(d_zi, wx_ref[...], _NT) + _dot(d_zr, wa_ref[...], _NT)
        dvec_ref[V_CAB:V_CAB + 1, :] += rsum(d_ca)
        ga[0:t, :] = d_ca
        d_ax = jnp.zeros((t, BW), F32)
        for k in range(CONV_A):
            d_ax = d_ax + cw_ref[CW_A + k:CW_A + k + 1, :] * ga[pl.ds(CONV_A - 1 - k, t), :]
            dcw_ref[CW_A + k:CW_A + k + 1, :] += rsum(d_ca * bufa[pl.ds(HALO - (CONV_A - 1) + k, t), :])
        ga[t:t + 8, :] = d_ca[0:8, :]
        put(C_AX, d_ax)
        dpb = dpre_ref[:, BW:2 * BW].astype(F32)
        put(C_BB, dpb * v["cb"])
        d_cb = dpb * cur(C_BB)
        gb[0:t, :] = d_cb
        d_cbin = jnp.zeros((t, BW), F32)
        for k in range(CONV_B):
            d_cbin = d_cbin + cw_ref[CW_B + k:CW_B + k + 1, :] * gb[pl.ds(CONV_B - 1 - k, t), :]
            dcw_ref[CW_B + k:CW_B + k + 1, :] += rsum(d_cb * bufb[pl.ds(HALO - (CONV_B - 1) + k, t), :])
        gb[t:t + 8, :] = d_cb[0:8, :]
        put(C_BC, d_cbin * cur(C_BV))
        put(C_BV, d_cbin * cur(C_BC))
        dpd = dpre_ref[:, 3 * BW:4 * BW].astype(F32)
        ln, xh, rstd, s2 = v["ln"], v["xh"], v["rstd"], v["s2"]
        sg = _sigmoid(ln)
        d_ln = dpd * sg * (1.0 + ln * (1.0 - sg))
        dvec_ref[V_LNG:V_LNG + 1, :] += rsum(d_ln * xh)
        dvec_ref[V_LNB:V_LNB + 1, :] += rsum(d_ln)
        d_xh = d_ln * vec_ref[V_LNG:V_LNG + 1, :]
        d_cd = rstd * (d_xh - jnp.mean(d_xh, axis=-1, keepdims=True)
                       - xh * jnp.mean(d_xh * xh, axis=-1, keepdims=True))
        dvec_ref[V_CDB:V_CDB + 1, :] += rsum(d_cd)
        gd[0:t, :] = d_cd
        _shifted_copies(gd, xg, t + HALO)
        d_dg = jnp.zeros((t, BW), F32)
        for k in range(CONV_D):
            d_dg = d_dg + cw_ref[CW_D + k:CW_D + k + 1, :] * _window(gd, xg, CONV_D - 1 - k, t)
            dcw_ref[CW_D + k:CW_D + k + 1, :] += rsum(d_cd * _window(bufd, xd, HALO - (CONV_D - 1) + k, t))
        gd[t:t + HALO, :] = d_cd[0:HALO, :]
        put(C_D1, d_dg * s2)
        put(C_D2, d_dg * cur(C_D1) * s2 * (1.0 - s2))
        dp_ref[:, C_Q:C_Q + BW] = dq_ref[...]
        dkp = jnp.where(step == 0, 0.0, dkp_ref[...].astype(F32))
        dp_ref[0:t - ATT_BLK, C_K:C_K + 256] = dkc_ref[0:t - ATT_BLK, :]
        dp_ref[t - ATT_BLK:t, C_K:C_K + 256] = (dkc_ref[t - ATT_BLK:t, :].astype(F32) + dkp).astype(BF16)

    rev = lambda i: nt - 1 - i
    full = lambda r, c: pl.BlockSpec((r, c), lambda i: (0, 0))
    return _call(
        body, comm, (proj, proj, dpre, h, h, dq, dkc, dkp, convw, vecs, wx_bd, wa_bd, saved, dproj), grid=(nt,),
        in_specs=[pl.BlockSpec((t, GL0), lambda i: (rev(i), 0)),
                  pl.BlockSpec((HALO, GL0), lambda i: (jnp.maximum(rev(i) * hb - 1, 0), 0)),
                  pl.BlockSpec((t, 4 * BW), lambda i: (rev(i), 0)),
                  pl.BlockSpec((t, BW), lambda i: (rev(i), 0)),
                  pl.BlockSpec((8, BW), lambda i: (jnp.maximum(rev(i) * (t // 8) - 1, 0), 0)),
                  pl.BlockSpec((t, BW), lambda i: (rev(i), 0)),
                  pl.BlockSpec((t, 256), lambda i: (rev(i), 0)),
                  pl.BlockSpec((ATT_BLK, 256), lambda i: (jnp.minimum(rev(i) + 1, nt - 1), 0)),
                  pl.BlockSpec((None, CW_ROWS, BW), lambda i: (l, 0, 0)),
                  pl.BlockSpec((None, V_ROWS, BW), lambda i: (l, 0, 0)),
                  pl.BlockSpec((None, BW, BW), lambda i: (l, 0, 0)),
                  pl.BlockSpec((None, BW, BW), lambda i: (l, 0, 0)),
                  pl.BlockSpec((t, 3 * BW), lambda i: (rev(i), 0)),
                  pl.BlockSpec(memory_space=pl.ANY)],
        out_specs=[pl.BlockSpec((t, GL0), lambda i: (rev(i), 0)),
                   full(CW_ROWS, BW), full(V_ROWS, BW), full(BW, BW), full(BW, BW)],
        out_shape=[_sds((s, IN_W), BF16), _sds((CW_ROWS, BW), F32), _sds((V_ROWS, BW), F32),
                   _sds((BW, BW), F32), _sds((BW, BW), F32)],
        scratch_shapes=[pltpu.VMEM((t + HALO, BW), F32)] * 3 + [pltpu.VMEM((7, t + HALO - 8, BW), F32)] * 2
        + [pltpu.VMEM((t + 8, BW), F32), pltpu.VMEM((t + 8, BW), F32)]
        + [pltpu.VMEM((t, BW), F32)] * 3
        + [pltpu.VMEM((t + 8, BW), F32), pltpu.VMEM((t + 8, BW), F32), pltpu.VMEM((t + HALO, BW), F32),
           pltpu.VMEM((1, BW), F32)],
        aliases={13: 0}, name=f"bwd_branch{l}")


def bwd_proj(dproj, x, dh1, g1, wt_in, l, comm=None):
    s = x.shape[0]
    tm = min(512, s)
    ck = 1408
    n_j, n_i = IN_W // ck, s // tm

    def body(dp_ref, x_ref, dh_ref, g_ref, w_ref, dx_ref, dw_ref, st_ref, dxn, xn_b, acc):
        j, i = pl.program_id(0), pl.program_id(1)
        rows = pl.ds(pl.multiple_of(i * tm, tm), tm)
        g = g_ref[l:l + 1, :]

        @pl.when(j == 0)
        def _():
            xv = x_ref[...]
            r = lax.rsqrt(jnp.mean(xv * xv, axis=-1, keepdims=True) + EPS)
            xn_b[rows, :] = (xv * r * g).astype(BF16)
            dxn[rows, :] = jnp.zeros((tm, D), F32)

        @pl.when((j == 0) & (i == 0))
        def _():
            st_ref[...] = jnp.zeros((8, D), F32)

        @pl.when(i == 0)
        def _():
            acc[...] = jnp.zeros((ck, D), F32)

        dp = dp_ref[...]
        dxn[rows, :] += _dot(dp, w_ref[...], _NN)
        acc[...] += _dot(dp, xn_b[rows, :], _TN)

        @pl.when(i == n_i - 1)
        def _():
            dw_ref[...] = acc[...].astype(BF16)

        @pl.when(j == n_j - 1)
        def _():
            xv = x_ref[...]
            r = lax.rsqrt(jnp.mean(xv * xv, axis=-1, keepdims=True) + EPS)
            n = xv * r
            dv = dxn[rows, :]
            dn = dv * g
            dx_ref[...] = dh_ref[...] + r * (dn - n * jnp.mean(dn * n, axis=-1, keepdims=True))
            st_ref[0:1, :] += jnp.sum(dv * n, axis=0, keepdims=True)

    lastrow = lambda j, i: (jnp.where(j == n_j - 1, i, 0), 0)
    return _call(
        body, comm, (dproj, x, dh1, g1, wt_in), grid=(n_j, n_i),
        in_specs=[pl.BlockSpec((tm, ck), lambda j, i: (i, j)),
                  pl.BlockSpec((tm, D), lambda j, i: (_edge_index(j, i, n_j, n_i), 0)),
                  pl.BlockSpec((tm, D), lastrow),
                  pl.BlockSpec((DEPTH, D), lambda j, i: (0, 0)),
                  pl.BlockSpec((ck, D), lambda j, i: (j, 0))],
        out_specs=[pl.BlockSpec((tm, D), lastrow), pl.BlockSpec((ck, D), lambda j, i: (j, 0)),
                   pl.BlockSpec((8, D), lambda j, i: (0, 0))],
        out_shape=[_sds((s, D), F32), _sds((IN_W, D), BF16), _sds((8, D), F32)],
        scratch_shapes=[pltpu.VMEM((s, D), F32), pltpu.VMEM((s, D), BF16), pltpu.VMEM((ck, D), F32)],
        name=f"bwd_proj{l}")


def bwd_proj_w(dproj, x, g1, l, half, comm=None):
    s = x.shape[0]
    tm = min(1024, s)
    ck = 1408
    c0, hw = W_IN_PARTS[half]
    n_j, n_i = IN_W // ck, s // tm

    def body(dp_ref, x_ref, g_ref, dw_ref, xn_b, acc):
        j, i = pl.program_id(0), pl.program_id(1)
        rows = pl.ds(pl.multiple_of(i * tm, tm), tm)

        @pl.when(j == 0)
        def _():
            xv = x_ref[...]
            r = lax.rsqrt(jnp.mean(xv * xv, axis=-1, keepdims=True) + EPS)
            xn_b[rows, :] = (xv * r * g_ref[l:l + 1, :])[:, c0:c0 + hw].astype(BF16)

        @pl.when(i == 0)
        def _():
            acc[...] = jnp.zeros((ck, hw), F32)

        acc[...] += _dot(dp_ref[...], xn_b[rows, :], _TN)

        @pl.when(i == n_i - 1)
        def _():
            dw_ref[...] = acc[...].astype(BF16)

    return _call(
        body, comm, (dproj, x, g1), grid=(n_j, n_i),
        in_specs=[pl.BlockSpec((tm, ck), lambda j, i: (i, j)),
                  pl.BlockSpec((tm, D), lambda j, i: (jnp.where(j == 0, i, n_i - 1), 0)),
                  pl.BlockSpec((DEPTH, D), lambda j, i: (0, 0))],
        out_specs=pl.BlockSpec((ck, hw), lambda j, i: (j, 0)),
        out_shape=_sds((IN_W, hw), BF16),
        scratch_shapes=[pltpu.VMEM((s, hw), BF16), pltpu.VMEM((ck, hw), F32)],
        name=f"bwd_proj_w{half}_{l}")


def bwd_proj_x(dproj, x, dh1, g1, wt_in, l, comm=None):
    s = x.shape[0]
    tm = min(512, s)
    ck = 1408
    n_j, n_i = IN_W // ck, s // tm

    def body(dp_ref, x_ref, dh_ref, g_ref, w_ref, dx_ref, st_ref, dxn):
        j, i = pl.program_id(0), pl.program_id(1)
        rows = pl.ds(pl.multiple_of(i * tm, tm), tm)
        g = g_ref[l:l + 1, :]

        @pl.when((j == 0) & (i == 0))
        def _():
            st_ref[...] = jnp.zeros((8, D), F32)

        part = _dot(dp_ref[...], w_ref[...], _NN)

        @pl.when(j == 0)
        def _():
            dxn[rows, :] = part

        @pl.when(j > 0)
        def _():
            dxn[rows, :] += part

        @pl.when(j == n_j - 1)
        def _():
            xv = x_ref[...]
            r = lax.rsqrt(jnp.mean(xv * xv, axis=-1, keepdims=True) + EPS)
            n = xv * r
            dv = dxn[rows, :]
            dn = dv * g
            dx_ref[...] = dh_ref[...] + r * (dn - n * jnp.mean(dn * n, axis=-1, keepdims=True))
            st_ref[0:1, :] += jnp.sum(dv * n, axis=0, keepdims=True)

    lastrow = lambda j, i: (jnp.where(j == n_j - 1, i, 0), 0)
    return _call(
        body, comm, (dproj, x, dh1, g1, wt_in), grid=(n_j, n_i),
        in_specs=[pl.BlockSpec((tm, ck), lambda j, i: (i, j)), pl.BlockSpec((tm, D), lastrow),
                  pl.BlockSpec((tm, D), lastrow),
                  pl.BlockSpec((DEPTH, D), lambda j, i: (0, 0)), pl.BlockSpec((ck, D), lambda j, i: (j, 0))],
        out_specs=[pl.BlockSpec((tm, D), lastrow), pl.BlockSpec((8, D), lambda j, i: (0, 0))],
        out_shape=[_sds((s, D), F32), _sds((8, D), F32)],
        scratch_shapes=[pltpu.VMEM((s, D), F32)], name=f"bwd_proj_x{l}")


def _block_diag(w):
    nl, nb, bw, _ = w.shape
    eye = jnp.eye(nb, dtype=w.dtype)
    return jnp.einsum("lhij,hk->lhikj", w, eye).reshape(nl, nb * bw, nb * bw).astype(BF16)


class NoOverlap:
    def __init__(self, big):
        self.big = big

    def weights(self, l):
        return self.big[l]

    def job(self, slot, l):
        return None

    def done(self, slot, l, results):
        pass

    def new_grads(self, group, l, grads):
        pass

    def new_small(self, l, arrays, head_stats):
        pass


def local_step(x, target, norm1_g, norm2_g, final_g, convw, vecs, lru_wx, lru_wa, plan):
    wx_bd, wa_bd = _block_diag(lru_wx), _block_diag(lru_wa)

    def run(fn, slot, l, *args):
        res, cres = fn(*args, l, comm=plan.job(slot, l))
        plan.done(slot, l, cres)
        return res

    saved = []
    for l in range(DEPTH):
        proj = run(fwd_proj, "fwd_proj", l, x, norm1_g, plan.weights(l)["in_t"])
        pre_abd, h, kept = run(fwd_branch, "fwd_branch", l, proj, convw, vecs, wx_bd, wa_bd)
        pre_c = run(fwd_attn, "fwd_attn", l, proj, vecs)
        w = plan.weights(l)
        y4, merged, h1 = run(fwd_merge, "fwd_merge", l, x, proj, pre_abd, pre_c, w["a_t"], w["b_t"], w["c_t"], w["d_t"], w["o"])
        w = plan.weights(l)
        x_out, fg, fu = run(fwd_ffn, "fwd_ffn", l, h1, norm2_g, w["gate_t"], w["up_t"], w["down"])
        saved.append((x, proj, pre_abd, h, kept, pre_c, y4, merged, h1, fg, fu))
        x = x_out
    dx, head_stats = loss_head(x, final_g.reshape(1, D), target)
    small = [None] * DEPTH
    for l in reversed(range(DEPTH)):
        x_in, proj, pre_abd, h, kept, pre_c, y4, merged, h1, fg, fu = saved[l]
        w = plan.weights(l)
        dh1, d_gate, d_up, d_down, st_ffn = run(bwd_ffn, "bwd_ffn", l, dx, h1, fg, fu, norm2_g, w["gate_t"], w["up_t"], w["down"])
        plan.new_grads("ffn", l, dict(gate_t=d_gate, up_t=d_up, down=d_down))
        dproj, dpre, d_o, d_a, d_b, d_c, d_d = run(
            bwd_merge, "bwd_merge", l, dh1, y4, proj, merged, pre_abd, pre_c, w["a_t"], w["b_t"], w["c_t"], w["d_t"], w["o"])
        plan.new_grads("out", l, dict(a_t=d_a, b_t=d_b, c_t=d_c, d_t=d_d, o=d_o))
        dq, dkc, dkp, st_attn = run(bwd_attn, "bwd_attn", l, proj, dpre, vecs)
        dproj, dcw, dvec, dwx, dwa = run(bwd_branch, "bwd_branch", l, proj, dproj, dpre, h, kept, dq, dkc, dkp, convw, vecs, wx_bd, wa_bd)
        if l > 0:
            dx, d_in, st_proj = run(bwd_proj, "bwd_proj", l, dproj, x_in, dh1, norm1_g, w["in_t"])
            plan.new_grads("in", l, dict(in_t=d_in))
        else:
            for half, name in enumerate(("in_a", "in_b")):
                d_half = run(functools.partial(bwd_proj_w, half=half), f"bwd_proj_w{half}", l, dproj, x_in, norm1_g)
                plan.new_grads(name, l, {name: d_half})
            dx, st_proj = run(bwd_proj_x, "bwd_proj_x", l, dproj, x_in, dh1, norm1_g, w["in_t"])
        small[l] = (st_proj, st_ffn, dvec, st_attn, dcw, dwx, dwa)
        plan.new_small(l, small[l], head_stats)
    return head_stats, dx, small


BIG = dict(in_t=("w_in", "view"), a_t=("w_a_out", "transpose"), b_t=("w_b_out", "transpose"), c_t=("w_c_out", "transpose"),
           d_t=("w_d_out", "transpose"), o=("w_o", "plain"), gate_t=("w_ffn_gate", "view"), up_t=("w_ffn_up", "view"),
           down=("w_ffn_down", "plain"))


def cast_transpose(ws, name):
    n = len(ws)
    nl, a, b = ws[0].shape
    ta = min(256, a)

    def body(*refs):
        for w_ref, o_ref in zip(refs[:n], refs[n:]):
            o_ref[...] = w_ref[...].T.astype(BF16)

    return pl.pallas_call(
        body, grid=(nl, a // ta),
        in_specs=[pl.BlockSpec((None, ta, b), lambda l, i: (l, i, 0))] * n,
        out_specs=[pl.BlockSpec((None, b, ta), lambda l, i: (l, 0, i))] * n,
        out_shape=[_sds((nl, b, a), BF16)] * n, compiler_params=_cparams(2), name=name)(*ws)


def add_partials(mine, recv, core, name):
    n = len(mine)

    def body(core_ref, *refs):
        del core_ref
        for a_ref, b_ref, o_ref in zip(refs[:n], refs[n:2 * n], refs[2 * n:]):
            o_ref[...] = (a_ref[...].astype(F32) + b_ref[...].astype(F32)).astype(BF16)

    return pl.pallas_call(
        body,
        grid_spec=pltpu.PrefetchScalarGridSpec(
            num_scalar_prefetch=1, grid=(4,),
            in_specs=[pl.BlockSpec((None, None) + a.shape[2:], lambda i, cr: (i, cr[0], 0, 0)) for a in mine]
            + [pl.BlockSpec((None,) + b.shape[1:], lambda i, cr: (i, 0, 0)) for b in recv],
            out_specs=[pl.BlockSpec((None,) + b.shape[1:], lambda i, cr: (i, 0, 0)) for b in recv]),
        out_shape=[_sds(b.shape, BF16) for b in recv], compiler_params=_cparams(1), name=name)(core, *mine, *recv)


def _adamw(w, g, m, v):
    m = ADAM_B1 * m + (1.0 - ADAM_B1) * g
    v = ADAM_B2 * v + (1.0 - ADAM_B2) * (g * g)
    m_hat = m / (1.0 - ADAM_B1 ** ADAM_STEP)
    v_hat = v / (1.0 - ADAM_B2 ** ADAM_STEP)
    delta = -ADAM_LR * (m_hat / (jnp.sqrt(v_hat) + ADAM_EPS) + ADAM_WD * w)
    return delta, m, v


def adamw_big(items, name, comm=None):
    n_tiles = 4
    n = len(items)
    nl = items[0][0].shape[1]

    def body(*refs):
        ins, outs = refs[:4 * n], refs[4 * n:]
        for k, (contrib, _, _, _, transposed) in enumerate(items):
            c_ref, w_ref, m_ref, v_ref = ins[4 * k:4 * k + 4]
            g = c_ref[0].astype(F32)
            for src in range(1, contrib.shape[0]):
                g = g + c_ref[src].astype(F32)
            if transposed:
                g = g.T
            delta, mn, vn = _adamw(w_ref[...], g, m_ref[...], v_ref[...])
            for o_ref, val in zip(outs[4 * k:4 * k + 4], (g, delta, mn, vn)):
                o_ref[...] = val

    in_specs, out_specs, out_shape, args = [], [], [], []
    for contrib, w, m, v, transposed in items:
        nsrc, _, rows, cols = contrib.shape
        ct = cols // n_tiles
        if transposed:
            wspec = pl.BlockSpec((None, ct, rows), lambda l, j: (l, j, 0))
        else:
            wspec = pl.BlockSpec((None, rows, ct), lambda l, j: (l, 0, j))
        in_specs += [pl.BlockSpec((nsrc, None, rows, ct), lambda l, j: (0, l, 0, j)), wspec, wspec, wspec]
        out_specs += [wspec] * 4
        out_shape += [_sds(w.shape, F32)] * 4
        args += [contrib, w, m, v]
    res, cres = _call(body, comm, tuple(args), grid=(nl, n_tiles), in_specs=in_specs, out_specs=out_specs,
                      out_shape=out_shape, name=name)
    return [res[4 * k:4 * k + 4] for k in range(n)], cres


VEC_NAMES = ("conv_a_b", "lru_bx", "lru_ba", "lru_lambda", "conv_d_b", "ln_d_g", "ln_d_b")
P_N1, P_N2, P_VEC, P_CONV, P_LRU = 0, 1, 2, 6, 6 + CW_ROWS
P_FINAL, P_LOSS, P_ROWS = P_LRU + HD, P_LRU + HD + 1, P_LRU + HD + 2
SMALL = ("norm1_g", "conv_a_w", "conv_a_b", "lru_wx", "lru_bx", "lru_wa", "lru_ba", "lru_lambda", "conv_b_w", "sinks",
         "conv_d_w", "conv_d_b", "ln_d_g", "ln_d_b", "norm2_g", "final_g")
VMEM_FULL = pl.BlockSpec(memory_space=pltpu.VMEM)


def _stack_vecs(p):
    rows = [p[n] for n in VEC_NAMES] + [jnp.pad(p["sinks"], ((0, 0), (0, BW - N_HEADS)))]
    return jnp.stack(rows, axis=1)


def _stack_convs(p):
    nl, _, ch = p["conv_a_w"].shape
    z = jnp.zeros((nl, 1, ch), F32)
    return jnp.concatenate([p["conv_a_w"], p["conv_b_w"], z, p["conv_d_w"], z], axis=1)


def _vec_place(r):
    return P_VEC + r // 2, (r % 2) * BW


def pack_small(arrays, head_stats, l):
    n = len(arrays)

    def body(*refs):
        st_proj, st_ffn, dvec, st_attn, dcw, dwx, dwa = refs[:n]
        pack = refs[-1]
        pack[...] = jnp.zeros((P_ROWS, D), F32)
        lane = lax.broadcasted_iota(jnp.int32, (HD, BW), 1)
        pack[P_N1:P_N1 + 1, :] = st_proj[0:1, :]
        pack[P_N2:P_N2 + 1, :] = st_ffn[0:1, :]
        for r in range(len(VEC_NAMES)):
            row, c0 = _vec_place(r)
            pack[row:row + 1, c0:c0 + BW] = dvec[r:r + 1, :]
        row, c0 = _vec_place(V_SINK)
        pack[row:row + 1, c0:c0 + 128] = st_attn[0:1, :]
        pack[P_CONV:P_CONV + CW_ROWS, 0:BW] = dcw[...]
        for mat, c0 in ((dwx, 0), (dwa, BW)):
            blocks = jnp.zeros((HD, BW), F32)
            for h in range(BW // HD):
                blocks = jnp.where((lane >= HD * h) & (lane < HD * (h + 1)), mat[HD * h:HD * (h + 1), :], blocks)
            pack[P_LRU:P_LRU + HD, c0:c0 + BW] = blocks
        if head_stats is not None:
            pack[P_FINAL:P_LOSS + 1, :] = refs[n][0:2, :]

    flat = list(arrays) + ([] if head_stats is None else [head_stats])
    return pl.pallas_call(body, out_shape=_sds((P_ROWS, D), F32), in_specs=[VMEM_FULL] * len(flat), out_specs=VMEM_FULL,
                          name=f"pack_small{l}", compiler_params=pltpu.CompilerParams(vmem_limit_bytes=VMEM_LIMIT))(*flat)


def adamw_small(gathered, me, w, m, v):
    ns = len(SMALL)

    def body(me_ref, *refs):
        c_refs, refs = refs[:DEPTH], refs[DEPTH:]
        w_refs, m_refs, v_refs = refs[:ns], refs[ns:2 * ns], refs[2 * ns:3 * ns]
        loss_ref, outs, gs = refs[3 * ns], refs[3 * ns + 1:3 * ns + 1 + 4 * ns], refs[-1]
        for l in range(DEPTH):
            gs[l] = c_refs[l][0]
            for dev in range(1, NDEV):
                gs[l] += c_refs[l][dev]
        loss_ref[...] = gs[DEPTH - 1, P_LOSS:P_LOSS + 1, 0:128]

        def update(name, sel, g):
            i = SMALL.index(name)
            delta, mn, vn = _adamw(w_refs[i][sel], g, m_refs[i][sel], v_refs[i][sel])
            for o_ref, val in zip(outs[4 * i:4 * i + 4], (g, delta, mn, vn)):
                o_ref[sel] = val

        update("final_g", (slice(0, 1), slice(None)), gs[DEPTH - 1, P_FINAL:P_FINAL + 1, :])
        shift = (BW - me_ref[0] * (BW // NDEV)) & (BW - 1)
        for l in range(DEPTH):
            row = (slice(l, l + 1), slice(None))
            update("norm1_g", row, gs[l, P_N1:P_N1 + 1, :])
            update("norm2_g", row, gs[l, P_N2:P_N2 + 1, :])
            for r, name in enumerate(VEC_NAMES):
                prow, c0 = _vec_place(r)
                update(name, row, gs[l, prow:prow + 1, c0:c0 + BW])
            prow, c0 = _vec_place(V_SINK)
            update("sinks", row, gs[l, prow:prow + 1, c0:c0 + N_HEADS])
            mine = pltpu.roll(gs[l, P_CONV:P_CONV + CW_ROWS, 0:BW], shift, 1)[:, 0:BW // NDEV]
            update("conv_a_w", (l,), mine[CW_A:CW_A + CONV_A])
            update("conv_b_w", (l,), mine[CW_B:CW_B + CONV_B])
            update("conv_d_w", (l,), mine[CW_D:CW_D + CONV_D])
            for h in range(BW // HD):
                update("lru_wx", (l, h), gs[l, P_LRU:P_LRU + HD, HD * h:HD * (h + 1)])
                update("lru_wa", (l, h), gs[l, P_LRU:P_LRU + HD, BW + HD * h:BW + HD * (h + 1)])

    args = [p[n] for p in (w, m, v) for n in SMALL]
    full = lambda a: pl.BlockSpec(a.shape, lambda i, me_ref: (0,) * a.ndim)
    out_shape = [_sds((1, 128), F32)] + [_sds(w[n].shape, F32) for n in SMALL for _ in range(4)]
    outs = pl.pallas_call(
        body,
        grid_spec=pltpu.PrefetchScalarGridSpec(
            num_scalar_prefetch=1, grid=(1,),
            in_specs=[full(a) for a in list(gathered) + args], out_specs=[full(o) for o in out_shape],
            scratch_shapes=[pltpu.VMEM((DEPTH, P_ROWS, D), F32)]),
        out_shape=out_shape, name="adamw_small", compiler_params=_cparams(1))(me, *gathered, *args)
    return outs[0], {n: outs[1 + 4 * i:5 + 4 * i] for i, n in enumerate(SMALL)}


def merge_jobs(jobs):
    jobs = [j for j in jobs if j is not None]
    if not jobs:
        return None, []
    inputs, aliases, outs, sems, cuts = [], {}, [], [], []
    for j in jobs:
        i0, o0, s0 = len(inputs), len(outs), len(sems)
        aliases.update({i0 + i: o0 + o for i, o in j.aliases.items()})
        inputs += j.inputs
        outs += j.out_shapes
        sems += j.sem_shapes
        cuts.append((i0, len(inputs), o0, len(outs), s0, len(sems)))

    def each(which):
        def go(cins, couts, s):
            for j, (i0, i1, o0, o1, s0, s1) in zip(jobs, cuts):
                if getattr(j, which) is not None:
                    getattr(j, which)(cins[i0:i1], couts[o0:o1], s[s0:s1])
        return go

    relay = each("relay") if any(j.relay is not None for j in jobs) else None
    return CommJob(inputs, aliases, outs, sems, each("start"), each("finish"), relay), [(c[2], c[3]) for c in cuts]


SIXTHS = 6
OUT_KINDS = ("a_t", "b_t", "c_t", "d_t", "o")
GATHER_PLAN = {
    "fwd_proj": [(k, 0, 0, 6) for k in OUT_KINDS] + [("gate_t", 0, 0, 6)],
    "fwd_branch": [("up_t", 0, 0, 6)],
    "fwd_attn": [("down", 0, 0, 6)],
    "fwd_merge": [("in_t", 1, 0, 2)],
    "fwd_ffn": [("in_t", 1, 2, 6)],
}
SIBLING_PLAN = {"bwd_merge": ("ffn", 0), "bwd_branch": ("out", 0), "bwd_ffn": ("in", 1),
                "bwd_proj_w1": ("in_a", 0), "bwd_proj_x": ("in_b", 0)}
GROUPS = dict(ffn=("gate_t", "up_t", "down"), out=OUT_KINDS, in_a=("in_a",), in_b=("in_b",))
GROUPS["in"] = ("in_t",)
COLUMN_HALF = dict(in_a=("in_t", W_IN_PARTS[0][0]), in_b=("in_t", W_IN_PARTS[1][0]))
CHIP_PLAN = {
    "bwd_attn": [("in_t", 1, 3, 5)],
    "bwd_branch": [("in_t", 1, 5, 6), ("gate_t", 0, 0, 6), ("up_t", 0, 0, 6), ("down", 0, 0, 3)],
    "bwd_proj": [(k, 0, 0, 6) for k in OUT_KINDS] + [("down", 0, 3, 6)],
    "bwd_proj_w0": [(k, 0, 0, 6) for k in OUT_KINDS[:3]] + [("down", 0, 3, 6)],
    "bwd_proj_w1": [(k, 0, 0, 6) for k in OUT_KINDS[3:]],
    "bwd_merge": [("in_t", 1, 0, 3)],
    "bwd_proj_x": [("in_a", 0, 0, 5)],
    "adamw_rest": [("in_a", 0, 5, 6), ("in_b", 0, 0, 6)],
}
SMALL_GATHER_PLAN = {"bwd_ffn": 1, "adamw_rest": 0}


class Overlap:
    def __init__(self, shards, core):
        self.shards = shards
        self.core = core
        self.gathered = [dict.fromkeys(BIG) for _ in range(DEPTH)]
        self.views = {}
        self.partial = {}
        self.contrib = dict.fromkeys(BIG)
        self.small_packs = [None] * DEPTH
        self.small_gathered = [None] * DEPTH
        self._open = None

    def weights(self, l):
        return self.gathered[l]

    def new_grads(self, group, l, grads):
        for k, g in grads.items():
            self.views[k, l] = g.reshape(4, 2, g.shape[0] // NDEV, g.shape[1])

    def new_small(self, l, arrays, head_stats):
        self.small_packs[l] = pack_small(arrays, head_stats if l == DEPTH - 1 else None, l)

    @staticmethod
    def _rows(shard_rows, f0, f1):
        return shard_rows * f0 // SIXTHS, shard_rows * (f1 - f0) // SIXTHS

    def job(self, slot, l):
        jobs, notes = [], []
        pieces = [(k, l + dl, f0, f1) for k, dl, f0, f1 in GATHER_PLAN.get(slot, []) if l + dl < DEPTH]
        if pieces:
            jobs.append(gather_job([((k, ll), self.shards[ll][k], self.gathered[ll][k],
                                     *self._rows(self.shards[ll][k].shape[0], f0, f1)) for k, ll, f0, f1 in pieces]))
            notes.append(("gather", list(dict.fromkeys((k, ll) for k, ll, _, _ in pieces))))
        if slot in SIBLING_PLAN and l + SIBLING_PLAN[slot][1] < DEPTH:
            group, dl = SIBLING_PLAN[slot]
            keys = [(k, l + dl) for k in GROUPS[group]]
            jobs.append(sibling_exchange_job([self.views[key] for key in keys]))
            notes.append(("sibling", keys))
        pieces = [(k, l + dl, f0, f1) for k, dl, f0, f1 in CHIP_PLAN.get(slot, []) if l + dl < DEPTH]
        if pieces:
            whole = [(*COLUMN_HALF.get(k, (k, 0)), k, ll, f0, f1) for k, ll, f0, f1 in pieces]
            jobs.append(chip_exchange_job([(self.partial[k, ll], self.contrib[kind], kind, ll,
                                            *self._rows(self.partial[k, ll].shape[1], f0, f1), col0, self.shards[ll][kind].shape[1])
                                           for kind, col0, k, ll, f0, f1 in whole]))
            notes.append(("chips", list(dict.fromkeys(kind for kind, *_ in whole))))
        if slot in SMALL_GATHER_PLAN and l + SMALL_GATHER_PLAN[slot] < DEPTH:
            ll = l + SMALL_GATHER_PLAN[slot]
            jobs.append(gather_job([("small", self.small_packs[ll], None, 0, P_ROWS)]))
            notes.append(("small", ll))
        job, spans = merge_jobs(jobs)
        self._open = (slot, l, notes, spans)
        return job

    def done(self, slot, l, results):
        open_slot, open_l, notes, spans = self._open
        assert (open_slot, open_l) == (slot, l)
        for (what, keys), (r0, r1) in zip(notes, spans):
            res = results[r0:r1]
            if what == "gather":
                for (k, ll), g in zip(keys, res):
                    self.gathered[ll][k] = g
            elif what == "sibling":
                sums = add_partials([self.views[key] for key in keys], list(res), self.core, f"chip_sum_{keys[0][0]}{keys[0][1]}")
                self.partial.update(zip(keys, sums))
            elif what == "chips":
                for k, c in zip(keys, res):
                    self.contrib[k] = c
            else:
                self.small_gathered[keys], = res


SMALL = ("norm1_g", "conv_a_w", "conv_a_b", "lru_wx", "lru_bx", "lru_wa", "lru_ba", "lru_lambda", "conv_b_w", "sinks",
         "conv_d_w", "conv_d_b", "ln_d_g", "ln_d_b", "norm2_g", "final_g")
WEIGHTS = ("norm1_g", "w_in", "conv_a_w", "conv_a_b", "lru_wx", "lru_bx", "lru_wa", "lru_ba", "lru_lambda", "w_a_out",
           "conv_b_w", "w_b_out", "sinks", "w_c_out", "conv_d_w", "conv_d_b", "ln_d_g", "ln_d_b", "w_d_out", "w_o",
           "norm2_g", "w_ffn_gate", "w_ffn_up", "w_ffn_down", "final_g")


def kernel(x, norm1_g, w_in, conv_a_w, conv_a_b, lru_wx, lru_bx, lru_wa, lru_ba, lru_lambda, w_a_out, conv_b_w, w_b_out, sinks, w_c_out, conv_d_w, conv_d_b, ln_d_g, ln_d_b, w_d_out, w_o, norm2_g, w_ffn_gate, w_ffn_up, w_ffn_down, final_g, loss_target, m_norm1_g, m_w_in, m_conv_a_w, m_conv_a_b, m_lru_wx, m_lru_bx, m_lru_wa, m_lru_ba, m_lru_lambda, m_w_a_out, m_conv_b_w, m_w_b_out, m_sinks, m_w_c_out, m_conv_d_w, m_conv_d_b, m_ln_d_g, m_ln_d_b, m_w_d_out, m_w_o, m_norm2_g, m_w_ffn_gate, m_w_ffn_up, m_w_ffn_down, m_final_g, v_norm1_g, v_w_in, v_conv_a_w, v_conv_a_b, v_lru_wx, v_lru_bx, v_lru_wa, v_lru_ba, v_lru_lambda, v_w_a_out, v_conv_b_w, v_w_b_out, v_sinks, v_w_c_out, v_conv_d_w, v_conv_d_b, v_ln_d_g, v_ln_d_b, v_w_d_out, v_w_o, v_norm2_g, v_w_ffn_gate, v_w_ffn_up, v_w_ffn_down, v_final_g):
    args = dict(locals())
    w = {n: args[n] for n in WEIGHTS}
    m = {n: args["m_" + n] for n in WEIGHTS}
    v = {n: args["v_" + n] for n in WEIGHTS}
    me = _dev_index(*_mesh_pos())

    def rows_major(a, how):
        return jnp.swapaxes(a, 1, 2) if how == "view" else a

    stacked = {k: rows_major(w[n], how).astype(BF16) for k, (n, how) in BIG.items() if how != "transpose"}
    turned = [k for k, (n, how) in BIG.items() if how == "transpose"]
    stacked.update(zip(turned, cast_transpose([w[BIG[k][0]] for k in turned], "prep_transposed")))
    plan = Overlap([{k: stacked[k][l] for k in BIG} for l in range(DEPTH)], lax.axis_index("c").astype(jnp.int32).reshape(1))
    convs = jnp.pad(_stack_convs(w).reshape(DEPTH * CW_ROWS, BW // NDEV), ((0, 0), (0, 256 - BW // NDEV)))
    g_in0, g_conv = _comm_only(gather_job([(("in_t", 0), plan.shards[0]["in_t"], None, 0, plan.shards[0]["in_t"].shape[0]),
                                           ("convs", convs, None, 0, convs.shape[0])]), "gather_first")
    plan.gathered[0]["in_t"] = g_in0
    convw = g_conv[:, :BW // NDEV].reshape(NDEV, DEPTH, CW_ROWS, BW // NDEV).transpose(1, 2, 0, 3).reshape(DEPTH, CW_ROWS, BW)

    vecs = _stack_vecs(w)
    head_stats, grad_x, grads = local_step(x[0], loss_target[0], norm1_g, norm2_g, final_g, convw, vecs, lru_wx, lru_wa, plan)


    out = {}
    for slot, kinds in (("adamw_rest", [k for k in BIG if k != "in_t"]), ("adamw_in_t", ["in_t"])):
        job = plan.job(slot, 0)
        items = [(plan.contrib[k], *[rows_major(p[BIG[k][0]], BIG[k][1]) for p in (w, m, v)], BIG[k][1] == "transpose")
                 for k in kinds]
        results, cres = adamw_big(items, slot, comm=job)
        plan.done(slot, 0, cres)
        for k, res in zip(kinds, results):
            out[BIG[k][0]] = [rows_major(r, BIG[k][1]) for r in res]

    def own_shapes(p):
        return {n: p[n].reshape(1, D) if n == "final_g" else p[n] for n in SMALL}

    loss, small = adamw_small([g.reshape(NDEV, P_ROWS, D) for g in plan.small_gathered], me.astype(jnp.int32).reshape(1),
                              own_shapes(w), own_shapes(m), own_shapes(v))
    for n in SMALL:
        out[n] = [r.reshape(w[n].shape) for r in small[n]]
    loss = loss[0, 0]
    return (loss, grad_x[None], *[out[n][0] for n in WEIGHTS], *[out[n][1] for n in WEIGHTS],
            *[out[n][2] for n in WEIGHTS], *[out[n][3] for n in WEIGHTS])
```

```python
import functools

import jax
import jax.numpy as jnp
from jax import lax
from jax.experimental import pallas as pl
from jax.experimental.pallas import tpu as pltpu

F32 = jnp.float32
BF16 = jnp.bfloat16
E = pl.Element

D = 1024
BW = 512
IN_W = 8448
GL0 = 4352
FF = 2816
N_HEADS = 8
N_KV = 2
HD = 64
ATT_BLK = 128
EPS = 1e-6
LRU_C = 8.0
NEG_INF = -1e30
DEPTH = 2
NDEV = 8
CONV_A, CONV_B, CONV_D = 4, 3, 31
C_AX, C_AG, C_BV, C_BC, C_BB, C_Q, C_K, C_V, C_D1, C_D2 = 0, 512, 1024, 1536, 2048, 2560, 3072, 3200, 3328, 3840
CW_A, CW_B, CW_D, CW_ROWS = 0, 4, 8, 40
V_CAB, V_BX, V_BA, V_LAM, V_CDB, V_LNG, V_LNB, V_SINK, V_ROWS = 0, 1, 2, 3, 4, 5, 6, 7, 8
HALO = 32
W_IN_PARTS = ((0, 768), (768, 256))

ADAM_LR, ADAM_B1, ADAM_B2, ADAM_EPS, ADAM_WD, ADAM_STEP = 0.001, 0.9, 0.999, 1e-08, 0.01, 10

VMEM_LIMIT = 56 * 1024 * 1024

_NN = (((1,), (0,)), ((), ()))
_NT = (((1,), (1,)), ((), ()))
_TN = (((0,), (0,)), ((), ()))


def _dot(a, b, dims):
    return lax.dot_general(a.astype(BF16), b.astype(BF16), dims, preferred_element_type=F32)


def _cparams(n_axes):
    return pltpu.CompilerParams(dimension_semantics=("arbitrary",) * n_axes, vmem_limit_bytes=VMEM_LIMIT)


def _sds(shape, dtype):
    return jax.ShapeDtypeStruct(tuple(shape), dtype)


def _sigmoid(x):
    return jax.nn.sigmoid(x)


def _neg_expm1(x):
    p = x * (1.0 + x * (0.5 + x * (1.0 / 6.0 + x * (1.0 / 24.0 + x * (1.0 / 120.0)))))
    return jnp.where(x > -0.1, -p, 1.0 - jnp.exp(x))


def _softplus(z):
    return jnp.maximum(z, 0.0) + jnp.log1p(jnp.exp(-jnp.abs(z)))


def _gelu_and_grad(x):
    c = 0.7978845608028654
    inner = c * (x + 0.044715 * x * x * x)
    t = jnp.tanh(inner)
    g = 0.5 * x * (1.0 + t)
    dg = 0.5 * (1.0 + t) + 0.5 * x * (1.0 - t * t) * c * (1.0 + 3.0 * 0.044715 * x * x)
    return g, dg


ANY = pl.BlockSpec(memory_space=pl.ANY)
MESH = pl.DeviceIdType.MESH


def _mesh_pos():
    return lax.axis_index("x"), lax.axis_index("y"), lax.axis_index("c")


def _dev_index(px, py, pc):
    return 4 * px + 2 * py + pc


class CommJob:
    def __init__(self, inputs, aliases, out_shapes, sem_shapes, start, finish, relay=None):
        self.inputs, self.aliases, self.out_shapes, self.sem_shapes = list(inputs), dict(aliases), list(out_shapes), list(sem_shapes)
        self.start, self.finish, self.relay = start, finish, relay


def _call(body, comm, args, *, grid, in_specs, out_specs, out_shape, scratch_shapes=(), name, aliases=None):
    single = not isinstance(out_shape, (list, tuple))
    out_specs = [out_specs] if single else list(out_specs)
    out_shape = [out_shape] if single else list(out_shape)
    scratch_shapes = list(scratch_shapes)
    n_in, n_out, n_scr, n_axes = len(in_specs), len(out_shape), len(scratch_shapes), len(grid)
    params = pltpu.CompilerParams(dimension_semantics=("arbitrary",) * n_axes, vmem_limit_bytes=VMEM_LIMIT)
    io_aliases = dict(aliases or {})
    if comm is None:
        outs = pl.pallas_call(body, grid=grid, in_specs=in_specs, out_specs=out_specs, out_shape=out_shape,
                              scratch_shapes=scratch_shapes, input_output_aliases=io_aliases, compiler_params=params,
                              name=name)(*args)
        return (outs[0] if single else outs), []
    c_in, c_out = len(comm.inputs), len(comm.out_shapes)
    io_aliases.update({n_in + i: n_out + o for i, o in comm.aliases.items()})

    def wrapped(*refs):
        ins, cins = refs[:n_in], refs[n_in:n_in + c_in]
        outs = refs[n_in + c_in:n_in + c_in + n_out]
        couts = refs[n_in + c_in + n_out:n_in + c_in + n_out + c_out]
        rest = refs[n_in + c_in + n_out + c_out:]
        scr, sems = rest[:n_scr], rest[n_scr:]
        first = functools.reduce(lambda a, b: a & b, [pl.program_id(a) == 0 for a in range(n_axes)])
        last = functools.reduce(lambda a, b: a & b, [pl.program_id(a) == pl.num_programs(a) - 1 for a in range(n_axes)])

        @pl.when(first)
        def _():
            comm.start(cins, couts, sems)

        if comm.relay is not None:
            step = functools.reduce(lambda a, b: a * grid[b] + pl.program_id(b), range(1, n_axes), pl.program_id(0))
            n_steps = functools.reduce(lambda a, b: a * b, grid)

            @pl.when(step == 2 * n_steps // 3)
            def _():
                comm.relay(cins, couts, sems)

        body(*ins, *outs, *scr)

        @pl.when(last)
        def _():
            comm.finish(cins, couts, sems)

    outs = pl.pallas_call(
        wrapped, grid=grid, in_specs=list(in_specs) + [ANY] * c_in, out_specs=out_specs + [ANY] * c_out,
        out_shape=out_shape + comm.out_shapes, scratch_shapes=scratch_shapes + comm.sem_shapes,
        input_output_aliases=io_aliases, compiler_params=params, name=name)(*args, *comm.inputs)
    res, cres = outs[:n_out], outs[n_out:]
    return (res[0] if single else res), cres


def _comm_only(comm, name):
    c_in, c_out = len(comm.inputs), len(comm.out_shapes)

    def body(*refs):
        cins, couts, sems = refs[:c_in], refs[c_in:c_in + c_out], refs[c_in + c_out:]
        comm.start(cins, couts, sems)
        if comm.relay is not None:
            comm.relay(cins, couts, sems)
        comm.finish(cins, couts, sems)

    return pl.pallas_call(body, in_specs=[ANY] * c_in, out_specs=[ANY] * c_out, out_shape=comm.out_shapes,
                          scratch_shapes=comm.sem_shapes, input_output_aliases=comm.aliases, name=name)(*comm.inputs)


def gather_job(pieces):
    inputs, aliases, out_shapes, plan, where = [], {}, [], [], {}
    for key, shard, gathered, row0, nrows in pieces:
        if key not in where:
            where[key] = (len(inputs), len(out_shapes))
            inputs.append(shard)
            if gathered is not None:
                aliases[len(inputs)] = len(out_shapes)
                inputs.append(gathered)
            out_shapes.append(_sds((NDEV * shard.shape[0], shard.shape[1]), shard.dtype))
        plan.append((*where[key], shard.shape[0], row0, nrows))
    n = len(plan)

    def copies(cins, couts, sems):
        send_sems, recv_sems, local_sems = sems
        x, y, c = _mesh_pos()
        me, sibling = (x, y, c), (x, y, 1 - c)
        xn, yn, dg = (1 - x, y), (x, 1 - y), (1 - x, 1 - y)
        local, first, pass1, pass2, got_ici, got_fwd, got_d2d = [], [], [], [], [], [], []
        for p, (i_shard, i_out, rows, row0, nrows) in enumerate(plan):
            src = cins[i_shard].at[pl.ds(row0, nrows), :]
            half = cins[i_shard].shape[1] // 2
            left, right, whole = pl.ds(0, half), pl.ds(half, half), slice(None)

            def slot(dev, lanes, i_out=i_out, rows=rows, row0=row0, nrows=nrows):
                return couts[i_out].at[pl.ds(_dev_index(*dev) * rows + row0, nrows), lanes]

            def copy(g, dev, to, lanes=whole, src=None, p=p, slot=slot):
                return pltpu.make_async_remote_copy(
                    src_ref=slot(dev, lanes) if src is None else src, dst_ref=slot(dev, lanes),
                    send_sem=send_sems.at[g, p], recv_sem=recv_sems.at[g, p], device_id=to, device_id_type=MESH)

            local.append(pltpu.make_async_copy(src, slot(me, whole), local_sems.at[p]))
            first += [copy(0, me, sibling, src=src), copy(1, me, (*xn, c), src=src), copy(2, me, (*yn, c), src=src)]
            got_ici += [copy(1, (*xn, c), me), copy(2, (*yn, c), me)]
            pass1 += [copy(3, (*xn, c), (*yn, c), left), copy(4, (*yn, c), (*xn, c), right),
                      copy(5, (*xn, c), sibling), copy(6, (*yn, c), sibling)]
            got_fwd += [copy(3, (*dg, c), me, left), copy(4, (*dg, c), me, right)]
            pass2 += [copy(7, (*dg, c), sibling, left), copy(8, (*dg, c), sibling, right)]
            got_d2d += [copy(0, sibling, me), copy(5, (*xn, 1 - c), me), copy(6, (*yn, 1 - c), me),
                        copy(7, (*dg, 1 - c), me, left), copy(8, (*dg, 1 - c), me, right)]
        return local, first, pass1, pass2, got_ici, got_fwd, got_d2d

    def start(cins, couts, sems):
        local, first, *_ = copies(cins, couts, sems)
        for cp in local + first:
            cp.start()

    def pass_on(cins, couts, sems):
        _, _, pass1, _, got_ici, _, _ = copies(cins, couts, sems)
        for cp in got_ici:
            cp.wait_recv()
        for cp in pass1:
            cp.start()

    def finish(cins, couts, sems):
        local, first, pass1, pass2, _, got_fwd, got_d2d = copies(cins, couts, sems)
        for cp in got_fwd:
            cp.wait_recv()
        for cp in pass2:
            cp.start()
        for cp in got_d2d:
            cp.wait_recv()
        for cp in first + pass1 + pass2:
            cp.wait_send()
        for cp in local:
            cp.wait()

    sem_shapes = [pltpu.SemaphoreType.DMA((9, n)), pltpu.SemaphoreType.DMA((9, n)), pltpu.SemaphoreType.DMA((n,))]
    return CommJob(inputs, aliases, out_shapes, sem_shapes, start, finish, relay=pass_on)


def sibling_exchange_job(grads):
    n = len(grads)

    def copies(cins, couts, sems):
        send_sems, recv_sems = sems
        x, y, c = _mesh_pos()
        return [pltpu.make_async_remote_copy(
            src_ref=cins[q].at[:, 1 - c], dst_ref=couts[q], send_sem=send_sems.at[q], recv_sem=recv_sems.at[q],
            device_id=(x, y, 1 - c), device_id_type=MESH) for q in range(n)]

    def start(cins, couts, sems):
        for cp in copies(cins, couts, sems):
            cp.start()

    def finish(cins, couts, sems):
        cps = copies(cins, couts, sems)
        for cp in cps:
            cp.wait_recv()
        for cp in cps:
            cp.wait_send()

    return CommJob(grads, {}, [_sds((4,) + g.shape[2:], g.dtype) for g in grads],
                   [pltpu.SemaphoreType.DMA((n,)), pltpu.SemaphoreType.DMA((n,))], start, finish)


def chip_exchange_job(pieces):
    inputs, aliases, out_shapes, plan, where = [], {}, [], [], {}
    for partial, contrib, key, layer, row0, nrows, col0, cols in pieces:
        if key not in where:
            where[key] = len(out_shapes)
            out_shapes.append(_sds((4, DEPTH, partial.shape[1], cols), partial.dtype))
            if contrib is not None:
                aliases[len(inputs)] = where[key]
                inputs.append(contrib)
        plan.append((len(inputs), where[key], layer, row0, nrows, col0, partial.shape[2]))
        inputs.append(partial)
    n = len(plan)

    def copies(cins, couts, sems):
        send_sems, recv_sems, local_sems = sems
        x, y, c = _mesh_pos()
        mine = 2 * x + y
        local, sends, recvs = [], [], []
        for p, (i_in, i_out, layer, row0, nrows, col0, ncols) in enumerate(plan):
            rows, lanes = pl.ds(row0, nrows), pl.ds(col0, ncols)
            local.append(pltpu.make_async_copy(cins[i_in].at[mine, rows, :], couts[i_out].at[mine, layer, rows, lanes],
                                               local_sems.at[p]))
            for j, (cx, cy) in enumerate([(1 - x, y), (x, 1 - y), (1 - x, 1 - y)]):
                theirs = 2 * cx + cy

                def copy(slot_there, j=j, p=p, cx=cx, cy=cy, theirs=theirs, i_in=i_in, i_out=i_out, layer=layer,
                         rows=rows, lanes=lanes):
                    return pltpu.make_async_remote_copy(
                        src_ref=cins[i_in].at[theirs, rows, :], dst_ref=couts[i_out].at[slot_there, layer, rows, lanes],
                        send_sem=send_sems.at[j, p], recv_sem=recv_sems.at[j, p], device_id=(cx, cy, c), device_id_type=MESH)
                sends.append(copy(mine))
                recvs.append(copy(theirs))
        return local, sends, recvs

    def start(cins, couts, sems):
        local, sends, _ = copies(cins, couts, sems)
        for cp in local + sends:
            cp.start()

    def finish(cins, couts, sems):
        local, sends, recvs = copies(cins, couts, sems)
        for cp in recvs:
            cp.wait_recv()
        for cp in sends:
            cp.wait_send()
        for cp in local:
            cp.wait()

    sem_shapes = [pltpu.SemaphoreType.DMA((3, n)), pltpu.SemaphoreType.DMA((3, n)), pltpu.SemaphoreType.DMA((n,))]
    return CommJob(inputs, aliases, out_shapes, sem_shapes, start, finish)


def fwd_proj(x, g1, wt_in, l, comm=None):
    s = x.shape[0]
    tm = min(1024, s)
    tn = 1408

    def body(x_ref, g_ref, w_ref, o_ref, xn_ref):
        @pl.when(pl.program_id(1) == 0)
        def _():
            xv = x_ref[...]
            r = lax.rsqrt(jnp.mean(xv * xv, axis=-1, keepdims=True) + EPS)
            xn_ref[...] = (xv * r * g_ref[l:l + 1, :]).astype(BF16)

        o_ref[...] = _dot(xn_ref[...], w_ref[...], _NT).astype(BF16)

    return _call(
        body, comm, (x, g1, wt_in), grid=(s // tm, IN_W // tn),
        in_specs=[pl.BlockSpec((tm, D), lambda i, j: (i, 0)),
                  pl.BlockSpec((DEPTH, D), lambda i, j: (0, 0)),
                  pl.BlockSpec((tn, D), lambda i, j: (j, 0))],
        out_specs=pl.BlockSpec((tm, tn), lambda i, j: (i, j)),
        out_shape=_sds((s, IN_W), BF16),
        scratch_shapes=[pltpu.VMEM((tm, D), BF16)], name=f"fwd_proj{l}")


def _scan_fwd(a_ref, u_ref, h_ref, h0, n_rows):
    row = lax.broadcasted_iota(jnp.int32, (8, BW), 0)

    def body(g, hprev):
        r = pl.multiple_of(g * 8, 8)
        a = a_ref[pl.ds(r, 8), :]
        u = u_ref[pl.ds(r, 8), :]
        for sft in (1, 2, 4):
            a_sh = jnp.where(row >= sft, pltpu.roll(a, sft, 0), 1.0)
            u_sh = jnp.where(row >= sft, pltpu.roll(u, sft, 0), 0.0)
            u = u + a * u_sh
            a = a * a_sh
        h = u + a * hprev
        h_ref[pl.ds(r, 8), :] = h
        return h[7:8, :]

    return lax.fori_loop(0, n_rows // 8, body, h0)


def _scan_bwd(b_ref, g_ref, o_ref, c0, n_rows):
    row = lax.broadcasted_iota(jnp.int32, (8, BW), 0)

    def body(k, cnext):
        r = pl.multiple_of((n_rows // 8 - 1 - k) * 8, 8)
        b = b_ref[pl.ds(r, 8), :]
        g = g_ref[pl.ds(r, 8), :]
        for sft in (1, 2, 4):
            b_sh = jnp.where(row < 8 - sft, pltpu.roll(b, 8 - sft, 0), 1.0)
            g_sh = jnp.where(row < 8 - sft, pltpu.roll(g, 8 - sft, 0), 0.0)
            g = g + b * g_sh
            b = b * b_sh
        o = g + b * cnext
        o_ref[pl.ds(r, 8), :] = o
        return o[0:1, :]

    return lax.fori_loop(0, n_rows // 8, body, c0)


def _shifted_copies(buf, shifted, n_rows):
    for r in range(1, 8):
        shifted[r - 1, 0:n_rows - 8, :] = buf[pl.ds(r, n_rows - 8), :]


def _window(buf, shifted, off, t):
    r = off % 8
    return buf[pl.ds(off, t), :] if r == 0 else shifted[r - 1, pl.ds(off - r, t), :]


def _branch_fwd_math(cur_ref, halo_ref, cw_ref, vec_ref, wx_ref, wa_ref, bufa, bufb, bufd, xd, first, t, saved_ref=None):
    def halo(c0):
        v = halo_ref[:, c0:c0 + BW].astype(F32)
        return jnp.where(first, 0.0, v)

    def cur(c0):
        return cur_ref[:, c0:c0 + BW].astype(F32)

    out = {}
    bufa[0:HALO, :] = halo(C_AX)
    bufa[HALO:HALO + t, :] = cur(C_AX)
    ca = jnp.zeros((t, BW), F32) + vec_ref[V_CAB:V_CAB + 1, :]
    for k in range(CONV_A):
        ca = ca + cw_ref[CW_A + k:CW_A + k + 1, :] * bufa[pl.ds(HALO - (CONV_A - 1) + k, t), :]
    if saved_ref is None:
        gi = _sigmoid(_dot(ca, wx_ref[...], _NN) + vec_ref[V_BX:V_BX + 1, :])
        gr = _sigmoid(_dot(ca, wa_ref[...], _NN) + vec_ref[V_BA:V_BA + 1, :])
    else:
        gi, gr = saved_ref[:, BW:2 * BW], saved_ref[:, 2 * BW:3 * BW]
    sp = _softplus(-vec_ref[V_LAM:V_LAM + 1, :])
    la = -LRU_C * sp * gr
    a = jnp.exp(la)
    mult = jnp.sqrt(_neg_expm1(2.0 * la))
    out.update(ca=ca, gi=gi, gr=gr, sp=sp, a=a, mult=mult)
    bufb[0:HALO, :] = halo(C_BC) * halo(C_BV)
    bufb[HALO:HALO + t, :] = cur(C_BC) * cur(C_BV)
    cb = jnp.zeros((t, BW), F32)
    for k in range(CONV_B):
        cb = cb + cw_ref[CW_B + k:CW_B + k + 1, :] * bufb[pl.ds(HALO - (CONV_B - 1) + k, t), :]
    out.update(cb=cb)
    bufd[0:HALO, :] = halo(C_D1) * _sigmoid(halo(C_D2))
    s2 = _sigmoid(cur(C_D2))
    bufd[HALO:HALO + t, :] = cur(C_D1) * s2
    _shifted_copies(bufd, xd, t + HALO)
    if saved_ref is None:
        cd = jnp.zeros((t, BW), F32) + vec_ref[V_CDB:V_CDB + 1, :]
        for k in range(CONV_D):
            cd = cd + cw_ref[CW_D + k:CW_D + k + 1, :] * _window(bufd, xd, HALO - (CONV_D - 1) + k, t)
    else:
        cd = saved_ref[:, 0:BW]
    mu = jnp.mean(cd, axis=-1, keepdims=True)
    xc = cd - mu
    rstd = lax.rsqrt(jnp.mean(xc * xc, axis=-1, keepdims=True) + EPS)
    xh = xc * rstd
    ln = xh * vec_ref[V_LNG:V_LNG + 1, :] + vec_ref[V_LNB:V_LNB + 1, :]
    out.update(s2=s2, xh=xh, rstd=rstd, ln=ln, cd=cd)
    return out


def fwd_branch(proj, convw, vecs, wx_bd, wa_bd, l, comm=None):
    s = proj.shape[0]
    t = min(256, s)

    def body(cur_ref, halo_ref, cw_ref, vec_ref, wx_ref, wa_ref, pre_ref, h_ref, sv_ref, bufa, bufb, bufd, xd, a_s, u_s, hcar):
        first = pl.program_id(0) == 0

        @pl.when(first)
        def _():
            hcar[...] = jnp.zeros((1, BW), F32)

        v = _branch_fwd_math(cur_ref, halo_ref, cw_ref, vec_ref, wx_ref, wa_ref, bufa, bufb, bufd, xd, first, t)
        a_s[...] = v["a"]
        u_s[...] = v["ca"] * v["gi"] * v["mult"]
        sv_ref[:, 0:BW] = v["cd"]
        sv_ref[:, BW:2 * BW] = v["gi"]
        sv_ref[:, 2 * BW:3 * BW] = v["gr"]
        hcar[...] = _scan_fwd(a_s, u_s, h_ref, hcar[...], t)
        gg, _ = _gelu_and_grad(cur_ref[:, C_AG:C_AG + BW].astype(F32))
        pre_ref[:, 0:BW] = (h_ref[...] * gg).astype(BF16)
        pre_ref[:, BW:2 * BW] = (cur_ref[:, C_BB:C_BB + BW].astype(F32) * v["cb"]).astype(BF16)
        ln = v["ln"]
        pre_ref[:, 2 * BW:3 * BW] = (ln * _sigmoid(ln)).astype(BF16)

    hb = t // HALO
    return _call(
        body, comm, (proj, proj, convw, vecs, wx_bd, wa_bd), grid=(s // t,),
        in_specs=[pl.BlockSpec((t, GL0), lambda i: (i, 0)),
                  pl.BlockSpec((HALO, GL0), lambda i: (jnp.maximum(i * hb - 1, 0), 0)),
                  pl.BlockSpec((None, CW_ROWS, BW), lambda i: (l, 0, 0)),
                  pl.BlockSpec((None, V_ROWS, BW), lambda i: (l, 0, 0)),
                  pl.BlockSpec((None, BW, BW), lambda i: (l, 0, 0)),
                  pl.BlockSpec((None, BW, BW), lambda i: (l, 0, 0))],
        out_specs=[pl.BlockSpec((t, 3 * BW), lambda i: (i, 0)), pl.BlockSpec((t, BW), lambda i: (i, 0)),
                   pl.BlockSpec((t, 3 * BW), lambda i: (i, 0))],
        out_shape=[_sds((s, 3 * BW), BF16), _sds((s, BW), F32), _sds((s, 3 * BW), F32)],
        scratch_shapes=[pltpu.VMEM((t + HALO, BW), F32)] * 3 + [pltpu.VMEM((7, t + HALO - 8, BW), F32)]
        + [pltpu.VMEM((t, BW), F32)] * 2 + [pltpu.VMEM((1, BW), F32)],
        name=f"fwd_branch{l}")


GRP = N_HEADS // N_KV


ATT_SUB = 2


def _attn_mask_bias(first_block):
    shape = (GRP * ATT_BLK, 2 * ATT_BLK)
    qi = lax.broadcasted_iota(jnp.int32, shape, 0) & (ATT_BLK - 1)
    ki = lax.broadcasted_iota(jnp.int32, shape, 1)
    dist = qi + ATT_BLK - ki
    valid = (dist >= 0) & (dist < ATT_BLK)
    if first_block is not None:
        valid = valid & (jnp.logical_not(first_block) | (ki >= ATT_BLK))
    return dist.astype(F32), valid


def _attn_units(q_ref, kvp_ref, kvc_ref, first_step):
    units = []
    for b in range(ATT_SUB):
        rows = slice(b * ATT_BLK, (b + 1) * ATT_BLK)
        if b == 0:
            prev = lambda c0, c1: kvp_ref[:, c0:c1]
        else:
            prev = lambda c0, c1, b=b: kvc_ref[(b - 1) * ATT_BLK:b * ATT_BLK, c0:c1]
        for hk in range(N_KV):
            units.append(dict(b=b, hk=hk, rows=rows, q=lambda c0, c1, rows=rows: q_ref[rows, c0:c1], prev=prev,
                              cur=lambda c0, c1, rows=rows: kvc_ref[rows, c0:c1], first=first_step if b == 0 else None))
    return units


def _per_head(hk, values):
    hl = lax.broadcasted_iota(jnp.int32, (GRP * ATT_BLK, 1), 0) // ATT_BLK
    out = values[GRP - 1]
    for j in range(GRP - 2, -1, -1):
        out = jnp.where(hl == j, values[j], out)
    return out


def _attn_probs(units, vec_ref):
    us = range(len(units))
    heads = [range(u["hk"] * GRP, (u["hk"] + 1) * GRP) for u in units]
    masks = {id(u["first"]): _attn_mask_bias(u["first"]) for u in units}
    distf = [masks[id(u["first"])][0] for u in units]
    valid = [masks[id(u["first"])][1] for u in units]
    q4 = [jnp.concatenate([units[i]["q"](h * HD, (h + 1) * HD) for h in heads[i]], axis=0) for i in us]
    kcol = [(u["hk"] * HD, (u["hk"] + 1) * HD) for u in units]
    vcol = [((N_KV + u["hk"]) * HD, (N_KV + u["hk"] + 1) * HD) for u in units]
    k2 = [jnp.concatenate([units[i]["prev"](*kcol[i]), units[i]["cur"](*kcol[i])], axis=0) for i in us]
    v2 = [jnp.concatenate([units[i]["prev"](*vcol[i]), units[i]["cur"](*vcol[i])], axis=0) for i in us]
    slope = [_per_head(units[i]["hk"], [2.0 ** (-8.0 * (h + 1) / N_HEADS) for h in heads[i]]) for i in us]
    sink = [_per_head(units[i]["hk"], [vec_ref[V_SINK:V_SINK + 1, h:h + 1] for h in heads[i]]) for i in us]
    sc = [_dot(q4[i], k2[i], _NT) for i in us]
    sc = [jnp.where(valid[i], sc[i] * (HD ** -0.5) - slope[i] * distf[i], NEG_INF) for i in us]
    m = [jnp.maximum(jnp.max(sc[i], axis=-1, keepdims=True), sink[i]) for i in us]
    p = [jnp.exp(sc[i] - m[i]) for i in us]
    es = [jnp.exp(sink[i] - m[i]) for i in us]
    inv = [1.0 / (jnp.sum(p[i], axis=-1, keepdims=True) + es[i]) for i in us]
    return [(q4[i], k2[i], v2[i], p[i] * inv[i], es[i] * inv[i]) for i in us]


def fwd_attn(proj, vecs, l, comm=None):
    s = proj.shape[0]
    t = ATT_SUB * ATT_BLK

    def body(q_ref, kvp_ref, kvc_ref, vec_ref, o_ref):
        units = _attn_units(q_ref, kvp_ref, kvc_ref, pl.program_id(0) == 0)
        groups = _attn_probs(units, vec_ref)
        outs = [_dot(p, v2, _NN).astype(BF16) for _, _, v2, p, _ in groups]
        for u, out in zip(units, outs):
            for j in range(GRP):
                h = u["hk"] * GRP + j
                o_ref[u["rows"], h * HD:(h + 1) * HD] = out[j * ATT_BLK:(j + 1) * ATT_BLK]

    return _call(
        body, comm, (proj, proj, proj, vecs), grid=(s // t,),
        in_specs=[pl.BlockSpec((t, BW), lambda i: (i, C_Q // BW)),
                  pl.BlockSpec((ATT_BLK, 256), lambda i: (jnp.maximum(ATT_SUB * i - 1, 0), C_K // 256)),
                  pl.BlockSpec((t, 256), lambda i: (i, C_K // 256)),
                  pl.BlockSpec((None, V_ROWS, BW), lambda i: (l, 0, 0))],
        out_specs=pl.BlockSpec((t, BW), lambda i: (i, 0)),
        out_shape=_sds((s, BW), BF16), name=f"fwd_attn{l}")


def fwd_merge(x, proj, pre_abd, pre_c, wt_a, wt_b, wt_c, wt_d, w_o, l, comm=None):
    s = x.shape[0]
    tm = min(512, s)

    def body(x_ref, gl_ref, pabd_ref, pc_ref, wa_ref, wb_ref, wc_ref, wd_ref, wo_ref, y_ref, mg_ref, h1_ref):
        pres = (pabd_ref[:, 0:BW], pabd_ref[:, BW:2 * BW], pc_ref[...], pabd_ref[:, 2 * BW:3 * BW])
        merged = jnp.zeros((tm, D), F32)
        for k, (pre, w_ref) in enumerate(zip(pres, (wa_ref, wb_ref, wc_ref, wd_ref))):
            yk = _dot(pre, w_ref[...], _NT)
            y_ref[:, k * D:(k + 1) * D] = yk.astype(BF16)
            merged = merged + _sigmoid(gl_ref[:, k * D:(k + 1) * D].astype(F32)) * yk
        mg_ref[...] = merged.astype(BF16)
        h1_ref[...] = x_ref[...] + _dot(merged, wo_ref[...], _NN)

    wspec = pl.BlockSpec((D, BW), lambda i: (0, 0))
    return _call(
        body, comm, (x, proj, pre_abd, pre_c, wt_a, wt_b, wt_c, wt_d, w_o), grid=(s // tm,),
        in_specs=[pl.BlockSpec((tm, D), lambda i: (i, 0)),
                  pl.BlockSpec((E(tm), E(4 * D)), lambda i: (i * tm, GL0)),
                  pl.BlockSpec((tm, 3 * BW), lambda i: (i, 0)),
                  pl.BlockSpec((tm, BW), lambda i: (i, 0)),
                  wspec, wspec, wspec, wspec,
                  pl.BlockSpec((D, D), lambda i: (0, 0))],
        out_specs=[pl.BlockSpec((tm, 4 * D), lambda i: (i, 0)), pl.BlockSpec((tm, D), lambda i: (i, 0)),
                   pl.BlockSpec((tm, D), lambda i: (i, 0))],
        out_shape=[_sds((s, 4 * D), BF16), _sds((s, D), BF16), _sds((s, D), F32)], name=f"fwd_merge{l}")


def fwd_ffn(h1, g2, wt_gate, wt_up, w_down, l, comm=None):
    s = h1.shape[0]
    tm = min(512, s)
    fc = FF // 2

    def body(h_ref, g_ref, wg_ref, wu_ref, wd_ref, xo_ref, fg_ref, fu_ref, hn_ref, acc_ref):
        j = pl.program_id(1)

        @pl.when(j == 0)
        def _():
            hv = h_ref[...]
            r = lax.rsqrt(jnp.mean(hv * hv, axis=-1, keepdims=True) + EPS)
            hn_ref[...] = (hv * r * g_ref[l:l + 1, :]).astype(BF16)
            acc_ref[...] = hv

        fg = _dot(hn_ref[...], wg_ref[...], _NT)
        fu = _dot(hn_ref[...], wu_ref[...], _NT)
        fg_ref[...] = fg.astype(BF16)
        fu_ref[...] = fu.astype(BF16)
        acc_ref[...] += _dot(fg * _sigmoid(fg) * fu, wd_ref[...], _NN)

        @pl.when(j == pl.num_programs(1) - 1)
        def _():
            xo_ref[...] = acc_ref[...]

    wspec = pl.BlockSpec((fc, D), lambda i, j: (j, 0))
    return _call(
        body, comm, (h1, g2, wt_gate, wt_up, w_down), grid=(s // tm, FF // fc),
        in_specs=[pl.BlockSpec((tm, D), lambda i, j: (i, 0)), pl.BlockSpec((DEPTH, D), lambda i, j: (0, 0)),
                  wspec, wspec, wspec],
        out_specs=[pl.BlockSpec((tm, D), lambda i, j: (i, 0)), pl.BlockSpec((tm, fc), lambda i, j: (i, j)),
                   pl.BlockSpec((tm, fc), lambda i, j: (i, j))],
        out_shape=[_sds((s, D), F32), _sds((s, FF), BF16), _sds((s, FF), BF16)],
        scratch_shapes=[pltpu.VMEM((tm, D), BF16), pltpu.VMEM((tm, D), F32)], name=f"fwd_ffn{l}")


def loss_head(x, gf, target):
    s = x.shape[0]
    tm = min(512, s)

    def body(x_ref, g_ref, t_ref, dx_ref, st_ref):
        @pl.when(pl.program_id(0) == 0)
        def _():
            st_ref[...] = jnp.zeros((8, D), F32)

        xv = x_ref[...]
        g = g_ref[...]
        r = lax.rsqrt(jnp.mean(xv * xv, axis=-1, keepdims=True) + EPS)
        n = xv * r
        err = n * g - t_ref[...]
        dy = err * (1.0 / D)
        dn = dy * g
        dx_ref[...] = r * (dn - n * jnp.mean(dn * n, axis=-1, keepdims=True))
        st_ref[0:1, :] += jnp.sum(dy * n, axis=0, keepdims=True)
        lsum = 0.5 * jnp.sum(jnp.mean(err * err, axis=-1, keepdims=True), axis=0, keepdims=True)
        st_ref[1:2, :] += jnp.broadcast_to(lsum, (1, D))

    return pl.pallas_call(
        body, grid=(s // tm,),
        in_specs=[pl.BlockSpec((tm, D), lambda i: (i, 0)), pl.BlockSpec((1, D), lambda i: (0, 0)),
                  pl.BlockSpec((tm, D), lambda i: (i, 0))],
        out_specs=[pl.BlockSpec((tm, D), lambda i: (i, 0)), pl.BlockSpec((8, D), lambda i: (0, 0))],
        out_shape=[_sds((s, D), F32), _sds((8, D), F32)],
        compiler_params=_cparams(1), name="loss_head")(x, gf, target)


def _edge_index(j, i, n_j, n_i):
    return jnp.where((j == 0) | (j == n_j - 1), i, n_i - 1)


def bwd_ffn(dxo, h1, fg, fu, g2, wt_gate, wt_up, w_down, l, comm=None):
    s = h1.shape[0]
    tm = min(512, s)
    fc = 256
    n_j, n_i = FF // fc, s // tm

    def body(dxo_ref, h_ref, fg_ref, fu_ref, g_ref, wg_ref, wu_ref, wd_ref,
             dh_ref, dwg_ref, dwu_ref, dwd_ref, st_ref, dhn, dxo_b, hn_b, ag, au, ad):
        j, i = pl.program_id(0), pl.program_id(1)
        rows = pl.ds(pl.multiple_of(i * tm, tm), tm)
        g = g_ref[l:l + 1, :]

        @pl.when(j == 0)
        def _():
            hv = h_ref[...]
            r = lax.rsqrt(jnp.mean(hv * hv, axis=-1, keepdims=True) + EPS)
            hn_b[rows, :] = (hv * r * g).astype(BF16)
            dxo_b[rows, :] = dxo_ref[...].astype(BF16)
            dhn[rows, :] = jnp.zeros((tm, D), F32)

        @pl.when((j == 0) & (i == 0))
        def _():
            st_ref[...] = jnp.zeros((8, D), F32)

        @pl.when(i == 0)
        def _():
            ag[...] = jnp.zeros((fc, D), F32)
            au[...] = jnp.zeros((fc, D), F32)
            ad[...] = jnp.zeros((fc, D), F32)

        fgv = fg_ref[...].astype(F32)
        fuv = fu_ref[...].astype(F32)
        sg = _sigmoid(fgv)
        sil = fgv * sg
        dxb = dxo_b[rows, :]
        hnb = hn_b[rows, :]
        d_act = _dot(dxb, wd_ref[...], _NT)
        ad[...] += _dot(sil * fuv, dxb, _TN)
        d_fg = (d_act * fuv * (sg * (1.0 + fgv * (1.0 - sg)))).astype(BF16)
        d_fu = (d_act * sil).astype(BF16)
        ag[...] += _dot(d_fg, hnb, _TN)
        au[...] += _dot(d_fu, hnb, _TN)
        dhn[rows, :] += _dot(d_fg, wg_ref[...], _NN) + _dot(d_fu, wu_ref[...], _NN)

        @pl.when(i == n_i - 1)
        def _():
            dwg_ref[...] = ag[...].astype(BF16)
            dwu_ref[...] = au[...].astype(BF16)
            dwd_ref[...] = ad[...].astype(BF16)

        @pl.when(j == n_j - 1)
        def _():
            hv = h_ref[...]
            r = lax.rsqrt(jnp.mean(hv * hv, axis=-1, keepdims=True) + EPS)
            n = hv * r
            dv = dhn[rows, :]
            dn = dv * g
            dh_ref[...] = dxo_ref[...] + r * (dn - n * jnp.mean(dn * n, axis=-1, keepdims=True))
            st_ref[0:1, :] += jnp.sum(dv * n, axis=0, keepdims=True)

    edge = lambda j, i: (_edge_index(j, i, n_j, n_i), 0)
    wspec = pl.BlockSpec((fc, D), lambda j, i: (j, 0))
    dwspec = pl.BlockSpec((fc, D), lambda j, i: (j, 0))
    return _call(
        body, comm, (dxo, h1, fg, fu, g2, wt_gate, wt_up, w_down), grid=(n_j, n_i),
        in_specs=[pl.BlockSpec((tm, D), edge),
                  pl.BlockSpec((tm, D), edge),
                  pl.BlockSpec((tm, fc), lambda j, i: (i, j)), pl.BlockSpec((tm, fc), lambda j, i: (i, j)),
                  pl.BlockSpec((DEPTH, D), lambda j, i: (0, 0)), wspec, wspec, wspec],
        out_specs=[pl.BlockSpec((tm, D), lambda j, i: (jnp.where(j == n_j - 1, i, 0), 0)),
                   dwspec, dwspec, dwspec, pl.BlockSpec((8, D), lambda j, i: (0, 0))],
        out_shape=[_sds((s, D), F32), _sds((FF, D), BF16), _sds((FF, D), BF16), _sds((FF, D), BF16), _sds((8, D), F32)],
        scratch_shapes=[pltpu.VMEM((s, D), F32), pltpu.VMEM((s, D), BF16), pltpu.VMEM((s, D), BF16),
                        pltpu.VMEM((fc, D), F32), pltpu.VMEM((fc, D), F32), pltpu.VMEM((fc, D), F32)],
        name=f"bwd_ffn{l}")


def bwd_merge(dh1, y4, proj, merged, pre_abd, pre_c, wt_a, wt_b, wt_c, wt_d, w_o, l, comm=None):
    s = dh1.shape[0]
    tm = min(256, s)
    n_i = s // tm

    def body(dh_ref, y_ref, gl_ref, mg_ref, pabd_ref, pc_ref, wa_ref, wb_ref, wc_ref, wd_ref, wo_ref,
             dgl_ref, dpre_ref, dwo_ref, dwa_ref, dwb_ref, dwc_ref, dwd_ref, ao, aa, ab, ac, ad):
        i = pl.program_id(0)
        accs = (aa, ab, ac, ad)

        @pl.when(i == 0)
        def _():
            ao[...] = jnp.zeros((D, D), F32)
            for acc in accs:
                acc[...] = jnp.zeros((D, BW), F32)

        dhb = dh_ref[...].astype(BF16)
        dmg = _dot(dhb, wo_ref[...], _NT)
        ao[...] += _dot(mg_ref[...], dhb, _TN)
        pres = (pabd_ref[:, 0:BW], pabd_ref[:, BW:2 * BW], pc_ref[...], pabd_ref[:, 2 * BW:3 * BW])
        for k, (pre, w_ref, acc) in enumerate(zip(pres, (wa_ref, wb_ref, wc_ref, wd_ref), accs)):
            gk = _sigmoid(gl_ref[:, k * D:(k + 1) * D].astype(F32))
            yk = y_ref[:, k * D:(k + 1) * D].astype(F32)
            dgl_ref[:, k * D:(k + 1) * D] = (dmg * yk * gk * (1.0 - gk)).astype(BF16)
            dyk = (dmg * gk).astype(BF16)
            dpre_ref[:, k * BW:(k + 1) * BW] = _dot(dyk, w_ref[...], _NN).astype(BF16)
            acc[...] += _dot(dyk, pre, _TN)

        @pl.when(i == n_i - 1)
        def _():
            dwo_ref[...] = ao[...].astype(BF16)
            for o_ref, acc in zip((dwa_ref, dwb_ref, dwc_ref, dwd_ref), accs):
                o_ref[...] = acc[...].astype(BF16)

    wspec = pl.BlockSpec((D, BW), lambda i: (0, 0))
    dwspec = pl.BlockSpec((D, BW), lambda i: (0, 0))
    return _call(
        body, comm, (dh1, y4, proj, merged, pre_abd, pre_c, wt_a, wt_b, wt_c, wt_d, w_o), grid=(n_i,),
        in_specs=[pl.BlockSpec((tm, D), lambda i: (i, 0)),
                  pl.BlockSpec((tm, 4 * D), lambda i: (i, 0)),
                  pl.BlockSpec((E(tm), E(4 * D)), lambda i: (i * tm, GL0)),
                  pl.BlockSpec((tm, D), lambda i: (i, 0)),
                  pl.BlockSpec((tm, 3 * BW), lambda i: (i, 0)),
                  pl.BlockSpec((tm, BW), lambda i: (i, 0)),
                  wspec, wspec, wspec, wspec,
                  pl.BlockSpec((D, D), lambda i: (0, 0))],
        out_specs=[pl.BlockSpec((E(tm), E(4 * D)), lambda i: (i * tm, GL0)),
                   pl.BlockSpec((tm, 4 * BW), lambda i: (i, 0)),
                   pl.BlockSpec((D, D), lambda i: (0, 0)), dwspec, dwspec, dwspec, dwspec],
        out_shape=[_sds((s, IN_W), BF16), _sds((s, 4 * BW), BF16), _sds((D, D), BF16)] + [_sds((D, BW), BF16)] * 4,
        scratch_shapes=[pltpu.VMEM((D, D), F32)] + [pltpu.VMEM((D, BW), F32)] * 4, name=f"bwd_merge{l}")


def bwd_attn(proj, dpre, vecs, l, comm=None):
    s = proj.shape[0]
    t = ATT_SUB * ATT_BLK
    grp = N_HEADS // N_KV

    def body(q_ref, kvp_ref, kvc_ref, do_ref, vec_ref, dq_ref, dkc_ref, dkp_ref, st_ref):
        @pl.when(pl.program_id(0) == 0)
        def _():
            st_ref[...] = jnp.zeros((8, 128), F32)

        lane = lax.broadcasted_iota(jnp.int32, (1, 128), 1)
        dsink = jnp.zeros((1, 128), F32)
        units = _attn_units(q_ref, kvp_ref, kvc_ref, pl.program_id(0) == 0)
        groups = _attn_probs(units, vec_ref)
        us = range(len(units))
        do4s = [jnp.concatenate([do_ref[u["rows"], h * HD:(h + 1) * HD] for h in range(u["hk"] * grp, (u["hk"] + 1) * grp)],
                                axis=0) for u in units]
        dps = [_dot(do4s[i], groups[i][2], _NT) for i in us]
        deltas = [jnp.sum(groups[i][3] * dps[i], axis=-1, keepdims=True) for i in us]
        dss = [groups[i][3] * (dps[i] - deltas[i]) * (HD ** -0.5) for i in us]
        dq4s = [_dot(dss[i], groups[i][1], _NN).astype(BF16) for i in us]
        dk2s = [_dot(dss[i], groups[i][0], _TN) for i in us]
        dv2s = [_dot(groups[i][3], do4s[i], _TN) for i in us]
        for i, u in enumerate(units):
            psd = groups[i][4] * deltas[i]
            for j in range(grp):
                h = u["hk"] * grp + j
                rows = slice(j * ATT_BLK, (j + 1) * ATT_BLK)
                dq_ref[u["rows"], h * HD:(h + 1) * HD] = dq4s[i][rows]
                dsink = dsink + jnp.where(lane == h, -jnp.sum(psd[rows], axis=0, keepdims=True), 0.0)
        for i, u in enumerate(units):
            nxt = [k for k, w in enumerate(units) if w["hk"] == u["hk"] and w["b"] == u["b"] + 1]
            for grad, c0 in ((dk2s, u["hk"] * HD), (dv2s, (N_KV + u["hk"]) * HD)):
                own = grad[i][ATT_BLK:]
                if nxt:
                    own = own + grad[nxt[0]][0:ATT_BLK]
                dkc_ref[u["rows"], c0:c0 + HD] = own.astype(BF16)
                if u["b"] == 0:
                    dkp_ref[:, c0:c0 + HD] = grad[i][0:ATT_BLK].astype(BF16)
        st_ref[0:1, :] += dsink

    return _call(
        body, comm, (proj, proj, proj, dpre, vecs), grid=(s // t,),
        in_specs=[pl.BlockSpec((t, BW), lambda i: (i, C_Q // BW)),
                  pl.BlockSpec((ATT_BLK, 256), lambda i: (jnp.maximum(ATT_SUB * i - 1, 0), C_K // 256)),
                  pl.BlockSpec((t, 256), lambda i: (i, C_K // 256)),
                  pl.BlockSpec((t, BW), lambda i: (i, 2)),
                  pl.BlockSpec((None, V_ROWS, BW), lambda i: (l, 0, 0))],
        out_specs=[pl.BlockSpec((t, BW), lambda i: (i, 0)), pl.BlockSpec((t, 256), lambda i: (i, 0)),
                   pl.BlockSpec((ATT_BLK, 256), lambda i: (i, 0)), pl.BlockSpec((8, 128), lambda i: (0, 0))],
        out_shape=[_sds((s, BW), BF16), _sds((s, 256), BF16), _sds((s // ATT_SUB, 256), BF16), _sds((8, 128), F32)],
        name=f"bwd_attn{l}")


def bwd_branch(proj, dproj, dpre, h, saved, dq, dkc, dkp, convw, vecs, wx_bd, wa_bd, l, comm=None):
    s = proj.shape[0]
    t = 2 * ATT_BLK
    nt = s // t
    nb = s // ATT_BLK
    hb = t // HALO

    def body(cur_ref, halo_ref, dpre_ref, h_ref, hp_ref, dq_ref, dkc_ref, dkp_ref,
             cw_ref, vec_ref, wx_ref, wa_ref, sv_ref, dproj_in, dp_ref, dcw_ref, dvec_ref, dwx_ref, dwa_ref,
             bufa, bufb, bufd, xd, xg, a_ext, hbuf, b_s, g_s, dh_s, ga, gb, gd, dhcar):
        del dproj_in
        step = pl.program_id(0)
        ti = nt - 1 - step
        first = ti == 0

        @pl.when(step == 0)
        def _():
            dcw_ref[...] = jnp.zeros((CW_ROWS, BW), F32)
            dvec_ref[...] = jnp.zeros((V_ROWS, BW), F32)
            dwx_ref[...] = jnp.zeros((BW, BW), F32)
            dwa_ref[...] = jnp.zeros((BW, BW), F32)
            dhcar[...] = jnp.zeros((1, BW), F32)
            a_ext[t:t + 8, :] = jnp.zeros((8, BW), F32)
            ga[t:t + 8, :] = jnp.zeros((8, BW), F32)
            gb[t:t + 8, :] = jnp.zeros((8, BW), F32)
            gd[t:t + HALO, :] = jnp.zeros((HALO, BW), F32)

        def cur(c0):
            return cur_ref[:, c0:c0 + BW].astype(F32)

        def rsum(v):
            return jnp.sum(v, axis=0, keepdims=True)

        def put(c0, v):
            dp_ref[:, c0:c0 + BW] = v.astype(BF16)

        v = _branch_fwd_math(cur_ref, halo_ref, cw_ref, vec_ref, wx_ref, wa_ref, bufa, bufb, bufd, xd, first, t, sv_ref)
        ca, gi, gr, sp, a, mult = v["ca"], v["gi"], v["gr"], v["sp"], v["a"], v["mult"]
        dpa = dpre_ref[:, 0:BW].astype(F32)
        gg, dgg = _gelu_and_grad(cur(C_AG))
        hv = h_ref[...]
        put(C_AG, dpa * hv * dgg)
        a_ext[0:t, :] = a
        b_s[...] = a_ext[pl.ds(1, t), :]
        g_s[...] = dpa * gg
        dhcar[...] = _scan_bwd(b_s, g_s, dh_s, dhcar[...], t)
        a_ext[t:t + 1, :] = a[0:1, :]
        dh = dh_s[...]
        hbuf[0:8, :] = jnp.where(first, 0.0, hp_ref[...])
        hbuf[8:8 + t, :] = hv
        da = dh * hbuf[pl.ds(7, t), :]
        d_ca = dh * gi * mult
        d_gi = dh * ca * mult
        d_mult = dh * ca * gi
        d_la = da * a - d_mult * (a * a) / mult
        lam = vec_ref[V_LAM:V_LAM + 1, :]
        dvec_ref[V_LAM:V_LAM + 1, :] += rsum(d_la * gr) * (LRU_C * _sigmoid(-lam))
        d_gr = d_la * (-LRU_C * sp)
        d_zr = d_gr * gr * (1.0 - gr)
        d_zi = d_gi * gi * (1.0 - gi)
        dvec_ref[V_BA:V_BA + 1, :] += rsum(d_zr)
        dvec_ref[V_BX:V_BX + 1, :] += rsum(d_zi)
        dwa_ref[...] += _dot(ca, d_zr, _TN)
        dwx_ref[...] += _dot(ca, d_zi, _TN)
        d_ca = d_ca + _dot(d_zi, wx_ref[...], _NT) + _dot(d_zr, wa_ref[...], _NT)
        dvec_ref[V_CAB:V_CAB + 1, :] += rsum(d_ca)
        ga[0:t, :] = d_ca
        d_ax = jnp.zeros((t, BW), F32)
        for k in range(CONV_A):
            d_ax = d_ax + cw_ref[CW_A + k:CW_A + k + 1, :] * ga[pl.ds(CONV_A - 1 - k, t), :]
            dcw_ref[CW_A + k:CW_A + k + 1, :] += rsum(d_ca * bufa[pl.ds(HALO - (CONV_A - 1) + k, t), :])
        ga[t:t + 8, :] = d_ca[0:8, :]
        put(C_AX, d_ax)
        dpb = dpre_ref[:, BW:2 * BW].astype(F32)
        put(C_BB, dpb * v["cb"])
        d_cb = dpb * cur(C_BB)
        gb[0:t, :] = d_cb
        d_cbin = jnp.zeros((t, BW), F32)
        for k in range(CONV_B):
            d_cbin = d_cbin + cw_ref[CW_B + k:CW_B + k + 1, :] * gb[pl.ds(CONV_B - 1 - k, t), :]
            dcw_ref[CW_B + k:CW_B + k + 1, :] += rsum(d_cb * bufb[pl.ds(HALO - (CONV_B - 1) + k, t), :])
        gb[t:t + 8, :] = d_cb[0:8, :]
        put(C_BC, d_cbin * cur(C_BV))
        put(C_BV, d_cbin * cur(C_BC))
        dpd = dpre_ref[:, 3 * BW:4 * BW].astype(F32)
        ln, xh, rstd, s2 = v["ln"], v["xh"], v["rstd"], v["s2"]
        sg = _sigmoid(ln)
        d_ln = dpd * sg * (1.0 + ln * (1.0 - sg))
        dvec_ref[V_LNG:V_LNG + 1, :] += rsum(d_ln * xh)
        dvec_ref[V_LNB:V_LNB + 1, :] += rsum(d_ln)
        d_xh = d_ln * vec_ref[V_LNG:V_LNG + 1, :]
        d_cd = rstd * (d_xh - jnp.mean(d_xh, axis=-1, keepdims=True)
                       - xh * jnp.mean(d_xh * xh, axis=-1, keepdims=True))
        dvec_ref[V_CDB:V_CDB + 1, :] += rsum(d_cd)
        gd[0:t, :] = d_cd
        _shifted_copies(gd, xg, t + HALO)
        d_dg = jnp.zeros((t, BW), F32)
        for k in range(CONV_D):
            d_dg = d_dg + cw_ref[CW_D + k:CW_D + k + 1, :] * _window(gd, xg, CONV_D - 1 - k, t)
            dcw_ref[CW_D + k:CW_D + k + 1, :] += rsum(d_cd * _window(bufd, xd, HALO - (CONV_D - 1) + k, t))
        gd[t:t + HALO, :] = d_cd[0:HALO, :]
        put(C_D1, d_dg * s2)
        put(C_D2, d_dg * cur(C_D1) * s2 * (1.0 - s2))
        dp_ref[:, C_Q:C_Q + BW] = dq_ref[...]
        dkp = jnp.where(step == 0, 0.0, dkp_ref[...].astype(F32))
        dp_ref[0:t - ATT_BLK, C_K:C_K + 256] = dkc_ref[0:t - ATT_BLK, :]
        dp_ref[t - ATT_BLK:t, C_K:C_K + 256] = (dkc_ref[t - ATT_BLK:t, :].astype(F32) + dkp).astype(BF16)

    rev = lambda i: nt - 1 - i
    full = lambda r, c: pl.BlockSpec((r, c), lambda i: (0, 0))
    return _call(
        body, comm, (proj, proj, dpre, h, h, dq, dkc, dkp, convw, vecs, wx_bd, wa_bd, saved, dproj), grid=(nt,),
        in_specs=[pl.BlockSpec((t, GL0), lambda i: (rev(i), 0)),
                  pl.BlockSpec((HALO, GL0), lambda i: (jnp.maximum(rev(i) * hb - 1, 0), 0)),
                  pl.BlockSpec((t, 4 * BW), lambda i: (rev(i), 0)),
                  pl.BlockSpec((t, BW), lambda i: (rev(i), 0)),
                  pl.BlockSpec((8, BW), lambda i: (jnp.maximum(rev(i) * (t // 8) - 1, 0), 0)),
                  pl.BlockSpec((t, BW), lambda i: (rev(i), 0)),
                  pl.BlockSpec((t, 256), lambda i: (rev(i), 0)),
                  pl.BlockSpec((ATT_BLK, 256), lambda i: (jnp.minimum(rev(i) + 1, nt - 1), 0)),
                  pl.BlockSpec((None, CW_ROWS, BW), lambda i: (l, 0, 0)),
                  pl.BlockSpec((None, V_ROWS, BW), lambda i: (l, 0, 0)),
                  pl.BlockSpec((None, BW, BW), lambda i: (l, 0, 0)),
                  pl.BlockSpec((None, BW, BW), lambda i: (l, 0, 0)),
                  pl.BlockSpec((t, 3 * BW), lambda i: (rev(i), 0)),
                  pl.BlockSpec(memory_space=pl.ANY)],
        out_specs=[pl.BlockSpec((t, GL0), lambda i: (rev(i), 0)),
                   full(CW_ROWS, BW), full(V_ROWS, BW), full(BW, BW), full(BW, BW)],
        out_shape=[_sds((s, IN_W), BF16), _sds((CW_ROWS, BW), F32), _sds((V_ROWS, BW), F32),
                   _sds((BW, BW), F32), _sds((BW, BW), F32)],
        scratch_shapes=[pltpu.VMEM((t + HALO, BW), F32)] * 3 + [pltpu.VMEM((7, t + HALO - 8, BW), F32)] * 2
        + [pltpu.VMEM((t + 8, BW), F32), pltpu.VMEM((t + 8, BW), F32)]
        + [pltpu.VMEM((t, BW), F32)] * 3
        + [pltpu.VMEM((t + 8, BW), F32), pltpu.VMEM((t + 8, BW), F32), pltpu.VMEM((t + HALO, BW), F32),
           pltpu.VMEM((1, BW), F32)],
        aliases={13: 0}, name=f"bwd_branch{l}")


def bwd_proj(dproj, x, dh1, g1, wt_in, l, comm=None):
    s = x.shape[0]
    tm = min(512, s)
    ck = 1408
    n_j, n_i = IN_W // ck, s // tm

    def body(dp_ref, x_ref, dh_ref, g_ref, w_ref, dx_ref, dw_ref, st_ref, dxn, xn_b, acc):
        j, i = pl.program_id(0), pl.program_id(1)
        rows = pl.ds(pl.multiple_of(i * tm, tm), tm)
        g = g_ref[l:l + 1, :]

        @pl.when(j == 0)
        def _():
            xv = x_ref[...]
            r = lax.rsqrt(jnp.mean(xv * xv, axis=-1, keepdims=True) + EPS)
            xn_b[rows, :] = (xv * r * g).astype(BF16)
            dxn[rows, :] = jnp.zeros((tm, D), F32)

        @pl.when((j == 0) & (i == 0))
        def _():
            st_ref[...] = jnp.zeros((8, D), F32)

        @pl.when(i == 0)
        def _():
            acc[...] = jnp.zeros((ck, D), F32)

        dp = dp_ref[...]
        dxn[rows, :] += _dot(dp, w_ref[...], _NN)
        acc[...] += _dot(dp, xn_b[rows, :], _TN)

        @pl.when(i == n_i - 1)
        def _():
            dw_ref[...] = acc[...].astype(BF16)

        @pl.when(j == n_j - 1)
        def _():
            xv = x_ref[...]
            r = lax.rsqrt(jnp.mean(xv * xv, axis=-1, keepdims=True) + EPS)
            n = xv * r
            dv = dxn[rows, :]
            dn = dv * g
            dx_ref[...] = dh_ref[...] + r * (dn - n * jnp.mean(dn * n, axis=-1, keepdims=True))
            st_ref[0:1, :] += jnp.sum(dv * n, axis=0, keepdims=True)

    lastrow = lambda j, i: (jnp.where(j == n_j - 1, i, 0), 0)
    return _call(
        body, comm, (dproj, x, dh1, g1, wt_in), grid=(n_j, n_i),
        in_specs=[pl.BlockSpec((tm, ck), lambda j, i: (i, j)),
                  pl.BlockSpec((tm, D), lambda j, i: (_edge_index(j, i, n_j, n_i), 0)),
                  pl.BlockSpec((tm, D), lastrow),
                  pl.BlockSpec((DEPTH, D), lambda j, i: (0, 0)),
                  pl.BlockSpec((ck, D), lambda j, i: (j, 0))],
        out_specs=[pl.BlockSpec((tm, D), lastrow), pl.BlockSpec((ck, D), lambda j, i: (j, 0)),
                   pl.BlockSpec((8, D), lambda j, i: (0, 0))],
        out_shape=[_sds((s, D), F32), _sds((IN_W, D), BF16), _sds((8, D), F32)],
        scratch_shapes=[pltpu.VMEM((s, D), F32), pltpu.VMEM((s, D), BF16), pltpu.VMEM((ck, D), F32)],
        name=f"bwd_proj{l}")


def bwd_proj_w(dproj, x, g1, l, half, comm=None):
    s = x.shape[0]
    tm = min(1024, s)
    ck = 1408
    c0, hw = W_IN_PARTS[half]
    n_j, n_i = IN_W // ck, s // tm

    def body(dp_ref, x_ref, g_ref, dw_ref, xn_b, acc):
        j, i = pl.program_id(0), pl.program_id(1)
        rows = pl.ds(pl.multiple_of(i * tm, tm), tm)

        @pl.when(j == 0)
        def _():
            xv = x_ref[...]
            r = lax.rsqrt(jnp.mean(xv * xv, axis=-1, keepdims=True) + EPS)
            xn_b[rows, :] = (xv * r * g_ref[l:l + 1, :])[:, c0:c0 + hw].astype(BF16)

        @pl.when(i == 0)
        def _():
            acc[...] = jnp.zeros((ck, hw), F32)

        acc[...] += _dot(dp_ref[...], xn_b[rows, :], _TN)

        @pl.when(i == n_i - 1)
        def _():
            dw_ref[...] = acc[...].astype(BF16)

    return _call(
        body, comm, (dproj, x, g1), grid=(n_j, n_i),
        in_specs=[pl.BlockSpec((tm, ck), lambda j, i: (i, j)),
                  pl.BlockSpec((tm, D), lambda j, i: (jnp.where(j == 0, i, n_i - 1), 0)),
                  pl.BlockSpec((DEPTH, D), lambda j, i: (0, 0))],
        out_specs=pl.BlockSpec((ck, hw), lambda j, i: (j, 0)),
        out_shape=_sds((IN_W, hw), BF16),
        scratch_shapes=[pltpu.VMEM((s, hw), BF16), pltpu.VMEM((ck, hw), F32)],
        name=f"bwd_proj_w{half}_{l}")


def bwd_proj_x(dproj, x, dh1, g1, wt_in, l, comm=None):
    s = x.shape[0]
    tm = min(512, s)
    ck = 1408
    n_j, n_i = IN_W // ck, s // tm

    def body(dp_ref, x_ref, dh_ref, g_ref, w_ref, dx_ref, st_ref, dxn):
        j, i = pl.program_id(0), pl.program_id(1)
        rows = pl.ds(pl.multiple_of(i * tm, tm), tm)
        g = g_ref[l:l + 1, :]

        @pl.when((j == 0) & (i == 0))
        def _():
            st_ref[...] = jnp.zeros((8, D), F32)

        part = _dot(dp_ref[...], w_ref[...], _NN)

        @pl.when(j == 0)
        def _():
            dxn[rows, :] = part

        @pl.when(j > 0)
        def _():
            dxn[rows, :] += part

        @pl.when(j == n_j - 1)
        def _():
            xv = x_ref[...]
            r = lax.rsqrt(jnp.mean(xv * xv, axis=-1, keepdims=True) + EPS)
            n = xv * r
            dv = dxn[rows, :]
            dn = dv * g
            dx_ref[...] = dh_ref[...] + r * (dn - n * jnp.mean(dn * n, axis=-1, keepdims=True))
            st_ref[0:1, :] += jnp.sum(dv * n, axis=0, keepdims=True)

    lastrow = lambda j, i: (jnp.where(j == n_j - 1, i, 0), 0)
    return _call(
        body, comm, (dproj, x, dh1, g1, wt_in), grid=(n_j, n_i),
        in_specs=[pl.BlockSpec((tm, ck), lambda j, i: (i, j)), pl.BlockSpec((tm, D), lastrow),
                  pl.BlockSpec((tm, D), lastrow),
                  pl.BlockSpec((DEPTH, D), lambda j, i: (0, 0)), pl.BlockSpec((ck, D), lambda j, i: (j, 0))],
        out_specs=[pl.BlockSpec((tm, D), lastrow), pl.BlockSpec((8, D), lambda j, i: (0, 0))],
        out_shape=[_sds((s, D), F32), _sds((8, D), F32)],
        scratch_shapes=[pltpu.VMEM((s, D), F32)], name=f"bwd_proj_x{l}")


def _block_diag(w):
    nl, nb, bw, _ = w.shape
    eye = jnp.eye(nb, dtype=w.dtype)
    return jnp.einsum("lhij,hk->lhikj", w, eye).reshape(nl, nb * bw, nb * bw).astype(BF16)


class NoOverlap:
    def __init__(self, big):
        self.big = big

    def weights(self, l):
        return self.big[l]

    def job(self, slot, l):
        return None

    def done(self, slot, l, results):
        pass

    def new_grads(self, group, l, grads):
        pass

    def new_small(self, l, arrays, head_stats):
        pass


def local_step(x, target, norm1_g, norm2_g, final_g, convw, vecs, lru_wx, lru_wa, plan):
    wx_bd, wa_bd = _block_diag(lru_wx), _block_diag(lru_wa)

    def run(fn, slot, l, *args):
        res, cres = fn(*args, l, comm=plan.job(slot, l))
        plan.done(slot, l, cres)
        return res

    saved = []
    for l in range(DEPTH):
        proj = run(fwd_proj, "fwd_proj", l, x, norm1_g, plan.weights(l)["in_t"])
        pre_abd, h, kept = run(fwd_branch, "fwd_branch", l, proj, convw, vecs, wx_bd, wa_bd)
        pre_c = run(fwd_attn, "fwd_attn", l, proj, vecs)
        w = plan.weights(l)
        y4, merged, h1 = run(fwd_merge, "fwd_merge", l, x, proj, pre_abd, pre_c, w["a_t"], w["b_t"], w["c_t"], w["d_t"], w["o"])
        w = plan.weights(l)
        x_out, fg, fu = run(fwd_ffn, "fwd_ffn", l, h1, norm2_g, w["gate_t"], w["up_t"], w["down"])
        saved.append((x, proj, pre_abd, h, kept, pre_c, y4, merged, h1, fg, fu))
        x = x_out
    dx, head_stats = loss_head(x, final_g.reshape(1, D), target)
    small = [None] * DEPTH
    for l in reversed(range(DEPTH)):
        x_in, proj, pre_abd, h, kept, pre_c, y4, merged, h1, fg, fu = saved[l]
        w = plan.weights(l)
        dh1, d_gate, d_up, d_down, st_ffn = run(bwd_ffn, "bwd_ffn", l, dx, h1, fg, fu, norm2_g, w["gate_t"], w["up_t"], w["down"])
        plan.new_grads("ffn", l, dict(gate_t=d_gate, up_t=d_up, down=d_down))
        dproj, dpre, d_o, d_a, d_b, d_c, d_d = run(
            bwd_merge, "bwd_merge", l, dh1, y4, proj, merged, pre_abd, pre_c, w["a_t"], w["b_t"], w["c_t"], w["d_t"], w["o"])
        plan.new_grads("out", l, dict(a_t=d_a, b_t=d_b, c_t=d_c, d_t=d_d, o=d_o))
        dq, dkc, dkp, st_attn = run(bwd_attn, "bwd_attn", l, proj, dpre, vecs)
        dproj, dcw, dvec, dwx, dwa = run(bwd_branch, "bwd_branch", l, proj, dproj, dpre, h, kept, dq, dkc, dkp, convw, vecs, wx_bd, wa_bd)
        if l > 0:
            dx, d_in, st_proj = run(bwd_proj, "bwd_proj", l, dproj, x_in, dh1, norm1_g, w["in_t"])
            plan.new_grads("in", l, dict(in_t=d_in))
        else:
            for half, name in enumerate(("in_a", "in_b")):
                d_half = run(functools.partial(bwd_proj_w, half=half), f"bwd_proj_w{half}", l, dproj, x_in, norm1_g)
                plan.new_grads(name, l, {name: d_half})
            dx, st_proj = run(bwd_proj_x, "bwd_proj_x", l, dproj, x_in, dh1, norm1_g, w["in_t"])
        small[l] = (st_proj, st_ffn, dvec, st_attn, dcw, dwx, dwa)
        plan.new_small(l, small[l], head_stats)
    return head_stats, dx, small


BIG = dict(in_t=("w_in", "view"), a_t=("w_a_out", "transpose"), b_t=("w_b_out", "transpose"), c_t=("w_c_out", "transpose"),
           d_t=("w_d_out", "transpose"), o=("w_o", "plain"), gate_t=("w_ffn_gate", "view"), up_t=("w_ffn_up", "view"),
           down=("w_ffn_down", "plain"))


def cast_transpose(ws, name):
    n = len(ws)
    nl, a, b = ws[0].shape
    ta = min(256, a)

    def body(*refs):
        for w_ref, o_ref in zip(refs[:n], refs[n:]):
            o_ref[...] = w_ref[...].T.astype(BF16)

    return pl.pallas_call(
        body, grid=(nl, a // ta),
        in_specs=[pl.BlockSpec((None, ta, b), lambda l, i: (l, i, 0))] * n,
        out_specs=[pl.BlockSpec((None, b, ta), lambda l, i: (l, 0, i))] * n,
        out_shape=[_sds((nl, b, a), BF16)] * n, compiler_params=_cparams(2), name=name)(*ws)


def add_partials(mine, recv, core, name):
    n = len(mine)

    def body(core_ref, *refs):
        del core_ref
        for a_ref, b_ref, o_ref in zip(refs[:n], refs[n:2 * n], refs[2 * n:]):
            o_ref[...] = (a_ref[...].astype(F32) + b_ref[...].astype(F32)).astype(BF16)

    return pl.pallas_call(
        body,
        grid_spec=pltpu.PrefetchScalarGridSpec(
            num_scalar_prefetch=1, grid=(4,),
            in_specs=[pl.BlockSpec((None, None) + a.shape[2:], lambda i, cr: (i, cr[0], 0, 0)) for a in mine]
            + [pl.BlockSpec((None,) + b.shape[1:], lambda i, cr: (i, 0, 0)) for b in recv],
            out_specs=[pl.BlockSpec((None,) + b.shape[1:], lambda i, cr: (i, 0, 0)) for b in recv]),
        out_shape=[_sds(b.shape, BF16) for b in recv], compiler_params=_cparams(1), name=name)(core, *mine, *recv)


def _adamw(w, g, m, v):
    m = ADAM_B1 * m + (1.0 - ADAM_B1) * g
    v = ADAM_B2 * v + (1.0 - ADAM_B2) * (g * g)
    m_hat = m / (1.0 - ADAM_B1 ** ADAM_STEP)
    v_hat = v / (1.0 - ADAM_B2 ** ADAM_STEP)
    delta = -ADAM_LR * (m_hat / (jnp.sqrt(v_hat) + ADAM_EPS) + ADAM_WD * w)
    return delta, m, v


def adamw_big(items, name, comm=None):
    n_tiles = 4
    n = len(items)
    nl = items[0][0].shape[1]

    def body(*refs):
        ins, outs = refs[:4 * n], refs[4 * n:]
        for k, (contrib, _, _, _, transposed) in enumerate(items):
            c_ref, w_ref, m_ref, v_ref = ins[4 * k:4 * k + 4]
            g = c_ref[0].astype(F32)
            for src in range(1, contrib.shape[0]):
                g = g + c_ref[src].astype(F32)
            if transposed:
                g = g.T
            delta, mn, vn = _adamw(w_ref[...], g, m_ref[...], v_ref[...])
            for o_ref, val in zip(outs[4 * k:4 * k + 4], (g, delta, mn, vn)):
                o_ref[...] = val

    in_specs, out_specs, out_shape, args = [], [], [], []
    for contrib, w, m, v, transposed in items:
        nsrc, _, rows, cols = contrib.shape
        ct = cols // n_tiles
        if transposed:
            wspec = pl.BlockSpec((None, ct, rows), lambda l, j: (l, j, 0))
        else:
            wspec = pl.BlockSpec((None, rows, ct), lambda l, j: (l, 0, j))
        in_specs += [pl.BlockSpec((nsrc, None, rows, ct), lambda l, j: (0, l, 0, j)), wspec, wspec, wspec]
        out_specs += [wspec] * 4
        out_shape += [_sds(w.shape, F32)] * 4
        args += [contrib, w, m, v]
    res, cres = _call(body, comm, tuple(args), grid=(nl, n_tiles), in_specs=in_specs, out_specs=out_specs,
                      out_shape=out_shape, name=name)
    return [res[4 * k:4 * k + 4] for k in range(n)], cres


VEC_NAMES = ("conv_a_b", "lru_bx", "lru_ba", "lru_lambda", "conv_d_b", "ln_d_g", "ln_d_b")
P_N1, P_N2, P_VEC, P_CONV, P_LRU = 0, 1, 2, 6, 6 + CW_ROWS
P_FINAL, P_LOSS, P_ROWS = P_LRU + HD, P_LRU + HD + 1, P_LRU + HD + 2
SMALL = ("norm1_g", "conv_a_w", "conv_a_b", "lru_wx", "lru_bx", "lru_wa", "lru_ba", "lru_lambda", "conv_b_w", "sinks",
         "conv_d_w", "conv_d_b", "ln_d_g", "ln_d_b", "norm2_g", "final_g")
VMEM_FULL = pl.BlockSpec(memory_space=pltpu.VMEM)


def _stack_vecs(p):
    rows = [p[n] for n in VEC_NAMES] + [jnp.pad(p["sinks"], ((0, 0), (0, BW - N_HEADS)))]
    return jnp.stack(rows, axis=1)


def _stack_convs(p):
    nl, _, ch = p["conv_a_w"].shape
    z = jnp.zeros((nl, 1, ch), F32)
    return jnp.concatenate([p["conv_a_w"], p["conv_b_w"], z, p["conv_d_w"], z], axis=1)


def _vec_place(r):
    return P_VEC + r // 2, (r % 2) * BW


def pack_small(arrays, head_stats, l):
    n = len(arrays)

    def body(*refs):
        st_proj, st_ffn, dvec, st_attn, dcw, dwx, dwa = refs[:n]
        pack = refs[-1]
        pack[...] = jnp.zeros((P_ROWS, D), F32)
        lane = lax.broadcasted_iota(jnp.int32, (HD, BW), 1)
        pack[P_N1:P_N1 + 1, :] = st_proj[0:1, :]
        pack[P_N2:P_N2 + 1, :] = st_ffn[0:1, :]
        for r in range(len(VEC_NAMES)):
            row, c0 = _vec_place(r)
            pack[row:row + 1, c0:c0 + BW] = dvec[r:r + 1, :]
        row, c0 = _vec_place(V_SINK)
        pack[row:row + 1, c0:c0 + 128] = st_attn[0:1, :]
        pack[P_CONV:P_CONV + CW_ROWS, 0:BW] = dcw[...]
        for mat, c0 in ((dwx, 0), (dwa, BW)):
            blocks = jnp.zeros((HD, BW), F32)
            for h in range(BW // HD):
                blocks = jnp.where((lane >= HD * h) & (lane < HD * (h + 1)), mat[HD * h:HD * (h + 1), :], blocks)
            pack[P_LRU:P_LRU + HD, c0:c0 + BW] = blocks
        if head_stats is not None:
            pack[P_FINAL:P_LOSS + 1, :] = refs[n][0:2, :]

    flat = list(arrays) + ([] if head_stats is None else [head_stats])
    return pl.pallas_call(body, out_shape=_sds((P_ROWS, D), F32), in_specs=[VMEM_FULL] * len(flat), out_specs=VMEM_FULL,
                          name=f"pack_small{l}", compiler_params=pltpu.CompilerParams(vmem_limit_bytes=VMEM_LIMIT))(*flat)


def adamw_small(gathered, me, w, m, v):
    ns = len(SMALL)

    def body(me_ref, *refs):
        c_refs, refs = refs[:DEPTH], refs[DEPTH:]
        w_refs, m_refs, v_refs = refs[:ns], refs[ns:2 * ns], refs[2 * ns:3 * ns]
        loss_ref, outs, gs = refs[3 * ns], refs[3 * ns + 1:3 * ns + 1 + 4 * ns], refs[-1]
        for l in range(DEPTH):
            gs[l] = c_refs[l][0]
            for dev in range(1, NDEV):
                gs[l] += c_refs[l][dev]
        loss_ref[...] = gs[DEPTH - 1, P_LOSS:P_LOSS + 1, 0:128]

        def update(name, sel, g):
            i = SMALL.index(name)
            delta, mn, vn = _adamw(w_refs[i][sel], g, m_refs[i][sel], v_refs[i][sel])
            for o_ref, val in zip(outs[4 * i:4 * i + 4], (g, delta, mn, vn)):
                o_ref[sel] = val

        update("final_g", (slice(0, 1), slice(None)), gs[DEPTH - 1, P_FINAL:P_FINAL + 1, :])
        shift = (BW - me_ref[0] * (BW // NDEV)) & (BW - 1)
        for l in range(DEPTH):
            row = (slice(l, l + 1), slice(None))
            update("norm1_g", row, gs[l, P_N1:P_N1 + 1, :])
            update("norm2_g", row, gs[l, P_N2:P_N2 + 1, :])
            for r, name in enumerate(VEC_NAMES):
                prow, c0 = _vec_place(r)
                update(name, row, gs[l, prow:prow + 1, c0:c0 + BW])
            prow, c0 = _vec_place(V_SINK)
            update("sinks", row, gs[l, prow:prow + 1, c0:c0 + N_HEADS])
            mine = pltpu.roll(gs[l, P_CONV:P_CONV + CW_ROWS, 0:BW], shift, 1)[:, 0:BW // NDEV]
            update("conv_a_w", (l,), mine[CW_A:CW_A + CONV_A])
            update("conv_b_w", (l,), mine[CW_B:CW_B + CONV_B])
            update("conv_d_w", (l,), mine[CW_D:CW_D + CONV_D])
            for h in range(BW // HD):
                update("lru_wx", (l, h), gs[l, P_LRU:P_LRU + HD, HD * h:HD * (h + 1)])
                update("lru_wa", (l, h), gs[l, P_LRU:P_LRU + HD, BW + HD * h:BW + HD * (h + 1)])

    args = [p[n] for p in (w, m, v) for n in SMALL]
    full = lambda a: pl.BlockSpec(a.shape, lambda i, me_ref: (0,) * a.ndim)
    out_shape = [_sds((1, 128), F32)] + [_sds(w[n].shape, F32) for n in SMALL for _ in range(4)]
    outs = pl.pallas_call(
        body,
        grid_spec=pltpu.PrefetchScalarGridSpec(
            num_scalar_prefetch=1, grid=(1,),
            in_specs=[full(a) for a in list(gathered) + args], out_specs=[full(o) for o in out_shape],
            scratch_shapes=[pltpu.VMEM((DEPTH, P_ROWS, D), F32)]),
        out_shape=out_shape, name="adamw_small", compiler_params=_cparams(1))(me, *gathered, *args)
    return outs[0], {n: outs[1 + 4 * i:5 + 4 * i] for i, n in enumerate(SMALL)}


def merge_jobs(jobs):
    jobs = [j for j in jobs if j is not None]
    if not jobs:
        return None, []
    inputs, aliases, outs, sems, cuts = [], {}, [], [], []
    for j in jobs:
        i0, o0, s0 = len(inputs), len(outs), len(sems)
        aliases.update({i0 + i: o0 + o for i, o in j.aliases.items()})
        inputs += j.inputs
        outs += j.out_shapes
        sems += j.sem_shapes
        cuts.append((i0, len(inputs), o0, len(outs), s0, len(sems)))

    def each(which):
        def go(cins, couts, s):
            for j, (i0, i1, o0, o1, s0, s1) in zip(jobs, cuts):
                if getattr(j, which) is not None:
                    getattr(j, which)(cins[i0:i1], couts[o0:o1], s[s0:s1])
        return go

    relay = each("relay") if any(j.relay is not None for j in jobs) else None
    return CommJob(inputs, aliases, outs, sems, each("start"), each("finish"), relay), [(c[2], c[3]) for c in cuts]


SIXTHS = 6
OUT_KINDS = ("a_t", "b_t", "c_t", "d_t", "o")
GATHER_PLAN = {
    "fwd_proj": [(k, 0, 0, 6) for k in OUT_KINDS] + [("gate_t", 0, 0, 6)],
    "fwd_branch": [("up_t", 0, 0, 6)],
    "fwd_attn": [("down", 0, 0, 6)],
    "fwd_merge": [("in_t", 1, 0, 2)],
    "fwd_ffn": [("in_t", 1, 2, 6)],
}
SIBLING_PLAN = {"bwd_merge": ("ffn", 0), "bwd_branch": ("out", 0), "bwd_ffn": ("in", 1),
                "bwd_proj_w1": ("in_a", 0), "bwd_proj_x": ("in_b", 0)}
GROUPS = dict(ffn=("gate_t", "up_t", "down"), out=OUT_KINDS, in_a=("in_a",), in_b=("in_b",))
GROUPS["in"] = ("in_t",)
COLUMN_HALF = dict(in_a=("in_t", W_IN_PARTS[0][0]), in_b=("in_t", W_IN_PARTS[1][0]))
CHIP_PLAN = {
    "bwd_attn": [("in_t", 1, 3, 5)],
    "bwd_branch": [("in_t", 1, 5, 6), ("gate_t", 0, 0, 6), ("up_t", 0, 0, 6), ("down", 0, 0, 3)],
    "bwd_proj": [(k, 0, 0, 6) for k in OUT_KINDS] + [("down", 0, 3, 6)],
    "bwd_proj_w0": [(k, 0, 0, 6) for k in OUT_KINDS[:3]] + [("down", 0, 3, 6)],
    "bwd_proj_w1": [(k, 0, 0, 6) for k in OUT_KINDS[3:]],
    "bwd_merge": [("in_t", 1, 0, 3)],
    "bwd_proj_x": [("in_a", 0, 0, 5)],
    "adamw_rest": [("in_a", 0, 5, 6), ("in_b", 0, 0, 6)],
}
SMALL_GATHER_PLAN = {"bwd_ffn": 1, "adamw_rest": 0}


class Overlap:
    def __init__(self, shards, core):
        self.shards = shards
        self.core = core
        self.gathered = [dict.fromkeys(BIG) for _ in range(DEPTH)]
        self.views = {}
        self.partial = {}
        self.contrib = dict.fromkeys(BIG)
        self.small_packs = [None] * DEPTH
        self.small_gathered = [None] * DEPTH
        self._open = None

    def weights(self, l):
        return self.gathered[l]

    def new_grads(self, group, l, grads):
        for k, g in grads.items():
            self.views[k, l] = g.reshape(4, 2, g.shape[0] // NDEV, g.shape[1])

    def new_small(self, l, arrays, head_stats):
        self.small_packs[l] = pack_small(arrays, head_stats if l == DEPTH - 1 else None, l)

    @staticmethod
    def _rows(shard_rows, f0, f1):
        return shard_rows * f0 // SIXTHS, shard_rows * (f1 - f0) // SIXTHS

    def job(self, slot, l):
        jobs, notes = [], []
        pieces = [(k, l + dl, f0, f1) for k, dl, f0, f1 in GATHER_PLAN.get(slot, []) if l + dl < DEPTH]
        if pieces:
            jobs.append(gather_job([((k, ll), self.shards[ll][k], self.gathered[ll][k],
                                     *self._rows(self.shards[ll][k].shape[0], f0, f1)) for k, ll, f0, f1 in pieces]))
            notes.append(("gather", list(dict.fromkeys((k, ll) for k, ll, _, _ in pieces))))
        if slot in SIBLING_PLAN and l + SIBLING_PLAN[slot][1] < DEPTH:
            group, dl = SIBLING_PLAN[slot]
            keys = [(k, l + dl) for k in GROUPS[group]]
            jobs.append(sibling_exchange_job([self.views[key] for key in keys]))
            notes.append(("sibling", keys))
        pieces = [(k, l + dl, f0, f1) for k, dl, f0, f1 in CHIP_PLAN.get(slot, []) if l + dl < DEPTH]
        if pieces:
            whole = [(*COLUMN_HALF.get(k, (k, 0)), k, ll, f0, f1) for k, ll, f0, f1 in pieces]
            jobs.append(chip_exchange_job([(self.partial[k, ll], self.contrib[kind], kind, ll,
                                            *self._rows(self.partial[k, ll].shape[1], f0, f1), col0, self.shards[ll][kind].shape[1])
                                           for kind, col0, k, ll, f0, f1 in whole]))
            notes.append(("chips", list(dict.fromkeys(kind for kind, *_ in whole))))
        if slot in SMALL_GATHER_PLAN and l + SMALL_GATHER_PLAN[slot] < DEPTH:
            ll = l + SMALL_GATHER_PLAN[slot]
            jobs.append(gather_job([("small", self.small_packs[ll], None, 0, P_ROWS)]))
            notes.append(("small", ll))
        job, spans = merge_jobs(jobs)
        self._open = (slot, l, notes, spans)
        return job

    def done(self, slot, l, results):
        open_slot, open_l, notes, spans = self._open
        assert (open_slot, open_l) == (slot, l)
        for (what, keys), (r0, r1) in zip(notes, spans):
            res = results[r0:r1]
            if what == "gather":
                for (k, ll), g in zip(keys, res):
                    self.gathered[ll][k] = g
            elif what == "sibling":
                sums = add_partials([self.views[key] for key in keys], list(res), self.core, f"chip_sum_{keys[0][0]}{keys[0][1]}")
                self.partial.update(zip(keys, sums))
            elif what == "chips":
                for k, c in zip(keys, res):
                    self.contrib[k] = c
            else:
                self.small_gathered[keys], = res


SMALL = ("norm1_g", "conv_a_w", "conv_a_b", "lru_wx", "lru_bx", "lru_wa", "lru_ba", "lru_lambda", "conv_b_w", "sinks",
         "conv_d_w", "conv_d_b", "ln_d_g", "ln_d_b", "norm2_g", "final_g")
WEIGHTS = ("norm1_g", "w_in", "conv_a_w", "conv_a_b", "lru_wx", "lru_bx", "lru_wa", "lru_ba", "lru_lambda", "w_a_out",
           "conv_b_w", "w_b_out", "sinks", "w_c_out", "conv_d_w", "conv_d_b", "ln_d_g", "ln_d_b", "w_d_out", "w_o",
           "norm2_g", "w_ffn_gate", "w_ffn_up", "w_ffn_down", "final_g")


def kernel(x, norm1_g, w_in, conv_a_w, conv_a_b, lru_wx, lru_bx, lru_wa, lru_ba, lru_lambda, w_a_out, conv_b_w, w_b_out, sinks, w_c_out, conv_d_w, conv_d_b, ln_d_g, ln_d_b, w_d_out, w_o, norm2_g, w_ffn_gate, w_ffn_up, w_ffn_down, final_g, loss_target, m_norm1_g, m_w_in, m_conv_a_w, m_conv_a_b, m_lru_wx, m_lru_bx, m_lru_wa, m_lru_ba, m_lru_lambda, m_w_a_out, m_conv_b_w, m_w_b_out, m_sinks, m_w_c_out, m_conv_d_w, m_conv_d_b, m_ln_d_g, m_ln_d_b, m_w_d_out, m_w_o, m_norm2_g, m_w_ffn_gate, m_w_ffn_up, m_w_ffn_down, m_final_g, v_norm1_g, v_w_in, v_conv_a_w, v_conv_a_b, v_lru_wx, v_lru_bx, v_lru_wa, v_lru_ba, v_lru_lambda, v_w_a_out, v_conv_b_w, v_w_b_out, v_sinks, v_w_c_out, v_conv_d_w, v_conv_d_b, v_ln_d_g, v_ln_d_b, v_w_d_out, v_w_o, v_norm2_g, v_w_ffn_gate, v_w_ffn_up, v_w_ffn_down, v_final_g):
    args = dict(locals())
    w = {n: args[n] for n in WEIGHTS}
    m = {n: args["m_" + n] for n in WEIGHTS}
    v = {n: args["v_" + n] for n in WEIGHTS}
    me = _dev_index(*_mesh_pos())

    def rows_major(a, how):
        return jnp.swapaxes(a, 1, 2) if how == "view" else a

    stacked = {k: rows_major(w[n], how).astype(BF16) for k, (n, how) in BIG.items() if how != "transpose"}
    turned = [k for k, (n, how) in BIG.items() if how == "transpose"]
    stacked.update(zip(turned, cast_transpose([w[BIG[k][0]] for k in turned], "prep_transposed")))
    plan = Overlap([{k: stacked[k][l] for k in BIG} for l in range(DEPTH)], lax.axis_index("c").astype(jnp.int32).reshape(1))
    convs = jnp.pad(_stack_convs(w).reshape(DEPTH * CW_ROWS, BW // NDEV), ((0, 0), (0, 256 - BW // NDEV)))
    g_in0, g_conv = _comm_only(gather_job([(("in_t", 0), plan.shards[0]["in_t"], None, 0, plan.shards[0]["in_t"].shape[0]),
                                           ("convs", convs, None, 0, convs.shape[0])]), "gather_first")
    plan.gathered[0]["in_t"] = g_in0
    convw = g_conv[:, :BW // NDEV].reshape(NDEV, DEPTH, CW_ROWS, BW // NDEV).transpose(1, 2, 0, 3).reshape(DEPTH, CW_ROWS, BW)

    vecs = _stack_vecs(w)
    head_stats, grad_x, grads = local_step(x[0], loss_target[0], norm1_g, norm2_g, final_g, convw, vecs, lru_wx, lru_wa, plan)


    out = {}
    for slot, kinds in (("adamw_rest", [k for k in BIG if k != "in_t"]), ("adamw_in_t", ["in_t"])):
        job = plan.job(slot, 0)
        items = [(plan.contrib[k], *[rows_major(p[BIG[k][0]], BIG[k][1]) for p in (w, m, v)], BIG[k][1] == "transpose")
                 for k in kinds]
        results, cres = adamw_big(items, slot, comm=job)
        plan.done(slot, 0, cres)
        for k, res in zip(kinds, results):
            out[BIG[k][0]] = [rows_major(r, BIG[k][1]) for r in res]

    def own_shapes(p):
        return {n: p[n].reshape(1, D) if n == "final_g" else p[n] for n in SMALL}

    loss, small = adamw_small([g.reshape(NDEV, P_ROWS, D) for g in plan.small_gathered], me.astype(jnp.int32).reshape(1),
                              own_shapes(w), own_shapes(m), own_shapes(v))
    for n in SMALL:
        out[n] = [r.reshape(w[n].shape) for r in small[n]]
    loss = loss[0, 0]
    return (loss, grad_x[None], *[out[n][0] for n in WEIGHTS], *[out[n][1] for n in WEIGHTS],
            *[out[n][2] for n in WEIGHTS], *[out[n][3] for n in WEIGHTS])
```
